```python
import jax, jax.numpy as jnp
from jax import lax
import numpy as np

D_MODEL = 1024
BATCH = 8
SEQ = 4096
DEPTH = 1

HG_HEADS = 4
HG_KEY_DIM = 128
HG_VAL_DIM = 128
HG_KEY_WIDTH = HG_HEADS * HG_KEY_DIM
HG_WIDTH = HG_HEADS * HG_VAL_DIM
HG_CHUNK = 64
MLA_HEADS = 4
MLA_NOPE_DIM = 128
MLA_ROPE_DIM = 64
MLA_V_DIM = 128
MLA_Q_RANK = 256
MLA_KV_RANK = 256
MLA_WIDTH = MLA_HEADS * MLA_V_DIM
ROPE_THETA = 10000.0
Q_BLOCK = 128
MIX_WIDTH = HG_WIDTH + MLA_WIDTH
D_FF = 4 * D_MODEL
N_MOD = 6
ADA_INIT = 0.5
RMS_EPS = 1e-6
LN_EPS = 1e-5
DN_ALPHA = (2.0 * DEPTH) ** 0.25
DN_BETA = (8.0 * DEPTH) ** -0.25
IN_SIZES = (HG_KEY_WIDTH, HG_KEY_WIDTH, HG_WIDTH, HG_WIDTH, MLA_Q_RANK, MLA_KV_RANK, MLA_ROPE_DIM)
IN_COLS = sum(IN_SIZES)
IN_SPLITS = tuple(int(s) for s in np.cumsum(IN_SIZES)[:-1])

kernel_name = 'hybrid_hgrn2_mla_deepnorm_adaln'


def rms_norm(x, w):
    xf = x.astype(jnp.float32)
    y = xf * lax.rsqrt(jnp.mean(xf * xf, axis=-1, keepdims=True) + RMS_EPS) * w.astype(jnp.float32)
    return y.astype(x.dtype)


def layer_norm(x, g, b):
    xf = x.astype(jnp.float32)
    mu = jnp.mean(xf, axis=-1, keepdims=True)
    xc = xf - mu
    var = jnp.mean(xc * xc, axis=-1, keepdims=True)
    y = xc * lax.rsqrt(var + LN_EPS) * g.astype(jnp.float32) + b.astype(jnp.float32)
    return y.astype(x.dtype)


def rope_tables(positions):
    inv_freq = 1.0 / (ROPE_THETA ** (jnp.arange(0, MLA_ROPE_DIM, 2, dtype=jnp.float32) / MLA_ROPE_DIM))
    ang = positions.astype(jnp.float32)[..., None] * inv_freq
    return jnp.cos(ang), jnp.sin(ang)


def apply_rope(x, cos, sin):
    xf = x.astype(jnp.float32)
    x1, x2 = jnp.split(xf, 2, axis=-1)
    return jnp.concatenate([x1 * cos - x2 * sin, x2 * cos + x1 * sin], axis=-1).astype(x.dtype)


def hgrn2_chunkwise(q, f_logit, v, lb):
    B, T, H, dk = q.shape
    dv = v.shape[-1]
    C = HG_CHUNK
    n = T // C
    f32 = jnp.float32
    forget = lb.astype(f32) + (1.0 - lb.astype(f32)) * jax.nn.sigmoid(f_logit.astype(f32))
    k = 1.0 - forget
    log_f = jnp.log(forget)

    def chunked(a):
        return a.astype(f32).reshape(B, n, C, H, a.shape[-1]).transpose(0, 3, 1, 2, 4)

    q, k, v, log_f = chunked(q), chunked(k), chunked(v), chunked(log_f)
    b = jnp.cumsum(log_f, axis=3)
    b_ref = b[:, :, :, C // 2 - 1:C // 2, :]
    b_last = b[:, :, :, C - 1:C, :]
    causal = jnp.tril(jnp.ones((C, C), dtype=bool))
    a = jnp.einsum('bhncd,bhnsd->bhncs', q * jnp.exp(b - b_ref), k * jnp.exp(b_ref - b))
    a = jnp.where(causal, a, 0.0)
    o_intra = jnp.einsum('bhncs,bhnse->bhnce', a, v)
    kv = jnp.einsum('bhnsd,bhnse->nbhde', k * jnp.exp(b_last - b), v)
    decay = jnp.exp(b_last[:, :, :, 0, :]).transpose(2, 0, 1, 3)

    def step(state, inp):
        d, kv_n = inp
        return d[..., None] * state + kv_n, state

    s0 = jnp.zeros((B, H, dk, dv), f32)
    _, s_prev = lax.scan(step, s0, (decay, kv))
    o_inter = jnp.einsum('bhncd,nbhde->bhnce', q * jnp.exp(b), s_prev)
    return (o_intra + o_inter).transpose(0, 2, 3, 1, 4).reshape(B, T, H, dv)


def mla_causal_attention(q_nope, q_pe, k_nope, k_pe, v):
    B, T, H, _ = q_nope.shape
    nb = T // Q_BLOCK
    scale = (MLA_NOPE_DIM + MLA_ROPE_DIM) ** -0.5
    qn = q_nope.reshape(B, nb, Q_BLOCK, H, MLA_NOPE_DIM).transpose(1, 0, 2, 3, 4)
    qp = q_pe.reshape(B, nb, Q_BLOCK, H, MLA_ROPE_DIM).transpose(1, 0, 2, 3, 4)
    starts = jnp.arange(nb, dtype=jnp.int32) * Q_BLOCK
    key_idx = jnp.arange(T, dtype=jnp.int32)
    neg = jnp.finfo(jnp.float32).min

    def block(args):
        qn_b, qp_b, start = args
        s = (jnp.einsum('bqhd,bkhd->bhqk', qn_b, k_nope).astype(jnp.float32)
             + jnp.einsum('bqhd,bkd->bhqk', qp_b, k_pe).astype(jnp.float32)) * scale
        q_idx = start + jnp.arange(Q_BLOCK, dtype=jnp.int32)
        s = jnp.where(key_idx[None, :] <= q_idx[:, None], s, neg)
        p = jax.nn.softmax(s, axis=-1)
        return jnp.einsum('bhqk,bkhd->bqhd', p.astype(v.dtype), v)

    out = lax.map(block, (qn, qp, starts))
    return out.transpose(1, 0, 2, 3, 4).reshape(B, T, H, MLA_V_DIM)


def hybrid_mixer(u, cos, sin, lb, w_in, hg_norm_w, q_norm_w, w_q_up, kv_norm_w, w_kv_up, w_out):
    B, T, _ = u.shape
    z = u @ w_in
    hq, hf, hi, hg, c_q, c_kv, k_pe = jnp.split(z, IN_SPLITS, axis=-1)
    o_hg = hgrn2_chunkwise(hq.reshape(B, T, HG_HEADS, HG_KEY_DIM),
                           hf.reshape(B, T, HG_HEADS, HG_KEY_DIM),
                           hi.reshape(B, T, HG_HEADS, HG_VAL_DIM),
                           lb.reshape(HG_HEADS, HG_KEY_DIM))
    o_hg = rms_norm(o_hg, hg_norm_w.reshape(HG_HEADS, HG_VAL_DIM))
    o_hg = (o_hg * jax.nn.silu(hg.astype(jnp.float32)).reshape(B, T, HG_HEADS, HG_VAL_DIM))
    o_hg = o_hg.astype(u.dtype).reshape(B, T, HG_WIDTH)
    q = (rms_norm(c_q, q_norm_w) @ w_q_up).reshape(B, T, MLA_HEADS, MLA_NOPE_DIM + MLA_ROPE_DIM)
    q_nope, q_pe = q[..., :MLA_NOPE_DIM], q[..., MLA_NOPE_DIM:]
    kvu = (rms_norm(c_kv, kv_norm_w) @ w_kv_up).reshape(B, T, MLA_HEADS, MLA_NOPE_DIM + MLA_V_DIM)
    k_nope, v = kvu[..., :MLA_NOPE_DIM], kvu[..., MLA_NOPE_DIM:]
    q_pe = apply_rope(q_pe, cos[:, :, None, :], sin[:, :, None, :])
    k_pe = apply_rope(k_pe, cos, sin)
    o_mla = mla_causal_attention(q_nope, q_pe, k_nope, k_pe, v).reshape(B, T, MLA_WIDTH)
    return jnp.concatenate([o_hg, o_mla.astype(u.dtype)], axis=-1) @ w_out


def _fwd_setup_inputs(seed: int = 0) -> dict:
    key = jax.random.key(seed)
    ks = jax.random.split(key, 20)

    def nrm(k, shape, scale):
        return jax.random.normal(k, shape, jnp.float32) * scale

    x = nrm(ks[0], (BATCH, SEQ, D_MODEL), 1.0)
    c = nrm(ks[1], (BATCH, D_MODEL), 1.0)
    offsets = jax.random.randint(ks[2], (BATCH, 1), 0, 1024, dtype=jnp.int32)
    positions = (jnp.arange(SEQ, dtype=jnp.int32)[None, :] + offsets).astype(jnp.int32)
    return {
        'x': x,
        'c': c,
        'positions': positions,
        'w_ada': nrm(ks[3], (DEPTH, D_MODEL, N_MOD * D_MODEL), ADA_INIT * D_MODEL ** -0.5),
        'b_ada': nrm(ks[4], (DEPTH, N_MOD * D_MODEL), 0.02),
        'w_in': nrm(ks[5], (DEPTH, D_MODEL, IN_COLS), D_MODEL ** -0.5),
        'hg_lower_bounds': nrm(ks[6], (DEPTH + 1, HG_KEY_WIDTH), 0.1),
        'hg_norm_w': 1.0 + nrm(ks[7], (DEPTH, HG_WIDTH), 0.02),
        'mla_q_norm_w': 1.0 + nrm(ks[8], (DEPTH, MLA_Q_RANK), 0.02),
        'w_q_up': nrm(ks[9], (DEPTH, MLA_Q_RANK, MLA_HEADS * (MLA_NOPE_DIM + MLA_ROPE_DIM)), MLA_Q_RANK ** -0.5),
        'mla_kv_norm_w': 1.0 + nrm(ks[10], (DEPTH, MLA_KV_RANK), 0.02),
        'w_kv_up': nrm(ks[11], (DEPTH, MLA_KV_RANK, MLA_HEADS * (MLA_NOPE_DIM + MLA_V_DIM)), MLA_KV_RANK ** -0.5),
        'w_out': nrm(ks[12], (DEPTH, MIX_WIDTH, D_MODEL), DN_BETA * MIX_WIDTH ** -0.5),
        'ln1_g': 1.0 + nrm(ks[13], (DEPTH, D_MODEL), 0.02),
        'ln1_b': nrm(ks[14], (DEPTH, D_MODEL), 0.02),
        'w_mlp_in': nrm(ks[15], (DEPTH, D_MODEL, D_FF), D_MODEL ** -0.5),
        'w_mlp_out': nrm(ks[16], (DEPTH, D_FF, D_MODEL), DN_BETA * D_FF ** -0.5),
        'ln2_g': 1.0 + nrm(ks[17], (DEPTH, D_MODEL), 0.02),
        'ln2_b': nrm(ks[18], (DEPTH, D_MODEL), 0.02),
    }


def _fwd_reference(x, c, positions, w_ada, b_ada, w_in, hg_lower_bounds, hg_norm_w, mla_q_norm_w, w_q_up,
              mla_kv_norm_w, w_kv_up, w_out, ln1_g, ln1_b, w_mlp_in, w_mlp_out, ln2_g, ln2_b):
    cos, sin = rope_tables(positions)
    lbs = jnp.cumsum(jax.nn.softmax(hg_lower_bounds.astype(jnp.float32), axis=0), axis=0)[:DEPTH]
    cond = jax.nn.silu(c)
    for l in range(DEPTH):
        mod = (cond @ w_ada[l] + b_ada[l])[:, None, :]
        sh_a, sc_a, g_a, sh_m, sc_m, g_m = jnp.split(mod, N_MOD, axis=-1)
        u = x * (1.0 + sc_a) + sh_a
        mix = hybrid_mixer(u, cos, sin, lbs[l], w_in[l], hg_norm_w[l], mla_q_norm_w[l], w_q_up[l],
                           mla_kv_norm_w[l], w_kv_up[l], w_out[l])
        x = layer_norm(DN_ALPHA * x + (1.0 + g_a) * mix, ln1_g[l], ln1_b[l])
        u = x * (1.0 + sc_m) + sh_m
        h = jnp.square(jax.nn.relu(u @ w_mlp_in[l])) @ w_mlp_out[l]
        x = layer_norm(DN_ALPHA * x + (1.0 + g_m) * h, ln2_g[l], ln2_b[l])
    return x


import jax as _jax
import jax.numpy as _jnp

TWIN_FORMAT = 'train_step'
FWD_PARAMS = ['x', 'c', 'positions', 'w_ada', 'b_ada', 'w_in', 'hg_lower_bounds', 'hg_norm_w', 'mla_q_norm_w', 'w_q_up', 'mla_kv_norm_w', 'w_kv_up', 'w_out', 'ln1_g', 'ln1_b', 'w_mlp_in', 'w_mlp_out', 'ln2_g', 'ln2_b']
TWIN_WEIGHTS = ['w_ada', 'b_ada', 'w_in', 'hg_lower_bounds', 'hg_norm_w', 'mla_q_norm_w', 'w_q_up', 'mla_kv_norm_w', 'w_kv_up', 'w_out', 'ln1_g', 'ln1_b', 'w_mlp_in', 'w_mlp_out', 'ln2_g', 'ln2_b']
TWIN_DIFF_INPUT = 'x'
TWIN_INPUTS = ['x', 'c', 'positions', 'w_ada', 'b_ada', 'w_in', 'hg_lower_bounds', 'hg_norm_w', 'mla_q_norm_w', 'w_q_up', 'mla_kv_norm_w', 'w_kv_up', 'w_out', 'ln1_g', 'ln1_b', 'w_mlp_in', 'w_mlp_out', 'ln2_g', 'ln2_b', 'loss_target', 'm_w_ada', 'm_b_ada', 'm_w_in', 'm_hg_lower_bounds', 'm_hg_norm_w', 'm_mla_q_norm_w', 'm_w_q_up', 'm_mla_kv_norm_w', 'm_w_kv_up', 'm_w_out', 'm_ln1_g', 'm_ln1_b', 'm_w_mlp_in', 'm_w_mlp_out', 'm_ln2_g', 'm_ln2_b', 'v_w_ada', 'v_b_ada', 'v_w_in', 'v_hg_lower_bounds', 'v_hg_norm_w', 'v_mla_q_norm_w', 'v_w_q_up', 'v_mla_kv_norm_w', 'v_w_kv_up', 'v_w_out', 'v_ln1_g', 'v_ln1_b', 'v_w_mlp_in', 'v_w_mlp_out', 'v_ln2_g', 'v_ln2_b']
TWIN_OUTPUTS = ['loss', 'grad_x', 'grad_w_ada', 'grad_b_ada', 'grad_w_in', 'grad_hg_lower_bounds', 'grad_hg_norm_w', 'grad_mla_q_norm_w', 'grad_w_q_up', 'grad_mla_kv_norm_w', 'grad_w_kv_up', 'grad_w_out', 'grad_ln1_g', 'grad_ln1_b', 'grad_w_mlp_in', 'grad_w_mlp_out', 'grad_ln2_g', 'grad_ln2_b', 'delta_w_ada', 'delta_b_ada', 'delta_w_in', 'delta_hg_lower_bounds', 'delta_hg_norm_w', 'delta_mla_q_norm_w', 'delta_w_q_up', 'delta_mla_kv_norm_w', 'delta_w_kv_up', 'delta_w_out', 'delta_ln1_g', 'delta_ln1_b', 'delta_w_mlp_in', 'delta_w_mlp_out', 'delta_ln2_g', 'delta_ln2_b', 'new_m_w_ada', 'new_m_b_ada', 'new_m_w_in', 'new_m_hg_lower_bounds', 'new_m_hg_norm_w', 'new_m_mla_q_norm_w', 'new_m_w_q_up', 'new_m_mla_kv_norm_w', 'new_m_w_kv_up', 'new_m_w_out', 'new_m_ln1_g', 'new_m_ln1_b', 'new_m_w_mlp_in', 'new_m_w_mlp_out', 'new_m_ln2_g', 'new_m_ln2_b', 'new_v_w_ada', 'new_v_b_ada', 'new_v_w_in', 'new_v_hg_lower_bounds', 'new_v_hg_norm_w', 'new_v_mla_q_norm_w', 'new_v_w_q_up', 'new_v_mla_kv_norm_w', 'new_v_w_kv_up', 'new_v_w_out', 'new_v_ln1_g', 'new_v_ln1_b', 'new_v_w_mlp_in', 'new_v_w_mlp_out', 'new_v_ln2_g', 'new_v_ln2_b']
TWIN_LEAF_KINDS = {'loss': 'loss', 'grad_x': 'grad_x', 'grad_w_ada': 'grad_w', 'grad_b_ada': 'grad_w', 'grad_w_in': 'grad_w', 'grad_hg_lower_bounds': 'grad_w', 'grad_hg_norm_w': 'grad_w', 'grad_mla_q_norm_w': 'grad_w', 'grad_w_q_up': 'grad_w', 'grad_mla_kv_norm_w': 'grad_w', 'grad_w_kv_up': 'grad_w', 'grad_w_out': 'grad_w', 'grad_ln1_g': 'grad_w', 'grad_ln1_b': 'grad_w', 'grad_w_mlp_in': 'grad_w', 'grad_w_mlp_out': 'grad_w', 'grad_ln2_g': 'grad_w', 'grad_ln2_b': 'grad_w', 'delta_w_ada': 'delta_w', 'delta_b_ada': 'delta_w', 'delta_w_in': 'delta_w', 'delta_hg_lower_bounds': 'delta_w', 'delta_hg_norm_w': 'delta_w', 'delta_mla_q_norm_w': 'delta_w', 'delta_w_q_up': 'delta_w', 'delta_mla_kv_norm_w': 'delta_w', 'delta_w_kv_up': 'delta_w', 'delta_w_out': 'delta_w', 'delta_ln1_g': 'delta_w', 'delta_ln1_b': 'delta_w', 'delta_w_mlp_in': 'delta_w', 'delta_w_mlp_out': 'delta_w', 'delta_ln2_g': 'delta_w', 'delta_ln2_b': 'delta_w', 'new_m_w_ada': 'new_m', 'new_m_b_ada': 'new_m', 'new_m_w_in': 'new_m', 'new_m_hg_lower_bounds': 'new_m', 'new_m_hg_norm_w': 'new_m', 'new_m_mla_q_norm_w': 'new_m', 'new_m_w_q_up': 'new_m', 'new_m_mla_kv_norm_w': 'new_m', 'new_m_w_kv_up': 'new_m', 'new_m_w_out': 'new_m', 'new_m_ln1_g': 'new_m', 'new_m_ln1_b': 'new_m', 'new_m_w_mlp_in': 'new_m', 'new_m_w_mlp_out': 'new_m', 'new_m_ln2_g': 'new_m', 'new_m_ln2_b': 'new_m', 'new_v_w_ada': 'new_v', 'new_v_b_ada': 'new_v', 'new_v_w_in': 'new_v', 'new_v_hg_lower_bounds': 'new_v', 'new_v_hg_norm_w': 'new_v', 'new_v_mla_q_norm_w': 'new_v', 'new_v_w_q_up': 'new_v', 'new_v_mla_kv_norm_w': 'new_v', 'new_v_w_kv_up': 'new_v', 'new_v_w_out': 'new_v', 'new_v_ln1_g': 'new_v', 'new_v_ln1_b': 'new_v', 'new_v_w_mlp_in': 'new_v', 'new_v_w_mlp_out': 'new_v', 'new_v_ln2_g': 'new_v', 'new_v_ln2_b': 'new_v'}


def _forward(args):
    return _fwd_reference(*[args[k] for k in FWD_PARAMS])


def _output_shape():
    out = _jax.eval_shape(lambda: _forward(_fwd_setup_inputs(0)))
    return out.shape, out.dtype

N_MICROBATCH = 1
ADAM_LR = 0.001
ADAM_B1 = 0.9
ADAM_B2 = 0.999
ADAM_EPS = 1e-08
ADAM_WD = 0.01
ADAM_STEP = 10
PER_EXAMPLE_BATCH_AXIS = {'x': 0, 'c': 0, 'positions': 0, 'loss_target': 0}
SHARED_INPUTS = []
_WEIGHT_DTYPES = {'w_ada': _jnp.float32, 'b_ada': _jnp.float32, 'w_in': _jnp.float32, 'hg_lower_bounds': _jnp.float32, 'hg_norm_w': _jnp.float32, 'mla_q_norm_w': _jnp.float32, 'w_q_up': _jnp.float32, 'mla_kv_norm_w': _jnp.float32, 'w_kv_up': _jnp.float32, 'w_out': _jnp.float32, 'ln1_g': _jnp.float32, 'ln1_b': _jnp.float32, 'w_mlp_in': _jnp.float32, 'w_mlp_out': _jnp.float32, 'ln2_g': _jnp.float32, 'ln2_b': _jnp.float32}
MOMENT_SCALE = {'w_ada': 8.111421e-02, 'b_ada': 1.753664e-01, 'w_in': 6.330002e-02, 'hg_lower_bounds': 3.818939e-02, 'hg_norm_w': 6.140487e-02, 'mla_q_norm_w': 2.174627e-02, 'w_q_up': 1.304237e-02, 'mla_kv_norm_w': 5.316013e-02, 'w_kv_up': 2.617969e-02, 'w_out': 8.414917e-02, 'ln1_g': 5.842729e-01, 'ln1_b': 4.449267e-01, 'w_mlp_in': 6.443688e-02, 'w_mlp_out': 2.551065e-01, 'ln2_g': 3.244095e+01, 'ln2_b': 8.132242e+00}


def _to_microbatches(a, axis):
    t = _jnp.moveaxis(a, axis, 0)
    t = t.reshape((N_MICROBATCH, t.shape[0] // N_MICROBATCH) + t.shape[1:])
    return _jnp.moveaxis(t, 1, axis + 1)


def setup_inputs(seed: int = 0) -> dict:
    inp = _fwd_setup_inputs(seed)
    key = _jax.random.fold_in(_jax.random.key(seed), 7919)
    shape, _ = _output_shape()
    out = dict(inp)
    out["loss_target"] = _jax.random.normal(_jax.random.fold_in(key, 0), shape, _jnp.float32)
    for i, name in enumerate(TWIN_WEIGHTS):
        w = inp[name].astype(_jnp.float32)
        if MOMENT_SCALE is None:
            s = _jnp.sqrt(_jnp.mean(_jnp.square(w)) + 1e-30)
        else:
            s = MOMENT_SCALE[name]
        km, kv = _jax.random.split(_jax.random.fold_in(key, i + 1))
        out[name] = w
        out["m_" + name] = s * _jax.random.normal(km, w.shape, _jnp.float32)
        out["v_" + name] = (s * s) * _jax.random.uniform(kv, w.shape, _jnp.float32, 0.5, 1.5)
    if N_MICROBATCH > 1:
        for name, axis in PER_EXAMPLE_BATCH_AXIS.items():
            out[name] = _to_microbatches(out[name], axis)
    return {'x': out['x'], 'c': out['c'], 'positions': out['positions'], 'w_ada': out['w_ada'], 'b_ada': out['b_ada'], 'w_in': out['w_in'], 'hg_lower_bounds': out['hg_lower_bounds'], 'hg_norm_w': out['hg_norm_w'], 'mla_q_norm_w': out['mla_q_norm_w'], 'w_q_up': out['w_q_up'], 'mla_kv_norm_w': out['mla_kv_norm_w'], 'w_kv_up': out['w_kv_up'], 'w_out': out['w_out'], 'ln1_g': out['ln1_g'], 'ln1_b': out['ln1_b'], 'w_mlp_in': out['w_mlp_in'], 'w_mlp_out': out['w_mlp_out'], 'ln2_g': out['ln2_g'], 'ln2_b': out['ln2_b'], 'loss_target': out['loss_target'], 'm_w_ada': out['m_w_ada'], 'm_b_ada': out['m_b_ada'], 'm_w_in': out['m_w_in'], 'm_hg_lower_bounds': out['m_hg_lower_bounds'], 'm_hg_norm_w': out['m_hg_norm_w'], 'm_mla_q_norm_w': out['m_mla_q_norm_w'], 'm_w_q_up': out['m_w_q_up'], 'm_mla_kv_norm_w': out['m_mla_kv_norm_w'], 'm_w_kv_up': out['m_w_kv_up'], 'm_w_out': out['m_w_out'], 'm_ln1_g': out['m_ln1_g'], 'm_ln1_b': out['m_ln1_b'], 'm_w_mlp_in': out['m_w_mlp_in'], 'm_w_mlp_out': out['m_w_mlp_out'], 'm_ln2_g': out['m_ln2_g'], 'm_ln2_b': out['m_ln2_b'], 'v_w_ada': out['v_w_ada'], 'v_b_ada': out['v_b_ada'], 'v_w_in': out['v_w_in'], 'v_hg_lower_bounds': out['v_hg_lower_bounds'], 'v_hg_norm_w': out['v_hg_norm_w'], 'v_mla_q_norm_w': out['v_mla_q_norm_w'], 'v_w_q_up': out['v_w_q_up'], 'v_mla_kv_norm_w': out['v_mla_kv_norm_w'], 'v_w_kv_up': out['v_w_kv_up'], 'v_w_out': out['v_w_out'], 'v_ln1_g': out['v_ln1_g'], 'v_ln1_b': out['v_ln1_b'], 'v_w_mlp_in': out['v_w_mlp_in'], 'v_w_mlp_out': out['v_w_mlp_out'], 'v_ln2_g': out['v_ln2_g'], 'v_ln2_b': out['v_ln2_b']}


def _loss(weights, diff, rest, loss_target):
    with _jax.named_scope("forward"):
        args = {**rest, TWIN_DIFF_INPUT: diff, **{k: w.astype(_WEIGHT_DTYPES[k]) for k, w in weights.items()}}
        y = _forward(args)
    with _jax.named_scope("loss_head"):
        err = _jnp.square(y.astype(_jnp.float32) - loss_target)
        return 0.5 * _jnp.sum(_jnp.mean(err, axis=-1)) if err.ndim else 0.5 * err


def _adamw(w, g, m, v):
    m = ADAM_B1 * m + (1.0 - ADAM_B1) * g
    v = ADAM_B2 * v + (1.0 - ADAM_B2) * _jnp.square(g)
    m_hat = m / (1.0 - ADAM_B1 ** ADAM_STEP)
    v_hat = v / (1.0 - ADAM_B2 ** ADAM_STEP)
    delta = -ADAM_LR * (m_hat / (_jnp.sqrt(v_hat) + ADAM_EPS) + ADAM_WD * w)
    return delta, m, v


def reference(x, c, positions, w_ada, b_ada, w_in, hg_lower_bounds, hg_norm_w, mla_q_norm_w, w_q_up, mla_kv_norm_w, w_kv_up, w_out, ln1_g, ln1_b, w_mlp_in, w_mlp_out, ln2_g, ln2_b, loss_target, m_w_ada, m_b_ada, m_w_in, m_hg_lower_bounds, m_hg_norm_w, m_mla_q_norm_w, m_w_q_up, m_mla_kv_norm_w, m_w_kv_up, m_w_out, m_ln1_g, m_ln1_b, m_w_mlp_in, m_w_mlp_out, m_ln2_g, m_ln2_b, v_w_ada, v_b_ada, v_w_in, v_hg_lower_bounds, v_hg_norm_w, v_mla_q_norm_w, v_w_q_up, v_mla_kv_norm_w, v_w_kv_up, v_w_out, v_ln1_g, v_ln1_b, v_w_mlp_in, v_w_mlp_out, v_ln2_g, v_ln2_b):
    given = dict(x=x, c=c, positions=positions, w_ada=w_ada, b_ada=b_ada, w_in=w_in, hg_lower_bounds=hg_lower_bounds, hg_norm_w=hg_norm_w, mla_q_norm_w=mla_q_norm_w, w_q_up=w_q_up, mla_kv_norm_w=mla_kv_norm_w, w_kv_up=w_kv_up, w_out=w_out, ln1_g=ln1_g, ln1_b=ln1_b, w_mlp_in=w_mlp_in, w_mlp_out=w_mlp_out, ln2_g=ln2_g, ln2_b=ln2_b, loss_target=loss_target, m_w_ada=m_w_ada, m_b_ada=m_b_ada, m_w_in=m_w_in, m_hg_lower_bounds=m_hg_lower_bounds, m_hg_norm_w=m_hg_norm_w, m_mla_q_norm_w=m_mla_q_norm_w, m_w_q_up=m_w_q_up, m_mla_kv_norm_w=m_mla_kv_norm_w, m_w_kv_up=m_w_kv_up, m_w_out=m_w_out, m_ln1_g=m_ln1_g, m_ln1_b=m_ln1_b, m_w_mlp_in=m_w_mlp_in, m_w_mlp_out=m_w_mlp_out, m_ln2_g=m_ln2_g, m_ln2_b=m_ln2_b, v_w_ada=v_w_ada, v_b_ada=v_b_ada, v_w_in=v_w_in, v_hg_lower_bounds=v_hg_lower_bounds, v_hg_norm_w=v_hg_norm_w, v_mla_q_norm_w=v_mla_q_norm_w, v_w_q_up=v_w_q_up, v_mla_kv_norm_w=v_mla_kv_norm_w, v_w_kv_up=v_w_kv_up, v_w_out=v_w_out, v_ln1_g=v_ln1_g, v_ln1_b=v_ln1_b, v_w_mlp_in=v_w_mlp_in, v_w_mlp_out=v_w_mlp_out, v_ln2_g=v_ln2_g, v_ln2_b=v_ln2_b)
    weights = {n: given[n] for n in TWIN_WEIGHTS}
    shared = {n: given[n] for n in SHARED_INPUTS}
    per_example = {n: given[n] for n in ['x', 'c', 'positions']}
    grad_fn = _jax.value_and_grad(_loss, argnums=(0, 1))

    def one_microbatch(ex, loss_target):
        ex = dict(ex)
        diff = ex.pop(TWIN_DIFF_INPUT)
        return grad_fn(weights, diff, {**shared, **ex}, loss_target)

    if N_MICROBATCH == 1:
        loss, (grad_w, grad_x) = one_microbatch(per_example, given["loss_target"])
    else:
        def body(carry, xs):
            loss_sum, grad_sum = carry
            l_k, (gw_k, gx_k) = one_microbatch(xs[0], xs[1])
            with _jax.named_scope("update"):
                return (loss_sum + l_k, _jax.tree.map(_jnp.add, grad_sum, gw_k)), gx_k

        init = (_jnp.zeros((), _jnp.float32), _jax.tree.map(_jnp.zeros_like, weights))
        (loss, grad_w), grad_x = _jax.lax.scan(body, init, (per_example, given["loss_target"]))
    with _jax.named_scope("update"):
        delta_w, new_m, new_v = {}, {}, {}
        for n in TWIN_WEIGHTS:
            delta_w[n], new_m[n], new_v[n] = _adamw(weights[n], grad_w[n], given["m_" + n], given["v_" + n])
    return (loss, grad_x, *[grad_w[n] for n in TWIN_WEIGHTS], *[delta_w[n] for n in TWIN_WEIGHTS],
            *[new_m[n] for n in TWIN_WEIGHTS], *[new_v[n] for n in TWIN_WEIGHTS])
```

```python
import functools

import jax
import jax.numpy as jnp
from jax import lax
from jax.experimental import pallas as pl
from jax.experimental.pallas import tpu as pltpu

F32 = jnp.float32
BF16 = jnp.bfloat16
MESH_IDS = pl.DeviceIdType.MESH

D_MODEL = 1024
N_HEADS = 4
HEAD_DIM = 128
ROPE_DIM = 64
HG_CHUNK = 64
HG_COLS = 2048
Q_RANK = 256
KV_RANK = 256
IN_COLS = 2624
IN_COLS_PAD = 2688
QK_DIM = 256
D_FF = 4096
N_CHIPS = 4
N_DEV = 8
ROPE_THETA = 10000.0
RMS_EPS = 1e-6
LN_EPS = 1e-5
DN_ALPHA = 2.0 ** 0.25
ATT_SCALE = (HEAD_DIM + ROPE_DIM) ** -0.5
NEG_BIG = -1e30
ADAM_LR = 0.001
ADAM_B1 = 0.9
ADAM_B2 = 0.999
ADAM_EPS = 1e-08
ADAM_WD = 0.01
ADAM_STEP = 10
SMALL_ROWS = 16
MIB = 1024 * 1024


def _dot(a, b):
    return jnp.dot(a, b, preferred_element_type=F32)


def _dot_nt(a, b):
    return lax.dot_general(a, b, (((1,), (1,)), ((), ())), preferred_element_type=F32)


def _dot_tn(a, b):
    return lax.dot_general(a, b, (((0,), (0,)), ((), ())), preferred_element_type=F32)


def _dot_f32(a, b):
    return jnp.dot(a, b, preferred_element_type=F32, precision=lax.Precision.HIGHEST)


def _params(vmem_mib, semantics=None):
    return pltpu.CompilerParams(vmem_limit_bytes=vmem_mib * MIB, dimension_semantics=semantics)


def _sigmoid(v):
    return 1.0 / (1.0 + jnp.exp(-v))


def _colsum(v):
    return jnp.sum(v, axis=0, keepdims=True)


def _rowmean(v):
    return jnp.mean(v, axis=-1, keepdims=True)


def _rope_tables(pos, invf):
    ang = pos * invf
    lane = lax.broadcasted_iota(jnp.int32, ang.shape, 1)
    cos_t = jnp.where(lane < ROPE_DIM, jnp.cos(ang), 0.0)
    sin = jnp.sin(ang)
    sin_t = jnp.where(lane < ROPE_DIM // 2, -sin, jnp.where(lane < ROPE_DIM, sin, 0.0))
    return cos_t, sin_t


def _swap_halves(t):
    lane = lax.broadcasted_iota(jnp.int32, t.shape, 1)
    return jnp.where(lane < ROPE_DIM // 2, pltpu.roll(t, 128 - ROPE_DIM // 2, 1), pltpu.roll(t, ROPE_DIM // 2, 1))


def _rope(t, cos_t, sin_t):
    return t * cos_t + _swap_halves(t) * sin_t


def _unrope(g, cos_t, sin_t):
    return g * cos_t - _swap_halves(g) * sin_t


def _mesh_pos():
    return lax.axis_index("x"), lax.axis_index("y"), lax.axis_index("c")


def _other_chips(x, y):
    out = []
    for dx, dy in ((1, 0), (0, 1), (1, 1)):
        px = 1 - x if dx else x
        py = 1 - y if dy else y
        out.append(((px, py), 2 * px + py))
    return out


def _allgather8(a, name):
    rows, cols = a.shape

    def body(a_ref, out_ref, send_sems, recv_sems):
        x, y, c = _mesh_pos()
        me = 4 * x + 2 * y + c
        out_ref[me] = a_ref[...]
        peers = []
        for r in range(1, N_DEV):
            px = 1 - x if r & 4 else x
            py = 1 - y if r & 2 else y
            pc = 1 - c if r & 1 else c
            peers.append(((px, py, pc), 4 * px + 2 * py + pc))

        def copy(r, block, to):
            return pltpu.make_async_remote_copy(
                src_ref=a_ref, dst_ref=out_ref.at[block], send_sem=send_sems.at[r], recv_sem=recv_sems.at[r],
                device_id=to, device_id_type=MESH_IDS)

        sends = [copy(r, me, peer) for r, (peer, _) in enumerate(peers)]
        for cp in sends:
            cp.start()
        for r, (peer, idx) in enumerate(peers):
            copy(r, idx, peer).wait_recv()
        for cp in sends:
            cp.wait_send()

    return pl.pallas_call(
        body, name=name,
        out_shape=jax.ShapeDtypeStruct((N_DEV, rows, cols), a.dtype),
        in_specs=[pl.BlockSpec(memory_space=pltpu.VMEM)],
        out_specs=pl.BlockSpec(memory_space=pltpu.VMEM),
        scratch_shapes=[pltpu.SemaphoreType.DMA((N_DEV - 1,)), pltpu.SemaphoreType.DMA((N_DEV - 1,))],
    )(a)


def _gather_weights(shards):
    n = len(shards)

    def body(*refs):
        src = refs[:n]
        out = refs[n:2 * n]
        send_a, recv_a, send_b, recv_b, local_sems = refs[2 * n:]
        x, y, c = _mesh_pos()
        k = 2 * x + y
        sibling = (x, y, 1 - c)
        chips = _other_chips(x, y)

        def half(i, which):
            h = shards[i].shape[0] // 2
            return pl.ds(pl.multiple_of(which * h, 16), h)

        local = [pltpu.make_async_copy(src[i], out[i].at[k], local_sems.at[i]) for i in range(n)]
        for cp in local:
            cp.start()

        def ici(j, i, chip_idx, to, from_src):
            rows = half(i, c)
            return pltpu.make_async_remote_copy(
                src_ref=src[i].at[rows] if from_src else out[i].at[chip_idx, rows],
                dst_ref=out[i].at[chip_idx, rows],
                send_sem=send_a.at[j, i], recv_sem=recv_a.at[j, i], device_id=to, device_id_type=MESH_IDS)

        def d2d(j, i, chip_idx, which):
            rows = half(i, which)
            return pltpu.make_async_remote_copy(
                src_ref=out[i].at[chip_idx, rows], dst_ref=out[i].at[chip_idx, rows],
                send_sem=send_b.at[j, i], recv_sem=recv_b.at[j, i], device_id=sibling, device_id_type=MESH_IDS)

        first = [ici(j, i, k, (*chip, c), True) for j, (chip, _) in enumerate(chips) for i in range(n)]
        for cp in first:
            cp.start()
        passed = []
        for j, (chip, kj) in enumerate(chips):
            for i in range(n):
                ici(j, i, kj, (*chip, c), False).wait_recv()
                cp = d2d(j, i, kj, c)
                cp.start()
                passed.append(cp)
        for j, (chip, kj) in enumerate(chips):
            for i in range(n):
                d2d(j, i, kj, 1 - c).wait_recv()
        for cp in first + passed:
            cp.wait_send()
        for cp in local:
            cp.wait()

    any_spec = pl.BlockSpec(memory_space=pl.ANY)
    return pl.pallas_call(
        body, name="gather_weights",
        out_shape=[jax.ShapeDtypeStruct((N_CHIPS,) + s.shape, s.dtype) for s in shards],
        in_specs=[any_spec] * n, out_specs=[any_spec] * n,
        scratch_shapes=[pltpu.SemaphoreType.DMA((3, n)), pltpu.SemaphoreType.DMA((3, n)),
                        pltpu.SemaphoreType.DMA((3, n)), pltpu.SemaphoreType.DMA((3, n)),
                        pltpu.SemaphoreType.DMA((n,))],
    )(*shards)


def _pair_exchange(grads):
    n = len(grads)

    def body(*refs):
        src = refs[:n]
        out = refs[n:2 * n]
        send_sems, recv_sems = refs[2 * n:]
        x, y, c = _mesh_pos()
        copies = []
        for i in range(n):
            h = grads[i].shape[1] // 2
            rows = pl.ds(pl.multiple_of((1 - c) * h, 16), h)
            copies.append(pltpu.make_async_remote_copy(
                src_ref=src[i].at[:, rows], dst_ref=out[i], send_sem=send_sems.at[i], recv_sem=recv_sems.at[i],
                device_id=(x, y, 1 - c), device_id_type=MESH_IDS))
        for cp in copies:
            cp.start()
        for cp in copies:
            cp.wait()

    any_spec = pl.BlockSpec(memory_space=pl.ANY)
    return pl.pallas_call(
        body, name="grad_pair_exchange",
        out_shape=[jax.ShapeDtypeStruct((N_CHIPS, g.shape[1] // 2, g.shape[2]), g.dtype) for g in grads],
        in_specs=[any_spec] * n, out_specs=[any_spec] * n,
        scratch_shapes=[pltpu.SemaphoreType.DMA((n,)), pltpu.SemaphoreType.DMA((n,))],
    )(*grads)


def _chip_exchange(partials):
    n = len(partials)

    def body(*refs):
        src = refs[:n]
        out = refs[n:2 * n]
        send_sems, recv_sems = refs[2 * n:]
        x, y, c = _mesh_pos()
        copies = []
        for j, (chip, kj) in enumerate(_other_chips(x, y)):
            for i in range(n):
                copies.append(pltpu.make_async_remote_copy(
                    src_ref=src[i].at[kj], dst_ref=out[i].at[j], send_sem=send_sems.at[j, i],
                    recv_sem=recv_sems.at[j, i], device_id=(*chip, c), device_id_type=MESH_IDS))
        for cp in copies:
            cp.start()
        for cp in copies:
            cp.wait()

    any_spec = pl.BlockSpec(memory_space=pl.ANY)
    return pl.pallas_call(
        body, name="grad_chip_exchange",
        out_shape=[jax.ShapeDtypeStruct((3,) + p.shape[1:], p.dtype) for p in partials],
        in_specs=[any_spec] * n, out_specs=[any_spec] * n,
        scratch_shapes=[pltpu.SemaphoreType.DMA((3, n)), pltpu.SemaphoreType.DMA((3, n))],
    )(*partials)


def _pair_assemble(halves):
    n = len(halves)

    def body(*refs):
        src = refs[:n]
        out = refs[n:2 * n]
        send_sems, recv_sems, local_sems = refs[2 * n:]
        x, y, c = _mesh_pos()
        local, copies = [], []
        for i in range(n):
            h = halves[i].shape[0]
            rows = pl.ds(pl.multiple_of(c * h, 16), h)
            theirs = pl.ds(pl.multiple_of((1 - c) * h, 16), h)
            local.append(pltpu.make_async_copy(src[i], out[i].at[rows], local_sems.at[i]))
            copies.append((
                pltpu.make_async_remote_copy(
                    src_ref=src[i], dst_ref=out[i].at[rows], send_sem=send_sems.at[i], recv_sem=recv_sems.at[i],
                    device_id=(x, y, 1 - c), device_id_type=MESH_IDS),
                pltpu.make_async_remote_copy(
                    src_ref=src[i], dst_ref=out[i].at[theirs], send_sem=send_sems.at[i], recv_sem=recv_sems.at[i],
                    device_id=(x, y, 1 - c), device_id_type=MESH_IDS)))
        for cp in local:
            cp.start()
        for send, _ in copies:
            send.start()
        for send, recv in copies:
            recv.wait_recv()
            send.wait_send()
        for cp in local:
            cp.wait()

    any_spec = pl.BlockSpec(memory_space=pl.ANY)
    return pl.pallas_call(
        body, name="grad_pair_assemble",
        out_shape=[jax.ShapeDtypeStruct((2 * s.shape[0], s.shape[1]), s.dtype) for s in halves],
        in_specs=[any_spec] * n, out_specs=[any_spec] * n,
        scratch_shapes=[pltpu.SemaphoreType.DMA((n,)), pltpu.SemaphoreType.DMA((n,)), pltpu.SemaphoreType.DMA((n,))],
    )(*halves)


def _row_tile(rows):
    for t in (256, 128, 64, 32, 16, 8):
        if rows % t == 0:
            return t
    return rows


def _add_pair(core, grad, landed):
    _, h, cols = landed.shape
    tr = _row_tile(h)
    nt = h // tr

    def body(core_ref, g_ref, l_ref, o_ref):
        o_ref[...] = g_ref[...] + l_ref[...]

    return pl.pallas_call(
        body, name="grad_add_pair",
        out_shape=jax.ShapeDtypeStruct(landed.shape, F32),
        grid_spec=pltpu.PrefetchScalarGridSpec(
            num_scalar_prefetch=1, grid=(N_CHIPS, nt),
            in_specs=[pl.BlockSpec((None, tr, cols), lambda k, t, core_ref: (k, core_ref[0] * nt + t, 0)),
                      pl.BlockSpec((None, tr, cols), lambda k, t, core_ref: (k, t, 0))],
            out_specs=pl.BlockSpec((None, tr, cols), lambda k, t, core_ref: (k, t, 0))),
        compiler_params=_params(32, ("arbitrary", "arbitrary")),
    )(core, grad, landed)


def _add_chips(chip, partial, landed):
    _, h, cols = partial.shape
    tr = _row_tile(h)

    def body(chip_ref, p_ref, l_ref, o_ref):
        o_ref[...] = ((p_ref[...] + l_ref[0]) + l_ref[1]) + l_ref[2]

    return pl.pallas_call(
        body, name="grad_add_chips",
        out_shape=jax.ShapeDtypeStruct((h, cols), F32),
        grid_spec=pltpu.PrefetchScalarGridSpec(
            num_scalar_prefetch=1, grid=(h // tr,),
            in_specs=[pl.BlockSpec((None, tr, cols), lambda t, chip_ref: (chip_ref[0], t, 0)),
                      pl.BlockSpec((3, tr, cols), lambda t, chip_ref: (0, t, 0))],
            out_specs=pl.BlockSpec((tr, cols), lambda t, chip_ref: (t, 0))),
        compiler_params=_params(32, ("arbitrary",)),
    )(chip, partial, landed)


def _ada_project(c_all, w_ada, b_shard):
    n = w_ada.shape[1]
    tn = 512

    def body(c_ref, w_ref, b_ref, mod_ref, cond_ref):
        cv = c_ref[...]
        cond = cv * _sigmoid(cv)
        mod_ref[...] = _dot(cond.astype(BF16), w_ref[...].astype(BF16)) + b_ref[...]
        cond_ref[0:N_DEV, :] = cond
        cond_ref[N_DEV:2 * N_DEV, :] = jnp.zeros_like(cond)

    return pl.pallas_call(
        body, name="ada_project", grid=(n // tn,),
        out_shape=[jax.ShapeDtypeStruct((N_DEV, n), F32), jax.ShapeDtypeStruct((2 * N_DEV, D_MODEL), F32)],
        in_specs=[pl.BlockSpec((N_DEV, D_MODEL), lambda j: (0, 0)), pl.BlockSpec((D_MODEL, tn), lambda j: (0, j)),
                  pl.BlockSpec((1, tn), lambda j: (0, j))],
        out_specs=[pl.BlockSpec((N_DEV, tn), lambda j: (0, j)), pl.BlockSpec((2 * N_DEV, D_MODEL), lambda j: (0, 0))],
        compiler_params=_params(32, ("arbitrary",)),
    )(c_all, w_ada, b_shard)


def _in_project(x, pos, sc_a, sh_a, w_in, q_norm_w, kv_norm_w, w_q, w_kv, invf):
    t_len = x.shape[0]
    tm = min(256, t_len)

    def body(x_ref, pos_ref, sc_ref, sh_ref, win_ref, qn_ref, kvn_ref, wq_ref, wkv_ref, invf_ref,
             u_ref, zhg_ref, cq_ref, ckv_ref, q_ref, k_ref, v_ref):
        u = (x_ref[...] * (1.0 + sc_ref[...]) + sh_ref[...]).astype(BF16)
        u_ref[...] = u
        z = _dot(u, win_ref[...])
        zhg_ref[...] = z[:, :HG_COLS]
        cq = z[:, HG_COLS:HG_COLS + Q_RANK]
        ckv = z[:, HG_COLS + Q_RANK:HG_COLS + Q_RANK + KV_RANK]
        cq_ref[...] = cq
        ckv_ref[...] = ckv
        cos_t, sin_t = _rope_tables(pos_ref[...], invf_ref[...])
        k_pe = _rope(z[:, HG_COLS + Q_RANK + KV_RANK:], cos_t, sin_t).astype(BF16)
        cqn = (cq * lax.rsqrt(_rowmean(cq * cq) + RMS_EPS) * qn_ref[...]).astype(BF16)
        ckvn = (ckv * lax.rsqrt(_rowmean(ckv * ckv) + RMS_EPS) * kvn_ref[...]).astype(BF16)
        for h in range(N_HEADS):
            qh = _dot(cqn, wq_ref[h])
            q_ref[h, :, 0:HEAD_DIM] = qh[:, :HEAD_DIM].astype(BF16)
            q_ref[h, :, HEAD_DIM:QK_DIM] = _rope(qh[:, HEAD_DIM:], cos_t, sin_t).astype(BF16)
            kvh = _dot(ckvn, wkv_ref[h])
            k_ref[h, :, 0:HEAD_DIM] = kvh[:, :HEAD_DIM].astype(BF16)
            k_ref[h, :, HEAD_DIM:QK_DIM] = k_pe
            v_ref[h] = kvh[:, HEAD_DIM:].astype(BF16)

    row = lambda i: (i, 0)
    fixed2 = lambda i: (0, 0)
    fixed3 = lambda i: (0, 0, 0)
    heads = lambda i: (0, i, 0)
    return pl.pallas_call(
        body, name="in_project", grid=(t_len // tm,),
        out_shape=[jax.ShapeDtypeStruct((t_len, D_MODEL), BF16), jax.ShapeDtypeStruct((t_len, HG_COLS), F32),
                   jax.ShapeDtypeStruct((t_len, Q_RANK), F32), jax.ShapeDtypeStruct((t_len, KV_RANK), F32),
                   jax.ShapeDtypeStruct((N_HEADS, t_len, QK_DIM), BF16),
                   jax.ShapeDtypeStruct((N_HEADS, t_len, QK_DIM), BF16),
                   jax.ShapeDtypeStruct((N_HEADS, t_len, HEAD_DIM), BF16)],
        in_specs=[pl.BlockSpec((tm, D_MODEL), row), pl.BlockSpec((tm, 1), row),
                  pl.BlockSpec((1, D_MODEL), fixed2), pl.BlockSpec((1, D_MODEL), fixed2),
                  pl.BlockSpec((D_MODEL, IN_COLS_PAD), fixed2),
                  pl.BlockSpec((1, Q_RANK), fixed2), pl.BlockSpec((1, KV_RANK), fixed2),
                  pl.BlockSpec((N_HEADS, Q_RANK, QK_DIM), fixed3), pl.BlockSpec((N_HEADS, KV_RANK, 2 * HEAD_DIM), fixed3),
                  pl.BlockSpec((1, 128), fixed2)],
        out_specs=[pl.BlockSpec((tm, D_MODEL), row), pl.BlockSpec((tm, HG_COLS), row),
                   pl.BlockSpec((tm, Q_RANK), row), pl.BlockSpec((tm, KV_RANK), row),
                   pl.BlockSpec((N_HEADS, tm, QK_DIM), heads), pl.BlockSpec((N_HEADS, tm, QK_DIM), heads),
                   pl.BlockSpec((N_HEADS, tm, HEAD_DIM), heads)],
        compiler_params=_params(48, ("arbitrary",)),
    )(x, pos, sc_a, sh_a, w_in, q_norm_w, kv_norm_w, w_q, w_kv, invf)


def _lower_bound(lb_raw):
    m = jnp.max(lb_raw, axis=0, keepdims=True)
    e = jnp.exp(lb_raw - m)
    return e[0:1] / jnp.sum(e, axis=0, keepdims=True)


def _tri(inclusive_lower):
    r = lax.broadcasted_iota(jnp.int32, (HG_CHUNK, HG_CHUNK), 0)
    c = lax.broadcasted_iota(jnp.int32, (HG_CHUNK, HG_CHUNK), 1)
    return (c <= r) if inclusive_lower else (c >= r)


def _hg_chunk(q, f_logit, lb):
    sg = _sigmoid(f_logit)
    forget = lb + (1.0 - lb) * sg
    kk = 1.0 - forget
    b = _dot_f32(_tri(True).astype(F32), jnp.log(forget))
    b_ref = b[HG_CHUNK // 2 - 1:HG_CHUNK // 2]
    b_last = b[HG_CHUNK - 1:HG_CHUNK]
    e_i = jnp.exp(b - b_ref)
    e_ri = jnp.exp(b_ref - b)
    e_b = jnp.exp(b)
    e_l = jnp.exp(b_last - b)
    return dict(sg=sg, forget=forget, e_i=e_i, e_ri=e_ri, e_b=e_b, e_l=e_l, dec=jnp.exp(b_last),
                qi=q * e_i, ki=kk * e_ri, qe=q * e_b, kl=kk * e_l)


def _hgrn_forward(zhg, lb_raw, norm_w):
    t_len = zhg.shape[0]
    tb = min(512, t_len)
    n_chunks = tb // HG_CHUNK

    def body(q_ref, f_ref, v_ref, g_ref, lb_ref, w_ref, opre_ref, o_ref, st_ref, state):
        @pl.when(pl.program_id(1) == 0)
        def _():
            state[...] = jnp.zeros_like(state)

        lb = _lower_bound(lb_ref[...])
        causal = _tri(True)
        for n in range(n_chunks):
            rows = pl.ds(n * HG_CHUNK, HG_CHUNK)
            v = v_ref[rows, :].astype(BF16)
            ch = _hg_chunk(q_ref[rows, :], f_ref[rows, :], lb)
            a = jnp.where(causal, _dot_nt(ch["qi"].astype(BF16), ch["ki"].astype(BF16)), 0.0)
            st = state[...]
            st_ref[0, n] = st
            o = _dot(a.astype(BF16), v) + _dot_nt(ch["qe"].astype(BF16), st.astype(BF16))
            state[...] = st * ch["dec"] + _dot_tn(v, ch["kl"].astype(BF16))
            opre_ref[rows, :] = o
            on = o * lax.rsqrt(_rowmean(o * o) + RMS_EPS) * w_ref[...]
            g = g_ref[rows, :]
            o_ref[rows, :] = on * (g * _sigmoid(g))

    col = lambda off: (lambda h, t: (t, off + h))
    return pl.pallas_call(
        body, name="hgrn_forward", grid=(N_HEADS, t_len // tb),
        out_shape=[jax.ShapeDtypeStruct((t_len, N_HEADS * HEAD_DIM), F32),
                   jax.ShapeDtypeStruct((t_len, N_HEADS * HEAD_DIM), F32),
                   jax.ShapeDtypeStruct((N_HEADS, t_len // HG_CHUNK, HEAD_DIM, HEAD_DIM), F32)],
        in_specs=[pl.BlockSpec((tb, HEAD_DIM), col(0)), pl.BlockSpec((tb, HEAD_DIM), col(N_HEADS)),
                  pl.BlockSpec((tb, HEAD_DIM), col(2 * N_HEADS)), pl.BlockSpec((tb, HEAD_DIM), col(3 * N_HEADS)),
                  pl.BlockSpec((2, HEAD_DIM), lambda h, t: (0, h)), pl.BlockSpec((1, HEAD_DIM), lambda h, t: (0, h))],
        out_specs=[pl.BlockSpec((tb, HEAD_DIM), col(0)), pl.BlockSpec((tb, HEAD_DIM), col(0)),
                   pl.BlockSpec((1, n_chunks, HEAD_DIM, HEAD_DIM), lambda h, t: (h, t, 0, 0))],
        scratch_shapes=[pltpu.VMEM((HEAD_DIM, HEAD_DIM), F32)],
        compiler_params=_params(32, ("arbitrary", "arbitrary")),
    )(zhg, zhg, zhg, zhg, lb_raw, norm_w)


def _hgrn_backward(zhg, lb_raw, norm_w, o_pre, d_cat, states):
    t_len = zhg.shape[0]
    tb = min(512, t_len)
    n_chunks = tb // HG_CHUNK
    nb = t_len // tb

    def body(q_ref, f_ref, v_ref, g_ref, lb_ref, w_ref, opre_ref, do_ref, st_ref,
             dq_ref, df_ref, dv_ref, dg_ref, sums_ref, gstate):
        @pl.when(pl.program_id(1) == 0)
        def _():
            gstate[...] = jnp.zeros_like(gstate)
            sums_ref[...] = jnp.zeros_like(sums_ref)

        lb = _lower_bound(lb_ref[...])
        w = w_ref[...]
        causal = _tri(True)
        upper = _tri(False).astype(F32)
        row_id = lax.broadcasted_iota(jnp.int32, (HG_CHUNK, HEAD_DIM), 0)
        d_lb = jnp.zeros((1, HEAD_DIM), F32)
        d_w = jnp.zeros((1, HEAD_DIM), F32)
        for n in reversed(range(n_chunks)):
            rows = pl.ds(n * HG_CHUNK, HG_CHUNK)
            o = opre_ref[rows, :]
            g = g_ref[rows, :]
            d_out = do_ref[rows, :]
            r = lax.rsqrt(_rowmean(o * o) + RMS_EPS)
            sg_g = _sigmoid(g)
            silu = g * sg_g
            dg_ref[rows, :] = (d_out * (o * r * w) * (sg_g * (1.0 + g * (1.0 - sg_g)))).astype(BF16)
            d_on = d_out * silu
            d_w = d_w + _colsum(d_on * o * r)
            dy = d_on * w
            d_o = (r * dy - o * (r * r * r) * _rowmean(dy * o)).astype(BF16)
            vf = v_ref[rows, :]
            v = vf.astype(BF16)
            ch = _hg_chunk(q_ref[rows, :], f_ref[rows, :], lb)
            qi, ki, qe, kl = (ch[name].astype(BF16) for name in ("qi", "ki", "qe", "kl"))
            st = st_ref[0, n]
            gt = gstate[...]
            a = jnp.where(causal, _dot_nt(qi, ki), 0.0).astype(BF16)
            d_a = jnp.where(causal, _dot_nt(d_o, v), 0.0).astype(BF16)
            gt_b = gt.astype(BF16)
            d_v = _dot_tn(a, d_o) + _dot_nt(kl, gt_b)
            d_qi = _dot(d_a, ki)
            d_ki = _dot_tn(d_a, qi)
            d_qe = _dot(d_o, st.astype(BF16))
            d_kl = _dot(v, gt_b)
            d_dec = _colsum(gt * st)
            gstate[...] = gt * ch["dec"] + _dot_tn(d_o, qe)
            dq_ref[rows, :] = (d_qi * ch["e_i"] + d_qe * ch["e_b"]).astype(BF16)
            d_k = d_ki * ch["e_ri"] + d_kl * ch["e_l"]
            t_qi = d_qi * ch["qi"]
            t_ki = d_ki * ch["ki"]
            t_kl = d_kl * ch["kl"]
            d_b = t_qi - t_ki + d_qe * ch["qe"] - t_kl
            d_b = d_b + jnp.where(row_id == HG_CHUNK // 2 - 1, _colsum(t_ki - t_qi), 0.0)
            d_b = d_b + jnp.where(row_id == HG_CHUNK - 1, _colsum(t_kl) + d_dec * ch["dec"], 0.0)
            d_forget = _dot_f32(upper, d_b) / ch["forget"] - d_k
            sg = ch["sg"]
            df_ref[rows, :] = (d_forget * (1.0 - lb) * sg * (1.0 - sg)).astype(BF16)
            d_lb = d_lb + _colsum(d_forget * (1.0 - sg))
            dv_ref[rows, :] = d_v.astype(BF16)
        sums_ref[0:1, :] += d_lb
        sums_ref[1:2, :] += d_w

    col = lambda off: (lambda h, t: (nb - 1 - t, off + h))
    return pl.pallas_call(
        body, name="hgrn_backward", grid=(N_HEADS, nb),
        out_shape=[jax.ShapeDtypeStruct((t_len, N_HEADS * HEAD_DIM), BF16)] * 4
        + [jax.ShapeDtypeStruct((8, N_HEADS * HEAD_DIM), F32)],
        in_specs=[pl.BlockSpec((tb, HEAD_DIM), col(0)), pl.BlockSpec((tb, HEAD_DIM), col(N_HEADS)),
                  pl.BlockSpec((tb, HEAD_DIM), col(2 * N_HEADS)), pl.BlockSpec((tb, HEAD_DIM), col(3 * N_HEADS)),
                  pl.BlockSpec((2, HEAD_DIM), lambda h, t: (0, h)), pl.BlockSpec((1, HEAD_DIM), lambda h, t: (0, h)),
                  pl.BlockSpec((tb, HEAD_DIM), col(0)), pl.BlockSpec((tb, HEAD_DIM), col(0)),
                  pl.BlockSpec((1, n_chunks, HEAD_DIM, HEAD_DIM), lambda h, t: (h, nb - 1 - t, 0, 0))],
        out_specs=[pl.BlockSpec((tb, HEAD_DIM), col(0))] * 4 + [pl.BlockSpec((8, HEAD_DIM), lambda h, t: (0, h))],
        scratch_shapes=[pltpu.VMEM((HEAD_DIM, HEAD_DIM), F32)],
        compiler_params=_params(32, ("arbitrary", "arbitrary")),
    )(zhg, zhg, zhg, zhg, lb_raw, norm_w, o_pre, d_cat, states)


def _causal_mask(qi, ki, tq, tk):
    row = qi * tq + lax.broadcasted_iota(jnp.int32, (tq, tk), 0)
    col = ki * tk + lax.broadcasted_iota(jnp.int32, (tq, tk), 1)
    return col <= row


def _attention_forward(q, k, v):
    t_len = q.shape[1]
    tq = min(512, t_len)
    nq = t_len // tq

    def body(q_ref, k_ref, v_ref, o_ref, lse_ref, m_s, l_s, acc_s):
        qi, ki = pl.program_id(1), pl.program_id(2)

        @pl.when(ki == 0)
        def _():
            m_s[...] = jnp.full_like(m_s, NEG_BIG)
            l_s[...] = jnp.zeros_like(l_s)
            acc_s[...] = jnp.zeros_like(acc_s)

        @pl.when(ki <= qi)
        def _():
            s = _dot_nt(q_ref[...], k_ref[...]) * ATT_SCALE
            s = jnp.where(_causal_mask(qi, ki, tq, tq), s, NEG_BIG)
            m_old = m_s[...]
            m_new = jnp.maximum(m_old, jnp.max(s, axis=-1, keepdims=True))
            alpha = jnp.exp(m_old - m_new)
            p = jnp.exp(s - m_new)
            l_s[...] = alpha * l_s[...] + jnp.sum(p, axis=-1, keepdims=True)
            acc_s[...] = alpha * acc_s[...] + _dot(p.astype(BF16), v_ref[...])
            m_s[...] = m_new

        @pl.when(ki == qi)
        def _():
            o_ref[...] = acc_s[...] / l_s[...]
            lse_ref[...] = m_s[...] + jnp.log(l_s[...])

    kv_map = lambda h, i, j: (h, jnp.minimum(i, j), 0)
    return pl.pallas_call(
        body, name="attention_forward", grid=(N_HEADS, nq, nq),
        out_shape=[jax.ShapeDtypeStruct((t_len, N_HEADS * HEAD_DIM), F32),
                   jax.ShapeDtypeStruct((N_HEADS, t_len, 1), F32)],
        in_specs=[pl.BlockSpec((None, tq, QK_DIM), lambda h, i, j: (h, i, 0)),
                  pl.BlockSpec((None, tq, QK_DIM), kv_map), pl.BlockSpec((None, tq, HEAD_DIM), kv_map)],
        out_specs=[pl.BlockSpec((tq, HEAD_DIM), lambda h, i, j: (i, h)),
                   pl.BlockSpec((None, tq, 1), lambda h, i, j: (h, i, 0))],
        scratch_shapes=[pltpu.VMEM((tq, 1), F32), pltpu.VMEM((tq, 1), F32), pltpu.VMEM((tq, HEAD_DIM), F32)],
        compiler_params=_params(32, ("arbitrary", "arbitrary", "arbitrary")),
    )(q, k, v)


def _attention_probs(q, k, lse, qi, ki, tq):
    s = _dot_nt(q, k) * ATT_SCALE
    s = jnp.where(_causal_mask(qi, ki, tq, tq), s, NEG_BIG)
    return jnp.exp(s - lse)


def _attention_backward_q(q, k, v, o, d_cat, lse):
    t_len = q.shape[1]
    tq = min(512, t_len)
    nq = t_len // tq

    def body(q_ref, k_ref, v_ref, o_ref, do_ref, lse_ref, dq_ref, acc_s, delta_s):
        qi, ki = pl.program_id(1), pl.program_id(2)

        @pl.when(ki == 0)
        def _():
            acc_s[...] = jnp.zeros_like(acc_s)
            delta_s[...] = jnp.sum(do_ref[...] * o_ref[...], axis=-1, keepdims=True)

        @pl.when(ki <= qi)
        def _():
            p = _attention_probs(q_ref[...], k_ref[...], lse_ref[...], qi, ki, tq)
            dp = _dot_nt(do_ref[...].astype(BF16), v_ref[...])
            ds = (p * (dp - delta_s[...]) * ATT_SCALE).astype(BF16)
            acc_s[...] += _dot(ds, k_ref[...])

        @pl.when(ki == qi)
        def _():
            dq_ref[...] = acc_s[...]

    kv_map = lambda h, i, j: (h, jnp.minimum(i, j), 0)
    return pl.pallas_call(
        body, name="attention_backward_q", grid=(N_HEADS, nq, nq),
        out_shape=jax.ShapeDtypeStruct((N_HEADS, t_len, QK_DIM), F32),
        in_specs=[pl.BlockSpec((None, tq, QK_DIM), lambda h, i, j: (h, i, 0)),
                  pl.BlockSpec((None, tq, QK_DIM), kv_map), pl.BlockSpec((None, tq, HEAD_DIM), kv_map),
                  pl.BlockSpec((tq, HEAD_DIM), lambda h, i, j: (i, h)),
                  pl.BlockSpec((tq, HEAD_DIM), lambda h, i, j: (i, N_HEADS + h)),
                  pl.BlockSpec((None, tq, 1), lambda h, i, j: (h, i, 0))],
        out_specs=pl.BlockSpec((None, tq, QK_DIM), lambda h, i, j: (h, i, 0)),
        scratch_shapes=[pltpu.VMEM((tq, QK_DIM), F32), pltpu.VMEM((tq, 1), F32)],
        compiler_params=_params(32, ("arbitrary", "arbitrary", "arbitrary")),
    )(q, k, v, o, d_cat, lse)


def _attention_backward_kv(q, k, v, o, d_cat, lse):
    t_len = q.shape[1]
    tq = min(512, t_len)
    nq = t_len // tq

    def body(q_ref, k_ref, v_ref, o_ref, do_ref, lse_ref, dk_ref, dv_ref, dk_s, dv_s):
        ki, qi = pl.program_id(1), pl.program_id(2)

        @pl.when(qi == 0)
        def _():
            dk_s[...] = jnp.zeros_like(dk_s)
            dv_s[...] = jnp.zeros_like(dv_s)

        @pl.when(qi >= ki)
        def _():
            do = do_ref[...]
            p = _attention_probs(q_ref[...], k_ref[...], lse_ref[...], qi, ki, tq)
            delta = jnp.sum(do * o_ref[...], axis=-1, keepdims=True)
            do_b = do.astype(BF16)
            dv_s[...] += _dot_tn(p.astype(BF16), do_b)
            dp = _dot_nt(do_b, v_ref[...])
            ds = (p * (dp - delta) * ATT_SCALE).astype(BF16)
            dk_s[...] += _dot_tn(ds, q_ref[...])

        @pl.when(qi == nq - 1)
        def _():
            dk_ref[...] = dk_s[...]
            dv_ref[...] = dv_s[...]

    q_map = lambda h, j, i: (h, jnp.maximum(i, j), 0)
    return pl.pallas_call(
        body, name="attention_backward_kv", grid=(N_HEADS, nq, nq),
        out_shape=[jax.ShapeDtypeStruct((N_HEADS, t_len, QK_DIM), F32),
                   jax.ShapeDtypeStruct((N_HEADS, t_len, HEAD_DIM), F32)],
        in_specs=[pl.BlockSpec((None, tq, QK_DIM), q_map),
                  pl.BlockSpec((None, tq, QK_DIM), lambda h, j, i: (h, j, 0)),
                  pl.BlockSpec((None, tq, HEAD_DIM), lambda h, j, i: (h, j, 0)),
                  pl.BlockSpec((tq, HEAD_DIM), lambda h, j, i: (jnp.maximum(i, j), h)),
                  pl.BlockSpec((tq, HEAD_DIM), lambda h, j, i: (jnp.maximum(i, j), N_HEADS + h)),
                  pl.BlockSpec((None, tq, 1), q_map)],
        out_specs=[pl.BlockSpec((None, tq, QK_DIM), lambda h, j, i: (h, j, 0)),
                   pl.BlockSpec((None, tq, HEAD_DIM), lambda h, j, i: (h, j, 0))],
        scratch_shapes=[pltpu.VMEM((tq, QK_DIM), F32), pltpu.VMEM((tq, HEAD_DIM), F32)],
        compiler_params=_params(32, ("arbitrary", "arbitrary", "arbitrary")),
    )(q, k, v, o, d_cat, lse)


def _out_project(o_hg, o_mla, x, g_a, w_out):
    t_len = x.shape[0]
    tm = min(512, t_len)
    half = N_HEADS * HEAD_DIM

    def body(ohg_ref, omla_ref, x_ref, ga_ref, w_ref, cat_ref, mix_ref, xhat_ref, rstd_ref):
        a = ohg_ref[...].astype(BF16)
        b = omla_ref[...].astype(BF16)
        cat_ref[:, 0:half] = a
        cat_ref[:, half:2 * half] = b
        mix = _dot(a, w_ref[0:half, :]) + _dot(b, w_ref[half:2 * half, :])
        mix_ref[...] = mix
        r1 = DN_ALPHA * x_ref[...] + (1.0 + ga_ref[...]) * mix
        xc = r1 - _rowmean(r1)
        rstd = lax.rsqrt(_rowmean(xc * xc) + LN_EPS)
        xhat_ref[...] = xc * rstd
        rstd_ref[...] = rstd

    row = lambda i: (i, 0)
    fixed = lambda i: (0, 0)
    return pl.pallas_call(
        body, name="out_project", grid=(t_len // tm,),
        out_shape=[jax.ShapeDtypeStruct((t_len, D_MODEL), BF16), jax.ShapeDtypeStruct((t_len, D_MODEL), F32),
                   jax.ShapeDtypeStruct((t_len, D_MODEL), F32), jax.ShapeDtypeStruct((t_len, 1), F32)],
        in_specs=[pl.BlockSpec((tm, half), row), pl.BlockSpec((tm, half), row), pl.BlockSpec((tm, D_MODEL), row),
                  pl.BlockSpec((1, D_MODEL), fixed), pl.BlockSpec((D_MODEL, D_MODEL), fixed)],
        out_specs=[pl.BlockSpec((tm, D_MODEL), row), pl.BlockSpec((tm, D_MODEL), row),
                   pl.BlockSpec((tm, D_MODEL), row), pl.BlockSpec((tm, 1), row)],
        compiler_params=_params(48, ("arbitrary",)),
    )(o_hg, o_mla, x, g_a, w_out)


V_LN1G, V_LN1B, V_SCM, V_SHM, V_GM, V_GA, V_LN2G, V_LN2B = range(8)
S_DLN2G, S_DLN2B, S_DGM, S_DSCM, S_DSHM, S_DLN1G, S_DLN1B, S_DGA, S_LOSS = range(9)


def _mlp_and_back(xhat1, rstd1, mix, target, vecs, w1, w2, w_out):
    t_len = xhat1.shape[0]
    tm = min(256, t_len)
    n_ff = w1.shape[0]
    ff = w1.shape[2]

    def body(xhat_ref, rstd_ref, mix_ref, tgt_ref, vec_ref, w1_hbm, w2_hbm, wout_hbm,
             act_ref, dhp_ref, um_ref, dh_ref, dmix_ref, dcat_ref, dr1_ref, sums_ref,
             w1_s, w2_s, wout_s, hp_s, load_sems):
        @pl.when(pl.program_id(0) == 0)
        def _():
            loads = [pltpu.make_async_copy(w1_hbm, w1_s, load_sems.at[0]),
                     pltpu.make_async_copy(w2_hbm, w2_s, load_sems.at[1]),
                     pltpu.make_async_copy(wout_hbm, wout_s, load_sems.at[2])]
            for cp in loads:
                cp.start()
            sums_ref[...] = jnp.zeros_like(sums_ref)
            for cp in loads:
                cp.wait()

        vec = lambda r: vec_ref[r:r + 1, :]
        xhat = xhat_ref[...]
        x1 = xhat * vec(V_LN1G) + vec(V_LN1B)
        um = (x1 * (1.0 + vec(V_SCM)) + vec(V_SHM)).astype(BF16)
        um_ref[...] = um
        h = jnp.zeros((tm, D_MODEL), F32)
        for j in range(n_ff):
            hp = _dot(um, w1_s[j])
            hp_s[j] = hp
            act = jnp.square(jnp.maximum(hp, 0.0)).astype(BF16)
            act_ref[:, j * ff:(j + 1) * ff] = act
            h = h + _dot(act, w2_s[j])
        r2 = DN_ALPHA * x1 + (1.0 + vec(V_GM)) * h
        xc = r2 - _rowmean(r2)
        rstd2 = lax.rsqrt(_rowmean(xc * xc) + LN_EPS)
        xhat2 = xc * rstd2
        err = xhat2 * vec(V_LN2G) + vec(V_LN2B) - tgt_ref[...]
        loss = 0.5 * jnp.sum(_rowmean(err * err))
        dy = err * (1.0 / D_MODEL)
        dxh = dy * vec(V_LN2G)
        dr2 = rstd2 * (dxh - _rowmean(dxh) - xhat2 * _rowmean(dxh * xhat2))
        dh = ((1.0 + vec(V_GM)) * dr2).astype(BF16)
        dh_ref[...] = dh
        sums_ref[S_DLN2G:S_DLN2G + 1, :] += _colsum(dy * xhat2)
        sums_ref[S_DLN2B:S_DLN2B + 1, :] += _colsum(dy)
        sums_ref[S_DGM:S_DGM + 1, :] += _colsum(dr2 * h)
        sums_ref[S_LOSS:S_LOSS + 1, :] += jnp.full((1, D_MODEL), loss, F32)
        du = jnp.zeros((tm, D_MODEL), F32)
        for j in range(n_ff):
            dhp = (_dot_nt(dh, w2_s[j]) * (2.0 * jnp.maximum(hp_s[j], 0.0))).astype(BF16)
            dhp_ref[:, j * ff:(j + 1) * ff] = dhp
            du = du + _dot_nt(dhp, w1_s[j])
        sums_ref[S_DSCM:S_DSCM + 1, :] += _colsum(du * x1)
        sums_ref[S_DSHM:S_DSHM + 1, :] += _colsum(du)
        dx1 = DN_ALPHA * dr2 + du * (1.0 + vec(V_SCM))
        sums_ref[S_DLN1G:S_DLN1G + 1, :] += _colsum(dx1 * xhat)
        sums_ref[S_DLN1B:S_DLN1B + 1, :] += _colsum(dx1)
        dxh1 = dx1 * vec(V_LN1G)
        dr1 = rstd_ref[...] * (dxh1 - _rowmean(dxh1) - xhat * _rowmean(dxh1 * xhat))
        dr1_ref[...] = dr1
        sums_ref[S_DGA:S_DGA + 1, :] += _colsum(dr1 * mix_ref[...])
        dmix = ((1.0 + vec(V_GA)) * dr1).astype(BF16)
        dmix_ref[...] = dmix
        dcat_ref[...] = _dot_nt(dmix, wout_s[...])

    row = lambda i: (i, 0)
    fixed = lambda i: (0, 0)
    any_spec = pl.BlockSpec(memory_space=pl.ANY)
    return pl.pallas_call(
        body, name="mlp_and_back", grid=(t_len // tm,),
        out_shape=[jax.ShapeDtypeStruct((t_len, D_FF), BF16), jax.ShapeDtypeStruct((t_len, D_FF), BF16),
                   jax.ShapeDtypeStruct((t_len, D_MODEL), BF16), jax.ShapeDtypeStruct((t_len, D_MODEL), BF16),
                   jax.ShapeDtypeStruct((t_len, D_MODEL), BF16), jax.ShapeDtypeStruct((t_len, D_MODEL), F32),
                   jax.ShapeDtypeStruct((t_len, D_MODEL), F32), jax.ShapeDtypeStruct((16, D_MODEL), F32)],
        in_specs=[pl.BlockSpec((tm, D_MODEL), row), pl.BlockSpec((tm, 1), row), pl.BlockSpec((tm, D_MODEL), row),
                  pl.BlockSpec((tm, D_MODEL), row), pl.BlockSpec((8, D_MODEL), fixed), any_spec, any_spec, any_spec],
        out_specs=[pl.BlockSpec((tm, D_FF), row), pl.BlockSpec((tm, D_FF), row), pl.BlockSpec((tm, D_MODEL), row),
                   pl.BlockSpec((tm, D_MODEL), row), pl.BlockSpec((tm, D_MODEL), row), pl.BlockSpec((tm, D_MODEL), row),
                   pl.BlockSpec((tm, D_MODEL), row), pl.BlockSpec((16, D_MODEL), fixed)],
        scratch_shapes=[pltpu.VMEM(w1.shape, BF16), pltpu.VMEM(w2.shape, BF16), pltpu.VMEM(w_out.shape, BF16),
                        pltpu.VMEM((n_ff, tm, ff), F32), pltpu.SemaphoreType.DMA((3,))],
        compiler_params=_params(56, ("arbitrary",)),
    )(xhat1, rstd1, mix, target, vecs, w1, w2, w_out)


def _in_project_backward(dq, dk, dv, cq, ckv, pos, invf, q_norm_w, kv_norm_w, w_q, w_kv,
                         d_hq, d_hf, d_hi, d_hg, w_in, dr1, x, sc_a):
    t_len = x.shape[0]
    tm = min(256, t_len)
    hgw = N_HEADS * HEAD_DIM

    def body(dq_ref, dk_ref, dv_ref, cq_ref, ckv_ref, pos_ref, invf_ref, qn_ref, kvn_ref, wq_ref, wkv_ref,
             dhq_ref, dhf_ref, dhi_ref, dhg_ref, win_ref, dr1_ref, x_ref, sc_ref,
             dz_ref, dqf_ref, dkvu_ref, cqn_ref, ckvn_ref, gx_ref, sums_ref):
        @pl.when(pl.program_id(0) == 0)
        def _():
            sums_ref[...] = jnp.zeros_like(sums_ref)

        cos_t, sin_t = _rope_tables(pos_ref[...], invf_ref[...])
        cq = cq_ref[...]
        ckv = ckv_ref[...]
        rq = lax.rsqrt(_rowmean(cq * cq) + RMS_EPS)
        rkv = lax.rsqrt(_rowmean(ckv * ckv) + RMS_EPS)
        cqn_ref[...] = (cq * rq * qn_ref[...]).astype(BF16)
        ckvn_ref[...] = (ckv * rkv * kvn_ref[...]).astype(BF16)
        d_cqn = jnp.zeros((tm, Q_RANK), F32)
        d_ckvn = jnp.zeros((tm, KV_RANK), F32)
        d_kpe = jnp.zeros((tm, 128), F32)
        for h in range(N_HEADS):
            dqh = dq_ref[h]
            dqf_ref[h, :, 0:HEAD_DIM] = dqh[:, :HEAD_DIM].astype(BF16)
            dqf_ref[h, :, HEAD_DIM:QK_DIM] = _unrope(dqh[:, HEAD_DIM:], cos_t, sin_t).astype(BF16)
            d_cqn = d_cqn + _dot_nt(dqf_ref[h], wq_ref[h])
            dkh = dk_ref[h]
            d_kpe = d_kpe + dkh[:, HEAD_DIM:]
            dkvu_ref[h, :, 0:HEAD_DIM] = dkh[:, :HEAD_DIM].astype(BF16)
            dkvu_ref[h, :, HEAD_DIM:2 * HEAD_DIM] = dv_ref[h].astype(BF16)
            d_ckvn = d_ckvn + _dot_nt(dkvu_ref[h], wkv_ref[h])
        dyq = d_cqn * qn_ref[...]
        dykv = d_ckvn * kvn_ref[...]
        sums_ref[2:3, 0:Q_RANK] += _colsum(d_cqn * cq * rq)
        sums_ref[3:4, 0:KV_RANK] += _colsum(d_ckvn * ckv * rkv)
        dz_ref[:, 0:hgw] = dhq_ref[...]
        dz_ref[:, hgw:2 * hgw] = dhf_ref[...]
        dz_ref[:, 2 * hgw:3 * hgw] = dhi_ref[...]
        dz_ref[:, 3 * hgw:4 * hgw] = dhg_ref[...]
        dz_ref[:, HG_COLS:HG_COLS + Q_RANK] = (rq * dyq - cq * (rq * rq * rq) * _rowmean(dyq * cq)).astype(BF16)
        dz_ref[:, HG_COLS + Q_RANK:HG_COLS + Q_RANK + KV_RANK] = (
            rkv * dykv - ckv * (rkv * rkv * rkv) * _rowmean(dykv * ckv)).astype(BF16)
        dz_ref[:, HG_COLS + Q_RANK + KV_RANK:] = _unrope(d_kpe, cos_t, sin_t).astype(BF16)
        du = _dot_nt(dz_ref[...], win_ref[...])
        xv = x_ref[...]
        gx_ref[...] = DN_ALPHA * dr1_ref[...] + (1.0 + sc_ref[...]) * du
        sums_ref[0:1, :] += _colsum(du * xv)
        sums_ref[1:2, :] += _colsum(du)

    row = lambda i: (i, 0)
    fixed2 = lambda i: (0, 0)
    fixed3 = lambda i: (0, 0, 0)
    heads = lambda i: (0, i, 0)
    return pl.pallas_call(
        body, name="in_project_backward", grid=(t_len // tm,),
        out_shape=[jax.ShapeDtypeStruct((t_len, IN_COLS_PAD), BF16), jax.ShapeDtypeStruct((N_HEADS, t_len, QK_DIM), BF16),
                   jax.ShapeDtypeStruct((N_HEADS, t_len, 2 * HEAD_DIM), BF16), jax.ShapeDtypeStruct((t_len, Q_RANK), BF16),
                   jax.ShapeDtypeStruct((t_len, KV_RANK), BF16), jax.ShapeDtypeStruct((t_len, D_MODEL), F32),
                   jax.ShapeDtypeStruct((8, D_MODEL), F32)],
        in_specs=[pl.BlockSpec((N_HEADS, tm, QK_DIM), heads), pl.BlockSpec((N_HEADS, tm, QK_DIM), heads),
                  pl.BlockSpec((N_HEADS, tm, HEAD_DIM), heads), pl.BlockSpec((tm, Q_RANK), row),
                  pl.BlockSpec((tm, KV_RANK), row), pl.BlockSpec((tm, 1), row), pl.BlockSpec((1, 128), fixed2),
                  pl.BlockSpec((1, Q_RANK), fixed2), pl.BlockSpec((1, KV_RANK), fixed2),
                  pl.BlockSpec((N_HEADS, Q_RANK, QK_DIM), fixed3), pl.BlockSpec((N_HEADS, KV_RANK, 2 * HEAD_DIM), fixed3),
                  pl.BlockSpec((tm, hgw), row), pl.BlockSpec((tm, hgw), row), pl.BlockSpec((tm, hgw), row),
                  pl.BlockSpec((tm, hgw), row), pl.BlockSpec((D_MODEL, IN_COLS_PAD), fixed2),
                  pl.BlockSpec((tm, D_MODEL), row), pl.BlockSpec((tm, D_MODEL), row), pl.BlockSpec((1, D_MODEL), fixed2)],
        out_specs=[pl.BlockSpec((tm, IN_COLS_PAD), row), pl.BlockSpec((N_HEADS, tm, QK_DIM), heads),
                   pl.BlockSpec((N_HEADS, tm, 2 * HEAD_DIM), heads), pl.BlockSpec((tm, Q_RANK), row),
                   pl.BlockSpec((tm, KV_RANK), row), pl.BlockSpec((tm, D_MODEL), row), pl.BlockSpec((8, D_MODEL), fixed2)],
        compiler_params=_params(48, ("arbitrary",)),
    )(dq, dk, dv, cq, ckv, pos, invf, q_norm_w, kv_norm_w, w_q, w_kv, d_hq, d_hf, d_hi, d_hg, w_in, dr1, x, sc_a)


def _weight_grad(a, b, name, n_blocks, bn, a_blocked=False, b_blocked=True):
    t_len = a.shape[0]
    m = a.shape[1] // n_blocks if a_blocked else a.shape[1]
    bt = min(512, t_len)

    def body(a_ref, b_ref, o_ref):
        @pl.when(pl.program_id(1) == 0)
        def _():
            o_ref[...] = jnp.zeros_like(o_ref)

        o_ref[...] += _dot_tn(a_ref[...].astype(BF16), b_ref[...].astype(BF16))

    a_spec = pl.BlockSpec((bt, m), (lambda n, t: (t, n)) if a_blocked else (lambda n, t: (t, 0)))
    if b.ndim == 3:
        b_spec = pl.BlockSpec((None, bt, bn), lambda n, t: (n, t, 0))
    else:
        b_spec = pl.BlockSpec((bt, bn), (lambda n, t: (t, n)) if b_blocked else (lambda n, t: (t, 0)))
    return pl.pallas_call(
        body, name=name, grid=(n_blocks, t_len // bt),
        out_shape=jax.ShapeDtypeStruct((n_blocks, m, bn), F32),
        in_specs=[a_spec, b_spec],
        out_specs=pl.BlockSpec((None, m, bn), lambda n, t: (n, 0, 0)),
        compiler_params=_params(40, ("arbitrary", "arbitrary")),
    )(a, b)


def _reduce_small(gathered, lb_raw):
    def body(g_ref, lb_ref, tot_ref, dlb_ref):
        tot = g_ref[0]
        for d in range(1, N_DEV):
            tot = tot + g_ref[d]
        tot_ref[...] = tot
        a = lb_ref[...]
        m = jnp.max(a, axis=0, keepdims=True)
        e = jnp.exp(a - m)
        lb = e[0:1] / jnp.sum(e, axis=0, keepdims=True)
        d0 = tot[10:11, 0:512] * lb * (1.0 - lb)
        dlb_ref[0:1, :] = d0
        dlb_ref[1:2, :] = -d0

    return pl.pallas_call(
        body, name="reduce_small",
        out_shape=[jax.ShapeDtypeStruct((SMALL_ROWS, D_MODEL), F32), jax.ShapeDtypeStruct((2, 512), F32)],
    )(gathered, lb_raw)


def _adamw(w, g, m, v, name):
    rows, cols = w.shape
    tr = _row_tile(rows) if rows >= 8 else rows

    def body(w_ref, g_ref, m_ref, v_ref, d_ref, nm_ref, nv_ref):
        gv = g_ref[...]
        nm = ADAM_B1 * m_ref[...] + (1.0 - ADAM_B1) * gv
        nv = ADAM_B2 * v_ref[...] + (1.0 - ADAM_B2) * jnp.square(gv)
        m_hat = nm / (1.0 - ADAM_B1 ** ADAM_STEP)
        v_hat = nv / (1.0 - ADAM_B2 ** ADAM_STEP)
        d_ref[...] = -ADAM_LR * (m_hat / (jnp.sqrt(v_hat) + ADAM_EPS) + ADAM_WD * w_ref[...])
        nm_ref[...] = nm
        nv_ref[...] = nv

    spec = pl.BlockSpec((tr, cols), lambda i: (i, 0))
    return pl.pallas_call(
        body, name=name, grid=(rows // tr,),
        out_shape=[jax.ShapeDtypeStruct(w.shape, F32)] * 3,
        in_specs=[spec] * 4, out_specs=[spec] * 3,
        compiler_params=_params(40, ("arbitrary",)),
    )(w, g, m, v)


def kernel(x, c, positions, w_ada, b_ada, w_in, hg_lower_bounds, hg_norm_w, mla_q_norm_w, w_q_up, mla_kv_norm_w, w_kv_up, w_out, ln1_g, ln1_b, w_mlp_in, w_mlp_out, ln2_g, ln2_b, loss_target, m_w_ada, m_b_ada, m_w_in, m_hg_lower_bounds, m_hg_norm_w, m_mla_q_norm_w, m_w_q_up, m_mla_kv_norm_w, m_w_kv_up, m_w_out, m_ln1_g, m_ln1_b, m_w_mlp_in, m_w_mlp_out, m_ln2_g, m_ln2_b, v_w_ada, v_b_ada, v_w_in, v_hg_lower_bounds, v_hg_norm_w, v_mla_q_norm_w, v_w_q_up, v_mla_kv_norm_w, v_w_kv_up, v_w_out, v_ln1_g, v_ln1_b, v_w_mlp_in, v_w_mlp_out, v_ln2_g, v_ln2_b):
    ix, iy, ic = _mesh_pos()
    chip = 2 * ix + iy
    me = 4 * ix + 2 * iy + ic
    core_arr = jnp.reshape(ic, (1,)).astype(jnp.int32)
    chip_arr = jnp.reshape(chip, (1,)).astype(jnp.int32)

    xs = x[0]
    target = loss_target[0]
    t_len = xs.shape[0]
    pos = positions.astype(F32).reshape(t_len, 1)
    inv = 1.0 / (ROPE_THETA ** (jnp.arange(0, ROPE_DIM, 2, dtype=F32) / ROPE_DIM))
    invf = jnp.concatenate([inv, inv, jnp.zeros((128 - ROPE_DIM,), F32)]).reshape(1, 128)

    ada_cols = w_ada.shape[2]
    c_all = _allgather8(jnp.broadcast_to(c, (8, D_MODEL)), "gather_c")[:, 0, :]
    b_shard = lax.dynamic_slice(b_ada, (0, chip * ada_cols), (1, ada_cols))
    mod_cols, cond16 = _ada_project(c_all, w_ada[0], b_shard)
    mod_all = _allgather8(mod_cols, "gather_mod")
    mod_mine = lax.dynamic_slice(mod_all, (0, me, 0), (N_DEV, 1, ada_cols))[::2, 0, :]
    mod_mine = mod_mine.reshape(6, D_MODEL)
    sh_a, sc_a, g_a, sh_m, sc_m, g_m = (mod_mine[i:i + 1] for i in range(6))

    shards = [w_in[0].astype(BF16), w_q_up[0].astype(BF16), w_kv_up[0].astype(BF16), w_out[0].astype(BF16),
              w_mlp_in[0].astype(BF16), w_mlp_out[0].astype(BF16)]
    g_in, g_q, g_kv, g_out, g_w1, g_w2 = _gather_weights(shards)
    w_in_full = jnp.transpose(g_in, (1, 0, 2)).reshape(D_MODEL, IN_COLS)
    w_in_full = jnp.pad(w_in_full, ((0, 0), (0, IN_COLS_PAD - IN_COLS)))
    w_q_full = jnp.pad(g_q, ((0, 0), (0, 0), (0, QK_DIM - g_q.shape[2])))
    w_out_full = g_out.reshape(D_MODEL, D_MODEL)

    u_a, zhg, cq, ckv, q, k, v = _in_project(xs, pos, sc_a, sh_a, w_in_full, mla_q_norm_w, mla_kv_norm_w,
                                            w_q_full, g_kv, invf)
    o_pre, o_hg, states = _hgrn_forward(zhg, hg_lower_bounds, hg_norm_w)
    o_mla, lse = _attention_forward(q, k, v)
    cat, mix, xhat1, rstd1 = _out_project(o_hg, o_mla, xs, g_a, w_out_full)
    vecs = jnp.concatenate([ln1_g, ln1_b, sc_m, sh_m, g_m, g_a, ln2_g, ln2_b], axis=0)
    act, dhp, um, dh, dmix, d_cat, dr1, mlp_sums = _mlp_and_back(xhat1, rstd1, mix, target, vecs, g_w1, g_w2, w_out_full)

    dq = _attention_backward_q(q, k, v, o_mla, d_cat, lse)
    dk, dv = _attention_backward_kv(q, k, v, o_mla, d_cat, lse)
    d_hq, d_hf, d_hi, d_hg, hg_sums = _hgrn_backward(zhg, hg_lower_bounds, hg_norm_w, o_pre, d_cat, states)
    dz, dqf, dkvu, cqn, ckvn, grad_x, in_sums = _in_project_backward(
        dq, dk, dv, cq, ckv, pos, invf, mla_q_norm_w, mla_kv_norm_w, w_q_full, g_kv,
        d_hq, d_hf, d_hi, d_hg, w_in_full, dr1, xs, sc_a)

    gw_in = _weight_grad(u_a, dz, "grad_w_in", 3, IN_COLS_PAD // 3)
    gw_in = jnp.transpose(gw_in, (1, 0, 2)).reshape(D_MODEL, IN_COLS_PAD)[:, :IN_COLS]
    gw_in = jnp.transpose(gw_in.reshape(D_MODEL, N_CHIPS, IN_COLS // N_CHIPS), (1, 0, 2))
    gw_q = _weight_grad(cqn, dqf, "grad_w_q_up", N_HEADS, QK_DIM)[:, :, :HEAD_DIM + ROPE_DIM]
    gw_kv = _weight_grad(ckvn, dkvu, "grad_w_kv_up", N_HEADS, 2 * HEAD_DIM)
    gw_out = _weight_grad(cat, dmix, "grad_w_out", 1, D_MODEL).reshape(N_CHIPS, D_MODEL // N_CHIPS, D_MODEL)
    gw_1 = _weight_grad(um, dhp, "grad_w_mlp_in", N_CHIPS, D_FF // N_CHIPS)
    gw_2 = _weight_grad(act, dh, "grad_w_mlp_out", N_CHIPS, D_MODEL, a_blocked=True, b_blocked=False)

    grads = [gw_in, gw_q, gw_kv, gw_out, gw_1, gw_2]
    landed = _pair_exchange(grads)
    chip_sums = [_add_pair(core_arr, g, l) for g, l in zip(grads, landed)]
    landed = _chip_exchange(chip_sums)
    halves = [_add_chips(chip_arr, p, l) for p, l in zip(chip_sums, landed)]
    g_w_in, g_w_q, g_w_kv, g_w_out, g_w_1, g_w_2 = _pair_assemble(halves)

    zeros = lambda n: jnp.zeros((1, n), F32)
    small = jnp.concatenate([
        in_sums[1:2], in_sums[0:1], mlp_sums[S_DGA:S_DGA + 1],
        mlp_sums[S_DSHM:S_DSHM + 1], mlp_sums[S_DSCM:S_DSCM + 1], mlp_sums[S_DGM:S_DGM + 1],
        mlp_sums[S_DLN1G:S_DLN1G + 1], mlp_sums[S_DLN1B:S_DLN1B + 1],
        mlp_sums[S_DLN2G:S_DLN2G + 1], mlp_sums[S_DLN2B:S_DLN2B + 1],
        jnp.concatenate([hg_sums[0:1], hg_sums[1:2]], axis=1),
        jnp.concatenate([in_sums[2:3, :Q_RANK], in_sums[3:4, :KV_RANK], zeros(D_MODEL - Q_RANK - KV_RANK)], axis=1),
        mlp_sums[S_LOSS:S_LOSS + 1],
        jnp.zeros((SMALL_ROWS - 13, D_MODEL), F32)], axis=0)
    small_all = _allgather8(small, "gather_small")
    tot, g_lb = _reduce_small(small_all, hg_lower_bounds)
    loss = tot[12, 0]
    g_b_ada = tot[0:6].reshape(1, 6 * D_MODEL)
    g_ln1_g, g_ln1_b, g_ln2_g, g_ln2_b = tot[6:7], tot[7:8], tot[8:9], tot[9:10]
    g_hg_norm = tot[10:11, 512:1024]
    g_q_norm = tot[11:12, 0:Q_RANK]
    g_kv_norm = tot[11:12, Q_RANK:Q_RANK + KV_RANK]

    d_mod_all = small_all[:, 0:6, :].reshape(N_DEV, 6 * D_MODEL)
    d_mod_cols = lax.dynamic_slice(d_mod_all, (0, chip * ada_cols), (N_DEV, ada_cols))
    d_mod_cols = jnp.concatenate([d_mod_cols, jnp.zeros_like(d_mod_cols)], axis=0)
    g_w_ada = _weight_grad(cond16, d_mod_cols, "grad_w_ada", 1, ada_cols)[0]

    names = ["w_ada", "b_ada", "w_in", "hg_lower_bounds", "hg_norm_w", "mla_q_norm_w", "w_q_up", "mla_kv_norm_w",
             "w_kv_up", "w_out", "ln1_g", "ln1_b", "w_mlp_in", "w_mlp_out", "ln2_g", "ln2_b"]
    weights = [w_ada, b_ada, w_in, hg_lower_bounds, hg_norm_w, mla_q_norm_w, w_q_up, mla_kv_norm_w,
               w_kv_up, w_out, ln1_g, ln1_b, w_mlp_in, w_mlp_out, ln2_g, ln2_b]
    moms = [m_w_ada, m_b_ada, m_w_in, m_hg_lower_bounds, m_hg_norm_w, m_mla_q_norm_w, m_w_q_up, m_mla_kv_norm_w,
            m_w_kv_up, m_w_out, m_ln1_g, m_ln1_b, m_w_mlp_in, m_w_mlp_out, m_ln2_g, m_ln2_b]
    vels = [v_w_ada, v_b_ada, v_w_in, v_hg_lower_bounds, v_hg_norm_w, v_mla_q_norm_w, v_w_q_up, v_mla_kv_norm_w,
            v_w_kv_up, v_w_out, v_ln1_g, v_ln1_b, v_w_mlp_in, v_w_mlp_out, v_ln2_g, v_ln2_b]
    grads2d = [g_w_ada, g_b_ada, g_w_in, g_lb, g_hg_norm, g_q_norm, g_w_q, g_kv_norm,
               g_w_kv, g_w_out, g_ln1_g, g_ln1_b, g_w_1, g_w_2, g_ln2_g, g_ln2_b]
    out_g, out_d, out_m, out_v = [], [], [], []
    for name, w, g, m, vv in zip(names, weights, grads2d, moms, vels):
        shape2 = g.shape
        d, nm, nv = _adamw(w.reshape(shape2), g, m.reshape(shape2), vv.reshape(shape2), "adamw_" + name)
        out_g.append(g.reshape(w.shape))
        out_d.append(d.reshape(w.shape))
        out_m.append(nm.reshape(w.shape))
        out_v.append(nv.reshape(w.shape))
    return (loss, grad_x[None], *out_g, *out_d, *out_m, *out_v)
```

```python
import functools

import jax
import jax.numpy as jnp
from jax import lax
from jax.experimental import pallas as pl
from jax.experimental.pallas import tpu as pltpu

F32 = jnp.float32
BF16 = jnp.bfloat16
MESH_IDS = pl.DeviceIdType.MESH

D_MODEL = 1024
N_HEADS = 4
HEAD_DIM = 128
ROPE_DIM = 64
HG_CHUNK = 64
HG_COLS = 2048
Q_RANK = 256
KV_RANK = 256
IN_COLS = 2624
IN_COLS_PAD = 2688
QK_DIM = 256
D_FF = 4096
N_CHIPS = 4
N_DEV = 8
ROPE_THETA = 10000.0
RMS_EPS = 1e-6
LN_EPS = 1e-5
DN_ALPHA = 2.0 ** 0.25
ATT_SCALE = (HEAD_DIM + ROPE_DIM) ** -0.5
NEG_BIG = -1e30
ADAM_LR = 0.001
ADAM_B1 = 0.9
ADAM_B2 = 0.999
ADAM_EPS = 1e-08
ADAM_WD = 0.01
ADAM_STEP = 10
SMALL_ROWS = 16
MIB = 1024 * 1024


def _dot(a, b):
    return jnp.dot(a, b, preferred_element_type=F32)


def _dot_nt(a, b):
    return lax.dot_general(a, b, (((1,), (1,)), ((), ())), preferred_element_type=F32)


def _dot_tn(a, b):
    return lax.dot_general(a, b, (((0,), (0,)), ((), ())), preferred_element_type=F32)


def _dot_f32(a, b):
    return jnp.dot(a, b, preferred_element_type=F32, precision=lax.Precision.HIGHEST)


def _params(vmem_mib, semantics=None):
    return pltpu.CompilerParams(vmem_limit_bytes=vmem_mib * MIB, dimension_semantics=semantics)


def _sigmoid(v):
    return 1.0 / (1.0 + jnp.exp(-v))


def _colsum(v):
    return jnp.sum(v, axis=0, keepdims=True)


def _rowmean(v):
    return jnp.mean(v, axis=-1, keepdims=True)


def _rope_tables(pos, invf):
    ang = pos * invf
    lane = lax.broadcasted_iota(jnp.int32, ang.shape, 1)
    cos_t = jnp.where(lane < ROPE_DIM, jnp.cos(ang), 0.0)
    sin = jnp.sin(ang)
    sin_t = jnp.where(lane < ROPE_DIM // 2, -sin, jnp.where(lane < ROPE_DIM, sin, 0.0))
    return cos_t, sin_t


def _swap_halves(t):
    lane = lax.broadcasted_iota(jnp.int32, t.shape, 1)
    return jnp.where(lane < ROPE_DIM // 2, pltpu.roll(t, 128 - ROPE_DIM // 2, 1), pltpu.roll(t, ROPE_DIM // 2, 1))


def _rope(t, cos_t, sin_t):
    return t * cos_t + _swap_halves(t) * sin_t


def _unrope(g, cos_t, sin_t):
    return g * cos_t - _swap_halves(g) * sin_t


def _mesh_pos():
    return lax.axis_index("x"), lax.axis_index("y"), lax.axis_index("c")


def _other_chips(x, y):
    out = []
    for dx, dy in ((1, 0), (0, 1), (1, 1)):
        px = 1 - x if dx else x
        py = 1 - y if dy else y
        out.append(((px, py), 2 * px + py))
    return out


def _allgather8(a, name):
    rows, cols = a.shape

    def body(a_ref, out_ref, send_sems, recv_sems):
        x, y, c = _mesh_pos()
        me = 4 * x + 2 * y + c
        out_ref[me] = a_ref[...]
        peers = []
        for r in range(1, N_DEV):
            px = 1 - x if r & 4 else x
            py = 1 - y if r & 2 else y
            pc = 1 - c if r & 1 else c
            peers.append(((px, py, pc), 4 * px + 2 * py + pc))

        def copy(r, block, to):
            return pltpu.make_async_remote_copy(
                src_ref=a_ref, dst_ref=out_ref.at[block], send_sem=send_sems.at[r], recv_sem=recv_sems.at[r],
                device_id=to, device_id_type=MESH_IDS)

        sends = [copy(r, me, peer) for r, (peer, _) in enumerate(peers)]
        for cp in sends:
            cp.start()
        for r, (peer, idx) in enumerate(peers):
            copy(r, idx, peer).wait_recv()
        for cp in sends:
            cp.wait_send()

    return pl.pallas_call(
        body, name=name,
        out_shape=jax.ShapeDtypeStruct((N_DEV, rows, cols), a.dtype),
        in_specs=[pl.BlockSpec(memory_space=pltpu.VMEM)],
        out_specs=pl.BlockSpec(memory_space=pltpu.VMEM),
        scratch_shapes=[pltpu.SemaphoreType.DMA((N_DEV - 1,)), pltpu.SemaphoreType.DMA((N_DEV - 1,))],
    )(a)


def _gather_weights(shards):
    n = len(shards)

    def body(*refs):
        src = refs[:n]
        out = refs[n:2 * n]
        send_a, recv_a, send_b, recv_b, local_sems = refs[2 * n:]
        x, y, c = _mesh_pos()
        k = 2 * x + y
        sibling = (x, y, 1 - c)
        chips = _other_chips(x, y)

        def half(i, which):
            h = shards[i].shape[0] // 2
            return pl.ds(pl.multiple_of(which * h, 16), h)

        local = [pltpu.make_async_copy(src[i], out[i].at[k], local_sems.at[i]) for i in range(n)]
        for cp in local:
            cp.start()

        def ici(j, i, chip_idx, to, from_src):
            rows = half(i, c)
            return pltpu.make_async_remote_copy(
                src_ref=src[i].at[rows] if from_src else out[i].at[chip_idx, rows],
                dst_ref=out[i].at[chip_idx, rows],
                send_sem=send_a.at[j, i], recv_sem=recv_a.at[j, i], device_id=to, device_id_type=MESH_IDS)

        def d2d(j, i, chip_idx, which):
            rows = half(i, which)
            return pltpu.make_async_remote_copy(
                src_ref=out[i].at[chip_idx, rows], dst_ref=out[i].at[chip_idx, rows],
                send_sem=send_b.at[j, i], recv_sem=recv_b.at[j, i], device_id=sibling, device_id_type=MESH_IDS)

        first = [ici(j, i, k, (*chip, c), True) for j, (chip, _) in enumerate(chips) for i in range(n)]
        for cp in first:
            cp.start()
        passed = []
        for j, (chip, kj) in enumerate(chips):
            for i in range(n):
                ici(j, i, kj, (*chip, c), False).wait_recv()
                cp = d2d(j, i, kj, c)
                cp.start()
                passed.append(cp)
        for j, (chip, kj) in enumerate(chips):
            for i in range(n):
                d2d(j, i, kj, 1 - c).wait_recv()
        for cp in first + passed:
            cp.wait_send()
        for cp in local:
            cp.wait()

    any_spec = pl.BlockSpec(memory_space=pl.ANY)
    return pl.pallas_call(
        body, name="gather_weights",
        out_shape=[jax.ShapeDtypeStruct((N_CHIPS,) + s.shape, s.dtype) for s in shards],
        in_specs=[any_spec] * n, out_specs=[any_spec] * n,
        scratch_shapes=[pltpu.SemaphoreType.DMA((3, n)), pltpu.SemaphoreType.DMA((3, n)),
                        pltpu.SemaphoreType.DMA((3, n)), pltpu.SemaphoreType.DMA((3, n)),
                        pltpu.SemaphoreType.DMA((n,))],
    )(*shards)


def _pair_exchange(grads):
    n = len(grads)

    def body(*refs):
        src = refs[:n]
        out = refs[n:2 * n]
        send_sems, recv_sems = refs[2 * n:]
        x, y, c = _mesh_pos()
        copies = []
        for i in range(n):
            h = grads[i].shape[1] // 2
            rows = pl.ds(pl.multiple_of((1 - c) * h, 16), h)
            copies.append(pltpu.make_async_remote_copy(
                src_ref=src[i].at[:, rows], dst_ref=out[i], send_sem=send_sems.at[i], recv_sem=recv_sems.at[i],
                device_id=(x, y, 1 - c), device_id_type=MESH_IDS))
        for cp in copies:
            cp.start()
        for cp in copies:
            cp.wait()

    any_spec = pl.BlockSpec(memory_space=pl.ANY)
    return pl.pallas_call(
        body, name="grad_pair_exchange",
        out_shape=[jax.ShapeDtypeStruct((N_CHIPS, g.shape[1] // 2, g.shape[2]), g.dtype) for g in grads],
        in_specs=[any_spec] * n, out_specs=[any_spec] * n,
        scratch_shapes=[pltpu.SemaphoreType.DMA((n,)), pltpu.SemaphoreType.DMA((n,))],
    )(*grads)


def _chip_exchange(partials):
    n = len(partials)

    def body(*refs):
        src = refs[:n]
        out = refs[n:2 * n]
        send_sems, recv_sems = refs[2 * n:]
        x, y, c = _mesh_pos()
        copies = []
        for j, (chip, kj) in enumerate(_other_chips(x, y)):
            for i in range(n):
                copies.append(pltpu.make_async_remote_copy(
                    src_ref=src[i].at[kj], dst_ref=out[i].at[j], send_sem=send_sems.at[j, i],
                    recv_sem=recv_sems.at[j, i], device_id=(*chip, c), device_id_type=MESH_IDS))
        for cp in copies:
            cp.start()
        for cp in copies:
            cp.wait()

    any_spec = pl.BlockSpec(memory_space=pl.ANY)
    return pl.pallas_call(
        body, name="grad_chip_exchange",
        out_shape=[jax.ShapeDtypeStruct((3,) + p.shape[1:], p.dtype) for p in partials],
        in_specs=[any_spec] * n, out_specs=[any_spec] * n,
        scratch_shapes=[pltpu.SemaphoreType.DMA((3, n)), pltpu.SemaphoreType.DMA((3, n))],
    )(*partials)


def _pair_assemble(halves):
    n = len(halves)

    def body(*refs):
        src = refs[:n]
        out = refs[n:2 * n]
        send_sems, recv_sems, local_sems = refs[2 * n:]
        x, y, c = _mesh_pos()
        local, copies = [], []
        for i in range(n):
            h = halves[i].shape[0]
            rows = pl.ds(pl.multiple_of(c * h, 16), h)
            theirs = pl.ds(pl.multiple_of((1 - c) * h, 16), h)
            local.append(pltpu.make_async_copy(src[i], out[i].at[rows], local_sems.at[i]))
            copies.append((
                pltpu.make_async_remote_copy(
                    src_ref=src[i], dst_ref=out[i].at[rows], send_sem=send_sems.at[i], recv_sem=recv_sems.at[i],
                    device_id=(x, y, 1 - c), device_id_type=MESH_IDS),
                pltpu.make_async_remote_copy(
                    src_ref=src[i], dst_ref=out[i].at[theirs], send_sem=send_sems.at[i], recv_sem=recv_sems.at[i],
                    device_id=(x, y, 1 - c), device_id_type=MESH_IDS)))
        for cp in local:
            cp.start()
        for send, _ in copies:
            send.start()
        for send, recv in copies:
            recv.wait_recv()
            send.wait_send()
        for cp in local:
            cp.wait()

    any_spec = pl.BlockSpec(memory_space=pl.ANY)
    return pl.pallas_call(
        body, name="grad_pair_assemble",
        out_shape=[jax.ShapeDtypeStruct((2 * s.shape[0], s.shape[1]), s.dtype) for s in halves],
        in_specs=[any_spec] * n, out_specs=[any_spec] * n,
        scratch_shapes=[pltpu.SemaphoreType.DMA((n,)), pltpu.SemaphoreType.DMA((n,)), pltpu.SemaphoreType.DMA((n,))],
    )(*halves)


def _row_tile(rows):
    for t in (256, 128, 64, 32, 16, 8):
        if rows % t == 0:
            return t
    return rows


def _add_pair(core, grad, landed):
    _, h, cols = landed.shape
    tr = _row_tile(h)
    nt = h // tr

    def body(core_ref, g_ref, l_ref, o_ref):
        o_ref[...] = g_ref[...] + l_ref[...]

    return pl.pallas_call(
        body, name="grad_add_pair",
        out_shape=jax.ShapeDtypeStruct(landed.shape, F32),
        grid_spec=pltpu.PrefetchScalarGridSpec(
            num_scalar_prefetch=1, grid=(N_CHIPS, nt),
            in_specs=[pl.BlockSpec((None, tr, cols), lambda k, t, core_ref: (k, core_ref[0] * nt + t, 0)),
                      pl.BlockSpec((None, tr, cols), lambda k, t, core_ref: (k, t, 0))],
            out_specs=pl.BlockSpec((None, tr, cols), lambda k, t, core_ref: (k, t, 0))),
        compiler_params=_params(32, ("arbitrary", "arbitrary")),
    )(core, grad, landed)


def _add_chips(chip, partial, landed):
    _, h, cols = partial.shape
    tr = _row_tile(h)

    def body(chip_ref, p_ref, l_ref, o_ref):
        o_ref[...] = ((p_ref[...] + l_ref[0]) + l_ref[1]) + l_ref[2]

    return pl.pallas_call(
        body, name="grad_add_chips",
        out_shape=jax.ShapeDtypeStruct((h, cols), F32),
        grid_spec=pltpu.PrefetchScalarGridSpec(
            num_scalar_prefetch=1, grid=(h // tr,),
            in_specs=[pl.BlockSpec((None, tr, cols), lambda t, chip_ref: (chip_ref[0], t, 0)),
                      pl.BlockSpec((3, tr, cols), lambda t, chip_ref: (0, t, 0))],
            out_specs=pl.BlockSpec((tr, cols), lambda t, chip_ref: (t, 0))),
        compiler_params=_params(32, ("arbitrary",)),
    )(chip, partial, landed)


def _ada_project(c_all, w_ada, b_shard):
    n = w_ada.shape[1]
    tn = 512

    def body(c_ref, w_ref, b_ref, mod_ref, cond_ref):
        cv = c_ref[...]
        cond = cv * _sigmoid(cv)
        mod_ref[...] = _dot(cond.astype(BF16), w_ref[...].astype(BF16)) + b_ref[...]
        cond_ref[0:N_DEV, :] = cond
        cond_ref[N_DEV:2 * N_DEV, :] = jnp.zeros_like(cond)

    return pl.pallas_call(
        body, name="ada_project", grid=(n // tn,),
        out_shape=[jax.ShapeDtypeStruct((N_DEV, n), F32), jax.ShapeDtypeStruct((2 * N_DEV, D_MODEL), F32)],
        in_specs=[pl.BlockSpec((N_DEV, D_MODEL), lambda j: (0, 0)), pl.BlockSpec((D_MODEL, tn), lambda j: (0, j)),
                  pl.BlockSpec((1, tn), lambda j: (0, j))],
        out_specs=[pl.BlockSpec((N_DEV, tn), lambda j: (0, j)), pl.BlockSpec((2 * N_DEV, D_MODEL), lambda j: (0, 0))],
        compiler_params=_params(32, ("arbitrary",)),
    )(c_all, w_ada, b_shard)


def _in_project(x, pos, sc_a, sh_a, w_in, q_norm_w, kv_norm_w, w_q, w_kv, invf):
    t_len = x.shape[0]
    tm = min(256, t_len)

    def body(x_ref, pos_ref, sc_ref, sh_ref, win_ref, qn_ref, kvn_ref, wq_ref, wkv_ref, invf_ref,
             u_ref, zhg_ref, cq_ref, ckv_ref, q_ref, k_ref, kt_ref, v_ref, vt_ref):
        u = (x_ref[...] * (1.0 + sc_ref[...]) + sh_ref[...]).astype(BF16)
        u_ref[...] = u
        z = _dot(u, win_ref[...])
        zhg_ref[...] = z[:, :HG_COLS]
        cq = z[:, HG_COLS:HG_COLS + Q_RANK]
        ckv = z[:, HG_COLS + Q_RANK:HG_COLS + Q_RANK + KV_RANK]
        cq_ref[...] = cq
        ckv_ref[...] = ckv
        cos_t, sin_t = _rope_tables(pos_ref[...], invf_ref[...])
        k_pe = _rope(z[:, HG_COLS + Q_RANK + KV_RANK:], cos_t, sin_t)
        k_pe_t = jnp.transpose(k_pe).astype(BF16)
        cqn = (cq * lax.rsqrt(_rowmean(cq * cq) + RMS_EPS) * qn_ref[...]).astype(BF16)
        ckvn = (ckv * lax.rsqrt(_rowmean(ckv * ckv) + RMS_EPS) * kvn_ref[...]).astype(BF16)
        for h in range(N_HEADS):
            qh = _dot(cqn, wq_ref[h])
            q_ref[h, :, 0:HEAD_DIM] = qh[:, :HEAD_DIM].astype(BF16)
            q_ref[h, :, HEAD_DIM:QK_DIM] = _rope(qh[:, HEAD_DIM:], cos_t, sin_t).astype(BF16)
            kvh = _dot(ckvn, wkv_ref[h])
            k_ref[h, :, 0:HEAD_DIM] = kvh[:, :HEAD_DIM].astype(BF16)
            k_ref[h, :, HEAD_DIM:QK_DIM] = k_pe.astype(BF16)
            kt_ref[h, 0:HEAD_DIM, :] = jnp.transpose(kvh[:, :HEAD_DIM]).astype(BF16)
            kt_ref[h, HEAD_DIM:QK_DIM, :] = k_pe_t
            v_ref[h] = kvh[:, HEAD_DIM:].astype(BF16)
            vt_ref[h] = jnp.transpose(kvh[:, HEAD_DIM:]).astype(BF16)

    row = lambda i: (i, 0)
    fixed2 = lambda i: (0, 0)
    fixed3 = lambda i: (0, 0, 0)
    heads = lambda i: (0, i, 0)
    return pl.pallas_call(
        body, name="in_project", grid=(t_len // tm,),
        out_shape=[jax.ShapeDtypeStruct((t_len, D_MODEL), BF16), jax.ShapeDtypeStruct((t_len, HG_COLS), F32),
                   jax.ShapeDtypeStruct((t_len, Q_RANK), F32), jax.ShapeDtypeStruct((t_len, KV_RANK), F32),
                   jax.ShapeDtypeStruct((N_HEADS, t_len, QK_DIM), BF16),
                   jax.ShapeDtypeStruct((N_HEADS, t_len, QK_DIM), BF16),
                   jax.ShapeDtypeStruct((N_HEADS, QK_DIM, t_len), BF16),
                   jax.ShapeDtypeStruct((N_HEADS, t_len, HEAD_DIM), BF16),
                   jax.ShapeDtypeStruct((N_HEADS, HEAD_DIM, t_len), BF16)],
        in_specs=[pl.BlockSpec((tm, D_MODEL), row), pl.BlockSpec((tm, 1), row),
                  pl.BlockSpec((1, D_MODEL), fixed2), pl.BlockSpec((1, D_MODEL), fixed2),
                  pl.BlockSpec((D_MODEL, IN_COLS_PAD), fixed2),
                  pl.BlockSpec((1, Q_RANK), fixed2), pl.BlockSpec((1, KV_RANK), fixed2),
                  pl.BlockSpec((N_HEADS, Q_RANK, QK_DIM), fixed3), pl.BlockSpec((N_HEADS, KV_RANK, 2 * HEAD_DIM), fixed3),
                  pl.BlockSpec((1, 128), fixed2)],
        out_specs=[pl.BlockSpec((tm, D_MODEL), row), pl.BlockSpec((tm, HG_COLS), row),
                   pl.BlockSpec((tm, Q_RANK), row), pl.BlockSpec((tm, KV_RANK), row),
                   pl.BlockSpec((N_HEADS, tm, QK_DIM), heads), pl.BlockSpec((N_HEADS, tm, QK_DIM), heads),
                   pl.BlockSpec((N_HEADS, QK_DIM, tm), lambda i: (0, 0, i)),
                   pl.BlockSpec((N_HEADS, tm, HEAD_DIM), heads),
                   pl.BlockSpec((N_HEADS, HEAD_DIM, tm), lambda i: (0, 0, i))],
        compiler_params=_params(48, ("arbitrary",)),
    )(x, pos, sc_a, sh_a, w_in, q_norm_w, kv_norm_w, w_q, w_kv, invf)


def _lower_bound(lb_raw):
    m = jnp.max(lb_raw, axis=0, keepdims=True)
    e = jnp.exp(lb_raw - m)
    return e[0:1] / jnp.sum(e, axis=0, keepdims=True)


def _tri(inclusive_lower):
    r = lax.broadcasted_iota(jnp.int32, (HG_CHUNK, HG_CHUNK), 0)
    c = lax.broadcasted_iota(jnp.int32, (HG_CHUNK, HG_CHUNK), 1)
    return (c <= r) if inclusive_lower else (c >= r)


def _hg_chunk(q, f_logit, lb):
    sg = _sigmoid(f_logit)
    forget = lb + (1.0 - lb) * sg
    kk = 1.0 - forget
    b = _dot_f32(_tri(True).astype(F32), jnp.log(forget))
    b_ref = b[HG_CHUNK // 2 - 1:HG_CHUNK // 2]
    b_last = b[HG_CHUNK - 1:HG_CHUNK]
    e_i = jnp.exp(b - b_ref)
    e_ri = jnp.exp(b_ref - b)
    e_b = jnp.exp(b)
    e_l = jnp.exp(b_last - b)
    return dict(sg=sg, forget=forget, e_i=e_i, e_ri=e_ri, e_b=e_b, e_l=e_l, dec=jnp.exp(b_last),
                qi=q * e_i, ki=kk * e_ri, qe=q * e_b, kl=kk * e_l)


def _hgrn_forward(zhg, lb_raw, norm_w):
    t_len = zhg.shape[0]
    tb = min(512, t_len)
    n_chunks = tb // HG_CHUNK

    def body(q_ref, f_ref, v_ref, g_ref, lb_ref, w_ref, opre_ref, o_ref, st_ref, state):
        @pl.when(pl.program_id(1) == 0)
        def _():
            state[...] = jnp.zeros_like(state)

        lb = _lower_bound(lb_ref[...])
        causal = _tri(True)
        for n in range(n_chunks):
            rows = pl.ds(n * HG_CHUNK, HG_CHUNK)
            v = v_ref[rows, :].astype(BF16)
            ch = _hg_chunk(q_ref[rows, :], f_ref[rows, :], lb)
            a = jnp.where(causal, _dot_nt(ch["qi"].astype(BF16), ch["ki"].astype(BF16)), 0.0)
            st = state[...]
            st_ref[0, n] = st
            o = _dot(a.astype(BF16), v) + _dot_nt(ch["qe"].astype(BF16), st.astype(BF16))
            state[...] = st * ch["dec"] + _dot_tn(v, ch["kl"].astype(BF16))
            opre_ref[rows, :] = o
            on = o * lax.rsqrt(_rowmean(o * o) + RMS_EPS) * w_ref[...]
            g = g_ref[rows, :]
            o_ref[rows, :] = on * (g * _sigmoid(g))

    col = lambda off: (lambda h, t: (t, off + h))
    return pl.pallas_call(
        body, name="hgrn_forward", grid=(N_HEADS, t_len // tb),
        out_shape=[jax.ShapeDtypeStruct((t_len, N_HEADS * HEAD_DIM), F32),
                   jax.ShapeDtypeStruct((t_len, N_HEADS * HEAD_DIM), F32),
                   jax.ShapeDtypeStruct((N_HEADS, t_len // HG_CHUNK, HEAD_DIM, HEAD_DIM), F32)],
        in_specs=[pl.BlockSpec((tb, HEAD_DIM), col(0)), pl.BlockSpec((tb, HEAD_DIM), col(N_HEADS)),
                  pl.BlockSpec((tb, HEAD_DIM), col(2 * N_HEADS)), pl.BlockSpec((tb, HEAD_DIM), col(3 * N_HEADS)),
                  pl.BlockSpec((2, HEAD_DIM), lambda h, t: (0, h)), pl.BlockSpec((1, HEAD_DIM), lambda h, t: (0, h))],
        out_specs=[pl.BlockSpec((tb, HEAD_DIM), col(0)), pl.BlockSpec((tb, HEAD_DIM), col(0)),
                   pl.BlockSpec((1, n_chunks, HEAD_DIM, HEAD_DIM), lambda h, t: (h, t, 0, 0))],
        scratch_shapes=[pltpu.VMEM((HEAD_DIM, HEAD_DIM), F32)],
        compiler_params=_params(32, ("arbitrary", "arbitrary")),
    )(zhg, zhg, zhg, zhg, lb_raw, norm_w)


def _hgrn_backward(zhg, lb_raw, norm_w, o_pre, d_cat, states):
    t_len = zhg.shape[0]
    tb = min(512, t_len)
    n_chunks = tb // HG_CHUNK
    nb = t_len // tb

    def body(q_ref, f_ref, v_ref, g_ref, lb_ref, w_ref, opre_ref, do_ref, st_ref,
             dq_ref, df_ref, dv_ref, dg_ref, sums_ref, gstate):
        @pl.when(pl.program_id(1) == 0)
        def _():
            gstate[...] = jnp.zeros_like(gstate)
            sums_ref[...] = jnp.zeros_like(sums_ref)

        lb = _lower_bound(lb_ref[...])
        w = w_ref[...]
        causal = _tri(True)
        upper = _tri(False).astype(F32)
        row_id = lax.broadcasted_iota(jnp.int32, (HG_CHUNK, HEAD_DIM), 0)
        d_lb = jnp.zeros((1, HEAD_DIM), F32)
        d_w = jnp.zeros((1, HEAD_DIM), F32)
        for n in reversed(range(n_chunks)):
            rows = pl.ds(n * HG_CHUNK, HG_CHUNK)
            o = opre_ref[rows, :]
            g = g_ref[rows, :]
            d_out = do_ref[rows, :]
            r = lax.rsqrt(_rowmean(o * o) + RMS_EPS)
            sg_g = _sigmoid(g)
            silu = g * sg_g
            dg_ref[rows, :] = (d_out * (o * r * w) * (sg_g * (1.0 + g * (1.0 - sg_g)))).astype(BF16)
            d_on = d_out * silu
            d_w = d_w + _colsum(d_on * o * r)
            dy = d_on * w
            d_o = (r * dy - o * (r * r * r) * _rowmean(dy * o)).astype(BF16)
            vf = v_ref[rows, :]
            v = vf.astype(BF16)
            ch = _hg_chunk(q_ref[rows, :], f_ref[rows, :], lb)
            qi, ki, qe, kl = (ch[name].astype(BF16) for name in ("qi", "ki", "qe", "kl"))
            st = st_ref[0, n]
            gt = gstate[...]
            a = jnp.where(causal, _dot_nt(qi, ki), 0.0).astype(BF16)
            d_a = jnp.where(causal, _dot_nt(d_o, v), 0.0).astype(BF16)
            gt_b = gt.astype(BF16)
            d_v = _dot_tn(a, d_o) + _dot_nt(kl, gt_b)
            d_qi = _dot(d_a, ki)
            d_ki = _dot_tn(d_a, qi)
            d_qe = _dot(d_o, st.astype(BF16))
            d_kl = _dot(v, gt_b)
            d_dec = _colsum(gt * st)
            gstate[...] = gt * ch["dec"] + _dot_tn(d_o, qe)
            dq_ref[rows, :] = (d_qi * ch["e_i"] + d_qe * ch["e_b"]).astype(BF16)
            d_k = d_ki * ch["e_ri"] + d_kl * ch["e_l"]
            t_qi = d_qi * ch["qi"]
            t_ki = d_ki * ch["ki"]
            t_kl = d_kl * ch["kl"]
            d_b = t_qi - t_ki + d_qe * ch["qe"] - t_kl
            d_b = d_b + jnp.where(row_id == HG_CHUNK // 2 - 1, _colsum(t_ki - t_qi), 0.0)
            d_b = d_b + jnp.where(row_id == HG_CHUNK - 1, _colsum(t_kl) + d_dec * ch["dec"], 0.0)
            d_forget = _dot_f32(upper, d_b) / ch["forget"] - d_k
            sg = ch["sg"]
            df_ref[rows, :] = (d_forget * (1.0 - lb) * sg * (1.0 - sg)).astype(BF16)
            d_lb = d_lb + _colsum(d_forget * (1.0 - sg))
            dv_ref[rows, :] = d_v.astype(BF16)
        sums_ref[0:1, :] += d_lb
        sums_ref[1:2, :] += d_w

    col = lambda off: (lambda h, t: (nb - 1 - t, off + h))
    return pl.pallas_call(
        body, name="hgrn_backward", grid=(N_HEADS, nb),
        out_shape=[jax.ShapeDtypeStruct((t_len, N_HEADS * HEAD_DIM), BF16)] * 4
        + [jax.ShapeDtypeStruct((8, N_HEADS * HEAD_DIM), F32)],
        in_specs=[pl.BlockSpec((tb, HEAD_DIM), col(0)), pl.BlockSpec((tb, HEAD_DIM), col(N_HEADS)),
                  pl.BlockSpec((tb, HEAD_DIM), col(2 * N_HEADS)), pl.BlockSpec((tb, HEAD_DIM), col(3 * N_HEADS)),
                  pl.BlockSpec((2, HEAD_DIM), lambda h, t: (0, h)), pl.BlockSpec((1, HEAD_DIM), lambda h, t: (0, h)),
                  pl.BlockSpec((tb, HEAD_DIM), col(0)), pl.BlockSpec((tb, HEAD_DIM), col(0)),
                  pl.BlockSpec((1, n_chunks, HEAD_DIM, HEAD_DIM), lambda h, t: (h, nb - 1 - t, 0, 0))],
        out_specs=[pl.BlockSpec((tb, HEAD_DIM), col(0))] * 4 + [pl.BlockSpec((8, HEAD_DIM), lambda h, t: (0, h))],
        scratch_shapes=[pltpu.VMEM((HEAD_DIM, HEAD_DIM), F32)],
        compiler_params=_params(32, ("arbitrary", "arbitrary")),
    )(zhg, zhg, zhg, zhg, lb_raw, norm_w, o_pre, d_cat, states)


ATT_LOG2 = ATT_SCALE * 1.4426950408889634


def _triangle_steps(nq, q_major):
    if q_major:
        pairs = [(i, j) for i in range(nq) for j in range(i + 1)]
    else:
        pairs = [(i, j) for j in range(nq) for i in range(j, nq)]
    return jnp.array([p[0] for p in pairs], jnp.int32), jnp.array([p[1] for p in pairs], jnp.int32)


def _key_le_query(t):
    return lax.broadcasted_iota(jnp.int32, (t, t), 0) <= lax.broadcasted_iota(jnp.int32, (t, t), 1)


def _attention_forward(q, k, v_t):
    t_len = q.shape[1]
    tq = min(512, t_len)
    nq = t_len // tq
    qi_tab, ki_tab = _triangle_steps(nq, True)

    def body(qi_ref, ki_ref, q_ref, k_ref, vt_ref, o_ref, lse_ref, m_s, l_s, acc_s):
        step = pl.program_id(0)
        qi, ki = qi_ref[step], ki_ref[step]

        @pl.when(ki == 0)
        def _():
            m_s[...] = jnp.full_like(m_s, NEG_BIG)
            l_s[...] = jnp.zeros_like(l_s)
            acc_s[...] = jnp.zeros_like(acc_s)

        def accumulate(masked):
            for h in range(N_HEADS):
                s_t = _dot_nt(k_ref[h], q_ref[h]) * ATT_LOG2
                if masked:
                    s_t = jnp.where(_key_le_query(tq), s_t, NEG_BIG)
                m_old = m_s[h]
                m_new = jnp.maximum(m_old, jnp.max(s_t, axis=0, keepdims=True))
                alpha = jnp.exp2(m_old - m_new)
                p_t = jnp.exp2(s_t - m_new)
                l_s[h] = alpha * l_s[h] + jnp.sum(p_t, axis=0, keepdims=True)
                acc_s[h] = alpha * acc_s[h] + _dot(vt_ref[h], p_t.astype(BF16))
                m_s[h] = m_new

        @pl.when(ki < qi)
        def _():
            accumulate(False)

        @pl.when(ki == qi)
        def _():
            accumulate(True)
            for h in range(N_HEADS):
                o_ref[:, h * HEAD_DIM:(h + 1) * HEAD_DIM] = jnp.transpose(acc_s[h] / l_s[h])
                lse_ref[h] = m_s[h] + jnp.log2(l_s[h])

    return pl.pallas_call(
        body, name="attention_forward",
        out_shape=[jax.ShapeDtypeStruct((t_len, N_HEADS * HEAD_DIM), F32),
                   jax.ShapeDtypeStruct((N_HEADS, 1, t_len), F32)],
        grid_spec=pltpu.PrefetchScalarGridSpec(
            num_scalar_prefetch=2, grid=(qi_tab.shape[0],),
            in_specs=[pl.BlockSpec((N_HEADS, tq, QK_DIM), lambda s, qt, kt: (0, qt[s], 0)),
                      pl.BlockSpec((N_HEADS, tq, QK_DIM), lambda s, qt, kt: (0, kt[s], 0)),
                      pl.BlockSpec((N_HEADS, HEAD_DIM, tq), lambda s, qt, kt: (0, 0, kt[s]))],
            out_specs=[pl.BlockSpec((tq, N_HEADS * HEAD_DIM), lambda s, qt, kt: (qt[s], 0)),
                       pl.BlockSpec((N_HEADS, 1, tq), lambda s, qt, kt: (0, 0, qt[s]))],
            scratch_shapes=[pltpu.VMEM((N_HEADS, 1, tq), F32), pltpu.VMEM((N_HEADS, 1, tq), F32),
                            pltpu.VMEM((N_HEADS, HEAD_DIM, tq), F32)]),
        compiler_params=_params(48, ("arbitrary",)),
    )(qi_tab, ki_tab, q, k, v_t)


BWD_HEADS = 2


def _attention_backward(q, k, k_t, v, o, d_cat, lse):
    t_len = q.shape[1]
    tq = min(512, t_len)
    nq = t_len // tq
    hp = BWD_HEADS
    qi_tab, ki_tab = _triangle_steps(nq, False)

    def body(qi_ref, ki_ref, q_ref, k_ref, kt_ref, v_ref, o_ref, do_ref, lse_ref, dqt_hbm, dk_ref, dv_ref,
             dqt_s, dk_s, dv_s):
        group, step = pl.program_id(0), pl.program_id(1)
        qi, ki = qi_ref[step], ki_ref[step]

        @pl.when(step == 0)
        def _():
            dqt_s[...] = jnp.zeros_like(dqt_s)

        @pl.when(qi == ki)
        def _():
            dk_s[...] = jnp.zeros_like(dk_s)
            dv_s[...] = jnp.zeros_like(dv_s)

        def accumulate(masked):
            ones = jnp.ones((8, HEAD_DIM), F32)
            for h in range(hp):
                cols = slice(h * HEAD_DIM, (h + 1) * HEAD_DIM)
                do = do_ref[:, cols]
                delta = lax.dot_general(ones, do * o_ref[:, cols], (((1,), (1,)), ((), ())),
                                        preferred_element_type=F32, precision=lax.Precision.HIGHEST)[0:1]
                s_t = _dot_nt(k_ref[h], q_ref[h]) * ATT_LOG2
                if masked:
                    s_t = jnp.where(_key_le_query(tq), s_t, NEG_BIG)
                p_t = jnp.exp2(s_t - lse_ref[h])
                do_b = do.astype(BF16)
                dp_t = _dot_nt(v_ref[h], do_b)
                ds_t = (p_t * (dp_t - delta) * ATT_SCALE).astype(BF16)
                dv_s[h] += _dot(p_t.astype(BF16), do_b)
                dk_s[h] += _dot(ds_t, q_ref[h])
                dqt_s[h, qi] += _dot(kt_ref[h], ds_t)

        @pl.when(ki < qi)
        def _():
            accumulate(False)

        @pl.when(ki == qi)
        def _():
            accumulate(True)
            for h in range(hp):
                pltpu.sync_copy(dqt_s.at[h, qi], dqt_hbm.at[group * hp + h, qi])

        @pl.when(qi == nq - 1)
        def _():
            dk_ref[...] = dk_s[...]
            dv_ref[...] = dv_s[...]

    wide = hp * HEAD_DIM
    return pl.pallas_call(
        body, name="attention_backward",
        out_shape=[jax.ShapeDtypeStruct((N_HEADS, nq, QK_DIM, tq), F32),
                   jax.ShapeDtypeStruct((N_HEADS, t_len, QK_DIM), F32),
                   jax.ShapeDtypeStruct((N_HEADS, t_len, HEAD_DIM), F32)],
        grid_spec=pltpu.PrefetchScalarGridSpec(
            num_scalar_prefetch=2, grid=(N_HEADS // hp, qi_tab.shape[0]),
            in_specs=[pl.BlockSpec((hp, tq, QK_DIM), lambda g, s, qt, kt: (g, qt[s], 0)),
                      pl.BlockSpec((hp, tq, QK_DIM), lambda g, s, qt, kt: (g, kt[s], 0)),
                      pl.BlockSpec((hp, QK_DIM, tq), lambda g, s, qt, kt: (g, 0, kt[s])),
                      pl.BlockSpec((hp, tq, HEAD_DIM), lambda g, s, qt, kt: (g, kt[s], 0)),
                      pl.BlockSpec((tq, wide), lambda g, s, qt, kt: (qt[s], g)),
                      pl.BlockSpec((tq, wide), lambda g, s, qt, kt: (qt[s], N_HEADS // hp + g)),
                      pl.BlockSpec((hp, 1, tq), lambda g, s, qt, kt: (g, 0, qt[s]))],
            out_specs=[pl.BlockSpec(memory_space=pl.ANY),
                       pl.BlockSpec((hp, tq, QK_DIM), lambda g, s, qt, kt: (g, kt[s], 0)),
                       pl.BlockSpec((hp, tq, HEAD_DIM), lambda g, s, qt, kt: (g, kt[s], 0))],
            scratch_shapes=[pltpu.VMEM((hp, nq, QK_DIM, tq), F32), pltpu.VMEM((hp, tq, QK_DIM), F32),
                            pltpu.VMEM((hp, tq, HEAD_DIM), F32)]),
        compiler_params=_params(48, ("arbitrary", "arbitrary")),
    )(qi_tab, ki_tab, q, k, k_t, v, o, d_cat, lse)


def _out_project(o_hg, o_mla, x, g_a, w_out):
    t_len = x.shape[0]
    tm = min(512, t_len)
    half = N_HEADS * HEAD_DIM

    def body(ohg_ref, omla_ref, x_ref, ga_ref, w_ref, cat_ref, mix_ref, xhat_ref, rstd_ref):
        a = ohg_ref[...].astype(BF16)
        b = omla_ref[...].astype(BF16)
        cat_ref[:, 0:half] = a
        cat_ref[:, half:2 * half] = b
        mix = _dot(a, w_ref[0:half, :]) + _dot(b, w_ref[half:2 * half, :])
        mix_ref[...] = mix
        r1 = DN_ALPHA * x_ref[...] + (1.0 + ga_ref[...]) * mix
        xc = r1 - _rowmean(r1)
        rstd = lax.rsqrt(_rowmean(xc * xc) + LN_EPS)
        xhat_ref[...] = xc * rstd
        rstd_ref[...] = rstd

    row = lambda i: (i, 0)
    fixed = lambda i: (0, 0)
    return pl.pallas_call(
        body, name="out_project", grid=(t_len // tm,),
        out_shape=[jax.ShapeDtypeStruct((t_len, D_MODEL), BF16), jax.ShapeDtypeStruct((t_len, D_MODEL), F32),
                   jax.ShapeDtypeStruct((t_len, D_MODEL), F32), jax.ShapeDtypeStruct((t_len, 1), F32)],
        in_specs=[pl.BlockSpec((tm, half), row), pl.BlockSpec((tm, half), row), pl.BlockSpec((tm, D_MODEL), row),
                  pl.BlockSpec((1, D_MODEL), fixed), pl.BlockSpec((D_MODEL, D_MODEL), fixed)],
        out_specs=[pl.BlockSpec((tm, D_MODEL), row), pl.BlockSpec((tm, D_MODEL), row),
                   pl.BlockSpec((tm, D_MODEL), row), pl.BlockSpec((tm, 1), row)],
        compiler_params=_params(48, ("arbitrary",)),
    )(o_hg, o_mla, x, g_a, w_out)


V_LN1G, V_LN1B, V_SCM, V_SHM, V_GM, V_GA, V_LN2G, V_LN2B = range(8)
S_DLN2G, S_DLN2B, S_DGM, S_DSCM, S_DSHM, S_DLN1G, S_DLN1B, S_DGA, S_LOSS = range(9)


def _mlp_and_back(xhat1, rstd1, mix, target, vecs, w1, w2, w_out):
    t_len = xhat1.shape[0]
    tm = min(256, t_len)
    n_ff = w1.shape[0]
    ff = w1.shape[2]

    def body(xhat_ref, rstd_ref, mix_ref, tgt_ref, vec_ref, w1_hbm, w2_hbm, wout_hbm,
             act_ref, dhp_ref, um_ref, dh_ref, dmix_ref, dcat_ref, dr1_ref, sums_ref,
             w1_s, w2_s, wout_s, hp_s, load_sems):
        @pl.when(pl.program_id(0) == 0)
        def _():
            loads = [pltpu.make_async_copy(w1_hbm, w1_s, load_sems.at[0]),
                     pltpu.make_async_copy(w2_hbm, w2_s, load_sems.at[1]),
                     pltpu.make_async_copy(wout_hbm, wout_s, load_sems.at[2])]
            for cp in loads:
                cp.start()
            sums_ref[...] = jnp.zeros_like(sums_ref)
            for cp in loads:
                cp.wait()

        vec = lambda r: vec_ref[r:r + 1, :]
        xhat = xhat_ref[...]
        x1 = xhat * vec(V_LN1G) + vec(V_LN1B)
        um = (x1 * (1.0 + vec(V_SCM)) + vec(V_SHM)).astype(BF16)
        um_ref[...] = um
        h = jnp.zeros((tm, D_MODEL), F32)
        for j in range(n_ff):
            hp = _dot(um, w1_s[j])
            hp_s[j] = hp
            act = jnp.square(jnp.maximum(hp, 0.0)).astype(BF16)
            act_ref[:, j * ff:(j + 1) * ff] = act
            h = h + _dot(act, w2_s[j])
        r2 = DN_ALPHA * x1 + (1.0 + vec(V_GM)) * h
        xc = r2 - _rowmean(r2)
        rstd2 = lax.rsqrt(_rowmean(xc * xc) + LN_EPS)
        xhat2 = xc * rstd2
        err = xhat2 * vec(V_LN2G) + vec(V_LN2B) - tgt_ref[...]
        loss = 0.5 * jnp.sum(_rowmean(err * err))
        dy = err * (1.0 / D_MODEL)
        dxh = dy * vec(V_LN2G)
        dr2 = rstd2 * (dxh - _rowmean(dxh) - xhat2 * _rowmean(dxh * xhat2))
        dh = ((1.0 + vec(V_GM)) * dr2).astype(BF16)
        dh_ref[...] = dh
        sums_ref[S_DLN2G:S_DLN2G + 1, :] += _colsum(dy * xhat2)
        sums_ref[S_DLN2B:S_DLN2B + 1, :] += _colsum(dy)
        sums_ref[S_DGM:S_DGM + 1, :] += _colsum(dr2 * h)
        sums_ref[S_LOSS:S_LOSS + 1, :] += jnp.full((1, D_MODEL), loss, F32)
        du = jnp.zeros((tm, D_MODEL), F32)
        for j in range(n_ff):
            dhp = (_dot_nt(dh, w2_s[j]) * (2.0 * jnp.maximum(hp_s[j], 0.0))).astype(BF16)
            dhp_ref[:, j * ff:(j + 1) * ff] = dhp
            du = du + _dot_nt(dhp, w1_s[j])
        sums_ref[S_DSCM:S_DSCM + 1, :] += _colsum(du * x1)
        sums_ref[S_DSHM:S_DSHM + 1, :] += _colsum(du)
        dx1 = DN_ALPHA * dr2 + du * (1.0 + vec(V_SCM))
        sums_ref[S_DLN1G:S_DLN1G + 1, :] += _colsum(dx1 * xhat)
        sums_ref[S_DLN1B:S_DLN1B + 1, :] += _colsum(dx1)
        dxh1 = dx1 * vec(V_LN1G)
        dr1 = rstd_ref[...] * (dxh1 - _rowmean(dxh1) - xhat * _rowmean(dxh1 * xhat))
        dr1_ref[...] = dr1
        sums_ref[S_DGA:S_DGA + 1, :] += _colsum(dr1 * mix_ref[...])
        dmix = ((1.0 + vec(V_GA)) * dr1).astype(BF16)
        dmix_ref[...] = dmix
        dcat_ref[...] = _dot_nt(dmix, wout_s[...])

    row = lambda i: (i, 0)
    fixed = lambda i: (0, 0)
    any_spec = pl.BlockSpec(memory_space=pl.ANY)
    return pl.pallas_call(
        body, name="mlp_and_back", grid=(t_len // tm,),
        out_shape=[jax.ShapeDtypeStruct((t_len, D_FF), BF16), jax.ShapeDtypeStruct((t_len, D_FF), BF16),
                   jax.ShapeDtypeStruct((t_len, D_MODEL), BF16), jax.ShapeDtypeStruct((t_len, D_MODEL), BF16),
                   jax.ShapeDtypeStruct((t_len, D_MODEL), BF16), jax.ShapeDtypeStruct((t_len, D_MODEL), F32),
                   jax.ShapeDtypeStruct((t_len, D_MODEL), F32), jax.ShapeDtypeStruct((16, D_MODEL), F32)],
        in_specs=[pl.BlockSpec((tm, D_MODEL), row), pl.BlockSpec((tm, 1), row), pl.BlockSpec((tm, D_MODEL), row),
                  pl.BlockSpec((tm, D_MODEL), row), pl.BlockSpec((8, D_MODEL), fixed), any_spec, any_spec, any_spec],
        out_specs=[pl.BlockSpec((tm, D_FF), row), pl.BlockSpec((tm, D_FF), row), pl.BlockSpec((tm, D_MODEL), row),
                   pl.BlockSpec((tm, D_MODEL), row), pl.BlockSpec((tm, D_MODEL), row), pl.BlockSpec((tm, D_MODEL), row),
                   pl.BlockSpec((tm, D_MODEL), row), pl.BlockSpec((16, D_MODEL), fixed)],
        scratch_shapes=[pltpu.VMEM(w1.shape, BF16), pltpu.VMEM(w2.shape, BF16), pltpu.VMEM(w_out.shape, BF16),
                        pltpu.VMEM((n_ff, tm, ff), F32), pltpu.SemaphoreType.DMA((3,))],
        compiler_params=_params(56, ("arbitrary",)),
    )(xhat1, rstd1, mix, target, vecs, w1, w2, w_out)


def _in_project_backward(dq, dk, dv, cq, ckv, pos, invf, q_norm_w, kv_norm_w, w_q, w_kv,
                         d_hq, d_hf, d_hi, d_hg, w_in, dr1, x, sc_a):
    t_len = x.shape[0]
    tm = min(256, t_len)
    per_q = dq.shape[3] // tm
    hgw = N_HEADS * HEAD_DIM

    def body(dq_ref, dk_ref, dv_ref, cq_ref, ckv_ref, pos_ref, invf_ref, qn_ref, kvn_ref, wq_ref, wkv_ref,
             dhq_ref, dhf_ref, dhi_ref, dhg_ref, win_ref, dr1_ref, x_ref, sc_ref,
             dz_ref, dqf_ref, dkvu_ref, cqn_ref, ckvn_ref, gx_ref, sums_ref):
        @pl.when(pl.program_id(0) == 0)
        def _():
            sums_ref[...] = jnp.zeros_like(sums_ref)

        cos_t, sin_t = _rope_tables(pos_ref[...], invf_ref[...])
        cq = cq_ref[...]
        ckv = ckv_ref[...]
        rq = lax.rsqrt(_rowmean(cq * cq) + RMS_EPS)
        rkv = lax.rsqrt(_rowmean(ckv * ckv) + RMS_EPS)
        cqn_ref[...] = (cq * rq * qn_ref[...]).astype(BF16)
        ckvn_ref[...] = (ckv * rkv * kvn_ref[...]).astype(BF16)
        d_cqn = jnp.zeros((tm, Q_RANK), F32)
        d_ckvn = jnp.zeros((tm, KV_RANK), F32)
        d_kpe = jnp.zeros((tm, 128), F32)
        for h in range(N_HEADS):
            dqh = jnp.transpose(dq_ref[h])
            dqf_ref[h, :, 0:HEAD_DIM] = dqh[:, :HEAD_DIM].astype(BF16)
            dqf_ref[h, :, HEAD_DIM:QK_DIM] = _unrope(dqh[:, HEAD_DIM:], cos_t, sin_t).astype(BF16)
            d_cqn = d_cqn + _dot_nt(dqf_ref[h], wq_ref[h])
            dkh = dk_ref[h]
            d_kpe = d_kpe + dkh[:, HEAD_DIM:]
            dkvu_ref[h, :, 0:HEAD_DIM] = dkh[:, :HEAD_DIM].astype(BF16)
            dkvu_ref[h, :, HEAD_DIM:2 * HEAD_DIM] = dv_ref[h].astype(BF16)
            d_ckvn = d_ckvn + _dot_nt(dkvu_ref[h], wkv_ref[h])
        dyq = d_cqn * qn_ref[...]
        dykv = d_ckvn * kvn_ref[...]
        sums_ref[2:3, 0:Q_RANK] += _colsum(d_cqn * cq * rq)
        sums_ref[3:4, 0:KV_RANK] += _colsum(d_ckvn * ckv * rkv)
        dz_ref[:, 0:hgw] = dhq_ref[...]
        dz_ref[:, hgw:2 * hgw] = dhf_ref[...]
        dz_ref[:, 2 * hgw:3 * hgw] = dhi_ref[...]
        dz_ref[:, 3 * hgw:4 * hgw] = dhg_ref[...]
        dz_ref[:, HG_COLS:HG_COLS + Q_RANK] = (rq * dyq - cq * (rq * rq * rq) * _rowmean(dyq * cq)).astype(BF16)
        dz_ref[:, HG_COLS + Q_RANK:HG_COLS + Q_RANK + KV_RANK] = (
            rkv * dykv - ckv * (rkv * rkv * rkv) * _rowmean(dykv * ckv)).astype(BF16)
        dz_ref[:, HG_COLS + Q_RANK + KV_RANK:] = _unrope(d_kpe, cos_t, sin_t).astype(BF16)
        du = _dot_nt(dz_ref[...], win_ref[...])
        xv = x_ref[...]
        gx_ref[...] = DN_ALPHA * dr1_ref[...] + (1.0 + sc_ref[...]) * du
        sums_ref[0:1, :] += _colsum(du * xv)
        sums_ref[1:2, :] += _colsum(du)

    row = lambda i: (i, 0)
    fixed2 = lambda i: (0, 0)
    fixed3 = lambda i: (0, 0, 0)
    heads = lambda i: (0, i, 0)
    return pl.pallas_call(
        body, name="in_project_backward", grid=(t_len // tm,),
        out_shape=[jax.ShapeDtypeStruct((t_len, IN_COLS_PAD), BF16), jax.ShapeDtypeStruct((N_HEADS, t_len, QK_DIM), BF16),
                   jax.ShapeDtypeStruct((N_HEADS, t_len, 2 * HEAD_DIM), BF16), jax.ShapeDtypeStruct((t_len, Q_RANK), BF16),
                   jax.ShapeDtypeStruct((t_len, KV_RANK), BF16), jax.ShapeDtypeStruct((t_len, D_MODEL), F32),
                   jax.ShapeDtypeStruct((8, D_MODEL), F32)],
        in_specs=[pl.BlockSpec((N_HEADS, None, QK_DIM, tm), lambda i: (0, i // per_q, 0, i % per_q)),
                  pl.BlockSpec((N_HEADS, tm, QK_DIM), heads),
                  pl.BlockSpec((N_HEADS, tm, HEAD_DIM), heads), pl.BlockSpec((tm, Q_RANK), row),
                  pl.BlockSpec((tm, KV_RANK), row), pl.BlockSpec((tm, 1), row), pl.BlockSpec((1, 128), fixed2),
                  pl.BlockSpec((1, Q_RANK), fixed2), pl.BlockSpec((1, KV_RANK), fixed2),
                  pl.BlockSpec((N_HEADS, Q_RANK, QK_DIM), fixed3), pl.BlockSpec((N_HEADS, KV_RANK, 2 * HEAD_DIM), fixed3),
                  pl.BlockSpec((tm, hgw), row), pl.BlockSpec((tm, hgw), row), pl.BlockSpec((tm, hgw), row),
                  pl.BlockSpec((tm, hgw), row), pl.BlockSpec((D_MODEL, IN_COLS_PAD), fixed2),
                  pl.BlockSpec((tm, D_MODEL), row), pl.BlockSpec((tm, D_MODEL), row), pl.BlockSpec((1, D_MODEL), fixed2)],
        out_specs=[pl.BlockSpec((tm, IN_COLS_PAD), row), pl.BlockSpec((N_HEADS, tm, QK_DIM), heads),
                   pl.BlockSpec((N_HEADS, tm, 2 * HEAD_DIM), heads), pl.BlockSpec((tm, Q_RANK), row),
                   pl.BlockSpec((tm, KV_RANK), row), pl.BlockSpec((tm, D_MODEL), row), pl.BlockSpec((8, D_MODEL), fixed2)],
        compiler_params=_params(48, ("arbitrary",)),
    )(dq, dk, dv, cq, ckv, pos, invf, q_norm_w, kv_norm_w, w_q, w_kv, d_hq, d_hf, d_hi, d_hg, w_in, dr1, x, sc_a)


def _weight_grad(a, b, name, n_blocks, bn, a_blocked=False, b_blocked=True):
    t_len = a.shape[0]
    m = a.shape[1] // n_blocks if a_blocked else a.shape[1]
    bt = min(512, t_len)

    def body(a_ref, b_ref, o_ref):
        @pl.when(pl.program_id(1) == 0)
        def _():
            o_ref[...] = jnp.zeros_like(o_ref)

        o_ref[...] += _dot_tn(a_ref[...].astype(BF16), b_ref[...].astype(BF16))

    a_spec = pl.BlockSpec((bt, m), (lambda n, t: (t, n)) if a_blocked else (lambda n, t: (t, 0)))
    if b.ndim == 3:
        b_spec = pl.BlockSpec((None, bt, bn), lambda n, t: (n, t, 0))
    else:
        b_spec = pl.BlockSpec((bt, bn), (lambda n, t: (t, n)) if b_blocked else (lambda n, t: (t, 0)))
    return pl.pallas_call(
        body, name=name, grid=(n_blocks, t_len // bt),
        out_shape=jax.ShapeDtypeStruct((n_blocks, m, bn), F32),
        in_specs=[a_spec, b_spec],
        out_specs=pl.BlockSpec((None, m, bn), lambda n, t: (n, 0, 0)),
        compiler_params=_params(40, ("arbitrary", "arbitrary")),
    )(a, b)


def _reduce_small(gathered, lb_raw):
    def body(g_ref, lb_ref, tot_ref, dlb_ref):
        tot = g_ref[0]
        for d in range(1, N_DEV):
            tot = tot + g_ref[d]
        tot_ref[...] = tot
        a = lb_ref[...]
        m = jnp.max(a, axis=0, keepdims=True)
        e = jnp.exp(a - m)
        lb = e[0:1] / jnp.sum(e, axis=0, keepdims=True)
        d0 = tot[10:11, 0:512] * lb * (1.0 - lb)
        dlb_ref[0:1, :] = d0
        dlb_ref[1:2, :] = -d0

    return pl.pallas_call(
        body, name="reduce_small",
        out_shape=[jax.ShapeDtypeStruct((SMALL_ROWS, D_MODEL), F32), jax.ShapeDtypeStruct((2, 512), F32)],
    )(gathered, lb_raw)


def _adamw(w, g, m, v, name):
    rows, cols = w.shape
    tr = _row_tile(rows) if rows >= 8 else rows

    def body(w_ref, g_ref, m_ref, v_ref, d_ref, nm_ref, nv_ref):
        gv = g_ref[...]
        nm = ADAM_B1 * m_ref[...] + (1.0 - ADAM_B1) * gv
        nv = ADAM_B2 * v_ref[...] + (1.0 - ADAM_B2) * jnp.square(gv)
        m_hat = nm / (1.0 - ADAM_B1 ** ADAM_STEP)
        v_hat = nv / (1.0 - ADAM_B2 ** ADAM_STEP)
        d_ref[...] = -ADAM_LR * (m_hat / (jnp.sqrt(v_hat) + ADAM_EPS) + ADAM_WD * w_ref[...])
        nm_ref[...] = nm
        nv_ref[...] = nv

    spec = pl.BlockSpec((tr, cols), lambda i: (i, 0))
    return pl.pallas_call(
        body, name=name, grid=(rows // tr,),
        out_shape=[jax.ShapeDtypeStruct(w.shape, F32)] * 3,
        in_specs=[spec] * 4, out_specs=[spec] * 3,
        compiler_params=_params(40, ("arbitrary",)),
    )(w, g, m, v)


def kernel(x, c, positions, w_ada, b_ada, w_in, hg_lower_bounds, hg_norm_w, mla_q_norm_w, w_q_up, mla_kv_norm_w, w_kv_up, w_out, ln1_g, ln1_b, w_mlp_in, w_mlp_out, ln2_g, ln2_b, loss_target, m_w_ada, m_b_ada, m_w_in, m_hg_lower_bounds, m_hg_norm_w, m_mla_q_norm_w, m_w_q_up, m_mla_kv_norm_w, m_w_kv_up, m_w_out, m_ln1_g, m_ln1_b, m_w_mlp_in, m_w_mlp_out, m_ln2_g, m_ln2_b, v_w_ada, v_b_ada, v_w_in, v_hg_lower_bounds, v_hg_norm_w, v_mla_q_norm_w, v_w_q_up, v_mla_kv_norm_w, v_w_kv_up, v_w_out, v_ln1_g, v_ln1_b, v_w_mlp_in, v_w_mlp_out, v_ln2_g, v_ln2_b):
    ix, iy, ic = _mesh_pos()
    chip = 2 * ix + iy
    me = 4 * ix + 2 * iy + ic
    core_arr = jnp.reshape(ic, (1,)).astype(jnp.int32)
    chip_arr = jnp.reshape(chip, (1,)).astype(jnp.int32)

    xs = x[0]
    target = loss_target[0]
    t_len = xs.shape[0]
    pos = positions.astype(F32).reshape(t_len, 1)
    inv = 1.0 / (ROPE_THETA ** (jnp.arange(0, ROPE_DIM, 2, dtype=F32) / ROPE_DIM))
    invf = jnp.concatenate([inv, inv, jnp.zeros((128 - ROPE_DIM,), F32)]).reshape(1, 128)

    ada_cols = w_ada.shape[2]
    c_all = _allgather8(jnp.broadcast_to(c, (8, D_MODEL)), "gather_c")[:, 0, :]
    b_shard = lax.dynamic_slice(b_ada, (0, chip * ada_cols), (1, ada_cols))
    mod_cols, cond16 = _ada_project(c_all, w_ada[0], b_shard)
    mod_all = _allgather8(mod_cols, "gather_mod")
    mod_mine = lax.dynamic_slice(mod_all, (0, me, 0), (N_DEV, 1, ada_cols))[::2, 0, :]
    mod_mine = mod_mine.reshape(6, D_MODEL)
    sh_a, sc_a, g_a, sh_m, sc_m, g_m = (mod_mine[i:i + 1] for i in range(6))

    shards = [w_in[0].astype(BF16), w_q_up[0].astype(BF16), w_kv_up[0].astype(BF16), w_out[0].astype(BF16),
              w_mlp_in[0].astype(BF16), w_mlp_out[0].astype(BF16)]
    g_in, g_q, g_kv, g_out, g_w1, g_w2 = _gather_weights(shards)
    w_in_full = jnp.transpose(g_in, (1, 0, 2)).reshape(D_MODEL, IN_COLS)
    w_in_full = jnp.pad(w_in_full, ((0, 0), (0, IN_COLS_PAD - IN_COLS)))
    w_q_full = jnp.pad(g_q, ((0, 0), (0, 0), (0, QK_DIM - g_q.shape[2])))
    w_out_full = g_out.reshape(D_MODEL, D_MODEL)

    u_a, zhg, cq, ckv, q, k, k_t, v, v_t = _in_project(xs, pos, sc_a, sh_a, w_in_full, mla_q_norm_w, mla_kv_norm_w,
                                            w_q_full, g_kv, invf)
    o_pre, o_hg, states = _hgrn_forward(zhg, hg_lower_bounds, hg_norm_w)
    o_mla, lse = _attention_forward(q, k, v_t)
    cat, mix, xhat1, rstd1 = _out_project(o_hg, o_mla, xs, g_a, w_out_full)
    vecs = jnp.concatenate([ln1_g, ln1_b, sc_m, sh_m, g_m, g_a, ln2_g, ln2_b], axis=0)
    act, dhp, um, dh, dmix, d_cat, dr1, mlp_sums = _mlp_and_back(xhat1, rstd1, mix, target, vecs, g_w1, g_w2, w_out_full)

    dq, dk, dv = _attention_backward(q, k, k_t, v, o_mla, d_cat, lse)
    d_hq, d_hf, d_hi, d_hg, hg_sums = _hgrn_backward(zhg, hg_lower_bounds, hg_norm_w, o_pre, d_cat, states)
    dz, dqf, dkvu, cqn, ckvn, grad_x, in_sums = _in_project_backward(
        dq, dk, dv, cq, ckv, pos, invf, mla_q_norm_w, mla_kv_norm_w, w_q_full, g_kv,
        d_hq, d_hf, d_hi, d_hg, w_in_full, dr1, xs, sc_a)

    gw_in = _weight_grad(u_a, dz, "grad_w_in", 3, IN_COLS_PAD // 3)
    gw_in = jnp.transpose(gw_in, (1, 0, 2)).reshape(D_MODEL, IN_COLS_PAD)[:, :IN_COLS]
    gw_in = jnp.transpose(gw_in.reshape(D_MODEL, N_CHIPS, IN_COLS // N_CHIPS), (1, 0, 2))
    gw_q = _weight_grad(cqn, dqf, "grad_w_q_up", N_HEADS, QK_DIM)[:, :, :HEAD_DIM + ROPE_DIM]
    gw_kv = _weight_grad(ckvn, dkvu, "grad_w_kv_up", N_HEADS, 2 * HEAD_DIM)
    gw_out = _weight_grad(cat, dmix, "grad_w_out", 1, D_MODEL).reshape(N_CHIPS, D_MODEL // N_CHIPS, D_MODEL)
    gw_1 = _weight_grad(um, dhp, "grad_w_mlp_in", N_CHIPS, D_FF // N_CHIPS)
    gw_2 = _weight_grad(act, dh, "grad_w_mlp_out", N_CHIPS, D_MODEL, a_blocked=True, b_blocked=False)

    grads = [gw_in, gw_q, gw_kv, gw_out, gw_1, gw_2]
    landed = _pair_exchange(grads)
    chip_sums = [_add_pair(core_arr, g, l) for g, l in zip(grads, landed)]
    landed = _chip_exchange(chip_sums)
    halves = [_add_chips(chip_arr, p, l) for p, l in zip(chip_sums, landed)]
    g_w_in, g_w_q, g_w_kv, g_w_out, g_w_1, g_w_2 = _pair_assemble(halves)

    zeros = lambda n: jnp.zeros((1, n), F32)
    small = jnp.concatenate([
        in_sums[1:2], in_sums[0:1], mlp_sums[S_DGA:S_DGA + 1],
        mlp_sums[S_DSHM:S_DSHM + 1], mlp_sums[S_DSCM:S_DSCM + 1], mlp_sums[S_DGM:S_DGM + 1],
        mlp_sums[S_DLN1G:S_DLN1G + 1], mlp_sums[S_DLN1B:S_DLN1B + 1],
        mlp_sums[S_DLN2G:S_DLN2G + 1], mlp_sums[S_DLN2B:S_DLN2B + 1],
        jnp.concatenate([hg_sums[0:1], hg_sums[1:2]], axis=1),
        jnp.concatenate([in_sums[2:3, :Q_RANK], in_sums[3:4, :KV_RANK], zeros(D_MODEL - Q_RANK - KV_RANK)], axis=1),
        mlp_sums[S_LOSS:S_LOSS + 1],
        jnp.zeros((SMALL_ROWS - 13, D_MODEL), F32)], axis=0)
    small_all = _allgather8(small, "gather_small")
    tot, g_lb = _reduce_small(small_all, hg_lower_bounds)
    loss = tot[12, 0]
    g_b_ada = tot[0:6].reshape(1, 6 * D_MODEL)
    g_ln1_g, g_ln1_b, g_ln2_g, g_ln2_b = tot[6:7], tot[7:8], tot[8:9], tot[9:10]
    g_hg_norm = tot[10:11, 512:1024]
    g_q_norm = tot[11:12, 0:Q_RANK]
    g_kv_norm = tot[11:12, Q_RANK:Q_RANK + KV_RANK]

    d_mod_all = small_all[:, 0:6, :].reshape(N_DEV, 6 * D_MODEL)
    d_mod_cols = lax.dynamic_slice(d_mod_all, (0, chip * ada_cols), (N_DEV, ada_cols))
    d_mod_cols = jnp.concatenate([d_mod_cols, jnp.zeros_like(d_mod_cols)], axis=0)
    g_w_ada = _weight_grad(cond16, d_mod_cols, "grad_w_ada", 1, ada_cols)[0]

    names = ["w_ada", "b_ada", "w_in", "hg_lower_bounds", "hg_norm_w", "mla_q_norm_w", "w_q_up", "mla_kv_norm_w",
             "w_kv_up", "w_out", "ln1_g", "ln1_b", "w_mlp_in", "w_mlp_out", "ln2_g", "ln2_b"]
    weights = [w_ada, b_ada, w_in, hg_lower_bounds, hg_norm_w, mla_q_norm_w, w_q_up, mla_kv_norm_w,
               w_kv_up, w_out, ln1_g, ln1_b, w_mlp_in, w_mlp_out, ln2_g, ln2_b]
    moms = [m_w_ada, m_b_ada, m_w_in, m_hg_lower_bounds, m_hg_norm_w, m_mla_q_norm_w, m_w_q_up, m_mla_kv_norm_w,
            m_w_kv_up, m_w_out, m_ln1_g, m_ln1_b, m_w_mlp_in, m_w_mlp_out, m_ln2_g, m_ln2_b]
    vels = [v_w_ada, v_b_ada, v_w_in, v_hg_lower_bounds, v_hg_norm_w, v_mla_q_norm_w, v_w_q_up, v_mla_kv_norm_w,
            v_w_kv_up, v_w_out, v_ln1_g, v_ln1_b, v_w_mlp_in, v_w_mlp_out, v_ln2_g, v_ln2_b]
    grads2d = [g_w_ada, g_b_ada, g_w_in, g_lb, g_hg_norm, g_q_norm, g_w_q, g_kv_norm,
               g_w_kv, g_w_out, g_ln1_g, g_ln1_b, g_w_1, g_w_2, g_ln2_g, g_ln2_b]
    out_g, out_d, out_m, out_v = [], [], [], []
    for name, w, g, m, vv in zip(names, weights, grads2d, moms, vels):
        shape2 = g.shape
        d, nm, nv = _adamw(w.reshape(shape2), g, m.reshape(shape2), vv.reshape(shape2), "adamw_" + name)
        out_g.append(g.reshape(w.shape))
        out_d.append(d.reshape(w.shape))
        out_m.append(nm.reshape(w.shape))
        out_v.append(nv.reshape(w.shape))
    return (loss, grad_x[None], *out_g, *out_d, *out_m, *out_v)
```

```python
import functools

import jax
import jax.numpy as jnp
from jax import lax
from jax.experimental import pallas as pl
from jax.experimental.pallas import tpu as pltpu

F32 = jnp.float32
BF16 = jnp.bfloat16
MESH_IDS = pl.DeviceIdType.MESH

D_MODEL = 1024
N_HEADS = 4
HEAD_DIM = 128
ROPE_DIM = 64
HG_CHUNK = 64
HG_COLS = 2048
Q_RANK = 256
KV_RANK = 256
IN_COLS = 2624
IN_COLS_PAD = 2688
QK_DIM = 256
D_FF = 4096
N_CHIPS = 4
N_DEV = 8
ROPE_THETA = 10000.0
RMS_EPS = 1e-6
LN_EPS = 1e-5
DN_ALPHA = 2.0 ** 0.25
ATT_SCALE = (HEAD_DIM + ROPE_DIM) ** -0.5
NEG_BIG = -1e30
ADAM_LR = 0.001
ADAM_B1 = 0.9
ADAM_B2 = 0.999
ADAM_EPS = 1e-08
ADAM_WD = 0.01
ADAM_STEP = 10
SMALL_ROWS = 16
MIB = 1024 * 1024


def _dot(a, b):
    return jnp.dot(a, b, preferred_element_type=F32)


def _dot_nt(a, b):
    return lax.dot_general(a, b, (((1,), (1,)), ((), ())), preferred_element_type=F32)


def _dot_tn(a, b):
    return lax.dot_general(a, b, (((0,), (0,)), ((), ())), preferred_element_type=F32)


def _dot_f32(a, b):
    return jnp.dot(a, b, preferred_element_type=F32, precision=lax.Precision.HIGHEST)


def _params(vmem_mib, semantics=None):
    return pltpu.CompilerParams(vmem_limit_bytes=vmem_mib * MIB, dimension_semantics=semantics)


def _sigmoid(v):
    return 1.0 / (1.0 + jnp.exp(-v))


def _colsum(v):
    return jnp.sum(v, axis=0, keepdims=True)


def _rowmean(v):
    return jnp.mean(v, axis=-1, keepdims=True)


def _rope_tables(pos, invf):
    ang = pos * invf
    lane = lax.broadcasted_iota(jnp.int32, ang.shape, 1)
    cos_t = jnp.where(lane < ROPE_DIM, jnp.cos(ang), 0.0)
    sin = jnp.sin(ang)
    sin_t = jnp.where(lane < ROPE_DIM // 2, -sin, jnp.where(lane < ROPE_DIM, sin, 0.0))
    return cos_t, sin_t


def _swap_halves(t):
    lane = lax.broadcasted_iota(jnp.int32, t.shape, 1)
    return jnp.where(lane < ROPE_DIM // 2, pltpu.roll(t, 128 - ROPE_DIM // 2, 1), pltpu.roll(t, ROPE_DIM // 2, 1))


def _rope(t, cos_t, sin_t):
    return t * cos_t + _swap_halves(t) * sin_t


def _unrope(g, cos_t, sin_t):
    return g * cos_t - _swap_halves(g) * sin_t


def _mesh_pos():
    return lax.axis_index("x"), lax.axis_index("y"), lax.axis_index("c")


def _other_chips(x, y):
    out = []
    for dx, dy in ((1, 0), (0, 1), (1, 1)):
        px = 1 - x if dx else x
        py = 1 - y if dy else y
        out.append(((px, py), 2 * px + py))
    return out


def _allgather8(a, name):
    rows, cols = a.shape

    def body(a_ref, out_ref, send_sems, recv_sems):
        x, y, c = _mesh_pos()
        me = 4 * x + 2 * y + c
        out_ref[me] = a_ref[...]
        peers = []
        for r in range(1, N_DEV):
            px = 1 - x if r & 4 else x
            py = 1 - y if r & 2 else y
            pc = 1 - c if r & 1 else c
            peers.append(((px, py, pc), 4 * px + 2 * py + pc))

        def copy(r, block, to):
            return pltpu.make_async_remote_copy(
                src_ref=a_ref, dst_ref=out_ref.at[block], send_sem=send_sems.at[r], recv_sem=recv_sems.at[r],
                device_id=to, device_id_type=MESH_IDS)

        sends = [copy(r, me, peer) for r, (peer, _) in enumerate(peers)]
        for cp in sends:
            cp.start()
        for r, (peer, idx) in enumerate(peers):
            copy(r, idx, peer).wait_recv()
        for cp in sends:
            cp.wait_send()

    return pl.pallas_call(
        body, name=name,
        out_shape=jax.ShapeDtypeStruct((N_DEV, rows, cols), a.dtype),
        in_specs=[pl.BlockSpec(memory_space=pltpu.VMEM)],
        out_specs=pl.BlockSpec(memory_space=pltpu.VMEM),
        scratch_shapes=[pltpu.SemaphoreType.DMA((N_DEV - 1,)), pltpu.SemaphoreType.DMA((N_DEV - 1,))],
    )(a)


class _Exchange:
    def __init__(self, inputs, out_shapes, aliases, sems, start, finish):
        self.inputs, self.out_shapes, self.aliases, self.sems = list(inputs), list(out_shapes), dict(aliases), list(sems)
        self.start, self.finish = start, finish


def _from_copies(inputs, out_shapes, aliases, sems, copies):
    def start(ins, outs, sem_refs):
        for send, _ in copies(ins, outs, sem_refs):
            send.start()

    def finish(ins, outs, sem_refs):
        for send, recv in copies(ins, outs, sem_refs):
            recv.wait_recv()
            send.wait_send()

    return _Exchange(inputs, out_shapes, aliases, sems, start, finish)


def _run_exchange(exchange, name):
    n_in, n_out = len(exchange.inputs), len(exchange.out_shapes)

    def body(*refs):
        ins, outs, sem_refs = refs[:n_in], refs[n_in:n_in + n_out], refs[n_in + n_out:]
        exchange.start(ins, outs, sem_refs)
        exchange.finish(ins, outs, sem_refs)

    any_spec = pl.BlockSpec(memory_space=pl.ANY)
    return pl.pallas_call(
        body, name=name, out_shape=exchange.out_shapes, in_specs=[any_spec] * n_in, out_specs=[any_spec] * n_out,
        scratch_shapes=exchange.sems, input_output_aliases=exchange.aliases,
    )(*exchange.inputs)


def _pallas(body, *, name, operands, in_specs, out_shape, out_specs, params, scratch_shapes=(), grid=(), prefetch=(),
            exchange=None, first=None, last=None):
    n_pre, n_in, n_out, n_scr = len(prefetch), len(in_specs), len(out_specs), len(scratch_shapes)
    ex_in = exchange.inputs if exchange else []
    ex_out = exchange.out_shapes if exchange else []
    ex_sems = exchange.sems if exchange else []

    def full_body(*refs):
        pre, rest = refs[:n_pre], refs[n_pre:]
        ins, rest = rest[:n_in], rest[n_in:]
        xin, rest = rest[:len(ex_in)], rest[len(ex_in):]
        outs, rest = rest[:n_out], rest[n_out:]
        xout, rest = rest[:len(ex_out)], rest[len(ex_out):]
        scr, sem_refs = rest[:n_scr], rest[n_scr:]
        if exchange:
            @pl.when(first(*pre))
            def _():
                exchange.start(xin, xout, sem_refs)

        body(*pre, *ins, *outs, *scr)
        if exchange:
            @pl.when(last(*pre))
            def _():
                exchange.finish(xin, xout, sem_refs)

    any_spec = pl.BlockSpec(memory_space=pl.ANY)
    aliases = {n_pre + n_in + i: n_out + o for i, o in exchange.aliases.items()} if exchange else {}
    results = pl.pallas_call(
        full_body, name=name, out_shape=list(out_shape) + ex_out,
        grid_spec=pltpu.PrefetchScalarGridSpec(
            num_scalar_prefetch=n_pre, grid=grid, in_specs=list(in_specs) + [any_spec] * len(ex_in),
            out_specs=list(out_specs) + [any_spec] * len(ex_out), scratch_shapes=list(scratch_shapes) + ex_sems),
        input_output_aliases=aliases, compiler_params=params,
    )(*prefetch, *operands, *ex_in)
    return results[:n_out], results[n_out:]


def _remote(src, dst, sems, idx, to):
    send_sems, recv_sems = sems
    return pltpu.make_async_remote_copy(src_ref=src, dst_ref=dst, send_sem=send_sems.at[idx], recv_sem=recv_sems.at[idx],
                                        device_id=to, device_id_type=MESH_IDS)


def _sem_pairs(*shape):
    return [pltpu.SemaphoreType.DMA(shape), pltpu.SemaphoreType.DMA(shape)]


def _same_shapes(arrays):
    return [jax.ShapeDtypeStruct(a.shape, a.dtype) for a in arrays]


def _gather_over_ici(slots):
    n = len(slots)

    def copies(ins, outs, sems):
        x, y, c = _mesh_pos()
        k = 2 * x + y
        out = []
        for j, (chip, kj) in enumerate(_other_chips(x, y)):
            for i in range(n):
                to = (*chip, c)
                out.append((_remote(ins[i].at[k, c], outs[i].at[k, c], sems, (j, i), to),
                            _remote(ins[i].at[k, c], outs[i].at[kj, c], sems, (j, i), to)))
        return out

    return _from_copies(slots, _same_shapes(slots), {i: i for i in range(n)}, _sem_pairs(3, n), copies)


def _gather_over_d2d(slots):
    n = len(slots)

    def copies(ins, outs, sems):
        x, y, c = _mesh_pos()
        sibling = (x, y, 1 - c)
        out = []
        for j, (_, kj) in enumerate(_other_chips(x, y)):
            for i in range(n):
                out.append((_remote(ins[i].at[kj, c], outs[i].at[kj, c], sems, (j, i), sibling),
                            _remote(ins[i].at[kj, c], outs[i].at[kj, 1 - c], sems, (j, i), sibling)))
        return out

    return _from_copies(slots, _same_shapes(slots), {i: i for i in range(n)}, _sem_pairs(3, n), copies)


def _pair_exchange(grads):
    n = len(grads)

    def copies(ins, outs, sems):
        x, y, c = _mesh_pos()
        cps = [_remote(ins[i].at[:, 1 - c], outs[i], sems, i, (x, y, 1 - c)) for i in range(n)]
        return [(cp, cp) for cp in cps]

    shapes = [jax.ShapeDtypeStruct((N_CHIPS,) + g.shape[2:], g.dtype) for g in grads]
    return _from_copies(grads, shapes, {}, _sem_pairs(n), copies)


def _chip_exchange(partials):
    n = len(partials)

    def copies(ins, outs, sems):
        x, y, c = _mesh_pos()
        cps = [_remote(ins[i].at[kj], outs[i].at[j], sems, (j, i), (*chip, c))
               for j, (chip, kj) in enumerate(_other_chips(x, y)) for i in range(n)]
        return [(cp, cp) for cp in cps]

    shapes = [jax.ShapeDtypeStruct((3,) + p.shape[1:], p.dtype) for p in partials]
    return _from_copies(partials, shapes, {}, _sem_pairs(3, n), copies)


def _pair_send(halves):
    n = len(halves)

    def copies(ins, outs, sems):
        x, y, c = _mesh_pos()
        cps = [_remote(ins[i], outs[i], sems, i, (x, y, 1 - c)) for i in range(n)]
        return [(cp, cp) for cp in cps]

    return _from_copies(halves, _same_shapes(halves), {}, _sem_pairs(n), copies)


def _row_tile(rows):
    for t in (256, 128, 64, 32, 16, 8):
        if rows % t == 0:
            return t
    return rows


def _add_pair(core, grad, landed):
    _, h, cols = landed.shape
    tr = _row_tile(h)

    def body(core_ref, g_ref, l_ref, o_ref, ob_ref):
        s = g_ref[...] + l_ref[...]
        o_ref[...] = s
        ob_ref[...] = s.astype(BF16)

    out_spec = pl.BlockSpec((None, tr, cols), lambda k, t, core_ref: (k, t, 0))
    return pl.pallas_call(
        body, name="grad_add_pair",
        out_shape=[jax.ShapeDtypeStruct(landed.shape, F32), jax.ShapeDtypeStruct(landed.shape, BF16)],
        grid_spec=pltpu.PrefetchScalarGridSpec(
            num_scalar_prefetch=1, grid=(N_CHIPS, h // tr),
            in_specs=[pl.BlockSpec((None, None, tr, cols), lambda k, t, core_ref: (k, core_ref[0], t, 0)),
                      pl.BlockSpec((None, tr, cols), lambda k, t, core_ref: (k, t, 0))],
            out_specs=[out_spec, out_spec]),
        compiler_params=_params(32, ("arbitrary", "arbitrary")),
    )(core, grad, landed)


def _add_chips(chip, partial, landed):
    _, h, cols = partial.shape
    tr = _row_tile(h)

    def body(chip_ref, p_ref, l_ref, o_ref):
        o_ref[...] = ((p_ref[...] + l_ref[0].astype(F32)) + l_ref[1].astype(F32)) + l_ref[2].astype(F32)

    return pl.pallas_call(
        body, name="grad_add_chips",
        out_shape=jax.ShapeDtypeStruct((h, cols), F32),
        grid_spec=pltpu.PrefetchScalarGridSpec(
            num_scalar_prefetch=1, grid=(h // tr,),
            in_specs=[pl.BlockSpec((None, tr, cols), lambda t, chip_ref: (chip_ref[0], t, 0)),
                      pl.BlockSpec((3, tr, cols), lambda t, chip_ref: (0, t, 0))],
            out_specs=pl.BlockSpec((tr, cols), lambda t, chip_ref: (t, 0))),
        compiler_params=_params(32, ("arbitrary",)),
    )(chip, partial, landed)


def _ada_project(c_all, w_ada, b_shard):
    n = w_ada.shape[1]
    tn = 512

    def body(c_ref, w_ref, b_ref, mod_ref, cond_ref):
        cv = c_ref[...]
        cond = cv * _sigmoid(cv)
        mod_ref[...] = _dot(cond.astype(BF16), w_ref[...].astype(BF16)) + b_ref[...]
        cond_ref[0:N_DEV, :] = cond
        cond_ref[N_DEV:2 * N_DEV, :] = jnp.zeros_like(cond)

    return pl.pallas_call(
        body, name="ada_project", grid=(n // tn,),
        out_shape=[jax.ShapeDtypeStruct((N_DEV, n), F32), jax.ShapeDtypeStruct((2 * N_DEV, D_MODEL), F32)],
        in_specs=[pl.BlockSpec((N_DEV, D_MODEL), lambda j: (0, 0)), pl.BlockSpec((D_MODEL, tn), lambda j: (0, j)),
                  pl.BlockSpec((1, tn), lambda j: (0, j))],
        out_specs=[pl.BlockSpec((N_DEV, tn), lambda j: (0, j)), pl.BlockSpec((2 * N_DEV, D_MODEL), lambda j: (0, 0))],
        compiler_params=_params(32, ("arbitrary",)),
    )(c_all, w_ada, b_shard)


def _in_project(x, pos, sc_a, sh_a, w_in, q_norm_w, kv_norm_w, w_q, w_kv, invf):
    t_len = x.shape[0]
    tm = min(256, t_len)

    def body(x_ref, pos_ref, sc_ref, sh_ref, win_ref, qn_ref, kvn_ref, wq_ref, wkv_ref, invf_ref,
             u_ref, zhg_ref, cq_ref, ckv_ref, q_ref, k_ref, kt_ref, v_ref, vt_ref):
        u = (x_ref[...] * (1.0 + sc_ref[...]) + sh_ref[...]).astype(BF16)
        u_ref[...] = u
        z = _dot(u, win_ref[...])
        zhg_ref[...] = z[:, :HG_COLS]
        cq = z[:, HG_COLS:HG_COLS + Q_RANK]
        ckv = z[:, HG_COLS + Q_RANK:HG_COLS + Q_RANK + KV_RANK]
        cq_ref[...] = cq
        ckv_ref[...] = ckv
        cos_t, sin_t = _rope_tables(pos_ref[...], invf_ref[...])
        k_pe = _rope(z[:, HG_COLS + Q_RANK + KV_RANK:], cos_t, sin_t)
        k_pe_t = jnp.transpose(k_pe).astype(BF16)
        cqn = (cq * lax.rsqrt(_rowmean(cq * cq) + RMS_EPS) * qn_ref[...]).astype(BF16)
        ckvn = (ckv * lax.rsqrt(_rowmean(ckv * ckv) + RMS_EPS) * kvn_ref[...]).astype(BF16)
        for h in range(N_HEADS):
            qh = _dot(cqn, wq_ref[h])
            q_ref[h, :, 0:HEAD_DIM] = qh[:, :HEAD_DIM].astype(BF16)
            q_ref[h, :, HEAD_DIM:QK_DIM] = _rope(qh[:, HEAD_DIM:], cos_t, sin_t).astype(BF16)
            kvh = _dot(ckvn, wkv_ref[h])
            k_ref[h, :, 0:HEAD_DIM] = kvh[:, :HEAD_DIM].astype(BF16)
            k_ref[h, :, HEAD_DIM:QK_DIM] = k_pe.astype(BF16)
            kt_ref[h, 0:HEAD_DIM, :] = jnp.transpose(kvh[:, :HEAD_DIM]).astype(BF16)
            kt_ref[h, HEAD_DIM:QK_DIM, :] = k_pe_t
            v_ref[h] = kvh[:, HEAD_DIM:].astype(BF16)
            vt_ref[h] = jnp.transpose(kvh[:, HEAD_DIM:]).astype(BF16)

    row = lambda i: (i, 0)
    fixed2 = lambda i: (0, 0)
    fixed3 = lambda i: (0, 0, 0)
    heads = lambda i: (0, i, 0)
    return pl.pallas_call(
        body, name="in_project", grid=(t_len // tm,),
        out_shape=[jax.ShapeDtypeStruct((t_len, D_MODEL), BF16), jax.ShapeDtypeStruct((t_len, HG_COLS), F32),
                   jax.ShapeDtypeStruct((t_len, Q_RANK), F32), jax.ShapeDtypeStruct((t_len, KV_RANK), F32),
                   jax.ShapeDtypeStruct((N_HEADS, t_len, QK_DIM), BF16),
                   jax.ShapeDtypeStruct((N_HEADS, t_len, QK_DIM), BF16),
                   jax.ShapeDtypeStruct((N_HEADS, QK_DIM, t_len), BF16),
                   jax.ShapeDtypeStruct((N_HEADS, t_len, HEAD_DIM), BF16),
                   jax.ShapeDtypeStruct((N_HEADS, HEAD_DIM, t_len), BF16)],
        in_specs=[pl.BlockSpec((tm, D_MODEL), row), pl.BlockSpec((tm, 1), row),
                  pl.BlockSpec((1, D_MODEL), fixed2), pl.BlockSpec((1, D_MODEL), fixed2),
                  pl.BlockSpec((D_MODEL, IN_COLS_PAD), fixed2),
                  pl.BlockSpec((1, Q_RANK), fixed2), pl.BlockSpec((1, KV_RANK), fixed2),
                  pl.BlockSpec((N_HEADS, Q_RANK, QK_DIM), fixed3), pl.BlockSpec((N_HEADS, KV_RANK, 2 * HEAD_DIM), fixed3),
                  pl.BlockSpec((1, 128), fixed2)],
        out_specs=[pl.BlockSpec((tm, D_MODEL), row), pl.BlockSpec((tm, HG_COLS), row),
                   pl.BlockSpec((tm, Q_RANK), row), pl.BlockSpec((tm, KV_RANK), row),
                   pl.BlockSpec((N_HEADS, tm, QK_DIM), heads), pl.BlockSpec((N_HEADS, tm, QK_DIM), heads),
                   pl.BlockSpec((N_HEADS, QK_DIM, tm), lambda i: (0, 0, i)),
                   pl.BlockSpec((N_HEADS, tm, HEAD_DIM), heads),
                   pl.BlockSpec((N_HEADS, HEAD_DIM, tm), lambda i: (0, 0, i))],
        compiler_params=_params(48, ("arbitrary",)),
    )(x, pos, sc_a, sh_a, w_in, q_norm_w, kv_norm_w, w_q, w_kv, invf)


def _lower_bound(lb_raw):
    m = jnp.max(lb_raw, axis=0, keepdims=True)
    e = jnp.exp(lb_raw - m)
    return e[0:1] / jnp.sum(e, axis=0, keepdims=True)


def _tri(inclusive_lower):
    r = lax.broadcasted_iota(jnp.int32, (HG_CHUNK, HG_CHUNK), 0)
    c = lax.broadcasted_iota(jnp.int32, (HG_CHUNK, HG_CHUNK), 1)
    return (c <= r) if inclusive_lower else (c >= r)


def _hg_chunk(q, f_logit, lb):
    sg = _sigmoid(f_logit)
    forget = lb + (1.0 - lb) * sg
    kk = 1.0 - forget
    b = _dot_f32(_tri(True).astype(F32), jnp.log(forget))
    b_ref = b[HG_CHUNK // 2 - 1:HG_CHUNK // 2]
    b_last = b[HG_CHUNK - 1:HG_CHUNK]
    e_i = jnp.exp(b - b_ref)
    e_ri = jnp.exp(b_ref - b)
    e_b = jnp.exp(b)
    e_l = jnp.exp(b_last - b)
    return dict(sg=sg, forget=forget, e_i=e_i, e_ri=e_ri, e_b=e_b, e_l=e_l, dec=jnp.exp(b_last),
                qi=q * e_i, ki=kk * e_ri, qe=q * e_b, kl=kk * e_l)


def _hgrn_forward(zhg, lb_raw, norm_w, exchange=None):
    t_len = zhg.shape[0]
    tb = min(512, t_len)
    n_chunks = tb // HG_CHUNK

    def body(q_ref, f_ref, v_ref, g_ref, lb_ref, w_ref, opre_ref, o_ref, st_ref, state):
        @pl.when(pl.program_id(1) == 0)
        def _():
            state[...] = jnp.zeros_like(state)

        lb = _lower_bound(lb_ref[...])
        causal = _tri(True)
        for n in range(n_chunks):
            rows = pl.ds(n * HG_CHUNK, HG_CHUNK)
            v = v_ref[rows, :].astype(BF16)
            ch = _hg_chunk(q_ref[rows, :], f_ref[rows, :], lb)
            a = jnp.where(causal, _dot_nt(ch["qi"].astype(BF16), ch["ki"].astype(BF16)), 0.0)
            st = state[...]
            st_ref[0, n] = st
            o = _dot(a.astype(BF16), v) + _dot_nt(ch["qe"].astype(BF16), st.astype(BF16))
            state[...] = st * ch["dec"] + _dot_tn(v, ch["kl"].astype(BF16))
            opre_ref[rows, :] = o
            on = o * lax.rsqrt(_rowmean(o * o) + RMS_EPS) * w_ref[...]
            g = g_ref[rows, :]
            o_ref[rows, :] = on * (g * _sigmoid(g))

    col = lambda off: (lambda h, t: (t, off + h))
    nb = t_len // tb
    return _pallas(
        body, name="hgrn_forward", grid=(N_HEADS, nb), operands=(zhg, zhg, zhg, zhg, lb_raw, norm_w),
        out_shape=[jax.ShapeDtypeStruct((t_len, N_HEADS * HEAD_DIM), F32),
                   jax.ShapeDtypeStruct((t_len, N_HEADS * HEAD_DIM), F32),
                   jax.ShapeDtypeStruct((N_HEADS, t_len // HG_CHUNK, HEAD_DIM, HEAD_DIM), F32)],
        in_specs=[pl.BlockSpec((tb, HEAD_DIM), col(0)), pl.BlockSpec((tb, HEAD_DIM), col(N_HEADS)),
                  pl.BlockSpec((tb, HEAD_DIM), col(2 * N_HEADS)), pl.BlockSpec((tb, HEAD_DIM), col(3 * N_HEADS)),
                  pl.BlockSpec((2, HEAD_DIM), lambda h, t: (0, h)), pl.BlockSpec((1, HEAD_DIM), lambda h, t: (0, h))],
        out_specs=[pl.BlockSpec((tb, HEAD_DIM), col(0)), pl.BlockSpec((tb, HEAD_DIM), col(0)),
                   pl.BlockSpec((1, n_chunks, HEAD_DIM, HEAD_DIM), lambda h, t: (h, t, 0, 0))],
        scratch_shapes=[pltpu.VMEM((HEAD_DIM, HEAD_DIM), F32)],
        params=_params(32, ("arbitrary", "arbitrary")), exchange=exchange,
        first=lambda: (pl.program_id(0) == 0) & (pl.program_id(1) == 0),
        last=lambda: (pl.program_id(0) == N_HEADS - 1) & (pl.program_id(1) == nb - 1))


def _hgrn_backward(zhg, lb_raw, norm_w, o_pre, d_cat, states, exchange=None):
    t_len = zhg.shape[0]
    tb = min(512, t_len)
    n_chunks = tb // HG_CHUNK
    nb = t_len // tb

    def body(q_ref, f_ref, v_ref, g_ref, lb_ref, w_ref, opre_ref, do_ref, st_ref,
             dq_ref, df_ref, dv_ref, dg_ref, sums_ref, gstate):
        @pl.when(pl.program_id(1) == 0)
        def _():
            gstate[...] = jnp.zeros_like(gstate)
            sums_ref[...] = jnp.zeros_like(sums_ref)

        lb = _lower_bound(lb_ref[...])
        w = w_ref[...]
        causal = _tri(True)
        upper = _tri(False).astype(F32)
        row_id = lax.broadcasted_iota(jnp.int32, (HG_CHUNK, HEAD_DIM), 0)
        d_lb = jnp.zeros((1, HEAD_DIM), F32)
        d_w = jnp.zeros((1, HEAD_DIM), F32)
        for n in reversed(range(n_chunks)):
            rows = pl.ds(n * HG_CHUNK, HG_CHUNK)
            o = opre_ref[rows, :]
            g = g_ref[rows, :]
            d_out = do_ref[rows, :]
            r = lax.rsqrt(_rowmean(o * o) + RMS_EPS)
            sg_g = _sigmoid(g)
            silu = g * sg_g
            dg_ref[rows, :] = (d_out * (o * r * w) * (sg_g * (1.0 + g * (1.0 - sg_g)))).astype(BF16)
            d_on = d_out * silu
            d_w = d_w + _colsum(d_on * o * r)
            dy = d_on * w
            d_o = (r * dy - o * (r * r * r) * _rowmean(dy * o)).astype(BF16)
            vf = v_ref[rows, :]
            v = vf.astype(BF16)
            ch = _hg_chunk(q_ref[rows, :], f_ref[rows, :], lb)
            qi, ki, qe, kl = (ch[name].astype(BF16) for name in ("qi", "ki", "qe", "kl"))
            st = st_ref[0, n]
            gt = gstate[...]
            a = jnp.where(causal, _dot_nt(qi, ki), 0.0).astype(BF16)
            d_a = jnp.where(causal, _dot_nt(d_o, v), 0.0).astype(BF16)
            gt_b = gt.astype(BF16)
            d_v = _dot_tn(a, d_o) + _dot_nt(kl, gt_b)
            d_qi = _dot(d_a, ki)
            d_ki = _dot_tn(d_a, qi)
            d_qe = _dot(d_o, st.astype(BF16))
            d_kl = _dot(v, gt_b)
            d_dec = _colsum(gt * st)
            gstate[...] = gt * ch["dec"] + _dot_tn(d_o, qe)
            dq_ref[rows, :] = (d_qi * ch["e_i"] + d_qe * ch["e_b"]).astype(BF16)
            d_k = d_ki * ch["e_ri"] + d_kl * ch["e_l"]
            t_qi = d_qi * ch["qi"]
            t_ki = d_ki * ch["ki"]
            t_kl = d_kl * ch["kl"]
            d_b = t_qi - t_ki + d_qe * ch["qe"] - t_kl
            d_b = d_b + jnp.where(row_id == HG_CHUNK // 2 - 1, _colsum(t_ki - t_qi), 0.0)
            d_b = d_b + jnp.where(row_id == HG_CHUNK - 1, _colsum(t_kl) + d_dec * ch["dec"], 0.0)
            d_forget = _dot_f32(upper, d_b) / ch["forget"] - d_k
            sg = ch["sg"]
            df_ref[rows, :] = (d_forget * (1.0 - lb) * sg * (1.0 - sg)).astype(BF16)
            d_lb = d_lb + _colsum(d_forget * (1.0 - sg))
            dv_ref[rows, :] = d_v.astype(BF16)
        sums_ref[0:1, :] += d_lb
        sums_ref[1:2, :] += d_w

    col = lambda off: (lambda h, t: (nb - 1 - t, off + h))
    return _pallas(
        body, name="hgrn_backward", grid=(N_HEADS, nb),
        operands=(zhg, zhg, zhg, zhg, lb_raw, norm_w, o_pre, d_cat, states),
        out_shape=[jax.ShapeDtypeStruct((t_len, N_HEADS * HEAD_DIM), BF16)] * 4
        + [jax.ShapeDtypeStruct((8, N_HEADS * HEAD_DIM), F32)],
        in_specs=[pl.BlockSpec((tb, HEAD_DIM), col(0)), pl.BlockSpec((tb, HEAD_DIM), col(N_HEADS)),
                  pl.BlockSpec((tb, HEAD_DIM), col(2 * N_HEADS)), pl.BlockSpec((tb, HEAD_DIM), col(3 * N_HEADS)),
                  pl.BlockSpec((2, HEAD_DIM), lambda h, t: (0, h)), pl.BlockSpec((1, HEAD_DIM), lambda h, t: (0, h)),
                  pl.BlockSpec((tb, HEAD_DIM), col(0)), pl.BlockSpec((tb, HEAD_DIM), col(0)),
                  pl.BlockSpec((1, n_chunks, HEAD_DIM, HEAD_DIM), lambda h, t: (h, nb - 1 - t, 0, 0))],
        out_specs=[pl.BlockSpec((tb, HEAD_DIM), col(0))] * 4 + [pl.BlockSpec((8, HEAD_DIM), lambda h, t: (0, h))],
        scratch_shapes=[pltpu.VMEM((HEAD_DIM, HEAD_DIM), F32)],
        params=_params(32, ("arbitrary", "arbitrary")), exchange=exchange,
        first=lambda: (pl.program_id(0) == 0) & (pl.program_id(1) == 0),
        last=lambda: (pl.program_id(0) == N_HEADS - 1) & (pl.program_id(1) == nb - 1))


ATT_LOG2 = ATT_SCALE * 1.4426950408889634


def _triangle_steps(nq, q_major):
    if q_major:
        pairs = [(i, j) for i in range(nq) for j in range(i + 1)]
    else:
        pairs = [(i, j) for j in range(nq) for i in range(j, nq)]
    return jnp.array([p[0] for p in pairs], jnp.int32), jnp.array([p[1] for p in pairs], jnp.int32)


def _key_le_query(t):
    return lax.broadcasted_iota(jnp.int32, (t, t), 0) <= lax.broadcasted_iota(jnp.int32, (t, t), 1)


def _attention_forward(q, k, v_t, exchange=None):
    t_len = q.shape[1]
    tq = min(512, t_len)
    nq = t_len // tq
    qi_tab, ki_tab = _triangle_steps(nq, True)

    def body(qi_ref, ki_ref, q_ref, k_ref, vt_ref, o_ref, lse_ref, m_s, l_s, acc_s):
        step = pl.program_id(0)
        qi, ki = qi_ref[step], ki_ref[step]

        @pl.when(ki == 0)
        def _():
            m_s[...] = jnp.full_like(m_s, NEG_BIG)
            l_s[...] = jnp.zeros_like(l_s)
            acc_s[...] = jnp.zeros_like(acc_s)

        def accumulate(masked):
            for h in range(N_HEADS):
                s_t = _dot_nt(k_ref[h], q_ref[h]) * ATT_LOG2
                if masked:
                    s_t = jnp.where(_key_le_query(tq), s_t, NEG_BIG)
                m_old = m_s[h]
                m_new = jnp.maximum(m_old, jnp.max(s_t, axis=0, keepdims=True))
                alpha = jnp.exp2(m_old - m_new)
                p_t = jnp.exp2(s_t - m_new)
                l_s[h] = alpha * l_s[h] + jnp.sum(p_t, axis=0, keepdims=True)
                acc_s[h] = alpha * acc_s[h] + _dot(vt_ref[h], p_t.astype(BF16))
                m_s[h] = m_new

        @pl.when(ki < qi)
        def _():
            accumulate(False)

        @pl.when(ki == qi)
        def _():
            accumulate(True)
            for h in range(N_HEADS):
                o_ref[:, h * HEAD_DIM:(h + 1) * HEAD_DIM] = jnp.transpose(acc_s[h] / l_s[h])
                lse_ref[h] = m_s[h] + jnp.log2(l_s[h])

    n_steps = qi_tab.shape[0]
    return _pallas(
        body, name="attention_forward", grid=(n_steps,), prefetch=(qi_tab, ki_tab), operands=(q, k, v_t),
        out_shape=[jax.ShapeDtypeStruct((t_len, N_HEADS * HEAD_DIM), F32),
                   jax.ShapeDtypeStruct((N_HEADS, 1, t_len), F32)],
        in_specs=[pl.BlockSpec((N_HEADS, tq, QK_DIM), lambda s, qt, kt: (0, qt[s], 0)),
                  pl.BlockSpec((N_HEADS, tq, QK_DIM), lambda s, qt, kt: (0, kt[s], 0)),
                  pl.BlockSpec((N_HEADS, HEAD_DIM, tq), lambda s, qt, kt: (0, 0, kt[s]))],
        out_specs=[pl.BlockSpec((tq, N_HEADS * HEAD_DIM), lambda s, qt, kt: (qt[s], 0)),
                   pl.BlockSpec((N_HEADS, 1, tq), lambda s, qt, kt: (0, 0, qt[s]))],
        scratch_shapes=[pltpu.VMEM((N_HEADS, 1, tq), F32), pltpu.VMEM((N_HEADS, 1, tq), F32),
                        pltpu.VMEM((N_HEADS, HEAD_DIM, tq), F32)],
        params=_params(48, ("arbitrary",)), exchange=exchange,
        first=lambda qt, kt: pl.program_id(0) == 0, last=lambda qt, kt: pl.program_id(0) == n_steps - 1)


BWD_HEADS = 2


def _attention_backward(q, k, k_t, v, o, d_cat, lse, exchange=None):
    t_len = q.shape[1]
    tq = min(512, t_len)
    nq = t_len // tq
    hp = BWD_HEADS
    qi_tab, ki_tab = _triangle_steps(nq, False)

    def body(qi_ref, ki_ref, q_ref, k_ref, kt_ref, v_ref, o_ref, do_ref, lse_ref, dqt_hbm, dk_ref, dv_ref,
             dqt_s, dk_s, dv_s):
        group, step = pl.program_id(0), pl.program_id(1)
        qi, ki = qi_ref[step], ki_ref[step]

        @pl.when(step == 0)
        def _():
            dqt_s[...] = jnp.zeros_like(dqt_s)

        @pl.when(qi == ki)
        def _():
            dk_s[...] = jnp.zeros_like(dk_s)
            dv_s[...] = jnp.zeros_like(dv_s)

        def accumulate(masked):
            ones = jnp.ones((8, HEAD_DIM), F32)
            for h in range(hp):
                cols = slice(h * HEAD_DIM, (h + 1) * HEAD_DIM)
                do = do_ref[:, cols]
                delta = lax.dot_general(ones, do * o_ref[:, cols], (((1,), (1,)), ((), ())),
                                        preferred_element_type=F32, precision=lax.Precision.HIGHEST)[0:1]
                s_t = _dot_nt(k_ref[h], q_ref[h]) * ATT_LOG2
                if masked:
                    s_t = jnp.where(_key_le_query(tq), s_t, NEG_BIG)
                p_t = jnp.exp2(s_t - lse_ref[h])
                do_b = do.astype(BF16)
                dp_t = _dot_nt(v_ref[h], do_b)
                ds_t = (p_t * (dp_t - delta) * ATT_SCALE).astype(BF16)
                dv_s[h] += _dot(p_t.astype(BF16), do_b)
                dk_s[h] += _dot(ds_t, q_ref[h])
                dqt_s[h, qi] += _dot(kt_ref[h], ds_t)

        @pl.when(ki < qi)
        def _():
            accumulate(False)

        @pl.when(ki == qi)
        def _():
            accumulate(True)
            for h in range(hp):
                pltpu.sync_copy(dqt_s.at[h, qi], dqt_hbm.at[group * hp + h, qi])

        @pl.when(qi == nq - 1)
        def _():
            dk_ref[...] = dk_s[...]
            dv_ref[...] = dv_s[...]

    wide = hp * HEAD_DIM
    n_groups, n_steps = N_HEADS // hp, qi_tab.shape[0]
    return _pallas(
        body, name="attention_backward", grid=(n_groups, n_steps), prefetch=(qi_tab, ki_tab),
        operands=(q, k, k_t, v, o, d_cat, lse),
        out_shape=[jax.ShapeDtypeStruct((N_HEADS, nq, QK_DIM, tq), F32),
                   jax.ShapeDtypeStruct((N_HEADS, t_len, QK_DIM), F32),
                   jax.ShapeDtypeStruct((N_HEADS, t_len, HEAD_DIM), F32)],
        in_specs=[pl.BlockSpec((hp, tq, QK_DIM), lambda g, s, qt, kt: (g, qt[s], 0)),
                  pl.BlockSpec((hp, tq, QK_DIM), lambda g, s, qt, kt: (g, kt[s], 0)),
                  pl.BlockSpec((hp, QK_DIM, tq), lambda g, s, qt, kt: (g, 0, kt[s])),
                  pl.BlockSpec((hp, tq, HEAD_DIM), lambda g, s, qt, kt: (g, kt[s], 0)),
                  pl.BlockSpec((tq, wide), lambda g, s, qt, kt: (qt[s], g)),
                  pl.BlockSpec((tq, wide), lambda g, s, qt, kt: (qt[s], n_groups + g)),
                  pl.BlockSpec((hp, 1, tq), lambda g, s, qt, kt: (g, 0, qt[s]))],
        out_specs=[pl.BlockSpec(memory_space=pl.ANY),
                   pl.BlockSpec((hp, tq, QK_DIM), lambda g, s, qt, kt: (g, kt[s], 0)),
                   pl.BlockSpec((hp, tq, HEAD_DIM), lambda g, s, qt, kt: (g, kt[s], 0))],
        scratch_shapes=[pltpu.VMEM((hp, nq, QK_DIM, tq), F32), pltpu.VMEM((hp, tq, QK_DIM), F32),
                        pltpu.VMEM((hp, tq, HEAD_DIM), F32)],
        params=_params(48, ("arbitrary", "arbitrary")), exchange=exchange,
        first=lambda qt, kt: (pl.program_id(0) == 0) & (pl.program_id(1) == 0),
        last=lambda qt, kt: (pl.program_id(0) == n_groups - 1) & (pl.program_id(1) == n_steps - 1))


def _out_project(o_hg, o_mla, x, g_a, w_out):
    t_len = x.shape[0]
    tm = min(512, t_len)
    half = N_HEADS * HEAD_DIM

    def body(ohg_ref, omla_ref, x_ref, ga_ref, w_ref, cat_ref, mix_ref, xhat_ref, rstd_ref):
        a = ohg_ref[...].astype(BF16)
        b = omla_ref[...].astype(BF16)
        cat_ref[:, 0:half] = a
        cat_ref[:, half:2 * half] = b
        mix = _dot(a, w_ref[0:half, :]) + _dot(b, w_ref[half:2 * half, :])
        mix_ref[...] = mix
        r1 = DN_ALPHA * x_ref[...] + (1.0 + ga_ref[...]) * mix
        xc = r1 - _rowmean(r1)
        rstd = lax.rsqrt(_rowmean(xc * xc) + LN_EPS)
        xhat_ref[...] = xc * rstd
        rstd_ref[...] = rstd

    row = lambda i: (i, 0)
    fixed = lambda i: (0, 0)
    return pl.pallas_call(
        body, name="out_project", grid=(t_len // tm,),
        out_shape=[jax.ShapeDtypeStruct((t_len, D_MODEL), BF16), jax.ShapeDtypeStruct((t_len, D_MODEL), F32),
                   jax.ShapeDtypeStruct((t_len, D_MODEL), F32), jax.ShapeDtypeStruct((t_len, 1), F32)],
        in_specs=[pl.BlockSpec((tm, half), row), pl.BlockSpec((tm, half), row), pl.BlockSpec((tm, D_MODEL), row),
                  pl.BlockSpec((1, D_MODEL), fixed), pl.BlockSpec((D_MODEL, D_MODEL), fixed)],
        out_specs=[pl.BlockSpec((tm, D_MODEL), row), pl.BlockSpec((tm, D_MODEL), row),
                   pl.BlockSpec((tm, D_MODEL), row), pl.BlockSpec((tm, 1), row)],
        compiler_params=_params(48, ("arbitrary",)),
    )(o_hg, o_mla, x, g_a, w_out)


V_LN1G, V_LN1B, V_SCM, V_SHM, V_GM, V_GA, V_LN2G, V_LN2B = range(8)
S_DLN2G, S_DLN2B, S_DGM, S_DSCM, S_DSHM, S_DLN1G, S_DLN1B, S_DGA, S_LOSS = range(9)


def _mlp_and_back(xhat1, rstd1, mix, target, vecs, w1, w2, w_out):
    t_len = xhat1.shape[0]
    tm = min(256, t_len)
    n_ff = w1.shape[0]
    ff = w1.shape[2]

    def body(xhat_ref, rstd_ref, mix_ref, tgt_ref, vec_ref, w1_hbm, w2_hbm, wout_hbm,
             act_ref, dhp_ref, um_ref, dh_ref, dmix_ref, dcat_ref, dr1_ref, sums_ref,
             w1_s, w2_s, wout_s, hp_s, load_sems):
        @pl.when(pl.program_id(0) == 0)
        def _():
            loads = [pltpu.make_async_copy(w1_hbm, w1_s, load_sems.at[0]),
                     pltpu.make_async_copy(w2_hbm, w2_s, load_sems.at[1]),
                     pltpu.make_async_copy(wout_hbm, wout_s, load_sems.at[2])]
            for cp in loads:
                cp.start()
            sums_ref[...] = jnp.zeros_like(sums_ref)
            for cp in loads:
                cp.wait()

        vec = lambda r: vec_ref[r:r + 1, :]
        xhat = xhat_ref[...]
        x1 = xhat * vec(V_LN1G) + vec(V_LN1B)
        um = (x1 * (1.0 + vec(V_SCM)) + vec(V_SHM)).astype(BF16)
        um_ref[...] = um
        h = jnp.zeros((tm, D_MODEL), F32)
        for j in range(n_ff):
            hp = _dot(um, w1_s[j])
            hp_s[j] = hp
            act = jnp.square(jnp.maximum(hp, 0.0)).astype(BF16)
            act_ref[:, j * ff:(j + 1) * ff] = act
            h = h + _dot(act, w2_s[j])
        r2 = DN_ALPHA * x1 + (1.0 + vec(V_GM)) * h
        xc = r2 - _rowmean(r2)
        rstd2 = lax.rsqrt(_rowmean(xc * xc) + LN_EPS)
        xhat2 = xc * rstd2
        err = xhat2 * vec(V_LN2G) + vec(V_LN2B) - tgt_ref[...]
        loss = 0.5 * jnp.sum(_rowmean(err * err))
        dy = err * (1.0 / D_MODEL)
        dxh = dy * vec(V_LN2G)
        dr2 = rstd2 * (dxh - _rowmean(dxh) - xhat2 * _rowmean(dxh * xhat2))
        dh = ((1.0 + vec(V_GM)) * dr2).astype(BF16)
        dh_ref[...] = dh
        sums_ref[S_DLN2G:S_DLN2G + 1, :] += _colsum(dy * xhat2)
        sums_ref[S_DLN2B:S_DLN2B + 1, :] += _colsum(dy)
        sums_ref[S_DGM:S_DGM + 1, :] += _colsum(dr2 * h)
        sums_ref[S_LOSS:S_LOSS + 1, :] += jnp.full((1, D_MODEL), loss, F32)
        du = jnp.zeros((tm, D_MODEL), F32)
        for j in range(n_ff):
            dhp = (_dot_nt(dh, w2_s[j]) * (2.0 * jnp.maximum(hp_s[j], 0.0))).astype(BF16)
            dhp_ref[:, j * ff:(j + 1) * ff] = dhp
            du = du + _dot_nt(dhp, w1_s[j])
        sums_ref[S_DSCM:S_DSCM + 1, :] += _colsum(du * x1)
        sums_ref[S_DSHM:S_DSHM + 1, :] += _colsum(du)
        dx1 = DN_ALPHA * dr2 + du * (1.0 + vec(V_SCM))
        sums_ref[S_DLN1G:S_DLN1G + 1, :] += _colsum(dx1 * xhat)
        sums_ref[S_DLN1B:S_DLN1B + 1, :] += _colsum(dx1)
        dxh1 = dx1 * vec(V_LN1G)
        dr1 = rstd_ref[...] * (dxh1 - _rowmean(dxh1) - xhat * _rowmean(dxh1 * xhat))
        dr1_ref[...] = dr1
        sums_ref[S_DGA:S_DGA + 1, :] += _colsum(dr1 * mix_ref[...])
        dmix = ((1.0 + vec(V_GA)) * dr1).astype(BF16)
        dmix_ref[...] = dmix
        dcat_ref[...] = _dot_nt(dmix, wout_s[...])

    row = lambda i: (i, 0)
    fixed = lambda i: (0, 0)
    any_spec = pl.BlockSpec(memory_space=pl.ANY)
    return pl.pallas_call(
        body, name="mlp_and_back", grid=(t_len // tm,),
        out_shape=[jax.ShapeDtypeStruct((t_len, D_FF), BF16), jax.ShapeDtypeStruct((t_len, D_FF), BF16),
                   jax.ShapeDtypeStruct((t_len, D_MODEL), BF16), jax.ShapeDtypeStruct((t_len, D_MODEL), BF16),
                   jax.ShapeDtypeStruct((t_len, D_MODEL), BF16), jax.ShapeDtypeStruct((t_len, D_MODEL), F32),
                   jax.ShapeDtypeStruct((t_len, D_MODEL), F32), jax.ShapeDtypeStruct((16, D_MODEL), F32)],
        in_specs=[pl.BlockSpec((tm, D_MODEL), row), pl.BlockSpec((tm, 1), row), pl.BlockSpec((tm, D_MODEL), row),
                  pl.BlockSpec((tm, D_MODEL), row), pl.BlockSpec((8, D_MODEL), fixed), any_spec, any_spec, any_spec],
        out_specs=[pl.BlockSpec((tm, D_FF), row), pl.BlockSpec((tm, D_FF), row), pl.BlockSpec((tm, D_MODEL), row),
                   pl.BlockSpec((tm, D_MODEL), row), pl.BlockSpec((tm, D_MODEL), row), pl.BlockSpec((tm, D_MODEL), row),
                   pl.BlockSpec((tm, D_MODEL), row), pl.BlockSpec((16, D_MODEL), fixed)],
        scratch_shapes=[pltpu.VMEM(w1.shape, BF16), pltpu.VMEM(w2.shape, BF16), pltpu.VMEM(w_out.shape, BF16),
                        pltpu.VMEM((n_ff, tm, ff), F32), pltpu.SemaphoreType.DMA((3,))],
        compiler_params=_params(56, ("arbitrary",)),
    )(xhat1, rstd1, mix, target, vecs, w1, w2, w_out)


def _in_project_backward(dq, dk, dv, cq, ckv, pos, invf, q_norm_w, kv_norm_w, w_q, w_kv,
                         d_hq, d_hf, d_hi, d_hg, w_in, dr1, x, sc_a, exchange=None):
    t_len = x.shape[0]
    tm = min(256, t_len)
    per_q = dq.shape[3] // tm
    hgw = N_HEADS * HEAD_DIM

    def body(dq_ref, dk_ref, dv_ref, cq_ref, ckv_ref, pos_ref, invf_ref, qn_ref, kvn_ref, wq_ref, wkv_ref,
             dhq_ref, dhf_ref, dhi_ref, dhg_ref, win_ref, dr1_ref, x_ref, sc_ref,
             dz_ref, dqf_ref, dkvu_ref, cqn_ref, ckvn_ref, gx_ref, sums_ref):
        @pl.when(pl.program_id(0) == 0)
        def _():
            sums_ref[...] = jnp.zeros_like(sums_ref)

        cos_t, sin_t = _rope_tables(pos_ref[...], invf_ref[...])
        cq = cq_ref[...]
        ckv = ckv_ref[...]
        rq = lax.rsqrt(_rowmean(cq * cq) + RMS_EPS)
        rkv = lax.rsqrt(_rowmean(ckv * ckv) + RMS_EPS)
        cqn_ref[...] = (cq * rq * qn_ref[...]).astype(BF16)
        ckvn_ref[...] = (ckv * rkv * kvn_ref[...]).astype(BF16)
        d_cqn = jnp.zeros((tm, Q_RANK), F32)
        d_ckvn = jnp.zeros((tm, KV_RANK), F32)
        d_kpe = jnp.zeros((tm, 128), F32)
        for h in range(N_HEADS):
            dqh = jnp.transpose(dq_ref[h])
            dqf_ref[h, :, 0:HEAD_DIM] = dqh[:, :HEAD_DIM].astype(BF16)
            dqf_ref[h, :, HEAD_DIM:QK_DIM] = _unrope(dqh[:, HEAD_DIM:], cos_t, sin_t).astype(BF16)
            d_cqn = d_cqn + _dot_nt(dqf_ref[h], wq_ref[h])
            dkh = dk_ref[h]
            d_kpe = d_kpe + dkh[:, HEAD_DIM:]
            dkvu_ref[h, :, 0:HEAD_DIM] = dkh[:, :HEAD_DIM].astype(BF16)
            dkvu_ref[h, :, HEAD_DIM:2 * HEAD_DIM] = dv_ref[h].astype(BF16)
            d_ckvn = d_ckvn + _dot_nt(dkvu_ref[h], wkv_ref[h])
        dyq = d_cqn * qn_ref[...]
        dykv = d_ckvn * kvn_ref[...]
        sums_ref[2:3, 0:Q_RANK] += _colsum(d_cqn * cq * rq)
        sums_ref[3:4, 0:KV_RANK] += _colsum(d_ckvn * ckv * rkv)
        dz_ref[:, 0:hgw] = dhq_ref[...]
        dz_ref[:, hgw:2 * hgw] = dhf_ref[...]
        dz_ref[:, 2 * hgw:3 * hgw] = dhi_ref[...]
        dz_ref[:, 3 * hgw:4 * hgw] = dhg_ref[...]
        dz_ref[:, HG_COLS:HG_COLS + Q_RANK] = (rq * dyq - cq * (rq * rq * rq) * _rowmean(dyq * cq)).astype(BF16)
        dz_ref[:, HG_COLS + Q_RANK:HG_COLS + Q_RANK + KV_RANK] = (
            rkv * dykv - ckv * (rkv * rkv * rkv) * _rowmean(dykv * ckv)).astype(BF16)
        dz_ref[:, HG_COLS + Q_RANK + KV_RANK:] = _unrope(d_kpe, cos_t, sin_t).astype(BF16)
        du = _dot_nt(dz_ref[...], win_ref[...])
        xv = x_ref[...]
        gx_ref[...] = DN_ALPHA * dr1_ref[...] + (1.0 + sc_ref[...]) * du
        sums_ref[0:1, :] += _colsum(du * xv)
        sums_ref[1:2, :] += _colsum(du)

    row = lambda i: (i, 0)
    fixed2 = lambda i: (0, 0)
    fixed3 = lambda i: (0, 0, 0)
    heads = lambda i: (0, i, 0)
    n_tiles = t_len // tm
    return _pallas(
        body, name="in_project_backward", grid=(n_tiles,),
        operands=(dq, dk, dv, cq, ckv, pos, invf, q_norm_w, kv_norm_w, w_q, w_kv, d_hq, d_hf, d_hi, d_hg, w_in, dr1, x,
                  sc_a),
        out_shape=[jax.ShapeDtypeStruct((t_len, IN_COLS_PAD), BF16), jax.ShapeDtypeStruct((N_HEADS, t_len, QK_DIM), BF16),
                   jax.ShapeDtypeStruct((N_HEADS, t_len, 2 * HEAD_DIM), BF16), jax.ShapeDtypeStruct((t_len, Q_RANK), BF16),
                   jax.ShapeDtypeStruct((t_len, KV_RANK), BF16), jax.ShapeDtypeStruct((t_len, D_MODEL), F32),
                   jax.ShapeDtypeStruct((8, D_MODEL), F32)],
        in_specs=[pl.BlockSpec((N_HEADS, None, QK_DIM, tm), lambda i: (0, i // per_q, 0, i % per_q)),
                  pl.BlockSpec((N_HEADS, tm, QK_DIM), heads),
                  pl.BlockSpec((N_HEADS, tm, HEAD_DIM), heads), pl.BlockSpec((tm, Q_RANK), row),
                  pl.BlockSpec((tm, KV_RANK), row), pl.BlockSpec((tm, 1), row), pl.BlockSpec((1, 128), fixed2),
                  pl.BlockSpec((1, Q_RANK), fixed2), pl.BlockSpec((1, KV_RANK), fixed2),
                  pl.BlockSpec((N_HEADS, Q_RANK, QK_DIM), fixed3), pl.BlockSpec((N_HEADS, KV_RANK, 2 * HEAD_DIM), fixed3),
                  pl.BlockSpec((tm, hgw), row), pl.BlockSpec((tm, hgw), row), pl.BlockSpec((tm, hgw), row),
                  pl.BlockSpec((tm, hgw), row), pl.BlockSpec((D_MODEL, IN_COLS_PAD), fixed2),
                  pl.BlockSpec((tm, D_MODEL), row), pl.BlockSpec((tm, D_MODEL), row), pl.BlockSpec((1, D_MODEL), fixed2)],
        out_specs=[pl.BlockSpec((tm, IN_COLS_PAD), row), pl.BlockSpec((N_HEADS, tm, QK_DIM), heads),
                   pl.BlockSpec((N_HEADS, tm, 2 * HEAD_DIM), heads), pl.BlockSpec((tm, Q_RANK), row),
                   pl.BlockSpec((tm, KV_RANK), row), pl.BlockSpec((tm, D_MODEL), row), pl.BlockSpec((8, D_MODEL), fixed2)],
        params=_params(48, ("arbitrary",)), exchange=exchange,
        first=lambda: pl.program_id(0) == 0, last=lambda: pl.program_id(0) == n_tiles - 1)


def _weight_grad(a, b, name, n_blocks, bn, a_blocked=False, b_blocked=True):
    t_len = a.shape[0]
    m = a.shape[1] // n_blocks if a_blocked else a.shape[1]
    bt = min(512, t_len)

    def body(a_ref, b_ref, o_ref):
        @pl.when(pl.program_id(1) == 0)
        def _():
            o_ref[...] = jnp.zeros_like(o_ref)

        o_ref[...] += _dot_tn(a_ref[...].astype(BF16), b_ref[...].astype(BF16))

    a_spec = pl.BlockSpec((bt, m), (lambda n, t: (t, n)) if a_blocked else (lambda n, t: (t, 0)))
    if b.ndim == 3:
        b_spec = pl.BlockSpec((None, bt, bn), lambda n, t: (n, t, 0))
    else:
        b_spec = pl.BlockSpec((bt, bn), (lambda n, t: (t, n)) if b_blocked else (lambda n, t: (t, 0)))
    return pl.pallas_call(
        body, name=name, grid=(n_blocks, t_len // bt),
        out_shape=jax.ShapeDtypeStruct((n_blocks, m, bn), F32),
        in_specs=[a_spec, b_spec],
        out_specs=pl.BlockSpec((None, m, bn), lambda n, t: (n, 0, 0)),
        compiler_params=_params(40, ("arbitrary", "arbitrary")),
    )(a, b)


def _reduce_small(gathered, lb_raw):
    def body(g_ref, lb_ref, tot_ref, dlb_ref):
        tot = g_ref[0]
        for d in range(1, N_DEV):
            tot = tot + g_ref[d]
        tot_ref[...] = tot
        a = lb_ref[...]
        m = jnp.max(a, axis=0, keepdims=True)
        e = jnp.exp(a - m)
        lb = e[0:1] / jnp.sum(e, axis=0, keepdims=True)
        d0 = tot[10:11, 0:512] * lb * (1.0 - lb)
        dlb_ref[0:1, :] = d0
        dlb_ref[1:2, :] = -d0

    return pl.pallas_call(
        body, name="reduce_small",
        out_shape=[jax.ShapeDtypeStruct((SMALL_ROWS, D_MODEL), F32), jax.ShapeDtypeStruct((2, 512), F32)],
    )(gathered, lb_raw)


def _adamw_update(w, gv, m, v):
    nm = ADAM_B1 * m + (1.0 - ADAM_B1) * gv
    nv = ADAM_B2 * v + (1.0 - ADAM_B2) * jnp.square(gv)
    m_hat = nm / (1.0 - ADAM_B1 ** ADAM_STEP)
    v_hat = nv / (1.0 - ADAM_B2 ** ADAM_STEP)
    return -ADAM_LR * (m_hat / (jnp.sqrt(v_hat) + ADAM_EPS) + ADAM_WD * w), nm, nv


def _adamw_halves(core, w, mine, theirs, m, v, name):
    rows, cols = w.shape
    h = rows // 2
    tr = _row_tile(h)
    per_half = h // tr

    def body(core_ref, w_ref, mine_ref, theirs_ref, m_ref, v_ref, g_ref, d_ref, nm_ref, nv_ref):
        is_mine = pl.program_id(0) // per_half == core_ref[0]
        gv = jnp.where(is_mine, mine_ref[...], theirs_ref[...])
        g_ref[...] = gv
        d_ref[...], nm_ref[...], nv_ref[...] = _adamw_update(w_ref[...], gv, m_ref[...], v_ref[...])

    full = pl.BlockSpec((tr, cols), lambda i, core_ref: (i, 0))
    part = pl.BlockSpec((tr, cols), lambda i, core_ref: (i % per_half, 0))
    return pl.pallas_call(
        body, name=name, out_shape=[jax.ShapeDtypeStruct(w.shape, F32)] * 4,
        grid_spec=pltpu.PrefetchScalarGridSpec(
            num_scalar_prefetch=1, grid=(rows // tr,), in_specs=[full, part, part, full, full], out_specs=[full] * 4),
        compiler_params=_params(40, ("arbitrary",)),
    )(core, w, mine, theirs, m, v)


def _adamw(w, g, m, v, name):
    rows, cols = w.shape
    tr = _row_tile(rows) if rows >= 8 else rows

    def body(w_ref, g_ref, m_ref, v_ref, d_ref, nm_ref, nv_ref):
        d_ref[...], nm_ref[...], nv_ref[...] = _adamw_update(w_ref[...], g_ref[...], m_ref[...], v_ref[...])

    spec = pl.BlockSpec((tr, cols), lambda i: (i, 0))
    return pl.pallas_call(
        body, name=name, grid=(rows // tr,),
        out_shape=[jax.ShapeDtypeStruct(w.shape, F32)] * 3,
        in_specs=[spec] * 4, out_specs=[spec] * 3,
        compiler_params=_params(40, ("arbitrary",)),
    )(w, g, m, v)


def kernel(x, c, positions, w_ada, b_ada, w_in, hg_lower_bounds, hg_norm_w, mla_q_norm_w, w_q_up, mla_kv_norm_w, w_kv_up, w_out, ln1_g, ln1_b, w_mlp_in, w_mlp_out, ln2_g, ln2_b, loss_target, m_w_ada, m_b_ada, m_w_in, m_hg_lower_bounds, m_hg_norm_w, m_mla_q_norm_w, m_w_q_up, m_mla_kv_norm_w, m_w_kv_up, m_w_out, m_ln1_g, m_ln1_b, m_w_mlp_in, m_w_mlp_out, m_ln2_g, m_ln2_b, v_w_ada, v_b_ada, v_w_in, v_hg_lower_bounds, v_hg_norm_w, v_mla_q_norm_w, v_w_q_up, v_mla_kv_norm_w, v_w_kv_up, v_w_out, v_ln1_g, v_ln1_b, v_w_mlp_in, v_w_mlp_out, v_ln2_g, v_ln2_b):
    ix, iy, ic = _mesh_pos()
    chip = 2 * ix + iy
    me = 4 * ix + 2 * iy + ic
    core_arr = jnp.reshape(ic, (1,)).astype(jnp.int32)
    chip_arr = jnp.reshape(chip, (1,)).astype(jnp.int32)

    xs = x[0]
    target = loss_target[0]
    t_len = xs.shape[0]
    pos = positions.astype(F32).reshape(t_len, 1)
    inv = 1.0 / (ROPE_THETA ** (jnp.arange(0, ROPE_DIM, 2, dtype=F32) / ROPE_DIM))
    invf = jnp.concatenate([inv, inv, jnp.zeros((128 - ROPE_DIM,), F32)]).reshape(1, 128)

    ada_cols = w_ada.shape[2]
    c_all = _allgather8(jnp.broadcast_to(c, (8, D_MODEL)), "gather_c")[:, 0, :]
    b_shard = lax.dynamic_slice(b_ada, (0, chip * ada_cols), (1, ada_cols))
    mod_cols, cond16 = _ada_project(c_all, w_ada[0], b_shard)
    mod_all = _allgather8(mod_cols, "gather_mod")
    mod_mine = lax.dynamic_slice(mod_all, (0, me, 0), (N_DEV, 1, ada_cols))[::2, 0, :]
    mod_mine = mod_mine.reshape(6, D_MODEL)
    sh_a, sc_a, g_a, sh_m, sc_m, g_m = (mod_mine[i:i + 1] for i in range(6))

    def slot(w):
        rows, cols = w.shape
        own = w.astype(BF16).reshape(1, 2, rows // 2, cols)
        return lax.dynamic_update_slice(jnp.zeros((N_CHIPS, 2, rows // 2, cols), BF16), own, (chip, 0, 0, 0))

    def whole(s):
        return s.reshape(N_CHIPS, 2 * s.shape[2], s.shape[3])

    def halved(g):
        return g.reshape(N_CHIPS, 2, g.shape[1] // 2, g.shape[2])

    early = _run_exchange(_gather_over_ici([slot(w_in[0]), slot(w_q_up[0]), slot(w_kv_up[0]), slot(w_out[0])]),
                          "gather_mixer_weights_ici")
    g_in, g_q, g_kv, g_out = (whole(s) for s in _run_exchange(_gather_over_d2d(early), "gather_mixer_weights_d2d"))
    w_in_full = jnp.transpose(g_in, (1, 0, 2)).reshape(D_MODEL, IN_COLS)
    w_in_full = jnp.pad(w_in_full, ((0, 0), (0, IN_COLS_PAD - IN_COLS)))
    w_q_full = jnp.pad(g_q, ((0, 0), (0, 0), (0, QK_DIM - g_q.shape[2])))
    w_out_full = g_out.reshape(D_MODEL, D_MODEL)

    u_a, zhg, cq, ckv, q, k, k_t, v, v_t = _in_project(xs, pos, sc_a, sh_a, w_in_full, mla_q_norm_w, mla_kv_norm_w,
                                                      w_q_full, g_kv, invf)
    (o_pre, o_hg, states), mlp_slots = _hgrn_forward(
        zhg, hg_lower_bounds, hg_norm_w, _gather_over_ici([slot(w_mlp_in[0]), slot(w_mlp_out[0])]))
    (o_mla, lse), mlp_slots = _attention_forward(q, k, v_t, _gather_over_d2d(mlp_slots))
    g_w1, g_w2 = (whole(s) for s in mlp_slots)
    cat, mix, xhat1, rstd1 = _out_project(o_hg, o_mla, xs, g_a, w_out_full)
    vecs = jnp.concatenate([ln1_g, ln1_b, sc_m, sh_m, g_m, g_a, ln2_g, ln2_b], axis=0)
    act, dhp, um, dh, dmix, d_cat, dr1, mlp_sums = _mlp_and_back(xhat1, rstd1, mix, target, vecs, g_w1, g_w2, w_out_full)

    gw_1 = _weight_grad(um, dhp, "grad_w_mlp_in", N_CHIPS, D_FF // N_CHIPS)
    gw_2 = _weight_grad(act, dh, "grad_w_mlp_out", N_CHIPS, D_MODEL, a_blocked=True, b_blocked=False)
    mlp_grads = [halved(gw_1), halved(gw_2)]
    (dq, dk, dv), landed = _attention_backward(q, k, k_t, v, o_mla, d_cat, lse, _pair_exchange(mlp_grads))
    chip_sums = [_add_pair(core_arr, g, l) for g, l in zip(mlp_grads, landed)]
    (d_hq, d_hf, d_hi, d_hg, hg_sums), landed = _hgrn_backward(
        zhg, hg_lower_bounds, hg_norm_w, o_pre, d_cat, states, _chip_exchange([b for _, b in chip_sums]))
    mlp_mine = [_add_chips(chip_arr, p, l) for (p, _), l in zip(chip_sums, landed)]
    (dz, dqf, dkvu, cqn, ckvn, grad_x, in_sums), mlp_theirs = _in_project_backward(
        dq, dk, dv, cq, ckv, pos, invf, mla_q_norm_w, mla_kv_norm_w, w_q_full, g_kv,
        d_hq, d_hf, d_hi, d_hg, w_in_full, dr1, xs, sc_a, _pair_send(mlp_mine))

    gw_in = _weight_grad(u_a, dz, "grad_w_in", 3, IN_COLS_PAD // 3)
    gw_in = jnp.transpose(gw_in, (1, 0, 2)).reshape(D_MODEL, IN_COLS_PAD)[:, :IN_COLS]
    gw_in = jnp.transpose(gw_in.reshape(D_MODEL, N_CHIPS, IN_COLS // N_CHIPS), (1, 0, 2))
    gw_q = _weight_grad(cqn, dqf, "grad_w_q_up", N_HEADS, QK_DIM)[:, :, :HEAD_DIM + ROPE_DIM]
    gw_kv = _weight_grad(ckvn, dkvu, "grad_w_kv_up", N_HEADS, 2 * HEAD_DIM)
    gw_out = _weight_grad(cat, dmix, "grad_w_out", 1, D_MODEL).reshape(N_CHIPS, D_MODEL // N_CHIPS, D_MODEL)
    mixer_grads = [halved(g) for g in (gw_in, gw_q, gw_kv, gw_out)]
    landed = _run_exchange(_pair_exchange(mixer_grads), "grad_pair_exchange")
    chip_sums = [_add_pair(core_arr, g, l) for g, l in zip(mixer_grads, landed)]
    landed = _run_exchange(_chip_exchange([b for _, b in chip_sums]), "grad_chip_exchange")
    mixer_mine = [_add_chips(chip_arr, p, l) for (p, _), l in zip(chip_sums, landed)]
    mixer_theirs = _run_exchange(_pair_send(mixer_mine), "grad_pair_send")
    halves_mine = dict(zip(("w_in", "w_q_up", "w_kv_up", "w_out", "w_mlp_in", "w_mlp_out"), mixer_mine + mlp_mine))
    halves_theirs = dict(zip(("w_in", "w_q_up", "w_kv_up", "w_out", "w_mlp_in", "w_mlp_out"),
                             list(mixer_theirs) + list(mlp_theirs)))

    zeros = lambda n: jnp.zeros((1, n), F32)
    small = jnp.concatenate([
        in_sums[1:2], in_sums[0:1], mlp_sums[S_DGA:S_DGA + 1],
        mlp_sums[S_DSHM:S_DSHM + 1], mlp_sums[S_DSCM:S_DSCM + 1], mlp_sums[S_DGM:S_DGM + 1],
        mlp_sums[S_DLN1G:S_DLN1G + 1], mlp_sums[S_DLN1B:S_DLN1B + 1],
        mlp_sums[S_DLN2G:S_DLN2G + 1], mlp_sums[S_DLN2B:S_DLN2B + 1],
        jnp.concatenate([hg_sums[0:1], hg_sums[1:2]], axis=1),
        jnp.concatenate([in_sums[2:3, :Q_RANK], in_sums[3:4, :KV_RANK], zeros(D_MODEL - Q_RANK - KV_RANK)], axis=1),
        mlp_sums[S_LOSS:S_LOSS + 1],
        jnp.zeros((SMALL_ROWS - 13, D_MODEL), F32)], axis=0)
    small_all = _allgather8(small, "gather_small")
    tot, g_lb = _reduce_small(small_all, hg_lower_bounds)
    loss = tot[12, 0]
    g_b_ada = tot[0:6].reshape(1, 6 * D_MODEL)
    g_ln1_g, g_ln1_b, g_ln2_g, g_ln2_b = tot[6:7], tot[7:8], tot[8:9], tot[9:10]
    g_hg_norm = tot[10:11, 512:1024]
    g_q_norm = tot[11:12, 0:Q_RANK]
    g_kv_norm = tot[11:12, Q_RANK:Q_RANK + KV_RANK]

    d_mod_all = small_all[:, 0:6, :].reshape(N_DEV, 6 * D_MODEL)
    d_mod_cols = lax.dynamic_slice(d_mod_all, (0, chip * ada_cols), (N_DEV, ada_cols))
    d_mod_cols = jnp.concatenate([d_mod_cols, jnp.zeros_like(d_mod_cols)], axis=0)
    g_w_ada = _weight_grad(cond16, d_mod_cols, "grad_w_ada", 1, ada_cols)[0]

    names = ["w_ada", "b_ada", "w_in", "hg_lower_bounds", "hg_norm_w", "mla_q_norm_w", "w_q_up", "mla_kv_norm_w",
             "w_kv_up", "w_out", "ln1_g", "ln1_b", "w_mlp_in", "w_mlp_out", "ln2_g", "ln2_b"]
    weights = [w_ada, b_ada, w_in, hg_lower_bounds, hg_norm_w, mla_q_norm_w, w_q_up, mla_kv_norm_w,
               w_kv_up, w_out, ln1_g, ln1_b, w_mlp_in, w_mlp_out, ln2_g, ln2_b]
    moms = [m_w_ada, m_b_ada, m_w_in, m_hg_lower_bounds, m_hg_norm_w, m_mla_q_norm_w, m_w_q_up, m_mla_kv_norm_w,
            m_w_kv_up, m_w_out, m_ln1_g, m_ln1_b, m_w_mlp_in, m_w_mlp_out, m_ln2_g, m_ln2_b]
    vels = [v_w_ada, v_b_ada, v_w_in, v_hg_lower_bounds, v_hg_norm_w, v_mla_q_norm_w, v_w_q_up, v_mla_kv_norm_w,
            v_w_kv_up, v_w_out, v_ln1_g, v_ln1_b, v_w_mlp_in, v_w_mlp_out, v_ln2_g, v_ln2_b]
    grads2d = [g_w_ada, g_b_ada, None, g_lb, g_hg_norm, g_q_norm, None, g_kv_norm,
               None, None, g_ln1_g, g_ln1_b, None, None, g_ln2_g, g_ln2_b]
    out_g, out_d, out_m, out_v = [], [], [], []
    for name, w, g, m, vv in zip(names, weights, grads2d, moms, vels):
        if g is None:
            shape2 = w.shape[1:]
            g, d, nm, nv = _adamw_halves(core_arr, w.reshape(shape2), halves_mine[name], halves_theirs[name],
                                         m.reshape(shape2), vv.reshape(shape2), "adamw_" + name)
        else:
            shape2 = g.shape
            d, nm, nv = _adamw(w.reshape(shape2), g, m.reshape(shape2), vv.reshape(shape2), "adamw_" + name)
        out_g.append(g.reshape(w.shape))
        out_d.append(d.reshape(w.shape))
        out_m.append(nm.reshape(w.shape))
        out_v.append(nv.reshape(w.shape))
    return (loss, grad_x[None], *out_g, *out_d, *out_m, *out_v)
```

```python
import functools

import jax
import jax.numpy as jnp
from jax import lax
from jax.experimental import pallas as pl
from jax.experimental.pallas import tpu as pltpu

F32 = jnp.float32
BF16 = jnp.bfloat16
MESH_IDS = pl.DeviceIdType.MESH

D_MODEL = 1024
N_HEADS = 4
HEAD_DIM = 128
ROPE_DIM = 64
HG_CHUNK = 64
HG_COLS = 2048
Q_RANK = 256
KV_RANK = 256
IN_COLS = 2624
IN_COLS_PAD = 2688
QK_DIM = 256
D_FF = 4096
N_CHIPS = 4
N_DEV = 8
ROPE_THETA = 10000.0
RMS_EPS = 1e-6
LN_EPS = 1e-5
DN_ALPHA = 2.0 ** 0.25
ATT_SCALE = (HEAD_DIM + ROPE_DIM) ** -0.5
NEG_BIG = -1e30
ADAM_LR = 0.001
ADAM_B1 = 0.9
ADAM_B2 = 0.999
ADAM_EPS = 1e-08
ADAM_WD = 0.01
ADAM_STEP = 10
SMALL_ROWS = 16
MIB = 1024 * 1024


def _dot(a, b):
    return jnp.dot(a, b, preferred_element_type=F32)


def _dot_nt(a, b):
    return lax.dot_general(a, b, (((1,), (1,)), ((), ())), preferred_element_type=F32)


def _dot_tn(a, b):
    return lax.dot_general(a, b, (((0,), (0,)), ((), ())), preferred_element_type=F32)


def _dot_f32(a, b):
    return jnp.dot(a, b, preferred_element_type=F32, precision=lax.Precision.HIGHEST)


def _params(vmem_mib, semantics=None):
    return pltpu.CompilerParams(vmem_limit_bytes=vmem_mib * MIB, dimension_semantics=semantics)


def _sigmoid(v):
    return 1.0 / (1.0 + jnp.exp(-v))


def _colsum(v):
    return jnp.sum(v, axis=0, keepdims=True)


def _rowmean(v):
    return jnp.mean(v, axis=-1, keepdims=True)


def _rope_tables(pos, invf):
    ang = pos * invf
    lane = lax.broadcasted_iota(jnp.int32, ang.shape, 1)
    cos_t = jnp.where(lane < ROPE_DIM, jnp.cos(ang), 0.0)
    sin = jnp.sin(ang)
    sin_t = jnp.where(lane < ROPE_DIM // 2, -sin, jnp.where(lane < ROPE_DIM, sin, 0.0))
    return cos_t, sin_t


def _swap_halves(t):
    lane = lax.broadcasted_iota(jnp.int32, t.shape, 1)
    return jnp.where(lane < ROPE_DIM // 2, pltpu.roll(t, 128 - ROPE_DIM // 2, 1), pltpu.roll(t, ROPE_DIM // 2, 1))


def _rope(t, cos_t, sin_t):
    return t * cos_t + _swap_halves(t) * sin_t


def _unrope(g, cos_t, sin_t):
    return g * cos_t - _swap_halves(g) * sin_t


def _mesh_pos():
    return lax.axis_index("x"), lax.axis_index("y"), lax.axis_index("c")


def _other_chips(x, y):
    out = []
    for dx, dy in ((1, 0), (0, 1), (1, 1)):
        px = 1 - x if dx else x
        py = 1 - y if dy else y
        out.append(((px, py), 2 * px + py))
    return out


def _allgather8(a, name):
    rows, cols = a.shape

    def body(a_ref, out_ref, send_sems, recv_sems):
        x, y, c = _mesh_pos()
        me = 4 * x + 2 * y + c
        out_ref[me] = a_ref[...]
        peers = []
        for r in range(1, N_DEV):
            px = 1 - x if r & 4 else x
            py = 1 - y if r & 2 else y
            pc = 1 - c if r & 1 else c
            peers.append(((px, py, pc), 4 * px + 2 * py + pc))

        def copy(r, block, to):
            return pltpu.make_async_remote_copy(
                src_ref=a_ref, dst_ref=out_ref.at[block], send_sem=send_sems.at[r], recv_sem=recv_sems.at[r],
                device_id=to, device_id_type=MESH_IDS)

        sends = [copy(r, me, peer) for r, (peer, _) in enumerate(peers)]
        for cp in sends:
            cp.start()
        for r, (peer, idx) in enumerate(peers):
            copy(r, idx, peer).wait_recv()
        for cp in sends:
            cp.wait_send()

    return pl.pallas_call(
        body, name=name,
        out_shape=jax.ShapeDtypeStruct((N_DEV, rows, cols), a.dtype),
        in_specs=[pl.BlockSpec(memory_space=pltpu.VMEM)],
        out_specs=pl.BlockSpec(memory_space=pltpu.VMEM),
        scratch_shapes=[pltpu.SemaphoreType.DMA((N_DEV - 1,)), pltpu.SemaphoreType.DMA((N_DEV - 1,))],
    )(a)


class _Exchange:
    def __init__(self, inputs, out_shapes, aliases, sems, start, finish):
        self.inputs, self.out_shapes, self.aliases, self.sems = list(inputs), list(out_shapes), dict(aliases), list(sems)
        self.start, self.finish = start, finish


def _from_copies(inputs, out_shapes, aliases, sems, copies):
    def start(ins, outs, sem_refs):
        for send, _ in copies(ins, outs, sem_refs):
            send.start()

    def finish(ins, outs, sem_refs):
        for send, recv in copies(ins, outs, sem_refs):
            recv.wait_recv()
            send.wait_send()

    return _Exchange(inputs, out_shapes, aliases, sems, start, finish)


HBM_MIN_BYTES = 256 * 1024


def _in_hbm(a):
    if a.size * a.dtype.itemsize < HBM_MIN_BYTES:
        return a
    return pltpu.with_memory_space_constraint(a, pltpu.HBM)


def _out_hbm(s):
    if s.size * s.dtype.itemsize < HBM_MIN_BYTES:
        return s
    return pltpu.HBM(s.shape, s.dtype)


def _pcall(body, *, operands, out_shape, **kwargs):
    single = not isinstance(out_shape, (list, tuple))
    shapes = [_out_hbm(s) for s in ([out_shape] if single else out_shape)]
    return pl.pallas_call(body, out_shape=shapes[0] if single else shapes, **kwargs)(*[_in_hbm(a) for a in operands])


def _run_exchange(exchange, name):
    n_in, n_out = len(exchange.inputs), len(exchange.out_shapes)

    def body(*refs):
        ins, outs, sem_refs = refs[:n_in], refs[n_in:n_in + n_out], refs[n_in + n_out:]
        exchange.start(ins, outs, sem_refs)
        exchange.finish(ins, outs, sem_refs)

    any_spec = pl.BlockSpec(memory_space=pl.ANY)
    return pl.pallas_call(
        body, name=name, out_shape=[_out_hbm(s) for s in exchange.out_shapes],
        in_specs=[any_spec] * n_in, out_specs=[any_spec] * n_out,
        scratch_shapes=exchange.sems, input_output_aliases=exchange.aliases,
    )(*[_in_hbm(a) for a in exchange.inputs])


def _pallas(body, *, name, operands, in_specs, out_shape, out_specs, params, scratch_shapes=(), grid=(), prefetch=(),
            exchange=None, first=None, last=None):
    n_pre, n_in, n_out, n_scr = len(prefetch), len(in_specs), len(out_specs), len(scratch_shapes)
    ex_in = exchange.inputs if exchange else []
    ex_out = exchange.out_shapes if exchange else []
    ex_sems = exchange.sems if exchange else []

    def full_body(*refs):
        pre, rest = refs[:n_pre], refs[n_pre:]
        ins, rest = rest[:n_in], rest[n_in:]
        xin, rest = rest[:len(ex_in)], rest[len(ex_in):]
        outs, rest = rest[:n_out], rest[n_out:]
        xout, rest = rest[:len(ex_out)], rest[len(ex_out):]
        scr, sem_refs = rest[:n_scr], rest[n_scr:]
        if exchange:
            @pl.when(first(*pre))
            def _():
                exchange.start(xin, xout, sem_refs)

        body(*pre, *ins, *outs, *scr)
        if exchange:
            @pl.when(last(*pre))
            def _():
                exchange.finish(xin, xout, sem_refs)

    any_spec = pl.BlockSpec(memory_space=pl.ANY)
    aliases = {n_pre + n_in + i: n_out + o for i, o in exchange.aliases.items()} if exchange else {}
    operands = [_in_hbm(a) for a in operands]
    results = pl.pallas_call(
        full_body, name=name, out_shape=[_out_hbm(s) for s in list(out_shape) + ex_out],
        grid_spec=pltpu.PrefetchScalarGridSpec(
            num_scalar_prefetch=n_pre, grid=grid, in_specs=list(in_specs) + [any_spec] * len(ex_in),
            out_specs=list(out_specs) + [any_spec] * len(ex_out), scratch_shapes=list(scratch_shapes) + ex_sems),
        input_output_aliases=aliases, compiler_params=params,
    )(*prefetch, *operands, *[_in_hbm(a) for a in ex_in])
    return results[:n_out], results[n_out:]


def _remote(src, dst, sems, idx, to):
    send_sems, recv_sems = sems
    return pltpu.make_async_remote_copy(src_ref=src, dst_ref=dst, send_sem=send_sems.at[idx], recv_sem=recv_sems.at[idx],
                                        device_id=to, device_id_type=MESH_IDS)


def _sem_pairs(*shape):
    return [pltpu.SemaphoreType.DMA(shape), pltpu.SemaphoreType.DMA(shape)]


def _same_shapes(arrays):
    return [jax.ShapeDtypeStruct(a.shape, a.dtype) for a in arrays]


def _gather_over_ici(slots):
    n = len(slots)

    def copies(ins, outs, sems):
        x, y, c = _mesh_pos()
        k = 2 * x + y
        out = []
        for j, (chip, kj) in enumerate(_other_chips(x, y)):
            for i in range(n):
                to = (*chip, c)
                out.append((_remote(ins[i].at[k, c], outs[i].at[k, c], sems, (j, i), to),
                            _remote(ins[i].at[k, c], outs[i].at[kj, c], sems, (j, i), to)))
        return out

    return _from_copies(slots, _same_shapes(slots), {i: i for i in range(n)}, _sem_pairs(3, n), copies)


def _gather_over_d2d(slots):
    n = len(slots)

    def copies(ins, outs, sems):
        x, y, c = _mesh_pos()
        sibling = (x, y, 1 - c)
        out = []
        for j, (_, kj) in enumerate(_other_chips(x, y)):
            for i in range(n):
                out.append((_remote(ins[i].at[kj, c], outs[i].at[kj, c], sems, (j, i), sibling),
                            _remote(ins[i].at[kj, c], outs[i].at[kj, 1 - c], sems, (j, i), sibling)))
        return out

    return _from_copies(slots, _same_shapes(slots), {i: i for i in range(n)}, _sem_pairs(3, n), copies)


def _pair_exchange(grads):
    n = len(grads)

    def copies(ins, outs, sems):
        x, y, c = _mesh_pos()
        cps = [_remote(ins[i].at[:, 1 - c], outs[i], sems, i, (x, y, 1 - c)) for i in range(n)]
        return [(cp, cp) for cp in cps]

    shapes = [jax.ShapeDtypeStruct((N_CHIPS,) + g.shape[2:], g.dtype) for g in grads]
    return _from_copies(grads, shapes, {}, _sem_pairs(n), copies)


def _chip_exchange(partials):
    n = len(partials)

    def copies(ins, outs, sems):
        x, y, c = _mesh_pos()
        cps = [_remote(ins[i].at[kj], outs[i].at[j], sems, (j, i), (*chip, c))
               for j, (chip, kj) in enumerate(_other_chips(x, y)) for i in range(n)]
        return [(cp, cp) for cp in cps]

    shapes = [jax.ShapeDtypeStruct((3,) + p.shape[1:], p.dtype) for p in partials]
    return _from_copies(partials, shapes, {}, _sem_pairs(3, n), copies)


def _pair_send(halves):
    n = len(halves)

    def copies(ins, outs, sems):
        x, y, c = _mesh_pos()
        cps = [_remote(ins[i], outs[i], sems, i, (x, y, 1 - c)) for i in range(n)]
        return [(cp, cp) for cp in cps]

    return _from_copies(halves, _same_shapes(halves), {}, _sem_pairs(n), copies)


def _row_tile(rows):
    for t in (256, 128, 64, 32, 16, 8):
        if rows % t == 0:
            return t
    return rows


def _add_pair(core, grad, landed):
    _, h, cols = landed.shape
    tr = _row_tile(h)

    def body(core_ref, g_ref, l_ref, o_ref, ob_ref):
        s = g_ref[...] + l_ref[...]
        o_ref[...] = s
        ob_ref[...] = s.astype(BF16)

    out_spec = pl.BlockSpec((None, tr, cols), lambda k, t, core_ref: (k, t, 0))
    return _pcall(
        body, name="grad_add_pair",
        out_shape=[jax.ShapeDtypeStruct(landed.shape, F32), jax.ShapeDtypeStruct(landed.shape, BF16)],
        grid_spec=pltpu.PrefetchScalarGridSpec(
            num_scalar_prefetch=1, grid=(N_CHIPS, h // tr),
            in_specs=[pl.BlockSpec((None, None, tr, cols), lambda k, t, core_ref: (k, core_ref[0], t, 0)),
                      pl.BlockSpec((None, tr, cols), lambda k, t, core_ref: (k, t, 0))],
            out_specs=[out_spec, out_spec]),
        compiler_params=_params(32, ("arbitrary", "arbitrary")),
        operands=(core, grad, landed))


def _add_chips(chip, partial, landed):
    _, h, cols = partial.shape
    tr = _row_tile(h)

    def body(chip_ref, p_ref, l_ref, o_ref):
        o_ref[...] = ((p_ref[...] + l_ref[0].astype(F32)) + l_ref[1].astype(F32)) + l_ref[2].astype(F32)

    return _pcall(
        body, name="grad_add_chips",
        out_shape=jax.ShapeDtypeStruct((h, cols), F32),
        grid_spec=pltpu.PrefetchScalarGridSpec(
            num_scalar_prefetch=1, grid=(h // tr,),
            in_specs=[pl.BlockSpec((None, tr, cols), lambda t, chip_ref: (chip_ref[0], t, 0)),
                      pl.BlockSpec((3, tr, cols), lambda t, chip_ref: (0, t, 0))],
            out_specs=pl.BlockSpec((tr, cols), lambda t, chip_ref: (t, 0))),
        compiler_params=_params(32, ("arbitrary",)),
        operands=(chip, partial, landed))


def _ada_project(c_all, w_ada, b_shard):
    n = w_ada.shape[1]
    tn = 512

    def body(c_ref, w_ref, b_ref, mod_ref, cond_ref):
        cv = c_ref[...]
        cond = cv * _sigmoid(cv)
        mod_ref[...] = _dot(cond.astype(BF16), w_ref[...].astype(BF16)) + b_ref[...]
        cond_ref[0:N_DEV, :] = cond
        cond_ref[N_DEV:2 * N_DEV, :] = jnp.zeros_like(cond)

    return _pcall(
        body, name="ada_project", grid=(n // tn,),
        out_shape=[jax.ShapeDtypeStruct((N_DEV, n), F32), jax.ShapeDtypeStruct((2 * N_DEV, D_MODEL), F32)],
        in_specs=[pl.BlockSpec((N_DEV, D_MODEL), lambda j: (0, 0)), pl.BlockSpec((D_MODEL, tn), lambda j: (0, j)),
                  pl.BlockSpec((1, tn), lambda j: (0, j))],
        out_specs=[pl.BlockSpec((N_DEV, tn), lambda j: (0, j)), pl.BlockSpec((2 * N_DEV, D_MODEL), lambda j: (0, 0))],
        compiler_params=_params(32, ("arbitrary",)),
        operands=(c_all, w_ada, b_shard))


def _in_project(x, pos, sc_a, sh_a, w_in, q_norm_w, kv_norm_w, w_q, w_kv, invf):
    t_len = x.shape[0]
    tm = min(256, t_len)

    def body(x_ref, pos_ref, sc_ref, sh_ref, win_ref, qn_ref, kvn_ref, wq_ref, wkv_ref, invf_ref,
             u_ref, zhg_ref, cq_ref, ckv_ref, q_ref, k_ref, kt_ref, v_ref, vt_ref):
        u = (x_ref[...] * (1.0 + sc_ref[...]) + sh_ref[...]).astype(BF16)
        u_ref[...] = u
        z = _dot(u, win_ref[...])
        zhg_ref[...] = z[:, :HG_COLS]
        cq = z[:, HG_COLS:HG_COLS + Q_RANK]
        ckv = z[:, HG_COLS + Q_RANK:HG_COLS + Q_RANK + KV_RANK]
        cq_ref[...] = cq
        ckv_ref[...] = ckv
        cos_t, sin_t = _rope_tables(pos_ref[...], invf_ref[...])
        k_pe = _rope(z[:, HG_COLS + Q_RANK + KV_RANK:], cos_t, sin_t)
        k_pe_t = jnp.transpose(k_pe).astype(BF16)
        cqn = (cq * lax.rsqrt(_rowmean(cq * cq) + RMS_EPS) * qn_ref[...]).astype(BF16)
        ckvn = (ckv * lax.rsqrt(_rowmean(ckv * ckv) + RMS_EPS) * kvn_ref[...]).astype(BF16)
        for h in range(N_HEADS):
            qh = _dot(cqn, wq_ref[h])
            q_ref[h, :, 0:HEAD_DIM] = qh[:, :HEAD_DIM].astype(BF16)
            q_ref[h, :, HEAD_DIM:QK_DIM] = _rope(qh[:, HEAD_DIM:], cos_t, sin_t).astype(BF16)
            kvh = _dot(ckvn, wkv_ref[h])
            k_ref[h, :, 0:HEAD_DIM] = kvh[:, :HEAD_DIM].astype(BF16)
            k_ref[h, :, HEAD_DIM:QK_DIM] = k_pe.astype(BF16)
            kt_ref[h, 0:HEAD_DIM, :] = jnp.transpose(kvh[:, :HEAD_DIM]).astype(BF16)
            kt_ref[h, HEAD_DIM:QK_DIM, :] = k_pe_t
            v_ref[h] = kvh[:, HEAD_DIM:].astype(BF16)
            vt_ref[h] = jnp.transpose(kvh[:, HEAD_DIM:]).astype(BF16)

    row = lambda i: (i, 0)
    fixed2 = lambda i: (0, 0)
    fixed3 = lambda i: (0, 0, 0)
    heads = lambda i: (0, i, 0)
    return _pcall(
        body, name="in_project", grid=(t_len // tm,),
        out_shape=[jax.ShapeDtypeStruct((t_len, D_MODEL), BF16), jax.ShapeDtypeStruct((t_len, HG_COLS), F32),
                   jax.ShapeDtypeStruct((t_len, Q_RANK), F32), jax.ShapeDtypeStruct((t_len, KV_RANK), F32),
                   jax.ShapeDtypeStruct((N_HEADS, t_len, QK_DIM), BF16),
                   jax.ShapeDtypeStruct((N_HEADS, t_len, QK_DIM), BF16),
                   jax.ShapeDtypeStruct((N_HEADS, QK_DIM, t_len), BF16),
                   jax.ShapeDtypeStruct((N_HEADS, t_len, HEAD_DIM), BF16),
                   jax.ShapeDtypeStruct((N_HEADS, HEAD_DIM, t_len), BF16)],
        in_specs=[pl.BlockSpec((tm, D_MODEL), row), pl.BlockSpec((tm, 1), row),
                  pl.BlockSpec((1, D_MODEL), fixed2), pl.BlockSpec((1, D_MODEL), fixed2),
                  pl.BlockSpec((D_MODEL, IN_COLS_PAD), fixed2),
                  pl.BlockSpec((1, Q_RANK), fixed2), pl.BlockSpec((1, KV_RANK), fixed2),
                  pl.BlockSpec((N_HEADS, Q_RANK, QK_DIM), fixed3), pl.BlockSpec((N_HEADS, KV_RANK, 2 * HEAD_DIM), fixed3),
                  pl.BlockSpec((1, 128), fixed2)],
        out_specs=[pl.BlockSpec((tm, D_MODEL), row), pl.BlockSpec((tm, HG_COLS), row),
                   pl.BlockSpec((tm, Q_RANK), row), pl.BlockSpec((tm, KV_RANK), row),
                   pl.BlockSpec((N_HEADS, tm, QK_DIM), heads), pl.BlockSpec((N_HEADS, tm, QK_DIM), heads),
                   pl.BlockSpec((N_HEADS, QK_DIM, tm), lambda i: (0, 0, i)),
                   pl.BlockSpec((N_HEADS, tm, HEAD_DIM), heads),
                   pl.BlockSpec((N_HEADS, HEAD_DIM, tm), lambda i: (0, 0, i))],
        compiler_params=_params(48, ("arbitrary",)),
        operands=(x, pos, sc_a, sh_a, w_in, q_norm_w, kv_norm_w, w_q, w_kv, invf))


def _lower_bound(lb_raw):
    m = jnp.max(lb_raw, axis=0, keepdims=True)
    e = jnp.exp(lb_raw - m)
    return e[0:1] / jnp.sum(e, axis=0, keepdims=True)


def _tri(inclusive_lower):
    r = lax.broadcasted_iota(jnp.int32, (HG_CHUNK, HG_CHUNK), 0)
    c = lax.broadcasted_iota(jnp.int32, (HG_CHUNK, HG_CHUNK), 1)
    return (c <= r) if inclusive_lower else (c >= r)


def _hg_chunk(q, f_logit, lb):
    sg = _sigmoid(f_logit)
    forget = lb + (1.0 - lb) * sg
    kk = 1.0 - forget
    b = _dot_f32(_tri(True).astype(F32), jnp.log(forget))
    b_ref = b[HG_CHUNK // 2 - 1:HG_CHUNK // 2]
    b_last = b[HG_CHUNK - 1:HG_CHUNK]
    e_i = jnp.exp(b - b_ref)
    e_ri = jnp.exp(b_ref - b)
    e_b = jnp.exp(b)
    e_l = jnp.exp(b_last - b)
    return dict(sg=sg, forget=forget, e_i=e_i, e_ri=e_ri, e_b=e_b, e_l=e_l, dec=jnp.exp(b_last),
                qi=q * e_i, ki=kk * e_ri, qe=q * e_b, kl=kk * e_l)


def _hgrn_forward(zhg, lb_raw, norm_w, exchange=None):
    t_len = zhg.shape[0]
    tb = min(512, t_len)
    n_chunks = tb // HG_CHUNK

    def body(q_ref, f_ref, v_ref, g_ref, lb_ref, w_ref, opre_ref, o_ref, st_ref, state):
        @pl.when(pl.program_id(1) == 0)
        def _():
            state[...] = jnp.zeros_like(state)

        lb = _lower_bound(lb_ref[...])
        causal = _tri(True)
        for n in range(n_chunks):
            rows = pl.ds(n * HG_CHUNK, HG_CHUNK)
            v = v_ref[rows, :].astype(BF16)
            ch = _hg_chunk(q_ref[rows, :], f_ref[rows, :], lb)
            a = jnp.where(causal, _dot_nt(ch["qi"].astype(BF16), ch["ki"].astype(BF16)), 0.0)
            st = state[...]
            st_ref[0, n] = st
            o = _dot(a.astype(BF16), v) + _dot_nt(ch["qe"].astype(BF16), st.astype(BF16))
            state[...] = st * ch["dec"] + _dot_tn(v, ch["kl"].astype(BF16))
            opre_ref[rows, :] = o
            on = o * lax.rsqrt(_rowmean(o * o) + RMS_EPS) * w_ref[...]
            g = g_ref[rows, :]
            o_ref[rows, :] = on * (g * _sigmoid(g))

    col = lambda off: (lambda h, t: (t, off + h))
    nb = t_len // tb
    return _pallas(
        body, name="hgrn_forward", grid=(N_HEADS, nb), operands=(zhg, zhg, zhg, zhg, lb_raw, norm_w),
        out_shape=[jax.ShapeDtypeStruct((t_len, N_HEADS * HEAD_DIM), F32),
                   jax.ShapeDtypeStruct((t_len, N_HEADS * HEAD_DIM), F32),
                   jax.ShapeDtypeStruct((N_HEADS, t_len // HG_CHUNK, HEAD_DIM, HEAD_DIM), F32)],
        in_specs=[pl.BlockSpec((tb, HEAD_DIM), col(0)), pl.BlockSpec((tb, HEAD_DIM), col(N_HEADS)),
                  pl.BlockSpec((tb, HEAD_DIM), col(2 * N_HEADS)), pl.BlockSpec((tb, HEAD_DIM), col(3 * N_HEADS)),
                  pl.BlockSpec((2, HEAD_DIM), lambda h, t: (0, h)), pl.BlockSpec((1, HEAD_DIM), lambda h, t: (0, h))],
        out_specs=[pl.BlockSpec((tb, HEAD_DIM), col(0)), pl.BlockSpec((tb, HEAD_DIM), col(0)),
                   pl.BlockSpec((1, n_chunks, HEAD_DIM, HEAD_DIM), lambda h, t: (h, t, 0, 0))],
        scratch_shapes=[pltpu.VMEM((HEAD_DIM, HEAD_DIM), F32)],
        params=_params(32, ("arbitrary", "arbitrary")), exchange=exchange,
        first=lambda: (pl.program_id(0) == 0) & (pl.program_id(1) == 0),
        last=lambda: (pl.program_id(0) == N_HEADS - 1) & (pl.program_id(1) == nb - 1))


def _hgrn_backward(zhg, lb_raw, norm_w, o_pre, d_cat, states, exchange=None):
    t_len = zhg.shape[0]
    tb = min(512, t_len)
    n_chunks = tb // HG_CHUNK
    nb = t_len // tb

    def body(q_ref, f_ref, v_ref, g_ref, lb_ref, w_ref, opre_ref, do_ref, st_ref,
             dq_ref, df_ref, dv_ref, dg_ref, sums_ref, gstate):
        @pl.when(pl.program_id(1) == 0)
        def _():
            gstate[...] = jnp.zeros_like(gstate)
            sums_ref[...] = jnp.zeros_like(sums_ref)

        lb = _lower_bound(lb_ref[...])
        w = w_ref[...]
        causal = _tri(True)
        upper = _tri(False).astype(F32)
        row_id = lax.broadcasted_iota(jnp.int32, (HG_CHUNK, HEAD_DIM), 0)
        d_lb = jnp.zeros((1, HEAD_DIM), F32)
        d_w = jnp.zeros((1, HEAD_DIM), F32)
        for n in reversed(range(n_chunks)):
            rows = pl.ds(n * HG_CHUNK, HG_CHUNK)
            o = opre_ref[rows, :]
            g = g_ref[rows, :]
            d_out = do_ref[rows, :]
            r = lax.rsqrt(_rowmean(o * o) + RMS_EPS)
            sg_g = _sigmoid(g)
            silu = g * sg_g
            dg_ref[rows, :] = (d_out * (o * r * w) * (sg_g * (1.0 + g * (1.0 - sg_g)))).astype(BF16)
            d_on = d_out * silu
            d_w = d_w + _colsum(d_on * o * r)
            dy = d_on * w
            d_o = (r * dy - o * (r * r * r) * _rowmean(dy * o)).astype(BF16)
            vf = v_ref[rows, :]
            v = vf.astype(BF16)
            ch = _hg_chunk(q_ref[rows, :], f_ref[rows, :], lb)
            qi, ki, qe, kl = (ch[name].astype(BF16) for name in ("qi", "ki", "qe", "kl"))
            st = st_ref[0, n]
            gt = gstate[...]
            a = jnp.where(causal, _dot_nt(qi, ki), 0.0).astype(BF16)
            d_a = jnp.where(causal, _dot_nt(d_o, v), 0.0).astype(BF16)
            gt_b = gt.astype(BF16)
            d_v = _dot_tn(a, d_o) + _dot_nt(kl, gt_b)
            d_qi = _dot(d_a, ki)
            d_ki = _dot_tn(d_a, qi)
            d_qe = _dot(d_o, st.astype(BF16))
            d_kl = _dot(v, gt_b)
            d_dec = _colsum(gt * st)
            gstate[...] = gt * ch["dec"] + _dot_tn(d_o, qe)
            dq_ref[rows, :] = (d_qi * ch["e_i"] + d_qe * ch["e_b"]).astype(BF16)
            d_k = d_ki * ch["e_ri"] + d_kl * ch["e_l"]
            t_qi = d_qi * ch["qi"]
            t_ki = d_ki * ch["ki"]
            t_kl = d_kl * ch["kl"]
            d_b = t_qi - t_ki + d_qe * ch["qe"] - t_kl
            d_b = d_b + jnp.where(row_id == HG_CHUNK // 2 - 1, _colsum(t_ki - t_qi), 0.0)
            d_b = d_b + jnp.where(row_id == HG_CHUNK - 1, _colsum(t_kl) + d_dec * ch["dec"], 0.0)
            d_forget = _dot_f32(upper, d_b) / ch["forget"] - d_k
            sg = ch["sg"]
            df_ref[rows, :] = (d_forget * (1.0 - lb) * sg * (1.0 - sg)).astype(BF16)
            d_lb = d_lb + _colsum(d_forget * (1.0 - sg))
            dv_ref[rows, :] = d_v.astype(BF16)
        sums_ref[0:1, :] += d_lb
        sums_ref[1:2, :] += d_w

    col = lambda off: (lambda h, t: (nb - 1 - t, off + h))
    return _pallas(
        body, name="hgrn_backward", grid=(N_HEADS, nb),
        operands=(zhg, zhg, zhg, zhg, lb_raw, norm_w, o_pre, d_cat, states),
        out_shape=[jax.ShapeDtypeStruct((t_len, N_HEADS * HEAD_DIM), BF16)] * 4
        + [jax.ShapeDtypeStruct((8, N_HEADS * HEAD_DIM), F32)],
        in_specs=[pl.BlockSpec((tb, HEAD_DIM), col(0)), pl.BlockSpec((tb, HEAD_DIM), col(N_HEADS)),
                  pl.BlockSpec((tb, HEAD_DIM), col(2 * N_HEADS)), pl.BlockSpec((tb, HEAD_DIM), col(3 * N_HEADS)),
                  pl.BlockSpec((2, HEAD_DIM), lambda h, t: (0, h)), pl.BlockSpec((1, HEAD_DIM), lambda h, t: (0, h)),
                  pl.BlockSpec((tb, HEAD_DIM), col(0)), pl.BlockSpec((tb, HEAD_DIM), col(0)),
                  pl.BlockSpec((1, n_chunks, HEAD_DIM, HEAD_DIM), lambda h, t: (h, nb - 1 - t, 0, 0))],
        out_specs=[pl.BlockSpec((tb, HEAD_DIM), col(0))] * 4 + [pl.BlockSpec((8, HEAD_DIM), lambda h, t: (0, h))],
        scratch_shapes=[pltpu.VMEM((HEAD_DIM, HEAD_DIM), F32)],
        params=_params(32, ("arbitrary", "arbitrary")), exchange=exchange,
        first=lambda: (pl.program_id(0) == 0) & (pl.program_id(1) == 0),
        last=lambda: (pl.program_id(0) == N_HEADS - 1) & (pl.program_id(1) == nb - 1))


ATT_LOG2 = ATT_SCALE * 1.4426950408889634


def _triangle_steps(nq, q_major):
    if q_major:
        pairs = [(i, j) for i in range(nq) for j in range(i + 1)]
    else:
        pairs = [(i, j) for j in range(nq) for i in range(j, nq)]
    return jnp.array([p[0] for p in pairs], jnp.int32), jnp.array([p[1] for p in pairs], jnp.int32)


def _key_le_query(t):
    return lax.broadcasted_iota(jnp.int32, (t, t), 0) <= lax.broadcasted_iota(jnp.int32, (t, t), 1)


def _attention_forward(q, k, v_t, exchange=None):
    t_len = q.shape[1]
    tq = min(512, t_len)
    nq = t_len // tq
    qi_tab, ki_tab = _triangle_steps(nq, True)

    def body(qi_ref, ki_ref, q_ref, k_ref, vt_ref, o_ref, lse_ref, m_s, l_s, acc_s):
        step = pl.program_id(0)
        qi, ki = qi_ref[step], ki_ref[step]

        @pl.when(ki == 0)
        def _():
            m_s[...] = jnp.full_like(m_s, NEG_BIG)
            l_s[...] = jnp.zeros_like(l_s)
            acc_s[...] = jnp.zeros_like(acc_s)

        def accumulate(masked):
            for h in range(N_HEADS):
                s_t = _dot_nt(k_ref[h], q_ref[h]) * ATT_LOG2
                if masked:
                    s_t = jnp.where(_key_le_query(tq), s_t, NEG_BIG)
                m_old = m_s[h]
                m_new = jnp.maximum(m_old, jnp.max(s_t, axis=0, keepdims=True))
                alpha = jnp.exp2(m_old - m_new)
                p_t = jnp.exp2(s_t - m_new)
                l_s[h] = alpha * l_s[h] + jnp.sum(p_t, axis=0, keepdims=True)
                acc_s[h] = alpha * acc_s[h] + _dot(vt_ref[h], p_t.astype(BF16))
                m_s[h] = m_new

        @pl.when(ki < qi)
        def _():
            accumulate(False)

        @pl.when(ki == qi)
        def _():
            accumulate(True)
            for h in range(N_HEADS):
                o_ref[:, h * HEAD_DIM:(h + 1) * HEAD_DIM] = jnp.transpose(acc_s[h] / l_s[h])
                lse_ref[h] = m_s[h] + jnp.log2(l_s[h])

    n_steps = qi_tab.shape[0]
    return _pallas(
        body, name="attention_forward", grid=(n_steps,), prefetch=(qi_tab, ki_tab), operands=(q, k, v_t),
        out_shape=[jax.ShapeDtypeStruct((t_len, N_HEADS * HEAD_DIM), F32),
                   jax.ShapeDtypeStruct((N_HEADS, 1, t_len), F32)],
        in_specs=[pl.BlockSpec((N_HEADS, tq, QK_DIM), lambda s, qt, kt: (0, qt[s], 0)),
                  pl.BlockSpec((N_HEADS, tq, QK_DIM), lambda s, qt, kt: (0, kt[s], 0)),
                  pl.BlockSpec((N_HEADS, HEAD_DIM, tq), lambda s, qt, kt: (0, 0, kt[s]))],
        out_specs=[pl.BlockSpec((tq, N_HEADS * HEAD_DIM), lambda s, qt, kt: (qt[s], 0)),
                   pl.BlockSpec((N_HEADS, 1, tq), lambda s, qt, kt: (0, 0, qt[s]))],
        scratch_shapes=[pltpu.VMEM((N_HEADS, 1, tq), F32), pltpu.VMEM((N_HEADS, 1, tq), F32),
                        pltpu.VMEM((N_HEADS, HEAD_DIM, tq), F32)],
        params=_params(48, ("arbitrary",)), exchange=exchange,
        first=lambda qt, kt: pl.program_id(0) == 0, last=lambda qt, kt: pl.program_id(0) == n_steps - 1)


BWD_HEADS = 2


def _attention_backward(q, k, k_t, v, o, d_cat, lse, exchange=None):
    t_len = q.shape[1]
    tq = min(512, t_len)
    nq = t_len // tq
    hp = BWD_HEADS
    qi_tab, ki_tab = _triangle_steps(nq, False)

    def body(qi_ref, ki_ref, q_ref, k_ref, kt_ref, v_ref, o_ref, do_ref, lse_ref, dqt_hbm, dk_ref, dv_ref,
             dqt_s, dk_s, dv_s):
        group, step = pl.program_id(0), pl.program_id(1)
        qi, ki = qi_ref[step], ki_ref[step]

        @pl.when(step == 0)
        def _():
            dqt_s[...] = jnp.zeros_like(dqt_s)

        @pl.when(qi == ki)
        def _():
            dk_s[...] = jnp.zeros_like(dk_s)
            dv_s[...] = jnp.zeros_like(dv_s)

        def accumulate(masked):
            ones = jnp.ones((8, HEAD_DIM), F32)
            for h in range(hp):
                cols = slice(h * HEAD_DIM, (h + 1) * HEAD_DIM)
                do = do_ref[:, cols]
                delta = lax.dot_general(ones, do * o_ref[:, cols], (((1,), (1,)), ((), ())),
                                        preferred_element_type=F32, precision=lax.Precision.HIGHEST)[0:1]
                s_t = _dot_nt(k_ref[h], q_ref[h]) * ATT_LOG2
                if masked:
                    s_t = jnp.where(_key_le_query(tq), s_t, NEG_BIG)
                p_t = jnp.exp2(s_t - lse_ref[h])
                do_b = do.astype(BF16)
                dp_t = _dot_nt(v_ref[h], do_b)
                ds_t = (p_t * (dp_t - delta) * ATT_SCALE).astype(BF16)
                dv_s[h] += _dot(p_t.astype(BF16), do_b)
                dk_s[h] += _dot(ds_t, q_ref[h])
                dqt_s[h, qi] += _dot(kt_ref[h], ds_t)

        @pl.when(ki < qi)
        def _():
            accumulate(False)

        @pl.when(ki == qi)
        def _():
            accumulate(True)
            for h in range(hp):
                pltpu.sync_copy(dqt_s.at[h, qi], dqt_hbm.at[group * hp + h, qi])

        @pl.when(qi == nq - 1)
        def _():
            dk_ref[...] = dk_s[...]
            dv_ref[...] = dv_s[...]

    wide = hp * HEAD_DIM
    n_groups, n_steps = N_HEADS // hp, qi_tab.shape[0]
    return _pallas(
        body, name="attention_backward", grid=(n_groups, n_steps), prefetch=(qi_tab, ki_tab),
        operands=(q, k, k_t, v, o, d_cat, lse),
        out_shape=[jax.ShapeDtypeStruct((N_HEADS, nq, QK_DIM, tq), F32),
                   jax.ShapeDtypeStruct((N_HEADS, t_len, QK_DIM), F32),
                   jax.ShapeDtypeStruct((N_HEADS, t_len, HEAD_DIM), F32)],
        in_specs=[pl.BlockSpec((hp, tq, QK_DIM), lambda g, s, qt, kt: (g, qt[s], 0)),
                  pl.BlockSpec((hp, tq, QK_DIM), lambda g, s, qt, kt: (g, kt[s], 0)),
                  pl.BlockSpec((hp, QK_DIM, tq), lambda g, s, qt, kt: (g, 0, kt[s])),
                  pl.BlockSpec((hp, tq, HEAD_DIM), lambda g, s, qt, kt: (g, kt[s], 0)),
                  pl.BlockSpec((tq, wide), lambda g, s, qt, kt: (qt[s], g)),
                  pl.BlockSpec((tq, wide), lambda g, s, qt, kt: (qt[s], n_groups + g)),
                  pl.BlockSpec((hp, 1, tq), lambda g, s, qt, kt: (g, 0, qt[s]))],
        out_specs=[pl.BlockSpec(memory_space=pl.ANY),
                   pl.BlockSpec((hp, tq, QK_DIM), lambda g, s, qt, kt: (g, kt[s], 0)),
                   pl.BlockSpec((hp, tq, HEAD_DIM), lambda g, s, qt, kt: (g, kt[s], 0))],
        scratch_shapes=[pltpu.VMEM((hp, nq, QK_DIM, tq), F32), pltpu.VMEM((hp, tq, QK_DIM), F32),
                        pltpu.VMEM((hp, tq, HEAD_DIM), F32)],
        params=_params(48, ("arbitrary", "arbitrary")), exchange=exchange,
        first=lambda qt, kt: (pl.program_id(0) == 0) & (pl.program_id(1) == 0),
        last=lambda qt, kt: (pl.program_id(0) == n_groups - 1) & (pl.program_id(1) == n_steps - 1))


def _out_project(o_hg, o_mla, x, g_a, w_out):
    t_len = x.shape[0]
    tm = min(512, t_len)
    half = N_HEADS * HEAD_DIM

    def body(ohg_ref, omla_ref, x_ref, ga_ref, w_ref, cat_ref, mix_ref, xhat_ref, rstd_ref):
        a = ohg_ref[...].astype(BF16)
        b = omla_ref[...].astype(BF16)
        cat_ref[:, 0:half] = a
        cat_ref[:, half:2 * half] = b
        mix = _dot(a, w_ref[0:half, :]) + _dot(b, w_ref[half:2 * half, :])
        mix_ref[...] = mix
        r1 = DN_ALPHA * x_ref[...] + (1.0 + ga_ref[...]) * mix
        xc = r1 - _rowmean(r1)
        rstd = lax.rsqrt(_rowmean(xc * xc) + LN_EPS)
        xhat_ref[...] = xc * rstd
        rstd_ref[...] = rstd

    row = lambda i: (i, 0)
    fixed = lambda i: (0, 0)
    return _pcall(
        body, name="out_project", grid=(t_len // tm,),
        out_shape=[jax.ShapeDtypeStruct((t_len, D_MODEL), BF16), jax.ShapeDtypeStruct((t_len, D_MODEL), F32),
                   jax.ShapeDtypeStruct((t_len, D_MODEL), F32), jax.ShapeDtypeStruct((t_len, 1), F32)],
        in_specs=[pl.BlockSpec((tm, half), row), pl.BlockSpec((tm, half), row), pl.BlockSpec((tm, D_MODEL), row),
                  pl.BlockSpec((1, D_MODEL), fixed), pl.BlockSpec((D_MODEL, D_MODEL), fixed)],
        out_specs=[pl.BlockSpec((tm, D_MODEL), row), pl.BlockSpec((tm, D_MODEL), row),
                   pl.BlockSpec((tm, D_MODEL), row), pl.BlockSpec((tm, 1), row)],
        compiler_params=_params(48, ("arbitrary",)),
        operands=(o_hg, o_mla, x, g_a, w_out))


V_LN1G, V_LN1B, V_SCM, V_SHM, V_GM, V_GA, V_LN2G, V_LN2B = range(8)
S_DLN2G, S_DLN2B, S_DGM, S_DSCM, S_DSHM, S_DLN1G, S_DLN1B, S_DGA, S_LOSS = range(9)


def _mlp_and_back(xhat1, rstd1, mix, target, vecs, w1, w2, w_out):
    t_len = xhat1.shape[0]
    tm = min(256, t_len)
    n_ff = w1.shape[0]
    ff = w1.shape[2]

    def body(xhat_ref, rstd_ref, mix_ref, tgt_ref, vec_ref, w1_hbm, w2_hbm, wout_hbm,
             act_ref, dhp_ref, um_ref, dh_ref, dmix_ref, dcat_ref, dr1_ref, sums_ref,
             w1_s, w2_s, wout_s, hp_s, load_sems):
        @pl.when(pl.program_id(0) == 0)
        def _():
            loads = [pltpu.make_async_copy(w1_hbm, w1_s, load_sems.at[0]),
                     pltpu.make_async_copy(w2_hbm, w2_s, load_sems.at[1]),
                     pltpu.make_async_copy(wout_hbm, wout_s, load_sems.at[2])]
            for cp in loads:
                cp.start()
            sums_ref[...] = jnp.zeros_like(sums_ref)
            for cp in loads:
                cp.wait()

        vec = lambda r: vec_ref[r:r + 1, :]
        xhat = xhat_ref[...]
        x1 = xhat * vec(V_LN1G) + vec(V_LN1B)
        um = (x1 * (1.0 + vec(V_SCM)) + vec(V_SHM)).astype(BF16)
        um_ref[...] = um
        h = jnp.zeros((tm, D_MODEL), F32)
        for j in range(n_ff):
            hp = _dot(um, w1_s[j])
            hp_s[j] = hp
            act = jnp.square(jnp.maximum(hp, 0.0)).astype(BF16)
            act_ref[:, j * ff:(j + 1) * ff] = act
            h = h + _dot(act, w2_s[j])
        r2 = DN_ALPHA * x1 + (1.0 + vec(V_GM)) * h
        xc = r2 - _rowmean(r2)
        rstd2 = lax.rsqrt(_rowmean(xc * xc) + LN_EPS)
        xhat2 = xc * rstd2
        err = xhat2 * vec(V_LN2G) + vec(V_LN2B) - tgt_ref[...]
        loss = 0.5 * jnp.sum(_rowmean(err * err))
        dy = err * (1.0 / D_MODEL)
        dxh = dy * vec(V_LN2G)
        dr2 = rstd2 * (dxh - _rowmean(dxh) - xhat2 * _rowmean(dxh * xhat2))
        dh = ((1.0 + vec(V_GM)) * dr2).astype(BF16)
        dh_ref[...] = dh
        sums_ref[S_DLN2G:S_DLN2G + 1, :] += _colsum(dy * xhat2)
        sums_ref[S_DLN2B:S_DLN2B + 1, :] += _colsum(dy)
        sums_ref[S_DGM:S_DGM + 1, :] += _colsum(dr2 * h)
        sums_ref[S_LOSS:S_LOSS + 1, :] += jnp.full((1, D_MODEL), loss, F32)
        du = jnp.zeros((tm, D_MODEL), F32)
        for j in range(n_ff):
            dhp = (_dot_nt(dh, w2_s[j]) * (2.0 * jnp.maximum(hp_s[j], 0.0))).astype(BF16)
            dhp_ref[:, j * ff:(j + 1) * ff] = dhp
            du = du + _dot_nt(dhp, w1_s[j])
        sums_ref[S_DSCM:S_DSCM + 1, :] += _colsum(du * x1)
        sums_ref[S_DSHM:S_DSHM + 1, :] += _colsum(du)
        dx1 = DN_ALPHA * dr2 + du * (1.0 + vec(V_SCM))
        sums_ref[S_DLN1G:S_DLN1G + 1, :] += _colsum(dx1 * xhat)
        sums_ref[S_DLN1B:S_DLN1B + 1, :] += _colsum(dx1)
        dxh1 = dx1 * vec(V_LN1G)
        dr1 = rstd_ref[...] * (dxh1 - _rowmean(dxh1) - xhat * _rowmean(dxh1 * xhat))
        dr1_ref[...] = dr1
        sums_ref[S_DGA:S_DGA + 1, :] += _colsum(dr1 * mix_ref[...])
        dmix = ((1.0 + vec(V_GA)) * dr1).astype(BF16)
        dmix_ref[...] = dmix
        dcat_ref[...] = _dot_nt(dmix, wout_s[...])

    row = lambda i: (i, 0)
    fixed = lambda i: (0, 0)
    any_spec = pl.BlockSpec(memory_space=pl.ANY)
    return _pcall(
        body, name="mlp_and_back", grid=(t_len // tm,),
        out_shape=[jax.ShapeDtypeStruct((t_len, D_FF), BF16), jax.ShapeDtypeStruct((t_len, D_FF), BF16),
                   jax.ShapeDtypeStruct((t_len, D_MODEL), BF16), jax.ShapeDtypeStruct((t_len, D_MODEL), BF16),
                   jax.ShapeDtypeStruct((t_len, D_MODEL), BF16), jax.ShapeDtypeStruct((t_len, D_MODEL), F32),
                   jax.ShapeDtypeStruct((t_len, D_MODEL), F32), jax.ShapeDtypeStruct((16, D_MODEL), F32)],
        in_specs=[pl.BlockSpec((tm, D_MODEL), row), pl.BlockSpec((tm, 1), row), pl.BlockSpec((tm, D_MODEL), row),
                  pl.BlockSpec((tm, D_MODEL), row), pl.BlockSpec((8, D_MODEL), fixed), any_spec, any_spec, any_spec],
        out_specs=[pl.BlockSpec((tm, D_FF), row), pl.BlockSpec((tm, D_FF), row), pl.BlockSpec((tm, D_MODEL), row),
                   pl.BlockSpec((tm, D_MODEL), row), pl.BlockSpec((tm, D_MODEL), row), pl.BlockSpec((tm, D_MODEL), row),
                   pl.BlockSpec((tm, D_MODEL), row), pl.BlockSpec((16, D_MODEL), fixed)],
        scratch_shapes=[pltpu.VMEM(w1.shape, BF16), pltpu.VMEM(w2.shape, BF16), pltpu.VMEM(w_out.shape, BF16),
                        pltpu.VMEM((n_ff, tm, ff), F32), pltpu.SemaphoreType.DMA((3,))],
        compiler_params=_params(56, ("arbitrary",)),
        operands=(xhat1, rstd1, mix, target, vecs, w1, w2, w_out))


def _in_project_backward(dq, dk, dv, cq, ckv, pos, invf, q_norm_w, kv_norm_w, w_q, w_kv,
                         d_hq, d_hf, d_hi, d_hg, w_in, dr1, x, sc_a, exchange=None):
    t_len = x.shape[0]
    tm = min(256, t_len)
    per_q = dq.shape[3] // tm
    hgw = N_HEADS * HEAD_DIM

    def body(dq_ref, dk_ref, dv_ref, cq_ref, ckv_ref, pos_ref, invf_ref, qn_ref, kvn_ref, wq_ref, wkv_ref,
             dhq_ref, dhf_ref, dhi_ref, dhg_ref, win_ref, dr1_ref, x_ref, sc_ref,
             dz_ref, dqf_ref, dkvu_ref, cqn_ref, ckvn_ref, gx_ref, sums_ref):
        @pl.when(pl.program_id(0) == 0)
        def _():
            sums_ref[...] = jnp.zeros_like(sums_ref)

        cos_t, sin_t = _rope_tables(pos_ref[...], invf_ref[...])
        cq = cq_ref[...]
        ckv = ckv_ref[...]
        rq = lax.rsqrt(_rowmean(cq * cq) + RMS_EPS)
        rkv = lax.rsqrt(_rowmean(ckv * ckv) + RMS_EPS)
        cqn_ref[...] = (cq * rq * qn_ref[...]).astype(BF16)
        ckvn_ref[...] = (ckv * rkv * kvn_ref[...]).astype(BF16)
        d_cqn = jnp.zeros((tm, Q_RANK), F32)
        d_ckvn = jnp.zeros((tm, KV_RANK), F32)
        d_kpe = jnp.zeros((tm, 128), F32)
        for h in range(N_HEADS):
            dqh = jnp.transpose(dq_ref[h])
            dqf_ref[h, :, 0:HEAD_DIM] = dqh[:, :HEAD_DIM].astype(BF16)
            dqf_ref[h, :, HEAD_DIM:QK_DIM] = _unrope(dqh[:, HEAD_DIM:], cos_t, sin_t).astype(BF16)
            d_cqn = d_cqn + _dot_nt(dqf_ref[h], wq_ref[h])
            dkh = dk_ref[h]
            d_kpe = d_kpe + dkh[:, HEAD_DIM:]
            dkvu_ref[h, :, 0:HEAD_DIM] = dkh[:, :HEAD_DIM].astype(BF16)
            dkvu_ref[h, :, HEAD_DIM:2 * HEAD_DIM] = dv_ref[h].astype(BF16)
            d_ckvn = d_ckvn + _dot_nt(dkvu_ref[h], wkv_ref[h])
        dyq = d_cqn * qn_ref[...]
        dykv = d_ckvn * kvn_ref[...]
        sums_ref[2:3, 0:Q_RANK] += _colsum(d_cqn * cq * rq)
        sums_ref[3:4, 0:KV_RANK] += _colsum(d_ckvn * ckv * rkv)
        dz_ref[:, 0:hgw] = dhq_ref[...]
        dz_ref[:, hgw:2 * hgw] = dhf_ref[...]
        dz_ref[:, 2 * hgw:3 * hgw] = dhi_ref[...]
        dz_ref[:, 3 * hgw:4 * hgw] = dhg_ref[...]
        dz_ref[:, HG_COLS:HG_COLS + Q_RANK] = (rq * dyq - cq * (rq * rq * rq) * _rowmean(dyq * cq)).astype(BF16)
        dz_ref[:, HG_COLS + Q_RANK:HG_COLS + Q_RANK + KV_RANK] = (
            rkv * dykv - ckv * (rkv * rkv * rkv) * _rowmean(dykv * ckv)).astype(BF16)
        dz_ref[:, HG_COLS + Q_RANK + KV_RANK:] = _unrope(d_kpe, cos_t, sin_t).astype(BF16)
        du = _dot_nt(dz_ref[...], win_ref[...])
        xv = x_ref[...]
        gx_ref[...] = DN_ALPHA * dr1_ref[...] + (1.0 + sc_ref[...]) * du
        sums_ref[0:1, :] += _colsum(du * xv)
        sums_ref[1:2, :] += _colsum(du)

    row = lambda i: (i, 0)
    fixed2 = lambda i: (0, 0)
    fixed3 = lambda i: (0, 0, 0)
    heads = lambda i: (0, i, 0)
    n_tiles = t_len // tm
    return _pallas(
        body, name="in_project_backward", grid=(n_tiles,),
        operands=(dq, dk, dv, cq, ckv, pos, invf, q_norm_w, kv_norm_w, w_q, w_kv, d_hq, d_hf, d_hi, d_hg, w_in, dr1, x,
                  sc_a),
        out_shape=[jax.ShapeDtypeStruct((t_len, IN_COLS_PAD), BF16), jax.ShapeDtypeStruct((N_HEADS, t_len, QK_DIM), BF16),
                   jax.ShapeDtypeStruct((N_HEADS, t_len, 2 * HEAD_DIM), BF16), jax.ShapeDtypeStruct((t_len, Q_RANK), BF16),
                   jax.ShapeDtypeStruct((t_len, KV_RANK), BF16), jax.ShapeDtypeStruct((t_len, D_MODEL), F32),
                   jax.ShapeDtypeStruct((8, D_MODEL), F32)],
        in_specs=[pl.BlockSpec((N_HEADS, None, QK_DIM, tm), lambda i: (0, i // per_q, 0, i % per_q)),
                  pl.BlockSpec((N_HEADS, tm, QK_DIM), heads),
                  pl.BlockSpec((N_HEADS, tm, HEAD_DIM), heads), pl.BlockSpec((tm, Q_RANK), row),
                  pl.BlockSpec((tm, KV_RANK), row), pl.BlockSpec((tm, 1), row), pl.BlockSpec((1, 128), fixed2),
                  pl.BlockSpec((1, Q_RANK), fixed2), pl.BlockSpec((1, KV_RANK), fixed2),
                  pl.BlockSpec((N_HEADS, Q_RANK, QK_DIM), fixed3), pl.BlockSpec((N_HEADS, KV_RANK, 2 * HEAD_DIM), fixed3),
                  pl.BlockSpec((tm, hgw), row), pl.BlockSpec((tm, hgw), row), pl.BlockSpec((tm, hgw), row),
                  pl.BlockSpec((tm, hgw), row), pl.BlockSpec((D_MODEL, IN_COLS_PAD), fixed2),
                  pl.BlockSpec((tm, D_MODEL), row), pl.BlockSpec((tm, D_MODEL), row), pl.BlockSpec((1, D_MODEL), fixed2)],
        out_specs=[pl.BlockSpec((tm, IN_COLS_PAD), row), pl.BlockSpec((N_HEADS, tm, QK_DIM), heads),
                   pl.BlockSpec((N_HEADS, tm, 2 * HEAD_DIM), heads), pl.BlockSpec((tm, Q_RANK), row),
                   pl.BlockSpec((tm, KV_RANK), row), pl.BlockSpec((tm, D_MODEL), row), pl.BlockSpec((8, D_MODEL), fixed2)],
        params=_params(48, ("arbitrary",)), exchange=exchange,
        first=lambda: pl.program_id(0) == 0, last=lambda: pl.program_id(0) == n_tiles - 1)


def _weight_grad(a, b, name, n_blocks, bn, a_blocked=False, b_blocked=True, exchange=None):
    t_len = a.shape[0]
    m = a.shape[1] // n_blocks if a_blocked else a.shape[1]
    bt = min(512, t_len)

    def body(a_ref, b_ref, o_ref):
        @pl.when(pl.program_id(1) == 0)
        def _():
            o_ref[...] = jnp.zeros_like(o_ref)

        o_ref[...] += _dot_tn(a_ref[...].astype(BF16), b_ref[...].astype(BF16))

    a_spec = pl.BlockSpec((bt, m), (lambda n, t: (t, n)) if a_blocked else (lambda n, t: (t, 0)))
    if b.ndim == 3:
        b_spec = pl.BlockSpec((None, bt, bn), lambda n, t: (n, t, 0))
    else:
        b_spec = pl.BlockSpec((bt, bn), (lambda n, t: (t, n)) if b_blocked else (lambda n, t: (t, 0)))
    nt = t_len // bt
    (out,), landed = _pallas(
        body, name=name, grid=(n_blocks, nt), operands=(a, b),
        out_shape=[jax.ShapeDtypeStruct((n_blocks, m, bn), F32)],
        in_specs=[a_spec, b_spec],
        out_specs=[pl.BlockSpec((None, m, bn), lambda n, t: (n, 0, 0))],
        params=_params(40, ("arbitrary", "arbitrary")), exchange=exchange,
        first=lambda: (pl.program_id(0) == 0) & (pl.program_id(1) == 0),
        last=lambda: (pl.program_id(0) == n_blocks - 1) & (pl.program_id(1) == nt - 1))
    return (out, landed) if exchange else out


def _reduce_small(gathered, lb_raw):
    def body(g_ref, lb_ref, tot_ref, dlb_ref):
        tot = g_ref[0]
        for d in range(1, N_DEV):
            tot = tot + g_ref[d]
        tot_ref[...] = tot
        a = lb_ref[...]
        m = jnp.max(a, axis=0, keepdims=True)
        e = jnp.exp(a - m)
        lb = e[0:1] / jnp.sum(e, axis=0, keepdims=True)
        d0 = tot[10:11, 0:512] * lb * (1.0 - lb)
        dlb_ref[0:1, :] = d0
        dlb_ref[1:2, :] = -d0

    return pl.pallas_call(
        body, name="reduce_small",
        out_shape=[jax.ShapeDtypeStruct((SMALL_ROWS, D_MODEL), F32), jax.ShapeDtypeStruct((2, 512), F32)],
    )(gathered, lb_raw)


def _adamw_update(w, gv, m, v):
    nm = ADAM_B1 * m + (1.0 - ADAM_B1) * gv
    nv = ADAM_B2 * v + (1.0 - ADAM_B2) * jnp.square(gv)
    m_hat = nm / (1.0 - ADAM_B1 ** ADAM_STEP)
    v_hat = nv / (1.0 - ADAM_B2 ** ADAM_STEP)
    return -ADAM_LR * (m_hat / (jnp.sqrt(v_hat) + ADAM_EPS) + ADAM_WD * w), nm, nv


def _adamw_halves(core, w, mine, theirs, m, v, name):
    rows, cols = w.shape
    h = rows // 2
    tr = _row_tile(h)
    per_half = h // tr

    def body(core_ref, w_ref, mine_ref, theirs_ref, m_ref, v_ref, g_ref, d_ref, nm_ref, nv_ref):
        is_mine = pl.program_id(0) // per_half == core_ref[0]
        gv = jnp.where(is_mine, mine_ref[...], theirs_ref[...])
        g_ref[...] = gv
        d_ref[...], nm_ref[...], nv_ref[...] = _adamw_update(w_ref[...], gv, m_ref[...], v_ref[...])

    full = pl.BlockSpec((tr, cols), lambda i, core_ref: (i, 0))
    part = pl.BlockSpec((tr, cols), lambda i, core_ref: (i % per_half, 0))
    return _pcall(
        body, name=name, out_shape=[jax.ShapeDtypeStruct(w.shape, F32)] * 4,
        grid_spec=pltpu.PrefetchScalarGridSpec(
            num_scalar_prefetch=1, grid=(rows // tr,), in_specs=[full, part, part, full, full], out_specs=[full] * 4),
        compiler_params=_params(40, ("arbitrary",)),
        operands=(core, w, mine, theirs, m, v))


def _adamw(w, g, m, v, name):
    rows, cols = w.shape
    tr = _row_tile(rows) if rows >= 8 else rows

    def body(w_ref, g_ref, m_ref, v_ref, d_ref, nm_ref, nv_ref):
        d_ref[...], nm_ref[...], nv_ref[...] = _adamw_update(w_ref[...], g_ref[...], m_ref[...], v_ref[...])

    spec = pl.BlockSpec((tr, cols), lambda i: (i, 0))
    return _pcall(
        body, name=name, grid=(rows // tr,),
        out_shape=[jax.ShapeDtypeStruct(w.shape, F32)] * 3,
        in_specs=[spec] * 4, out_specs=[spec] * 3,
        compiler_params=_params(40, ("arbitrary",)),
        operands=(w, g, m, v))


def kernel(x, c, positions, w_ada, b_ada, w_in, hg_lower_bounds, hg_norm_w, mla_q_norm_w, w_q_up, mla_kv_norm_w, w_kv_up, w_out, ln1_g, ln1_b, w_mlp_in, w_mlp_out, ln2_g, ln2_b, loss_target, m_w_ada, m_b_ada, m_w_in, m_hg_lower_bounds, m_hg_norm_w, m_mla_q_norm_w, m_w_q_up, m_mla_kv_norm_w, m_w_kv_up, m_w_out, m_ln1_g, m_ln1_b, m_w_mlp_in, m_w_mlp_out, m_ln2_g, m_ln2_b, v_w_ada, v_b_ada, v_w_in, v_hg_lower_bounds, v_hg_norm_w, v_mla_q_norm_w, v_w_q_up, v_mla_kv_norm_w, v_w_kv_up, v_w_out, v_ln1_g, v_ln1_b, v_w_mlp_in, v_w_mlp_out, v_ln2_g, v_ln2_b):
    ix, iy, ic = _mesh_pos()
    chip = 2 * ix + iy
    me = 4 * ix + 2 * iy + ic
    core_arr = jnp.reshape(ic, (1,)).astype(jnp.int32)
    chip_arr = jnp.reshape(chip, (1,)).astype(jnp.int32)

    xs = x[0]
    target = loss_target[0]
    t_len = xs.shape[0]
    pos = positions.astype(F32).reshape(t_len, 1)
    inv = 1.0 / (ROPE_THETA ** (jnp.arange(0, ROPE_DIM, 2, dtype=F32) / ROPE_DIM))
    invf = jnp.concatenate([inv, inv, jnp.zeros((128 - ROPE_DIM,), F32)]).reshape(1, 128)

    ada_cols = w_ada.shape[2]
    c_all = _allgather8(jnp.broadcast_to(c, (8, D_MODEL)), "gather_c")[:, 0, :]
    b_shard = lax.dynamic_slice(b_ada, (0, chip * ada_cols), (1, ada_cols))
    mod_cols, cond16 = _ada_project(c_all, w_ada[0], b_shard)
    mod_all = _allgather8(mod_cols, "gather_mod")
    mod_mine = lax.dynamic_slice(mod_all, (0, me, 0), (N_DEV, 1, ada_cols))[::2, 0, :]
    mod_mine = mod_mine.reshape(6, D_MODEL)
    sh_a, sc_a, g_a, sh_m, sc_m, g_m = (mod_mine[i:i + 1] for i in range(6))

    def slot(w):
        rows, cols = w.shape
        own = w.astype(BF16).reshape(1, 2, rows // 2, cols)
        return lax.dynamic_update_slice(jnp.zeros((N_CHIPS, 2, rows // 2, cols), BF16), own, (chip, 0, 0, 0))

    def whole(s):
        return s.reshape(N_CHIPS, 2 * s.shape[2], s.shape[3])

    def halved(g):
        return g.reshape(N_CHIPS, 2, g.shape[1] // 2, g.shape[2])

    early = _run_exchange(_gather_over_ici([slot(w_in[0]), slot(w_q_up[0]), slot(w_kv_up[0])]),
                          "gather_mixer_weights_ici")
    g_in, g_q, g_kv = (whole(s) for s in _run_exchange(_gather_over_d2d(early), "gather_mixer_weights_d2d"))
    w_in_full = jnp.transpose(g_in, (1, 0, 2)).reshape(D_MODEL, IN_COLS)
    w_in_full = jnp.pad(w_in_full, ((0, 0), (0, IN_COLS_PAD - IN_COLS)))
    w_q_full = jnp.pad(g_q, ((0, 0), (0, 0), (0, QK_DIM - g_q.shape[2])))

    u_a, zhg, cq, ckv, q, k, k_t, v, v_t = _in_project(xs, pos, sc_a, sh_a, w_in_full, mla_q_norm_w, mla_kv_norm_w,
                                                      w_q_full, g_kv, invf)
    (o_pre, o_hg, states), mlp_slots = _hgrn_forward(
        zhg, hg_lower_bounds, hg_norm_w, _gather_over_ici([slot(w_mlp_in[0]), slot(w_mlp_out[0]), slot(w_out[0])]))
    (o_mla, lse), mlp_slots = _attention_forward(q, k, v_t, _gather_over_d2d(mlp_slots))
    g_w1, g_w2, g_out = (whole(s) for s in mlp_slots)
    w_out_full = g_out.reshape(D_MODEL, D_MODEL)
    cat, mix, xhat1, rstd1 = _out_project(o_hg, o_mla, xs, g_a, w_out_full)
    vecs = jnp.concatenate([ln1_g, ln1_b, sc_m, sh_m, g_m, g_a, ln2_g, ln2_b], axis=0)
    act, dhp, um, dh, dmix, d_cat, dr1, mlp_sums = _mlp_and_back(xhat1, rstd1, mix, target, vecs, g_w1, g_w2, w_out_full)

    gw_1 = _weight_grad(um, dhp, "grad_w_mlp_in", N_CHIPS, D_FF // N_CHIPS)
    gw_2 = _weight_grad(act, dh, "grad_w_mlp_out", N_CHIPS, D_MODEL, a_blocked=True, b_blocked=False)
    gw_out = _weight_grad(cat, dmix, "grad_w_out", 1, D_MODEL).reshape(N_CHIPS, D_MODEL // N_CHIPS, D_MODEL)
    mlp_grads = [halved(gw_1), halved(gw_2), halved(gw_out)]
    (dq, dk, dv), landed = _attention_backward(q, k, k_t, v, o_mla, d_cat, lse, _pair_exchange(mlp_grads))
    chip_sums = [_add_pair(core_arr, g, l) for g, l in zip(mlp_grads, landed)]
    (d_hq, d_hf, d_hi, d_hg, hg_sums), landed = _hgrn_backward(
        zhg, hg_lower_bounds, hg_norm_w, o_pre, d_cat, states, _chip_exchange([b for _, b in chip_sums]))
    mlp_mine = [_add_chips(chip_arr, p, l) for (p, _), l in zip(chip_sums, landed)]
    (dz, dqf, dkvu, cqn, ckvn, grad_x, in_sums), _ = _in_project_backward(
        dq, dk, dv, cq, ckv, pos, invf, mla_q_norm_w, mla_kv_norm_w, w_q_full, g_kv,
        d_hq, d_hf, d_hi, d_hg, w_in_full, dr1, xs, sc_a)

    gw_in, mlp_theirs = _weight_grad(u_a, dz, "grad_w_in", 3, IN_COLS_PAD // 3, exchange=_pair_send(mlp_mine))
    gw_in = jnp.transpose(gw_in, (1, 0, 2)).reshape(D_MODEL, IN_COLS_PAD)[:, :IN_COLS]
    gw_in = jnp.transpose(gw_in.reshape(D_MODEL, N_CHIPS, IN_COLS // N_CHIPS), (1, 0, 2))
    gw_q = _weight_grad(cqn, dqf, "grad_w_q_up", N_HEADS, QK_DIM)[:, :, :HEAD_DIM + ROPE_DIM]
    gw_kv = _weight_grad(ckvn, dkvu, "grad_w_kv_up", N_HEADS, 2 * HEAD_DIM)
    mixer_grads = [halved(g) for g in (gw_in, gw_q, gw_kv)]
    landed = _run_exchange(_pair_exchange(mixer_grads), "grad_pair_exchange")
    chip_sums = [_add_pair(core_arr, g, l) for g, l in zip(mixer_grads, landed)]
    landed = _run_exchange(_chip_exchange([b for _, b in chip_sums]), "grad_chip_exchange")
    mixer_mine = [_add_chips(chip_arr, p, l) for (p, _), l in zip(chip_sums, landed)]
    mixer_theirs = _run_exchange(_pair_send(mixer_mine), "grad_pair_send")
    reduced = ("w_in", "w_q_up", "w_kv_up", "w_mlp_in", "w_mlp_out", "w_out")
    halves_mine = dict(zip(reduced, mixer_mine + mlp_mine))
    halves_theirs = dict(zip(reduced, list(mixer_theirs) + list(mlp_theirs)))

    zeros = lambda n: jnp.zeros((1, n), F32)
    small = jnp.concatenate([
        in_sums[1:2], in_sums[0:1], mlp_sums[S_DGA:S_DGA + 1],
        mlp_sums[S_DSHM:S_DSHM + 1], mlp_sums[S_DSCM:S_DSCM + 1], mlp_sums[S_DGM:S_DGM + 1],
        mlp_sums[S_DLN1G:S_DLN1G + 1], mlp_sums[S_DLN1B:S_DLN1B + 1],
        mlp_sums[S_DLN2G:S_DLN2G + 1], mlp_sums[S_DLN2B:S_DLN2B + 1],
        jnp.concatenate([hg_sums[0:1], hg_sums[1:2]], axis=1),
        jnp.concatenate([in_sums[2:3, :Q_RANK], in_sums[3:4, :KV_RANK], zeros(D_MODEL - Q_RANK - KV_RANK)], axis=1),
        mlp_sums[S_LOSS:S_LOSS + 1],
        jnp.zeros((SMALL_ROWS - 13, D_MODEL), F32)], axis=0)
    small_all = _allgather8(small, "gather_small")
    tot, g_lb = _reduce_small(small_all, hg_lower_bounds)
    loss = tot[12, 0]
    g_b_ada = tot[0:6].reshape(1, 6 * D_MODEL)
    g_ln1_g, g_ln1_b, g_ln2_g, g_ln2_b = tot[6:7], tot[7:8], tot[8:9], tot[9:10]
    g_hg_norm = tot[10:11, 512:1024]
    g_q_norm = tot[11:12, 0:Q_RANK]
    g_kv_norm = tot[11:12, Q_RANK:Q_RANK + KV_RANK]

    d_mod_all = small_all[:, 0:6, :].reshape(N_DEV, 6 * D_MODEL)
    d_mod_cols = lax.dynamic_slice(d_mod_all, (0, chip * ada_cols), (N_DEV, ada_cols))
    d_mod_cols = jnp.concatenate([d_mod_cols, jnp.zeros_like(d_mod_cols)], axis=0)
    g_w_ada = _weight_grad(cond16, d_mod_cols, "grad_w_ada", 1, ada_cols)[0]

    names = ["w_ada", "b_ada", "w_in", "hg_lower_bounds", "hg_norm_w", "mla_q_norm_w", "w_q_up", "mla_kv_norm_w",
             "w_kv_up", "w_out", "ln1_g", "ln1_b", "w_mlp_in", "w_mlp_out", "ln2_g", "ln2_b"]
    weights = [w_ada, b_ada, w_in, hg_lower_bounds, hg_norm_w, mla_q_norm_w, w_q_up, mla_kv_norm_w,
               w_kv_up, w_out, ln1_g, ln1_b, w_mlp_in, w_mlp_out, ln2_g, ln2_b]
    moms = [m_w_ada, m_b_ada, m_w_in, m_hg_lower_bounds, m_hg_norm_w, m_mla_q_norm_w, m_w_q_up, m_mla_kv_norm_w,
            m_w_kv_up, m_w_out, m_ln1_g, m_ln1_b, m_w_mlp_in, m_w_mlp_out, m_ln2_g, m_ln2_b]
    vels = [v_w_ada, v_b_ada, v_w_in, v_hg_lower_bounds, v_hg_norm_w, v_mla_q_norm_w, v_w_q_up, v_mla_kv_norm_w,
            v_w_kv_up, v_w_out, v_ln1_g, v_ln1_b, v_w_mlp_in, v_w_mlp_out, v_ln2_g, v_ln2_b]
    grads2d = [g_w_ada, g_b_ada, None, g_lb, g_hg_norm, g_q_norm, None, g_kv_norm,
               None, None, g_ln1_g, g_ln1_b, None, None, g_ln2_g, g_ln2_b]
    out_g, out_d, out_m, out_v = [], [], [], []
    for name, w, g, m, vv in zip(names, weights, grads2d, moms, vels):
        if g is None:
            shape2 = w.shape[1:]
            g, d, nm, nv = _adamw_halves(core_arr, w.reshape(shape2), halves_mine[name], halves_theirs[name],
                                         m.reshape(shape2), vv.reshape(shape2), "adamw_" + name)
        else:
            shape2 = g.shape
            d, nm, nv = _adamw(w.reshape(shape2), g, m.reshape(shape2), vv.reshape(shape2), "adamw_" + name)
        out_g.append(g.reshape(w.shape))
        out_d.append(d.reshape(w.shape))
        out_m.append(nm.reshape(w.shape))
        out_v.append(nv.reshape(w.shape))
    return (loss, grad_x[None], *out_g, *out_d, *out_m, *out_v)
```

```python
import functools

import jax
import jax.numpy as jnp
from jax import lax
from jax.experimental import pallas as pl
from jax.experimental.pallas import tpu as pltpu

F32 = jnp.float32
BF16 = jnp.bfloat16
MESH_IDS = pl.DeviceIdType.MESH

D_MODEL = 1024
N_HEADS = 4
HEAD_DIM = 128
ROPE_DIM = 64
HG_CHUNK = 64
HG_COLS = 2048
Q_RANK = 256
KV_RANK = 256
IN_COLS = 2624
IN_COLS_PAD = 2688
QK_DIM = 256
D_FF = 4096
N_CHIPS = 4
N_DEV = 8
ROPE_THETA = 10000.0
RMS_EPS = 1e-6
LN_EPS = 1e-5
DN_ALPHA = 2.0 ** 0.25
ATT_SCALE = (HEAD_DIM + ROPE_DIM) ** -0.5
NEG_BIG = -1e30
ADAM_LR = 0.001
ADAM_B1 = 0.9
ADAM_B2 = 0.999
ADAM_EPS = 1e-08
ADAM_WD = 0.01
ADAM_STEP = 10
SMALL_ROWS = 16
MIB = 1024 * 1024


def _dot(a, b):
    return jnp.dot(a, b, preferred_element_type=F32)


def _dot_nt(a, b):
    return lax.dot_general(a, b, (((1,), (1,)), ((), ())), preferred_element_type=F32)


def _dot_tn(a, b):
    return lax.dot_general(a, b, (((0,), (0,)), ((), ())), preferred_element_type=F32)


def _params(vmem_mib, semantics=None):
    return pltpu.CompilerParams(vmem_limit_bytes=vmem_mib * MIB, dimension_semantics=semantics)


def _sigmoid(v):
    return 1.0 / (1.0 + jnp.exp(-v))


def _colsum(v):
    return jnp.sum(v, axis=0, keepdims=True)


def _rowmean(v):
    return jnp.mean(v, axis=-1, keepdims=True)


def _rope_tables(pos, invf):
    ang = pos * invf
    lane = lax.broadcasted_iota(jnp.int32, ang.shape, 1)
    cos_t = jnp.where(lane < ROPE_DIM, jnp.cos(ang), 0.0)
    sin = jnp.sin(ang)
    sin_t = jnp.where(lane < ROPE_DIM // 2, -sin, jnp.where(lane < ROPE_DIM, sin, 0.0))
    return cos_t, sin_t


def _swap_halves(t):
    lane = lax.broadcasted_iota(jnp.int32, t.shape, 1)
    return jnp.where(lane < ROPE_DIM // 2, pltpu.roll(t, 128 - ROPE_DIM // 2, 1), pltpu.roll(t, ROPE_DIM // 2, 1))


def _rope(t, cos_t, sin_t):
    return t * cos_t + _swap_halves(t) * sin_t


def _unrope(g, cos_t, sin_t):
    return g * cos_t - _swap_halves(g) * sin_t


def _mesh_pos():
    return lax.axis_index("x"), lax.axis_index("y"), lax.axis_index("c")


def _other_chips(x, y):
    out = []
    for dx, dy in ((1, 0), (0, 1), (1, 1)):
        px = 1 - x if dx else x
        py = 1 - y if dy else y
        out.append(((px, py), 2 * px + py))
    return out


def _allgather8(a, name):
    rows, cols = a.shape

    def body(a_ref, out_ref, send_sems, recv_sems):
        x, y, c = _mesh_pos()
        me = 4 * x + 2 * y + c
        out_ref[me] = a_ref[...]
        peers = []
        for r in range(1, N_DEV):
            px = 1 - x if r & 4 else x
            py = 1 - y if r & 2 else y
            pc = 1 - c if r & 1 else c
            peers.append(((px, py, pc), 4 * px + 2 * py + pc))

        def copy(r, block, to):
            return pltpu.make_async_remote_copy(
                src_ref=a_ref, dst_ref=out_ref.at[block], send_sem=send_sems.at[r], recv_sem=recv_sems.at[r],
                device_id=to, device_id_type=MESH_IDS)

        sends = [copy(r, me, peer) for r, (peer, _) in enumerate(peers)]
        for cp in sends:
            cp.start()
        for r, (peer, idx) in enumerate(peers):
            copy(r, idx, peer).wait_recv()
        for cp in sends:
            cp.wait_send()

    return pl.pallas_call(
        body, name=name,
        out_shape=jax.ShapeDtypeStruct((N_DEV, rows, cols), a.dtype),
        in_specs=[pl.BlockSpec(memory_space=pltpu.VMEM)],
        out_specs=pl.BlockSpec(memory_space=pltpu.VMEM),
        scratch_shapes=[pltpu.SemaphoreType.DMA((N_DEV - 1,)), pltpu.SemaphoreType.DMA((N_DEV - 1,))],
    )(a)


class _Exchange:
    def __init__(self, inputs, out_shapes, aliases, sems, start, finish):
        self.inputs, self.out_shapes, self.aliases, self.sems = list(inputs), list(out_shapes), dict(aliases), list(sems)
        self.start, self.finish = start, finish


def _from_copies(inputs, out_shapes, aliases, sems, copies):
    def start(ins, outs, sem_refs):
        for send, _ in copies(ins, outs, sem_refs):
            send.start()

    def finish(ins, outs, sem_refs):
        for send, recv in copies(ins, outs, sem_refs):
            recv.wait_recv()
            send.wait_send()

    return _Exchange(inputs, out_shapes, aliases, sems, start, finish)


HBM_MIN_BYTES = 256 * 1024


def _in_hbm(a):
    if a.size * a.dtype.itemsize < HBM_MIN_BYTES:
        return a
    return pltpu.with_memory_space_constraint(a, pltpu.HBM)


def _out_hbm(s):
    if s.size * s.dtype.itemsize < HBM_MIN_BYTES:
        return s
    return pltpu.HBM(s.shape, s.dtype)


def _pcall(body, *, operands, out_shape, **kwargs):
    single = not isinstance(out_shape, (list, tuple))
    shapes = [_out_hbm(s) for s in ([out_shape] if single else out_shape)]
    return pl.pallas_call(body, out_shape=shapes[0] if single else shapes, **kwargs)(*[_in_hbm(a) for a in operands])


def _run_exchange(exchange, name):
    n_in, n_out = len(exchange.inputs), len(exchange.out_shapes)

    def body(*refs):
        ins, outs, sem_refs = refs[:n_in], refs[n_in:n_in + n_out], refs[n_in + n_out:]
        exchange.start(ins, outs, sem_refs)
        exchange.finish(ins, outs, sem_refs)

    any_spec = pl.BlockSpec(memory_space=pl.ANY)
    return pl.pallas_call(
        body, name=name, out_shape=[_out_hbm(s) for s in exchange.out_shapes],
        in_specs=[any_spec] * n_in, out_specs=[any_spec] * n_out,
        scratch_shapes=exchange.sems, input_output_aliases=exchange.aliases,
    )(*[_in_hbm(a) for a in exchange.inputs])


def _pallas(body, *, name, operands, in_specs, out_shape, out_specs, params, scratch_shapes=(), grid=(), prefetch=(),
            exchange=None, first=None, last=None):
    n_pre, n_in, n_out, n_scr = len(prefetch), len(in_specs), len(out_specs), len(scratch_shapes)
    ex_in = exchange.inputs if exchange else []
    ex_out = exchange.out_shapes if exchange else []
    ex_sems = exchange.sems if exchange else []

    def full_body(*refs):
        pre, rest = refs[:n_pre], refs[n_pre:]
        ins, rest = rest[:n_in], rest[n_in:]
        xin, rest = rest[:len(ex_in)], rest[len(ex_in):]
        outs, rest = rest[:n_out], rest[n_out:]
        xout, rest = rest[:len(ex_out)], rest[len(ex_out):]
        scr, sem_refs = rest[:n_scr], rest[n_scr:]
        if exchange:
            @pl.when(first(*pre))
            def _():
                exchange.start(xin, xout, sem_refs)

        body(*pre, *ins, *outs, *scr)
        if exchange:
            @pl.when(last(*pre))
            def _():
                exchange.finish(xin, xout, sem_refs)

    any_spec = pl.BlockSpec(memory_space=pl.ANY)
    aliases = {n_pre + n_in + i: n_out + o for i, o in exchange.aliases.items()} if exchange else {}
    operands = [_in_hbm(a) for a in operands]
    results = pl.pallas_call(
        full_body, name=name, out_shape=[_out_hbm(s) for s in list(out_shape) + ex_out],
        grid_spec=pltpu.PrefetchScalarGridSpec(
            num_scalar_prefetch=n_pre, grid=grid, in_specs=list(in_specs) + [any_spec] * len(ex_in),
            out_specs=list(out_specs) + [any_spec] * len(ex_out), scratch_shapes=list(scratch_shapes) + ex_sems),
        input_output_aliases=aliases, compiler_params=params,
    )(*prefetch, *operands, *[_in_hbm(a) for a in ex_in])
    return results[:n_out], results[n_out:]


def _remote(src, dst, sems, idx, to):
    send_sems, recv_sems = sems
    return pltpu.make_async_remote_copy(src_ref=src, dst_ref=dst, send_sem=send_sems.at[idx], recv_sem=recv_sems.at[idx],
                                        device_id=to, device_id_type=MESH_IDS)


def _sem_pairs(*shape):
    return [pltpu.SemaphoreType.DMA(shape), pltpu.SemaphoreType.DMA(shape)]


def _same_shapes(arrays):
    return [jax.ShapeDtypeStruct(a.shape, a.dtype) for a in arrays]


def _gather_over_ici(slots):
    n = len(slots)

    def copies(ins, outs, sems):
        x, y, c = _mesh_pos()
        k = 2 * x + y
        out = []
        for j, (chip, kj) in enumerate(_other_chips(x, y)):
            for i in range(n):
                to = (*chip, c)
                out.append((_remote(ins[i].at[k, c], outs[i].at[k, c], sems, (j, i), to),
                            _remote(ins[i].at[k, c], outs[i].at[kj, c], sems, (j, i), to)))
        return out

    return _from_copies(slots, _same_shapes(slots), {i: i for i in range(n)}, _sem_pairs(3, n), copies)


def _gather_over_d2d(slots):
    n = len(slots)

    def copies(ins, outs, sems):
        x, y, c = _mesh_pos()
        sibling = (x, y, 1 - c)
        out = []
        for j, (_, kj) in enumerate(_other_chips(x, y)):
            for i in range(n):
                out.append((_remote(ins[i].at[kj, c], outs[i].at[kj, c], sems, (j, i), sibling),
                            _remote(ins[i].at[kj, c], outs[i].at[kj, 1 - c], sems, (j, i), sibling)))
        return out

    return _from_copies(slots, _same_shapes(slots), {i: i for i in range(n)}, _sem_pairs(3, n), copies)


def _pair_exchange(grads):
    n = len(grads)

    def copies(ins, outs, sems):
        x, y, c = _mesh_pos()
        cps = [_remote(ins[i].at[:, 1 - c], outs[i], sems, i, (x, y, 1 - c)) for i in range(n)]
        return [(cp, cp) for cp in cps]

    shapes = [jax.ShapeDtypeStruct((N_CHIPS,) + g.shape[2:], g.dtype) for g in grads]
    return _from_copies(grads, shapes, {}, _sem_pairs(n), copies)


def _chip_exchange(partials):
    n = len(partials)

    def copies(ins, outs, sems):
        x, y, c = _mesh_pos()
        cps = [_remote(ins[i].at[kj], outs[i].at[j], sems, (j, i), (*chip, c))
               for j, (chip, kj) in enumerate(_other_chips(x, y)) for i in range(n)]
        return [(cp, cp) for cp in cps]

    shapes = [jax.ShapeDtypeStruct((3,) + p.shape[1:], p.dtype) for p in partials]
    return _from_copies(partials, shapes, {}, _sem_pairs(3, n), copies)


def _pair_send(halves):
    n = len(halves)

    def copies(ins, outs, sems):
        x, y, c = _mesh_pos()
        cps = [_remote(ins[i], outs[i], sems, i, (x, y, 1 - c)) for i in range(n)]
        return [(cp, cp) for cp in cps]

    return _from_copies(halves, _same_shapes(halves), {}, _sem_pairs(n), copies)


def _row_tile(rows):
    for t in (256, 128, 64, 32, 16, 8):
        if rows % t == 0:
            return t
    return rows


def _add_pair(core, grad, landed):
    _, h, cols = landed.shape
    tr = _row_tile(h)

    def body(core_ref, g_ref, l_ref, o_ref, ob_ref):
        s = g_ref[...] + l_ref[...]
        o_ref[...] = s
        ob_ref[...] = s.astype(BF16)

    out_spec = pl.BlockSpec((None, tr, cols), lambda k, t, core_ref: (k, t, 0))
    return _pcall(
        body, name="grad_add_pair",
        out_shape=[jax.ShapeDtypeStruct(landed.shape, F32), jax.ShapeDtypeStruct(landed.shape, BF16)],
        grid_spec=pltpu.PrefetchScalarGridSpec(
            num_scalar_prefetch=1, grid=(N_CHIPS, h // tr),
            in_specs=[pl.BlockSpec((None, None, tr, cols), lambda k, t, core_ref: (k, core_ref[0], t, 0)),
                      pl.BlockSpec((None, tr, cols), lambda k, t, core_ref: (k, t, 0))],
            out_specs=[out_spec, out_spec]),
        compiler_params=_params(32, ("arbitrary", "arbitrary")),
        operands=(core, grad, landed))


def _add_chips(chip, partial, landed):
    _, h, cols = partial.shape
    tr = _row_tile(h)

    def body(chip_ref, p_ref, l_ref, o_ref):
        o_ref[...] = ((p_ref[...] + l_ref[0].astype(F32)) + l_ref[1].astype(F32)) + l_ref[2].astype(F32)

    return _pcall(
        body, name="grad_add_chips",
        out_shape=jax.ShapeDtypeStruct((h, cols), F32),
        grid_spec=pltpu.PrefetchScalarGridSpec(
            num_scalar_prefetch=1, grid=(h // tr,),
            in_specs=[pl.BlockSpec((None, tr, cols), lambda t, chip_ref: (chip_ref[0], t, 0)),
                      pl.BlockSpec((3, tr, cols), lambda t, chip_ref: (0, t, 0))],
            out_specs=pl.BlockSpec((tr, cols), lambda t, chip_ref: (t, 0))),
        compiler_params=_params(32, ("arbitrary",)),
        operands=(chip, partial, landed))


def _ada_project(c_all, w_ada, b_shard):
    n = w_ada.shape[1]
    tn = 512

    def body(c_ref, w_ref, b_ref, mod_ref, cond_ref):
        cv = c_ref[...]
        cond = cv * _sigmoid(cv)
        mod_ref[...] = _dot(cond.astype(BF16), w_ref[...].astype(BF16)) + b_ref[...]
        cond_ref[0:N_DEV, :] = cond
        cond_ref[N_DEV:2 * N_DEV, :] = jnp.zeros_like(cond)

    return _pcall(
        body, name="ada_project", grid=(n // tn,),
        out_shape=[jax.ShapeDtypeStruct((N_DEV, n), F32), jax.ShapeDtypeStruct((2 * N_DEV, D_MODEL), F32)],
        in_specs=[pl.BlockSpec((N_DEV, D_MODEL), lambda j: (0, 0)), pl.BlockSpec((D_MODEL, tn), lambda j: (0, j)),
                  pl.BlockSpec((1, tn), lambda j: (0, j))],
        out_specs=[pl.BlockSpec((N_DEV, tn), lambda j: (0, j)), pl.BlockSpec((2 * N_DEV, D_MODEL), lambda j: (0, 0))],
        compiler_params=_params(32, ("arbitrary",)),
        operands=(c_all, w_ada, b_shard))


def _in_project(x, pos, sc_a, sh_a, w_in, q_norm_w, kv_norm_w, w_q, w_kv, invf):
    t_len = x.shape[0]
    tm = min(256, t_len)

    def body(x_ref, pos_ref, sc_ref, sh_ref, win_ref, qn_ref, kvn_ref, wq_ref, wkv_ref, invf_ref,
             u_ref, zhg_ref, cq_ref, ckv_ref, q_ref, k_ref, kt_ref, v_ref, vt_ref):
        u = (x_ref[...] * (1.0 + sc_ref[...]) + sh_ref[...]).astype(BF16)
        u_ref[...] = u
        z = _dot(u, win_ref[...])
        zhg_ref[...] = z[:, :HG_COLS]
        cq = z[:, HG_COLS:HG_COLS + Q_RANK]
        ckv = z[:, HG_COLS + Q_RANK:HG_COLS + Q_RANK + KV_RANK]
        cq_ref[...] = cq
        ckv_ref[...] = ckv
        cos_t, sin_t = _rope_tables(pos_ref[...], invf_ref[...])
        k_pe = _rope(z[:, HG_COLS + Q_RANK + KV_RANK:], cos_t, sin_t)
        k_pe_t = jnp.transpose(k_pe).astype(BF16)
        cqn = (cq * lax.rsqrt(_rowmean(cq * cq) + RMS_EPS) * qn_ref[...]).astype(BF16)
        ckvn = (ckv * lax.rsqrt(_rowmean(ckv * ckv) + RMS_EPS) * kvn_ref[...]).astype(BF16)
        for h in range(N_HEADS):
            qh = _dot(cqn, wq_ref[h])
            q_ref[h, :, 0:HEAD_DIM] = qh[:, :HEAD_DIM].astype(BF16)
            q_ref[h, :, HEAD_DIM:QK_DIM] = _rope(qh[:, HEAD_DIM:], cos_t, sin_t).astype(BF16)
            kvh = _dot(ckvn, wkv_ref[h])
            k_ref[h, :, 0:HEAD_DIM] = kvh[:, :HEAD_DIM].astype(BF16)
            k_ref[h, :, HEAD_DIM:QK_DIM] = k_pe.astype(BF16)
            kt_ref[h, 0:HEAD_DIM, :] = jnp.transpose(kvh[:, :HEAD_DIM]).astype(BF16)
            kt_ref[h, HEAD_DIM:QK_DIM, :] = k_pe_t
            v_ref[h] = kvh[:, HEAD_DIM:].astype(BF16)
            vt_ref[h] = jnp.transpose(kvh[:, HEAD_DIM:]).astype(BF16)

    row = lambda i: (i, 0)
    fixed2 = lambda i: (0, 0)
    fixed3 = lambda i: (0, 0, 0)
    heads = lambda i: (0, i, 0)
    return _pcall(
        body, name="in_project", grid=(t_len // tm,),
        out_shape=[jax.ShapeDtypeStruct((t_len, D_MODEL), BF16), jax.ShapeDtypeStruct((t_len, HG_COLS), F32),
                   jax.ShapeDtypeStruct((t_len, Q_RANK), F32), jax.ShapeDtypeStruct((t_len, KV_RANK), F32),
                   jax.ShapeDtypeStruct((N_HEADS, t_len, QK_DIM), BF16),
                   jax.ShapeDtypeStruct((N_HEADS, t_len, QK_DIM), BF16),
                   jax.ShapeDtypeStruct((N_HEADS, QK_DIM, t_len), BF16),
                   jax.ShapeDtypeStruct((N_HEADS, t_len, HEAD_DIM), BF16),
                   jax.ShapeDtypeStruct((N_HEADS, HEAD_DIM, t_len), BF16)],
        in_specs=[pl.BlockSpec((tm, D_MODEL), row), pl.BlockSpec((tm, 1), row),
                  pl.BlockSpec((1, D_MODEL), fixed2), pl.BlockSpec((1, D_MODEL), fixed2),
                  pl.BlockSpec((D_MODEL, IN_COLS_PAD), fixed2),
                  pl.BlockSpec((1, Q_RANK), fixed2), pl.BlockSpec((1, KV_RANK), fixed2),
                  pl.BlockSpec((N_HEADS, Q_RANK, QK_DIM), fixed3), pl.BlockSpec((N_HEADS, KV_RANK, 2 * HEAD_DIM), fixed3),
                  pl.BlockSpec((1, 128), fixed2)],
        out_specs=[pl.BlockSpec((tm, D_MODEL), row), pl.BlockSpec((tm, HG_COLS), row),
                   pl.BlockSpec((tm, Q_RANK), row), pl.BlockSpec((tm, KV_RANK), row),
                   pl.BlockSpec((N_HEADS, tm, QK_DIM), heads), pl.BlockSpec((N_HEADS, tm, QK_DIM), heads),
                   pl.BlockSpec((N_HEADS, QK_DIM, tm), lambda i: (0, 0, i)),
                   pl.BlockSpec((N_HEADS, tm, HEAD_DIM), heads),
                   pl.BlockSpec((N_HEADS, HEAD_DIM, tm), lambda i: (0, 0, i))],
        compiler_params=_params(48, ("arbitrary",)),
        operands=(x, pos, sc_a, sh_a, w_in, q_norm_w, kv_norm_w, w_q, w_kv, invf))


def _lower_bound(lb_raw):
    m = jnp.max(lb_raw, axis=0, keepdims=True)
    e = jnp.exp(lb_raw - m)
    return e[0:1] / jnp.sum(e, axis=0, keepdims=True)


def _tri(inclusive_lower):
    r = lax.broadcasted_iota(jnp.int32, (HG_CHUNK, HG_CHUNK), 0)
    c = lax.broadcasted_iota(jnp.int32, (HG_CHUNK, HG_CHUNK), 1)
    return (c <= r) if inclusive_lower else (c >= r)


def _chunk_rows(n):
    return slice(n * HG_CHUNK, (n + 1) * HG_CHUNK)


def _chunk_prefix_sums(v, inclusive_lower):
    tri = _tri(inclusive_lower).astype(BF16)
    hi = v.astype(BF16)
    rest = v - hi.astype(F32)
    mid = rest.astype(BF16)
    lo = (rest - mid.astype(F32)).astype(BF16)
    pieces = jnp.concatenate([hi, mid, lo], axis=1)
    out = []
    for n in range(v.shape[0] // HG_CHUNK):
        s = _dot(tri, pieces[_chunk_rows(n)])
        out.append((s[:, 0:HEAD_DIM] + s[:, HEAD_DIM:2 * HEAD_DIM]) + s[:, 2 * HEAD_DIM:])
    return jnp.concatenate(out, axis=0)


def _per_chunk(v, row):
    n = v.shape[0] // HG_CHUNK
    v3 = v.reshape(n, HG_CHUNK, HEAD_DIM)
    return jnp.broadcast_to(v3[:, row:row + 1, :], v3.shape).reshape(v.shape)


def _hg_block(q, f_logit, lb):
    sg = _sigmoid(f_logit)
    forget = lb + (1.0 - lb) * sg
    kk = 1.0 - forget
    b = _chunk_prefix_sums(jnp.log(forget), True)
    b_ref = _per_chunk(b, HG_CHUNK // 2 - 1)
    b_last = _per_chunk(b, HG_CHUNK - 1)
    e_i = jnp.exp(b - b_ref)
    e_ri = jnp.exp(b_ref - b)
    e_b = jnp.exp(b)
    e_l = jnp.exp(b_last - b)
    return dict(sg=sg, forget=forget, e_i=e_i, e_ri=e_ri, e_b=e_b, e_l=e_l, dec=jnp.exp(b_last),
                qi=q * e_i, ki=kk * e_ri, qe=q * e_b, kl=kk * e_l)


def _hgrn_forward(zhg, lb_raw, norm_w, exchange=None):
    t_len = zhg.shape[0]
    tb = min(512, t_len)
    n_chunks = tb // HG_CHUNK

    def body(q_ref, f_ref, v_ref, g_ref, lb_ref, w_ref, opre_ref, o_ref, st_ref, state):
        @pl.when(pl.program_id(1) == 0)
        def _():
            state[...] = jnp.zeros_like(state)

        blk = _hg_block(q_ref[...], f_ref[...], _lower_bound(lb_ref[...]))
        v = v_ref[...].astype(BF16)
        qi, ki, qe, kl = (blk[name].astype(BF16) for name in ("qi", "ki", "qe", "kl"))
        causal = _tri(True)
        st = state[...]
        parts = []
        for n in range(n_chunks):
            r = _chunk_rows(n)
            a = jnp.where(causal, _dot_nt(qi[r], ki[r]), 0.0).astype(BF16)
            st_ref[0, n] = st
            parts.append(_dot(a, v[r]) + _dot_nt(qe[r], st.astype(BF16)))
            st = st * blk["dec"][n * HG_CHUNK:n * HG_CHUNK + 1] + _dot_tn(v[r], kl[r])
        state[...] = st
        o = jnp.concatenate(parts, axis=0)
        opre_ref[...] = o
        g = g_ref[...]
        o_ref[...] = o * lax.rsqrt(_rowmean(o * o) + RMS_EPS) * w_ref[...] * (g * _sigmoid(g))

    col = lambda off: (lambda h, t: (t, off + h))
    nb = t_len // tb
    return _pallas(
        body, name="hgrn_forward", grid=(N_HEADS, nb), operands=(zhg, zhg, zhg, zhg, lb_raw, norm_w),
        out_shape=[jax.ShapeDtypeStruct((t_len, N_HEADS * HEAD_DIM), F32),
                   jax.ShapeDtypeStruct((t_len, N_HEADS * HEAD_DIM), F32),
                   jax.ShapeDtypeStruct((N_HEADS, t_len // HG_CHUNK, HEAD_DIM, HEAD_DIM), F32)],
        in_specs=[pl.BlockSpec((tb, HEAD_DIM), col(0)), pl.BlockSpec((tb, HEAD_DIM), col(N_HEADS)),
                  pl.BlockSpec((tb, HEAD_DIM), col(2 * N_HEADS)), pl.BlockSpec((tb, HEAD_DIM), col(3 * N_HEADS)),
                  pl.BlockSpec((2, HEAD_DIM), lambda h, t: (0, h)), pl.BlockSpec((1, HEAD_DIM), lambda h, t: (0, h))],
        out_specs=[pl.BlockSpec((tb, HEAD_DIM), col(0)), pl.BlockSpec((tb, HEAD_DIM), col(0)),
                   pl.BlockSpec((1, n_chunks, HEAD_DIM, HEAD_DIM), lambda h, t: (h, t, 0, 0))],
        scratch_shapes=[pltpu.VMEM((HEAD_DIM, HEAD_DIM), F32)],
        params=_params(32, ("arbitrary", "arbitrary")), exchange=exchange,
        first=lambda: (pl.program_id(0) == 0) & (pl.program_id(1) == 0),
        last=lambda: (pl.program_id(0) == N_HEADS - 1) & (pl.program_id(1) == nb - 1))


def _hgrn_backward(zhg, lb_raw, norm_w, o_pre, d_cat, states, exchange=None):
    t_len = zhg.shape[0]
    tb = min(512, t_len)
    n_chunks = tb // HG_CHUNK
    nb = t_len // tb

    def body(q_ref, f_ref, v_ref, g_ref, lb_ref, w_ref, opre_ref, do_ref, st_ref,
             dq_ref, df_ref, dv_ref, dg_ref, sums_ref, gstate):
        @pl.when(pl.program_id(1) == 0)
        def _():
            gstate[...] = jnp.zeros_like(gstate)
            sums_ref[...] = jnp.zeros_like(sums_ref)

        lb = _lower_bound(lb_ref[...])
        w = w_ref[...]
        o = opre_ref[...]
        g = g_ref[...]
        d_out = do_ref[...]
        r = lax.rsqrt(_rowmean(o * o) + RMS_EPS)
        sg_g = _sigmoid(g)
        dg_ref[...] = (d_out * (o * r * w) * (sg_g * (1.0 + g * (1.0 - sg_g)))).astype(BF16)
        d_on = d_out * (g * sg_g)
        sums_ref[1:2, :] += _colsum(d_on * o * r)
        dy = d_on * w
        d_o = (r * dy - o * (r * r * r) * _rowmean(dy * o)).astype(BF16)
        blk = _hg_block(q_ref[...], f_ref[...], lb)
        v = v_ref[...].astype(BF16)
        qi, ki, qe, kl = (blk[name].astype(BF16) for name in ("qi", "ki", "qe", "kl"))
        causal = _tri(True)
        row_id = lax.broadcasted_iota(jnp.int32, (HG_CHUNK, HEAD_DIM), 0)
        gt = gstate[...]
        d_v, d_qi, d_ki, d_qe, d_kl, d_dec = ([None] * n_chunks for _ in range(6))
        for n in reversed(range(n_chunks)):
            rows = _chunk_rows(n)
            st = st_ref[0, n]
            a = jnp.where(causal, _dot_nt(qi[rows], ki[rows]), 0.0).astype(BF16)
            d_a = jnp.where(causal, _dot_nt(d_o[rows], v[rows]), 0.0).astype(BF16)
            gt_b = gt.astype(BF16)
            d_v[n] = _dot_tn(a, d_o[rows]) + _dot_nt(kl[rows], gt_b)
            d_qi[n] = _dot(d_a, ki[rows])
            d_ki[n] = _dot_tn(d_a, qi[rows])
            d_qe[n] = _dot(d_o[rows], st.astype(BF16))
            d_kl[n] = _dot(v[rows], gt_b)
            d_dec[n] = jnp.where(row_id == HG_CHUNK - 1, _colsum(gt * st), 0.0)
            gt = gt * blk["dec"][n * HG_CHUNK:n * HG_CHUNK + 1] + _dot_tn(d_o[rows], qe[rows])
        gstate[...] = gt
        d_qi, d_ki, d_qe, d_kl, d_dec = (jnp.concatenate(p, axis=0) for p in (d_qi, d_ki, d_qe, d_kl, d_dec))
        dv_ref[...] = jnp.concatenate(d_v, axis=0).astype(BF16)
        dq_ref[...] = (d_qi * blk["e_i"] + d_qe * blk["e_b"]).astype(BF16)
        d_k = d_ki * blk["e_ri"] + d_kl * blk["e_l"]
        t_qi = d_qi * blk["qi"]
        t_ki = d_ki * blk["ki"]
        t_kl = d_kl * blk["kl"]
        at_ref, at_last = [], []
        for n in range(n_chunks):
            rows = _chunk_rows(n)
            at_ref.append(jnp.where(row_id == HG_CHUNK // 2 - 1, _colsum(t_ki[rows] - t_qi[rows]), 0.0))
            at_last.append(jnp.where(row_id == HG_CHUNK - 1, _colsum(t_kl[rows]), 0.0))
        d_b = (t_qi - t_ki + d_qe * blk["qe"] - t_kl + jnp.concatenate(at_ref, axis=0)
               + jnp.concatenate(at_last, axis=0) + d_dec * blk["dec"])
        d_forget = _chunk_prefix_sums(d_b, False) / blk["forget"] - d_k
        sg = blk["sg"]
        df_ref[...] = (d_forget * (1.0 - lb) * sg * (1.0 - sg)).astype(BF16)
        sums_ref[0:1, :] += _colsum(d_forget * (1.0 - sg))

    col = lambda off: (lambda h, t: (nb - 1 - t, off + h))
    return _pallas(
        body, name="hgrn_backward", grid=(N_HEADS, nb),
        operands=(zhg, zhg, zhg, zhg, lb_raw, norm_w, o_pre, d_cat, states),
        out_shape=[jax.ShapeDtypeStruct((t_len, N_HEADS * HEAD_DIM), BF16)] * 4
        + [jax.ShapeDtypeStruct((8, N_HEADS * HEAD_DIM), F32)],
        in_specs=[pl.BlockSpec((tb, HEAD_DIM), col(0)), pl.BlockSpec((tb, HEAD_DIM), col(N_HEADS)),
                  pl.BlockSpec((tb, HEAD_DIM), col(2 * N_HEADS)), pl.BlockSpec((tb, HEAD_DIM), col(3 * N_HEADS)),
                  pl.BlockSpec((2, HEAD_DIM), lambda h, t: (0, h)), pl.BlockSpec((1, HEAD_DIM), lambda h, t: (0, h)),
                  pl.BlockSpec((tb, HEAD_DIM), col(0)), pl.BlockSpec((tb, HEAD_DIM), col(0)),
                  pl.BlockSpec((1, n_chunks, HEAD_DIM, HEAD_DIM), lambda h, t: (h, nb - 1 - t, 0, 0))],
        out_specs=[pl.BlockSpec((tb, HEAD_DIM), col(0))] * 4 + [pl.BlockSpec((8, HEAD_DIM), lambda h, t: (0, h))],
        scratch_shapes=[pltpu.VMEM((HEAD_DIM, HEAD_DIM), F32)],
        params=_params(32, ("arbitrary", "arbitrary")), exchange=exchange,
        first=lambda: (pl.program_id(0) == 0) & (pl.program_id(1) == 0),
        last=lambda: (pl.program_id(0) == N_HEADS - 1) & (pl.program_id(1) == nb - 1))


ATT_LOG2 = ATT_SCALE * 1.4426950408889634


def _triangle_steps(nq, q_major):
    if q_major:
        pairs = [(i, j) for i in range(nq) for j in range(i + 1)]
    else:
        pairs = [(i, j) for j in range(nq) for i in range(j, nq)]
    return jnp.array([p[0] for p in pairs], jnp.int32), jnp.array([p[1] for p in pairs], jnp.int32)


def _key_le_query(t):
    return lax.broadcasted_iota(jnp.int32, (t, t), 0) <= lax.broadcasted_iota(jnp.int32, (t, t), 1)


def _attention_forward(q, k, v_t, exchange=None):
    t_len = q.shape[1]
    tq = min(512, t_len)
    nq = t_len // tq
    qi_tab, ki_tab = _triangle_steps(nq, True)

    def body(qi_ref, ki_ref, q_ref, k_ref, vt_ref, o_ref, lse_ref, m_s, l_s, acc_s):
        step = pl.program_id(0)
        qi, ki = qi_ref[step], ki_ref[step]

        @pl.when(ki == 0)
        def _():
            m_s[...] = jnp.full_like(m_s, NEG_BIG)
            l_s[...] = jnp.zeros_like(l_s)
            acc_s[...] = jnp.zeros_like(acc_s)

        def accumulate(masked):
            for h in range(N_HEADS):
                s_t = _dot_nt(k_ref[h], q_ref[h]) * ATT_LOG2
                if masked:
                    s_t = jnp.where(_key_le_query(tq), s_t, NEG_BIG)
                m_old = m_s[h]
                m_new = jnp.maximum(m_old, jnp.max(s_t, axis=0, keepdims=True))
                alpha = jnp.exp2(m_old - m_new)
                p_t = jnp.exp2(s_t - m_new)
                l_s[h] = alpha * l_s[h] + jnp.sum(p_t, axis=0, keepdims=True)
                acc_s[h] = alpha * acc_s[h] + _dot(vt_ref[h], p_t.astype(BF16))
                m_s[h] = m_new

        @pl.when(ki < qi)
        def _():
            accumulate(False)

        @pl.when(ki == qi)
        def _():
            accumulate(True)
            for h in range(N_HEADS):
                o_ref[:, h * HEAD_DIM:(h + 1) * HEAD_DIM] = jnp.transpose(acc_s[h] / l_s[h])
                lse_ref[h] = m_s[h] + jnp.log2(l_s[h])

    n_steps = qi_tab.shape[0]
    return _pallas(
        body, name="attention_forward", grid=(n_steps,), prefetch=(qi_tab, ki_tab), operands=(q, k, v_t),
        out_shape=[jax.ShapeDtypeStruct((t_len, N_HEADS * HEAD_DIM), F32),
                   jax.ShapeDtypeStruct((N_HEADS, 1, t_len), F32)],
        in_specs=[pl.BlockSpec((N_HEADS, tq, QK_DIM), lambda s, qt, kt: (0, qt[s], 0)),
                  pl.BlockSpec((N_HEADS, tq, QK_DIM), lambda s, qt, kt: (0, kt[s], 0)),
                  pl.BlockSpec((N_HEADS, HEAD_DIM, tq), lambda s, qt, kt: (0, 0, kt[s]))],
        out_specs=[pl.BlockSpec((tq, N_HEADS * HEAD_DIM), lambda s, qt, kt: (qt[s], 0)),
                   pl.BlockSpec((N_HEADS, 1, tq), lambda s, qt, kt: (0, 0, qt[s]))],
        scratch_shapes=[pltpu.VMEM((N_HEADS, 1, tq), F32), pltpu.VMEM((N_HEADS, 1, tq), F32),
                        pltpu.VMEM((N_HEADS, HEAD_DIM, tq), F32)],
        params=_params(48, ("arbitrary",)), exchange=exchange,
        first=lambda qt, kt: pl.program_id(0) == 0, last=lambda qt, kt: pl.program_id(0) == n_steps - 1)


BWD_HEADS = 2


def _attention_backward(q, k, k_t, v, d_cat, lse, delta, exchange=None):
    t_len = q.shape[1]
    tq = min(512, t_len)
    nq = t_len // tq
    hp = BWD_HEADS
    qi_tab, ki_tab = _triangle_steps(nq, False)

    def body(qi_ref, ki_ref, q_ref, k_ref, kt_ref, v_ref, do_ref, lse_ref, delta_ref, dqt_hbm, dk_ref, dv_ref,
             dqt_s, dk_s, dv_s):
        group, step = pl.program_id(0), pl.program_id(1)
        qi, ki = qi_ref[step], ki_ref[step]

        @pl.when(step == 0)
        def _():
            dqt_s[...] = jnp.zeros_like(dqt_s)

        @pl.when(qi == ki)
        def _():
            dk_s[...] = jnp.zeros_like(dk_s)
            dv_s[...] = jnp.zeros_like(dv_s)

        def accumulate(masked):
            for h in range(hp):
                do_b = do_ref[:, h * HEAD_DIM:(h + 1) * HEAD_DIM].astype(BF16)
                s_t = _dot_nt(k_ref[h], q_ref[h]) * ATT_LOG2
                if masked:
                    s_t = jnp.where(_key_le_query(tq), s_t, NEG_BIG)
                p_t = jnp.exp2(s_t - lse_ref[h])
                dp_t = _dot_nt(v_ref[h], do_b)
                ds_t = (p_t * (dp_t - delta_ref[h]) * ATT_SCALE).astype(BF16)
                dv_s[h] += _dot(p_t.astype(BF16), do_b)
                dk_s[h] += _dot(ds_t, q_ref[h])
                dqt_s[h, qi] += _dot(kt_ref[h], ds_t)

        @pl.when(ki < qi)
        def _():
            accumulate(False)

        @pl.when(ki == qi)
        def _():
            accumulate(True)
            for h in range(hp):
                pltpu.sync_copy(dqt_s.at[h, qi], dqt_hbm.at[group * hp + h, qi])

        @pl.when(qi == nq - 1)
        def _():
            dk_ref[...] = dk_s[...]
            dv_ref[...] = dv_s[...]

    wide = hp * HEAD_DIM
    n_groups, n_steps = N_HEADS // hp, qi_tab.shape[0]
    return _pallas(
        body, name="attention_backward", grid=(n_groups, n_steps), prefetch=(qi_tab, ki_tab),
        operands=(q, k, k_t, v, d_cat, lse, delta),
        out_shape=[jax.ShapeDtypeStruct((N_HEADS, nq, QK_DIM, tq), F32),
                   jax.ShapeDtypeStruct((N_HEADS, t_len, QK_DIM), F32),
                   jax.ShapeDtypeStruct((N_HEADS, t_len, HEAD_DIM), F32)],
        in_specs=[pl.BlockSpec((hp, tq, QK_DIM), lambda g, s, qt, kt: (g, qt[s], 0)),
                  pl.BlockSpec((hp, tq, QK_DIM), lambda g, s, qt, kt: (g, kt[s], 0)),
                  pl.BlockSpec((hp, QK_DIM, tq), lambda g, s, qt, kt: (g, 0, kt[s])),
                  pl.BlockSpec((hp, tq, HEAD_DIM), lambda g, s, qt, kt: (g, kt[s], 0)),
                  pl.BlockSpec((tq, wide), lambda g, s, qt, kt: (qt[s], n_groups + g)),
                  pl.BlockSpec((hp, 1, tq), lambda g, s, qt, kt: (g, 0, qt[s])),
                  pl.BlockSpec((hp, 1, tq), lambda g, s, qt, kt: (g, 0, qt[s]))],
        out_specs=[pl.BlockSpec(memory_space=pl.ANY),
                   pl.BlockSpec((hp, tq, QK_DIM), lambda g, s, qt, kt: (g, kt[s], 0)),
                   pl.BlockSpec((hp, tq, HEAD_DIM), lambda g, s, qt, kt: (g, kt[s], 0))],
        scratch_shapes=[pltpu.VMEM((hp, nq, QK_DIM, tq), F32), pltpu.VMEM((hp, tq, QK_DIM), F32),
                        pltpu.VMEM((hp, tq, HEAD_DIM), F32)],
        params=_params(48, ("arbitrary", "arbitrary")), exchange=exchange,
        first=lambda qt, kt: (pl.program_id(0) == 0) & (pl.program_id(1) == 0),
        last=lambda qt, kt: (pl.program_id(0) == n_groups - 1) & (pl.program_id(1) == n_steps - 1))


def _out_project(o_hg, o_mla, x, g_a, w_out):
    t_len = x.shape[0]
    tm = min(512, t_len)
    half = N_HEADS * HEAD_DIM

    def body(ohg_ref, omla_ref, x_ref, ga_ref, w_ref, cat_ref, mix_ref, xhat_ref, rstd_ref):
        a = ohg_ref[...].astype(BF16)
        b = omla_ref[...].astype(BF16)
        cat_ref[:, 0:half] = a
        cat_ref[:, half:2 * half] = b
        mix = _dot(a, w_ref[0:half, :]) + _dot(b, w_ref[half:2 * half, :])
        mix_ref[...] = mix
        r1 = DN_ALPHA * x_ref[...] + (1.0 + ga_ref[...]) * mix
        xc = r1 - _rowmean(r1)
        rstd = lax.rsqrt(_rowmean(xc * xc) + LN_EPS)
        xhat_ref[...] = xc * rstd
        rstd_ref[...] = rstd

    row = lambda i: (i, 0)
    fixed = lambda i: (0, 0)
    return _pcall(
        body, name="out_project", grid=(t_len // tm,),
        out_shape=[jax.ShapeDtypeStruct((t_len, D_MODEL), BF16), jax.ShapeDtypeStruct((t_len, D_MODEL), F32),
                   jax.ShapeDtypeStruct((t_len, D_MODEL), F32), jax.ShapeDtypeStruct((t_len, 1), F32)],
        in_specs=[pl.BlockSpec((tm, half), row), pl.BlockSpec((tm, half), row), pl.BlockSpec((tm, D_MODEL), row),
                  pl.BlockSpec((1, D_MODEL), fixed), pl.BlockSpec((D_MODEL, D_MODEL), fixed)],
        out_specs=[pl.BlockSpec((tm, D_MODEL), row), pl.BlockSpec((tm, D_MODEL), row),
                   pl.BlockSpec((tm, D_MODEL), row), pl.BlockSpec((tm, 1), row)],
        compiler_params=_params(48, ("arbitrary",)),
        operands=(o_hg, o_mla, x, g_a, w_out))


V_LN1G, V_LN1B, V_SCM, V_SHM, V_GM, V_GA, V_LN2G, V_LN2B = range(8)
S_DLN2G, S_DLN2B, S_DGM, S_DSCM, S_DSHM, S_DLN1G, S_DLN1B, S_DGA, S_LOSS = range(9)


def _mlp_and_back(xhat1, rstd1, mix, target, o_mla, vecs, w1, w2, w_out):
    t_len = xhat1.shape[0]
    tm = min(256, t_len)
    n_ff = w1.shape[0]
    ff = w1.shape[2]

    def body(xhat_ref, rstd_ref, mix_ref, tgt_ref, omla_ref, vec_ref, w1_hbm, w2_hbm, wout_hbm,
             act_ref, dhp_ref, um_ref, dh_ref, dmix_ref, dcat_ref, dr1_ref, sums_ref, delta_ref,
             w1_s, w2_s, wout_s, hp_s, load_sems):
        @pl.when(pl.program_id(0) == 0)
        def _():
            loads = [pltpu.make_async_copy(w1_hbm, w1_s, load_sems.at[0]),
                     pltpu.make_async_copy(w2_hbm, w2_s, load_sems.at[1]),
                     pltpu.make_async_copy(wout_hbm, wout_s, load_sems.at[2])]
            for cp in loads:
                cp.start()
            sums_ref[...] = jnp.zeros_like(sums_ref)
            for cp in loads:
                cp.wait()

        vec = lambda r: vec_ref[r:r + 1, :]
        xhat = xhat_ref[...]
        x1 = xhat * vec(V_LN1G) + vec(V_LN1B)
        um = (x1 * (1.0 + vec(V_SCM)) + vec(V_SHM)).astype(BF16)
        um_ref[...] = um
        h = jnp.zeros((tm, D_MODEL), F32)
        for j in range(n_ff):
            hp = _dot(um, w1_s[j])
            hp_s[j] = hp
            act = jnp.square(jnp.maximum(hp, 0.0)).astype(BF16)
            act_ref[:, j * ff:(j + 1) * ff] = act
            h = h + _dot(act, w2_s[j])
        r2 = DN_ALPHA * x1 + (1.0 + vec(V_GM)) * h
        xc = r2 - _rowmean(r2)
        rstd2 = lax.rsqrt(_rowmean(xc * xc) + LN_EPS)
        xhat2 = xc * rstd2
        err = xhat2 * vec(V_LN2G) + vec(V_LN2B) - tgt_ref[...]
        loss = 0.5 * jnp.sum(_rowmean(err * err))
        dy = err * (1.0 / D_MODEL)
        dxh = dy * vec(V_LN2G)
        dr2 = rstd2 * (dxh - _rowmean(dxh) - xhat2 * _rowmean(dxh * xhat2))
        dh = ((1.0 + vec(V_GM)) * dr2).astype(BF16)
        dh_ref[...] = dh
        sums_ref[S_DLN2G:S_DLN2G + 1, :] += _colsum(dy * xhat2)
        sums_ref[S_DLN2B:S_DLN2B + 1, :] += _colsum(dy)
        sums_ref[S_DGM:S_DGM + 1, :] += _colsum(dr2 * h)
        sums_ref[S_LOSS:S_LOSS + 1, :] += jnp.full((1, D_MODEL), loss, F32)
        du = jnp.zeros((tm, D_MODEL), F32)
        for j in range(n_ff):
            dhp = (_dot_nt(dh, w2_s[j]) * (2.0 * jnp.maximum(hp_s[j], 0.0))).astype(BF16)
            dhp_ref[:, j * ff:(j + 1) * ff] = dhp
            du = du + _dot_nt(dhp, w1_s[j])
        sums_ref[S_DSCM:S_DSCM + 1, :] += _colsum(du * x1)
        sums_ref[S_DSHM:S_DSHM + 1, :] += _colsum(du)
        dx1 = DN_ALPHA * dr2 + du * (1.0 + vec(V_SCM))
        sums_ref[S_DLN1G:S_DLN1G + 1, :] += _colsum(dx1 * xhat)
        sums_ref[S_DLN1B:S_DLN1B + 1, :] += _colsum(dx1)
        dxh1 = dx1 * vec(V_LN1G)
        dr1 = rstd_ref[...] * (dxh1 - _rowmean(dxh1) - xhat * _rowmean(dxh1 * xhat))
        dr1_ref[...] = dr1
        sums_ref[S_DGA:S_DGA + 1, :] += _colsum(dr1 * mix_ref[...])
        dmix = ((1.0 + vec(V_GA)) * dr1).astype(BF16)
        dmix_ref[...] = dmix
        dcat = _dot_nt(dmix, wout_s[...])
        dcat_ref[...] = dcat
        ones = jnp.ones((8, HEAD_DIM), F32)
        half = N_HEADS * HEAD_DIM
        for hd in range(N_HEADS):
            prod = dcat[:, half + hd * HEAD_DIM:half + (hd + 1) * HEAD_DIM] * omla_ref[:, hd * HEAD_DIM:(hd + 1) * HEAD_DIM]
            delta_ref[hd] = lax.dot_general(ones, prod, (((1,), (1,)), ((), ())), preferred_element_type=F32,
                                            precision=lax.Precision.HIGHEST)[0:1]

    row = lambda i: (i, 0)
    fixed = lambda i: (0, 0)
    any_spec = pl.BlockSpec(memory_space=pl.ANY)
    return _pcall(
        body, name="mlp_and_back", grid=(t_len // tm,),
        out_shape=[jax.ShapeDtypeStruct((t_len, D_FF), BF16), jax.ShapeDtypeStruct((t_len, D_FF), BF16),
                   jax.ShapeDtypeStruct((t_len, D_MODEL), BF16), jax.ShapeDtypeStruct((t_len, D_MODEL), BF16),
                   jax.ShapeDtypeStruct((t_len, D_MODEL), BF16), jax.ShapeDtypeStruct((t_len, D_MODEL), F32),
                   jax.ShapeDtypeStruct((t_len, D_MODEL), F32), jax.ShapeDtypeStruct((16, D_MODEL), F32),
                   jax.ShapeDtypeStruct((N_HEADS, 1, t_len), F32)],
        in_specs=[pl.BlockSpec((tm, D_MODEL), row), pl.BlockSpec((tm, 1), row), pl.BlockSpec((tm, D_MODEL), row),
                  pl.BlockSpec((tm, D_MODEL), row), pl.BlockSpec((tm, N_HEADS * HEAD_DIM), row),
                  pl.BlockSpec((8, D_MODEL), fixed), any_spec, any_spec, any_spec],
        out_specs=[pl.BlockSpec((tm, D_FF), row), pl.BlockSpec((tm, D_FF), row), pl.BlockSpec((tm, D_MODEL), row),
                   pl.BlockSpec((tm, D_MODEL), row), pl.BlockSpec((tm, D_MODEL), row), pl.BlockSpec((tm, D_MODEL), row),
                   pl.BlockSpec((tm, D_MODEL), row), pl.BlockSpec((16, D_MODEL), fixed),
                   pl.BlockSpec((N_HEADS, 1, tm), lambda i: (0, 0, i))],
        scratch_shapes=[pltpu.VMEM(w1.shape, BF16), pltpu.VMEM(w2.shape, BF16), pltpu.VMEM(w_out.shape, BF16),
                        pltpu.VMEM((n_ff, tm, ff), F32), pltpu.SemaphoreType.DMA((3,))],
        compiler_params=_params(56, ("arbitrary",)),
        operands=(xhat1, rstd1, mix, target, o_mla, vecs, w1, w2, w_out))


def _in_project_backward(dq, dk, dv, cq, ckv, pos, invf, q_norm_w, kv_norm_w, w_q, w_kv,
                         d_hq, d_hf, d_hi, d_hg, w_in, dr1, x, sc_a, exchange=None):
    t_len = x.shape[0]
    tm = min(256, t_len)
    per_q = dq.shape[3] // tm
    hgw = N_HEADS * HEAD_DIM

    def body(dq_ref, dk_ref, dv_ref, cq_ref, ckv_ref, pos_ref, invf_ref, qn_ref, kvn_ref, wq_ref, wkv_ref,
             dhq_ref, dhf_ref, dhi_ref, dhg_ref, win_ref, dr1_ref, x_ref, sc_ref,
             dz_ref, dqf_ref, dkvu_ref, cqn_ref, ckvn_ref, gx_ref, sums_ref):
        @pl.when(pl.program_id(0) == 0)
        def _():
            sums_ref[...] = jnp.zeros_like(sums_ref)

        cos_t, sin_t = _rope_tables(pos_ref[...], invf_ref[...])
        cq = cq_ref[...]
        ckv = ckv_ref[...]
        rq = lax.rsqrt(_rowmean(cq * cq) + RMS_EPS)
        rkv = lax.rsqrt(_rowmean(ckv * ckv) + RMS_EPS)
        cqn_ref[...] = (cq * rq * qn_ref[...]).astype(BF16)
        ckvn_ref[...] = (ckv * rkv * kvn_ref[...]).astype(BF16)
        d_cqn = jnp.zeros((tm, Q_RANK), F32)
        d_ckvn = jnp.zeros((tm, KV_RANK), F32)
        d_kpe = jnp.zeros((tm, 128), F32)
        for h in range(N_HEADS):
            dqh = jnp.transpose(dq_ref[h])
            dqf_ref[h, :, 0:HEAD_DIM] = dqh[:, :HEAD_DIM].astype(BF16)
            dqf_ref[h, :, HEAD_DIM:QK_DIM] = _unrope(dqh[:, HEAD_DIM:], cos_t, sin_t).astype(BF16)
            d_cqn = d_cqn + _dot_nt(dqf_ref[h], wq_ref[h])
            dkh = dk_ref[h]
            d_kpe = d_kpe + dkh[:, HEAD_DIM:]
            dkvu_ref[h, :, 0:HEAD_DIM] = dkh[:, :HEAD_DIM].astype(BF16)
            dkvu_ref[h, :, HEAD_DIM:2 * HEAD_DIM] = dv_ref[h].astype(BF16)
            d_ckvn = d_ckvn + _dot_nt(dkvu_ref[h], wkv_ref[h])
        dyq = d_cqn * qn_ref[...]
        dykv = d_ckvn * kvn_ref[...]
        sums_ref[2:3, 0:Q_RANK] += _colsum(d_cqn * cq * rq)
        sums_ref[3:4, 0:KV_RANK] += _colsum(d_ckvn * ckv * rkv)
        dz_ref[:, 0:hgw] = dhq_ref[...]
        dz_ref[:, hgw:2 * hgw] = dhf_ref[...]
        dz_ref[:, 2 * hgw:3 * hgw] = dhi_ref[...]
        dz_ref[:, 3 * hgw:4 * hgw] = dhg_ref[...]
        dz_ref[:, HG_COLS:HG_COLS + Q_RANK] = (rq * dyq - cq * (rq * rq * rq) * _rowmean(dyq * cq)).astype(BF16)
        dz_ref[:, HG_COLS + Q_RANK:HG_COLS + Q_RANK + KV_RANK] = (
            rkv * dykv - ckv * (rkv * rkv * rkv) * _rowmean(dykv * ckv)).astype(BF16)
        dz_ref[:, HG_COLS + Q_RANK + KV_RANK:] = _unrope(d_kpe, cos_t, sin_t).astype(BF16)
        du = _dot_nt(dz_ref[...], win_ref[...])
        xv = x_ref[...]
        gx_ref[...] = DN_ALPHA * dr1_ref[...] + (1.0 + sc_ref[...]) * du
        sums_ref[0:1, :] += _colsum(du * xv)
        sums_ref[1:2, :] += _colsum(du)

    row = lambda i: (i, 0)
    fixed2 = lambda i: (0, 0)
    fixed3 = lambda i: (0, 0, 0)
    heads = lambda i: (0, i, 0)
    n_tiles = t_len // tm
    return _pallas(
        body, name="in_project_backward", grid=(n_tiles,),
        operands=(dq, dk, dv, cq, ckv, pos, invf, q_norm_w, kv_norm_w, w_q, w_kv, d_hq, d_hf, d_hi, d_hg, w_in, dr1, x,
                  sc_a),
        out_shape=[jax.ShapeDtypeStruct((t_len, IN_COLS_PAD), BF16), jax.ShapeDtypeStruct((N_HEADS, t_len, QK_DIM), BF16),
                   jax.ShapeDtypeStruct((N_HEADS, t_len, 2 * HEAD_DIM), BF16), jax.ShapeDtypeStruct((t_len, Q_RANK), BF16),
                   jax.ShapeDtypeStruct((t_len, KV_RANK), BF16), jax.ShapeDtypeStruct((t_len, D_MODEL), F32),
                   jax.ShapeDtypeStruct((8, D_MODEL), F32)],
        in_specs=[pl.BlockSpec((N_HEADS, None, QK_DIM, tm), lambda i: (0, i // per_q, 0, i % per_q)),
                  pl.BlockSpec((N_HEADS, tm, QK_DIM), heads),
                  pl.BlockSpec((N_HEADS, tm, HEAD_DIM), heads), pl.BlockSpec((tm, Q_RANK), row),
                  pl.BlockSpec((tm, KV_RANK), row), pl.BlockSpec((tm, 1), row), pl.BlockSpec((1, 128), fixed2),
                  pl.BlockSpec((1, Q_RANK), fixed2), pl.BlockSpec((1, KV_RANK), fixed2),
                  pl.BlockSpec((N_HEADS, Q_RANK, QK_DIM), fixed3), pl.BlockSpec((N_HEADS, KV_RANK, 2 * HEAD_DIM), fixed3),
                  pl.BlockSpec((tm, hgw), row), pl.BlockSpec((tm, hgw), row), pl.BlockSpec((tm, hgw), row),
                  pl.BlockSpec((tm, hgw), row), pl.BlockSpec((D_MODEL, IN_COLS_PAD), fixed2),
                  pl.BlockSpec((tm, D_MODEL), row), pl.BlockSpec((tm, D_MODEL), row), pl.BlockSpec((1, D_MODEL), fixed2)],
        out_specs=[pl.BlockSpec((tm, IN_COLS_PAD), row), pl.BlockSpec((N_HEADS, tm, QK_DIM), heads),
                   pl.BlockSpec((N_HEADS, tm, 2 * HEAD_DIM), heads), pl.BlockSpec((tm, Q_RANK), row),
                   pl.BlockSpec((tm, KV_RANK), row), pl.BlockSpec((tm, D_MODEL), row), pl.BlockSpec((8, D_MODEL), fixed2)],
        params=_params(48, ("arbitrary",)), exchange=exchange,
        first=lambda: pl.program_id(0) == 0, last=lambda: pl.program_id(0) == n_tiles - 1)


def _weight_grad(a, b, name, n_blocks, bn, a_blocked=False, b_blocked=True, exchange=None, token_tile=512):
    t_len = a.shape[0]
    m = a.shape[1] // n_blocks if a_blocked else a.shape[1]
    bt = min(token_tile, t_len)

    def body(a_ref, b_ref, o_ref):
        @pl.when(pl.program_id(1) == 0)
        def _():
            o_ref[...] = jnp.zeros_like(o_ref)

        o_ref[...] += _dot_tn(a_ref[...].astype(BF16), b_ref[...].astype(BF16))

    a_spec = pl.BlockSpec((bt, m), (lambda n, t: (t, n)) if a_blocked else (lambda n, t: (t, 0)))
    if b.ndim == 3:
        b_spec = pl.BlockSpec((None, bt, bn), lambda n, t: (n, t, 0))
    else:
        b_spec = pl.BlockSpec((bt, bn), (lambda n, t: (t, n)) if b_blocked else (lambda n, t: (t, 0)))
    nt = t_len // bt
    (out,), landed = _pallas(
        body, name=name, grid=(n_blocks, nt), operands=(a, b),
        out_shape=[jax.ShapeDtypeStruct((n_blocks, m, bn), F32)],
        in_specs=[a_spec, b_spec],
        out_specs=[pl.BlockSpec((None, m, bn), lambda n, t: (n, 0, 0))],
        params=_params(40, ("arbitrary", "arbitrary")), exchange=exchange,
        first=lambda: (pl.program_id(0) == 0) & (pl.program_id(1) == 0),
        last=lambda: (pl.program_id(0) == n_blocks - 1) & (pl.program_id(1) == nt - 1))
    return (out, landed) if exchange else out


def _reduce_small(gathered, lb_raw):
    def body(g_ref, lb_ref, tot_ref, dlb_ref):
        tot = g_ref[0]
        for d in range(1, N_DEV):
            tot = tot + g_ref[d]
        tot_ref[...] = tot
        a = lb_ref[...]
        m = jnp.max(a, axis=0, keepdims=True)
        e = jnp.exp(a - m)
        lb = e[0:1] / jnp.sum(e, axis=0, keepdims=True)
        d0 = tot[10:11, 0:512] * lb * (1.0 - lb)
        dlb_ref[0:1, :] = d0
        dlb_ref[1:2, :] = -d0

    return pl.pallas_call(
        body, name="reduce_small",
        out_shape=[jax.ShapeDtypeStruct((SMALL_ROWS, D_MODEL), F32), jax.ShapeDtypeStruct((2, 512), F32)],
    )(gathered, lb_raw)


def _adamw_update(w, gv, m, v):
    nm = ADAM_B1 * m + (1.0 - ADAM_B1) * gv
    nv = ADAM_B2 * v + (1.0 - ADAM_B2) * jnp.square(gv)
    m_hat = nm / (1.0 - ADAM_B1 ** ADAM_STEP)
    v_hat = nv / (1.0 - ADAM_B2 ** ADAM_STEP)
    return -ADAM_LR * (m_hat / (jnp.sqrt(v_hat) + ADAM_EPS) + ADAM_WD * w), nm, nv


def _adamw_halves(core, w, mine, theirs, m, v, name):
    rows, cols = w.shape
    h = rows // 2
    tr = _row_tile(h)
    per_half = h // tr

    def body(core_ref, w_ref, mine_ref, theirs_ref, m_ref, v_ref, g_ref, d_ref, nm_ref, nv_ref):
        is_mine = pl.program_id(0) // per_half == core_ref[0]
        gv = jnp.where(is_mine, mine_ref[...], theirs_ref[...])
        g_ref[...] = gv
        d_ref[...], nm_ref[...], nv_ref[...] = _adamw_update(w_ref[...], gv, m_ref[...], v_ref[...])

    full = pl.BlockSpec((tr, cols), lambda i, core_ref: (i, 0))
    part = pl.BlockSpec((tr, cols), lambda i, core_ref: (i % per_half, 0))
    return _pcall(
        body, name=name, out_shape=[jax.ShapeDtypeStruct(w.shape, F32)] * 4,
        grid_spec=pltpu.PrefetchScalarGridSpec(
            num_scalar_prefetch=1, grid=(rows // tr,), in_specs=[full, part, part, full, full], out_specs=[full] * 4),
        compiler_params=_params(40, ("arbitrary",)),
        operands=(core, w, mine, theirs, m, v))


def _adamw(w, g, m, v, name):
    rows, cols = w.shape
    tr = _row_tile(rows) if rows >= 8 else rows

    def body(w_ref, g_ref, m_ref, v_ref, d_ref, nm_ref, nv_ref):
        d_ref[...], nm_ref[...], nv_ref[...] = _adamw_update(w_ref[...], g_ref[...], m_ref[...], v_ref[...])

    spec = pl.BlockSpec((tr, cols), lambda i: (i, 0))
    return _pcall(
        body, name=name, grid=(rows // tr,),
        out_shape=[jax.ShapeDtypeStruct(w.shape, F32)] * 3,
        in_specs=[spec] * 4, out_specs=[spec] * 3,
        compiler_params=_params(40, ("arbitrary",)),
        operands=(w, g, m, v))


def kernel(x, c, positions, w_ada, b_ada, w_in, hg_lower_bounds, hg_norm_w, mla_q_norm_w, w_q_up, mla_kv_norm_w, w_kv_up, w_out, ln1_g, ln1_b, w_mlp_in, w_mlp_out, ln2_g, ln2_b, loss_target, m_w_ada, m_b_ada, m_w_in, m_hg_lower_bounds, m_hg_norm_w, m_mla_q_norm_w, m_w_q_up, m_mla_kv_norm_w, m_w_kv_up, m_w_out, m_ln1_g, m_ln1_b, m_w_mlp_in, m_w_mlp_out, m_ln2_g, m_ln2_b, v_w_ada, v_b_ada, v_w_in, v_hg_lower_bounds, v_hg_norm_w, v_mla_q_norm_w, v_w_q_up, v_mla_kv_norm_w, v_w_kv_up, v_w_out, v_ln1_g, v_ln1_b, v_w_mlp_in, v_w_mlp_out, v_ln2_g, v_ln2_b):
    ix, iy, ic = _mesh_pos()
    chip = 2 * ix + iy
    me = 4 * ix + 2 * iy + ic
    core_arr = jnp.reshape(ic, (1,)).astype(jnp.int32)
    chip_arr = jnp.reshape(chip, (1,)).astype(jnp.int32)

    xs = x[0]
    target = loss_target[0]
    t_len = xs.shape[0]
    pos = positions.astype(F32).reshape(t_len, 1)
    inv = 1.0 / (ROPE_THETA ** (jnp.arange(0, ROPE_DIM, 2, dtype=F32) / ROPE_DIM))
    invf = jnp.concatenate([inv, inv, jnp.zeros((128 - ROPE_DIM,), F32)]).reshape(1, 128)

    ada_cols = w_ada.shape[2]
    c_all = _allgather8(jnp.broadcast_to(c, (8, D_MODEL)), "gather_c")[:, 0, :]
    b_shard = lax.dynamic_slice(b_ada, (0, chip * ada_cols), (1, ada_cols))
    mod_cols, cond16 = _ada_project(c_all, w_ada[0], b_shard)
    mod_all = _allgather8(mod_cols, "gather_mod")
    mod_mine = lax.dynamic_slice(mod_all, (0, me, 0), (N_DEV, 1, ada_cols))[::2, 0, :]
    mod_mine = mod_mine.reshape(6, D_MODEL)
    sh_a, sc_a, g_a, sh_m, sc_m, g_m = (mod_mine[i:i + 1] for i in range(6))

    def slot(w):
        rows, cols = w.shape
        own = w.astype(BF16).reshape(1, 2, rows // 2, cols)
        return lax.dynamic_update_slice(jnp.zeros((N_CHIPS, 2, rows // 2, cols), BF16), own, (chip, 0, 0, 0))

    def whole(s):
        return s.reshape(N_CHIPS, 2 * s.shape[2], s.shape[3])

    def halved(g):
        return g.reshape(N_CHIPS, 2, g.shape[1] // 2, g.shape[2])

    early = _run_exchange(_gather_over_ici([slot(w_in[0]), slot(w_q_up[0]), slot(w_kv_up[0])]),
                          "gather_mixer_weights_ici")
    g_in, g_q, g_kv = (whole(s) for s in _run_exchange(_gather_over_d2d(early), "gather_mixer_weights_d2d"))
    w_in_full = jnp.transpose(g_in, (1, 0, 2)).reshape(D_MODEL, IN_COLS)
    w_in_full = jnp.pad(w_in_full, ((0, 0), (0, IN_COLS_PAD - IN_COLS)))
    w_q_full = jnp.pad(g_q, ((0, 0), (0, 0), (0, QK_DIM - g_q.shape[2])))

    u_a, zhg, cq, ckv, q, k, k_t, v, v_t = _in_project(xs, pos, sc_a, sh_a, w_in_full, mla_q_norm_w, mla_kv_norm_w,
                                                      w_q_full, g_kv, invf)
    (o_pre, o_hg, states), mlp_slots = _hgrn_forward(
        zhg, hg_lower_bounds, hg_norm_w, _gather_over_ici([slot(w_mlp_in[0]), slot(w_mlp_out[0]), slot(w_out[0])]))
    (o_mla, lse), mlp_slots = _attention_forward(q, k, v_t, _gather_over_d2d(mlp_slots))
    g_w1, g_w2, g_out = (whole(s) for s in mlp_slots)
    w_out_full = g_out.reshape(D_MODEL, D_MODEL)
    cat, mix, xhat1, rstd1 = _out_project(o_hg, o_mla, xs, g_a, w_out_full)
    vecs = jnp.concatenate([ln1_g, ln1_b, sc_m, sh_m, g_m, g_a, ln2_g, ln2_b], axis=0)
    act, dhp, um, dh, dmix, d_cat, dr1, mlp_sums, delta = _mlp_and_back(
        xhat1, rstd1, mix, target, o_mla, vecs, g_w1, g_w2, w_out_full)

    gw_1 = _weight_grad(um, dhp, "grad_w_mlp_in", N_CHIPS, D_FF // N_CHIPS)
    gw_2 = _weight_grad(act, dh, "grad_w_mlp_out", N_CHIPS, D_MODEL, a_blocked=True, b_blocked=False)
    gw_out = _weight_grad(cat, dmix, "grad_w_out", 1, D_MODEL).reshape(N_CHIPS, D_MODEL // N_CHIPS, D_MODEL)
    mlp_grads = [halved(gw_1), halved(gw_2), halved(gw_out)]
    (dq, dk, dv), landed = _attention_backward(q, k, k_t, v, d_cat, lse, delta, _pair_exchange(mlp_grads))
    chip_sums = [_add_pair(core_arr, g, l) for g, l in zip(mlp_grads, landed)]
    (d_hq, d_hf, d_hi, d_hg, hg_sums), landed = _hgrn_backward(
        zhg, hg_lower_bounds, hg_norm_w, o_pre, d_cat, states, _chip_exchange([b for _, b in chip_sums]))
    mlp_mine = [_add_chips(chip_arr, p, l) for (p, _), l in zip(chip_sums, landed)]
    (dz, dqf, dkvu, cqn, ckvn, grad_x, in_sums), _ = _in_project_backward(
        dq, dk, dv, cq, ckv, pos, invf, mla_q_norm_w, mla_kv_norm_w, w_q_full, g_kv,
        d_hq, d_hf, d_hi, d_hg, w_in_full, dr1, xs, sc_a)

    gw_in, mlp_theirs = _weight_grad(u_a, dz, "grad_w_in", 3, IN_COLS_PAD // 3, exchange=_pair_send(mlp_mine))
    gw_in = jnp.transpose(gw_in, (1, 0, 2)).reshape(D_MODEL, IN_COLS_PAD)[:, :IN_COLS]
    gw_in = jnp.transpose(gw_in.reshape(D_MODEL, N_CHIPS, IN_COLS // N_CHIPS), (1, 0, 2))
    gw_q = _weight_grad(cqn, dqf, "grad_w_q_up", N_HEADS, QK_DIM, token_tile=2048)[:, :, :HEAD_DIM + ROPE_DIM]
    gw_kv = _weight_grad(ckvn, dkvu, "grad_w_kv_up", N_HEADS, 2 * HEAD_DIM, token_tile=2048)
    mixer_grads = [halved(g) for g in (gw_in, gw_q, gw_kv)]
    landed = _run_exchange(_pair_exchange(mixer_grads), "grad_pair_exchange")
    chip_sums = [_add_pair(core_arr, g, l) for g, l in zip(mixer_grads, landed)]
    landed = _run_exchange(_chip_exchange([b for _, b in chip_sums]), "grad_chip_exchange")
    mixer_mine = [_add_chips(chip_arr, p, l) for (p, _), l in zip(chip_sums, landed)]
    mixer_theirs = _run_exchange(_pair_send(mixer_mine), "grad_pair_send")
    reduced = ("w_in", "w_q_up", "w_kv_up", "w_mlp_in", "w_mlp_out", "w_out")
    halves_mine = dict(zip(reduced, mixer_mine + mlp_mine))
    halves_theirs = dict(zip(reduced, list(mixer_theirs) + list(mlp_theirs)))

    zeros = lambda n: jnp.zeros((1, n), F32)
    small = jnp.concatenate([
        in_sums[1:2], in_sums[0:1], mlp_sums[S_DGA:S_DGA + 1],
        mlp_sums[S_DSHM:S_DSHM + 1], mlp_sums[S_DSCM:S_DSCM + 1], mlp_sums[S_DGM:S_DGM + 1],
        mlp_sums[S_DLN1G:S_DLN1G + 1], mlp_sums[S_DLN1B:S_DLN1B + 1],
        mlp_sums[S_DLN2G:S_DLN2G + 1], mlp_sums[S_DLN2B:S_DLN2B + 1],
        jnp.concatenate([hg_sums[0:1], hg_sums[1:2]], axis=1),
        jnp.concatenate([in_sums[2:3, :Q_RANK], in_sums[3:4, :KV_RANK], zeros(D_MODEL - Q_RANK - KV_RANK)], axis=1),
        mlp_sums[S_LOSS:S_LOSS + 1],
        jnp.zeros((SMALL_ROWS - 13, D_MODEL), F32)], axis=0)
    small_all = _allgather8(small, "gather_small")
    tot, g_lb = _reduce_small(small_all, hg_lower_bounds)
    loss = tot[12, 0]
    g_b_ada = tot[0:6].reshape(1, 6 * D_MODEL)
    g_ln1_g, g_ln1_b, g_ln2_g, g_ln2_b = tot[6:7], tot[7:8], tot[8:9], tot[9:10]
    g_hg_norm = tot[10:11, 512:1024]
    g_q_norm = tot[11:12, 0:Q_RANK]
    g_kv_norm = tot[11:12, Q_RANK:Q_RANK + KV_RANK]

    d_mod_all = small_all[:, 0:6, :].reshape(N_DEV, 6 * D_MODEL)
    d_mod_cols = lax.dynamic_slice(d_mod_all, (0, chip * ada_cols), (N_DEV, ada_cols))
    d_mod_cols = jnp.concatenate([d_mod_cols, jnp.zeros_like(d_mod_cols)], axis=0)
    g_w_ada = _weight_grad(cond16, d_mod_cols, "grad_w_ada", 1, ada_cols)[0]

    names = ["w_ada", "b_ada", "w_in", "hg_lower_bounds", "hg_norm_w", "mla_q_norm_w", "w_q_up", "mla_kv_norm_w",
             "w_kv_up", "w_out", "ln1_g", "ln1_b", "w_mlp_in", "w_mlp_out", "ln2_g", "ln2_b"]
    weights = [w_ada, b_ada, w_in, hg_lower_bounds, hg_norm_w, mla_q_norm_w, w_q_up, mla_kv_norm_w,
               w_kv_up, w_out, ln1_g, ln1_b, w_mlp_in, w_mlp_out, ln2_g, ln2_b]
    moms = [m_w_ada, m_b_ada, m_w_in, m_hg_lower_bounds, m_hg_norm_w, m_mla_q_norm_w, m_w_q_up, m_mla_kv_norm_w,
            m_w_kv_up, m_w_out, m_ln1_g, m_ln1_b, m_w_mlp_in, m_w_mlp_out, m_ln2_g, m_ln2_b]
    vels = [v_w_ada, v_b_ada, v_w_in, v_hg_lower_bounds, v_hg_norm_w, v_mla_q_norm_w, v_w_q_up, v_mla_kv_norm_w,
            v_w_kv_up, v_w_out, v_ln1_g, v_ln1_b, v_w_mlp_in, v_w_mlp_out, v_ln2_g, v_ln2_b]
    grads2d = [g_w_ada, g_b_ada, None, g_lb, g_hg_norm, g_q_norm, None, g_kv_norm,
               None, None, g_ln1_g, g_ln1_b, None, None, g_ln2_g, g_ln2_b]
    out_g, out_d, out_m, out_v = [], [], [], []
    for name, w, g, m, vv in zip(names, weights, grads2d, moms, vels):
        if g is None:
            shape2 = w.shape[1:]
            g, d, nm, nv = _adamw_halves(core_arr, w.reshape(shape2), halves_mine[name], halves_theirs[name],
                                         m.reshape(shape2), vv.reshape(shape2), "adamw_" + name)
        else:
            shape2 = g.shape
            d, nm, nv = _adamw(w.reshape(shape2), g, m.reshape(shape2), vv.reshape(shape2), "adamw_" + name)
        out_g.append(g.reshape(w.shape))
        out_d.append(d.reshape(w.shape))
        out_m.append(nm.reshape(w.shape))
        out_v.append(nv.reshape(w.shape))
    return (loss, grad_x[None], *out_g, *out_d, *out_m, *out_v)
```

```python
import functools

import jax
import jax.numpy as jnp
from jax import lax
from jax.experimental import pallas as pl
from jax.experimental.pallas import tpu as pltpu

F32 = jnp.float32
BF16 = jnp.bfloat16
MESH_IDS = pl.DeviceIdType.MESH

D_MODEL = 1024
N_HEADS = 4
HEAD_DIM = 128
ROPE_DIM = 64
HG_CHUNK = 64
HG_COLS = 2048
Q_RANK = 256
KV_RANK = 256
IN_COLS = 2624
IN_COLS_PAD = 2688
QK_DIM = 256
D_FF = 4096
N_CHIPS = 4
N_DEV = 8
ROPE_THETA = 10000.0
RMS_EPS = 1e-6
LN_EPS = 1e-5
DN_ALPHA = 2.0 ** 0.25
ATT_SCALE = (HEAD_DIM + ROPE_DIM) ** -0.5
NEG_BIG = -1e30
ADAM_LR = 0.001
ADAM_B1 = 0.9
ADAM_B2 = 0.999
ADAM_EPS = 1e-08
ADAM_WD = 0.01
ADAM_STEP = 10
SMALL_ROWS = 16
MIB = 1024 * 1024


def _dot(a, b):
    return jnp.dot(a, b, preferred_element_type=F32)


def _dot_nt(a, b):
    return lax.dot_general(a, b, (((1,), (1,)), ((), ())), preferred_element_type=F32)


def _dot_tn(a, b):
    return lax.dot_general(a, b, (((0,), (0,)), ((), ())), preferred_element_type=F32)


def _params(vmem_mib, semantics=None):
    return pltpu.CompilerParams(vmem_limit_bytes=vmem_mib * MIB, dimension_semantics=semantics)


def _sigmoid(v):
    return 1.0 / (1.0 + jnp.exp(-v))


def _colsum(v):
    return jnp.sum(v, axis=0, keepdims=True)


def _rowmean(v):
    return jnp.mean(v, axis=-1, keepdims=True)


def _rope_tables(pos, invf):
    ang = pos * invf
    lane = lax.broadcasted_iota(jnp.int32, ang.shape, 1)
    cos_t = jnp.where(lane < ROPE_DIM, jnp.cos(ang), 0.0)
    sin = jnp.sin(ang)
    sin_t = jnp.where(lane < ROPE_DIM // 2, -sin, jnp.where(lane < ROPE_DIM, sin, 0.0))
    return cos_t, sin_t


def _swap_halves(t):
    lane = lax.broadcasted_iota(jnp.int32, t.shape, 1)
    return jnp.where(lane < ROPE_DIM // 2, pltpu.roll(t, 128 - ROPE_DIM // 2, 1), pltpu.roll(t, ROPE_DIM // 2, 1))


def _rope(t, cos_t, sin_t):
    return t * cos_t + _swap_halves(t) * sin_t


def _unrope(g, cos_t, sin_t):
    return g * cos_t - _swap_halves(g) * sin_t


def _mesh_pos():
    return lax.axis_index("x"), lax.axis_index("y"), lax.axis_index("c")


def _other_chips(x, y):
    out = []
    for dx, dy in ((1, 0), (0, 1), (1, 1)):
        px = 1 - x if dx else x
        py = 1 - y if dy else y
        out.append(((px, py), 2 * px + py))
    return out


def _allgather8(a, name):
    rows, cols = a.shape

    def body(a_ref, out_ref, send_sems, recv_sems):
        x, y, c = _mesh_pos()
        me = 4 * x + 2 * y + c
        out_ref[me] = a_ref[...]
        peers = []
        for r in range(1, N_DEV):
            px = 1 - x if r & 4 else x
            py = 1 - y if r & 2 else y
            pc = 1 - c if r & 1 else c
            peers.append(((px, py, pc), 4 * px + 2 * py + pc))

        def copy(r, block, to):
            return pltpu.make_async_remote_copy(
                src_ref=a_ref, dst_ref=out_ref.at[block], send_sem=send_sems.at[r], recv_sem=recv_sems.at[r],
                device_id=to, device_id_type=MESH_IDS)

        sends = [copy(r, me, peer) for r, (peer, _) in enumerate(peers)]
        for cp in sends:
            cp.start()
        for r, (peer, idx) in enumerate(peers):
            copy(r, idx, peer).wait_recv()
        for cp in sends:
            cp.wait_send()

    return pl.pallas_call(
        body, name=name,
        out_shape=jax.ShapeDtypeStruct((N_DEV, rows, cols), a.dtype),
        in_specs=[pl.BlockSpec(memory_space=pltpu.VMEM)],
        out_specs=pl.BlockSpec(memory_space=pltpu.VMEM),
        scratch_shapes=[pltpu.SemaphoreType.DMA((N_DEV - 1,)), pltpu.SemaphoreType.DMA((N_DEV - 1,))],
    )(a)


class _Exchange:
    def __init__(self, inputs, out_shapes, aliases, sems, start, finish):
        self.inputs, self.out_shapes, self.aliases, self.sems = list(inputs), list(out_shapes), dict(aliases), list(sems)
        self.start, self.finish = start, finish


def _from_copies(inputs, out_shapes, aliases, sems, copies):
    def start(ins, outs, sem_refs):
        for send, _ in copies(ins, outs, sem_refs):
            send.start()

    def finish(ins, outs, sem_refs):
        for send, recv in copies(ins, outs, sem_refs):
            recv.wait_recv()
            send.wait_send()

    return _Exchange(inputs, out_shapes, aliases, sems, start, finish)


HBM_MIN_BYTES = 256 * 1024


def _in_hbm(a):
    if a.size * a.dtype.itemsize < HBM_MIN_BYTES:
        return a
    return pltpu.with_memory_space_constraint(a, pltpu.HBM)


def _out_hbm(s):
    if s.size * s.dtype.itemsize < HBM_MIN_BYTES:
        return s
    return pltpu.HBM(s.shape, s.dtype)


def _pcall(body, *, operands, out_shape, **kwargs):
    single = not isinstance(out_shape, (list, tuple))
    shapes = [_out_hbm(s) for s in ([out_shape] if single else out_shape)]
    return pl.pallas_call(body, out_shape=shapes[0] if single else shapes, **kwargs)(*[_in_hbm(a) for a in operands])


def _run_exchange(exchange, name):
    n_in, n_out = len(exchange.inputs), len(exchange.out_shapes)

    def body(*refs):
        ins, outs, sem_refs = refs[:n_in], refs[n_in:n_in + n_out], refs[n_in + n_out:]
        exchange.start(ins, outs, sem_refs)
        exchange.finish(ins, outs, sem_refs)

    any_spec = pl.BlockSpec(memory_space=pl.ANY)
    return pl.pallas_call(
        body, name=name, out_shape=[_out_hbm(s) for s in exchange.out_shapes],
        in_specs=[any_spec] * n_in, out_specs=[any_spec] * n_out,
        scratch_shapes=exchange.sems, input_output_aliases=exchange.aliases,
    )(*[_in_hbm(a) for a in exchange.inputs])


def _pallas(body, *, name, operands, in_specs, out_shape, out_specs, params, scratch_shapes=(), grid=(), prefetch=(),
            exchange=None, first=None, last=None):
    n_pre, n_in, n_out, n_scr = len(prefetch), len(in_specs), len(out_specs), len(scratch_shapes)
    ex_in = exchange.inputs if exchange else []
    ex_out = exchange.out_shapes if exchange else []
    ex_sems = exchange.sems if exchange else []

    def full_body(*refs):
        pre, rest = refs[:n_pre], refs[n_pre:]
        ins, rest = rest[:n_in], rest[n_in:]
        xin, rest = rest[:len(ex_in)], rest[len(ex_in):]
        outs, rest = rest[:n_out], rest[n_out:]
        xout, rest = rest[:len(ex_out)], rest[len(ex_out):]
        scr, sem_refs = rest[:n_scr], rest[n_scr:]
        if exchange:
            @pl.when(first(*pre))
            def _():
                exchange.start(xin, xout, sem_refs)

        body(*pre, *ins, *outs, *scr)
        if exchange:
            @pl.when(last(*pre))
            def _():
                exchange.finish(xin, xout, sem_refs)

    any_spec = pl.BlockSpec(memory_space=pl.ANY)
    aliases = {n_pre + n_in + i: n_out + o for i, o in exchange.aliases.items()} if exchange else {}
    operands = [_in_hbm(a) for a in operands]
    results = pl.pallas_call(
        full_body, name=name, out_shape=[_out_hbm(s) for s in list(out_shape) + ex_out],
        grid_spec=pltpu.PrefetchScalarGridSpec(
            num_scalar_prefetch=n_pre, grid=grid, in_specs=list(in_specs) + [any_spec] * len(ex_in),
            out_specs=list(out_specs) + [any_spec] * len(ex_out), scratch_shapes=list(scratch_shapes) + ex_sems),
        input_output_aliases=aliases, compiler_params=params,
    )(*prefetch, *operands, *[_in_hbm(a) for a in ex_in])
    return results[:n_out], results[n_out:]


def _remote(src, dst, sems, idx, to):
    send_sems, recv_sems = sems
    return pltpu.make_async_remote_copy(src_ref=src, dst_ref=dst, send_sem=send_sems.at[idx], recv_sem=recv_sems.at[idx],
                                        device_id=to, device_id_type=MESH_IDS)


def _sem_pairs(*shape):
    return [pltpu.SemaphoreType.DMA(shape), pltpu.SemaphoreType.DMA(shape)]


def _same_shapes(arrays):
    return [jax.ShapeDtypeStruct(a.shape, a.dtype) for a in arrays]


def _gather_over_ici(slots):
    n = len(slots)

    def copies(ins, outs, sems):
        x, y, c = _mesh_pos()
        k = 2 * x + y
        out = []
        for j, (chip, kj) in enumerate(_other_chips(x, y)):
            for i in range(n):
                to = (*chip, c)
                out.append((_remote(ins[i].at[k, c], outs[i].at[k, c], sems, (j, i), to),
                            _remote(ins[i].at[k, c], outs[i].at[kj, c], sems, (j, i), to)))
        return out

    return _from_copies(slots, _same_shapes(slots), {i: i for i in range(n)}, _sem_pairs(3, n), copies)


def _gather_over_d2d(slots):
    n = len(slots)

    def copies(ins, outs, sems):
        x, y, c = _mesh_pos()
        sibling = (x, y, 1 - c)
        out = []
        for j, (_, kj) in enumerate(_other_chips(x, y)):
            for i in range(n):
                out.append((_remote(ins[i].at[kj, c], outs[i].at[kj, c], sems, (j, i), sibling),
                            _remote(ins[i].at[kj, c], outs[i].at[kj, 1 - c], sems, (j, i), sibling)))
        return out

    return _from_copies(slots, _same_shapes(slots), {i: i for i in range(n)}, _sem_pairs(3, n), copies)


def _gather_all(slots8):
    def copies(ins, outs, sems):
        x, y, c = _mesh_pos()
        me = 4 * x + 2 * y + c
        out = []
        for r in range(1, N_DEV):
            px = 1 - x if r & 4 else x
            py = 1 - y if r & 2 else y
            pc = 1 - c if r & 1 else c
            to = (px, py, pc)
            out.append((_remote(ins[0].at[me], outs[0].at[me], sems, r - 1, to),
                        _remote(ins[0].at[me], outs[0].at[4 * px + 2 * py + pc], sems, r - 1, to)))
        return out

    return _from_copies([slots8], _same_shapes([slots8]), {0: 0}, _sem_pairs(N_DEV - 1), copies)


def _merge(first, second):
    n_in, n_out, n_sem = len(first.inputs), len(first.out_shapes), len(first.sems)

    def start(ins, outs, sems):
        first.start(ins[:n_in], outs[:n_out], sems[:n_sem])
        second.start(ins[n_in:], outs[n_out:], sems[n_sem:])

    def finish(ins, outs, sems):
        first.finish(ins[:n_in], outs[:n_out], sems[:n_sem])
        second.finish(ins[n_in:], outs[n_out:], sems[n_sem:])

    aliases = dict(first.aliases)
    aliases.update({n_in + i: n_out + o for i, o in second.aliases.items()})
    return _Exchange(first.inputs + second.inputs, first.out_shapes + second.out_shapes, aliases,
                     first.sems + second.sems, start, finish)


def _pair_exchange(grads):
    n = len(grads)

    def copies(ins, outs, sems):
        x, y, c = _mesh_pos()
        cps = [_remote(ins[i].at[:, 1 - c], outs[i], sems, i, (x, y, 1 - c)) for i in range(n)]
        return [(cp, cp) for cp in cps]

    shapes = [jax.ShapeDtypeStruct((N_CHIPS,) + g.shape[2:], g.dtype) for g in grads]
    return _from_copies(grads, shapes, {}, _sem_pairs(n), copies)


def _chip_exchange(partials):
    n = len(partials)

    def copies(ins, outs, sems):
        x, y, c = _mesh_pos()
        cps = [_remote(ins[i].at[kj], outs[i].at[j], sems, (j, i), (*chip, c))
               for j, (chip, kj) in enumerate(_other_chips(x, y)) for i in range(n)]
        return [(cp, cp) for cp in cps]

    shapes = [jax.ShapeDtypeStruct((3,) + p.shape[1:], p.dtype) for p in partials]
    return _from_copies(partials, shapes, {}, _sem_pairs(3, n), copies)


def _pair_send(halves):
    n = len(halves)

    def copies(ins, outs, sems):
        x, y, c = _mesh_pos()
        cps = [_remote(ins[i], outs[i], sems, i, (x, y, 1 - c)) for i in range(n)]
        return [(cp, cp) for cp in cps]

    return _from_copies(halves, _same_shapes(halves), {}, _sem_pairs(n), copies)


def _row_tile(rows):
    for t in (256, 128, 64, 32, 16, 8):
        if rows % t == 0:
            return t
    return rows


def _add_pair(core, grad, landed):
    _, h, cols = landed.shape
    tr = _row_tile(h)

    def body(core_ref, g_ref, l_ref, o_ref, ob_ref):
        s = g_ref[...] + l_ref[...]
        o_ref[...] = s
        ob_ref[...] = s.astype(BF16)

    out_spec = pl.BlockSpec((None, tr, cols), lambda k, t, core_ref: (k, t, 0))
    return _pcall(
        body, name="grad_add_pair",
        out_shape=[jax.ShapeDtypeStruct(landed.shape, F32), jax.ShapeDtypeStruct(landed.shape, BF16)],
        grid_spec=pltpu.PrefetchScalarGridSpec(
            num_scalar_prefetch=1, grid=(N_CHIPS, h // tr),
            in_specs=[pl.BlockSpec((None, None, tr, cols), lambda k, t, core_ref: (k, core_ref[0], t, 0)),
                      pl.BlockSpec((None, tr, cols), lambda k, t, core_ref: (k, t, 0))],
            out_specs=[out_spec, out_spec]),
        compiler_params=_params(32, ("arbitrary", "arbitrary")),
        operands=(core, grad, landed))


def _add_chips(chip, partial, landed):
    _, h, cols = partial.shape
    tr = _row_tile(h)

    def body(chip_ref, p_ref, l_ref, o_ref):
        o_ref[...] = ((p_ref[...] + l_ref[0].astype(F32)) + l_ref[1].astype(F32)) + l_ref[2].astype(F32)

    return _pcall(
        body, name="grad_add_chips",
        out_shape=jax.ShapeDtypeStruct((h, cols), F32),
        grid_spec=pltpu.PrefetchScalarGridSpec(
            num_scalar_prefetch=1, grid=(h // tr,),
            in_specs=[pl.BlockSpec((None, tr, cols), lambda t, chip_ref: (chip_ref[0], t, 0)),
                      pl.BlockSpec((3, tr, cols), lambda t, chip_ref: (0, t, 0))],
            out_specs=pl.BlockSpec((tr, cols), lambda t, chip_ref: (t, 0))),
        compiler_params=_params(32, ("arbitrary",)),
        operands=(chip, partial, landed))


def _ada_project(c_all, w_ada, b_shard):
    n = w_ada.shape[1]
    tn = 512

    def body(c_ref, w_ref, b_ref, mod_ref, cond_ref):
        cv = c_ref[...]
        cond = cv * _sigmoid(cv)
        mod_ref[...] = _dot(cond.astype(BF16), w_ref[...].astype(BF16)) + b_ref[...]
        cond_ref[0:N_DEV, :] = cond
        cond_ref[N_DEV:2 * N_DEV, :] = jnp.zeros_like(cond)

    return _pcall(
        body, name="ada_project", grid=(n // tn,),
        out_shape=[jax.ShapeDtypeStruct((N_DEV, n), F32), jax.ShapeDtypeStruct((2 * N_DEV, D_MODEL), F32)],
        in_specs=[pl.BlockSpec((N_DEV, D_MODEL), lambda j: (0, 0)), pl.BlockSpec((D_MODEL, tn), lambda j: (0, j)),
                  pl.BlockSpec((1, tn), lambda j: (0, j))],
        out_specs=[pl.BlockSpec((N_DEV, tn), lambda j: (0, j)), pl.BlockSpec((2 * N_DEV, D_MODEL), lambda j: (0, 0))],
        compiler_params=_params(32, ("arbitrary",)),
        operands=(c_all, w_ada, b_shard))


def _in_project(x, pos, sc_a, sh_a, w_in, q_norm_w, kv_norm_w, w_q, w_kv, invf, exchange=None):
    t_len = x.shape[0]
    tm = min(256, t_len)

    def body(x_ref, pos_ref, sc_ref, sh_ref, win_ref, qn_ref, kvn_ref, wq_ref, wkv_ref, invf_ref,
             u_ref, zhg_ref, cq_ref, ckv_ref, q_ref, k_ref, kt_ref, v_ref, vt_ref):
        u = (x_ref[...] * (1.0 + sc_ref[...]) + sh_ref[...]).astype(BF16)
        u_ref[...] = u
        z = _dot(u, win_ref[...])
        zhg_ref[...] = z[:, :HG_COLS]
        cq = z[:, HG_COLS:HG_COLS + Q_RANK]
        ckv = z[:, HG_COLS + Q_RANK:HG_COLS + Q_RANK + KV_RANK]
        cq_ref[...] = cq
        ckv_ref[...] = ckv
        cos_t, sin_t = _rope_tables(pos_ref[...], invf_ref[...])
        k_pe = _rope(z[:, HG_COLS + Q_RANK + KV_RANK:], cos_t, sin_t)
        k_pe_t = jnp.transpose(k_pe).astype(BF16)
        cqn = (cq * lax.rsqrt(_rowmean(cq * cq) + RMS_EPS) * qn_ref[...]).astype(BF16)
        ckvn = (ckv * lax.rsqrt(_rowmean(ckv * ckv) + RMS_EPS) * kvn_ref[...]).astype(BF16)
        for h in range(N_HEADS):
            qh = _dot(cqn, wq_ref[h])
            q_ref[h, :, 0:HEAD_DIM] = qh[:, :HEAD_DIM].astype(BF16)
            q_ref[h, :, HEAD_DIM:QK_DIM] = _rope(qh[:, HEAD_DIM:], cos_t, sin_t).astype(BF16)
            kvh = _dot(ckvn, wkv_ref[h])
            k_ref[h, :, 0:HEAD_DIM] = kvh[:, :HEAD_DIM].astype(BF16)
            k_ref[h, :, HEAD_DIM:QK_DIM] = k_pe.astype(BF16)
            kt_ref[h, 0:HEAD_DIM, :] = jnp.transpose(kvh[:, :HEAD_DIM]).astype(BF16)
            kt_ref[h, HEAD_DIM:QK_DIM, :] = k_pe_t
            v_ref[h] = kvh[:, HEAD_DIM:].astype(BF16)
            vt_ref[h] = jnp.transpose(kvh[:, HEAD_DIM:]).astype(BF16)

    row = lambda i: (i, 0)
    fixed2 = lambda i: (0, 0)
    fixed3 = lambda i: (0, 0, 0)
    heads = lambda i: (0, i, 0)
    n_tiles = t_len // tm
    return _pallas(
        body, name="in_project", grid=(n_tiles,),
        operands=(x, pos, sc_a, sh_a, w_in, q_norm_w, kv_norm_w, w_q, w_kv, invf),
        out_shape=[jax.ShapeDtypeStruct((t_len, D_MODEL), BF16), jax.ShapeDtypeStruct((t_len, HG_COLS), F32),
                   jax.ShapeDtypeStruct((t_len, Q_RANK), F32), jax.ShapeDtypeStruct((t_len, KV_RANK), F32),
                   jax.ShapeDtypeStruct((N_HEADS, t_len, QK_DIM), BF16),
                   jax.ShapeDtypeStruct((N_HEADS, t_len, QK_DIM), BF16),
                   jax.ShapeDtypeStruct((N_HEADS, QK_DIM, t_len), BF16),
                   jax.ShapeDtypeStruct((N_HEADS, t_len, HEAD_DIM), BF16),
                   jax.ShapeDtypeStruct((N_HEADS, HEAD_DIM, t_len), BF16)],
        in_specs=[pl.BlockSpec((tm, D_MODEL), row), pl.BlockSpec((tm, 1), row),
                  pl.BlockSpec((1, D_MODEL), fixed2), pl.BlockSpec((1, D_MODEL), fixed2),
                  pl.BlockSpec((D_MODEL, IN_COLS_PAD), fixed2),
                  pl.BlockSpec((1, Q_RANK), fixed2), pl.BlockSpec((1, KV_RANK), fixed2),
                  pl.BlockSpec((N_HEADS, Q_RANK, QK_DIM), fixed3), pl.BlockSpec((N_HEADS, KV_RANK, 2 * HEAD_DIM), fixed3),
                  pl.BlockSpec((1, 128), fixed2)],
        out_specs=[pl.BlockSpec((tm, D_MODEL), row), pl.BlockSpec((tm, HG_COLS), row),
                   pl.BlockSpec((tm, Q_RANK), row), pl.BlockSpec((tm, KV_RANK), row),
                   pl.BlockSpec((N_HEADS, tm, QK_DIM), heads), pl.BlockSpec((N_HEADS, tm, QK_DIM), heads),
                   pl.BlockSpec((N_HEADS, QK_DIM, tm), lambda i: (0, 0, i)),
                   pl.BlockSpec((N_HEADS, tm, HEAD_DIM), heads),
                   pl.BlockSpec((N_HEADS, HEAD_DIM, tm), lambda i: (0, 0, i))],
        params=_params(48, ("arbitrary",)), exchange=exchange,
        first=lambda: pl.program_id(0) == 0, last=lambda: pl.program_id(0) == n_tiles - 1)


def _lower_bound(lb_raw):
    m = jnp.max(lb_raw, axis=0, keepdims=True)
    e = jnp.exp(lb_raw - m)
    return e[0:1] / jnp.sum(e, axis=0, keepdims=True)


def _tri(inclusive_lower):
    r = lax.broadcasted_iota(jnp.int32, (HG_CHUNK, HG_CHUNK), 0)
    c = lax.broadcasted_iota(jnp.int32, (HG_CHUNK, HG_CHUNK), 1)
    return (c <= r) if inclusive_lower else (c >= r)


def _chunk_rows(n):
    return slice(n * HG_CHUNK, (n + 1) * HG_CHUNK)


def _chunk_prefix_sums(v, inclusive_lower):
    tri = _tri(inclusive_lower).astype(BF16)
    hi = v.astype(BF16)
    rest = v - hi.astype(F32)
    mid = rest.astype(BF16)
    lo = (rest - mid.astype(F32)).astype(BF16)
    pieces = jnp.concatenate([hi, mid, lo], axis=1)
    out = []
    for n in range(v.shape[0] // HG_CHUNK):
        s = _dot(tri, pieces[_chunk_rows(n)])
        out.append((s[:, 0:HEAD_DIM] + s[:, HEAD_DIM:2 * HEAD_DIM]) + s[:, 2 * HEAD_DIM:])
    return jnp.concatenate(out, axis=0)


def _per_chunk(v, row):
    n = v.shape[0] // HG_CHUNK
    v3 = v.reshape(n, HG_CHUNK, HEAD_DIM)
    return jnp.broadcast_to(v3[:, row:row + 1, :], v3.shape).reshape(v.shape)


def _hg_block(q, f_logit, lb):
    sg = _sigmoid(f_logit)
    forget = lb + (1.0 - lb) * sg
    kk = 1.0 - forget
    b = _chunk_prefix_sums(jnp.log(forget), True)
    b_ref = _per_chunk(b, HG_CHUNK // 2 - 1)
    b_last = _per_chunk(b, HG_CHUNK - 1)
    e_i = jnp.exp(b - b_ref)
    e_ri = jnp.exp(b_ref - b)
    e_b = jnp.exp(b)
    e_l = jnp.exp(b_last - b)
    return dict(sg=sg, forget=forget, e_i=e_i, e_ri=e_ri, e_b=e_b, e_l=e_l, dec=jnp.exp(b_last),
                qi=q * e_i, ki=kk * e_ri, qe=q * e_b, kl=kk * e_l)


def _hgrn_forward(zhg, lb_raw, norm_w, exchange=None):
    t_len = zhg.shape[0]
    tb = min(512, t_len)
    n_chunks = tb // HG_CHUNK

    def body(q_ref, f_ref, v_ref, g_ref, lb_ref, w_ref, opre_ref, o_ref, st_ref, state):
        @pl.when(pl.program_id(1) == 0)
        def _():
            state[...] = jnp.zeros_like(state)

        blk = _hg_block(q_ref[...], f_ref[...], _lower_bound(lb_ref[...]))
        v = v_ref[...].astype(BF16)
        qi, ki, qe, kl = (blk[name].astype(BF16) for name in ("qi", "ki", "qe", "kl"))
        causal = _tri(True)
        st = state[...]
        parts = []
        for n in range(n_chunks):
            r = _chunk_rows(n)
            a = jnp.where(causal, _dot_nt(qi[r], ki[r]), 0.0).astype(BF16)
            st_ref[0, n] = st
            parts.append(_dot(a, v[r]) + _dot_nt(qe[r], st.astype(BF16)))
            st = st * blk["dec"][n * HG_CHUNK:n * HG_CHUNK + 1] + _dot_tn(v[r], kl[r])
        state[...] = st
        o = jnp.concatenate(parts, axis=0)
        opre_ref[...] = o
        g = g_ref[...]
        o_ref[...] = o * lax.rsqrt(_rowmean(o * o) + RMS_EPS) * w_ref[...] * (g * _sigmoid(g))

    col = lambda off: (lambda h, t: (t, off + h))
    nb = t_len // tb
    return _pallas(
        body, name="hgrn_forward", grid=(N_HEADS, nb), operands=(zhg, zhg, zhg, zhg, lb_raw, norm_w),
        out_shape=[jax.ShapeDtypeStruct((t_len, N_HEADS * HEAD_DIM), F32),
                   jax.ShapeDtypeStruct((t_len, N_HEADS * HEAD_DIM), F32),
                   jax.ShapeDtypeStruct((N_HEADS, t_len // HG_CHUNK, HEAD_DIM, HEAD_DIM), F32)],
        in_specs=[pl.BlockSpec((tb, HEAD_DIM), col(0)), pl.BlockSpec((tb, HEAD_DIM), col(N_HEADS)),
                  pl.BlockSpec((tb, HEAD_DIM), col(2 * N_HEADS)), pl.BlockSpec((tb, HEAD_DIM), col(3 * N_HEADS)),
                  pl.BlockSpec((2, HEAD_DIM), lambda h, t: (0, h)), pl.BlockSpec((1, HEAD_DIM), lambda h, t: (0, h))],
        out_specs=[pl.BlockSpec((tb, HEAD_DIM), col(0)), pl.BlockSpec((tb, HEAD_DIM), col(0)),
                   pl.BlockSpec((1, n_chunks, HEAD_DIM, HEAD_DIM), lambda h, t: (h, t, 0, 0))],
        scratch_shapes=[pltpu.VMEM((HEAD_DIM, HEAD_DIM), F32)],
        params=_params(32, ("arbitrary", "arbitrary")), exchange=exchange,
        first=lambda: (pl.program_id(0) == 0) & (pl.program_id(1) == 0),
        last=lambda: (pl.program_id(0) == N_HEADS - 1) & (pl.program_id(1) == nb - 1))


def _hgrn_backward(zhg, lb_raw, norm_w, o_pre, d_cat, states, exchange=None):
    t_len = zhg.shape[0]
    tb = min(512, t_len)
    n_chunks = tb // HG_CHUNK
    nb = t_len // tb

    def body(q_ref, f_ref, v_ref, g_ref, lb_ref, w_ref, opre_ref, do_ref, st_ref,
             dq_ref, df_ref, dv_ref, dg_ref, sums_ref, gstate):
        @pl.when(pl.program_id(1) == 0)
        def _():
            gstate[...] = jnp.zeros_like(gstate)
            sums_ref[...] = jnp.zeros_like(sums_ref)

        lb = _lower_bound(lb_ref[...])
        w = w_ref[...]
        o = opre_ref[...]
        g = g_ref[...]
        d_out = do_ref[...]
        r = lax.rsqrt(_rowmean(o * o) + RMS_EPS)
        sg_g = _sigmoid(g)
        dg_ref[...] = (d_out * (o * r * w) * (sg_g * (1.0 + g * (1.0 - sg_g)))).astype(BF16)
        d_on = d_out * (g * sg_g)
        sums_ref[1:2, :] += _colsum(d_on * o * r)
        dy = d_on * w
        d_o = (r * dy - o * (r * r * r) * _rowmean(dy * o)).astype(BF16)
        blk = _hg_block(q_ref[...], f_ref[...], lb)
        v = v_ref[...].astype(BF16)
        qi, ki, qe, kl = (blk[name].astype(BF16) for name in ("qi", "ki", "qe", "kl"))
        causal = _tri(True)
        row_id = lax.broadcasted_iota(jnp.int32, (HG_CHUNK, HEAD_DIM), 0)
        gt = gstate[...]
        d_v, d_qi, d_ki, d_qe, d_kl, d_dec = ([None] * n_chunks for _ in range(6))
        for n in reversed(range(n_chunks)):
            rows = _chunk_rows(n)
            st = st_ref[0, n]
            a = jnp.where(causal, _dot_nt(qi[rows], ki[rows]), 0.0).astype(BF16)
            d_a = jnp.where(causal, _dot_nt(d_o[rows], v[rows]), 0.0).astype(BF16)
            gt_b = gt.astype(BF16)
            d_v[n] = _dot_tn(a, d_o[rows]) + _dot_nt(kl[rows], gt_b)
            d_qi[n] = _dot(d_a, ki[rows])
            d_ki[n] = _dot_tn(d_a, qi[rows])
            d_qe[n] = _dot(d_o[rows], st.astype(BF16))
            d_kl[n] = _dot(v[rows], gt_b)
            d_dec[n] = jnp.where(row_id == HG_CHUNK - 1, _colsum(gt * st), 0.0)
            gt = gt * blk["dec"][n * HG_CHUNK:n * HG_CHUNK + 1] + _dot_tn(d_o[rows], qe[rows])
        gstate[...] = gt
        d_qi, d_ki, d_qe, d_kl, d_dec = (jnp.concatenate(p, axis=0) for p in (d_qi, d_ki, d_qe, d_kl, d_dec))
        dv_ref[...] = jnp.concatenate(d_v, axis=0).astype(BF16)
        dq_ref[...] = (d_qi * blk["e_i"] + d_qe * blk["e_b"]).astype(BF16)
        d_k = d_ki * blk["e_ri"] + d_kl * blk["e_l"]
        t_qi = d_qi * blk["qi"]
        t_ki = d_ki * blk["ki"]
        t_kl = d_kl * blk["kl"]
        at_ref, at_last = [], []
        for n in range(n_chunks):
            rows = _chunk_rows(n)
            at_ref.append(jnp.where(row_id == HG_CHUNK // 2 - 1, _colsum(t_ki[rows] - t_qi[rows]), 0.0))
            at_last.append(jnp.where(row_id == HG_CHUNK - 1, _colsum(t_kl[rows]), 0.0))
        d_b = (t_qi - t_ki + d_qe * blk["qe"] - t_kl + jnp.concatenate(at_ref, axis=0)
               + jnp.concatenate(at_last, axis=0) + d_dec * blk["dec"])
        d_forget = _chunk_prefix_sums(d_b, False) / blk["forget"] - d_k
        sg = blk["sg"]
        df_ref[...] = (d_forget * (1.0 - lb) * sg * (1.0 - sg)).astype(BF16)
        sums_ref[0:1, :] += _colsum(d_forget * (1.0 - sg))

    col = lambda off: (lambda h, t: (nb - 1 - t, off + h))
    return _pallas(
        body, name="hgrn_backward", grid=(N_HEADS, nb),
        operands=(zhg, zhg, zhg, zhg, lb_raw, norm_w, o_pre, d_cat, states),
        out_shape=[jax.ShapeDtypeStruct((t_len, N_HEADS * HEAD_DIM), BF16)] * 4
        + [jax.ShapeDtypeStruct((8, N_HEADS * HEAD_DIM), F32)],
        in_specs=[pl.BlockSpec((tb, HEAD_DIM), col(0)), pl.BlockSpec((tb, HEAD_DIM), col(N_HEADS)),
                  pl.BlockSpec((tb, HEAD_DIM), col(2 * N_HEADS)), pl.BlockSpec((tb, HEAD_DIM), col(3 * N_HEADS)),
                  pl.BlockSpec((2, HEAD_DIM), lambda h, t: (0, h)), pl.BlockSpec((1, HEAD_DIM), lambda h, t: (0, h)),
                  pl.BlockSpec((tb, HEAD_DIM), col(0)), pl.BlockSpec((tb, HEAD_DIM), col(0)),
                  pl.BlockSpec((1, n_chunks, HEAD_DIM, HEAD_DIM), lambda h, t: (h, nb - 1 - t, 0, 0))],
        out_specs=[pl.BlockSpec((tb, HEAD_DIM), col(0))] * 4 + [pl.BlockSpec((8, HEAD_DIM), lambda h, t: (0, h))],
        scratch_shapes=[pltpu.VMEM((HEAD_DIM, HEAD_DIM), F32)],
        params=_params(32, ("arbitrary", "arbitrary")), exchange=exchange,
        first=lambda: (pl.program_id(0) == 0) & (pl.program_id(1) == 0),
        last=lambda: (pl.program_id(0) == N_HEADS - 1) & (pl.program_id(1) == nb - 1))


ATT_LOG2 = ATT_SCALE * 1.4426950408889634


def _triangle_steps(nq, q_major):
    if q_major:
        pairs = [(i, j) for i in range(nq) for j in range(i + 1)]
    else:
        pairs = [(i, j) for j in range(nq) for i in range(j, nq)]
    return jnp.array([p[0] for p in pairs], jnp.int32), jnp.array([p[1] for p in pairs], jnp.int32)


def _key_le_query(t):
    return lax.broadcasted_iota(jnp.int32, (t, t), 0) <= lax.broadcasted_iota(jnp.int32, (t, t), 1)


def _attention_forward(q, k, v_t, exchange=None):
    t_len = q.shape[1]
    tq = min(512, t_len)
    nq = t_len // tq
    qi_tab, ki_tab = _triangle_steps(nq, True)

    def body(qi_ref, ki_ref, q_ref, k_ref, vt_ref, o_ref, lse_ref, m_s, l_s, acc_s):
        step = pl.program_id(0)
        qi, ki = qi_ref[step], ki_ref[step]

        @pl.when(ki == 0)
        def _():
            m_s[...] = jnp.full_like(m_s, NEG_BIG)
            l_s[...] = jnp.zeros_like(l_s)
            acc_s[...] = jnp.zeros_like(acc_s)

        def accumulate(masked):
            for h in range(N_HEADS):
                s_t = _dot_nt(k_ref[h], q_ref[h]) * ATT_LOG2
                if masked:
                    s_t = jnp.where(_key_le_query(tq), s_t, NEG_BIG)
                m_old = m_s[h]
                m_new = jnp.maximum(m_old, jnp.max(s_t, axis=0, keepdims=True))
                alpha = jnp.exp2(m_old - m_new)
                p_t = jnp.exp2(s_t - m_new)
                l_s[h] = alpha * l_s[h] + jnp.sum(p_t, axis=0, keepdims=True)
                acc_s[h] = alpha * acc_s[h] + _dot(vt_ref[h], p_t.astype(BF16))
                m_s[h] = m_new

        @pl.when(ki < qi)
        def _():
            accumulate(False)

        @pl.when(ki == qi)
        def _():
            accumulate(True)
            for h in range(N_HEADS):
                o_ref[:, h * HEAD_DIM:(h + 1) * HEAD_DIM] = jnp.transpose(acc_s[h] / l_s[h])
                lse_ref[h] = m_s[h] + jnp.log2(l_s[h])

    n_steps = qi_tab.shape[0]
    return _pallas(
        body, name="attention_forward", grid=(n_steps,), prefetch=(qi_tab, ki_tab), operands=(q, k, v_t),
        out_shape=[jax.ShapeDtypeStruct((t_len, N_HEADS * HEAD_DIM), F32),
                   jax.ShapeDtypeStruct((N_HEADS, 1, t_len), F32)],
        in_specs=[pl.BlockSpec((N_HEADS, tq, QK_DIM), lambda s, qt, kt: (0, qt[s], 0)),
                  pl.BlockSpec((N_HEADS, tq, QK_DIM), lambda s, qt, kt: (0, kt[s], 0)),
                  pl.BlockSpec((N_HEADS, HEAD_DIM, tq), lambda s, qt, kt: (0, 0, kt[s]))],
        out_specs=[pl.BlockSpec((tq, N_HEADS * HEAD_DIM), lambda s, qt, kt: (qt[s], 0)),
                   pl.BlockSpec((N_HEADS, 1, tq), lambda s, qt, kt: (0, 0, qt[s]))],
        scratch_shapes=[pltpu.VMEM((N_HEADS, 1, tq), F32), pltpu.VMEM((N_HEADS, 1, tq), F32),
                        pltpu.VMEM((N_HEADS, HEAD_DIM, tq), F32)],
        params=_params(48, ("arbitrary",)), exchange=exchange,
        first=lambda qt, kt: pl.program_id(0) == 0, last=lambda qt, kt: pl.program_id(0) == n_steps - 1)


BWD_HEADS = 2


def _attention_backward(q, k, k_t, v, d_cat, lse, delta, exchange=None):
    t_len = q.shape[1]
    tq = min(512, t_len)
    nq = t_len // tq
    hp = BWD_HEADS
    qi_tab, ki_tab = _triangle_steps(nq, False)

    def body(qi_ref, ki_ref, q_ref, k_ref, kt_ref, v_ref, do_ref, lse_ref, delta_ref, dqt_hbm, dk_ref, dv_ref,
             dqt_s, dk_s, dv_s):
        group, step = pl.program_id(0), pl.program_id(1)
        qi, ki = qi_ref[step], ki_ref[step]

        @pl.when(step == 0)
        def _():
            dqt_s[...] = jnp.zeros_like(dqt_s)

        @pl.when(qi == ki)
        def _():
            dk_s[...] = jnp.zeros_like(dk_s)
            dv_s[...] = jnp.zeros_like(dv_s)

        def accumulate(masked):
            for h in range(hp):
                do_b = do_ref[:, h * HEAD_DIM:(h + 1) * HEAD_DIM].astype(BF16)
                s_t = _dot_nt(k_ref[h], q_ref[h]) * ATT_LOG2
                if masked:
                    s_t = jnp.where(_key_le_query(tq), s_t, NEG_BIG)
                p_t = jnp.exp2(s_t - lse_ref[h])
                dp_t = _dot_nt(v_ref[h], do_b)
                ds_t = (p_t * (dp_t - delta_ref[h]) * ATT_SCALE).astype(BF16)
                dv_s[h] += _dot(p_t.astype(BF16), do_b)
                dk_s[h] += _dot(ds_t, q_ref[h])
                dqt_s[h, qi] += _dot(kt_ref[h], ds_t)

        @pl.when(ki < qi)
        def _():
            accumulate(False)

        @pl.when(ki == qi)
        def _():
            accumulate(True)
            for h in range(hp):
                pltpu.sync_copy(dqt_s.at[h, qi], dqt_hbm.at[group * hp + h, qi])

        @pl.when(qi == nq - 1)
        def _():
            dk_ref[...] = dk_s[...]
            dv_ref[...] = dv_s[...]

    wide = hp * HEAD_DIM
    n_groups, n_steps = N_HEADS // hp, qi_tab.shape[0]
    return _pallas(
        body, name="attention_backward", grid=(n_groups, n_steps), prefetch=(qi_tab, ki_tab),
        operands=(q, k, k_t, v, d_cat, lse, delta),
        out_shape=[jax.ShapeDtypeStruct((N_HEADS, nq, QK_DIM, tq), F32),
                   jax.ShapeDtypeStruct((N_HEADS, t_len, QK_DIM), F32),
                   jax.ShapeDtypeStruct((N_HEADS, t_len, HEAD_DIM), F32)],
        in_specs=[pl.BlockSpec((hp, tq, QK_DIM), lambda g, s, qt, kt: (g, qt[s], 0)),
                  pl.BlockSpec((hp, tq, QK_DIM), lambda g, s, qt, kt: (g, kt[s], 0)),
                  pl.BlockSpec((hp, QK_DIM, tq), lambda g, s, qt, kt: (g, 0, kt[s])),
                  pl.BlockSpec((hp, tq, HEAD_DIM), lambda g, s, qt, kt: (g, kt[s], 0)),
                  pl.BlockSpec((tq, wide), lambda g, s, qt, kt: (qt[s], n_groups + g)),
                  pl.BlockSpec((hp, 1, tq), lambda g, s, qt, kt: (g, 0, qt[s])),
                  pl.BlockSpec((hp, 1, tq), lambda g, s, qt, kt: (g, 0, qt[s]))],
        out_specs=[pl.BlockSpec(memory_space=pl.ANY),
                   pl.BlockSpec((hp, tq, QK_DIM), lambda g, s, qt, kt: (g, kt[s], 0)),
                   pl.BlockSpec((hp, tq, HEAD_DIM), lambda g, s, qt, kt: (g, kt[s], 0))],
        scratch_shapes=[pltpu.VMEM((hp, nq, QK_DIM, tq), F32), pltpu.VMEM((hp, tq, QK_DIM), F32),
                        pltpu.VMEM((hp, tq, HEAD_DIM), F32)],
        params=_params(48, ("arbitrary", "arbitrary")), exchange=exchange,
        first=lambda qt, kt: (pl.program_id(0) == 0) & (pl.program_id(1) == 0),
        last=lambda qt, kt: (pl.program_id(0) == n_groups - 1) & (pl.program_id(1) == n_steps - 1))


def _out_project(o_hg, o_mla, x, g_a, w_out):
    t_len = x.shape[0]
    tm = min(512, t_len)
    half = N_HEADS * HEAD_DIM

    def body(ohg_ref, omla_ref, x_ref, ga_ref, w_ref, cat_ref, mix_ref, xhat_ref, rstd_ref):
        a = ohg_ref[...].astype(BF16)
        b = omla_ref[...].astype(BF16)
        cat_ref[:, 0:half] = a
        cat_ref[:, half:2 * half] = b
        mix = _dot(a, w_ref[0:half, :]) + _dot(b, w_ref[half:2 * half, :])
        mix_ref[...] = mix
        r1 = DN_ALPHA * x_ref[...] + (1.0 + ga_ref[...]) * mix
        xc = r1 - _rowmean(r1)
        rstd = lax.rsqrt(_rowmean(xc * xc) + LN_EPS)
        xhat_ref[...] = xc * rstd
        rstd_ref[...] = rstd

    row = lambda i: (i, 0)
    fixed = lambda i: (0, 0)
    return _pcall(
        body, name="out_project", grid=(t_len // tm,),
        out_shape=[jax.ShapeDtypeStruct((t_len, D_MODEL), BF16), jax.ShapeDtypeStruct((t_len, D_MODEL), F32),
                   jax.ShapeDtypeStruct((t_len, D_MODEL), F32), jax.ShapeDtypeStruct((t_len, 1), F32)],
        in_specs=[pl.BlockSpec((tm, half), row), pl.BlockSpec((tm, half), row), pl.BlockSpec((tm, D_MODEL), row),
                  pl.BlockSpec((1, D_MODEL), fixed), pl.BlockSpec((D_MODEL, D_MODEL), fixed)],
        out_specs=[pl.BlockSpec((tm, D_MODEL), row), pl.BlockSpec((tm, D_MODEL), row),
                   pl.BlockSpec((tm, D_MODEL), row), pl.BlockSpec((tm, 1), row)],
        compiler_params=_params(48, ("arbitrary",)),
        operands=(o_hg, o_mla, x, g_a, w_out))


V_LN1G, V_LN1B, V_SCM, V_SHM, V_GM, V_GA, V_LN2G, V_LN2B = range(8)
S_DLN2G, S_DLN2B, S_DGM, S_DSCM, S_DSHM, S_DLN1G, S_DLN1B, S_DGA, S_LOSS = range(9)


def _mlp_and_back(xhat1, rstd1, mix, target, o_mla, vecs, w1, w2, w_out):
    t_len = xhat1.shape[0]
    tm = min(256, t_len)
    n_ff = w1.shape[0]
    ff = w1.shape[2]

    def body(xhat_ref, rstd_ref, mix_ref, tgt_ref, omla_ref, vec_ref, w1_hbm, w2_hbm, wout_hbm,
             act_ref, dhp_ref, um_ref, dh_ref, dmix_ref, dcat_ref, dr1_ref, sums_ref, delta_ref,
             w1_s, w2_s, wout_s, hp_s, load_sems):
        @pl.when(pl.program_id(0) == 0)
        def _():
            loads = [pltpu.make_async_copy(w1_hbm, w1_s, load_sems.at[0]),
                     pltpu.make_async_copy(w2_hbm, w2_s, load_sems.at[1]),
                     pltpu.make_async_copy(wout_hbm, wout_s, load_sems.at[2])]
            for cp in loads:
                cp.start()
            sums_ref[...] = jnp.zeros_like(sums_ref)
            for cp in loads:
                cp.wait()

        vec = lambda r: vec_ref[r:r + 1, :]
        xhat = xhat_ref[...]
        x1 = xhat * vec(V_LN1G) + vec(V_LN1B)
        um = (x1 * (1.0 + vec(V_SCM)) + vec(V_SHM)).astype(BF16)
        um_ref[...] = um
        h = jnp.zeros((tm, D_MODEL), F32)
        for j in range(n_ff):
            hp = _dot(um, w1_s[j])
            hp_s[j] = hp
            act = jnp.square(jnp.maximum(hp, 0.0)).astype(BF16)
            act_ref[:, j * ff:(j + 1) * ff] = act
            h = h + _dot(act, w2_s[j])
        r2 = DN_ALPHA * x1 + (1.0 + vec(V_GM)) * h
        xc = r2 - _rowmean(r2)
        rstd2 = lax.rsqrt(_rowmean(xc * xc) + LN_EPS)
        xhat2 = xc * rstd2
        err = xhat2 * vec(V_LN2G) + vec(V_LN2B) - tgt_ref[...]
        loss = 0.5 * jnp.sum(_rowmean(err * err))
        dy = err * (1.0 / D_MODEL)
        dxh = dy * vec(V_LN2G)
        dr2 = rstd2 * (dxh - _rowmean(dxh) - xhat2 * _rowmean(dxh * xhat2))
        dh = ((1.0 + vec(V_GM)) * dr2).astype(BF16)
        dh_ref[...] = dh
        sums_ref[S_DLN2G:S_DLN2G + 1, :] += _colsum(dy * xhat2)
        sums_ref[S_DLN2B:S_DLN2B + 1, :] += _colsum(dy)
        sums_ref[S_DGM:S_DGM + 1, :] += _colsum(dr2 * h)
        sums_ref[S_LOSS:S_LOSS + 1, :] += jnp.full((1, D_MODEL), loss, F32)
        du = jnp.zeros((tm, D_MODEL), F32)
        for j in range(n_ff):
            dhp = (_dot_nt(dh, w2_s[j]) * (2.0 * jnp.maximum(hp_s[j], 0.0))).astype(BF16)
            dhp_ref[:, j * ff:(j + 1) * ff] = dhp
            du = du + _dot_nt(dhp, w1_s[j])
        sums_ref[S_DSCM:S_DSCM + 1, :] += _colsum(du * x1)
        sums_ref[S_DSHM:S_DSHM + 1, :] += _colsum(du)
        dx1 = DN_ALPHA * dr2 + du * (1.0 + vec(V_SCM))
        sums_ref[S_DLN1G:S_DLN1G + 1, :] += _colsum(dx1 * xhat)
        sums_ref[S_DLN1B:S_DLN1B + 1, :] += _colsum(dx1)
        dxh1 = dx1 * vec(V_LN1G)
        dr1 = rstd_ref[...] * (dxh1 - _rowmean(dxh1) - xhat * _rowmean(dxh1 * xhat))
        dr1_ref[...] = dr1
        sums_ref[S_DGA:S_DGA + 1, :] += _colsum(dr1 * mix_ref[...])
        dmix = ((1.0 + vec(V_GA)) * dr1).astype(BF16)
        dmix_ref[...] = dmix
        dcat = _dot_nt(dmix, wout_s[...])
        dcat_ref[...] = dcat
        ones = jnp.ones((8, HEAD_DIM), F32)
        half = N_HEADS * HEAD_DIM
        for hd in range(N_HEADS):
            prod = dcat[:, half + hd * HEAD_DIM:half + (hd + 1) * HEAD_DIM] * omla_ref[:, hd * HEAD_DIM:(hd + 1) * HEAD_DIM]
            delta_ref[hd] = lax.dot_general(ones, prod, (((1,), (1,)), ((), ())), preferred_element_type=F32,
                                            precision=lax.Precision.HIGHEST)[0:1]

    row = lambda i: (i, 0)
    fixed = lambda i: (0, 0)
    any_spec = pl.BlockSpec(memory_space=pl.ANY)
    return _pcall(
        body, name="mlp_and_back", grid=(t_len // tm,),
        out_shape=[jax.ShapeDtypeStruct((t_len, D_FF), BF16), jax.ShapeDtypeStruct((t_len, D_FF), BF16),
                   jax.ShapeDtypeStruct((t_len, D_MODEL), BF16), jax.ShapeDtypeStruct((t_len, D_MODEL), BF16),
                   jax.ShapeDtypeStruct((t_len, D_MODEL), BF16), jax.ShapeDtypeStruct((t_len, D_MODEL), F32),
                   jax.ShapeDtypeStruct((t_len, D_MODEL), F32), jax.ShapeDtypeStruct((16, D_MODEL), F32),
                   jax.ShapeDtypeStruct((N_HEADS, 1, t_len), F32)],
        in_specs=[pl.BlockSpec((tm, D_MODEL), row), pl.BlockSpec((tm, 1), row), pl.BlockSpec((tm, D_MODEL), row),
                  pl.BlockSpec((tm, D_MODEL), row), pl.BlockSpec((tm, N_HEADS * HEAD_DIM), row),
                  pl.BlockSpec((8, D_MODEL), fixed), any_spec, any_spec, any_spec],
        out_specs=[pl.BlockSpec((tm, D_FF), row), pl.BlockSpec((tm, D_FF), row), pl.BlockSpec((tm, D_MODEL), row),
                   pl.BlockSpec((tm, D_MODEL), row), pl.BlockSpec((tm, D_MODEL), row), pl.BlockSpec((tm, D_MODEL), row),
                   pl.BlockSpec((tm, D_MODEL), row), pl.BlockSpec((16, D_MODEL), fixed),
                   pl.BlockSpec((N_HEADS, 1, tm), lambda i: (0, 0, i))],
        scratch_shapes=[pltpu.VMEM(w1.shape, BF16), pltpu.VMEM(w2.shape, BF16), pltpu.VMEM(w_out.shape, BF16),
                        pltpu.VMEM((n_ff, tm, ff), F32), pltpu.SemaphoreType.DMA((3,))],
        compiler_params=_params(56, ("arbitrary",)),
        operands=(xhat1, rstd1, mix, target, o_mla, vecs, w1, w2, w_out))


def _in_project_backward(dq, dk, dv, cq, ckv, pos, invf, q_norm_w, kv_norm_w, w_q, w_kv,
                         d_hq, d_hf, d_hi, d_hg, w_in, dr1, x, sc_a, exchange=None):
    t_len = x.shape[0]
    tm = min(256, t_len)
    per_q = dq.shape[3] // tm
    hgw = N_HEADS * HEAD_DIM

    def body(dq_ref, dk_ref, dv_ref, cq_ref, ckv_ref, pos_ref, invf_ref, qn_ref, kvn_ref, wq_ref, wkv_ref,
             dhq_ref, dhf_ref, dhi_ref, dhg_ref, win_ref, dr1_ref, x_ref, sc_ref,
             dz_ref, dqf_ref, dkvu_ref, cqn_ref, ckvn_ref, gx_ref, sums_ref):
        @pl.when(pl.program_id(0) == 0)
        def _():
            sums_ref[...] = jnp.zeros_like(sums_ref)

        cos_t, sin_t = _rope_tables(pos_ref[...], invf_ref[...])
        cq = cq_ref[...]
        ckv = ckv_ref[...]
        rq = lax.rsqrt(_rowmean(cq * cq) + RMS_EPS)
        rkv = lax.rsqrt(_rowmean(ckv * ckv) + RMS_EPS)
        cqn_ref[...] = (cq * rq * qn_ref[...]).astype(BF16)
        ckvn_ref[...] = (ckv * rkv * kvn_ref[...]).astype(BF16)
        d_cqn = jnp.zeros((tm, Q_RANK), F32)
        d_ckvn = jnp.zeros((tm, KV_RANK), F32)
        d_kpe = jnp.zeros((tm, 128), F32)
        for h in range(N_HEADS):
            dqh = jnp.transpose(dq_ref[h])
            dqf_ref[h, :, 0:HEAD_DIM] = dqh[:, :HEAD_DIM].astype(BF16)
            dqf_ref[h, :, HEAD_DIM:QK_DIM] = _unrope(dqh[:, HEAD_DIM:], cos_t, sin_t).astype(BF16)
            d_cqn = d_cqn + _dot_nt(dqf_ref[h], wq_ref[h])
            dkh = dk_ref[h]
            d_kpe = d_kpe + dkh[:, HEAD_DIM:]
            dkvu_ref[h, :, 0:HEAD_DIM] = dkh[:, :HEAD_DIM].astype(BF16)
            dkvu_ref[h, :, HEAD_DIM:2 * HEAD_DIM] = dv_ref[h].astype(BF16)
            d_ckvn = d_ckvn + _dot_nt(dkvu_ref[h], wkv_ref[h])
        dyq = d_cqn * qn_ref[...]
        dykv = d_ckvn * kvn_ref[...]
        sums_ref[2:3, 0:Q_RANK] += _colsum(d_cqn * cq * rq)
        sums_ref[3:4, 0:KV_RANK] += _colsum(d_ckvn * ckv * rkv)
        dz_ref[:, 0:hgw] = dhq_ref[...]
        dz_ref[:, hgw:2 * hgw] = dhf_ref[...]
        dz_ref[:, 2 * hgw:3 * hgw] = dhi_ref[...]
        dz_ref[:, 3 * hgw:4 * hgw] = dhg_ref[...]
        dz_ref[:, HG_COLS:HG_COLS + Q_RANK] = (rq * dyq - cq * (rq * rq * rq) * _rowmean(dyq * cq)).astype(BF16)
        dz_ref[:, HG_COLS + Q_RANK:HG_COLS + Q_RANK + KV_RANK] = (
            rkv * dykv - ckv * (rkv * rkv * rkv) * _rowmean(dykv * ckv)).astype(BF16)
        dz_ref[:, HG_COLS + Q_RANK + KV_RANK:] = _unrope(d_kpe, cos_t, sin_t).astype(BF16)
        du = _dot_nt(dz_ref[...], win_ref[...])
        xv = x_ref[...]
        gx_ref[...] = DN_ALPHA * dr1_ref[...] + (1.0 + sc_ref[...]) * du
        sums_ref[0:1, :] += _colsum(du * xv)
        sums_ref[1:2, :] += _colsum(du)

    row = lambda i: (i, 0)
    fixed2 = lambda i: (0, 0)
    fixed3 = lambda i: (0, 0, 0)
    heads = lambda i: (0, i, 0)
    n_tiles = t_len // tm
    return _pallas(
        body, name="in_project_backward", grid=(n_tiles,),
        operands=(dq, dk, dv, cq, ckv, pos, invf, q_norm_w, kv_norm_w, w_q, w_kv, d_hq, d_hf, d_hi, d_hg, w_in, dr1, x,
                  sc_a),
        out_shape=[jax.ShapeDtypeStruct((t_len, IN_COLS_PAD), BF16), jax.ShapeDtypeStruct((N_HEADS, t_len, QK_DIM), BF16),
                   jax.ShapeDtypeStruct((N_HEADS, t_len, 2 * HEAD_DIM), BF16), jax.ShapeDtypeStruct((t_len, Q_RANK), BF16),
                   jax.ShapeDtypeStruct((t_len, KV_RANK), BF16), jax.ShapeDtypeStruct((t_len, D_MODEL), F32),
                   jax.ShapeDtypeStruct((8, D_MODEL), F32)],
        in_specs=[pl.BlockSpec((N_HEADS, None, QK_DIM, tm), lambda i: (0, i // per_q, 0, i % per_q)),
                  pl.BlockSpec((N_HEADS, tm, QK_DIM), heads),
                  pl.BlockSpec((N_HEADS, tm, HEAD_DIM), heads), pl.BlockSpec((tm, Q_RANK), row),
                  pl.BlockSpec((tm, KV_RANK), row), pl.BlockSpec((tm, 1), row), pl.BlockSpec((1, 128), fixed2),
                  pl.BlockSpec((1, Q_RANK), fixed2), pl.BlockSpec((1, KV_RANK), fixed2),
                  pl.BlockSpec((N_HEADS, Q_RANK, QK_DIM), fixed3), pl.BlockSpec((N_HEADS, KV_RANK, 2 * HEAD_DIM), fixed3),
                  pl.BlockSpec((tm, hgw), row), pl.BlockSpec((tm, hgw), row), pl.BlockSpec((tm, hgw), row),
                  pl.BlockSpec((tm, hgw), row), pl.BlockSpec((D_MODEL, IN_COLS_PAD), fixed2),
                  pl.BlockSpec((tm, D_MODEL), row), pl.BlockSpec((tm, D_MODEL), row), pl.BlockSpec((1, D_MODEL), fixed2)],
        out_specs=[pl.BlockSpec((tm, IN_COLS_PAD), row), pl.BlockSpec((N_HEADS, tm, QK_DIM), heads),
                   pl.BlockSpec((N_HEADS, tm, 2 * HEAD_DIM), heads), pl.BlockSpec((tm, Q_RANK), row),
                   pl.BlockSpec((tm, KV_RANK), row), pl.BlockSpec((tm, D_MODEL), row), pl.BlockSpec((8, D_MODEL), fixed2)],
        params=_params(48, ("arbitrary",)), exchange=exchange,
        first=lambda: pl.program_id(0) == 0, last=lambda: pl.program_id(0) == n_tiles - 1)


def _weight_grad(a, b, name, n_blocks, bn, a_blocked=False, b_blocked=True, exchange=None, token_tile=512):
    t_len = a.shape[0]
    m = a.shape[1] // n_blocks if a_blocked else a.shape[1]
    bt = min(token_tile, t_len)

    def body(a_ref, b_ref, o_ref):
        @pl.when(pl.program_id(1) == 0)
        def _():
            o_ref[...] = jnp.zeros_like(o_ref)

        o_ref[...] += _dot_tn(a_ref[...].astype(BF16), b_ref[...].astype(BF16))

    a_spec = pl.BlockSpec((bt, m), (lambda n, t: (t, n)) if a_blocked else (lambda n, t: (t, 0)))
    if b.ndim == 3:
        b_spec = pl.BlockSpec((None, bt, bn), lambda n, t: (n, t, 0))
    else:
        b_spec = pl.BlockSpec((bt, bn), (lambda n, t: (t, n)) if b_blocked else (lambda n, t: (t, 0)))
    nt = t_len // bt
    (out,), landed = _pallas(
        body, name=name, grid=(n_blocks, nt), operands=(a, b),
        out_shape=[jax.ShapeDtypeStruct((n_blocks, m, bn), F32)],
        in_specs=[a_spec, b_spec],
        out_specs=[pl.BlockSpec((None, m, bn), lambda n, t: (n, 0, 0))],
        params=_params(40, ("arbitrary", "arbitrary")), exchange=exchange,
        first=lambda: (pl.program_id(0) == 0) & (pl.program_id(1) == 0),
        last=lambda: (pl.program_id(0) == n_blocks - 1) & (pl.program_id(1) == nt - 1))
    return (out, landed) if exchange else out


def _reduce_small(gathered, lb_raw):
    def body(g_ref, lb_ref, tot_ref, dlb_ref):
        tot = g_ref[0]
        for d in range(1, N_DEV):
            tot = tot + g_ref[d]
        tot_ref[...] = tot
        a = lb_ref[...]
        m = jnp.max(a, axis=0, keepdims=True)
        e = jnp.exp(a - m)
        lb = e[0:1] / jnp.sum(e, axis=0, keepdims=True)
        d0 = tot[10:11, 0:512] * lb * (1.0 - lb)
        dlb_ref[0:1, :] = d0
        dlb_ref[1:2, :] = -d0

    return pl.pallas_call(
        body, name="reduce_small",
        out_shape=[jax.ShapeDtypeStruct((SMALL_ROWS, D_MODEL), F32), jax.ShapeDtypeStruct((2, 512), F32)],
    )(gathered, lb_raw)


def _adamw_update(w, gv, m, v):
    nm = ADAM_B1 * m + (1.0 - ADAM_B1) * gv
    nv = ADAM_B2 * v + (1.0 - ADAM_B2) * jnp.square(gv)
    m_hat = nm / (1.0 - ADAM_B1 ** ADAM_STEP)
    v_hat = nv / (1.0 - ADAM_B2 ** ADAM_STEP)
    return -ADAM_LR * (m_hat / (jnp.sqrt(v_hat) + ADAM_EPS) + ADAM_WD * w), nm, nv


def _adamw_halves(core, w, mine, theirs, m, v, name):
    rows, cols = w.shape
    h = rows // 2
    tr = _row_tile(h)
    per_half = h // tr

    def body(core_ref, w_ref, mine_ref, theirs_ref, m_ref, v_ref, g_ref, d_ref, nm_ref, nv_ref):
        is_mine = pl.program_id(0) // per_half == core_ref[0]
        gv = jnp.where(is_mine, mine_ref[...], theirs_ref[...])
        g_ref[...] = gv
        d_ref[...], nm_ref[...], nv_ref[...] = _adamw_update(w_ref[...], gv, m_ref[...], v_ref[...])

    full = pl.BlockSpec((tr, cols), lambda i, core_ref: (i, 0))
    part = pl.BlockSpec((tr, cols), lambda i, core_ref: (i % per_half, 0))
    return _pcall(
        body, name=name, out_shape=[jax.ShapeDtypeStruct(w.shape, F32)] * 4,
        grid_spec=pltpu.PrefetchScalarGridSpec(
            num_scalar_prefetch=1, grid=(rows // tr,), in_specs=[full, part, part, full, full], out_specs=[full] * 4),
        compiler_params=_params(40, ("arbitrary",)),
        operands=(core, w, mine, theirs, m, v))


def _adamw(w, g, m, v, name):
    rows, cols = w.shape
    tr = _row_tile(rows) if rows >= 8 else rows

    def body(w_ref, g_ref, m_ref, v_ref, d_ref, nm_ref, nv_ref):
        d_ref[...], nm_ref[...], nv_ref[...] = _adamw_update(w_ref[...], g_ref[...], m_ref[...], v_ref[...])

    spec = pl.BlockSpec((tr, cols), lambda i: (i, 0))
    return _pcall(
        body, name=name, grid=(rows // tr,),
        out_shape=[jax.ShapeDtypeStruct(w.shape, F32)] * 3,
        in_specs=[spec] * 4, out_specs=[spec] * 3,
        compiler_params=_params(40, ("arbitrary",)),
        operands=(w, g, m, v))


def kernel(x, c, positions, w_ada, b_ada, w_in, hg_lower_bounds, hg_norm_w, mla_q_norm_w, w_q_up, mla_kv_norm_w, w_kv_up, w_out, ln1_g, ln1_b, w_mlp_in, w_mlp_out, ln2_g, ln2_b, loss_target, m_w_ada, m_b_ada, m_w_in, m_hg_lower_bounds, m_hg_norm_w, m_mla_q_norm_w, m_w_q_up, m_mla_kv_norm_w, m_w_kv_up, m_w_out, m_ln1_g, m_ln1_b, m_w_mlp_in, m_w_mlp_out, m_ln2_g, m_ln2_b, v_w_ada, v_b_ada, v_w_in, v_hg_lower_bounds, v_hg_norm_w, v_mla_q_norm_w, v_w_q_up, v_mla_kv_norm_w, v_w_kv_up, v_w_out, v_ln1_g, v_ln1_b, v_w_mlp_in, v_w_mlp_out, v_ln2_g, v_ln2_b):
    ix, iy, ic = _mesh_pos()
    chip = 2 * ix + iy
    me = 4 * ix + 2 * iy + ic
    core_arr = jnp.reshape(ic, (1,)).astype(jnp.int32)
    chip_arr = jnp.reshape(chip, (1,)).astype(jnp.int32)

    xs = x[0]
    target = loss_target[0]
    t_len = xs.shape[0]
    pos = positions.astype(F32).reshape(t_len, 1)
    inv = 1.0 / (ROPE_THETA ** (jnp.arange(0, ROPE_DIM, 2, dtype=F32) / ROPE_DIM))
    invf = jnp.concatenate([inv, inv, jnp.zeros((128 - ROPE_DIM,), F32)]).reshape(1, 128)

    def slot(w):
        rows, cols = w.shape
        own = w.astype(BF16).reshape(1, 2, rows // 2, cols)
        return lax.dynamic_update_slice(jnp.zeros((N_CHIPS, 2, rows // 2, cols), BF16), own, (chip, 0, 0, 0))

    def slot8(a):
        return lax.dynamic_update_slice(jnp.zeros((N_DEV,) + a.shape, a.dtype), a[None], (me, 0, 0))

    def whole(s):
        return s.reshape(N_CHIPS, 2 * s.shape[2], s.shape[3])

    def halved(g):
        return g.reshape(N_CHIPS, 2, g.shape[1] // 2, g.shape[2])

    ada_cols = w_ada.shape[2]
    c_all, *early = _run_exchange(
        _merge(_gather_all(slot8(jnp.broadcast_to(c, (8, D_MODEL)))),
               _gather_over_ici([slot(w_in[0]), slot(w_q_up[0]), slot(w_kv_up[0])])), "gather_c_and_mixer_weights_ici")
    b_shard = lax.dynamic_slice(b_ada, (0, chip * ada_cols), (1, ada_cols))
    mod_cols, cond16 = _ada_project(c_all[:, 0, :], w_ada[0], b_shard)
    mod_all, *early = _run_exchange(_merge(_gather_all(slot8(mod_cols)), _gather_over_d2d(early)),
                                    "gather_mod_and_mixer_weights_d2d")
    mod_mine = lax.dynamic_slice(mod_all, (0, me, 0), (N_DEV, 1, ada_cols))[::2, 0, :].reshape(6, D_MODEL)
    sh_a, sc_a, g_a, sh_m, sc_m, g_m = (mod_mine[i:i + 1] for i in range(6))
    g_in, g_q, g_kv = (whole(s) for s in early)
    w_in_full = jnp.transpose(g_in, (1, 0, 2)).reshape(D_MODEL, IN_COLS)
    w_in_full = jnp.pad(w_in_full, ((0, 0), (0, IN_COLS_PAD - IN_COLS)))
    w_q_full = jnp.pad(g_q, ((0, 0), (0, 0), (0, QK_DIM - g_q.shape[2])))

    (u_a, zhg, cq, ckv, q, k, k_t, v, v_t), slots_a = _in_project(
        xs, pos, sc_a, sh_a, w_in_full, mla_q_norm_w, mla_kv_norm_w, w_q_full, g_kv, invf,
        _gather_over_ici([slot(w_mlp_in[0])]))
    (o_pre, o_hg, states), slots_b = _hgrn_forward(
        zhg, hg_lower_bounds, hg_norm_w, _gather_over_ici([slot(w_mlp_out[0]), slot(w_out[0])]))
    (o_mla, lse), mlp_slots = _attention_forward(q, k, v_t, _gather_over_d2d(list(slots_a) + list(slots_b)))
    g_w1, g_w2, g_out = (whole(s) for s in mlp_slots)
    w_out_full = g_out.reshape(D_MODEL, D_MODEL)
    cat, mix, xhat1, rstd1 = _out_project(o_hg, o_mla, xs, g_a, w_out_full)
    vecs = jnp.concatenate([ln1_g, ln1_b, sc_m, sh_m, g_m, g_a, ln2_g, ln2_b], axis=0)
    act, dhp, um, dh, dmix, d_cat, dr1, mlp_sums, delta = _mlp_and_back(
        xhat1, rstd1, mix, target, o_mla, vecs, g_w1, g_w2, w_out_full)

    gw_1 = _weight_grad(um, dhp, "grad_w_mlp_in", N_CHIPS, D_FF // N_CHIPS, token_tile=1024)
    gw_2 = _weight_grad(act, dh, "grad_w_mlp_out", N_CHIPS, D_MODEL, a_blocked=True, b_blocked=False, token_tile=1024)
    gw_out = _weight_grad(cat, dmix, "grad_w_out", 1, D_MODEL, token_tile=1024)
    gw_out = gw_out.reshape(N_CHIPS, D_MODEL // N_CHIPS, D_MODEL)
    mlp_grads = [halved(gw_1), halved(gw_2), halved(gw_out)]
    (dq, dk, dv), landed = _attention_backward(q, k, k_t, v, d_cat, lse, delta, _pair_exchange(mlp_grads))
    chip_sums = [_add_pair(core_arr, g, l) for g, l in zip(mlp_grads, landed)]
    (d_hq, d_hf, d_hi, d_hg, hg_sums), landed = _hgrn_backward(
        zhg, hg_lower_bounds, hg_norm_w, o_pre, d_cat, states, _chip_exchange([b for _, b in chip_sums]))
    mlp_mine = [_add_chips(chip_arr, p, l) for (p, _), l in zip(chip_sums, landed)]
    (dz, dqf, dkvu, cqn, ckvn, grad_x, in_sums), _ = _in_project_backward(
        dq, dk, dv, cq, ckv, pos, invf, mla_q_norm_w, mla_kv_norm_w, w_q_full, g_kv,
        d_hq, d_hf, d_hi, d_hg, w_in_full, dr1, xs, sc_a)

    gw_in, mlp_theirs = _weight_grad(u_a, dz, "grad_w_in", 3, IN_COLS_PAD // 3, exchange=_pair_send(mlp_mine),
                                     token_tile=1024)
    gw_in = jnp.transpose(gw_in, (1, 0, 2)).reshape(D_MODEL, IN_COLS_PAD)[:, :IN_COLS]
    gw_in = jnp.transpose(gw_in.reshape(D_MODEL, N_CHIPS, IN_COLS // N_CHIPS), (1, 0, 2))
    gw_q = _weight_grad(cqn, dqf, "grad_w_q_up", N_HEADS, QK_DIM, token_tile=2048)[:, :, :HEAD_DIM + ROPE_DIM]
    gw_kv = _weight_grad(ckvn, dkvu, "grad_w_kv_up", N_HEADS, 2 * HEAD_DIM, token_tile=2048)
    mixer_grads = [halved(g) for g in (gw_in, gw_q, gw_kv)]
    landed = _run_exchange(_pair_exchange(mixer_grads), "grad_pair_exchange")
    chip_sums = [_add_pair(core_arr, g, l) for g, l in zip(mixer_grads, landed)]
    landed = _run_exchange(_chip_exchange([b for _, b in chip_sums]), "grad_chip_exchange")
    mixer_mine = [_add_chips(chip_arr, p, l) for (p, _), l in zip(chip_sums, landed)]
    mixer_theirs = _run_exchange(_pair_send(mixer_mine), "grad_pair_send")
    reduced = ("w_in", "w_q_up", "w_kv_up", "w_mlp_in", "w_mlp_out", "w_out")
    halves_mine = dict(zip(reduced, mixer_mine + mlp_mine))
    halves_theirs = dict(zip(reduced, list(mixer_theirs) + list(mlp_theirs)))

    zeros = lambda n: jnp.zeros((1, n), F32)
    small = jnp.concatenate([
        in_sums[1:2], in_sums[0:1], mlp_sums[S_DGA:S_DGA + 1],
        mlp_sums[S_DSHM:S_DSHM + 1], mlp_sums[S_DSCM:S_DSCM + 1], mlp_sums[S_DGM:S_DGM + 1],
        mlp_sums[S_DLN1G:S_DLN1G + 1], mlp_sums[S_DLN1B:S_DLN1B + 1],
        mlp_sums[S_DLN2G:S_DLN2G + 1], mlp_sums[S_DLN2B:S_DLN2B + 1],
        jnp.concatenate([hg_sums[0:1], hg_sums[1:2]], axis=1),
        jnp.concatenate([in_sums[2:3, :Q_RANK], in_sums[3:4, :KV_RANK], zeros(D_MODEL - Q_RANK - KV_RANK)], axis=1),
        mlp_sums[S_LOSS:S_LOSS + 1],
        jnp.zeros((SMALL_ROWS - 13, D_MODEL), F32)], axis=0)
    small_all = _allgather8(small, "gather_small")
    tot, g_lb = _reduce_small(small_all, hg_lower_bounds)
    loss = tot[12, 0]
    g_b_ada = tot[0:6].reshape(1, 6 * D_MODEL)
    g_ln1_g, g_ln1_b, g_ln2_g, g_ln2_b = tot[6:7], tot[7:8], tot[8:9], tot[9:10]
    g_hg_norm = tot[10:11, 512:1024]
    g_q_norm = tot[11:12, 0:Q_RANK]
    g_kv_norm = tot[11:12, Q_RANK:Q_RANK + KV_RANK]

    d_mod_all = small_all[:, 0:6, :].reshape(N_DEV, 6 * D_MODEL)
    d_mod_cols = lax.dynamic_slice(d_mod_all, (0, chip * ada_cols), (N_DEV, ada_cols))
    d_mod_cols = jnp.concatenate([d_mod_cols, jnp.zeros_like(d_mod_cols)], axis=0)
    g_w_ada = _weight_grad(cond16, d_mod_cols, "grad_w_ada", 1, ada_cols)[0]

    names = ["w_ada", "b_ada", "w_in", "hg_lower_bounds", "hg_norm_w", "mla_q_norm_w", "w_q_up", "mla_kv_norm_w",
             "w_kv_up", "w_out", "ln1_g", "ln1_b", "w_mlp_in", "w_mlp_out", "ln2_g", "ln2_b"]
    weights = [w_ada, b_ada, w_in, hg_lower_bounds, hg_norm_w, mla_q_norm_w, w_q_up, mla_kv_norm_w,
               w_kv_up, w_out, ln1_g, ln1_b, w_mlp_in, w_mlp_out, ln2_g, ln2_b]
    moms = [m_w_ada, m_b_ada, m_w_in, m_hg_lower_bounds, m_hg_norm_w, m_mla_q_norm_w, m_w_q_up, m_mla_kv_norm_w,
            m_w_kv_up, m_w_out, m_ln1_g, m_ln1_b, m_w_mlp_in, m_w_mlp_out, m_ln2_g, m_ln2_b]
    vels = [v_w_ada, v_b_ada, v_w_in, v_hg_lower_bounds, v_hg_norm_w, v_mla_q_norm_w, v_w_q_up, v_mla_kv_norm_w,
            v_w_kv_up, v_w_out, v_ln1_g, v_ln1_b, v_w_mlp_in, v_w_mlp_out, v_ln2_g, v_ln2_b]
    grads2d = [g_w_ada, g_b_ada, None, g_lb, g_hg_norm, g_q_norm, None, g_kv_norm,
               None, None, g_ln1_g, g_ln1_b, None, None, g_ln2_g, g_ln2_b]
    out_g, out_d, out_m, out_v = [], [], [], []
    for name, w, g, m, vv in zip(names, weights, grads2d, moms, vels):
        if g is None:
            shape2 = w.shape[1:]
            g, d, nm, nv = _adamw_halves(core_arr, w.reshape(shape2), halves_mine[name], halves_theirs[name],
                                         m.reshape(shape2), vv.reshape(shape2), "adamw_" + name)
        else:
            shape2 = g.shape
            d, nm, nv = _adamw(w.reshape(shape2), g, m.reshape(shape2), vv.reshape(shape2), "adamw_" + name)
        out_g.append(g.reshape(w.shape))
        out_d.append(d.reshape(w.shape))
        out_m.append(nm.reshape(w.shape))
        out_v.append(nv.reshape(w.shape))
    return (loss, grad_x[None], *out_g, *out_d, *out_m, *out_v)
```

```python
import functools

import jax
import jax.numpy as jnp
from jax import lax
from jax.experimental import pallas as pl
from jax.experimental.pallas import tpu as pltpu

F32 = jnp.float32
BF16 = jnp.bfloat16
MESH_IDS = pl.DeviceIdType.MESH

D_MODEL = 1024
N_HEADS = 4
HEAD_DIM = 128
ROPE_DIM = 64
HG_CHUNK = 64
HG_COLS = 2048
Q_RANK = 256
KV_RANK = 256
IN_COLS = 2624
IN_COLS_PAD = 2688
QK_DIM = 256
D_FF = 4096
N_CHIPS = 4
N_DEV = 8
ROPE_THETA = 10000.0
RMS_EPS = 1e-6
LN_EPS = 1e-5
DN_ALPHA = 2.0 ** 0.25
ATT_SCALE = (HEAD_DIM + ROPE_DIM) ** -0.5
NEG_BIG = -1e30
ADAM_LR = 0.001
ADAM_B1 = 0.9
ADAM_B2 = 0.999
ADAM_EPS = 1e-08
ADAM_WD = 0.01
ADAM_STEP = 10
SMALL_ROWS = 16
MIB = 1024 * 1024


def _dot(a, b):
    return jnp.dot(a, b, preferred_element_type=F32)


def _dot_nt(a, b):
    return lax.dot_general(a, b, (((1,), (1,)), ((), ())), preferred_element_type=F32)


def _dot_tn(a, b):
    return lax.dot_general(a, b, (((0,), (0,)), ((), ())), preferred_element_type=F32)


def _params(vmem_mib, semantics=None):
    return pltpu.CompilerParams(vmem_limit_bytes=vmem_mib * MIB, dimension_semantics=semantics)


def _sigmoid(v):
    return 1.0 / (1.0 + jnp.exp(-v))


def _colsum(v):
    return jnp.sum(v, axis=0, keepdims=True)


def _rowmean(v):
    return jnp.mean(v, axis=-1, keepdims=True)


def _rope_tables(pos, invf):
    ang = pos * invf
    lane = lax.broadcasted_iota(jnp.int32, ang.shape, 1)
    cos_t = jnp.where(lane < ROPE_DIM, jnp.cos(ang), 0.0)
    sin = jnp.sin(ang)
    sin_t = jnp.where(lane < ROPE_DIM // 2, -sin, jnp.where(lane < ROPE_DIM, sin, 0.0))
    return cos_t, sin_t


def _swap_halves(t):
    lane = lax.broadcasted_iota(jnp.int32, t.shape, 1)
    return jnp.where(lane < ROPE_DIM // 2, pltpu.roll(t, 128 - ROPE_DIM // 2, 1), pltpu.roll(t, ROPE_DIM // 2, 1))


def _rope(t, cos_t, sin_t):
    return t * cos_t + _swap_halves(t) * sin_t


def _unrope(g, cos_t, sin_t):
    return g * cos_t - _swap_halves(g) * sin_t


def _mesh_pos():
    return lax.axis_index("x"), lax.axis_index("y"), lax.axis_index("c")


def _other_chips(x, y):
    out = []
    for dx, dy in ((1, 0), (0, 1), (1, 1)):
        px = 1 - x if dx else x
        py = 1 - y if dy else y
        out.append(((px, py), 2 * px + py))
    return out


def _allgather8(a, name):
    rows, cols = a.shape

    def body(a_ref, out_ref, send_sems, recv_sems):
        x, y, c = _mesh_pos()
        me = 4 * x + 2 * y + c
        out_ref[me] = a_ref[...]
        peers = []
        for r in range(1, N_DEV):
            px = 1 - x if r & 4 else x
            py = 1 - y if r & 2 else y
            pc = 1 - c if r & 1 else c
            peers.append(((px, py, pc), 4 * px + 2 * py + pc))

        def copy(r, block, to):
            return pltpu.make_async_remote_copy(
                src_ref=a_ref, dst_ref=out_ref.at[block], send_sem=send_sems.at[r], recv_sem=recv_sems.at[r],
                device_id=to, device_id_type=MESH_IDS)

        sends = [copy(r, me, peer) for r, (peer, _) in enumerate(peers)]
        for cp in sends:
            cp.start()
        for r, (peer, idx) in enumerate(peers):
            copy(r, idx, peer).wait_recv()
        for cp in sends:
            cp.wait_send()

    return pl.pallas_call(
        body, name=name,
        out_shape=jax.ShapeDtypeStruct((N_DEV, rows, cols), a.dtype),
        in_specs=[pl.BlockSpec(memory_space=pltpu.VMEM)],
        out_specs=pl.BlockSpec(memory_space=pltpu.VMEM),
        scratch_shapes=[pltpu.SemaphoreType.DMA((N_DEV - 1,)), pltpu.SemaphoreType.DMA((N_DEV - 1,))],
    )(a)


class _Exchange:
    def __init__(self, inputs, out_shapes, aliases, sems, start, finish):
        self.inputs, self.out_shapes, self.aliases, self.sems = list(inputs), list(out_shapes), dict(aliases), list(sems)
        self.start, self.finish = start, finish


def _from_copies(inputs, out_shapes, aliases, sems, copies):
    def start(ins, outs, sem_refs):
        for send, _ in copies(ins, outs, sem_refs):
            send.start()

    def finish(ins, outs, sem_refs):
        for send, recv in copies(ins, outs, sem_refs):
            recv.wait_recv()
            send.wait_send()

    return _Exchange(inputs, out_shapes, aliases, sems, start, finish)


HBM_MIN_BYTES = 256 * 1024


def _in_hbm(a):
    if a.size * a.dtype.itemsize < HBM_MIN_BYTES:
        return a
    return pltpu.with_memory_space_constraint(a, pltpu.HBM)


def _out_hbm(s):
    if s.size * s.dtype.itemsize < HBM_MIN_BYTES:
        return s
    return pltpu.HBM(s.shape, s.dtype)


def _pcall(body, *, operands, out_shape, **kwargs):
    single = not isinstance(out_shape, (list, tuple))
    shapes = [_out_hbm(s) for s in ([out_shape] if single else out_shape)]
    return pl.pallas_call(body, out_shape=shapes[0] if single else shapes, **kwargs)(*[_in_hbm(a) for a in operands])


def _run_exchange(exchange, name):
    n_in, n_out = len(exchange.inputs), len(exchange.out_shapes)

    def body(*refs):
        ins, outs, sem_refs = refs[:n_in], refs[n_in:n_in + n_out], refs[n_in + n_out:]
        exchange.start(ins, outs, sem_refs)
        exchange.finish(ins, outs, sem_refs)

    any_spec = pl.BlockSpec(memory_space=pl.ANY)
    return pl.pallas_call(
        body, name=name, out_shape=[_out_hbm(s) for s in exchange.out_shapes],
        in_specs=[any_spec] * n_in, out_specs=[any_spec] * n_out,
        scratch_shapes=exchange.sems, input_output_aliases=exchange.aliases,
    )(*[_in_hbm(a) for a in exchange.inputs])


def _pallas(body, *, name, operands, in_specs, out_shape, out_specs, params, scratch_shapes=(), grid=(), prefetch=(),
            exchange=None, first=None, last=None):
    n_pre, n_in, n_out, n_scr = len(prefetch), len(in_specs), len(out_specs), len(scratch_shapes)
    ex_in = exchange.inputs if exchange else []
    ex_out = exchange.out_shapes if exchange else []
    ex_sems = exchange.sems if exchange else []

    def full_body(*refs):
        pre, rest = refs[:n_pre], refs[n_pre:]
        ins, rest = rest[:n_in], rest[n_in:]
        xin, rest = rest[:len(ex_in)], rest[len(ex_in):]
        outs, rest = rest[:n_out], rest[n_out:]
        xout, rest = rest[:len(ex_out)], rest[len(ex_out):]
        scr, sem_refs = rest[:n_scr], rest[n_scr:]
        if exchange:
            @pl.when(first(*pre))
            def _():
                exchange.start(xin, xout, sem_refs)

        body(*pre, *ins, *outs, *scr)
        if exchange:
            @pl.when(last(*pre))
            def _():
                exchange.finish(xin, xout, sem_refs)

    any_spec = pl.BlockSpec(memory_space=pl.ANY)
    aliases = {n_pre + n_in + i: n_out + o for i, o in exchange.aliases.items()} if exchange else {}
    operands = [_in_hbm(a) for a in operands]
    results = pl.pallas_call(
        full_body, name=name, out_shape=[_out_hbm(s) for s in list(out_shape) + ex_out],
        grid_spec=pltpu.PrefetchScalarGridSpec(
            num_scalar_prefetch=n_pre, grid=grid, in_specs=list(in_specs) + [any_spec] * len(ex_in),
            out_specs=list(out_specs) + [any_spec] * len(ex_out), scratch_shapes=list(scratch_shapes) + ex_sems),
        input_output_aliases=aliases, compiler_params=params,
    )(*prefetch, *operands, *[_in_hbm(a) for a in ex_in])
    return results[:n_out], results[n_out:]


def _remote(src, dst, sems, idx, to):
    send_sems, recv_sems = sems
    return pltpu.make_async_remote_copy(src_ref=src, dst_ref=dst, send_sem=send_sems.at[idx], recv_sem=recv_sems.at[idx],
                                        device_id=to, device_id_type=MESH_IDS)


def _sem_pairs(*shape):
    return [pltpu.SemaphoreType.DMA(shape), pltpu.SemaphoreType.DMA(shape)]


def _same_shapes(arrays):
    return [jax.ShapeDtypeStruct(a.shape, a.dtype) for a in arrays]


def _gather_over_ici(slots):
    n = len(slots)

    def copies(ins, outs, sems):
        x, y, c = _mesh_pos()
        k = 2 * x + y
        out = []
        for j, (chip, kj) in enumerate(_other_chips(x, y)):
            for i in range(n):
                to = (*chip, c)
                out.append((_remote(ins[i].at[k, c], outs[i].at[k, c], sems, (j, i), to),
                            _remote(ins[i].at[k, c], outs[i].at[kj, c], sems, (j, i), to)))
        return out

    return _from_copies(slots, _same_shapes(slots), {i: i for i in range(n)}, _sem_pairs(3, n), copies)


def _gather_over_d2d(slots):
    n = len(slots)

    def copies(ins, outs, sems):
        x, y, c = _mesh_pos()
        sibling = (x, y, 1 - c)
        out = []
        for j, (_, kj) in enumerate(_other_chips(x, y)):
            for i in range(n):
                out.append((_remote(ins[i].at[kj, c], outs[i].at[kj, c], sems, (j, i), sibling),
                            _remote(ins[i].at[kj, c], outs[i].at[kj, 1 - c], sems, (j, i), sibling)))
        return out

    return _from_copies(slots, _same_shapes(slots), {i: i for i in range(n)}, _sem_pairs(3, n), copies)


def _gather_all(slots8):
    def copies(ins, outs, sems):
        x, y, c = _mesh_pos()
        me = 4 * x + 2 * y + c
        out = []
        for r in range(1, N_DEV):
            px = 1 - x if r & 4 else x
            py = 1 - y if r & 2 else y
            pc = 1 - c if r & 1 else c
            to = (px, py, pc)
            out.append((_remote(ins[0].at[me], outs[0].at[me], sems, r - 1, to),
                        _remote(ins[0].at[me], outs[0].at[4 * px + 2 * py + pc], sems, r - 1, to)))
        return out

    return _from_copies([slots8], _same_shapes([slots8]), {0: 0}, _sem_pairs(N_DEV - 1), copies)


def _merge(first, second):
    n_in, n_out, n_sem = len(first.inputs), len(first.out_shapes), len(first.sems)

    def start(ins, outs, sems):
        first.start(ins[:n_in], outs[:n_out], sems[:n_sem])
        second.start(ins[n_in:], outs[n_out:], sems[n_sem:])

    def finish(ins, outs, sems):
        first.finish(ins[:n_in], outs[:n_out], sems[:n_sem])
        second.finish(ins[n_in:], outs[n_out:], sems[n_sem:])

    aliases = dict(first.aliases)
    aliases.update({n_in + i: n_out + o for i, o in second.aliases.items()})
    return _Exchange(first.inputs + second.inputs, first.out_shapes + second.out_shapes, aliases,
                     first.sems + second.sems, start, finish)


def _pair_exchange(grads):
    n = len(grads)

    def copies(ins, outs, sems):
        x, y, c = _mesh_pos()
        cps = [_remote(ins[i].at[:, 1 - c], outs[i], sems, i, (x, y, 1 - c)) for i in range(n)]
        return [(cp, cp) for cp in cps]

    shapes = [jax.ShapeDtypeStruct((N_CHIPS,) + g.shape[2:], g.dtype) for g in grads]
    return _from_copies(grads, shapes, {}, _sem_pairs(n), copies)


def _chip_exchange(partials):
    n = len(partials)

    def copies(ins, outs, sems):
        x, y, c = _mesh_pos()
        cps = [_remote(ins[i].at[kj], outs[i].at[j], sems, (j, i), (*chip, c))
               for j, (chip, kj) in enumerate(_other_chips(x, y)) for i in range(n)]
        return [(cp, cp) for cp in cps]

    shapes = [jax.ShapeDtypeStruct((3,) + p.shape[1:], p.dtype) for p in partials]
    return _from_copies(partials, shapes, {}, _sem_pairs(3, n), copies)


def _pair_send(halves):
    n = len(halves)

    def copies(ins, outs, sems):
        x, y, c = _mesh_pos()
        cps = [_remote(ins[i], outs[i], sems, i, (x, y, 1 - c)) for i in range(n)]
        return [(cp, cp) for cp in cps]

    return _from_copies(halves, _same_shapes(halves), {}, _sem_pairs(n), copies)


def _row_tile(rows):
    for t in (256, 128, 64, 32, 16, 8):
        if rows % t == 0:
            return t
    return rows


def _add_pair(core, grad, landed):
    _, h, cols = landed.shape
    tr = _row_tile(h)

    def body(core_ref, g_ref, l_ref, o_ref, ob_ref):
        s = g_ref[...] + l_ref[...]
        o_ref[...] = s
        ob_ref[...] = s.astype(BF16)

    out_spec = pl.BlockSpec((None, tr, cols), lambda k, t, core_ref: (k, t, 0))
    return _pcall(
        body, name="grad_add_pair",
        out_shape=[jax.ShapeDtypeStruct(landed.shape, F32), jax.ShapeDtypeStruct(landed.shape, BF16)],
        grid_spec=pltpu.PrefetchScalarGridSpec(
            num_scalar_prefetch=1, grid=(N_CHIPS, h // tr),
            in_specs=[pl.BlockSpec((None, None, tr, cols), lambda k, t, core_ref: (k, core_ref[0], t, 0)),
                      pl.BlockSpec((None, tr, cols), lambda k, t, core_ref: (k, t, 0))],
            out_specs=[out_spec, out_spec]),
        compiler_params=_params(32, ("arbitrary", "arbitrary")),
        operands=(core, grad, landed))


def _add_chips(chip, partial, landed):
    _, h, cols = partial.shape
    tr = _row_tile(h)

    def body(chip_ref, p_ref, l_ref, o_ref):
        o_ref[...] = ((p_ref[...] + l_ref[0].astype(F32)) + l_ref[1].astype(F32)) + l_ref[2].astype(F32)

    return _pcall(
        body, name="grad_add_chips",
        out_shape=jax.ShapeDtypeStruct((h, cols), F32),
        grid_spec=pltpu.PrefetchScalarGridSpec(
            num_scalar_prefetch=1, grid=(h // tr,),
            in_specs=[pl.BlockSpec((None, tr, cols), lambda t, chip_ref: (chip_ref[0], t, 0)),
                      pl.BlockSpec((3, tr, cols), lambda t, chip_ref: (0, t, 0))],
            out_specs=pl.BlockSpec((tr, cols), lambda t, chip_ref: (t, 0))),
        compiler_params=_params(32, ("arbitrary",)),
        operands=(chip, partial, landed))


def _ada_project(c_all, w_ada, b_shard):
    n = w_ada.shape[1]
    tn = 512

    def body(c_ref, w_ref, b_ref, mod_ref, cond_ref):
        cv = c_ref[...]
        cond = cv * _sigmoid(cv)
        mod_ref[...] = _dot(cond.astype(BF16), w_ref[...].astype(BF16)) + b_ref[...]
        cond_ref[0:N_DEV, :] = cond
        cond_ref[N_DEV:2 * N_DEV, :] = jnp.zeros_like(cond)

    return _pcall(
        body, name="ada_project", grid=(n // tn,),
        out_shape=[jax.ShapeDtypeStruct((N_DEV, n), F32), jax.ShapeDtypeStruct((2 * N_DEV, D_MODEL), F32)],
        in_specs=[pl.BlockSpec((N_DEV, D_MODEL), lambda j: (0, 0)), pl.BlockSpec((D_MODEL, tn), lambda j: (0, j)),
                  pl.BlockSpec((1, tn), lambda j: (0, j))],
        out_specs=[pl.BlockSpec((N_DEV, tn), lambda j: (0, j)), pl.BlockSpec((2 * N_DEV, D_MODEL), lambda j: (0, 0))],
        compiler_params=_params(32, ("arbitrary",)),
        operands=(c_all, w_ada, b_shard))


def _in_project(x, pos, sc_a, sh_a, w_in, q_norm_w, kv_norm_w, w_q, w_kv, invf, exchange=None):
    t_len = x.shape[0]
    tm = min(512, t_len)

    def body(x_ref, pos_ref, sc_ref, sh_ref, win_ref, qn_ref, kvn_ref, wq_ref, wkv_ref, invf_ref,
             u_ref, zhg_ref, cq_ref, ckv_ref, q_ref, k_ref, kt_ref, v_ref, vt_ref):
        u = (x_ref[...] * (1.0 + sc_ref[...]) + sh_ref[...]).astype(BF16)
        u_ref[...] = u
        z = _dot(u, win_ref[...])
        zhg_ref[...] = z[:, :HG_COLS]
        cq = z[:, HG_COLS:HG_COLS + Q_RANK]
        ckv = z[:, HG_COLS + Q_RANK:HG_COLS + Q_RANK + KV_RANK]
        cq_ref[...] = cq
        ckv_ref[...] = ckv
        cos_t, sin_t = _rope_tables(pos_ref[...], invf_ref[...])
        k_pe = _rope(z[:, HG_COLS + Q_RANK + KV_RANK:], cos_t, sin_t)
        k_pe_t = jnp.transpose(k_pe).astype(BF16)
        cqn = (cq * lax.rsqrt(_rowmean(cq * cq) + RMS_EPS) * qn_ref[...]).astype(BF16)
        ckvn = (ckv * lax.rsqrt(_rowmean(ckv * ckv) + RMS_EPS) * kvn_ref[...]).astype(BF16)
        for h in range(N_HEADS):
            qh = _dot(cqn, wq_ref[h])
            q_ref[h, :, 0:HEAD_DIM] = qh[:, :HEAD_DIM].astype(BF16)
            q_ref[h, :, HEAD_DIM:QK_DIM] = _rope(qh[:, HEAD_DIM:], cos_t, sin_t).astype(BF16)
            kvh = _dot(ckvn, wkv_ref[h])
            k_ref[h, :, 0:HEAD_DIM] = kvh[:, :HEAD_DIM].astype(BF16)
            k_ref[h, :, HEAD_DIM:QK_DIM] = k_pe.astype(BF16)
            kt_ref[h, 0:HEAD_DIM, :] = jnp.transpose(kvh[:, :HEAD_DIM]).astype(BF16)
            kt_ref[h, HEAD_DIM:QK_DIM, :] = k_pe_t
            v_ref[h] = kvh[:, HEAD_DIM:].astype(BF16)
            vt_ref[h] = jnp.transpose(kvh[:, HEAD_DIM:]).astype(BF16)

    row = lambda i: (i, 0)
    fixed2 = lambda i: (0, 0)
    fixed3 = lambda i: (0, 0, 0)
    heads = lambda i: (0, i, 0)
    n_tiles = t_len // tm
    return _pallas(
        body, name="in_project", grid=(n_tiles,),
        operands=(x, pos, sc_a, sh_a, w_in, q_norm_w, kv_norm_w, w_q, w_kv, invf),
        out_shape=[jax.ShapeDtypeStruct((t_len, D_MODEL), BF16), jax.ShapeDtypeStruct((t_len, HG_COLS), F32),
                   jax.ShapeDtypeStruct((t_len, Q_RANK), F32), jax.ShapeDtypeStruct((t_len, KV_RANK), F32),
                   jax.ShapeDtypeStruct((N_HEADS, t_len, QK_DIM), BF16),
                   jax.ShapeDtypeStruct((N_HEADS, t_len, QK_DIM), BF16),
                   jax.ShapeDtypeStruct((N_HEADS, QK_DIM, t_len), BF16),
                   jax.ShapeDtypeStruct((N_HEADS, t_len, HEAD_DIM), BF16),
                   jax.ShapeDtypeStruct((N_HEADS, HEAD_DIM, t_len), BF16)],
        in_specs=[pl.BlockSpec((tm, D_MODEL), row), pl.BlockSpec((tm, 1), row),
                  pl.BlockSpec((1, D_MODEL), fixed2), pl.BlockSpec((1, D_MODEL), fixed2),
                  pl.BlockSpec((D_MODEL, IN_COLS_PAD), fixed2),
                  pl.BlockSpec((1, Q_RANK), fixed2), pl.BlockSpec((1, KV_RANK), fixed2),
                  pl.BlockSpec((N_HEADS, Q_RANK, QK_DIM), fixed3), pl.BlockSpec((N_HEADS, KV_RANK, 2 * HEAD_DIM), fixed3),
                  pl.BlockSpec((1, 128), fixed2)],
        out_specs=[pl.BlockSpec((tm, D_MODEL), row), pl.BlockSpec((tm, HG_COLS), row),
                   pl.BlockSpec((tm, Q_RANK), row), pl.BlockSpec((tm, KV_RANK), row),
                   pl.BlockSpec((N_HEADS, tm, QK_DIM), heads), pl.BlockSpec((N_HEADS, tm, QK_DIM), heads),
                   pl.BlockSpec((N_HEADS, QK_DIM, tm), lambda i: (0, 0, i)),
                   pl.BlockSpec((N_HEADS, tm, HEAD_DIM), heads),
                   pl.BlockSpec((N_HEADS, HEAD_DIM, tm), lambda i: (0, 0, i))],
        params=_params(48, ("arbitrary",)), exchange=exchange,
        first=lambda: pl.program_id(0) == 0, last=lambda: pl.program_id(0) == n_tiles - 1)


def _lower_bound(lb_raw):
    m = jnp.max(lb_raw, axis=0, keepdims=True)
    e = jnp.exp(lb_raw - m)
    return e[0:1] / jnp.sum(e, axis=0, keepdims=True)


def _tri(inclusive_lower):
    r = lax.broadcasted_iota(jnp.int32, (HG_CHUNK, HG_CHUNK), 0)
    c = lax.broadcasted_iota(jnp.int32, (HG_CHUNK, HG_CHUNK), 1)
    return (c <= r) if inclusive_lower else (c >= r)


def _chunk_rows(n):
    return slice(n * HG_CHUNK, (n + 1) * HG_CHUNK)


def _chunk_prefix_sums(v, inclusive_lower):
    tri = _tri(inclusive_lower).astype(BF16)
    hi = v.astype(BF16)
    rest = v - hi.astype(F32)
    mid = rest.astype(BF16)
    lo = (rest - mid.astype(F32)).astype(BF16)
    pieces = jnp.concatenate([hi, mid, lo], axis=1)
    out = []
    for n in range(v.shape[0] // HG_CHUNK):
        s = _dot(tri, pieces[_chunk_rows(n)])
        out.append((s[:, 0:HEAD_DIM] + s[:, HEAD_DIM:2 * HEAD_DIM]) + s[:, 2 * HEAD_DIM:])
    return jnp.concatenate(out, axis=0)


def _per_chunk(v, row):
    n = v.shape[0] // HG_CHUNK
    v3 = v.reshape(n, HG_CHUNK, HEAD_DIM)
    return jnp.broadcast_to(v3[:, row:row + 1, :], v3.shape).reshape(v.shape)


def _hg_block(q, f_logit, lb):
    sg = _sigmoid(f_logit)
    forget = lb + (1.0 - lb) * sg
    kk = 1.0 - forget
    b = _chunk_prefix_sums(jnp.log(forget), True)
    b_ref = _per_chunk(b, HG_CHUNK // 2 - 1)
    b_last = _per_chunk(b, HG_CHUNK - 1)
    e_i = jnp.exp(b - b_ref)
    e_ri = jnp.exp(b_ref - b)
    e_b = jnp.exp(b)
    e_l = jnp.exp(b_last - b)
    return dict(sg=sg, forget=forget, e_i=e_i, e_ri=e_ri, e_b=e_b, e_l=e_l, dec=jnp.exp(b_last),
                qi=q * e_i, ki=kk * e_ri, qe=q * e_b, kl=kk * e_l)


def _hgrn_forward(zhg, lb_raw, norm_w, exchange=None):
    t_len = zhg.shape[0]
    tb = min(512, t_len)
    n_chunks = tb // HG_CHUNK

    def body(q_ref, f_ref, v_ref, g_ref, lb_ref, w_ref, opre_ref, o_ref, st_ref, state):
        @pl.when(pl.program_id(1) == 0)
        def _():
            state[...] = jnp.zeros_like(state)

        blk = _hg_block(q_ref[...], f_ref[...], _lower_bound(lb_ref[...]))
        v = v_ref[...].astype(BF16)
        qi, ki, qe, kl = (blk[name].astype(BF16) for name in ("qi", "ki", "qe", "kl"))
        causal = _tri(True)
        st = state[...]
        parts = []
        for n in range(n_chunks):
            r = _chunk_rows(n)
            a = jnp.where(causal, _dot_nt(qi[r], ki[r]), 0.0).astype(BF16)
            st_ref[0, n] = st
            parts.append(_dot(a, v[r]) + _dot_nt(qe[r], st.astype(BF16)))
            st = st * blk["dec"][n * HG_CHUNK:n * HG_CHUNK + 1] + _dot_tn(v[r], kl[r])
        state[...] = st
        o = jnp.concatenate(parts, axis=0)
        opre_ref[...] = o
        g = g_ref[...]
        o_ref[...] = o * lax.rsqrt(_rowmean(o * o) + RMS_EPS) * w_ref[...] * (g * _sigmoid(g))

    col = lambda off: (lambda h, t: (t, off + h))
    nb = t_len // tb
    return _pallas(
        body, name="hgrn_forward", grid=(N_HEADS, nb), operands=(zhg, zhg, zhg, zhg, lb_raw, norm_w),
        out_shape=[jax.ShapeDtypeStruct((t_len, N_HEADS * HEAD_DIM), F32),
                   jax.ShapeDtypeStruct((t_len, N_HEADS * HEAD_DIM), F32),
                   jax.ShapeDtypeStruct((N_HEADS, t_len // HG_CHUNK, HEAD_DIM, HEAD_DIM), F32)],
        in_specs=[pl.BlockSpec((tb, HEAD_DIM), col(0)), pl.BlockSpec((tb, HEAD_DIM), col(N_HEADS)),
                  pl.BlockSpec((tb, HEAD_DIM), col(2 * N_HEADS)), pl.BlockSpec((tb, HEAD_DIM), col(3 * N_HEADS)),
                  pl.BlockSpec((2, HEAD_DIM), lambda h, t: (0, h)), pl.BlockSpec((1, HEAD_DIM), lambda h, t: (0, h))],
        out_specs=[pl.BlockSpec((tb, HEAD_DIM), col(0)), pl.BlockSpec((tb, HEAD_DIM), col(0)),
                   pl.BlockSpec((1, n_chunks, HEAD_DIM, HEAD_DIM), lambda h, t: (h, t, 0, 0))],
        scratch_shapes=[pltpu.VMEM((HEAD_DIM, HEAD_DIM), F32)],
        params=_params(32, ("arbitrary", "arbitrary")), exchange=exchange,
        first=lambda: (pl.program_id(0) == 0) & (pl.program_id(1) == 0),
        last=lambda: (pl.program_id(0) == N_HEADS - 1) & (pl.program_id(1) == nb - 1))


def _hgrn_backward(zhg, lb_raw, norm_w, o_pre, d_cat, states, exchange=None):
    t_len = zhg.shape[0]
    tb = min(512, t_len)
    n_chunks = tb // HG_CHUNK
    nb = t_len // tb

    def body(q_ref, f_ref, v_ref, g_ref, lb_ref, w_ref, opre_ref, do_ref, st_ref,
             dq_ref, df_ref, dv_ref, dg_ref, sums_ref, gstate):
        @pl.when(pl.program_id(1) == 0)
        def _():
            gstate[...] = jnp.zeros_like(gstate)
            sums_ref[...] = jnp.zeros_like(sums_ref)

        lb = _lower_bound(lb_ref[...])
        w = w_ref[...]
        o = opre_ref[...]
        g = g_ref[...]
        d_out = do_ref[...]
        r = lax.rsqrt(_rowmean(o * o) + RMS_EPS)
        sg_g = _sigmoid(g)
        dg_ref[...] = (d_out * (o * r * w) * (sg_g * (1.0 + g * (1.0 - sg_g)))).astype(BF16)
        d_on = d_out * (g * sg_g)
        sums_ref[1:2, :] += _colsum(d_on * o * r)
        dy = d_on * w
        d_o = (r * dy - o * (r * r * r) * _rowmean(dy * o)).astype(BF16)
        blk = _hg_block(q_ref[...], f_ref[...], lb)
        v = v_ref[...].astype(BF16)
        qi, ki, qe, kl = (blk[name].astype(BF16) for name in ("qi", "ki", "qe", "kl"))
        causal = _tri(True)
        row_id = lax.broadcasted_iota(jnp.int32, (HG_CHUNK, HEAD_DIM), 0)
        gt = gstate[...]
        d_v, d_qi, d_ki, d_qe, d_kl, d_dec = ([None] * n_chunks for _ in range(6))
        for n in reversed(range(n_chunks)):
            rows = _chunk_rows(n)
            st = st_ref[0, n]
            a = jnp.where(causal, _dot_nt(qi[rows], ki[rows]), 0.0).astype(BF16)
            d_a = jnp.where(causal, _dot_nt(d_o[rows], v[rows]), 0.0).astype(BF16)
            gt_b = gt.astype(BF16)
            d_v[n] = _dot_tn(a, d_o[rows]) + _dot_nt(kl[rows], gt_b)
            d_qi[n] = _dot(d_a, ki[rows])
            d_ki[n] = _dot_tn(d_a, qi[rows])
            d_qe[n] = _dot(d_o[rows], st.astype(BF16))
            d_kl[n] = _dot(v[rows], gt_b)
            d_dec[n] = jnp.where(row_id == HG_CHUNK - 1, _colsum(gt * st), 0.0)
            gt = gt * blk["dec"][n * HG_CHUNK:n * HG_CHUNK + 1] + _dot_tn(d_o[rows], qe[rows])
        gstate[...] = gt
        d_qi, d_ki, d_qe, d_kl, d_dec = (jnp.concatenate(p, axis=0) for p in (d_qi, d_ki, d_qe, d_kl, d_dec))
        dv_ref[...] = jnp.concatenate(d_v, axis=0).astype(BF16)
        dq_ref[...] = (d_qi * blk["e_i"] + d_qe * blk["e_b"]).astype(BF16)
        d_k = d_ki * blk["e_ri"] + d_kl * blk["e_l"]
        t_qi = d_qi * blk["qi"]
        t_ki = d_ki * blk["ki"]
        t_kl = d_kl * blk["kl"]
        at_ref, at_last = [], []
        for n in range(n_chunks):
            rows = _chunk_rows(n)
            at_ref.append(jnp.where(row_id == HG_CHUNK // 2 - 1, _colsum(t_ki[rows] - t_qi[rows]), 0.0))
            at_last.append(jnp.where(row_id == HG_CHUNK - 1, _colsum(t_kl[rows]), 0.0))
        d_b = (t_qi - t_ki + d_qe * blk["qe"] - t_kl + jnp.concatenate(at_ref, axis=0)
               + jnp.concatenate(at_last, axis=0) + d_dec * blk["dec"])
        d_forget = _chunk_prefix_sums(d_b, False) / blk["forget"] - d_k
        sg = blk["sg"]
        df_ref[...] = (d_forget * (1.0 - lb) * sg * (1.0 - sg)).astype(BF16)
        sums_ref[0:1, :] += _colsum(d_forget * (1.0 - sg))

    col = lambda off: (lambda h, t: (nb - 1 - t, off + h))
    return _pallas(
        body, name="hgrn_backward", grid=(N_HEADS, nb),
        operands=(zhg, zhg, zhg, zhg, lb_raw, norm_w, o_pre, d_cat, states),
        out_shape=[jax.ShapeDtypeStruct((t_len, N_HEADS * HEAD_DIM), BF16)] * 4
        + [jax.ShapeDtypeStruct((8, N_HEADS * HEAD_DIM), F32)],
        in_specs=[pl.BlockSpec((tb, HEAD_DIM), col(0)), pl.BlockSpec((tb, HEAD_DIM), col(N_HEADS)),
                  pl.BlockSpec((tb, HEAD_DIM), col(2 * N_HEADS)), pl.BlockSpec((tb, HEAD_DIM), col(3 * N_HEADS)),
                  pl.BlockSpec((2, HEAD_DIM), lambda h, t: (0, h)), pl.BlockSpec((1, HEAD_DIM), lambda h, t: (0, h)),
                  pl.BlockSpec((tb, HEAD_DIM), col(0)), pl.BlockSpec((tb, HEAD_DIM), col(0)),
                  pl.BlockSpec((1, n_chunks, HEAD_DIM, HEAD_DIM), lambda h, t: (h, nb - 1 - t, 0, 0))],
        out_specs=[pl.BlockSpec((tb, HEAD_DIM), col(0))] * 4 + [pl.BlockSpec((8, HEAD_DIM), lambda h, t: (0, h))],
        scratch_shapes=[pltpu.VMEM((HEAD_DIM, HEAD_DIM), F32)],
        params=_params(32, ("arbitrary", "arbitrary")), exchange=exchange,
        first=lambda: (pl.program_id(0) == 0) & (pl.program_id(1) == 0),
        last=lambda: (pl.program_id(0) == N_HEADS - 1) & (pl.program_id(1) == nb - 1))


ATT_LOG2 = ATT_SCALE * 1.4426950408889634


def _triangle_steps(nq, q_major):
    if q_major:
        pairs = [(i, j) for i in range(nq) for j in range(i + 1)]
    else:
        pairs = [(i, j) for j in range(nq) for i in range(j, nq)]
    return jnp.array([p[0] for p in pairs], jnp.int32), jnp.array([p[1] for p in pairs], jnp.int32)


def _key_le_query(t):
    return lax.broadcasted_iota(jnp.int32, (t, t), 0) <= lax.broadcasted_iota(jnp.int32, (t, t), 1)


def _attention_forward(q, k, v_t, exchange=None):
    t_len = q.shape[1]
    tq = min(512, t_len)
    nq = t_len // tq
    qi_tab, ki_tab = _triangle_steps(nq, True)

    def body(qi_ref, ki_ref, q_ref, k_ref, vt_ref, o_ref, lse_ref, m_s, l_s, acc_s):
        step = pl.program_id(0)
        qi, ki = qi_ref[step], ki_ref[step]

        @pl.when(ki == 0)
        def _():
            m_s[...] = jnp.full_like(m_s, NEG_BIG)
            l_s[...] = jnp.zeros_like(l_s)
            acc_s[...] = jnp.zeros_like(acc_s)

        def accumulate(masked):
            for h in range(N_HEADS):
                s_t = _dot_nt(k_ref[h], q_ref[h]) * ATT_LOG2
                if masked:
                    s_t = jnp.where(_key_le_query(tq), s_t, NEG_BIG)
                m_old = m_s[h]
                m_new = jnp.maximum(m_old, jnp.max(s_t, axis=0, keepdims=True))
                alpha = jnp.exp2(m_old - m_new)
                p_t = jnp.exp2(s_t - m_new)
                l_s[h] = alpha * l_s[h] + jnp.sum(p_t, axis=0, keepdims=True)
                acc_s[h] = alpha * acc_s[h] + _dot(vt_ref[h], p_t.astype(BF16))
                m_s[h] = m_new

        @pl.when(ki < qi)
        def _():
            accumulate(False)

        @pl.when(ki == qi)
        def _():
            accumulate(True)
            for h in range(N_HEADS):
                o_ref[:, h * HEAD_DIM:(h + 1) * HEAD_DIM] = jnp.transpose(acc_s[h] / l_s[h])
                lse_ref[h] = m_s[h] + jnp.log2(l_s[h])

    n_steps = qi_tab.shape[0]
    return _pallas(
        body, name="attention_forward", grid=(n_steps,), prefetch=(qi_tab, ki_tab), operands=(q, k, v_t),
        out_shape=[jax.ShapeDtypeStruct((t_len, N_HEADS * HEAD_DIM), F32),
                   jax.ShapeDtypeStruct((N_HEADS, 1, t_len), F32)],
        in_specs=[pl.BlockSpec((N_HEADS, tq, QK_DIM), lambda s, qt, kt: (0, qt[s], 0)),
                  pl.BlockSpec((N_HEADS, tq, QK_DIM), lambda s, qt, kt: (0, kt[s], 0)),
                  pl.BlockSpec((N_HEADS, HEAD_DIM, tq), lambda s, qt, kt: (0, 0, kt[s]))],
        out_specs=[pl.BlockSpec((tq, N_HEADS * HEAD_DIM), lambda s, qt, kt: (qt[s], 0)),
                   pl.BlockSpec((N_HEADS, 1, tq), lambda s, qt, kt: (0, 0, qt[s]))],
        scratch_shapes=[pltpu.VMEM((N_HEADS, 1, tq), F32), pltpu.VMEM((N_HEADS, 1, tq), F32),
                        pltpu.VMEM((N_HEADS, HEAD_DIM, tq), F32)],
        params=_params(48, ("arbitrary",)), exchange=exchange,
        first=lambda qt, kt: pl.program_id(0) == 0, last=lambda qt, kt: pl.program_id(0) == n_steps - 1)


BWD_HEADS = 2


def _attention_backward(q, k, k_t, v, d_cat, lse, delta, exchange=None):
    t_len = q.shape[1]
    tq = min(512, t_len)
    nq = t_len // tq
    hp = BWD_HEADS
    qi_tab, ki_tab = _triangle_steps(nq, False)

    def body(qi_ref, ki_ref, q_ref, k_ref, kt_ref, v_ref, do_ref, lse_ref, delta_ref, dqt_hbm, dk_ref, dv_ref,
             dqt_s, dk_s, dv_s):
        group, step = pl.program_id(0), pl.program_id(1)
        qi, ki = qi_ref[step], ki_ref[step]

        @pl.when(step == 0)
        def _():
            dqt_s[...] = jnp.zeros_like(dqt_s)

        @pl.when(qi == ki)
        def _():
            dk_s[...] = jnp.zeros_like(dk_s)
            dv_s[...] = jnp.zeros_like(dv_s)

        def accumulate(masked):
            for h in range(hp):
                do_b = do_ref[:, h * HEAD_DIM:(h + 1) * HEAD_DIM].astype(BF16)
                s_t = _dot_nt(k_ref[h], q_ref[h]) * ATT_LOG2
                if masked:
                    s_t = jnp.where(_key_le_query(tq), s_t, NEG_BIG)
                p_t = jnp.exp2(s_t - lse_ref[h])
                dp_t = _dot_nt(v_ref[h], do_b)
                ds_t = (p_t * (dp_t - delta_ref[h]) * ATT_SCALE).astype(BF16)
                dv_s[h] += _dot(p_t.astype(BF16), do_b)
                dk_s[h] += _dot(ds_t, q_ref[h])
                dqt_s[h, qi] += _dot(kt_ref[h], ds_t)

        @pl.when(ki < qi)
        def _():
            accumulate(False)

        @pl.when(ki == qi)
        def _():
            accumulate(True)
            for h in range(hp):
                pltpu.sync_copy(dqt_s.at[h, qi], dqt_hbm.at[group * hp + h, qi])

        @pl.when(qi == nq - 1)
        def _():
            dk_ref[...] = dk_s[...]
            dv_ref[...] = dv_s[...]

    wide = hp * HEAD_DIM
    n_groups, n_steps = N_HEADS // hp, qi_tab.shape[0]
    return _pallas(
        body, name="attention_backward", grid=(n_groups, n_steps), prefetch=(qi_tab, ki_tab),
        operands=(q, k, k_t, v, d_cat, lse, delta),
        out_shape=[jax.ShapeDtypeStruct((N_HEADS, nq, QK_DIM, tq), F32),
                   jax.ShapeDtypeStruct((N_HEADS, t_len, QK_DIM), F32),
                   jax.ShapeDtypeStruct((N_HEADS, t_len, HEAD_DIM), F32)],
        in_specs=[pl.BlockSpec((hp, tq, QK_DIM), lambda g, s, qt, kt: (g, qt[s], 0)),
                  pl.BlockSpec((hp, tq, QK_DIM), lambda g, s, qt, kt: (g, kt[s], 0)),
                  pl.BlockSpec((hp, QK_DIM, tq), lambda g, s, qt, kt: (g, 0, kt[s])),
                  pl.BlockSpec((hp, tq, HEAD_DIM), lambda g, s, qt, kt: (g, kt[s], 0)),
                  pl.BlockSpec((tq, wide), lambda g, s, qt, kt: (qt[s], n_groups + g)),
                  pl.BlockSpec((hp, 1, tq), lambda g, s, qt, kt: (g, 0, qt[s])),
                  pl.BlockSpec((hp, 1, tq), lambda g, s, qt, kt: (g, 0, qt[s]))],
        out_specs=[pl.BlockSpec(memory_space=pl.ANY),
                   pl.BlockSpec((hp, tq, QK_DIM), lambda g, s, qt, kt: (g, kt[s], 0)),
                   pl.BlockSpec((hp, tq, HEAD_DIM), lambda g, s, qt, kt: (g, kt[s], 0))],
        scratch_shapes=[pltpu.VMEM((hp, nq, QK_DIM, tq), F32), pltpu.VMEM((hp, tq, QK_DIM), F32),
                        pltpu.VMEM((hp, tq, HEAD_DIM), F32)],
        params=_params(48, ("arbitrary", "arbitrary")), exchange=exchange,
        first=lambda qt, kt: (pl.program_id(0) == 0) & (pl.program_id(1) == 0),
        last=lambda qt, kt: (pl.program_id(0) == n_groups - 1) & (pl.program_id(1) == n_steps - 1))


def _out_project(o_hg, o_mla, x, g_a, w_out):
    t_len = x.shape[0]
    tm = min(512, t_len)
    half = N_HEADS * HEAD_DIM

    def body(ohg_ref, omla_ref, x_ref, ga_ref, w_ref, cat_ref, mix_ref, xhat_ref, rstd_ref):
        a = ohg_ref[...].astype(BF16)
        b = omla_ref[...].astype(BF16)
        cat_ref[:, 0:half] = a
        cat_ref[:, half:2 * half] = b
        mix = _dot(a, w_ref[0:half, :]) + _dot(b, w_ref[half:2 * half, :])
        mix_ref[...] = mix
        r1 = DN_ALPHA * x_ref[...] + (1.0 + ga_ref[...]) * mix
        xc = r1 - _rowmean(r1)
        rstd = lax.rsqrt(_rowmean(xc * xc) + LN_EPS)
        xhat_ref[...] = xc * rstd
        rstd_ref[...] = rstd

    row = lambda i: (i, 0)
    fixed = lambda i: (0, 0)
    return _pcall(
        body, name="out_project", grid=(t_len // tm,),
        out_shape=[jax.ShapeDtypeStruct((t_len, D_MODEL), BF16), jax.ShapeDtypeStruct((t_len, D_MODEL), F32),
                   jax.ShapeDtypeStruct((t_len, D_MODEL), F32), jax.ShapeDtypeStruct((t_len, 1), F32)],
        in_specs=[pl.BlockSpec((tm, half), row), pl.BlockSpec((tm, half), row), pl.BlockSpec((tm, D_MODEL), row),
                  pl.BlockSpec((1, D_MODEL), fixed), pl.BlockSpec((D_MODEL, D_MODEL), fixed)],
        out_specs=[pl.BlockSpec((tm, D_MODEL), row), pl.BlockSpec((tm, D_MODEL), row),
                   pl.BlockSpec((tm, D_MODEL), row), pl.BlockSpec((tm, 1), row)],
        compiler_params=_params(48, ("arbitrary",)),
        operands=(o_hg, o_mla, x, g_a, w_out))


V_LN1G, V_LN1B, V_SCM, V_SHM, V_GM, V_GA, V_LN2G, V_LN2B = range(8)
S_DLN2G, S_DLN2B, S_DGM, S_DSCM, S_DSHM, S_DLN1G, S_DLN1B, S_DGA, S_LOSS = range(9)


def _mlp_and_back(xhat1, rstd1, mix, target, o_mla, vecs, w1, w2, w_out):
    t_len = xhat1.shape[0]
    tm = min(256, t_len)
    n_ff = w1.shape[0]
    ff = w1.shape[2]

    def body(xhat_ref, rstd_ref, mix_ref, tgt_ref, omla_ref, vec_ref, w1_hbm, w2_hbm, wout_hbm,
             act_ref, dhp_ref, um_ref, dh_ref, dmix_ref, dcat_ref, dr1_ref, sums_ref, delta_ref,
             w1_s, w2_s, wout_s, hp_s, load_sems):
        @pl.when(pl.program_id(0) == 0)
        def _():
            loads = [pltpu.make_async_copy(w1_hbm, w1_s, load_sems.at[0]),
                     pltpu.make_async_copy(w2_hbm, w2_s, load_sems.at[1]),
                     pltpu.make_async_copy(wout_hbm, wout_s, load_sems.at[2])]
            for cp in loads:
                cp.start()
            sums_ref[...] = jnp.zeros_like(sums_ref)
            for cp in loads:
                cp.wait()

        vec = lambda r: vec_ref[r:r + 1, :]
        xhat = xhat_ref[...]
        x1 = xhat * vec(V_LN1G) + vec(V_LN1B)
        um = (x1 * (1.0 + vec(V_SCM)) + vec(V_SHM)).astype(BF16)
        um_ref[...] = um
        h = jnp.zeros((tm, D_MODEL), F32)
        for j in range(n_ff):
            hp = _dot(um, w1_s[j])
            hp_s[j] = hp
            act = jnp.square(jnp.maximum(hp, 0.0)).astype(BF16)
            act_ref[:, j * ff:(j + 1) * ff] = act
            h = h + _dot(act, w2_s[j])
        r2 = DN_ALPHA * x1 + (1.0 + vec(V_GM)) * h
        xc = r2 - _rowmean(r2)
        rstd2 = lax.rsqrt(_rowmean(xc * xc) + LN_EPS)
        xhat2 = xc * rstd2
        err = xhat2 * vec(V_LN2G) + vec(V_LN2B) - tgt_ref[...]
        loss = 0.5 * jnp.sum(_rowmean(err * err))
        dy = err * (1.0 / D_MODEL)
        dxh = dy * vec(V_LN2G)
        dr2 = rstd2 * (dxh - _rowmean(dxh) - xhat2 * _rowmean(dxh * xhat2))
        dh = ((1.0 + vec(V_GM)) * dr2).astype(BF16)
        dh_ref[...] = dh
        sums_ref[S_DLN2G:S_DLN2G + 1, :] += _colsum(dy * xhat2)
        sums_ref[S_DLN2B:S_DLN2B + 1, :] += _colsum(dy)
        sums_ref[S_DGM:S_DGM + 1, :] += _colsum(dr2 * h)
        sums_ref[S_LOSS:S_LOSS + 1, :] += jnp.full((1, D_MODEL), loss, F32)
        du = jnp.zeros((tm, D_MODEL), F32)
        for j in range(n_ff):
            dhp = (_dot_nt(dh, w2_s[j]) * (2.0 * jnp.maximum(hp_s[j], 0.0))).astype(BF16)
            dhp_ref[:, j * ff:(j + 1) * ff] = dhp
            du = du + _dot_nt(dhp, w1_s[j])
        sums_ref[S_DSCM:S_DSCM + 1, :] += _colsum(du * x1)
        sums_ref[S_DSHM:S_DSHM + 1, :] += _colsum(du)
        dx1 = DN_ALPHA * dr2 + du * (1.0 + vec(V_SCM))
        sums_ref[S_DLN1G:S_DLN1G + 1, :] += _colsum(dx1 * xhat)
        sums_ref[S_DLN1B:S_DLN1B + 1, :] += _colsum(dx1)
        dxh1 = dx1 * vec(V_LN1G)
        dr1 = rstd_ref[...] * (dxh1 - _rowmean(dxh1) - xhat * _rowmean(dxh1 * xhat))
        dr1_ref[...] = dr1
        sums_ref[S_DGA:S_DGA + 1, :] += _colsum(dr1 * mix_ref[...])
        dmix = ((1.0 + vec(V_GA)) * dr1).astype(BF16)
        dmix_ref[...] = dmix
        dcat = _dot_nt(dmix, wout_s[...])
        dcat_ref[...] = dcat
        ones = jnp.ones((8, HEAD_DIM), F32)
        half = N_HEADS * HEAD_DIM
        for hd in range(N_HEADS):
            prod = dcat[:, half + hd * HEAD_DIM:half + (hd + 1) * HEAD_DIM] * omla_ref[:, hd * HEAD_DIM:(hd + 1) * HEAD_DIM]
            delta_ref[hd] = lax.dot_general(ones, prod, (((1,), (1,)), ((), ())), preferred_element_type=F32,
                                            precision=lax.Precision.HIGHEST)[0:1]

    row = lambda i: (i, 0)
    fixed = lambda i: (0, 0)
    any_spec = pl.BlockSpec(memory_space=pl.ANY)
    return _pcall(
        body, name="mlp_and_back", grid=(t_len // tm,),
        out_shape=[jax.ShapeDtypeStruct((t_len, D_FF), BF16), jax.ShapeDtypeStruct((t_len, D_FF), BF16),
                   jax.ShapeDtypeStruct((t_len, D_MODEL), BF16), jax.ShapeDtypeStruct((t_len, D_MODEL), BF16),
                   jax.ShapeDtypeStruct((t_len, D_MODEL), BF16), jax.ShapeDtypeStruct((t_len, D_MODEL), F32),
                   jax.ShapeDtypeStruct((t_len, D_MODEL), F32), jax.ShapeDtypeStruct((16, D_MODEL), F32),
                   jax.ShapeDtypeStruct((N_HEADS, 1, t_len), F32)],
        in_specs=[pl.BlockSpec((tm, D_MODEL), row), pl.BlockSpec((tm, 1), row), pl.BlockSpec((tm, D_MODEL), row),
                  pl.BlockSpec((tm, D_MODEL), row), pl.BlockSpec((tm, N_HEADS * HEAD_DIM), row),
                  pl.BlockSpec((8, D_MODEL), fixed), any_spec, any_spec, any_spec],
        out_specs=[pl.BlockSpec((tm, D_FF), row), pl.BlockSpec((tm, D_FF), row), pl.BlockSpec((tm, D_MODEL), row),
                   pl.BlockSpec((tm, D_MODEL), row), pl.BlockSpec((tm, D_MODEL), row), pl.BlockSpec((tm, D_MODEL), row),
                   pl.BlockSpec((tm, D_MODEL), row), pl.BlockSpec((16, D_MODEL), fixed),
                   pl.BlockSpec((N_HEADS, 1, tm), lambda i: (0, 0, i))],
        scratch_shapes=[pltpu.VMEM(w1.shape, BF16), pltpu.VMEM(w2.shape, BF16), pltpu.VMEM(w_out.shape, BF16),
                        pltpu.VMEM((n_ff, tm, ff), F32), pltpu.SemaphoreType.DMA((3,))],
        compiler_params=_params(56, ("arbitrary",)),
        operands=(xhat1, rstd1, mix, target, o_mla, vecs, w1, w2, w_out))


def _in_project_backward(dq, dk, dv, cq, ckv, pos, invf, q_norm_w, kv_norm_w, w_q, w_kv,
                         d_hq, d_hf, d_hi, d_hg, w_in, dr1, x, sc_a, exchange=None):
    t_len = x.shape[0]
    tm = min(512, t_len)
    per_q = dq.shape[3] // tm
    hgw = N_HEADS * HEAD_DIM

    def body(dq_ref, dk_ref, dv_ref, cq_ref, ckv_ref, pos_ref, invf_ref, qn_ref, kvn_ref, wq_ref, wkv_ref,
             dhq_ref, dhf_ref, dhi_ref, dhg_ref, win_ref, dr1_ref, x_ref, sc_ref,
             dz_ref, dqf_ref, dkvu_ref, cqn_ref, ckvn_ref, gx_ref, sums_ref):
        @pl.when(pl.program_id(0) == 0)
        def _():
            sums_ref[...] = jnp.zeros_like(sums_ref)

        cos_t, sin_t = _rope_tables(pos_ref[...], invf_ref[...])
        cq = cq_ref[...]
        ckv = ckv_ref[...]
        rq = lax.rsqrt(_rowmean(cq * cq) + RMS_EPS)
        rkv = lax.rsqrt(_rowmean(ckv * ckv) + RMS_EPS)
        cqn_ref[...] = (cq * rq * qn_ref[...]).astype(BF16)
        ckvn_ref[...] = (ckv * rkv * kvn_ref[...]).astype(BF16)
        d_cqn = jnp.zeros((tm, Q_RANK), F32)
        d_ckvn = jnp.zeros((tm, KV_RANK), F32)
        d_kpe = jnp.zeros((tm, 128), F32)
        for h in range(N_HEADS):
            dqh = jnp.transpose(dq_ref[h])
            dqf_ref[h, :, 0:HEAD_DIM] = dqh[:, :HEAD_DIM].astype(BF16)
            dqf_ref[h, :, HEAD_DIM:QK_DIM] = _unrope(dqh[:, HEAD_DIM:], cos_t, sin_t).astype(BF16)
            d_cqn = d_cqn + _dot_nt(dqf_ref[h], wq_ref[h])
            dkh = dk_ref[h]
            d_kpe = d_kpe + dkh[:, HEAD_DIM:]
            dkvu_ref[h, :, 0:HEAD_DIM] = dkh[:, :HEAD_DIM].astype(BF16)
            dkvu_ref[h, :, HEAD_DIM:2 * HEAD_DIM] = dv_ref[h].astype(BF16)
            d_ckvn = d_ckvn + _dot_nt(dkvu_ref[h], wkv_ref[h])
        dyq = d_cqn * qn_ref[...]
        dykv = d_ckvn * kvn_ref[...]
        sums_ref[2:3, 0:Q_RANK] += _colsum(d_cqn * cq * rq)
        sums_ref[3:4, 0:KV_RANK] += _colsum(d_ckvn * ckv * rkv)
        dz_ref[:, 0:hgw] = dhq_ref[...]
        dz_ref[:, hgw:2 * hgw] = dhf_ref[...]
        dz_ref[:, 2 * hgw:3 * hgw] = dhi_ref[...]
        dz_ref[:, 3 * hgw:4 * hgw] = dhg_ref[...]
        dz_ref[:, HG_COLS:HG_COLS + Q_RANK] = (rq * dyq - cq * (rq * rq * rq) * _rowmean(dyq * cq)).astype(BF16)
        dz_ref[:, HG_COLS + Q_RANK:HG_COLS + Q_RANK + KV_RANK] = (
            rkv * dykv - ckv * (rkv * rkv * rkv) * _rowmean(dykv * ckv)).astype(BF16)
        dz_ref[:, HG_COLS + Q_RANK + KV_RANK:] = _unrope(d_kpe, cos_t, sin_t).astype(BF16)
        du = _dot_nt(dz_ref[...], win_ref[...])
        xv = x_ref[...]
        gx_ref[...] = DN_ALPHA * dr1_ref[...] + (1.0 + sc_ref[...]) * du
        sums_ref[0:1, :] += _colsum(du * xv)
        sums_ref[1:2, :] += _colsum(du)

    row = lambda i: (i, 0)
    fixed2 = lambda i: (0, 0)
    fixed3 = lambda i: (0, 0, 0)
    heads = lambda i: (0, i, 0)
    n_tiles = t_len // tm
    return _pallas(
        body, name="in_project_backward", grid=(n_tiles,),
        operands=(dq, dk, dv, cq, ckv, pos, invf, q_norm_w, kv_norm_w, w_q, w_kv, d_hq, d_hf, d_hi, d_hg, w_in, dr1, x,
                  sc_a),
        out_shape=[jax.ShapeDtypeStruct((t_len, IN_COLS_PAD), BF16), jax.ShapeDtypeStruct((N_HEADS, t_len, QK_DIM), BF16),
                   jax.ShapeDtypeStruct((N_HEADS, t_len, 2 * HEAD_DIM), BF16), jax.ShapeDtypeStruct((t_len, Q_RANK), BF16),
                   jax.ShapeDtypeStruct((t_len, KV_RANK), BF16), jax.ShapeDtypeStruct((t_len, D_MODEL), F32),
                   jax.ShapeDtypeStruct((8, D_MODEL), F32)],
        in_specs=[pl.BlockSpec((N_HEADS, None, QK_DIM, tm), lambda i: (0, i // per_q, 0, i % per_q)),
                  pl.BlockSpec((N_HEADS, tm, QK_DIM), heads),
                  pl.BlockSpec((N_HEADS, tm, HEAD_DIM), heads), pl.BlockSpec((tm, Q_RANK), row),
                  pl.BlockSpec((tm, KV_RANK), row), pl.BlockSpec((tm, 1), row), pl.BlockSpec((1, 128), fixed2),
                  pl.BlockSpec((1, Q_RANK), fixed2), pl.BlockSpec((1, KV_RANK), fixed2),
                  pl.BlockSpec((N_HEADS, Q_RANK, QK_DIM), fixed3), pl.BlockSpec((N_HEADS, KV_RANK, 2 * HEAD_DIM), fixed3),
                  pl.BlockSpec((tm, hgw), row), pl.BlockSpec((tm, hgw), row), pl.BlockSpec((tm, hgw), row),
                  pl.BlockSpec((tm, hgw), row), pl.BlockSpec((D_MODEL, IN_COLS_PAD), fixed2),
                  pl.BlockSpec((tm, D_MODEL), row), pl.BlockSpec((tm, D_MODEL), row), pl.BlockSpec((1, D_MODEL), fixed2)],
        out_specs=[pl.BlockSpec((tm, IN_COLS_PAD), row), pl.BlockSpec((N_HEADS, tm, QK_DIM), heads),
                   pl.BlockSpec((N_HEADS, tm, 2 * HEAD_DIM), heads), pl.BlockSpec((tm, Q_RANK), row),
                   pl.BlockSpec((tm, KV_RANK), row), pl.BlockSpec((tm, D_MODEL), row), pl.BlockSpec((8, D_MODEL), fixed2)],
        params=_params(48, ("arbitrary",)), exchange=exchange,
        first=lambda: pl.program_id(0) == 0, last=lambda: pl.program_id(0) == n_tiles - 1)


def _weight_grad(a, b, name, n_blocks, bn, a_blocked=False, b_blocked=True, exchange=None, token_tile=512):
    t_len = a.shape[0]
    m = a.shape[1] // n_blocks if a_blocked else a.shape[1]
    bt = min(token_tile, t_len)

    def body(a_ref, b_ref, o_ref):
        @pl.when(pl.program_id(1) == 0)
        def _():
            o_ref[...] = jnp.zeros_like(o_ref)

        o_ref[...] += _dot_tn(a_ref[...].astype(BF16), b_ref[...].astype(BF16))

    a_spec = pl.BlockSpec((bt, m), (lambda n, t: (t, n)) if a_blocked else (lambda n, t: (t, 0)))
    if b.ndim == 3:
        b_spec = pl.BlockSpec((None, bt, bn), lambda n, t: (n, t, 0))
    else:
        b_spec = pl.BlockSpec((bt, bn), (lambda n, t: (t, n)) if b_blocked else (lambda n, t: (t, 0)))
    nt = t_len // bt
    (out,), landed = _pallas(
        body, name=name, grid=(n_blocks, nt), operands=(a, b),
        out_shape=[jax.ShapeDtypeStruct((n_blocks, m, bn), F32)],
        in_specs=[a_spec, b_spec],
        out_specs=[pl.BlockSpec((None, m, bn), lambda n, t: (n, 0, 0))],
        params=_params(56, ("arbitrary", "arbitrary")), exchange=exchange,
        first=lambda: (pl.program_id(0) == 0) & (pl.program_id(1) == 0),
        last=lambda: (pl.program_id(0) == n_blocks - 1) & (pl.program_id(1) == nt - 1))
    return (out, landed) if exchange else out


def _reduce_small(gathered, lb_raw):
    def body(g_ref, lb_ref, tot_ref, dlb_ref):
        tot = g_ref[0]
        for d in range(1, N_DEV):
            tot = tot + g_ref[d]
        tot_ref[...] = tot
        a = lb_ref[...]
        m = jnp.max(a, axis=0, keepdims=True)
        e = jnp.exp(a - m)
        lb = e[0:1] / jnp.sum(e, axis=0, keepdims=True)
        d0 = tot[10:11, 0:512] * lb * (1.0 - lb)
        dlb_ref[0:1, :] = d0
        dlb_ref[1:2, :] = -d0

    return pl.pallas_call(
        body, name="reduce_small",
        out_shape=[jax.ShapeDtypeStruct((SMALL_ROWS, D_MODEL), F32), jax.ShapeDtypeStruct((2, 512), F32)],
    )(gathered, lb_raw)


def _adamw_update(w, gv, m, v):
    nm = ADAM_B1 * m + (1.0 - ADAM_B1) * gv
    nv = ADAM_B2 * v + (1.0 - ADAM_B2) * jnp.square(gv)
    m_hat = nm / (1.0 - ADAM_B1 ** ADAM_STEP)
    v_hat = nv / (1.0 - ADAM_B2 ** ADAM_STEP)
    return -ADAM_LR * (m_hat / (jnp.sqrt(v_hat) + ADAM_EPS) + ADAM_WD * w), nm, nv


def _adamw_halves(core, w, mine, theirs, m, v, name):
    rows, cols = w.shape
    h = rows // 2
    tr = _row_tile(h)
    per_half = h // tr

    def body(core_ref, w_ref, mine_ref, theirs_ref, m_ref, v_ref, g_ref, d_ref, nm_ref, nv_ref):
        is_mine = pl.program_id(0) // per_half == core_ref[0]
        gv = jnp.where(is_mine, mine_ref[...], theirs_ref[...])
        g_ref[...] = gv
        d_ref[...], nm_ref[...], nv_ref[...] = _adamw_update(w_ref[...], gv, m_ref[...], v_ref[...])

    full = pl.BlockSpec((tr, cols), lambda i, core_ref: (i, 0))
    part = pl.BlockSpec((tr, cols), lambda i, core_ref: (i % per_half, 0))
    return _pcall(
        body, name=name, out_shape=[jax.ShapeDtypeStruct(w.shape, F32)] * 4,
        grid_spec=pltpu.PrefetchScalarGridSpec(
            num_scalar_prefetch=1, grid=(rows // tr,), in_specs=[full, part, part, full, full], out_specs=[full] * 4),
        compiler_params=_params(40, ("arbitrary",)),
        operands=(core, w, mine, theirs, m, v))


def _adamw(w, g, m, v, name):
    rows, cols = w.shape
    tr = _row_tile(rows) if rows >= 8 else rows

    def body(w_ref, g_ref, m_ref, v_ref, d_ref, nm_ref, nv_ref):
        d_ref[...], nm_ref[...], nv_ref[...] = _adamw_update(w_ref[...], g_ref[...], m_ref[...], v_ref[...])

    spec = pl.BlockSpec((tr, cols), lambda i: (i, 0))
    return _pcall(
        body, name=name, grid=(rows // tr,),
        out_shape=[jax.ShapeDtypeStruct(w.shape, F32)] * 3,
        in_specs=[spec] * 4, out_specs=[spec] * 3,
        compiler_params=_params(40, ("arbitrary",)),
        operands=(w, g, m, v))


def kernel(x, c, positions, w_ada, b_ada, w_in, hg_lower_bounds, hg_norm_w, mla_q_norm_w, w_q_up, mla_kv_norm_w, w_kv_up, w_out, ln1_g, ln1_b, w_mlp_in, w_mlp_out, ln2_g, ln2_b, loss_target, m_w_ada, m_b_ada, m_w_in, m_hg_lower_bounds, m_hg_norm_w, m_mla_q_norm_w, m_w_q_up, m_mla_kv_norm_w, m_w_kv_up, m_w_out, m_ln1_g, m_ln1_b, m_w_mlp_in, m_w_mlp_out, m_ln2_g, m_ln2_b, v_w_ada, v_b_ada, v_w_in, v_hg_lower_bounds, v_hg_norm_w, v_mla_q_norm_w, v_w_q_up, v_mla_kv_norm_w, v_w_kv_up, v_w_out, v_ln1_g, v_ln1_b, v_w_mlp_in, v_w_mlp_out, v_ln2_g, v_ln2_b):
    ix, iy, ic = _mesh_pos()
    chip = 2 * ix + iy
    me = 4 * ix + 2 * iy + ic
    core_arr = jnp.reshape(ic, (1,)).astype(jnp.int32)
    chip_arr = jnp.reshape(chip, (1,)).astype(jnp.int32)

    xs = x[0]
    target = loss_target[0]
    t_len = xs.shape[0]
    pos = positions.astype(F32).reshape(t_len, 1)
    inv = 1.0 / (ROPE_THETA ** (jnp.arange(0, ROPE_DIM, 2, dtype=F32) / ROPE_DIM))
    invf = jnp.concatenate([inv, inv, jnp.zeros((128 - ROPE_DIM,), F32)]).reshape(1, 128)

    def slot(w):
        rows, cols = w.shape
        own = w.astype(BF16).reshape(1, 2, rows // 2, cols)
        return lax.dynamic_update_slice(jnp.zeros((N_CHIPS, 2, rows // 2, cols), BF16), own, (chip, 0, 0, 0))

    def slot8(a):
        return lax.dynamic_update_slice(jnp.zeros((N_DEV,) + a.shape, a.dtype), a[None], (me, 0, 0))

    def whole(s):
        return s.reshape(N_CHIPS, 2 * s.shape[2], s.shape[3])

    def halved(g):
        return g.reshape(N_CHIPS, 2, g.shape[1] // 2, g.shape[2])

    ada_cols = w_ada.shape[2]
    c_all, *early = _run_exchange(
        _merge(_gather_all(slot8(jnp.broadcast_to(c, (8, D_MODEL)))),
               _gather_over_ici([slot(w_in[0]), slot(w_q_up[0]), slot(w_kv_up[0])])), "gather_c_and_mixer_weights_ici")
    b_shard = lax.dynamic_slice(b_ada, (0, chip * ada_cols), (1, ada_cols))
    mod_cols, cond16 = _ada_project(c_all[:, 0, :], w_ada[0], b_shard)
    mod_all, *early = _run_exchange(_merge(_gather_all(slot8(mod_cols)), _gather_over_d2d(early)),
                                    "gather_mod_and_mixer_weights_d2d")
    mod_mine = lax.dynamic_slice(mod_all, (0, me, 0), (N_DEV, 1, ada_cols))[::2, 0, :].reshape(6, D_MODEL)
    sh_a, sc_a, g_a, sh_m, sc_m, g_m = (mod_mine[i:i + 1] for i in range(6))
    g_in, g_q, g_kv = (whole(s) for s in early)
    w_in_full = jnp.transpose(g_in, (1, 0, 2)).reshape(D_MODEL, IN_COLS)
    w_in_full = jnp.pad(w_in_full, ((0, 0), (0, IN_COLS_PAD - IN_COLS)))
    w_q_full = jnp.pad(g_q, ((0, 0), (0, 0), (0, QK_DIM - g_q.shape[2])))

    (u_a, zhg, cq, ckv, q, k, k_t, v, v_t), slots_a = _in_project(
        xs, pos, sc_a, sh_a, w_in_full, mla_q_norm_w, mla_kv_norm_w, w_q_full, g_kv, invf,
        _gather_over_ici([slot(w_mlp_in[0])]))
    (o_pre, o_hg, states), slots_b = _hgrn_forward(
        zhg, hg_lower_bounds, hg_norm_w, _gather_over_ici([slot(w_mlp_out[0]), slot(w_out[0])]))
    (o_mla, lse), mlp_slots = _attention_forward(q, k, v_t, _gather_over_d2d(list(slots_a) + list(slots_b)))
    g_w1, g_w2, g_out = (whole(s) for s in mlp_slots)
    w_out_full = g_out.reshape(D_MODEL, D_MODEL)
    cat, mix, xhat1, rstd1 = _out_project(o_hg, o_mla, xs, g_a, w_out_full)
    vecs = jnp.concatenate([ln1_g, ln1_b, sc_m, sh_m, g_m, g_a, ln2_g, ln2_b], axis=0)
    act, dhp, um, dh, dmix, d_cat, dr1, mlp_sums, delta = _mlp_and_back(
        xhat1, rstd1, mix, target, o_mla, vecs, g_w1, g_w2, w_out_full)

    gw_1 = _weight_grad(um, dhp, "grad_w_mlp_in", N_CHIPS, D_FF // N_CHIPS, token_tile=4096)
    gw_2 = _weight_grad(act, dh, "grad_w_mlp_out", N_CHIPS, D_MODEL, a_blocked=True, b_blocked=False, token_tile=4096)
    gw_out = _weight_grad(cat, dmix, "grad_w_out", 1, D_MODEL, token_tile=2048)
    gw_out = gw_out.reshape(N_CHIPS, D_MODEL // N_CHIPS, D_MODEL)
    mlp_grads = [halved(gw_1), halved(gw_2), halved(gw_out)]
    (dq, dk, dv), landed = _attention_backward(q, k, k_t, v, d_cat, lse, delta, _pair_exchange(mlp_grads))
    chip_sums = [_add_pair(core_arr, g, l) for g, l in zip(mlp_grads, landed)]
    (d_hq, d_hf, d_hi, d_hg, hg_sums), landed = _hgrn_backward(
        zhg, hg_lower_bounds, hg_norm_w, o_pre, d_cat, states, _chip_exchange([b for _, b in chip_sums]))
    mlp_mine = [_add_chips(chip_arr, p, l) for (p, _), l in zip(chip_sums, landed)]
    (dz, dqf, dkvu, cqn, ckvn, grad_x, in_sums), _ = _in_project_backward(
        dq, dk, dv, cq, ckv, pos, invf, mla_q_norm_w, mla_kv_norm_w, w_q_full, g_kv,
        d_hq, d_hf, d_hi, d_hg, w_in_full, dr1, xs, sc_a)

    gw_in, mlp_theirs = _weight_grad(u_a, dz, "grad_w_in", 3, IN_COLS_PAD // 3, exchange=_pair_send(mlp_mine),
                                     token_tile=4096)
    gw_in = jnp.transpose(gw_in, (1, 0, 2)).reshape(D_MODEL, IN_COLS_PAD)[:, :IN_COLS]
    gw_in = jnp.transpose(gw_in.reshape(D_MODEL, N_CHIPS, IN_COLS // N_CHIPS), (1, 0, 2))
    gw_q = _weight_grad(cqn, dqf, "grad_w_q_up", N_HEADS, QK_DIM, token_tile=2048)[:, :, :HEAD_DIM + ROPE_DIM]
    gw_kv = _weight_grad(ckvn, dkvu, "grad_w_kv_up", N_HEADS, 2 * HEAD_DIM, token_tile=2048)
    mixer_grads = [halved(g) for g in (gw_in, gw_q, gw_kv)]
    landed = _run_exchange(_pair_exchange(mixer_grads), "grad_pair_exchange")
    chip_sums = [_add_pair(core_arr, g, l) for g, l in zip(mixer_grads, landed)]
    landed = _run_exchange(_chip_exchange([b for _, b in chip_sums]), "grad_chip_exchange")
    mixer_mine = [_add_chips(chip_arr, p, l) for (p, _), l in zip(chip_sums, landed)]
    mixer_theirs = _run_exchange(_pair_send(mixer_mine), "grad_pair_send")
    reduced = ("w_in", "w_q_up", "w_kv_up", "w_mlp_in", "w_mlp_out", "w_out")
    halves_mine = dict(zip(reduced, mixer_mine + mlp_mine))
    halves_theirs = dict(zip(reduced, list(mixer_theirs) + list(mlp_theirs)))

    zeros = lambda n: jnp.zeros((1, n), F32)
    small = jnp.concatenate([
        in_sums[1:2], in_sums[0:1], mlp_sums[S_DGA:S_DGA + 1],
        mlp_sums[S_DSHM:S_DSHM + 1], mlp_sums[S_DSCM:S_DSCM + 1], mlp_sums[S_DGM:S_DGM + 1],
        mlp_sums[S_DLN1G:S_DLN1G + 1], mlp_sums[S_DLN1B:S_DLN1B + 1],
        mlp_sums[S_DLN2G:S_DLN2G + 1], mlp_sums[S_DLN2B:S_DLN2B + 1],
        jnp.concatenate([hg_sums[0:1], hg_sums[1:2]], axis=1),
        jnp.concatenate([in_sums[2:3, :Q_RANK], in_sums[3:4, :KV_RANK], zeros(D_MODEL - Q_RANK - KV_RANK)], axis=1),
        mlp_sums[S_LOSS:S_LOSS + 1],
        jnp.zeros((SMALL_ROWS - 13, D_MODEL), F32)], axis=0)
    small_all = _allgather8(small, "gather_small")
    tot, g_lb = _reduce_small(small_all, hg_lower_bounds)
    loss = tot[12, 0]
    g_b_ada = tot[0:6].reshape(1, 6 * D_MODEL)
    g_ln1_g, g_ln1_b, g_ln2_g, g_ln2_b = tot[6:7], tot[7:8], tot[8:9], tot[9:10]
    g_hg_norm = tot[10:11, 512:1024]
    g_q_norm = tot[11:12, 0:Q_RANK]
    g_kv_norm = tot[11:12, Q_RANK:Q_RANK + KV_RANK]

    d_mod_all = small_all[:, 0:6, :].reshape(N_DEV, 6 * D_MODEL)
    d_mod_cols = lax.dynamic_slice(d_mod_all, (0, chip * ada_cols), (N_DEV, ada_cols))
    d_mod_cols = jnp.concatenate([d_mod_cols, jnp.zeros_like(d_mod_cols)], axis=0)
    g_w_ada = _weight_grad(cond16, d_mod_cols, "grad_w_ada", 1, ada_cols)[0]

    names = ["w_ada", "b_ada", "w_in", "hg_lower_bounds", "hg_norm_w", "mla_q_norm_w", "w_q_up", "mla_kv_norm_w",
             "w_kv_up", "w_out", "ln1_g", "ln1_b", "w_mlp_in", "w_mlp_out", "ln2_g", "ln2_b"]
    weights = [w_ada, b_ada, w_in, hg_lower_bounds, hg_norm_w, mla_q_norm_w, w_q_up, mla_kv_norm_w,
               w_kv_up, w_out, ln1_g, ln1_b, w_mlp_in, w_mlp_out, ln2_g, ln2_b]
    moms = [m_w_ada, m_b_ada, m_w_in, m_hg_lower_bounds, m_hg_norm_w, m_mla_q_norm_w, m_w_q_up, m_mla_kv_norm_w,
            m_w_kv_up, m_w_out, m_ln1_g, m_ln1_b, m_w_mlp_in, m_w_mlp_out, m_ln2_g, m_ln2_b]
    vels = [v_w_ada, v_b_ada, v_w_in, v_hg_lower_bounds, v_hg_norm_w, v_mla_q_norm_w, v_w_q_up, v_mla_kv_norm_w,
            v_w_kv_up, v_w_out, v_ln1_g, v_ln1_b, v_w_mlp_in, v_w_mlp_out, v_ln2_g, v_ln2_b]
    grads2d = [g_w_ada, g_b_ada, None, g_lb, g_hg_norm, g_q_norm, None, g_kv_norm,
               None, None, g_ln1_g, g_ln1_b, None, None, g_ln2_g, g_ln2_b]
    out_g, out_d, out_m, out_v = [], [], [], []
    for name, w, g, m, vv in zip(names, weights, grads2d, moms, vels):
        if g is None:
            shape2 = w.shape[1:]
            g, d, nm, nv = _adamw_halves(core_arr, w.reshape(shape2), halves_mine[name], halves_theirs[name],
                                         m.reshape(shape2), vv.reshape(shape2), "adamw_" + name)
        else:
            shape2 = g.shape
            d, nm, nv = _adamw(w.reshape(shape2), g, m.reshape(shape2), vv.reshape(shape2), "adamw_" + name)
        out_g.append(g.reshape(w.shape))
        out_d.append(d.reshape(w.shape))
        out_m.append(nm.reshape(w.shape))
        out_v.append(nv.reshape(w.shape))
    return (loss, grad_x[None], *out_g, *out_d, *out_m, *out_v)
```

```python
import functools

import jax
import jax.numpy as jnp
from jax import lax
from jax.experimental import pallas as pl
from jax.experimental.pallas import tpu as pltpu

F32 = jnp.float32
BF16 = jnp.bfloat16
MESH_IDS = pl.DeviceIdType.MESH

D_MODEL = 1024
N_HEADS = 4
HEAD_DIM = 128
ROPE_DIM = 64
HG_CHUNK = 64
HG_COLS = 2048
Q_RANK = 256
KV_RANK = 256
IN_COLS = 2624
IN_COLS_PAD = 2688
QK_DIM = 256
D_FF = 4096
N_CHIPS = 4
N_DEV = 8
ROPE_THETA = 10000.0
RMS_EPS = 1e-6
LN_EPS = 1e-5
DN_ALPHA = 2.0 ** 0.25
ATT_SCALE = (HEAD_DIM + ROPE_DIM) ** -0.5
NEG_BIG = -1e30
ADAM_LR = 0.001
ADAM_B1 = 0.9
ADAM_B2 = 0.999
ADAM_EPS = 1e-08
ADAM_WD = 0.01
ADAM_STEP = 10
SMALL_ROWS = 16
MIB = 1024 * 1024


def _dot(a, b):
    return jnp.dot(a, b, preferred_element_type=F32)


def _dot_nt(a, b):
    return lax.dot_general(a, b, (((1,), (1,)), ((), ())), preferred_element_type=F32)


def _dot_tn(a, b):
    return lax.dot_general(a, b, (((0,), (0,)), ((), ())), preferred_element_type=F32)


def _params(vmem_mib, semantics=None):
    return pltpu.CompilerParams(vmem_limit_bytes=vmem_mib * MIB, dimension_semantics=semantics)


def _sigmoid(v):
    return 1.0 / (1.0 + jnp.exp(-v))


def _colsum(v):
    return jnp.sum(v, axis=0, keepdims=True)


def _rowmean(v):
    return jnp.mean(v, axis=-1, keepdims=True)


def _rope_tables(pos, invf):
    ang = pos * invf
    lane = lax.broadcasted_iota(jnp.int32, ang.shape, 1)
    cos_t = jnp.where(lane < ROPE_DIM, jnp.cos(ang), 0.0)
    sin = jnp.sin(ang)
    sin_t = jnp.where(lane < ROPE_DIM // 2, -sin, jnp.where(lane < ROPE_DIM, sin, 0.0))
    return cos_t, sin_t


def _swap_halves(t):
    lane = lax.broadcasted_iota(jnp.int32, t.shape, 1)
    return jnp.where(lane < ROPE_DIM // 2, pltpu.roll(t, 128 - ROPE_DIM // 2, 1), pltpu.roll(t, ROPE_DIM // 2, 1))


def _rope(t, cos_t, sin_t):
    return t * cos_t + _swap_halves(t) * sin_t


def _unrope(g, cos_t, sin_t):
    return g * cos_t - _swap_halves(g) * sin_t


def _mesh_pos():
    return lax.axis_index("x"), lax.axis_index("y"), lax.axis_index("c")


def _other_chips(x, y):
    out = []
    for dx, dy in ((1, 0), (0, 1), (1, 1)):
        px = 1 - x if dx else x
        py = 1 - y if dy else y
        out.append(((px, py), 2 * px + py))
    return out


def _allgather8(a, name):
    rows, cols = a.shape

    def body(a_ref, out_ref, send_sems, recv_sems):
        x, y, c = _mesh_pos()
        me = 4 * x + 2 * y + c
        out_ref[me] = a_ref[...]
        peers = []
        for r in range(1, N_DEV):
            px = 1 - x if r & 4 else x
            py = 1 - y if r & 2 else y
            pc = 1 - c if r & 1 else c
            peers.append(((px, py, pc), 4 * px + 2 * py + pc))

        def copy(r, block, to):
            return pltpu.make_async_remote_copy(
                src_ref=a_ref, dst_ref=out_ref.at[block], send_sem=send_sems.at[r], recv_sem=recv_sems.at[r],
                device_id=to, device_id_type=MESH_IDS)

        sends = [copy(r, me, peer) for r, (peer, _) in enumerate(peers)]
        for cp in sends:
            cp.start()
        for r, (peer, idx) in enumerate(peers):
            copy(r, idx, peer).wait_recv()
        for cp in sends:
            cp.wait_send()

    return pl.pallas_call(
        body, name=name,
        out_shape=jax.ShapeDtypeStruct((N_DEV, rows, cols), a.dtype),
        in_specs=[pl.BlockSpec(memory_space=pltpu.VMEM)],
        out_specs=pl.BlockSpec(memory_space=pltpu.VMEM),
        scratch_shapes=[pltpu.SemaphoreType.DMA((N_DEV - 1,)), pltpu.SemaphoreType.DMA((N_DEV - 1,))],
    )(a)


class _Exchange:
    def __init__(self, inputs, out_shapes, aliases, sems, start, finish):
        self.inputs, self.out_shapes, self.aliases, self.sems = list(inputs), list(out_shapes), dict(aliases), list(sems)
        self.start, self.finish = start, finish


def _from_copies(inputs, out_shapes, aliases, sems, copies):
    def start(ins, outs, sem_refs):
        for send, _ in copies(ins, outs, sem_refs):
            send.start()

    def finish(ins, outs, sem_refs):
        for send, recv in copies(ins, outs, sem_refs):
            recv.wait_recv()
            send.wait_send()

    return _Exchange(inputs, out_shapes, aliases, sems, start, finish)


HBM_MIN_BYTES = 256 * 1024


def _in_hbm(a):
    if a.size * a.dtype.itemsize < HBM_MIN_BYTES:
        return a
    return pltpu.with_memory_space_constraint(a, pltpu.HBM)


def _out_hbm(s):
    if s.size * s.dtype.itemsize < HBM_MIN_BYTES:
        return s
    return pltpu.HBM(s.shape, s.dtype)


def _pcall(body, *, operands, out_shape, **kwargs):
    single = not isinstance(out_shape, (list, tuple))
    shapes = [_out_hbm(s) for s in ([out_shape] if single else out_shape)]
    return pl.pallas_call(body, out_shape=shapes[0] if single else shapes, **kwargs)(*[_in_hbm(a) for a in operands])


def _run_exchange(exchange, name):
    n_in, n_out = len(exchange.inputs), len(exchange.out_shapes)

    def body(*refs):
        ins, outs, sem_refs = refs[:n_in], refs[n_in:n_in + n_out], refs[n_in + n_out:]
        exchange.start(ins, outs, sem_refs)
        exchange.finish(ins, outs, sem_refs)

    any_spec = pl.BlockSpec(memory_space=pl.ANY)
    return pl.pallas_call(
        body, name=name, out_shape=[_out_hbm(s) for s in exchange.out_shapes],
        in_specs=[any_spec] * n_in, out_specs=[any_spec] * n_out,
        scratch_shapes=exchange.sems, input_output_aliases=exchange.aliases,
    )(*[_in_hbm(a) for a in exchange.inputs])


def _pallas(body, *, name, operands, in_specs, out_shape, out_specs, params, scratch_shapes=(), grid=(), prefetch=(),
            exchange=None, first=None, last=None):
    n_pre, n_in, n_out, n_scr = len(prefetch), len(in_specs), len(out_specs), len(scratch_shapes)
    ex_in = exchange.inputs if exchange else []
    ex_out = exchange.out_shapes if exchange else []
    ex_sems = exchange.sems if exchange else []

    def full_body(*refs):
        pre, rest = refs[:n_pre], refs[n_pre:]
        ins, rest = rest[:n_in], rest[n_in:]
        xin, rest = rest[:len(ex_in)], rest[len(ex_in):]
        outs, rest = rest[:n_out], rest[n_out:]
        xout, rest = rest[:len(ex_out)], rest[len(ex_out):]
        scr, sem_refs = rest[:n_scr], rest[n_scr:]
        if exchange:
            @pl.when(first(*pre))
            def _():
                exchange.start(xin, xout, sem_refs)

        body(*pre, *ins, *outs, *scr)
        if exchange:
            @pl.when(last(*pre))
            def _():
                exchange.finish(xin, xout, sem_refs)

    any_spec = pl.BlockSpec(memory_space=pl.ANY)
    aliases = {n_pre + n_in + i: n_out + o for i, o in exchange.aliases.items()} if exchange else {}
    operands = [_in_hbm(a) for a in operands]
    results = pl.pallas_call(
        full_body, name=name, out_shape=[_out_hbm(s) for s in list(out_shape) + ex_out],
        grid_spec=pltpu.PrefetchScalarGridSpec(
            num_scalar_prefetch=n_pre, grid=grid, in_specs=list(in_specs) + [any_spec] * len(ex_in),
            out_specs=list(out_specs) + [any_spec] * len(ex_out), scratch_shapes=list(scratch_shapes) + ex_sems),
        input_output_aliases=aliases, compiler_params=params,
    )(*prefetch, *operands, *[_in_hbm(a) for a in ex_in])
    return results[:n_out], results[n_out:]


def _remote(src, dst, sems, idx, to):
    send_sems, recv_sems = sems
    return pltpu.make_async_remote_copy(src_ref=src, dst_ref=dst, send_sem=send_sems.at[idx], recv_sem=recv_sems.at[idx],
                                        device_id=to, device_id_type=MESH_IDS)


def _sem_pairs(*shape):
    return [pltpu.SemaphoreType.DMA(shape), pltpu.SemaphoreType.DMA(shape)]


def _same_shapes(arrays):
    return [jax.ShapeDtypeStruct(a.shape, a.dtype) for a in arrays]


def _gather_over_ici(slots):
    n = len(slots)

    def copies(ins, outs, sems):
        x, y, c = _mesh_pos()
        k = 2 * x + y
        out = []
        for j, (chip, kj) in enumerate(_other_chips(x, y)):
            for i in range(n):
                to = (*chip, c)
                out.append((_remote(ins[i].at[k, c], outs[i].at[k, c], sems, (j, i), to),
                            _remote(ins[i].at[k, c], outs[i].at[kj, c], sems, (j, i), to)))
        return out

    return _from_copies(slots, _same_shapes(slots), {i: i for i in range(n)}, _sem_pairs(3, n), copies)


def _gather_over_d2d(slots):
    n = len(slots)

    def copies(ins, outs, sems):
        x, y, c = _mesh_pos()
        sibling = (x, y, 1 - c)
        out = []
        for j, (_, kj) in enumerate(_other_chips(x, y)):
            for i in range(n):
                out.append((_remote(ins[i].at[kj, c], outs[i].at[kj, c], sems, (j, i), sibling),
                            _remote(ins[i].at[kj, c], outs[i].at[kj, 1 - c], sems, (j, i), sibling)))
        return out

    return _from_copies(slots, _same_shapes(slots), {i: i for i in range(n)}, _sem_pairs(3, n), copies)


def _gather_all(slots8):
    def copies(ins, outs, sems):
        x, y, c = _mesh_pos()
        me = 4 * x + 2 * y + c
        out = []
        for r in range(1, N_DEV):
            px = 1 - x if r & 4 else x
            py = 1 - y if r & 2 else y
            pc = 1 - c if r & 1 else c
            to = (px, py, pc)
            out.append((_remote(ins[0].at[me], outs[0].at[me], sems, r - 1, to),
                        _remote(ins[0].at[me], outs[0].at[4 * px + 2 * py + pc], sems, r - 1, to)))
        return out

    return _from_copies([slots8], _same_shapes([slots8]), {0: 0}, _sem_pairs(N_DEV - 1), copies)


def _merge(first, second):
    n_in, n_out, n_sem = len(first.inputs), len(first.out_shapes), len(first.sems)

    def start(ins, outs, sems):
        first.start(ins[:n_in], outs[:n_out], sems[:n_sem])
        second.start(ins[n_in:], outs[n_out:], sems[n_sem:])

    def finish(ins, outs, sems):
        first.finish(ins[:n_in], outs[:n_out], sems[:n_sem])
        second.finish(ins[n_in:], outs[n_out:], sems[n_sem:])

    aliases = dict(first.aliases)
    aliases.update({n_in + i: n_out + o for i, o in second.aliases.items()})
    return _Exchange(first.inputs + second.inputs, first.out_shapes + second.out_shapes, aliases,
                     first.sems + second.sems, start, finish)


def _pair_exchange(grads):
    n = len(grads)

    def copies(ins, outs, sems):
        x, y, c = _mesh_pos()
        cps = [_remote(ins[i].at[:, 1 - c], outs[i], sems, i, (x, y, 1 - c)) for i in range(n)]
        return [(cp, cp) for cp in cps]

    shapes = [jax.ShapeDtypeStruct((N_CHIPS,) + g.shape[2:], g.dtype) for g in grads]
    return _from_copies(grads, shapes, {}, _sem_pairs(n), copies)


def _chip_exchange(partials):
    n = len(partials)

    def copies(ins, outs, sems):
        x, y, c = _mesh_pos()
        cps = [_remote(ins[i].at[kj], outs[i].at[j], sems, (j, i), (*chip, c))
               for j, (chip, kj) in enumerate(_other_chips(x, y)) for i in range(n)]
        return [(cp, cp) for cp in cps]

    shapes = [jax.ShapeDtypeStruct((3,) + p.shape[1:], p.dtype) for p in partials]
    return _from_copies(partials, shapes, {}, _sem_pairs(3, n), copies)


def _pair_send(halves):
    n = len(halves)

    def copies(ins, outs, sems):
        x, y, c = _mesh_pos()
        cps = [_remote(ins[i], outs[i], sems, i, (x, y, 1 - c)) for i in range(n)]
        return [(cp, cp) for cp in cps]

    return _from_copies(halves, _same_shapes(halves), {}, _sem_pairs(n), copies)


def _reduce_in_vmem(grads, name):
    n = len(grads)

    def body(*refs):
        g, mine, theirs = refs[:n], refs[n:2 * n], refs[2 * n:3 * n]
        landed_pair, partial, landed_chips = refs[3 * n:4 * n], refs[4 * n:5 * n], refs[5 * n:6 * n]
        sems = refs[6 * n:]
        x, y, c = _mesh_pos()
        k = 2 * x + y
        sibling = (x, y, 1 - c)

        def run(copies):
            for cp in copies:
                cp.start()
            for cp in copies:
                cp.wait_recv()
                cp.wait_send()

        run([_remote(g[i].at[:, 1 - c], landed_pair[i], sems[0:2], i, sibling) for i in range(n)])
        for i in range(n):
            for kk in range(N_CHIPS):
                partial[i][kk] = (g[i][kk, c] + landed_pair[i][kk]).astype(BF16)
        run([_remote(partial[i].at[kj], landed_chips[i].at[j], sems[2:4], (j, i), (*chip, c))
             for j, (chip, kj) in enumerate(_other_chips(x, y)) for i in range(n)])
        for i in range(n):
            own = g[i][k, c] + landed_pair[i][k]
            mine[i][...] = ((own + landed_chips[i][0].astype(F32)) + landed_chips[i][1].astype(F32)) \
                + landed_chips[i][2].astype(F32)
        run([_remote(mine[i], theirs[i], sems[4:6], i, sibling) for i in range(n)])

    halves = [jax.ShapeDtypeStruct(gr.shape[2:], F32) for gr in grads]
    vmem = pl.BlockSpec(memory_space=pltpu.VMEM)
    scratch = ([pltpu.VMEM((N_CHIPS,) + gr.shape[2:], F32) for gr in grads]
               + [pltpu.VMEM((N_CHIPS,) + gr.shape[2:], BF16) for gr in grads]
               + [pltpu.VMEM((3,) + gr.shape[2:], BF16) for gr in grads]
               + _sem_pairs(n) + _sem_pairs(3, n) + _sem_pairs(n))
    out = pl.pallas_call(
        body, name=name, out_shape=halves + halves, in_specs=[vmem] * n, out_specs=[vmem] * (2 * n),
        scratch_shapes=scratch, compiler_params=_params(48),
    )(*grads)
    return out[:n], out[n:]


def _row_tile(rows):
    for t in (256, 128, 64, 32, 16, 8):
        if rows % t == 0:
            return t
    return rows


def _add_pair(core, grad, landed):
    _, h, cols = landed.shape
    tr = _row_tile(h)

    def body(core_ref, g_ref, l_ref, o_ref, ob_ref):
        s = g_ref[...] + l_ref[...]
        o_ref[...] = s
        ob_ref[...] = s.astype(BF16)

    out_spec = pl.BlockSpec((None, tr, cols), lambda k, t, core_ref: (k, t, 0))
    return _pcall(
        body, name="grad_add_pair",
        out_shape=[jax.ShapeDtypeStruct(landed.shape, F32), jax.ShapeDtypeStruct(landed.shape, BF16)],
        grid_spec=pltpu.PrefetchScalarGridSpec(
            num_scalar_prefetch=1, grid=(N_CHIPS, h // tr),
            in_specs=[pl.BlockSpec((None, None, tr, cols), lambda k, t, core_ref: (k, core_ref[0], t, 0)),
                      pl.BlockSpec((None, tr, cols), lambda k, t, core_ref: (k, t, 0))],
            out_specs=[out_spec, out_spec]),
        compiler_params=_params(32, ("arbitrary", "arbitrary")),
        operands=(core, grad, landed))


def _add_chips(chip, partial, landed):
    _, h, cols = partial.shape
    tr = _row_tile(h)

    def body(chip_ref, p_ref, l_ref, o_ref):
        o_ref[...] = ((p_ref[...] + l_ref[0].astype(F32)) + l_ref[1].astype(F32)) + l_ref[2].astype(F32)

    return _pcall(
        body, name="grad_add_chips",
        out_shape=jax.ShapeDtypeStruct((h, cols), F32),
        grid_spec=pltpu.PrefetchScalarGridSpec(
            num_scalar_prefetch=1, grid=(h // tr,),
            in_specs=[pl.BlockSpec((None, tr, cols), lambda t, chip_ref: (chip_ref[0], t, 0)),
                      pl.BlockSpec((3, tr, cols), lambda t, chip_ref: (0, t, 0))],
            out_specs=pl.BlockSpec((tr, cols), lambda t, chip_ref: (t, 0))),
        compiler_params=_params(32, ("arbitrary",)),
        operands=(chip, partial, landed))


def _ada_project(c_all, w_ada, b_shard):
    n = w_ada.shape[1]
    tn = 512

    def body(c_ref, w_ref, b_ref, mod_ref, cond_ref):
        cv = c_ref[...]
        cond = cv * _sigmoid(cv)
        mod_ref[...] = _dot(cond.astype(BF16), w_ref[...].astype(BF16)) + b_ref[...]
        cond_ref[0:N_DEV, :] = cond
        cond_ref[N_DEV:2 * N_DEV, :] = jnp.zeros_like(cond)

    return _pcall(
        body, name="ada_project", grid=(n // tn,),
        out_shape=[jax.ShapeDtypeStruct((N_DEV, n), F32), jax.ShapeDtypeStruct((2 * N_DEV, D_MODEL), F32)],
        in_specs=[pl.BlockSpec((N_DEV, D_MODEL), lambda j: (0, 0)), pl.BlockSpec((D_MODEL, tn), lambda j: (0, j)),
                  pl.BlockSpec((1, tn), lambda j: (0, j))],
        out_specs=[pl.BlockSpec((N_DEV, tn), lambda j: (0, j)), pl.BlockSpec((2 * N_DEV, D_MODEL), lambda j: (0, 0))],
        compiler_params=_params(32, ("arbitrary",)),
        operands=(c_all, w_ada, b_shard))


def _in_project(x, pos, sc_a, sh_a, w_in, q_norm_w, kv_norm_w, w_q, w_kv, invf, exchange=None):
    t_len = x.shape[0]
    tm = min(512, t_len)

    def body(x_ref, pos_ref, sc_ref, sh_ref, win_ref, qn_ref, kvn_ref, wq_ref, wkv_ref, invf_ref,
             u_ref, zhg_ref, cq_ref, ckv_ref, q_ref, k_ref, kt_ref, v_ref, vt_ref):
        u = (x_ref[...] * (1.0 + sc_ref[...]) + sh_ref[...]).astype(BF16)
        u_ref[...] = u
        z = _dot(u, win_ref[...])
        zhg_ref[...] = z[:, :HG_COLS]
        cq = z[:, HG_COLS:HG_COLS + Q_RANK]
        ckv = z[:, HG_COLS + Q_RANK:HG_COLS + Q_RANK + KV_RANK]
        cq_ref[...] = cq
        ckv_ref[...] = ckv
        cos_t, sin_t = _rope_tables(pos_ref[...], invf_ref[...])
        k_pe = _rope(z[:, HG_COLS + Q_RANK + KV_RANK:], cos_t, sin_t)
        k_pe_t = jnp.transpose(k_pe).astype(BF16)
        cqn = (cq * lax.rsqrt(_rowmean(cq * cq) + RMS_EPS) * qn_ref[...]).astype(BF16)
        ckvn = (ckv * lax.rsqrt(_rowmean(ckv * ckv) + RMS_EPS) * kvn_ref[...]).astype(BF16)
        for h in range(N_HEADS):
            qh = _dot(cqn, wq_ref[h])
            q_ref[h, :, 0:HEAD_DIM] = qh[:, :HEAD_DIM].astype(BF16)
            q_ref[h, :, HEAD_DIM:QK_DIM] = _rope(qh[:, HEAD_DIM:], cos_t, sin_t).astype(BF16)
            kvh = _dot(ckvn, wkv_ref[h])
            k_ref[h, :, 0:HEAD_DIM] = kvh[:, :HEAD_DIM].astype(BF16)
            k_ref[h, :, HEAD_DIM:QK_DIM] = k_pe.astype(BF16)
            kt_ref[h, 0:HEAD_DIM, :] = jnp.transpose(kvh[:, :HEAD_DIM]).astype(BF16)
            kt_ref[h, HEAD_DIM:QK_DIM, :] = k_pe_t
            v_ref[h] = kvh[:, HEAD_DIM:].astype(BF16)
            vt_ref[h] = jnp.transpose(kvh[:, HEAD_DIM:]).astype(BF16)

    row = lambda i: (i, 0)
    fixed2 = lambda i: (0, 0)
    fixed3 = lambda i: (0, 0, 0)
    heads = lambda i: (0, i, 0)
    n_tiles = t_len // tm
    return _pallas(
        body, name="in_project", grid=(n_tiles,),
        operands=(x, pos, sc_a, sh_a, w_in, q_norm_w, kv_norm_w, w_q, w_kv, invf),
        out_shape=[jax.ShapeDtypeStruct((t_len, D_MODEL), BF16), jax.ShapeDtypeStruct((t_len, HG_COLS), F32),
                   jax.ShapeDtypeStruct((t_len, Q_RANK), F32), jax.ShapeDtypeStruct((t_len, KV_RANK), F32),
                   jax.ShapeDtypeStruct((N_HEADS, t_len, QK_DIM), BF16),
                   jax.ShapeDtypeStruct((N_HEADS, t_len, QK_DIM), BF16),
                   jax.ShapeDtypeStruct((N_HEADS, QK_DIM, t_len), BF16),
                   jax.ShapeDtypeStruct((N_HEADS, t_len, HEAD_DIM), BF16),
                   jax.ShapeDtypeStruct((N_HEADS, HEAD_DIM, t_len), BF16)],
        in_specs=[pl.BlockSpec((tm, D_MODEL), row), pl.BlockSpec((tm, 1), row),
                  pl.BlockSpec((1, D_MODEL), fixed2), pl.BlockSpec((1, D_MODEL), fixed2),
                  pl.BlockSpec((D_MODEL, IN_COLS_PAD), fixed2),
                  pl.BlockSpec((1, Q_RANK), fixed2), pl.BlockSpec((1, KV_RANK), fixed2),
                  pl.BlockSpec((N_HEADS, Q_RANK, QK_DIM), fixed3), pl.BlockSpec((N_HEADS, KV_RANK, 2 * HEAD_DIM), fixed3),
                  pl.BlockSpec((1, 128), fixed2)],
        out_specs=[pl.BlockSpec((tm, D_MODEL), row), pl.BlockSpec((tm, HG_COLS), row),
                   pl.BlockSpec((tm, Q_RANK), row), pl.BlockSpec((tm, KV_RANK), row),
                   pl.BlockSpec((N_HEADS, tm, QK_DIM), heads), pl.BlockSpec((N_HEADS, tm, QK_DIM), heads),
                   pl.BlockSpec((N_HEADS, QK_DIM, tm), lambda i: (0, 0, i)),
                   pl.BlockSpec((N_HEADS, tm, HEAD_DIM), heads),
                   pl.BlockSpec((N_HEADS, HEAD_DIM, tm), lambda i: (0, 0, i))],
        params=_params(48, ("arbitrary",)), exchange=exchange,
        first=lambda: pl.program_id(0) == 0, last=lambda: pl.program_id(0) == n_tiles - 1)


def _lower_bound(lb_raw):
    m = jnp.max(lb_raw, axis=0, keepdims=True)
    e = jnp.exp(lb_raw - m)
    return e[0:1] / jnp.sum(e, axis=0, keepdims=True)


def _tri(inclusive_lower):
    r = lax.broadcasted_iota(jnp.int32, (HG_CHUNK, HG_CHUNK), 0)
    c = lax.broadcasted_iota(jnp.int32, (HG_CHUNK, HG_CHUNK), 1)
    return (c <= r) if inclusive_lower else (c >= r)


def _chunk_rows(n):
    return slice(n * HG_CHUNK, (n + 1) * HG_CHUNK)


def _chunk_prefix_sums(v, inclusive_lower):
    tri = _tri(inclusive_lower).astype(BF16)
    hi = v.astype(BF16)
    rest = v - hi.astype(F32)
    mid = rest.astype(BF16)
    lo = (rest - mid.astype(F32)).astype(BF16)
    pieces = jnp.concatenate([hi, mid, lo], axis=1)
    out = []
    for n in range(v.shape[0] // HG_CHUNK):
        s = _dot(tri, pieces[_chunk_rows(n)])
        out.append((s[:, 0:HEAD_DIM] + s[:, HEAD_DIM:2 * HEAD_DIM]) + s[:, 2 * HEAD_DIM:])
    return jnp.concatenate(out, axis=0)


def _per_chunk(v, row):
    n = v.shape[0] // HG_CHUNK
    v3 = v.reshape(n, HG_CHUNK, HEAD_DIM)
    return jnp.broadcast_to(v3[:, row:row + 1, :], v3.shape).reshape(v.shape)


def _hg_block(q, f_logit, lb):
    sg = _sigmoid(f_logit)
    forget = lb + (1.0 - lb) * sg
    kk = 1.0 - forget
    b = _chunk_prefix_sums(jnp.log(forget), True)
    b_ref = _per_chunk(b, HG_CHUNK // 2 - 1)
    b_last = _per_chunk(b, HG_CHUNK - 1)
    e_i = jnp.exp(b - b_ref)
    e_ri = jnp.exp(b_ref - b)
    e_b = jnp.exp(b)
    e_l = jnp.exp(b_last - b)
    return dict(sg=sg, forget=forget, e_i=e_i, e_ri=e_ri, e_b=e_b, e_l=e_l, dec=jnp.exp(b_last),
                qi=q * e_i, ki=kk * e_ri, qe=q * e_b, kl=kk * e_l)


def _hgrn_forward(zhg, lb_raw, norm_w, exchange=None):
    t_len = zhg.shape[0]
    tb = min(512, t_len)
    n_chunks = tb // HG_CHUNK

    def body(q_ref, f_ref, v_ref, g_ref, lb_ref, w_ref, opre_ref, o_ref, st_ref, state):
        @pl.when(pl.program_id(1) == 0)
        def _():
            state[...] = jnp.zeros_like(state)

        blk = _hg_block(q_ref[...], f_ref[...], _lower_bound(lb_ref[...]))
        v = v_ref[...].astype(BF16)
        qi, ki, qe, kl = (blk[name].astype(BF16) for name in ("qi", "ki", "qe", "kl"))
        causal = _tri(True)
        st = state[...]
        parts = []
        for n in range(n_chunks):
            r = _chunk_rows(n)
            a = jnp.where(causal, _dot_nt(qi[r], ki[r]), 0.0).astype(BF16)
            st_ref[0, n] = st
            parts.append(_dot(a, v[r]) + _dot_nt(qe[r], st.astype(BF16)))
            st = st * blk["dec"][n * HG_CHUNK:n * HG_CHUNK + 1] + _dot_tn(v[r], kl[r])
        state[...] = st
        o = jnp.concatenate(parts, axis=0)
        opre_ref[...] = o
        g = g_ref[...]
        o_ref[...] = o * lax.rsqrt(_rowmean(o * o) + RMS_EPS) * w_ref[...] * (g * _sigmoid(g))

    col = lambda off: (lambda h, t: (t, off + h))
    nb = t_len // tb
    return _pallas(
        body, name="hgrn_forward", grid=(N_HEADS, nb), operands=(zhg, zhg, zhg, zhg, lb_raw, norm_w),
        out_shape=[jax.ShapeDtypeStruct((t_len, N_HEADS * HEAD_DIM), F32),
                   jax.ShapeDtypeStruct((t_len, N_HEADS * HEAD_DIM), F32),
                   jax.ShapeDtypeStruct((N_HEADS, t_len // HG_CHUNK, HEAD_DIM, HEAD_DIM), F32)],
        in_specs=[pl.BlockSpec((tb, HEAD_DIM), col(0)), pl.BlockSpec((tb, HEAD_DIM), col(N_HEADS)),
                  pl.BlockSpec((tb, HEAD_DIM), col(2 * N_HEADS)), pl.BlockSpec((tb, HEAD_DIM), col(3 * N_HEADS)),
                  pl.BlockSpec((2, HEAD_DIM), lambda h, t: (0, h)), pl.BlockSpec((1, HEAD_DIM), lambda h, t: (0, h))],
        out_specs=[pl.BlockSpec((tb, HEAD_DIM), col(0)), pl.BlockSpec((tb, HEAD_DIM), col(0)),
                   pl.BlockSpec((1, n_chunks, HEAD_DIM, HEAD_DIM), lambda h, t: (h, t, 0, 0))],
        scratch_shapes=[pltpu.VMEM((HEAD_DIM, HEAD_DIM), F32)],
        params=_params(32, ("arbitrary", "arbitrary")), exchange=exchange,
        first=lambda: (pl.program_id(0) == 0) & (pl.program_id(1) == 0),
        last=lambda: (pl.program_id(0) == N_HEADS - 1) & (pl.program_id(1) == nb - 1))


def _hgrn_backward(zhg, lb_raw, norm_w, o_pre, d_cat, states, exchange=None):
    t_len = zhg.shape[0]
    tb = min(512, t_len)
    n_chunks = tb // HG_CHUNK
    nb = t_len // tb

    def body(q_ref, f_ref, v_ref, g_ref, lb_ref, w_ref, opre_ref, do_ref, st_ref,
             dq_ref, df_ref, dv_ref, dg_ref, sums_ref, gstate):
        @pl.when(pl.program_id(1) == 0)
        def _():
            gstate[...] = jnp.zeros_like(gstate)
            sums_ref[...] = jnp.zeros_like(sums_ref)

        lb = _lower_bound(lb_ref[...])
        w = w_ref[...]
        o = opre_ref[...]
        g = g_ref[...]
        d_out = do_ref[...]
        r = lax.rsqrt(_rowmean(o * o) + RMS_EPS)
        sg_g = _sigmoid(g)
        dg_ref[...] = (d_out * (o * r * w) * (sg_g * (1.0 + g * (1.0 - sg_g)))).astype(BF16)
        d_on = d_out * (g * sg_g)
        sums_ref[1:2, :] += _colsum(d_on * o * r)
        dy = d_on * w
        d_o = (r * dy - o * (r * r * r) * _rowmean(dy * o)).astype(BF16)
        blk = _hg_block(q_ref[...], f_ref[...], lb)
        v = v_ref[...].astype(BF16)
        qi, ki, qe, kl = (blk[name].astype(BF16) for name in ("qi", "ki", "qe", "kl"))
        causal = _tri(True)
        row_id = lax.broadcasted_iota(jnp.int32, (HG_CHUNK, HEAD_DIM), 0)
        gt = gstate[...]
        d_v, d_qi, d_ki, d_qe, d_kl, d_dec = ([None] * n_chunks for _ in range(6))
        for n in reversed(range(n_chunks)):
            rows = _chunk_rows(n)
            st = st_ref[0, n]
            a = jnp.where(causal, _dot_nt(qi[rows], ki[rows]), 0.0).astype(BF16)
            d_a = jnp.where(causal, _dot_nt(d_o[rows], v[rows]), 0.0).astype(BF16)
            gt_b = gt.astype(BF16)
            d_v[n] = _dot_tn(a, d_o[rows]) + _dot_nt(kl[rows], gt_b)
            d_qi[n] = _dot(d_a, ki[rows])
            d_ki[n] = _dot_tn(d_a, qi[rows])
            d_qe[n] = _dot(d_o[rows], st.astype(BF16))
            d_kl[n] = _dot(v[rows], gt_b)
            d_dec[n] = jnp.where(row_id == HG_CHUNK - 1, _colsum(gt * st), 0.0)
            gt = gt * blk["dec"][n * HG_CHUNK:n * HG_CHUNK + 1] + _dot_tn(d_o[rows], qe[rows])
        gstate[...] = gt
        d_qi, d_ki, d_qe, d_kl, d_dec = (jnp.concatenate(p, axis=0) for p in (d_qi, d_ki, d_qe, d_kl, d_dec))
        dv_ref[...] = jnp.concatenate(d_v, axis=0).astype(BF16)
        dq_ref[...] = (d_qi * blk["e_i"] + d_qe * blk["e_b"]).astype(BF16)
        d_k = d_ki * blk["e_ri"] + d_kl * blk["e_l"]
        t_qi = d_qi * blk["qi"]
        t_ki = d_ki * blk["ki"]
        t_kl = d_kl * blk["kl"]
        at_ref, at_last = [], []
        for n in range(n_chunks):
            rows = _chunk_rows(n)
            at_ref.append(jnp.where(row_id == HG_CHUNK // 2 - 1, _colsum(t_ki[rows] - t_qi[rows]), 0.0))
            at_last.append(jnp.where(row_id == HG_CHUNK - 1, _colsum(t_kl[rows]), 0.0))
        d_b = (t_qi - t_ki + d_qe * blk["qe"] - t_kl + jnp.concatenate(at_ref, axis=0)
               + jnp.concatenate(at_last, axis=0) + d_dec * blk["dec"])
        d_forget = _chunk_prefix_sums(d_b, False) / blk["forget"] - d_k
        sg = blk["sg"]
        df_ref[...] = (d_forget * (1.0 - lb) * sg * (1.0 - sg)).astype(BF16)
        sums_ref[0:1, :] += _colsum(d_forget * (1.0 - sg))

    col = lambda off: (lambda h, t: (nb - 1 - t, off + h))
    return _pallas(
        body, name="hgrn_backward", grid=(N_HEADS, nb),
        operands=(zhg, zhg, zhg, zhg, lb_raw, norm_w, o_pre, d_cat, states),
        out_shape=[jax.ShapeDtypeStruct((t_len, N_HEADS * HEAD_DIM), BF16)] * 4
        + [jax.ShapeDtypeStruct((8, N_HEADS * HEAD_DIM), F32)],
        in_specs=[pl.BlockSpec((tb, HEAD_DIM), col(0)), pl.BlockSpec((tb, HEAD_DIM), col(N_HEADS)),
                  pl.BlockSpec((tb, HEAD_DIM), col(2 * N_HEADS)), pl.BlockSpec((tb, HEAD_DIM), col(3 * N_HEADS)),
                  pl.BlockSpec((2, HEAD_DIM), lambda h, t: (0, h)), pl.BlockSpec((1, HEAD_DIM), lambda h, t: (0, h)),
                  pl.BlockSpec((tb, HEAD_DIM), col(0)), pl.BlockSpec((tb, HEAD_DIM), col(0)),
                  pl.BlockSpec((1, n_chunks, HEAD_DIM, HEAD_DIM), lambda h, t: (h, nb - 1 - t, 0, 0))],
        out_specs=[pl.BlockSpec((tb, HEAD_DIM), col(0))] * 4 + [pl.BlockSpec((8, HEAD_DIM), lambda h, t: (0, h))],
        scratch_shapes=[pltpu.VMEM((HEAD_DIM, HEAD_DIM), F32)],
        params=_params(32, ("arbitrary", "arbitrary")), exchange=exchange,
        first=lambda: (pl.program_id(0) == 0) & (pl.program_id(1) == 0),
        last=lambda: (pl.program_id(0) == N_HEADS - 1) & (pl.program_id(1) == nb - 1))


ATT_LOG2 = ATT_SCALE * 1.4426950408889634


def _triangle_steps(nq, q_major):
    if q_major:
        pairs = [(i, j) for i in range(nq) for j in range(i + 1)]
    else:
        pairs = [(i, j) for j in range(nq) for i in range(j, nq)]
    return jnp.array([p[0] for p in pairs], jnp.int32), jnp.array([p[1] for p in pairs], jnp.int32)


def _key_le_query(t):
    return lax.broadcasted_iota(jnp.int32, (t, t), 0) <= lax.broadcasted_iota(jnp.int32, (t, t), 1)


def _attention_forward(q, k, v_t, exchange=None):
    t_len = q.shape[1]
    tq = min(512, t_len)
    nq = t_len // tq
    qi_tab, ki_tab = _triangle_steps(nq, True)

    def body(qi_ref, ki_ref, q_ref, k_ref, vt_ref, o_ref, lse_ref, m_s, l_s, acc_s):
        step = pl.program_id(0)
        qi, ki = qi_ref[step], ki_ref[step]

        @pl.when(ki == 0)
        def _():
            m_s[...] = jnp.full_like(m_s, NEG_BIG)
            l_s[...] = jnp.zeros_like(l_s)
            acc_s[...] = jnp.zeros_like(acc_s)

        def accumulate(masked):
            for h in range(N_HEADS):
                s_t = _dot_nt(k_ref[h], q_ref[h]) * ATT_LOG2
                if masked:
                    s_t = jnp.where(_key_le_query(tq), s_t, NEG_BIG)
                m_old = m_s[h]
                m_new = jnp.maximum(m_old, jnp.max(s_t, axis=0, keepdims=True))
                alpha = jnp.exp2(m_old - m_new)
                p_t = jnp.exp2(s_t - m_new)
                l_s[h] = alpha * l_s[h] + jnp.sum(p_t, axis=0, keepdims=True)
                acc_s[h] = alpha * acc_s[h] + _dot(vt_ref[h], p_t.astype(BF16))
                m_s[h] = m_new

        @pl.when(ki < qi)
        def _():
            accumulate(False)

        @pl.when(ki == qi)
        def _():
            accumulate(True)
            for h in range(N_HEADS):
                o_ref[:, h * HEAD_DIM:(h + 1) * HEAD_DIM] = jnp.transpose(acc_s[h] / l_s[h])
                lse_ref[h] = m_s[h] + jnp.log2(l_s[h])

    n_steps = qi_tab.shape[0]
    return _pallas(
        body, name="attention_forward", grid=(n_steps,), prefetch=(qi_tab, ki_tab), operands=(q, k, v_t),
        out_shape=[jax.ShapeDtypeStruct((t_len, N_HEADS * HEAD_DIM), F32),
                   jax.ShapeDtypeStruct((N_HEADS, 1, t_len), F32)],
        in_specs=[pl.BlockSpec((N_HEADS, tq, QK_DIM), lambda s, qt, kt: (0, qt[s], 0)),
                  pl.BlockSpec((N_HEADS, tq, QK_DIM), lambda s, qt, kt: (0, kt[s], 0)),
                  pl.BlockSpec((N_HEADS, HEAD_DIM, tq), lambda s, qt, kt: (0, 0, kt[s]))],
        out_specs=[pl.BlockSpec((tq, N_HEADS * HEAD_DIM), lambda s, qt, kt: (qt[s], 0)),
                   pl.BlockSpec((N_HEADS, 1, tq), lambda s, qt, kt: (0, 0, qt[s]))],
        scratch_shapes=[pltpu.VMEM((N_HEADS, 1, tq), F32), pltpu.VMEM((N_HEADS, 1, tq), F32),
                        pltpu.VMEM((N_HEADS, HEAD_DIM, tq), F32)],
        params=_params(48, ("arbitrary",)), exchange=exchange,
        first=lambda qt, kt: pl.program_id(0) == 0, last=lambda qt, kt: pl.program_id(0) == n_steps - 1)


BWD_HEADS = 2


def _attention_backward(q, k, k_t, v, d_cat, lse, delta, exchange=None):
    t_len = q.shape[1]
    tq = min(512, t_len)
    nq = t_len // tq
    hp = BWD_HEADS
    qi_tab, ki_tab = _triangle_steps(nq, False)

    def body(qi_ref, ki_ref, q_ref, k_ref, kt_ref, v_ref, do_ref, lse_ref, delta_ref, dqt_hbm, dk_ref, dv_ref,
             dqt_s, dk_s, dv_s):
        group, step = pl.program_id(0), pl.program_id(1)
        qi, ki = qi_ref[step], ki_ref[step]

        @pl.when(step == 0)
        def _():
            dqt_s[...] = jnp.zeros_like(dqt_s)

        @pl.when(qi == ki)
        def _():
            dk_s[...] = jnp.zeros_like(dk_s)
            dv_s[...] = jnp.zeros_like(dv_s)

        def accumulate(masked):
            for h in range(hp):
                do_b = do_ref[:, h * HEAD_DIM:(h + 1) * HEAD_DIM].astype(BF16)
                s_t = _dot_nt(k_ref[h], q_ref[h]) * ATT_LOG2
                if masked:
                    s_t = jnp.where(_key_le_query(tq), s_t, NEG_BIG)
                p_t = jnp.exp2(s_t - lse_ref[h])
                dp_t = _dot_nt(v_ref[h], do_b)
                ds_t = (p_t * (dp_t - delta_ref[h]) * ATT_SCALE).astype(BF16)
                dv_s[h] += _dot(p_t.astype(BF16), do_b)
                dk_s[h] += _dot(ds_t, q_ref[h])
                dqt_s[h, qi] += _dot(kt_ref[h], ds_t)

        @pl.when(ki < qi)
        def _():
            accumulate(False)

        @pl.when(ki == qi)
        def _():
            accumulate(True)
            for h in range(hp):
                pltpu.sync_copy(dqt_s.at[h, qi], dqt_hbm.at[group * hp + h, qi])

        @pl.when(qi == nq - 1)
        def _():
            dk_ref[...] = dk_s[...]
            dv_ref[...] = dv_s[...]

    wide = hp * HEAD_DIM
    n_groups, n_steps = N_HEADS // hp, qi_tab.shape[0]
    return _pallas(
        body, name="attention_backward", grid=(n_groups, n_steps), prefetch=(qi_tab, ki_tab),
        operands=(q, k, k_t, v, d_cat, lse, delta),
        out_shape=[jax.ShapeDtypeStruct((N_HEADS, nq, QK_DIM, tq), F32),
                   jax.ShapeDtypeStruct((N_HEADS, t_len, QK_DIM), F32),
                   jax.ShapeDtypeStruct((N_HEADS, t_len, HEAD_DIM), F32)],
        in_specs=[pl.BlockSpec((hp, tq, QK_DIM), lambda g, s, qt, kt: (g, qt[s], 0)),
                  pl.BlockSpec((hp, tq, QK_DIM), lambda g, s, qt, kt: (g, kt[s], 0)),
                  pl.BlockSpec((hp, QK_DIM, tq), lambda g, s, qt, kt: (g, 0, kt[s])),
                  pl.BlockSpec((hp, tq, HEAD_DIM), lambda g, s, qt, kt: (g, kt[s], 0)),
                  pl.BlockSpec((tq, wide), lambda g, s, qt, kt: (qt[s], n_groups + g)),
                  pl.BlockSpec((hp, 1, tq), lambda g, s, qt, kt: (g, 0, qt[s])),
                  pl.BlockSpec((hp, 1, tq), lambda g, s, qt, kt: (g, 0, qt[s]))],
        out_specs=[pl.BlockSpec(memory_space=pl.ANY),
                   pl.BlockSpec((hp, tq, QK_DIM), lambda g, s, qt, kt: (g, kt[s], 0)),
                   pl.BlockSpec((hp, tq, HEAD_DIM), lambda g, s, qt, kt: (g, kt[s], 0))],
        scratch_shapes=[pltpu.VMEM((hp, nq, QK_DIM, tq), F32), pltpu.VMEM((hp, tq, QK_DIM), F32),
                        pltpu.VMEM((hp, tq, HEAD_DIM), F32)],
        params=_params(48, ("arbitrary", "arbitrary")), exchange=exchange,
        first=lambda qt, kt: (pl.program_id(0) == 0) & (pl.program_id(1) == 0),
        last=lambda qt, kt: (pl.program_id(0) == n_groups - 1) & (pl.program_id(1) == n_steps - 1))


def _out_project(o_hg, o_mla, x, g_a, w_out, exchange=None):
    t_len = x.shape[0]
    tm = min(512, t_len)
    half = N_HEADS * HEAD_DIM

    def body(ohg_ref, omla_ref, x_ref, ga_ref, w_ref, cat_ref, mix_ref, xhat_ref, rstd_ref):
        a = ohg_ref[...].astype(BF16)
        b = omla_ref[...].astype(BF16)
        cat_ref[:, 0:half] = a
        cat_ref[:, half:2 * half] = b
        mix = _dot(a, w_ref[0:half, :]) + _dot(b, w_ref[half:2 * half, :])
        mix_ref[...] = mix
        r1 = DN_ALPHA * x_ref[...] + (1.0 + ga_ref[...]) * mix
        xc = r1 - _rowmean(r1)
        rstd = lax.rsqrt(_rowmean(xc * xc) + LN_EPS)
        xhat_ref[...] = xc * rstd
        rstd_ref[...] = rstd

    row = lambda i: (i, 0)
    fixed = lambda i: (0, 0)
    n_tiles = t_len // tm
    return _pallas(
        body, name="out_project", grid=(n_tiles,), operands=(o_hg, o_mla, x, g_a, w_out),
        out_shape=[jax.ShapeDtypeStruct((t_len, D_MODEL), BF16), jax.ShapeDtypeStruct((t_len, D_MODEL), F32),
                   jax.ShapeDtypeStruct((t_len, D_MODEL), F32), jax.ShapeDtypeStruct((t_len, 1), F32)],
        in_specs=[pl.BlockSpec((tm, half), row), pl.BlockSpec((tm, half), row), pl.BlockSpec((tm, D_MODEL), row),
                  pl.BlockSpec((1, D_MODEL), fixed), pl.BlockSpec((D_MODEL, D_MODEL), fixed)],
        out_specs=[pl.BlockSpec((tm, D_MODEL), row), pl.BlockSpec((tm, D_MODEL), row),
                   pl.BlockSpec((tm, D_MODEL), row), pl.BlockSpec((tm, 1), row)],
        params=_params(48, ("arbitrary",)), exchange=exchange,
        first=lambda: pl.program_id(0) == 0, last=lambda: pl.program_id(0) == n_tiles - 1)


V_LN1G, V_LN1B, V_SCM, V_SHM, V_GM, V_GA, V_LN2G, V_LN2B = range(8)
S_DLN2G, S_DLN2B, S_DGM, S_DSCM, S_DSHM, S_DLN1G, S_DLN1B, S_DGA, S_LOSS = range(9)


def _mlp_and_back(xhat1, rstd1, mix, target, o_mla, vecs, w1, w2, w_out):
    t_len = xhat1.shape[0]
    tm = min(256, t_len)
    n_ff = w1.shape[0]
    ff = w1.shape[2]

    def body(xhat_ref, rstd_ref, mix_ref, tgt_ref, omla_ref, vec_ref, w1_hbm, w2_hbm, wout_hbm,
             act_ref, dhp_ref, um_ref, dh_ref, dmix_ref, dcat_ref, dr1_ref, sums_ref, delta_ref,
             w1_s, w2_s, wout_s, hp_s, load_sems):
        @pl.when(pl.program_id(0) == 0)
        def _():
            loads = [pltpu.make_async_copy(w1_hbm, w1_s, load_sems.at[0]),
                     pltpu.make_async_copy(w2_hbm, w2_s, load_sems.at[1]),
                     pltpu.make_async_copy(wout_hbm, wout_s, load_sems.at[2])]
            for cp in loads:
                cp.start()
            sums_ref[...] = jnp.zeros_like(sums_ref)
            for cp in loads:
                cp.wait()

        vec = lambda r: vec_ref[r:r + 1, :]
        xhat = xhat_ref[...]
        x1 = xhat * vec(V_LN1G) + vec(V_LN1B)
        um = (x1 * (1.0 + vec(V_SCM)) + vec(V_SHM)).astype(BF16)
        um_ref[...] = um
        h = jnp.zeros((tm, D_MODEL), F32)
        for j in range(n_ff):
            hp = _dot(um, w1_s[j])
            hp_s[j] = hp
            act = jnp.square(jnp.maximum(hp, 0.0)).astype(BF16)
            act_ref[:, j * ff:(j + 1) * ff] = act
            h = h + _dot(act, w2_s[j])
        r2 = DN_ALPHA * x1 + (1.0 + vec(V_GM)) * h
        xc = r2 - _rowmean(r2)
        rstd2 = lax.rsqrt(_rowmean(xc * xc) + LN_EPS)
        xhat2 = xc * rstd2
        err = xhat2 * vec(V_LN2G) + vec(V_LN2B) - tgt_ref[...]
        loss = 0.5 * jnp.sum(_rowmean(err * err))
        dy = err * (1.0 / D_MODEL)
        dxh = dy * vec(V_LN2G)
        dr2 = rstd2 * (dxh - _rowmean(dxh) - xhat2 * _rowmean(dxh * xhat2))
        dh = ((1.0 + vec(V_GM)) * dr2).astype(BF16)
        dh_ref[...] = dh
        sums_ref[S_DLN2G:S_DLN2G + 1, :] += _colsum(dy * xhat2)
        sums_ref[S_DLN2B:S_DLN2B + 1, :] += _colsum(dy)
        sums_ref[S_DGM:S_DGM + 1, :] += _colsum(dr2 * h)
        sums_ref[S_LOSS:S_LOSS + 1, :] += jnp.full((1, D_MODEL), loss, F32)
        du = jnp.zeros((tm, D_MODEL), F32)
        for j in range(n_ff):
            dhp = (_dot_nt(dh, w2_s[j]) * (2.0 * jnp.maximum(hp_s[j], 0.0))).astype(BF16)
            dhp_ref[:, j * ff:(j + 1) * ff] = dhp
            du = du + _dot_nt(dhp, w1_s[j])
        sums_ref[S_DSCM:S_DSCM + 1, :] += _colsum(du * x1)
        sums_ref[S_DSHM:S_DSHM + 1, :] += _colsum(du)
        dx1 = DN_ALPHA * dr2 + du * (1.0 + vec(V_SCM))
        sums_ref[S_DLN1G:S_DLN1G + 1, :] += _colsum(dx1 * xhat)
        sums_ref[S_DLN1B:S_DLN1B + 1, :] += _colsum(dx1)
        dxh1 = dx1 * vec(V_LN1G)
        dr1 = rstd_ref[...] * (dxh1 - _rowmean(dxh1) - xhat * _rowmean(dxh1 * xhat))
        dr1_ref[...] = dr1
        sums_ref[S_DGA:S_DGA + 1, :] += _colsum(dr1 * mix_ref[...])
        dmix = ((1.0 + vec(V_GA)) * dr1).astype(BF16)
        dmix_ref[...] = dmix
        dcat = _dot_nt(dmix, wout_s[...])
        dcat_ref[...] = dcat
        ones = jnp.ones((8, HEAD_DIM), F32)
        half = N_HEADS * HEAD_DIM
        for hd in range(N_HEADS):
            prod = dcat[:, half + hd * HEAD_DIM:half + (hd + 1) * HEAD_DIM] * omla_ref[:, hd * HEAD_DIM:(hd + 1) * HEAD_DIM]
            delta_ref[hd] = lax.dot_general(ones, prod, (((1,), (1,)), ((), ())), preferred_element_type=F32,
                                            precision=lax.Precision.HIGHEST)[0:1]

    row = lambda i: (i, 0)
    fixed = lambda i: (0, 0)
    any_spec = pl.BlockSpec(memory_space=pl.ANY)
    return _pcall(
        body, name="mlp_and_back", grid=(t_len // tm,),
        out_shape=[jax.ShapeDtypeStruct((t_len, D_FF), BF16), jax.ShapeDtypeStruct((t_len, D_FF), BF16),
                   jax.ShapeDtypeStruct((t_len, D_MODEL), BF16), jax.ShapeDtypeStruct((t_len, D_MODEL), BF16),
                   jax.ShapeDtypeStruct((t_len, D_MODEL), BF16), jax.ShapeDtypeStruct((t_len, D_MODEL), F32),
                   jax.ShapeDtypeStruct((t_len, D_MODEL), F32), jax.ShapeDtypeStruct((16, D_MODEL), F32),
                   jax.ShapeDtypeStruct((N_HEADS, 1, t_len), F32)],
        in_specs=[pl.BlockSpec((tm, D_MODEL), row), pl.BlockSpec((tm, 1), row), pl.BlockSpec((tm, D_MODEL), row),
                  pl.BlockSpec((tm, D_MODEL), row), pl.BlockSpec((tm, N_HEADS * HEAD_DIM), row),
                  pl.BlockSpec((8, D_MODEL), fixed), any_spec, any_spec, any_spec],
        out_specs=[pl.BlockSpec((tm, D_FF), row), pl.BlockSpec((tm, D_FF), row), pl.BlockSpec((tm, D_MODEL), row),
                   pl.BlockSpec((tm, D_MODEL), row), pl.BlockSpec((tm, D_MODEL), row), pl.BlockSpec((tm, D_MODEL), row),
                   pl.BlockSpec((tm, D_MODEL), row), pl.BlockSpec((16, D_MODEL), fixed),
                   pl.BlockSpec((N_HEADS, 1, tm), lambda i: (0, 0, i))],
        scratch_shapes=[pltpu.VMEM(w1.shape, BF16), pltpu.VMEM(w2.shape, BF16), pltpu.VMEM(w_out.shape, BF16),
                        pltpu.VMEM((n_ff, tm, ff), F32), pltpu.SemaphoreType.DMA((3,))],
        compiler_params=_params(56, ("arbitrary",)),
        operands=(xhat1, rstd1, mix, target, o_mla, vecs, w1, w2, w_out))


def _in_project_backward(dq, dk, dv, cq, ckv, pos, invf, q_norm_w, kv_norm_w, w_q, w_kv,
                         d_hq, d_hf, d_hi, d_hg, w_in, dr1, x, sc_a, exchange=None):
    t_len = x.shape[0]
    tm = min(512, t_len)
    per_q = dq.shape[3] // tm
    hgw = N_HEADS * HEAD_DIM

    def body(dq_ref, dk_ref, dv_ref, cq_ref, ckv_ref, pos_ref, invf_ref, qn_ref, kvn_ref, wq_ref, wkv_ref,
             dhq_ref, dhf_ref, dhi_ref, dhg_ref, win_ref, dr1_ref, x_ref, sc_ref,
             dz_ref, dqf_ref, dkvu_ref, cqn_ref, ckvn_ref, gx_ref, sums_ref):
        @pl.when(pl.program_id(0) == 0)
        def _():
            sums_ref[...] = jnp.zeros_like(sums_ref)

        cos_t, sin_t = _rope_tables(pos_ref[...], invf_ref[...])
        cq = cq_ref[...]
        ckv = ckv_ref[...]
        rq = lax.rsqrt(_rowmean(cq * cq) + RMS_EPS)
        rkv = lax.rsqrt(_rowmean(ckv * ckv) + RMS_EPS)
        cqn_ref[...] = (cq * rq * qn_ref[...]).astype(BF16)
        ckvn_ref[...] = (ckv * rkv * kvn_ref[...]).astype(BF16)
        d_cqn = jnp.zeros((tm, Q_RANK), F32)
        d_ckvn = jnp.zeros((tm, KV_RANK), F32)
        d_kpe = jnp.zeros((tm, 128), F32)
        for h in range(N_HEADS):
            dqh = jnp.transpose(dq_ref[h])
            dqf_ref[h, :, 0:HEAD_DIM] = dqh[:, :HEAD_DIM].astype(BF16)
            dqf_ref[h, :, HEAD_DIM:QK_DIM] = _unrope(dqh[:, HEAD_DIM:], cos_t, sin_t).astype(BF16)
            d_cqn = d_cqn + _dot_nt(dqf_ref[h], wq_ref[h])
            dkh = dk_ref[h]
            d_kpe = d_kpe + dkh[:, HEAD_DIM:]
            dkvu_ref[h, :, 0:HEAD_DIM] = dkh[:, :HEAD_DIM].astype(BF16)
            dkvu_ref[h, :, HEAD_DIM:2 * HEAD_DIM] = dv_ref[h].astype(BF16)
            d_ckvn = d_ckvn + _dot_nt(dkvu_ref[h], wkv_ref[h])
        dyq = d_cqn * qn_ref[...]
        dykv = d_ckvn * kvn_ref[...]
        sums_ref[2:3, 0:Q_RANK] += _colsum(d_cqn * cq * rq)
        sums_ref[3:4, 0:KV_RANK] += _colsum(d_ckvn * ckv * rkv)
        dz_ref[:, 0:hgw] = dhq_ref[...]
        dz_ref[:, hgw:2 * hgw] = dhf_ref[...]
        dz_ref[:, 2 * hgw:3 * hgw] = dhi_ref[...]
        dz_ref[:, 3 * hgw:4 * hgw] = dhg_ref[...]
        dz_ref[:, HG_COLS:HG_COLS + Q_RANK] = (rq * dyq - cq * (rq * rq * rq) * _rowmean(dyq * cq)).astype(BF16)
        dz_ref[:, HG_COLS + Q_RANK:HG_COLS + Q_RANK + KV_RANK] = (
            rkv * dykv - ckv * (rkv * rkv * rkv) * _rowmean(dykv * ckv)).astype(BF16)
        dz_ref[:, HG_COLS + Q_RANK + KV_RANK:] = _unrope(d_kpe, cos_t, sin_t).astype(BF16)
        du = _dot_nt(dz_ref[...], win_ref[...])
        xv = x_ref[...]
        gx_ref[...] = DN_ALPHA * dr1_ref[...] + (1.0 + sc_ref[...]) * du
        sums_ref[0:1, :] += _colsum(du * xv)
        sums_ref[1:2, :] += _colsum(du)

    row = lambda i: (i, 0)
    fixed2 = lambda i: (0, 0)
    fixed3 = lambda i: (0, 0, 0)
    heads = lambda i: (0, i, 0)
    n_tiles = t_len // tm
    return _pallas(
        body, name="in_project_backward", grid=(n_tiles,),
        operands=(dq, dk, dv, cq, ckv, pos, invf, q_norm_w, kv_norm_w, w_q, w_kv, d_hq, d_hf, d_hi, d_hg, w_in, dr1, x,
                  sc_a),
        out_shape=[jax.ShapeDtypeStruct((t_len, IN_COLS_PAD), BF16), jax.ShapeDtypeStruct((N_HEADS, t_len, QK_DIM), BF16),
                   jax.ShapeDtypeStruct((N_HEADS, t_len, 2 * HEAD_DIM), BF16), jax.ShapeDtypeStruct((t_len, Q_RANK), BF16),
                   jax.ShapeDtypeStruct((t_len, KV_RANK), BF16), jax.ShapeDtypeStruct((t_len, D_MODEL), F32),
                   jax.ShapeDtypeStruct((8, D_MODEL), F32)],
        in_specs=[pl.BlockSpec((N_HEADS, None, QK_DIM, tm), lambda i: (0, i // per_q, 0, i % per_q)),
                  pl.BlockSpec((N_HEADS, tm, QK_DIM), heads),
                  pl.BlockSpec((N_HEADS, tm, HEAD_DIM), heads), pl.BlockSpec((tm, Q_RANK), row),
                  pl.BlockSpec((tm, KV_RANK), row), pl.BlockSpec((tm, 1), row), pl.BlockSpec((1, 128), fixed2),
                  pl.BlockSpec((1, Q_RANK), fixed2), pl.BlockSpec((1, KV_RANK), fixed2),
                  pl.BlockSpec((N_HEADS, Q_RANK, QK_DIM), fixed3), pl.BlockSpec((N_HEADS, KV_RANK, 2 * HEAD_DIM), fixed3),
                  pl.BlockSpec((tm, hgw), row), pl.BlockSpec((tm, hgw), row), pl.BlockSpec((tm, hgw), row),
                  pl.BlockSpec((tm, hgw), row), pl.BlockSpec((D_MODEL, IN_COLS_PAD), fixed2),
                  pl.BlockSpec((tm, D_MODEL), row), pl.BlockSpec((tm, D_MODEL), row), pl.BlockSpec((1, D_MODEL), fixed2)],
        out_specs=[pl.BlockSpec((tm, IN_COLS_PAD), row), pl.BlockSpec((N_HEADS, tm, QK_DIM), heads),
                   pl.BlockSpec((N_HEADS, tm, 2 * HEAD_DIM), heads), pl.BlockSpec((tm, Q_RANK), row),
                   pl.BlockSpec((tm, KV_RANK), row), pl.BlockSpec((tm, D_MODEL), row), pl.BlockSpec((8, D_MODEL), fixed2)],
        params=_params(48, ("arbitrary",)), exchange=exchange,
        first=lambda: pl.program_id(0) == 0, last=lambda: pl.program_id(0) == n_tiles - 1)


def _weight_grad(a, b, name, n_blocks, bn, a_blocked=False, b_blocked=True, exchange=None, token_tile=512):
    t_len = a.shape[0]
    m = a.shape[1] // n_blocks if a_blocked else a.shape[1]
    bt = min(token_tile, t_len)

    def body(a_ref, b_ref, o_ref):
        @pl.when(pl.program_id(1) == 0)
        def _():
            o_ref[...] = jnp.zeros_like(o_ref)

        o_ref[...] += _dot_tn(a_ref[...].astype(BF16), b_ref[...].astype(BF16))

    a_spec = pl.BlockSpec((bt, m), (lambda n, t: (t, n)) if a_blocked else (lambda n, t: (t, 0)))
    if b.ndim == 3:
        b_spec = pl.BlockSpec((None, bt, bn), lambda n, t: (n, t, 0))
    else:
        b_spec = pl.BlockSpec((bt, bn), (lambda n, t: (t, n)) if b_blocked else (lambda n, t: (t, 0)))
    nt = t_len // bt
    (out,), landed = _pallas(
        body, name=name, grid=(n_blocks, nt), operands=(a, b),
        out_shape=[jax.ShapeDtypeStruct((n_blocks, m, bn), F32)],
        in_specs=[a_spec, b_spec],
        out_specs=[pl.BlockSpec((None, m, bn), lambda n, t: (n, 0, 0))],
        params=_params(56, ("arbitrary", "arbitrary")), exchange=exchange,
        first=lambda: (pl.program_id(0) == 0) & (pl.program_id(1) == 0),
        last=lambda: (pl.program_id(0) == n_blocks - 1) & (pl.program_id(1) == nt - 1))
    return (out, landed) if exchange else out


def _reduce_small(gathered, lb_raw):
    def body(g_ref, lb_ref, tot_ref, dlb_ref):
        tot = g_ref[0]
        for d in range(1, N_DEV):
            tot = tot + g_ref[d]
        tot_ref[...] = tot
        a = lb_ref[...]
        m = jnp.max(a, axis=0, keepdims=True)
        e = jnp.exp(a - m)
        lb = e[0:1] / jnp.sum(e, axis=0, keepdims=True)
        d0 = tot[10:11, 0:512] * lb * (1.0 - lb)
        dlb_ref[0:1, :] = d0
        dlb_ref[1:2, :] = -d0

    return pl.pallas_call(
        body, name="reduce_small",
        out_shape=[jax.ShapeDtypeStruct((SMALL_ROWS, D_MODEL), F32), jax.ShapeDtypeStruct((2, 512), F32)],
    )(gathered, lb_raw)


def _adamw_update(w, gv, m, v):
    nm = ADAM_B1 * m + (1.0 - ADAM_B1) * gv
    nv = ADAM_B2 * v + (1.0 - ADAM_B2) * jnp.square(gv)
    m_hat = nm / (1.0 - ADAM_B1 ** ADAM_STEP)
    v_hat = nv / (1.0 - ADAM_B2 ** ADAM_STEP)
    return -ADAM_LR * (m_hat / (jnp.sqrt(v_hat) + ADAM_EPS) + ADAM_WD * w), nm, nv


def _adamw_halves(core, w, mine, theirs, m, v, name):
    rows, cols = w.shape
    h = rows // 2
    tr = _row_tile(h)
    per_half = h // tr

    def body(core_ref, w_ref, mine_ref, theirs_ref, m_ref, v_ref, g_ref, d_ref, nm_ref, nv_ref):
        is_mine = pl.program_id(0) // per_half == core_ref[0]
        gv = jnp.where(is_mine, mine_ref[...], theirs_ref[...])
        g_ref[...] = gv
        d_ref[...], nm_ref[...], nv_ref[...] = _adamw_update(w_ref[...], gv, m_ref[...], v_ref[...])

    full = pl.BlockSpec((tr, cols), lambda i, core_ref: (i, 0))
    part = pl.BlockSpec((tr, cols), lambda i, core_ref: (i % per_half, 0))
    return _pcall(
        body, name=name, out_shape=[jax.ShapeDtypeStruct(w.shape, F32)] * 4,
        grid_spec=pltpu.PrefetchScalarGridSpec(
            num_scalar_prefetch=1, grid=(rows // tr,), in_specs=[full, part, part, full, full], out_specs=[full] * 4),
        compiler_params=_params(40, ("arbitrary",)),
        operands=(core, w, mine, theirs, m, v))


def _adamw(w, g, m, v, name):
    rows, cols = w.shape
    tr = _row_tile(rows) if rows >= 8 else rows

    def body(w_ref, g_ref, m_ref, v_ref, d_ref, nm_ref, nv_ref):
        d_ref[...], nm_ref[...], nv_ref[...] = _adamw_update(w_ref[...], g_ref[...], m_ref[...], v_ref[...])

    spec = pl.BlockSpec((tr, cols), lambda i: (i, 0))
    return _pcall(
        body, name=name, grid=(rows // tr,),
        out_shape=[jax.ShapeDtypeStruct(w.shape, F32)] * 3,
        in_specs=[spec] * 4, out_specs=[spec] * 3,
        compiler_params=_params(40, ("arbitrary",)),
        operands=(w, g, m, v))


def kernel(x, c, positions, w_ada, b_ada, w_in, hg_lower_bounds, hg_norm_w, mla_q_norm_w, w_q_up, mla_kv_norm_w, w_kv_up, w_out, ln1_g, ln1_b, w_mlp_in, w_mlp_out, ln2_g, ln2_b, loss_target, m_w_ada, m_b_ada, m_w_in, m_hg_lower_bounds, m_hg_norm_w, m_mla_q_norm_w, m_w_q_up, m_mla_kv_norm_w, m_w_kv_up, m_w_out, m_ln1_g, m_ln1_b, m_w_mlp_in, m_w_mlp_out, m_ln2_g, m_ln2_b, v_w_ada, v_b_ada, v_w_in, v_hg_lower_bounds, v_hg_norm_w, v_mla_q_norm_w, v_w_q_up, v_mla_kv_norm_w, v_w_kv_up, v_w_out, v_ln1_g, v_ln1_b, v_w_mlp_in, v_w_mlp_out, v_ln2_g, v_ln2_b):
    ix, iy, ic = _mesh_pos()
    chip = 2 * ix + iy
    me = 4 * ix + 2 * iy + ic
    core_arr = jnp.reshape(ic, (1,)).astype(jnp.int32)
    chip_arr = jnp.reshape(chip, (1,)).astype(jnp.int32)

    xs = x[0]
    target = loss_target[0]
    t_len = xs.shape[0]
    pos = positions.astype(F32).reshape(t_len, 1)
    inv = 1.0 / (ROPE_THETA ** (jnp.arange(0, ROPE_DIM, 2, dtype=F32) / ROPE_DIM))
    invf = jnp.concatenate([inv, inv, jnp.zeros((128 - ROPE_DIM,), F32)]).reshape(1, 128)

    def slot(w):
        rows, cols = w.shape
        own = w.astype(BF16).reshape(1, 2, rows // 2, cols)
        return lax.dynamic_update_slice(jnp.zeros((N_CHIPS, 2, rows // 2, cols), BF16), own, (chip, 0, 0, 0))

    def slot8(a):
        return lax.dynamic_update_slice(jnp.zeros((N_DEV,) + a.shape, a.dtype), a[None], (me, 0, 0))

    def whole(s):
        return s.reshape(N_CHIPS, 2 * s.shape[2], s.shape[3])

    def halved(g):
        return g.reshape(N_CHIPS, 2, g.shape[1] // 2, g.shape[2])

    ada_cols = w_ada.shape[2]
    c_all, *early = _run_exchange(
        _merge(_gather_all(slot8(jnp.broadcast_to(c, (8, D_MODEL)))),
               _gather_over_ici([slot(w_in[0]), slot(w_q_up[0]), slot(w_kv_up[0])])), "gather_c_and_mixer_weights_ici")
    b_shard = lax.dynamic_slice(b_ada, (0, chip * ada_cols), (1, ada_cols))
    mod_cols, cond16 = _ada_project(c_all[:, 0, :], w_ada[0], b_shard)
    mod_all, *early = _run_exchange(_merge(_gather_all(slot8(mod_cols)), _gather_over_d2d(early)),
                                    "gather_mod_and_mixer_weights_d2d")
    mod_mine = lax.dynamic_slice(mod_all, (0, me, 0), (N_DEV, 1, ada_cols))[::2, 0, :].reshape(6, D_MODEL)
    sh_a, sc_a, g_a, sh_m, sc_m, g_m = (mod_mine[i:i + 1] for i in range(6))
    g_in, g_q, g_kv = (whole(s) for s in early)
    w_in_full = jnp.transpose(g_in, (1, 0, 2)).reshape(D_MODEL, IN_COLS)
    w_in_full = jnp.pad(w_in_full, ((0, 0), (0, IN_COLS_PAD - IN_COLS)))
    w_q_full = jnp.pad(g_q, ((0, 0), (0, 0), (0, QK_DIM - g_q.shape[2])))

    (u_a, zhg, cq, ckv, q, k, k_t, v, v_t), (s_w1,) = _in_project(
        xs, pos, sc_a, sh_a, w_in_full, mla_q_norm_w, mla_kv_norm_w, w_q_full, g_kv, invf,
        _gather_over_ici([slot(w_mlp_in[0])]))
    (o_pre, o_hg, states), (s_out, s_w1) = _hgrn_forward(
        zhg, hg_lower_bounds, hg_norm_w, _merge(_gather_over_ici([slot(w_out[0])]), _gather_over_d2d([s_w1])))
    (o_mla, lse), (s_w2, s_out) = _attention_forward(
        q, k, v_t, _merge(_gather_over_ici([slot(w_mlp_out[0])]), _gather_over_d2d([s_out])))
    w_out_full = whole(s_out).reshape(D_MODEL, D_MODEL)
    (cat, mix, xhat1, rstd1), (s_w2,) = _out_project(o_hg, o_mla, xs, g_a, w_out_full, _gather_over_d2d([s_w2]))
    g_w1, g_w2 = whole(s_w1), whole(s_w2)
    vecs = jnp.concatenate([ln1_g, ln1_b, sc_m, sh_m, g_m, g_a, ln2_g, ln2_b], axis=0)
    act, dhp, um, dh, dmix, d_cat, dr1, mlp_sums, delta = _mlp_and_back(
        xhat1, rstd1, mix, target, o_mla, vecs, g_w1, g_w2, w_out_full)

    gw_1 = _weight_grad(um, dhp, "grad_w_mlp_in", N_CHIPS, D_FF // N_CHIPS, token_tile=4096)
    gw_2 = _weight_grad(act, dh, "grad_w_mlp_out", N_CHIPS, D_MODEL, a_blocked=True, b_blocked=False, token_tile=4096)
    gw_out = _weight_grad(cat, dmix, "grad_w_out", 1, D_MODEL, token_tile=2048)
    gw_out = gw_out.reshape(N_CHIPS, D_MODEL // N_CHIPS, D_MODEL)
    mlp_grads = [halved(gw_1), halved(gw_2), halved(gw_out)]
    (dq, dk, dv), landed = _attention_backward(q, k, k_t, v, d_cat, lse, delta, _pair_exchange(mlp_grads))
    chip_sums = [_add_pair(core_arr, g, l) for g, l in zip(mlp_grads, landed)]
    (d_hq, d_hf, d_hi, d_hg, hg_sums), landed = _hgrn_backward(
        zhg, hg_lower_bounds, hg_norm_w, o_pre, d_cat, states, _chip_exchange([b for _, b in chip_sums]))
    mlp_mine = [_add_chips(chip_arr, p, l) for (p, _), l in zip(chip_sums, landed)]
    (dz, dqf, dkvu, cqn, ckvn, grad_x, in_sums), _ = _in_project_backward(
        dq, dk, dv, cq, ckv, pos, invf, mla_q_norm_w, mla_kv_norm_w, w_q_full, g_kv,
        d_hq, d_hf, d_hi, d_hg, w_in_full, dr1, xs, sc_a)

    gw_in, mlp_theirs = _weight_grad(u_a, dz, "grad_w_in", 3, IN_COLS_PAD // 3, exchange=_pair_send(mlp_mine),
                                     token_tile=4096)
    gw_in = jnp.transpose(gw_in, (1, 0, 2)).reshape(D_MODEL, IN_COLS_PAD)[:, :IN_COLS]
    gw_in = jnp.transpose(gw_in.reshape(D_MODEL, N_CHIPS, IN_COLS // N_CHIPS), (1, 0, 2))
    gw_q = _weight_grad(cqn, dqf, "grad_w_q_up", N_HEADS, QK_DIM, token_tile=2048)[:, :, :HEAD_DIM + ROPE_DIM]
    gw_kv = _weight_grad(ckvn, dkvu, "grad_w_kv_up", N_HEADS, 2 * HEAD_DIM, token_tile=2048)
    mixer_mine, mixer_theirs = _reduce_in_vmem([halved(g) for g in (gw_in, gw_q, gw_kv)], "reduce_mixer_grads")
    reduced = ("w_in", "w_q_up", "w_kv_up", "w_mlp_in", "w_mlp_out", "w_out")
    halves_mine = dict(zip(reduced, list(mixer_mine) + mlp_mine))
    halves_theirs = dict(zip(reduced, list(mixer_theirs) + list(mlp_theirs)))

    zeros = lambda n: jnp.zeros((1, n), F32)
    small = jnp.concatenate([
        in_sums[1:2], in_sums[0:1], mlp_sums[S_DGA:S_DGA + 1],
        mlp_sums[S_DSHM:S_DSHM + 1], mlp_sums[S_DSCM:S_DSCM + 1], mlp_sums[S_DGM:S_DGM + 1],
        mlp_sums[S_DLN1G:S_DLN1G + 1], mlp_sums[S_DLN1B:S_DLN1B + 1],
        mlp_sums[S_DLN2G:S_DLN2G + 1], mlp_sums[S_DLN2B:S_DLN2B + 1],
        jnp.concatenate([hg_sums[0:1], hg_sums[1:2]], axis=1),
        jnp.concatenate([in_sums[2:3, :Q_RANK], in_sums[3:4, :KV_RANK], zeros(D_MODEL - Q_RANK - KV_RANK)], axis=1),
        mlp_sums[S_LOSS:S_LOSS + 1],
        jnp.zeros((SMALL_ROWS - 13, D_MODEL), F32)], axis=0)
    small_all = _allgather8(small, "gather_small")
    tot, g_lb = _reduce_small(small_all, hg_lower_bounds)
    loss = tot[12, 0]
    g_b_ada = tot[0:6].reshape(1, 6 * D_MODEL)
    g_ln1_g, g_ln1_b, g_ln2_g, g_ln2_b = tot[6:7], tot[7:8], tot[8:9], tot[9:10]
    g_hg_norm = tot[10:11, 512:1024]
    g_q_norm = tot[11:12, 0:Q_RANK]
    g_kv_norm = tot[11:12, Q_RANK:Q_RANK + KV_RANK]

    d_mod_all = small_all[:, 0:6, :].reshape(N_DEV, 6 * D_MODEL)
    d_mod_cols = lax.dynamic_slice(d_mod_all, (0, chip * ada_cols), (N_DEV, ada_cols))
    d_mod_cols = jnp.concatenate([d_mod_cols, jnp.zeros_like(d_mod_cols)], axis=0)
    g_w_ada = _weight_grad(cond16, d_mod_cols, "grad_w_ada", 1, ada_cols)[0]

    names = ["w_ada", "b_ada", "w_in", "hg_lower_bounds", "hg_norm_w", "mla_q_norm_w", "w_q_up", "mla_kv_norm_w",
             "w_kv_up", "w_out", "ln1_g", "ln1_b", "w_mlp_in", "w_mlp_out", "ln2_g", "ln2_b"]
    weights = [w_ada, b_ada, w_in, hg_lower_bounds, hg_norm_w, mla_q_norm_w, w_q_up, mla_kv_norm_w,
               w_kv_up, w_out, ln1_g, ln1_b, w_mlp_in, w_mlp_out, ln2_g, ln2_b]
    moms = [m_w_ada, m_b_ada, m_w_in, m_hg_lower_bounds, m_hg_norm_w, m_mla_q_norm_w, m_w_q_up, m_mla_kv_norm_w,
            m_w_kv_up, m_w_out, m_ln1_g, m_ln1_b, m_w_mlp_in, m_w_mlp_out, m_ln2_g, m_ln2_b]
    vels = [v_w_ada, v_b_ada, v_w_in, v_hg_lower_bounds, v_hg_norm_w, v_mla_q_norm_w, v_w_q_up, v_mla_kv_norm_w,
            v_w_kv_up, v_w_out, v_ln1_g, v_ln1_b, v_w_mlp_in, v_w_mlp_out, v_ln2_g, v_ln2_b]
    grads2d = [g_w_ada, g_b_ada, None, g_lb, g_hg_norm, g_q_norm, None, g_kv_norm,
               None, None, g_ln1_g, g_ln1_b, None, None, g_ln2_g, g_ln2_b]
    out_g, out_d, out_m, out_v = [], [], [], []
    for name, w, g, m, vv in zip(names, weights, grads2d, moms, vels):
        if g is None:
            shape2 = w.shape[1:]
            g, d, nm, nv = _adamw_halves(core_arr, w.reshape(shape2), halves_mine[name], halves_theirs[name],
                                         m.reshape(shape2), vv.reshape(shape2), "adamw_" + name)
        else:
            shape2 = g.shape
            d, nm, nv = _adamw(w.reshape(shape2), g, m.reshape(shape2), vv.reshape(shape2), "adamw_" + name)
        out_g.append(g.reshape(w.shape))
        out_d.append(d.reshape(w.shape))
        out_m.append(nm.reshape(w.shape))
        out_v.append(nv.reshape(w.shape))
    return (loss, grad_x[None], *out_g, *out_d, *out_m, *out_v)
```

```python
import functools

import jax
import jax.numpy as jnp
from jax import lax
from jax.experimental import pallas as pl
from jax.experimental.pallas import tpu as pltpu

F32 = jnp.float32
BF16 = jnp.bfloat16
MESH_IDS = pl.DeviceIdType.MESH

D_MODEL = 1024
N_HEADS = 4
HEAD_DIM = 128
ROPE_DIM = 64
HG_CHUNK = 64
HG_COLS = 2048
Q_RANK = 256
KV_RANK = 256
IN_COLS = 2624
IN_COLS_PAD = 2688
QK_DIM = 256
D_FF = 4096
N_CHIPS = 4
N_DEV = 8
ROPE_THETA = 10000.0
RMS_EPS = 1e-6
LN_EPS = 1e-5
DN_ALPHA = 2.0 ** 0.25
ATT_SCALE = (HEAD_DIM + ROPE_DIM) ** -0.5
NEG_BIG = -1e30
ADAM_LR = 0.001
ADAM_B1 = 0.9
ADAM_B2 = 0.999
ADAM_EPS = 1e-08
ADAM_WD = 0.01
ADAM_STEP = 10
SMALL_ROWS = 16
MIB = 1024 * 1024


def _dot(a, b):
    return jnp.dot(a, b, preferred_element_type=F32)


def _dot_nt(a, b):
    return lax.dot_general(a, b, (((1,), (1,)), ((), ())), preferred_element_type=F32)


def _dot_tn(a, b):
    return lax.dot_general(a, b, (((0,), (0,)), ((), ())), preferred_element_type=F32)


def _params(vmem_mib, semantics=None):
    return pltpu.CompilerParams(vmem_limit_bytes=vmem_mib * MIB, dimension_semantics=semantics)


def _sigmoid(v):
    return 1.0 / (1.0 + jnp.exp(-v))


def _colsum(v):
    return jnp.sum(v, axis=0, keepdims=True)


def _rowmean(v):
    return jnp.mean(v, axis=-1, keepdims=True)


def _rope_tables(pos, invf):
    ang = pos * invf
    lane = lax.broadcasted_iota(jnp.int32, ang.shape, 1)
    cos_t = jnp.where(lane < ROPE_DIM, jnp.cos(ang), 0.0)
    sin = jnp.sin(ang)
    sin_t = jnp.where(lane < ROPE_DIM // 2, -sin, jnp.where(lane < ROPE_DIM, sin, 0.0))
    return cos_t, sin_t


def _swap_halves(t):
    lane = lax.broadcasted_iota(jnp.int32, t.shape, 1)
    return jnp.where(lane < ROPE_DIM // 2, pltpu.roll(t, 128 - ROPE_DIM // 2, 1), pltpu.roll(t, ROPE_DIM // 2, 1))


def _rope(t, cos_t, sin_t):
    return t * cos_t + _swap_halves(t) * sin_t


def _unrope(g, cos_t, sin_t):
    return g * cos_t - _swap_halves(g) * sin_t


def _mesh_pos():
    return lax.axis_index("x"), lax.axis_index("y"), lax.axis_index("c")


def _other_chips(x, y):
    out = []
    for dx, dy in ((1, 0), (0, 1), (1, 1)):
        px = 1 - x if dx else x
        py = 1 - y if dy else y
        out.append(((px, py), 2 * px + py))
    return out


def _allgather8(a, name):
    rows, cols = a.shape

    def body(a_ref, out_ref, send_sems, recv_sems):
        x, y, c = _mesh_pos()
        me = 4 * x + 2 * y + c
        out_ref[me] = a_ref[...]
        peers = []
        for r in range(1, N_DEV):
            px = 1 - x if r & 4 else x
            py = 1 - y if r & 2 else y
            pc = 1 - c if r & 1 else c
            peers.append(((px, py, pc), 4 * px + 2 * py + pc))

        def copy(r, block, to):
            return pltpu.make_async_remote_copy(
                src_ref=a_ref, dst_ref=out_ref.at[block], send_sem=send_sems.at[r], recv_sem=recv_sems.at[r],
                device_id=to, device_id_type=MESH_IDS)

        sends = [copy(r, me, peer) for r, (peer, _) in enumerate(peers)]
        for cp in sends:
            cp.start()
        for r, (peer, idx) in enumerate(peers):
            copy(r, idx, peer).wait_recv()
        for cp in sends:
            cp.wait_send()

    return pl.pallas_call(
        body, name=name,
        out_shape=jax.ShapeDtypeStruct((N_DEV, rows, cols), a.dtype),
        in_specs=[pl.BlockSpec(memory_space=pltpu.VMEM)],
        out_specs=pl.BlockSpec(memory_space=pltpu.VMEM),
        scratch_shapes=[pltpu.SemaphoreType.DMA((N_DEV - 1,)), pltpu.SemaphoreType.DMA((N_DEV - 1,))],
    )(a)


class _Exchange:
    def __init__(self, inputs, out_shapes, aliases, sems, start, finish):
        self.inputs, self.out_shapes, self.aliases, self.sems = list(inputs), list(out_shapes), dict(aliases), list(sems)
        self.start, self.finish = start, finish


def _from_copies(inputs, out_shapes, aliases, sems, copies):
    def start(ins, outs, sem_refs):
        for send, _ in copies(ins, outs, sem_refs):
            send.start()

    def finish(ins, outs, sem_refs):
        for send, recv in copies(ins, outs, sem_refs):
            recv.wait_recv()
            send.wait_send()

    return _Exchange(inputs, out_shapes, aliases, sems, start, finish)


HBM_MIN_BYTES = 256 * 1024


def _in_hbm(a):
    if a.size * a.dtype.itemsize < HBM_MIN_BYTES:
        return a
    return pltpu.with_memory_space_constraint(a, pltpu.HBM)


def _out_hbm(s):
    if s.size * s.dtype.itemsize < HBM_MIN_BYTES:
        return s
    return pltpu.HBM(s.shape, s.dtype)


def _pcall(body, *, operands, out_shape, **kwargs):
    single = not isinstance(out_shape, (list, tuple))
    shapes = [_out_hbm(s) for s in ([out_shape] if single else out_shape)]
    return pl.pallas_call(body, out_shape=shapes[0] if single else shapes, **kwargs)(*[_in_hbm(a) for a in operands])


def _run_exchange(exchange, name):
    n_in, n_out = len(exchange.inputs), len(exchange.out_shapes)

    def body(*refs):
        ins, outs, sem_refs = refs[:n_in], refs[n_in:n_in + n_out], refs[n_in + n_out:]
        exchange.start(ins, outs, sem_refs)
        exchange.finish(ins, outs, sem_refs)

    any_spec = pl.BlockSpec(memory_space=pl.ANY)
    return pl.pallas_call(
        body, name=name, out_shape=[_out_hbm(s) for s in exchange.out_shapes],
        in_specs=[any_spec] * n_in, out_specs=[any_spec] * n_out,
        scratch_shapes=exchange.sems, input_output_aliases=exchange.aliases,
    )(*[_in_hbm(a) for a in exchange.inputs])


def _pallas(body, *, name, operands, in_specs, out_shape, out_specs, params, scratch_shapes=(), grid=(), prefetch=(),
            exchange=None, first=None, last=None):
    n_pre, n_in, n_out, n_scr = len(prefetch), len(in_specs), len(out_specs), len(scratch_shapes)
    ex_in = exchange.inputs if exchange else []
    ex_out = exchange.out_shapes if exchange else []
    ex_sems = exchange.sems if exchange else []

    def full_body(*refs):
        pre, rest = refs[:n_pre], refs[n_pre:]
        ins, rest = rest[:n_in], rest[n_in:]
        xin, rest = rest[:len(ex_in)], rest[len(ex_in):]
        outs, rest = rest[:n_out], rest[n_out:]
        xout, rest = rest[:len(ex_out)], rest[len(ex_out):]
        scr, sem_refs = rest[:n_scr], rest[n_scr:]
        if exchange:
            @pl.when(first(*pre))
            def _():
                exchange.start(xin, xout, sem_refs)

        body(*pre, *ins, *outs, *scr)
        if exchange:
            @pl.when(last(*pre))
            def _():
                exchange.finish(xin, xout, sem_refs)

    any_spec = pl.BlockSpec(memory_space=pl.ANY)
    aliases = {n_pre + n_in + i: n_out + o for i, o in exchange.aliases.items()} if exchange else {}
    operands = [_in_hbm(a) for a in operands]
    results = pl.pallas_call(
        full_body, name=name, out_shape=[_out_hbm(s) for s in list(out_shape) + ex_out],
        grid_spec=pltpu.PrefetchScalarGridSpec(
            num_scalar_prefetch=n_pre, grid=grid, in_specs=list(in_specs) + [any_spec] * len(ex_in),
            out_specs=list(out_specs) + [any_spec] * len(ex_out), scratch_shapes=list(scratch_shapes) + ex_sems),
        input_output_aliases=aliases, compiler_params=params,
    )(*prefetch, *operands, *[_in_hbm(a) for a in ex_in])
    return results[:n_out], results[n_out:]


def _remote(src, dst, sems, idx, to):
    send_sems, recv_sems = sems
    return pltpu.make_async_remote_copy(src_ref=src, dst_ref=dst, send_sem=send_sems.at[idx], recv_sem=recv_sems.at[idx],
                                        device_id=to, device_id_type=MESH_IDS)


def _sem_pairs(*shape):
    return [pltpu.SemaphoreType.DMA(shape), pltpu.SemaphoreType.DMA(shape)]


def _same_shapes(arrays):
    return [jax.ShapeDtypeStruct(a.shape, a.dtype) for a in arrays]


def _gather_over_ici(slots):
    n = len(slots)

    def copies(ins, outs, sems):
        x, y, c = _mesh_pos()
        k = 2 * x + y
        out = []
        for j, (chip, kj) in enumerate(_other_chips(x, y)):
            for i in range(n):
                to = (*chip, c)
                out.append((_remote(ins[i].at[k, c], outs[i].at[k, c], sems, (j, i), to),
                            _remote(ins[i].at[k, c], outs[i].at[kj, c], sems, (j, i), to)))
        return out

    return _from_copies(slots, _same_shapes(slots), {i: i for i in range(n)}, _sem_pairs(3, n), copies)


def _gather_over_d2d(slots):
    n = len(slots)

    def copies(ins, outs, sems):
        x, y, c = _mesh_pos()
        sibling = (x, y, 1 - c)
        out = []
        for j, (_, kj) in enumerate(_other_chips(x, y)):
            for i in range(n):
                out.append((_remote(ins[i].at[kj, c], outs[i].at[kj, c], sems, (j, i), sibling),
                            _remote(ins[i].at[kj, c], outs[i].at[kj, 1 - c], sems, (j, i), sibling)))
        return out

    return _from_copies(slots, _same_shapes(slots), {i: i for i in range(n)}, _sem_pairs(3, n), copies)


def _gather_all(slots8):
    def copies(ins, outs, sems):
        x, y, c = _mesh_pos()
        me = 4 * x + 2 * y + c
        out = []
        for r in range(1, N_DEV):
            px = 1 - x if r & 4 else x
            py = 1 - y if r & 2 else y
            pc = 1 - c if r & 1 else c
            to = (px, py, pc)
            out.append((_remote(ins[0].at[me], outs[0].at[me], sems, r - 1, to),
                        _remote(ins[0].at[me], outs[0].at[4 * px + 2 * py + pc], sems, r - 1, to)))
        return out

    return _from_copies([slots8], _same_shapes([slots8]), {0: 0}, _sem_pairs(N_DEV - 1), copies)


def _merge(first, second):
    n_in, n_out, n_sem = len(first.inputs), len(first.out_shapes), len(first.sems)

    def start(ins, outs, sems):
        first.start(ins[:n_in], outs[:n_out], sems[:n_sem])
        second.start(ins[n_in:], outs[n_out:], sems[n_sem:])

    def finish(ins, outs, sems):
        first.finish(ins[:n_in], outs[:n_out], sems[:n_sem])
        second.finish(ins[n_in:], outs[n_out:], sems[n_sem:])

    aliases = dict(first.aliases)
    aliases.update({n_in + i: n_out + o for i, o in second.aliases.items()})
    return _Exchange(first.inputs + second.inputs, first.out_shapes + second.out_shapes, aliases,
                     first.sems + second.sems, start, finish)


def _pair_exchange(grads):
    n = len(grads)

    def copies(ins, outs, sems):
        x, y, c = _mesh_pos()
        cps = [_remote(ins[i].at[:, 1 - c], outs[i], sems, i, (x, y, 1 - c)) for i in range(n)]
        return [(cp, cp) for cp in cps]

    shapes = [jax.ShapeDtypeStruct((N_CHIPS,) + g.shape[2:], g.dtype) for g in grads]
    return _from_copies(grads, shapes, {}, _sem_pairs(n), copies)


def _chip_exchange(partials):
    n = len(partials)

    def copies(ins, outs, sems):
        x, y, c = _mesh_pos()
        cps = [_remote(ins[i].at[kj], outs[i].at[j], sems, (j, i), (*chip, c))
               for j, (chip, kj) in enumerate(_other_chips(x, y)) for i in range(n)]
        return [(cp, cp) for cp in cps]

    shapes = [jax.ShapeDtypeStruct((3,) + p.shape[1:], p.dtype) for p in partials]
    return _from_copies(partials, shapes, {}, _sem_pairs(3, n), copies)


def _pair_send(halves):
    n = len(halves)

    def copies(ins, outs, sems):
        x, y, c = _mesh_pos()
        cps = [_remote(ins[i], outs[i], sems, i, (x, y, 1 - c)) for i in range(n)]
        return [(cp, cp) for cp in cps]

    return _from_copies(halves, _same_shapes(halves), {}, _sem_pairs(n), copies)


def _reduce_in_vmem(grads, name):
    n = len(grads)

    def body(*refs):
        g, mine, theirs = refs[:n], refs[n:2 * n], refs[2 * n:3 * n]
        landed_pair, partial, landed_chips = refs[3 * n:4 * n], refs[4 * n:5 * n], refs[5 * n:6 * n]
        sems = refs[6 * n:]
        x, y, c = _mesh_pos()
        k = 2 * x + y
        sibling = (x, y, 1 - c)

        def run(copies):
            for cp in copies:
                cp.start()
            for cp in copies:
                cp.wait_recv()
                cp.wait_send()

        run([_remote(g[i].at[:, 1 - c], landed_pair[i], sems[0:2], i, sibling) for i in range(n)])
        for i in range(n):
            for kk in range(N_CHIPS):
                partial[i][kk] = (g[i][kk, c] + landed_pair[i][kk]).astype(BF16)
        run([_remote(partial[i].at[kj], landed_chips[i].at[j], sems[2:4], (j, i), (*chip, c))
             for j, (chip, kj) in enumerate(_other_chips(x, y)) for i in range(n)])
        for i in range(n):
            own = g[i][k, c] + landed_pair[i][k]
            mine[i][...] = ((own + landed_chips[i][0].astype(F32)) + landed_chips[i][1].astype(F32)) \
                + landed_chips[i][2].astype(F32)
        run([_remote(mine[i], theirs[i], sems[4:6], i, sibling) for i in range(n)])

    halves = [jax.ShapeDtypeStruct(gr.shape[2:], F32) for gr in grads]
    vmem = pl.BlockSpec(memory_space=pltpu.VMEM)
    scratch = ([pltpu.VMEM((N_CHIPS,) + gr.shape[2:], F32) for gr in grads]
               + [pltpu.VMEM((N_CHIPS,) + gr.shape[2:], BF16) for gr in grads]
               + [pltpu.VMEM((3,) + gr.shape[2:], BF16) for gr in grads]
               + _sem_pairs(n) + _sem_pairs(3, n) + _sem_pairs(n))
    out = pl.pallas_call(
        body, name=name, out_shape=halves + halves, in_specs=[vmem] * n, out_specs=[vmem] * (2 * n),
        scratch_shapes=scratch, compiler_params=_params(48),
    )(*grads)
    return out[:n], out[n:]


def _row_tile(rows):
    for t in (256, 128, 64):
        if rows % t == 0:
            return t
    return rows


def _add_pair(core, grad, landed):
    _, h, cols = landed.shape
    tr = _row_tile(h)

    def body(core_ref, g_ref, l_ref, o_ref, ob_ref):
        s = g_ref[...] + l_ref[...]
        o_ref[...] = s
        ob_ref[...] = s.astype(BF16)

    out_spec = pl.BlockSpec((None, tr, cols), lambda k, t, core_ref: (k, t, 0))
    return _pcall(
        body, name="grad_add_pair",
        out_shape=[jax.ShapeDtypeStruct(landed.shape, F32), jax.ShapeDtypeStruct(landed.shape, BF16)],
        grid_spec=pltpu.PrefetchScalarGridSpec(
            num_scalar_prefetch=1, grid=(N_CHIPS, h // tr),
            in_specs=[pl.BlockSpec((None, None, tr, cols), lambda k, t, core_ref: (k, core_ref[0], t, 0)),
                      pl.BlockSpec((None, tr, cols), lambda k, t, core_ref: (k, t, 0))],
            out_specs=[out_spec, out_spec]),
        compiler_params=_params(32, ("arbitrary", "arbitrary")),
        operands=(core, grad, landed))


def _add_chips(chip, partial, landed):
    _, h, cols = partial.shape
    tr = _row_tile(h)

    def body(chip_ref, p_ref, l_ref, o_ref):
        o_ref[...] = ((p_ref[...] + l_ref[0].astype(F32)) + l_ref[1].astype(F32)) + l_ref[2].astype(F32)

    return _pcall(
        body, name="grad_add_chips",
        out_shape=jax.ShapeDtypeStruct((h, cols), F32),
        grid_spec=pltpu.PrefetchScalarGridSpec(
            num_scalar_prefetch=1, grid=(h // tr,),
            in_specs=[pl.BlockSpec((None, tr, cols), lambda t, chip_ref: (chip_ref[0], t, 0)),
                      pl.BlockSpec((3, tr, cols), lambda t, chip_ref: (0, t, 0))],
            out_specs=pl.BlockSpec((tr, cols), lambda t, chip_ref: (t, 0))),
        compiler_params=_params(32, ("arbitrary",)),
        operands=(chip, partial, landed))


def _ada_project(c_all, w_ada, b_shard):
    n = w_ada.shape[1]
    tn = 512

    def body(c_ref, w_ref, b_ref, mod_ref, cond_ref):
        cv = c_ref[...]
        cond = cv * _sigmoid(cv)
        mod_ref[...] = _dot(cond.astype(BF16), w_ref[...].astype(BF16)) + b_ref[...]
        cond_ref[0:N_DEV, :] = cond
        cond_ref[N_DEV:2 * N_DEV, :] = jnp.zeros_like(cond)

    return _pcall(
        body, name="ada_project", grid=(n // tn,),
        out_shape=[jax.ShapeDtypeStruct((N_DEV, n), F32), jax.ShapeDtypeStruct((2 * N_DEV, D_MODEL), F32)],
        in_specs=[pl.BlockSpec((N_DEV, D_MODEL), lambda j: (0, 0)), pl.BlockSpec((D_MODEL, tn), lambda j: (0, j)),
                  pl.BlockSpec((1, tn), lambda j: (0, j))],
        out_specs=[pl.BlockSpec((N_DEV, tn), lambda j: (0, j)), pl.BlockSpec((2 * N_DEV, D_MODEL), lambda j: (0, 0))],
        compiler_params=_params(32, ("arbitrary",)),
        operands=(c_all, w_ada, b_shard))


def _in_project(x, pos, sc_a, sh_a, w_in, q_norm_w, kv_norm_w, w_q, w_kv, invf, exchange=None):
    t_len = x.shape[0]
    tm = min(512, t_len)

    def body(x_ref, pos_ref, sc_ref, sh_ref, win_ref, qn_ref, kvn_ref, wq_ref, wkv_ref, invf_ref,
             u_ref, zhg_ref, cq_ref, ckv_ref, q_ref, k_ref, kt_ref, v_ref, vt_ref):
        u = (x_ref[...] * (1.0 + sc_ref[...]) + sh_ref[...]).astype(BF16)
        u_ref[...] = u
        z = _dot_nt(u, win_ref[...])
        zhg_ref[...] = z[:, :HG_COLS]
        cq = z[:, HG_COLS:HG_COLS + Q_RANK]
        ckv = z[:, HG_COLS + Q_RANK:HG_COLS + Q_RANK + KV_RANK]
        cq_ref[...] = cq
        ckv_ref[...] = ckv
        cos_t, sin_t = _rope_tables(pos_ref[...], invf_ref[...])
        k_pe = _rope(z[:, HG_COLS + Q_RANK + KV_RANK:], cos_t, sin_t)
        k_pe_t = jnp.transpose(k_pe).astype(BF16)
        cqn = (cq * lax.rsqrt(_rowmean(cq * cq) + RMS_EPS) * qn_ref[...]).astype(BF16)
        ckvn = (ckv * lax.rsqrt(_rowmean(ckv * ckv) + RMS_EPS) * kvn_ref[...]).astype(BF16)
        for h in range(N_HEADS):
            qh = _dot(cqn, wq_ref[h])
            q_ref[h, :, 0:HEAD_DIM] = qh[:, :HEAD_DIM].astype(BF16)
            q_ref[h, :, HEAD_DIM:QK_DIM] = _rope(qh[:, HEAD_DIM:], cos_t, sin_t).astype(BF16)
            kvh = _dot(ckvn, wkv_ref[h])
            k_ref[h, :, 0:HEAD_DIM] = kvh[:, :HEAD_DIM].astype(BF16)
            k_ref[h, :, HEAD_DIM:QK_DIM] = k_pe.astype(BF16)
            kt_ref[h, 0:HEAD_DIM, :] = jnp.transpose(kvh[:, :HEAD_DIM]).astype(BF16)
            kt_ref[h, HEAD_DIM:QK_DIM, :] = k_pe_t
            v_ref[h] = kvh[:, HEAD_DIM:].astype(BF16)
            vt_ref[h] = jnp.transpose(kvh[:, HEAD_DIM:]).astype(BF16)

    row = lambda i: (i, 0)
    fixed2 = lambda i: (0, 0)
    fixed3 = lambda i: (0, 0, 0)
    heads = lambda i: (0, i, 0)
    n_tiles = t_len // tm
    return _pallas(
        body, name="in_project", grid=(n_tiles,),
        operands=(x, pos, sc_a, sh_a, w_in, q_norm_w, kv_norm_w, w_q, w_kv, invf),
        out_shape=[jax.ShapeDtypeStruct((t_len, D_MODEL), BF16), jax.ShapeDtypeStruct((t_len, HG_COLS), F32),
                   jax.ShapeDtypeStruct((t_len, Q_RANK), F32), jax.ShapeDtypeStruct((t_len, KV_RANK), F32),
                   jax.ShapeDtypeStruct((N_HEADS, t_len, QK_DIM), BF16),
                   jax.ShapeDtypeStruct((N_HEADS, t_len, QK_DIM), BF16),
                   jax.ShapeDtypeStruct((N_HEADS, QK_DIM, t_len), BF16),
                   jax.ShapeDtypeStruct((N_HEADS, t_len, HEAD_DIM), BF16),
                   jax.ShapeDtypeStruct((N_HEADS, HEAD_DIM, t_len), BF16)],
        in_specs=[pl.BlockSpec((tm, D_MODEL), row), pl.BlockSpec((tm, 1), row),
                  pl.BlockSpec((1, D_MODEL), fixed2), pl.BlockSpec((1, D_MODEL), fixed2),
                  pl.BlockSpec((IN_COLS_PAD, D_MODEL), fixed2),
                  pl.BlockSpec((1, Q_RANK), fixed2), pl.BlockSpec((1, KV_RANK), fixed2),
                  pl.BlockSpec((N_HEADS, Q_RANK, QK_DIM), fixed3), pl.BlockSpec((N_HEADS, KV_RANK, 2 * HEAD_DIM), fixed3),
                  pl.BlockSpec((1, 128), fixed2)],
        out_specs=[pl.BlockSpec((tm, D_MODEL), row), pl.BlockSpec((tm, HG_COLS), row),
                   pl.BlockSpec((tm, Q_RANK), row), pl.BlockSpec((tm, KV_RANK), row),
                   pl.BlockSpec((N_HEADS, tm, QK_DIM), heads), pl.BlockSpec((N_HEADS, tm, QK_DIM), heads),
                   pl.BlockSpec((N_HEADS, QK_DIM, tm), lambda i: (0, 0, i)),
                   pl.BlockSpec((N_HEADS, tm, HEAD_DIM), heads),
                   pl.BlockSpec((N_HEADS, HEAD_DIM, tm), lambda i: (0, 0, i))],
        params=_params(48, ("arbitrary",)), exchange=exchange,
        first=lambda: pl.program_id(0) == 0, last=lambda: pl.program_id(0) == n_tiles - 1)


def _lower_bound(lb_raw):
    m = jnp.max(lb_raw, axis=0, keepdims=True)
    e = jnp.exp(lb_raw - m)
    return e[0:1] / jnp.sum(e, axis=0, keepdims=True)


def _tri(inclusive_lower):
    r = lax.broadcasted_iota(jnp.int32, (HG_CHUNK, HG_CHUNK), 0)
    c = lax.broadcasted_iota(jnp.int32, (HG_CHUNK, HG_CHUNK), 1)
    return (c <= r) if inclusive_lower else (c >= r)


def _chunk_rows(n):
    return slice(n * HG_CHUNK, (n + 1) * HG_CHUNK)


def _chunk_prefix_sums(v, inclusive_lower):
    tri = _tri(inclusive_lower).astype(BF16)
    hi = v.astype(BF16)
    rest = v - hi.astype(F32)
    mid = rest.astype(BF16)
    lo = (rest - mid.astype(F32)).astype(BF16)
    pieces = jnp.concatenate([hi, mid, lo], axis=1)
    out = []
    for n in range(v.shape[0] // HG_CHUNK):
        s = _dot(tri, pieces[_chunk_rows(n)])
        out.append((s[:, 0:HEAD_DIM] + s[:, HEAD_DIM:2 * HEAD_DIM]) + s[:, 2 * HEAD_DIM:])
    return jnp.concatenate(out, axis=0)


def _per_chunk(v, row):
    n = v.shape[0] // HG_CHUNK
    v3 = v.reshape(n, HG_CHUNK, HEAD_DIM)
    return jnp.broadcast_to(v3[:, row:row + 1, :], v3.shape).reshape(v.shape)


def _hg_block(q, f_logit, lb):
    sg = _sigmoid(f_logit)
    forget = lb + (1.0 - lb) * sg
    kk = 1.0 - forget
    b = _chunk_prefix_sums(jnp.log(forget), True)
    b_ref = _per_chunk(b, HG_CHUNK // 2 - 1)
    b_last = _per_chunk(b, HG_CHUNK - 1)
    e_i = jnp.exp(b - b_ref)
    e_ri = jnp.exp(b_ref - b)
    e_b = jnp.exp(b)
    e_l = jnp.exp(b_last - b)
    return dict(sg=sg, forget=forget, e_i=e_i, e_ri=e_ri, e_b=e_b, e_l=e_l, dec=jnp.exp(b_last),
                qi=q * e_i, ki=kk * e_ri, qe=q * e_b, kl=kk * e_l)


def _hgrn_forward(zhg, lb_raw, norm_w, exchange=None):
    t_len = zhg.shape[0]
    tb = min(512, t_len)
    n_chunks = tb // HG_CHUNK

    def body(q_ref, f_ref, v_ref, g_ref, lb_ref, w_ref, opre_ref, o_ref, st_ref, state):
        @pl.when(pl.program_id(1) == 0)
        def _():
            state[...] = jnp.zeros_like(state)

        blk = _hg_block(q_ref[...], f_ref[...], _lower_bound(lb_ref[...]))
        v = v_ref[...].astype(BF16)
        qi, ki, qe, kl = (blk[name].astype(BF16) for name in ("qi", "ki", "qe", "kl"))
        causal = _tri(True)
        st = state[...]
        parts = []
        for n in range(n_chunks):
            r = _chunk_rows(n)
            a = jnp.where(causal, _dot_nt(qi[r], ki[r]), 0.0).astype(BF16)
            st_ref[0, n] = st
            parts.append(_dot(a, v[r]) + _dot_nt(qe[r], st.astype(BF16)))
            st = st * blk["dec"][n * HG_CHUNK:n * HG_CHUNK + 1] + _dot_tn(v[r], kl[r])
        state[...] = st
        o = jnp.concatenate(parts, axis=0)
        opre_ref[...] = o
        g = g_ref[...]
        o_ref[...] = o * lax.rsqrt(_rowmean(o * o) + RMS_EPS) * w_ref[...] * (g * _sigmoid(g))

    col = lambda off: (lambda h, t: (t, off + h))
    nb = t_len // tb
    return _pallas(
        body, name="hgrn_forward", grid=(N_HEADS, nb), operands=(zhg, zhg, zhg, zhg, lb_raw, norm_w),
        out_shape=[jax.ShapeDtypeStruct((t_len, N_HEADS * HEAD_DIM), F32),
                   jax.ShapeDtypeStruct((t_len, N_HEADS * HEAD_DIM), F32),
                   jax.ShapeDtypeStruct((N_HEADS, t_len // HG_CHUNK, HEAD_DIM, HEAD_DIM), F32)],
        in_specs=[pl.BlockSpec((tb, HEAD_DIM), col(0)), pl.BlockSpec((tb, HEAD_DIM), col(N_HEADS)),
                  pl.BlockSpec((tb, HEAD_DIM), col(2 * N_HEADS)), pl.BlockSpec((tb, HEAD_DIM), col(3 * N_HEADS)),
                  pl.BlockSpec((2, HEAD_DIM), lambda h, t: (0, h)), pl.BlockSpec((1, HEAD_DIM), lambda h, t: (0, h))],
        out_specs=[pl.BlockSpec((tb, HEAD_DIM), col(0)), pl.BlockSpec((tb, HEAD_DIM), col(0)),
                   pl.BlockSpec((1, n_chunks, HEAD_DIM, HEAD_DIM), lambda h, t: (h, t, 0, 0))],
        scratch_shapes=[pltpu.VMEM((HEAD_DIM, HEAD_DIM), F32)],
        params=_params(32, ("arbitrary", "arbitrary")), exchange=exchange,
        first=lambda: (pl.program_id(0) == 0) & (pl.program_id(1) == 0),
        last=lambda: (pl.program_id(0) == N_HEADS - 1) & (pl.program_id(1) == nb - 1))


def _hgrn_backward(zhg, lb_raw, norm_w, o_pre, d_cat, states, exchange=None):
    t_len = zhg.shape[0]
    tb = min(512, t_len)
    n_chunks = tb // HG_CHUNK
    nb = t_len // tb

    def body(q_ref, f_ref, v_ref, g_ref, lb_ref, w_ref, opre_ref, do_ref, st_ref,
             dq_ref, df_ref, dv_ref, dg_ref, sums_ref, gstate):
        @pl.when(pl.program_id(1) == 0)
        def _():
            gstate[...] = jnp.zeros_like(gstate)
            sums_ref[...] = jnp.zeros_like(sums_ref)

        lb = _lower_bound(lb_ref[...])
        w = w_ref[...]
        o = opre_ref[...]
        g = g_ref[...]
        d_out = do_ref[...]
        r = lax.rsqrt(_rowmean(o * o) + RMS_EPS)
        sg_g = _sigmoid(g)
        dg_ref[...] = (d_out * (o * r * w) * (sg_g * (1.0 + g * (1.0 - sg_g)))).astype(BF16)
        d_on = d_out * (g * sg_g)
        sums_ref[1:2, :] += _colsum(d_on * o * r)
        dy = d_on * w
        d_o = (r * dy - o * (r * r * r) * _rowmean(dy * o)).astype(BF16)
        blk = _hg_block(q_ref[...], f_ref[...], lb)
        v = v_ref[...].astype(BF16)
        qi, ki, qe, kl = (blk[name].astype(BF16) for name in ("qi", "ki", "qe", "kl"))
        causal = _tri(True)
        row_id = lax.broadcasted_iota(jnp.int32, (HG_CHUNK, HEAD_DIM), 0)
        gt = gstate[...]
        d_v, d_qi, d_ki, d_qe, d_kl, d_dec = ([None] * n_chunks for _ in range(6))
        for n in reversed(range(n_chunks)):
            rows = _chunk_rows(n)
            st = st_ref[0, n]
            a = jnp.where(causal, _dot_nt(qi[rows], ki[rows]), 0.0).astype(BF16)
            d_a = jnp.where(causal, _dot_nt(d_o[rows], v[rows]), 0.0).astype(BF16)
            gt_b = gt.astype(BF16)
            d_v[n] = _dot_tn(a, d_o[rows]) + _dot_nt(kl[rows], gt_b)
            d_qi[n] = _dot(d_a, ki[rows])
            d_ki[n] = _dot_tn(d_a, qi[rows])
            d_qe[n] = _dot(d_o[rows], st.astype(BF16))
            d_kl[n] = _dot(v[rows], gt_b)
            d_dec[n] = jnp.where(row_id == HG_CHUNK - 1, _colsum(gt * st), 0.0)
            gt = gt * blk["dec"][n * HG_CHUNK:n * HG_CHUNK + 1] + _dot_tn(d_o[rows], qe[rows])
        gstate[...] = gt
        d_qi, d_ki, d_qe, d_kl, d_dec = (jnp.concatenate(p, axis=0) for p in (d_qi, d_ki, d_qe, d_kl, d_dec))
        dv_ref[...] = jnp.concatenate(d_v, axis=0).astype(BF16)
        dq_ref[...] = (d_qi * blk["e_i"] + d_qe * blk["e_b"]).astype(BF16)
        d_k = d_ki * blk["e_ri"] + d_kl * blk["e_l"]
        t_qi = d_qi * blk["qi"]
        t_ki = d_ki * blk["ki"]
        t_kl = d_kl * blk["kl"]
        at_ref, at_last = [], []
        for n in range(n_chunks):
            rows = _chunk_rows(n)
            at_ref.append(jnp.where(row_id == HG_CHUNK // 2 - 1, _colsum(t_ki[rows] - t_qi[rows]), 0.0))
            at_last.append(jnp.where(row_id == HG_CHUNK - 1, _colsum(t_kl[rows]), 0.0))
        d_b = (t_qi - t_ki + d_qe * blk["qe"] - t_kl + jnp.concatenate(at_ref, axis=0)
               + jnp.concatenate(at_last, axis=0) + d_dec * blk["dec"])
        d_forget = _chunk_prefix_sums(d_b, False) / blk["forget"] - d_k
        sg = blk["sg"]
        df_ref[...] = (d_forget * (1.0 - lb) * sg * (1.0 - sg)).astype(BF16)
        sums_ref[0:1, :] += _colsum(d_forget * (1.0 - sg))

    col = lambda off: (lambda h, t: (nb - 1 - t, off + h))
    return _pallas(
        body, name="hgrn_backward", grid=(N_HEADS, nb),
        operands=(zhg, zhg, zhg, zhg, lb_raw, norm_w, o_pre, d_cat, states),
        out_shape=[jax.ShapeDtypeStruct((t_len, N_HEADS * HEAD_DIM), BF16)] * 4
        + [jax.ShapeDtypeStruct((8, N_HEADS * HEAD_DIM), F32)],
        in_specs=[pl.BlockSpec((tb, HEAD_DIM), col(0)), pl.BlockSpec((tb, HEAD_DIM), col(N_HEADS)),
                  pl.BlockSpec((tb, HEAD_DIM), col(2 * N_HEADS)), pl.BlockSpec((tb, HEAD_DIM), col(3 * N_HEADS)),
                  pl.BlockSpec((2, HEAD_DIM), lambda h, t: (0, h)), pl.BlockSpec((1, HEAD_DIM), lambda h, t: (0, h)),
                  pl.BlockSpec((tb, HEAD_DIM), col(0)), pl.BlockSpec((tb, HEAD_DIM), col(0)),
                  pl.BlockSpec((1, n_chunks, HEAD_DIM, HEAD_DIM), lambda h, t: (h, nb - 1 - t, 0, 0))],
        out_specs=[pl.BlockSpec((tb, HEAD_DIM), col(0))] * 4 + [pl.BlockSpec((8, HEAD_DIM), lambda h, t: (0, h))],
        scratch_shapes=[pltpu.VMEM((HEAD_DIM, HEAD_DIM), F32)],
        params=_params(32, ("arbitrary", "arbitrary")), exchange=exchange,
        first=lambda: (pl.program_id(0) == 0) & (pl.program_id(1) == 0),
        last=lambda: (pl.program_id(0) == N_HEADS - 1) & (pl.program_id(1) == nb - 1))


ATT_LOG2 = ATT_SCALE * 1.4426950408889634


def _triangle_steps(nq, q_major):
    if q_major:
        pairs = [(i, j) for i in range(nq) for j in range(i + 1)]
    else:
        pairs = [(i, j) for j in range(nq) for i in range(j, nq)]
    return jnp.array([p[0] for p in pairs], jnp.int32), jnp.array([p[1] for p in pairs], jnp.int32)


def _key_le_query(t):
    return lax.broadcasted_iota(jnp.int32, (t, t), 0) <= lax.broadcasted_iota(jnp.int32, (t, t), 1)


def _attention_forward(q, k, v_t, exchange=None):
    t_len = q.shape[1]
    tq = min(512, t_len)
    nq = t_len // tq
    qi_tab, ki_tab = _triangle_steps(nq, True)

    def body(qi_ref, ki_ref, q_ref, k_ref, vt_ref, o_ref, lse_ref, m_s, l_s, acc_s):
        step = pl.program_id(0)
        qi, ki = qi_ref[step], ki_ref[step]

        @pl.when(ki == 0)
        def _():
            m_s[...] = jnp.full_like(m_s, NEG_BIG)
            l_s[...] = jnp.zeros_like(l_s)
            acc_s[...] = jnp.zeros_like(acc_s)

        def accumulate(masked):
            for h in range(N_HEADS):
                s_t = _dot_nt(k_ref[h], q_ref[h]) * ATT_LOG2
                if masked:
                    s_t = jnp.where(_key_le_query(tq), s_t, NEG_BIG)
                m_old = m_s[h]
                m_new = jnp.maximum(m_old, jnp.max(s_t, axis=0, keepdims=True))
                alpha = jnp.exp2(m_old - m_new)
                p_t = jnp.exp2(s_t - m_new)
                l_s[h] = alpha * l_s[h] + jnp.sum(p_t, axis=0, keepdims=True)
                acc_s[h] = alpha * acc_s[h] + _dot(vt_ref[h], p_t.astype(BF16))
                m_s[h] = m_new

        @pl.when(ki < qi)
        def _():
            accumulate(False)

        @pl.when(ki == qi)
        def _():
            accumulate(True)
            for h in range(N_HEADS):
                o_ref[:, h * HEAD_DIM:(h + 1) * HEAD_DIM] = jnp.transpose(acc_s[h] / l_s[h])
                lse_ref[h] = m_s[h] + jnp.log2(l_s[h])

    n_steps = qi_tab.shape[0]
    return _pallas(
        body, name="attention_forward", grid=(n_steps,), prefetch=(qi_tab, ki_tab), operands=(q, k, v_t),
        out_shape=[jax.ShapeDtypeStruct((t_len, N_HEADS * HEAD_DIM), F32),
                   jax.ShapeDtypeStruct((N_HEADS, 1, t_len), F32)],
        in_specs=[pl.BlockSpec((N_HEADS, tq, QK_DIM), lambda s, qt, kt: (0, qt[s], 0)),
                  pl.BlockSpec((N_HEADS, tq, QK_DIM), lambda s, qt, kt: (0, kt[s], 0)),
                  pl.BlockSpec((N_HEADS, HEAD_DIM, tq), lambda s, qt, kt: (0, 0, kt[s]))],
        out_specs=[pl.BlockSpec((tq, N_HEADS * HEAD_DIM), lambda s, qt, kt: (qt[s], 0)),
                   pl.BlockSpec((N_HEADS, 1, tq), lambda s, qt, kt: (0, 0, qt[s]))],
        scratch_shapes=[pltpu.VMEM((N_HEADS, 1, tq), F32), pltpu.VMEM((N_HEADS, 1, tq), F32),
                        pltpu.VMEM((N_HEADS, HEAD_DIM, tq), F32)],
        params=_params(48, ("arbitrary",)), exchange=exchange,
        first=lambda qt, kt: pl.program_id(0) == 0, last=lambda qt, kt: pl.program_id(0) == n_steps - 1)


BWD_HEADS = 2


def _attention_backward(q, k, k_t, v, d_cat, lse, delta, exchange=None):
    t_len = q.shape[1]
    tq = min(512, t_len)
    nq = t_len // tq
    hp = BWD_HEADS
    qi_tab, ki_tab = _triangle_steps(nq, False)

    def body(qi_ref, ki_ref, q_ref, k_ref, kt_ref, v_ref, do_ref, lse_ref, delta_ref, dqt_hbm, dk_ref, dv_ref,
             dqt_s, dk_s, dv_s):
        group, step = pl.program_id(0), pl.program_id(1)
        qi, ki = qi_ref[step], ki_ref[step]

        @pl.when(step == 0)
        def _():
            dqt_s[...] = jnp.zeros_like(dqt_s)

        @pl.when(qi == ki)
        def _():
            dk_s[...] = jnp.zeros_like(dk_s)
            dv_s[...] = jnp.zeros_like(dv_s)

        def accumulate(masked):
            for h in range(hp):
                do_b = do_ref[:, h * HEAD_DIM:(h + 1) * HEAD_DIM].astype(BF16)
                s_t = _dot_nt(k_ref[h], q_ref[h]) * ATT_LOG2
                if masked:
                    s_t = jnp.where(_key_le_query(tq), s_t, NEG_BIG)
                p_t = jnp.exp2(s_t - lse_ref[h])
                dp_t = _dot_nt(v_ref[h], do_b)
                ds_t = (p_t * (dp_t - delta_ref[h]) * ATT_SCALE).astype(BF16)
                dv_s[h] += _dot(p_t.astype(BF16), do_b)
                dk_s[h] += _dot(ds_t, q_ref[h])
                dqt_s[h, qi] += _dot(kt_ref[h], ds_t)

        @pl.when(ki < qi)
        def _():
            accumulate(False)

        @pl.when(ki == qi)
        def _():
            accumulate(True)
            for h in range(hp):
                pltpu.sync_copy(dqt_s.at[h, qi], dqt_hbm.at[group * hp + h, qi])

        @pl.when(qi == nq - 1)
        def _():
            dk_ref[...] = dk_s[...]
            dv_ref[...] = dv_s[...]

    wide = hp * HEAD_DIM
    n_groups, n_steps = N_HEADS // hp, qi_tab.shape[0]
    return _pallas(
        body, name="attention_backward", grid=(n_groups, n_steps), prefetch=(qi_tab, ki_tab),
        operands=(q, k, k_t, v, d_cat, lse, delta),
        out_shape=[jax.ShapeDtypeStruct((N_HEADS, nq, QK_DIM, tq), F32),
                   jax.ShapeDtypeStruct((N_HEADS, t_len, QK_DIM), F32),
                   jax.ShapeDtypeStruct((N_HEADS, t_len, HEAD_DIM), F32)],
        in_specs=[pl.BlockSpec((hp, tq, QK_DIM), lambda g, s, qt, kt: (g, qt[s], 0)),
                  pl.BlockSpec((hp, tq, QK_DIM), lambda g, s, qt, kt: (g, kt[s], 0)),
                  pl.BlockSpec((hp, QK_DIM, tq), lambda g, s, qt, kt: (g, 0, kt[s])),
                  pl.BlockSpec((hp, tq, HEAD_DIM), lambda g, s, qt, kt: (g, kt[s], 0)),
                  pl.BlockSpec((tq, wide), lambda g, s, qt, kt: (qt[s], n_groups + g)),
                  pl.BlockSpec((hp, 1, tq), lambda g, s, qt, kt: (g, 0, qt[s])),
                  pl.BlockSpec((hp, 1, tq), lambda g, s, qt, kt: (g, 0, qt[s]))],
        out_specs=[pl.BlockSpec(memory_space=pl.ANY),
                   pl.BlockSpec((hp, tq, QK_DIM), lambda g, s, qt, kt: (g, kt[s], 0)),
                   pl.BlockSpec((hp, tq, HEAD_DIM), lambda g, s, qt, kt: (g, kt[s], 0))],
        scratch_shapes=[pltpu.VMEM((hp, nq, QK_DIM, tq), F32), pltpu.VMEM((hp, tq, QK_DIM), F32),
                        pltpu.VMEM((hp, tq, HEAD_DIM), F32)],
        params=_params(48, ("arbitrary", "arbitrary")), exchange=exchange,
        first=lambda qt, kt: (pl.program_id(0) == 0) & (pl.program_id(1) == 0),
        last=lambda qt, kt: (pl.program_id(0) == n_groups - 1) & (pl.program_id(1) == n_steps - 1))


def _out_project(o_hg, o_mla, x, g_a, w_out, exchange=None):
    t_len = x.shape[0]
    tm = min(512, t_len)
    half = N_HEADS * HEAD_DIM

    def body(ohg_ref, omla_ref, x_ref, ga_ref, w_ref, cat_ref, mix_ref, xhat_ref, rstd_ref):
        a = ohg_ref[...].astype(BF16)
        b = omla_ref[...].astype(BF16)
        cat_ref[:, 0:half] = a
        cat_ref[:, half:2 * half] = b
        mix = _dot(a, w_ref[0:half, :]) + _dot(b, w_ref[half:2 * half, :])
        mix_ref[...] = mix
        r1 = DN_ALPHA * x_ref[...] + (1.0 + ga_ref[...]) * mix
        xc = r1 - _rowmean(r1)
        rstd = lax.rsqrt(_rowmean(xc * xc) + LN_EPS)
        xhat_ref[...] = xc * rstd
        rstd_ref[...] = rstd

    row = lambda i: (i, 0)
    fixed = lambda i: (0, 0)
    n_tiles = t_len // tm
    return _pallas(
        body, name="out_project", grid=(n_tiles,), operands=(o_hg, o_mla, x, g_a, w_out),
        out_shape=[jax.ShapeDtypeStruct((t_len, D_MODEL), BF16), jax.ShapeDtypeStruct((t_len, D_MODEL), F32),
                   jax.ShapeDtypeStruct((t_len, D_MODEL), F32), jax.ShapeDtypeStruct((t_len, 1), F32)],
        in_specs=[pl.BlockSpec((tm, half), row), pl.BlockSpec((tm, half), row), pl.BlockSpec((tm, D_MODEL), row),
                  pl.BlockSpec((1, D_MODEL), fixed), pl.BlockSpec((D_MODEL, D_MODEL), fixed)],
        out_specs=[pl.BlockSpec((tm, D_MODEL), row), pl.BlockSpec((tm, D_MODEL), row),
                   pl.BlockSpec((tm, D_MODEL), row), pl.BlockSpec((tm, 1), row)],
        params=_params(48, ("arbitrary",)), exchange=exchange,
        first=lambda: pl.program_id(0) == 0, last=lambda: pl.program_id(0) == n_tiles - 1)


V_LN1G, V_LN1B, V_SCM, V_SHM, V_GM, V_GA, V_LN2G, V_LN2B = range(8)
S_DLN2G, S_DLN2B, S_DGM, S_DSCM, S_DSHM, S_DLN1G, S_DLN1B, S_DGA, S_LOSS = range(9)


def _mlp_and_back(xhat1, rstd1, mix, target, o_mla, vecs, w1, w2, w_out):
    t_len = xhat1.shape[0]
    tm = min(256, t_len)
    n_ff = w1.shape[0]
    ff = w1.shape[2]

    def body(xhat_ref, rstd_ref, mix_ref, tgt_ref, omla_ref, vec_ref, w1_hbm, w2_hbm, wout_hbm,
             act_ref, dhp_ref, um_ref, dh_ref, dmix_ref, dcat_ref, dr1_ref, sums_ref, delta_ref,
             w1_s, w2_s, wout_s, hp_s, load_sems):
        @pl.when(pl.program_id(0) == 0)
        def _():
            loads = [pltpu.make_async_copy(w1_hbm, w1_s, load_sems.at[0]),
                     pltpu.make_async_copy(w2_hbm, w2_s, load_sems.at[1]),
                     pltpu.make_async_copy(wout_hbm, wout_s, load_sems.at[2])]
            for cp in loads:
                cp.start()
            sums_ref[...] = jnp.zeros_like(sums_ref)
            for cp in loads:
                cp.wait()

        vec = lambda r: vec_ref[r:r + 1, :]
        xhat = xhat_ref[...]
        x1 = xhat * vec(V_LN1G) + vec(V_LN1B)
        um = (x1 * (1.0 + vec(V_SCM)) + vec(V_SHM)).astype(BF16)
        um_ref[...] = um
        h = jnp.zeros((tm, D_MODEL), F32)
        for j in range(n_ff):
            hp = _dot(um, w1_s[j])
            hp_s[j] = hp
            act = jnp.square(jnp.maximum(hp, 0.0)).astype(BF16)
            act_ref[:, j * ff:(j + 1) * ff] = act
            h = h + _dot(act, w2_s[j])
        r2 = DN_ALPHA * x1 + (1.0 + vec(V_GM)) * h
        xc = r2 - _rowmean(r2)
        rstd2 = lax.rsqrt(_rowmean(xc * xc) + LN_EPS)
        xhat2 = xc * rstd2
        err = xhat2 * vec(V_LN2G) + vec(V_LN2B) - tgt_ref[...]
        loss = 0.5 * jnp.sum(_rowmean(err * err))
        dy = err * (1.0 / D_MODEL)
        dxh = dy * vec(V_LN2G)
        dr2 = rstd2 * (dxh - _rowmean(dxh) - xhat2 * _rowmean(dxh * xhat2))
        dh = ((1.0 + vec(V_GM)) * dr2).astype(BF16)
        dh_ref[...] = dh
        sums_ref[S_DLN2G:S_DLN2G + 1, :] += _colsum(dy * xhat2)
        sums_ref[S_DLN2B:S_DLN2B + 1, :] += _colsum(dy)
        sums_ref[S_DGM:S_DGM + 1, :] += _colsum(dr2 * h)
        sums_ref[S_LOSS:S_LOSS + 1, :] += jnp.full((1, D_MODEL), loss, F32)
        du = jnp.zeros((tm, D_MODEL), F32)
        for j in range(n_ff):
            dhp = (_dot_nt(dh, w2_s[j]) * (2.0 * jnp.maximum(hp_s[j], 0.0))).astype(BF16)
            dhp_ref[:, j * ff:(j + 1) * ff] = dhp
            du = du + _dot_nt(dhp, w1_s[j])
        sums_ref[S_DSCM:S_DSCM + 1, :] += _colsum(du * x1)
        sums_ref[S_DSHM:S_DSHM + 1, :] += _colsum(du)
        dx1 = DN_ALPHA * dr2 + du * (1.0 + vec(V_SCM))
        sums_ref[S_DLN1G:S_DLN1G + 1, :] += _colsum(dx1 * xhat)
        sums_ref[S_DLN1B:S_DLN1B + 1, :] += _colsum(dx1)
        dxh1 = dx1 * vec(V_LN1G)
        dr1 = rstd_ref[...] * (dxh1 - _rowmean(dxh1) - xhat * _rowmean(dxh1 * xhat))
        dr1_ref[...] = dr1
        sums_ref[S_DGA:S_DGA + 1, :] += _colsum(dr1 * mix_ref[...])
        dmix = ((1.0 + vec(V_GA)) * dr1).astype(BF16)
        dmix_ref[...] = dmix
        dcat = _dot_nt(dmix, wout_s[...])
        dcat_ref[...] = dcat
        ones = jnp.ones((8, HEAD_DIM), F32)
        half = N_HEADS * HEAD_DIM
        for hd in range(N_HEADS):
            prod = dcat[:, half + hd * HEAD_DIM:half + (hd + 1) * HEAD_DIM] * omla_ref[:, hd * HEAD_DIM:(hd + 1) * HEAD_DIM]
            delta_ref[hd] = lax.dot_general(ones, prod, (((1,), (1,)), ((), ())), preferred_element_type=F32,
                                            precision=lax.Precision.HIGHEST)[0:1]

    row = lambda i: (i, 0)
    fixed = lambda i: (0, 0)
    any_spec = pl.BlockSpec(memory_space=pl.ANY)
    return _pcall(
        body, name="mlp_and_back", grid=(t_len // tm,),
        out_shape=[jax.ShapeDtypeStruct((t_len, D_FF), BF16), jax.ShapeDtypeStruct((t_len, D_FF), BF16),
                   jax.ShapeDtypeStruct((t_len, D_MODEL), BF16), jax.ShapeDtypeStruct((t_len, D_MODEL), BF16),
                   jax.ShapeDtypeStruct((t_len, D_MODEL), BF16), jax.ShapeDtypeStruct((t_len, D_MODEL), F32),
                   jax.ShapeDtypeStruct((t_len, D_MODEL), F32), jax.ShapeDtypeStruct((16, D_MODEL), F32),
                   jax.ShapeDtypeStruct((N_HEADS, 1, t_len), F32)],
        in_specs=[pl.BlockSpec((tm, D_MODEL), row), pl.BlockSpec((tm, 1), row), pl.BlockSpec((tm, D_MODEL), row),
                  pl.BlockSpec((tm, D_MODEL), row), pl.BlockSpec((tm, N_HEADS * HEAD_DIM), row),
                  pl.BlockSpec((8, D_MODEL), fixed), any_spec, any_spec, any_spec],
        out_specs=[pl.BlockSpec((tm, D_FF), row), pl.BlockSpec((tm, D_FF), row), pl.BlockSpec((tm, D_MODEL), row),
                   pl.BlockSpec((tm, D_MODEL), row), pl.BlockSpec((tm, D_MODEL), row), pl.BlockSpec((tm, D_MODEL), row),
                   pl.BlockSpec((tm, D_MODEL), row), pl.BlockSpec((16, D_MODEL), fixed),
                   pl.BlockSpec((N_HEADS, 1, tm), lambda i: (0, 0, i))],
        scratch_shapes=[pltpu.VMEM(w1.shape, BF16), pltpu.VMEM(w2.shape, BF16), pltpu.VMEM(w_out.shape, BF16),
                        pltpu.VMEM((n_ff, tm, ff), F32), pltpu.SemaphoreType.DMA((3,))],
        compiler_params=_params(56, ("arbitrary",)),
        operands=(xhat1, rstd1, mix, target, o_mla, vecs, w1, w2, w_out))


def _in_project_backward(dq, dk, dv, cq, ckv, pos, invf, q_norm_w, kv_norm_w, w_q, w_kv,
                         d_hq, d_hf, d_hi, d_hg, w_in, dr1, x, sc_a, exchange=None):
    t_len = x.shape[0]
    tm = min(512, t_len)
    per_q = dq.shape[3] // tm
    hgw = N_HEADS * HEAD_DIM

    def body(dq_ref, dk_ref, dv_ref, cq_ref, ckv_ref, pos_ref, invf_ref, qn_ref, kvn_ref, wq_ref, wkv_ref,
             dhq_ref, dhf_ref, dhi_ref, dhg_ref, win_ref, dr1_ref, x_ref, sc_ref,
             dz_ref, dqf_ref, dkvu_ref, cqn_ref, ckvn_ref, gx_ref, sums_ref):
        @pl.when(pl.program_id(0) == 0)
        def _():
            sums_ref[...] = jnp.zeros_like(sums_ref)

        cos_t, sin_t = _rope_tables(pos_ref[...], invf_ref[...])
        cq = cq_ref[...]
        ckv = ckv_ref[...]
        rq = lax.rsqrt(_rowmean(cq * cq) + RMS_EPS)
        rkv = lax.rsqrt(_rowmean(ckv * ckv) + RMS_EPS)
        cqn_ref[...] = (cq * rq * qn_ref[...]).astype(BF16)
        ckvn_ref[...] = (ckv * rkv * kvn_ref[...]).astype(BF16)
        d_cqn = jnp.zeros((tm, Q_RANK), F32)
        d_ckvn = jnp.zeros((tm, KV_RANK), F32)
        d_kpe = jnp.zeros((tm, 128), F32)
        for h in range(N_HEADS):
            dqh = jnp.transpose(dq_ref[h])
            dqf_ref[h, :, 0:HEAD_DIM] = dqh[:, :HEAD_DIM].astype(BF16)
            dqf_ref[h, :, HEAD_DIM:QK_DIM] = _unrope(dqh[:, HEAD_DIM:], cos_t, sin_t).astype(BF16)
            d_cqn = d_cqn + _dot_nt(dqf_ref[h], wq_ref[h])
            dkh = dk_ref[h]
            d_kpe = d_kpe + dkh[:, HEAD_DIM:]
            dkvu_ref[h, :, 0:HEAD_DIM] = dkh[:, :HEAD_DIM].astype(BF16)
            dkvu_ref[h, :, HEAD_DIM:2 * HEAD_DIM] = dv_ref[h].astype(BF16)
            d_ckvn = d_ckvn + _dot_nt(dkvu_ref[h], wkv_ref[h])
        dyq = d_cqn * qn_ref[...]
        dykv = d_ckvn * kvn_ref[...]
        sums_ref[2:3, 0:Q_RANK] += _colsum(d_cqn * cq * rq)
        sums_ref[3:4, 0:KV_RANK] += _colsum(d_ckvn * ckv * rkv)
        dz_ref[:, 0:hgw] = dhq_ref[...]
        dz_ref[:, hgw:2 * hgw] = dhf_ref[...]
        dz_ref[:, 2 * hgw:3 * hgw] = dhi_ref[...]
        dz_ref[:, 3 * hgw:4 * hgw] = dhg_ref[...]
        dz_ref[:, HG_COLS:HG_COLS + Q_RANK] = (rq * dyq - cq * (rq * rq * rq) * _rowmean(dyq * cq)).astype(BF16)
        dz_ref[:, HG_COLS + Q_RANK:HG_COLS + Q_RANK + KV_RANK] = (
            rkv * dykv - ckv * (rkv * rkv * rkv) * _rowmean(dykv * ckv)).astype(BF16)
        dz_ref[:, HG_COLS + Q_RANK + KV_RANK:] = _unrope(d_kpe, cos_t, sin_t).astype(BF16)
        du = _dot(dz_ref[...], win_ref[...])
        xv = x_ref[...]
        gx_ref[...] = DN_ALPHA * dr1_ref[...] + (1.0 + sc_ref[...]) * du
        sums_ref[0:1, :] += _colsum(du * xv)
        sums_ref[1:2, :] += _colsum(du)

    row = lambda i: (i, 0)
    fixed2 = lambda i: (0, 0)
    fixed3 = lambda i: (0, 0, 0)
    heads = lambda i: (0, i, 0)
    n_tiles = t_len // tm
    return _pallas(
        body, name="in_project_backward", grid=(n_tiles,),
        operands=(dq, dk, dv, cq, ckv, pos, invf, q_norm_w, kv_norm_w, w_q, w_kv, d_hq, d_hf, d_hi, d_hg, w_in, dr1, x,
                  sc_a),
        out_shape=[jax.ShapeDtypeStruct((t_len, IN_COLS_PAD), BF16), jax.ShapeDtypeStruct((N_HEADS, t_len, QK_DIM), BF16),
                   jax.ShapeDtypeStruct((N_HEADS, t_len, 2 * HEAD_DIM), BF16), jax.ShapeDtypeStruct((t_len, Q_RANK), BF16),
                   jax.ShapeDtypeStruct((t_len, KV_RANK), BF16), jax.ShapeDtypeStruct((t_len, D_MODEL), F32),
                   jax.ShapeDtypeStruct((8, D_MODEL), F32)],
        in_specs=[pl.BlockSpec((N_HEADS, None, QK_DIM, tm), lambda i: (0, i // per_q, 0, i % per_q)),
                  pl.BlockSpec((N_HEADS, tm, QK_DIM), heads),
                  pl.BlockSpec((N_HEADS, tm, HEAD_DIM), heads), pl.BlockSpec((tm, Q_RANK), row),
                  pl.BlockSpec((tm, KV_RANK), row), pl.BlockSpec((tm, 1), row), pl.BlockSpec((1, 128), fixed2),
                  pl.BlockSpec((1, Q_RANK), fixed2), pl.BlockSpec((1, KV_RANK), fixed2),
                  pl.BlockSpec((N_HEADS, Q_RANK, QK_DIM), fixed3), pl.BlockSpec((N_HEADS, KV_RANK, 2 * HEAD_DIM), fixed3),
                  pl.BlockSpec((tm, hgw), row), pl.BlockSpec((tm, hgw), row), pl.BlockSpec((tm, hgw), row),
                  pl.BlockSpec((tm, hgw), row), pl.BlockSpec((IN_COLS_PAD, D_MODEL), fixed2),
                  pl.BlockSpec((tm, D_MODEL), row), pl.BlockSpec((tm, D_MODEL), row), pl.BlockSpec((1, D_MODEL), fixed2)],
        out_specs=[pl.BlockSpec((tm, IN_COLS_PAD), row), pl.BlockSpec((N_HEADS, tm, QK_DIM), heads),
                   pl.BlockSpec((N_HEADS, tm, 2 * HEAD_DIM), heads), pl.BlockSpec((tm, Q_RANK), row),
                   pl.BlockSpec((tm, KV_RANK), row), pl.BlockSpec((tm, D_MODEL), row), pl.BlockSpec((8, D_MODEL), fixed2)],
        params=_params(48, ("arbitrary",)), exchange=exchange,
        first=lambda: pl.program_id(0) == 0, last=lambda: pl.program_id(0) == n_tiles - 1)


def _weight_grad(a, b, name, n_blocks, bn, a_blocked=False, b_blocked=True, exchange=None, token_tile=512):
    t_len = a.shape[0]
    m = a.shape[1] // n_blocks if a_blocked else a.shape[1]
    bt = min(token_tile, t_len)

    def body(a_ref, b_ref, o_ref):
        @pl.when(pl.program_id(1) == 0)
        def _():
            o_ref[...] = jnp.zeros_like(o_ref)

        o_ref[...] += _dot_tn(a_ref[...].astype(BF16), b_ref[...].astype(BF16))

    a_spec = pl.BlockSpec((bt, m), (lambda n, t: (t, n)) if a_blocked else (lambda n, t: (t, 0)))
    if b.ndim == 3:
        b_spec = pl.BlockSpec((None, bt, bn), lambda n, t: (n, t, 0))
    else:
        b_spec = pl.BlockSpec((bt, bn), (lambda n, t: (t, n)) if b_blocked else (lambda n, t: (t, 0)))
    nt = t_len // bt
    (out,), landed = _pallas(
        body, name=name, grid=(n_blocks, nt), operands=(a, b),
        out_shape=[jax.ShapeDtypeStruct((n_blocks, m, bn), F32)],
        in_specs=[a_spec, b_spec],
        out_specs=[pl.BlockSpec((None, m, bn), lambda n, t: (n, 0, 0))],
        params=_params(56, ("arbitrary", "arbitrary")), exchange=exchange,
        first=lambda: (pl.program_id(0) == 0) & (pl.program_id(1) == 0),
        last=lambda: (pl.program_id(0) == n_blocks - 1) & (pl.program_id(1) == nt - 1))
    return (out, landed) if exchange else out


def _reduce_small(gathered, lb_raw):
    def body(g_ref, lb_ref, tot_ref, dlb_ref):
        tot = g_ref[0]
        for d in range(1, N_DEV):
            tot = tot + g_ref[d]
        tot_ref[...] = tot
        a = lb_ref[...]
        m = jnp.max(a, axis=0, keepdims=True)
        e = jnp.exp(a - m)
        lb = e[0:1] / jnp.sum(e, axis=0, keepdims=True)
        d0 = tot[10:11, 0:512] * lb * (1.0 - lb)
        dlb_ref[0:1, :] = d0
        dlb_ref[1:2, :] = -d0

    return pl.pallas_call(
        body, name="reduce_small",
        out_shape=[jax.ShapeDtypeStruct((SMALL_ROWS, D_MODEL), F32), jax.ShapeDtypeStruct((2, 512), F32)],
    )(gathered, lb_raw)


def _adamw_update(w, gv, m, v):
    nm = ADAM_B1 * m + (1.0 - ADAM_B1) * gv
    nv = ADAM_B2 * v + (1.0 - ADAM_B2) * jnp.square(gv)
    m_hat = nm / (1.0 - ADAM_B1 ** ADAM_STEP)
    v_hat = nv / (1.0 - ADAM_B2 ** ADAM_STEP)
    return -ADAM_LR * (m_hat / (jnp.sqrt(v_hat) + ADAM_EPS) + ADAM_WD * w), nm, nv


def _adamw_halves(core, w, mine, theirs, m, v, name):
    rows, cols = w.shape
    h = rows // 2
    tr = _row_tile(h)
    per_half = h // tr

    def body(core_ref, w_ref, mine_ref, theirs_ref, m_ref, v_ref, g_ref, d_ref, nm_ref, nv_ref):
        is_mine = pl.program_id(0) // per_half == core_ref[0]
        gv = jnp.where(is_mine, mine_ref[...], theirs_ref[...])
        g_ref[...] = gv
        d_ref[...], nm_ref[...], nv_ref[...] = _adamw_update(w_ref[...], gv, m_ref[...], v_ref[...])

    full = pl.BlockSpec((tr, cols), lambda i, core_ref: (i, 0))
    part = pl.BlockSpec((tr, cols), lambda i, core_ref: (i % per_half, 0))
    return _pcall(
        body, name=name, out_shape=[jax.ShapeDtypeStruct(w.shape, F32)] * 4,
        grid_spec=pltpu.PrefetchScalarGridSpec(
            num_scalar_prefetch=1, grid=(rows // tr,), in_specs=[full, part, part, full, full], out_specs=[full] * 4),
        compiler_params=_params(40, ("arbitrary",)),
        operands=(core, w, mine, theirs, m, v))


def _adamw(w, g, m, v, name):
    rows, cols = w.shape
    tr = _row_tile(rows) if rows >= 8 else rows

    def body(w_ref, g_ref, m_ref, v_ref, d_ref, nm_ref, nv_ref):
        d_ref[...], nm_ref[...], nv_ref[...] = _adamw_update(w_ref[...], g_ref[...], m_ref[...], v_ref[...])

    spec = pl.BlockSpec((tr, cols), lambda i: (i, 0))
    return _pcall(
        body, name=name, grid=(rows // tr,),
        out_shape=[jax.ShapeDtypeStruct(w.shape, F32)] * 3,
        in_specs=[spec] * 4, out_specs=[spec] * 3,
        compiler_params=_params(40, ("arbitrary",)),
        operands=(w, g, m, v))


def kernel(x, c, positions, w_ada, b_ada, w_in, hg_lower_bounds, hg_norm_w, mla_q_norm_w, w_q_up, mla_kv_norm_w, w_kv_up, w_out, ln1_g, ln1_b, w_mlp_in, w_mlp_out, ln2_g, ln2_b, loss_target, m_w_ada, m_b_ada, m_w_in, m_hg_lower_bounds, m_hg_norm_w, m_mla_q_norm_w, m_w_q_up, m_mla_kv_norm_w, m_w_kv_up, m_w_out, m_ln1_g, m_ln1_b, m_w_mlp_in, m_w_mlp_out, m_ln2_g, m_ln2_b, v_w_ada, v_b_ada, v_w_in, v_hg_lower_bounds, v_hg_norm_w, v_mla_q_norm_w, v_w_q_up, v_mla_kv_norm_w, v_w_kv_up, v_w_out, v_ln1_g, v_ln1_b, v_w_mlp_in, v_w_mlp_out, v_ln2_g, v_ln2_b):
    ix, iy, ic = _mesh_pos()
    chip = 2 * ix + iy
    me = 4 * ix + 2 * iy + ic
    core_arr = jnp.reshape(ic, (1,)).astype(jnp.int32)
    chip_arr = jnp.reshape(chip, (1,)).astype(jnp.int32)

    xs = x[0]
    target = loss_target[0]
    t_len = xs.shape[0]
    pos = positions.astype(F32).reshape(t_len, 1)
    inv = 1.0 / (ROPE_THETA ** (jnp.arange(0, ROPE_DIM, 2, dtype=F32) / ROPE_DIM))
    invf = jnp.concatenate([inv, inv, jnp.zeros((128 - ROPE_DIM,), F32)]).reshape(1, 128)

    def slot(w):
        rows, cols = w.shape
        own = w.astype(BF16).reshape(1, 2, rows // 2, cols)
        return lax.dynamic_update_slice(jnp.zeros((N_CHIPS, 2, rows // 2, cols), BF16), own, (chip, 0, 0, 0))

    def slot8(a):
        return lax.dynamic_update_slice(jnp.zeros((N_DEV,) + a.shape, a.dtype), a[None], (me, 0, 0))

    def whole(s):
        return s.reshape(N_CHIPS, 2 * s.shape[2], s.shape[3])

    def halved(g):
        return g.reshape(N_CHIPS, 2, g.shape[1] // 2, g.shape[2])

    ada_cols = w_ada.shape[2]
    c_all, *early = _run_exchange(
        _merge(_gather_all(slot8(jnp.broadcast_to(c, (8, D_MODEL)))),
               _gather_over_ici([slot(jnp.transpose(w_in[0])), slot(w_q_up[0]), slot(w_kv_up[0])])),
        "gather_c_and_mixer_weights_ici")
    b_shard = lax.dynamic_slice(b_ada, (0, chip * ada_cols), (1, ada_cols))
    mod_cols, cond16 = _ada_project(c_all[:, 0, :], w_ada[0], b_shard)
    mod_all, *early = _run_exchange(_merge(_gather_all(slot8(mod_cols)), _gather_over_d2d(early)),
                                    "gather_mod_and_mixer_weights_d2d")
    mod_mine = lax.dynamic_slice(mod_all, (0, me, 0), (N_DEV, 1, ada_cols))[::2, 0, :].reshape(6, D_MODEL)
    sh_a, sc_a, g_a, sh_m, sc_m, g_m = (mod_mine[i:i + 1] for i in range(6))
    g_in, g_q, g_kv = (whole(s) for s in early)
    w_in_full = jnp.pad(g_in.reshape(IN_COLS, D_MODEL), ((0, IN_COLS_PAD - IN_COLS), (0, 0)))
    w_q_full = jnp.pad(g_q, ((0, 0), (0, 0), (0, QK_DIM - g_q.shape[2])))

    (u_a, zhg, cq, ckv, q, k, k_t, v, v_t), (s_w1,) = _in_project(
        xs, pos, sc_a, sh_a, w_in_full, mla_q_norm_w, mla_kv_norm_w, w_q_full, g_kv, invf,
        _gather_over_ici([slot(w_mlp_in[0])]))
    (o_pre, o_hg, states), (s_out, s_w1) = _hgrn_forward(
        zhg, hg_lower_bounds, hg_norm_w, _merge(_gather_over_ici([slot(w_out[0])]), _gather_over_d2d([s_w1])))
    (o_mla, lse), (s_w2, s_out) = _attention_forward(
        q, k, v_t, _merge(_gather_over_ici([slot(w_mlp_out[0])]), _gather_over_d2d([s_out])))
    w_out_full = whole(s_out).reshape(D_MODEL, D_MODEL)
    (cat, mix, xhat1, rstd1), (s_w2,) = _out_project(o_hg, o_mla, xs, g_a, w_out_full, _gather_over_d2d([s_w2]))
    g_w1, g_w2 = whole(s_w1), whole(s_w2)
    vecs = jnp.concatenate([ln1_g, ln1_b, sc_m, sh_m, g_m, g_a, ln2_g, ln2_b], axis=0)
    act, dhp, um, dh, dmix, d_cat, dr1, mlp_sums, delta = _mlp_and_back(
        xhat1, rstd1, mix, target, o_mla, vecs, g_w1, g_w2, w_out_full)

    gw_1 = _weight_grad(um, dhp, "grad_w_mlp_in", N_CHIPS, D_FF // N_CHIPS, token_tile=4096)
    gw_2 = _weight_grad(act, dh, "grad_w_mlp_out", N_CHIPS, D_MODEL, a_blocked=True, b_blocked=False, token_tile=4096)
    gw_out = _weight_grad(cat, dmix, "grad_w_out", 1, D_MODEL, token_tile=2048)
    gw_out = gw_out.reshape(N_CHIPS, D_MODEL // N_CHIPS, D_MODEL)
    mlp_grads = [halved(gw_1), halved(gw_2), halved(gw_out)]
    (dq, dk, dv), landed = _attention_backward(q, k, k_t, v, d_cat, lse, delta, _pair_exchange(mlp_grads))
    chip_sums = [_add_pair(core_arr, g, l) for g, l in zip(mlp_grads, landed)]
    (d_hq, d_hf, d_hi, d_hg, hg_sums), landed = _hgrn_backward(
        zhg, hg_lower_bounds, hg_norm_w, o_pre, d_cat, states, _chip_exchange([b for _, b in chip_sums]))
    mlp_mine = [_add_chips(chip_arr, p, l) for (p, _), l in zip(chip_sums, landed)]
    (dz, dqf, dkvu, cqn, ckvn, grad_x, in_sums), _ = _in_project_backward(
        dq, dk, dv, cq, ckv, pos, invf, mla_q_norm_w, mla_kv_norm_w, w_q_full, g_kv,
        d_hq, d_hf, d_hi, d_hg, w_in_full, dr1, xs, sc_a)

    gw_in, mlp_theirs = _weight_grad(dz, u_a, "grad_w_in", 3, D_MODEL, a_blocked=True, b_blocked=False,
                                     exchange=_pair_send(mlp_mine), token_tile=4096)
    gw_in = gw_in.reshape(IN_COLS_PAD, D_MODEL)[:IN_COLS].reshape(N_CHIPS, IN_COLS // N_CHIPS, D_MODEL)
    gw_q = _weight_grad(cqn, dqf, "grad_w_q_up", N_HEADS, QK_DIM, token_tile=2048)[:, :, :HEAD_DIM + ROPE_DIM]
    gw_kv = _weight_grad(ckvn, dkvu, "grad_w_kv_up", N_HEADS, 2 * HEAD_DIM, token_tile=2048)
    mixer_mine, mixer_theirs = _reduce_in_vmem([halved(g) for g in (gw_in, gw_q, gw_kv)], "reduce_mixer_grads")
    reduced = ("w_in", "w_q_up", "w_kv_up", "w_mlp_in", "w_mlp_out", "w_out")
    halves_mine = dict(zip(reduced, list(mixer_mine) + mlp_mine))
    halves_theirs = dict(zip(reduced, list(mixer_theirs) + list(mlp_theirs)))

    zeros = lambda n: jnp.zeros((1, n), F32)
    small = jnp.concatenate([
        in_sums[1:2], in_sums[0:1], mlp_sums[S_DGA:S_DGA + 1],
        mlp_sums[S_DSHM:S_DSHM + 1], mlp_sums[S_DSCM:S_DSCM + 1], mlp_sums[S_DGM:S_DGM + 1],
        mlp_sums[S_DLN1G:S_DLN1G + 1], mlp_sums[S_DLN1B:S_DLN1B + 1],
        mlp_sums[S_DLN2G:S_DLN2G + 1], mlp_sums[S_DLN2B:S_DLN2B + 1],
        jnp.concatenate([hg_sums[0:1], hg_sums[1:2]], axis=1),
        jnp.concatenate([in_sums[2:3, :Q_RANK], in_sums[3:4, :KV_RANK], zeros(D_MODEL - Q_RANK - KV_RANK)], axis=1),
        mlp_sums[S_LOSS:S_LOSS + 1],
        jnp.zeros((SMALL_ROWS - 13, D_MODEL), F32)], axis=0)
    small_all = _allgather8(small, "gather_small")
    tot, g_lb = _reduce_small(small_all, hg_lower_bounds)
    loss = tot[12, 0]
    g_b_ada = tot[0:6].reshape(1, 6 * D_MODEL)
    g_ln1_g, g_ln1_b, g_ln2_g, g_ln2_b = tot[6:7], tot[7:8], tot[8:9], tot[9:10]
    g_hg_norm = tot[10:11, 512:1024]
    g_q_norm = tot[11:12, 0:Q_RANK]
    g_kv_norm = tot[11:12, Q_RANK:Q_RANK + KV_RANK]

    d_mod_all = small_all[:, 0:6, :].reshape(N_DEV, 6 * D_MODEL)
    d_mod_cols = lax.dynamic_slice(d_mod_all, (0, chip * ada_cols), (N_DEV, ada_cols))
    d_mod_cols = jnp.concatenate([d_mod_cols, jnp.zeros_like(d_mod_cols)], axis=0)
    g_w_ada = _weight_grad(cond16, d_mod_cols, "grad_w_ada", 1, ada_cols)[0]

    names = ["w_ada", "b_ada", "w_in", "hg_lower_bounds", "hg_norm_w", "mla_q_norm_w", "w_q_up", "mla_kv_norm_w",
             "w_kv_up", "w_out", "ln1_g", "ln1_b", "w_mlp_in", "w_mlp_out", "ln2_g", "ln2_b"]
    weights = [w_ada, b_ada, w_in, hg_lower_bounds, hg_norm_w, mla_q_norm_w, w_q_up, mla_kv_norm_w,
               w_kv_up, w_out, ln1_g, ln1_b, w_mlp_in, w_mlp_out, ln2_g, ln2_b]
    moms = [m_w_ada, m_b_ada, m_w_in, m_hg_lower_bounds, m_hg_norm_w, m_mla_q_norm_w, m_w_q_up, m_mla_kv_norm_w,
            m_w_kv_up, m_w_out, m_ln1_g, m_ln1_b, m_w_mlp_in, m_w_mlp_out, m_ln2_g, m_ln2_b]
    vels = [v_w_ada, v_b_ada, v_w_in, v_hg_lower_bounds, v_hg_norm_w, v_mla_q_norm_w, v_w_q_up, v_mla_kv_norm_w,
            v_w_kv_up, v_w_out, v_ln1_g, v_ln1_b, v_w_mlp_in, v_w_mlp_out, v_ln2_g, v_ln2_b]
    grads2d = [g_w_ada, g_b_ada, None, g_lb, g_hg_norm, g_q_norm, None, g_kv_norm,
               None, None, g_ln1_g, g_ln1_b, None, None, g_ln2_g, g_ln2_b]
    out_g, out_d, out_m, out_v = [], [], [], []
    for name, w, g, m, vv in zip(names, weights, grads2d, moms, vels):
        if name == "w_in":
            to2d, back = (lambda a: jnp.transpose(a[0])), (lambda a: jnp.transpose(a)[None])
        else:
            shape2 = w.shape[1:] if g is None else g.shape
            to2d, back = (lambda a, s=shape2: a.reshape(s)), (lambda a, s=w.shape: a.reshape(s))
        if g is None:
            g, d, nm, nv = _adamw_halves(core_arr, to2d(w), halves_mine[name], halves_theirs[name], to2d(m), to2d(vv),
                                         "adamw_" + name)
        else:
            d, nm, nv = _adamw(to2d(w), g, to2d(m), to2d(vv), "adamw_" + name)
        out_g.append(back(g))
        out_d.append(back(d))
        out_m.append(back(nm))
        out_v.append(back(nv))
    return (loss, grad_x[None], *out_g, *out_d, *out_m, *out_v)
```

```python
import functools

import jax
import jax.numpy as jnp
from jax import lax
from jax.experimental import pallas as pl
from jax.experimental.pallas import tpu as pltpu

F32 = jnp.float32
BF16 = jnp.bfloat16
MESH_IDS = pl.DeviceIdType.MESH

D_MODEL = 1024
N_HEADS = 4
HEAD_DIM = 128
ROPE_DIM = 64
HG_CHUNK = 64
HG_COLS = 2048
Q_RANK = 256
KV_RANK = 256
IN_COLS = 2624
IN_COLS_PAD = 2688
QK_DIM = 256
D_FF = 4096
N_CHIPS = 4
N_DEV = 8
ROPE_THETA = 10000.0
RMS_EPS = 1e-6
LN_EPS = 1e-5
DN_ALPHA = 2.0 ** 0.25
ATT_SCALE = (HEAD_DIM + ROPE_DIM) ** -0.5
NEG_BIG = -1e30
ADAM_LR = 0.001
ADAM_B1 = 0.9
ADAM_B2 = 0.999
ADAM_EPS = 1e-08
ADAM_WD = 0.01
ADAM_STEP = 10
SMALL_ROWS = 16
MIB = 1024 * 1024


def _dot(a, b):
    return jnp.dot(a, b, preferred_element_type=F32)


def _dot_nt(a, b):
    return lax.dot_general(a, b, (((1,), (1,)), ((), ())), preferred_element_type=F32)


def _dot_tn(a, b):
    return lax.dot_general(a, b, (((0,), (0,)), ((), ())), preferred_element_type=F32)


def _params(vmem_mib, semantics=None):
    return pltpu.CompilerParams(vmem_limit_bytes=vmem_mib * MIB, dimension_semantics=semantics)


def _sigmoid(v):
    return 1.0 / (1.0 + jnp.exp(-v))


def _colsum(v):
    return jnp.sum(v, axis=0, keepdims=True)


def _rowmean(v):
    return jnp.mean(v, axis=-1, keepdims=True)


def _rope_tables(pos, invf):
    ang = pos * invf
    lane = lax.broadcasted_iota(jnp.int32, ang.shape, 1)
    cos_t = jnp.where(lane < ROPE_DIM, jnp.cos(ang), 0.0)
    sin = jnp.sin(ang)
    sin_t = jnp.where(lane < ROPE_DIM // 2, -sin, jnp.where(lane < ROPE_DIM, sin, 0.0))
    return cos_t, sin_t


def _swap_halves(t):
    lane = lax.broadcasted_iota(jnp.int32, t.shape, 1)
    return jnp.where(lane < ROPE_DIM // 2, pltpu.roll(t, 128 - ROPE_DIM // 2, 1), pltpu.roll(t, ROPE_DIM // 2, 1))


def _rope(t, cos_t, sin_t):
    return t * cos_t + _swap_halves(t) * sin_t


def _unrope(g, cos_t, sin_t):
    return g * cos_t - _swap_halves(g) * sin_t


def _mesh_pos():
    return lax.axis_index("x"), lax.axis_index("y"), lax.axis_index("c")


def _other_chips(x, y):
    out = []
    for dx, dy in ((1, 0), (0, 1), (1, 1)):
        px = 1 - x if dx else x
        py = 1 - y if dy else y
        out.append(((px, py), 2 * px + py))
    return out


def _allgather8(a, name):
    rows, cols = a.shape

    def body(a_ref, out_ref, send_sems, recv_sems):
        x, y, c = _mesh_pos()
        me = 4 * x + 2 * y + c
        out_ref[me] = a_ref[...]
        peers = []
        for r in range(1, N_DEV):
            px = 1 - x if r & 4 else x
            py = 1 - y if r & 2 else y
            pc = 1 - c if r & 1 else c
            peers.append(((px, py, pc), 4 * px + 2 * py + pc))

        def copy(r, block, to):
            return pltpu.make_async_remote_copy(
                src_ref=a_ref, dst_ref=out_ref.at[block], send_sem=send_sems.at[r], recv_sem=recv_sems.at[r],
                device_id=to, device_id_type=MESH_IDS)

        sends = [copy(r, me, peer) for r, (peer, _) in enumerate(peers)]
        for cp in sends:
            cp.start()
        for r, (peer, idx) in enumerate(peers):
            copy(r, idx, peer).wait_recv()
        for cp in sends:
            cp.wait_send()

    return pl.pallas_call(
        body, name=name,
        out_shape=jax.ShapeDtypeStruct((N_DEV, rows, cols), a.dtype),
        in_specs=[pl.BlockSpec(memory_space=pltpu.VMEM)],
        out_specs=pl.BlockSpec(memory_space=pltpu.VMEM),
        scratch_shapes=[pltpu.SemaphoreType.DMA((N_DEV - 1,)), pltpu.SemaphoreType.DMA((N_DEV - 1,))],
    )(a)


class _Exchange:
    def __init__(self, inputs, out_shapes, aliases, sems, start, finish):
        self.inputs, self.out_shapes, self.aliases, self.sems = list(inputs), list(out_shapes), dict(aliases), list(sems)
        self.start, self.finish = start, finish


def _from_copies(inputs, out_shapes, aliases, sems, copies):
    def start(ins, outs, sem_refs):
        for send, _ in copies(ins, outs, sem_refs):
            send.start()

    def finish(ins, outs, sem_refs):
        for send, recv in copies(ins, outs, sem_refs):
            recv.wait_recv()
            send.wait_send()

    return _Exchange(inputs, out_shapes, aliases, sems, start, finish)


HBM_MIN_BYTES = 256 * 1024


def _in_hbm(a):
    if a.size * a.dtype.itemsize < HBM_MIN_BYTES:
        return a
    return pltpu.with_memory_space_constraint(a, pltpu.HBM)


def _out_hbm(s):
    if s.size * s.dtype.itemsize < HBM_MIN_BYTES:
        return s
    return pltpu.HBM(s.shape, s.dtype)


def _pcall(body, *, operands, out_shape, **kwargs):
    single = not isinstance(out_shape, (list, tuple))
    shapes = [_out_hbm(s) for s in ([out_shape] if single else out_shape)]
    return pl.pallas_call(body, out_shape=shapes[0] if single else shapes, **kwargs)(*[_in_hbm(a) for a in operands])


def _run_exchange(exchange, name):
    n_in, n_out = len(exchange.inputs), len(exchange.out_shapes)

    def body(*refs):
        ins, outs, sem_refs = refs[:n_in], refs[n_in:n_in + n_out], refs[n_in + n_out:]
        exchange.start(ins, outs, sem_refs)
        exchange.finish(ins, outs, sem_refs)

    any_spec = pl.BlockSpec(memory_space=pl.ANY)
    return pl.pallas_call(
        body, name=name, out_shape=[_out_hbm(s) for s in exchange.out_shapes],
        in_specs=[any_spec] * n_in, out_specs=[any_spec] * n_out,
        scratch_shapes=exchange.sems, input_output_aliases=exchange.aliases,
    )(*[_in_hbm(a) for a in exchange.inputs])


def _pallas(body, *, name, operands, in_specs, out_shape, out_specs, params, scratch_shapes=(), grid=(), prefetch=(),
            exchange=None, first=None, last=None):
    n_pre, n_in, n_out, n_scr = len(prefetch), len(in_specs), len(out_specs), len(scratch_shapes)
    ex_in = exchange.inputs if exchange else []
    ex_out = exchange.out_shapes if exchange else []
    ex_sems = exchange.sems if exchange else []

    def full_body(*refs):
        pre, rest = refs[:n_pre], refs[n_pre:]
        ins, rest = rest[:n_in], rest[n_in:]
        xin, rest = rest[:len(ex_in)], rest[len(ex_in):]
        outs, rest = rest[:n_out], rest[n_out:]
        xout, rest = rest[:len(ex_out)], rest[len(ex_out):]
        scr, sem_refs = rest[:n_scr], rest[n_scr:]
        if exchange:
            @pl.when(first(*pre))
            def _():
                exchange.start(xin, xout, sem_refs)

        body(*pre, *ins, *outs, *scr)
        if exchange:
            @pl.when(last(*pre))
            def _():
                exchange.finish(xin, xout, sem_refs)

    any_spec = pl.BlockSpec(memory_space=pl.ANY)
    aliases = {n_pre + n_in + i: n_out + o for i, o in exchange.aliases.items()} if exchange else {}
    operands = [_in_hbm(a) for a in operands]
    results = pl.pallas_call(
        full_body, name=name, out_shape=[_out_hbm(s) for s in list(out_shape) + ex_out],
        grid_spec=pltpu.PrefetchScalarGridSpec(
            num_scalar_prefetch=n_pre, grid=grid, in_specs=list(in_specs) + [any_spec] * len(ex_in),
            out_specs=list(out_specs) + [any_spec] * len(ex_out), scratch_shapes=list(scratch_shapes) + ex_sems),
        input_output_aliases=aliases, compiler_params=params,
    )(*prefetch, *operands, *[_in_hbm(a) for a in ex_in])
    return results[:n_out], results[n_out:]


def _remote(src, dst, sems, idx, to):
    send_sems, recv_sems = sems
    return pltpu.make_async_remote_copy(src_ref=src, dst_ref=dst, send_sem=send_sems.at[idx], recv_sem=recv_sems.at[idx],
                                        device_id=to, device_id_type=MESH_IDS)


def _sem_pairs(*shape):
    return [pltpu.SemaphoreType.DMA(shape), pltpu.SemaphoreType.DMA(shape)]


def _same_shapes(arrays):
    return [jax.ShapeDtypeStruct(a.shape, a.dtype) for a in arrays]


def _gather_over_ici(slots):
    n = len(slots)

    def copies(ins, outs, sems):
        x, y, c = _mesh_pos()
        k = 2 * x + y
        out = []
        for j, (chip, kj) in enumerate(_other_chips(x, y)):
            for i in range(n):
                to = (*chip, c)
                out.append((_remote(ins[i].at[k, c], outs[i].at[k, c], sems, (j, i), to),
                            _remote(ins[i].at[k, c], outs[i].at[kj, c], sems, (j, i), to)))
        return out

    return _from_copies(slots, _same_shapes(slots), {i: i for i in range(n)}, _sem_pairs(3, n), copies)


def _gather_over_d2d(slots):
    n = len(slots)

    def copies(ins, outs, sems):
        x, y, c = _mesh_pos()
        sibling = (x, y, 1 - c)
        out = []
        for j, (_, kj) in enumerate(_other_chips(x, y)):
            for i in range(n):
                out.append((_remote(ins[i].at[kj, c], outs[i].at[kj, c], sems, (j, i), sibling),
                            _remote(ins[i].at[kj, c], outs[i].at[kj, 1 - c], sems, (j, i), sibling)))
        return out

    return _from_copies(slots, _same_shapes(slots), {i: i for i in range(n)}, _sem_pairs(3, n), copies)


def _gather_all(slots8):
    def copies(ins, outs, sems):
        x, y, c = _mesh_pos()
        me = 4 * x + 2 * y + c
        out = []
        for r in range(1, N_DEV):
            px = 1 - x if r & 4 else x
            py = 1 - y if r & 2 else y
            pc = 1 - c if r & 1 else c
            to = (px, py, pc)
            out.append((_remote(ins[0].at[me], outs[0].at[me], sems, r - 1, to),
                        _remote(ins[0].at[me], outs[0].at[4 * px + 2 * py + pc], sems, r - 1, to)))
        return out

    return _from_copies([slots8], _same_shapes([slots8]), {0: 0}, _sem_pairs(N_DEV - 1), copies)


def _merge(first, second):
    n_in, n_out, n_sem = len(first.inputs), len(first.out_shapes), len(first.sems)

    def start(ins, outs, sems):
        first.start(ins[:n_in], outs[:n_out], sems[:n_sem])
        second.start(ins[n_in:], outs[n_out:], sems[n_sem:])

    def finish(ins, outs, sems):
        first.finish(ins[:n_in], outs[:n_out], sems[:n_sem])
        second.finish(ins[n_in:], outs[n_out:], sems[n_sem:])

    aliases = dict(first.aliases)
    aliases.update({n_in + i: n_out + o for i, o in second.aliases.items()})
    return _Exchange(first.inputs + second.inputs, first.out_shapes + second.out_shapes, aliases,
                     first.sems + second.sems, start, finish)


def _pair_exchange(grads):
    n = len(grads)

    def copies(ins, outs, sems):
        x, y, c = _mesh_pos()
        cps = [_remote(ins[i].at[:, 1 - c], outs[i], sems, i, (x, y, 1 - c)) for i in range(n)]
        return [(cp, cp) for cp in cps]

    shapes = [jax.ShapeDtypeStruct((N_CHIPS,) + g.shape[2:], g.dtype) for g in grads]
    return _from_copies(grads, shapes, {}, _sem_pairs(n), copies)


def _chip_exchange(partials):
    n = len(partials)

    def copies(ins, outs, sems):
        x, y, c = _mesh_pos()
        cps = [_remote(ins[i].at[kj], outs[i].at[j], sems, (j, i), (*chip, c))
               for j, (chip, kj) in enumerate(_other_chips(x, y)) for i in range(n)]
        return [(cp, cp) for cp in cps]

    shapes = [jax.ShapeDtypeStruct((3,) + p.shape[1:], p.dtype) for p in partials]
    return _from_copies(partials, shapes, {}, _sem_pairs(3, n), copies)


def _pair_send(halves):
    n = len(halves)

    def copies(ins, outs, sems):
        x, y, c = _mesh_pos()
        cps = [_remote(ins[i], outs[i], sems, i, (x, y, 1 - c)) for i in range(n)]
        return [(cp, cp) for cp in cps]

    return _from_copies(halves, _same_shapes(halves), {}, _sem_pairs(n), copies)


def _reduce_in_vmem(grads, half_rows, name):
    n = len(grads)

    def body(*refs):
        g, mine, theirs = refs[:n], refs[n:2 * n], refs[2 * n:3 * n]
        landed_pair, partial, landed_chips = refs[3 * n:4 * n], refs[4 * n:5 * n], refs[5 * n:6 * n]
        sems = refs[6 * n:]
        x, y, c = _mesh_pos()
        k = 2 * x + y
        sibling = (x, y, 1 - c)

        def half(i, chip_idx, which):
            return pl.ds(pl.multiple_of((2 * chip_idx + which) * half_rows[i], 8), half_rows[i])

        def run(copies):
            for cp in copies:
                cp.start()
            for cp in copies:
                cp.wait_recv()
                cp.wait_send()

        run([_remote(g[i].at[half(i, kk, 1 - c)], landed_pair[i].at[kk], sems[0:2], (kk, i), sibling)
             for kk in range(N_CHIPS) for i in range(n)])
        for i in range(n):
            for kk in range(N_CHIPS):
                partial[i][kk] = (g[i][half(i, kk, c), :] + landed_pair[i][kk]).astype(BF16)
        run([_remote(partial[i].at[kj], landed_chips[i].at[j], sems[2:4], (j, i), (*chip, c))
             for j, (chip, kj) in enumerate(_other_chips(x, y)) for i in range(n)])
        for i in range(n):
            own = g[i][half(i, k, c), :] + landed_pair[i][k]
            mine[i][...] = ((own + landed_chips[i][0].astype(F32)) + landed_chips[i][1].astype(F32)) \
                + landed_chips[i][2].astype(F32)
        run([_remote(mine[i], theirs[i], sems[4:6], i, sibling) for i in range(n)])

    shapes = [(h, gr.shape[1]) for gr, h in zip(grads, half_rows)]
    halves = [jax.ShapeDtypeStruct(s, F32) for s in shapes]
    vmem = pl.BlockSpec(memory_space=pltpu.VMEM)
    scratch = ([pltpu.VMEM((N_CHIPS,) + s, F32) for s in shapes]
               + [pltpu.VMEM((N_CHIPS,) + s, BF16) for s in shapes]
               + [pltpu.VMEM((3,) + s, BF16) for s in shapes]
               + _sem_pairs(N_CHIPS, n) + _sem_pairs(3, n) + _sem_pairs(n))
    out = pl.pallas_call(
        body, name=name, out_shape=halves + halves, in_specs=[vmem] * n, out_specs=[vmem] * (2 * n),
        scratch_shapes=scratch, compiler_params=_params(48),
    )(*grads)
    return out[:n], out[n:]


def _row_tile(rows):
    for t in (256, 128, 64):
        if rows % t == 0:
            return t
    return rows


def _add_pair(core, grad, landed):
    _, h, cols = landed.shape
    tr = _row_tile(h)

    def body(core_ref, g_ref, l_ref, o_ref, ob_ref):
        s = g_ref[...] + l_ref[...]
        o_ref[...] = s
        ob_ref[...] = s.astype(BF16)

    out_spec = pl.BlockSpec((None, tr, cols), lambda k, t, core_ref: (k, t, 0))
    return _pcall(
        body, name="grad_add_pair",
        out_shape=[jax.ShapeDtypeStruct(landed.shape, F32), jax.ShapeDtypeStruct(landed.shape, BF16)],
        grid_spec=pltpu.PrefetchScalarGridSpec(
            num_scalar_prefetch=1, grid=(N_CHIPS, h // tr),
            in_specs=[pl.BlockSpec((None, None, tr, cols), lambda k, t, core_ref: (k, core_ref[0], t, 0)),
                      pl.BlockSpec((None, tr, cols), lambda k, t, core_ref: (k, t, 0))],
            out_specs=[out_spec, out_spec]),
        compiler_params=_params(32, ("arbitrary", "arbitrary")),
        operands=(core, grad, landed))


def _add_chips(chip, partial, landed):
    _, h, cols = partial.shape
    tr = _row_tile(h)

    def body(chip_ref, p_ref, l_ref, o_ref):
        o_ref[...] = ((p_ref[...] + l_ref[0].astype(F32)) + l_ref[1].astype(F32)) + l_ref[2].astype(F32)

    return _pcall(
        body, name="grad_add_chips",
        out_shape=jax.ShapeDtypeStruct((h, cols), F32),
        grid_spec=pltpu.PrefetchScalarGridSpec(
            num_scalar_prefetch=1, grid=(h // tr,),
            in_specs=[pl.BlockSpec((None, tr, cols), lambda t, chip_ref: (chip_ref[0], t, 0)),
                      pl.BlockSpec((3, tr, cols), lambda t, chip_ref: (0, t, 0))],
            out_specs=pl.BlockSpec((tr, cols), lambda t, chip_ref: (t, 0))),
        compiler_params=_params(32, ("arbitrary",)),
        operands=(chip, partial, landed))


def _ada_project(c_all, w_ada, b_shard):
    n = w_ada.shape[1]
    tn = 512

    def body(c_ref, w_ref, b_ref, mod_ref, cond_ref):
        cv = c_ref[...]
        cond = cv * _sigmoid(cv)
        mod_ref[...] = _dot(cond.astype(BF16), w_ref[...].astype(BF16)) + b_ref[...]
        cond_ref[0:N_DEV, :] = cond
        cond_ref[N_DEV:2 * N_DEV, :] = jnp.zeros_like(cond)

    return _pcall(
        body, name="ada_project", grid=(n // tn,),
        out_shape=[jax.ShapeDtypeStruct((N_DEV, n), F32), jax.ShapeDtypeStruct((2 * N_DEV, D_MODEL), F32)],
        in_specs=[pl.BlockSpec((N_DEV, D_MODEL), lambda j: (0, 0)), pl.BlockSpec((D_MODEL, tn), lambda j: (0, j)),
                  pl.BlockSpec((1, tn), lambda j: (0, j))],
        out_specs=[pl.BlockSpec((N_DEV, tn), lambda j: (0, j)), pl.BlockSpec((2 * N_DEV, D_MODEL), lambda j: (0, 0))],
        compiler_params=_params(32, ("arbitrary",)),
        operands=(c_all, w_ada, b_shard))


def _in_project(x, pos, sc_a, sh_a, w_in, q_norm_w, kv_norm_w, w_q, w_kv, invf, exchange=None):
    t_len = x.shape[0]
    tm = min(512, t_len)

    def body(x_ref, pos_ref, sc_ref, sh_ref, win_ref, qn_ref, kvn_ref, wq_ref, wkv_ref, invf_ref,
             u_ref, zhg_ref, cq_ref, ckv_ref, q_ref, k_ref, kt_ref, v_ref, vt_ref):
        u = (x_ref[...] * (1.0 + sc_ref[...]) + sh_ref[...]).astype(BF16)
        u_ref[...] = u
        z = _dot_nt(u, win_ref[...])
        zhg_ref[...] = z[:, :HG_COLS]
        cq = z[:, HG_COLS:HG_COLS + Q_RANK]
        ckv = z[:, HG_COLS + Q_RANK:HG_COLS + Q_RANK + KV_RANK]
        cq_ref[...] = cq
        ckv_ref[...] = ckv
        cos_t, sin_t = _rope_tables(pos_ref[...], invf_ref[...])
        k_pe = _rope(z[:, HG_COLS + Q_RANK + KV_RANK:], cos_t, sin_t)
        k_pe_t = jnp.transpose(k_pe).astype(BF16)
        cqn = (cq * lax.rsqrt(_rowmean(cq * cq) + RMS_EPS) * qn_ref[...]).astype(BF16)
        ckvn = (ckv * lax.rsqrt(_rowmean(ckv * ckv) + RMS_EPS) * kvn_ref[...]).astype(BF16)
        for h in range(N_HEADS):
            qh = _dot(cqn, wq_ref[h])
            q_ref[h, :, 0:HEAD_DIM] = qh[:, :HEAD_DIM].astype(BF16)
            q_ref[h, :, HEAD_DIM:QK_DIM] = _rope(qh[:, HEAD_DIM:], cos_t, sin_t).astype(BF16)
            kvh = _dot(ckvn, wkv_ref[h])
            k_ref[h, :, 0:HEAD_DIM] = kvh[:, :HEAD_DIM].astype(BF16)
            k_ref[h, :, HEAD_DIM:QK_DIM] = k_pe.astype(BF16)
            kt_ref[h, 0:HEAD_DIM, :] = jnp.transpose(kvh[:, :HEAD_DIM]).astype(BF16)
            kt_ref[h, HEAD_DIM:QK_DIM, :] = k_pe_t
            v_ref[h] = kvh[:, HEAD_DIM:].astype(BF16)
            vt_ref[h] = jnp.transpose(kvh[:, HEAD_DIM:]).astype(BF16)

    row = lambda i: (i, 0)
    fixed2 = lambda i: (0, 0)
    fixed3 = lambda i: (0, 0, 0)
    heads = lambda i: (0, i, 0)
    n_tiles = t_len // tm
    return _pallas(
        body, name="in_project", grid=(n_tiles,),
        operands=(x, pos, sc_a, sh_a, w_in, q_norm_w, kv_norm_w, w_q, w_kv, invf),
        out_shape=[jax.ShapeDtypeStruct((t_len, D_MODEL), BF16), jax.ShapeDtypeStruct((t_len, HG_COLS), F32),
                   jax.ShapeDtypeStruct((t_len, Q_RANK), F32), jax.ShapeDtypeStruct((t_len, KV_RANK), F32),
                   jax.ShapeDtypeStruct((N_HEADS, t_len, QK_DIM), BF16),
                   jax.ShapeDtypeStruct((N_HEADS, t_len, QK_DIM), BF16),
                   jax.ShapeDtypeStruct((N_HEADS, QK_DIM, t_len), BF16),
                   jax.ShapeDtypeStruct((N_HEADS, t_len, HEAD_DIM), BF16),
                   jax.ShapeDtypeStruct((N_HEADS, HEAD_DIM, t_len), BF16)],
        in_specs=[pl.BlockSpec((tm, D_MODEL), row), pl.BlockSpec((tm, 1), row),
                  pl.BlockSpec((1, D_MODEL), fixed2), pl.BlockSpec((1, D_MODEL), fixed2),
                  pl.BlockSpec((IN_COLS_PAD, D_MODEL), fixed2),
                  pl.BlockSpec((1, Q_RANK), fixed2), pl.BlockSpec((1, KV_RANK), fixed2),
                  pl.BlockSpec((N_HEADS, Q_RANK, QK_DIM), fixed3), pl.BlockSpec((N_HEADS, KV_RANK, 2 * HEAD_DIM), fixed3),
                  pl.BlockSpec((1, 128), fixed2)],
        out_specs=[pl.BlockSpec((tm, D_MODEL), row), pl.BlockSpec((tm, HG_COLS), row),
                   pl.BlockSpec((tm, Q_RANK), row), pl.BlockSpec((tm, KV_RANK), row),
                   pl.BlockSpec((N_HEADS, tm, QK_DIM), heads), pl.BlockSpec((N_HEADS, tm, QK_DIM), heads),
                   pl.BlockSpec((N_HEADS, QK_DIM, tm), lambda i: (0, 0, i)),
                   pl.BlockSpec((N_HEADS, tm, HEAD_DIM), heads),
                   pl.BlockSpec((N_HEADS, HEAD_DIM, tm), lambda i: (0, 0, i))],
        params=_params(48, ("arbitrary",)), exchange=exchange,
        first=lambda: pl.program_id(0) == 0, last=lambda: pl.program_id(0) == n_tiles - 1)


def _lower_bound(lb_raw):
    m = jnp.max(lb_raw, axis=0, keepdims=True)
    e = jnp.exp(lb_raw - m)
    return e[0:1] / jnp.sum(e, axis=0, keepdims=True)


def _tri(inclusive_lower):
    r = lax.broadcasted_iota(jnp.int32, (HG_CHUNK, HG_CHUNK), 0)
    c = lax.broadcasted_iota(jnp.int32, (HG_CHUNK, HG_CHUNK), 1)
    return (c <= r) if inclusive_lower else (c >= r)


def _chunk_rows(n):
    return slice(n * HG_CHUNK, (n + 1) * HG_CHUNK)


def _chunk_prefix_sums(v, inclusive_lower):
    tri = _tri(inclusive_lower).astype(BF16)
    hi = v.astype(BF16)
    rest = v - hi.astype(F32)
    mid = rest.astype(BF16)
    lo = (rest - mid.astype(F32)).astype(BF16)
    pieces = jnp.concatenate([hi, mid, lo], axis=1)
    out = []
    for n in range(v.shape[0] // HG_CHUNK):
        s = _dot(tri, pieces[_chunk_rows(n)])
        out.append((s[:, 0:HEAD_DIM] + s[:, HEAD_DIM:2 * HEAD_DIM]) + s[:, 2 * HEAD_DIM:])
    return jnp.concatenate(out, axis=0)


def _per_chunk(v, row):
    n = v.shape[0] // HG_CHUNK
    v3 = v.reshape(n, HG_CHUNK, HEAD_DIM)
    return jnp.broadcast_to(v3[:, row:row + 1, :], v3.shape).reshape(v.shape)


def _hg_block(q, f_logit, lb):
    sg = _sigmoid(f_logit)
    forget = lb + (1.0 - lb) * sg
    kk = 1.0 - forget
    b = _chunk_prefix_sums(jnp.log(forget), True)
    b_ref = _per_chunk(b, HG_CHUNK // 2 - 1)
    b_last = _per_chunk(b, HG_CHUNK - 1)
    e_i = jnp.exp(b - b_ref)
    e_ri = jnp.exp(b_ref - b)
    e_b = jnp.exp(b)
    e_l = jnp.exp(b_last - b)
    return dict(sg=sg, forget=forget, e_i=e_i, e_ri=e_ri, e_b=e_b, e_l=e_l, dec=jnp.exp(b_last),
                qi=q * e_i, ki=kk * e_ri, qe=q * e_b, kl=kk * e_l)


def _hgrn_forward(zhg, lb_raw, norm_w, exchange=None):
    t_len = zhg.shape[0]
    tb = min(512, t_len)
    n_chunks = tb // HG_CHUNK

    def body(q_ref, f_ref, v_ref, g_ref, lb_ref, w_ref, opre_ref, o_ref, st_ref, state):
        @pl.when(pl.program_id(1) == 0)
        def _():
            state[...] = jnp.zeros_like(state)

        blk = _hg_block(q_ref[...], f_ref[...], _lower_bound(lb_ref[...]))
        v = v_ref[...].astype(BF16)
        qi, ki, qe, kl = (blk[name].astype(BF16) for name in ("qi", "ki", "qe", "kl"))
        causal = _tri(True)
        st = state[...]
        parts = []
        for n in range(n_chunks):
            r = _chunk_rows(n)
            a = jnp.where(causal, _dot_nt(qi[r], ki[r]), 0.0).astype(BF16)
            st_ref[0, n] = st
            parts.append(_dot(a, v[r]) + _dot_nt(qe[r], st.astype(BF16)))
            st = st * blk["dec"][n * HG_CHUNK:n * HG_CHUNK + 1] + _dot_tn(v[r], kl[r])
        state[...] = st
        o = jnp.concatenate(parts, axis=0)
        opre_ref[...] = o
        g = g_ref[...]
        o_ref[...] = o * lax.rsqrt(_rowmean(o * o) + RMS_EPS) * w_ref[...] * (g * _sigmoid(g))

    col = lambda off: (lambda h, t: (t, off + h))
    nb = t_len // tb
    return _pallas(
        body, name="hgrn_forward", grid=(N_HEADS, nb), operands=(zhg, zhg, zhg, zhg, lb_raw, norm_w),
        out_shape=[jax.ShapeDtypeStruct((t_len, N_HEADS * HEAD_DIM), F32),
                   jax.ShapeDtypeStruct((t_len, N_HEADS * HEAD_DIM), F32),
                   jax.ShapeDtypeStruct((N_HEADS, t_len // HG_CHUNK, HEAD_DIM, HEAD_DIM), F32)],
        in_specs=[pl.BlockSpec((tb, HEAD_DIM), col(0)), pl.BlockSpec((tb, HEAD_DIM), col(N_HEADS)),
                  pl.BlockSpec((tb, HEAD_DIM), col(2 * N_HEADS)), pl.BlockSpec((tb, HEAD_DIM), col(3 * N_HEADS)),
                  pl.BlockSpec((2, HEAD_DIM), lambda h, t: (0, h)), pl.BlockSpec((1, HEAD_DIM), lambda h, t: (0, h))],
        out_specs=[pl.BlockSpec((tb, HEAD_DIM), col(0)), pl.BlockSpec((tb, HEAD_DIM), col(0)),
                   pl.BlockSpec((1, n_chunks, HEAD_DIM, HEAD_DIM), lambda h, t: (h, t, 0, 0))],
        scratch_shapes=[pltpu.VMEM((HEAD_DIM, HEAD_DIM), F32)],
        params=_params(32, ("arbitrary", "arbitrary")), exchange=exchange,
        first=lambda: (pl.program_id(0) == 0) & (pl.program_id(1) == 0),
        last=lambda: (pl.program_id(0) == N_HEADS - 1) & (pl.program_id(1) == nb - 1))


def _hgrn_backward(zhg, lb_raw, norm_w, o_pre, d_cat, states, exchange=None):
    t_len = zhg.shape[0]
    tb = min(512, t_len)
    n_chunks = tb // HG_CHUNK
    nb = t_len // tb

    def body(q_ref, f_ref, v_ref, g_ref, lb_ref, w_ref, opre_ref, do_ref, st_ref,
             dq_ref, df_ref, dv_ref, dg_ref, sums_ref, gstate):
        @pl.when(pl.program_id(1) == 0)
        def _():
            gstate[...] = jnp.zeros_like(gstate)
            sums_ref[...] = jnp.zeros_like(sums_ref)

        lb = _lower_bound(lb_ref[...])
        w = w_ref[...]
        o = opre_ref[...]
        g = g_ref[...]
        d_out = do_ref[...]
        r = lax.rsqrt(_rowmean(o * o) + RMS_EPS)
        sg_g = _sigmoid(g)
        dg_ref[...] = (d_out * (o * r * w) * (sg_g * (1.0 + g * (1.0 - sg_g)))).astype(BF16)
        d_on = d_out * (g * sg_g)
        sums_ref[1:2, :] += _colsum(d_on * o * r)
        dy = d_on * w
        d_o = (r * dy - o * (r * r * r) * _rowmean(dy * o)).astype(BF16)
        blk = _hg_block(q_ref[...], f_ref[...], lb)
        v = v_ref[...].astype(BF16)
        qi, ki, qe, kl = (blk[name].astype(BF16) for name in ("qi", "ki", "qe", "kl"))
        causal = _tri(True)
        row_id = lax.broadcasted_iota(jnp.int32, (HG_CHUNK, HEAD_DIM), 0)
        gt = gstate[...]
        d_v, d_qi, d_ki, d_qe, d_kl, d_dec = ([None] * n_chunks for _ in range(6))
        for n in reversed(range(n_chunks)):
            rows = _chunk_rows(n)
            st = st_ref[0, n]
            a = jnp.where(causal, _dot_nt(qi[rows], ki[rows]), 0.0).astype(BF16)
            d_a = jnp.where(causal, _dot_nt(d_o[rows], v[rows]), 0.0).astype(BF16)
            gt_b = gt.astype(BF16)
            d_v[n] = _dot_tn(a, d_o[rows]) + _dot_nt(kl[rows], gt_b)
            d_qi[n] = _dot(d_a, ki[rows])
            d_ki[n] = _dot_tn(d_a, qi[rows])
            d_qe[n] = _dot(d_o[rows], st.astype(BF16))
            d_kl[n] = _dot(v[rows], gt_b)
            d_dec[n] = jnp.where(row_id == HG_CHUNK - 1, _colsum(gt * st), 0.0)
            gt = gt * blk["dec"][n * HG_CHUNK:n * HG_CHUNK + 1] + _dot_tn(d_o[rows], qe[rows])
        gstate[...] = gt
        d_qi, d_ki, d_qe, d_kl, d_dec = (jnp.concatenate(p, axis=0) for p in (d_qi, d_ki, d_qe, d_kl, d_dec))
        dv_ref[...] = jnp.concatenate(d_v, axis=0).astype(BF16)
        dq_ref[...] = (d_qi * blk["e_i"] + d_qe * blk["e_b"]).astype(BF16)
        d_k = d_ki * blk["e_ri"] + d_kl * blk["e_l"]
        t_qi = d_qi * blk["qi"]
        t_ki = d_ki * blk["ki"]
        t_kl = d_kl * blk["kl"]
        at_ref, at_last = [], []
        for n in range(n_chunks):
            rows = _chunk_rows(n)
            at_ref.append(jnp.where(row_id == HG_CHUNK // 2 - 1, _colsum(t_ki[rows] - t_qi[rows]), 0.0))
            at_last.append(jnp.where(row_id == HG_CHUNK - 1, _colsum(t_kl[rows]), 0.0))
        d_b = (t_qi - t_ki + d_qe * blk["qe"] - t_kl + jnp.concatenate(at_ref, axis=0)
               + jnp.concatenate(at_last, axis=0) + d_dec * blk["dec"])
        d_forget = _chunk_prefix_sums(d_b, False) / blk["forget"] - d_k
        sg = blk["sg"]
        df_ref[...] = (d_forget * (1.0 - lb) * sg * (1.0 - sg)).astype(BF16)
        sums_ref[0:1, :] += _colsum(d_forget * (1.0 - sg))

    col = lambda off: (lambda h, t: (nb - 1 - t, off + h))
    return _pallas(
        body, name="hgrn_backward", grid=(N_HEADS, nb),
        operands=(zhg, zhg, zhg, zhg, lb_raw, norm_w, o_pre, d_cat, states),
        out_shape=[jax.ShapeDtypeStruct((t_len, N_HEADS * HEAD_DIM), BF16)] * 4
        + [jax.ShapeDtypeStruct((8, N_HEADS * HEAD_DIM), F32)],
        in_specs=[pl.BlockSpec((tb, HEAD_DIM), col(0)), pl.BlockSpec((tb, HEAD_DIM), col(N_HEADS)),
                  pl.BlockSpec((tb, HEAD_DIM), col(2 * N_HEADS)), pl.BlockSpec((tb, HEAD_DIM), col(3 * N_HEADS)),
                  pl.BlockSpec((2, HEAD_DIM), lambda h, t: (0, h)), pl.BlockSpec((1, HEAD_DIM), lambda h, t: (0, h)),
                  pl.BlockSpec((tb, HEAD_DIM), col(0)), pl.BlockSpec((tb, HEAD_DIM), col(0)),
                  pl.BlockSpec((1, n_chunks, HEAD_DIM, HEAD_DIM), lambda h, t: (h, nb - 1 - t, 0, 0))],
        out_specs=[pl.BlockSpec((tb, HEAD_DIM), col(0))] * 4 + [pl.BlockSpec((8, HEAD_DIM), lambda h, t: (0, h))],
        scratch_shapes=[pltpu.VMEM((HEAD_DIM, HEAD_DIM), F32)],
        params=_params(32, ("arbitrary", "arbitrary")), exchange=exchange,
        first=lambda: (pl.program_id(0) == 0) & (pl.program_id(1) == 0),
        last=lambda: (pl.program_id(0) == N_HEADS - 1) & (pl.program_id(1) == nb - 1))


ATT_LOG2 = ATT_SCALE * 1.4426950408889634


def _triangle_steps(nq, q_major):
    if q_major:
        pairs = [(i, j) for i in range(nq) for j in range(i + 1)]
    else:
        pairs = [(i, j) for j in range(nq) for i in range(j, nq)]
    return jnp.array([p[0] for p in pairs], jnp.int32), jnp.array([p[1] for p in pairs], jnp.int32)


def _key_le_query(t):
    return lax.broadcasted_iota(jnp.int32, (t, t), 0) <= lax.broadcasted_iota(jnp.int32, (t, t), 1)


def _attention_forward(q, k, v_t, exchange=None):
    t_len = q.shape[1]
    tq = min(512, t_len)
    nq = t_len // tq
    qi_tab, ki_tab = _triangle_steps(nq, True)

    def body(qi_ref, ki_ref, q_ref, k_ref, vt_ref, o_ref, lse_ref, m_s, l_s, acc_s):
        step = pl.program_id(0)
        qi, ki = qi_ref[step], ki_ref[step]

        @pl.when(ki == 0)
        def _():
            m_s[...] = jnp.full_like(m_s, NEG_BIG)
            l_s[...] = jnp.zeros_like(l_s)
            acc_s[...] = jnp.zeros_like(acc_s)

        def accumulate(masked):
            for h in range(N_HEADS):
                s_t = _dot_nt(k_ref[h], q_ref[h]) * ATT_LOG2
                if masked:
                    s_t = jnp.where(_key_le_query(tq), s_t, NEG_BIG)
                m_old = m_s[h]
                m_new = jnp.maximum(m_old, jnp.max(s_t, axis=0, keepdims=True))
                alpha = jnp.exp2(m_old - m_new)
                p_t = jnp.exp2(s_t - m_new)
                l_s[h] = alpha * l_s[h] + jnp.sum(p_t, axis=0, keepdims=True)
                acc_s[h] = alpha * acc_s[h] + _dot(vt_ref[h], p_t.astype(BF16))
                m_s[h] = m_new

        @pl.when(ki < qi)
        def _():
            accumulate(False)

        @pl.when(ki == qi)
        def _():
            accumulate(True)
            for h in range(N_HEADS):
                o_ref[:, h * HEAD_DIM:(h + 1) * HEAD_DIM] = jnp.transpose(acc_s[h] / l_s[h])
                lse_ref[h] = m_s[h] + jnp.log2(l_s[h])

    n_steps = qi_tab.shape[0]
    return _pallas(
        body, name="attention_forward", grid=(n_steps,), prefetch=(qi_tab, ki_tab), operands=(q, k, v_t),
        out_shape=[jax.ShapeDtypeStruct((t_len, N_HEADS * HEAD_DIM), F32),
                   jax.ShapeDtypeStruct((N_HEADS, 1, t_len), F32)],
        in_specs=[pl.BlockSpec((N_HEADS, tq, QK_DIM), lambda s, qt, kt: (0, qt[s], 0)),
                  pl.BlockSpec((N_HEADS, tq, QK_DIM), lambda s, qt, kt: (0, kt[s], 0)),
                  pl.BlockSpec((N_HEADS, HEAD_DIM, tq), lambda s, qt, kt: (0, 0, kt[s]))],
        out_specs=[pl.BlockSpec((tq, N_HEADS * HEAD_DIM), lambda s, qt, kt: (qt[s], 0)),
                   pl.BlockSpec((N_HEADS, 1, tq), lambda s, qt, kt: (0, 0, qt[s]))],
        scratch_shapes=[pltpu.VMEM((N_HEADS, 1, tq), F32), pltpu.VMEM((N_HEADS, 1, tq), F32),
                        pltpu.VMEM((N_HEADS, HEAD_DIM, tq), F32)],
        params=_params(48, ("arbitrary",)), exchange=exchange,
        first=lambda qt, kt: pl.program_id(0) == 0, last=lambda qt, kt: pl.program_id(0) == n_steps - 1)


BWD_HEADS = 4


def _attention_backward(q, k, k_t, v, d_cat, lse, delta, exchange=None):
    t_len = q.shape[1]
    tq = min(512, t_len)
    nq = t_len // tq
    hp = BWD_HEADS
    qi_tab, ki_tab = _triangle_steps(nq, False)

    def body(qi_ref, ki_ref, q_ref, k_ref, kt_ref, v_ref, do_ref, lse_ref, delta_ref, dqt_hbm, dk_ref, dv_ref,
             dqt_s, dk_s, dv_s):
        group, step = pl.program_id(0), pl.program_id(1)
        qi, ki = qi_ref[step], ki_ref[step]

        @pl.when(step == 0)
        def _():
            dqt_s[...] = jnp.zeros_like(dqt_s)

        @pl.when(qi == ki)
        def _():
            dk_s[...] = jnp.zeros_like(dk_s)
            dv_s[...] = jnp.zeros_like(dv_s)

        def accumulate(masked):
            for h in range(hp):
                do_b = do_ref[:, h * HEAD_DIM:(h + 1) * HEAD_DIM].astype(BF16)
                s_t = _dot_nt(k_ref[h], q_ref[h]) * ATT_LOG2
                if masked:
                    s_t = jnp.where(_key_le_query(tq), s_t, NEG_BIG)
                p_t = jnp.exp2(s_t - lse_ref[h])
                dp_t = _dot_nt(v_ref[h], do_b)
                ds_t = (p_t * (dp_t - delta_ref[h]) * ATT_SCALE).astype(BF16)
                dv_s[h] += _dot(p_t.astype(BF16), do_b)
                dk_s[h] += _dot(ds_t, q_ref[h])
                dqt_s[h, qi] += _dot(kt_ref[h], ds_t)

        @pl.when(ki < qi)
        def _():
            accumulate(False)

        @pl.when(ki == qi)
        def _():
            accumulate(True)
            for h in range(hp):
                pltpu.sync_copy(dqt_s.at[h, qi], dqt_hbm.at[group * hp + h, qi])

        @pl.when(qi == nq - 1)
        def _():
            dk_ref[...] = dk_s[...]
            dv_ref[...] = dv_s[...]

    wide = hp * HEAD_DIM
    n_groups, n_steps = N_HEADS // hp, qi_tab.shape[0]
    return _pallas(
        body, name="attention_backward", grid=(n_groups, n_steps), prefetch=(qi_tab, ki_tab),
        operands=(q, k, k_t, v, d_cat, lse, delta),
        out_shape=[jax.ShapeDtypeStruct((N_HEADS, nq, QK_DIM, tq), F32),
                   jax.ShapeDtypeStruct((N_HEADS, t_len, QK_DIM), F32),
                   jax.ShapeDtypeStruct((N_HEADS, t_len, HEAD_DIM), F32)],
        in_specs=[pl.BlockSpec((hp, tq, QK_DIM), lambda g, s, qt, kt: (g, qt[s], 0)),
                  pl.BlockSpec((hp, tq, QK_DIM), lambda g, s, qt, kt: (g, kt[s], 0)),
                  pl.BlockSpec((hp, QK_DIM, tq), lambda g, s, qt, kt: (g, 0, kt[s])),
                  pl.BlockSpec((hp, tq, HEAD_DIM), lambda g, s, qt, kt: (g, kt[s], 0)),
                  pl.BlockSpec((tq, wide), lambda g, s, qt, kt: (qt[s], n_groups + g)),
                  pl.BlockSpec((hp, 1, tq), lambda g, s, qt, kt: (g, 0, qt[s])),
                  pl.BlockSpec((hp, 1, tq), lambda g, s, qt, kt: (g, 0, qt[s]))],
        out_specs=[pl.BlockSpec(memory_space=pl.ANY),
                   pl.BlockSpec((hp, tq, QK_DIM), lambda g, s, qt, kt: (g, kt[s], 0)),
                   pl.BlockSpec((hp, tq, HEAD_DIM), lambda g, s, qt, kt: (g, kt[s], 0))],
        scratch_shapes=[pltpu.VMEM((hp, nq, QK_DIM, tq), F32), pltpu.VMEM((hp, tq, QK_DIM), F32),
                        pltpu.VMEM((hp, tq, HEAD_DIM), F32)],
        params=_params(58, ("arbitrary", "arbitrary")), exchange=exchange,
        first=lambda qt, kt: (pl.program_id(0) == 0) & (pl.program_id(1) == 0),
        last=lambda qt, kt: (pl.program_id(0) == n_groups - 1) & (pl.program_id(1) == n_steps - 1))


def _out_project(o_hg, o_mla, x, g_a, w_out, exchange=None):
    t_len = x.shape[0]
    tm = min(512, t_len)
    half = N_HEADS * HEAD_DIM

    def body(ohg_ref, omla_ref, x_ref, ga_ref, w_ref, cat_ref, mix_ref, xhat_ref, rstd_ref):
        a = ohg_ref[...].astype(BF16)
        b = omla_ref[...].astype(BF16)
        cat_ref[:, 0:half] = a
        cat_ref[:, half:2 * half] = b
        mix = _dot(a, w_ref[0:half, :]) + _dot(b, w_ref[half:2 * half, :])
        mix_ref[...] = mix
        r1 = DN_ALPHA * x_ref[...] + (1.0 + ga_ref[...]) * mix
        xc = r1 - _rowmean(r1)
        rstd = lax.rsqrt(_rowmean(xc * xc) + LN_EPS)
        xhat_ref[...] = xc * rstd
        rstd_ref[...] = rstd

    row = lambda i: (i, 0)
    fixed = lambda i: (0, 0)
    n_tiles = t_len // tm
    return _pallas(
        body, name="out_project", grid=(n_tiles,), operands=(o_hg, o_mla, x, g_a, w_out),
        out_shape=[jax.ShapeDtypeStruct((t_len, D_MODEL), BF16), jax.ShapeDtypeStruct((t_len, D_MODEL), F32),
                   jax.ShapeDtypeStruct((t_len, D_MODEL), F32), jax.ShapeDtypeStruct((t_len, 1), F32)],
        in_specs=[pl.BlockSpec((tm, half), row), pl.BlockSpec((tm, half), row), pl.BlockSpec((tm, D_MODEL), row),
                  pl.BlockSpec((1, D_MODEL), fixed), pl.BlockSpec((D_MODEL, D_MODEL), fixed)],
        out_specs=[pl.BlockSpec((tm, D_MODEL), row), pl.BlockSpec((tm, D_MODEL), row),
                   pl.BlockSpec((tm, D_MODEL), row), pl.BlockSpec((tm, 1), row)],
        params=_params(48, ("arbitrary",)), exchange=exchange,
        first=lambda: pl.program_id(0) == 0, last=lambda: pl.program_id(0) == n_tiles - 1)


V_LN1G, V_LN1B, V_SCM, V_SHM, V_GM, V_GA, V_LN2G, V_LN2B = range(8)
S_DLN2G, S_DLN2B, S_DGM, S_DSCM, S_DSHM, S_DLN1G, S_DLN1B, S_DGA, S_LOSS = range(9)


def _mlp_and_back(xhat1, rstd1, mix, target, o_mla, vecs, w1_top, w1_bottom, w2, w_out):
    t_len = xhat1.shape[0]
    tm = min(256, t_len)
    n_ff = w1_top.shape[0]
    ff = w1_top.shape[2]
    top_rows = w1_top.shape[1]

    def body(xhat_ref, rstd_ref, mix_ref, tgt_ref, omla_ref, vec_ref, w1_top_hbm, w1_bottom_hbm, w2_hbm, wout_hbm,
             act_ref, dhp_ref, um_ref, dh_ref, dmix_ref, dcat_ref, dr1_ref, sums_ref, delta_ref,
             w1_s, w2_s, wout_s, hp_s, load_sems):
        @pl.when(pl.program_id(0) == 0)
        def _():
            loads = [pltpu.make_async_copy(w1_top_hbm, w1_s.at[:, 0:top_rows], load_sems.at[0]),
                     pltpu.make_async_copy(w1_bottom_hbm, w1_s.at[:, top_rows:D_MODEL], load_sems.at[3]),
                     pltpu.make_async_copy(w2_hbm, w2_s, load_sems.at[1]),
                     pltpu.make_async_copy(wout_hbm, wout_s, load_sems.at[2])]
            for cp in loads:
                cp.start()
            sums_ref[...] = jnp.zeros_like(sums_ref)
            for cp in loads:
                cp.wait()

        vec = lambda r: vec_ref[r:r + 1, :]
        xhat = xhat_ref[...]
        x1 = xhat * vec(V_LN1G) + vec(V_LN1B)
        um = (x1 * (1.0 + vec(V_SCM)) + vec(V_SHM)).astype(BF16)
        um_ref[...] = um
        h = jnp.zeros((tm, D_MODEL), F32)
        for j in range(n_ff):
            hp = _dot(um, w1_s[j])
            hp_s[j] = hp
            act = jnp.square(jnp.maximum(hp, 0.0)).astype(BF16)
            act_ref[:, j * ff:(j + 1) * ff] = act
            h = h + _dot(act, w2_s[j])
        r2 = DN_ALPHA * x1 + (1.0 + vec(V_GM)) * h
        xc = r2 - _rowmean(r2)
        rstd2 = lax.rsqrt(_rowmean(xc * xc) + LN_EPS)
        xhat2 = xc * rstd2
        err = xhat2 * vec(V_LN2G) + vec(V_LN2B) - tgt_ref[...]
        loss = 0.5 * jnp.sum(_rowmean(err * err))
        dy = err * (1.0 / D_MODEL)
        dxh = dy * vec(V_LN2G)
        dr2 = rstd2 * (dxh - _rowmean(dxh) - xhat2 * _rowmean(dxh * xhat2))
        dh = ((1.0 + vec(V_GM)) * dr2).astype(BF16)
        dh_ref[...] = dh
        sums_ref[S_DLN2G:S_DLN2G + 1, :] += _colsum(dy * xhat2)
        sums_ref[S_DLN2B:S_DLN2B + 1, :] += _colsum(dy)
        sums_ref[S_DGM:S_DGM + 1, :] += _colsum(dr2 * h)
        sums_ref[S_LOSS:S_LOSS + 1, :] += jnp.full((1, D_MODEL), loss, F32)
        du = jnp.zeros((tm, D_MODEL), F32)
        for j in range(n_ff):
            dhp = (_dot_nt(dh, w2_s[j]) * (2.0 * jnp.maximum(hp_s[j], 0.0))).astype(BF16)
            dhp_ref[:, j * ff:(j + 1) * ff] = dhp
            du = du + _dot_nt(dhp, w1_s[j])
        sums_ref[S_DSCM:S_DSCM + 1, :] += _colsum(du * x1)
        sums_ref[S_DSHM:S_DSHM + 1, :] += _colsum(du)
        dx1 = DN_ALPHA * dr2 + du * (1.0 + vec(V_SCM))
        sums_ref[S_DLN1G:S_DLN1G + 1, :] += _colsum(dx1 * xhat)
        sums_ref[S_DLN1B:S_DLN1B + 1, :] += _colsum(dx1)
        dxh1 = dx1 * vec(V_LN1G)
        dr1 = rstd_ref[...] * (dxh1 - _rowmean(dxh1) - xhat * _rowmean(dxh1 * xhat))
        dr1_ref[...] = dr1
        sums_ref[S_DGA:S_DGA + 1, :] += _colsum(dr1 * mix_ref[...])
        dmix = ((1.0 + vec(V_GA)) * dr1).astype(BF16)
        dmix_ref[...] = dmix
        dcat = _dot_nt(dmix, wout_s[...])
        dcat_ref[...] = dcat
        ones = jnp.ones((8, HEAD_DIM), F32)
        half = N_HEADS * HEAD_DIM
        for hd in range(N_HEADS):
            prod = dcat[:, half + hd * HEAD_DIM:half + (hd + 1) * HEAD_DIM] * omla_ref[:, hd * HEAD_DIM:(hd + 1) * HEAD_DIM]
            delta_ref[hd] = lax.dot_general(ones, prod, (((1,), (1,)), ((), ())), preferred_element_type=F32,
                                            precision=lax.Precision.HIGHEST)[0:1]

    row = lambda i: (i, 0)
    fixed = lambda i: (0, 0)
    any_spec = pl.BlockSpec(memory_space=pl.ANY)
    return _pcall(
        body, name="mlp_and_back", grid=(t_len // tm,),
        out_shape=[jax.ShapeDtypeStruct((t_len, D_FF), BF16), jax.ShapeDtypeStruct((t_len, D_FF), BF16),
                   jax.ShapeDtypeStruct((t_len, D_MODEL), BF16), jax.ShapeDtypeStruct((t_len, D_MODEL), BF16),
                   jax.ShapeDtypeStruct((t_len, D_MODEL), BF16), jax.ShapeDtypeStruct((t_len, D_MODEL), F32),
                   jax.ShapeDtypeStruct((t_len, D_MODEL), F32), jax.ShapeDtypeStruct((16, D_MODEL), F32),
                   jax.ShapeDtypeStruct((N_HEADS, 1, t_len), F32)],
        in_specs=[pl.BlockSpec((tm, D_MODEL), row), pl.BlockSpec((tm, 1), row), pl.BlockSpec((tm, D_MODEL), row),
                  pl.BlockSpec((tm, D_MODEL), row), pl.BlockSpec((tm, N_HEADS * HEAD_DIM), row),
                  pl.BlockSpec((8, D_MODEL), fixed), any_spec, any_spec, any_spec, any_spec],
        out_specs=[pl.BlockSpec((tm, D_FF), row), pl.BlockSpec((tm, D_FF), row), pl.BlockSpec((tm, D_MODEL), row),
                   pl.BlockSpec((tm, D_MODEL), row), pl.BlockSpec((tm, D_MODEL), row), pl.BlockSpec((tm, D_MODEL), row),
                   pl.BlockSpec((tm, D_MODEL), row), pl.BlockSpec((16, D_MODEL), fixed),
                   pl.BlockSpec((N_HEADS, 1, tm), lambda i: (0, 0, i))],
        scratch_shapes=[pltpu.VMEM((n_ff, D_MODEL, ff), BF16), pltpu.VMEM(w2.shape, BF16), pltpu.VMEM(w_out.shape, BF16),
                        pltpu.VMEM((n_ff, tm, ff), F32), pltpu.SemaphoreType.DMA((4,))],
        compiler_params=_params(56, ("arbitrary",)),
        operands=(xhat1, rstd1, mix, target, o_mla, vecs, w1_top, w1_bottom, w2, w_out))


def _in_project_backward(dq, dk, dv, cq, ckv, pos, invf, q_norm_w, kv_norm_w, w_q, w_kv,
                         d_hq, d_hf, d_hi, d_hg, w_in, dr1, x, sc_a, exchange=None):
    t_len = x.shape[0]
    tm = min(512, t_len)
    per_q = dq.shape[3] // tm
    hgw = N_HEADS * HEAD_DIM

    def body(dq_ref, dk_ref, dv_ref, cq_ref, ckv_ref, pos_ref, invf_ref, qn_ref, kvn_ref, wq_ref, wkv_ref,
             dhq_ref, dhf_ref, dhi_ref, dhg_ref, win_ref, dr1_ref, x_ref, sc_ref,
             dz_ref, dqf_ref, dkvu_ref, cqn_ref, ckvn_ref, gx_ref, sums_ref):
        @pl.when(pl.program_id(0) == 0)
        def _():
            sums_ref[...] = jnp.zeros_like(sums_ref)

        cos_t, sin_t = _rope_tables(pos_ref[...], invf_ref[...])
        cq = cq_ref[...]
        ckv = ckv_ref[...]
        rq = lax.rsqrt(_rowmean(cq * cq) + RMS_EPS)
        rkv = lax.rsqrt(_rowmean(ckv * ckv) + RMS_EPS)
        cqn_ref[...] = (cq * rq * qn_ref[...]).astype(BF16)
        ckvn_ref[...] = (ckv * rkv * kvn_ref[...]).astype(BF16)
        d_cqn = jnp.zeros((tm, Q_RANK), F32)
        d_ckvn = jnp.zeros((tm, KV_RANK), F32)
        d_kpe = jnp.zeros((tm, 128), F32)
        for h in range(N_HEADS):
            dqh = jnp.transpose(dq_ref[h])
            dqf_ref[h, :, 0:HEAD_DIM] = dqh[:, :HEAD_DIM].astype(BF16)
            dqf_ref[h, :, HEAD_DIM:QK_DIM] = _unrope(dqh[:, HEAD_DIM:], cos_t, sin_t).astype(BF16)
            d_cqn = d_cqn + _dot_nt(dqf_ref[h], wq_ref[h])
            dkh = dk_ref[h]
            d_kpe = d_kpe + dkh[:, HEAD_DIM:]
            dkvu_ref[h, :, 0:HEAD_DIM] = dkh[:, :HEAD_DIM].astype(BF16)
            dkvu_ref[h, :, HEAD_DIM:2 * HEAD_DIM] = dv_ref[h].astype(BF16)
            d_ckvn = d_ckvn + _dot_nt(dkvu_ref[h], wkv_ref[h])
        dyq = d_cqn * qn_ref[...]
        dykv = d_ckvn * kvn_ref[...]
        sums_ref[2:3, 0:Q_RANK] += _colsum(d_cqn * cq * rq)
        sums_ref[3:4, 0:KV_RANK] += _colsum(d_ckvn * ckv * rkv)
        dz_ref[:, 0:hgw] = dhq_ref[...]
        dz_ref[:, hgw:2 * hgw] = dhf_ref[...]
        dz_ref[:, 2 * hgw:3 * hgw] = dhi_ref[...]
        dz_ref[:, 3 * hgw:4 * hgw] = dhg_ref[...]
        dz_ref[:, HG_COLS:HG_COLS + Q_RANK] = (rq * dyq - cq * (rq * rq * rq) * _rowmean(dyq * cq)).astype(BF16)
        dz_ref[:, HG_COLS + Q_RANK:HG_COLS + Q_RANK + KV_RANK] = (
            rkv * dykv - ckv * (rkv * rkv * rkv) * _rowmean(dykv * ckv)).astype(BF16)
        dz_ref[:, HG_COLS + Q_RANK + KV_RANK:] = _unrope(d_kpe, cos_t, sin_t).astype(BF16)
        du = _dot(dz_ref[...], win_ref[...])
        xv = x_ref[...]
        gx_ref[...] = DN_ALPHA * dr1_ref[...] + (1.0 + sc_ref[...]) * du
        sums_ref[0:1, :] += _colsum(du * xv)
        sums_ref[1:2, :] += _colsum(du)

    row = lambda i: (i, 0)
    fixed2 = lambda i: (0, 0)
    fixed3 = lambda i: (0, 0, 0)
    heads = lambda i: (0, i, 0)
    n_tiles = t_len // tm
    return _pallas(
        body, name="in_project_backward", grid=(n_tiles,),
        operands=(dq, dk, dv, cq, ckv, pos, invf, q_norm_w, kv_norm_w, w_q, w_kv, d_hq, d_hf, d_hi, d_hg, w_in, dr1, x,
                  sc_a),
        out_shape=[jax.ShapeDtypeStruct((t_len, IN_COLS_PAD), BF16), jax.ShapeDtypeStruct((N_HEADS, t_len, QK_DIM), BF16),
                   jax.ShapeDtypeStruct((N_HEADS, t_len, 2 * HEAD_DIM), BF16), jax.ShapeDtypeStruct((t_len, Q_RANK), BF16),
                   jax.ShapeDtypeStruct((t_len, KV_RANK), BF16), jax.ShapeDtypeStruct((t_len, D_MODEL), F32),
                   jax.ShapeDtypeStruct((8, D_MODEL), F32)],
        in_specs=[pl.BlockSpec((N_HEADS, None, QK_DIM, tm), lambda i: (0, i // per_q, 0, i % per_q)),
                  pl.BlockSpec((N_HEADS, tm, QK_DIM), heads),
                  pl.BlockSpec((N_HEADS, tm, HEAD_DIM), heads), pl.BlockSpec((tm, Q_RANK), row),
                  pl.BlockSpec((tm, KV_RANK), row), pl.BlockSpec((tm, 1), row), pl.BlockSpec((1, 128), fixed2),
                  pl.BlockSpec((1, Q_RANK), fixed2), pl.BlockSpec((1, KV_RANK), fixed2),
                  pl.BlockSpec((N_HEADS, Q_RANK, QK_DIM), fixed3), pl.BlockSpec((N_HEADS, KV_RANK, 2 * HEAD_DIM), fixed3),
                  pl.BlockSpec((tm, hgw), row), pl.BlockSpec((tm, hgw), row), pl.BlockSpec((tm, hgw), row),
                  pl.BlockSpec((tm, hgw), row), pl.BlockSpec((IN_COLS_PAD, D_MODEL), fixed2),
                  pl.BlockSpec((tm, D_MODEL), row), pl.BlockSpec((tm, D_MODEL), row), pl.BlockSpec((1, D_MODEL), fixed2)],
        out_specs=[pl.BlockSpec((tm, IN_COLS_PAD), row), pl.BlockSpec((N_HEADS, tm, QK_DIM), heads),
                   pl.BlockSpec((N_HEADS, tm, 2 * HEAD_DIM), heads), pl.BlockSpec((tm, Q_RANK), row),
                   pl.BlockSpec((tm, KV_RANK), row), pl.BlockSpec((tm, D_MODEL), row), pl.BlockSpec((8, D_MODEL), fixed2)],
        params=_params(48, ("arbitrary",)), exchange=exchange,
        first=lambda: pl.program_id(0) == 0, last=lambda: pl.program_id(0) == n_tiles - 1)


def _weight_grad(a, b, name, n_blocks, bn, a_blocked=False, b_blocked=True, exchange=None, token_tile=512):
    t_len = a.shape[0]
    m = a.shape[1] // n_blocks if a_blocked else a.shape[1]
    bt = min(token_tile, t_len)

    def body(a_ref, b_ref, o_ref):
        @pl.when(pl.program_id(1) == 0)
        def _():
            o_ref[...] = jnp.zeros_like(o_ref)

        o_ref[...] += _dot_tn(a_ref[...].astype(BF16), b_ref[...].astype(BF16))

    a_spec = pl.BlockSpec((bt, m), (lambda n, t: (t, n)) if a_blocked else (lambda n, t: (t, 0)))
    if b.ndim == 3:
        b_spec = pl.BlockSpec((None, bt, bn), lambda n, t: (n, t, 0))
    else:
        b_spec = pl.BlockSpec((bt, bn), (lambda n, t: (t, n)) if b_blocked else (lambda n, t: (t, 0)))
    nt = t_len // bt
    (out,), landed = _pallas(
        body, name=name, grid=(n_blocks, nt), operands=(a, b),
        out_shape=[jax.ShapeDtypeStruct((n_blocks, m, bn), F32)],
        in_specs=[a_spec, b_spec],
        out_specs=[pl.BlockSpec((None, m, bn), lambda n, t: (n, 0, 0))],
        params=_params(56, ("arbitrary", "arbitrary")), exchange=exchange,
        first=lambda: (pl.program_id(0) == 0) & (pl.program_id(1) == 0),
        last=lambda: (pl.program_id(0) == n_blocks - 1) & (pl.program_id(1) == nt - 1))
    return (out, landed) if exchange else out


def _reduce_small(gathered, lb_raw):
    def body(g_ref, lb_ref, tot_ref, dlb_ref):
        tot = g_ref[0]
        for d in range(1, N_DEV):
            tot = tot + g_ref[d]
        tot_ref[...] = tot
        a = lb_ref[...]
        m = jnp.max(a, axis=0, keepdims=True)
        e = jnp.exp(a - m)
        lb = e[0:1] / jnp.sum(e, axis=0, keepdims=True)
        d0 = tot[10:11, 0:512] * lb * (1.0 - lb)
        dlb_ref[0:1, :] = d0
        dlb_ref[1:2, :] = -d0

    return pl.pallas_call(
        body, name="reduce_small",
        out_shape=[jax.ShapeDtypeStruct((SMALL_ROWS, D_MODEL), F32), jax.ShapeDtypeStruct((2, 512), F32)],
    )(gathered, lb_raw)


def _adamw_update(w, gv, m, v):
    nm = ADAM_B1 * m + (1.0 - ADAM_B1) * gv
    nv = ADAM_B2 * v + (1.0 - ADAM_B2) * jnp.square(gv)
    m_hat = nm / (1.0 - ADAM_B1 ** ADAM_STEP)
    v_hat = nv / (1.0 - ADAM_B2 ** ADAM_STEP)
    return -ADAM_LR * (m_hat / (jnp.sqrt(v_hat) + ADAM_EPS) + ADAM_WD * w), nm, nv


def _adamw_halves(core, w, mine, theirs, m, v, name):
    rows, cols = w.shape
    h = rows // 2
    tr = _row_tile(h)
    per_half = h // tr

    def body(core_ref, w_ref, mine_ref, theirs_ref, m_ref, v_ref, g_ref, d_ref, nm_ref, nv_ref):
        is_mine = pl.program_id(0) // per_half == core_ref[0]
        gv = jnp.where(is_mine, mine_ref[...], theirs_ref[...])
        g_ref[...] = gv
        d_ref[...], nm_ref[...], nv_ref[...] = _adamw_update(w_ref[...], gv, m_ref[...], v_ref[...])

    full = pl.BlockSpec((tr, cols), lambda i, core_ref: (i, 0))
    part = pl.BlockSpec((tr, cols), lambda i, core_ref: (i % per_half, 0))
    return _pcall(
        body, name=name, out_shape=[jax.ShapeDtypeStruct(w.shape, F32)] * 4,
        grid_spec=pltpu.PrefetchScalarGridSpec(
            num_scalar_prefetch=1, grid=(rows // tr,), in_specs=[full, part, part, full, full], out_specs=[full] * 4),
        compiler_params=_params(40, ("arbitrary",)),
        operands=(core, w, mine, theirs, m, v))


def _adamw(w, g, m, v, name):
    rows, cols = w.shape
    tr = _row_tile(rows) if rows >= 8 else rows

    def body(w_ref, g_ref, m_ref, v_ref, d_ref, nm_ref, nv_ref):
        d_ref[...], nm_ref[...], nv_ref[...] = _adamw_update(w_ref[...], g_ref[...], m_ref[...], v_ref[...])

    spec = pl.BlockSpec((tr, cols), lambda i: (i, 0))
    return _pcall(
        body, name=name, grid=(rows // tr,),
        out_shape=[jax.ShapeDtypeStruct(w.shape, F32)] * 3,
        in_specs=[spec] * 4, out_specs=[spec] * 3,
        compiler_params=_params(40, ("arbitrary",)),
        operands=(w, g, m, v))


def kernel(x, c, positions, w_ada, b_ada, w_in, hg_lower_bounds, hg_norm_w, mla_q_norm_w, w_q_up, mla_kv_norm_w, w_kv_up, w_out, ln1_g, ln1_b, w_mlp_in, w_mlp_out, ln2_g, ln2_b, loss_target, m_w_ada, m_b_ada, m_w_in, m_hg_lower_bounds, m_hg_norm_w, m_mla_q_norm_w, m_w_q_up, m_mla_kv_norm_w, m_w_kv_up, m_w_out, m_ln1_g, m_ln1_b, m_w_mlp_in, m_w_mlp_out, m_ln2_g, m_ln2_b, v_w_ada, v_b_ada, v_w_in, v_hg_lower_bounds, v_hg_norm_w, v_mla_q_norm_w, v_w_q_up, v_mla_kv_norm_w, v_w_kv_up, v_w_out, v_ln1_g, v_ln1_b, v_w_mlp_in, v_w_mlp_out, v_ln2_g, v_ln2_b):
    ix, iy, ic = _mesh_pos()
    chip = 2 * ix + iy
    me = 4 * ix + 2 * iy + ic
    core_arr = jnp.reshape(ic, (1,)).astype(jnp.int32)
    chip_arr = jnp.reshape(chip, (1,)).astype(jnp.int32)

    xs = x[0]
    target = loss_target[0]
    t_len = xs.shape[0]
    pos = positions.astype(F32).reshape(t_len, 1)
    inv = 1.0 / (ROPE_THETA ** (jnp.arange(0, ROPE_DIM, 2, dtype=F32) / ROPE_DIM))
    invf = jnp.concatenate([inv, inv, jnp.zeros((128 - ROPE_DIM,), F32)]).reshape(1, 128)

    def slot(w):
        rows, cols = w.shape
        own = w.astype(BF16).reshape(1, 2, rows // 2, cols)
        return lax.dynamic_update_slice(jnp.zeros((N_CHIPS, 2, rows // 2, cols), BF16), own, (chip, 0, 0, 0))

    def slot8(a):
        return lax.dynamic_update_slice(jnp.zeros((N_DEV,) + a.shape, a.dtype), a[None], (me, 0, 0))

    def whole(s):
        return s.reshape(N_CHIPS, 2 * s.shape[2], s.shape[3])

    def halved(g):
        return g.reshape(N_CHIPS, 2, g.shape[1] // 2, g.shape[2])

    ada_cols = w_ada.shape[2]
    c_all, *early = _run_exchange(
        _merge(_gather_all(slot8(jnp.broadcast_to(c, (8, D_MODEL)))),
               _gather_over_ici([slot(jnp.transpose(w_in[0])), slot(w_q_up[0]), slot(w_kv_up[0])])),
        "gather_c_and_mixer_weights_ici")
    b_shard = lax.dynamic_slice(b_ada, (0, chip * ada_cols), (1, ada_cols))
    mod_cols, cond16 = _ada_project(c_all[:, 0, :], w_ada[0], b_shard)
    mod_all, *early = _run_exchange(_merge(_gather_all(slot8(mod_cols)), _gather_over_d2d(early)),
                                    "gather_mod_and_mixer_weights_d2d")
    mod_mine = lax.dynamic_slice(mod_all, (0, me, 0), (N_DEV, 1, ada_cols))[::2, 0, :].reshape(6, D_MODEL)
    sh_a, sc_a, g_a, sh_m, sc_m, g_m = (mod_mine[i:i + 1] for i in range(6))
    g_in, g_q, g_kv = (whole(s) for s in early)
    w_in_full = jnp.pad(g_in.reshape(IN_COLS, D_MODEL), ((0, IN_COLS_PAD - IN_COLS), (0, 0)))
    w_q_full = jnp.pad(g_q, ((0, 0), (0, 0), (0, QK_DIM - g_q.shape[2])))

    w1_rows = D_MODEL // 2
    (u_a, zhg, cq, ckv, q, k, k_t, v, v_t), (s_top,) = _in_project(
        xs, pos, sc_a, sh_a, w_in_full, mla_q_norm_w, mla_kv_norm_w, w_q_full, g_kv, invf,
        _gather_over_ici([slot(w_mlp_in[0, :w1_rows])]))
    (o_pre, o_hg, states), (s_out, s_bottom, s_top) = _hgrn_forward(
        zhg, hg_lower_bounds, hg_norm_w,
        _merge(_gather_over_ici([slot(w_out[0]), slot(w_mlp_in[0, w1_rows:])]), _gather_over_d2d([s_top])))
    (o_mla, lse), (s_w2, s_out, s_bottom) = _attention_forward(
        q, k, v_t, _merge(_gather_over_ici([slot(w_mlp_out[0])]), _gather_over_d2d([s_out, s_bottom])))
    w_out_full = whole(s_out).reshape(D_MODEL, D_MODEL)
    (cat, mix, xhat1, rstd1), (s_w2,) = _out_project(o_hg, o_mla, xs, g_a, w_out_full, _gather_over_d2d([s_w2]))
    g_w1_top, g_w1_bottom, g_w2 = whole(s_top), whole(s_bottom), whole(s_w2)
    vecs = jnp.concatenate([ln1_g, ln1_b, sc_m, sh_m, g_m, g_a, ln2_g, ln2_b], axis=0)
    act, dhp, um, dh, dmix, d_cat, dr1, mlp_sums, delta = _mlp_and_back(
        xhat1, rstd1, mix, target, o_mla, vecs, g_w1_top, g_w1_bottom, g_w2, w_out_full)

    gw_1 = _weight_grad(um, dhp, "grad_w_mlp_in", N_CHIPS, D_FF // N_CHIPS, token_tile=4096)
    gw_2 = _weight_grad(act, dh, "grad_w_mlp_out", N_CHIPS, D_MODEL, a_blocked=True, b_blocked=False, token_tile=4096)
    gw_out = _weight_grad(cat, dmix, "grad_w_out", 1, D_MODEL, token_tile=2048)
    gw_out = gw_out.reshape(N_CHIPS, D_MODEL // N_CHIPS, D_MODEL)
    mlp_grads = [halved(gw_1), halved(gw_2), halved(gw_out)]
    (dq, dk, dv), landed = _attention_backward(q, k, k_t, v, d_cat, lse, delta, _pair_exchange(mlp_grads))
    chip_sums = [_add_pair(core_arr, g, l) for g, l in zip(mlp_grads, landed)]
    (d_hq, d_hf, d_hi, d_hg, hg_sums), landed = _hgrn_backward(
        zhg, hg_lower_bounds, hg_norm_w, o_pre, d_cat, states, _chip_exchange([b for _, b in chip_sums]))
    mlp_mine = [_add_chips(chip_arr, p, l) for (p, _), l in zip(chip_sums, landed)]
    (dz, dqf, dkvu, cqn, ckvn, grad_x, in_sums), _ = _in_project_backward(
        dq, dk, dv, cq, ckv, pos, invf, mla_q_norm_w, mla_kv_norm_w, w_q_full, g_kv,
        d_hq, d_hf, d_hi, d_hg, w_in_full, dr1, xs, sc_a)

    gw_in, mlp_theirs = _weight_grad(dz, u_a, "grad_w_in", 3, D_MODEL, a_blocked=True, b_blocked=False,
                                     exchange=_pair_send(mlp_mine), token_tile=4096)
    gw_in = gw_in.reshape(IN_COLS_PAD, D_MODEL)
    gw_q = _weight_grad(cqn, dqf, "grad_w_q_up", N_HEADS, QK_DIM, token_tile=2048)[:, :, :HEAD_DIM + ROPE_DIM]
    gw_kv = _weight_grad(ckvn, dkvu, "grad_w_kv_up", N_HEADS, 2 * HEAD_DIM, token_tile=2048)
    flat = lambda g: g.reshape(g.shape[0] * g.shape[1], g.shape[2])
    mixer_mine, mixer_theirs = _reduce_in_vmem(
        [gw_in, flat(gw_q), flat(gw_kv)], [IN_COLS // N_CHIPS // 2, Q_RANK // 2, KV_RANK // 2], "reduce_mixer_grads")
    reduced = ("w_in", "w_q_up", "w_kv_up", "w_mlp_in", "w_mlp_out", "w_out")
    halves_mine = dict(zip(reduced, list(mixer_mine) + mlp_mine))
    halves_theirs = dict(zip(reduced, list(mixer_theirs) + list(mlp_theirs)))

    zeros = lambda n: jnp.zeros((1, n), F32)
    small = jnp.concatenate([
        in_sums[1:2], in_sums[0:1], mlp_sums[S_DGA:S_DGA + 1],
        mlp_sums[S_DSHM:S_DSHM + 1], mlp_sums[S_DSCM:S_DSCM + 1], mlp_sums[S_DGM:S_DGM + 1],
        mlp_sums[S_DLN1G:S_DLN1G + 1], mlp_sums[S_DLN1B:S_DLN1B + 1],
        mlp_sums[S_DLN2G:S_DLN2G + 1], mlp_sums[S_DLN2B:S_DLN2B + 1],
        jnp.concatenate([hg_sums[0:1], hg_sums[1:2]], axis=1),
        jnp.concatenate([in_sums[2:3, :Q_RANK], in_sums[3:4, :KV_RANK], zeros(D_MODEL - Q_RANK - KV_RANK)], axis=1),
        mlp_sums[S_LOSS:S_LOSS + 1],
        jnp.zeros((SMALL_ROWS - 13, D_MODEL), F32)], axis=0)
    small_all = _allgather8(small, "gather_small")
    tot, g_lb = _reduce_small(small_all, hg_lower_bounds)
    loss = tot[12, 0]
    g_b_ada = tot[0:6].reshape(1, 6 * D_MODEL)
    g_ln1_g, g_ln1_b, g_ln2_g, g_ln2_b = tot[6:7], tot[7:8], tot[8:9], tot[9:10]
    g_hg_norm = tot[10:11, 512:1024]
    g_q_norm = tot[11:12, 0:Q_RANK]
    g_kv_norm = tot[11:12, Q_RANK:Q_RANK + KV_RANK]

    d_mod_all = small_all[:, 0:6, :].reshape(N_DEV, 6 * D_MODEL)
    d_mod_cols = lax.dynamic_slice(d_mod_all, (0, chip * ada_cols), (N_DEV, ada_cols))
    d_mod_cols = jnp.concatenate([d_mod_cols, jnp.zeros_like(d_mod_cols)], axis=0)
    g_w_ada = _weight_grad(cond16, d_mod_cols, "grad_w_ada", 1, ada_cols)[0]

    names = ["w_ada", "b_ada", "w_in", "hg_lower_bounds", "hg_norm_w", "mla_q_norm_w", "w_q_up", "mla_kv_norm_w",
             "w_kv_up", "w_out", "ln1_g", "ln1_b", "w_mlp_in", "w_mlp_out", "ln2_g", "ln2_b"]
    weights = [w_ada, b_ada, w_in, hg_lower_bounds, hg_norm_w, mla_q_norm_w, w_q_up, mla_kv_norm_w,
               w_kv_up, w_out, ln1_g, ln1_b, w_mlp_in, w_mlp_out, ln2_g, ln2_b]
    moms = [m_w_ada, m_b_ada, m_w_in, m_hg_lower_bounds, m_hg_norm_w, m_mla_q_norm_w, m_w_q_up, m_mla_kv_norm_w,
            m_w_kv_up, m_w_out, m_ln1_g, m_ln1_b, m_w_mlp_in, m_w_mlp_out, m_ln2_g, m_ln2_b]
    vels = [v_w_ada, v_b_ada, v_w_in, v_hg_lower_bounds, v_hg_norm_w, v_mla_q_norm_w, v_w_q_up, v_mla_kv_norm_w,
            v_w_kv_up, v_w_out, v_ln1_g, v_ln1_b, v_w_mlp_in, v_w_mlp_out, v_ln2_g, v_ln2_b]
    grads2d = [g_w_ada, g_b_ada, None, g_lb, g_hg_norm, g_q_norm, None, g_kv_norm,
               None, None, g_ln1_g, g_ln1_b, None, None, g_ln2_g, g_ln2_b]
    out_g, out_d, out_m, out_v = [], [], [], []
    for name, w, g, m, vv in zip(names, weights, grads2d, moms, vels):
        if name == "w_in":
            to2d, back = (lambda a: jnp.transpose(a[0])), (lambda a: jnp.transpose(a)[None])
        else:
            shape2 = w.shape[1:] if g is None else g.shape
            to2d, back = (lambda a, s=shape2: a.reshape(s)), (lambda a, s=w.shape: a.reshape(s))
        if g is None:
            g, d, nm, nv = _adamw_halves(core_arr, to2d(w), halves_mine[name], halves_theirs[name], to2d(m), to2d(vv),
                                         "adamw_" + name)
        else:
            d, nm, nv = _adamw(to2d(w), g, to2d(m), to2d(vv), "adamw_" + name)
        out_g.append(back(g))
        out_d.append(back(d))
        out_m.append(back(nm))
        out_v.append(back(nv))
    return (loss, grad_x[None], *out_g, *out_d, *out_m, *out_v)
```

```python
import functools

import jax
import jax.numpy as jnp
from jax import lax
from jax.experimental import pallas as pl
from jax.experimental.pallas import tpu as pltpu

F32 = jnp.float32
BF16 = jnp.bfloat16
MESH_IDS = pl.DeviceIdType.MESH

D_MODEL = 1024
N_HEADS = 4
HEAD_DIM = 128
ROPE_DIM = 64
HG_CHUNK = 64
HG_COLS = 2048
Q_RANK = 256
KV_RANK = 256
IN_COLS = 2624
IN_COLS_PAD = 2688
QK_DIM = 256
D_FF = 4096
N_CHIPS = 4
N_DEV = 8
ROPE_THETA = 10000.0
RMS_EPS = 1e-6
LN_EPS = 1e-5
DN_ALPHA = 2.0 ** 0.25
ATT_SCALE = (HEAD_DIM + ROPE_DIM) ** -0.5
NEG_BIG = -1e30
ADAM_LR = 0.001
ADAM_B1 = 0.9
ADAM_B2 = 0.999
ADAM_EPS = 1e-08
ADAM_WD = 0.01
ADAM_STEP = 10
SMALL_ROWS = 16
MIB = 1024 * 1024


def _dot(a, b):
    return jnp.dot(a, b, preferred_element_type=F32)


def _dot_nt(a, b):
    return lax.dot_general(a, b, (((1,), (1,)), ((), ())), preferred_element_type=F32)


def _dot_tn(a, b):
    return lax.dot_general(a, b, (((0,), (0,)), ((), ())), preferred_element_type=F32)


def _params(vmem_mib, semantics=None):
    return pltpu.CompilerParams(vmem_limit_bytes=vmem_mib * MIB, dimension_semantics=semantics)


def _sigmoid(v):
    return 1.0 / (1.0 + jnp.exp(-v))


def _colsum(v):
    return jnp.sum(v, axis=0, keepdims=True)


def _rowmean(v):
    return jnp.mean(v, axis=-1, keepdims=True)


def _rope_tables(pos, invf):
    ang = pos * invf
    lane = lax.broadcasted_iota(jnp.int32, ang.shape, 1)
    cos_t = jnp.where(lane < ROPE_DIM, jnp.cos(ang), 0.0)
    sin = jnp.sin(ang)
    sin_t = jnp.where(lane < ROPE_DIM // 2, -sin, jnp.where(lane < ROPE_DIM, sin, 0.0))
    return cos_t, sin_t


def _swap_halves(t):
    lane = lax.broadcasted_iota(jnp.int32, t.shape, 1)
    return jnp.where(lane < ROPE_DIM // 2, pltpu.roll(t, 128 - ROPE_DIM // 2, 1), pltpu.roll(t, ROPE_DIM // 2, 1))


def _rope(t, cos_t, sin_t):
    return t * cos_t + _swap_halves(t) * sin_t


def _unrope(g, cos_t, sin_t):
    return g * cos_t - _swap_halves(g) * sin_t


def _mesh_pos():
    return lax.axis_index("x"), lax.axis_index("y"), lax.axis_index("c")


def _other_chips(x, y):
    out = []
    for dx, dy in ((1, 0), (0, 1), (1, 1)):
        px = 1 - x if dx else x
        py = 1 - y if dy else y
        out.append(((px, py), 2 * px + py))
    return out


def _allgather8(a, name):
    rows, cols = a.shape

    def body(a_ref, out_ref, send_sems, recv_sems):
        x, y, c = _mesh_pos()
        me = 4 * x + 2 * y + c
        out_ref[me] = a_ref[...]
        peers = []
        for r in range(1, N_DEV):
            px = 1 - x if r & 4 else x
            py = 1 - y if r & 2 else y
            pc = 1 - c if r & 1 else c
            peers.append(((px, py, pc), 4 * px + 2 * py + pc))

        def copy(r, block, to):
            return pltpu.make_async_remote_copy(
                src_ref=a_ref, dst_ref=out_ref.at[block], send_sem=send_sems.at[r], recv_sem=recv_sems.at[r],
                device_id=to, device_id_type=MESH_IDS)

        sends = [copy(r, me, peer) for r, (peer, _) in enumerate(peers)]
        for cp in sends:
            cp.start()
        for r, (peer, idx) in enumerate(peers):
            copy(r, idx, peer).wait_recv()
        for cp in sends:
            cp.wait_send()

    return pl.pallas_call(
        body, name=name,
        out_shape=jax.ShapeDtypeStruct((N_DEV, rows, cols), a.dtype),
        in_specs=[pl.BlockSpec(memory_space=pltpu.VMEM)],
        out_specs=pl.BlockSpec(memory_space=pltpu.VMEM),
        scratch_shapes=[pltpu.SemaphoreType.DMA((N_DEV - 1,)), pltpu.SemaphoreType.DMA((N_DEV - 1,))],
    )(a)


class _Exchange:
    def __init__(self, inputs, out_shapes, aliases, sems, start, finish):
        self.inputs, self.out_shapes, self.aliases, self.sems = list(inputs), list(out_shapes), dict(aliases), list(sems)
        self.start, self.finish = start, finish


def _from_copies(inputs, out_shapes, aliases, sems, copies):
    def start(ins, outs, sem_refs):
        for send, _ in copies(ins, outs, sem_refs):
            send.start()

    def finish(ins, outs, sem_refs):
        for send, recv in copies(ins, outs, sem_refs):
            recv.wait_recv()
            send.wait_send()

    return _Exchange(inputs, out_shapes, aliases, sems, start, finish)


HBM_MIN_BYTES = 256 * 1024


def _in_hbm(a):
    if a.size * a.dtype.itemsize < HBM_MIN_BYTES:
        return a
    return pltpu.with_memory_space_constraint(a, pltpu.HBM)


def _out_hbm(s):
    if s.size * s.dtype.itemsize < HBM_MIN_BYTES:
        return s
    return pltpu.HBM(s.shape, s.dtype)


def _pcall(body, *, operands, out_shape, **kwargs):
    single = not isinstance(out_shape, (list, tuple))
    shapes = [_out_hbm(s) for s in ([out_shape] if single else out_shape)]
    return pl.pallas_call(body, out_shape=shapes[0] if single else shapes, **kwargs)(*[_in_hbm(a) for a in operands])


def _run_exchange(exchange, name):
    n_in, n_out = len(exchange.inputs), len(exchange.out_shapes)

    def body(*refs):
        ins, outs, sem_refs = refs[:n_in], refs[n_in:n_in + n_out], refs[n_in + n_out:]
        exchange.start(ins, outs, sem_refs)
        exchange.finish(ins, outs, sem_refs)

    any_spec = pl.BlockSpec(memory_space=pl.ANY)
    return pl.pallas_call(
        body, name=name, out_shape=[_out_hbm(s) for s in exchange.out_shapes],
        in_specs=[any_spec] * n_in, out_specs=[any_spec] * n_out,
        scratch_shapes=exchange.sems, input_output_aliases=exchange.aliases,
    )(*[_in_hbm(a) for a in exchange.inputs])


def _pallas(body, *, name, operands, in_specs, out_shape, out_specs, params, scratch_shapes=(), grid=(), prefetch=(),
            exchange=None, first=None, last=None):
    n_pre, n_in, n_out, n_scr = len(prefetch), len(in_specs), len(out_specs), len(scratch_shapes)
    ex_in = exchange.inputs if exchange else []
    ex_out = exchange.out_shapes if exchange else []
    ex_sems = exchange.sems if exchange else []

    def full_body(*refs):
        pre, rest = refs[:n_pre], refs[n_pre:]
        ins, rest = rest[:n_in], rest[n_in:]
        xin, rest = rest[:len(ex_in)], rest[len(ex_in):]
        outs, rest = rest[:n_out], rest[n_out:]
        xout, rest = rest[:len(ex_out)], rest[len(ex_out):]
        scr, sem_refs = rest[:n_scr], rest[n_scr:]
        if exchange:
            @pl.when(first(*pre))
            def _():
                exchange.start(xin, xout, sem_refs)

        body(*pre, *ins, *outs, *scr)
        if exchange:
            @pl.when(last(*pre))
            def _():
                exchange.finish(xin, xout, sem_refs)

    any_spec = pl.BlockSpec(memory_space=pl.ANY)
    aliases = {n_pre + n_in + i: n_out + o for i, o in exchange.aliases.items()} if exchange else {}
    operands = [_in_hbm(a) for a in operands]
    results = pl.pallas_call(
        full_body, name=name, out_shape=[_out_hbm(s) for s in list(out_shape) + ex_out],
        grid_spec=pltpu.PrefetchScalarGridSpec(
            num_scalar_prefetch=n_pre, grid=grid, in_specs=list(in_specs) + [any_spec] * len(ex_in),
            out_specs=list(out_specs) + [any_spec] * len(ex_out), scratch_shapes=list(scratch_shapes) + ex_sems),
        input_output_aliases=aliases, compiler_params=params,
    )(*prefetch, *operands, *[_in_hbm(a) for a in ex_in])
    return results[:n_out], results[n_out:]


def _remote(src, dst, sems, idx, to):
    send_sems, recv_sems = sems
    return pltpu.make_async_remote_copy(src_ref=src, dst_ref=dst, send_sem=send_sems.at[idx], recv_sem=recv_sems.at[idx],
                                        device_id=to, device_id_type=MESH_IDS)


def _sem_pairs(*shape):
    return [pltpu.SemaphoreType.DMA(shape), pltpu.SemaphoreType.DMA(shape)]


def _same_shapes(arrays):
    return [jax.ShapeDtypeStruct(a.shape, a.dtype) for a in arrays]


def _gather_over_ici(slots):
    n = len(slots)

    def copies(ins, outs, sems):
        x, y, c = _mesh_pos()
        k = 2 * x + y
        out = []
        for j, (chip, kj) in enumerate(_other_chips(x, y)):
            for i in range(n):
                to = (*chip, c)
                out.append((_remote(ins[i].at[k, c], outs[i].at[k, c], sems, (j, i), to),
                            _remote(ins[i].at[k, c], outs[i].at[kj, c], sems, (j, i), to)))
        return out

    return _from_copies(slots, _same_shapes(slots), {i: i for i in range(n)}, _sem_pairs(3, n), copies)


def _gather_over_d2d(slots):
    n = len(slots)

    def copies(ins, outs, sems):
        x, y, c = _mesh_pos()
        sibling = (x, y, 1 - c)
        out = []
        for j, (_, kj) in enumerate(_other_chips(x, y)):
            for i in range(n):
                out.append((_remote(ins[i].at[kj, c], outs[i].at[kj, c], sems, (j, i), sibling),
                            _remote(ins[i].at[kj, c], outs[i].at[kj, 1 - c], sems, (j, i), sibling)))
        return out

    return _from_copies(slots, _same_shapes(slots), {i: i for i in range(n)}, _sem_pairs(3, n), copies)


def _gather_all(slots8):
    def copies(ins, outs, sems):
        x, y, c = _mesh_pos()
        me = 4 * x + 2 * y + c
        out = []
        for r in range(1, N_DEV):
            px = 1 - x if r & 4 else x
            py = 1 - y if r & 2 else y
            pc = 1 - c if r & 1 else c
            to = (px, py, pc)
            out.append((_remote(ins[0].at[me], outs[0].at[me], sems, r - 1, to),
                        _remote(ins[0].at[me], outs[0].at[4 * px + 2 * py + pc], sems, r - 1, to)))
        return out

    return _from_copies([slots8], _same_shapes([slots8]), {0: 0}, _sem_pairs(N_DEV - 1), copies)


def _merge(first, second):
    n_in, n_out, n_sem = len(first.inputs), len(first.out_shapes), len(first.sems)

    def start(ins, outs, sems):
        first.start(ins[:n_in], outs[:n_out], sems[:n_sem])
        second.start(ins[n_in:], outs[n_out:], sems[n_sem:])

    def finish(ins, outs, sems):
        first.finish(ins[:n_in], outs[:n_out], sems[:n_sem])
        second.finish(ins[n_in:], outs[n_out:], sems[n_sem:])

    aliases = dict(first.aliases)
    aliases.update({n_in + i: n_out + o for i, o in second.aliases.items()})
    return _Exchange(first.inputs + second.inputs, first.out_shapes + second.out_shapes, aliases,
                     first.sems + second.sems, start, finish)


def _pair_exchange(grads):
    n = len(grads)

    def copies(ins, outs, sems):
        x, y, c = _mesh_pos()
        cps = [_remote(ins[i].at[:, 1 - c], outs[i], sems, i, (x, y, 1 - c)) for i in range(n)]
        return [(cp, cp) for cp in cps]

    shapes = [jax.ShapeDtypeStruct((N_CHIPS,) + g.shape[2:], g.dtype) for g in grads]
    return _from_copies(grads, shapes, {}, _sem_pairs(n), copies)


def _chip_exchange(partials):
    n = len(partials)

    def copies(ins, outs, sems):
        x, y, c = _mesh_pos()
        cps = [_remote(ins[i].at[kj], outs[i].at[j], sems, (j, i), (*chip, c))
               for j, (chip, kj) in enumerate(_other_chips(x, y)) for i in range(n)]
        return [(cp, cp) for cp in cps]

    shapes = [jax.ShapeDtypeStruct((3,) + p.shape[1:], p.dtype) for p in partials]
    return _from_copies(partials, shapes, {}, _sem_pairs(3, n), copies)


def _pair_send(halves):
    n = len(halves)

    def copies(ins, outs, sems):
        x, y, c = _mesh_pos()
        cps = [_remote(ins[i], outs[i], sems, i, (x, y, 1 - c)) for i in range(n)]
        return [(cp, cp) for cp in cps]

    return _from_copies(halves, _same_shapes(halves), {}, _sem_pairs(n), copies)


def _reduce_in_vmem(grads, half_rows, name):
    n = len(grads)

    def body(*refs):
        g, mine, theirs = refs[:n], refs[n:2 * n], refs[2 * n:3 * n]
        landed_pair, partial, landed_chips = refs[3 * n:4 * n], refs[4 * n:5 * n], refs[5 * n:6 * n]
        sems = refs[6 * n:]
        x, y, c = _mesh_pos()
        k = 2 * x + y
        sibling = (x, y, 1 - c)

        def half(i, chip_idx, which):
            return pl.ds(pl.multiple_of((2 * chip_idx + which) * half_rows[i], 8), half_rows[i])

        def run(copies):
            for cp in copies:
                cp.start()
            for cp in copies:
                cp.wait_recv()
                cp.wait_send()

        run([_remote(g[i].at[half(i, kk, 1 - c)], landed_pair[i].at[kk], sems[0:2], (kk, i), sibling)
             for kk in range(N_CHIPS) for i in range(n)])
        for i in range(n):
            for kk in range(N_CHIPS):
                partial[i][kk] = (g[i][half(i, kk, c), :] + landed_pair[i][kk]).astype(BF16)
        run([_remote(partial[i].at[kj], landed_chips[i].at[j], sems[2:4], (j, i), (*chip, c))
             for j, (chip, kj) in enumerate(_other_chips(x, y)) for i in range(n)])
        for i in range(n):
            own = g[i][half(i, k, c), :] + landed_pair[i][k]
            mine[i][...] = ((own + landed_chips[i][0].astype(F32)) + landed_chips[i][1].astype(F32)) \
                + landed_chips[i][2].astype(F32)
        run([_remote(mine[i], theirs[i], sems[4:6], i, sibling) for i in range(n)])

    shapes = [(h, gr.shape[1]) for gr, h in zip(grads, half_rows)]
    halves = [jax.ShapeDtypeStruct(s, F32) for s in shapes]
    vmem = pl.BlockSpec(memory_space=pltpu.VMEM)
    scratch = ([pltpu.VMEM((N_CHIPS,) + s, F32) for s in shapes]
               + [pltpu.VMEM((N_CHIPS,) + s, BF16) for s in shapes]
               + [pltpu.VMEM((3,) + s, BF16) for s in shapes]
               + _sem_pairs(N_CHIPS, n) + _sem_pairs(3, n) + _sem_pairs(n))
    out = pl.pallas_call(
        body, name=name, out_shape=halves + halves, in_specs=[vmem] * n, out_specs=[vmem] * (2 * n),
        scratch_shapes=scratch, compiler_params=_params(48),
    )(*grads)
    return out[:n], out[n:]


def _row_tile(rows):
    for t in (256, 128, 64):
        if rows % t == 0:
            return t
    return rows


def _add_pair(core, chip, grad, landed):
    _, h, cols = landed.shape
    tr = _row_tile(h)

    def body(core_ref, chip_ref, g_ref, l_ref, own_ref, ob_ref):
        s = g_ref[...] + l_ref[...]
        ob_ref[...] = s.astype(BF16)

        @pl.when(pl.program_id(1) == chip_ref[0])
        def _():
            own_ref[...] = s

    return _pcall(
        body, name="grad_add_pair",
        out_shape=[jax.ShapeDtypeStruct((h, cols), F32), jax.ShapeDtypeStruct(landed.shape, BF16)],
        grid_spec=pltpu.PrefetchScalarGridSpec(
            num_scalar_prefetch=2, grid=(h // tr, N_CHIPS),
            in_specs=[pl.BlockSpec((None, None, tr, cols), lambda t, k, core_ref, chip_ref: (k, core_ref[0], t, 0)),
                      pl.BlockSpec((None, tr, cols), lambda t, k, core_ref, chip_ref: (k, t, 0))],
            out_specs=[pl.BlockSpec((tr, cols), lambda t, k, core_ref, chip_ref: (t, 0)),
                       pl.BlockSpec((None, tr, cols), lambda t, k, core_ref, chip_ref: (k, t, 0))]),
        compiler_params=_params(32, ("arbitrary", "arbitrary")),
        operands=(core, chip, grad, landed))


def _add_chips(own, landed):
    h, cols = own.shape
    tr = _row_tile(h)

    def body(p_ref, l_ref, o_ref):
        o_ref[...] = ((p_ref[...] + l_ref[0].astype(F32)) + l_ref[1].astype(F32)) + l_ref[2].astype(F32)

    return _pcall(
        body, name="grad_add_chips", grid=(h // tr,),
        out_shape=jax.ShapeDtypeStruct((h, cols), F32),
        in_specs=[pl.BlockSpec((tr, cols), lambda t: (t, 0)), pl.BlockSpec((3, tr, cols), lambda t: (0, t, 0))],
        out_specs=pl.BlockSpec((tr, cols), lambda t: (t, 0)),
        compiler_params=_params(32, ("arbitrary",)),
        operands=(own, landed))


def _ada_project(c_all, w_ada, b_shard):
    n = w_ada.shape[1]
    tn = 512

    def body(c_ref, w_ref, b_ref, mod_ref, cond_ref):
        cv = c_ref[...]
        cond = cv * _sigmoid(cv)
        mod_ref[...] = _dot(cond.astype(BF16), w_ref[...].astype(BF16)) + b_ref[...]
        cond_ref[0:N_DEV, :] = cond
        cond_ref[N_DEV:2 * N_DEV, :] = jnp.zeros_like(cond)

    return _pcall(
        body, name="ada_project", grid=(n // tn,),
        out_shape=[jax.ShapeDtypeStruct((N_DEV, n), F32), jax.ShapeDtypeStruct((2 * N_DEV, D_MODEL), F32)],
        in_specs=[pl.BlockSpec((N_DEV, D_MODEL), lambda j: (0, 0)), pl.BlockSpec((D_MODEL, tn), lambda j: (0, j)),
                  pl.BlockSpec((1, tn), lambda j: (0, j))],
        out_specs=[pl.BlockSpec((N_DEV, tn), lambda j: (0, j)), pl.BlockSpec((2 * N_DEV, D_MODEL), lambda j: (0, 0))],
        compiler_params=_params(32, ("arbitrary",)),
        operands=(c_all, w_ada, b_shard))


def _in_project(x, pos, sc_a, sh_a, w_in, q_norm_w, kv_norm_w, w_q, w_kv, invf, exchange=None):
    t_len = x.shape[0]
    tm = min(512, t_len)

    def body(x_ref, pos_ref, sc_ref, sh_ref, win_ref, qn_ref, kvn_ref, wq_ref, wkv_ref, invf_ref,
             u_ref, zhg_ref, cq_ref, ckv_ref, q_ref, k_ref, kt_ref, v_ref, vt_ref):
        u = (x_ref[...] * (1.0 + sc_ref[...]) + sh_ref[...]).astype(BF16)
        u_ref[...] = u
        z = _dot_nt(u, win_ref[...])
        zhg_ref[...] = z[:, :HG_COLS]
        cq = z[:, HG_COLS:HG_COLS + Q_RANK]
        ckv = z[:, HG_COLS + Q_RANK:HG_COLS + Q_RANK + KV_RANK]
        cq_ref[...] = cq
        ckv_ref[...] = ckv
        cos_t, sin_t = _rope_tables(pos_ref[...], invf_ref[...])
        k_pe = _rope(z[:, HG_COLS + Q_RANK + KV_RANK:], cos_t, sin_t)
        k_pe_t = jnp.transpose(k_pe).astype(BF16)
        cqn = (cq * lax.rsqrt(_rowmean(cq * cq) + RMS_EPS) * qn_ref[...]).astype(BF16)
        ckvn = (ckv * lax.rsqrt(_rowmean(ckv * ckv) + RMS_EPS) * kvn_ref[...]).astype(BF16)
        for h in range(N_HEADS):
            qh = _dot(cqn, wq_ref[h])
            q_ref[h, :, 0:HEAD_DIM] = qh[:, :HEAD_DIM].astype(BF16)
            q_ref[h, :, HEAD_DIM:QK_DIM] = _rope(qh[:, HEAD_DIM:], cos_t, sin_t).astype(BF16)
            kvh = _dot(ckvn, wkv_ref[h])
            k_ref[h, :, 0:HEAD_DIM] = kvh[:, :HEAD_DIM].astype(BF16)
            k_ref[h, :, HEAD_DIM:QK_DIM] = k_pe.astype(BF16)
            kt_ref[h, 0:HEAD_DIM, :] = jnp.transpose(kvh[:, :HEAD_DIM]).astype(BF16)
            kt_ref[h, HEAD_DIM:QK_DIM, :] = k_pe_t
            v_ref[h] = kvh[:, HEAD_DIM:].astype(BF16)
            vt_ref[h] = jnp.transpose(kvh[:, HEAD_DIM:]).astype(BF16)

    row = lambda i: (i, 0)
    fixed2 = lambda i: (0, 0)
    fixed3 = lambda i: (0, 0, 0)
    heads = lambda i: (0, i, 0)
    n_tiles = t_len // tm
    return _pallas(
        body, name="in_project", grid=(n_tiles,),
        operands=(x, pos, sc_a, sh_a, w_in, q_norm_w, kv_norm_w, w_q, w_kv, invf),
        out_shape=[jax.ShapeDtypeStruct((t_len, D_MODEL), BF16), jax.ShapeDtypeStruct((t_len, HG_COLS), F32),
                   jax.ShapeDtypeStruct((t_len, Q_RANK), F32), jax.ShapeDtypeStruct((t_len, KV_RANK), F32),
                   jax.ShapeDtypeStruct((N_HEADS, t_len, QK_DIM), BF16),
                   jax.ShapeDtypeStruct((N_HEADS, t_len, QK_DIM), BF16),
                   jax.ShapeDtypeStruct((N_HEADS, QK_DIM, t_len), BF16),
                   jax.ShapeDtypeStruct((N_HEADS, t_len, HEAD_DIM), BF16),
                   jax.ShapeDtypeStruct((N_HEADS, HEAD_DIM, t_len), BF16)],
        in_specs=[pl.BlockSpec((tm, D_MODEL), row), pl.BlockSpec((tm, 1), row),
                  pl.BlockSpec((1, D_MODEL), fixed2), pl.BlockSpec((1, D_MODEL), fixed2),
                  pl.BlockSpec((IN_COLS_PAD, D_MODEL), fixed2),
                  pl.BlockSpec((1, Q_RANK), fixed2), pl.BlockSpec((1, KV_RANK), fixed2),
                  pl.BlockSpec((N_HEADS, Q_RANK, QK_DIM), fixed3), pl.BlockSpec((N_HEADS, KV_RANK, 2 * HEAD_DIM), fixed3),
                  pl.BlockSpec((1, 128), fixed2)],
        out_specs=[pl.BlockSpec((tm, D_MODEL), row), pl.BlockSpec((tm, HG_COLS), row),
                   pl.BlockSpec((tm, Q_RANK), row), pl.BlockSpec((tm, KV_RANK), row),
                   pl.BlockSpec((N_HEADS, tm, QK_DIM), heads), pl.BlockSpec((N_HEADS, tm, QK_DIM), heads),
                   pl.BlockSpec((N_HEADS, QK_DIM, tm), lambda i: (0, 0, i)),
                   pl.BlockSpec((N_HEADS, tm, HEAD_DIM), heads),
                   pl.BlockSpec((N_HEADS, HEAD_DIM, tm), lambda i: (0, 0, i))],
        params=_params(48, ("arbitrary",)), exchange=exchange,
        first=lambda: pl.program_id(0) == 0, last=lambda: pl.program_id(0) == n_tiles - 1)


def _lower_bound(lb_raw):
    m = jnp.max(lb_raw, axis=0, keepdims=True)
    e = jnp.exp(lb_raw - m)
    return e[0:1] / jnp.sum(e, axis=0, keepdims=True)


def _tri(inclusive_lower):
    r = lax.broadcasted_iota(jnp.int32, (HG_CHUNK, HG_CHUNK), 0)
    c = lax.broadcasted_iota(jnp.int32, (HG_CHUNK, HG_CHUNK), 1)
    return (c <= r) if inclusive_lower else (c >= r)


def _chunk_rows(n):
    return slice(n * HG_CHUNK, (n + 1) * HG_CHUNK)


def _chunk_prefix_sums(v, inclusive_lower):
    tri = _tri(inclusive_lower).astype(BF16)
    hi = v.astype(BF16)
    rest = v - hi.astype(F32)
    mid = rest.astype(BF16)
    lo = (rest - mid.astype(F32)).astype(BF16)
    pieces = jnp.concatenate([hi, mid, lo], axis=1)
    out = []
    for n in range(v.shape[0] // HG_CHUNK):
        s = _dot(tri, pieces[_chunk_rows(n)])
        out.append((s[:, 0:HEAD_DIM] + s[:, HEAD_DIM:2 * HEAD_DIM]) + s[:, 2 * HEAD_DIM:])
    return jnp.concatenate(out, axis=0)


def _per_chunk(v, row):
    n = v.shape[0] // HG_CHUNK
    v3 = v.reshape(n, HG_CHUNK, HEAD_DIM)
    return jnp.broadcast_to(v3[:, row:row + 1, :], v3.shape).reshape(v.shape)


def _hg_block(q, f_logit, lb):
    sg = _sigmoid(f_logit)
    forget = lb + (1.0 - lb) * sg
    kk = 1.0 - forget
    b = _chunk_prefix_sums(jnp.log(forget), True)
    b_ref = _per_chunk(b, HG_CHUNK // 2 - 1)
    b_last = _per_chunk(b, HG_CHUNK - 1)
    e_i = jnp.exp(b - b_ref)
    e_ri = jnp.exp(b_ref - b)
    e_b = jnp.exp(b)
    e_l = jnp.exp(b_last - b)
    return dict(sg=sg, forget=forget, e_i=e_i, e_ri=e_ri, e_b=e_b, e_l=e_l, dec=jnp.exp(b_last),
                qi=q * e_i, ki=kk * e_ri, qe=q * e_b, kl=kk * e_l)


HG_STEP_HEADS = 4


def _head_cols(hh):
    return slice(hh * HEAD_DIM, (hh + 1) * HEAD_DIM)


def _hgrn_forward(zhg, lb_raw, norm_w, exchange=None):
    t_len = zhg.shape[0]
    tb = min(512, t_len)
    n_chunks = tb // HG_CHUNK
    hs = HG_STEP_HEADS

    def body(q_ref, f_ref, v_ref, g_ref, lb_ref, w_ref, opre_ref, o_ref, st_ref, state):
        @pl.when(pl.program_id(1) == 0)
        def _():
            state[...] = jnp.zeros_like(state)

        causal = _tri(True)
        for hh in range(hs):
            cols = _head_cols(hh)
            blk = _hg_block(q_ref[:, cols], f_ref[:, cols], _lower_bound(lb_ref[:, cols]))
            v = v_ref[:, cols].astype(BF16)
            qi, ki, qe, kl = (blk[name].astype(BF16) for name in ("qi", "ki", "qe", "kl"))
            st = state[hh]
            parts = []
            for n in range(n_chunks):
                r = _chunk_rows(n)
                a = jnp.where(causal, _dot_nt(qi[r], ki[r]), 0.0).astype(BF16)
                st_ref[hh, n] = st
                parts.append(_dot(a, v[r]) + _dot_nt(qe[r], st.astype(BF16)))
                st = st * blk["dec"][n * HG_CHUNK:n * HG_CHUNK + 1] + _dot_tn(v[r], kl[r])
            state[hh] = st
            o = jnp.concatenate(parts, axis=0)
            opre_ref[:, cols] = o
            g = g_ref[:, cols]
            o_ref[:, cols] = o * lax.rsqrt(_rowmean(o * o) + RMS_EPS) * w_ref[:, cols] * (g * _sigmoid(g))

    groups = N_HEADS // hs
    wide = hs * HEAD_DIM
    col = lambda off: (lambda h, t: (t, off + h))
    nb = t_len // tb
    return _pallas(
        body, name="hgrn_forward", grid=(groups, nb), operands=(zhg, zhg, zhg, zhg, lb_raw, norm_w),
        out_shape=[jax.ShapeDtypeStruct((t_len, N_HEADS * HEAD_DIM), F32),
                   jax.ShapeDtypeStruct((t_len, N_HEADS * HEAD_DIM), F32),
                   jax.ShapeDtypeStruct((N_HEADS, t_len // HG_CHUNK, HEAD_DIM, HEAD_DIM), F32)],
        in_specs=[pl.BlockSpec((tb, wide), col(0)), pl.BlockSpec((tb, wide), col(groups)),
                  pl.BlockSpec((tb, wide), col(2 * groups)), pl.BlockSpec((tb, wide), col(3 * groups)),
                  pl.BlockSpec((2, wide), lambda h, t: (0, h)), pl.BlockSpec((1, wide), lambda h, t: (0, h))],
        out_specs=[pl.BlockSpec((tb, wide), col(0)), pl.BlockSpec((tb, wide), col(0)),
                   pl.BlockSpec((hs, n_chunks, HEAD_DIM, HEAD_DIM), lambda h, t: (h, t, 0, 0))],
        scratch_shapes=[pltpu.VMEM((hs, HEAD_DIM, HEAD_DIM), F32)],
        params=_params(40, ("arbitrary", "arbitrary")), exchange=exchange,
        first=lambda: (pl.program_id(0) == 0) & (pl.program_id(1) == 0),
        last=lambda: (pl.program_id(0) == groups - 1) & (pl.program_id(1) == nb - 1))


def _hgrn_backward(zhg, lb_raw, norm_w, o_pre, d_cat, states, exchange=None):
    t_len = zhg.shape[0]
    tb = min(512, t_len)
    n_chunks = tb // HG_CHUNK
    nb = t_len // tb
    hs = HG_STEP_HEADS

    def head(hh, q_ref, f_ref, v_ref, g_ref, lb_ref, w_ref, opre_ref, do_ref, st_ref,
             dq_ref, df_ref, dv_ref, dg_ref, sums_ref, gstate):
        cols = _head_cols(hh)
        lb = _lower_bound(lb_ref[:, cols])
        w = w_ref[:, cols]
        o = opre_ref[:, cols]
        g = g_ref[:, cols]
        d_out = do_ref[:, cols]
        r = lax.rsqrt(_rowmean(o * o) + RMS_EPS)
        sg_g = _sigmoid(g)
        dg_ref[:, cols] = (d_out * (o * r * w) * (sg_g * (1.0 + g * (1.0 - sg_g)))).astype(BF16)
        d_on = d_out * (g * sg_g)
        sums_ref[1:2, cols] += _colsum(d_on * o * r)
        dy = d_on * w
        d_o = (r * dy - o * (r * r * r) * _rowmean(dy * o)).astype(BF16)
        blk = _hg_block(q_ref[:, cols], f_ref[:, cols], lb)
        v = v_ref[:, cols].astype(BF16)
        qi, ki, qe, kl = (blk[name].astype(BF16) for name in ("qi", "ki", "qe", "kl"))
        causal = _tri(True)
        row_id = lax.broadcasted_iota(jnp.int32, (HG_CHUNK, HEAD_DIM), 0)
        gt = gstate[hh]
        d_v, d_qi, d_ki, d_qe, d_kl, d_dec = ([None] * n_chunks for _ in range(6))
        for n in reversed(range(n_chunks)):
            rows = _chunk_rows(n)
            st = st_ref[hh, n]
            a = jnp.where(causal, _dot_nt(qi[rows], ki[rows]), 0.0).astype(BF16)
            d_a = jnp.where(causal, _dot_nt(d_o[rows], v[rows]), 0.0).astype(BF16)
            gt_b = gt.astype(BF16)
            d_v[n] = _dot_tn(a, d_o[rows]) + _dot_nt(kl[rows], gt_b)
            d_qi[n] = _dot(d_a, ki[rows])
            d_ki[n] = _dot_tn(d_a, qi[rows])
            d_qe[n] = _dot(d_o[rows], st.astype(BF16))
            d_kl[n] = _dot(v[rows], gt_b)
            d_dec[n] = jnp.where(row_id == HG_CHUNK - 1, _colsum(gt * st), 0.0)
            gt = gt * blk["dec"][n * HG_CHUNK:n * HG_CHUNK + 1] + _dot_tn(d_o[rows], qe[rows])
        gstate[hh] = gt
        d_qi, d_ki, d_qe, d_kl, d_dec = (jnp.concatenate(p, axis=0) for p in (d_qi, d_ki, d_qe, d_kl, d_dec))
        dv_ref[:, cols] = jnp.concatenate(d_v, axis=0).astype(BF16)
        dq_ref[:, cols] = (d_qi * blk["e_i"] + d_qe * blk["e_b"]).astype(BF16)
        d_k = d_ki * blk["e_ri"] + d_kl * blk["e_l"]
        t_qi = d_qi * blk["qi"]
        t_ki = d_ki * blk["ki"]
        t_kl = d_kl * blk["kl"]
        at_ref, at_last = [], []
        for n in range(n_chunks):
            rows = _chunk_rows(n)
            at_ref.append(jnp.where(row_id == HG_CHUNK // 2 - 1, _colsum(t_ki[rows] - t_qi[rows]), 0.0))
            at_last.append(jnp.where(row_id == HG_CHUNK - 1, _colsum(t_kl[rows]), 0.0))
        d_b = (t_qi - t_ki + d_qe * blk["qe"] - t_kl + jnp.concatenate(at_ref, axis=0)
               + jnp.concatenate(at_last, axis=0) + d_dec * blk["dec"])
        d_forget = _chunk_prefix_sums(d_b, False) / blk["forget"] - d_k
        sg = blk["sg"]
        df_ref[:, cols] = (d_forget * (1.0 - lb) * sg * (1.0 - sg)).astype(BF16)
        sums_ref[0:1, cols] += _colsum(d_forget * (1.0 - sg))

    def body(*refs):
        sums_ref, gstate = refs[-2], refs[-1]

        @pl.when(pl.program_id(1) == 0)
        def _():
            gstate[...] = jnp.zeros_like(gstate)
            sums_ref[...] = jnp.zeros_like(sums_ref)

        for hh in range(hs):
            head(hh, *refs)

    groups = N_HEADS // hs
    wide = hs * HEAD_DIM
    col = lambda off: (lambda h, t: (nb - 1 - t, off + h))
    return _pallas(
        body, name="hgrn_backward", grid=(groups, nb),
        operands=(zhg, zhg, zhg, zhg, lb_raw, norm_w, o_pre, d_cat, states),
        out_shape=[jax.ShapeDtypeStruct((t_len, N_HEADS * HEAD_DIM), BF16)] * 4
        + [jax.ShapeDtypeStruct((8, N_HEADS * HEAD_DIM), F32)],
        in_specs=[pl.BlockSpec((tb, wide), col(0)), pl.BlockSpec((tb, wide), col(groups)),
                  pl.BlockSpec((tb, wide), col(2 * groups)), pl.BlockSpec((tb, wide), col(3 * groups)),
                  pl.BlockSpec((2, wide), lambda h, t: (0, h)), pl.BlockSpec((1, wide), lambda h, t: (0, h)),
                  pl.BlockSpec((tb, wide), col(0)), pl.BlockSpec((tb, wide), col(0)),
                  pl.BlockSpec((hs, n_chunks, HEAD_DIM, HEAD_DIM), lambda h, t: (h, nb - 1 - t, 0, 0))],
        out_specs=[pl.BlockSpec((tb, wide), col(0))] * 4 + [pl.BlockSpec((8, wide), lambda h, t: (0, h))],
        scratch_shapes=[pltpu.VMEM((hs, HEAD_DIM, HEAD_DIM), F32)],
        params=_params(40, ("arbitrary", "arbitrary")), exchange=exchange,
        first=lambda: (pl.program_id(0) == 0) & (pl.program_id(1) == 0),
        last=lambda: (pl.program_id(0) == groups - 1) & (pl.program_id(1) == nb - 1))


ATT_LOG2 = ATT_SCALE * 1.4426950408889634


def _triangle_steps(nq, q_major):
    if q_major:
        pairs = [(i, j) for i in range(nq) for j in range(i + 1)]
    else:
        pairs = [(i, j) for j in range(nq) for i in range(j, nq)]
    return jnp.array([p[0] for p in pairs], jnp.int32), jnp.array([p[1] for p in pairs], jnp.int32)


def _key_le_query(t):
    return lax.broadcasted_iota(jnp.int32, (t, t), 0) <= lax.broadcasted_iota(jnp.int32, (t, t), 1)


def _attention_forward(q, k, v_t, exchange=None):
    t_len = q.shape[1]
    tq = min(512, t_len)
    nq = t_len // tq
    qi_tab, ki_tab = _triangle_steps(nq, True)

    def body(qi_ref, ki_ref, q_ref, k_ref, vt_ref, o_ref, lse_ref, m_s, l_s, acc_s):
        step = pl.program_id(0)
        qi, ki = qi_ref[step], ki_ref[step]

        @pl.when(ki == 0)
        def _():
            m_s[...] = jnp.full_like(m_s, NEG_BIG)
            l_s[...] = jnp.zeros_like(l_s)
            acc_s[...] = jnp.zeros_like(acc_s)

        def accumulate(masked):
            for h in range(N_HEADS):
                s_t = _dot_nt(k_ref[h], q_ref[h]) * ATT_LOG2
                if masked:
                    s_t = jnp.where(_key_le_query(tq), s_t, NEG_BIG)
                m_old = m_s[h]
                m_new = jnp.maximum(m_old, jnp.max(s_t, axis=0, keepdims=True))
                alpha = jnp.exp2(m_old - m_new)
                p_t = jnp.exp2(s_t - m_new)
                l_s[h] = alpha * l_s[h] + jnp.sum(p_t, axis=0, keepdims=True)
                acc_s[h] = alpha * acc_s[h] + _dot(vt_ref[h], p_t.astype(BF16))
                m_s[h] = m_new

        @pl.when(ki < qi)
        def _():
            accumulate(False)

        @pl.when(ki == qi)
        def _():
            accumulate(True)
            for h in range(N_HEADS):
                o_ref[:, h * HEAD_DIM:(h + 1) * HEAD_DIM] = jnp.transpose(acc_s[h] / l_s[h])
                lse_ref[h] = m_s[h] + jnp.log2(l_s[h])

    n_steps = qi_tab.shape[0]
    return _pallas(
        body, name="attention_forward", grid=(n_steps,), prefetch=(qi_tab, ki_tab), operands=(q, k, v_t),
        out_shape=[jax.ShapeDtypeStruct((t_len, N_HEADS * HEAD_DIM), F32),
                   jax.ShapeDtypeStruct((N_HEADS, 1, t_len), F32)],
        in_specs=[pl.BlockSpec((N_HEADS, tq, QK_DIM), lambda s, qt, kt: (0, qt[s], 0)),
                  pl.BlockSpec((N_HEADS, tq, QK_DIM), lambda s, qt, kt: (0, kt[s], 0)),
                  pl.BlockSpec((N_HEADS, HEAD_DIM, tq), lambda s, qt, kt: (0, 0, kt[s]))],
        out_specs=[pl.BlockSpec((tq, N_HEADS * HEAD_DIM), lambda s, qt, kt: (qt[s], 0)),
                   pl.BlockSpec((N_HEADS, 1, tq), lambda s, qt, kt: (0, 0, qt[s]))],
        scratch_shapes=[pltpu.VMEM((N_HEADS, 1, tq), F32), pltpu.VMEM((N_HEADS, 1, tq), F32),
                        pltpu.VMEM((N_HEADS, HEAD_DIM, tq), F32)],
        params=_params(48, ("arbitrary",)), exchange=exchange,
        first=lambda qt, kt: pl.program_id(0) == 0, last=lambda qt, kt: pl.program_id(0) == n_steps - 1)


BWD_HEADS = 4


def _attention_backward(q, k, k_t, v, d_cat, lse, delta, exchange=None):
    t_len = q.shape[1]
    tq = min(512, t_len)
    nq = t_len // tq
    hp = BWD_HEADS
    qi_tab, ki_tab = _triangle_steps(nq, False)

    def body(qi_ref, ki_ref, q_ref, k_ref, kt_ref, v_ref, do_ref, lse_ref, delta_ref, dqt_hbm, dk_ref, dv_ref,
             dqt_s, dk_s, dv_s):
        group, step = pl.program_id(0), pl.program_id(1)
        qi, ki = qi_ref[step], ki_ref[step]

        @pl.when(step == 0)
        def _():
            dqt_s[...] = jnp.zeros_like(dqt_s)

        @pl.when(qi == ki)
        def _():
            dk_s[...] = jnp.zeros_like(dk_s)
            dv_s[...] = jnp.zeros_like(dv_s)

        def accumulate(masked):
            for h in range(hp):
                do_b = do_ref[:, h * HEAD_DIM:(h + 1) * HEAD_DIM].astype(BF16)
                s_t = _dot_nt(k_ref[h], q_ref[h]) * ATT_LOG2
                if masked:
                    s_t = jnp.where(_key_le_query(tq), s_t, NEG_BIG)
                p_t = jnp.exp2(s_t - lse_ref[h])
                dp_t = _dot_nt(v_ref[h], do_b)
                ds_t = (p_t * (dp_t - delta_ref[h]) * ATT_SCALE).astype(BF16)
                dv_s[h] += _dot(p_t.astype(BF16), do_b)
                dk_s[h] += _dot(ds_t, q_ref[h])
                dqt_s[h, qi] += _dot(kt_ref[h], ds_t)

        @pl.when(ki < qi)
        def _():
            accumulate(False)

        @pl.when(ki == qi)
        def _():
            accumulate(True)
            for h in range(hp):
                pltpu.sync_copy(dqt_s.at[h, qi], dqt_hbm.at[group * hp + h, qi])

        @pl.when(qi == nq - 1)
        def _():
            dk_ref[...] = dk_s[...]
            dv_ref[...] = dv_s[...]

    wide = hp * HEAD_DIM
    n_groups, n_steps = N_HEADS // hp, qi_tab.shape[0]
    return _pallas(
        body, name="attention_backward", grid=(n_groups, n_steps), prefetch=(qi_tab, ki_tab),
        operands=(q, k, k_t, v, d_cat, lse, delta),
        out_shape=[jax.ShapeDtypeStruct((N_HEADS, nq, QK_DIM, tq), F32),
                   jax.ShapeDtypeStruct((N_HEADS, t_len, QK_DIM), F32),
                   jax.ShapeDtypeStruct((N_HEADS, t_len, HEAD_DIM), F32)],
        in_specs=[pl.BlockSpec((hp, tq, QK_DIM), lambda g, s, qt, kt: (g, qt[s], 0)),
                  pl.BlockSpec((hp, tq, QK_DIM), lambda g, s, qt, kt: (g, kt[s], 0)),
                  pl.BlockSpec((hp, QK_DIM, tq), lambda g, s, qt, kt: (g, 0, kt[s])),
                  pl.BlockSpec((hp, tq, HEAD_DIM), lambda g, s, qt, kt: (g, kt[s], 0)),
                  pl.BlockSpec((tq, wide), lambda g, s, qt, kt: (qt[s], n_groups + g)),
                  pl.BlockSpec((hp, 1, tq), lambda g, s, qt, kt: (g, 0, qt[s])),
                  pl.BlockSpec((hp, 1, tq), lambda g, s, qt, kt: (g, 0, qt[s]))],
        out_specs=[pl.BlockSpec(memory_space=pl.ANY),
                   pl.BlockSpec((hp, tq, QK_DIM), lambda g, s, qt, kt: (g, kt[s], 0)),
                   pl.BlockSpec((hp, tq, HEAD_DIM), lambda g, s, qt, kt: (g, kt[s], 0))],
        scratch_shapes=[pltpu.VMEM((hp, nq, QK_DIM, tq), F32), pltpu.VMEM((hp, tq, QK_DIM), F32),
                        pltpu.VMEM((hp, tq, HEAD_DIM), F32)],
        params=_params(58, ("arbitrary", "arbitrary")), exchange=exchange,
        first=lambda qt, kt: (pl.program_id(0) == 0) & (pl.program_id(1) == 0),
        last=lambda qt, kt: (pl.program_id(0) == n_groups - 1) & (pl.program_id(1) == n_steps - 1))


def _out_project(o_hg, o_mla, x, g_a, w_out, exchange=None):
    t_len = x.shape[0]
    tm = min(512, t_len)
    half = N_HEADS * HEAD_DIM

    def body(ohg_ref, omla_ref, x_ref, ga_ref, w_ref, cat_ref, mix_ref, xhat_ref, rstd_ref):
        a = ohg_ref[...].astype(BF16)
        b = omla_ref[...].astype(BF16)
        cat_ref[:, 0:half] = a
        cat_ref[:, half:2 * half] = b
        mix = _dot(a, w_ref[0:half, :]) + _dot(b, w_ref[half:2 * half, :])
        mix_ref[...] = mix
        r1 = DN_ALPHA * x_ref[...] + (1.0 + ga_ref[...]) * mix
        xc = r1 - _rowmean(r1)
        rstd = lax.rsqrt(_rowmean(xc * xc) + LN_EPS)
        xhat_ref[...] = xc * rstd
        rstd_ref[...] = rstd

    row = lambda i: (i, 0)
    fixed = lambda i: (0, 0)
    n_tiles = t_len // tm
    return _pallas(
        body, name="out_project", grid=(n_tiles,), operands=(o_hg, o_mla, x, g_a, w_out),
        out_shape=[jax.ShapeDtypeStruct((t_len, D_MODEL), BF16), jax.ShapeDtypeStruct((t_len, D_MODEL), F32),
                   jax.ShapeDtypeStruct((t_len, D_MODEL), F32), jax.ShapeDtypeStruct((t_len, 1), F32)],
        in_specs=[pl.BlockSpec((tm, half), row), pl.BlockSpec((tm, half), row), pl.BlockSpec((tm, D_MODEL), row),
                  pl.BlockSpec((1, D_MODEL), fixed), pl.BlockSpec((D_MODEL, D_MODEL), fixed)],
        out_specs=[pl.BlockSpec((tm, D_MODEL), row), pl.BlockSpec((tm, D_MODEL), row),
                   pl.BlockSpec((tm, D_MODEL), row), pl.BlockSpec((tm, 1), row)],
        params=_params(48, ("arbitrary",)), exchange=exchange,
        first=lambda: pl.program_id(0) == 0, last=lambda: pl.program_id(0) == n_tiles - 1)


V_LN1G, V_LN1B, V_SCM, V_SHM, V_GM, V_GA, V_LN2G, V_LN2B = range(8)
S_DLN2G, S_DLN2B, S_DGM, S_DSCM, S_DSHM, S_DLN1G, S_DLN1B, S_DGA, S_LOSS = range(9)


def _mlp_and_back(xhat1, rstd1, mix, target, o_mla, vecs, w1_top, w1_bottom, w2, w_out):
    t_len = xhat1.shape[0]
    tm = min(256, t_len)
    n_ff = w1_top.shape[0]
    ff = w1_top.shape[2]
    top_rows = w1_top.shape[1]

    def body(xhat_ref, rstd_ref, mix_ref, tgt_ref, omla_ref, vec_ref, w1_top_hbm, w1_bottom_hbm, w2_hbm, wout_hbm,
             act_ref, dhp_ref, um_ref, dh_ref, dmix_ref, dcat_ref, dr1_ref, sums_ref, delta_ref,
             w1_s, w2_s, wout_s, hp_s, load_sems):
        @pl.when(pl.program_id(0) == 0)
        def _():
            loads = [pltpu.make_async_copy(w1_top_hbm, w1_s.at[:, 0:top_rows], load_sems.at[0]),
                     pltpu.make_async_copy(w1_bottom_hbm, w1_s.at[:, top_rows:D_MODEL], load_sems.at[3]),
                     pltpu.make_async_copy(w2_hbm, w2_s, load_sems.at[1]),
                     pltpu.make_async_copy(wout_hbm, wout_s, load_sems.at[2])]
            for cp in loads:
                cp.start()
            sums_ref[...] = jnp.zeros_like(sums_ref)
            for cp in loads:
                cp.wait()

        vec = lambda r: vec_ref[r:r + 1, :]
        xhat = xhat_ref[...]
        x1 = xhat * vec(V_LN1G) + vec(V_LN1B)
        um = (x1 * (1.0 + vec(V_SCM)) + vec(V_SHM)).astype(BF16)
        um_ref[...] = um
        h = jnp.zeros((tm, D_MODEL), F32)
        for j in range(n_ff):
            hp = _dot(um, w1_s[j])
            hp_s[j] = hp
            act = jnp.square(jnp.maximum(hp, 0.0)).astype(BF16)
            act_ref[:, j * ff:(j + 1) * ff] = act
            h = h + _dot(act, w2_s[j])
        r2 = DN_ALPHA * x1 + (1.0 + vec(V_GM)) * h
        xc = r2 - _rowmean(r2)
        rstd2 = lax.rsqrt(_rowmean(xc * xc) + LN_EPS)
        xhat2 = xc * rstd2
        err = xhat2 * vec(V_LN2G) + vec(V_LN2B) - tgt_ref[...]
        loss = 0.5 * jnp.sum(_rowmean(err * err))
        dy = err * (1.0 / D_MODEL)
        dxh = dy * vec(V_LN2G)
        dr2 = rstd2 * (dxh - _rowmean(dxh) - xhat2 * _rowmean(dxh * xhat2))
        dh = ((1.0 + vec(V_GM)) * dr2).astype(BF16)
        dh_ref[...] = dh
        sums_ref[S_DLN2G:S_DLN2G + 1, :] += _colsum(dy * xhat2)
        sums_ref[S_DLN2B:S_DLN2B + 1, :] += _colsum(dy)
        sums_ref[S_DGM:S_DGM + 1, :] += _colsum(dr2 * h)
        sums_ref[S_LOSS:S_LOSS + 1, :] += jnp.full((1, D_MODEL), loss, F32)
        du = jnp.zeros((tm, D_MODEL), F32)
        for j in range(n_ff):
            dhp = (_dot_nt(dh, w2_s[j]) * (2.0 * jnp.maximum(hp_s[j], 0.0))).astype(BF16)
            dhp_ref[:, j * ff:(j + 1) * ff] = dhp
            du = du + _dot_nt(dhp, w1_s[j])
        sums_ref[S_DSCM:S_DSCM + 1, :] += _colsum(du * x1)
        sums_ref[S_DSHM:S_DSHM + 1, :] += _colsum(du)
        dx1 = DN_ALPHA * dr2 + du * (1.0 + vec(V_SCM))
        sums_ref[S_DLN1G:S_DLN1G + 1, :] += _colsum(dx1 * xhat)
        sums_ref[S_DLN1B:S_DLN1B + 1, :] += _colsum(dx1)
        dxh1 = dx1 * vec(V_LN1G)
        dr1 = rstd_ref[...] * (dxh1 - _rowmean(dxh1) - xhat * _rowmean(dxh1 * xhat))
        dr1_ref[...] = dr1
        sums_ref[S_DGA:S_DGA + 1, :] += _colsum(dr1 * mix_ref[...])
        dmix = ((1.0 + vec(V_GA)) * dr1).astype(BF16)
        dmix_ref[...] = dmix
        dcat = _dot_nt(dmix, wout_s[...])
        dcat_ref[...] = dcat
        ones = jnp.ones((8, HEAD_DIM), F32)
        half = N_HEADS * HEAD_DIM
        for hd in range(N_HEADS):
            prod = dcat[:, half + hd * HEAD_DIM:half + (hd + 1) * HEAD_DIM] * omla_ref[:, hd * HEAD_DIM:(hd + 1) * HEAD_DIM]
            delta_ref[hd] = lax.dot_general(ones, prod, (((1,), (1,)), ((), ())), preferred_element_type=F32,
                                            precision=lax.Precision.HIGHEST)[0:1]

    row = lambda i: (i, 0)
    fixed = lambda i: (0, 0)
    any_spec = pl.BlockSpec(memory_space=pl.ANY)
    return _pcall(
        body, name="mlp_and_back", grid=(t_len // tm,),
        out_shape=[jax.ShapeDtypeStruct((t_len, D_FF), BF16), jax.ShapeDtypeStruct((t_len, D_FF), BF16),
                   jax.ShapeDtypeStruct((t_len, D_MODEL), BF16), jax.ShapeDtypeStruct((t_len, D_MODEL), BF16),
                   jax.ShapeDtypeStruct((t_len, D_MODEL), BF16), jax.ShapeDtypeStruct((t_len, D_MODEL), F32),
                   jax.ShapeDtypeStruct((t_len, D_MODEL), F32), jax.ShapeDtypeStruct((16, D_MODEL), F32),
                   jax.ShapeDtypeStruct((N_HEADS, 1, t_len), F32)],
        in_specs=[pl.BlockSpec((tm, D_MODEL), row), pl.BlockSpec((tm, 1), row), pl.BlockSpec((tm, D_MODEL), row),
                  pl.BlockSpec((tm, D_MODEL), row), pl.BlockSpec((tm, N_HEADS * HEAD_DIM), row),
                  pl.BlockSpec((8, D_MODEL), fixed), any_spec, any_spec, any_spec, any_spec],
        out_specs=[pl.BlockSpec((tm, D_FF), row), pl.BlockSpec((tm, D_FF), row), pl.BlockSpec((tm, D_MODEL), row),
                   pl.BlockSpec((tm, D_MODEL), row), pl.BlockSpec((tm, D_MODEL), row), pl.BlockSpec((tm, D_MODEL), row),
                   pl.BlockSpec((tm, D_MODEL), row), pl.BlockSpec((16, D_MODEL), fixed),
                   pl.BlockSpec((N_HEADS, 1, tm), lambda i: (0, 0, i))],
        scratch_shapes=[pltpu.VMEM((n_ff, D_MODEL, ff), BF16), pltpu.VMEM(w2.shape, BF16), pltpu.VMEM(w_out.shape, BF16),
                        pltpu.VMEM((n_ff, tm, ff), F32), pltpu.SemaphoreType.DMA((4,))],
        compiler_params=_params(56, ("arbitrary",)),
        operands=(xhat1, rstd1, mix, target, o_mla, vecs, w1_top, w1_bottom, w2, w_out))


def _in_project_backward(dq, dk, dv, cq, ckv, pos, invf, q_norm_w, kv_norm_w, w_q, w_kv,
                         d_hq, d_hf, d_hi, d_hg, w_in, dr1, x, sc_a, exchange=None):
    t_len = x.shape[0]
    tm = min(512, t_len)
    per_q = dq.shape[3] // tm
    hgw = N_HEADS * HEAD_DIM

    def body(dq_ref, dk_ref, dv_ref, cq_ref, ckv_ref, pos_ref, invf_ref, qn_ref, kvn_ref, wq_ref, wkv_ref,
             dhq_ref, dhf_ref, dhi_ref, dhg_ref, win_ref, dr1_ref, x_ref, sc_ref,
             dz_ref, dqf_ref, dkvu_ref, cqn_ref, ckvn_ref, gx_ref, sums_ref):
        @pl.when(pl.program_id(0) == 0)
        def _():
            sums_ref[...] = jnp.zeros_like(sums_ref)

        cos_t, sin_t = _rope_tables(pos_ref[...], invf_ref[...])
        cq = cq_ref[...]
        ckv = ckv_ref[...]
        rq = lax.rsqrt(_rowmean(cq * cq) + RMS_EPS)
        rkv = lax.rsqrt(_rowmean(ckv * ckv) + RMS_EPS)
        cqn_ref[...] = (cq * rq * qn_ref[...]).astype(BF16)
        ckvn_ref[...] = (ckv * rkv * kvn_ref[...]).astype(BF16)
        d_cqn = jnp.zeros((tm, Q_RANK), F32)
        d_ckvn = jnp.zeros((tm, KV_RANK), F32)
        d_kpe = jnp.zeros((tm, 128), F32)
        for h in range(N_HEADS):
            dqh = jnp.transpose(dq_ref[h])
            dqf_ref[h, :, 0:HEAD_DIM] = dqh[:, :HEAD_DIM].astype(BF16)
            dqf_ref[h, :, HEAD_DIM:QK_DIM] = _unrope(dqh[:, HEAD_DIM:], cos_t, sin_t).astype(BF16)
            d_cqn = d_cqn + _dot_nt(dqf_ref[h], wq_ref[h])
            dkh = dk_ref[h]
            d_kpe = d_kpe + dkh[:, HEAD_DIM:]
            dkvu_ref[h, :, 0:HEAD_DIM] = dkh[:, :HEAD_DIM].astype(BF16)
            dkvu_ref[h, :, HEAD_DIM:2 * HEAD_DIM] = dv_ref[h].astype(BF16)
            d_ckvn = d_ckvn + _dot_nt(dkvu_ref[h], wkv_ref[h])
        dyq = d_cqn * qn_ref[...]
        dykv = d_ckvn * kvn_ref[...]
        sums_ref[2:3, 0:Q_RANK] += _colsum(d_cqn * cq * rq)
        sums_ref[3:4, 0:KV_RANK] += _colsum(d_ckvn * ckv * rkv)
        dz_ref[:, 0:hgw] = dhq_ref[...]
        dz_ref[:, hgw:2 * hgw] = dhf_ref[...]
        dz_ref[:, 2 * hgw:3 * hgw] = dhi_ref[...]
        dz_ref[:, 3 * hgw:4 * hgw] = dhg_ref[...]
        dz_ref[:, HG_COLS:HG_COLS + Q_RANK] = (rq * dyq - cq * (rq * rq * rq) * _rowmean(dyq * cq)).astype(BF16)
        dz_ref[:, HG_COLS + Q_RANK:HG_COLS + Q_RANK + KV_RANK] = (
            rkv * dykv - ckv * (rkv * rkv * rkv) * _rowmean(dykv * ckv)).astype(BF16)
        dz_ref[:, HG_COLS + Q_RANK + KV_RANK:] = _unrope(d_kpe, cos_t, sin_t).astype(BF16)
        du = _dot(dz_ref[...], win_ref[...])
        xv = x_ref[...]
        gx_ref[...] = DN_ALPHA * dr1_ref[...] + (1.0 + sc_ref[...]) * du
        sums_ref[0:1, :] += _colsum(du * xv)
        sums_ref[1:2, :] += _colsum(du)

    row = lambda i: (i, 0)
    fixed2 = lambda i: (0, 0)
    fixed3 = lambda i: (0, 0, 0)
    heads = lambda i: (0, i, 0)
    n_tiles = t_len // tm
    return _pallas(
        body, name="in_project_backward", grid=(n_tiles,),
        operands=(dq, dk, dv, cq, ckv, pos, invf, q_norm_w, kv_norm_w, w_q, w_kv, d_hq, d_hf, d_hi, d_hg, w_in, dr1, x,
                  sc_a),
        out_shape=[jax.ShapeDtypeStruct((t_len, IN_COLS_PAD), BF16), jax.ShapeDtypeStruct((N_HEADS, t_len, QK_DIM), BF16),
                   jax.ShapeDtypeStruct((N_HEADS, t_len, 2 * HEAD_DIM), BF16), jax.ShapeDtypeStruct((t_len, Q_RANK), BF16),
                   jax.ShapeDtypeStruct((t_len, KV_RANK), BF16), jax.ShapeDtypeStruct((t_len, D_MODEL), F32),
                   jax.ShapeDtypeStruct((8, D_MODEL), F32)],
        in_specs=[pl.BlockSpec((N_HEADS, None, QK_DIM, tm), lambda i: (0, i // per_q, 0, i % per_q)),
                  pl.BlockSpec((N_HEADS, tm, QK_DIM), heads),
                  pl.BlockSpec((N_HEADS, tm, HEAD_DIM), heads), pl.BlockSpec((tm, Q_RANK), row),
                  pl.BlockSpec((tm, KV_RANK), row), pl.BlockSpec((tm, 1), row), pl.BlockSpec((1, 128), fixed2),
                  pl.BlockSpec((1, Q_RANK), fixed2), pl.BlockSpec((1, KV_RANK), fixed2),
                  pl.BlockSpec((N_HEADS, Q_RANK, QK_DIM), fixed3), pl.BlockSpec((N_HEADS, KV_RANK, 2 * HEAD_DIM), fixed3),
                  pl.BlockSpec((tm, hgw), row), pl.BlockSpec((tm, hgw), row), pl.BlockSpec((tm, hgw), row),
                  pl.BlockSpec((tm, hgw), row), pl.BlockSpec((IN_COLS_PAD, D_MODEL), fixed2),
                  pl.BlockSpec((tm, D_MODEL), row), pl.BlockSpec((tm, D_MODEL), row), pl.BlockSpec((1, D_MODEL), fixed2)],
        out_specs=[pl.BlockSpec((tm, IN_COLS_PAD), row), pl.BlockSpec((N_HEADS, tm, QK_DIM), heads),
                   pl.BlockSpec((N_HEADS, tm, 2 * HEAD_DIM), heads), pl.BlockSpec((tm, Q_RANK), row),
                   pl.BlockSpec((tm, KV_RANK), row), pl.BlockSpec((tm, D_MODEL), row), pl.BlockSpec((8, D_MODEL), fixed2)],
        params=_params(48, ("arbitrary",)), exchange=exchange,
        first=lambda: pl.program_id(0) == 0, last=lambda: pl.program_id(0) == n_tiles - 1)


def _weight_grad(a, b, name, n_blocks, bn, a_blocked=False, b_blocked=True, exchange=None, token_tile=512):
    t_len = a.shape[0]
    m = a.shape[1] // n_blocks if a_blocked else a.shape[1]
    bt = min(token_tile, t_len)

    def body(a_ref, b_ref, o_ref):
        @pl.when(pl.program_id(1) == 0)
        def _():
            o_ref[...] = jnp.zeros_like(o_ref)

        o_ref[...] += _dot_tn(a_ref[...].astype(BF16), b_ref[...].astype(BF16))

    a_spec = pl.BlockSpec((bt, m), (lambda n, t: (t, n)) if a_blocked else (lambda n, t: (t, 0)))
    if b.ndim == 3:
        b_spec = pl.BlockSpec((None, bt, bn), lambda n, t: (n, t, 0))
    else:
        b_spec = pl.BlockSpec((bt, bn), (lambda n, t: (t, n)) if b_blocked else (lambda n, t: (t, 0)))
    nt = t_len // bt
    (out,), landed = _pallas(
        body, name=name, grid=(n_blocks, nt), operands=(a, b),
        out_shape=[jax.ShapeDtypeStruct((n_blocks, m, bn), F32)],
        in_specs=[a_spec, b_spec],
        out_specs=[pl.BlockSpec((None, m, bn), lambda n, t: (n, 0, 0))],
        params=_params(56, ("arbitrary", "arbitrary")), exchange=exchange,
        first=lambda: (pl.program_id(0) == 0) & (pl.program_id(1) == 0),
        last=lambda: (pl.program_id(0) == n_blocks - 1) & (pl.program_id(1) == nt - 1))
    return (out, landed) if exchange else out


def _reduce_small(gathered, lb_raw):
    def body(g_ref, lb_ref, tot_ref, dlb_ref):
        tot = g_ref[0]
        for d in range(1, N_DEV):
            tot = tot + g_ref[d]
        tot_ref[...] = tot
        a = lb_ref[...]
        m = jnp.max(a, axis=0, keepdims=True)
        e = jnp.exp(a - m)
        lb = e[0:1] / jnp.sum(e, axis=0, keepdims=True)
        d0 = tot[10:11, 0:512] * lb * (1.0 - lb)
        dlb_ref[0:1, :] = d0
        dlb_ref[1:2, :] = -d0

    return pl.pallas_call(
        body, name="reduce_small",
        out_shape=[jax.ShapeDtypeStruct((SMALL_ROWS, D_MODEL), F32), jax.ShapeDtypeStruct((2, 512), F32)],
    )(gathered, lb_raw)


def _adamw_update(w, gv, m, v):
    nm = ADAM_B1 * m + (1.0 - ADAM_B1) * gv
    nv = ADAM_B2 * v + (1.0 - ADAM_B2) * jnp.square(gv)
    m_hat = nm / (1.0 - ADAM_B1 ** ADAM_STEP)
    v_hat = nv / (1.0 - ADAM_B2 ** ADAM_STEP)
    return -ADAM_LR * (m_hat / (jnp.sqrt(v_hat) + ADAM_EPS) + ADAM_WD * w), nm, nv


def _adamw_halves(core, w, mine, theirs, m, v, name):
    rows, cols = w.shape
    h = rows // 2
    tr = _row_tile(h)
    per_half = h // tr

    def body(core_ref, w_ref, mine_ref, theirs_ref, m_ref, v_ref, g_ref, d_ref, nm_ref, nv_ref):
        is_mine = pl.program_id(0) // per_half == core_ref[0]
        gv = jnp.where(is_mine, mine_ref[...], theirs_ref[...])
        g_ref[...] = gv
        d_ref[...], nm_ref[...], nv_ref[...] = _adamw_update(w_ref[...], gv, m_ref[...], v_ref[...])

    full = pl.BlockSpec((tr, cols), lambda i, core_ref: (i, 0))
    part = pl.BlockSpec((tr, cols), lambda i, core_ref: (i % per_half, 0))
    return _pcall(
        body, name=name, out_shape=[jax.ShapeDtypeStruct(w.shape, F32)] * 4,
        grid_spec=pltpu.PrefetchScalarGridSpec(
            num_scalar_prefetch=1, grid=(rows // tr,), in_specs=[full, part, part, full, full], out_specs=[full] * 4),
        compiler_params=_params(40, ("arbitrary",)),
        operands=(core, w, mine, theirs, m, v))


def _adamw(w, g, m, v, name):
    rows, cols = w.shape
    tr = _row_tile(rows) if rows >= 8 else rows

    def body(w_ref, g_ref, m_ref, v_ref, d_ref, nm_ref, nv_ref):
        d_ref[...], nm_ref[...], nv_ref[...] = _adamw_update(w_ref[...], g_ref[...], m_ref[...], v_ref[...])

    spec = pl.BlockSpec((tr, cols), lambda i: (i, 0))
    return _pcall(
        body, name=name, grid=(rows // tr,),
        out_shape=[jax.ShapeDtypeStruct(w.shape, F32)] * 3,
        in_specs=[spec] * 4, out_specs=[spec] * 3,
        compiler_params=_params(40, ("arbitrary",)),
        operands=(w, g, m, v))


def kernel(x, c, positions, w_ada, b_ada, w_in, hg_lower_bounds, hg_norm_w, mla_q_norm_w, w_q_up, mla_kv_norm_w, w_kv_up, w_out, ln1_g, ln1_b, w_mlp_in, w_mlp_out, ln2_g, ln2_b, loss_target, m_w_ada, m_b_ada, m_w_in, m_hg_lower_bounds, m_hg_norm_w, m_mla_q_norm_w, m_w_q_up, m_mla_kv_norm_w, m_w_kv_up, m_w_out, m_ln1_g, m_ln1_b, m_w_mlp_in, m_w_mlp_out, m_ln2_g, m_ln2_b, v_w_ada, v_b_ada, v_w_in, v_hg_lower_bounds, v_hg_norm_w, v_mla_q_norm_w, v_w_q_up, v_mla_kv_norm_w, v_w_kv_up, v_w_out, v_ln1_g, v_ln1_b, v_w_mlp_in, v_w_mlp_out, v_ln2_g, v_ln2_b):
    ix, iy, ic = _mesh_pos()
    chip = 2 * ix + iy
    me = 4 * ix + 2 * iy + ic
    core_arr = jnp.reshape(ic, (1,)).astype(jnp.int32)
    chip_arr = jnp.reshape(chip, (1,)).astype(jnp.int32)

    xs = x[0]
    target = loss_target[0]
    t_len = xs.shape[0]
    pos = positions.astype(F32).reshape(t_len, 1)
    inv = 1.0 / (ROPE_THETA ** (jnp.arange(0, ROPE_DIM, 2, dtype=F32) / ROPE_DIM))
    invf = jnp.concatenate([inv, inv, jnp.zeros((128 - ROPE_DIM,), F32)]).reshape(1, 128)

    def slot(w):
        rows, cols = w.shape
        own = w.astype(BF16).reshape(1, 2, rows // 2, cols)
        return lax.dynamic_update_slice(jnp.zeros((N_CHIPS, 2, rows // 2, cols), BF16), own, (chip, 0, 0, 0))

    def slot8(a):
        return lax.dynamic_update_slice(jnp.zeros((N_DEV,) + a.shape, a.dtype), a[None], (me, 0, 0))

    def whole(s):
        return s.reshape(N_CHIPS, 2 * s.shape[2], s.shape[3])

    def halved(g):
        return g.reshape(N_CHIPS, 2, g.shape[1] // 2, g.shape[2])

    ada_cols = w_ada.shape[2]
    c_all, *early = _run_exchange(
        _merge(_gather_all(slot8(jnp.broadcast_to(c, (8, D_MODEL)))),
               _gather_over_ici([slot(jnp.transpose(w_in[0])), slot(w_q_up[0]), slot(w_kv_up[0])])),
        "gather_c_and_mixer_weights_ici")
    b_shard = lax.dynamic_slice(b_ada, (0, chip * ada_cols), (1, ada_cols))
    mod_cols, cond16 = _ada_project(c_all[:, 0, :], w_ada[0], b_shard)
    mod_all, *early = _run_exchange(_merge(_gather_all(slot8(mod_cols)), _gather_over_d2d(early)),
                                    "gather_mod_and_mixer_weights_d2d")
    mod_mine = lax.dynamic_slice(mod_all, (0, me, 0), (N_DEV, 1, ada_cols))[::2, 0, :].reshape(6, D_MODEL)
    sh_a, sc_a, g_a, sh_m, sc_m, g_m = (mod_mine[i:i + 1] for i in range(6))
    g_in, g_q, g_kv = (whole(s) for s in early)
    w_in_full = jnp.pad(g_in.reshape(IN_COLS, D_MODEL), ((0, IN_COLS_PAD - IN_COLS), (0, 0)))
    w_q_full = jnp.pad(g_q, ((0, 0), (0, 0), (0, QK_DIM - g_q.shape[2])))

    w1_rows = D_MODEL // 2
    (u_a, zhg, cq, ckv, q, k, k_t, v, v_t), (s_top,) = _in_project(
        xs, pos, sc_a, sh_a, w_in_full, mla_q_norm_w, mla_kv_norm_w, w_q_full, g_kv, invf,
        _gather_over_ici([slot(w_mlp_in[0, :w1_rows])]))
    (o_pre, o_hg, states), (s_out, s_bottom, s_top) = _hgrn_forward(
        zhg, hg_lower_bounds, hg_norm_w,
        _merge(_gather_over_ici([slot(w_out[0]), slot(w_mlp_in[0, w1_rows:])]), _gather_over_d2d([s_top])))
    (o_mla, lse), (s_w2, s_out, s_bottom) = _attention_forward(
        q, k, v_t, _merge(_gather_over_ici([slot(w_mlp_out[0])]), _gather_over_d2d([s_out, s_bottom])))
    w_out_full = whole(s_out).reshape(D_MODEL, D_MODEL)
    (cat, mix, xhat1, rstd1), (s_w2,) = _out_project(o_hg, o_mla, xs, g_a, w_out_full, _gather_over_d2d([s_w2]))
    g_w1_top, g_w1_bottom, g_w2 = whole(s_top), whole(s_bottom), whole(s_w2)
    vecs = jnp.concatenate([ln1_g, ln1_b, sc_m, sh_m, g_m, g_a, ln2_g, ln2_b], axis=0)
    act, dhp, um, dh, dmix, d_cat, dr1, mlp_sums, delta = _mlp_and_back(
        xhat1, rstd1, mix, target, o_mla, vecs, g_w1_top, g_w1_bottom, g_w2, w_out_full)

    gw_1 = _weight_grad(um, dhp, "grad_w_mlp_in", N_CHIPS, D_FF // N_CHIPS, token_tile=4096)
    gw_2 = _weight_grad(act, dh, "grad_w_mlp_out", N_CHIPS, D_MODEL, a_blocked=True, b_blocked=False, token_tile=4096)
    gw_out = _weight_grad(cat, dmix, "grad_w_out", 1, D_MODEL, token_tile=2048)
    gw_out = gw_out.reshape(N_CHIPS, D_MODEL // N_CHIPS, D_MODEL)
    mlp_grads = [halved(gw_1), halved(gw_2), halved(gw_out)]
    (dq, dk, dv), landed = _attention_backward(q, k, k_t, v, d_cat, lse, delta, _pair_exchange(mlp_grads))
    chip_sums = [_add_pair(core_arr, chip_arr, g, l) for g, l in zip(mlp_grads, landed)]
    (d_hq, d_hf, d_hi, d_hg, hg_sums), landed = _hgrn_backward(
        zhg, hg_lower_bounds, hg_norm_w, o_pre, d_cat, states, _chip_exchange([b for _, b in chip_sums]))
    mlp_mine = [_add_chips(own, l) for (own, _), l in zip(chip_sums, landed)]
    (dz, dqf, dkvu, cqn, ckvn, grad_x, in_sums), _ = _in_project_backward(
        dq, dk, dv, cq, ckv, pos, invf, mla_q_norm_w, mla_kv_norm_w, w_q_full, g_kv,
        d_hq, d_hf, d_hi, d_hg, w_in_full, dr1, xs, sc_a)

    gw_in, mlp_theirs = _weight_grad(dz, u_a, "grad_w_in", 3, D_MODEL, a_blocked=True, b_blocked=False,
                                     exchange=_pair_send(mlp_mine), token_tile=4096)
    gw_in = gw_in.reshape(IN_COLS_PAD, D_MODEL)
    gw_q = _weight_grad(cqn, dqf, "grad_w_q_up", N_HEADS, QK_DIM, token_tile=2048)[:, :, :HEAD_DIM + ROPE_DIM]
    gw_kv = _weight_grad(ckvn, dkvu, "grad_w_kv_up", N_HEADS, 2 * HEAD_DIM, token_tile=2048)
    flat = lambda g: g.reshape(g.shape[0] * g.shape[1], g.shape[2])
    mixer_mine, mixer_theirs = _reduce_in_vmem(
        [gw_in, flat(gw_q), flat(gw_kv)], [IN_COLS // N_CHIPS // 2, Q_RANK // 2, KV_RANK // 2], "reduce_mixer_grads")
    reduced = ("w_in", "w_q_up", "w_kv_up", "w_mlp_in", "w_mlp_out", "w_out")
    halves_mine = dict(zip(reduced, list(mixer_mine) + mlp_mine))
    halves_theirs = dict(zip(reduced, list(mixer_theirs) + list(mlp_theirs)))

    zeros = lambda n: jnp.zeros((1, n), F32)
    small = jnp.concatenate([
        in_sums[1:2], in_sums[0:1], mlp_sums[S_DGA:S_DGA + 1],
        mlp_sums[S_DSHM:S_DSHM + 1], mlp_sums[S_DSCM:S_DSCM + 1], mlp_sums[S_DGM:S_DGM + 1],
        mlp_sums[S_DLN1G:S_DLN1G + 1], mlp_sums[S_DLN1B:S_DLN1B + 1],
        mlp_sums[S_DLN2G:S_DLN2G + 1], mlp_sums[S_DLN2B:S_DLN2B + 1],
        jnp.concatenate([hg_sums[0:1], hg_sums[1:2]], axis=1),
        jnp.concatenate([in_sums[2:3, :Q_RANK], in_sums[3:4, :KV_RANK], zeros(D_MODEL - Q_RANK - KV_RANK)], axis=1),
        mlp_sums[S_LOSS:S_LOSS + 1],
        jnp.zeros((SMALL_ROWS - 13, D_MODEL), F32)], axis=0)
    small_all = _allgather8(small, "gather_small")
    tot, g_lb = _reduce_small(small_all, hg_lower_bounds)
    loss = tot[12, 0]
    g_b_ada = tot[0:6].reshape(1, 6 * D_MODEL)
    g_ln1_g, g_ln1_b, g_ln2_g, g_ln2_b = tot[6:7], tot[7:8], tot[8:9], tot[9:10]
    g_hg_norm = tot[10:11, 512:1024]
    g_q_norm = tot[11:12, 0:Q_RANK]
    g_kv_norm = tot[11:12, Q_RANK:Q_RANK + KV_RANK]

    d_mod_all = small_all[:, 0:6, :].reshape(N_DEV, 6 * D_MODEL)
    d_mod_cols = lax.dynamic_slice(d_mod_all, (0, chip * ada_cols), (N_DEV, ada_cols))
    d_mod_cols = jnp.concatenate([d_mod_cols, jnp.zeros_like(d_mod_cols)], axis=0)
    g_w_ada = _weight_grad(cond16, d_mod_cols, "grad_w_ada", 1, ada_cols)[0]

    names = ["w_ada", "b_ada", "w_in", "hg_lower_bounds", "hg_norm_w", "mla_q_norm_w", "w_q_up", "mla_kv_norm_w",
             "w_kv_up", "w_out", "ln1_g", "ln1_b", "w_mlp_in", "w_mlp_out", "ln2_g", "ln2_b"]
    weights = [w_ada, b_ada, w_in, hg_lower_bounds, hg_norm_w, mla_q_norm_w, w_q_up, mla_kv_norm_w,
               w_kv_up, w_out, ln1_g, ln1_b, w_mlp_in, w_mlp_out, ln2_g, ln2_b]
    moms = [m_w_ada, m_b_ada, m_w_in, m_hg_lower_bounds, m_hg_norm_w, m_mla_q_norm_w, m_w_q_up, m_mla_kv_norm_w,
            m_w_kv_up, m_w_out, m_ln1_g, m_ln1_b, m_w_mlp_in, m_w_mlp_out, m_ln2_g, m_ln2_b]
    vels = [v_w_ada, v_b_ada, v_w_in, v_hg_lower_bounds, v_hg_norm_w, v_mla_q_norm_w, v_w_q_up, v_mla_kv_norm_w,
            v_w_kv_up, v_w_out, v_ln1_g, v_ln1_b, v_w_mlp_in, v_w_mlp_out, v_ln2_g, v_ln2_b]
    grads2d = [g_w_ada, g_b_ada, None, g_lb, g_hg_norm, g_q_norm, None, g_kv_norm,
               None, None, g_ln1_g, g_ln1_b, None, None, g_ln2_g, g_ln2_b]
    out_g, out_d, out_m, out_v = [], [], [], []
    for name, w, g, m, vv in zip(names, weights, grads2d, moms, vels):
        if name == "w_in":
            to2d, back = (lambda a: jnp.transpose(a[0])), (lambda a: jnp.transpose(a)[None])
        else:
            shape2 = w.shape[1:] if g is None else g.shape
            to2d, back = (lambda a, s=shape2: a.reshape(s)), (lambda a, s=w.shape: a.reshape(s))
        if g is None:
            g, d, nm, nv = _adamw_halves(core_arr, to2d(w), halves_mine[name], halves_theirs[name], to2d(m), to2d(vv),
                                         "adamw_" + name)
        else:
            d, nm, nv = _adamw(to2d(w), g, to2d(m), to2d(vv), "adamw_" + name)
        out_g.append(back(g))
        out_d.append(back(d))
        out_m.append(back(nm))
        out_v.append(back(nv))
    return (loss, grad_x[None], *out_g, *out_d, *out_m, *out_v)
```

```python
import functools

import jax
import jax.numpy as jnp
from jax import lax
from jax.experimental import pallas as pl
from jax.experimental.pallas import tpu as pltpu

F32 = jnp.float32
BF16 = jnp.bfloat16
MESH_IDS = pl.DeviceIdType.MESH

D_MODEL = 1024
N_HEADS = 4
HEAD_DIM = 128
ROPE_DIM = 64
HG_CHUNK = 64
HG_COLS = 2048
Q_RANK = 256
KV_RANK = 256
IN_COLS = 2624
IN_COLS_PAD = 2688
QK_DIM = 256
D_FF = 4096
N_CHIPS = 4
N_DEV = 8
ROPE_THETA = 10000.0
RMS_EPS = 1e-6
LN_EPS = 1e-5
DN_ALPHA = 2.0 ** 0.25
ATT_SCALE = (HEAD_DIM + ROPE_DIM) ** -0.5
NEG_BIG = -1e30
ADAM_LR = 0.001
ADAM_B1 = 0.9
ADAM_B2 = 0.999
ADAM_EPS = 1e-08
ADAM_WD = 0.01
ADAM_STEP = 10
SMALL_ROWS = 16
MIB = 1024 * 1024


def _dot(a, b):
    return jnp.dot(a, b, preferred_element_type=F32)


def _dot_nt(a, b):
    return lax.dot_general(a, b, (((1,), (1,)), ((), ())), preferred_element_type=F32)


def _dot_tn(a, b):
    return lax.dot_general(a, b, (((0,), (0,)), ((), ())), preferred_element_type=F32)


def _params(vmem_mib, semantics=None):
    return pltpu.CompilerParams(vmem_limit_bytes=vmem_mib * MIB, dimension_semantics=semantics)


def _sigmoid(v):
    return 1.0 / (1.0 + jnp.exp(-v))


def _colsum(v):
    return jnp.sum(v, axis=0, keepdims=True)


def _rowmean(v):
    return jnp.mean(v, axis=-1, keepdims=True)


def _rope_tables(pos, invf):
    ang = pos * invf
    lane = lax.broadcasted_iota(jnp.int32, ang.shape, 1)
    cos_t = jnp.where(lane < ROPE_DIM, jnp.cos(ang), 0.0)
    sin = jnp.sin(ang)
    sin_t = jnp.where(lane < ROPE_DIM // 2, -sin, jnp.where(lane < ROPE_DIM, sin, 0.0))
    return cos_t, sin_t


def _swap_halves(t):
    lane = lax.broadcasted_iota(jnp.int32, t.shape, 1)
    return jnp.where(lane < ROPE_DIM // 2, pltpu.roll(t, 128 - ROPE_DIM // 2, 1), pltpu.roll(t, ROPE_DIM // 2, 1))


def _rope(t, cos_t, sin_t):
    return t * cos_t + _swap_halves(t) * sin_t


def _unrope(g, cos_t, sin_t):
    return g * cos_t - _swap_halves(g) * sin_t


def _mesh_pos():
    return lax.axis_index("x"), lax.axis_index("y"), lax.axis_index("c")


def _other_chips(x, y):
    out = []
    for dx, dy in ((1, 0), (0, 1), (1, 1)):
        px = 1 - x if dx else x
        py = 1 - y if dy else y
        out.append(((px, py), 2 * px + py))
    return out


def _allgather8(a, name):
    rows, cols = a.shape

    def body(a_ref, out_ref, send_sems, recv_sems):
        x, y, c = _mesh_pos()
        me = 4 * x + 2 * y + c
        out_ref[me] = a_ref[...]
        peers = []
        for r in range(1, N_DEV):
            px = 1 - x if r & 4 else x
            py = 1 - y if r & 2 else y
            pc = 1 - c if r & 1 else c
            peers.append(((px, py, pc), 4 * px + 2 * py + pc))

        def copy(r, block, to):
            return pltpu.make_async_remote_copy(
                src_ref=a_ref, dst_ref=out_ref.at[block], send_sem=send_sems.at[r], recv_sem=recv_sems.at[r],
                device_id=to, device_id_type=MESH_IDS)

        sends = [copy(r, me, peer) for r, (peer, _) in enumerate(peers)]
        for cp in sends:
            cp.start()
        for r, (peer, idx) in enumerate(peers):
            copy(r, idx, peer).wait_recv()
        for cp in sends:
            cp.wait_send()

    return pl.pallas_call(
        body, name=name,
        out_shape=jax.ShapeDtypeStruct((N_DEV, rows, cols), a.dtype),
        in_specs=[pl.BlockSpec(memory_space=pltpu.VMEM)],
        out_specs=pl.BlockSpec(memory_space=pltpu.VMEM),
        scratch_shapes=[pltpu.SemaphoreType.DMA((N_DEV - 1,)), pltpu.SemaphoreType.DMA((N_DEV - 1,))],
    )(a)


class _Exchange:
    def __init__(self, inputs, out_shapes, aliases, sems, start, finish):
        self.inputs, self.out_shapes, self.aliases, self.sems = list(inputs), list(out_shapes), dict(aliases), list(sems)
        self.start, self.finish = start, finish


def _from_copies(inputs, out_shapes, aliases, sems, copies):
    def start(ins, outs, sem_refs):
        for send, _ in copies(ins, outs, sem_refs):
            send.start()

    def finish(ins, outs, sem_refs):
        for send, recv in copies(ins, outs, sem_refs):
            recv.wait_recv()
            send.wait_send()

    return _Exchange(inputs, out_shapes, aliases, sems, start, finish)


HBM_MIN_BYTES = 256 * 1024


def _in_hbm(a):
    if a.size * a.dtype.itemsize < HBM_MIN_BYTES:
        return a
    return pltpu.with_memory_space_constraint(a, pltpu.HBM)


def _out_hbm(s):
    if s.size * s.dtype.itemsize < HBM_MIN_BYTES:
        return s
    return pltpu.HBM(s.shape, s.dtype)


def _pcall(body, *, operands, out_shape, **kwargs):
    single = not isinstance(out_shape, (list, tuple))
    shapes = [_out_hbm(s) for s in ([out_shape] if single else out_shape)]
    return pl.pallas_call(body, out_shape=shapes[0] if single else shapes, **kwargs)(*[_in_hbm(a) for a in operands])


def _run_exchange(exchange, name):
    n_in, n_out = len(exchange.inputs), len(exchange.out_shapes)

    def body(*refs):
        ins, outs, sem_refs = refs[:n_in], refs[n_in:n_in + n_out], refs[n_in + n_out:]
        exchange.start(ins, outs, sem_refs)
        exchange.finish(ins, outs, sem_refs)

    any_spec = pl.BlockSpec(memory_space=pl.ANY)
    return pl.pallas_call(
        body, name=name, out_shape=[_out_hbm(s) for s in exchange.out_shapes],
        in_specs=[any_spec] * n_in, out_specs=[any_spec] * n_out,
        scratch_shapes=exchange.sems, input_output_aliases=exchange.aliases,
    )(*[_in_hbm(a) for a in exchange.inputs])


def _pallas(body, *, name, operands, in_specs, out_shape, out_specs, params, scratch_shapes=(), grid=(), prefetch=(),
            exchange=None, first=None, last=None):
    n_pre, n_in, n_out, n_scr = len(prefetch), len(in_specs), len(out_specs), len(scratch_shapes)
    ex_in = exchange.inputs if exchange else []
    ex_out = exchange.out_shapes if exchange else []
    ex_sems = exchange.sems if exchange else []

    def full_body(*refs):
        pre, rest = refs[:n_pre], refs[n_pre:]
        ins, rest = rest[:n_in], rest[n_in:]
        xin, rest = rest[:len(ex_in)], rest[len(ex_in):]
        outs, rest = rest[:n_out], rest[n_out:]
        xout, rest = rest[:len(ex_out)], rest[len(ex_out):]
        scr, sem_refs = rest[:n_scr], rest[n_scr:]
        if exchange:
            @pl.when(first(*pre))
            def _():
                exchange.start(xin, xout, sem_refs)

        body(*pre, *ins, *outs, *scr)
        if exchange:
            @pl.when(last(*pre))
            def _():
                exchange.finish(xin, xout, sem_refs)

    any_spec = pl.BlockSpec(memory_space=pl.ANY)
    aliases = {n_pre + n_in + i: n_out + o for i, o in exchange.aliases.items()} if exchange else {}
    operands = [_in_hbm(a) for a in operands]
    results = pl.pallas_call(
        full_body, name=name, out_shape=[_out_hbm(s) for s in list(out_shape) + ex_out],
        grid_spec=pltpu.PrefetchScalarGridSpec(
            num_scalar_prefetch=n_pre, grid=grid, in_specs=list(in_specs) + [any_spec] * len(ex_in),
            out_specs=list(out_specs) + [any_spec] * len(ex_out), scratch_shapes=list(scratch_shapes) + ex_sems),
        input_output_aliases=aliases, compiler_params=params,
    )(*prefetch, *operands, *[_in_hbm(a) for a in ex_in])
    return results[:n_out], results[n_out:]


def _remote(src, dst, sems, idx, to):
    send_sems, recv_sems = sems
    return pltpu.make_async_remote_copy(src_ref=src, dst_ref=dst, send_sem=send_sems.at[idx], recv_sem=recv_sems.at[idx],
                                        device_id=to, device_id_type=MESH_IDS)


def _sem_pairs(*shape):
    return [pltpu.SemaphoreType.DMA(shape), pltpu.SemaphoreType.DMA(shape)]


def _same_shapes(arrays):
    return [jax.ShapeDtypeStruct(a.shape, a.dtype) for a in arrays]


def _gather_over_ici(slots):
    n = len(slots)

    def copies(ins, outs, sems):
        x, y, c = _mesh_pos()
        k = 2 * x + y
        out = []
        for j, (chip, kj) in enumerate(_other_chips(x, y)):
            for i in range(n):
                to = (*chip, c)
                out.append((_remote(ins[i].at[k, c], outs[i].at[k, c], sems, (j, i), to),
                            _remote(ins[i].at[k, c], outs[i].at[kj, c], sems, (j, i), to)))
        return out

    return _from_copies(slots, _same_shapes(slots), {i: i for i in range(n)}, _sem_pairs(3, n), copies)


def _gather_over_d2d(slots):
    n = len(slots)

    def copies(ins, outs, sems):
        x, y, c = _mesh_pos()
        sibling = (x, y, 1 - c)
        out = []
        for j, (_, kj) in enumerate(_other_chips(x, y)):
            for i in range(n):
                out.append((_remote(ins[i].at[kj, c], outs[i].at[kj, c], sems, (j, i), sibling),
                            _remote(ins[i].at[kj, c], outs[i].at[kj, 1 - c], sems, (j, i), sibling)))
        return out

    return _from_copies(slots, _same_shapes(slots), {i: i for i in range(n)}, _sem_pairs(3, n), copies)


def _gather_all(slots8):
    def copies(ins, outs, sems):
        x, y, c = _mesh_pos()
        me = 4 * x + 2 * y + c
        out = []
        for r in range(1, N_DEV):
            px = 1 - x if r & 4 else x
            py = 1 - y if r & 2 else y
            pc = 1 - c if r & 1 else c
            to = (px, py, pc)
            out.append((_remote(ins[0].at[me], outs[0].at[me], sems, r - 1, to),
                        _remote(ins[0].at[me], outs[0].at[4 * px + 2 * py + pc], sems, r - 1, to)))
        return out

    return _from_copies([slots8], _same_shapes([slots8]), {0: 0}, _sem_pairs(N_DEV - 1), copies)


def _merge(first, second):
    n_in, n_out, n_sem = len(first.inputs), len(first.out_shapes), len(first.sems)

    def start(ins, outs, sems):
        first.start(ins[:n_in], outs[:n_out], sems[:n_sem])
        second.start(ins[n_in:], outs[n_out:], sems[n_sem:])

    def finish(ins, outs, sems):
        first.finish(ins[:n_in], outs[:n_out], sems[:n_sem])
        second.finish(ins[n_in:], outs[n_out:], sems[n_sem:])

    aliases = dict(first.aliases)
    aliases.update({n_in + i: n_out + o for i, o in second.aliases.items()})
    return _Exchange(first.inputs + second.inputs, first.out_shapes + second.out_shapes, aliases,
                     first.sems + second.sems, start, finish)


def _pair_exchange(grads):
    n = len(grads)

    def copies(ins, outs, sems):
        x, y, c = _mesh_pos()
        cps = [_remote(ins[i].at[:, 1 - c], outs[i], sems, i, (x, y, 1 - c)) for i in range(n)]
        return [(cp, cp) for cp in cps]

    shapes = [jax.ShapeDtypeStruct((N_CHIPS,) + g.shape[2:], g.dtype) for g in grads]
    return _from_copies(grads, shapes, {}, _sem_pairs(n), copies)


def _chip_exchange(partials):
    n = len(partials)

    def copies(ins, outs, sems):
        x, y, c = _mesh_pos()
        cps = [_remote(ins[i].at[kj], outs[i].at[j], sems, (j, i), (*chip, c))
               for j, (chip, kj) in enumerate(_other_chips(x, y)) for i in range(n)]
        return [(cp, cp) for cp in cps]

    shapes = [jax.ShapeDtypeStruct((3,) + p.shape[1:], p.dtype) for p in partials]
    return _from_copies(partials, shapes, {}, _sem_pairs(3, n), copies)


def _pair_send(halves):
    n = len(halves)

    def copies(ins, outs, sems):
        x, y, c = _mesh_pos()
        cps = [_remote(ins[i], outs[i], sems, i, (x, y, 1 - c)) for i in range(n)]
        return [(cp, cp) for cp in cps]

    return _from_copies(halves, _same_shapes(halves), {}, _sem_pairs(n), copies)


def _reduce_in_vmem(grads, half_rows, name):
    n = len(grads)

    def body(*refs):
        g, mine, theirs = refs[:n], refs[n:2 * n], refs[2 * n:3 * n]
        landed_pair, partial, landed_chips = refs[3 * n:4 * n], refs[4 * n:5 * n], refs[5 * n:6 * n]
        sems = refs[6 * n:]
        x, y, c = _mesh_pos()
        k = 2 * x + y
        sibling = (x, y, 1 - c)

        def half(i, chip_idx, which):
            return pl.ds(pl.multiple_of((2 * chip_idx + which) * half_rows[i], 8), half_rows[i])

        def run(copies):
            for cp in copies:
                cp.start()
            for cp in copies:
                cp.wait_recv()
                cp.wait_send()

        run([_remote(g[i].at[half(i, kk, 1 - c)], landed_pair[i].at[kk], sems[0:2], (kk, i), sibling)
             for kk in range(N_CHIPS) for i in range(n)])
        for i in range(n):
            for kk in range(N_CHIPS):
                partial[i][kk] = (g[i][half(i, kk, c), :] + landed_pair[i][kk]).astype(BF16)
        run([_remote(partial[i].at[kj], landed_chips[i].at[j], sems[2:4], (j, i), (*chip, c))
             for j, (chip, kj) in enumerate(_other_chips(x, y)) for i in range(n)])
        for i in range(n):
            own = g[i][half(i, k, c), :] + landed_pair[i][k]
            mine[i][...] = ((own + landed_chips[i][0].astype(F32)) + landed_chips[i][1].astype(F32)) \
                + landed_chips[i][2].astype(F32)
        run([_remote(mine[i], theirs[i], sems[4:6], i, sibling) for i in range(n)])

    shapes = [(h, gr.shape[1]) for gr, h in zip(grads, half_rows)]
    halves = [jax.ShapeDtypeStruct(s, F32) for s in shapes]
    vmem = pl.BlockSpec(memory_space=pltpu.VMEM)
    scratch = ([pltpu.VMEM((N_CHIPS,) + s, F32) for s in shapes]
               + [pltpu.VMEM((N_CHIPS,) + s, BF16) for s in shapes]
               + [pltpu.VMEM((3,) + s, BF16) for s in shapes]
               + _sem_pairs(N_CHIPS, n) + _sem_pairs(3, n) + _sem_pairs(n))
    out = pl.pallas_call(
        body, name=name, out_shape=halves + halves, in_specs=[vmem] * n, out_specs=[vmem] * (2 * n),
        scratch_shapes=scratch, compiler_params=_params(48),
    )(*grads)
    return out[:n], out[n:]


def _row_tile(rows):
    for t in (256, 128, 64):
        if rows % t == 0:
            return t
    return rows


def _add_pair(core, chip, grad, landed):
    _, h, cols = landed.shape
    tr = _row_tile(h)

    def body(core_ref, chip_ref, g_ref, l_ref, own_ref, ob_ref):
        s = g_ref[...] + l_ref[...]
        ob_ref[...] = s.astype(BF16)

        @pl.when(pl.program_id(1) == chip_ref[0])
        def _():
            own_ref[...] = s

    return _pcall(
        body, name="grad_add_pair",
        out_shape=[jax.ShapeDtypeStruct((h, cols), F32), jax.ShapeDtypeStruct(landed.shape, BF16)],
        grid_spec=pltpu.PrefetchScalarGridSpec(
            num_scalar_prefetch=2, grid=(h // tr, N_CHIPS),
            in_specs=[pl.BlockSpec((None, None, tr, cols), lambda t, k, core_ref, chip_ref: (k, core_ref[0], t, 0)),
                      pl.BlockSpec((None, tr, cols), lambda t, k, core_ref, chip_ref: (k, t, 0))],
            out_specs=[pl.BlockSpec((tr, cols), lambda t, k, core_ref, chip_ref: (t, 0)),
                       pl.BlockSpec((None, tr, cols), lambda t, k, core_ref, chip_ref: (k, t, 0))]),
        compiler_params=_params(32, ("arbitrary", "arbitrary")),
        operands=(core, chip, grad, landed))


def _add_chips(own, landed):
    h, cols = own.shape
    tr = _row_tile(h)

    def body(p_ref, l_ref, o_ref):
        o_ref[...] = ((p_ref[...] + l_ref[0].astype(F32)) + l_ref[1].astype(F32)) + l_ref[2].astype(F32)

    return _pcall(
        body, name="grad_add_chips", grid=(h // tr,),
        out_shape=jax.ShapeDtypeStruct((h, cols), F32),
        in_specs=[pl.BlockSpec((tr, cols), lambda t: (t, 0)), pl.BlockSpec((3, tr, cols), lambda t: (0, t, 0))],
        out_specs=pl.BlockSpec((tr, cols), lambda t: (t, 0)),
        compiler_params=_params(32, ("arbitrary",)),
        operands=(own, landed))


def _ada_project(c_all, w_ada, b_shard):
    n = w_ada.shape[1]
    tn = 512

    def body(c_ref, w_ref, b_ref, mod_ref, cond_ref):
        cv = c_ref[...]
        cond = cv * _sigmoid(cv)
        mod_ref[...] = _dot(cond.astype(BF16), w_ref[...].astype(BF16)) + b_ref[...]
        cond_ref[0:N_DEV, :] = cond
        cond_ref[N_DEV:2 * N_DEV, :] = jnp.zeros_like(cond)

    return _pcall(
        body, name="ada_project", grid=(n // tn,),
        out_shape=[jax.ShapeDtypeStruct((N_DEV, n), F32), jax.ShapeDtypeStruct((2 * N_DEV, D_MODEL), F32)],
        in_specs=[pl.BlockSpec((N_DEV, D_MODEL), lambda j: (0, 0)), pl.BlockSpec((D_MODEL, tn), lambda j: (0, j)),
                  pl.BlockSpec((1, tn), lambda j: (0, j))],
        out_specs=[pl.BlockSpec((N_DEV, tn), lambda j: (0, j)), pl.BlockSpec((2 * N_DEV, D_MODEL), lambda j: (0, 0))],
        compiler_params=_params(32, ("arbitrary",)),
        operands=(c_all, w_ada, b_shard))


def _in_project(x, pos, sc_a, sh_a, w_in, q_norm_w, kv_norm_w, w_q, w_kv, invf, exchange=None):
    t_len = x.shape[0]
    tm = min(512, t_len)

    def body(x_ref, pos_ref, sc_ref, sh_ref, win_ref, qn_ref, kvn_ref, wq_ref, wkv_ref, invf_ref,
             u_ref, zhg_ref, cq_ref, ckv_ref, q_ref, k_ref, kt_ref, v_ref, vt_ref):
        u = (x_ref[...] * (1.0 + sc_ref[...]) + sh_ref[...]).astype(BF16)
        u_ref[...] = u
        z = _dot_nt(u, win_ref[...])
        zhg_ref[...] = z[:, :HG_COLS]
        cq = z[:, HG_COLS:HG_COLS + Q_RANK]
        ckv = z[:, HG_COLS + Q_RANK:HG_COLS + Q_RANK + KV_RANK]
        cq_ref[...] = cq
        ckv_ref[...] = ckv
        cos_t, sin_t = _rope_tables(pos_ref[...], invf_ref[...])
        k_pe = _rope(z[:, HG_COLS + Q_RANK + KV_RANK:], cos_t, sin_t)
        k_pe_t = jnp.transpose(k_pe).astype(BF16)
        cqn = (cq * lax.rsqrt(_rowmean(cq * cq) + RMS_EPS) * qn_ref[...]).astype(BF16)
        ckvn = (ckv * lax.rsqrt(_rowmean(ckv * ckv) + RMS_EPS) * kvn_ref[...]).astype(BF16)
        for h in range(N_HEADS):
            qh = _dot(cqn, wq_ref[h])
            q_ref[h, :, 0:HEAD_DIM] = qh[:, :HEAD_DIM].astype(BF16)
            q_ref[h, :, HEAD_DIM:QK_DIM] = _rope(qh[:, HEAD_DIM:], cos_t, sin_t).astype(BF16)
            kvh = _dot(ckvn, wkv_ref[h])
            k_ref[h, :, 0:HEAD_DIM] = kvh[:, :HEAD_DIM].astype(BF16)
            k_ref[h, :, HEAD_DIM:QK_DIM] = k_pe.astype(BF16)
            kt_ref[h, 0:HEAD_DIM, :] = jnp.transpose(kvh[:, :HEAD_DIM]).astype(BF16)
            kt_ref[h, HEAD_DIM:QK_DIM, :] = k_pe_t
            v_ref[h] = kvh[:, HEAD_DIM:].astype(BF16)
            vt_ref[h] = jnp.transpose(kvh[:, HEAD_DIM:]).astype(BF16)

    row = lambda i: (i, 0)
    fixed2 = lambda i: (0, 0)
    fixed3 = lambda i: (0, 0, 0)
    heads = lambda i: (0, i, 0)
    n_tiles = t_len // tm
    return _pallas(
        body, name="in_project", grid=(n_tiles,),
        operands=(x, pos, sc_a, sh_a, w_in, q_norm_w, kv_norm_w, w_q, w_kv, invf),
        out_shape=[jax.ShapeDtypeStruct((t_len, D_MODEL), BF16), jax.ShapeDtypeStruct((t_len, HG_COLS), F32),
                   jax.ShapeDtypeStruct((t_len, Q_RANK), F32), jax.ShapeDtypeStruct((t_len, KV_RANK), F32),
                   jax.ShapeDtypeStruct((N_HEADS, t_len, QK_DIM), BF16),
                   jax.ShapeDtypeStruct((N_HEADS, t_len, QK_DIM), BF16),
                   jax.ShapeDtypeStruct((N_HEADS, QK_DIM, t_len), BF16),
                   jax.ShapeDtypeStruct((N_HEADS, t_len, HEAD_DIM), BF16),
                   jax.ShapeDtypeStruct((N_HEADS, HEAD_DIM, t_len), BF16)],
        in_specs=[pl.BlockSpec((tm, D_MODEL), row), pl.BlockSpec((tm, 1), row),
                  pl.BlockSpec((1, D_MODEL), fixed2), pl.BlockSpec((1, D_MODEL), fixed2),
                  pl.BlockSpec((IN_COLS_PAD, D_MODEL), fixed2),
                  pl.BlockSpec((1, Q_RANK), fixed2), pl.BlockSpec((1, KV_RANK), fixed2),
                  pl.BlockSpec((N_HEADS, Q_RANK, QK_DIM), fixed3), pl.BlockSpec((N_HEADS, KV_RANK, 2 * HEAD_DIM), fixed3),
                  pl.BlockSpec((1, 128), fixed2)],
        out_specs=[pl.BlockSpec((tm, D_MODEL), row), pl.BlockSpec((tm, HG_COLS), row),
                   pl.BlockSpec((tm, Q_RANK), row), pl.BlockSpec((tm, KV_RANK), row),
                   pl.BlockSpec((N_HEADS, tm, QK_DIM), heads), pl.BlockSpec((N_HEADS, tm, QK_DIM), heads),
                   pl.BlockSpec((N_HEADS, QK_DIM, tm), lambda i: (0, 0, i)),
                   pl.BlockSpec((N_HEADS, tm, HEAD_DIM), heads),
                   pl.BlockSpec((N_HEADS, HEAD_DIM, tm), lambda i: (0, 0, i))],
        params=_params(48, ("arbitrary",)), exchange=exchange,
        first=lambda: pl.program_id(0) == 0, last=lambda: pl.program_id(0) == n_tiles - 1)


def _lower_bound(lb_raw):
    m = jnp.max(lb_raw, axis=0, keepdims=True)
    e = jnp.exp(lb_raw - m)
    return e[0:1] / jnp.sum(e, axis=0, keepdims=True)


def _tri(inclusive_lower):
    r = lax.broadcasted_iota(jnp.int32, (HG_CHUNK, HG_CHUNK), 0)
    c = lax.broadcasted_iota(jnp.int32, (HG_CHUNK, HG_CHUNK), 1)
    return (c <= r) if inclusive_lower else (c >= r)


def _chunk_rows(n):
    return slice(n * HG_CHUNK, (n + 1) * HG_CHUNK)


def _chunk_prefix_sums(v, inclusive_lower):
    tri = _tri(inclusive_lower).astype(BF16)
    hi = v.astype(BF16)
    rest = v - hi.astype(F32)
    mid = rest.astype(BF16)
    lo = (rest - mid.astype(F32)).astype(BF16)
    pieces = jnp.concatenate([hi, mid, lo], axis=1)
    out = []
    for n in range(v.shape[0] // HG_CHUNK):
        s = _dot(tri, pieces[_chunk_rows(n)])
        out.append((s[:, 0:HEAD_DIM] + s[:, HEAD_DIM:2 * HEAD_DIM]) + s[:, 2 * HEAD_DIM:])
    return jnp.concatenate(out, axis=0)


def _per_chunk(v, row):
    n = v.shape[0] // HG_CHUNK
    v3 = v.reshape(n, HG_CHUNK, HEAD_DIM)
    return jnp.broadcast_to(v3[:, row:row + 1, :], v3.shape).reshape(v.shape)


def _hg_block(q, f_logit, lb):
    sg = _sigmoid(f_logit)
    forget = lb + (1.0 - lb) * sg
    kk = 1.0 - forget
    b = _chunk_prefix_sums(jnp.log(forget), True)
    b_ref = _per_chunk(b, HG_CHUNK // 2 - 1)
    b_last = _per_chunk(b, HG_CHUNK - 1)
    e_i = jnp.exp(b - b_ref)
    e_ri = jnp.exp(b_ref - b)
    e_b = jnp.exp(b)
    e_l = jnp.exp(b_last - b)
    return dict(sg=sg, forget=forget, e_i=e_i, e_ri=e_ri, e_b=e_b, e_l=e_l, dec=jnp.exp(b_last),
                qi=q * e_i, ki=kk * e_ri, qe=q * e_b, kl=kk * e_l)


HG_STEP_HEADS = 4


def _head_cols(hh):
    return slice(hh * HEAD_DIM, (hh + 1) * HEAD_DIM)


def _hgrn_forward(zhg, lb_raw, norm_w, exchange=None):
    t_len = zhg.shape[0]
    tb = min(512, t_len)
    n_chunks = tb // HG_CHUNK
    hs = HG_STEP_HEADS

    def body(q_ref, f_ref, v_ref, g_ref, lb_ref, w_ref, opre_ref, o_ref, st_ref, state):
        @pl.when(pl.program_id(1) == 0)
        def _():
            state[...] = jnp.zeros_like(state)

        causal = _tri(True)
        for hh in range(hs):
            cols = _head_cols(hh)
            blk = _hg_block(q_ref[:, cols], f_ref[:, cols], _lower_bound(lb_ref[:, cols]))
            v = v_ref[:, cols].astype(BF16)
            qi, ki, qe, kl = (blk[name].astype(BF16) for name in ("qi", "ki", "qe", "kl"))
            st = state[hh]
            parts = []
            for n in range(n_chunks):
                r = _chunk_rows(n)
                a = jnp.where(causal, _dot_nt(qi[r], ki[r]), 0.0).astype(BF16)
                st_ref[hh, n] = st
                parts.append(_dot(a, v[r]) + _dot_nt(qe[r], st.astype(BF16)))
                st = st * blk["dec"][n * HG_CHUNK:n * HG_CHUNK + 1] + _dot_tn(v[r], kl[r])
            state[hh] = st
            o = jnp.concatenate(parts, axis=0)
            opre_ref[:, cols] = o
            g = g_ref[:, cols]
            o_ref[:, cols] = o * lax.rsqrt(_rowmean(o * o) + RMS_EPS) * w_ref[:, cols] * (g * _sigmoid(g))

    groups = N_HEADS // hs
    wide = hs * HEAD_DIM
    col = lambda off: (lambda h, t: (t, off + h))
    nb = t_len // tb
    return _pallas(
        body, name="hgrn_forward", grid=(groups, nb), operands=(zhg, zhg, zhg, zhg, lb_raw, norm_w),
        out_shape=[jax.ShapeDtypeStruct((t_len, N_HEADS * HEAD_DIM), F32),
                   jax.ShapeDtypeStruct((t_len, N_HEADS * HEAD_DIM), F32),
                   jax.ShapeDtypeStruct((N_HEADS, t_len // HG_CHUNK, HEAD_DIM, HEAD_DIM), F32)],
        in_specs=[pl.BlockSpec((tb, wide), col(0)), pl.BlockSpec((tb, wide), col(groups)),
                  pl.BlockSpec((tb, wide), col(2 * groups)), pl.BlockSpec((tb, wide), col(3 * groups)),
                  pl.BlockSpec((2, wide), lambda h, t: (0, h)), pl.BlockSpec((1, wide), lambda h, t: (0, h))],
        out_specs=[pl.BlockSpec((tb, wide), col(0)), pl.BlockSpec((tb, wide), col(0)),
                   pl.BlockSpec((hs, n_chunks, HEAD_DIM, HEAD_DIM), lambda h, t: (h, t, 0, 0))],
        scratch_shapes=[pltpu.VMEM((hs, HEAD_DIM, HEAD_DIM), F32)],
        params=_params(40, ("arbitrary", "arbitrary")), exchange=exchange,
        first=lambda: (pl.program_id(0) == 0) & (pl.program_id(1) == 0),
        last=lambda: (pl.program_id(0) == groups - 1) & (pl.program_id(1) == nb - 1))


def _hgrn_backward(zhg, lb_raw, norm_w, o_pre, d_cat, states, exchange=None):
    t_len = zhg.shape[0]
    tb = min(512, t_len)
    n_chunks = tb // HG_CHUNK
    nb = t_len // tb
    hs = HG_STEP_HEADS

    def head(hh, q_ref, f_ref, v_ref, g_ref, lb_ref, w_ref, opre_ref, do_ref, st_ref,
             dq_ref, df_ref, dv_ref, dg_ref, sums_ref, gstate):
        cols = _head_cols(hh)
        lb = _lower_bound(lb_ref[:, cols])
        w = w_ref[:, cols]
        o = opre_ref[:, cols]
        g = g_ref[:, cols]
        d_out = do_ref[:, cols]
        r = lax.rsqrt(_rowmean(o * o) + RMS_EPS)
        sg_g = _sigmoid(g)
        dg_ref[:, cols] = (d_out * (o * r * w) * (sg_g * (1.0 + g * (1.0 - sg_g)))).astype(BF16)
        d_on = d_out * (g * sg_g)
        sums_ref[1:2, cols] += _colsum(d_on * o * r)
        dy = d_on * w
        d_o = (r * dy - o * (r * r * r) * _rowmean(dy * o)).astype(BF16)
        blk = _hg_block(q_ref[:, cols], f_ref[:, cols], lb)
        v = v_ref[:, cols].astype(BF16)
        qi, ki, qe, kl = (blk[name].astype(BF16) for name in ("qi", "ki", "qe", "kl"))
        causal = _tri(True)
        row_id = lax.broadcasted_iota(jnp.int32, (HG_CHUNK, HEAD_DIM), 0)
        gt = gstate[hh]
        d_v, d_qi, d_ki, d_qe, d_kl, d_dec = ([None] * n_chunks for _ in range(6))
        for n in reversed(range(n_chunks)):
            rows = _chunk_rows(n)
            st = st_ref[hh, n]
            a = jnp.where(causal, _dot_nt(qi[rows], ki[rows]), 0.0).astype(BF16)
            d_a = jnp.where(causal, _dot_nt(d_o[rows], v[rows]), 0.0).astype(BF16)
            gt_b = gt.astype(BF16)
            d_v[n] = _dot_tn(a, d_o[rows]) + _dot_nt(kl[rows], gt_b)
            d_qi[n] = _dot(d_a, ki[rows])
            d_ki[n] = _dot_tn(d_a, qi[rows])
            d_qe[n] = _dot(d_o[rows], st.astype(BF16))
            d_kl[n] = _dot(v[rows], gt_b)
            d_dec[n] = jnp.where(row_id == HG_CHUNK - 1, _colsum(gt * st), 0.0)
            gt = gt * blk["dec"][n * HG_CHUNK:n * HG_CHUNK + 1] + _dot_tn(d_o[rows], qe[rows])
        gstate[hh] = gt
        d_qi, d_ki, d_qe, d_kl, d_dec = (jnp.concatenate(p, axis=0) for p in (d_qi, d_ki, d_qe, d_kl, d_dec))
        dv_ref[:, cols] = jnp.concatenate(d_v, axis=0).astype(BF16)
        dq_ref[:, cols] = (d_qi * blk["e_i"] + d_qe * blk["e_b"]).astype(BF16)
        d_k = d_ki * blk["e_ri"] + d_kl * blk["e_l"]
        t_qi = d_qi * blk["qi"]
        t_ki = d_ki * blk["ki"]
        t_kl = d_kl * blk["kl"]
        at_ref, at_last = [], []
        for n in range(n_chunks):
            rows = _chunk_rows(n)
            at_ref.append(jnp.where(row_id == HG_CHUNK // 2 - 1, _colsum(t_ki[rows] - t_qi[rows]), 0.0))
            at_last.append(jnp.where(row_id == HG_CHUNK - 1, _colsum(t_kl[rows]), 0.0))
        d_b = (t_qi - t_ki + d_qe * blk["qe"] - t_kl + jnp.concatenate(at_ref, axis=0)
               + jnp.concatenate(at_last, axis=0) + d_dec * blk["dec"])
        d_forget = _chunk_prefix_sums(d_b, False) / blk["forget"] - d_k
        sg = blk["sg"]
        df_ref[:, cols] = (d_forget * (1.0 - lb) * sg * (1.0 - sg)).astype(BF16)
        sums_ref[0:1, cols] += _colsum(d_forget * (1.0 - sg))

    def body(*refs):
        sums_ref, gstate = refs[-2], refs[-1]

        @pl.when(pl.program_id(1) == 0)
        def _():
            gstate[...] = jnp.zeros_like(gstate)
            sums_ref[...] = jnp.zeros_like(sums_ref)

        for hh in range(hs):
            head(hh, *refs)

    groups = N_HEADS // hs
    wide = hs * HEAD_DIM
    col = lambda off: (lambda h, t: (nb - 1 - t, off + h))
    return _pallas(
        body, name="hgrn_backward", grid=(groups, nb),
        operands=(zhg, zhg, zhg, zhg, lb_raw, norm_w, o_pre, d_cat, states),
        out_shape=[jax.ShapeDtypeStruct((t_len, N_HEADS * HEAD_DIM), BF16)] * 4
        + [jax.ShapeDtypeStruct((8, N_HEADS * HEAD_DIM), F32)],
        in_specs=[pl.BlockSpec((tb, wide), col(0)), pl.BlockSpec((tb, wide), col(groups)),
                  pl.BlockSpec((tb, wide), col(2 * groups)), pl.BlockSpec((tb, wide), col(3 * groups)),
                  pl.BlockSpec((2, wide), lambda h, t: (0, h)), pl.BlockSpec((1, wide), lambda h, t: (0, h)),
                  pl.BlockSpec((tb, wide), col(0)), pl.BlockSpec((tb, wide), col(0)),
                  pl.BlockSpec((hs, n_chunks, HEAD_DIM, HEAD_DIM), lambda h, t: (h, nb - 1 - t, 0, 0))],
        out_specs=[pl.BlockSpec((tb, wide), col(0))] * 4 + [pl.BlockSpec((8, wide), lambda h, t: (0, h))],
        scratch_shapes=[pltpu.VMEM((hs, HEAD_DIM, HEAD_DIM), F32)],
        params=_params(40, ("arbitrary", "arbitrary")), exchange=exchange,
        first=lambda: (pl.program_id(0) == 0) & (pl.program_id(1) == 0),
        last=lambda: (pl.program_id(0) == groups - 1) & (pl.program_id(1) == nb - 1))


ATT_LOG2 = ATT_SCALE * 1.4426950408889634


def _triangle_steps(nq, q_major):
    if q_major:
        pairs = [(i, j) for i in range(nq) for j in range(i + 1)]
    else:
        pairs = [(i, j) for j in range(nq) for i in range(j, nq)]
    return jnp.array([p[0] for p in pairs], jnp.int32), jnp.array([p[1] for p in pairs], jnp.int32)


def _key_le_query(t):
    return lax.broadcasted_iota(jnp.int32, (t, t), 0) <= lax.broadcasted_iota(jnp.int32, (t, t), 1)


def _attention_forward(q, k, v_t, exchange=None):
    t_len = q.shape[1]
    tq = min(512, t_len)
    nq = t_len // tq
    qi_tab, ki_tab = _triangle_steps(nq, True)

    def body(qi_ref, ki_ref, q_ref, k_ref, vt_ref, o_ref, lse_ref, m_s, l_s, acc_s):
        step = pl.program_id(0)
        qi, ki = qi_ref[step], ki_ref[step]

        @pl.when(ki == 0)
        def _():
            m_s[...] = jnp.full_like(m_s, NEG_BIG)
            l_s[...] = jnp.zeros_like(l_s)
            acc_s[...] = jnp.zeros_like(acc_s)

        def accumulate(masked):
            for h in range(N_HEADS):
                s_t = _dot_nt(k_ref[h], q_ref[h]) * ATT_LOG2
                if masked:
                    s_t = jnp.where(_key_le_query(tq), s_t, NEG_BIG)
                m_old = m_s[h]
                m_new = jnp.maximum(m_old, jnp.max(s_t, axis=0, keepdims=True))
                alpha = jnp.exp2(m_old - m_new)
                p_t = jnp.exp2(s_t - m_new)
                l_s[h] = alpha * l_s[h] + jnp.sum(p_t, axis=0, keepdims=True)
                acc_s[h] = alpha * acc_s[h] + _dot(vt_ref[h], p_t.astype(BF16))
                m_s[h] = m_new

        @pl.when(ki < qi)
        def _():
            accumulate(False)

        @pl.when(ki == qi)
        def _():
            accumulate(True)
            for h in range(N_HEADS):
                o_ref[:, h * HEAD_DIM:(h + 1) * HEAD_DIM] = jnp.transpose(acc_s[h] / l_s[h])
                lse_ref[h] = m_s[h] + jnp.log2(l_s[h])

    n_steps = qi_tab.shape[0]
    return _pallas(
        body, name="attention_forward", grid=(n_steps,), prefetch=(qi_tab, ki_tab), operands=(q, k, v_t),
        out_shape=[jax.ShapeDtypeStruct((t_len, N_HEADS * HEAD_DIM), F32),
                   jax.ShapeDtypeStruct((N_HEADS, 1, t_len), F32)],
        in_specs=[pl.BlockSpec((N_HEADS, tq, QK_DIM), lambda s, qt, kt: (0, qt[s], 0)),
                  pl.BlockSpec((N_HEADS, tq, QK_DIM), lambda s, qt, kt: (0, kt[s], 0)),
                  pl.BlockSpec((N_HEADS, HEAD_DIM, tq), lambda s, qt, kt: (0, 0, kt[s]))],
        out_specs=[pl.BlockSpec((tq, N_HEADS * HEAD_DIM), lambda s, qt, kt: (qt[s], 0)),
                   pl.BlockSpec((N_HEADS, 1, tq), lambda s, qt, kt: (0, 0, qt[s]))],
        scratch_shapes=[pltpu.VMEM((N_HEADS, 1, tq), F32), pltpu.VMEM((N_HEADS, 1, tq), F32),
                        pltpu.VMEM((N_HEADS, HEAD_DIM, tq), F32)],
        params=_params(48, ("arbitrary",)), exchange=exchange,
        first=lambda qt, kt: pl.program_id(0) == 0, last=lambda qt, kt: pl.program_id(0) == n_steps - 1)


BWD_HEADS = 4


def _attention_backward(q, k, k_t, v, d_cat, lse, delta, exchange=None):
    t_len = q.shape[1]
    tq = min(512, t_len)
    nq = t_len // tq
    hp = BWD_HEADS
    qi_tab, ki_tab = _triangle_steps(nq, False)

    def body(qi_ref, ki_ref, q_ref, k_ref, kt_ref, v_ref, do_ref, lse_ref, delta_ref, dqt_hbm, dk_ref, dv_ref,
             dqt_s, dk_s, dv_s):
        group, step = pl.program_id(0), pl.program_id(1)
        qi, ki = qi_ref[step], ki_ref[step]

        @pl.when(step == 0)
        def _():
            dqt_s[...] = jnp.zeros_like(dqt_s)

        @pl.when(qi == ki)
        def _():
            dk_s[...] = jnp.zeros_like(dk_s)
            dv_s[...] = jnp.zeros_like(dv_s)

        def accumulate(masked):
            for h in range(hp):
                do_b = do_ref[:, h * HEAD_DIM:(h + 1) * HEAD_DIM].astype(BF16)
                s_t = _dot_nt(k_ref[h], q_ref[h]) * ATT_LOG2
                if masked:
                    s_t = jnp.where(_key_le_query(tq), s_t, NEG_BIG)
                p_t = jnp.exp2(s_t - lse_ref[h])
                dp_t = _dot_nt(v_ref[h], do_b)
                ds_t = (p_t * (dp_t - delta_ref[h]) * ATT_SCALE).astype(BF16)
                dv_s[h] += _dot(p_t.astype(BF16), do_b)
                dk_s[h] += _dot(ds_t, q_ref[h])
                dqt_s[h, qi] += _dot(kt_ref[h], ds_t)

        @pl.when(ki < qi)
        def _():
            accumulate(False)

        @pl.when(ki == qi)
        def _():
            accumulate(True)
            for h in range(hp):
                pltpu.sync_copy(dqt_s.at[h, qi], dqt_hbm.at[group * hp + h, qi])

        @pl.when(qi == nq - 1)
        def _():
            dk_ref[...] = dk_s[...]
            dv_ref[...] = dv_s[...]

    wide = hp * HEAD_DIM
    n_groups, n_steps = N_HEADS // hp, qi_tab.shape[0]
    return _pallas(
        body, name="attention_backward", grid=(n_groups, n_steps), prefetch=(qi_tab, ki_tab),
        operands=(q, k, k_t, v, d_cat, lse, delta),
        out_shape=[jax.ShapeDtypeStruct((N_HEADS, nq, QK_DIM, tq), F32),
                   jax.ShapeDtypeStruct((N_HEADS, t_len, QK_DIM), F32),
                   jax.ShapeDtypeStruct((N_HEADS, t_len, HEAD_DIM), F32)],
        in_specs=[pl.BlockSpec((hp, tq, QK_DIM), lambda g, s, qt, kt: (g, qt[s], 0)),
                  pl.BlockSpec((hp, tq, QK_DIM), lambda g, s, qt, kt: (g, kt[s], 0)),
                  pl.BlockSpec((hp, QK_DIM, tq), lambda g, s, qt, kt: (g, 0, kt[s])),
                  pl.BlockSpec((hp, tq, HEAD_DIM), lambda g, s, qt, kt: (g, kt[s], 0)),
                  pl.BlockSpec((tq, wide), lambda g, s, qt, kt: (qt[s], n_groups + g)),
                  pl.BlockSpec((hp, 1, tq), lambda g, s, qt, kt: (g, 0, qt[s])),
                  pl.BlockSpec((hp, 1, tq), lambda g, s, qt, kt: (g, 0, qt[s]))],
        out_specs=[pl.BlockSpec(memory_space=pl.ANY),
                   pl.BlockSpec((hp, tq, QK_DIM), lambda g, s, qt, kt: (g, kt[s], 0)),
                   pl.BlockSpec((hp, tq, HEAD_DIM), lambda g, s, qt, kt: (g, kt[s], 0))],
        scratch_shapes=[pltpu.VMEM((hp, nq, QK_DIM, tq), F32), pltpu.VMEM((hp, tq, QK_DIM), F32),
                        pltpu.VMEM((hp, tq, HEAD_DIM), F32)],
        params=_params(58, ("arbitrary", "arbitrary")), exchange=exchange,
        first=lambda qt, kt: (pl.program_id(0) == 0) & (pl.program_id(1) == 0),
        last=lambda qt, kt: (pl.program_id(0) == n_groups - 1) & (pl.program_id(1) == n_steps - 1))


def _out_project(o_hg, o_mla, x, g_a, w_out, exchange=None):
    t_len = x.shape[0]
    tm = min(512, t_len)
    half = N_HEADS * HEAD_DIM

    def body(ohg_ref, omla_ref, x_ref, ga_ref, w_ref, cat_ref, mix_ref, xhat_ref, rstd_ref):
        a = ohg_ref[...].astype(BF16)
        b = omla_ref[...].astype(BF16)
        cat_ref[:, 0:half] = a
        cat_ref[:, half:2 * half] = b
        mix = _dot(a, w_ref[0:half, :]) + _dot(b, w_ref[half:2 * half, :])
        mix_ref[...] = mix
        r1 = DN_ALPHA * x_ref[...] + (1.0 + ga_ref[...]) * mix
        xc = r1 - _rowmean(r1)
        rstd = lax.rsqrt(_rowmean(xc * xc) + LN_EPS)
        xhat_ref[...] = xc * rstd
        rstd_ref[...] = rstd

    row = lambda i: (i, 0)
    fixed = lambda i: (0, 0)
    n_tiles = t_len // tm
    return _pallas(
        body, name="out_project", grid=(n_tiles,), operands=(o_hg, o_mla, x, g_a, w_out),
        out_shape=[jax.ShapeDtypeStruct((t_len, D_MODEL), BF16), jax.ShapeDtypeStruct((t_len, D_MODEL), F32),
                   jax.ShapeDtypeStruct((t_len, D_MODEL), F32), jax.ShapeDtypeStruct((t_len, 1), F32)],
        in_specs=[pl.BlockSpec((tm, half), row), pl.BlockSpec((tm, half), row), pl.BlockSpec((tm, D_MODEL), row),
                  pl.BlockSpec((1, D_MODEL), fixed), pl.BlockSpec((D_MODEL, D_MODEL), fixed)],
        out_specs=[pl.BlockSpec((tm, D_MODEL), row), pl.BlockSpec((tm, D_MODEL), row),
                   pl.BlockSpec((tm, D_MODEL), row), pl.BlockSpec((tm, 1), row)],
        params=_params(48, ("arbitrary",)), exchange=exchange,
        first=lambda: pl.program_id(0) == 0, last=lambda: pl.program_id(0) == n_tiles - 1)


V_LN1G, V_LN1B, V_SCM, V_SHM, V_GM, V_GA, V_LN2G, V_LN2B = range(8)
S_DLN2G, S_DLN2B, S_DGM, S_DSCM, S_DSHM, S_DLN1G, S_DLN1B, S_DGA, S_LOSS = range(9)


def _mlp_and_back(xhat1, rstd1, mix, target, o_mla, vecs, w1_top, w1_bottom, w2, w_out):
    t_len = xhat1.shape[0]
    tm = min(256, t_len)
    n_ff = w1_top.shape[0]
    ff = w1_top.shape[2]
    top_rows = w1_top.shape[1]

    def body(xhat_ref, rstd_ref, mix_ref, tgt_ref, omla_ref, vec_ref, w1_top_hbm, w1_bottom_hbm, w2_hbm, wout_hbm,
             act_ref, dhp_ref, um_ref, dh_ref, dmix_ref, dcat_ref, dr1_ref, sums_ref, delta_ref,
             w1_s, w2_s, wout_s, hp_s, load_sems):
        @pl.when(pl.program_id(0) == 0)
        def _():
            loads = [pltpu.make_async_copy(w1_top_hbm, w1_s.at[:, 0:top_rows], load_sems.at[0]),
                     pltpu.make_async_copy(w1_bottom_hbm, w1_s.at[:, top_rows:D_MODEL], load_sems.at[3]),
                     pltpu.make_async_copy(w2_hbm, w2_s, load_sems.at[1]),
                     pltpu.make_async_copy(wout_hbm, wout_s, load_sems.at[2])]
            for cp in loads:
                cp.start()
            sums_ref[...] = jnp.zeros_like(sums_ref)
            for cp in loads:
                cp.wait()

        vec = lambda r: vec_ref[r:r + 1, :]
        xhat = xhat_ref[...]
        x1 = xhat * vec(V_LN1G) + vec(V_LN1B)
        um = (x1 * (1.0 + vec(V_SCM)) + vec(V_SHM)).astype(BF16)
        um_ref[...] = um
        h = jnp.zeros((tm, D_MODEL), F32)
        for j in range(n_ff):
            hp = _dot(um, w1_s[j])
            hp_s[j] = hp
            act = jnp.square(jnp.maximum(hp, 0.0)).astype(BF16)
            act_ref[:, j * ff:(j + 1) * ff] = act
            h = h + _dot(act, w2_s[j])
        r2 = DN_ALPHA * x1 + (1.0 + vec(V_GM)) * h
        xc = r2 - _rowmean(r2)
        rstd2 = lax.rsqrt(_rowmean(xc * xc) + LN_EPS)
        xhat2 = xc * rstd2
        err = xhat2 * vec(V_LN2G) + vec(V_LN2B) - tgt_ref[...]
        loss = 0.5 * jnp.sum(_rowmean(err * err))
        dy = err * (1.0 / D_MODEL)
        dxh = dy * vec(V_LN2G)
        dr2 = rstd2 * (dxh - _rowmean(dxh) - xhat2 * _rowmean(dxh * xhat2))
        dh = ((1.0 + vec(V_GM)) * dr2).astype(BF16)
        dh_ref[...] = dh
        sums_ref[S_DLN2G:S_DLN2G + 1, :] += _colsum(dy * xhat2)
        sums_ref[S_DLN2B:S_DLN2B + 1, :] += _colsum(dy)
        sums_ref[S_DGM:S_DGM + 1, :] += _colsum(dr2 * h)
        sums_ref[S_LOSS:S_LOSS + 1, :] += jnp.full((1, D_MODEL), loss, F32)
        du = jnp.zeros((tm, D_MODEL), F32)
        for j in range(n_ff):
            dhp = (_dot_nt(dh, w2_s[j]) * (2.0 * jnp.maximum(hp_s[j], 0.0))).astype(BF16)
            dhp_ref[:, j * ff:(j + 1) * ff] = dhp
            du = du + _dot_nt(dhp, w1_s[j])
        sums_ref[S_DSCM:S_DSCM + 1, :] += _colsum(du * x1)
        sums_ref[S_DSHM:S_DSHM + 1, :] += _colsum(du)
        dx1 = DN_ALPHA * dr2 + du * (1.0 + vec(V_SCM))
        sums_ref[S_DLN1G:S_DLN1G + 1, :] += _colsum(dx1 * xhat)
        sums_ref[S_DLN1B:S_DLN1B + 1, :] += _colsum(dx1)
        dxh1 = dx1 * vec(V_LN1G)
        dr1 = rstd_ref[...] * (dxh1 - _rowmean(dxh1) - xhat * _rowmean(dxh1 * xhat))
        dr1_ref[...] = dr1
        sums_ref[S_DGA:S_DGA + 1, :] += _colsum(dr1 * mix_ref[...])
        dmix = ((1.0 + vec(V_GA)) * dr1).astype(BF16)
        dmix_ref[...] = dmix
        dcat = _dot_nt(dmix, wout_s[...])
        dcat_ref[...] = dcat
        ones = jnp.ones((8, HEAD_DIM), F32)
        half = N_HEADS * HEAD_DIM
        for hd in range(N_HEADS):
            prod = dcat[:, half + hd * HEAD_DIM:half + (hd + 1) * HEAD_DIM] * omla_ref[:, hd * HEAD_DIM:(hd + 1) * HEAD_DIM]
            delta_ref[hd] = lax.dot_general(ones, prod, (((1,), (1,)), ((), ())), preferred_element_type=F32,
                                            precision=lax.Precision.HIGHEST)[0:1]

    row = lambda i: (i, 0)
    fixed = lambda i: (0, 0)
    any_spec = pl.BlockSpec(memory_space=pl.ANY)
    return _pcall(
        body, name="mlp_and_back", grid=(t_len // tm,),
        out_shape=[jax.ShapeDtypeStruct((t_len, D_FF), BF16), jax.ShapeDtypeStruct((t_len, D_FF), BF16),
                   jax.ShapeDtypeStruct((t_len, D_MODEL), BF16), jax.ShapeDtypeStruct((t_len, D_MODEL), BF16),
                   jax.ShapeDtypeStruct((t_len, D_MODEL), BF16), jax.ShapeDtypeStruct((t_len, D_MODEL), F32),
                   jax.ShapeDtypeStruct((t_len, D_MODEL), F32), jax.ShapeDtypeStruct((16, D_MODEL), F32),
                   jax.ShapeDtypeStruct((N_HEADS, 1, t_len), F32)],
        in_specs=[pl.BlockSpec((tm, D_MODEL), row), pl.BlockSpec((tm, 1), row), pl.BlockSpec((tm, D_MODEL), row),
                  pl.BlockSpec((tm, D_MODEL), row), pl.BlockSpec((tm, N_HEADS * HEAD_DIM), row),
                  pl.BlockSpec((8, D_MODEL), fixed), any_spec, any_spec, any_spec, any_spec],
        out_specs=[pl.BlockSpec((tm, D_FF), row), pl.BlockSpec((tm, D_FF), row), pl.BlockSpec((tm, D_MODEL), row),
                   pl.BlockSpec((tm, D_MODEL), row), pl.BlockSpec((tm, D_MODEL), row), pl.BlockSpec((tm, D_MODEL), row),
                   pl.BlockSpec((tm, D_MODEL), row), pl.BlockSpec((16, D_MODEL), fixed),
                   pl.BlockSpec((N_HEADS, 1, tm), lambda i: (0, 0, i))],
        scratch_shapes=[pltpu.VMEM((n_ff, D_MODEL, ff), BF16), pltpu.VMEM(w2.shape, BF16), pltpu.VMEM(w_out.shape, BF16),
                        pltpu.VMEM((n_ff, tm, ff), F32), pltpu.SemaphoreType.DMA((4,))],
        compiler_params=_params(56, ("arbitrary",)),
        operands=(xhat1, rstd1, mix, target, o_mla, vecs, w1_top, w1_bottom, w2, w_out))


def _in_project_backward(dq, dk, dv, cq, ckv, pos, invf, q_norm_w, kv_norm_w, w_q, w_kv,
                         d_hq, d_hf, d_hi, d_hg, w_in, dr1, x, sc_a, exchange=None):
    t_len = x.shape[0]
    tm = min(512, t_len)
    per_q = dq.shape[3] // tm
    hgw = N_HEADS * HEAD_DIM

    def body(dq_ref, dk_ref, dv_ref, cq_ref, ckv_ref, pos_ref, invf_ref, qn_ref, kvn_ref, wq_ref, wkv_ref,
             dhq_ref, dhf_ref, dhi_ref, dhg_ref, win_ref, dr1_ref, x_ref, sc_ref,
             dz_ref, dqf_ref, dkvu_ref, cqn_ref, ckvn_ref, gx_ref, sums_ref):
        @pl.when(pl.program_id(0) == 0)
        def _():
            sums_ref[...] = jnp.zeros_like(sums_ref)

        cos_t, sin_t = _rope_tables(pos_ref[...], invf_ref[...])
        cq = cq_ref[...]
        ckv = ckv_ref[...]
        rq = lax.rsqrt(_rowmean(cq * cq) + RMS_EPS)
        rkv = lax.rsqrt(_rowmean(ckv * ckv) + RMS_EPS)
        cqn_ref[...] = (cq * rq * qn_ref[...]).astype(BF16)
        ckvn_ref[...] = (ckv * rkv * kvn_ref[...]).astype(BF16)
        d_cqn = jnp.zeros((tm, Q_RANK), F32)
        d_ckvn = jnp.zeros((tm, KV_RANK), F32)
        d_kpe = jnp.zeros((tm, 128), F32)
        for h in range(N_HEADS):
            dqh = jnp.transpose(dq_ref[h])
            dqf_ref[h, :, 0:HEAD_DIM] = dqh[:, :HEAD_DIM].astype(BF16)
            dqf_ref[h, :, HEAD_DIM:QK_DIM] = _unrope(dqh[:, HEAD_DIM:], cos_t, sin_t).astype(BF16)
            d_cqn = d_cqn + _dot_nt(dqf_ref[h], wq_ref[h])
            dkh = dk_ref[h]
            d_kpe = d_kpe + dkh[:, HEAD_DIM:]
            dkvu_ref[h, :, 0:HEAD_DIM] = dkh[:, :HEAD_DIM].astype(BF16)
            dkvu_ref[h, :, HEAD_DIM:2 * HEAD_DIM] = dv_ref[h].astype(BF16)
            d_ckvn = d_ckvn + _dot_nt(dkvu_ref[h], wkv_ref[h])
        dyq = d_cqn * qn_ref[...]
        dykv = d_ckvn * kvn_ref[...]
        sums_ref[2:3, 0:Q_RANK] += _colsum(d_cqn * cq * rq)
        sums_ref[3:4, 0:KV_RANK] += _colsum(d_ckvn * ckv * rkv)
        dz_ref[:, 0:hgw] = dhq_ref[...]
        dz_ref[:, hgw:2 * hgw] = dhf_ref[...]
        dz_ref[:, 2 * hgw:3 * hgw] = dhi_ref[...]
        dz_ref[:, 3 * hgw:4 * hgw] = dhg_ref[...]
        dz_ref[:, HG_COLS:HG_COLS + Q_RANK] = (rq * dyq - cq * (rq * rq * rq) * _rowmean(dyq * cq)).astype(BF16)
        dz_ref[:, HG_COLS + Q_RANK:HG_COLS + Q_RANK + KV_RANK] = (
            rkv * dykv - ckv * (rkv * rkv * rkv) * _rowmean(dykv * ckv)).astype(BF16)
        dz_ref[:, HG_COLS + Q_RANK + KV_RANK:] = _unrope(d_kpe, cos_t, sin_t).astype(BF16)
        du = _dot(dz_ref[...], win_ref[...])
        xv = x_ref[...]
        gx_ref[...] = DN_ALPHA * dr1_ref[...] + (1.0 + sc_ref[...]) * du
        sums_ref[0:1, :] += _colsum(du * xv)
        sums_ref[1:2, :] += _colsum(du)

    row = lambda i: (i, 0)
    fixed2 = lambda i: (0, 0)
    fixed3 = lambda i: (0, 0, 0)
    heads = lambda i: (0, i, 0)
    n_tiles = t_len // tm
    return _pallas(
        body, name="in_project_backward", grid=(n_tiles,),
        operands=(dq, dk, dv, cq, ckv, pos, invf, q_norm_w, kv_norm_w, w_q, w_kv, d_hq, d_hf, d_hi, d_hg, w_in, dr1, x,
                  sc_a),
        out_shape=[jax.ShapeDtypeStruct((t_len, IN_COLS_PAD), BF16), jax.ShapeDtypeStruct((N_HEADS, t_len, QK_DIM), BF16),
                   jax.ShapeDtypeStruct((N_HEADS, t_len, 2 * HEAD_DIM), BF16), jax.ShapeDtypeStruct((t_len, Q_RANK), BF16),
                   jax.ShapeDtypeStruct((t_len, KV_RANK), BF16), jax.ShapeDtypeStruct((t_len, D_MODEL), F32),
                   jax.ShapeDtypeStruct((8, D_MODEL), F32)],
        in_specs=[pl.BlockSpec((N_HEADS, None, QK_DIM, tm), lambda i: (0, i // per_q, 0, i % per_q)),
                  pl.BlockSpec((N_HEADS, tm, QK_DIM), heads),
                  pl.BlockSpec((N_HEADS, tm, HEAD_DIM), heads), pl.BlockSpec((tm, Q_RANK), row),
                  pl.BlockSpec((tm, KV_RANK), row), pl.BlockSpec((tm, 1), row), pl.BlockSpec((1, 128), fixed2),
                  pl.BlockSpec((1, Q_RANK), fixed2), pl.BlockSpec((1, KV_RANK), fixed2),
                  pl.BlockSpec((N_HEADS, Q_RANK, QK_DIM), fixed3), pl.BlockSpec((N_HEADS, KV_RANK, 2 * HEAD_DIM), fixed3),
                  pl.BlockSpec((tm, hgw), row), pl.BlockSpec((tm, hgw), row), pl.BlockSpec((tm, hgw), row),
                  pl.BlockSpec((tm, hgw), row), pl.BlockSpec((IN_COLS_PAD, D_MODEL), fixed2),
                  pl.BlockSpec((tm, D_MODEL), row), pl.BlockSpec((tm, D_MODEL), row), pl.BlockSpec((1, D_MODEL), fixed2)],
        out_specs=[pl.BlockSpec((tm, IN_COLS_PAD), row), pl.BlockSpec((N_HEADS, tm, QK_DIM), heads),
                   pl.BlockSpec((N_HEADS, tm, 2 * HEAD_DIM), heads), pl.BlockSpec((tm, Q_RANK), row),
                   pl.BlockSpec((tm, KV_RANK), row), pl.BlockSpec((tm, D_MODEL), row), pl.BlockSpec((8, D_MODEL), fixed2)],
        params=_params(48, ("arbitrary",)), exchange=exchange,
        first=lambda: pl.program_id(0) == 0, last=lambda: pl.program_id(0) == n_tiles - 1)


def _weight_grad(a, b, name, n_blocks, bn, a_blocked=False, b_blocked=True, exchange=None, token_tile=512):
    t_len = a.shape[0]
    m = a.shape[1] // n_blocks if a_blocked else a.shape[1]
    bt = min(token_tile, t_len)

    def body(a_ref, b_ref, o_ref):
        @pl.when(pl.program_id(1) == 0)
        def _():
            o_ref[...] = jnp.zeros_like(o_ref)

        o_ref[...] += _dot_tn(a_ref[...].astype(BF16), b_ref[...].astype(BF16))

    a_spec = pl.BlockSpec((bt, m), (lambda n, t: (t, n)) if a_blocked else (lambda n, t: (t, 0)))
    if b.ndim == 3:
        b_spec = pl.BlockSpec((None, bt, bn), lambda n, t: (n, t, 0))
    else:
        b_spec = pl.BlockSpec((bt, bn), (lambda n, t: (t, n)) if b_blocked else (lambda n, t: (t, 0)))
    nt = t_len // bt
    (out,), landed = _pallas(
        body, name=name, grid=(n_blocks, nt), operands=(a, b),
        out_shape=[jax.ShapeDtypeStruct((n_blocks, m, bn), F32)],
        in_specs=[a_spec, b_spec],
        out_specs=[pl.BlockSpec((None, m, bn), lambda n, t: (n, 0, 0))],
        params=_params(56, ("arbitrary", "arbitrary")), exchange=exchange,
        first=lambda: (pl.program_id(0) == 0) & (pl.program_id(1) == 0),
        last=lambda: (pl.program_id(0) == n_blocks - 1) & (pl.program_id(1) == nt - 1))
    return (out, landed) if exchange else out


def _reduce_small(gathered, lb_raw):
    def body(g_ref, lb_ref, tot_ref, dlb_ref):
        tot = g_ref[0]
        for d in range(1, N_DEV):
            tot = tot + g_ref[d]
        tot_ref[...] = tot
        a = lb_ref[...]
        m = jnp.max(a, axis=0, keepdims=True)
        e = jnp.exp(a - m)
        lb = e[0:1] / jnp.sum(e, axis=0, keepdims=True)
        d0 = tot[10:11, 0:512] * lb * (1.0 - lb)
        dlb_ref[0:1, :] = d0
        dlb_ref[1:2, :] = -d0

    return pl.pallas_call(
        body, name="reduce_small",
        out_shape=[jax.ShapeDtypeStruct((SMALL_ROWS, D_MODEL), F32), jax.ShapeDtypeStruct((2, 512), F32)],
    )(gathered, lb_raw)


def _adamw_update(w, gv, m, v):
    nm = ADAM_B1 * m + (1.0 - ADAM_B1) * gv
    nv = ADAM_B2 * v + (1.0 - ADAM_B2) * jnp.square(gv)
    m_hat = nm / (1.0 - ADAM_B1 ** ADAM_STEP)
    v_hat = nv / (1.0 - ADAM_B2 ** ADAM_STEP)
    return -ADAM_LR * (m_hat / (jnp.sqrt(v_hat) + ADAM_EPS) + ADAM_WD * w), nm, nv


def _adamw_halves(core, w, mine, theirs, m, v, name):
    rows, cols = w.shape
    h = rows // 2
    tr = _row_tile(h)
    per_half = h // tr

    def body(core_ref, w_ref, mine_ref, theirs_ref, m_ref, v_ref, g_ref, d_ref, nm_ref, nv_ref):
        is_mine = pl.program_id(0) // per_half == core_ref[0]
        gv = jnp.where(is_mine, mine_ref[...], theirs_ref[...])
        g_ref[...] = gv
        d_ref[...], nm_ref[...], nv_ref[...] = _adamw_update(w_ref[...], gv, m_ref[...], v_ref[...])

    full = pl.BlockSpec((tr, cols), lambda i, core_ref: (i, 0))
    part = pl.BlockSpec((tr, cols), lambda i, core_ref: (i % per_half, 0))
    return _pcall(
        body, name=name, out_shape=[jax.ShapeDtypeStruct(w.shape, F32)] * 4,
        grid_spec=pltpu.PrefetchScalarGridSpec(
            num_scalar_prefetch=1, grid=(rows // tr,), in_specs=[full, part, part, full, full], out_specs=[full] * 4),
        compiler_params=_params(40, ("arbitrary",)),
        operands=(core, w, mine, theirs, m, v))


def _adamw(w, g, m, v, name):
    rows, cols = w.shape
    tr = _row_tile(rows) if rows >= 8 else rows

    def body(w_ref, g_ref, m_ref, v_ref, d_ref, nm_ref, nv_ref):
        d_ref[...], nm_ref[...], nv_ref[...] = _adamw_update(w_ref[...], g_ref[...], m_ref[...], v_ref[...])

    spec = pl.BlockSpec((tr, cols), lambda i: (i, 0))
    return _pcall(
        body, name=name, grid=(rows // tr,),
        out_shape=[jax.ShapeDtypeStruct(w.shape, F32)] * 3,
        in_specs=[spec] * 4, out_specs=[spec] * 3,
        compiler_params=_params(40, ("arbitrary",)),
        operands=(w, g, m, v))


def kernel(x, c, positions, w_ada, b_ada, w_in, hg_lower_bounds, hg_norm_w, mla_q_norm_w, w_q_up, mla_kv_norm_w, w_kv_up, w_out, ln1_g, ln1_b, w_mlp_in, w_mlp_out, ln2_g, ln2_b, loss_target, m_w_ada, m_b_ada, m_w_in, m_hg_lower_bounds, m_hg_norm_w, m_mla_q_norm_w, m_w_q_up, m_mla_kv_norm_w, m_w_kv_up, m_w_out, m_ln1_g, m_ln1_b, m_w_mlp_in, m_w_mlp_out, m_ln2_g, m_ln2_b, v_w_ada, v_b_ada, v_w_in, v_hg_lower_bounds, v_hg_norm_w, v_mla_q_norm_w, v_w_q_up, v_mla_kv_norm_w, v_w_kv_up, v_w_out, v_ln1_g, v_ln1_b, v_w_mlp_in, v_w_mlp_out, v_ln2_g, v_ln2_b):
    ix, iy, ic = _mesh_pos()
    chip = 2 * ix + iy
    me = 4 * ix + 2 * iy + ic
    core_arr = jnp.reshape(ic, (1,)).astype(jnp.int32)
    chip_arr = jnp.reshape(chip, (1,)).astype(jnp.int32)

    xs = x[0]
    target = loss_target[0]
    t_len = xs.shape[0]
    pos = positions.astype(F32).reshape(t_len, 1)
    inv = 1.0 / (ROPE_THETA ** (jnp.arange(0, ROPE_DIM, 2, dtype=F32) / ROPE_DIM))
    invf = jnp.concatenate([inv, inv, jnp.zeros((128 - ROPE_DIM,), F32)]).reshape(1, 128)

    def slot(w):
        rows, cols = w.shape
        own = w.astype(BF16).reshape(1, 2, rows // 2, cols)
        return lax.dynamic_update_slice(jnp.zeros((N_CHIPS, 2, rows // 2, cols), BF16), own, (chip, 0, 0, 0))

    def slot8(a):
        return lax.dynamic_update_slice(jnp.zeros((N_DEV,) + a.shape, a.dtype), a[None], (me, 0, 0))

    def whole(s):
        return s.reshape(N_CHIPS, 2 * s.shape[2], s.shape[3])

    def halved(g):
        return g.reshape(N_CHIPS, 2, g.shape[1] // 2, g.shape[2])

    ada_cols = w_ada.shape[2]
    c_all, *early = _run_exchange(
        _merge(_gather_all(slot8(jnp.broadcast_to(c, (8, D_MODEL)))),
               _gather_over_ici([slot(jnp.transpose(w_in[0])), slot(w_q_up[0]), slot(w_kv_up[0])])),
        "gather_c_and_mixer_weights_ici")
    b_shard = lax.dynamic_slice(b_ada, (0, chip * ada_cols), (1, ada_cols))
    mod_cols, cond16 = _ada_project(c_all[:, 0, :], w_ada[0], b_shard)
    mod_all, *early = _run_exchange(_merge(_gather_all(slot8(mod_cols)), _gather_over_d2d(early)),
                                    "gather_mod_and_mixer_weights_d2d")
    mod_mine = lax.dynamic_slice(mod_all, (0, me, 0), (N_DEV, 1, ada_cols))[::2, 0, :].reshape(6, D_MODEL)
    sh_a, sc_a, g_a, sh_m, sc_m, g_m = (mod_mine[i:i + 1] for i in range(6))
    g_in, g_q, g_kv = (whole(s) for s in early)
    w_in_full = jnp.pad(g_in.reshape(IN_COLS, D_MODEL), ((0, IN_COLS_PAD - IN_COLS), (0, 0)))
    w_q_full = jnp.pad(g_q, ((0, 0), (0, 0), (0, QK_DIM - g_q.shape[2])))

    w1_rows = D_MODEL // 2
    (u_a, zhg, cq, ckv, q, k, k_t, v, v_t), (s_top,) = _in_project(
        xs, pos, sc_a, sh_a, w_in_full, mla_q_norm_w, mla_kv_norm_w, w_q_full, g_kv, invf,
        _gather_over_ici([slot(w_mlp_in[0, :w1_rows])]))
    (o_pre, o_hg, states), (s_out, s_bottom, s_top) = _hgrn_forward(
        zhg, hg_lower_bounds, hg_norm_w,
        _merge(_gather_over_ici([slot(w_out[0]), slot(w_mlp_in[0, w1_rows:])]), _gather_over_d2d([s_top])))
    (o_mla, lse), (s_w2, s_out, s_bottom) = _attention_forward(
        q, k, v_t, _merge(_gather_over_ici([slot(w_mlp_out[0])]), _gather_over_d2d([s_out, s_bottom])))
    w_out_full = whole(s_out).reshape(D_MODEL, D_MODEL)
    (cat, mix, xhat1, rstd1), (s_w2,) = _out_project(o_hg, o_mla, xs, g_a, w_out_full, _gather_over_d2d([s_w2]))
    g_w1_top, g_w1_bottom, g_w2 = whole(s_top), whole(s_bottom), whole(s_w2)
    vecs = jnp.concatenate([ln1_g, ln1_b, sc_m, sh_m, g_m, g_a, ln2_g, ln2_b], axis=0)
    act, dhp, um, dh, dmix, d_cat, dr1, mlp_sums, delta = _mlp_and_back(
        xhat1, rstd1, mix, target, o_mla, vecs, g_w1_top, g_w1_bottom, g_w2, w_out_full)

    gw_1 = halved(_weight_grad(um, dhp, "grad_w_mlp_in", N_CHIPS, D_FF // N_CHIPS, token_tile=4096))
    gw_2, (landed_1,) = _weight_grad(act, dh, "grad_w_mlp_out", N_CHIPS, D_MODEL, a_blocked=True, b_blocked=False,
                                     token_tile=4096, exchange=_pair_exchange([gw_1]))
    gw_out = _weight_grad(cat, dmix, "grad_w_out", 1, D_MODEL, token_tile=2048)
    later = [halved(gw_2), halved(gw_out.reshape(N_CHIPS, D_MODEL // N_CHIPS, D_MODEL))]
    own_1, travels_1 = _add_pair(core_arr, chip_arr, gw_1, landed_1)
    (dq, dk, dv), (landed_1, *landed) = _attention_backward(
        q, k, k_t, v, d_cat, lse, delta, _merge(_chip_exchange([travels_1]), _pair_exchange(later)))
    mine_1 = _add_chips(own_1, landed_1)
    chip_sums = [_add_pair(core_arr, chip_arr, g, l) for g, l in zip(later, landed)]
    (d_hq, d_hf, d_hi, d_hg, hg_sums), (theirs_1, *landed) = _hgrn_backward(
        zhg, hg_lower_bounds, hg_norm_w, o_pre, d_cat, states,
        _merge(_pair_send([mine_1]), _chip_exchange([b for _, b in chip_sums])))
    later_mine = [_add_chips(own, l) for (own, _), l in zip(chip_sums, landed)]
    mlp_mine = [mine_1] + later_mine
    (dz, dqf, dkvu, cqn, ckvn, grad_x, in_sums), _ = _in_project_backward(
        dq, dk, dv, cq, ckv, pos, invf, mla_q_norm_w, mla_kv_norm_w, w_q_full, g_kv,
        d_hq, d_hf, d_hi, d_hg, w_in_full, dr1, xs, sc_a)

    gw_in, later_theirs = _weight_grad(dz, u_a, "grad_w_in", 3, D_MODEL, a_blocked=True, b_blocked=False,
                                       exchange=_pair_send(later_mine), token_tile=4096)
    mlp_theirs = [theirs_1] + list(later_theirs)
    gw_in = gw_in.reshape(IN_COLS_PAD, D_MODEL)
    gw_q = _weight_grad(cqn, dqf, "grad_w_q_up", N_HEADS, QK_DIM, token_tile=2048)[:, :, :HEAD_DIM + ROPE_DIM]
    gw_kv = _weight_grad(ckvn, dkvu, "grad_w_kv_up", N_HEADS, 2 * HEAD_DIM, token_tile=2048)
    flat = lambda g: g.reshape(g.shape[0] * g.shape[1], g.shape[2])
    mixer_mine, mixer_theirs = _reduce_in_vmem(
        [gw_in, flat(gw_q), flat(gw_kv)], [IN_COLS // N_CHIPS // 2, Q_RANK // 2, KV_RANK // 2], "reduce_mixer_grads")
    reduced = ("w_in", "w_q_up", "w_kv_up", "w_mlp_in", "w_mlp_out", "w_out")
    halves_mine = dict(zip(reduced, list(mixer_mine) + mlp_mine))
    halves_theirs = dict(zip(reduced, list(mixer_theirs) + list(mlp_theirs)))

    zeros = lambda n: jnp.zeros((1, n), F32)
    small = jnp.concatenate([
        in_sums[1:2], in_sums[0:1], mlp_sums[S_DGA:S_DGA + 1],
        mlp_sums[S_DSHM:S_DSHM + 1], mlp_sums[S_DSCM:S_DSCM + 1], mlp_sums[S_DGM:S_DGM + 1],
        mlp_sums[S_DLN1G:S_DLN1G + 1], mlp_sums[S_DLN1B:S_DLN1B + 1],
        mlp_sums[S_DLN2G:S_DLN2G + 1], mlp_sums[S_DLN2B:S_DLN2B + 1],
        jnp.concatenate([hg_sums[0:1], hg_sums[1:2]], axis=1),
        jnp.concatenate([in_sums[2:3, :Q_RANK], in_sums[3:4, :KV_RANK], zeros(D_MODEL - Q_RANK - KV_RANK)], axis=1),
        mlp_sums[S_LOSS:S_LOSS + 1],
        jnp.zeros((SMALL_ROWS - 13, D_MODEL), F32)], axis=0)
    small_all = _allgather8(small, "gather_small")
    tot, g_lb = _reduce_small(small_all, hg_lower_bounds)
    loss = tot[12, 0]
    g_b_ada = tot[0:6].reshape(1, 6 * D_MODEL)
    g_ln1_g, g_ln1_b, g_ln2_g, g_ln2_b = tot[6:7], tot[7:8], tot[8:9], tot[9:10]
    g_hg_norm = tot[10:11, 512:1024]
    g_q_norm = tot[11:12, 0:Q_RANK]
    g_kv_norm = tot[11:12, Q_RANK:Q_RANK + KV_RANK]

    d_mod_all = small_all[:, 0:6, :].reshape(N_DEV, 6 * D_MODEL)
    d_mod_cols = lax.dynamic_slice(d_mod_all, (0, chip * ada_cols), (N_DEV, ada_cols))
    d_mod_cols = jnp.concatenate([d_mod_cols, jnp.zeros_like(d_mod_cols)], axis=0)
    g_w_ada = _weight_grad(cond16, d_mod_cols, "grad_w_ada", 1, ada_cols)[0]

    names = ["w_ada", "b_ada", "w_in", "hg_lower_bounds", "hg_norm_w", "mla_q_norm_w", "w_q_up", "mla_kv_norm_w",
             "w_kv_up", "w_out", "ln1_g", "ln1_b", "w_mlp_in", "w_mlp_out", "ln2_g", "ln2_b"]
    weights = [w_ada, b_ada, w_in, hg_lower_bounds, hg_norm_w, mla_q_norm_w, w_q_up, mla_kv_norm_w,
               w_kv_up, w_out, ln1_g, ln1_b, w_mlp_in, w_mlp_out, ln2_g, ln2_b]
    moms = [m_w_ada, m_b_ada, m_w_in, m_hg_lower_bounds, m_hg_norm_w, m_mla_q_norm_w, m_w_q_up, m_mla_kv_norm_w,
            m_w_kv_up, m_w_out, m_ln1_g, m_ln1_b, m_w_mlp_in, m_w_mlp_out, m_ln2_g, m_ln2_b]
    vels = [v_w_ada, v_b_ada, v_w_in, v_hg_lower_bounds, v_hg_norm_w, v_mla_q_norm_w, v_w_q_up, v_mla_kv_norm_w,
            v_w_kv_up, v_w_out, v_ln1_g, v_ln1_b, v_w_mlp_in, v_w_mlp_out, v_ln2_g, v_ln2_b]
    grads2d = [g_w_ada, g_b_ada, None, g_lb, g_hg_norm, g_q_norm, None, g_kv_norm,
               None, None, g_ln1_g, g_ln1_b, None, None, g_ln2_g, g_ln2_b]
    out_g, out_d, out_m, out_v = [], [], [], []
    for name, w, g, m, vv in zip(names, weights, grads2d, moms, vels):
        if name == "w_in":
            to2d, back = (lambda a: jnp.transpose(a[0])), (lambda a: jnp.transpose(a)[None])
        else:
            shape2 = w.shape[1:] if g is None else g.shape
            to2d, back = (lambda a, s=shape2: a.reshape(s)), (lambda a, s=w.shape: a.reshape(s))
        if g is None:
            g, d, nm, nv = _adamw_halves(core_arr, to2d(w), halves_mine[name], halves_theirs[name], to2d(m), to2d(vv),
                                         "adamw_" + name)
        else:
            d, nm, nv = _adamw(to2d(w), g, to2d(m), to2d(vv), "adamw_" + name)
        out_g.append(back(g))
        out_d.append(back(d))
        out_m.append(back(nm))
        out_v.append(back(nv))
    return (loss, grad_x[None], *out_g, *out_d, *out_m, *out_v)
```

```python
import functools

import jax
import jax.numpy as jnp
from jax import lax
from jax.experimental import pallas as pl
from jax.experimental.pallas import tpu as pltpu

F32 = jnp.float32
BF16 = jnp.bfloat16
MESH_IDS = pl.DeviceIdType.MESH

D_MODEL = 1024
N_HEADS = 4
HEAD_DIM = 128
ROPE_DIM = 64
HG_CHUNK = 64
HG_COLS = 2048
Q_RANK = 256
KV_RANK = 256
IN_COLS = 2624
IN_COLS_PAD = 2688
QK_DIM = 256
D_FF = 4096
N_CHIPS = 4
N_DEV = 8
ROPE_THETA = 10000.0
RMS_EPS = 1e-6
LN_EPS = 1e-5
DN_ALPHA = 2.0 ** 0.25
ATT_SCALE = (HEAD_DIM + ROPE_DIM) ** -0.5
NEG_BIG = -1e30
ADAM_LR = 0.001
ADAM_B1 = 0.9
ADAM_B2 = 0.999
ADAM_EPS = 1e-08
ADAM_WD = 0.01
ADAM_STEP = 10
SMALL_ROWS = 16
MIB = 1024 * 1024


def _dot(a, b):
    return jnp.dot(a, b, preferred_element_type=F32)


def _dot_nt(a, b):
    return lax.dot_general(a, b, (((1,), (1,)), ((), ())), preferred_element_type=F32)


def _dot_tn(a, b):
    return lax.dot_general(a, b, (((0,), (0,)), ((), ())), preferred_element_type=F32)


def _params(vmem_mib, semantics=None):
    return pltpu.CompilerParams(vmem_limit_bytes=vmem_mib * MIB, dimension_semantics=semantics)


def _sigmoid(v):
    return 1.0 / (1.0 + jnp.exp(-v))


def _colsum(v):
    return jnp.sum(v, axis=0, keepdims=True)


def _rowmean(v):
    return jnp.mean(v, axis=-1, keepdims=True)


def _rope_tables(pos, invf):
    ang = pos * invf
    lane = lax.broadcasted_iota(jnp.int32, ang.shape, 1)
    cos_t = jnp.where(lane < ROPE_DIM, jnp.cos(ang), 0.0)
    sin = jnp.sin(ang)
    sin_t = jnp.where(lane < ROPE_DIM // 2, -sin, jnp.where(lane < ROPE_DIM, sin, 0.0))
    return cos_t, sin_t


def _swap_halves(t):
    lane = lax.broadcasted_iota(jnp.int32, t.shape, 1)
    return jnp.where(lane < ROPE_DIM // 2, pltpu.roll(t, 128 - ROPE_DIM // 2, 1), pltpu.roll(t, ROPE_DIM // 2, 1))


def _rope(t, cos_t, sin_t):
    return t * cos_t + _swap_halves(t) * sin_t


def _unrope(g, cos_t, sin_t):
    return g * cos_t - _swap_halves(g) * sin_t


def _mesh_pos():
    return lax.axis_index("x"), lax.axis_index("y"), lax.axis_index("c")


def _other_chips(x, y):
    out = []
    for dx, dy in ((1, 0), (0, 1), (1, 1)):
        px = 1 - x if dx else x
        py = 1 - y if dy else y
        out.append(((px, py), 2 * px + py))
    return out


class _Exchange:
    def __init__(self, inputs, out_shapes, aliases, sems, start, finish):
        self.inputs, self.out_shapes, self.aliases, self.sems = list(inputs), list(out_shapes), dict(aliases), list(sems)
        self.start, self.finish = start, finish


def _from_copies(inputs, out_shapes, aliases, sems, copies):
    def start(ins, outs, sem_refs):
        for send, _ in copies(ins, outs, sem_refs):
            send.start()

    def finish(ins, outs, sem_refs):
        for send, recv in copies(ins, outs, sem_refs):
            recv.wait_recv()
            send.wait_send()

    return _Exchange(inputs, out_shapes, aliases, sems, start, finish)


HBM_MIN_BYTES = 256 * 1024


def _in_hbm(a):
    if a.size * a.dtype.itemsize < HBM_MIN_BYTES:
        return a
    return pltpu.with_memory_space_constraint(a, pltpu.HBM)


def _out_hbm(s):
    if s.size * s.dtype.itemsize < HBM_MIN_BYTES:
        return s
    return pltpu.HBM(s.shape, s.dtype)


def _pcall(body, *, operands, out_shape, **kwargs):
    single = not isinstance(out_shape, (list, tuple))
    shapes = [_out_hbm(s) for s in ([out_shape] if single else out_shape)]
    return pl.pallas_call(body, out_shape=shapes[0] if single else shapes, **kwargs)(*[_in_hbm(a) for a in operands])


def _run_exchange(exchange, name):
    n_in, n_out = len(exchange.inputs), len(exchange.out_shapes)

    def body(*refs):
        ins, outs, sem_refs = refs[:n_in], refs[n_in:n_in + n_out], refs[n_in + n_out:]
        exchange.start(ins, outs, sem_refs)
        exchange.finish(ins, outs, sem_refs)

    any_spec = pl.BlockSpec(memory_space=pl.ANY)
    return pl.pallas_call(
        body, name=name, out_shape=[_out_hbm(s) for s in exchange.out_shapes],
        in_specs=[any_spec] * n_in, out_specs=[any_spec] * n_out,
        scratch_shapes=exchange.sems, input_output_aliases=exchange.aliases,
    )(*[_in_hbm(a) for a in exchange.inputs])


def _pallas(body, *, name, operands, in_specs, out_shape, out_specs, params, scratch_shapes=(), grid=(), prefetch=(),
            exchange=None, first=None, last=None):
    n_pre, n_in, n_out, n_scr = len(prefetch), len(in_specs), len(out_specs), len(scratch_shapes)
    ex_in = exchange.inputs if exchange else []
    ex_out = exchange.out_shapes if exchange else []
    ex_sems = exchange.sems if exchange else []

    def full_body(*refs):
        pre, rest = refs[:n_pre], refs[n_pre:]
        ins, rest = rest[:n_in], rest[n_in:]
        xin, rest = rest[:len(ex_in)], rest[len(ex_in):]
        outs, rest = rest[:n_out], rest[n_out:]
        xout, rest = rest[:len(ex_out)], rest[len(ex_out):]
        scr, sem_refs = rest[:n_scr], rest[n_scr:]
        if exchange:
            @pl.when(first(*pre))
            def _():
                exchange.start(xin, xout, sem_refs)

        body(*pre, *ins, *outs, *scr)
        if exchange:
            @pl.when(last(*pre))
            def _():
                exchange.finish(xin, xout, sem_refs)

    any_spec = pl.BlockSpec(memory_space=pl.ANY)
    aliases = {n_pre + n_in + i: n_out + o for i, o in exchange.aliases.items()} if exchange else {}
    operands = [_in_hbm(a) for a in operands]
    results = pl.pallas_call(
        full_body, name=name, out_shape=[_out_hbm(s) for s in list(out_shape) + ex_out],
        grid_spec=pltpu.PrefetchScalarGridSpec(
            num_scalar_prefetch=n_pre, grid=grid, in_specs=list(in_specs) + [any_spec] * len(ex_in),
            out_specs=list(out_specs) + [any_spec] * len(ex_out), scratch_shapes=list(scratch_shapes) + ex_sems),
        input_output_aliases=aliases, compiler_params=params,
    )(*prefetch, *operands, *[_in_hbm(a) for a in ex_in])
    return results[:n_out], results[n_out:]


def _remote(src, dst, sems, idx, to):
    send_sems, recv_sems = sems
    return pltpu.make_async_remote_copy(src_ref=src, dst_ref=dst, send_sem=send_sems.at[idx], recv_sem=recv_sems.at[idx],
                                        device_id=to, device_id_type=MESH_IDS)


def _sem_pairs(*shape):
    return [pltpu.SemaphoreType.DMA(shape), pltpu.SemaphoreType.DMA(shape)]


def _same_shapes(arrays):
    return [jax.ShapeDtypeStruct(a.shape, a.dtype) for a in arrays]


def _gather_over_ici(slots):
    n = len(slots)

    def copies(ins, outs, sems):
        x, y, c = _mesh_pos()
        k = 2 * x + y
        out = []
        for j, (chip, kj) in enumerate(_other_chips(x, y)):
            for i in range(n):
                to = (*chip, c)
                out.append((_remote(ins[i].at[k, c], outs[i].at[k, c], sems, (j, i), to),
                            _remote(ins[i].at[k, c], outs[i].at[kj, c], sems, (j, i), to)))
        return out

    return _from_copies(slots, _same_shapes(slots), {i: i for i in range(n)}, _sem_pairs(3, n), copies)


def _gather_over_d2d(slots):
    n = len(slots)

    def copies(ins, outs, sems):
        x, y, c = _mesh_pos()
        sibling = (x, y, 1 - c)
        out = []
        for j, (_, kj) in enumerate(_other_chips(x, y)):
            for i in range(n):
                out.append((_remote(ins[i].at[kj, c], outs[i].at[kj, c], sems, (j, i), sibling),
                            _remote(ins[i].at[kj, c], outs[i].at[kj, 1 - c], sems, (j, i), sibling)))
        return out

    return _from_copies(slots, _same_shapes(slots), {i: i for i in range(n)}, _sem_pairs(3, n), copies)


def _gather_all(slots8):
    def copies(ins, outs, sems):
        x, y, c = _mesh_pos()
        me = 4 * x + 2 * y + c
        out = []
        for r in range(1, N_DEV):
            px = 1 - x if r & 4 else x
            py = 1 - y if r & 2 else y
            pc = 1 - c if r & 1 else c
            to = (px, py, pc)
            out.append((_remote(ins[0].at[me], outs[0].at[me], sems, r - 1, to),
                        _remote(ins[0].at[me], outs[0].at[4 * px + 2 * py + pc], sems, r - 1, to)))
        return out

    return _from_copies([slots8], _same_shapes([slots8]), {0: 0}, _sem_pairs(N_DEV - 1), copies)


def _merge(first, second):
    n_in, n_out, n_sem = len(first.inputs), len(first.out_shapes), len(first.sems)

    def start(ins, outs, sems):
        first.start(ins[:n_in], outs[:n_out], sems[:n_sem])
        second.start(ins[n_in:], outs[n_out:], sems[n_sem:])

    def finish(ins, outs, sems):
        first.finish(ins[:n_in], outs[:n_out], sems[:n_sem])
        second.finish(ins[n_in:], outs[n_out:], sems[n_sem:])

    aliases = dict(first.aliases)
    aliases.update({n_in + i: n_out + o for i, o in second.aliases.items()})
    return _Exchange(first.inputs + second.inputs, first.out_shapes + second.out_shapes, aliases,
                     first.sems + second.sems, start, finish)


def _pair_exchange(grads):
    n = len(grads)

    def copies(ins, outs, sems):
        x, y, c = _mesh_pos()
        cps = [_remote(ins[i].at[:, 1 - c], outs[i], sems, i, (x, y, 1 - c)) for i in range(n)]
        return [(cp, cp) for cp in cps]

    shapes = [jax.ShapeDtypeStruct((N_CHIPS,) + g.shape[2:], g.dtype) for g in grads]
    return _from_copies(grads, shapes, {}, _sem_pairs(n), copies)


def _chip_exchange(partials):
    n = len(partials)

    def copies(ins, outs, sems):
        x, y, c = _mesh_pos()
        cps = [_remote(ins[i].at[kj], outs[i].at[j], sems, (j, i), (*chip, c))
               for j, (chip, kj) in enumerate(_other_chips(x, y)) for i in range(n)]
        return [(cp, cp) for cp in cps]

    shapes = [jax.ShapeDtypeStruct((3,) + p.shape[1:], p.dtype) for p in partials]
    return _from_copies(partials, shapes, {}, _sem_pairs(3, n), copies)


def _pair_send(halves):
    n = len(halves)

    def copies(ins, outs, sems):
        x, y, c = _mesh_pos()
        cps = [_remote(ins[i], outs[i], sems, i, (x, y, 1 - c)) for i in range(n)]
        return [(cp, cp) for cp in cps]

    return _from_copies(halves, _same_shapes(halves), {}, _sem_pairs(n), copies)


def _reduce_in_vmem(grads, half_rows, name):
    n = len(grads)

    def body(*refs):
        g, mine, theirs = refs[:n], refs[n:2 * n], refs[2 * n:3 * n]
        landed_pair, partial, landed_chips = refs[3 * n:4 * n], refs[4 * n:5 * n], refs[5 * n:6 * n]
        sems = refs[6 * n:]
        x, y, c = _mesh_pos()
        k = 2 * x + y
        sibling = (x, y, 1 - c)

        def half(i, chip_idx, which):
            return pl.ds(pl.multiple_of((2 * chip_idx + which) * half_rows[i], 8), half_rows[i])

        def run(copies):
            for cp in copies:
                cp.start()
            for cp in copies:
                cp.wait_recv()
                cp.wait_send()

        run([_remote(g[i].at[half(i, kk, 1 - c)], landed_pair[i].at[kk], sems[0:2], (kk, i), sibling)
             for kk in range(N_CHIPS) for i in range(n)])
        for i in range(n):
            for kk in range(N_CHIPS):
                partial[i][kk] = (g[i][half(i, kk, c), :] + landed_pair[i][kk]).astype(BF16)
        run([_remote(partial[i].at[kj], landed_chips[i].at[j], sems[2:4], (j, i), (*chip, c))
             for j, (chip, kj) in enumerate(_other_chips(x, y)) for i in range(n)])
        for i in range(n):
            own = g[i][half(i, k, c), :] + landed_pair[i][k]
            mine[i][...] = ((own + landed_chips[i][0].astype(F32)) + landed_chips[i][1].astype(F32)) \
                + landed_chips[i][2].astype(F32)
        run([_remote(mine[i], theirs[i], sems[4:6], i, sibling) for i in range(n)])

    shapes = [(h, gr.shape[1]) for gr, h in zip(grads, half_rows)]
    halves = [jax.ShapeDtypeStruct(s, F32) for s in shapes]
    vmem = pl.BlockSpec(memory_space=pltpu.VMEM)
    scratch = ([pltpu.VMEM((N_CHIPS,) + s, F32) for s in shapes]
               + [pltpu.VMEM((N_CHIPS,) + s, BF16) for s in shapes]
               + [pltpu.VMEM((3,) + s, BF16) for s in shapes]
               + _sem_pairs(N_CHIPS, n) + _sem_pairs(3, n) + _sem_pairs(n))
    out = pl.pallas_call(
        body, name=name, out_shape=halves + halves, in_specs=[vmem] * n, out_specs=[vmem] * (2 * n),
        scratch_shapes=scratch, compiler_params=_params(48),
    )(*grads)
    return out[:n], out[n:]


def _row_tile(rows):
    for t in (256, 128, 64):
        if rows % t == 0:
            return t
    return rows


def _add_pair(core, chip, grad, landed):
    _, h, cols = landed.shape
    tr = _row_tile(h)

    def body(core_ref, chip_ref, g_ref, l_ref, own_ref, ob_ref):
        s = g_ref[...] + l_ref[...]
        ob_ref[...] = s.astype(BF16)

        @pl.when(pl.program_id(1) == chip_ref[0])
        def _():
            own_ref[...] = s

    return _pcall(
        body, name="grad_add_pair",
        out_shape=[jax.ShapeDtypeStruct((h, cols), F32), jax.ShapeDtypeStruct(landed.shape, BF16)],
        grid_spec=pltpu.PrefetchScalarGridSpec(
            num_scalar_prefetch=2, grid=(h // tr, N_CHIPS),
            in_specs=[pl.BlockSpec((None, None, tr, cols), lambda t, k, core_ref, chip_ref: (k, core_ref[0], t, 0)),
                      pl.BlockSpec((None, tr, cols), lambda t, k, core_ref, chip_ref: (k, t, 0))],
            out_specs=[pl.BlockSpec((tr, cols), lambda t, k, core_ref, chip_ref: (t, 0)),
                       pl.BlockSpec((None, tr, cols), lambda t, k, core_ref, chip_ref: (k, t, 0))]),
        compiler_params=_params(32, ("arbitrary", "arbitrary")),
        operands=(core, chip, grad, landed))


def _add_chips(own, landed):
    h, cols = own.shape
    tr = _row_tile(h)

    def body(p_ref, l_ref, o_ref):
        o_ref[...] = ((p_ref[...] + l_ref[0].astype(F32)) + l_ref[1].astype(F32)) + l_ref[2].astype(F32)

    return _pcall(
        body, name="grad_add_chips", grid=(h // tr,),
        out_shape=jax.ShapeDtypeStruct((h, cols), F32),
        in_specs=[pl.BlockSpec((tr, cols), lambda t: (t, 0)), pl.BlockSpec((3, tr, cols), lambda t: (0, t, 0))],
        out_specs=pl.BlockSpec((tr, cols), lambda t: (t, 0)),
        compiler_params=_params(32, ("arbitrary",)),
        operands=(own, landed))


def _ada_project(c_all, w_ada, b_shard):
    n = w_ada.shape[1]
    tn = 512

    def body(c_ref, w_ref, b_ref, mod_ref, cond_ref):
        cv = c_ref[...]
        cond = cv * _sigmoid(cv)
        mod_ref[...] = _dot(cond.astype(BF16), w_ref[...].astype(BF16)) + b_ref[...]
        cond_ref[0:N_DEV, :] = cond
        cond_ref[N_DEV:2 * N_DEV, :] = jnp.zeros_like(cond)

    return _pcall(
        body, name="ada_project", grid=(n // tn,),
        out_shape=[jax.ShapeDtypeStruct((N_DEV, n), F32), jax.ShapeDtypeStruct((2 * N_DEV, D_MODEL), F32)],
        in_specs=[pl.BlockSpec((N_DEV, D_MODEL), lambda j: (0, 0)), pl.BlockSpec((D_MODEL, tn), lambda j: (0, j)),
                  pl.BlockSpec((1, tn), lambda j: (0, j))],
        out_specs=[pl.BlockSpec((N_DEV, tn), lambda j: (0, j)), pl.BlockSpec((2 * N_DEV, D_MODEL), lambda j: (0, 0))],
        compiler_params=_params(32, ("arbitrary",)),
        operands=(c_all, w_ada, b_shard))


def _in_project(x, pos, sc_a, sh_a, w_in, q_norm_w, kv_norm_w, w_q, w_kv, invf, exchange=None):
    t_len = x.shape[0]
    tm = min(512, t_len)

    def body(x_ref, pos_ref, sc_ref, sh_ref, win_ref, qn_ref, kvn_ref, wq_ref, wkv_ref, invf_ref,
             u_ref, zhg_ref, cq_ref, ckv_ref, q_ref, k_ref, kt_ref, v_ref, vt_ref):
        u = (x_ref[...] * (1.0 + sc_ref[...]) + sh_ref[...]).astype(BF16)
        u_ref[...] = u
        z = _dot_nt(u, win_ref[...])
        zhg_ref[...] = z[:, :HG_COLS]
        cq = z[:, HG_COLS:HG_COLS + Q_RANK]
        ckv = z[:, HG_COLS + Q_RANK:HG_COLS + Q_RANK + KV_RANK]
        cq_ref[...] = cq
        ckv_ref[...] = ckv
        cos_t, sin_t = _rope_tables(pos_ref[...], invf_ref[...])
        k_pe = _rope(z[:, HG_COLS + Q_RANK + KV_RANK:], cos_t, sin_t)
        k_pe_t = jnp.transpose(k_pe).astype(BF16)
        cqn = (cq * lax.rsqrt(_rowmean(cq * cq) + RMS_EPS) * qn_ref[...]).astype(BF16)
        ckvn = (ckv * lax.rsqrt(_rowmean(ckv * ckv) + RMS_EPS) * kvn_ref[...]).astype(BF16)
        for h in range(N_HEADS):
            qh = _dot(cqn, wq_ref[h])
            q_ref[h, :, 0:HEAD_DIM] = qh[:, :HEAD_DIM].astype(BF16)
            q_ref[h, :, HEAD_DIM:QK_DIM] = _rope(qh[:, HEAD_DIM:], cos_t, sin_t).astype(BF16)
            kvh = _dot(ckvn, wkv_ref[h])
            k_ref[h, :, 0:HEAD_DIM] = kvh[:, :HEAD_DIM].astype(BF16)
            k_ref[h, :, HEAD_DIM:QK_DIM] = k_pe.astype(BF16)
            kt_ref[h, 0:HEAD_DIM, :] = jnp.transpose(kvh[:, :HEAD_DIM]).astype(BF16)
            kt_ref[h, HEAD_DIM:QK_DIM, :] = k_pe_t
            v_ref[h] = kvh[:, HEAD_DIM:].astype(BF16)
            vt_ref[h] = jnp.transpose(kvh[:, HEAD_DIM:]).astype(BF16)

    row = lambda i: (i, 0)
    fixed2 = lambda i: (0, 0)
    fixed3 = lambda i: (0, 0, 0)
    heads = lambda i: (0, i, 0)
    n_tiles = t_len // tm
    return _pallas(
        body, name="in_project", grid=(n_tiles,),
        operands=(x, pos, sc_a, sh_a, w_in, q_norm_w, kv_norm_w, w_q, w_kv, invf),
        out_shape=[jax.ShapeDtypeStruct((t_len, D_MODEL), BF16), jax.ShapeDtypeStruct((t_len, HG_COLS), F32),
                   jax.ShapeDtypeStruct((t_len, Q_RANK), F32), jax.ShapeDtypeStruct((t_len, KV_RANK), F32),
                   jax.ShapeDtypeStruct((N_HEADS, t_len, QK_DIM), BF16),
                   jax.ShapeDtypeStruct((N_HEADS, t_len, QK_DIM), BF16),
                   jax.ShapeDtypeStruct((N_HEADS, QK_DIM, t_len), BF16),
                   jax.ShapeDtypeStruct((N_HEADS, t_len, HEAD_DIM), BF16),
                   jax.ShapeDtypeStruct((N_HEADS, HEAD_DIM, t_len), BF16)],
        in_specs=[pl.BlockSpec((tm, D_MODEL), row), pl.BlockSpec((tm, 1), row),
                  pl.BlockSpec((1, D_MODEL), fixed2), pl.BlockSpec((1, D_MODEL), fixed2),
                  pl.BlockSpec((IN_COLS_PAD, D_MODEL), fixed2),
                  pl.BlockSpec((1, Q_RANK), fixed2), pl.BlockSpec((1, KV_RANK), fixed2),
                  pl.BlockSpec((N_HEADS, Q_RANK, QK_DIM), fixed3), pl.BlockSpec((N_HEADS, KV_RANK, 2 * HEAD_DIM), fixed3),
                  pl.BlockSpec((1, 128), fixed2)],
        out_specs=[pl.BlockSpec((tm, D_MODEL), row), pl.BlockSpec((tm, HG_COLS), row),
                   pl.BlockSpec((tm, Q_RANK), row), pl.BlockSpec((tm, KV_RANK), row),
                   pl.BlockSpec((N_HEADS, tm, QK_DIM), heads), pl.BlockSpec((N_HEADS, tm, QK_DIM), heads),
                   pl.BlockSpec((N_HEADS, QK_DIM, tm), lambda i: (0, 0, i)),
                   pl.BlockSpec((N_HEADS, tm, HEAD_DIM), heads),
                   pl.BlockSpec((N_HEADS, HEAD_DIM, tm), lambda i: (0, 0, i))],
        params=_params(48, ("arbitrary",)), exchange=exchange,
        first=lambda: pl.program_id(0) == 0, last=lambda: pl.program_id(0) == n_tiles - 1)


def _lower_bound(lb_raw):
    m = jnp.max(lb_raw, axis=0, keepdims=True)
    e = jnp.exp(lb_raw - m)
    return e[0:1] / jnp.sum(e, axis=0, keepdims=True)


def _tri(inclusive_lower):
    r = lax.broadcasted_iota(jnp.int32, (HG_CHUNK, HG_CHUNK), 0)
    c = lax.broadcasted_iota(jnp.int32, (HG_CHUNK, HG_CHUNK), 1)
    return (c <= r) if inclusive_lower else (c >= r)


def _chunk_rows(n):
    return slice(n * HG_CHUNK, (n + 1) * HG_CHUNK)


def _chunk_prefix_sums(v, inclusive_lower):
    tri = _tri(inclusive_lower).astype(BF16)
    hi = v.astype(BF16)
    rest = v - hi.astype(F32)
    mid = rest.astype(BF16)
    lo = (rest - mid.astype(F32)).astype(BF16)
    pieces = jnp.concatenate([hi, mid, lo], axis=1)
    out = []
    for n in range(v.shape[0] // HG_CHUNK):
        s = _dot(tri, pieces[_chunk_rows(n)])
        out.append((s[:, 0:HEAD_DIM] + s[:, HEAD_DIM:2 * HEAD_DIM]) + s[:, 2 * HEAD_DIM:])
    return jnp.concatenate(out, axis=0)


def _per_chunk(v, row):
    n = v.shape[0] // HG_CHUNK
    v3 = v.reshape(n, HG_CHUNK, HEAD_DIM)
    return jnp.broadcast_to(v3[:, row:row + 1, :], v3.shape).reshape(v.shape)


def _hg_block(q, f_logit, lb):
    sg = _sigmoid(f_logit)
    forget = lb + (1.0 - lb) * sg
    kk = 1.0 - forget
    b = _chunk_prefix_sums(jnp.log(forget), True)
    b_ref = _per_chunk(b, HG_CHUNK // 2 - 1)
    b_last = _per_chunk(b, HG_CHUNK - 1)
    e_i = jnp.exp(b - b_ref)
    e_ri = jnp.exp(b_ref - b)
    e_b = jnp.exp(b)
    e_l = jnp.exp(b_last - b)
    return dict(sg=sg, forget=forget, e_i=e_i, e_ri=e_ri, e_b=e_b, e_l=e_l, dec=jnp.exp(b_last),
                qi=q * e_i, ki=kk * e_ri, qe=q * e_b, kl=kk * e_l)


HG_STEP_HEADS = 4


def _head_cols(hh):
    return slice(hh * HEAD_DIM, (hh + 1) * HEAD_DIM)


def _hgrn_forward(zhg, lb_raw, norm_w, exchange=None):
    t_len = zhg.shape[0]
    tb = min(512, t_len)
    n_chunks = tb // HG_CHUNK
    hs = HG_STEP_HEADS

    def body(q_ref, f_ref, v_ref, g_ref, lb_ref, w_ref, opre_ref, o_ref, st_ref, state):
        @pl.when(pl.program_id(1) == 0)
        def _():
            state[...] = jnp.zeros_like(state)

        causal = _tri(True)
        for hh in range(hs):
            cols = _head_cols(hh)
            blk = _hg_block(q_ref[:, cols], f_ref[:, cols], _lower_bound(lb_ref[:, cols]))
            v = v_ref[:, cols].astype(BF16)
            qi, ki, qe, kl = (blk[name].astype(BF16) for name in ("qi", "ki", "qe", "kl"))
            st = state[hh]
            parts = []
            for n in range(n_chunks):
                r = _chunk_rows(n)
                a = jnp.where(causal, _dot_nt(qi[r], ki[r]), 0.0).astype(BF16)
                st_ref[hh, n] = st
                parts.append(_dot(a, v[r]) + _dot_nt(qe[r], st.astype(BF16)))
                st = st * blk["dec"][n * HG_CHUNK:n * HG_CHUNK + 1] + _dot_tn(v[r], kl[r])
            state[hh] = st
            o = jnp.concatenate(parts, axis=0)
            opre_ref[:, cols] = o
            g = g_ref[:, cols]
            o_ref[:, cols] = o * lax.rsqrt(_rowmean(o * o) + RMS_EPS) * w_ref[:, cols] * (g * _sigmoid(g))

    groups = N_HEADS // hs
    wide = hs * HEAD_DIM
    col = lambda off: (lambda h, t: (t, off + h))
    nb = t_len // tb
    return _pallas(
        body, name="hgrn_forward", grid=(groups, nb), operands=(zhg, zhg, zhg, zhg, lb_raw, norm_w),
        out_shape=[jax.ShapeDtypeStruct((t_len, N_HEADS * HEAD_DIM), F32),
                   jax.ShapeDtypeStruct((t_len, N_HEADS * HEAD_DIM), F32),
                   jax.ShapeDtypeStruct((N_HEADS, t_len // HG_CHUNK, HEAD_DIM, HEAD_DIM), F32)],
        in_specs=[pl.BlockSpec((tb, wide), col(0)), pl.BlockSpec((tb, wide), col(groups)),
                  pl.BlockSpec((tb, wide), col(2 * groups)), pl.BlockSpec((tb, wide), col(3 * groups)),
                  pl.BlockSpec((2, wide), lambda h, t: (0, h)), pl.BlockSpec((1, wide), lambda h, t: (0, h))],
        out_specs=[pl.BlockSpec((tb, wide), col(0)), pl.BlockSpec((tb, wide), col(0)),
                   pl.BlockSpec((hs, n_chunks, HEAD_DIM, HEAD_DIM), lambda h, t: (h, t, 0, 0))],
        scratch_shapes=[pltpu.VMEM((hs, HEAD_DIM, HEAD_DIM), F32)],
        params=_params(40, ("arbitrary", "arbitrary")), exchange=exchange,
        first=lambda: (pl.program_id(0) == 0) & (pl.program_id(1) == 0),
        last=lambda: (pl.program_id(0) == groups - 1) & (pl.program_id(1) == nb - 1))


def _hgrn_backward(zhg, lb_raw, norm_w, o_pre, d_cat, states, exchange=None):
    t_len = zhg.shape[0]
    tb = min(512, t_len)
    n_chunks = tb // HG_CHUNK
    nb = t_len // tb
    hs = HG_STEP_HEADS

    def head(hh, q_ref, f_ref, v_ref, g_ref, lb_ref, w_ref, opre_ref, do_ref, st_ref,
             dq_ref, df_ref, dv_ref, dg_ref, sums_ref, gstate):
        cols = _head_cols(hh)
        lb = _lower_bound(lb_ref[:, cols])
        w = w_ref[:, cols]
        o = opre_ref[:, cols]
        g = g_ref[:, cols]
        d_out = do_ref[:, cols]
        r = lax.rsqrt(_rowmean(o * o) + RMS_EPS)
        sg_g = _sigmoid(g)
        dg_ref[:, cols] = (d_out * (o * r * w) * (sg_g * (1.0 + g * (1.0 - sg_g)))).astype(BF16)
        d_on = d_out * (g * sg_g)
        sums_ref[1:2, cols] += _colsum(d_on * o * r)
        dy = d_on * w
        d_o = (r * dy - o * (r * r * r) * _rowmean(dy * o)).astype(BF16)
        blk = _hg_block(q_ref[:, cols], f_ref[:, cols], lb)
        v = v_ref[:, cols].astype(BF16)
        qi, ki, qe, kl = (blk[name].astype(BF16) for name in ("qi", "ki", "qe", "kl"))
        causal = _tri(True)
        row_id = lax.broadcasted_iota(jnp.int32, (HG_CHUNK, HEAD_DIM), 0)
        gt = gstate[hh]
        d_v, d_qi, d_ki, d_qe, d_kl, d_dec = ([None] * n_chunks for _ in range(6))
        for n in reversed(range(n_chunks)):
            rows = _chunk_rows(n)
            st = st_ref[hh, n]
            a = jnp.where(causal, _dot_nt(qi[rows], ki[rows]), 0.0).astype(BF16)
            d_a = jnp.where(causal, _dot_nt(d_o[rows], v[rows]), 0.0).astype(BF16)
            gt_b = gt.astype(BF16)
            d_v[n] = _dot_tn(a, d_o[rows]) + _dot_nt(kl[rows], gt_b)
            d_qi[n] = _dot(d_a, ki[rows])
            d_ki[n] = _dot_tn(d_a, qi[rows])
            d_qe[n] = _dot(d_o[rows], st.astype(BF16))
            d_kl[n] = _dot(v[rows], gt_b)
            d_dec[n] = jnp.where(row_id == HG_CHUNK - 1, _colsum(gt * st), 0.0)
            gt = gt * blk["dec"][n * HG_CHUNK:n * HG_CHUNK + 1] + _dot_tn(d_o[rows], qe[rows])
        gstate[hh] = gt
        d_qi, d_ki, d_qe, d_kl, d_dec = (jnp.concatenate(p, axis=0) for p in (d_qi, d_ki, d_qe, d_kl, d_dec))
        dv_ref[:, cols] = jnp.concatenate(d_v, axis=0).astype(BF16)
        dq_ref[:, cols] = (d_qi * blk["e_i"] + d_qe * blk["e_b"]).astype(BF16)
        d_k = d_ki * blk["e_ri"] + d_kl * blk["e_l"]
        t_qi = d_qi * blk["qi"]
        t_ki = d_ki * blk["ki"]
        t_kl = d_kl * blk["kl"]
        at_ref, at_last = [], []
        for n in range(n_chunks):
            rows = _chunk_rows(n)
            at_ref.append(jnp.where(row_id == HG_CHUNK // 2 - 1, _colsum(t_ki[rows] - t_qi[rows]), 0.0))
            at_last.append(jnp.where(row_id == HG_CHUNK - 1, _colsum(t_kl[rows]), 0.0))
        d_b = (t_qi - t_ki + d_qe * blk["qe"] - t_kl + jnp.concatenate(at_ref, axis=0)
               + jnp.concatenate(at_last, axis=0) + d_dec * blk["dec"])
        d_forget = _chunk_prefix_sums(d_b, False) / blk["forget"] - d_k
        sg = blk["sg"]
        df_ref[:, cols] = (d_forget * (1.0 - lb) * sg * (1.0 - sg)).astype(BF16)
        sums_ref[0:1, cols] += _colsum(d_forget * (1.0 - sg))

    def body(*refs):
        sums_ref, gstate = refs[-2], refs[-1]

        @pl.when(pl.program_id(1) == 0)
        def _():
            gstate[...] = jnp.zeros_like(gstate)
            sums_ref[...] = jnp.zeros_like(sums_ref)

        for hh in range(hs):
            head(hh, *refs)

    groups = N_HEADS // hs
    wide = hs * HEAD_DIM
    col = lambda off: (lambda h, t: (nb - 1 - t, off + h))
    return _pallas(
        body, name="hgrn_backward", grid=(groups, nb),
        operands=(zhg, zhg, zhg, zhg, lb_raw, norm_w, o_pre, d_cat, states),
        out_shape=[jax.ShapeDtypeStruct((t_len, N_HEADS * HEAD_DIM), BF16)] * 4
        + [jax.ShapeDtypeStruct((8, N_HEADS * HEAD_DIM), F32)],
        in_specs=[pl.BlockSpec((tb, wide), col(0)), pl.BlockSpec((tb, wide), col(groups)),
                  pl.BlockSpec((tb, wide), col(2 * groups)), pl.BlockSpec((tb, wide), col(3 * groups)),
                  pl.BlockSpec((2, wide), lambda h, t: (0, h)), pl.BlockSpec((1, wide), lambda h, t: (0, h)),
                  pl.BlockSpec((tb, wide), col(0)), pl.BlockSpec((tb, wide), col(0)),
                  pl.BlockSpec((hs, n_chunks, HEAD_DIM, HEAD_DIM), lambda h, t: (h, nb - 1 - t, 0, 0))],
        out_specs=[pl.BlockSpec((tb, wide), col(0))] * 4 + [pl.BlockSpec((8, wide), lambda h, t: (0, h))],
        scratch_shapes=[pltpu.VMEM((hs, HEAD_DIM, HEAD_DIM), F32)],
        params=_params(40, ("arbitrary", "arbitrary")), exchange=exchange,
        first=lambda: (pl.program_id(0) == 0) & (pl.program_id(1) == 0),
        last=lambda: (pl.program_id(0) == groups - 1) & (pl.program_id(1) == nb - 1))


ATT_LOG2 = ATT_SCALE * 1.4426950408889634


def _triangle_steps(nq, q_major):
    if q_major:
        pairs = [(i, j) for i in range(nq) for j in range(i + 1)]
    else:
        pairs = [(i, j) for j in range(nq) for i in range(j, nq)]
    return jnp.array([p[0] for p in pairs], jnp.int32), jnp.array([p[1] for p in pairs], jnp.int32)


def _key_le_query(t):
    return lax.broadcasted_iota(jnp.int32, (t, t), 0) <= lax.broadcasted_iota(jnp.int32, (t, t), 1)


def _attention_forward(q, k, v_t, exchange=None):
    t_len = q.shape[1]
    tq = min(512, t_len)
    nq = t_len // tq
    qi_tab, ki_tab = _triangle_steps(nq, True)

    def body(qi_ref, ki_ref, q_ref, k_ref, vt_ref, o_ref, lse_ref, m_s, l_s, acc_s):
        step = pl.program_id(0)
        qi, ki = qi_ref[step], ki_ref[step]

        @pl.when(ki == 0)
        def _():
            m_s[...] = jnp.full_like(m_s, NEG_BIG)
            l_s[...] = jnp.zeros_like(l_s)
            acc_s[...] = jnp.zeros_like(acc_s)

        def accumulate(masked):
            for h in range(N_HEADS):
                s_t = _dot_nt(k_ref[h], q_ref[h]) * ATT_LOG2
                if masked:
                    s_t = jnp.where(_key_le_query(tq), s_t, NEG_BIG)
                m_old = m_s[h]
                m_new = jnp.maximum(m_old, jnp.max(s_t, axis=0, keepdims=True))
                alpha = jnp.exp2(m_old - m_new)
                p_t = jnp.exp2(s_t - m_new)
                l_s[h] = alpha * l_s[h] + jnp.sum(p_t, axis=0, keepdims=True)
                acc_s[h] = alpha * acc_s[h] + _dot(vt_ref[h], p_t.astype(BF16))
                m_s[h] = m_new

        @pl.when(ki < qi)
        def _():
            accumulate(False)

        @pl.when(ki == qi)
        def _():
            accumulate(True)
            for h in range(N_HEADS):
                o_ref[:, h * HEAD_DIM:(h + 1) * HEAD_DIM] = jnp.transpose(acc_s[h] / l_s[h])
                lse_ref[h] = m_s[h] + jnp.log2(l_s[h])

    n_steps = qi_tab.shape[0]
    return _pallas(
        body, name="attention_forward", grid=(n_steps,), prefetch=(qi_tab, ki_tab), operands=(q, k, v_t),
        out_shape=[jax.ShapeDtypeStruct((t_len, N_HEADS * HEAD_DIM), F32),
                   jax.ShapeDtypeStruct((N_HEADS, 1, t_len), F32)],
        in_specs=[pl.BlockSpec((N_HEADS, tq, QK_DIM), lambda s, qt, kt: (0, qt[s], 0)),
                  pl.BlockSpec((N_HEADS, tq, QK_DIM), lambda s, qt, kt: (0, kt[s], 0)),
                  pl.BlockSpec((N_HEADS, HEAD_DIM, tq), lambda s, qt, kt: (0, 0, kt[s]))],
        out_specs=[pl.BlockSpec((tq, N_HEADS * HEAD_DIM), lambda s, qt, kt: (qt[s], 0)),
                   pl.BlockSpec((N_HEADS, 1, tq), lambda s, qt, kt: (0, 0, qt[s]))],
        scratch_shapes=[pltpu.VMEM((N_HEADS, 1, tq), F32), pltpu.VMEM((N_HEADS, 1, tq), F32),
                        pltpu.VMEM((N_HEADS, HEAD_DIM, tq), F32)],
        params=_params(48, ("arbitrary",)), exchange=exchange,
        first=lambda qt, kt: pl.program_id(0) == 0, last=lambda qt, kt: pl.program_id(0) == n_steps - 1)


BWD_HEADS = 4


def _attention_backward(q, k, k_t, v, d_cat, lse, delta, exchange=None):
    t_len = q.shape[1]
    tq = min(512, t_len)
    nq = t_len // tq
    hp = BWD_HEADS
    qi_tab, ki_tab = _triangle_steps(nq, False)

    def body(qi_ref, ki_ref, q_ref, k_ref, kt_ref, v_ref, do_ref, lse_ref, delta_ref, dqt_hbm, dk_ref, dv_ref,
             dqt_s, dk_s, dv_s):
        group, step = pl.program_id(0), pl.program_id(1)
        qi, ki = qi_ref[step], ki_ref[step]

        @pl.when(step == 0)
        def _():
            dqt_s[...] = jnp.zeros_like(dqt_s)

        @pl.when(qi == ki)
        def _():
            dk_s[...] = jnp.zeros_like(dk_s)
            dv_s[...] = jnp.zeros_like(dv_s)

        def accumulate(masked):
            for h in range(hp):
                do_b = do_ref[:, h * HEAD_DIM:(h + 1) * HEAD_DIM].astype(BF16)
                s_t = _dot_nt(k_ref[h], q_ref[h]) * ATT_LOG2
                if masked:
                    s_t = jnp.where(_key_le_query(tq), s_t, NEG_BIG)
                p_t = jnp.exp2(s_t - lse_ref[h])
                dp_t = _dot_nt(v_ref[h], do_b)
                ds_t = (p_t * (dp_t - delta_ref[h]) * ATT_SCALE).astype(BF16)
                dv_s[h] += _dot(p_t.astype(BF16), do_b)
                dk_s[h] += _dot(ds_t, q_ref[h])
                dqt_s[h, qi] += _dot(kt_ref[h], ds_t)

        @pl.when(ki < qi)
        def _():
            accumulate(False)

        @pl.when(ki == qi)
        def _():
            accumulate(True)
            for h in range(hp):
                pltpu.sync_copy(dqt_s.at[h, qi], dqt_hbm.at[group * hp + h, qi])

        @pl.when(qi == nq - 1)
        def _():
            dk_ref[...] = dk_s[...]
            dv_ref[...] = dv_s[...]

    wide = hp * HEAD_DIM
    n_groups, n_steps = N_HEADS // hp, qi_tab.shape[0]
    return _pallas(
        body, name="attention_backward", grid=(n_groups, n_steps), prefetch=(qi_tab, ki_tab),
        operands=(q, k, k_t, v, d_cat, lse, delta),
        out_shape=[jax.ShapeDtypeStruct((N_HEADS, nq, QK_DIM, tq), F32),
                   jax.ShapeDtypeStruct((N_HEADS, t_len, QK_DIM), F32),
                   jax.ShapeDtypeStruct((N_HEADS, t_len, HEAD_DIM), F32)],
        in_specs=[pl.BlockSpec((hp, tq, QK_DIM), lambda g, s, qt, kt: (g, qt[s], 0)),
                  pl.BlockSpec((hp, tq, QK_DIM), lambda g, s, qt, kt: (g, kt[s], 0)),
                  pl.BlockSpec((hp, QK_DIM, tq), lambda g, s, qt, kt: (g, 0, kt[s])),
                  pl.BlockSpec((hp, tq, HEAD_DIM), lambda g, s, qt, kt: (g, kt[s], 0)),
                  pl.BlockSpec((tq, wide), lambda g, s, qt, kt: (qt[s], n_groups + g)),
                  pl.BlockSpec((hp, 1, tq), lambda g, s, qt, kt: (g, 0, qt[s])),
                  pl.BlockSpec((hp, 1, tq), lambda g, s, qt, kt: (g, 0, qt[s]))],
        out_specs=[pl.BlockSpec(memory_space=pl.ANY),
                   pl.BlockSpec((hp, tq, QK_DIM), lambda g, s, qt, kt: (g, kt[s], 0)),
                   pl.BlockSpec((hp, tq, HEAD_DIM), lambda g, s, qt, kt: (g, kt[s], 0))],
        scratch_shapes=[pltpu.VMEM((hp, nq, QK_DIM, tq), F32), pltpu.VMEM((hp, tq, QK_DIM), F32),
                        pltpu.VMEM((hp, tq, HEAD_DIM), F32)],
        params=_params(58, ("arbitrary", "arbitrary")), exchange=exchange,
        first=lambda qt, kt: (pl.program_id(0) == 0) & (pl.program_id(1) == 0),
        last=lambda qt, kt: (pl.program_id(0) == n_groups - 1) & (pl.program_id(1) == n_steps - 1))


def _out_project(o_hg, o_mla, x, g_a, w_out, exchange=None):
    t_len = x.shape[0]
    tm = min(512, t_len)
    half = N_HEADS * HEAD_DIM

    def body(ohg_ref, omla_ref, x_ref, ga_ref, w_ref, cat_ref, mix_ref, xhat_ref, rstd_ref):
        a = ohg_ref[...].astype(BF16)
        b = omla_ref[...].astype(BF16)
        cat_ref[:, 0:half] = a
        cat_ref[:, half:2 * half] = b
        mix = _dot(a, w_ref[0:half, :]) + _dot(b, w_ref[half:2 * half, :])
        mix_ref[...] = mix
        r1 = DN_ALPHA * x_ref[...] + (1.0 + ga_ref[...]) * mix
        xc = r1 - _rowmean(r1)
        rstd = lax.rsqrt(_rowmean(xc * xc) + LN_EPS)
        xhat_ref[...] = xc * rstd
        rstd_ref[...] = rstd

    row = lambda i: (i, 0)
    fixed = lambda i: (0, 0)
    n_tiles = t_len // tm
    return _pallas(
        body, name="out_project", grid=(n_tiles,), operands=(o_hg, o_mla, x, g_a, w_out),
        out_shape=[jax.ShapeDtypeStruct((t_len, D_MODEL), BF16), jax.ShapeDtypeStruct((t_len, D_MODEL), F32),
                   jax.ShapeDtypeStruct((t_len, D_MODEL), F32), jax.ShapeDtypeStruct((t_len, 1), F32)],
        in_specs=[pl.BlockSpec((tm, half), row), pl.BlockSpec((tm, half), row), pl.BlockSpec((tm, D_MODEL), row),
                  pl.BlockSpec((1, D_MODEL), fixed), pl.BlockSpec((D_MODEL, D_MODEL), fixed)],
        out_specs=[pl.BlockSpec((tm, D_MODEL), row), pl.BlockSpec((tm, D_MODEL), row),
                   pl.BlockSpec((tm, D_MODEL), row), pl.BlockSpec((tm, 1), row)],
        params=_params(48, ("arbitrary",)), exchange=exchange,
        first=lambda: pl.program_id(0) == 0, last=lambda: pl.program_id(0) == n_tiles - 1)


V_LN1G, V_LN1B, V_SCM, V_SHM, V_GM, V_GA, V_LN2G, V_LN2B = range(8)
S_DLN2G, S_DLN2B, S_DGM, S_DSCM, S_DSHM, S_DLN1G, S_DLN1B, S_DGA, S_LOSS = range(9)


def _mlp_and_back(xhat1, rstd1, mix, target, o_mla, vecs, w1_top, w1_bottom, w2, w_out):
    t_len = xhat1.shape[0]
    tm = min(256, t_len)
    n_ff = w1_top.shape[0]
    ff = w1_top.shape[2]
    top_rows = w1_top.shape[1]

    def body(xhat_ref, rstd_ref, mix_ref, tgt_ref, omla_ref, vec_ref, w1_top_hbm, w1_bottom_hbm, w2_hbm, wout_hbm,
             act_ref, dhp_ref, um_ref, dh_ref, dmix_ref, dcat_ref, dr1_ref, sums_ref, delta_ref,
             w1_s, w2_s, wout_s, hp_s, load_sems):
        @pl.when(pl.program_id(0) == 0)
        def _():
            loads = [pltpu.make_async_copy(w1_top_hbm, w1_s.at[:, 0:top_rows], load_sems.at[0]),
                     pltpu.make_async_copy(w1_bottom_hbm, w1_s.at[:, top_rows:D_MODEL], load_sems.at[3]),
                     pltpu.make_async_copy(w2_hbm, w2_s, load_sems.at[1]),
                     pltpu.make_async_copy(wout_hbm, wout_s, load_sems.at[2])]
            for cp in loads:
                cp.start()
            sums_ref[...] = jnp.zeros_like(sums_ref)
            for cp in loads:
                cp.wait()

        vec = lambda r: vec_ref[r:r + 1, :]
        xhat = xhat_ref[...]
        x1 = xhat * vec(V_LN1G) + vec(V_LN1B)
        um = (x1 * (1.0 + vec(V_SCM)) + vec(V_SHM)).astype(BF16)
        um_ref[...] = um
        h = jnp.zeros((tm, D_MODEL), F32)
        for j in range(n_ff):
            hp = _dot(um, w1_s[j])
            hp_s[j] = hp
            act = jnp.square(jnp.maximum(hp, 0.0)).astype(BF16)
            act_ref[:, j * ff:(j + 1) * ff] = act
            h = h + _dot(act, w2_s[j])
        r2 = DN_ALPHA * x1 + (1.0 + vec(V_GM)) * h
        xc = r2 - _rowmean(r2)
        rstd2 = lax.rsqrt(_rowmean(xc * xc) + LN_EPS)
        xhat2 = xc * rstd2
        err = xhat2 * vec(V_LN2G) + vec(V_LN2B) - tgt_ref[...]
        loss = 0.5 * jnp.sum(_rowmean(err * err))
        dy = err * (1.0 / D_MODEL)
        dxh = dy * vec(V_LN2G)
        dr2 = rstd2 * (dxh - _rowmean(dxh) - xhat2 * _rowmean(dxh * xhat2))
        dh = ((1.0 + vec(V_GM)) * dr2).astype(BF16)
        dh_ref[...] = dh
        sums_ref[S_DLN2G:S_DLN2G + 1, :] += _colsum(dy * xhat2)
        sums_ref[S_DLN2B:S_DLN2B + 1, :] += _colsum(dy)
        sums_ref[S_DGM:S_DGM + 1, :] += _colsum(dr2 * h)
        sums_ref[S_LOSS:S_LOSS + 1, :] += jnp.full((1, D_MODEL), loss, F32)
        du = jnp.zeros((tm, D_MODEL), F32)
        for j in range(n_ff):
            dhp = (_dot_nt(dh, w2_s[j]) * (2.0 * jnp.maximum(hp_s[j], 0.0))).astype(BF16)
            dhp_ref[:, j * ff:(j + 1) * ff] = dhp
            du = du + _dot_nt(dhp, w1_s[j])
        sums_ref[S_DSCM:S_DSCM + 1, :] += _colsum(du * x1)
        sums_ref[S_DSHM:S_DSHM + 1, :] += _colsum(du)
        dx1 = DN_ALPHA * dr2 + du * (1.0 + vec(V_SCM))
        sums_ref[S_DLN1G:S_DLN1G + 1, :] += _colsum(dx1 * xhat)
        sums_ref[S_DLN1B:S_DLN1B + 1, :] += _colsum(dx1)
        dxh1 = dx1 * vec(V_LN1G)
        dr1 = rstd_ref[...] * (dxh1 - _rowmean(dxh1) - xhat * _rowmean(dxh1 * xhat))
        dr1_ref[...] = dr1
        sums_ref[S_DGA:S_DGA + 1, :] += _colsum(dr1 * mix_ref[...])
        dmix = ((1.0 + vec(V_GA)) * dr1).astype(BF16)
        dmix_ref[...] = dmix
        dcat = _dot_nt(dmix, wout_s[...])
        dcat_ref[...] = dcat
        ones = jnp.ones((8, HEAD_DIM), F32)
        half = N_HEADS * HEAD_DIM
        for hd in range(N_HEADS):
            prod = dcat[:, half + hd * HEAD_DIM:half + (hd + 1) * HEAD_DIM] * omla_ref[:, hd * HEAD_DIM:(hd + 1) * HEAD_DIM]
            delta_ref[hd] = lax.dot_general(ones, prod, (((1,), (1,)), ((), ())), preferred_element_type=F32,
                                            precision=lax.Precision.HIGHEST)[0:1]

    row = lambda i: (i, 0)
    fixed = lambda i: (0, 0)
    any_spec = pl.BlockSpec(memory_space=pl.ANY)
    return _pcall(
        body, name="mlp_and_back", grid=(t_len // tm,),
        out_shape=[jax.ShapeDtypeStruct((t_len, D_FF), BF16), jax.ShapeDtypeStruct((t_len, D_FF), BF16),
                   jax.ShapeDtypeStruct((t_len, D_MODEL), BF16), jax.ShapeDtypeStruct((t_len, D_MODEL), BF16),
                   jax.ShapeDtypeStruct((t_len, D_MODEL), BF16), jax.ShapeDtypeStruct((t_len, D_MODEL), F32),
                   jax.ShapeDtypeStruct((t_len, D_MODEL), F32), jax.ShapeDtypeStruct((16, D_MODEL), F32),
                   jax.ShapeDtypeStruct((N_HEADS, 1, t_len), F32)],
        in_specs=[pl.BlockSpec((tm, D_MODEL), row), pl.BlockSpec((tm, 1), row), pl.BlockSpec((tm, D_MODEL), row),
                  pl.BlockSpec((tm, D_MODEL), row), pl.BlockSpec((tm, N_HEADS * HEAD_DIM), row),
                  pl.BlockSpec((8, D_MODEL), fixed), any_spec, any_spec, any_spec, any_spec],
        out_specs=[pl.BlockSpec((tm, D_FF), row), pl.BlockSpec((tm, D_FF), row), pl.BlockSpec((tm, D_MODEL), row),
                   pl.BlockSpec((tm, D_MODEL), row), pl.BlockSpec((tm, D_MODEL), row), pl.BlockSpec((tm, D_MODEL), row),
                   pl.BlockSpec((tm, D_MODEL), row), pl.BlockSpec((16, D_MODEL), fixed),
                   pl.BlockSpec((N_HEADS, 1, tm), lambda i: (0, 0, i))],
        scratch_shapes=[pltpu.VMEM((n_ff, D_MODEL, ff), BF16), pltpu.VMEM(w2.shape, BF16), pltpu.VMEM(w_out.shape, BF16),
                        pltpu.VMEM((n_ff, tm, ff), F32), pltpu.SemaphoreType.DMA((4,))],
        compiler_params=_params(56, ("arbitrary",)),
        operands=(xhat1, rstd1, mix, target, o_mla, vecs, w1_top, w1_bottom, w2, w_out))


def _in_project_backward(dq, dk, dv, cq, ckv, pos, invf, q_norm_w, kv_norm_w, w_q, w_kv,
                         d_hq, d_hf, d_hi, d_hg, w_in, dr1, x, sc_a, exchange=None):
    t_len = x.shape[0]
    tm = min(512, t_len)
    per_q = dq.shape[3] // tm
    hgw = N_HEADS * HEAD_DIM

    def body(dq_ref, dk_ref, dv_ref, cq_ref, ckv_ref, pos_ref, invf_ref, qn_ref, kvn_ref, wq_ref, wkv_ref,
             dhq_ref, dhf_ref, dhi_ref, dhg_ref, win_ref, dr1_ref, x_ref, sc_ref,
             dz_ref, dqf_ref, dkvu_ref, cqn_ref, ckvn_ref, gx_ref, sums_ref):
        @pl.when(pl.program_id(0) == 0)
        def _():
            sums_ref[...] = jnp.zeros_like(sums_ref)

        cos_t, sin_t = _rope_tables(pos_ref[...], invf_ref[...])
        cq = cq_ref[...]
        ckv = ckv_ref[...]
        rq = lax.rsqrt(_rowmean(cq * cq) + RMS_EPS)
        rkv = lax.rsqrt(_rowmean(ckv * ckv) + RMS_EPS)
        cqn_ref[...] = (cq * rq * qn_ref[...]).astype(BF16)
        ckvn_ref[...] = (ckv * rkv * kvn_ref[...]).astype(BF16)
        d_cqn = jnp.zeros((tm, Q_RANK), F32)
        d_ckvn = jnp.zeros((tm, KV_RANK), F32)
        d_kpe = jnp.zeros((tm, 128), F32)
        for h in range(N_HEADS):
            dqh = jnp.transpose(dq_ref[h])
            dqf_ref[h, :, 0:HEAD_DIM] = dqh[:, :HEAD_DIM].astype(BF16)
            dqf_ref[h, :, HEAD_DIM:QK_DIM] = _unrope(dqh[:, HEAD_DIM:], cos_t, sin_t).astype(BF16)
            d_cqn = d_cqn + _dot_nt(dqf_ref[h], wq_ref[h])
            dkh = dk_ref[h]
            d_kpe = d_kpe + dkh[:, HEAD_DIM:]
            dkvu_ref[h, :, 0:HEAD_DIM] = dkh[:, :HEAD_DIM].astype(BF16)
            dkvu_ref[h, :, HEAD_DIM:2 * HEAD_DIM] = dv_ref[h].astype(BF16)
            d_ckvn = d_ckvn + _dot_nt(dkvu_ref[h], wkv_ref[h])
        dyq = d_cqn * qn_ref[...]
        dykv = d_ckvn * kvn_ref[...]
        sums_ref[2:3, 0:Q_RANK] += _colsum(d_cqn * cq * rq)
        sums_ref[3:4, 0:KV_RANK] += _colsum(d_ckvn * ckv * rkv)
        dz_ref[:, 0:hgw] = dhq_ref[...]
        dz_ref[:, hgw:2 * hgw] = dhf_ref[...]
        dz_ref[:, 2 * hgw:3 * hgw] = dhi_ref[...]
        dz_ref[:, 3 * hgw:4 * hgw] = dhg_ref[...]
        dz_ref[:, HG_COLS:HG_COLS + Q_RANK] = (rq * dyq - cq * (rq * rq * rq) * _rowmean(dyq * cq)).astype(BF16)
        dz_ref[:, HG_COLS + Q_RANK:HG_COLS + Q_RANK + KV_RANK] = (
            rkv * dykv - ckv * (rkv * rkv * rkv) * _rowmean(dykv * ckv)).astype(BF16)
        dz_ref[:, HG_COLS + Q_RANK + KV_RANK:] = _unrope(d_kpe, cos_t, sin_t).astype(BF16)
        du = _dot(dz_ref[...], win_ref[...])
        xv = x_ref[...]
        gx_ref[...] = DN_ALPHA * dr1_ref[...] + (1.0 + sc_ref[...]) * du
        sums_ref[0:1, :] += _colsum(du * xv)
        sums_ref[1:2, :] += _colsum(du)

    row = lambda i: (i, 0)
    fixed2 = lambda i: (0, 0)
    fixed3 = lambda i: (0, 0, 0)
    heads = lambda i: (0, i, 0)
    n_tiles = t_len // tm
    return _pallas(
        body, name="in_project_backward", grid=(n_tiles,),
        operands=(dq, dk, dv, cq, ckv, pos, invf, q_norm_w, kv_norm_w, w_q, w_kv, d_hq, d_hf, d_hi, d_hg, w_in, dr1, x,
                  sc_a),
        out_shape=[jax.ShapeDtypeStruct((t_len, IN_COLS_PAD), BF16), jax.ShapeDtypeStruct((N_HEADS, t_len, QK_DIM), BF16),
                   jax.ShapeDtypeStruct((N_HEADS, t_len, 2 * HEAD_DIM), BF16), jax.ShapeDtypeStruct((t_len, Q_RANK), BF16),
                   jax.ShapeDtypeStruct((t_len, KV_RANK), BF16), jax.ShapeDtypeStruct((t_len, D_MODEL), F32),
                   jax.ShapeDtypeStruct((8, D_MODEL), F32)],
        in_specs=[pl.BlockSpec((N_HEADS, None, QK_DIM, tm), lambda i: (0, i // per_q, 0, i % per_q)),
                  pl.BlockSpec((N_HEADS, tm, QK_DIM), heads),
                  pl.BlockSpec((N_HEADS, tm, HEAD_DIM), heads), pl.BlockSpec((tm, Q_RANK), row),
                  pl.BlockSpec((tm, KV_RANK), row), pl.BlockSpec((tm, 1), row), pl.BlockSpec((1, 128), fixed2),
                  pl.BlockSpec((1, Q_RANK), fixed2), pl.BlockSpec((1, KV_RANK), fixed2),
                  pl.BlockSpec((N_HEADS, Q_RANK, QK_DIM), fixed3), pl.BlockSpec((N_HEADS, KV_RANK, 2 * HEAD_DIM), fixed3),
                  pl.BlockSpec((tm, hgw), row), pl.BlockSpec((tm, hgw), row), pl.BlockSpec((tm, hgw), row),
                  pl.BlockSpec((tm, hgw), row), pl.BlockSpec((IN_COLS_PAD, D_MODEL), fixed2),
                  pl.BlockSpec((tm, D_MODEL), row), pl.BlockSpec((tm, D_MODEL), row), pl.BlockSpec((1, D_MODEL), fixed2)],
        out_specs=[pl.BlockSpec((tm, IN_COLS_PAD), row), pl.BlockSpec((N_HEADS, tm, QK_DIM), heads),
                   pl.BlockSpec((N_HEADS, tm, 2 * HEAD_DIM), heads), pl.BlockSpec((tm, Q_RANK), row),
                   pl.BlockSpec((tm, KV_RANK), row), pl.BlockSpec((tm, D_MODEL), row), pl.BlockSpec((8, D_MODEL), fixed2)],
        params=_params(48, ("arbitrary",)), exchange=exchange,
        first=lambda: pl.program_id(0) == 0, last=lambda: pl.program_id(0) == n_tiles - 1)


def _weight_grad(a, b, name, n_blocks, bn, a_blocked=False, b_blocked=True, exchange=None, token_tile=512):
    t_len = a.shape[0]
    m = a.shape[1] // n_blocks if a_blocked else a.shape[1]
    bt = min(token_tile, t_len)

    def body(a_ref, b_ref, o_ref):
        @pl.when(pl.program_id(1) == 0)
        def _():
            o_ref[...] = jnp.zeros_like(o_ref)

        o_ref[...] += _dot_tn(a_ref[...].astype(BF16), b_ref[...].astype(BF16))

    a_spec = pl.BlockSpec((bt, m), (lambda n, t: (t, n)) if a_blocked else (lambda n, t: (t, 0)))
    if b.ndim == 3:
        b_spec = pl.BlockSpec((None, bt, bn), lambda n, t: (n, t, 0))
    else:
        b_spec = pl.BlockSpec((bt, bn), (lambda n, t: (t, n)) if b_blocked else (lambda n, t: (t, 0)))
    nt = t_len // bt
    (out,), landed = _pallas(
        body, name=name, grid=(n_blocks, nt), operands=(a, b),
        out_shape=[jax.ShapeDtypeStruct((n_blocks, m, bn), F32)],
        in_specs=[a_spec, b_spec],
        out_specs=[pl.BlockSpec((None, m, bn), lambda n, t: (n, 0, 0))],
        params=_params(56, ("arbitrary", "arbitrary")), exchange=exchange,
        first=lambda: (pl.program_id(0) == 0) & (pl.program_id(1) == 0),
        last=lambda: (pl.program_id(0) == n_blocks - 1) & (pl.program_id(1) == nt - 1))
    return (out, landed) if exchange else out


SMALL_PLACE = {"ln1_g": (6, 0), "ln1_b": (7, 0), "ln2_g": (8, 0), "ln2_b": (9, 0), "hg_norm_w": (10, 512),
               "mla_q_norm_w": (11, 0), "mla_kv_norm_w": (11, Q_RANK)}
SMALL_LB_ROW, SMALL_LOSS_ROW = 10, 12


def _small_params_step(gathered, params):
    names = list(params)

    def body(g_ref, *refs):
        ins, outs = refs[:3 * len(names)], refs[3 * len(names):]
        loss_ref, outs = outs[0], outs[1:]
        tot = g_ref[0]
        for d in range(1, N_DEV):
            tot = tot + g_ref[d]
        loss_ref[...] = tot[SMALL_LOSS_ROW:SMALL_LOSS_ROW + 1, 0:128]

        def update(i, grad, rows=slice(None), lanes=slice(None)):
            w_ref, m_ref, v_ref = ins[3 * i:3 * i + 3]
            g_out, d_out, nm_out, nv_out = outs[4 * i:4 * i + 4]
            g_out[rows, lanes] = grad
            d_out[rows, lanes], nm_out[rows, lanes], nv_out[rows, lanes] = _adamw_update(
                w_ref[rows, lanes], grad, m_ref[rows, lanes], v_ref[rows, lanes])

        for i, name in enumerate(names):
            if name == "b_ada":
                for r in range(6):
                    update(i, tot[r:r + 1, :], lanes=slice(r * D_MODEL, (r + 1) * D_MODEL))
            elif name == "hg_lower_bounds":
                lb = _lower_bound(ins[3 * i][...])
                d0 = tot[SMALL_LB_ROW:SMALL_LB_ROW + 1, 0:512] * lb * (1.0 - lb)
                update(i, d0, rows=slice(0, 1))
                update(i, -d0, rows=slice(1, 2))
            else:
                row, lane = SMALL_PLACE[name]
                update(i, tot[row:row + 1, lane:lane + params[name][0].shape[1]])

    flat_in = [a for name in names for a in params[name]]
    shapes = [jax.ShapeDtypeStruct((1, 128), F32)] + [jax.ShapeDtypeStruct(params[name][0].shape, F32)
                                                      for name in names for _ in range(4)]
    out = pl.pallas_call(body, name="small_params_step", out_shape=shapes)(gathered, *flat_in)
    return out[0], {name: out[1 + 4 * i:5 + 4 * i] for i, name in enumerate(names)}


def _adamw_update(w, gv, m, v):
    nm = ADAM_B1 * m + (1.0 - ADAM_B1) * gv
    nv = ADAM_B2 * v + (1.0 - ADAM_B2) * jnp.square(gv)
    m_hat = nm / (1.0 - ADAM_B1 ** ADAM_STEP)
    v_hat = nv / (1.0 - ADAM_B2 ** ADAM_STEP)
    return -ADAM_LR * (m_hat / (jnp.sqrt(v_hat) + ADAM_EPS) + ADAM_WD * w), nm, nv


def _adamw_halves(core, w, mine, theirs, m, v, name):
    rows, cols = w.shape
    h = rows // 2
    tr = _row_tile(h)
    per_half = h // tr

    def body(core_ref, w_ref, mine_ref, theirs_ref, m_ref, v_ref, g_ref, d_ref, nm_ref, nv_ref):
        is_mine = pl.program_id(0) // per_half == core_ref[0]
        gv = jnp.where(is_mine, mine_ref[...], theirs_ref[...])
        g_ref[...] = gv
        d_ref[...], nm_ref[...], nv_ref[...] = _adamw_update(w_ref[...], gv, m_ref[...], v_ref[...])

    full = pl.BlockSpec((tr, cols), lambda i, core_ref: (i, 0))
    part = pl.BlockSpec((tr, cols), lambda i, core_ref: (i % per_half, 0))
    return _pcall(
        body, name=name, out_shape=[jax.ShapeDtypeStruct(w.shape, F32)] * 4,
        grid_spec=pltpu.PrefetchScalarGridSpec(
            num_scalar_prefetch=1, grid=(rows // tr,), in_specs=[full, part, part, full, full], out_specs=[full] * 4),
        compiler_params=_params(40, ("arbitrary",)),
        operands=(core, w, mine, theirs, m, v))


def _adamw(w, g, m, v, name):
    rows, cols = w.shape
    tr = _row_tile(rows) if rows >= 8 else rows

    def body(w_ref, g_ref, m_ref, v_ref, d_ref, nm_ref, nv_ref):
        d_ref[...], nm_ref[...], nv_ref[...] = _adamw_update(w_ref[...], g_ref[...], m_ref[...], v_ref[...])

    spec = pl.BlockSpec((tr, cols), lambda i: (i, 0))
    return _pcall(
        body, name=name, grid=(rows // tr,),
        out_shape=[jax.ShapeDtypeStruct(w.shape, F32)] * 3,
        in_specs=[spec] * 4, out_specs=[spec] * 3,
        compiler_params=_params(40, ("arbitrary",)),
        operands=(w, g, m, v))


def kernel(x, c, positions, w_ada, b_ada, w_in, hg_lower_bounds, hg_norm_w, mla_q_norm_w, w_q_up, mla_kv_norm_w, w_kv_up, w_out, ln1_g, ln1_b, w_mlp_in, w_mlp_out, ln2_g, ln2_b, loss_target, m_w_ada, m_b_ada, m_w_in, m_hg_lower_bounds, m_hg_norm_w, m_mla_q_norm_w, m_w_q_up, m_mla_kv_norm_w, m_w_kv_up, m_w_out, m_ln1_g, m_ln1_b, m_w_mlp_in, m_w_mlp_out, m_ln2_g, m_ln2_b, v_w_ada, v_b_ada, v_w_in, v_hg_lower_bounds, v_hg_norm_w, v_mla_q_norm_w, v_w_q_up, v_mla_kv_norm_w, v_w_kv_up, v_w_out, v_ln1_g, v_ln1_b, v_w_mlp_in, v_w_mlp_out, v_ln2_g, v_ln2_b):
    ix, iy, ic = _mesh_pos()
    chip = 2 * ix + iy
    me = 4 * ix + 2 * iy + ic
    core_arr = jnp.reshape(ic, (1,)).astype(jnp.int32)
    chip_arr = jnp.reshape(chip, (1,)).astype(jnp.int32)

    xs = x[0]
    target = loss_target[0]
    t_len = xs.shape[0]
    pos = positions.astype(F32).reshape(t_len, 1)
    inv = 1.0 / (ROPE_THETA ** (jnp.arange(0, ROPE_DIM, 2, dtype=F32) / ROPE_DIM))
    invf = jnp.concatenate([inv, inv, jnp.zeros((128 - ROPE_DIM,), F32)]).reshape(1, 128)

    def slot(w):
        rows, cols = w.shape
        own = w.astype(BF16).reshape(1, 2, rows // 2, cols)
        return lax.dynamic_update_slice(jnp.zeros((N_CHIPS, 2, rows // 2, cols), BF16), own, (chip, 0, 0, 0))

    def slot8(a):
        return lax.dynamic_update_slice(jnp.zeros((N_DEV,) + a.shape, a.dtype), a[None], (me, 0, 0))

    def whole(s):
        return s.reshape(N_CHIPS, 2 * s.shape[2], s.shape[3])

    def halved(g):
        return g.reshape(N_CHIPS, 2, g.shape[1] // 2, g.shape[2])

    ada_cols = w_ada.shape[2]
    c_all, *early = _run_exchange(
        _merge(_gather_all(slot8(jnp.broadcast_to(c, (8, D_MODEL)))),
               _gather_over_ici([slot(jnp.transpose(w_in[0])), slot(w_q_up[0]), slot(w_kv_up[0])])),
        "gather_c_and_mixer_weights_ici")
    b_shard = lax.dynamic_slice(b_ada, (0, chip * ada_cols), (1, ada_cols))
    mod_cols, cond16 = _ada_project(c_all[:, 0, :], w_ada[0], b_shard)
    mod_all, *early = _run_exchange(_merge(_gather_all(slot8(mod_cols)), _gather_over_d2d(early)),
                                    "gather_mod_and_mixer_weights_d2d")
    mod_mine = lax.dynamic_slice(mod_all, (0, me, 0), (N_DEV, 1, ada_cols))[::2, 0, :].reshape(6, D_MODEL)
    sh_a, sc_a, g_a, sh_m, sc_m, g_m = (mod_mine[i:i + 1] for i in range(6))
    g_in, g_q, g_kv = (whole(s) for s in early)
    w_in_full = jnp.pad(g_in.reshape(IN_COLS, D_MODEL), ((0, IN_COLS_PAD - IN_COLS), (0, 0)))
    w_q_full = jnp.pad(g_q, ((0, 0), (0, 0), (0, QK_DIM - g_q.shape[2])))

    w1_rows = D_MODEL // 2
    (u_a, zhg, cq, ckv, q, k, k_t, v, v_t), (s_top,) = _in_project(
        xs, pos, sc_a, sh_a, w_in_full, mla_q_norm_w, mla_kv_norm_w, w_q_full, g_kv, invf,
        _gather_over_ici([slot(w_mlp_in[0, :w1_rows])]))
    (o_pre, o_hg, states), (s_out, s_bottom, s_top) = _hgrn_forward(
        zhg, hg_lower_bounds, hg_norm_w,
        _merge(_gather_over_ici([slot(w_out[0]), slot(w_mlp_in[0, w1_rows:])]), _gather_over_d2d([s_top])))
    (o_mla, lse), (s_w2, s_out, s_bottom) = _attention_forward(
        q, k, v_t, _merge(_gather_over_ici([slot(w_mlp_out[0])]), _gather_over_d2d([s_out, s_bottom])))
    w_out_full = whole(s_out).reshape(D_MODEL, D_MODEL)
    (cat, mix, xhat1, rstd1), (s_w2,) = _out_project(o_hg, o_mla, xs, g_a, w_out_full, _gather_over_d2d([s_w2]))
    g_w1_top, g_w1_bottom, g_w2 = whole(s_top), whole(s_bottom), whole(s_w2)
    vecs = jnp.concatenate([ln1_g, ln1_b, sc_m, sh_m, g_m, g_a, ln2_g, ln2_b], axis=0)
    act, dhp, um, dh, dmix, d_cat, dr1, mlp_sums, delta = _mlp_and_back(
        xhat1, rstd1, mix, target, o_mla, vecs, g_w1_top, g_w1_bottom, g_w2, w_out_full)

    gw_1 = halved(_weight_grad(um, dhp, "grad_w_mlp_in", N_CHIPS, D_FF // N_CHIPS, token_tile=4096))
    gw_2, (landed_1,) = _weight_grad(act, dh, "grad_w_mlp_out", N_CHIPS, D_MODEL, a_blocked=True, b_blocked=False,
                                     token_tile=4096, exchange=_pair_exchange([gw_1]))
    gw_out = _weight_grad(cat, dmix, "grad_w_out", 1, D_MODEL, token_tile=2048)
    later = [halved(gw_2), halved(gw_out.reshape(N_CHIPS, D_MODEL // N_CHIPS, D_MODEL))]
    own_1, travels_1 = _add_pair(core_arr, chip_arr, gw_1, landed_1)
    (dq, dk, dv), (landed_1, *landed) = _attention_backward(
        q, k, k_t, v, d_cat, lse, delta, _merge(_chip_exchange([travels_1]), _pair_exchange(later)))
    mine_1 = _add_chips(own_1, landed_1)
    chip_sums = [_add_pair(core_arr, chip_arr, g, l) for g, l in zip(later, landed)]
    (d_hq, d_hf, d_hi, d_hg, hg_sums), (theirs_1, *landed) = _hgrn_backward(
        zhg, hg_lower_bounds, hg_norm_w, o_pre, d_cat, states,
        _merge(_pair_send([mine_1]), _chip_exchange([b for _, b in chip_sums])))
    later_mine = [_add_chips(own, l) for (own, _), l in zip(chip_sums, landed)]
    mlp_mine = [mine_1] + later_mine
    (dz, dqf, dkvu, cqn, ckvn, grad_x, in_sums), _ = _in_project_backward(
        dq, dk, dv, cq, ckv, pos, invf, mla_q_norm_w, mla_kv_norm_w, w_q_full, g_kv,
        d_hq, d_hf, d_hi, d_hg, w_in_full, dr1, xs, sc_a)

    zeros = lambda n: jnp.zeros((1, n), F32)
    small = jnp.concatenate([
        in_sums[1:2], in_sums[0:1], mlp_sums[S_DGA:S_DGA + 1],
        mlp_sums[S_DSHM:S_DSHM + 1], mlp_sums[S_DSCM:S_DSCM + 1], mlp_sums[S_DGM:S_DGM + 1],
        mlp_sums[S_DLN1G:S_DLN1G + 1], mlp_sums[S_DLN1B:S_DLN1B + 1],
        mlp_sums[S_DLN2G:S_DLN2G + 1], mlp_sums[S_DLN2B:S_DLN2B + 1],
        jnp.concatenate([hg_sums[0:1], hg_sums[1:2]], axis=1),
        jnp.concatenate([in_sums[2:3, :Q_RANK], in_sums[3:4, :KV_RANK], zeros(D_MODEL - Q_RANK - KV_RANK)], axis=1),
        mlp_sums[S_LOSS:S_LOSS + 1],
        jnp.zeros((SMALL_ROWS - 13, D_MODEL), F32)], axis=0)

    gw_in, (*later_theirs, small_all) = _weight_grad(
        dz, u_a, "grad_w_in", 3, D_MODEL, a_blocked=True, b_blocked=False, token_tile=4096,
        exchange=_merge(_pair_send(later_mine), _gather_all(slot8(small))))
    mlp_theirs = [theirs_1] + list(later_theirs)
    gw_in = gw_in.reshape(IN_COLS_PAD, D_MODEL)
    gw_q = _weight_grad(cqn, dqf, "grad_w_q_up", N_HEADS, QK_DIM, token_tile=2048)[:, :, :HEAD_DIM + ROPE_DIM]
    gw_kv = _weight_grad(ckvn, dkvu, "grad_w_kv_up", N_HEADS, 2 * HEAD_DIM, token_tile=2048)
    flat = lambda g: g.reshape(g.shape[0] * g.shape[1], g.shape[2])
    mixer_mine, mixer_theirs = _reduce_in_vmem(
        [gw_in, flat(gw_q), flat(gw_kv)], [IN_COLS // N_CHIPS // 2, Q_RANK // 2, KV_RANK // 2], "reduce_mixer_grads")
    reduced = ("w_in", "w_q_up", "w_kv_up", "w_mlp_in", "w_mlp_out", "w_out")
    halves_mine = dict(zip(reduced, list(mixer_mine) + mlp_mine))
    halves_theirs = dict(zip(reduced, list(mixer_theirs) + list(mlp_theirs)))

    small_names = ("b_ada", "hg_lower_bounds", "hg_norm_w", "mla_q_norm_w", "mla_kv_norm_w",
                   "ln1_g", "ln1_b", "ln2_g", "ln2_b")
    loss_row, small_out = _small_params_step(small_all, {
        "b_ada": (b_ada, m_b_ada, v_b_ada),
        "hg_lower_bounds": (hg_lower_bounds, m_hg_lower_bounds, v_hg_lower_bounds),
        "hg_norm_w": (hg_norm_w, m_hg_norm_w, v_hg_norm_w),
        "mla_q_norm_w": (mla_q_norm_w, m_mla_q_norm_w, v_mla_q_norm_w),
        "mla_kv_norm_w": (mla_kv_norm_w, m_mla_kv_norm_w, v_mla_kv_norm_w),
        "ln1_g": (ln1_g, m_ln1_g, v_ln1_g), "ln1_b": (ln1_b, m_ln1_b, v_ln1_b),
        "ln2_g": (ln2_g, m_ln2_g, v_ln2_g), "ln2_b": (ln2_b, m_ln2_b, v_ln2_b)})
    loss = loss_row[0, 0]

    d_mod_all = small_all[:, 0:6, :].reshape(N_DEV, 6 * D_MODEL)
    d_mod_cols = lax.dynamic_slice(d_mod_all, (0, chip * ada_cols), (N_DEV, ada_cols))
    d_mod_cols = jnp.concatenate([d_mod_cols, jnp.zeros_like(d_mod_cols)], axis=0)
    g_w_ada = _weight_grad(cond16, d_mod_cols, "grad_w_ada", 1, ada_cols)[0]

    names = ["w_ada", "b_ada", "w_in", "hg_lower_bounds", "hg_norm_w", "mla_q_norm_w", "w_q_up", "mla_kv_norm_w",
             "w_kv_up", "w_out", "ln1_g", "ln1_b", "w_mlp_in", "w_mlp_out", "ln2_g", "ln2_b"]
    weights = [w_ada, b_ada, w_in, hg_lower_bounds, hg_norm_w, mla_q_norm_w, w_q_up, mla_kv_norm_w,
               w_kv_up, w_out, ln1_g, ln1_b, w_mlp_in, w_mlp_out, ln2_g, ln2_b]
    moms = [m_w_ada, m_b_ada, m_w_in, m_hg_lower_bounds, m_hg_norm_w, m_mla_q_norm_w, m_w_q_up, m_mla_kv_norm_w,
            m_w_kv_up, m_w_out, m_ln1_g, m_ln1_b, m_w_mlp_in, m_w_mlp_out, m_ln2_g, m_ln2_b]
    vels = [v_w_ada, v_b_ada, v_w_in, v_hg_lower_bounds, v_hg_norm_w, v_mla_q_norm_w, v_w_q_up, v_mla_kv_norm_w,
            v_w_kv_up, v_w_out, v_ln1_g, v_ln1_b, v_w_mlp_in, v_w_mlp_out, v_ln2_g, v_ln2_b]
    out_g, out_d, out_m, out_v = [], [], [], []
    for name, w, m, vv in zip(names, weights, moms, vels):
        if name in small_names:
            g, d, nm, nv = small_out[name]
            back = lambda a: a
        elif name == "w_in":
            to2d, back = (lambda a: jnp.transpose(a[0])), (lambda a: jnp.transpose(a)[None])
        else:
            to2d, back = (lambda a, s=w.shape[1:]: a.reshape(s)), (lambda a, s=w.shape: a.reshape(s))
        if name == "w_ada":
            d, nm, nv = _adamw(to2d(w), g_w_ada, to2d(m), to2d(vv), "adamw_" + name)
            g = g_w_ada
        elif name not in small_names:
            g, d, nm, nv = _adamw_halves(core_arr, to2d(w), halves_mine[name], halves_theirs[name], to2d(m), to2d(vv),
                                         "adamw_" + name)
        out_g.append(back(g))
        out_d.append(back(d))
        out_m.append(back(nm))
        out_v.append(back(nv))
    return (loss, grad_x[None], *out_g, *out_d, *out_m, *out_v)
```

```python
import functools

import jax
import jax.numpy as jnp
from jax import lax
from jax.experimental import pallas as pl
from jax.experimental.pallas import tpu as pltpu

F32 = jnp.float32
BF16 = jnp.bfloat16
MESH_IDS = pl.DeviceIdType.MESH

D_MODEL = 1024
N_HEADS = 4
HEAD_DIM = 128
ROPE_DIM = 64
HG_CHUNK = 64
HG_COLS = 2048
Q_RANK = 256
KV_RANK = 256
IN_COLS = 2624
IN_COLS_PAD = 2688
QK_DIM = 256
D_FF = 4096
N_CHIPS = 4
N_DEV = 8
ROPE_THETA = 10000.0
RMS_EPS = 1e-6
LN_EPS = 1e-5
DN_ALPHA = 2.0 ** 0.25
ATT_SCALE = (HEAD_DIM + ROPE_DIM) ** -0.5
NEG_BIG = -1e30
ADAM_LR = 0.001
ADAM_B1 = 0.9
ADAM_B2 = 0.999
ADAM_EPS = 1e-08
ADAM_WD = 0.01
ADAM_STEP = 10
SMALL_ROWS = 16
MIB = 1024 * 1024


def _dot(a, b):
    return jnp.dot(a, b, preferred_element_type=F32)


def _dot_nt(a, b):
    return lax.dot_general(a, b, (((1,), (1,)), ((), ())), preferred_element_type=F32)


def _dot_tn(a, b):
    return lax.dot_general(a, b, (((0,), (0,)), ((), ())), preferred_element_type=F32)


def _params(vmem_mib, semantics=None):
    return pltpu.CompilerParams(vmem_limit_bytes=vmem_mib * MIB, dimension_semantics=semantics)


def _sigmoid(v):
    return 1.0 / (1.0 + jnp.exp(-v))


def _colsum(v):
    return jnp.sum(v, axis=0, keepdims=True)


def _rowmean(v):
    return jnp.mean(v, axis=-1, keepdims=True)


def _rope_tables(pos, invf):
    ang = pos * invf
    lane = lax.broadcasted_iota(jnp.int32, ang.shape, 1)
    cos_t = jnp.where(lane < ROPE_DIM, jnp.cos(ang), 0.0)
    sin = jnp.sin(ang)
    sin_t = jnp.where(lane < ROPE_DIM // 2, -sin, jnp.where(lane < ROPE_DIM, sin, 0.0))
    return cos_t, sin_t


def _swap_halves(t):
    lane = lax.broadcasted_iota(jnp.int32, t.shape, 1)
    return jnp.where(lane < ROPE_DIM // 2, pltpu.roll(t, 128 - ROPE_DIM // 2, 1), pltpu.roll(t, ROPE_DIM // 2, 1))


def _rope(t, cos_t, sin_t):
    return t * cos_t + _swap_halves(t) * sin_t


def _unrope(g, cos_t, sin_t):
    return g * cos_t - _swap_halves(g) * sin_t


def _mesh_pos():
    return lax.axis_index("x"), lax.axis_index("y"), lax.axis_index("c")


def _other_chips(x, y):
    out = []
    for dx, dy in ((1, 0), (0, 1), (1, 1)):
        px = 1 - x if dx else x
        py = 1 - y if dy else y
        out.append(((px, py), 2 * px + py))
    return out


class _Exchange:
    def __init__(self, inputs, out_shapes, aliases, sems, start, finish):
        self.inputs, self.out_shapes, self.aliases, self.sems = list(inputs), list(out_shapes), dict(aliases), list(sems)
        self.start, self.finish = start, finish


def _from_copies(inputs, out_shapes, aliases, sems, copies):
    def start(ins, outs, sem_refs):
        for send, _ in copies(ins, outs, sem_refs):
            send.start()

    def finish(ins, outs, sem_refs):
        for send, recv in copies(ins, outs, sem_refs):
            recv.wait_recv()
            send.wait_send()

    return _Exchange(inputs, out_shapes, aliases, sems, start, finish)


HBM_MIN_BYTES = 256 * 1024


def _in_hbm(a):
    if a.size * a.dtype.itemsize < HBM_MIN_BYTES:
        return a
    return pltpu.with_memory_space_constraint(a, pltpu.HBM)


def _out_hbm(s):
    if s.size * s.dtype.itemsize < HBM_MIN_BYTES:
        return s
    return pltpu.HBM(s.shape, s.dtype)


def _pcall(body, *, operands, out_shape, **kwargs):
    single = not isinstance(out_shape, (list, tuple))
    shapes = [_out_hbm(s) for s in ([out_shape] if single else out_shape)]
    return pl.pallas_call(body, out_shape=shapes[0] if single else shapes, **kwargs)(*[_in_hbm(a) for a in operands])


def _run_exchange(exchange, name):
    n_in, n_out = len(exchange.inputs), len(exchange.out_shapes)

    def body(*refs):
        ins, outs, sem_refs = refs[:n_in], refs[n_in:n_in + n_out], refs[n_in + n_out:]
        exchange.start(ins, outs, sem_refs)
        exchange.finish(ins, outs, sem_refs)

    any_spec = pl.BlockSpec(memory_space=pl.ANY)
    return pl.pallas_call(
        body, name=name, out_shape=[_out_hbm(s) for s in exchange.out_shapes],
        in_specs=[any_spec] * n_in, out_specs=[any_spec] * n_out,
        scratch_shapes=exchange.sems, input_output_aliases=exchange.aliases,
    )(*[_in_hbm(a) for a in exchange.inputs])


def _pallas(body, *, name, operands, in_specs, out_shape, out_specs, params, scratch_shapes=(), grid=(), prefetch=(),
            exchange=None, first=None, last=None):
    n_pre, n_in, n_out, n_scr = len(prefetch), len(in_specs), len(out_specs), len(scratch_shapes)
    ex_in = exchange.inputs if exchange else []
    ex_out = exchange.out_shapes if exchange else []
    ex_sems = exchange.sems if exchange else []

    def full_body(*refs):
        pre, rest = refs[:n_pre], refs[n_pre:]
        ins, rest = rest[:n_in], rest[n_in:]
        xin, rest = rest[:len(ex_in)], rest[len(ex_in):]
        outs, rest = rest[:n_out], rest[n_out:]
        xout, rest = rest[:len(ex_out)], rest[len(ex_out):]
        scr, sem_refs = rest[:n_scr], rest[n_scr:]
        if exchange:
            @pl.when(first(*pre))
            def _():
                exchange.start(xin, xout, sem_refs)

        body(*pre, *ins, *outs, *scr)
        if exchange:
            @pl.when(last(*pre))
            def _():
                exchange.finish(xin, xout, sem_refs)

    any_spec = pl.BlockSpec(memory_space=pl.ANY)
    aliases = {n_pre + n_in + i: n_out + o for i, o in exchange.aliases.items()} if exchange else {}
    operands = [_in_hbm(a) for a in operands]
    results = pl.pallas_call(
        full_body, name=name, out_shape=[_out_hbm(s) for s in list(out_shape) + ex_out],
        grid_spec=pltpu.PrefetchScalarGridSpec(
            num_scalar_prefetch=n_pre, grid=grid, in_specs=list(in_specs) + [any_spec] * len(ex_in),
            out_specs=list(out_specs) + [any_spec] * len(ex_out), scratch_shapes=list(scratch_shapes) + ex_sems),
        input_output_aliases=aliases, compiler_params=params,
    )(*prefetch, *operands, *[_in_hbm(a) for a in ex_in])
    return results[:n_out], results[n_out:]


def _remote(src, dst, sems, idx, to):
    send_sems, recv_sems = sems
    return pltpu.make_async_remote_copy(src_ref=src, dst_ref=dst, send_sem=send_sems.at[idx], recv_sem=recv_sems.at[idx],
                                        device_id=to, device_id_type=MESH_IDS)


def _sem_pairs(*shape):
    return [pltpu.SemaphoreType.DMA(shape), pltpu.SemaphoreType.DMA(shape)]


def _same_shapes(arrays):
    return [jax.ShapeDtypeStruct(a.shape, a.dtype) for a in arrays]


def _gather_over_ici(slots):
    n = len(slots)

    def copies(ins, outs, sems):
        x, y, c = _mesh_pos()
        k = 2 * x + y
        out = []
        for j, (chip, kj) in enumerate(_other_chips(x, y)):
            for i in range(n):
                to = (*chip, c)
                out.append((_remote(ins[i].at[k, c], outs[i].at[k, c], sems, (j, i), to),
                            _remote(ins[i].at[k, c], outs[i].at[kj, c], sems, (j, i), to)))
        return out

    return _from_copies(slots, _same_shapes(slots), {i: i for i in range(n)}, _sem_pairs(3, n), copies)


def _gather_over_d2d(slots):
    n = len(slots)

    def copies(ins, outs, sems):
        x, y, c = _mesh_pos()
        sibling = (x, y, 1 - c)
        out = []
        for j, (_, kj) in enumerate(_other_chips(x, y)):
            for i in range(n):
                out.append((_remote(ins[i].at[kj, c], outs[i].at[kj, c], sems, (j, i), sibling),
                            _remote(ins[i].at[kj, c], outs[i].at[kj, 1 - c], sems, (j, i), sibling)))
        return out

    return _from_copies(slots, _same_shapes(slots), {i: i for i in range(n)}, _sem_pairs(3, n), copies)


def _gather_all(slots8):
    def copies(ins, outs, sems):
        x, y, c = _mesh_pos()
        me = 4 * x + 2 * y + c
        out = []
        for r in range(1, N_DEV):
            px = 1 - x if r & 4 else x
            py = 1 - y if r & 2 else y
            pc = 1 - c if r & 1 else c
            to = (px, py, pc)
            out.append((_remote(ins[0].at[me], outs[0].at[me], sems, r - 1, to),
                        _remote(ins[0].at[me], outs[0].at[4 * px + 2 * py + pc], sems, r - 1, to)))
        return out

    return _from_copies([slots8], _same_shapes([slots8]), {0: 0}, _sem_pairs(N_DEV - 1), copies)


def _merge(first, second):
    n_in, n_out, n_sem = len(first.inputs), len(first.out_shapes), len(first.sems)

    def start(ins, outs, sems):
        first.start(ins[:n_in], outs[:n_out], sems[:n_sem])
        second.start(ins[n_in:], outs[n_out:], sems[n_sem:])

    def finish(ins, outs, sems):
        first.finish(ins[:n_in], outs[:n_out], sems[:n_sem])
        second.finish(ins[n_in:], outs[n_out:], sems[n_sem:])

    aliases = dict(first.aliases)
    aliases.update({n_in + i: n_out + o for i, o in second.aliases.items()})
    return _Exchange(first.inputs + second.inputs, first.out_shapes + second.out_shapes, aliases,
                     first.sems + second.sems, start, finish)


def _pair_exchange(grads):
    n = len(grads)

    def copies(ins, outs, sems):
        x, y, c = _mesh_pos()
        cps = [_remote(ins[i].at[:, 1 - c], outs[i], sems, i, (x, y, 1 - c)) for i in range(n)]
        return [(cp, cp) for cp in cps]

    shapes = [jax.ShapeDtypeStruct((N_CHIPS,) + g.shape[2:], g.dtype) for g in grads]
    return _from_copies(grads, shapes, {}, _sem_pairs(n), copies)


def _chip_exchange(partials):
    n = len(partials)

    def copies(ins, outs, sems):
        x, y, c = _mesh_pos()
        cps = [_remote(ins[i].at[kj], outs[i].at[j], sems, (j, i), (*chip, c))
               for j, (chip, kj) in enumerate(_other_chips(x, y)) for i in range(n)]
        return [(cp, cp) for cp in cps]

    shapes = [jax.ShapeDtypeStruct((3,) + p.shape[1:], p.dtype) for p in partials]
    return _from_copies(partials, shapes, {}, _sem_pairs(3, n), copies)


def _pair_send(halves):
    n = len(halves)

    def copies(ins, outs, sems):
        x, y, c = _mesh_pos()
        cps = [_remote(ins[i], outs[i], sems, i, (x, y, 1 - c)) for i in range(n)]
        return [(cp, cp) for cp in cps]

    return _from_copies(halves, _same_shapes(halves), {}, _sem_pairs(n), copies)


def _reduce_in_vmem(grads, half_rows, name):
    n = len(grads)

    def body(*refs):
        g, mine, theirs = refs[:n], refs[n:2 * n], refs[2 * n:3 * n]
        landed_pair, partial, landed_chips = refs[3 * n:4 * n], refs[4 * n:5 * n], refs[5 * n:6 * n]
        sems = refs[6 * n:]
        x, y, c = _mesh_pos()
        k = 2 * x + y
        sibling = (x, y, 1 - c)

        def half(i, chip_idx, which):
            return pl.ds(pl.multiple_of((2 * chip_idx + which) * half_rows[i], 8), half_rows[i])

        def run(copies):
            for cp in copies:
                cp.start()
            for cp in copies:
                cp.wait_recv()
                cp.wait_send()

        run([_remote(g[i].at[half(i, kk, 1 - c)], landed_pair[i].at[kk], sems[0:2], (kk, i), sibling)
             for kk in range(N_CHIPS) for i in range(n)])
        for i in range(n):
            for kk in range(N_CHIPS):
                partial[i][kk] = (g[i][half(i, kk, c), :] + landed_pair[i][kk]).astype(BF16)
        run([_remote(partial[i].at[kj], landed_chips[i].at[j], sems[2:4], (j, i), (*chip, c))
             for j, (chip, kj) in enumerate(_other_chips(x, y)) for i in range(n)])
        for i in range(n):
            own = g[i][half(i, k, c), :] + landed_pair[i][k]
            mine[i][...] = ((own + landed_chips[i][0].astype(F32)) + landed_chips[i][1].astype(F32)) \
                + landed_chips[i][2].astype(F32)
        run([_remote(mine[i], theirs[i], sems[4:6], i, sibling) for i in range(n)])

    shapes = [(h, gr.shape[1]) for gr, h in zip(grads, half_rows)]
    halves = [jax.ShapeDtypeStruct(s, F32) for s in shapes]
    vmem = pl.BlockSpec(memory_space=pltpu.VMEM)
    scratch = ([pltpu.VMEM((N_CHIPS,) + s, F32) for s in shapes]
               + [pltpu.VMEM((N_CHIPS,) + s, BF16) for s in shapes]
               + [pltpu.VMEM((3,) + s, BF16) for s in shapes]
               + _sem_pairs(N_CHIPS, n) + _sem_pairs(3, n) + _sem_pairs(n))
    out = pl.pallas_call(
        body, name=name, out_shape=halves + halves, in_specs=[vmem] * n, out_specs=[vmem] * (2 * n),
        scratch_shapes=scratch, compiler_params=_params(48),
    )(*grads)
    return out[:n], out[n:]


def _row_tile(rows):
    for t in (256, 128, 64):
        if rows % t == 0:
            return t
    return rows


def _add_pair(core, chip, grad, landed):
    _, h, cols = landed.shape
    tr = _row_tile(h)

    def body(core_ref, chip_ref, g_ref, l_ref, own_ref, ob_ref):
        s = g_ref[...] + l_ref[...]
        ob_ref[...] = s.astype(BF16)

        @pl.when(pl.program_id(1) == chip_ref[0])
        def _():
            own_ref[...] = s

    return _pcall(
        body, name="grad_add_pair",
        out_shape=[jax.ShapeDtypeStruct((h, cols), F32), jax.ShapeDtypeStruct(landed.shape, BF16)],
        grid_spec=pltpu.PrefetchScalarGridSpec(
            num_scalar_prefetch=2, grid=(h // tr, N_CHIPS),
            in_specs=[pl.BlockSpec((None, None, tr, cols), lambda t, k, core_ref, chip_ref: (k, core_ref[0], t, 0)),
                      pl.BlockSpec((None, tr, cols), lambda t, k, core_ref, chip_ref: (k, t, 0))],
            out_specs=[pl.BlockSpec((tr, cols), lambda t, k, core_ref, chip_ref: (t, 0)),
                       pl.BlockSpec((None, tr, cols), lambda t, k, core_ref, chip_ref: (k, t, 0))]),
        compiler_params=_params(32, ("arbitrary", "arbitrary")),
        operands=(core, chip, grad, landed))


def _add_chips(own, landed):
    h, cols = own.shape
    tr = _row_tile(h)

    def body(p_ref, l_ref, o_ref):
        o_ref[...] = ((p_ref[...] + l_ref[0].astype(F32)) + l_ref[1].astype(F32)) + l_ref[2].astype(F32)

    return _pcall(
        body, name="grad_add_chips", grid=(h // tr,),
        out_shape=jax.ShapeDtypeStruct((h, cols), F32),
        in_specs=[pl.BlockSpec((tr, cols), lambda t: (t, 0)), pl.BlockSpec((3, tr, cols), lambda t: (0, t, 0))],
        out_specs=pl.BlockSpec((tr, cols), lambda t: (t, 0)),
        compiler_params=_params(32, ("arbitrary",)),
        operands=(own, landed))


def _ada_project(c_all, w_ada, b_shard):
    n = w_ada.shape[1]
    tn = 512

    def body(c_ref, w_ref, b_ref, mod_ref, cond_ref):
        cv = c_ref[...]
        cond = cv * _sigmoid(cv)
        mod_ref[...] = _dot(cond.astype(BF16), w_ref[...].astype(BF16)) + b_ref[...]
        cond_ref[0:N_DEV, :] = cond
        cond_ref[N_DEV:2 * N_DEV, :] = jnp.zeros_like(cond)

    return _pcall(
        body, name="ada_project", grid=(n // tn,),
        out_shape=[jax.ShapeDtypeStruct((N_DEV, n), F32), jax.ShapeDtypeStruct((2 * N_DEV, D_MODEL), F32)],
        in_specs=[pl.BlockSpec((N_DEV, D_MODEL), lambda j: (0, 0)), pl.BlockSpec((D_MODEL, tn), lambda j: (0, j)),
                  pl.BlockSpec((1, tn), lambda j: (0, j))],
        out_specs=[pl.BlockSpec((N_DEV, tn), lambda j: (0, j)), pl.BlockSpec((2 * N_DEV, D_MODEL), lambda j: (0, 0))],
        compiler_params=_params(32, ("arbitrary",)),
        operands=(c_all, w_ada, b_shard))


def _in_project(x, pos, sc_a, sh_a, w_in, q_norm_w, kv_norm_w, w_q, w_kv, invf, exchange=None):
    t_len = x.shape[0]
    tm = min(512, t_len)

    def body(x_ref, pos_ref, sc_ref, sh_ref, win_ref, qn_ref, kvn_ref, wq_ref, wkv_ref, invf_ref,
             u_ref, zhg_ref, cq_ref, ckv_ref, q_ref, k_ref, kt_ref, v_ref, vt_ref):
        u = (x_ref[...] * (1.0 + sc_ref[...]) + sh_ref[...]).astype(BF16)
        u_ref[...] = u
        z = _dot_nt(u, win_ref[...])
        zhg_ref[...] = z[:, :HG_COLS]
        cq = z[:, HG_COLS:HG_COLS + Q_RANK]
        ckv = z[:, HG_COLS + Q_RANK:HG_COLS + Q_RANK + KV_RANK]
        cq_ref[...] = cq
        ckv_ref[...] = ckv
        cos_t, sin_t = _rope_tables(pos_ref[...], invf_ref[...])
        k_pe = _rope(z[:, HG_COLS + Q_RANK + KV_RANK:], cos_t, sin_t)
        k_pe_t = jnp.transpose(k_pe).astype(BF16)
        cqn = (cq * lax.rsqrt(_rowmean(cq * cq) + RMS_EPS) * qn_ref[...]).astype(BF16)
        ckvn = (ckv * lax.rsqrt(_rowmean(ckv * ckv) + RMS_EPS) * kvn_ref[...]).astype(BF16)
        for h in range(N_HEADS):
            qh = _dot(cqn, wq_ref[h])
            q_ref[h, :, 0:HEAD_DIM] = qh[:, :HEAD_DIM].astype(BF16)
            q_ref[h, :, HEAD_DIM:QK_DIM] = _rope(qh[:, HEAD_DIM:], cos_t, sin_t).astype(BF16)
            kvh = _dot(ckvn, wkv_ref[h])
            k_ref[h, :, 0:HEAD_DIM] = kvh[:, :HEAD_DIM].astype(BF16)
            k_ref[h, :, HEAD_DIM:QK_DIM] = k_pe.astype(BF16)
            kt_ref[h, 0:HEAD_DIM, :] = jnp.transpose(kvh[:, :HEAD_DIM]).astype(BF16)
            kt_ref[h, HEAD_DIM:QK_DIM, :] = k_pe_t
            v_ref[h] = kvh[:, HEAD_DIM:].astype(BF16)
            vt_ref[h] = jnp.transpose(kvh[:, HEAD_DIM:]).astype(BF16)

    row = lambda i: (i, 0)
    fixed2 = lambda i: (0, 0)
    fixed3 = lambda i: (0, 0, 0)
    heads = lambda i: (0, i, 0)
    n_tiles = t_len // tm
    return _pallas(
        body, name="in_project", grid=(n_tiles,),
        operands=(x, pos, sc_a, sh_a, w_in, q_norm_w, kv_norm_w, w_q, w_kv, invf),
        out_shape=[jax.ShapeDtypeStruct((t_len, D_MODEL), BF16), jax.ShapeDtypeStruct((t_len, HG_COLS), F32),
                   jax.ShapeDtypeStruct((t_len, Q_RANK), F32), jax.ShapeDtypeStruct((t_len, KV_RANK), F32),
                   jax.ShapeDtypeStruct((N_HEADS, t_len, QK_DIM), BF16),
                   jax.ShapeDtypeStruct((N_HEADS, t_len, QK_DIM), BF16),
                   jax.ShapeDtypeStruct((N_HEADS, QK_DIM, t_len), BF16),
                   jax.ShapeDtypeStruct((N_HEADS, t_len, HEAD_DIM), BF16),
                   jax.ShapeDtypeStruct((N_HEADS, HEAD_DIM, t_len), BF16)],
        in_specs=[pl.BlockSpec((tm, D_MODEL), row), pl.BlockSpec((tm, 1), row),
                  pl.BlockSpec((1, D_MODEL), fixed2), pl.BlockSpec((1, D_MODEL), fixed2),
                  pl.BlockSpec((IN_COLS_PAD, D_MODEL), fixed2),
                  pl.BlockSpec((1, Q_RANK), fixed2), pl.BlockSpec((1, KV_RANK), fixed2),
                  pl.BlockSpec((N_HEADS, Q_RANK, QK_DIM), fixed3), pl.BlockSpec((N_HEADS, KV_RANK, 2 * HEAD_DIM), fixed3),
                  pl.BlockSpec((1, 128), fixed2)],
        out_specs=[pl.BlockSpec((tm, D_MODEL), row), pl.BlockSpec((tm, HG_COLS), row),
                   pl.BlockSpec((tm, Q_RANK), row), pl.BlockSpec((tm, KV_RANK), row),
                   pl.BlockSpec((N_HEADS, tm, QK_DIM), heads), pl.BlockSpec((N_HEADS, tm, QK_DIM), heads),
                   pl.BlockSpec((N_HEADS, QK_DIM, tm), lambda i: (0, 0, i)),
                   pl.BlockSpec((N_HEADS, tm, HEAD_DIM), heads),
                   pl.BlockSpec((N_HEADS, HEAD_DIM, tm), lambda i: (0, 0, i))],
        params=_params(48, ("arbitrary",)), exchange=exchange,
        first=lambda: pl.program_id(0) == 0, last=lambda: pl.program_id(0) == n_tiles - 1)


def _lower_bound(lb_raw):
    m = jnp.max(lb_raw, axis=0, keepdims=True)
    e = jnp.exp(lb_raw - m)
    return e[0:1] / jnp.sum(e, axis=0, keepdims=True)


def _tri(inclusive_lower):
    r = lax.broadcasted_iota(jnp.int32, (HG_CHUNK, HG_CHUNK), 0)
    c = lax.broadcasted_iota(jnp.int32, (HG_CHUNK, HG_CHUNK), 1)
    return (c <= r) if inclusive_lower else (c >= r)


def _chunk_rows(n):
    return slice(n * HG_CHUNK, (n + 1) * HG_CHUNK)


def _chunk_prefix_sums(v, inclusive_lower):
    tri = _tri(inclusive_lower).astype(BF16)
    hi = v.astype(BF16)
    rest = v - hi.astype(F32)
    mid = rest.astype(BF16)
    lo = (rest - mid.astype(F32)).astype(BF16)
    pieces = jnp.concatenate([hi, mid, lo], axis=1)
    out = []
    for n in range(v.shape[0] // HG_CHUNK):
        s = _dot(tri, pieces[_chunk_rows(n)])
        out.append((s[:, 0:HEAD_DIM] + s[:, HEAD_DIM:2 * HEAD_DIM]) + s[:, 2 * HEAD_DIM:])
    return jnp.concatenate(out, axis=0)


def _per_chunk(v, row):
    n = v.shape[0] // HG_CHUNK
    v3 = v.reshape(n, HG_CHUNK, HEAD_DIM)
    return jnp.broadcast_to(v3[:, row:row + 1, :], v3.shape).reshape(v.shape)


def _hg_block(q, f_logit, lb):
    sg = _sigmoid(f_logit)
    forget = lb + (1.0 - lb) * sg
    kk = 1.0 - forget
    b = _chunk_prefix_sums(jnp.log(forget), True)
    b_ref = _per_chunk(b, HG_CHUNK // 2 - 1)
    b_last = _per_chunk(b, HG_CHUNK - 1)
    e_i = jnp.exp(b - b_ref)
    e_ri = jnp.exp(b_ref - b)
    e_b = jnp.exp(b)
    e_l = jnp.exp(b_last - b)
    return dict(sg=sg, forget=forget, e_i=e_i, e_ri=e_ri, e_b=e_b, e_l=e_l, dec=jnp.exp(b_last),
                qi=q * e_i, ki=kk * e_ri, qe=q * e_b, kl=kk * e_l)


HG_STEP_HEADS = 4


def _head_cols(hh):
    return slice(hh * HEAD_DIM, (hh + 1) * HEAD_DIM)


def _hgrn_forward(zhg, lb_raw, norm_w, exchange=None):
    t_len = zhg.shape[0]
    tb = min(512, t_len)
    n_chunks = tb // HG_CHUNK
    hs = HG_STEP_HEADS

    def body(q_ref, f_ref, v_ref, g_ref, lb_ref, w_ref, opre_ref, o_ref, st_ref, state):
        @pl.when(pl.program_id(1) == 0)
        def _():
            state[...] = jnp.zeros_like(state)

        causal = _tri(True)
        for hh in range(hs):
            cols = _head_cols(hh)
            blk = _hg_block(q_ref[:, cols], f_ref[:, cols], _lower_bound(lb_ref[:, cols]))
            v = v_ref[:, cols].astype(BF16)
            qi, ki, qe, kl = (blk[name].astype(BF16) for name in ("qi", "ki", "qe", "kl"))
            st = state[hh]
            parts = []
            for n in range(n_chunks):
                r = _chunk_rows(n)
                a = jnp.where(causal, _dot_nt(qi[r], ki[r]), 0.0).astype(BF16)
                st_ref[hh, n] = st
                parts.append(_dot(a, v[r]) + _dot_nt(qe[r], st.astype(BF16)))
                st = st * blk["dec"][n * HG_CHUNK:n * HG_CHUNK + 1] + _dot_tn(v[r], kl[r])
            state[hh] = st
            o = jnp.concatenate(parts, axis=0)
            opre_ref[:, cols] = o
            g = g_ref[:, cols]
            o_ref[:, cols] = o * lax.rsqrt(_rowmean(o * o) + RMS_EPS) * w_ref[:, cols] * (g * _sigmoid(g))

    groups = N_HEADS // hs
    wide = hs * HEAD_DIM
    col = lambda off: (lambda h, t: (t, off + h))
    nb = t_len // tb
    return _pallas(
        body, name="hgrn_forward", grid=(groups, nb), operands=(zhg, zhg, zhg, zhg, lb_raw, norm_w),
        out_shape=[jax.ShapeDtypeStruct((t_len, N_HEADS * HEAD_DIM), F32),
                   jax.ShapeDtypeStruct((t_len, N_HEADS * HEAD_DIM), F32),
                   jax.ShapeDtypeStruct((N_HEADS, t_len // HG_CHUNK, HEAD_DIM, HEAD_DIM), F32)],
        in_specs=[pl.BlockSpec((tb, wide), col(0)), pl.BlockSpec((tb, wide), col(groups)),
                  pl.BlockSpec((tb, wide), col(2 * groups)), pl.BlockSpec((tb, wide), col(3 * groups)),
                  pl.BlockSpec((2, wide), lambda h, t: (0, h)), pl.BlockSpec((1, wide), lambda h, t: (0, h))],
        out_specs=[pl.BlockSpec((tb, wide), col(0)), pl.BlockSpec((tb, wide), col(0)),
                   pl.BlockSpec((hs, n_chunks, HEAD_DIM, HEAD_DIM), lambda h, t: (h, t, 0, 0))],
        scratch_shapes=[pltpu.VMEM((hs, HEAD_DIM, HEAD_DIM), F32)],
        params=_params(40, ("arbitrary", "arbitrary")), exchange=exchange,
        first=lambda: (pl.program_id(0) == 0) & (pl.program_id(1) == 0),
        last=lambda: (pl.program_id(0) == groups - 1) & (pl.program_id(1) == nb - 1))


def _hgrn_backward(zhg, lb_raw, norm_w, o_pre, d_cat, states, exchange=None):
    t_len = zhg.shape[0]
    tb = min(512, t_len)
    n_chunks = tb // HG_CHUNK
    nb = t_len // tb
    hs = HG_STEP_HEADS

    def head(hh, q_ref, f_ref, v_ref, g_ref, lb_ref, w_ref, opre_ref, do_ref, st_ref,
             dq_ref, df_ref, dv_ref, dg_ref, sums_ref, gstate):
        cols = _head_cols(hh)
        lb = _lower_bound(lb_ref[:, cols])
        w = w_ref[:, cols]
        o = opre_ref[:, cols]
        g = g_ref[:, cols]
        d_out = do_ref[:, cols]
        r = lax.rsqrt(_rowmean(o * o) + RMS_EPS)
        sg_g = _sigmoid(g)
        dg_ref[:, cols] = (d_out * (o * r * w) * (sg_g * (1.0 + g * (1.0 - sg_g)))).astype(BF16)
        d_on = d_out * (g * sg_g)
        sums_ref[1:2, cols] += _colsum(d_on * o * r)
        dy = d_on * w
        d_o = (r * dy - o * (r * r * r) * _rowmean(dy * o)).astype(BF16)
        blk = _hg_block(q_ref[:, cols], f_ref[:, cols], lb)
        v = v_ref[:, cols].astype(BF16)
        qi, ki, qe, kl = (blk[name].astype(BF16) for name in ("qi", "ki", "qe", "kl"))
        causal = _tri(True)
        row_id = lax.broadcasted_iota(jnp.int32, (HG_CHUNK, HEAD_DIM), 0)
        gt = gstate[hh]
        d_v, d_qi, d_ki, d_qe, d_kl, d_dec = ([None] * n_chunks for _ in range(6))
        for n in reversed(range(n_chunks)):
            rows = _chunk_rows(n)
            st = st_ref[hh, n]
            a = jnp.where(causal, _dot_nt(qi[rows], ki[rows]), 0.0).astype(BF16)
            d_a = jnp.where(causal, _dot_nt(d_o[rows], v[rows]), 0.0).astype(BF16)
            gt_b = gt.astype(BF16)
            d_v[n] = _dot_tn(a, d_o[rows]) + _dot_nt(kl[rows], gt_b)
            d_qi[n] = _dot(d_a, ki[rows])
            d_ki[n] = _dot_tn(d_a, qi[rows])
            d_qe[n] = _dot(d_o[rows], st.astype(BF16))
            d_kl[n] = _dot(v[rows], gt_b)
            d_dec[n] = jnp.where(row_id == HG_CHUNK - 1, _colsum(gt * st), 0.0)
            gt = gt * blk["dec"][n * HG_CHUNK:n * HG_CHUNK + 1] + _dot_tn(d_o[rows], qe[rows])
        gstate[hh] = gt
        d_qi, d_ki, d_qe, d_kl, d_dec = (jnp.concatenate(p, axis=0) for p in (d_qi, d_ki, d_qe, d_kl, d_dec))
        dv_ref[:, cols] = jnp.concatenate(d_v, axis=0).astype(BF16)
        dq_ref[:, cols] = (d_qi * blk["e_i"] + d_qe * blk["e_b"]).astype(BF16)
        d_k = d_ki * blk["e_ri"] + d_kl * blk["e_l"]
        t_qi = d_qi * blk["qi"]
        t_ki = d_ki * blk["ki"]
        t_kl = d_kl * blk["kl"]
        at_ref, at_last = [], []
        for n in range(n_chunks):
            rows = _chunk_rows(n)
            at_ref.append(jnp.where(row_id == HG_CHUNK // 2 - 1, _colsum(t_ki[rows] - t_qi[rows]), 0.0))
            at_last.append(jnp.where(row_id == HG_CHUNK - 1, _colsum(t_kl[rows]), 0.0))
        d_b = (t_qi - t_ki + d_qe * blk["qe"] - t_kl + jnp.concatenate(at_ref, axis=0)
               + jnp.concatenate(at_last, axis=0) + d_dec * blk["dec"])
        d_forget = _chunk_prefix_sums(d_b, False) / blk["forget"] - d_k
        sg = blk["sg"]
        df_ref[:, cols] = (d_forget * (1.0 - lb) * sg * (1.0 - sg)).astype(BF16)
        sums_ref[0:1, cols] += _colsum(d_forget * (1.0 - sg))

    def body(*refs):
        sums_ref, gstate = refs[-2], refs[-1]

        @pl.when(pl.program_id(1) == 0)
        def _():
            gstate[...] = jnp.zeros_like(gstate)
            sums_ref[...] = jnp.zeros_like(sums_ref)

        for hh in range(hs):
            head(hh, *refs)

    groups = N_HEADS // hs
    wide = hs * HEAD_DIM
    col = lambda off: (lambda h, t: (nb - 1 - t, off + h))
    return _pallas(
        body, name="hgrn_backward", grid=(groups, nb),
        operands=(zhg, zhg, zhg, zhg, lb_raw, norm_w, o_pre, d_cat, states),
        out_shape=[jax.ShapeDtypeStruct((t_len, N_HEADS * HEAD_DIM), BF16)] * 4
        + [jax.ShapeDtypeStruct((8, N_HEADS * HEAD_DIM), F32)],
        in_specs=[pl.BlockSpec((tb, wide), col(0)), pl.BlockSpec((tb, wide), col(groups)),
                  pl.BlockSpec((tb, wide), col(2 * groups)), pl.BlockSpec((tb, wide), col(3 * groups)),
                  pl.BlockSpec((2, wide), lambda h, t: (0, h)), pl.BlockSpec((1, wide), lambda h, t: (0, h)),
                  pl.BlockSpec((tb, wide), col(0)), pl.BlockSpec((tb, wide), col(0)),
                  pl.BlockSpec((hs, n_chunks, HEAD_DIM, HEAD_DIM), lambda h, t: (h, nb - 1 - t, 0, 0))],
        out_specs=[pl.BlockSpec((tb, wide), col(0))] * 4 + [pl.BlockSpec((8, wide), lambda h, t: (0, h))],
        scratch_shapes=[pltpu.VMEM((hs, HEAD_DIM, HEAD_DIM), F32)],
        params=_params(40, ("arbitrary", "arbitrary")), exchange=exchange,
        first=lambda: (pl.program_id(0) == 0) & (pl.program_id(1) == 0),
        last=lambda: (pl.program_id(0) == groups - 1) & (pl.program_id(1) == nb - 1))


ATT_LOG2 = ATT_SCALE * 1.4426950408889634


def _triangle_steps(nq, q_major):
    if q_major:
        pairs = [(i, j) for i in range(nq) for j in range(i + 1)]
    else:
        pairs = [(i, j) for j in range(nq) for i in range(j, nq)]
    return jnp.array([p[0] for p in pairs], jnp.int32), jnp.array([p[1] for p in pairs], jnp.int32)


def _key_le_query(t):
    return lax.broadcasted_iota(jnp.int32, (t, t), 0) <= lax.broadcasted_iota(jnp.int32, (t, t), 1)


def _attention_forward(q, k, v_t, exchange=None):
    t_len = q.shape[1]
    tq = min(512, t_len)
    nq = t_len // tq
    qi_tab, ki_tab = _triangle_steps(nq, True)

    def body(qi_ref, ki_ref, q_ref, k_ref, vt_ref, o_ref, lse_ref, m_s, l_s, acc_s):
        step = pl.program_id(0)
        qi, ki = qi_ref[step], ki_ref[step]

        @pl.when(ki == 0)
        def _():
            m_s[...] = jnp.full_like(m_s, NEG_BIG)
            l_s[...] = jnp.zeros_like(l_s)
            acc_s[...] = jnp.zeros_like(acc_s)

        def accumulate(masked):
            for h in range(N_HEADS):
                s_t = _dot_nt(k_ref[h], q_ref[h]) * ATT_LOG2
                if masked:
                    s_t = jnp.where(_key_le_query(tq), s_t, NEG_BIG)
                m_old = m_s[h]
                m_new = jnp.maximum(m_old, jnp.max(s_t, axis=0, keepdims=True))
                alpha = jnp.exp2(m_old - m_new)
                p_t = jnp.exp2(s_t - m_new)
                l_s[h] = alpha * l_s[h] + jnp.sum(p_t, axis=0, keepdims=True)
                acc_s[h] = alpha * acc_s[h] + _dot(vt_ref[h], p_t.astype(BF16))
                m_s[h] = m_new

        @pl.when(ki < qi)
        def _():
            accumulate(False)

        @pl.when(ki == qi)
        def _():
            accumulate(True)
            for h in range(N_HEADS):
                o_ref[:, h * HEAD_DIM:(h + 1) * HEAD_DIM] = jnp.transpose(acc_s[h] / l_s[h])
                lse_ref[h] = m_s[h] + jnp.log2(l_s[h])

    n_steps = qi_tab.shape[0]
    return _pallas(
        body, name="attention_forward", grid=(n_steps,), prefetch=(qi_tab, ki_tab), operands=(q, k, v_t),
        out_shape=[jax.ShapeDtypeStruct((t_len, N_HEADS * HEAD_DIM), F32),
                   jax.ShapeDtypeStruct((N_HEADS, 1, t_len), F32)],
        in_specs=[pl.BlockSpec((N_HEADS, tq, QK_DIM), lambda s, qt, kt: (0, qt[s], 0)),
                  pl.BlockSpec((N_HEADS, tq, QK_DIM), lambda s, qt, kt: (0, kt[s], 0)),
                  pl.BlockSpec((N_HEADS, HEAD_DIM, tq), lambda s, qt, kt: (0, 0, kt[s]))],
        out_specs=[pl.BlockSpec((tq, N_HEADS * HEAD_DIM), lambda s, qt, kt: (qt[s], 0)),
                   pl.BlockSpec((N_HEADS, 1, tq), lambda s, qt, kt: (0, 0, qt[s]))],
        scratch_shapes=[pltpu.VMEM((N_HEADS, 1, tq), F32), pltpu.VMEM((N_HEADS, 1, tq), F32),
                        pltpu.VMEM((N_HEADS, HEAD_DIM, tq), F32)],
        params=_params(48, ("arbitrary",)), exchange=exchange,
        first=lambda qt, kt: pl.program_id(0) == 0, last=lambda qt, kt: pl.program_id(0) == n_steps - 1)


BWD_HEADS = 4


def _attention_backward(q, k, k_t, v, d_cat, lse, delta, exchange=None):
    t_len = q.shape[1]
    tq = min(512, t_len)
    nq = t_len // tq
    hp = BWD_HEADS
    qi_tab, ki_tab = _triangle_steps(nq, False)

    def body(qi_ref, ki_ref, q_ref, k_ref, kt_ref, v_ref, do_ref, lse_ref, delta_ref, dqt_hbm, dk_ref, dv_ref,
             dqt_s, dk_s, dv_s):
        group, step = pl.program_id(0), pl.program_id(1)
        qi, ki = qi_ref[step], ki_ref[step]

        @pl.when(step == 0)
        def _():
            dqt_s[...] = jnp.zeros_like(dqt_s)

        @pl.when(qi == ki)
        def _():
            dk_s[...] = jnp.zeros_like(dk_s)
            dv_s[...] = jnp.zeros_like(dv_s)

        def accumulate(masked):
            for h in range(hp):
                do_b = do_ref[:, h * HEAD_DIM:(h + 1) * HEAD_DIM].astype(BF16)
                s_t = _dot_nt(k_ref[h], q_ref[h]) * ATT_LOG2
                if masked:
                    s_t = jnp.where(_key_le_query(tq), s_t, NEG_BIG)
                p_t = jnp.exp2(s_t - lse_ref[h])
                dp_t = _dot_nt(v_ref[h], do_b)
                ds_t = (p_t * (dp_t - delta_ref[h]) * ATT_SCALE).astype(BF16)
                dv_s[h] += _dot(p_t.astype(BF16), do_b)
                dk_s[h] += _dot(ds_t, q_ref[h])
                dqt_s[h, qi] += _dot(kt_ref[h], ds_t)

        @pl.when(ki < qi)
        def _():
            accumulate(False)

        @pl.when(ki == qi)
        def _():
            accumulate(True)
            for h in range(hp):
                pltpu.sync_copy(dqt_s.at[h, qi], dqt_hbm.at[group * hp + h, qi])

        @pl.when(qi == nq - 1)
        def _():
            dk_ref[...] = dk_s[...]
            dv_ref[...] = dv_s[...]

    wide = hp * HEAD_DIM
    n_groups, n_steps = N_HEADS // hp, qi_tab.shape[0]
    return _pallas(
        body, name="attention_backward", grid=(n_groups, n_steps), prefetch=(qi_tab, ki_tab),
        operands=(q, k, k_t, v, d_cat, lse, delta),
        out_shape=[jax.ShapeDtypeStruct((N_HEADS, nq, QK_DIM, tq), F32),
                   jax.ShapeDtypeStruct((N_HEADS, t_len, QK_DIM), F32),
                   jax.ShapeDtypeStruct((N_HEADS, t_len, HEAD_DIM), F32)],
        in_specs=[pl.BlockSpec((hp, tq, QK_DIM), lambda g, s, qt, kt: (g, qt[s], 0)),
                  pl.BlockSpec((hp, tq, QK_DIM), lambda g, s, qt, kt: (g, kt[s], 0)),
                  pl.BlockSpec((hp, QK_DIM, tq), lambda g, s, qt, kt: (g, 0, kt[s])),
                  pl.BlockSpec((hp, tq, HEAD_DIM), lambda g, s, qt, kt: (g, kt[s], 0)),
                  pl.BlockSpec((tq, wide), lambda g, s, qt, kt: (qt[s], n_groups + g)),
                  pl.BlockSpec((hp, 1, tq), lambda g, s, qt, kt: (g, 0, qt[s])),
                  pl.BlockSpec((hp, 1, tq), lambda g, s, qt, kt: (g, 0, qt[s]))],
        out_specs=[pl.BlockSpec(memory_space=pl.ANY),
                   pl.BlockSpec((hp, tq, QK_DIM), lambda g, s, qt, kt: (g, kt[s], 0)),
                   pl.BlockSpec((hp, tq, HEAD_DIM), lambda g, s, qt, kt: (g, kt[s], 0))],
        scratch_shapes=[pltpu.VMEM((hp, nq, QK_DIM, tq), F32), pltpu.VMEM((hp, tq, QK_DIM), F32),
                        pltpu.VMEM((hp, tq, HEAD_DIM), F32)],
        params=_params(58, ("arbitrary", "arbitrary")), exchange=exchange,
        first=lambda qt, kt: (pl.program_id(0) == 0) & (pl.program_id(1) == 0),
        last=lambda qt, kt: (pl.program_id(0) == n_groups - 1) & (pl.program_id(1) == n_steps - 1))


def _out_project(o_hg, o_mla, x, g_a, w_out, exchange=None):
    t_len = x.shape[0]
    tm = min(512, t_len)
    half = N_HEADS * HEAD_DIM

    def body(ohg_ref, omla_ref, x_ref, ga_ref, w_ref, cat_ref, mix_ref, xhat_ref, rstd_ref):
        a = ohg_ref[...].astype(BF16)
        b = omla_ref[...].astype(BF16)
        cat_ref[:, 0:half] = a
        cat_ref[:, half:2 * half] = b
        mix = _dot(a, w_ref[0:half, :]) + _dot(b, w_ref[half:2 * half, :])
        mix_ref[...] = mix
        r1 = DN_ALPHA * x_ref[...] + (1.0 + ga_ref[...]) * mix
        xc = r1 - _rowmean(r1)
        rstd = lax.rsqrt(_rowmean(xc * xc) + LN_EPS)
        xhat_ref[...] = xc * rstd
        rstd_ref[...] = rstd

    row = lambda i: (i, 0)
    fixed = lambda i: (0, 0)
    n_tiles = t_len // tm
    return _pallas(
        body, name="out_project", grid=(n_tiles,), operands=(o_hg, o_mla, x, g_a, w_out),
        out_shape=[jax.ShapeDtypeStruct((t_len, D_MODEL), BF16), jax.ShapeDtypeStruct((t_len, D_MODEL), F32),
                   jax.ShapeDtypeStruct((t_len, D_MODEL), F32), jax.ShapeDtypeStruct((t_len, 1), F32)],
        in_specs=[pl.BlockSpec((tm, half), row), pl.BlockSpec((tm, half), row), pl.BlockSpec((tm, D_MODEL), row),
                  pl.BlockSpec((1, D_MODEL), fixed), pl.BlockSpec((D_MODEL, D_MODEL), fixed)],
        out_specs=[pl.BlockSpec((tm, D_MODEL), row), pl.BlockSpec((tm, D_MODEL), row),
                   pl.BlockSpec((tm, D_MODEL), row), pl.BlockSpec((tm, 1), row)],
        params=_params(48, ("arbitrary",)), exchange=exchange,
        first=lambda: pl.program_id(0) == 0, last=lambda: pl.program_id(0) == n_tiles - 1)


V_LN1G, V_LN1B, V_SCM, V_SHM, V_GM, V_GA, V_LN2G, V_LN2B = range(8)
S_DLN2G, S_DLN2B, S_DGM, S_DSCM, S_DSHM, S_DLN1G, S_DLN1B, S_DGA, S_LOSS = range(9)


def _mlp_and_back(xhat1, rstd1, mix, target, o_mla, vecs, w1_top, w1_bottom, w2, w_out):
    t_len = xhat1.shape[0]
    tm = min(256, t_len)
    n_ff = w1_top.shape[0]
    ff = w1_top.shape[2]
    top_rows = w1_top.shape[1]

    def body(xhat_ref, rstd_ref, mix_ref, tgt_ref, omla_ref, vec_ref, w1_top_hbm, w1_bottom_hbm, w2_hbm, wout_hbm,
             act_ref, dhp_ref, um_ref, dh_ref, dmix_ref, dcat_ref, dr1_ref, sums_ref, delta_ref,
             w1_s, w2_s, wout_s, hp_s, load_sems):
        @pl.when(pl.program_id(0) == 0)
        def _():
            loads = [pltpu.make_async_copy(w1_top_hbm, w1_s.at[:, 0:top_rows], load_sems.at[0]),
                     pltpu.make_async_copy(w1_bottom_hbm, w1_s.at[:, top_rows:D_MODEL], load_sems.at[3]),
                     pltpu.make_async_copy(w2_hbm, w2_s, load_sems.at[1]),
                     pltpu.make_async_copy(wout_hbm, wout_s, load_sems.at[2])]
            for cp in loads:
                cp.start()
            sums_ref[...] = jnp.zeros_like(sums_ref)
            for cp in loads:
                cp.wait()

        vec = lambda r: vec_ref[r:r + 1, :]
        xhat = xhat_ref[...]
        x1 = xhat * vec(V_LN1G) + vec(V_LN1B)
        um = (x1 * (1.0 + vec(V_SCM)) + vec(V_SHM)).astype(BF16)
        um_ref[...] = um
        h = jnp.zeros((tm, D_MODEL), F32)
        for j in range(n_ff):
            hp = _dot(um, w1_s[j])
            hp_s[j] = hp
            act = jnp.square(jnp.maximum(hp, 0.0)).astype(BF16)
            act_ref[:, j * ff:(j + 1) * ff] = act
            h = h + _dot(act, w2_s[j])
        r2 = DN_ALPHA * x1 + (1.0 + vec(V_GM)) * h
        xc = r2 - _rowmean(r2)
        rstd2 = lax.rsqrt(_rowmean(xc * xc) + LN_EPS)
        xhat2 = xc * rstd2
        err = xhat2 * vec(V_LN2G) + vec(V_LN2B) - tgt_ref[...]
        loss = 0.5 * jnp.sum(_rowmean(err * err))
        dy = err * (1.0 / D_MODEL)
        dxh = dy * vec(V_LN2G)
        dr2 = rstd2 * (dxh - _rowmean(dxh) - xhat2 * _rowmean(dxh * xhat2))
        dh = ((1.0 + vec(V_GM)) * dr2).astype(BF16)
        dh_ref[...] = dh
        sums_ref[S_DLN2G:S_DLN2G + 1, :] += _colsum(dy * xhat2)
        sums_ref[S_DLN2B:S_DLN2B + 1, :] += _colsum(dy)
        sums_ref[S_DGM:S_DGM + 1, :] += _colsum(dr2 * h)
        sums_ref[S_LOSS:S_LOSS + 1, :] += jnp.full((1, D_MODEL), loss, F32)
        du = jnp.zeros((tm, D_MODEL), F32)
        for j in range(n_ff):
            dhp = (_dot_nt(dh, w2_s[j]) * (2.0 * jnp.maximum(hp_s[j], 0.0))).astype(BF16)
            dhp_ref[:, j * ff:(j + 1) * ff] = dhp
            du = du + _dot_nt(dhp, w1_s[j])
        sums_ref[S_DSCM:S_DSCM + 1, :] += _colsum(du * x1)
        sums_ref[S_DSHM:S_DSHM + 1, :] += _colsum(du)
        dx1 = DN_ALPHA * dr2 + du * (1.0 + vec(V_SCM))
        sums_ref[S_DLN1G:S_DLN1G + 1, :] += _colsum(dx1 * xhat)
        sums_ref[S_DLN1B:S_DLN1B + 1, :] += _colsum(dx1)
        dxh1 = dx1 * vec(V_LN1G)
        dr1 = rstd_ref[...] * (dxh1 - _rowmean(dxh1) - xhat * _rowmean(dxh1 * xhat))
        dr1_ref[...] = dr1
        sums_ref[S_DGA:S_DGA + 1, :] += _colsum(dr1 * mix_ref[...])
        dmix = ((1.0 + vec(V_GA)) * dr1).astype(BF16)
        dmix_ref[...] = dmix
        dcat = _dot_nt(dmix, wout_s[...])
        dcat_ref[...] = dcat
        ones = jnp.ones((8, HEAD_DIM), F32)
        half = N_HEADS * HEAD_DIM
        for hd in range(N_HEADS):
            prod = dcat[:, half + hd * HEAD_DIM:half + (hd + 1) * HEAD_DIM] * omla_ref[:, hd * HEAD_DIM:(hd + 1) * HEAD_DIM]
            delta_ref[hd] = lax.dot_general(ones, prod, (((1,), (1,)), ((), ())), preferred_element_type=F32,
                                            precision=lax.Precision.HIGHEST)[0:1]

    row = lambda i: (i, 0)
    fixed = lambda i: (0, 0)
    any_spec = pl.BlockSpec(memory_space=pl.ANY)
    return _pcall(
        body, name="mlp_and_back", grid=(t_len // tm,),
        out_shape=[jax.ShapeDtypeStruct((t_len, D_FF), BF16), jax.ShapeDtypeStruct((t_len, D_FF), BF16),
                   jax.ShapeDtypeStruct((t_len, D_MODEL), BF16), jax.ShapeDtypeStruct((t_len, D_MODEL), BF16),
                   jax.ShapeDtypeStruct((t_len, D_MODEL), BF16), jax.ShapeDtypeStruct((t_len, D_MODEL), F32),
                   jax.ShapeDtypeStruct((t_len, D_MODEL), F32), jax.ShapeDtypeStruct((16, D_MODEL), F32),
                   jax.ShapeDtypeStruct((N_HEADS, 1, t_len), F32)],
        in_specs=[pl.BlockSpec((tm, D_MODEL), row), pl.BlockSpec((tm, 1), row), pl.BlockSpec((tm, D_MODEL), row),
                  pl.BlockSpec((tm, D_MODEL), row), pl.BlockSpec((tm, N_HEADS * HEAD_DIM), row),
                  pl.BlockSpec((8, D_MODEL), fixed), any_spec, any_spec, any_spec, any_spec],
        out_specs=[pl.BlockSpec((tm, D_FF), row), pl.BlockSpec((tm, D_FF), row), pl.BlockSpec((tm, D_MODEL), row),
                   pl.BlockSpec((tm, D_MODEL), row), pl.BlockSpec((tm, D_MODEL), row), pl.BlockSpec((tm, D_MODEL), row),
                   pl.BlockSpec((tm, D_MODEL), row), pl.BlockSpec((16, D_MODEL), fixed),
                   pl.BlockSpec((N_HEADS, 1, tm), lambda i: (0, 0, i))],
        scratch_shapes=[pltpu.VMEM((n_ff, D_MODEL, ff), BF16), pltpu.VMEM(w2.shape, BF16), pltpu.VMEM(w_out.shape, BF16),
                        pltpu.VMEM((n_ff, tm, ff), F32), pltpu.SemaphoreType.DMA((4,))],
        compiler_params=_params(56, ("arbitrary",)),
        operands=(xhat1, rstd1, mix, target, o_mla, vecs, w1_top, w1_bottom, w2, w_out))


def _in_project_backward(dq, dk, dv, cq, ckv, pos, invf, q_norm_w, kv_norm_w, w_q, w_kv,
                         d_hq, d_hf, d_hi, d_hg, w_in, dr1, x, sc_a, exchange=None):
    t_len = x.shape[0]
    tm = min(512, t_len)
    per_q = dq.shape[3] // tm
    hgw = N_HEADS * HEAD_DIM

    def body(dq_ref, dk_ref, dv_ref, cq_ref, ckv_ref, pos_ref, invf_ref, qn_ref, kvn_ref, wq_ref, wkv_ref,
             dhq_ref, dhf_ref, dhi_ref, dhg_ref, win_ref, dr1_ref, x_ref, sc_ref,
             dz_ref, dqf_ref, dkvu_ref, cqn_ref, ckvn_ref, gx_ref, sums_ref):
        @pl.when(pl.program_id(0) == 0)
        def _():
            sums_ref[...] = jnp.zeros_like(sums_ref)

        cos_t, sin_t = _rope_tables(pos_ref[...], invf_ref[...])
        cq = cq_ref[...]
        ckv = ckv_ref[...]
        rq = lax.rsqrt(_rowmean(cq * cq) + RMS_EPS)
        rkv = lax.rsqrt(_rowmean(ckv * ckv) + RMS_EPS)
        cqn_ref[...] = (cq * rq * qn_ref[...]).astype(BF16)
        ckvn_ref[...] = (ckv * rkv * kvn_ref[...]).astype(BF16)
        d_cqn = jnp.zeros((tm, Q_RANK), F32)
        d_ckvn = jnp.zeros((tm, KV_RANK), F32)
        d_kpe = jnp.zeros((tm, 128), F32)
        for h in range(N_HEADS):
            dqh = jnp.transpose(dq_ref[h])
            dqf_ref[h, :, 0:HEAD_DIM] = dqh[:, :HEAD_DIM].astype(BF16)
            dqf_ref[h, :, HEAD_DIM:QK_DIM] = _unrope(dqh[:, HEAD_DIM:], cos_t, sin_t).astype(BF16)
            d_cqn = d_cqn + _dot_nt(dqf_ref[h], wq_ref[h])
            dkh = dk_ref[h]
            d_kpe = d_kpe + dkh[:, HEAD_DIM:]
            dkvu_ref[h, :, 0:HEAD_DIM] = dkh[:, :HEAD_DIM].astype(BF16)
            dkvu_ref[h, :, HEAD_DIM:2 * HEAD_DIM] = dv_ref[h].astype(BF16)
            d_ckvn = d_ckvn + _dot_nt(dkvu_ref[h], wkv_ref[h])
        dyq = d_cqn * qn_ref[...]
        dykv = d_ckvn * kvn_ref[...]
        sums_ref[2:3, 0:Q_RANK] += _colsum(d_cqn * cq * rq)
        sums_ref[3:4, 0:KV_RANK] += _colsum(d_ckvn * ckv * rkv)
        dz_ref[:, 0:hgw] = dhq_ref[...]
        dz_ref[:, hgw:2 * hgw] = dhf_ref[...]
        dz_ref[:, 2 * hgw:3 * hgw] = dhi_ref[...]
        dz_ref[:, 3 * hgw:4 * hgw] = dhg_ref[...]
        dz_ref[:, HG_COLS:HG_COLS + Q_RANK] = (rq * dyq - cq * (rq * rq * rq) * _rowmean(dyq * cq)).astype(BF16)
        dz_ref[:, HG_COLS + Q_RANK:HG_COLS + Q_RANK + KV_RANK] = (
            rkv * dykv - ckv * (rkv * rkv * rkv) * _rowmean(dykv * ckv)).astype(BF16)
        dz_ref[:, HG_COLS + Q_RANK + KV_RANK:] = _unrope(d_kpe, cos_t, sin_t).astype(BF16)
        du = _dot(dz_ref[...], win_ref[...])
        xv = x_ref[...]
        gx_ref[...] = DN_ALPHA * dr1_ref[...] + (1.0 + sc_ref[...]) * du
        sums_ref[0:1, :] += _colsum(du * xv)
        sums_ref[1:2, :] += _colsum(du)

    row = lambda i: (i, 0)
    fixed2 = lambda i: (0, 0)
    fixed3 = lambda i: (0, 0, 0)
    heads = lambda i: (0, i, 0)
    n_tiles = t_len // tm
    return _pallas(
        body, name="in_project_backward", grid=(n_tiles,),
        operands=(dq, dk, dv, cq, ckv, pos, invf, q_norm_w, kv_norm_w, w_q, w_kv, d_hq, d_hf, d_hi, d_hg, w_in, dr1, x,
                  sc_a),
        out_shape=[jax.ShapeDtypeStruct((t_len, IN_COLS_PAD), BF16), jax.ShapeDtypeStruct((N_HEADS, t_len, QK_DIM), BF16),
                   jax.ShapeDtypeStruct((N_HEADS, t_len, 2 * HEAD_DIM), BF16), jax.ShapeDtypeStruct((t_len, Q_RANK), BF16),
                   jax.ShapeDtypeStruct((t_len, KV_RANK), BF16), jax.ShapeDtypeStruct((t_len, D_MODEL), F32),
                   jax.ShapeDtypeStruct((8, D_MODEL), F32)],
        in_specs=[pl.BlockSpec((N_HEADS, None, QK_DIM, tm), lambda i: (0, i // per_q, 0, i % per_q)),
                  pl.BlockSpec((N_HEADS, tm, QK_DIM), heads),
                  pl.BlockSpec((N_HEADS, tm, HEAD_DIM), heads), pl.BlockSpec((tm, Q_RANK), row),
                  pl.BlockSpec((tm, KV_RANK), row), pl.BlockSpec((tm, 1), row), pl.BlockSpec((1, 128), fixed2),
                  pl.BlockSpec((1, Q_RANK), fixed2), pl.BlockSpec((1, KV_RANK), fixed2),
                  pl.BlockSpec((N_HEADS, Q_RANK, QK_DIM), fixed3), pl.BlockSpec((N_HEADS, KV_RANK, 2 * HEAD_DIM), fixed3),
                  pl.BlockSpec((tm, hgw), row), pl.BlockSpec((tm, hgw), row), pl.BlockSpec((tm, hgw), row),
                  pl.BlockSpec((tm, hgw), row), pl.BlockSpec((IN_COLS_PAD, D_MODEL), fixed2),
                  pl.BlockSpec((tm, D_MODEL), row), pl.BlockSpec((tm, D_MODEL), row), pl.BlockSpec((1, D_MODEL), fixed2)],
        out_specs=[pl.BlockSpec((tm, IN_COLS_PAD), row), pl.BlockSpec((N_HEADS, tm, QK_DIM), heads),
                   pl.BlockSpec((N_HEADS, tm, 2 * HEAD_DIM), heads), pl.BlockSpec((tm, Q_RANK), row),
                   pl.BlockSpec((tm, KV_RANK), row), pl.BlockSpec((tm, D_MODEL), row), pl.BlockSpec((8, D_MODEL), fixed2)],
        params=_params(48, ("arbitrary",)), exchange=exchange,
        first=lambda: pl.program_id(0) == 0, last=lambda: pl.program_id(0) == n_tiles - 1)


def _weight_grad(a, b, name, n_blocks, bn, a_blocked=False, b_blocked=True, exchange=None, token_tile=512):
    t_len = a.shape[0]
    m = a.shape[1] // n_blocks if a_blocked else a.shape[1]
    bt = min(token_tile, t_len)

    def body(a_ref, b_ref, o_ref):
        @pl.when(pl.program_id(1) == 0)
        def _():
            o_ref[...] = jnp.zeros_like(o_ref)

        o_ref[...] += _dot_tn(a_ref[...].astype(BF16), b_ref[...].astype(BF16))

    a_spec = pl.BlockSpec((bt, m), (lambda n, t: (t, n)) if a_blocked else (lambda n, t: (t, 0)))
    if b.ndim == 3:
        b_spec = pl.BlockSpec((None, bt, bn), lambda n, t: (n, t, 0))
    else:
        b_spec = pl.BlockSpec((bt, bn), (lambda n, t: (t, n)) if b_blocked else (lambda n, t: (t, 0)))
    nt = t_len // bt
    (out,), landed = _pallas(
        body, name=name, grid=(n_blocks, nt), operands=(a, b),
        out_shape=[jax.ShapeDtypeStruct((n_blocks, m, bn), F32)],
        in_specs=[a_spec, b_spec],
        out_specs=[pl.BlockSpec((None, m, bn), lambda n, t: (n, 0, 0))],
        params=_params(56, ("arbitrary", "arbitrary")), exchange=exchange,
        first=lambda: (pl.program_id(0) == 0) & (pl.program_id(1) == 0),
        last=lambda: (pl.program_id(0) == n_blocks - 1) & (pl.program_id(1) == nt - 1))
    return (out, landed) if exchange else out


SMALL_PLACE = {"ln1_g": (6, 0), "ln1_b": (7, 0), "ln2_g": (8, 0), "ln2_b": (9, 0), "hg_norm_w": (10, 512),
               "mla_q_norm_w": (11, 0), "mla_kv_norm_w": (11, Q_RANK)}
SMALL_LB_ROW, SMALL_LOSS_ROW = 10, 12


def _small_params_step(gathered, params):
    names = list(params)

    def body(g_ref, *refs):
        ins, outs = refs[:3 * len(names)], refs[3 * len(names):]
        loss_ref, outs = outs[0], outs[1:]
        tot = g_ref[0]
        for d in range(1, N_DEV):
            tot = tot + g_ref[d]
        loss_ref[...] = tot[SMALL_LOSS_ROW:SMALL_LOSS_ROW + 1, 0:128]

        def update(i, grad, rows=slice(None), lanes=slice(None)):
            w_ref, m_ref, v_ref = ins[3 * i:3 * i + 3]
            g_out, d_out, nm_out, nv_out = outs[4 * i:4 * i + 4]
            g_out[rows, lanes] = grad
            d_out[rows, lanes], nm_out[rows, lanes], nv_out[rows, lanes] = _adamw_update(
                w_ref[rows, lanes], grad, m_ref[rows, lanes], v_ref[rows, lanes])

        for i, name in enumerate(names):
            if name == "b_ada":
                for r in range(6):
                    update(i, tot[r:r + 1, :], lanes=slice(r * D_MODEL, (r + 1) * D_MODEL))
            elif name == "hg_lower_bounds":
                lb = _lower_bound(ins[3 * i][...])
                d0 = tot[SMALL_LB_ROW:SMALL_LB_ROW + 1, 0:512] * lb * (1.0 - lb)
                update(i, d0, rows=slice(0, 1))
                update(i, -d0, rows=slice(1, 2))
            else:
                row, lane = SMALL_PLACE[name]
                update(i, tot[row:row + 1, lane:lane + params[name][0].shape[1]])

    flat_in = [a for name in names for a in params[name]]
    shapes = [jax.ShapeDtypeStruct((1, 128), F32)] + [jax.ShapeDtypeStruct(params[name][0].shape, F32)
                                                      for name in names for _ in range(4)]
    out = pl.pallas_call(body, name="small_params_step", out_shape=shapes)(gathered, *flat_in)
    return out[0], {name: out[1 + 4 * i:5 + 4 * i] for i, name in enumerate(names)}


def _adamw_update(w, gv, m, v):
    nm = ADAM_B1 * m + (1.0 - ADAM_B1) * gv
    nv = ADAM_B2 * v + (1.0 - ADAM_B2) * jnp.square(gv)
    m_hat = nm / (1.0 - ADAM_B1 ** ADAM_STEP)
    v_hat = nv / (1.0 - ADAM_B2 ** ADAM_STEP)
    return -ADAM_LR * (m_hat / (jnp.sqrt(v_hat) + ADAM_EPS) + ADAM_WD * w), nm, nv


def _adamw_halves(core, w, mine, theirs, m, v, name):
    rows, cols = w.shape
    h = rows // 2
    tr = _row_tile(h)
    per_half = h // tr

    def body(core_ref, w_ref, mine_ref, theirs_ref, m_ref, v_ref, g_ref, d_ref, nm_ref, nv_ref):
        is_mine = pl.program_id(0) // per_half == core_ref[0]
        gv = jnp.where(is_mine, mine_ref[...], theirs_ref[...])
        g_ref[...] = gv
        d_ref[...], nm_ref[...], nv_ref[...] = _adamw_update(w_ref[...], gv, m_ref[...], v_ref[...])

    full = pl.BlockSpec((tr, cols), lambda i, core_ref: (i, 0))
    mine_spec = pl.BlockSpec((tr, cols), lambda i, core_ref: (jnp.where(i // per_half == core_ref[0], i % per_half, 0), 0))
    theirs_spec = pl.BlockSpec((tr, cols), lambda i, core_ref: (jnp.where(i // per_half == core_ref[0], 0, i % per_half), 0))
    return _pcall(
        body, name=name, out_shape=[jax.ShapeDtypeStruct(w.shape, F32)] * 4,
        grid_spec=pltpu.PrefetchScalarGridSpec(
            num_scalar_prefetch=1, grid=(rows // tr,), in_specs=[full, mine_spec, theirs_spec, full, full],
            out_specs=[full] * 4),
        compiler_params=_params(40, ("arbitrary",)),
        operands=(core, w, mine, theirs, m, v))


def _adamw(w, g, m, v, name):
    rows, cols = w.shape
    tr = _row_tile(rows) if rows >= 8 else rows

    def body(w_ref, g_ref, m_ref, v_ref, d_ref, nm_ref, nv_ref):
        d_ref[...], nm_ref[...], nv_ref[...] = _adamw_update(w_ref[...], g_ref[...], m_ref[...], v_ref[...])

    spec = pl.BlockSpec((tr, cols), lambda i: (i, 0))
    return _pcall(
        body, name=name, grid=(rows // tr,),
        out_shape=[jax.ShapeDtypeStruct(w.shape, F32)] * 3,
        in_specs=[spec] * 4, out_specs=[spec] * 3,
        compiler_params=_params(40, ("arbitrary",)),
        operands=(w, g, m, v))


def kernel(x, c, positions, w_ada, b_ada, w_in, hg_lower_bounds, hg_norm_w, mla_q_norm_w, w_q_up, mla_kv_norm_w, w_kv_up, w_out, ln1_g, ln1_b, w_mlp_in, w_mlp_out, ln2_g, ln2_b, loss_target, m_w_ada, m_b_ada, m_w_in, m_hg_lower_bounds, m_hg_norm_w, m_mla_q_norm_w, m_w_q_up, m_mla_kv_norm_w, m_w_kv_up, m_w_out, m_ln1_g, m_ln1_b, m_w_mlp_in, m_w_mlp_out, m_ln2_g, m_ln2_b, v_w_ada, v_b_ada, v_w_in, v_hg_lower_bounds, v_hg_norm_w, v_mla_q_norm_w, v_w_q_up, v_mla_kv_norm_w, v_w_kv_up, v_w_out, v_ln1_g, v_ln1_b, v_w_mlp_in, v_w_mlp_out, v_ln2_g, v_ln2_b):
    ix, iy, ic = _mesh_pos()
    chip = 2 * ix + iy
    me = 4 * ix + 2 * iy + ic
    core_arr = jnp.reshape(ic, (1,)).astype(jnp.int32)
    chip_arr = jnp.reshape(chip, (1,)).astype(jnp.int32)

    xs = x[0]
    target = loss_target[0]
    t_len = xs.shape[0]
    pos = positions.astype(F32).reshape(t_len, 1)
    inv = 1.0 / (ROPE_THETA ** (jnp.arange(0, ROPE_DIM, 2, dtype=F32) / ROPE_DIM))
    invf = jnp.concatenate([inv, inv, jnp.zeros((128 - ROPE_DIM,), F32)]).reshape(1, 128)

    def slot(w):
        rows, cols = w.shape
        own = w.astype(BF16).reshape(1, 2, rows // 2, cols)
        return lax.dynamic_update_slice(jnp.zeros((N_CHIPS, 2, rows // 2, cols), BF16), own, (chip, 0, 0, 0))

    def slot8(a):
        return lax.dynamic_update_slice(jnp.zeros((N_DEV,) + a.shape, a.dtype), a[None], (me, 0, 0))

    def whole(s):
        return s.reshape(N_CHIPS, 2 * s.shape[2], s.shape[3])

    def halved(g):
        return g.reshape(N_CHIPS, 2, g.shape[1] // 2, g.shape[2])

    ada_cols = w_ada.shape[2]
    c_all, *early = _run_exchange(
        _merge(_gather_all(slot8(jnp.broadcast_to(c, (8, D_MODEL)))),
               _gather_over_ici([slot(jnp.transpose(w_in[0])), slot(w_q_up[0]), slot(w_kv_up[0])])),
        "gather_c_and_mixer_weights_ici")
    b_shard = lax.dynamic_slice(b_ada, (0, chip * ada_cols), (1, ada_cols))
    mod_cols, cond16 = _ada_project(c_all[:, 0, :], w_ada[0], b_shard)
    mod_all, *early = _run_exchange(_merge(_gather_all(slot8(mod_cols)), _gather_over_d2d(early)),
                                    "gather_mod_and_mixer_weights_d2d")
    mod_mine = lax.dynamic_slice(mod_all, (0, me, 0), (N_DEV, 1, ada_cols))[::2, 0, :].reshape(6, D_MODEL)
    sh_a, sc_a, g_a, sh_m, sc_m, g_m = (mod_mine[i:i + 1] for i in range(6))
    g_in, g_q, g_kv = (whole(s) for s in early)
    w_in_full = jnp.pad(g_in.reshape(IN_COLS, D_MODEL), ((0, IN_COLS_PAD - IN_COLS), (0, 0)))
    w_q_full = jnp.pad(g_q, ((0, 0), (0, 0), (0, QK_DIM - g_q.shape[2])))

    w1_rows = D_MODEL // 2
    (u_a, zhg, cq, ckv, q, k, k_t, v, v_t), (s_top,) = _in_project(
        xs, pos, sc_a, sh_a, w_in_full, mla_q_norm_w, mla_kv_norm_w, w_q_full, g_kv, invf,
        _gather_over_ici([slot(w_mlp_in[0, :w1_rows])]))
    (o_pre, o_hg, states), (s_out, s_bottom, s_top) = _hgrn_forward(
        zhg, hg_lower_bounds, hg_norm_w,
        _merge(_gather_over_ici([slot(w_out[0]), slot(w_mlp_in[0, w1_rows:])]), _gather_over_d2d([s_top])))
    (o_mla, lse), (s_w2, s_out, s_bottom) = _attention_forward(
        q, k, v_t, _merge(_gather_over_ici([slot(w_mlp_out[0])]), _gather_over_d2d([s_out, s_bottom])))
    w_out_full = whole(s_out).reshape(D_MODEL, D_MODEL)
    (cat, mix, xhat1, rstd1), (s_w2,) = _out_project(o_hg, o_mla, xs, g_a, w_out_full, _gather_over_d2d([s_w2]))
    g_w1_top, g_w1_bottom, g_w2 = whole(s_top), whole(s_bottom), whole(s_w2)
    vecs = jnp.concatenate([ln1_g, ln1_b, sc_m, sh_m, g_m, g_a, ln2_g, ln2_b], axis=0)
    act, dhp, um, dh, dmix, d_cat, dr1, mlp_sums, delta = _mlp_and_back(
        xhat1, rstd1, mix, target, o_mla, vecs, g_w1_top, g_w1_bottom, g_w2, w_out_full)

    gw_1 = halved(_weight_grad(um, dhp, "grad_w_mlp_in", N_CHIPS, D_FF // N_CHIPS, token_tile=4096))
    gw_2, (landed_1,) = _weight_grad(act, dh, "grad_w_mlp_out", N_CHIPS, D_MODEL, a_blocked=True, b_blocked=False,
                                     token_tile=4096, exchange=_pair_exchange([gw_1]))
    gw_out = _weight_grad(cat, dmix, "grad_w_out", 1, D_MODEL, token_tile=2048)
    later = [halved(gw_2), halved(gw_out.reshape(N_CHIPS, D_MODEL // N_CHIPS, D_MODEL))]
    own_1, travels_1 = _add_pair(core_arr, chip_arr, gw_1, landed_1)
    (dq, dk, dv), (landed_1, *landed) = _attention_backward(
        q, k, k_t, v, d_cat, lse, delta, _merge(_chip_exchange([travels_1]), _pair_exchange(later)))
    mine_1 = _add_chips(own_1, landed_1)
    chip_sums = [_add_pair(core_arr, chip_arr, g, l) for g, l in zip(later, landed)]
    (d_hq, d_hf, d_hi, d_hg, hg_sums), (theirs_1, *landed) = _hgrn_backward(
        zhg, hg_lower_bounds, hg_norm_w, o_pre, d_cat, states,
        _merge(_pair_send([mine_1]), _chip_exchange([b for _, b in chip_sums])))
    later_mine = [_add_chips(own, l) for (own, _), l in zip(chip_sums, landed)]
    mlp_mine = [mine_1] + later_mine
    (dz, dqf, dkvu, cqn, ckvn, grad_x, in_sums), _ = _in_project_backward(
        dq, dk, dv, cq, ckv, pos, invf, mla_q_norm_w, mla_kv_norm_w, w_q_full, g_kv,
        d_hq, d_hf, d_hi, d_hg, w_in_full, dr1, xs, sc_a)

    zeros = lambda n: jnp.zeros((1, n), F32)
    small = jnp.concatenate([
        in_sums[1:2], in_sums[0:1], mlp_sums[S_DGA:S_DGA + 1],
        mlp_sums[S_DSHM:S_DSHM + 1], mlp_sums[S_DSCM:S_DSCM + 1], mlp_sums[S_DGM:S_DGM + 1],
        mlp_sums[S_DLN1G:S_DLN1G + 1], mlp_sums[S_DLN1B:S_DLN1B + 1],
        mlp_sums[S_DLN2G:S_DLN2G + 1], mlp_sums[S_DLN2B:S_DLN2B + 1],
        jnp.concatenate([hg_sums[0:1], hg_sums[1:2]], axis=1),
        jnp.concatenate([in_sums[2:3, :Q_RANK], in_sums[3:4, :KV_RANK], zeros(D_MODEL - Q_RANK - KV_RANK)], axis=1),
        mlp_sums[S_LOSS:S_LOSS + 1],
        jnp.zeros((SMALL_ROWS - 13, D_MODEL), F32)], axis=0)

    gw_in, (*later_theirs, small_all) = _weight_grad(
        dz, u_a, "grad_w_in", 3, D_MODEL, a_blocked=True, b_blocked=False, token_tile=4096,
        exchange=_merge(_pair_send(later_mine), _gather_all(slot8(small))))
    mlp_theirs = [theirs_1] + list(later_theirs)
    gw_in = gw_in.reshape(IN_COLS_PAD, D_MODEL)
    gw_q = _weight_grad(cqn, dqf, "grad_w_q_up", N_HEADS, QK_DIM, token_tile=2048)[:, :, :HEAD_DIM + ROPE_DIM]
    gw_kv = _weight_grad(ckvn, dkvu, "grad_w_kv_up", N_HEADS, 2 * HEAD_DIM, token_tile=2048)
    flat = lambda g: g.reshape(g.shape[0] * g.shape[1], g.shape[2])
    mixer_mine, mixer_theirs = _reduce_in_vmem(
        [gw_in, flat(gw_q), flat(gw_kv)], [IN_COLS // N_CHIPS // 2, Q_RANK // 2, KV_RANK // 2], "reduce_mixer_grads")
    reduced = ("w_in", "w_q_up", "w_kv_up", "w_mlp_in", "w_mlp_out", "w_out")
    halves_mine = dict(zip(reduced, list(mixer_mine) + mlp_mine))
    halves_theirs = dict(zip(reduced, list(mixer_theirs) + list(mlp_theirs)))

    small_names = ("b_ada", "hg_lower_bounds", "hg_norm_w", "mla_q_norm_w", "mla_kv_norm_w",
                   "ln1_g", "ln1_b", "ln2_g", "ln2_b")
    loss_row, small_out = _small_params_step(small_all, {
        "b_ada": (b_ada, m_b_ada, v_b_ada),
        "hg_lower_bounds": (hg_lower_bounds, m_hg_lower_bounds, v_hg_lower_bounds),
        "hg_norm_w": (hg_norm_w, m_hg_norm_w, v_hg_norm_w),
        "mla_q_norm_w": (mla_q_norm_w, m_mla_q_norm_w, v_mla_q_norm_w),
        "mla_kv_norm_w": (mla_kv_norm_w, m_mla_kv_norm_w, v_mla_kv_norm_w),
        "ln1_g": (ln1_g, m_ln1_g, v_ln1_g), "ln1_b": (ln1_b, m_ln1_b, v_ln1_b),
        "ln2_g": (ln2_g, m_ln2_g, v_ln2_g), "ln2_b": (ln2_b, m_ln2_b, v_ln2_b)})
    loss = loss_row[0, 0]

    d_mod_all = small_all[:, 0:6, :].reshape(N_DEV, 6 * D_MODEL)
    d_mod_cols = lax.dynamic_slice(d_mod_all, (0, chip * ada_cols), (N_DEV, ada_cols))
    d_mod_cols = jnp.concatenate([d_mod_cols, jnp.zeros_like(d_mod_cols)], axis=0)
    g_w_ada = _weight_grad(cond16, d_mod_cols, "grad_w_ada", 1, ada_cols)[0]

    names = ["w_ada", "b_ada", "w_in", "hg_lower_bounds", "hg_norm_w", "mla_q_norm_w", "w_q_up", "mla_kv_norm_w",
             "w_kv_up", "w_out", "ln1_g", "ln1_b", "w_mlp_in", "w_mlp_out", "ln2_g", "ln2_b"]
    weights = [w_ada, b_ada, w_in, hg_lower_bounds, hg_norm_w, mla_q_norm_w, w_q_up, mla_kv_norm_w,
               w_kv_up, w_out, ln1_g, ln1_b, w_mlp_in, w_mlp_out, ln2_g, ln2_b]
    moms = [m_w_ada, m_b_ada, m_w_in, m_hg_lower_bounds, m_hg_norm_w, m_mla_q_norm_w, m_w_q_up, m_mla_kv_norm_w,
            m_w_kv_up, m_w_out, m_ln1_g, m_ln1_b, m_w_mlp_in, m_w_mlp_out, m_ln2_g, m_ln2_b]
    vels = [v_w_ada, v_b_ada, v_w_in, v_hg_lower_bounds, v_hg_norm_w, v_mla_q_norm_w, v_w_q_up, v_mla_kv_norm_w,
            v_w_kv_up, v_w_out, v_ln1_g, v_ln1_b, v_w_mlp_in, v_w_mlp_out, v_ln2_g, v_ln2_b]
    out_g, out_d, out_m, out_v = [], [], [], []
    for name, w, m, vv in zip(names, weights, moms, vels):
        if name in small_names:
            g, d, nm, nv = small_out[name]
            back = lambda a: a
        elif name == "w_in":
            to2d, back = (lambda a: jnp.transpose(a[0])), (lambda a: jnp.transpose(a)[None])
        else:
            to2d, back = (lambda a, s=w.shape[1:]: a.reshape(s)), (lambda a, s=w.shape: a.reshape(s))
        if name == "w_ada":
            d, nm, nv = _adamw(to2d(w), g_w_ada, to2d(m), to2d(vv), "adamw_" + name)
            g = g_w_ada
        elif name not in small_names:
            g, d, nm, nv = _adamw_halves(core_arr, to2d(w), halves_mine[name], halves_theirs[name], to2d(m), to2d(vv),
                                         "adamw_" + name)
        out_g.append(back(g))
        out_d.append(back(d))
        out_m.append(back(nm))
        out_v.append(back(nv))
    return (loss, grad_x[None], *out_g, *out_d, *out_m, *out_v)
```

```python
import functools

import jax
import jax.numpy as jnp
from jax import lax
from jax.experimental import pallas as pl
from jax.experimental.pallas import tpu as pltpu

F32 = jnp.float32
BF16 = jnp.bfloat16
MESH_IDS = pl.DeviceIdType.MESH

D_MODEL = 1024
N_HEADS = 4
HEAD_DIM = 128
ROPE_DIM = 64
HG_CHUNK = 64
HG_COLS = 2048
Q_RANK = 256
KV_RANK = 256
IN_COLS = 2624
IN_COLS_PAD = 2688
QK_DIM = 256
D_FF = 4096
N_CHIPS = 4
N_DEV = 8
ROPE_THETA = 10000.0
RMS_EPS = 1e-6
LN_EPS = 1e-5
DN_ALPHA = 2.0 ** 0.25
ATT_SCALE = (HEAD_DIM + ROPE_DIM) ** -0.5
NEG_BIG = -1e30
ADAM_LR = 0.001
ADAM_B1 = 0.9
ADAM_B2 = 0.999
ADAM_EPS = 1e-08
ADAM_WD = 0.01
ADAM_STEP = 10
SMALL_ROWS = 16
MIB = 1024 * 1024


def _dot(a, b):
    return jnp.dot(a, b, preferred_element_type=F32)


def _dot_nt(a, b):
    return lax.dot_general(a, b, (((1,), (1,)), ((), ())), preferred_element_type=F32)


def _dot_tn(a, b):
    return lax.dot_general(a, b, (((0,), (0,)), ((), ())), preferred_element_type=F32)


def _params(vmem_mib, semantics=None):
    return pltpu.CompilerParams(vmem_limit_bytes=vmem_mib * MIB, dimension_semantics=semantics)


def _sigmoid(v):
    return 1.0 / (1.0 + jnp.exp(-v))


def _colsum(v):
    return jnp.sum(v, axis=0, keepdims=True)


def _rowmean(v):
    return jnp.mean(v, axis=-1, keepdims=True)


def _rope_tables(pos, invf):
    ang = pos * invf
    lane = lax.broadcasted_iota(jnp.int32, ang.shape, 1)
    cos_t = jnp.where(lane < ROPE_DIM, jnp.cos(ang), 0.0)
    sin = jnp.sin(ang)
    sin_t = jnp.where(lane < ROPE_DIM // 2, -sin, jnp.where(lane < ROPE_DIM, sin, 0.0))
    return cos_t, sin_t


def _swap_halves(t):
    lane = lax.broadcasted_iota(jnp.int32, t.shape, 1)
    return jnp.where(lane < ROPE_DIM // 2, pltpu.roll(t, 128 - ROPE_DIM // 2, 1), pltpu.roll(t, ROPE_DIM // 2, 1))


def _rope(t, cos_t, sin_t):
    return t * cos_t + _swap_halves(t) * sin_t


def _unrope(g, cos_t, sin_t):
    return g * cos_t - _swap_halves(g) * sin_t


def _mesh_pos():
    return lax.axis_index("x"), lax.axis_index("y"), lax.axis_index("c")


def _other_chips(x, y):
    out = []
    for dx, dy in ((1, 0), (0, 1), (1, 1)):
        px = 1 - x if dx else x
        py = 1 - y if dy else y
        out.append(((px, py), 2 * px + py))
    return out


class _Exchange:
    def __init__(self, inputs, out_shapes, aliases, sems, start, finish):
        self.inputs, self.out_shapes, self.aliases, self.sems = list(inputs), list(out_shapes), dict(aliases), list(sems)
        self.start, self.finish = start, finish


def _from_copies(inputs, out_shapes, aliases, sems, copies):
    def start(ins, outs, sem_refs):
        for send, _ in copies(ins, outs, sem_refs):
            send.start()

    def finish(ins, outs, sem_refs):
        for send, recv in copies(ins, outs, sem_refs):
            recv.wait_recv()
            send.wait_send()

    return _Exchange(inputs, out_shapes, aliases, sems, start, finish)


HBM_MIN_BYTES = 256 * 1024


def _in_hbm(a):
    if a.size * a.dtype.itemsize < HBM_MIN_BYTES:
        return a
    return pltpu.with_memory_space_constraint(a, pltpu.HBM)


def _out_hbm(s):
    if s.size * s.dtype.itemsize < HBM_MIN_BYTES:
        return s
    return pltpu.HBM(s.shape, s.dtype)


def _pcall(body, *, operands, out_shape, **kwargs):
    single = not isinstance(out_shape, (list, tuple))
    shapes = [_out_hbm(s) for s in ([out_shape] if single else out_shape)]
    return pl.pallas_call(body, out_shape=shapes[0] if single else shapes, **kwargs)(*[_in_hbm(a) for a in operands])


def _run_exchange(exchange, name):
    n_in, n_out = len(exchange.inputs), len(exchange.out_shapes)

    def body(*refs):
        ins, outs, sem_refs = refs[:n_in], refs[n_in:n_in + n_out], refs[n_in + n_out:]
        exchange.start(ins, outs, sem_refs)
        exchange.finish(ins, outs, sem_refs)

    any_spec = pl.BlockSpec(memory_space=pl.ANY)
    return pl.pallas_call(
        body, name=name, out_shape=[_out_hbm(s) for s in exchange.out_shapes],
        in_specs=[any_spec] * n_in, out_specs=[any_spec] * n_out,
        scratch_shapes=exchange.sems, input_output_aliases=exchange.aliases,
    )(*[_in_hbm(a) for a in exchange.inputs])


def _pallas(body, *, name, operands, in_specs, out_shape, out_specs, params, scratch_shapes=(), grid=(), prefetch=(),
            exchange=None, first=None, last=None):
    n_pre, n_in, n_out, n_scr = len(prefetch), len(in_specs), len(out_specs), len(scratch_shapes)
    ex_in = exchange.inputs if exchange else []
    ex_out = exchange.out_shapes if exchange else []
    ex_sems = exchange.sems if exchange else []

    def full_body(*refs):
        pre, rest = refs[:n_pre], refs[n_pre:]
        ins, rest = rest[:n_in], rest[n_in:]
        xin, rest = rest[:len(ex_in)], rest[len(ex_in):]
        outs, rest = rest[:n_out], rest[n_out:]
        xout, rest = rest[:len(ex_out)], rest[len(ex_out):]
        scr, sem_refs = rest[:n_scr], rest[n_scr:]
        if exchange:
            @pl.when(first(*pre))
            def _():
                exchange.start(xin, xout, sem_refs)

        body(*pre, *ins, *outs, *scr)
        if exchange:
            @pl.when(last(*pre))
            def _():
                exchange.finish(xin, xout, sem_refs)

    any_spec = pl.BlockSpec(memory_space=pl.ANY)
    aliases = {n_pre + n_in + i: n_out + o for i, o in exchange.aliases.items()} if exchange else {}
    operands = [_in_hbm(a) for a in operands]
    results = pl.pallas_call(
        full_body, name=name, out_shape=[_out_hbm(s) for s in list(out_shape) + ex_out],
        grid_spec=pltpu.PrefetchScalarGridSpec(
            num_scalar_prefetch=n_pre, grid=grid, in_specs=list(in_specs) + [any_spec] * len(ex_in),
            out_specs=list(out_specs) + [any_spec] * len(ex_out), scratch_shapes=list(scratch_shapes) + ex_sems),
        input_output_aliases=aliases, compiler_params=params,
    )(*prefetch, *operands, *[_in_hbm(a) for a in ex_in])
    return results[:n_out], results[n_out:]


def _remote(src, dst, sems, idx, to):
    send_sems, recv_sems = sems
    return pltpu.make_async_remote_copy(src_ref=src, dst_ref=dst, send_sem=send_sems.at[idx], recv_sem=recv_sems.at[idx],
                                        device_id=to, device_id_type=MESH_IDS)


def _sem_pairs(*shape):
    return [pltpu.SemaphoreType.DMA(shape), pltpu.SemaphoreType.DMA(shape)]


def _same_shapes(arrays):
    return [jax.ShapeDtypeStruct(a.shape, a.dtype) for a in arrays]


def _gather_over_ici(slots):
    n = len(slots)

    def copies(ins, outs, sems):
        x, y, c = _mesh_pos()
        k = 2 * x + y
        out = []
        for j, (chip, kj) in enumerate(_other_chips(x, y)):
            for i in range(n):
                to = (*chip, c)
                out.append((_remote(ins[i].at[k, c], outs[i].at[k, c], sems, (j, i), to),
                            _remote(ins[i].at[k, c], outs[i].at[kj, c], sems, (j, i), to)))
        return out

    return _from_copies(slots, _same_shapes(slots), {i: i for i in range(n)}, _sem_pairs(3, n), copies)


def _gather_over_d2d(slots):
    n = len(slots)

    def copies(ins, outs, sems):
        x, y, c = _mesh_pos()
        sibling = (x, y, 1 - c)
        out = []
        for j, (_, kj) in enumerate(_other_chips(x, y)):
            for i in range(n):
                out.append((_remote(ins[i].at[kj, c], outs[i].at[kj, c], sems, (j, i), sibling),
                            _remote(ins[i].at[kj, c], outs[i].at[kj, 1 - c], sems, (j, i), sibling)))
        return out

    return _from_copies(slots, _same_shapes(slots), {i: i for i in range(n)}, _sem_pairs(3, n), copies)


def _gather_all(slots8):
    def copies(ins, outs, sems):
        x, y, c = _mesh_pos()
        me = 4 * x + 2 * y + c
        out = []
        for r in range(1, N_DEV):
            px = 1 - x if r & 4 else x
            py = 1 - y if r & 2 else y
            pc = 1 - c if r & 1 else c
            to = (px, py, pc)
            out.append((_remote(ins[0].at[me], outs[0].at[me], sems, r - 1, to),
                        _remote(ins[0].at[me], outs[0].at[4 * px + 2 * py + pc], sems, r - 1, to)))
        return out

    return _from_copies([slots8], _same_shapes([slots8]), {0: 0}, _sem_pairs(N_DEV - 1), copies)


def _merge(first, second):
    n_in, n_out, n_sem = len(first.inputs), len(first.out_shapes), len(first.sems)

    def start(ins, outs, sems):
        first.start(ins[:n_in], outs[:n_out], sems[:n_sem])
        second.start(ins[n_in:], outs[n_out:], sems[n_sem:])

    def finish(ins, outs, sems):
        first.finish(ins[:n_in], outs[:n_out], sems[:n_sem])
        second.finish(ins[n_in:], outs[n_out:], sems[n_sem:])

    aliases = dict(first.aliases)
    aliases.update({n_in + i: n_out + o for i, o in second.aliases.items()})
    return _Exchange(first.inputs + second.inputs, first.out_shapes + second.out_shapes, aliases,
                     first.sems + second.sems, start, finish)


def _pair_exchange(grads):
    n = len(grads)

    def copies(ins, outs, sems):
        x, y, c = _mesh_pos()
        cps = [_remote(ins[i].at[:, 1 - c], outs[i], sems, i, (x, y, 1 - c)) for i in range(n)]
        return [(cp, cp) for cp in cps]

    shapes = [jax.ShapeDtypeStruct((N_CHIPS,) + g.shape[2:], g.dtype) for g in grads]
    return _from_copies(grads, shapes, {}, _sem_pairs(n), copies)


def _chip_exchange(partials):
    n = len(partials)

    def copies(ins, outs, sems):
        x, y, c = _mesh_pos()
        cps = [_remote(ins[i].at[kj], outs[i].at[j], sems, (j, i), (*chip, c))
               for j, (chip, kj) in enumerate(_other_chips(x, y)) for i in range(n)]
        return [(cp, cp) for cp in cps]

    shapes = [jax.ShapeDtypeStruct((3,) + p.shape[1:], p.dtype) for p in partials]
    return _from_copies(partials, shapes, {}, _sem_pairs(3, n), copies)


def _pair_send(halves):
    n = len(halves)

    def copies(ins, outs, sems):
        x, y, c = _mesh_pos()
        cps = [_remote(ins[i], outs[i], sems, i, (x, y, 1 - c)) for i in range(n)]
        return [(cp, cp) for cp in cps]

    return _from_copies(halves, _same_shapes(halves), {}, _sem_pairs(n), copies)


def _reduce_in_vmem(grads, half_rows, name):
    n = len(grads)

    def body(*refs):
        g, mine, theirs = refs[:n], refs[n:2 * n], refs[2 * n:3 * n]
        landed_pair, partial, landed_chips = refs[3 * n:4 * n], refs[4 * n:5 * n], refs[5 * n:6 * n]
        sems = refs[6 * n:]
        x, y, c = _mesh_pos()
        k = 2 * x + y
        sibling = (x, y, 1 - c)

        def half(i, chip_idx, which):
            return pl.ds(pl.multiple_of((2 * chip_idx + which) * half_rows[i], 8), half_rows[i])

        def run(copies):
            for cp in copies:
                cp.start()
            for cp in copies:
                cp.wait_recv()
                cp.wait_send()

        run([_remote(g[i].at[half(i, kk, 1 - c)], landed_pair[i].at[kk], sems[0:2], (kk, i), sibling)
             for kk in range(N_CHIPS) for i in range(n)])
        for i in range(n):
            for kk in range(N_CHIPS):
                partial[i][kk] = (g[i][half(i, kk, c), :] + landed_pair[i][kk]).astype(BF16)
        run([_remote(partial[i].at[kj], landed_chips[i].at[j], sems[2:4], (j, i), (*chip, c))
             for j, (chip, kj) in enumerate(_other_chips(x, y)) for i in range(n)])
        for i in range(n):
            own = g[i][half(i, k, c), :] + landed_pair[i][k]
            mine[i][...] = ((own + landed_chips[i][0].astype(F32)) + landed_chips[i][1].astype(F32)) \
                + landed_chips[i][2].astype(F32)
        run([_remote(mine[i], theirs[i], sems[4:6], i, sibling) for i in range(n)])

    shapes = [(h, gr.shape[1]) for gr, h in zip(grads, half_rows)]
    halves = [jax.ShapeDtypeStruct(s, F32) for s in shapes]
    vmem = pl.BlockSpec(memory_space=pltpu.VMEM)
    scratch = ([pltpu.VMEM((N_CHIPS,) + s, F32) for s in shapes]
               + [pltpu.VMEM((N_CHIPS,) + s, BF16) for s in shapes]
               + [pltpu.VMEM((3,) + s, BF16) for s in shapes]
               + _sem_pairs(N_CHIPS, n) + _sem_pairs(3, n) + _sem_pairs(n))
    out = pl.pallas_call(
        body, name=name, out_shape=halves + halves, in_specs=[vmem] * n, out_specs=[vmem] * (2 * n),
        scratch_shapes=scratch, compiler_params=_params(48),
    )(*grads)
    return out[:n], out[n:]


def _row_tile(rows):
    for t in (256, 128, 64):
        if rows % t == 0:
            return t
    return rows


def _add_pair(core, chip, grad, landed):
    _, h, cols = landed.shape
    tr = _row_tile(h)

    def body(core_ref, chip_ref, g_ref, l_ref, own_ref, ob_ref):
        s = g_ref[...] + l_ref[...]
        ob_ref[...] = s.astype(BF16)

        @pl.when(pl.program_id(1) == chip_ref[0])
        def _():
            own_ref[...] = s

    return _pcall(
        body, name="grad_add_pair",
        out_shape=[jax.ShapeDtypeStruct((h, cols), F32), jax.ShapeDtypeStruct(landed.shape, BF16)],
        grid_spec=pltpu.PrefetchScalarGridSpec(
            num_scalar_prefetch=2, grid=(h // tr, N_CHIPS),
            in_specs=[pl.BlockSpec((None, None, tr, cols), lambda t, k, core_ref, chip_ref: (k, core_ref[0], t, 0)),
                      pl.BlockSpec((None, tr, cols), lambda t, k, core_ref, chip_ref: (k, t, 0))],
            out_specs=[pl.BlockSpec((tr, cols), lambda t, k, core_ref, chip_ref: (t, 0)),
                       pl.BlockSpec((None, tr, cols), lambda t, k, core_ref, chip_ref: (k, t, 0))]),
        compiler_params=_params(32, ("arbitrary", "arbitrary")),
        operands=(core, chip, grad, landed))


def _add_chips(own, landed):
    h, cols = own.shape
    tr = _row_tile(h)

    def body(p_ref, l_ref, o_ref):
        o_ref[...] = ((p_ref[...] + l_ref[0].astype(F32)) + l_ref[1].astype(F32)) + l_ref[2].astype(F32)

    return _pcall(
        body, name="grad_add_chips", grid=(h // tr,),
        out_shape=jax.ShapeDtypeStruct((h, cols), F32),
        in_specs=[pl.BlockSpec((tr, cols), lambda t: (t, 0)), pl.BlockSpec((3, tr, cols), lambda t: (0, t, 0))],
        out_specs=pl.BlockSpec((tr, cols), lambda t: (t, 0)),
        compiler_params=_params(32, ("arbitrary",)),
        operands=(own, landed))


def _ada_project(c_all, w_ada, b_shard):
    n = w_ada.shape[1]
    tn = 512

    def body(c_ref, w_ref, b_ref, mod_ref, cond_ref):
        cv = c_ref[...]
        cond = cv * _sigmoid(cv)
        mod_ref[...] = _dot(cond.astype(BF16), w_ref[...].astype(BF16)) + b_ref[...]
        cond_ref[0:N_DEV, :] = cond
        cond_ref[N_DEV:2 * N_DEV, :] = jnp.zeros_like(cond)

    return _pcall(
        body, name="ada_project", grid=(n // tn,),
        out_shape=[jax.ShapeDtypeStruct((N_DEV, n), F32), jax.ShapeDtypeStruct((2 * N_DEV, D_MODEL), F32)],
        in_specs=[pl.BlockSpec((N_DEV, D_MODEL), lambda j: (0, 0)), pl.BlockSpec((D_MODEL, tn), lambda j: (0, j)),
                  pl.BlockSpec((1, tn), lambda j: (0, j))],
        out_specs=[pl.BlockSpec((N_DEV, tn), lambda j: (0, j)), pl.BlockSpec((2 * N_DEV, D_MODEL), lambda j: (0, 0))],
        compiler_params=_params(32, ("arbitrary",)),
        operands=(c_all, w_ada, b_shard))


def _in_project(x, pos, sc_a, sh_a, w_in, q_norm_w, kv_norm_w, w_q, w_kv, invf, exchange=None):
    t_len = x.shape[0]
    tm = min(512, t_len)

    def body(x_ref, pos_ref, sc_ref, sh_ref, win_ref, qn_ref, kvn_ref, wq_ref, wkv_ref, invf_ref,
             u_ref, zhg_ref, cq_ref, ckv_ref, q_ref, k_ref, kt_ref, v_ref, vt_ref):
        u = (x_ref[...] * (1.0 + sc_ref[...]) + sh_ref[...]).astype(BF16)
        u_ref[...] = u
        z = _dot_nt(u, win_ref[...])
        zhg_ref[...] = z[:, :HG_COLS]
        cq = z[:, HG_COLS:HG_COLS + Q_RANK]
        ckv = z[:, HG_COLS + Q_RANK:HG_COLS + Q_RANK + KV_RANK]
        cq_ref[...] = cq
        ckv_ref[...] = ckv
        cos_t, sin_t = _rope_tables(pos_ref[...], invf_ref[...])
        k_pe = _rope(z[:, HG_COLS + Q_RANK + KV_RANK:], cos_t, sin_t)
        k_pe_t = jnp.transpose(k_pe).astype(BF16)
        cqn = (cq * lax.rsqrt(_rowmean(cq * cq) + RMS_EPS) * qn_ref[...]).astype(BF16)
        ckvn = (ckv * lax.rsqrt(_rowmean(ckv * ckv) + RMS_EPS) * kvn_ref[...]).astype(BF16)
        for h in range(N_HEADS):
            qh = _dot(cqn, wq_ref[h])
            q_ref[h, :, 0:HEAD_DIM] = qh[:, :HEAD_DIM].astype(BF16)
            q_ref[h, :, HEAD_DIM:QK_DIM] = _rope(qh[:, HEAD_DIM:], cos_t, sin_t).astype(BF16)
            kvh = _dot(ckvn, wkv_ref[h])
            k_ref[h, :, 0:HEAD_DIM] = kvh[:, :HEAD_DIM].astype(BF16)
            k_ref[h, :, HEAD_DIM:QK_DIM] = k_pe.astype(BF16)
            kt_ref[h, 0:HEAD_DIM, :] = jnp.transpose(kvh[:, :HEAD_DIM]).astype(BF16)
            kt_ref[h, HEAD_DIM:QK_DIM, :] = k_pe_t
            v_ref[h] = kvh[:, HEAD_DIM:].astype(BF16)
            vt_ref[h] = jnp.transpose(kvh[:, HEAD_DIM:]).astype(BF16)

    row = lambda i: (i, 0)
    fixed2 = lambda i: (0, 0)
    fixed3 = lambda i: (0, 0, 0)
    heads = lambda i: (0, i, 0)
    n_tiles = t_len // tm
    return _pallas(
        body, name="in_project", grid=(n_tiles,),
        operands=(x, pos, sc_a, sh_a, w_in, q_norm_w, kv_norm_w, w_q, w_kv, invf),
        out_shape=[jax.ShapeDtypeStruct((t_len, D_MODEL), BF16), jax.ShapeDtypeStruct((t_len, HG_COLS), F32),
                   jax.ShapeDtypeStruct((t_len, Q_RANK), F32), jax.ShapeDtypeStruct((t_len, KV_RANK), F32),
                   jax.ShapeDtypeStruct((N_HEADS, t_len, QK_DIM), BF16),
                   jax.ShapeDtypeStruct((N_HEADS, t_len, QK_DIM), BF16),
                   jax.ShapeDtypeStruct((N_HEADS, QK_DIM, t_len), BF16),
                   jax.ShapeDtypeStruct((N_HEADS, t_len, HEAD_DIM), BF16),
                   jax.ShapeDtypeStruct((N_HEADS, HEAD_DIM, t_len), BF16)],
        in_specs=[pl.BlockSpec((tm, D_MODEL), row), pl.BlockSpec((tm, 1), row),
                  pl.BlockSpec((1, D_MODEL), fixed2), pl.BlockSpec((1, D_MODEL), fixed2),
                  pl.BlockSpec((IN_COLS_PAD, D_MODEL), fixed2),
                  pl.BlockSpec((1, Q_RANK), fixed2), pl.BlockSpec((1, KV_RANK), fixed2),
                  pl.BlockSpec((N_HEADS, Q_RANK, QK_DIM), fixed3), pl.BlockSpec((N_HEADS, KV_RANK, 2 * HEAD_DIM), fixed3),
                  pl.BlockSpec((1, 128), fixed2)],
        out_specs=[pl.BlockSpec((tm, D_MODEL), row), pl.BlockSpec((tm, HG_COLS), row),
                   pl.BlockSpec((tm, Q_RANK), row), pl.BlockSpec((tm, KV_RANK), row),
                   pl.BlockSpec((N_HEADS, tm, QK_DIM), heads), pl.BlockSpec((N_HEADS, tm, QK_DIM), heads),
                   pl.BlockSpec((N_HEADS, QK_DIM, tm), lambda i: (0, 0, i)),
                   pl.BlockSpec((N_HEADS, tm, HEAD_DIM), heads),
                   pl.BlockSpec((N_HEADS, HEAD_DIM, tm), lambda i: (0, 0, i))],
        params=_params(48, ("arbitrary",)), exchange=exchange,
        first=lambda: pl.program_id(0) == 0, last=lambda: pl.program_id(0) == n_tiles - 1)


def _lower_bound(lb_raw):
    m = jnp.max(lb_raw, axis=0, keepdims=True)
    e = jnp.exp(lb_raw - m)
    return e[0:1] / jnp.sum(e, axis=0, keepdims=True)


def _tri(inclusive_lower):
    r = lax.broadcasted_iota(jnp.int32, (HG_CHUNK, HG_CHUNK), 0)
    c = lax.broadcasted_iota(jnp.int32, (HG_CHUNK, HG_CHUNK), 1)
    return (c <= r) if inclusive_lower else (c >= r)


def _chunk_rows(n):
    return slice(n * HG_CHUNK, (n + 1) * HG_CHUNK)


def _chunk_prefix_sums(v, inclusive_lower):
    tri = _tri(inclusive_lower).astype(BF16)
    hi = v.astype(BF16)
    rest = v - hi.astype(F32)
    mid = rest.astype(BF16)
    lo = (rest - mid.astype(F32)).astype(BF16)
    pieces = jnp.concatenate([hi, mid, lo], axis=1)
    out = []
    for n in range(v.shape[0] // HG_CHUNK):
        s = _dot(tri, pieces[_chunk_rows(n)])
        out.append((s[:, 0:HEAD_DIM] + s[:, HEAD_DIM:2 * HEAD_DIM]) + s[:, 2 * HEAD_DIM:])
    return jnp.concatenate(out, axis=0)


def _per_chunk(v, row):
    n = v.shape[0] // HG_CHUNK
    v3 = v.reshape(n, HG_CHUNK, HEAD_DIM)
    return jnp.broadcast_to(v3[:, row:row + 1, :], v3.shape).reshape(v.shape)


def _hg_block(q, f_logit, lb):
    sg = _sigmoid(f_logit)
    forget = lb + (1.0 - lb) * sg
    kk = 1.0 - forget
    b = _chunk_prefix_sums(jnp.log(forget), True)
    b_ref = _per_chunk(b, HG_CHUNK // 2 - 1)
    b_last = _per_chunk(b, HG_CHUNK - 1)
    e_i = jnp.exp(b - b_ref)
    e_ri = jnp.exp(b_ref - b)
    e_b = jnp.exp(b)
    e_l = jnp.exp(b_last - b)
    return dict(sg=sg, forget=forget, e_i=e_i, e_ri=e_ri, e_b=e_b, e_l=e_l, dec=jnp.exp(b_last),
                qi=q * e_i, ki=kk * e_ri, qe=q * e_b, kl=kk * e_l)


HG_STEP_HEADS = 4


def _head_cols(hh):
    return slice(hh * HEAD_DIM, (hh + 1) * HEAD_DIM)


def _hgrn_forward(zhg, lb_raw, norm_w, exchange=None):
    t_len = zhg.shape[0]
    tb = min(512, t_len)
    n_chunks = tb // HG_CHUNK
    hs = HG_STEP_HEADS

    def body(q_ref, f_ref, v_ref, g_ref, lb_ref, w_ref, opre_ref, o_ref, st_ref, state):
        @pl.when(pl.program_id(1) == 0)
        def _():
            state[...] = jnp.zeros_like(state)

        causal = _tri(True)
        heads = range(hs)
        blk, v, qi, ki, qe, kl = {}, {}, {}, {}, {}, {}
        for hh in heads:
            cols = _head_cols(hh)
            blk[hh] = _hg_block(q_ref[:, cols], f_ref[:, cols], _lower_bound(lb_ref[:, cols]))
            v[hh] = v_ref[:, cols].astype(BF16)
            qi[hh], ki[hh], qe[hh], kl[hh] = (blk[hh][name].astype(BF16) for name in ("qi", "ki", "qe", "kl"))
        st = {hh: state[hh] for hh in heads}
        parts = {hh: [] for hh in heads}
        for n in range(n_chunks):
            r = _chunk_rows(n)
            for hh in heads:
                a = jnp.where(causal, _dot_nt(qi[hh][r], ki[hh][r]), 0.0).astype(BF16)
                st_ref[hh, n] = st[hh]
                parts[hh].append(_dot(a, v[hh][r]) + _dot_nt(qe[hh][r], st[hh].astype(BF16)))
                st[hh] = st[hh] * blk[hh]["dec"][n * HG_CHUNK:n * HG_CHUNK + 1] + _dot_tn(v[hh][r], kl[hh][r])
        for hh in heads:
            cols = _head_cols(hh)
            state[hh] = st[hh]
            o = jnp.concatenate(parts[hh], axis=0)
            opre_ref[:, cols] = o
            g = g_ref[:, cols]
            o_ref[:, cols] = o * lax.rsqrt(_rowmean(o * o) + RMS_EPS) * w_ref[:, cols] * (g * _sigmoid(g))

    groups = N_HEADS // hs
    wide = hs * HEAD_DIM
    col = lambda off: (lambda h, t: (t, off + h))
    nb = t_len // tb
    return _pallas(
        body, name="hgrn_forward", grid=(groups, nb), operands=(zhg, zhg, zhg, zhg, lb_raw, norm_w),
        out_shape=[jax.ShapeDtypeStruct((t_len, N_HEADS * HEAD_DIM), F32),
                   jax.ShapeDtypeStruct((t_len, N_HEADS * HEAD_DIM), F32),
                   jax.ShapeDtypeStruct((N_HEADS, t_len // HG_CHUNK, HEAD_DIM, HEAD_DIM), F32)],
        in_specs=[pl.BlockSpec((tb, wide), col(0)), pl.BlockSpec((tb, wide), col(groups)),
                  pl.BlockSpec((tb, wide), col(2 * groups)), pl.BlockSpec((tb, wide), col(3 * groups)),
                  pl.BlockSpec((2, wide), lambda h, t: (0, h)), pl.BlockSpec((1, wide), lambda h, t: (0, h))],
        out_specs=[pl.BlockSpec((tb, wide), col(0)), pl.BlockSpec((tb, wide), col(0)),
                   pl.BlockSpec((hs, n_chunks, HEAD_DIM, HEAD_DIM), lambda h, t: (h, t, 0, 0))],
        scratch_shapes=[pltpu.VMEM((hs, HEAD_DIM, HEAD_DIM), F32)],
        params=_params(40, ("arbitrary", "arbitrary")), exchange=exchange,
        first=lambda: (pl.program_id(0) == 0) & (pl.program_id(1) == 0),
        last=lambda: (pl.program_id(0) == groups - 1) & (pl.program_id(1) == nb - 1))


def _hgrn_backward(zhg, lb_raw, norm_w, o_pre, d_cat, states, exchange=None):
    t_len = zhg.shape[0]
    tb = min(512, t_len)
    n_chunks = tb // HG_CHUNK
    nb = t_len // tb
    hs = HG_STEP_HEADS

    def body(q_ref, f_ref, v_ref, g_ref, lb_ref, w_ref, opre_ref, do_ref, st_ref,
             dq_ref, df_ref, dv_ref, dg_ref, sums_ref, gstate):
        @pl.when(pl.program_id(1) == 0)
        def _():
            gstate[...] = jnp.zeros_like(gstate)
            sums_ref[...] = jnp.zeros_like(sums_ref)

        heads = range(hs)
        causal = _tri(True)
        row_id = lax.broadcasted_iota(jnp.int32, (HG_CHUNK, HEAD_DIM), 0)
        lb, d_o, blk, v, qi, ki, qe, kl = ({} for _ in range(8))
        for hh in heads:
            cols = _head_cols(hh)
            lb[hh] = _lower_bound(lb_ref[:, cols])
            w = w_ref[:, cols]
            o = opre_ref[:, cols]
            g = g_ref[:, cols]
            d_out = do_ref[:, cols]
            r = lax.rsqrt(_rowmean(o * o) + RMS_EPS)
            sg_g = _sigmoid(g)
            dg_ref[:, cols] = (d_out * (o * r * w) * (sg_g * (1.0 + g * (1.0 - sg_g)))).astype(BF16)
            d_on = d_out * (g * sg_g)
            sums_ref[1:2, cols] += _colsum(d_on * o * r)
            dy = d_on * w
            d_o[hh] = (r * dy - o * (r * r * r) * _rowmean(dy * o)).astype(BF16)
            blk[hh] = _hg_block(q_ref[:, cols], f_ref[:, cols], lb[hh])
            v[hh] = v_ref[:, cols].astype(BF16)
            qi[hh], ki[hh], qe[hh], kl[hh] = (blk[hh][name].astype(BF16) for name in ("qi", "ki", "qe", "kl"))
        gt = {hh: gstate[hh] for hh in heads}
        d_v, d_qi, d_ki, d_qe, d_kl, d_dec = ({hh: [None] * n_chunks for hh in heads} for _ in range(6))
        for n in reversed(range(n_chunks)):
            rows = _chunk_rows(n)
            for hh in heads:
                st = st_ref[hh, n]
                a = jnp.where(causal, _dot_nt(qi[hh][rows], ki[hh][rows]), 0.0).astype(BF16)
                d_a = jnp.where(causal, _dot_nt(d_o[hh][rows], v[hh][rows]), 0.0).astype(BF16)
                gt_b = gt[hh].astype(BF16)
                d_v[hh][n] = _dot_tn(a, d_o[hh][rows]) + _dot_nt(kl[hh][rows], gt_b)
                d_qi[hh][n] = _dot(d_a, ki[hh][rows])
                d_ki[hh][n] = _dot_tn(d_a, qi[hh][rows])
                d_qe[hh][n] = _dot(d_o[hh][rows], st.astype(BF16))
                d_kl[hh][n] = _dot(v[hh][rows], gt_b)
                d_dec[hh][n] = jnp.where(row_id == HG_CHUNK - 1, _colsum(gt[hh] * st), 0.0)
                gt[hh] = gt[hh] * blk[hh]["dec"][n * HG_CHUNK:n * HG_CHUNK + 1] + _dot_tn(d_o[hh][rows], qe[hh][rows])
        for hh in heads:
            cols = _head_cols(hh)
            b = blk[hh]
            gstate[hh] = gt[hh]
            dqi, dki, dqe, dkl, ddec = (jnp.concatenate(p[hh], axis=0) for p in (d_qi, d_ki, d_qe, d_kl, d_dec))
            dv_ref[:, cols] = jnp.concatenate(d_v[hh], axis=0).astype(BF16)
            dq_ref[:, cols] = (dqi * b["e_i"] + dqe * b["e_b"]).astype(BF16)
            d_k = dki * b["e_ri"] + dkl * b["e_l"]
            t_qi = dqi * b["qi"]
            t_ki = dki * b["ki"]
            t_kl = dkl * b["kl"]
            at_ref, at_last = [], []
            for n in range(n_chunks):
                rows = _chunk_rows(n)
                at_ref.append(jnp.where(row_id == HG_CHUNK // 2 - 1, _colsum(t_ki[rows] - t_qi[rows]), 0.0))
                at_last.append(jnp.where(row_id == HG_CHUNK - 1, _colsum(t_kl[rows]), 0.0))
            d_b = (t_qi - t_ki + dqe * b["qe"] - t_kl + jnp.concatenate(at_ref, axis=0)
                   + jnp.concatenate(at_last, axis=0) + ddec * b["dec"])
            d_forget = _chunk_prefix_sums(d_b, False) / b["forget"] - d_k
            sg = b["sg"]
            df_ref[:, cols] = (d_forget * (1.0 - lb[hh]) * sg * (1.0 - sg)).astype(BF16)
            sums_ref[0:1, cols] += _colsum(d_forget * (1.0 - sg))

    groups = N_HEADS // hs
    wide = hs * HEAD_DIM
    col = lambda off: (lambda h, t: (nb - 1 - t, off + h))
    return _pallas(
        body, name="hgrn_backward", grid=(groups, nb),
        operands=(zhg, zhg, zhg, zhg, lb_raw, norm_w, o_pre, d_cat, states),
        out_shape=[jax.ShapeDtypeStruct((t_len, N_HEADS * HEAD_DIM), BF16)] * 4
        + [jax.ShapeDtypeStruct((8, N_HEADS * HEAD_DIM), F32)],
        in_specs=[pl.BlockSpec((tb, wide), col(0)), pl.BlockSpec((tb, wide), col(groups)),
                  pl.BlockSpec((tb, wide), col(2 * groups)), pl.BlockSpec((tb, wide), col(3 * groups)),
                  pl.BlockSpec((2, wide), lambda h, t: (0, h)), pl.BlockSpec((1, wide), lambda h, t: (0, h)),
                  pl.BlockSpec((tb, wide), col(0)), pl.BlockSpec((tb, wide), col(0)),
                  pl.BlockSpec((hs, n_chunks, HEAD_DIM, HEAD_DIM), lambda h, t: (h, nb - 1 - t, 0, 0))],
        out_specs=[pl.BlockSpec((tb, wide), col(0))] * 4 + [pl.BlockSpec((8, wide), lambda h, t: (0, h))],
        scratch_shapes=[pltpu.VMEM((hs, HEAD_DIM, HEAD_DIM), F32)],
        params=_params(40, ("arbitrary", "arbitrary")), exchange=exchange,
        first=lambda: (pl.program_id(0) == 0) & (pl.program_id(1) == 0),
        last=lambda: (pl.program_id(0) == groups - 1) & (pl.program_id(1) == nb - 1))


ATT_LOG2 = ATT_SCALE * 1.4426950408889634


def _triangle_steps(nq, q_major):
    if q_major:
        pairs = [(i, j) for i in range(nq) for j in range(i + 1)]
    else:
        pairs = [(i, j) for j in range(nq) for i in range(j, nq)]
    return jnp.array([p[0] for p in pairs], jnp.int32), jnp.array([p[1] for p in pairs], jnp.int32)


def _key_le_query(t):
    return lax.broadcasted_iota(jnp.int32, (t, t), 0) <= lax.broadcasted_iota(jnp.int32, (t, t), 1)


def _attention_forward(q, k, v_t, exchange=None):
    t_len = q.shape[1]
    tq = min(512, t_len)
    nq = t_len // tq
    qi_tab, ki_tab = _triangle_steps(nq, True)

    def body(qi_ref, ki_ref, q_ref, k_ref, vt_ref, o_ref, lse_ref, m_s, l_s, acc_s):
        step = pl.program_id(0)
        qi, ki = qi_ref[step], ki_ref[step]

        @pl.when(ki == 0)
        def _():
            m_s[...] = jnp.full_like(m_s, NEG_BIG)
            l_s[...] = jnp.zeros_like(l_s)
            acc_s[...] = jnp.zeros_like(acc_s)

        def accumulate(masked):
            for h in range(N_HEADS):
                s_t = _dot_nt(k_ref[h], q_ref[h]) * ATT_LOG2
                if masked:
                    s_t = jnp.where(_key_le_query(tq), s_t, NEG_BIG)
                m_old = m_s[h]
                m_new = jnp.maximum(m_old, jnp.max(s_t, axis=0, keepdims=True))
                alpha = jnp.exp2(m_old - m_new)
                p_t = jnp.exp2(s_t - m_new)
                l_s[h] = alpha * l_s[h] + jnp.sum(p_t, axis=0, keepdims=True)
                acc_s[h] = alpha * acc_s[h] + _dot(vt_ref[h], p_t.astype(BF16))
                m_s[h] = m_new

        @pl.when(ki < qi)
        def _():
            accumulate(False)

        @pl.when(ki == qi)
        def _():
            accumulate(True)
            for h in range(N_HEADS):
                o_ref[:, h * HEAD_DIM:(h + 1) * HEAD_DIM] = jnp.transpose(acc_s[h] / l_s[h])
                lse_ref[h] = m_s[h] + jnp.log2(l_s[h])

    n_steps = qi_tab.shape[0]
    return _pallas(
        body, name="attention_forward", grid=(n_steps,), prefetch=(qi_tab, ki_tab), operands=(q, k, v_t),
        out_shape=[jax.ShapeDtypeStruct((t_len, N_HEADS * HEAD_DIM), F32),
                   jax.ShapeDtypeStruct((N_HEADS, 1, t_len), F32)],
        in_specs=[pl.BlockSpec((N_HEADS, tq, QK_DIM), lambda s, qt, kt: (0, qt[s], 0)),
                  pl.BlockSpec((N_HEADS, tq, QK_DIM), lambda s, qt, kt: (0, kt[s], 0)),
                  pl.BlockSpec((N_HEADS, HEAD_DIM, tq), lambda s, qt, kt: (0, 0, kt[s]))],
        out_specs=[pl.BlockSpec((tq, N_HEADS * HEAD_DIM), lambda s, qt, kt: (qt[s], 0)),
                   pl.BlockSpec((N_HEADS, 1, tq), lambda s, qt, kt: (0, 0, qt[s]))],
        scratch_shapes=[pltpu.VMEM((N_HEADS, 1, tq), F32), pltpu.VMEM((N_HEADS, 1, tq), F32),
                        pltpu.VMEM((N_HEADS, HEAD_DIM, tq), F32)],
        params=_params(48, ("arbitrary",)), exchange=exchange,
        first=lambda qt, kt: pl.program_id(0) == 0, last=lambda qt, kt: pl.program_id(0) == n_steps - 1)


BWD_HEADS = 4


def _attention_backward(q, k, k_t, v, d_cat, lse, delta, exchange=None):
    t_len = q.shape[1]
    tq = min(512, t_len)
    nq = t_len // tq
    hp = BWD_HEADS
    qi_tab, ki_tab = _triangle_steps(nq, False)

    def body(qi_ref, ki_ref, q_ref, k_ref, kt_ref, v_ref, do_ref, lse_ref, delta_ref, dqt_hbm, dk_ref, dv_ref,
             dqt_s, dk_s, dv_s):
        group, step = pl.program_id(0), pl.program_id(1)
        qi, ki = qi_ref[step], ki_ref[step]

        @pl.when(step == 0)
        def _():
            dqt_s[...] = jnp.zeros_like(dqt_s)

        @pl.when(qi == ki)
        def _():
            dk_s[...] = jnp.zeros_like(dk_s)
            dv_s[...] = jnp.zeros_like(dv_s)

        def accumulate(masked):
            for h in range(hp):
                do_b = do_ref[:, h * HEAD_DIM:(h + 1) * HEAD_DIM].astype(BF16)
                s_t = _dot_nt(k_ref[h], q_ref[h]) * ATT_LOG2
                if masked:
                    s_t = jnp.where(_key_le_query(tq), s_t, NEG_BIG)
                p_t = jnp.exp2(s_t - lse_ref[h])
                dp_t = _dot_nt(v_ref[h], do_b)
                ds_t = (p_t * (dp_t - delta_ref[h]) * ATT_SCALE).astype(BF16)
                dv_s[h] += _dot(p_t.astype(BF16), do_b)
                dk_s[h] += _dot(ds_t, q_ref[h])
                dqt_s[h, qi] += _dot(kt_ref[h], ds_t)

        @pl.when(ki < qi)
        def _():
            accumulate(False)

        @pl.when(ki == qi)
        def _():
            accumulate(True)
            for h in range(hp):
                pltpu.sync_copy(dqt_s.at[h, qi], dqt_hbm.at[group * hp + h, qi])

        @pl.when(qi == nq - 1)
        def _():
            dk_ref[...] = dk_s[...]
            dv_ref[...] = dv_s[...]

    wide = hp * HEAD_DIM
    n_groups, n_steps = N_HEADS // hp, qi_tab.shape[0]
    return _pallas(
        body, name="attention_backward", grid=(n_groups, n_steps), prefetch=(qi_tab, ki_tab),
        operands=(q, k, k_t, v, d_cat, lse, delta),
        out_shape=[jax.ShapeDtypeStruct((N_HEADS, nq, QK_DIM, tq), F32),
                   jax.ShapeDtypeStruct((N_HEADS, t_len, QK_DIM), F32),
                   jax.ShapeDtypeStruct((N_HEADS, t_len, HEAD_DIM), F32)],
        in_specs=[pl.BlockSpec((hp, tq, QK_DIM), lambda g, s, qt, kt: (g, qt[s], 0)),
                  pl.BlockSpec((hp, tq, QK_DIM), lambda g, s, qt, kt: (g, kt[s], 0)),
                  pl.BlockSpec((hp, QK_DIM, tq), lambda g, s, qt, kt: (g, 0, kt[s])),
                  pl.BlockSpec((hp, tq, HEAD_DIM), lambda g, s, qt, kt: (g, kt[s], 0)),
                  pl.BlockSpec((tq, wide), lambda g, s, qt, kt: (qt[s], n_groups + g)),
                  pl.BlockSpec((hp, 1, tq), lambda g, s, qt, kt: (g, 0, qt[s])),
                  pl.BlockSpec((hp, 1, tq), lambda g, s, qt, kt: (g, 0, qt[s]))],
        out_specs=[pl.BlockSpec(memory_space=pl.ANY),
                   pl.BlockSpec((hp, tq, QK_DIM), lambda g, s, qt, kt: (g, kt[s], 0)),
                   pl.BlockSpec((hp, tq, HEAD_DIM), lambda g, s, qt, kt: (g, kt[s], 0))],
        scratch_shapes=[pltpu.VMEM((hp, nq, QK_DIM, tq), F32), pltpu.VMEM((hp, tq, QK_DIM), F32),
                        pltpu.VMEM((hp, tq, HEAD_DIM), F32)],
        params=_params(58, ("arbitrary", "arbitrary")), exchange=exchange,
        first=lambda qt, kt: (pl.program_id(0) == 0) & (pl.program_id(1) == 0),
        last=lambda qt, kt: (pl.program_id(0) == n_groups - 1) & (pl.program_id(1) == n_steps - 1))


def _out_project(o_hg, o_mla, x, g_a, w_out, exchange=None):
    t_len = x.shape[0]
    tm = min(512, t_len)
    half = N_HEADS * HEAD_DIM

    def body(ohg_ref, omla_ref, x_ref, ga_ref, w_ref, cat_ref, mix_ref, xhat_ref, rstd_ref):
        a = ohg_ref[...].astype(BF16)
        b = omla_ref[...].astype(BF16)
        cat_ref[:, 0:half] = a
        cat_ref[:, half:2 * half] = b
        mix = _dot(a, w_ref[0:half, :]) + _dot(b, w_ref[half:2 * half, :])
        mix_ref[...] = mix
        r1 = DN_ALPHA * x_ref[...] + (1.0 + ga_ref[...]) * mix
        xc = r1 - _rowmean(r1)
        rstd = lax.rsqrt(_rowmean(xc * xc) + LN_EPS)
        xhat_ref[...] = xc * rstd
        rstd_ref[...] = rstd

    row = lambda i: (i, 0)
    fixed = lambda i: (0, 0)
    n_tiles = t_len // tm
    return _pallas(
        body, name="out_project", grid=(n_tiles,), operands=(o_hg, o_mla, x, g_a, w_out),
        out_shape=[jax.ShapeDtypeStruct((t_len, D_MODEL), BF16), jax.ShapeDtypeStruct((t_len, D_MODEL), F32),
                   jax.ShapeDtypeStruct((t_len, D_MODEL), F32), jax.ShapeDtypeStruct((t_len, 1), F32)],
        in_specs=[pl.BlockSpec((tm, half), row), pl.BlockSpec((tm, half), row), pl.BlockSpec((tm, D_MODEL), row),
                  pl.BlockSpec((1, D_MODEL), fixed), pl.BlockSpec((D_MODEL, D_MODEL), fixed)],
        out_specs=[pl.BlockSpec((tm, D_MODEL), row), pl.BlockSpec((tm, D_MODEL), row),
                   pl.BlockSpec((tm, D_MODEL), row), pl.BlockSpec((tm, 1), row)],
        params=_params(48, ("arbitrary",)), exchange=exchange,
        first=lambda: pl.program_id(0) == 0, last=lambda: pl.program_id(0) == n_tiles - 1)


V_LN1G, V_LN1B, V_SCM, V_SHM, V_GM, V_GA, V_LN2G, V_LN2B = range(8)
S_DLN2G, S_DLN2B, S_DGM, S_DSCM, S_DSHM, S_DLN1G, S_DLN1B, S_DGA, S_LOSS = range(9)


def _mlp_and_back(xhat1, rstd1, mix, target, o_mla, vecs, w1_top, w1_bottom, w2, w_out):
    t_len = xhat1.shape[0]
    tm = min(256, t_len)
    n_ff = w1_top.shape[0]
    ff = w1_top.shape[2]
    top_rows = w1_top.shape[1]

    def body(xhat_ref, rstd_ref, mix_ref, tgt_ref, omla_ref, vec_ref, w1_top_hbm, w1_bottom_hbm, w2_hbm, wout_hbm,
             act_ref, dhp_ref, um_ref, dh_ref, dmix_ref, dcat_ref, dr1_ref, sums_ref, delta_ref,
             w1_s, w2_s, wout_s, hp_s, load_sems):
        @pl.when(pl.program_id(0) == 0)
        def _():
            loads = [pltpu.make_async_copy(w1_top_hbm, w1_s.at[:, 0:top_rows], load_sems.at[0]),
                     pltpu.make_async_copy(w1_bottom_hbm, w1_s.at[:, top_rows:D_MODEL], load_sems.at[3]),
                     pltpu.make_async_copy(w2_hbm, w2_s, load_sems.at[1]),
                     pltpu.make_async_copy(wout_hbm, wout_s, load_sems.at[2])]
            for cp in loads:
                cp.start()
            sums_ref[...] = jnp.zeros_like(sums_ref)
            for cp in loads:
                cp.wait()

        vec = lambda r: vec_ref[r:r + 1, :]
        xhat = xhat_ref[...]
        x1 = xhat * vec(V_LN1G) + vec(V_LN1B)
        um = (x1 * (1.0 + vec(V_SCM)) + vec(V_SHM)).astype(BF16)
        um_ref[...] = um
        h = jnp.zeros((tm, D_MODEL), F32)
        for j in range(n_ff):
            hp = _dot(um, w1_s[j])
            hp_s[j] = hp
            act = jnp.square(jnp.maximum(hp, 0.0)).astype(BF16)
            act_ref[:, j * ff:(j + 1) * ff] = act
            h = h + _dot(act, w2_s[j])
        r2 = DN_ALPHA * x1 + (1.0 + vec(V_GM)) * h
        xc = r2 - _rowmean(r2)
        rstd2 = lax.rsqrt(_rowmean(xc * xc) + LN_EPS)
        xhat2 = xc * rstd2
        err = xhat2 * vec(V_LN2G) + vec(V_LN2B) - tgt_ref[...]
        loss = 0.5 * jnp.sum(_rowmean(err * err))
        dy = err * (1.0 / D_MODEL)
        dxh = dy * vec(V_LN2G)
        dr2 = rstd2 * (dxh - _rowmean(dxh) - xhat2 * _rowmean(dxh * xhat2))
        dh = ((1.0 + vec(V_GM)) * dr2).astype(BF16)
        dh_ref[...] = dh
        sums_ref[S_DLN2G:S_DLN2G + 1, :] += _colsum(dy * xhat2)
        sums_ref[S_DLN2B:S_DLN2B + 1, :] += _colsum(dy)
        sums_ref[S_DGM:S_DGM + 1, :] += _colsum(dr2 * h)
        sums_ref[S_LOSS:S_LOSS + 1, :] += jnp.full((1, D_MODEL), loss, F32)
        du = jnp.zeros((tm, D_MODEL), F32)
        for j in range(n_ff):
            dhp = (_dot_nt(dh, w2_s[j]) * (2.0 * jnp.maximum(hp_s[j], 0.0))).astype(BF16)
            dhp_ref[:, j * ff:(j + 1) * ff] = dhp
            du = du + _dot_nt(dhp, w1_s[j])
        sums_ref[S_DSCM:S_DSCM + 1, :] += _colsum(du * x1)
        sums_ref[S_DSHM:S_DSHM + 1, :] += _colsum(du)
        dx1 = DN_ALPHA * dr2 + du * (1.0 + vec(V_SCM))
        sums_ref[S_DLN1G:S_DLN1G + 1, :] += _colsum(dx1 * xhat)
        sums_ref[S_DLN1B:S_DLN1B + 1, :] += _colsum(dx1)
        dxh1 = dx1 * vec(V_LN1G)
        dr1 = rstd_ref[...] * (dxh1 - _rowmean(dxh1) - xhat * _rowmean(dxh1 * xhat))
        dr1_ref[...] = dr1
        sums_ref[S_DGA:S_DGA + 1, :] += _colsum(dr1 * mix_ref[...])
        dmix = ((1.0 + vec(V_GA)) * dr1).astype(BF16)
        dmix_ref[...] = dmix
        dcat = _dot_nt(dmix, wout_s[...])
        dcat_ref[...] = dcat
        ones = jnp.ones((8, HEAD_DIM), F32)
        half = N_HEADS * HEAD_DIM
        for hd in range(N_HEADS):
            prod = dcat[:, half + hd * HEAD_DIM:half + (hd + 1) * HEAD_DIM] * omla_ref[:, hd * HEAD_DIM:(hd + 1) * HEAD_DIM]
            delta_ref[hd] = lax.dot_general(ones, prod, (((1,), (1,)), ((), ())), preferred_element_type=F32,
                                            precision=lax.Precision.HIGHEST)[0:1]

    row = lambda i: (i, 0)
    fixed = lambda i: (0, 0)
    any_spec = pl.BlockSpec(memory_space=pl.ANY)
    return _pcall(
        body, name="mlp_and_back", grid=(t_len // tm,),
        out_shape=[jax.ShapeDtypeStruct((t_len, D_FF), BF16), jax.ShapeDtypeStruct((t_len, D_FF), BF16),
                   jax.ShapeDtypeStruct((t_len, D_MODEL), BF16), jax.ShapeDtypeStruct((t_len, D_MODEL), BF16),
                   jax.ShapeDtypeStruct((t_len, D_MODEL), BF16), jax.ShapeDtypeStruct((t_len, D_MODEL), F32),
                   jax.ShapeDtypeStruct((t_len, D_MODEL), F32), jax.ShapeDtypeStruct((16, D_MODEL), F32),
                   jax.ShapeDtypeStruct((N_HEADS, 1, t_len), F32)],
        in_specs=[pl.BlockSpec((tm, D_MODEL), row), pl.BlockSpec((tm, 1), row), pl.BlockSpec((tm, D_MODEL), row),
                  pl.BlockSpec((tm, D_MODEL), row), pl.BlockSpec((tm, N_HEADS * HEAD_DIM), row),
                  pl.BlockSpec((8, D_MODEL), fixed), any_spec, any_spec, any_spec, any_spec],
        out_specs=[pl.BlockSpec((tm, D_FF), row), pl.BlockSpec((tm, D_FF), row), pl.BlockSpec((tm, D_MODEL), row),
                   pl.BlockSpec((tm, D_MODEL), row), pl.BlockSpec((tm, D_MODEL), row), pl.BlockSpec((tm, D_MODEL), row),
                   pl.BlockSpec((tm, D_MODEL), row), pl.BlockSpec((16, D_MODEL), fixed),
                   pl.BlockSpec((N_HEADS, 1, tm), lambda i: (0, 0, i))],
        scratch_shapes=[pltpu.VMEM((n_ff, D_MODEL, ff), BF16), pltpu.VMEM(w2.shape, BF16), pltpu.VMEM(w_out.shape, BF16),
                        pltpu.VMEM((n_ff, tm, ff), F32), pltpu.SemaphoreType.DMA((4,))],
        compiler_params=_params(56, ("arbitrary",)),
        operands=(xhat1, rstd1, mix, target, o_mla, vecs, w1_top, w1_bottom, w2, w_out))


def _in_project_backward(dq, dk, dv, cq, ckv, pos, invf, q_norm_w, kv_norm_w, w_q, w_kv,
                         d_hq, d_hf, d_hi, d_hg, w_in, dr1, x, sc_a, exchange=None):
    t_len = x.shape[0]
    tm = min(512, t_len)
    per_q = dq.shape[3] // tm
    hgw = N_HEADS * HEAD_DIM

    def body(dq_ref, dk_ref, dv_ref, cq_ref, ckv_ref, pos_ref, invf_ref, qn_ref, kvn_ref, wq_ref, wkv_ref,
             dhq_ref, dhf_ref, dhi_ref, dhg_ref, win_ref, dr1_ref, x_ref, sc_ref,
             dz_ref, dqf_ref, dkvu_ref, cqn_ref, ckvn_ref, gx_ref, sums_ref):
        @pl.when(pl.program_id(0) == 0)
        def _():
            sums_ref[...] = jnp.zeros_like(sums_ref)

        cos_t, sin_t = _rope_tables(pos_ref[...], invf_ref[...])
        cq = cq_ref[...]
        ckv = ckv_ref[...]
        rq = lax.rsqrt(_rowmean(cq * cq) + RMS_EPS)
        rkv = lax.rsqrt(_rowmean(ckv * ckv) + RMS_EPS)
        cqn_ref[...] = (cq * rq * qn_ref[...]).astype(BF16)
        ckvn_ref[...] = (ckv * rkv * kvn_ref[...]).astype(BF16)
        d_cqn = jnp.zeros((tm, Q_RANK), F32)
        d_ckvn = jnp.zeros((tm, KV_RANK), F32)
        d_kpe = jnp.zeros((tm, 128), F32)
        for h in range(N_HEADS):
            dqh = jnp.transpose(dq_ref[h])
            dqf_ref[h, :, 0:HEAD_DIM] = dqh[:, :HEAD_DIM].astype(BF16)
            dqf_ref[h, :, HEAD_DIM:QK_DIM] = _unrope(dqh[:, HEAD_DIM:], cos_t, sin_t).astype(BF16)
            d_cqn = d_cqn + _dot_nt(dqf_ref[h], wq_ref[h])
            dkh = dk_ref[h]
            d_kpe = d_kpe + dkh[:, HEAD_DIM:]
            dkvu_ref[h, :, 0:HEAD_DIM] = dkh[:, :HEAD_DIM].astype(BF16)
            dkvu_ref[h, :, HEAD_DIM:2 * HEAD_DIM] = dv_ref[h].astype(BF16)
            d_ckvn = d_ckvn + _dot_nt(dkvu_ref[h], wkv_ref[h])
        dyq = d_cqn * qn_ref[...]
        dykv = d_ckvn * kvn_ref[...]
        sums_ref[2:3, 0:Q_RANK] += _colsum(d_cqn * cq * rq)
        sums_ref[3:4, 0:KV_RANK] += _colsum(d_ckvn * ckv * rkv)
        dz_ref[:, 0:hgw] = dhq_ref[...]
        dz_ref[:, hgw:2 * hgw] = dhf_ref[...]
        dz_ref[:, 2 * hgw:3 * hgw] = dhi_ref[...]
        dz_ref[:, 3 * hgw:4 * hgw] = dhg_ref[...]
        dz_ref[:, HG_COLS:HG_COLS + Q_RANK] = (rq * dyq - cq * (rq * rq * rq) * _rowmean(dyq * cq)).astype(BF16)
        dz_ref[:, HG_COLS + Q_RANK:HG_COLS + Q_RANK + KV_RANK] = (
            rkv * dykv - ckv * (rkv * rkv * rkv) * _rowmean(dykv * ckv)).astype(BF16)
        dz_ref[:, HG_COLS + Q_RANK + KV_RANK:] = _unrope(d_kpe, cos_t, sin_t).astype(BF16)
        du = _dot(dz_ref[...], win_ref[...])
        xv = x_ref[...]
        gx_ref[...] = DN_ALPHA * dr1_ref[...] + (1.0 + sc_ref[...]) * du
        sums_ref[0:1, :] += _colsum(du * xv)
        sums_ref[1:2, :] += _colsum(du)

    row = lambda i: (i, 0)
    fixed2 = lambda i: (0, 0)
    fixed3 = lambda i: (0, 0, 0)
    heads = lambda i: (0, i, 0)
    n_tiles = t_len // tm
    return _pallas(
        body, name="in_project_backward", grid=(n_tiles,),
        operands=(dq, dk, dv, cq, ckv, pos, invf, q_norm_w, kv_norm_w, w_q, w_kv, d_hq, d_hf, d_hi, d_hg, w_in, dr1, x,
                  sc_a),
        out_shape=[jax.ShapeDtypeStruct((t_len, IN_COLS_PAD), BF16), jax.ShapeDtypeStruct((N_HEADS, t_len, QK_DIM), BF16),
                   jax.ShapeDtypeStruct((N_HEADS, t_len, 2 * HEAD_DIM), BF16), jax.ShapeDtypeStruct((t_len, Q_RANK), BF16),
                   jax.ShapeDtypeStruct((t_len, KV_RANK), BF16), jax.ShapeDtypeStruct((t_len, D_MODEL), F32),
                   jax.ShapeDtypeStruct((8, D_MODEL), F32)],
        in_specs=[pl.BlockSpec((N_HEADS, None, QK_DIM, tm), lambda i: (0, i // per_q, 0, i % per_q)),
                  pl.BlockSpec((N_HEADS, tm, QK_DIM), heads),
                  pl.BlockSpec((N_HEADS, tm, HEAD_DIM), heads), pl.BlockSpec((tm, Q_RANK), row),
                  pl.BlockSpec((tm, KV_RANK), row), pl.BlockSpec((tm, 1), row), pl.BlockSpec((1, 128), fixed2),
                  pl.BlockSpec((1, Q_RANK), fixed2), pl.BlockSpec((1, KV_RANK), fixed2),
                  pl.BlockSpec((N_HEADS, Q_RANK, QK_DIM), fixed3), pl.BlockSpec((N_HEADS, KV_RANK, 2 * HEAD_DIM), fixed3),
                  pl.BlockSpec((tm, hgw), row), pl.BlockSpec((tm, hgw), row), pl.BlockSpec((tm, hgw), row),
                  pl.BlockSpec((tm, hgw), row), pl.BlockSpec((IN_COLS_PAD, D_MODEL), fixed2),
                  pl.BlockSpec((tm, D_MODEL), row), pl.BlockSpec((tm, D_MODEL), row), pl.BlockSpec((1, D_MODEL), fixed2)],
        out_specs=[pl.BlockSpec((tm, IN_COLS_PAD), row), pl.BlockSpec((N_HEADS, tm, QK_DIM), heads),
                   pl.BlockSpec((N_HEADS, tm, 2 * HEAD_DIM), heads), pl.BlockSpec((tm, Q_RANK), row),
                   pl.BlockSpec((tm, KV_RANK), row), pl.BlockSpec((tm, D_MODEL), row), pl.BlockSpec((8, D_MODEL), fixed2)],
        params=_params(48, ("arbitrary",)), exchange=exchange,
        first=lambda: pl.program_id(0) == 0, last=lambda: pl.program_id(0) == n_tiles - 1)


def _weight_grad(a, b, name, n_blocks, bn, a_blocked=False, b_blocked=True, exchange=None, token_tile=512):
    t_len = a.shape[0]
    m = a.shape[1] // n_blocks if a_blocked else a.shape[1]
    bt = min(token_tile, t_len)

    def body(a_ref, b_ref, o_ref):
        @pl.when(pl.program_id(1) == 0)
        def _():
            o_ref[...] = jnp.zeros_like(o_ref)

        o_ref[...] += _dot_tn(a_ref[...].astype(BF16), b_ref[...].astype(BF16))

    a_spec = pl.BlockSpec((bt, m), (lambda n, t: (t, n)) if a_blocked else (lambda n, t: (t, 0)))
    if b.ndim == 3:
        b_spec = pl.BlockSpec((None, bt, bn), lambda n, t: (n, t, 0))
    else:
        b_spec = pl.BlockSpec((bt, bn), (lambda n, t: (t, n)) if b_blocked else (lambda n, t: (t, 0)))
    nt = t_len // bt
    (out,), landed = _pallas(
        body, name=name, grid=(n_blocks, nt), operands=(a, b),
        out_shape=[jax.ShapeDtypeStruct((n_blocks, m, bn), F32)],
        in_specs=[a_spec, b_spec],
        out_specs=[pl.BlockSpec((None, m, bn), lambda n, t: (n, 0, 0))],
        params=_params(56, ("arbitrary", "arbitrary")), exchange=exchange,
        first=lambda: (pl.program_id(0) == 0) & (pl.program_id(1) == 0),
        last=lambda: (pl.program_id(0) == n_blocks - 1) & (pl.program_id(1) == nt - 1))
    return (out, landed) if exchange else out


SMALL_PLACE = {"ln1_g": (6, 0), "ln1_b": (7, 0), "ln2_g": (8, 0), "ln2_b": (9, 0), "hg_norm_w": (10, 512),
               "mla_q_norm_w": (11, 0), "mla_kv_norm_w": (11, Q_RANK)}
SMALL_LB_ROW, SMALL_LOSS_ROW = 10, 12


def _small_params_step(gathered, params):
    names = list(params)

    def body(g_ref, *refs):
        ins, outs = refs[:3 * len(names)], refs[3 * len(names):]
        loss_ref, outs = outs[0], outs[1:]
        tot = g_ref[0]
        for d in range(1, N_DEV):
            tot = tot + g_ref[d]
        loss_ref[...] = tot[SMALL_LOSS_ROW:SMALL_LOSS_ROW + 1, 0:128]

        def update(i, grad, rows=slice(None), lanes=slice(None)):
            w_ref, m_ref, v_ref = ins[3 * i:3 * i + 3]
            g_out, d_out, nm_out, nv_out = outs[4 * i:4 * i + 4]
            g_out[rows, lanes] = grad
            d_out[rows, lanes], nm_out[rows, lanes], nv_out[rows, lanes] = _adamw_update(
                w_ref[rows, lanes], grad, m_ref[rows, lanes], v_ref[rows, lanes])

        for i, name in enumerate(names):
            if name == "b_ada":
                for r in range(6):
                    update(i, tot[r:r + 1, :], lanes=slice(r * D_MODEL, (r + 1) * D_MODEL))
            elif name == "hg_lower_bounds":
                lb = _lower_bound(ins[3 * i][...])
                d0 = tot[SMALL_LB_ROW:SMALL_LB_ROW + 1, 0:512] * lb * (1.0 - lb)
                update(i, d0, rows=slice(0, 1))
                update(i, -d0, rows=slice(1, 2))
            else:
                row, lane = SMALL_PLACE[name]
                update(i, tot[row:row + 1, lane:lane + params[name][0].shape[1]])

    flat_in = [a for name in names for a in params[name]]
    shapes = [jax.ShapeDtypeStruct((1, 128), F32)] + [jax.ShapeDtypeStruct(params[name][0].shape, F32)
                                                      for name in names for _ in range(4)]
    out = pl.pallas_call(body, name="small_params_step", out_shape=shapes)(gathered, *flat_in)
    return out[0], {name: out[1 + 4 * i:5 + 4 * i] for i, name in enumerate(names)}


def _adamw_update(w, gv, m, v):
    nm = ADAM_B1 * m + (1.0 - ADAM_B1) * gv
    nv = ADAM_B2 * v + (1.0 - ADAM_B2) * jnp.square(gv)
    m_hat = nm / (1.0 - ADAM_B1 ** ADAM_STEP)
    v_hat = nv / (1.0 - ADAM_B2 ** ADAM_STEP)
    return -ADAM_LR * (m_hat / (jnp.sqrt(v_hat) + ADAM_EPS) + ADAM_WD * w), nm, nv


def _adamw_halves(core, w, mine, theirs, m, v, name):
    rows, cols = w.shape
    h = rows // 2
    tr = _row_tile(h)
    per_half = h // tr

    def body(core_ref, w_ref, mine_ref, theirs_ref, m_ref, v_ref, g_ref, d_ref, nm_ref, nv_ref):
        is_mine = pl.program_id(0) // per_half == core_ref[0]
        gv = jnp.where(is_mine, mine_ref[...], theirs_ref[...])
        g_ref[...] = gv
        d_ref[...], nm_ref[...], nv_ref[...] = _adamw_update(w_ref[...], gv, m_ref[...], v_ref[...])

    full = pl.BlockSpec((tr, cols), lambda i, core_ref: (i, 0))
    part = pl.BlockSpec((tr, cols), lambda i, core_ref: (i % per_half, 0))
    return _pcall(
        body, name=name, out_shape=[jax.ShapeDtypeStruct(w.shape, F32)] * 4,
        grid_spec=pltpu.PrefetchScalarGridSpec(
            num_scalar_prefetch=1, grid=(rows // tr,), in_specs=[full, part, part, full, full], out_specs=[full] * 4),
        compiler_params=_params(40, ("arbitrary",)),
        operands=(core, w, mine, theirs, m, v))


def _adamw(w, g, m, v, name):
    rows, cols = w.shape
    tr = _row_tile(rows) if rows >= 8 else rows

    def body(w_ref, g_ref, m_ref, v_ref, d_ref, nm_ref, nv_ref):
        d_ref[...], nm_ref[...], nv_ref[...] = _adamw_update(w_ref[...], g_ref[...], m_ref[...], v_ref[...])

    spec = pl.BlockSpec((tr, cols), lambda i: (i, 0))
    return _pcall(
        body, name=name, grid=(rows // tr,),
        out_shape=[jax.ShapeDtypeStruct(w.shape, F32)] * 3,
        in_specs=[spec] * 4, out_specs=[spec] * 3,
        compiler_params=_params(40, ("arbitrary",)),
        operands=(w, g, m, v))


def kernel(x, c, positions, w_ada, b_ada, w_in, hg_lower_bounds, hg_norm_w, mla_q_norm_w, w_q_up, mla_kv_norm_w, w_kv_up, w_out, ln1_g, ln1_b, w_mlp_in, w_mlp_out, ln2_g, ln2_b, loss_target, m_w_ada, m_b_ada, m_w_in, m_hg_lower_bounds, m_hg_norm_w, m_mla_q_norm_w, m_w_q_up, m_mla_kv_norm_w, m_w_kv_up, m_w_out, m_ln1_g, m_ln1_b, m_w_mlp_in, m_w_mlp_out, m_ln2_g, m_ln2_b, v_w_ada, v_b_ada, v_w_in, v_hg_lower_bounds, v_hg_norm_w, v_mla_q_norm_w, v_w_q_up, v_mla_kv_norm_w, v_w_kv_up, v_w_out, v_ln1_g, v_ln1_b, v_w_mlp_in, v_w_mlp_out, v_ln2_g, v_ln2_b):
    ix, iy, ic = _mesh_pos()
    chip = 2 * ix + iy
    me = 4 * ix + 2 * iy + ic
    core_arr = jnp.reshape(ic, (1,)).astype(jnp.int32)
    chip_arr = jnp.reshape(chip, (1,)).astype(jnp.int32)

    xs = x[0]
    target = loss_target[0]
    t_len = xs.shape[0]
    pos = positions.astype(F32).reshape(t_len, 1)
    inv = 1.0 / (ROPE_THETA ** (jnp.arange(0, ROPE_DIM, 2, dtype=F32) / ROPE_DIM))
    invf = jnp.concatenate([inv, inv, jnp.zeros((128 - ROPE_DIM,), F32)]).reshape(1, 128)

    def slot(w):
        rows, cols = w.shape
        own = w.astype(BF16).reshape(1, 2, rows // 2, cols)
        return lax.dynamic_update_slice(jnp.zeros((N_CHIPS, 2, rows // 2, cols), BF16), own, (chip, 0, 0, 0))

    def slot8(a):
        return lax.dynamic_update_slice(jnp.zeros((N_DEV,) + a.shape, a.dtype), a[None], (me, 0, 0))

    def whole(s):
        return s.reshape(N_CHIPS, 2 * s.shape[2], s.shape[3])

    def halved(g):
        return g.reshape(N_CHIPS, 2, g.shape[1] // 2, g.shape[2])

    ada_cols = w_ada.shape[2]
    c_all, *early = _run_exchange(
        _merge(_gather_all(slot8(jnp.broadcast_to(c, (8, D_MODEL)))),
               _gather_over_ici([slot(jnp.transpose(w_in[0])), slot(w_q_up[0]), slot(w_kv_up[0])])),
        "gather_c_and_mixer_weights_ici")
    b_shard = lax.dynamic_slice(b_ada, (0, chip * ada_cols), (1, ada_cols))
    mod_cols, cond16 = _ada_project(c_all[:, 0, :], w_ada[0], b_shard)
    mod_all, *early = _run_exchange(_merge(_gather_all(slot8(mod_cols)), _gather_over_d2d(early)),
                                    "gather_mod_and_mixer_weights_d2d")
    mod_mine = lax.dynamic_slice(mod_all, (0, me, 0), (N_DEV, 1, ada_cols))[::2, 0, :].reshape(6, D_MODEL)
    sh_a, sc_a, g_a, sh_m, sc_m, g_m = (mod_mine[i:i + 1] for i in range(6))
    g_in, g_q, g_kv = (whole(s) for s in early)
    w_in_full = jnp.pad(g_in.reshape(IN_COLS, D_MODEL), ((0, IN_COLS_PAD - IN_COLS), (0, 0)))
    w_q_full = jnp.pad(g_q, ((0, 0), (0, 0), (0, QK_DIM - g_q.shape[2])))

    w1_rows = D_MODEL // 2
    (u_a, zhg, cq, ckv, q, k, k_t, v, v_t), (s_top,) = _in_project(
        xs, pos, sc_a, sh_a, w_in_full, mla_q_norm_w, mla_kv_norm_w, w_q_full, g_kv, invf,
        _gather_over_ici([slot(w_mlp_in[0, :w1_rows])]))
    (o_pre, o_hg, states), (s_out, s_bottom, s_top) = _hgrn_forward(
        zhg, hg_lower_bounds, hg_norm_w,
        _merge(_gather_over_ici([slot(w_out[0]), slot(w_mlp_in[0, w1_rows:])]), _gather_over_d2d([s_top])))
    (o_mla, lse), (s_w2, s_out, s_bottom) = _attention_forward(
        q, k, v_t, _merge(_gather_over_ici([slot(w_mlp_out[0])]), _gather_over_d2d([s_out, s_bottom])))
    w_out_full = whole(s_out).reshape(D_MODEL, D_MODEL)
    (cat, mix, xhat1, rstd1), (s_w2,) = _out_project(o_hg, o_mla, xs, g_a, w_out_full, _gather_over_d2d([s_w2]))
    g_w1_top, g_w1_bottom, g_w2 = whole(s_top), whole(s_bottom), whole(s_w2)
    vecs = jnp.concatenate([ln1_g, ln1_b, sc_m, sh_m, g_m, g_a, ln2_g, ln2_b], axis=0)
    act, dhp, um, dh, dmix, d_cat, dr1, mlp_sums, delta = _mlp_and_back(
        xhat1, rstd1, mix, target, o_mla, vecs, g_w1_top, g_w1_bottom, g_w2, w_out_full)

    gw_1 = halved(_weight_grad(um, dhp, "grad_w_mlp_in", N_CHIPS, D_FF // N_CHIPS, token_tile=4096))
    gw_2, (landed_1,) = _weight_grad(act, dh, "grad_w_mlp_out", N_CHIPS, D_MODEL, a_blocked=True, b_blocked=False,
                                     token_tile=4096, exchange=_pair_exchange([gw_1]))
    gw_out = _weight_grad(cat, dmix, "grad_w_out", 1, D_MODEL, token_tile=2048)
    later = [halved(gw_2), halved(gw_out.reshape(N_CHIPS, D_MODEL // N_CHIPS, D_MODEL))]
    own_1, travels_1 = _add_pair(core_arr, chip_arr, gw_1, landed_1)
    (dq, dk, dv), (landed_1, *landed) = _attention_backward(
        q, k, k_t, v, d_cat, lse, delta, _merge(_chip_exchange([travels_1]), _pair_exchange(later)))
    mine_1 = _add_chips(own_1, landed_1)
    chip_sums = [_add_pair(core_arr, chip_arr, g, l) for g, l in zip(later, landed)]
    (d_hq, d_hf, d_hi, d_hg, hg_sums), (theirs_1, *landed) = _hgrn_backward(
        zhg, hg_lower_bounds, hg_norm_w, o_pre, d_cat, states,
        _merge(_pair_send([mine_1]), _chip_exchange([b for _, b in chip_sums])))
    later_mine = [_add_chips(own, l) for (own, _), l in zip(chip_sums, landed)]
    mlp_mine = [mine_1] + later_mine
    (dz, dqf, dkvu, cqn, ckvn, grad_x, in_sums), _ = _in_project_backward(
        dq, dk, dv, cq, ckv, pos, invf, mla_q_norm_w, mla_kv_norm_w, w_q_full, g_kv,
        d_hq, d_hf, d_hi, d_hg, w_in_full, dr1, xs, sc_a)

    zeros = lambda n: jnp.zeros((1, n), F32)
    small = jnp.concatenate([
        in_sums[1:2], in_sums[0:1], mlp_sums[S_DGA:S_DGA + 1],
        mlp_sums[S_DSHM:S_DSHM + 1], mlp_sums[S_DSCM:S_DSCM + 1], mlp_sums[S_DGM:S_DGM + 1],
        mlp_sums[S_DLN1G:S_DLN1G + 1], mlp_sums[S_DLN1B:S_DLN1B + 1],
        mlp_sums[S_DLN2G:S_DLN2G + 1], mlp_sums[S_DLN2B:S_DLN2B + 1],
        jnp.concatenate([hg_sums[0:1], hg_sums[1:2]], axis=1),
        jnp.concatenate([in_sums[2:3, :Q_RANK], in_sums[3:4, :KV_RANK], zeros(D_MODEL - Q_RANK - KV_RANK)], axis=1),
        mlp_sums[S_LOSS:S_LOSS + 1],
        jnp.zeros((SMALL_ROWS - 13, D_MODEL), F32)], axis=0)

    gw_in, (*later_theirs, small_all) = _weight_grad(
        dz, u_a, "grad_w_in", 3, D_MODEL, a_blocked=True, b_blocked=False, token_tile=4096,
        exchange=_merge(_pair_send(later_mine), _gather_all(slot8(small))))
    mlp_theirs = [theirs_1] + list(later_theirs)
    gw_in = gw_in.reshape(IN_COLS_PAD, D_MODEL)
    gw_q = _weight_grad(cqn, dqf, "grad_w_q_up", N_HEADS, QK_DIM, token_tile=2048)[:, :, :HEAD_DIM + ROPE_DIM]
    gw_kv = _weight_grad(ckvn, dkvu, "grad_w_kv_up", N_HEADS, 2 * HEAD_DIM, token_tile=2048)
    flat = lambda g: g.reshape(g.shape[0] * g.shape[1], g.shape[2])
    mixer_mine, mixer_theirs = _reduce_in_vmem(
        [gw_in, flat(gw_q), flat(gw_kv)], [IN_COLS // N_CHIPS // 2, Q_RANK // 2, KV_RANK // 2], "reduce_mixer_grads")
    reduced = ("w_in", "w_q_up", "w_kv_up", "w_mlp_in", "w_mlp_out", "w_out")
    halves_mine = dict(zip(reduced, list(mixer_mine) + mlp_mine))
    halves_theirs = dict(zip(reduced, list(mixer_theirs) + list(mlp_theirs)))

    small_names = ("b_ada", "hg_lower_bounds", "hg_norm_w", "mla_q_norm_w", "mla_kv_norm_w",
                   "ln1_g", "ln1_b", "ln2_g", "ln2_b")
    loss_row, small_out = _small_params_step(small_all, {
        "b_ada": (b_ada, m_b_ada, v_b_ada),
        "hg_lower_bounds": (hg_lower_bounds, m_hg_lower_bounds, v_hg_lower_bounds),
        "hg_norm_w": (hg_norm_w, m_hg_norm_w, v_hg_norm_w),
        "mla_q_norm_w": (mla_q_norm_w, m_mla_q_norm_w, v_mla_q_norm_w),
        "mla_kv_norm_w": (mla_kv_norm_w, m_mla_kv_norm_w, v_mla_kv_norm_w),
        "ln1_g": (ln1_g, m_ln1_g, v_ln1_g), "ln1_b": (ln1_b, m_ln1_b, v_ln1_b),
        "ln2_g": (ln2_g, m_ln2_g, v_ln2_g), "ln2_b": (ln2_b, m_ln2_b, v_ln2_b)})
    loss = loss_row[0, 0]

    d_mod_all = small_all[:, 0:6, :].reshape(N_DEV, 6 * D_MODEL)
    d_mod_cols = lax.dynamic_slice(d_mod_all, (0, chip * ada_cols), (N_DEV, ada_cols))
    d_mod_cols = jnp.concatenate([d_mod_cols, jnp.zeros_like(d_mod_cols)], axis=0)
    g_w_ada = _weight_grad(cond16, d_mod_cols, "grad_w_ada", 1, ada_cols)[0]

    names = ["w_ada", "b_ada", "w_in", "hg_lower_bounds", "hg_norm_w", "mla_q_norm_w", "w_q_up", "mla_kv_norm_w",
             "w_kv_up", "w_out", "ln1_g", "ln1_b", "w_mlp_in", "w_mlp_out", "ln2_g", "ln2_b"]
    weights = [w_ada, b_ada, w_in, hg_lower_bounds, hg_norm_w, mla_q_norm_w, w_q_up, mla_kv_norm_w,
               w_kv_up, w_out, ln1_g, ln1_b, w_mlp_in, w_mlp_out, ln2_g, ln2_b]
    moms = [m_w_ada, m_b_ada, m_w_in, m_hg_lower_bounds, m_hg_norm_w, m_mla_q_norm_w, m_w_q_up, m_mla_kv_norm_w,
            m_w_kv_up, m_w_out, m_ln1_g, m_ln1_b, m_w_mlp_in, m_w_mlp_out, m_ln2_g, m_ln2_b]
    vels = [v_w_ada, v_b_ada, v_w_in, v_hg_lower_bounds, v_hg_norm_w, v_mla_q_norm_w, v_w_q_up, v_mla_kv_norm_w,
            v_w_kv_up, v_w_out, v_ln1_g, v_ln1_b, v_w_mlp_in, v_w_mlp_out, v_ln2_g, v_ln2_b]
    out_g, out_d, out_m, out_v = [], [], [], []
    for name, w, m, vv in zip(names, weights, moms, vels):
        if name in small_names:
            g, d, nm, nv = small_out[name]
            back = lambda a: a
        elif name == "w_in":
            to2d, back = (lambda a: jnp.transpose(a[0])), (lambda a: jnp.transpose(a)[None])
        else:
            to2d, back = (lambda a, s=w.shape[1:]: a.reshape(s)), (lambda a, s=w.shape: a.reshape(s))
        if name == "w_ada":
            d, nm, nv = _adamw(to2d(w), g_w_ada, to2d(m), to2d(vv), "adamw_" + name)
            g = g_w_ada
        elif name not in small_names:
            g, d, nm, nv = _adamw_halves(core_arr, to2d(w), halves_mine[name], halves_theirs[name], to2d(m), to2d(vv),
                                         "adamw_" + name)
        out_g.append(back(g))
        out_d.append(back(d))
        out_m.append(back(nm))
        out_v.append(back(nv))
    return (loss, grad_x[None], *out_g, *out_d, *out_m, *out_v)
```

```python
import functools

import jax
import jax.numpy as jnp
from jax import lax
from jax.experimental import pallas as pl
from jax.experimental.pallas import tpu as pltpu

F32 = jnp.float32
BF16 = jnp.bfloat16
MESH_IDS = pl.DeviceIdType.MESH

D_MODEL = 1024
N_HEADS = 4
HEAD_DIM = 128
ROPE_DIM = 64
HG_CHUNK = 64
HG_COLS = 2048
Q_RANK = 256
KV_RANK = 256
IN_COLS = 2624
IN_COLS_PAD = 2688
QK_DIM = 256
D_FF = 4096
N_CHIPS = 4
N_DEV = 8
ROPE_THETA = 10000.0
RMS_EPS = 1e-6
LN_EPS = 1e-5
DN_ALPHA = 2.0 ** 0.25
ATT_SCALE = (HEAD_DIM + ROPE_DIM) ** -0.5
NEG_BIG = -1e30
ADAM_LR = 0.001
ADAM_B1 = 0.9
ADAM_B2 = 0.999
ADAM_EPS = 1e-08
ADAM_WD = 0.01
ADAM_STEP = 10
SMALL_ROWS = 16
MIB = 1024 * 1024


def _dot(a, b):
    return jnp.dot(a, b, preferred_element_type=F32)


def _dot_nt(a, b):
    return lax.dot_general(a, b, (((1,), (1,)), ((), ())), preferred_element_type=F32)


def _dot_tn(a, b):
    return lax.dot_general(a, b, (((0,), (0,)), ((), ())), preferred_element_type=F32)


def _params(vmem_mib, semantics=None):
    return pltpu.CompilerParams(vmem_limit_bytes=vmem_mib * MIB, dimension_semantics=semantics)


def _sigmoid(v):
    return 1.0 / (1.0 + jnp.exp(-v))


def _colsum(v):
    return jnp.sum(v, axis=0, keepdims=True)


def _rowmean(v):
    return jnp.mean(v, axis=-1, keepdims=True)


def _rope_tables(pos, invf):
    ang = pos * invf
    lane = lax.broadcasted_iota(jnp.int32, ang.shape, 1)
    cos_t = jnp.where(lane < ROPE_DIM, jnp.cos(ang), 0.0)
    sin = jnp.sin(ang)
    sin_t = jnp.where(lane < ROPE_DIM // 2, -sin, jnp.where(lane < ROPE_DIM, sin, 0.0))
    return cos_t, sin_t


def _swap_halves(t):
    lane = lax.broadcasted_iota(jnp.int32, t.shape, 1)
    return jnp.where(lane < ROPE_DIM // 2, pltpu.roll(t, 128 - ROPE_DIM // 2, 1), pltpu.roll(t, ROPE_DIM // 2, 1))


def _rope(t, cos_t, sin_t):
    return t * cos_t + _swap_halves(t) * sin_t


def _unrope(g, cos_t, sin_t):
    return g * cos_t - _swap_halves(g) * sin_t


def _mesh_pos():
    return lax.axis_index("x"), lax.axis_index("y"), lax.axis_index("c")


def _other_chips(x, y):
    out = []
    for dx, dy in ((1, 0), (0, 1), (1, 1)):
        px = 1 - x if dx else x
        py = 1 - y if dy else y
        out.append(((px, py), 2 * px + py))
    return out


class _Exchange:
    def __init__(self, inputs, out_shapes, aliases, sems, start, finish):
        self.inputs, self.out_shapes, self.aliases, self.sems = list(inputs), list(out_shapes), dict(aliases), list(sems)
        self.start, self.finish = start, finish


def _from_copies(inputs, out_shapes, aliases, sems, copies):
    def start(ins, outs, sem_refs):
        for send, _ in copies(ins, outs, sem_refs):
            send.start()

    def finish(ins, outs, sem_refs):
        for send, recv in copies(ins, outs, sem_refs):
            recv.wait_recv()
            send.wait_send()

    return _Exchange(inputs, out_shapes, aliases, sems, start, finish)


HBM_MIN_BYTES = 256 * 1024


def _in_hbm(a):
    if a.size * a.dtype.itemsize < HBM_MIN_BYTES:
        return a
    return pltpu.with_memory_space_constraint(a, pltpu.HBM)


def _out_hbm(s):
    if s.size * s.dtype.itemsize < HBM_MIN_BYTES:
        return s
    return pltpu.HBM(s.shape, s.dtype)


def _pcall(body, *, operands, out_shape, **kwargs):
    single = not isinstance(out_shape, (list, tuple))
    shapes = [_out_hbm(s) for s in ([out_shape] if single else out_shape)]
    return pl.pallas_call(body, out_shape=shapes[0] if single else shapes, **kwargs)(*[_in_hbm(a) for a in operands])


def _run_exchange(exchange, name):
    n_in, n_out = len(exchange.inputs), len(exchange.out_shapes)

    def body(*refs):
        ins, outs, sem_refs = refs[:n_in], refs[n_in:n_in + n_out], refs[n_in + n_out:]
        exchange.start(ins, outs, sem_refs)
        exchange.finish(ins, outs, sem_refs)

    any_spec = pl.BlockSpec(memory_space=pl.ANY)
    return pl.pallas_call(
        body, name=name, out_shape=[_out_hbm(s) for s in exchange.out_shapes],
        in_specs=[any_spec] * n_in, out_specs=[any_spec] * n_out,
        scratch_shapes=exchange.sems, input_output_aliases=exchange.aliases,
    )(*[_in_hbm(a) for a in exchange.inputs])


def _pallas(body, *, name, operands, in_specs, out_shape, out_specs, params, scratch_shapes=(), grid=(), prefetch=(),
            exchange=None, first=None, last=None):
    n_pre, n_in, n_out, n_scr = len(prefetch), len(in_specs), len(out_specs), len(scratch_shapes)
    ex_in = exchange.inputs if exchange else []
    ex_out = exchange.out_shapes if exchange else []
    ex_sems = exchange.sems if exchange else []

    def full_body(*refs):
        pre, rest = refs[:n_pre], refs[n_pre:]
        ins, rest = rest[:n_in], rest[n_in:]
        xin, rest = rest[:len(ex_in)], rest[len(ex_in):]
        outs, rest = rest[:n_out], rest[n_out:]
        xout, rest = rest[:len(ex_out)], rest[len(ex_out):]
        scr, sem_refs = rest[:n_scr], rest[n_scr:]
        if exchange:
            @pl.when(first(*pre))
            def _():
                exchange.start(xin, xout, sem_refs)

        body(*pre, *ins, *outs, *scr)
        if exchange:
            @pl.when(last(*pre))
            def _():
                exchange.finish(xin, xout, sem_refs)

    any_spec = pl.BlockSpec(memory_space=pl.ANY)
    aliases = {n_pre + n_in + i: n_out + o for i, o in exchange.aliases.items()} if exchange else {}
    operands = [_in_hbm(a) for a in operands]
    results = pl.pallas_call(
        full_body, name=name, out_shape=[_out_hbm(s) for s in list(out_shape) + ex_out],
        grid_spec=pltpu.PrefetchScalarGridSpec(
            num_scalar_prefetch=n_pre, grid=grid, in_specs=list(in_specs) + [any_spec] * len(ex_in),
            out_specs=list(out_specs) + [any_spec] * len(ex_out), scratch_shapes=list(scratch_shapes) + ex_sems),
        input_output_aliases=aliases, compiler_params=params,
    )(*prefetch, *operands, *[_in_hbm(a) for a in ex_in])
    return results[:n_out], results[n_out:]


def _remote(src, dst, sems, idx, to):
    send_sems, recv_sems = sems
    return pltpu.make_async_remote_copy(src_ref=src, dst_ref=dst, send_sem=send_sems.at[idx], recv_sem=recv_sems.at[idx],
                                        device_id=to, device_id_type=MESH_IDS)


def _sem_pairs(*shape):
    return [pltpu.SemaphoreType.DMA(shape), pltpu.SemaphoreType.DMA(shape)]


def _same_shapes(arrays):
    return [jax.ShapeDtypeStruct(a.shape, a.dtype) for a in arrays]


def _gather_over_ici(slots):
    n = len(slots)

    def copies(ins, outs, sems):
        x, y, c = _mesh_pos()
        k = 2 * x + y
        out = []
        for j, (chip, kj) in enumerate(_other_chips(x, y)):
            for i in range(n):
                to = (*chip, c)
                out.append((_remote(ins[i].at[k, c], outs[i].at[k, c], sems, (j, i), to),
                            _remote(ins[i].at[k, c], outs[i].at[kj, c], sems, (j, i), to)))
        return out

    return _from_copies(slots, _same_shapes(slots), {i: i for i in range(n)}, _sem_pairs(3, n), copies)


def _gather_over_d2d(slots):
    n = len(slots)

    def copies(ins, outs, sems):
        x, y, c = _mesh_pos()
        sibling = (x, y, 1 - c)
        out = []
        for j, (_, kj) in enumerate(_other_chips(x, y)):
            for i in range(n):
                out.append((_remote(ins[i].at[kj, c], outs[i].at[kj, c], sems, (j, i), sibling),
                            _remote(ins[i].at[kj, c], outs[i].at[kj, 1 - c], sems, (j, i), sibling)))
        return out

    return _from_copies(slots, _same_shapes(slots), {i: i for i in range(n)}, _sem_pairs(3, n), copies)


def _gather_all(slots8):
    def copies(ins, outs, sems):
        x, y, c = _mesh_pos()
        me = 4 * x + 2 * y + c
        out = []
        for r in range(1, N_DEV):
            px = 1 - x if r & 4 else x
            py = 1 - y if r & 2 else y
            pc = 1 - c if r & 1 else c
            to = (px, py, pc)
            out.append((_remote(ins[0].at[me], outs[0].at[me], sems, r - 1, to),
                        _remote(ins[0].at[me], outs[0].at[4 * px + 2 * py + pc], sems, r - 1, to)))
        return out

    return _from_copies([slots8], _same_shapes([slots8]), {0: 0}, _sem_pairs(N_DEV - 1), copies)


def _merge(first, second):
    n_in, n_out, n_sem = len(first.inputs), len(first.out_shapes), len(first.sems)

    def start(ins, outs, sems):
        first.start(ins[:n_in], outs[:n_out], sems[:n_sem])
        second.start(ins[n_in:], outs[n_out:], sems[n_sem:])

    def finish(ins, outs, sems):
        first.finish(ins[:n_in], outs[:n_out], sems[:n_sem])
        second.finish(ins[n_in:], outs[n_out:], sems[n_sem:])

    aliases = dict(first.aliases)
    aliases.update({n_in + i: n_out + o for i, o in second.aliases.items()})
    return _Exchange(first.inputs + second.inputs, first.out_shapes + second.out_shapes, aliases,
                     first.sems + second.sems, start, finish)


def _pair_exchange(grads):
    n = len(grads)

    def copies(ins, outs, sems):
        x, y, c = _mesh_pos()
        cps = [_remote(ins[i].at[:, 1 - c], outs[i], sems, i, (x, y, 1 - c)) for i in range(n)]
        return [(cp, cp) for cp in cps]

    shapes = [jax.ShapeDtypeStruct((N_CHIPS,) + g.shape[2:], g.dtype) for g in grads]
    return _from_copies(grads, shapes, {}, _sem_pairs(n), copies)


def _chip_exchange(partials):
    n = len(partials)

    def copies(ins, outs, sems):
        x, y, c = _mesh_pos()
        cps = [_remote(ins[i].at[kj], outs[i].at[j], sems, (j, i), (*chip, c))
               for j, (chip, kj) in enumerate(_other_chips(x, y)) for i in range(n)]
        return [(cp, cp) for cp in cps]

    shapes = [jax.ShapeDtypeStruct((3,) + p.shape[1:], p.dtype) for p in partials]
    return _from_copies(partials, shapes, {}, _sem_pairs(3, n), copies)


def _pair_send(halves):
    n = len(halves)

    def copies(ins, outs, sems):
        x, y, c = _mesh_pos()
        cps = [_remote(ins[i], outs[i], sems, i, (x, y, 1 - c)) for i in range(n)]
        return [(cp, cp) for cp in cps]

    return _from_copies(halves, _same_shapes(halves), {}, _sem_pairs(n), copies)


def _reduce_in_vmem(grads, half_rows, name):
    n = len(grads)

    def body(*refs):
        g, mine, theirs = refs[:n], refs[n:2 * n], refs[2 * n:3 * n]
        landed_pair, partial, landed_chips = refs[3 * n:4 * n], refs[4 * n:5 * n], refs[5 * n:6 * n]
        sems = refs[6 * n:]
        x, y, c = _mesh_pos()
        k = 2 * x + y
        sibling = (x, y, 1 - c)

        def half(i, chip_idx, which):
            return pl.ds(pl.multiple_of((2 * chip_idx + which) * half_rows[i], 8), half_rows[i])

        def run(copies):
            for cp in copies:
                cp.start()
            for cp in copies:
                cp.wait_recv()
                cp.wait_send()

        run([_remote(g[i].at[half(i, kk, 1 - c)], landed_pair[i].at[kk], sems[0:2], (kk, i), sibling)
             for kk in range(N_CHIPS) for i in range(n)])
        for i in range(n):
            for kk in range(N_CHIPS):
                partial[i][kk] = (g[i][half(i, kk, c), :] + landed_pair[i][kk]).astype(BF16)
        run([_remote(partial[i].at[kj], landed_chips[i].at[j], sems[2:4], (j, i), (*chip, c))
             for j, (chip, kj) in enumerate(_other_chips(x, y)) for i in range(n)])
        for i in range(n):
            own = g[i][half(i, k, c), :] + landed_pair[i][k]
            mine[i][...] = ((own + landed_chips[i][0].astype(F32)) + landed_chips[i][1].astype(F32)) \
                + landed_chips[i][2].astype(F32)
        run([_remote(mine[i], theirs[i], sems[4:6], i, sibling) for i in range(n)])

    shapes = [(h, gr.shape[1]) for gr, h in zip(grads, half_rows)]
    halves = [jax.ShapeDtypeStruct(s, F32) for s in shapes]
    vmem = pl.BlockSpec(memory_space=pltpu.VMEM)
    scratch = ([pltpu.VMEM((N_CHIPS,) + s, F32) for s in shapes]
               + [pltpu.VMEM((N_CHIPS,) + s, BF16) for s in shapes]
               + [pltpu.VMEM((3,) + s, BF16) for s in shapes]
               + _sem_pairs(N_CHIPS, n) + _sem_pairs(3, n) + _sem_pairs(n))
    out = pl.pallas_call(
        body, name=name, out_shape=halves + halves, in_specs=[vmem] * n, out_specs=[vmem] * (2 * n),
        scratch_shapes=scratch, compiler_params=_params(48),
    )(*grads)
    return out[:n], out[n:]


def _row_tile(rows):
    for t in (256, 128, 64):
        if rows % t == 0:
            return t
    return rows


def _add_pair(core, chip, grad, landed):
    _, h, cols = landed.shape
    tr = _row_tile(h)

    def body(core_ref, chip_ref, g_ref, l_ref, own_ref, ob_ref):
        s = g_ref[...] + l_ref[...]
        ob_ref[...] = s.astype(BF16)

        @pl.when(pl.program_id(1) == chip_ref[0])
        def _():
            own_ref[...] = s

    return _pcall(
        body, name="grad_add_pair",
        out_shape=[jax.ShapeDtypeStruct((h, cols), F32), jax.ShapeDtypeStruct(landed.shape, BF16)],
        grid_spec=pltpu.PrefetchScalarGridSpec(
            num_scalar_prefetch=2, grid=(h // tr, N_CHIPS),
            in_specs=[pl.BlockSpec((None, None, tr, cols), lambda t, k, core_ref, chip_ref: (k, core_ref[0], t, 0)),
                      pl.BlockSpec((None, tr, cols), lambda t, k, core_ref, chip_ref: (k, t, 0))],
            out_specs=[pl.BlockSpec((tr, cols), lambda t, k, core_ref, chip_ref: (t, 0)),
                       pl.BlockSpec((None, tr, cols), lambda t, k, core_ref, chip_ref: (k, t, 0))]),
        compiler_params=_params(32, ("arbitrary", "arbitrary")),
        operands=(core, chip, grad, landed))


def _add_chips(own, landed):
    h, cols = own.shape
    tr = _row_tile(h)

    def body(p_ref, l_ref, o_ref):
        o_ref[...] = ((p_ref[...] + l_ref[0].astype(F32)) + l_ref[1].astype(F32)) + l_ref[2].astype(F32)

    return _pcall(
        body, name="grad_add_chips", grid=(h // tr,),
        out_shape=jax.ShapeDtypeStruct((h, cols), F32),
        in_specs=[pl.BlockSpec((tr, cols), lambda t: (t, 0)), pl.BlockSpec((3, tr, cols), lambda t: (0, t, 0))],
        out_specs=pl.BlockSpec((tr, cols), lambda t: (t, 0)),
        compiler_params=_params(32, ("arbitrary",)),
        operands=(own, landed))


def _ada_project(c_all, w_ada, b_shard):
    n = w_ada.shape[1]
    tn = 512

    def body(c_ref, w_ref, b_ref, mod_ref, cond_ref):
        cv = c_ref[...]
        cond = cv * _sigmoid(cv)
        mod_ref[...] = _dot(cond.astype(BF16), w_ref[...].astype(BF16)) + b_ref[...]
        cond_ref[0:N_DEV, :] = cond
        cond_ref[N_DEV:2 * N_DEV, :] = jnp.zeros_like(cond)

    return _pcall(
        body, name="ada_project", grid=(n // tn,),
        out_shape=[jax.ShapeDtypeStruct((N_DEV, n), F32), jax.ShapeDtypeStruct((2 * N_DEV, D_MODEL), F32)],
        in_specs=[pl.BlockSpec((N_DEV, D_MODEL), lambda j: (0, 0)), pl.BlockSpec((D_MODEL, tn), lambda j: (0, j)),
                  pl.BlockSpec((1, tn), lambda j: (0, j))],
        out_specs=[pl.BlockSpec((N_DEV, tn), lambda j: (0, j)), pl.BlockSpec((2 * N_DEV, D_MODEL), lambda j: (0, 0))],
        compiler_params=_params(32, ("arbitrary",)),
        operands=(c_all, w_ada, b_shard))


def _in_project(x, pos, sc_a, sh_a, w_in, q_norm_w, kv_norm_w, w_q, w_kv, invf, exchange=None):
    t_len = x.shape[0]
    tm = min(512, t_len)

    def body(x_ref, pos_ref, sc_ref, sh_ref, win_ref, qn_ref, kvn_ref, wq_ref, wkv_ref, invf_ref,
             u_ref, zhg_ref, cq_ref, ckv_ref, q_ref, k_ref, kt_ref, v_ref, vt_ref):
        u = (x_ref[...] * (1.0 + sc_ref[...]) + sh_ref[...]).astype(BF16)
        u_ref[...] = u
        z = _dot_nt(u, win_ref[...])
        zhg_ref[...] = z[:, :HG_COLS]
        cq = z[:, HG_COLS:HG_COLS + Q_RANK]
        ckv = z[:, HG_COLS + Q_RANK:HG_COLS + Q_RANK + KV_RANK]
        cq_ref[...] = cq
        ckv_ref[...] = ckv
        cos_t, sin_t = _rope_tables(pos_ref[...], invf_ref[...])
        k_pe = _rope(z[:, HG_COLS + Q_RANK + KV_RANK:], cos_t, sin_t)
        k_pe_t = jnp.transpose(k_pe).astype(BF16)
        cqn = (cq * lax.rsqrt(_rowmean(cq * cq) + RMS_EPS) * qn_ref[...]).astype(BF16)
        ckvn = (ckv * lax.rsqrt(_rowmean(ckv * ckv) + RMS_EPS) * kvn_ref[...]).astype(BF16)
        for h in range(N_HEADS):
            qh = _dot(cqn, wq_ref[h])
            q_ref[h, :, 0:HEAD_DIM] = qh[:, :HEAD_DIM].astype(BF16)
            q_ref[h, :, HEAD_DIM:QK_DIM] = _rope(qh[:, HEAD_DIM:], cos_t, sin_t).astype(BF16)
            kvh = _dot(ckvn, wkv_ref[h])
            k_ref[h, :, 0:HEAD_DIM] = kvh[:, :HEAD_DIM].astype(BF16)
            k_ref[h, :, HEAD_DIM:QK_DIM] = k_pe.astype(BF16)
            kt_ref[h, 0:HEAD_DIM, :] = jnp.transpose(kvh[:, :HEAD_DIM]).astype(BF16)
            kt_ref[h, HEAD_DIM:QK_DIM, :] = k_pe_t
            v_ref[h] = kvh[:, HEAD_DIM:].astype(BF16)
            vt_ref[h] = jnp.transpose(kvh[:, HEAD_DIM:]).astype(BF16)

    row = lambda i: (i, 0)
    fixed2 = lambda i: (0, 0)
    fixed3 = lambda i: (0, 0, 0)
    heads = lambda i: (0, i, 0)
    n_tiles = t_len // tm
    return _pallas(
        body, name="in_project", grid=(n_tiles,),
        operands=(x, pos, sc_a, sh_a, w_in, q_norm_w, kv_norm_w, w_q, w_kv, invf),
        out_shape=[jax.ShapeDtypeStruct((t_len, D_MODEL), BF16), jax.ShapeDtypeStruct((t_len, HG_COLS), F32),
                   jax.ShapeDtypeStruct((t_len, Q_RANK), F32), jax.ShapeDtypeStruct((t_len, KV_RANK), F32),
                   jax.ShapeDtypeStruct((N_HEADS, t_len, QK_DIM), BF16),
                   jax.ShapeDtypeStruct((N_HEADS, t_len, QK_DIM), BF16),
                   jax.ShapeDtypeStruct((N_HEADS, QK_DIM, t_len), BF16),
                   jax.ShapeDtypeStruct((N_HEADS, t_len, HEAD_DIM), BF16),
                   jax.ShapeDtypeStruct((N_HEADS, HEAD_DIM, t_len), BF16)],
        in_specs=[pl.BlockSpec((tm, D_MODEL), row), pl.BlockSpec((tm, 1), row),
                  pl.BlockSpec((1, D_MODEL), fixed2), pl.BlockSpec((1, D_MODEL), fixed2),
                  pl.BlockSpec((IN_COLS_PAD, D_MODEL), fixed2),
                  pl.BlockSpec((1, Q_RANK), fixed2), pl.BlockSpec((1, KV_RANK), fixed2),
                  pl.BlockSpec((N_HEADS, Q_RANK, QK_DIM), fixed3), pl.BlockSpec((N_HEADS, KV_RANK, 2 * HEAD_DIM), fixed3),
                  pl.BlockSpec((1, 128), fixed2)],
        out_specs=[pl.BlockSpec((tm, D_MODEL), row), pl.BlockSpec((tm, HG_COLS), row),
                   pl.BlockSpec((tm, Q_RANK), row), pl.BlockSpec((tm, KV_RANK), row),
                   pl.BlockSpec((N_HEADS, tm, QK_DIM), heads), pl.BlockSpec((N_HEADS, tm, QK_DIM), heads),
                   pl.BlockSpec((N_HEADS, QK_DIM, tm), lambda i: (0, 0, i)),
                   pl.BlockSpec((N_HEADS, tm, HEAD_DIM), heads),
                   pl.BlockSpec((N_HEADS, HEAD_DIM, tm), lambda i: (0, 0, i))],
        params=_params(48, ("arbitrary",)), exchange=exchange,
        first=lambda: pl.program_id(0) == 0, last=lambda: pl.program_id(0) == n_tiles - 1)


def _lower_bound(lb_raw):
    m = jnp.max(lb_raw, axis=0, keepdims=True)
    e = jnp.exp(lb_raw - m)
    return e[0:1] / jnp.sum(e, axis=0, keepdims=True)


def _tri(inclusive_lower):
    r = lax.broadcasted_iota(jnp.int32, (HG_CHUNK, HG_CHUNK), 0)
    c = lax.broadcasted_iota(jnp.int32, (HG_CHUNK, HG_CHUNK), 1)
    return (c <= r) if inclusive_lower else (c >= r)


def _chunk_rows(n):
    return slice(n * HG_CHUNK, (n + 1) * HG_CHUNK)


def _chunk_prefix_sums(v, inclusive_lower):
    tri = _tri(inclusive_lower).astype(BF16)
    hi = v.astype(BF16)
    rest = v - hi.astype(F32)
    mid = rest.astype(BF16)
    lo = (rest - mid.astype(F32)).astype(BF16)
    pieces = jnp.concatenate([hi, mid, lo], axis=1)
    out = []
    for n in range(v.shape[0] // HG_CHUNK):
        s = _dot(tri, pieces[_chunk_rows(n)])
        out.append((s[:, 0:HEAD_DIM] + s[:, HEAD_DIM:2 * HEAD_DIM]) + s[:, 2 * HEAD_DIM:])
    return jnp.concatenate(out, axis=0)


def _per_chunk(v, row):
    n = v.shape[0] // HG_CHUNK
    v3 = v.reshape(n, HG_CHUNK, HEAD_DIM)
    return jnp.broadcast_to(v3[:, row:row + 1, :], v3.shape).reshape(v.shape)


def _hg_block(q, f_logit, lb):
    sg = _sigmoid(f_logit)
    forget = lb + (1.0 - lb) * sg
    kk = 1.0 - forget
    b = _chunk_prefix_sums(jnp.log(forget), True)
    b_ref = _per_chunk(b, HG_CHUNK // 2 - 1)
    b_last = _per_chunk(b, HG_CHUNK - 1)
    e_i = jnp.exp(b - b_ref)
    e_ri = jnp.exp(b_ref - b)
    e_b = jnp.exp(b)
    e_l = jnp.exp(b_last - b)
    return dict(sg=sg, forget=forget, e_i=e_i, e_ri=e_ri, e_b=e_b, e_l=e_l, dec=jnp.exp(b_last),
                qi=q * e_i, ki=kk * e_ri, qe=q * e_b, kl=kk * e_l)


HG_STEP_HEADS = 4


def _head_cols(hh):
    return slice(hh * HEAD_DIM, (hh + 1) * HEAD_DIM)


def _hgrn_forward(zhg, lb_raw, norm_w, exchange=None):
    t_len = zhg.shape[0]
    tb = min(512, t_len)
    n_chunks = tb // HG_CHUNK
    hs = HG_STEP_HEADS

    def body(q_ref, f_ref, v_ref, g_ref, lb_ref, w_ref, opre_ref, o_ref, st_ref, state):
        @pl.when(pl.program_id(1) == 0)
        def _():
            state[...] = jnp.zeros_like(state)

        causal = _tri(True)
        heads = range(hs)
        blk, v, qi, ki, qe, kl = {}, {}, {}, {}, {}, {}
        for hh in heads:
            cols = _head_cols(hh)
            blk[hh] = _hg_block(q_ref[:, cols], f_ref[:, cols], _lower_bound(lb_ref[:, cols]))
            v[hh] = v_ref[:, cols].astype(BF16)
            qi[hh], ki[hh], qe[hh], kl[hh] = (blk[hh][name].astype(BF16) for name in ("qi", "ki", "qe", "kl"))
        st = {hh: state[hh] for hh in heads}
        parts = {hh: [] for hh in heads}
        for n in range(n_chunks):
            r = _chunk_rows(n)
            for hh in heads:
                a = jnp.where(causal, _dot_nt(qi[hh][r], ki[hh][r]), 0.0).astype(BF16)
                st_ref[hh, n] = st[hh]
                parts[hh].append(_dot(a, v[hh][r]) + _dot_nt(qe[hh][r], st[hh].astype(BF16)))
                st[hh] = st[hh] * blk[hh]["dec"][n * HG_CHUNK:n * HG_CHUNK + 1] + _dot_tn(v[hh][r], kl[hh][r])
        for hh in heads:
            cols = _head_cols(hh)
            state[hh] = st[hh]
            o = jnp.concatenate(parts[hh], axis=0)
            opre_ref[:, cols] = o
            g = g_ref[:, cols]
            o_ref[:, cols] = o * lax.rsqrt(_rowmean(o * o) + RMS_EPS) * w_ref[:, cols] * (g * _sigmoid(g))

    groups = N_HEADS // hs
    wide = hs * HEAD_DIM
    col = lambda off: (lambda h, t: (t, off + h))
    nb = t_len // tb
    return _pallas(
        body, name="hgrn_forward", grid=(groups, nb), operands=(zhg, zhg, zhg, zhg, lb_raw, norm_w),
        out_shape=[jax.ShapeDtypeStruct((t_len, N_HEADS * HEAD_DIM), F32),
                   jax.ShapeDtypeStruct((t_len, N_HEADS * HEAD_DIM), F32),
                   jax.ShapeDtypeStruct((N_HEADS, t_len // HG_CHUNK, HEAD_DIM, HEAD_DIM), F32)],
        in_specs=[pl.BlockSpec((tb, wide), col(0)), pl.BlockSpec((tb, wide), col(groups)),
                  pl.BlockSpec((tb, wide), col(2 * groups)), pl.BlockSpec((tb, wide), col(3 * groups)),
                  pl.BlockSpec((2, wide), lambda h, t: (0, h)), pl.BlockSpec((1, wide), lambda h, t: (0, h))],
        out_specs=[pl.BlockSpec((tb, wide), col(0)), pl.BlockSpec((tb, wide), col(0)),
                   pl.BlockSpec((hs, n_chunks, HEAD_DIM, HEAD_DIM), lambda h, t: (h, t, 0, 0))],
        scratch_shapes=[pltpu.VMEM((hs, HEAD_DIM, HEAD_DIM), F32)],
        params=_params(40, ("arbitrary", "arbitrary")), exchange=exchange,
        first=lambda: (pl.program_id(0) == 0) & (pl.program_id(1) == 0),
        last=lambda: (pl.program_id(0) == groups - 1) & (pl.program_id(1) == nb - 1))


def _hgrn_backward(zhg, lb_raw, norm_w, o_pre, d_cat, states, exchange=None):
    t_len = zhg.shape[0]
    tb = min(512, t_len)
    n_chunks = tb // HG_CHUNK
    nb = t_len // tb
    hs = HG_STEP_HEADS

    def body(q_ref, f_ref, v_ref, g_ref, lb_ref, w_ref, opre_ref, do_ref, st_ref,
             dq_ref, df_ref, dv_ref, dg_ref, sums_ref, gstate):
        @pl.when(pl.program_id(1) == 0)
        def _():
            gstate[...] = jnp.zeros_like(gstate)
            sums_ref[...] = jnp.zeros_like(sums_ref)

        heads = range(hs)
        causal = _tri(True)
        row_id = lax.broadcasted_iota(jnp.int32, (HG_CHUNK, HEAD_DIM), 0)
        lb, d_o, blk, v, qi, ki, qe, kl = ({} for _ in range(8))
        for hh in heads:
            cols = _head_cols(hh)
            lb[hh] = _lower_bound(lb_ref[:, cols])
            w = w_ref[:, cols]
            o = opre_ref[:, cols]
            g = g_ref[:, cols]
            d_out = do_ref[:, cols]
            r = lax.rsqrt(_rowmean(o * o) + RMS_EPS)
            sg_g = _sigmoid(g)
            dg_ref[:, cols] = (d_out * (o * r * w) * (sg_g * (1.0 + g * (1.0 - sg_g)))).astype(BF16)
            d_on = d_out * (g * sg_g)
            sums_ref[1:2, cols] += _colsum(d_on * o * r)
            dy = d_on * w
            d_o[hh] = (r * dy - o * (r * r * r) * _rowmean(dy * o)).astype(BF16)
            blk[hh] = _hg_block(q_ref[:, cols], f_ref[:, cols], lb[hh])
            v[hh] = v_ref[:, cols].astype(BF16)
            qi[hh], ki[hh], qe[hh], kl[hh] = (blk[hh][name].astype(BF16) for name in ("qi", "ki", "qe", "kl"))
        gt = {hh: gstate[hh] for hh in heads}
        d_v, d_qi, d_ki, d_qe, d_kl, d_dec = ({hh: [None] * n_chunks for hh in heads} for _ in range(6))
        for n in reversed(range(n_chunks)):
            rows = _chunk_rows(n)
            for hh in heads:
                st = st_ref[hh, n]
                a = jnp.where(causal, _dot_nt(qi[hh][rows], ki[hh][rows]), 0.0).astype(BF16)
                d_a = jnp.where(causal, _dot_nt(d_o[hh][rows], v[hh][rows]), 0.0).astype(BF16)
                gt_b = gt[hh].astype(BF16)
                d_v[hh][n] = _dot_tn(a, d_o[hh][rows]) + _dot_nt(kl[hh][rows], gt_b)
                d_qi[hh][n] = _dot(d_a, ki[hh][rows])
                d_ki[hh][n] = _dot_tn(d_a, qi[hh][rows])
                d_qe[hh][n] = _dot(d_o[hh][rows], st.astype(BF16))
                d_kl[hh][n] = _dot(v[hh][rows], gt_b)
                d_dec[hh][n] = jnp.where(row_id == HG_CHUNK - 1, _colsum(gt[hh] * st), 0.0)
                gt[hh] = gt[hh] * blk[hh]["dec"][n * HG_CHUNK:n * HG_CHUNK + 1] + _dot_tn(d_o[hh][rows], qe[hh][rows])
        for hh in heads:
            cols = _head_cols(hh)
            b = blk[hh]
            gstate[hh] = gt[hh]
            dqi, dki, dqe, dkl, ddec = (jnp.concatenate(p[hh], axis=0) for p in (d_qi, d_ki, d_qe, d_kl, d_dec))
            dv_ref[:, cols] = jnp.concatenate(d_v[hh], axis=0).astype(BF16)
            dq_ref[:, cols] = (dqi * b["e_i"] + dqe * b["e_b"]).astype(BF16)
            d_k = dki * b["e_ri"] + dkl * b["e_l"]
            t_qi = dqi * b["qi"]
            t_ki = dki * b["ki"]
            t_kl = dkl * b["kl"]
            at_ref, at_last = [], []
            for n in range(n_chunks):
                rows = _chunk_rows(n)
                at_ref.append(jnp.where(row_id == HG_CHUNK // 2 - 1, _colsum(t_ki[rows] - t_qi[rows]), 0.0))
                at_last.append(jnp.where(row_id == HG_CHUNK - 1, _colsum(t_kl[rows]), 0.0))
            d_b = (t_qi - t_ki + dqe * b["qe"] - t_kl + jnp.concatenate(at_ref, axis=0)
                   + jnp.concatenate(at_last, axis=0) + ddec * b["dec"])
            d_forget = _chunk_prefix_sums(d_b, False) / b["forget"] - d_k
            sg = b["sg"]
            df_ref[:, cols] = (d_forget * (1.0 - lb[hh]) * sg * (1.0 - sg)).astype(BF16)
            sums_ref[0:1, cols] += _colsum(d_forget * (1.0 - sg))

    groups = N_HEADS // hs
    wide = hs * HEAD_DIM
    col = lambda off: (lambda h, t: (nb - 1 - t, off + h))
    return _pallas(
        body, name="hgrn_backward", grid=(groups, nb),
        operands=(zhg, zhg, zhg, zhg, lb_raw, norm_w, o_pre, d_cat, states),
        out_shape=[jax.ShapeDtypeStruct((t_len, N_HEADS * HEAD_DIM), BF16)] * 4
        + [jax.ShapeDtypeStruct((8, N_HEADS * HEAD_DIM), F32)],
        in_specs=[pl.BlockSpec((tb, wide), col(0)), pl.BlockSpec((tb, wide), col(groups)),
                  pl.BlockSpec((tb, wide), col(2 * groups)), pl.BlockSpec((tb, wide), col(3 * groups)),
                  pl.BlockSpec((2, wide), lambda h, t: (0, h)), pl.BlockSpec((1, wide), lambda h, t: (0, h)),
                  pl.BlockSpec((tb, wide), col(0)), pl.BlockSpec((tb, wide), col(0)),
                  pl.BlockSpec((hs, n_chunks, HEAD_DIM, HEAD_DIM), lambda h, t: (h, nb - 1 - t, 0, 0))],
        out_specs=[pl.BlockSpec((tb, wide), col(0))] * 4 + [pl.BlockSpec((8, wide), lambda h, t: (0, h))],
        scratch_shapes=[pltpu.VMEM((hs, HEAD_DIM, HEAD_DIM), F32)],
        params=_params(40, ("arbitrary", "arbitrary")), exchange=exchange,
        first=lambda: (pl.program_id(0) == 0) & (pl.program_id(1) == 0),
        last=lambda: (pl.program_id(0) == groups - 1) & (pl.program_id(1) == nb - 1))


ATT_LOG2 = ATT_SCALE * 1.4426950408889634


def _triangle_steps(nq, q_major):
    if q_major:
        pairs = [(i, j) for i in range(nq) for j in range(i + 1)]
    else:
        pairs = [(i, j) for j in range(nq) for i in range(j, nq)]
    return jnp.array([p[0] for p in pairs], jnp.int32), jnp.array([p[1] for p in pairs], jnp.int32)


def _key_le_query(t):
    return lax.broadcasted_iota(jnp.int32, (t, t), 0) <= lax.broadcasted_iota(jnp.int32, (t, t), 1)


def _attention_forward(q, k, v_t, exchange=None):
    t_len = q.shape[1]
    tq = min(512, t_len)
    nq = t_len // tq
    qi_tab, ki_tab = _triangle_steps(nq, True)

    def body(qi_ref, ki_ref, q_ref, k_ref, vt_ref, o_ref, lse_ref, m_s, l_s, acc_s):
        step = pl.program_id(0)
        qi, ki = qi_ref[step], ki_ref[step]

        @pl.when(ki == 0)
        def _():
            m_s[...] = jnp.full_like(m_s, NEG_BIG)
            l_s[...] = jnp.zeros_like(l_s)
            acc_s[...] = jnp.zeros_like(acc_s)

        def accumulate(masked):
            for h in range(N_HEADS):
                s_t = _dot_nt(k_ref[h], q_ref[h]) * ATT_LOG2
                if masked:
                    s_t = jnp.where(_key_le_query(tq), s_t, NEG_BIG)
                m_old = m_s[h]
                m_new = jnp.maximum(m_old, jnp.max(s_t, axis=0, keepdims=True))
                alpha = jnp.exp2(m_old - m_new)
                p_t = jnp.exp2(s_t - m_new)
                l_s[h] = alpha * l_s[h] + jnp.sum(p_t, axis=0, keepdims=True)
                acc_s[h] = alpha * acc_s[h] + _dot(vt_ref[h], p_t.astype(BF16))
                m_s[h] = m_new

        @pl.when(ki < qi)
        def _():
            accumulate(False)

        @pl.when(ki == qi)
        def _():
            accumulate(True)
            for h in range(N_HEADS):
                o_ref[:, h * HEAD_DIM:(h + 1) * HEAD_DIM] = jnp.transpose(acc_s[h] / l_s[h])
                lse_ref[h] = m_s[h] + jnp.log2(l_s[h])

    n_steps = qi_tab.shape[0]
    return _pallas(
        body, name="attention_forward", grid=(n_steps,), prefetch=(qi_tab, ki_tab), operands=(q, k, v_t),
        out_shape=[jax.ShapeDtypeStruct((t_len, N_HEADS * HEAD_DIM), F32),
                   jax.ShapeDtypeStruct((N_HEADS, 1, t_len), F32)],
        in_specs=[pl.BlockSpec((N_HEADS, tq, QK_DIM), lambda s, qt, kt: (0, qt[s], 0)),
                  pl.BlockSpec((N_HEADS, tq, QK_DIM), lambda s, qt, kt: (0, kt[s], 0)),
                  pl.BlockSpec((N_HEADS, HEAD_DIM, tq), lambda s, qt, kt: (0, 0, kt[s]))],
        out_specs=[pl.BlockSpec((tq, N_HEADS * HEAD_DIM), lambda s, qt, kt: (qt[s], 0)),
                   pl.BlockSpec((N_HEADS, 1, tq), lambda s, qt, kt: (0, 0, qt[s]))],
        scratch_shapes=[pltpu.VMEM((N_HEADS, 1, tq), F32), pltpu.VMEM((N_HEADS, 1, tq), F32),
                        pltpu.VMEM((N_HEADS, HEAD_DIM, tq), F32)],
        params=_params(48, ("arbitrary",)), exchange=exchange,
        first=lambda qt, kt: pl.program_id(0) == 0, last=lambda qt, kt: pl.program_id(0) == n_steps - 1)


BWD_HEADS = 4


def _attention_backward(q, k, k_t, v, d_cat, lse, delta, exchange=None):
    t_len = q.shape[1]
    tq = min(512, t_len)
    nq = t_len // tq
    hp = BWD_HEADS
    qi_tab, ki_tab = _triangle_steps(nq, False)

    def body(qi_ref, ki_ref, q_ref, k_ref, kt_ref, v_ref, do_ref, lse_ref, delta_ref, dqt_hbm, dk_ref, dv_ref,
             dqt_s, dk_s, dv_s):
        group, step = pl.program_id(0), pl.program_id(1)
        qi, ki = qi_ref[step], ki_ref[step]

        @pl.when(step == 0)
        def _():
            dqt_s[...] = jnp.zeros_like(dqt_s)

        @pl.when(qi == ki)
        def _():
            dk_s[...] = jnp.zeros_like(dk_s)
            dv_s[...] = jnp.zeros_like(dv_s)

        def accumulate(masked):
            for h in range(hp):
                do_b = do_ref[:, h * HEAD_DIM:(h + 1) * HEAD_DIM].astype(BF16)
                s_t = _dot_nt(k_ref[h], q_ref[h]) * ATT_LOG2
                if masked:
                    s_t = jnp.where(_key_le_query(tq), s_t, NEG_BIG)
                p_t = jnp.exp2(s_t - lse_ref[h])
                dp_t = _dot_nt(v_ref[h], do_b)
                ds_t = (p_t * (dp_t - delta_ref[h]) * ATT_SCALE).astype(BF16)
                dv_s[h] += _dot(p_t.astype(BF16), do_b)
                dk_s[h] += _dot(ds_t, q_ref[h])
                dqt_s[h, qi] += _dot(kt_ref[h], ds_t)

        @pl.when(ki < qi)
        def _():
            accumulate(False)

        @pl.when(ki == qi)
        def _():
            accumulate(True)
            for h in range(hp):
                pltpu.sync_copy(dqt_s.at[h, qi], dqt_hbm.at[group * hp + h, qi])

        @pl.when(qi == nq - 1)
        def _():
            dk_ref[...] = dk_s[...]
            dv_ref[...] = dv_s[...]

    wide = hp * HEAD_DIM
    n_groups, n_steps = N_HEADS // hp, qi_tab.shape[0]
    return _pallas(
        body, name="attention_backward", grid=(n_groups, n_steps), prefetch=(qi_tab, ki_tab),
        operands=(q, k, k_t, v, d_cat, lse, delta),
        out_shape=[jax.ShapeDtypeStruct((N_HEADS, nq, QK_DIM, tq), F32),
                   jax.ShapeDtypeStruct((N_HEADS, t_len, QK_DIM), F32),
                   jax.ShapeDtypeStruct((N_HEADS, t_len, HEAD_DIM), F32)],
        in_specs=[pl.BlockSpec((hp, tq, QK_DIM), lambda g, s, qt, kt: (g, qt[s], 0)),
                  pl.BlockSpec((hp, tq, QK_DIM), lambda g, s, qt, kt: (g, kt[s], 0)),
                  pl.BlockSpec((hp, QK_DIM, tq), lambda g, s, qt, kt: (g, 0, kt[s])),
                  pl.BlockSpec((hp, tq, HEAD_DIM), lambda g, s, qt, kt: (g, kt[s], 0)),
                  pl.BlockSpec((tq, wide), lambda g, s, qt, kt: (qt[s], n_groups + g)),
                  pl.BlockSpec((hp, 1, tq), lambda g, s, qt, kt: (g, 0, qt[s])),
                  pl.BlockSpec((hp, 1, tq), lambda g, s, qt, kt: (g, 0, qt[s]))],
        out_specs=[pl.BlockSpec(memory_space=pl.ANY),
                   pl.BlockSpec((hp, tq, QK_DIM), lambda g, s, qt, kt: (g, kt[s], 0)),
                   pl.BlockSpec((hp, tq, HEAD_DIM), lambda g, s, qt, kt: (g, kt[s], 0))],
        scratch_shapes=[pltpu.VMEM((hp, nq, QK_DIM, tq), F32), pltpu.VMEM((hp, tq, QK_DIM), F32),
                        pltpu.VMEM((hp, tq, HEAD_DIM), F32)],
        params=_params(58, ("arbitrary", "arbitrary")), exchange=exchange,
        first=lambda qt, kt: (pl.program_id(0) == 0) & (pl.program_id(1) == 0),
        last=lambda qt, kt: (pl.program_id(0) == n_groups - 1) & (pl.program_id(1) == n_steps - 1))


def _out_project(o_hg, o_mla, x, g_a, w_out, exchange=None):
    t_len = x.shape[0]
    tm = min(512, t_len)
    half = N_HEADS * HEAD_DIM

    def body(ohg_ref, omla_ref, x_ref, ga_ref, w_ref, cat_ref, mix_ref, xhat_ref, rstd_ref):
        a = ohg_ref[...].astype(BF16)
        b = omla_ref[...].astype(BF16)
        cat_ref[:, 0:half] = a
        cat_ref[:, half:2 * half] = b
        mix = _dot(a, w_ref[0:half, :]) + _dot(b, w_ref[half:2 * half, :])
        mix_ref[...] = mix
        r1 = DN_ALPHA * x_ref[...] + (1.0 + ga_ref[...]) * mix
        xc = r1 - _rowmean(r1)
        rstd = lax.rsqrt(_rowmean(xc * xc) + LN_EPS)
        xhat_ref[...] = xc * rstd
        rstd_ref[...] = rstd

    row = lambda i: (i, 0)
    fixed = lambda i: (0, 0)
    n_tiles = t_len // tm
    return _pallas(
        body, name="out_project", grid=(n_tiles,), operands=(o_hg, o_mla, x, g_a, w_out),
        out_shape=[jax.ShapeDtypeStruct((t_len, D_MODEL), BF16), jax.ShapeDtypeStruct((t_len, D_MODEL), F32),
                   jax.ShapeDtypeStruct((t_len, D_MODEL), F32), jax.ShapeDtypeStruct((t_len, 1), F32)],
        in_specs=[pl.BlockSpec((tm, half), row), pl.BlockSpec((tm, half), row), pl.BlockSpec((tm, D_MODEL), row),
                  pl.BlockSpec((1, D_MODEL), fixed), pl.BlockSpec((D_MODEL, D_MODEL), fixed)],
        out_specs=[pl.BlockSpec((tm, D_MODEL), row), pl.BlockSpec((tm, D_MODEL), row),
                   pl.BlockSpec((tm, D_MODEL), row), pl.BlockSpec((tm, 1), row)],
        params=_params(48, ("arbitrary",)), exchange=exchange,
        first=lambda: pl.program_id(0) == 0, last=lambda: pl.program_id(0) == n_tiles - 1)


V_LN1G, V_LN1B, V_SCM, V_SHM, V_GM, V_GA, V_LN2G, V_LN2B = range(8)
S_DLN2G, S_DLN2B, S_DGM, S_DSCM, S_DSHM, S_DLN1G, S_DLN1B, S_DGA, S_LOSS = range(9)


def _mlp_and_back(xhat1, rstd1, mix, target, o_mla, vecs, w1_top, w1_bottom, w2, w_out):
    t_len = xhat1.shape[0]
    tm = min(256, t_len)
    n_ff = w1_top.shape[0]
    ff = w1_top.shape[2]
    top_rows = w1_top.shape[1]

    def body(xhat_ref, rstd_ref, mix_ref, tgt_ref, omla_ref, vec_ref, w1_top_hbm, w1_bottom_hbm, w2_hbm, wout_hbm,
             act_ref, dhp_ref, um_ref, dh_ref, dmix_ref, dcat_ref, dr1_ref, sums_ref, delta_ref,
             w1_s, w2_s, wout_s, hp_s, load_sems):
        @pl.when(pl.program_id(0) == 0)
        def _():
            loads = [pltpu.make_async_copy(w1_top_hbm, w1_s.at[:, 0:top_rows], load_sems.at[0]),
                     pltpu.make_async_copy(w1_bottom_hbm, w1_s.at[:, top_rows:D_MODEL], load_sems.at[3]),
                     pltpu.make_async_copy(w2_hbm, w2_s, load_sems.at[1]),
                     pltpu.make_async_copy(wout_hbm, wout_s, load_sems.at[2])]
            for cp in loads:
                cp.start()
            sums_ref[...] = jnp.zeros_like(sums_ref)
            for cp in loads:
                cp.wait()

        vec = lambda r: vec_ref[r:r + 1, :]
        xhat = xhat_ref[...]
        x1 = xhat * vec(V_LN1G) + vec(V_LN1B)
        um = (x1 * (1.0 + vec(V_SCM)) + vec(V_SHM)).astype(BF16)
        um_ref[...] = um
        h = jnp.zeros((tm, D_MODEL), F32)
        for j in range(n_ff):
            hp = _dot(um, w1_s[j])
            hp_s[j] = hp
            act = jnp.square(jnp.maximum(hp, 0.0)).astype(BF16)
            act_ref[:, j * ff:(j + 1) * ff] = act
            h = h + _dot(act, w2_s[j])
        r2 = DN_ALPHA * x1 + (1.0 + vec(V_GM)) * h
        xc = r2 - _rowmean(r2)
        rstd2 = lax.rsqrt(_rowmean(xc * xc) + LN_EPS)
        xhat2 = xc * rstd2
        err = xhat2 * vec(V_LN2G) + vec(V_LN2B) - tgt_ref[...]
        loss = 0.5 * jnp.sum(_rowmean(err * err))
        dy = err * (1.0 / D_MODEL)
        dxh = dy * vec(V_LN2G)
        dr2 = rstd2 * (dxh - _rowmean(dxh) - xhat2 * _rowmean(dxh * xhat2))
        dh = ((1.0 + vec(V_GM)) * dr2).astype(BF16)
        dh_ref[...] = dh
        sums_ref[S_DLN2G:S_DLN2G + 1, :] += _colsum(dy * xhat2)
        sums_ref[S_DLN2B:S_DLN2B + 1, :] += _colsum(dy)
        sums_ref[S_DGM:S_DGM + 1, :] += _colsum(dr2 * h)
        sums_ref[S_LOSS:S_LOSS + 1, :] += jnp.full((1, D_MODEL), loss, F32)
        du = jnp.zeros((tm, D_MODEL), F32)
        for j in range(n_ff):
            dhp = (_dot_nt(dh, w2_s[j]) * (2.0 * jnp.maximum(hp_s[j], 0.0))).astype(BF16)
            dhp_ref[:, j * ff:(j + 1) * ff] = dhp
            du = du + _dot_nt(dhp, w1_s[j])
        sums_ref[S_DSCM:S_DSCM + 1, :] += _colsum(du * x1)
        sums_ref[S_DSHM:S_DSHM + 1, :] += _colsum(du)
        dx1 = DN_ALPHA * dr2 + du * (1.0 + vec(V_SCM))
        sums_ref[S_DLN1G:S_DLN1G + 1, :] += _colsum(dx1 * xhat)
        sums_ref[S_DLN1B:S_DLN1B + 1, :] += _colsum(dx1)
        dxh1 = dx1 * vec(V_LN1G)
        dr1 = rstd_ref[...] * (dxh1 - _rowmean(dxh1) - xhat * _rowmean(dxh1 * xhat))
        dr1_ref[...] = dr1
        sums_ref[S_DGA:S_DGA + 1, :] += _colsum(dr1 * mix_ref[...])
        dmix = ((1.0 + vec(V_GA)) * dr1).astype(BF16)
        dmix_ref[...] = dmix
        dcat = _dot_nt(dmix, wout_s[...])
        dcat_ref[...] = dcat
        ones = jnp.ones((8, HEAD_DIM), F32)
        half = N_HEADS * HEAD_DIM
        for hd in range(N_HEADS):
            prod = dcat[:, half + hd * HEAD_DIM:half + (hd + 1) * HEAD_DIM] * omla_ref[:, hd * HEAD_DIM:(hd + 1) * HEAD_DIM]
            delta_ref[hd] = lax.dot_general(ones, prod, (((1,), (1,)), ((), ())), preferred_element_type=F32,
                                            precision=lax.Precision.HIGHEST)[0:1]

    row = lambda i: (i, 0)
    fixed = lambda i: (0, 0)
    any_spec = pl.BlockSpec(memory_space=pl.ANY)
    return _pcall(
        body, name="mlp_and_back", grid=(t_len // tm,),
        out_shape=[jax.ShapeDtypeStruct((t_len, D_FF), BF16), jax.ShapeDtypeStruct((t_len, D_FF), BF16),
                   jax.ShapeDtypeStruct((t_len, D_MODEL), BF16), jax.ShapeDtypeStruct((t_len, D_MODEL), BF16),
                   jax.ShapeDtypeStruct((t_len, D_MODEL), BF16), jax.ShapeDtypeStruct((t_len, D_MODEL), F32),
                   jax.ShapeDtypeStruct((t_len, D_MODEL), F32), jax.ShapeDtypeStruct((16, D_MODEL), F32),
                   jax.ShapeDtypeStruct((N_HEADS, 1, t_len), F32)],
        in_specs=[pl.BlockSpec((tm, D_MODEL), row), pl.BlockSpec((tm, 1), row), pl.BlockSpec((tm, D_MODEL), row),
                  pl.BlockSpec((tm, D_MODEL), row), pl.BlockSpec((tm, N_HEADS * HEAD_DIM), row),
                  pl.BlockSpec((8, D_MODEL), fixed), any_spec, any_spec, any_spec, any_spec],
        out_specs=[pl.BlockSpec((tm, D_FF), row), pl.BlockSpec((tm, D_FF), row), pl.BlockSpec((tm, D_MODEL), row),
                   pl.BlockSpec((tm, D_MODEL), row), pl.BlockSpec((tm, D_MODEL), row), pl.BlockSpec((tm, D_MODEL), row),
                   pl.BlockSpec((tm, D_MODEL), row), pl.BlockSpec((16, D_MODEL), fixed),
                   pl.BlockSpec((N_HEADS, 1, tm), lambda i: (0, 0, i))],
        scratch_shapes=[pltpu.VMEM((n_ff, D_MODEL, ff), BF16), pltpu.VMEM(w2.shape, BF16), pltpu.VMEM(w_out.shape, BF16),
                        pltpu.VMEM((n_ff, tm, ff), F32), pltpu.SemaphoreType.DMA((4,))],
        compiler_params=_params(56, ("arbitrary",)),
        operands=(xhat1, rstd1, mix, target, o_mla, vecs, w1_top, w1_bottom, w2, w_out))


def _in_project_backward(dq, dk, dv, cq, ckv, pos, invf, q_norm_w, kv_norm_w, w_q, w_kv,
                         d_hq, d_hf, d_hi, d_hg, w_in, dr1, x, sc_a, exchange=None):
    t_len = x.shape[0]
    tm = min(512, t_len)
    per_q = dq.shape[3] // tm
    hgw = N_HEADS * HEAD_DIM

    def body(dq_ref, dk_ref, dv_ref, cq_ref, ckv_ref, pos_ref, invf_ref, qn_ref, kvn_ref, wq_ref, wkv_ref,
             dhq_ref, dhf_ref, dhi_ref, dhg_ref, win_ref, dr1_ref, x_ref, sc_ref,
             dz_ref, dqf_ref, dkvu_ref, cqn_ref, ckvn_ref, gx_ref, sums_ref):
        @pl.when(pl.program_id(0) == 0)
        def _():
            sums_ref[...] = jnp.zeros_like(sums_ref)

        cos_t, sin_t = _rope_tables(pos_ref[...], invf_ref[...])
        cq = cq_ref[...]
        ckv = ckv_ref[...]
        rq = lax.rsqrt(_rowmean(cq * cq) + RMS_EPS)
        rkv = lax.rsqrt(_rowmean(ckv * ckv) + RMS_EPS)
        cqn_ref[...] = (cq * rq * qn_ref[...]).astype(BF16)
        ckvn_ref[...] = (ckv * rkv * kvn_ref[...]).astype(BF16)
        d_cqn = jnp.zeros((tm, Q_RANK), F32)
        d_ckvn = jnp.zeros((tm, KV_RANK), F32)
        d_kpe = jnp.zeros((tm, 128), F32)
        for h in range(N_HEADS):
            dqh = jnp.transpose(dq_ref[h])
            dqf_ref[h, :, 0:HEAD_DIM] = dqh[:, :HEAD_DIM].astype(BF16)
            dqf_ref[h, :, HEAD_DIM:QK_DIM] = _unrope(dqh[:, HEAD_DIM:], cos_t, sin_t).astype(BF16)
            d_cqn = d_cqn + _dot_nt(dqf_ref[h], wq_ref[h])
            dkh = dk_ref[h]
            d_kpe = d_kpe + dkh[:, HEAD_DIM:]
            dkvu_ref[h, :, 0:HEAD_DIM] = dkh[:, :HEAD_DIM].astype(BF16)
            dkvu_ref[h, :, HEAD_DIM:2 * HEAD_DIM] = dv_ref[h].astype(BF16)
            d_ckvn = d_ckvn + _dot_nt(dkvu_ref[h], wkv_ref[h])
        dyq = d_cqn * qn_ref[...]
        dykv = d_ckvn * kvn_ref[...]
        sums_ref[2:3, 0:Q_RANK] += _colsum(d_cqn * cq * rq)
        sums_ref[3:4, 0:KV_RANK] += _colsum(d_ckvn * ckv * rkv)
        dz_ref[:, 0:hgw] = dhq_ref[...]
        dz_ref[:, hgw:2 * hgw] = dhf_ref[...]
        dz_ref[:, 2 * hgw:3 * hgw] = dhi_ref[...]
        dz_ref[:, 3 * hgw:4 * hgw] = dhg_ref[...]
        dz_ref[:, HG_COLS:HG_COLS + Q_RANK] = (rq * dyq - cq * (rq * rq * rq) * _rowmean(dyq * cq)).astype(BF16)
        dz_ref[:, HG_COLS + Q_RANK:HG_COLS + Q_RANK + KV_RANK] = (
            rkv * dykv - ckv * (rkv * rkv * rkv) * _rowmean(dykv * ckv)).astype(BF16)
        dz_ref[:, HG_COLS + Q_RANK + KV_RANK:] = _unrope(d_kpe, cos_t, sin_t).astype(BF16)
        du = _dot(dz_ref[...], win_ref[...])
        xv = x_ref[...]
        gx_ref[...] = DN_ALPHA * dr1_ref[...] + (1.0 + sc_ref[...]) * du
        sums_ref[0:1, :] += _colsum(du * xv)
        sums_ref[1:2, :] += _colsum(du)

    row = lambda i: (i, 0)
    fixed2 = lambda i: (0, 0)
    fixed3 = lambda i: (0, 0, 0)
    heads = lambda i: (0, i, 0)
    n_tiles = t_len // tm
    return _pallas(
        body, name="in_project_backward", grid=(n_tiles,),
        operands=(dq, dk, dv, cq, ckv, pos, invf, q_norm_w, kv_norm_w, w_q, w_kv, d_hq, d_hf, d_hi, d_hg, w_in, dr1, x,
                  sc_a),
        out_shape=[jax.ShapeDtypeStruct((t_len, IN_COLS_PAD), BF16), jax.ShapeDtypeStruct((N_HEADS, t_len, QK_DIM), BF16),
                   jax.ShapeDtypeStruct((N_HEADS, t_len, 2 * HEAD_DIM), BF16), jax.ShapeDtypeStruct((t_len, Q_RANK), BF16),
                   jax.ShapeDtypeStruct((t_len, KV_RANK), BF16), jax.ShapeDtypeStruct((t_len, D_MODEL), F32),
                   jax.ShapeDtypeStruct((8, D_MODEL), F32)],
        in_specs=[pl.BlockSpec((N_HEADS, None, QK_DIM, tm), lambda i: (0, i // per_q, 0, i % per_q)),
                  pl.BlockSpec((N_HEADS, tm, QK_DIM), heads),
                  pl.BlockSpec((N_HEADS, tm, HEAD_DIM), heads), pl.BlockSpec((tm, Q_RANK), row),
                  pl.BlockSpec((tm, KV_RANK), row), pl.BlockSpec((tm, 1), row), pl.BlockSpec((1, 128), fixed2),
                  pl.BlockSpec((1, Q_RANK), fixed2), pl.BlockSpec((1, KV_RANK), fixed2),
                  pl.BlockSpec((N_HEADS, Q_RANK, QK_DIM), fixed3), pl.BlockSpec((N_HEADS, KV_RANK, 2 * HEAD_DIM), fixed3),
                  pl.BlockSpec((tm, hgw), row), pl.BlockSpec((tm, hgw), row), pl.BlockSpec((tm, hgw), row),
                  pl.BlockSpec((tm, hgw), row), pl.BlockSpec((IN_COLS_PAD, D_MODEL), fixed2),
                  pl.BlockSpec((tm, D_MODEL), row), pl.BlockSpec((tm, D_MODEL), row), pl.BlockSpec((1, D_MODEL), fixed2)],
        out_specs=[pl.BlockSpec((tm, IN_COLS_PAD), row), pl.BlockSpec((N_HEADS, tm, QK_DIM), heads),
                   pl.BlockSpec((N_HEADS, tm, 2 * HEAD_DIM), heads), pl.BlockSpec((tm, Q_RANK), row),
                   pl.BlockSpec((tm, KV_RANK), row), pl.BlockSpec((tm, D_MODEL), row), pl.BlockSpec((8, D_MODEL), fixed2)],
        params=_params(48, ("arbitrary",)), exchange=exchange,
        first=lambda: pl.program_id(0) == 0, last=lambda: pl.program_id(0) == n_tiles - 1)


def _weight_grad(a, b, name, n_blocks, bn, a_blocked=False, b_blocked=True, exchange=None, token_tile=512):
    t_len = a.shape[0]
    m = a.shape[1] // n_blocks if a_blocked else a.shape[1]
    bt = min(token_tile, t_len)

    def body(a_ref, b_ref, o_ref):
        @pl.when(pl.program_id(1) == 0)
        def _():
            o_ref[...] = jnp.zeros_like(o_ref)

        o_ref[...] += _dot_tn(a_ref[...].astype(BF16), b_ref[...].astype(BF16))

    a_spec = pl.BlockSpec((bt, m), (lambda n, t: (t, n)) if a_blocked else (lambda n, t: (t, 0)))
    if b.ndim == 3:
        b_spec = pl.BlockSpec((None, bt, bn), lambda n, t: (n, t, 0))
    else:
        b_spec = pl.BlockSpec((bt, bn), (lambda n, t: (t, n)) if b_blocked else (lambda n, t: (t, 0)))
    nt = t_len // bt
    (out,), landed = _pallas(
        body, name=name, grid=(n_blocks, nt), operands=(a, b),
        out_shape=[jax.ShapeDtypeStruct((n_blocks, m, bn), F32)],
        in_specs=[a_spec, b_spec],
        out_specs=[pl.BlockSpec((None, m, bn), lambda n, t: (n, 0, 0))],
        params=_params(56, ("arbitrary", "arbitrary")), exchange=exchange,
        first=lambda: (pl.program_id(0) == 0) & (pl.program_id(1) == 0),
        last=lambda: (pl.program_id(0) == n_blocks - 1) & (pl.program_id(1) == nt - 1))
    return (out, landed) if exchange else out


SMALL_PLACE = {"ln1_g": (6, 0), "ln1_b": (7, 0), "ln2_g": (8, 0), "ln2_b": (9, 0), "hg_norm_w": (10, 512),
               "mla_q_norm_w": (11, 0), "mla_kv_norm_w": (11, Q_RANK)}
SMALL_LB_ROW, SMALL_LOSS_ROW = 10, 12


def _small_params_step(gathered, params):
    names = list(params)

    def body(g_ref, *refs):
        ins, outs = refs[:3 * len(names)], refs[3 * len(names):]
        loss_ref, outs = outs[0], outs[1:]
        tot = g_ref[0]
        for d in range(1, N_DEV):
            tot = tot + g_ref[d]
        loss_ref[...] = tot[SMALL_LOSS_ROW:SMALL_LOSS_ROW + 1, 0:128]

        def update(i, grad, rows=slice(None), lanes=slice(None)):
            w_ref, m_ref, v_ref = ins[3 * i:3 * i + 3]
            g_out, d_out, nm_out, nv_out = outs[4 * i:4 * i + 4]
            g_out[rows, lanes] = grad
            d_out[rows, lanes], nm_out[rows, lanes], nv_out[rows, lanes] = _adamw_update(
                w_ref[rows, lanes], grad, m_ref[rows, lanes], v_ref[rows, lanes])

        for i, name in enumerate(names):
            if name == "b_ada":
                for r in range(6):
                    update(i, tot[r:r + 1, :], lanes=slice(r * D_MODEL, (r + 1) * D_MODEL))
            elif name == "hg_lower_bounds":
                lb = _lower_bound(ins[3 * i][...])
                d0 = tot[SMALL_LB_ROW:SMALL_LB_ROW + 1, 0:512] * lb * (1.0 - lb)
                update(i, d0, rows=slice(0, 1))
                update(i, -d0, rows=slice(1, 2))
            else:
                row, lane = SMALL_PLACE[name]
                update(i, tot[row:row + 1, lane:lane + params[name][0].shape[1]])

    flat_in = [a for name in names for a in params[name]]
    shapes = [jax.ShapeDtypeStruct((1, 128), F32)] + [jax.ShapeDtypeStruct(params[name][0].shape, F32)
                                                      for name in names for _ in range(4)]
    out = pl.pallas_call(body, name="small_params_step", out_shape=shapes)(gathered, *flat_in)
    return out[0], {name: out[1 + 4 * i:5 + 4 * i] for i, name in enumerate(names)}


def _adamw_update(w, gv, m, v):
    nm = ADAM_B1 * m + (1.0 - ADAM_B1) * gv
    nv = ADAM_B2 * v + (1.0 - ADAM_B2) * jnp.square(gv)
    m_hat = nm / (1.0 - ADAM_B1 ** ADAM_STEP)
    v_hat = nv / (1.0 - ADAM_B2 ** ADAM_STEP)
    return -ADAM_LR * (m_hat / (jnp.sqrt(v_hat) + ADAM_EPS) + ADAM_WD * w), nm, nv


def _adamw_halves(core, w, mine, theirs, m, v, name):
    rows, cols = w.shape
    h = rows // 2
    tr = _row_tile(h)
    per_half = h // tr

    def body(core_ref, w_ref, mine_ref, theirs_ref, m_ref, v_ref, g_ref, d_ref, nm_ref, nv_ref):
        is_mine = pl.program_id(0) // per_half == core_ref[0]
        gv = jnp.where(is_mine, mine_ref[...], theirs_ref[...])
        g_ref[...] = gv
        d_ref[...], nm_ref[...], nv_ref[...] = _adamw_update(w_ref[...], gv, m_ref[...], v_ref[...])

    full = pl.BlockSpec((tr, cols), lambda i, core_ref: (i, 0))
    part = pl.BlockSpec((tr, cols), lambda i, core_ref: (i % per_half, 0))
    return _pcall(
        body, name=name, out_shape=[jax.ShapeDtypeStruct(w.shape, F32)] * 4,
        grid_spec=pltpu.PrefetchScalarGridSpec(
            num_scalar_prefetch=1, grid=(rows // tr,), in_specs=[full, part, part, full, full], out_specs=[full] * 4),
        compiler_params=_params(40, ("arbitrary",)),
        operands=(core, w, mine, theirs, m, v))


def _adamw(w, g, m, v, name):
    rows, cols = w.shape
    tr = _row_tile(rows) if rows >= 8 else rows

    def body(w_ref, g_ref, m_ref, v_ref, d_ref, nm_ref, nv_ref):
        d_ref[...], nm_ref[...], nv_ref[...] = _adamw_update(w_ref[...], g_ref[...], m_ref[...], v_ref[...])

    spec = pl.BlockSpec((tr, cols), lambda i: (i, 0))
    return _pcall(
        body, name=name, grid=(rows // tr,),
        out_shape=[jax.ShapeDtypeStruct(w.shape, F32)] * 3,
        in_specs=[spec] * 4, out_specs=[spec] * 3,
        compiler_params=_params(40, ("arbitrary",)),
        operands=(w, g, m, v))


def kernel(x, c, positions, w_ada, b_ada, w_in, hg_lower_bounds, hg_norm_w, mla_q_norm_w, w_q_up, mla_kv_norm_w, w_kv_up, w_out, ln1_g, ln1_b, w_mlp_in, w_mlp_out, ln2_g, ln2_b, loss_target, m_w_ada, m_b_ada, m_w_in, m_hg_lower_bounds, m_hg_norm_w, m_mla_q_norm_w, m_w_q_up, m_mla_kv_norm_w, m_w_kv_up, m_w_out, m_ln1_g, m_ln1_b, m_w_mlp_in, m_w_mlp_out, m_ln2_g, m_ln2_b, v_w_ada, v_b_ada, v_w_in, v_hg_lower_bounds, v_hg_norm_w, v_mla_q_norm_w, v_w_q_up, v_mla_kv_norm_w, v_w_kv_up, v_w_out, v_ln1_g, v_ln1_b, v_w_mlp_in, v_w_mlp_out, v_ln2_g, v_ln2_b):
    ix, iy, ic = _mesh_pos()
    chip = 2 * ix + iy
    me = 4 * ix + 2 * iy + ic
    core_arr = jnp.reshape(ic, (1,)).astype(jnp.int32)
    chip_arr = jnp.reshape(chip, (1,)).astype(jnp.int32)

    xs = x[0]
    target = loss_target[0]
    t_len = xs.shape[0]
    pos = positions.astype(F32).reshape(t_len, 1)
    inv = 1.0 / (ROPE_THETA ** (jnp.arange(0, ROPE_DIM, 2, dtype=F32) / ROPE_DIM))
    invf = jnp.concatenate([inv, inv, jnp.zeros((128 - ROPE_DIM,), F32)]).reshape(1, 128)

    def slot(w):
        rows, cols = w.shape
        own = w.astype(BF16).reshape(1, 2, rows // 2, cols)
        return lax.dynamic_update_slice(jnp.zeros((N_CHIPS, 2, rows // 2, cols), BF16), own, (chip, 0, 0, 0))

    def slot8(a):
        return lax.dynamic_update_slice(jnp.zeros((N_DEV,) + a.shape, a.dtype), a[None], (me, 0, 0))

    def whole(s):
        return s.reshape(N_CHIPS, 2 * s.shape[2], s.shape[3])

    def halved(g):
        return g.reshape(N_CHIPS, 2, g.shape[1] // 2, g.shape[2])

    ada_cols = w_ada.shape[2]
    c_all, *early = _run_exchange(
        _merge(_gather_all(slot8(jnp.broadcast_to(c, (8, D_MODEL)))),
               _gather_over_ici([slot(jnp.transpose(w_in[0])), slot(w_q_up[0]), slot(w_kv_up[0])])),
        "gather_c_and_mixer_weights_ici")
    b_shard = lax.dynamic_slice(b_ada, (0, chip * ada_cols), (1, ada_cols))
    mod_cols, cond16 = _ada_project(c_all[:, 0, :], w_ada[0], b_shard)
    mod_all, *early = _run_exchange(_merge(_gather_all(slot8(mod_cols)), _gather_over_d2d(early)),
                                    "gather_mod_and_mixer_weights_d2d")
    mod_mine = lax.dynamic_slice(mod_all, (0, me, 0), (N_DEV, 1, ada_cols))[::2, 0, :].reshape(6, D_MODEL)
    sh_a, sc_a, g_a, sh_m, sc_m, g_m = (mod_mine[i:i + 1] for i in range(6))
    g_in, g_q, g_kv = (whole(s) for s in early)
    w_in_full = jnp.pad(g_in.reshape(IN_COLS, D_MODEL), ((0, IN_COLS_PAD - IN_COLS), (0, 0)))
    w_q_full = jnp.pad(g_q, ((0, 0), (0, 0), (0, QK_DIM - g_q.shape[2])))

    w1_rows = D_MODEL // 2
    (u_a, zhg, cq, ckv, q, k, k_t, v, v_t), (s_top, s_out) = _in_project(
        xs, pos, sc_a, sh_a, w_in_full, mla_q_norm_w, mla_kv_norm_w, w_q_full, g_kv, invf,
        _gather_over_ici([slot(w_mlp_in[0, :w1_rows]), slot(w_out[0])]))
    (o_pre, o_hg, states), (s_bottom, s_top, s_out) = _hgrn_forward(
        zhg, hg_lower_bounds, hg_norm_w,
        _merge(_gather_over_ici([slot(w_mlp_in[0, w1_rows:])]), _gather_over_d2d([s_top, s_out])))
    (o_mla, lse), (s_w2, s_bottom) = _attention_forward(
        q, k, v_t, _merge(_gather_over_ici([slot(w_mlp_out[0])]), _gather_over_d2d([s_bottom])))
    w_out_full = whole(s_out).reshape(D_MODEL, D_MODEL)
    (cat, mix, xhat1, rstd1), (s_w2,) = _out_project(o_hg, o_mla, xs, g_a, w_out_full, _gather_over_d2d([s_w2]))
    g_w1_top, g_w1_bottom, g_w2 = whole(s_top), whole(s_bottom), whole(s_w2)
    vecs = jnp.concatenate([ln1_g, ln1_b, sc_m, sh_m, g_m, g_a, ln2_g, ln2_b], axis=0)
    act, dhp, um, dh, dmix, d_cat, dr1, mlp_sums, delta = _mlp_and_back(
        xhat1, rstd1, mix, target, o_mla, vecs, g_w1_top, g_w1_bottom, g_w2, w_out_full)

    gw_1 = halved(_weight_grad(um, dhp, "grad_w_mlp_in", N_CHIPS, D_FF // N_CHIPS, token_tile=4096))
    gw_2, (landed_1,) = _weight_grad(act, dh, "grad_w_mlp_out", N_CHIPS, D_MODEL, a_blocked=True, b_blocked=False,
                                     token_tile=4096, exchange=_pair_exchange([gw_1]))
    gw_out = _weight_grad(cat, dmix, "grad_w_out", 1, D_MODEL, token_tile=2048)
    later = [halved(gw_2), halved(gw_out.reshape(N_CHIPS, D_MODEL // N_CHIPS, D_MODEL))]
    own_1, travels_1 = _add_pair(core_arr, chip_arr, gw_1, landed_1)
    (dq, dk, dv), (landed_1, *landed) = _attention_backward(
        q, k, k_t, v, d_cat, lse, delta, _merge(_chip_exchange([travels_1]), _pair_exchange(later)))
    mine_1 = _add_chips(own_1, landed_1)
    chip_sums = [_add_pair(core_arr, chip_arr, g, l) for g, l in zip(later, landed)]
    (d_hq, d_hf, d_hi, d_hg, hg_sums), (theirs_1, *landed) = _hgrn_backward(
        zhg, hg_lower_bounds, hg_norm_w, o_pre, d_cat, states,
        _merge(_pair_send([mine_1]), _chip_exchange([b for _, b in chip_sums])))
    later_mine = [_add_chips(own, l) for (own, _), l in zip(chip_sums, landed)]
    mlp_mine = [mine_1] + later_mine
    (dz, dqf, dkvu, cqn, ckvn, grad_x, in_sums), _ = _in_project_backward(
        dq, dk, dv, cq, ckv, pos, invf, mla_q_norm_w, mla_kv_norm_w, w_q_full, g_kv,
        d_hq, d_hf, d_hi, d_hg, w_in_full, dr1, xs, sc_a)

    zeros = lambda n: jnp.zeros((1, n), F32)
    small = jnp.concatenate([
        in_sums[1:2], in_sums[0:1], mlp_sums[S_DGA:S_DGA + 1],
        mlp_sums[S_DSHM:S_DSHM + 1], mlp_sums[S_DSCM:S_DSCM + 1], mlp_sums[S_DGM:S_DGM + 1],
        mlp_sums[S_DLN1G:S_DLN1G + 1], mlp_sums[S_DLN1B:S_DLN1B + 1],
        mlp_sums[S_DLN2G:S_DLN2G + 1], mlp_sums[S_DLN2B:S_DLN2B + 1],
        jnp.concatenate([hg_sums[0:1], hg_sums[1:2]], axis=1),
        jnp.concatenate([in_sums[2:3, :Q_RANK], in_sums[3:4, :KV_RANK], zeros(D_MODEL - Q_RANK - KV_RANK)], axis=1),
        mlp_sums[S_LOSS:S_LOSS + 1],
        jnp.zeros((SMALL_ROWS - 13, D_MODEL), F32)], axis=0)

    gw_in, (*later_theirs, small_all) = _weight_grad(
        dz, u_a, "grad_w_in", 3, D_MODEL, a_blocked=True, b_blocked=False, token_tile=4096,
        exchange=_merge(_pair_send(later_mine), _gather_all(slot8(small))))
    mlp_theirs = [theirs_1] + list(later_theirs)
    gw_in = gw_in.reshape(IN_COLS_PAD, D_MODEL)
    gw_q = _weight_grad(cqn, dqf, "grad_w_q_up", N_HEADS, QK_DIM, token_tile=2048)[:, :, :HEAD_DIM + ROPE_DIM]
    gw_kv = _weight_grad(ckvn, dkvu, "grad_w_kv_up", N_HEADS, 2 * HEAD_DIM, token_tile=2048)
    flat = lambda g: g.reshape(g.shape[0] * g.shape[1], g.shape[2])
    mixer_mine, mixer_theirs = _reduce_in_vmem(
        [gw_in, flat(gw_q), flat(gw_kv)], [IN_COLS // N_CHIPS // 2, Q_RANK // 2, KV_RANK // 2], "reduce_mixer_grads")
    reduced = ("w_in", "w_q_up", "w_kv_up", "w_mlp_in", "w_mlp_out", "w_out")
    halves_mine = dict(zip(reduced, list(mixer_mine) + mlp_mine))
    halves_theirs = dict(zip(reduced, list(mixer_theirs) + list(mlp_theirs)))

    small_names = ("b_ada", "hg_lower_bounds", "hg_norm_w", "mla_q_norm_w", "mla_kv_norm_w",
                   "ln1_g", "ln1_b", "ln2_g", "ln2_b")
    loss_row, small_out = _small_params_step(small_all, {
        "b_ada": (b_ada, m_b_ada, v_b_ada),
        "hg_lower_bounds": (hg_lower_bounds, m_hg_lower_bounds, v_hg_lower_bounds),
        "hg_norm_w": (hg_norm_w, m_hg_norm_w, v_hg_norm_w),
        "mla_q_norm_w": (mla_q_norm_w, m_mla_q_norm_w, v_mla_q_norm_w),
        "mla_kv_norm_w": (mla_kv_norm_w, m_mla_kv_norm_w, v_mla_kv_norm_w),
        "ln1_g": (ln1_g, m_ln1_g, v_ln1_g), "ln1_b": (ln1_b, m_ln1_b, v_ln1_b),
        "ln2_g": (ln2_g, m_ln2_g, v_ln2_g), "ln2_b": (ln2_b, m_ln2_b, v_ln2_b)})
    loss = loss_row[0, 0]

    d_mod_all = small_all[:, 0:6, :].reshape(N_DEV, 6 * D_MODEL)
    d_mod_cols = lax.dynamic_slice(d_mod_all, (0, chip * ada_cols), (N_DEV, ada_cols))
    d_mod_cols = jnp.concatenate([d_mod_cols, jnp.zeros_like(d_mod_cols)], axis=0)
    g_w_ada = _weight_grad(cond16, d_mod_cols, "grad_w_ada", 1, ada_cols)[0]

    names = ["w_ada", "b_ada", "w_in", "hg_lower_bounds", "hg_norm_w", "mla_q_norm_w", "w_q_up", "mla_kv_norm_w",
             "w_kv_up", "w_out", "ln1_g", "ln1_b", "w_mlp_in", "w_mlp_out", "ln2_g", "ln2_b"]
    weights = [w_ada, b_ada, w_in, hg_lower_bounds, hg_norm_w, mla_q_norm_w, w_q_up, mla_kv_norm_w,
               w_kv_up, w_out, ln1_g, ln1_b, w_mlp_in, w_mlp_out, ln2_g, ln2_b]
    moms = [m_w_ada, m_b_ada, m_w_in, m_hg_lower_bounds, m_hg_norm_w, m_mla_q_norm_w, m_w_q_up, m_mla_kv_norm_w,
            m_w_kv_up, m_w_out, m_ln1_g, m_ln1_b, m_w_mlp_in, m_w_mlp_out, m_ln2_g, m_ln2_b]
    vels = [v_w_ada, v_b_ada, v_w_in, v_hg_lower_bounds, v_hg_norm_w, v_mla_q_norm_w, v_w_q_up, v_mla_kv_norm_w,
            v_w_kv_up, v_w_out, v_ln1_g, v_ln1_b, v_w_mlp_in, v_w_mlp_out, v_ln2_g, v_ln2_b]
    out_g, out_d, out_m, out_v = [], [], [], []
    for name, w, m, vv in zip(names, weights, moms, vels):
        if name in small_names:
            g, d, nm, nv = small_out[name]
            back = lambda a: a
        elif name == "w_in":
            to2d, back = (lambda a: jnp.transpose(a[0])), (lambda a: jnp.transpose(a)[None])
        else:
            to2d, back = (lambda a, s=w.shape[1:]: a.reshape(s)), (lambda a, s=w.shape: a.reshape(s))
        if name == "w_ada":
            d, nm, nv = _adamw(to2d(w), g_w_ada, to2d(m), to2d(vv), "adamw_" + name)
            g = g_w_ada
        elif name not in small_names:
            g, d, nm, nv = _adamw_halves(core_arr, to2d(w), halves_mine[name], halves_theirs[name], to2d(m), to2d(vv),
                                         "adamw_" + name)
        out_g.append(back(g))
        out_d.append(back(d))
        out_m.append(back(nm))
        out_v.append(back(nv))
    return (loss, grad_x[None], *out_g, *out_d, *out_m, *out_v)
```

```python
import functools

import jax
import jax.numpy as jnp
from jax import lax
from jax.experimental import pallas as pl
from jax.experimental.pallas import tpu as pltpu

F32 = jnp.float32
BF16 = jnp.bfloat16
MESH_IDS = pl.DeviceIdType.MESH

D_MODEL = 1024
N_HEADS = 4
HEAD_DIM = 128
ROPE_DIM = 64
HG_CHUNK = 64
HG_COLS = 2048
Q_RANK = 256
KV_RANK = 256
IN_COLS = 2624
IN_COLS_PAD = 2688
QK_DIM = 256
D_FF = 4096
N_CHIPS = 4
N_DEV = 8
ROPE_THETA = 10000.0
RMS_EPS = 1e-6
LN_EPS = 1e-5
DN_ALPHA = 2.0 ** 0.25
ATT_SCALE = (HEAD_DIM + ROPE_DIM) ** -0.5
NEG_BIG = -1e30
ADAM_LR = 0.001
ADAM_B1 = 0.9
ADAM_B2 = 0.999
ADAM_EPS = 1e-08
ADAM_WD = 0.01
ADAM_STEP = 10
SMALL_ROWS = 16
MIB = 1024 * 1024


def _dot(a, b):
    return jnp.dot(a, b, preferred_element_type=F32)


def _dot_nt(a, b):
    return lax.dot_general(a, b, (((1,), (1,)), ((), ())), preferred_element_type=F32)


def _dot_tn(a, b):
    return lax.dot_general(a, b, (((0,), (0,)), ((), ())), preferred_element_type=F32)


def _params(vmem_mib, semantics=None):
    return pltpu.CompilerParams(vmem_limit_bytes=vmem_mib * MIB, dimension_semantics=semantics)


def _sigmoid(v):
    return 1.0 / (1.0 + jnp.exp(-v))


def _colsum(v):
    return jnp.sum(v, axis=0, keepdims=True)


def _rowmean(v):
    return jnp.mean(v, axis=-1, keepdims=True)


def _rope_tables(pos, invf):
    ang = pos * invf
    lane = lax.broadcasted_iota(jnp.int32, ang.shape, 1)
    cos_t = jnp.where(lane < ROPE_DIM, jnp.cos(ang), 0.0)
    sin = jnp.sin(ang)
    sin_t = jnp.where(lane < ROPE_DIM // 2, -sin, jnp.where(lane < ROPE_DIM, sin, 0.0))
    return cos_t, sin_t


def _swap_halves(t):
    lane = lax.broadcasted_iota(jnp.int32, t.shape, 1)
    return jnp.where(lane < ROPE_DIM // 2, pltpu.roll(t, 128 - ROPE_DIM // 2, 1), pltpu.roll(t, ROPE_DIM // 2, 1))


def _rope(t, cos_t, sin_t):
    return t * cos_t + _swap_halves(t) * sin_t


def _unrope(g, cos_t, sin_t):
    return g * cos_t - _swap_halves(g) * sin_t


def _mesh_pos():
    return lax.axis_index("x"), lax.axis_index("y"), lax.axis_index("c")


def _other_chips(x, y):
    out = []
    for dx, dy in ((1, 0), (0, 1), (1, 1)):
        px = 1 - x if dx else x
        py = 1 - y if dy else y
        out.append(((px, py), 2 * px + py))
    return out


class _Exchange:
    def __init__(self, inputs, out_shapes, aliases, sems, start, finish):
        self.inputs, self.out_shapes, self.aliases, self.sems = list(inputs), list(out_shapes), dict(aliases), list(sems)
        self.start, self.finish = start, finish


def _from_copies(inputs, out_shapes, aliases, sems, copies):
    def start(ins, outs, sem_refs):
        for send, _ in copies(ins, outs, sem_refs):
            send.start()

    def finish(ins, outs, sem_refs):
        for send, recv in copies(ins, outs, sem_refs):
            recv.wait_recv()
            send.wait_send()

    return _Exchange(inputs, out_shapes, aliases, sems, start, finish)


HBM_MIN_BYTES = 256 * 1024


def _in_hbm(a):
    if a.size * a.dtype.itemsize < HBM_MIN_BYTES:
        return a
    return pltpu.with_memory_space_constraint(a, pltpu.HBM)


def _out_hbm(s):
    if s.size * s.dtype.itemsize < HBM_MIN_BYTES:
        return s
    return pltpu.HBM(s.shape, s.dtype)


def _pcall(body, *, operands, out_shape, **kwargs):
    single = not isinstance(out_shape, (list, tuple))
    shapes = [_out_hbm(s) for s in ([out_shape] if single else out_shape)]
    return pl.pallas_call(body, out_shape=shapes[0] if single else shapes, **kwargs)(*[_in_hbm(a) for a in operands])


def _run_exchange(exchange, name):
    n_in, n_out = len(exchange.inputs), len(exchange.out_shapes)

    def body(*refs):
        ins, outs, sem_refs = refs[:n_in], refs[n_in:n_in + n_out], refs[n_in + n_out:]
        exchange.start(ins, outs, sem_refs)
        exchange.finish(ins, outs, sem_refs)

    any_spec = pl.BlockSpec(memory_space=pl.ANY)
    return pl.pallas_call(
        body, name=name, out_shape=[_out_hbm(s) for s in exchange.out_shapes],
        in_specs=[any_spec] * n_in, out_specs=[any_spec] * n_out,
        scratch_shapes=exchange.sems, input_output_aliases=exchange.aliases,
    )(*[_in_hbm(a) for a in exchange.inputs])


def _pallas(body, *, name, operands, in_specs, out_shape, out_specs, params, scratch_shapes=(), grid=(), prefetch=(),
            exchange=None, first=None, last=None):
    n_pre, n_in, n_out, n_scr = len(prefetch), len(in_specs), len(out_specs), len(scratch_shapes)
    ex_in = exchange.inputs if exchange else []
    ex_out = exchange.out_shapes if exchange else []
    ex_sems = exchange.sems if exchange else []

    def full_body(*refs):
        pre, rest = refs[:n_pre], refs[n_pre:]
        ins, rest = rest[:n_in], rest[n_in:]
        xin, rest = rest[:len(ex_in)], rest[len(ex_in):]
        outs, rest = rest[:n_out], rest[n_out:]
        xout, rest = rest[:len(ex_out)], rest[len(ex_out):]
        scr, sem_refs = rest[:n_scr], rest[n_scr:]
        if exchange:
            @pl.when(first(*pre))
            def _():
                exchange.start(xin, xout, sem_refs)

        body(*pre, *ins, *outs, *scr)
        if exchange:
            @pl.when(last(*pre))
            def _():
                exchange.finish(xin, xout, sem_refs)

    any_spec = pl.BlockSpec(memory_space=pl.ANY)
    aliases = {n_pre + n_in + i: n_out + o for i, o in exchange.aliases.items()} if exchange else {}
    operands = [_in_hbm(a) for a in operands]
    results = pl.pallas_call(
        full_body, name=name, out_shape=[_out_hbm(s) for s in list(out_shape) + ex_out],
        grid_spec=pltpu.PrefetchScalarGridSpec(
            num_scalar_prefetch=n_pre, grid=grid, in_specs=list(in_specs) + [any_spec] * len(ex_in),
            out_specs=list(out_specs) + [any_spec] * len(ex_out), scratch_shapes=list(scratch_shapes) + ex_sems),
        input_output_aliases=aliases, compiler_params=params,
    )(*prefetch, *operands, *[_in_hbm(a) for a in ex_in])
    return results[:n_out], results[n_out:]


def _remote(src, dst, sems, idx, to):
    send_sems, recv_sems = sems
    return pltpu.make_async_remote_copy(src_ref=src, dst_ref=dst, send_sem=send_sems.at[idx], recv_sem=recv_sems.at[idx],
                                        device_id=to, device_id_type=MESH_IDS)


def _sem_pairs(*shape):
    return [pltpu.SemaphoreType.DMA(shape), pltpu.SemaphoreType.DMA(shape)]


def _same_shapes(arrays):
    return [jax.ShapeDtypeStruct(a.shape, a.dtype) for a in arrays]


def _gather_over_ici(slots):
    n = len(slots)

    def copies(ins, outs, sems):
        x, y, c = _mesh_pos()
        k = 2 * x + y
        out = []
        for j, (chip, kj) in enumerate(_other_chips(x, y)):
            for i in range(n):
                to = (*chip, c)
                out.append((_remote(ins[i].at[k, c], outs[i].at[k, c], sems, (j, i), to),
                            _remote(ins[i].at[k, c], outs[i].at[kj, c], sems, (j, i), to)))
        return out

    return _from_copies(slots, _same_shapes(slots), {i: i for i in range(n)}, _sem_pairs(3, n), copies)


def _gather_over_d2d(slots):
    n = len(slots)

    def copies(ins, outs, sems):
        x, y, c = _mesh_pos()
        sibling = (x, y, 1 - c)
        out = []
        for j, (_, kj) in enumerate(_other_chips(x, y)):
            for i in range(n):
                out.append((_remote(ins[i].at[kj, c], outs[i].at[kj, c], sems, (j, i), sibling),
                            _remote(ins[i].at[kj, c], outs[i].at[kj, 1 - c], sems, (j, i), sibling)))
        return out

    return _from_copies(slots, _same_shapes(slots), {i: i for i in range(n)}, _sem_pairs(3, n), copies)


def _gather_all(slots8):
    def copies(ins, outs, sems):
        x, y, c = _mesh_pos()
        me = 4 * x + 2 * y + c
        out = []
        for r in range(1, N_DEV):
            px = 1 - x if r & 4 else x
            py = 1 - y if r & 2 else y
            pc = 1 - c if r & 1 else c
            to = (px, py, pc)
            out.append((_remote(ins[0].at[me], outs[0].at[me], sems, r - 1, to),
                        _remote(ins[0].at[me], outs[0].at[4 * px + 2 * py + pc], sems, r - 1, to)))
        return out

    return _from_copies([slots8], _same_shapes([slots8]), {0: 0}, _sem_pairs(N_DEV - 1), copies)


def _merge(first, second):
    n_in, n_out, n_sem = len(first.inputs), len(first.out_shapes), len(first.sems)

    def start(ins, outs, sems):
        first.start(ins[:n_in], outs[:n_out], sems[:n_sem])
        second.start(ins[n_in:], outs[n_out:], sems[n_sem:])

    def finish(ins, outs, sems):
        first.finish(ins[:n_in], outs[:n_out], sems[:n_sem])
        second.finish(ins[n_in:], outs[n_out:], sems[n_sem:])

    aliases = dict(first.aliases)
    aliases.update({n_in + i: n_out + o for i, o in second.aliases.items()})
    return _Exchange(first.inputs + second.inputs, first.out_shapes + second.out_shapes, aliases,
                     first.sems + second.sems, start, finish)


def _pair_exchange(grads):
    n = len(grads)

    def copies(ins, outs, sems):
        x, y, c = _mesh_pos()
        cps = [_remote(ins[i].at[:, 1 - c], outs[i], sems, i, (x, y, 1 - c)) for i in range(n)]
        return [(cp, cp) for cp in cps]

    shapes = [jax.ShapeDtypeStruct((N_CHIPS,) + g.shape[2:], g.dtype) for g in grads]
    return _from_copies(grads, shapes, {}, _sem_pairs(n), copies)


def _chip_exchange(partials):
    n = len(partials)

    def copies(ins, outs, sems):
        x, y, c = _mesh_pos()
        cps = [_remote(ins[i].at[kj], outs[i].at[j], sems, (j, i), (*chip, c))
               for j, (chip, kj) in enumerate(_other_chips(x, y)) for i in range(n)]
        return [(cp, cp) for cp in cps]

    shapes = [jax.ShapeDtypeStruct((3,) + p.shape[1:], p.dtype) for p in partials]
    return _from_copies(partials, shapes, {}, _sem_pairs(3, n), copies)


def _pair_send(halves):
    n = len(halves)

    def copies(ins, outs, sems):
        x, y, c = _mesh_pos()
        cps = [_remote(ins[i], outs[i], sems, i, (x, y, 1 - c)) for i in range(n)]
        return [(cp, cp) for cp in cps]

    return _from_copies(halves, _same_shapes(halves), {}, _sem_pairs(n), copies)


def _reduce_in_vmem(grads, half_rows, name):
    n = len(grads)

    def body(*refs):
        g, mine, theirs = refs[:n], refs[n:2 * n], refs[2 * n:3 * n]
        landed_pair, partial, landed_chips = refs[3 * n:4 * n], refs[4 * n:5 * n], refs[5 * n:6 * n]
        sems = refs[6 * n:]
        x, y, c = _mesh_pos()
        k = 2 * x + y
        sibling = (x, y, 1 - c)

        def half(i, chip_idx, which):
            return pl.ds(pl.multiple_of((2 * chip_idx + which) * half_rows[i], 8), half_rows[i])

        def run(copies):
            for cp in copies:
                cp.start()
            for cp in copies:
                cp.wait_recv()
                cp.wait_send()

        run([_remote(g[i].at[half(i, kk, 1 - c)], landed_pair[i].at[kk], sems[0:2], (kk, i), sibling)
             for kk in range(N_CHIPS) for i in range(n)])
        for i in range(n):
            for kk in range(N_CHIPS):
                partial[i][kk] = (g[i][half(i, kk, c), :] + landed_pair[i][kk]).astype(BF16)
        run([_remote(partial[i].at[kj], landed_chips[i].at[j], sems[2:4], (j, i), (*chip, c))
             for j, (chip, kj) in enumerate(_other_chips(x, y)) for i in range(n)])
        for i in range(n):
            own = g[i][half(i, k, c), :] + landed_pair[i][k]
            mine[i][...] = ((own + landed_chips[i][0].astype(F32)) + landed_chips[i][1].astype(F32)) \
                + landed_chips[i][2].astype(F32)
        run([_remote(mine[i], theirs[i], sems[4:6], i, sibling) for i in range(n)])

    shapes = [(h, gr.shape[1]) for gr, h in zip(grads, half_rows)]
    halves = [jax.ShapeDtypeStruct(s, F32) for s in shapes]
    vmem = pl.BlockSpec(memory_space=pltpu.VMEM)
    scratch = ([pltpu.VMEM((N_CHIPS,) + s, F32) for s in shapes]
               + [pltpu.VMEM((N_CHIPS,) + s, BF16) for s in shapes]
               + [pltpu.VMEM((3,) + s, BF16) for s in shapes]
               + _sem_pairs(N_CHIPS, n) + _sem_pairs(3, n) + _sem_pairs(n))
    out = pl.pallas_call(
        body, name=name, out_shape=halves + halves, in_specs=[vmem] * n, out_specs=[vmem] * (2 * n),
        scratch_shapes=scratch, compiler_params=_params(48),
    )(*grads)
    return out[:n], out[n:]


def _row_tile(rows):
    for t in (256, 128, 64):
        if rows % t == 0:
            return t
    return rows


def _add_pair(core, chip, grad, landed):
    _, h, cols = landed.shape
    tr = _row_tile(h)

    def body(core_ref, chip_ref, g_ref, l_ref, own_ref, ob_ref):
        s = g_ref[...] + l_ref[...]
        ob_ref[...] = s.astype(BF16)

        @pl.when(pl.program_id(1) == chip_ref[0])
        def _():
            own_ref[...] = s

    return _pcall(
        body, name="grad_add_pair",
        out_shape=[jax.ShapeDtypeStruct((h, cols), F32), jax.ShapeDtypeStruct(landed.shape, BF16)],
        grid_spec=pltpu.PrefetchScalarGridSpec(
            num_scalar_prefetch=2, grid=(h // tr, N_CHIPS),
            in_specs=[pl.BlockSpec((None, None, tr, cols), lambda t, k, core_ref, chip_ref: (k, core_ref[0], t, 0)),
                      pl.BlockSpec((None, tr, cols), lambda t, k, core_ref, chip_ref: (k, t, 0))],
            out_specs=[pl.BlockSpec((tr, cols), lambda t, k, core_ref, chip_ref: (t, 0)),
                       pl.BlockSpec((None, tr, cols), lambda t, k, core_ref, chip_ref: (k, t, 0))]),
        compiler_params=_params(32, ("arbitrary", "arbitrary")),
        operands=(core, chip, grad, landed))


def _add_chips(own, landed):
    h, cols = own.shape
    tr = _row_tile(h)

    def body(p_ref, l_ref, o_ref):
        o_ref[...] = ((p_ref[...] + l_ref[0].astype(F32)) + l_ref[1].astype(F32)) + l_ref[2].astype(F32)

    return _pcall(
        body, name="grad_add_chips", grid=(h // tr,),
        out_shape=jax.ShapeDtypeStruct((h, cols), F32),
        in_specs=[pl.BlockSpec((tr, cols), lambda t: (t, 0)), pl.BlockSpec((3, tr, cols), lambda t: (0, t, 0))],
        out_specs=pl.BlockSpec((tr, cols), lambda t: (t, 0)),
        compiler_params=_params(32, ("arbitrary",)),
        operands=(own, landed))


def _ada_project(c_all, w_ada, b_shard):
    n = w_ada.shape[1]
    tn = 512

    def body(c_ref, w_ref, b_ref, mod_ref, cond_ref):
        cv = c_ref[...]
        cond = cv * _sigmoid(cv)
        mod_ref[...] = _dot(cond.astype(BF16), w_ref[...].astype(BF16)) + b_ref[...]
        cond_ref[0:N_DEV, :] = cond
        cond_ref[N_DEV:2 * N_DEV, :] = jnp.zeros_like(cond)

    return _pcall(
        body, name="ada_project", grid=(n // tn,),
        out_shape=[jax.ShapeDtypeStruct((N_DEV, n), F32), jax.ShapeDtypeStruct((2 * N_DEV, D_MODEL), F32)],
        in_specs=[pl.BlockSpec((N_DEV, D_MODEL), lambda j: (0, 0)), pl.BlockSpec((D_MODEL, tn), lambda j: (0, j)),
                  pl.BlockSpec((1, tn), lambda j: (0, j))],
        out_specs=[pl.BlockSpec((N_DEV, tn), lambda j: (0, j)), pl.BlockSpec((2 * N_DEV, D_MODEL), lambda j: (0, 0))],
        compiler_params=_params(32, ("arbitrary",)),
        operands=(c_all, w_ada, b_shard))


def _in_project(x, pos, sc_a, sh_a, w_in, q_norm_w, kv_norm_w, w_q, w_kv, invf, exchange=None):
    t_len = x.shape[0]
    tm = min(512, t_len)

    def body(x_ref, pos_ref, sc_ref, sh_ref, win_ref, qn_ref, kvn_ref, wq_ref, wkv_ref, invf_ref,
             u_ref, zhg_ref, cq_ref, ckv_ref, q_ref, k_ref, kt_ref, v_ref, vt_ref):
        u = (x_ref[...] * (1.0 + sc_ref[...]) + sh_ref[...]).astype(BF16)
        u_ref[...] = u
        z = _dot_nt(u, win_ref[...])
        zhg_ref[...] = z[:, :HG_COLS]
        cq = z[:, HG_COLS:HG_COLS + Q_RANK]
        ckv = z[:, HG_COLS + Q_RANK:HG_COLS + Q_RANK + KV_RANK]
        cq_ref[...] = cq
        ckv_ref[...] = ckv
        cos_t, sin_t = _rope_tables(pos_ref[...], invf_ref[...])
        k_pe = _rope(z[:, HG_COLS + Q_RANK + KV_RANK:], cos_t, sin_t)
        k_pe_t = jnp.transpose(k_pe).astype(BF16)
        cqn = (cq * lax.rsqrt(_rowmean(cq * cq) + RMS_EPS) * qn_ref[...]).astype(BF16)
        ckvn = (ckv * lax.rsqrt(_rowmean(ckv * ckv) + RMS_EPS) * kvn_ref[...]).astype(BF16)
        for h in range(N_HEADS):
            qh = _dot(cqn, wq_ref[h])
            q_ref[h, :, 0:HEAD_DIM] = qh[:, :HEAD_DIM].astype(BF16)
            q_ref[h, :, HEAD_DIM:QK_DIM] = _rope(qh[:, HEAD_DIM:], cos_t, sin_t).astype(BF16)
            kvh = _dot(ckvn, wkv_ref[h])
            k_ref[h, :, 0:HEAD_DIM] = kvh[:, :HEAD_DIM].astype(BF16)
            k_ref[h, :, HEAD_DIM:QK_DIM] = k_pe.astype(BF16)
            kt_ref[h, 0:HEAD_DIM, :] = jnp.transpose(kvh[:, :HEAD_DIM]).astype(BF16)
            kt_ref[h, HEAD_DIM:QK_DIM, :] = k_pe_t
            v_ref[h] = kvh[:, HEAD_DIM:].astype(BF16)
            vt_ref[h] = jnp.transpose(kvh[:, HEAD_DIM:]).astype(BF16)

    row = lambda i: (i, 0)
    fixed2 = lambda i: (0, 0)
    fixed3 = lambda i: (0, 0, 0)
    heads = lambda i: (0, i, 0)
    n_tiles = t_len // tm
    return _pallas(
        body, name="in_project", grid=(n_tiles,),
        operands=(x, pos, sc_a, sh_a, w_in, q_norm_w, kv_norm_w, w_q, w_kv, invf),
        out_shape=[jax.ShapeDtypeStruct((t_len, D_MODEL), BF16), jax.ShapeDtypeStruct((t_len, HG_COLS), F32),
                   jax.ShapeDtypeStruct((t_len, Q_RANK), F32), jax.ShapeDtypeStruct((t_len, KV_RANK), F32),
                   jax.ShapeDtypeStruct((N_HEADS, t_len, QK_DIM), BF16),
                   jax.ShapeDtypeStruct((N_HEADS, t_len, QK_DIM), BF16),
                   jax.ShapeDtypeStruct((N_HEADS, QK_DIM, t_len), BF16),
                   jax.ShapeDtypeStruct((N_HEADS, t_len, HEAD_DIM), BF16),
                   jax.ShapeDtypeStruct((N_HEADS, HEAD_DIM, t_len), BF16)],
        in_specs=[pl.BlockSpec((tm, D_MODEL), row), pl.BlockSpec((tm, 1), row),
                  pl.BlockSpec((1, D_MODEL), fixed2), pl.BlockSpec((1, D_MODEL), fixed2),
                  pl.BlockSpec((IN_COLS_PAD, D_MODEL), fixed2),
                  pl.BlockSpec((1, Q_RANK), fixed2), pl.BlockSpec((1, KV_RANK), fixed2),
                  pl.BlockSpec((N_HEADS, Q_RANK, QK_DIM), fixed3), pl.BlockSpec((N_HEADS, KV_RANK, 2 * HEAD_DIM), fixed3),
                  pl.BlockSpec((1, 128), fixed2)],
        out_specs=[pl.BlockSpec((tm, D_MODEL), row), pl.BlockSpec((tm, HG_COLS), row),
                   pl.BlockSpec((tm, Q_RANK), row), pl.BlockSpec((tm, KV_RANK), row),
                   pl.BlockSpec((N_HEADS, tm, QK_DIM), heads), pl.BlockSpec((N_HEADS, tm, QK_DIM), heads),
                   pl.BlockSpec((N_HEADS, QK_DIM, tm), lambda i: (0, 0, i)),
                   pl.BlockSpec((N_HEADS, tm, HEAD_DIM), heads),
                   pl.BlockSpec((N_HEADS, HEAD_DIM, tm), lambda i: (0, 0, i))],
        params=_params(48, ("arbitrary",)), exchange=exchange,
        first=lambda: pl.program_id(0) == 0, last=lambda: pl.program_id(0) == n_tiles - 1)


def _lower_bound(lb_raw):
    m = jnp.max(lb_raw, axis=0, keepdims=True)
    e = jnp.exp(lb_raw - m)
    return e[0:1] / jnp.sum(e, axis=0, keepdims=True)


def _tri(inclusive_lower):
    r = lax.broadcasted_iota(jnp.int32, (HG_CHUNK, HG_CHUNK), 0)
    c = lax.broadcasted_iota(jnp.int32, (HG_CHUNK, HG_CHUNK), 1)
    return (c <= r) if inclusive_lower else (c >= r)


def _chunk_rows(n):
    return slice(n * HG_CHUNK, (n + 1) * HG_CHUNK)


def _chunk_prefix_sums(v, inclusive_lower):
    tri = _tri(inclusive_lower).astype(BF16)
    hi = v.astype(BF16)
    rest = v - hi.astype(F32)
    mid = rest.astype(BF16)
    lo = (rest - mid.astype(F32)).astype(BF16)
    pieces = jnp.concatenate([hi, mid, lo], axis=1)
    out = []
    for n in range(v.shape[0] // HG_CHUNK):
        s = _dot(tri, pieces[_chunk_rows(n)])
        out.append((s[:, 0:HEAD_DIM] + s[:, HEAD_DIM:2 * HEAD_DIM]) + s[:, 2 * HEAD_DIM:])
    return jnp.concatenate(out, axis=0)


def _per_chunk(v, row):
    n = v.shape[0] // HG_CHUNK
    v3 = v.reshape(n, HG_CHUNK, HEAD_DIM)
    return jnp.broadcast_to(v3[:, row:row + 1, :], v3.shape).reshape(v.shape)


def _hg_block(q, f_logit, lb):
    sg = _sigmoid(f_logit)
    forget = lb + (1.0 - lb) * sg
    kk = 1.0 - forget
    b = _chunk_prefix_sums(jnp.log(forget), True)
    b_ref = _per_chunk(b, HG_CHUNK // 2 - 1)
    b_last = _per_chunk(b, HG_CHUNK - 1)
    e_i = jnp.exp(b - b_ref)
    e_ri = jnp.exp(b_ref - b)
    e_b = jnp.exp(b)
    e_l = jnp.exp(b_last - b)
    return dict(sg=sg, forget=forget, e_i=e_i, e_ri=e_ri, e_b=e_b, e_l=e_l, dec=jnp.exp(b_last),
                qi=q * e_i, ki=kk * e_ri, qe=q * e_b, kl=kk * e_l)


HG_STEP_HEADS = 4


def _head_cols(hh):
    return slice(hh * HEAD_DIM, (hh + 1) * HEAD_DIM)


def _hgrn_forward(zhg, lb_raw, norm_w, exchange=None):
    t_len = zhg.shape[0]
    tb = min(512, t_len)
    n_chunks = tb // HG_CHUNK
    hs = HG_STEP_HEADS

    def body(q_ref, f_ref, v_ref, g_ref, lb_ref, w_ref, opre_ref, o_ref, st_ref, state):
        @pl.when(pl.program_id(1) == 0)
        def _():
            state[...] = jnp.zeros_like(state)

        causal = _tri(True)
        heads = range(hs)
        blk, v, qi, ki, qe, kl = {}, {}, {}, {}, {}, {}
        for hh in heads:
            cols = _head_cols(hh)
            blk[hh] = _hg_block(q_ref[:, cols], f_ref[:, cols], _lower_bound(lb_ref[:, cols]))
            v[hh] = v_ref[:, cols].astype(BF16)
            qi[hh], ki[hh], qe[hh], kl[hh] = (blk[hh][name].astype(BF16) for name in ("qi", "ki", "qe", "kl"))
        st = {hh: state[hh] for hh in heads}
        parts = {hh: [] for hh in heads}
        for n in range(n_chunks):
            r = _chunk_rows(n)
            for hh in heads:
                a = jnp.where(causal, _dot_nt(qi[hh][r], ki[hh][r]), 0.0).astype(BF16)
                st_ref[hh, n] = st[hh]
                parts[hh].append(_dot(a, v[hh][r]) + _dot_nt(qe[hh][r], st[hh].astype(BF16)))
                st[hh] = st[hh] * blk[hh]["dec"][n * HG_CHUNK:n * HG_CHUNK + 1] + _dot_tn(v[hh][r], kl[hh][r])
        for hh in heads:
            cols = _head_cols(hh)
            state[hh] = st[hh]
            o = jnp.concatenate(parts[hh], axis=0)
            opre_ref[:, cols] = o
            g = g_ref[:, cols]
            o_ref[:, cols] = o * lax.rsqrt(_rowmean(o * o) + RMS_EPS) * w_ref[:, cols] * (g * _sigmoid(g))

    groups = N_HEADS // hs
    wide = hs * HEAD_DIM
    col = lambda off: (lambda h, t: (t, off + h))
    nb = t_len // tb
    return _pallas(
        body, name="hgrn_forward", grid=(groups, nb), operands=(zhg, zhg, zhg, zhg, lb_raw, norm_w),
        out_shape=[jax.ShapeDtypeStruct((t_len, N_HEADS * HEAD_DIM), F32),
                   jax.ShapeDtypeStruct((t_len, N_HEADS * HEAD_DIM), F32),
                   jax.ShapeDtypeStruct((N_HEADS, t_len // HG_CHUNK, HEAD_DIM, HEAD_DIM), F32)],
        in_specs=[pl.BlockSpec((tb, wide), col(0)), pl.BlockSpec((tb, wide), col(groups)),
                  pl.BlockSpec((tb, wide), col(2 * groups)), pl.BlockSpec((tb, wide), col(3 * groups)),
                  pl.BlockSpec((2, wide), lambda h, t: (0, h)), pl.BlockSpec((1, wide), lambda h, t: (0, h))],
        out_specs=[pl.BlockSpec((tb, wide), col(0)), pl.BlockSpec((tb, wide), col(0)),
                   pl.BlockSpec((hs, n_chunks, HEAD_DIM, HEAD_DIM), lambda h, t: (h, t, 0, 0))],
        scratch_shapes=[pltpu.VMEM((hs, HEAD_DIM, HEAD_DIM), F32)],
        params=_params(40, ("arbitrary", "arbitrary")), exchange=exchange,
        first=lambda: (pl.program_id(0) == 0) & (pl.program_id(1) == 0),
        last=lambda: (pl.program_id(0) == groups - 1) & (pl.program_id(1) == nb - 1))


def _hgrn_backward(zhg, lb_raw, norm_w, o_pre, d_cat, states, exchange=None):
    t_len = zhg.shape[0]
    tb = min(512, t_len)
    n_chunks = tb // HG_CHUNK
    nb = t_len // tb
    hs = HG_STEP_HEADS

    def body(q_ref, f_ref, v_ref, g_ref, lb_ref, w_ref, opre_ref, do_ref, st_ref,
             dq_ref, df_ref, dv_ref, dg_ref, sums_ref, gstate):
        @pl.when(pl.program_id(1) == 0)
        def _():
            gstate[...] = jnp.zeros_like(gstate)
            sums_ref[...] = jnp.zeros_like(sums_ref)

        heads = range(hs)
        causal = _tri(True)
        row_id = lax.broadcasted_iota(jnp.int32, (HG_CHUNK, HEAD_DIM), 0)
        lb, d_o, blk, v, qi, ki, qe, kl = ({} for _ in range(8))
        for hh in heads:
            cols = _head_cols(hh)
            lb[hh] = _lower_bound(lb_ref[:, cols])
            w = w_ref[:, cols]
            o = opre_ref[:, cols]
            g = g_ref[:, cols]
            d_out = do_ref[:, cols]
            r = lax.rsqrt(_rowmean(o * o) + RMS_EPS)
            sg_g = _sigmoid(g)
            dg_ref[:, cols] = (d_out * (o * r * w) * (sg_g * (1.0 + g * (1.0 - sg_g)))).astype(BF16)
            d_on = d_out * (g * sg_g)
            sums_ref[1:2, cols] += _colsum(d_on * o * r)
            dy = d_on * w
            d_o[hh] = (r * dy - o * (r * r * r) * _rowmean(dy * o)).astype(BF16)
            blk[hh] = _hg_block(q_ref[:, cols], f_ref[:, cols], lb[hh])
            v[hh] = v_ref[:, cols].astype(BF16)
            qi[hh], ki[hh], qe[hh], kl[hh] = (blk[hh][name].astype(BF16) for name in ("qi", "ki", "qe", "kl"))
        gt = {hh: gstate[hh] for hh in heads}
        d_v, d_qi, d_ki, d_qe, d_kl, d_dec = ({hh: [None] * n_chunks for hh in heads} for _ in range(6))
        for n in reversed(range(n_chunks)):
            rows = _chunk_rows(n)
            for hh in heads:
                st = st_ref[hh, n]
                a = jnp.where(causal, _dot_nt(qi[hh][rows], ki[hh][rows]), 0.0).astype(BF16)
                d_a = jnp.where(causal, _dot_nt(d_o[hh][rows], v[hh][rows]), 0.0).astype(BF16)
                gt_b = gt[hh].astype(BF16)
                d_v[hh][n] = _dot_tn(a, d_o[hh][rows]) + _dot_nt(kl[hh][rows], gt_b)
                d_qi[hh][n] = _dot(d_a, ki[hh][rows])
                d_ki[hh][n] = _dot_tn(d_a, qi[hh][rows])
                d_qe[hh][n] = _dot(d_o[hh][rows], st.astype(BF16))
                d_kl[hh][n] = _dot(v[hh][rows], gt_b)
                d_dec[hh][n] = jnp.where(row_id == HG_CHUNK - 1, _colsum(gt[hh] * st), 0.0)
                gt[hh] = gt[hh] * blk[hh]["dec"][n * HG_CHUNK:n * HG_CHUNK + 1] + _dot_tn(d_o[hh][rows], qe[hh][rows])
        for hh in heads:
            cols = _head_cols(hh)
            b = blk[hh]
            gstate[hh] = gt[hh]
            dqi, dki, dqe, dkl, ddec = (jnp.concatenate(p[hh], axis=0) for p in (d_qi, d_ki, d_qe, d_kl, d_dec))
            dv_ref[:, cols] = jnp.concatenate(d_v[hh], axis=0).astype(BF16)
            dq_ref[:, cols] = (dqi * b["e_i"] + dqe * b["e_b"]).astype(BF16)
            d_k = dki * b["e_ri"] + dkl * b["e_l"]
            t_qi = dqi * b["qi"]
            t_ki = dki * b["ki"]
            t_kl = dkl * b["kl"]
            at_ref, at_last = [], []
            for n in range(n_chunks):
                rows = _chunk_rows(n)
                at_ref.append(jnp.where(row_id == HG_CHUNK // 2 - 1, _colsum(t_ki[rows] - t_qi[rows]), 0.0))
                at_last.append(jnp.where(row_id == HG_CHUNK - 1, _colsum(t_kl[rows]), 0.0))
            d_b = (t_qi - t_ki + dqe * b["qe"] - t_kl + jnp.concatenate(at_ref, axis=0)
                   + jnp.concatenate(at_last, axis=0) + ddec * b["dec"])
            d_forget = _chunk_prefix_sums(d_b, False) / b["forget"] - d_k
            sg = b["sg"]
            df_ref[:, cols] = (d_forget * (1.0 - lb[hh]) * sg * (1.0 - sg)).astype(BF16)
            sums_ref[0:1, cols] += _colsum(d_forget * (1.0 - sg))

    groups = N_HEADS // hs
    wide = hs * HEAD_DIM
    col = lambda off: (lambda h, t: (nb - 1 - t, off + h))
    return _pallas(
        body, name="hgrn_backward", grid=(groups, nb),
        operands=(zhg, zhg, zhg, zhg, lb_raw, norm_w, o_pre, d_cat, states),
        out_shape=[jax.ShapeDtypeStruct((t_len, N_HEADS * HEAD_DIM), BF16)] * 4
        + [jax.ShapeDtypeStruct((8, N_HEADS * HEAD_DIM), F32)],
        in_specs=[pl.BlockSpec((tb, wide), col(0)), pl.BlockSpec((tb, wide), col(groups)),
                  pl.BlockSpec((tb, wide), col(2 * groups)), pl.BlockSpec((tb, wide), col(3 * groups)),
                  pl.BlockSpec((2, wide), lambda h, t: (0, h)), pl.BlockSpec((1, wide), lambda h, t: (0, h)),
                  pl.BlockSpec((tb, wide), col(0)), pl.BlockSpec((tb, wide), col(0)),
                  pl.BlockSpec((hs, n_chunks, HEAD_DIM, HEAD_DIM), lambda h, t: (h, nb - 1 - t, 0, 0))],
        out_specs=[pl.BlockSpec((tb, wide), col(0))] * 4 + [pl.BlockSpec((8, wide), lambda h, t: (0, h))],
        scratch_shapes=[pltpu.VMEM((hs, HEAD_DIM, HEAD_DIM), F32)],
        params=_params(40, ("arbitrary", "arbitrary")), exchange=exchange,
        first=lambda: (pl.program_id(0) == 0) & (pl.program_id(1) == 0),
        last=lambda: (pl.program_id(0) == groups - 1) & (pl.program_id(1) == nb - 1))


ATT_LOG2 = ATT_SCALE * 1.4426950408889634


def _triangle_steps(nq, q_major):
    if q_major:
        pairs = [(i, j) for i in range(nq) for j in range(i + 1)]
    else:
        pairs = [(i, j) for j in range(nq) for i in range(j, nq)]
    return jnp.array([p[0] for p in pairs], jnp.int32), jnp.array([p[1] for p in pairs], jnp.int32)


def _key_le_query(t):
    return lax.broadcasted_iota(jnp.int32, (t, t), 0) <= lax.broadcasted_iota(jnp.int32, (t, t), 1)


def _attention_forward(q, k, v_t, exchange=None):
    t_len = q.shape[1]
    tq = min(512, t_len)
    nq = t_len // tq
    qi_tab, ki_tab = _triangle_steps(nq, True)

    def body(qi_ref, ki_ref, q_ref, k_ref, vt_ref, o_ref, lse_ref, m_s, l_s, acc_s):
        step = pl.program_id(0)
        qi, ki = qi_ref[step], ki_ref[step]

        @pl.when(ki == 0)
        def _():
            m_s[...] = jnp.full_like(m_s, NEG_BIG)
            l_s[...] = jnp.zeros_like(l_s)
            acc_s[...] = jnp.zeros_like(acc_s)

        def accumulate(masked):
            for h in range(N_HEADS):
                s_t = _dot_nt(k_ref[h], q_ref[h]) * ATT_LOG2
                if masked:
                    s_t = jnp.where(_key_le_query(tq), s_t, NEG_BIG)
                m_old = m_s[h]
                m_new = jnp.maximum(m_old, jnp.max(s_t, axis=0, keepdims=True))
                alpha = jnp.exp2(m_old - m_new)
                p_t = jnp.exp2(s_t - m_new)
                l_s[h] = alpha * l_s[h] + jnp.sum(p_t, axis=0, keepdims=True)
                acc_s[h] = alpha * acc_s[h] + _dot(vt_ref[h], p_t.astype(BF16))
                m_s[h] = m_new

        @pl.when(ki < qi)
        def _():
            accumulate(False)

        @pl.when(ki == qi)
        def _():
            accumulate(True)
            for h in range(N_HEADS):
                o_ref[:, h * HEAD_DIM:(h + 1) * HEAD_DIM] = jnp.transpose(acc_s[h] / l_s[h])
                lse_ref[h] = m_s[h] + jnp.log2(l_s[h])

    n_steps = qi_tab.shape[0]
    return _pallas(
        body, name="attention_forward", grid=(n_steps,), prefetch=(qi_tab, ki_tab), operands=(q, k, v_t),
        out_shape=[jax.ShapeDtypeStruct((t_len, N_HEADS * HEAD_DIM), F32),
                   jax.ShapeDtypeStruct((N_HEADS, 1, t_len), F32)],
        in_specs=[pl.BlockSpec((N_HEADS, tq, QK_DIM), lambda s, qt, kt: (0, qt[s], 0)),
                  pl.BlockSpec((N_HEADS, tq, QK_DIM), lambda s, qt, kt: (0, kt[s], 0)),
                  pl.BlockSpec((N_HEADS, HEAD_DIM, tq), lambda s, qt, kt: (0, 0, kt[s]))],
        out_specs=[pl.BlockSpec((tq, N_HEADS * HEAD_DIM), lambda s, qt, kt: (qt[s], 0)),
                   pl.BlockSpec((N_HEADS, 1, tq), lambda s, qt, kt: (0, 0, qt[s]))],
        scratch_shapes=[pltpu.VMEM((N_HEADS, 1, tq), F32), pltpu.VMEM((N_HEADS, 1, tq), F32),
                        pltpu.VMEM((N_HEADS, HEAD_DIM, tq), F32)],
        params=_params(48, ("arbitrary",)), exchange=exchange,
        first=lambda qt, kt: pl.program_id(0) == 0, last=lambda qt, kt: pl.program_id(0) == n_steps - 1)


BWD_HEADS = 4


def _attention_backward(q, k, k_t, v, d_cat, lse, delta, exchange=None):
    t_len = q.shape[1]
    tq = min(512, t_len)
    nq = t_len // tq
    hp = BWD_HEADS
    qi_tab, ki_tab = _triangle_steps(nq, False)

    def body(qi_ref, ki_ref, q_ref, k_ref, kt_ref, v_ref, do_ref, lse_ref, delta_ref, dqt_hbm, dk_ref, dv_ref,
             dqt_s, dk_s, dv_s):
        group, step = pl.program_id(0), pl.program_id(1)
        qi, ki = qi_ref[step], ki_ref[step]

        @pl.when(step == 0)
        def _():
            dqt_s[...] = jnp.zeros_like(dqt_s)

        @pl.when(qi == ki)
        def _():
            dk_s[...] = jnp.zeros_like(dk_s)
            dv_s[...] = jnp.zeros_like(dv_s)

        def accumulate(masked):
            for h in range(hp):
                do_b = do_ref[:, h * HEAD_DIM:(h + 1) * HEAD_DIM].astype(BF16)
                s_t = _dot_nt(k_ref[h], q_ref[h]) * ATT_LOG2
                if masked:
                    s_t = jnp.where(_key_le_query(tq), s_t, NEG_BIG)
                p_t = jnp.exp2(s_t - lse_ref[h])
                dp_t = _dot_nt(v_ref[h], do_b)
                ds_t = (p_t * (dp_t - delta_ref[h]) * ATT_SCALE).astype(BF16)
                dv_s[h] += _dot(p_t.astype(BF16), do_b)
                dk_s[h] += _dot(ds_t, q_ref[h])
                dqt_s[h, qi] += _dot(kt_ref[h], ds_t)

        @pl.when(ki < qi)
        def _():
            accumulate(False)

        @pl.when(ki == qi)
        def _():
            accumulate(True)
            for h in range(hp):
                pltpu.sync_copy(dqt_s.at[h, qi], dqt_hbm.at[group * hp + h, qi])

        @pl.when(qi == nq - 1)
        def _():
            dk_ref[...] = dk_s[...]
            dv_ref[...] = dv_s[...]

    wide = hp * HEAD_DIM
    n_groups, n_steps = N_HEADS // hp, qi_tab.shape[0]
    return _pallas(
        body, name="attention_backward", grid=(n_groups, n_steps), prefetch=(qi_tab, ki_tab),
        operands=(q, k, k_t, v, d_cat, lse, delta),
        out_shape=[jax.ShapeDtypeStruct((N_HEADS, nq, QK_DIM, tq), F32),
                   jax.ShapeDtypeStruct((N_HEADS, t_len, QK_DIM), F32),
                   jax.ShapeDtypeStruct((N_HEADS, t_len, HEAD_DIM), F32)],
        in_specs=[pl.BlockSpec((hp, tq, QK_DIM), lambda g, s, qt, kt: (g, qt[s], 0)),
                  pl.BlockSpec((hp, tq, QK_DIM), lambda g, s, qt, kt: (g, kt[s], 0)),
                  pl.BlockSpec((hp, QK_DIM, tq), lambda g, s, qt, kt: (g, 0, kt[s])),
                  pl.BlockSpec((hp, tq, HEAD_DIM), lambda g, s, qt, kt: (g, kt[s], 0)),
                  pl.BlockSpec((tq, wide), lambda g, s, qt, kt: (qt[s], n_groups + g)),
                  pl.BlockSpec((hp, 1, tq), lambda g, s, qt, kt: (g, 0, qt[s])),
                  pl.BlockSpec((hp, 1, tq), lambda g, s, qt, kt: (g, 0, qt[s]))],
        out_specs=[pl.BlockSpec(memory_space=pl.ANY),
                   pl.BlockSpec((hp, tq, QK_DIM), lambda g, s, qt, kt: (g, kt[s], 0)),
                   pl.BlockSpec((hp, tq, HEAD_DIM), lambda g, s, qt, kt: (g, kt[s], 0))],
        scratch_shapes=[pltpu.VMEM((hp, nq, QK_DIM, tq), F32), pltpu.VMEM((hp, tq, QK_DIM), F32),
                        pltpu.VMEM((hp, tq, HEAD_DIM), F32)],
        params=_params(58, ("arbitrary", "arbitrary")), exchange=exchange,
        first=lambda qt, kt: (pl.program_id(0) == 0) & (pl.program_id(1) == 0),
        last=lambda qt, kt: (pl.program_id(0) == n_groups - 1) & (pl.program_id(1) == n_steps - 1))


def _out_project(o_hg, o_mla, x, g_a, w_out, exchange=None):
    t_len = x.shape[0]
    tm = min(512, t_len)
    half = N_HEADS * HEAD_DIM

    def body(ohg_ref, omla_ref, x_ref, ga_ref, w_ref, cat_ref, mix_ref, xhat_ref, rstd_ref):
        a = ohg_ref[...].astype(BF16)
        b = omla_ref[...].astype(BF16)
        cat_ref[:, 0:half] = a
        cat_ref[:, half:2 * half] = b
        mix = _dot(a, w_ref[0:half, :]) + _dot(b, w_ref[half:2 * half, :])
        mix_ref[...] = mix
        r1 = DN_ALPHA * x_ref[...] + (1.0 + ga_ref[...]) * mix
        xc = r1 - _rowmean(r1)
        rstd = lax.rsqrt(_rowmean(xc * xc) + LN_EPS)
        xhat_ref[...] = xc * rstd
        rstd_ref[...] = rstd

    row = lambda i: (i, 0)
    fixed = lambda i: (0, 0)
    n_tiles = t_len // tm
    return _pallas(
        body, name="out_project", grid=(n_tiles,), operands=(o_hg, o_mla, x, g_a, w_out),
        out_shape=[jax.ShapeDtypeStruct((t_len, D_MODEL), BF16), jax.ShapeDtypeStruct((t_len, D_MODEL), F32),
                   jax.ShapeDtypeStruct((t_len, D_MODEL), F32), jax.ShapeDtypeStruct((t_len, 1), F32)],
        in_specs=[pl.BlockSpec((tm, half), row), pl.BlockSpec((tm, half), row), pl.BlockSpec((tm, D_MODEL), row),
                  pl.BlockSpec((1, D_MODEL), fixed), pl.BlockSpec((D_MODEL, D_MODEL), fixed)],
        out_specs=[pl.BlockSpec((tm, D_MODEL), row), pl.BlockSpec((tm, D_MODEL), row),
                   pl.BlockSpec((tm, D_MODEL), row), pl.BlockSpec((tm, 1), row)],
        params=_params(48, ("arbitrary",)), exchange=exchange,
        first=lambda: pl.program_id(0) == 0, last=lambda: pl.program_id(0) == n_tiles - 1)


V_LN1G, V_LN1B, V_SCM, V_SHM, V_GM, V_GA, V_LN2G, V_LN2B = range(8)
S_DLN2G, S_DLN2B, S_DGM, S_DSCM, S_DSHM, S_DLN1G, S_DLN1B, S_DGA, S_LOSS = range(9)


def _mlp_and_back(xhat1, rstd1, mix, target, o_mla, vecs, w1_top, w1_bottom, w2, w_out):
    t_len = xhat1.shape[0]
    tm = min(256, t_len)
    n_ff = w1_top.shape[0]
    ff = w1_top.shape[2]
    top_rows = w1_top.shape[1]

    def body(xhat_ref, rstd_ref, mix_ref, tgt_ref, omla_ref, vec_ref, w1_top_hbm, w1_bottom_hbm, w2_hbm, wout_hbm,
             act_ref, dhp_ref, um_ref, dh_ref, dmix_ref, dcat_ref, dr1_ref, sums_ref, delta_ref,
             w1_s, w2_s, wout_s, hp_s, load_sems):
        @pl.when(pl.program_id(0) == 0)
        def _():
            loads = [pltpu.make_async_copy(w1_top_hbm, w1_s.at[:, 0:top_rows], load_sems.at[0]),
                     pltpu.make_async_copy(w1_bottom_hbm, w1_s.at[:, top_rows:D_MODEL], load_sems.at[3]),
                     pltpu.make_async_copy(w2_hbm, w2_s, load_sems.at[1]),
                     pltpu.make_async_copy(wout_hbm, wout_s, load_sems.at[2])]
            for cp in loads:
                cp.start()
            sums_ref[...] = jnp.zeros_like(sums_ref)
            for cp in loads:
                cp.wait()

        vec = lambda r: vec_ref[r:r + 1, :]
        xhat = xhat_ref[...]
        x1 = xhat * vec(V_LN1G) + vec(V_LN1B)
        um = (x1 * (1.0 + vec(V_SCM)) + vec(V_SHM)).astype(BF16)
        um_ref[...] = um
        h = jnp.zeros((tm, D_MODEL), F32)
        for j in range(n_ff):
            hp = _dot(um, w1_s[j])
            hp_s[j] = hp
            act = jnp.square(jnp.maximum(hp, 0.0)).astype(BF16)
            act_ref[:, j * ff:(j + 1) * ff] = act
            h = h + _dot(act, w2_s[j])
        r2 = DN_ALPHA * x1 + (1.0 + vec(V_GM)) * h
        xc = r2 - _rowmean(r2)
        rstd2 = lax.rsqrt(_rowmean(xc * xc) + LN_EPS)
        xhat2 = xc * rstd2
        err = xhat2 * vec(V_LN2G) + vec(V_LN2B) - tgt_ref[...]
        loss = 0.5 * jnp.sum(_rowmean(err * err))
        dy = err * (1.0 / D_MODEL)
        dxh = dy * vec(V_LN2G)
        dr2 = rstd2 * (dxh - _rowmean(dxh) - xhat2 * _rowmean(dxh * xhat2))
        dh = ((1.0 + vec(V_GM)) * dr2).astype(BF16)
        dh_ref[...] = dh
        sums_ref[S_DLN2G:S_DLN2G + 1, :] += _colsum(dy * xhat2)
        sums_ref[S_DLN2B:S_DLN2B + 1, :] += _colsum(dy)
        sums_ref[S_DGM:S_DGM + 1, :] += _colsum(dr2 * h)
        sums_ref[S_LOSS:S_LOSS + 1, :] += jnp.full((1, D_MODEL), loss, F32)
        du = jnp.zeros((tm, D_MODEL), F32)
        for j in range(n_ff):
            dhp = (_dot_nt(dh, w2_s[j]) * (2.0 * jnp.maximum(hp_s[j], 0.0))).astype(BF16)
            dhp_ref[:, j * ff:(j + 1) * ff] = dhp
            du = du + _dot_nt(dhp, w1_s[j])
        sums_ref[S_DSCM:S_DSCM + 1, :] += _colsum(du * x1)
        sums_ref[S_DSHM:S_DSHM + 1, :] += _colsum(du)
        dx1 = DN_ALPHA * dr2 + du * (1.0 + vec(V_SCM))
        sums_ref[S_DLN1G:S_DLN1G + 1, :] += _colsum(dx1 * xhat)
        sums_ref[S_DLN1B:S_DLN1B + 1, :] += _colsum(dx1)
        dxh1 = dx1 * vec(V_LN1G)
        dr1 = rstd_ref[...] * (dxh1 - _rowmean(dxh1) - xhat * _rowmean(dxh1 * xhat))
        dr1_ref[...] = dr1
        sums_ref[S_DGA:S_DGA + 1, :] += _colsum(dr1 * mix_ref[...])
        dmix = ((1.0 + vec(V_GA)) * dr1).astype(BF16)
        dmix_ref[...] = dmix
        dcat = _dot_nt(dmix, wout_s[...])
        dcat_ref[...] = dcat
        ones = jnp.ones((8, HEAD_DIM), F32)
        half = N_HEADS * HEAD_DIM
        for hd in range(N_HEADS):
            prod = dcat[:, half + hd * HEAD_DIM:half + (hd + 1) * HEAD_DIM] * omla_ref[:, hd * HEAD_DIM:(hd + 1) * HEAD_DIM]
            delta_ref[hd] = lax.dot_general(ones, prod, (((1,), (1,)), ((), ())), preferred_element_type=F32,
                                            precision=lax.Precision.HIGHEST)[0:1]

    row = lambda i: (i, 0)
    fixed = lambda i: (0, 0)
    any_spec = pl.BlockSpec(memory_space=pl.ANY)
    return _pcall(
        body, name="mlp_and_back", grid=(t_len // tm,),
        out_shape=[jax.ShapeDtypeStruct((t_len, D_FF), BF16), jax.ShapeDtypeStruct((t_len, D_FF), BF16),
                   jax.ShapeDtypeStruct((t_len, D_MODEL), BF16), jax.ShapeDtypeStruct((t_len, D_MODEL), BF16),
                   jax.ShapeDtypeStruct((t_len, D_MODEL), BF16), jax.ShapeDtypeStruct((t_len, D_MODEL), F32),
                   jax.ShapeDtypeStruct((t_len, D_MODEL), F32), jax.ShapeDtypeStruct((16, D_MODEL), F32),
                   jax.ShapeDtypeStruct((N_HEADS, 1, t_len), F32)],
        in_specs=[pl.BlockSpec((tm, D_MODEL), row), pl.BlockSpec((tm, 1), row), pl.BlockSpec((tm, D_MODEL), row),
                  pl.BlockSpec((tm, D_MODEL), row), pl.BlockSpec((tm, N_HEADS * HEAD_DIM), row),
                  pl.BlockSpec((8, D_MODEL), fixed), any_spec, any_spec, any_spec, any_spec],
        out_specs=[pl.BlockSpec((tm, D_FF), row), pl.BlockSpec((tm, D_FF), row), pl.BlockSpec((tm, D_MODEL), row),
                   pl.BlockSpec((tm, D_MODEL), row), pl.BlockSpec((tm, D_MODEL), row), pl.BlockSpec((tm, D_MODEL), row),
                   pl.BlockSpec((tm, D_MODEL), row), pl.BlockSpec((16, D_MODEL), fixed),
                   pl.BlockSpec((N_HEADS, 1, tm), lambda i: (0, 0, i))],
        scratch_shapes=[pltpu.VMEM((n_ff, D_MODEL, ff), BF16), pltpu.VMEM(w2.shape, BF16), pltpu.VMEM(w_out.shape, BF16),
                        pltpu.VMEM((n_ff, tm, ff), F32), pltpu.SemaphoreType.DMA((4,))],
        compiler_params=_params(56, ("arbitrary",)),
        operands=(xhat1, rstd1, mix, target, o_mla, vecs, w1_top, w1_bottom, w2, w_out))


def _in_project_backward(dq, dk, dv, cq, ckv, pos, invf, q_norm_w, kv_norm_w, w_q, w_kv,
                         d_hq, d_hf, d_hi, d_hg, w_in, dr1, x, sc_a, exchange=None):
    t_len = x.shape[0]
    tm = min(512, t_len)
    per_q = dq.shape[3] // tm
    hgw = N_HEADS * HEAD_DIM

    def body(dq_ref, dk_ref, dv_ref, cq_ref, ckv_ref, pos_ref, invf_ref, qn_ref, kvn_ref, wq_ref, wkv_ref,
             dhq_ref, dhf_ref, dhi_ref, dhg_ref, win_ref, dr1_ref, x_ref, sc_ref,
             dz_ref, gx_ref, sums_ref, dwq_ref, dwkv_ref):
        @pl.when(pl.program_id(0) == 0)
        def _():
            sums_ref[...] = jnp.zeros_like(sums_ref)
            dwq_ref[...] = jnp.zeros_like(dwq_ref)
            dwkv_ref[...] = jnp.zeros_like(dwkv_ref)

        cos_t, sin_t = _rope_tables(pos_ref[...], invf_ref[...])
        cq = cq_ref[...]
        ckv = ckv_ref[...]
        rq = lax.rsqrt(_rowmean(cq * cq) + RMS_EPS)
        rkv = lax.rsqrt(_rowmean(ckv * ckv) + RMS_EPS)
        cqn = (cq * rq * qn_ref[...]).astype(BF16)
        ckvn = (ckv * rkv * kvn_ref[...]).astype(BF16)
        d_cqn = jnp.zeros((tm, Q_RANK), F32)
        d_ckvn = jnp.zeros((tm, KV_RANK), F32)
        d_kpe = jnp.zeros((tm, 128), F32)
        for h in range(N_HEADS):
            dqh = jnp.transpose(dq_ref[h])
            dq_full = jnp.concatenate(
                [dqh[:, :HEAD_DIM].astype(BF16), _unrope(dqh[:, HEAD_DIM:], cos_t, sin_t).astype(BF16)], axis=1)
            d_cqn = d_cqn + _dot_nt(dq_full, wq_ref[h])
            dwq_ref[h] += _dot_tn(cqn, dq_full)
            dkh = dk_ref[h]
            d_kpe = d_kpe + dkh[:, HEAD_DIM:]
            dkv_up = jnp.concatenate([dkh[:, :HEAD_DIM].astype(BF16), dv_ref[h].astype(BF16)], axis=1)
            d_ckvn = d_ckvn + _dot_nt(dkv_up, wkv_ref[h])
            dwkv_ref[h] += _dot_tn(ckvn, dkv_up)
        dyq = d_cqn * qn_ref[...]
        dykv = d_ckvn * kvn_ref[...]
        sums_ref[2:3, 0:Q_RANK] += _colsum(d_cqn * cq * rq)
        sums_ref[3:4, 0:KV_RANK] += _colsum(d_ckvn * ckv * rkv)
        dz_ref[:, 0:hgw] = dhq_ref[...]
        dz_ref[:, hgw:2 * hgw] = dhf_ref[...]
        dz_ref[:, 2 * hgw:3 * hgw] = dhi_ref[...]
        dz_ref[:, 3 * hgw:4 * hgw] = dhg_ref[...]
        dz_ref[:, HG_COLS:HG_COLS + Q_RANK] = (rq * dyq - cq * (rq * rq * rq) * _rowmean(dyq * cq)).astype(BF16)
        dz_ref[:, HG_COLS + Q_RANK:HG_COLS + Q_RANK + KV_RANK] = (
            rkv * dykv - ckv * (rkv * rkv * rkv) * _rowmean(dykv * ckv)).astype(BF16)
        dz_ref[:, HG_COLS + Q_RANK + KV_RANK:] = _unrope(d_kpe, cos_t, sin_t).astype(BF16)
        du = _dot(dz_ref[...], win_ref[...])
        xv = x_ref[...]
        gx_ref[...] = DN_ALPHA * dr1_ref[...] + (1.0 + sc_ref[...]) * du
        sums_ref[0:1, :] += _colsum(du * xv)
        sums_ref[1:2, :] += _colsum(du)

    row = lambda i: (i, 0)
    fixed2 = lambda i: (0, 0)
    fixed3 = lambda i: (0, 0, 0)
    heads = lambda i: (0, i, 0)
    n_tiles = t_len // tm
    return _pallas(
        body, name="in_project_backward", grid=(n_tiles,),
        operands=(dq, dk, dv, cq, ckv, pos, invf, q_norm_w, kv_norm_w, w_q, w_kv, d_hq, d_hf, d_hi, d_hg, w_in, dr1, x,
                  sc_a),
        out_shape=[jax.ShapeDtypeStruct((t_len, IN_COLS_PAD), BF16), jax.ShapeDtypeStruct((t_len, D_MODEL), F32),
                   jax.ShapeDtypeStruct((8, D_MODEL), F32), jax.ShapeDtypeStruct((N_HEADS, Q_RANK, QK_DIM), F32),
                   jax.ShapeDtypeStruct((N_HEADS, KV_RANK, 2 * HEAD_DIM), F32)],
        in_specs=[pl.BlockSpec((N_HEADS, None, QK_DIM, tm), lambda i: (0, i // per_q, 0, i % per_q)),
                  pl.BlockSpec((N_HEADS, tm, QK_DIM), heads),
                  pl.BlockSpec((N_HEADS, tm, HEAD_DIM), heads), pl.BlockSpec((tm, Q_RANK), row),
                  pl.BlockSpec((tm, KV_RANK), row), pl.BlockSpec((tm, 1), row), pl.BlockSpec((1, 128), fixed2),
                  pl.BlockSpec((1, Q_RANK), fixed2), pl.BlockSpec((1, KV_RANK), fixed2),
                  pl.BlockSpec((N_HEADS, Q_RANK, QK_DIM), fixed3), pl.BlockSpec((N_HEADS, KV_RANK, 2 * HEAD_DIM), fixed3),
                  pl.BlockSpec((tm, hgw), row), pl.BlockSpec((tm, hgw), row), pl.BlockSpec((tm, hgw), row),
                  pl.BlockSpec((tm, hgw), row), pl.BlockSpec((IN_COLS_PAD, D_MODEL), fixed2),
                  pl.BlockSpec((tm, D_MODEL), row), pl.BlockSpec((tm, D_MODEL), row), pl.BlockSpec((1, D_MODEL), fixed2)],
        out_specs=[pl.BlockSpec((tm, IN_COLS_PAD), row), pl.BlockSpec((tm, D_MODEL), row),
                   pl.BlockSpec((8, D_MODEL), fixed2), pl.BlockSpec((N_HEADS, Q_RANK, QK_DIM), fixed3),
                   pl.BlockSpec((N_HEADS, KV_RANK, 2 * HEAD_DIM), fixed3)],
        params=_params(48, ("arbitrary",)), exchange=exchange,
        first=lambda: pl.program_id(0) == 0, last=lambda: pl.program_id(0) == n_tiles - 1)


def _weight_grad(a, b, name, n_blocks, bn, a_blocked=False, b_blocked=True, exchange=None, token_tile=512):
    t_len = a.shape[0]
    m = a.shape[1] // n_blocks if a_blocked else a.shape[1]
    bt = min(token_tile, t_len)

    def body(a_ref, b_ref, o_ref):
        @pl.when(pl.program_id(1) == 0)
        def _():
            o_ref[...] = jnp.zeros_like(o_ref)

        o_ref[...] += _dot_tn(a_ref[...].astype(BF16), b_ref[...].astype(BF16))

    a_spec = pl.BlockSpec((bt, m), (lambda n, t: (t, n)) if a_blocked else (lambda n, t: (t, 0)))
    b_spec = pl.BlockSpec((bt, bn), (lambda n, t: (t, n)) if b_blocked else (lambda n, t: (t, 0)))
    nt = t_len // bt
    (out,), landed = _pallas(
        body, name=name, grid=(n_blocks, nt), operands=(a, b),
        out_shape=[jax.ShapeDtypeStruct((n_blocks, m, bn), F32)],
        in_specs=[a_spec, b_spec],
        out_specs=[pl.BlockSpec((None, m, bn), lambda n, t: (n, 0, 0))],
        params=_params(56, ("arbitrary", "arbitrary")), exchange=exchange,
        first=lambda: (pl.program_id(0) == 0) & (pl.program_id(1) == 0),
        last=lambda: (pl.program_id(0) == n_blocks - 1) & (pl.program_id(1) == nt - 1))
    return (out, landed) if exchange else out


SMALL_PLACE = {"ln1_g": (6, 0), "ln1_b": (7, 0), "ln2_g": (8, 0), "ln2_b": (9, 0), "hg_norm_w": (10, 512),
               "mla_q_norm_w": (11, 0), "mla_kv_norm_w": (11, Q_RANK)}
SMALL_LB_ROW, SMALL_LOSS_ROW = 10, 12


def _small_params_step(gathered, params):
    names = list(params)

    def body(g_ref, *refs):
        ins, outs = refs[:3 * len(names)], refs[3 * len(names):]
        loss_ref, outs = outs[0], outs[1:]
        tot = g_ref[0]
        for d in range(1, N_DEV):
            tot = tot + g_ref[d]
        loss_ref[...] = tot[SMALL_LOSS_ROW:SMALL_LOSS_ROW + 1, 0:128]

        def update(i, grad, rows=slice(None), lanes=slice(None)):
            w_ref, m_ref, v_ref = ins[3 * i:3 * i + 3]
            g_out, d_out, nm_out, nv_out = outs[4 * i:4 * i + 4]
            g_out[rows, lanes] = grad
            d_out[rows, lanes], nm_out[rows, lanes], nv_out[rows, lanes] = _adamw_update(
                w_ref[rows, lanes], grad, m_ref[rows, lanes], v_ref[rows, lanes])

        for i, name in enumerate(names):
            if name == "b_ada":
                for r in range(6):
                    update(i, tot[r:r + 1, :], lanes=slice(r * D_MODEL, (r + 1) * D_MODEL))
            elif name == "hg_lower_bounds":
                lb = _lower_bound(ins[3 * i][...])
                d0 = tot[SMALL_LB_ROW:SMALL_LB_ROW + 1, 0:512] * lb * (1.0 - lb)
                update(i, d0, rows=slice(0, 1))
                update(i, -d0, rows=slice(1, 2))
            else:
                row, lane = SMALL_PLACE[name]
                update(i, tot[row:row + 1, lane:lane + params[name][0].shape[1]])

    flat_in = [a for name in names for a in params[name]]
    shapes = [jax.ShapeDtypeStruct((1, 128), F32)] + [jax.ShapeDtypeStruct(params[name][0].shape, F32)
                                                      for name in names for _ in range(4)]
    out = pl.pallas_call(body, name="small_params_step", out_shape=shapes)(gathered, *flat_in)
    return out[0], {name: out[1 + 4 * i:5 + 4 * i] for i, name in enumerate(names)}


def _adamw_update(w, gv, m, v):
    nm = ADAM_B1 * m + (1.0 - ADAM_B1) * gv
    nv = ADAM_B2 * v + (1.0 - ADAM_B2) * jnp.square(gv)
    m_hat = nm / (1.0 - ADAM_B1 ** ADAM_STEP)
    v_hat = nv / (1.0 - ADAM_B2 ** ADAM_STEP)
    return -ADAM_LR * (m_hat / (jnp.sqrt(v_hat) + ADAM_EPS) + ADAM_WD * w), nm, nv


def _adamw_halves(core, w, mine, theirs, m, v, name):
    rows, cols = w.shape
    h = rows // 2
    tr = _row_tile(h)
    per_half = h // tr

    def body(core_ref, w_ref, mine_ref, theirs_ref, m_ref, v_ref, g_ref, d_ref, nm_ref, nv_ref):
        is_mine = pl.program_id(0) // per_half == core_ref[0]
        gv = jnp.where(is_mine, mine_ref[...], theirs_ref[...])
        g_ref[...] = gv
        d_ref[...], nm_ref[...], nv_ref[...] = _adamw_update(w_ref[...], gv, m_ref[...], v_ref[...])

    full = pl.BlockSpec((tr, cols), lambda i, core_ref: (i, 0))
    part = pl.BlockSpec((tr, cols), lambda i, core_ref: (i % per_half, 0))
    return _pcall(
        body, name=name, out_shape=[jax.ShapeDtypeStruct(w.shape, F32)] * 4,
        grid_spec=pltpu.PrefetchScalarGridSpec(
            num_scalar_prefetch=1, grid=(rows // tr,), in_specs=[full, part, part, full, full], out_specs=[full] * 4),
        compiler_params=_params(40, ("arbitrary",)),
        operands=(core, w, mine, theirs, m, v))


def _adamw(w, g, m, v, name):
    rows, cols = w.shape
    tr = _row_tile(rows) if rows >= 8 else rows

    def body(w_ref, g_ref, m_ref, v_ref, d_ref, nm_ref, nv_ref):
        d_ref[...], nm_ref[...], nv_ref[...] = _adamw_update(w_ref[...], g_ref[...], m_ref[...], v_ref[...])

    spec = pl.BlockSpec((tr, cols), lambda i: (i, 0))
    return _pcall(
        body, name=name, grid=(rows // tr,),
        out_shape=[jax.ShapeDtypeStruct(w.shape, F32)] * 3,
        in_specs=[spec] * 4, out_specs=[spec] * 3,
        compiler_params=_params(40, ("arbitrary",)),
        operands=(w, g, m, v))


def kernel(x, c, positions, w_ada, b_ada, w_in, hg_lower_bounds, hg_norm_w, mla_q_norm_w, w_q_up, mla_kv_norm_w, w_kv_up, w_out, ln1_g, ln1_b, w_mlp_in, w_mlp_out, ln2_g, ln2_b, loss_target, m_w_ada, m_b_ada, m_w_in, m_hg_lower_bounds, m_hg_norm_w, m_mla_q_norm_w, m_w_q_up, m_mla_kv_norm_w, m_w_kv_up, m_w_out, m_ln1_g, m_ln1_b, m_w_mlp_in, m_w_mlp_out, m_ln2_g, m_ln2_b, v_w_ada, v_b_ada, v_w_in, v_hg_lower_bounds, v_hg_norm_w, v_mla_q_norm_w, v_w_q_up, v_mla_kv_norm_w, v_w_kv_up, v_w_out, v_ln1_g, v_ln1_b, v_w_mlp_in, v_w_mlp_out, v_ln2_g, v_ln2_b):
    ix, iy, ic = _mesh_pos()
    chip = 2 * ix + iy
    me = 4 * ix + 2 * iy + ic
    core_arr = jnp.reshape(ic, (1,)).astype(jnp.int32)
    chip_arr = jnp.reshape(chip, (1,)).astype(jnp.int32)

    xs = x[0]
    target = loss_target[0]
    t_len = xs.shape[0]
    pos = positions.astype(F32).reshape(t_len, 1)
    inv = 1.0 / (ROPE_THETA ** (jnp.arange(0, ROPE_DIM, 2, dtype=F32) / ROPE_DIM))
    invf = jnp.concatenate([inv, inv, jnp.zeros((128 - ROPE_DIM,), F32)]).reshape(1, 128)

    def slot(w):
        rows, cols = w.shape
        own = w.astype(BF16).reshape(1, 2, rows // 2, cols)
        return lax.dynamic_update_slice(jnp.zeros((N_CHIPS, 2, rows // 2, cols), BF16), own, (chip, 0, 0, 0))

    def slot8(a):
        return lax.dynamic_update_slice(jnp.zeros((N_DEV,) + a.shape, a.dtype), a[None], (me, 0, 0))

    def whole(s):
        return s.reshape(N_CHIPS, 2 * s.shape[2], s.shape[3])

    def halved(g):
        return g.reshape(N_CHIPS, 2, g.shape[1] // 2, g.shape[2])

    ada_cols = w_ada.shape[2]
    c_all, *early = _run_exchange(
        _merge(_gather_all(slot8(jnp.broadcast_to(c, (8, D_MODEL)))),
               _gather_over_ici([slot(jnp.transpose(w_in[0])), slot(w_q_up[0]), slot(w_kv_up[0])])),
        "gather_c_and_mixer_weights_ici")
    b_shard = lax.dynamic_slice(b_ada, (0, chip * ada_cols), (1, ada_cols))
    mod_cols, cond16 = _ada_project(c_all[:, 0, :], w_ada[0], b_shard)
    mod_all, *early = _run_exchange(_merge(_gather_all(slot8(mod_cols)), _gather_over_d2d(early)),
                                    "gather_mod_and_mixer_weights_d2d")
    mod_mine = lax.dynamic_slice(mod_all, (0, me, 0), (N_DEV, 1, ada_cols))[::2, 0, :].reshape(6, D_MODEL)
    sh_a, sc_a, g_a, sh_m, sc_m, g_m = (mod_mine[i:i + 1] for i in range(6))
    g_in, g_q, g_kv = (whole(s) for s in early)
    w_in_full = jnp.pad(g_in.reshape(IN_COLS, D_MODEL), ((0, IN_COLS_PAD - IN_COLS), (0, 0)))
    w_q_full = jnp.pad(g_q, ((0, 0), (0, 0), (0, QK_DIM - g_q.shape[2])))

    w1_rows = D_MODEL // 2
    (u_a, zhg, cq, ckv, q, k, k_t, v, v_t), (s_top, s_out) = _in_project(
        xs, pos, sc_a, sh_a, w_in_full, mla_q_norm_w, mla_kv_norm_w, w_q_full, g_kv, invf,
        _gather_over_ici([slot(w_mlp_in[0, :w1_rows]), slot(w_out[0])]))
    (o_pre, o_hg, states), (s_bottom, s_top, s_out) = _hgrn_forward(
        zhg, hg_lower_bounds, hg_norm_w,
        _merge(_gather_over_ici([slot(w_mlp_in[0, w1_rows:])]), _gather_over_d2d([s_top, s_out])))
    (o_mla, lse), (s_w2, s_bottom) = _attention_forward(
        q, k, v_t, _merge(_gather_over_ici([slot(w_mlp_out[0])]), _gather_over_d2d([s_bottom])))
    w_out_full = whole(s_out).reshape(D_MODEL, D_MODEL)
    (cat, mix, xhat1, rstd1), (s_w2,) = _out_project(o_hg, o_mla, xs, g_a, w_out_full, _gather_over_d2d([s_w2]))
    g_w1_top, g_w1_bottom, g_w2 = whole(s_top), whole(s_bottom), whole(s_w2)
    vecs = jnp.concatenate([ln1_g, ln1_b, sc_m, sh_m, g_m, g_a, ln2_g, ln2_b], axis=0)
    act, dhp, um, dh, dmix, d_cat, dr1, mlp_sums, delta = _mlp_and_back(
        xhat1, rstd1, mix, target, o_mla, vecs, g_w1_top, g_w1_bottom, g_w2, w_out_full)

    gw_1 = halved(_weight_grad(um, dhp, "grad_w_mlp_in", N_CHIPS, D_FF // N_CHIPS, token_tile=4096))
    gw_2, (landed_1,) = _weight_grad(act, dh, "grad_w_mlp_out", N_CHIPS, D_MODEL, a_blocked=True, b_blocked=False,
                                     token_tile=4096, exchange=_pair_exchange([gw_1]))
    gw_out = _weight_grad(cat, dmix, "grad_w_out", 1, D_MODEL, token_tile=2048)
    later = [halved(gw_2), halved(gw_out.reshape(N_CHIPS, D_MODEL // N_CHIPS, D_MODEL))]
    own_1, travels_1 = _add_pair(core_arr, chip_arr, gw_1, landed_1)
    (dq, dk, dv), (landed_1, *landed) = _attention_backward(
        q, k, k_t, v, d_cat, lse, delta, _merge(_chip_exchange([travels_1]), _pair_exchange(later)))
    mine_1 = _add_chips(own_1, landed_1)
    chip_sums = [_add_pair(core_arr, chip_arr, g, l) for g, l in zip(later, landed)]
    (d_hq, d_hf, d_hi, d_hg, hg_sums), (theirs_1, *landed) = _hgrn_backward(
        zhg, hg_lower_bounds, hg_norm_w, o_pre, d_cat, states,
        _merge(_pair_send([mine_1]), _chip_exchange([b for _, b in chip_sums])))
    later_mine = [_add_chips(own, l) for (own, _), l in zip(chip_sums, landed)]
    mlp_mine = [mine_1] + later_mine
    (dz, grad_x, in_sums, gw_q, gw_kv), _ = _in_project_backward(
        dq, dk, dv, cq, ckv, pos, invf, mla_q_norm_w, mla_kv_norm_w, w_q_full, g_kv,
        d_hq, d_hf, d_hi, d_hg, w_in_full, dr1, xs, sc_a)

    zeros = lambda n: jnp.zeros((1, n), F32)
    small = jnp.concatenate([
        in_sums[1:2], in_sums[0:1], mlp_sums[S_DGA:S_DGA + 1],
        mlp_sums[S_DSHM:S_DSHM + 1], mlp_sums[S_DSCM:S_DSCM + 1], mlp_sums[S_DGM:S_DGM + 1],
        mlp_sums[S_DLN1G:S_DLN1G + 1], mlp_sums[S_DLN1B:S_DLN1B + 1],
        mlp_sums[S_DLN2G:S_DLN2G + 1], mlp_sums[S_DLN2B:S_DLN2B + 1],
        jnp.concatenate([hg_sums[0:1], hg_sums[1:2]], axis=1),
        jnp.concatenate([in_sums[2:3, :Q_RANK], in_sums[3:4, :KV_RANK], zeros(D_MODEL - Q_RANK - KV_RANK)], axis=1),
        mlp_sums[S_LOSS:S_LOSS + 1],
        jnp.zeros((SMALL_ROWS - 13, D_MODEL), F32)], axis=0)

    gw_in, (*later_theirs, small_all) = _weight_grad(
        dz, u_a, "grad_w_in", 3, D_MODEL, a_blocked=True, b_blocked=False, token_tile=4096,
        exchange=_merge(_pair_send(later_mine), _gather_all(slot8(small))))
    mlp_theirs = [theirs_1] + list(later_theirs)
    gw_in = gw_in.reshape(IN_COLS_PAD, D_MODEL)
    gw_q = gw_q[:, :, :HEAD_DIM + ROPE_DIM]
    flat = lambda g: g.reshape(g.shape[0] * g.shape[1], g.shape[2])
    mixer_mine, mixer_theirs = _reduce_in_vmem(
        [gw_in, flat(gw_q), flat(gw_kv)], [IN_COLS // N_CHIPS // 2, Q_RANK // 2, KV_RANK // 2], "reduce_mixer_grads")
    reduced = ("w_in", "w_q_up", "w_kv_up", "w_mlp_in", "w_mlp_out", "w_out")
    halves_mine = dict(zip(reduced, list(mixer_mine) + mlp_mine))
    halves_theirs = dict(zip(reduced, list(mixer_theirs) + list(mlp_theirs)))

    small_names = ("b_ada", "hg_lower_bounds", "hg_norm_w", "mla_q_norm_w", "mla_kv_norm_w",
                   "ln1_g", "ln1_b", "ln2_g", "ln2_b")
    loss_row, small_out = _small_params_step(small_all, {
        "b_ada": (b_ada, m_b_ada, v_b_ada),
        "hg_lower_bounds": (hg_lower_bounds, m_hg_lower_bounds, v_hg_lower_bounds),
        "hg_norm_w": (hg_norm_w, m_hg_norm_w, v_hg_norm_w),
        "mla_q_norm_w": (mla_q_norm_w, m_mla_q_norm_w, v_mla_q_norm_w),
        "mla_kv_norm_w": (mla_kv_norm_w, m_mla_kv_norm_w, v_mla_kv_norm_w),
        "ln1_g": (ln1_g, m_ln1_g, v_ln1_g), "ln1_b": (ln1_b, m_ln1_b, v_ln1_b),
        "ln2_g": (ln2_g, m_ln2_g, v_ln2_g), "ln2_b": (ln2_b, m_ln2_b, v_ln2_b)})
    loss = loss_row[0, 0]

    d_mod_all = small_all[:, 0:6, :].reshape(N_DEV, 6 * D_MODEL)
    d_mod_cols = lax.dynamic_slice(d_mod_all, (0, chip * ada_cols), (N_DEV, ada_cols))
    d_mod_cols = jnp.concatenate([d_mod_cols, jnp.zeros_like(d_mod_cols)], axis=0)
    g_w_ada = _weight_grad(cond16, d_mod_cols, "grad_w_ada", 1, ada_cols)[0]

    names = ["w_ada", "b_ada", "w_in", "hg_lower_bounds", "hg_norm_w", "mla_q_norm_w", "w_q_up", "mla_kv_norm_w",
             "w_kv_up", "w_out", "ln1_g", "ln1_b", "w_mlp_in", "w_mlp_out", "ln2_g", "ln2_b"]
    weights = [w_ada, b_ada, w_in, hg_lower_bounds, hg_norm_w, mla_q_norm_w, w_q_up, mla_kv_norm_w,
               w_kv_up, w_out, ln1_g, ln1_b, w_mlp_in, w_mlp_out, ln2_g, ln2_b]
    moms = [m_w_ada, m_b_ada, m_w_in, m_hg_lower_bounds, m_hg_norm_w, m_mla_q_norm_w, m_w_q_up, m_mla_kv_norm_w,
            m_w_kv_up, m_w_out, m_ln1_g, m_ln1_b, m_w_mlp_in, m_w_mlp_out, m_ln2_g, m_ln2_b]
    vels = [v_w_ada, v_b_ada, v_w_in, v_hg_lower_bounds, v_hg_norm_w, v_mla_q_norm_w, v_w_q_up, v_mla_kv_norm_w,
            v_w_kv_up, v_w_out, v_ln1_g, v_ln1_b, v_w_mlp_in, v_w_mlp_out, v_ln2_g, v_ln2_b]
    out_g, out_d, out_m, out_v = [], [], [], []
    for name, w, m, vv in zip(names, weights, moms, vels):
        if name in small_names:
            g, d, nm, nv = small_out[name]
            back = lambda a: a
        elif name == "w_in":
            to2d, back = (lambda a: jnp.transpose(a[0])), (lambda a: jnp.transpose(a)[None])
        else:
            to2d, back = (lambda a, s=w.shape[1:]: a.reshape(s)), (lambda a, s=w.shape: a.reshape(s))
        if name == "w_ada":
            d, nm, nv = _adamw(to2d(w), g_w_ada, to2d(m), to2d(vv), "adamw_" + name)
            g = g_w_ada
        elif name not in small_names:
            g, d, nm, nv = _adamw_halves(core_arr, to2d(w), halves_mine[name], halves_theirs[name], to2d(m), to2d(vv),
                                         "adamw_" + name)
        out_g.append(back(g))
        out_d.append(back(d))
        out_m.append(back(nm))
        out_v.append(back(nv))
    return (loss, grad_x[None], *out_g, *out_d, *out_m, *out_v)
```

```python
import functools

import jax
import jax.numpy as jnp
from jax import lax
from jax.experimental import pallas as pl
from jax.experimental.pallas import tpu as pltpu

F32 = jnp.float32
BF16 = jnp.bfloat16
MESH_IDS = pl.DeviceIdType.MESH

D_MODEL = 1024
N_HEADS = 4
HEAD_DIM = 128
ROPE_DIM = 64
HG_CHUNK = 64
HG_COLS = 2048
Q_RANK = 256
KV_RANK = 256
IN_COLS = 2624
IN_COLS_PAD = 2688
QK_DIM = 256
D_FF = 4096
N_CHIPS = 4
N_DEV = 8
ROPE_THETA = 10000.0
RMS_EPS = 1e-6
LN_EPS = 1e-5
DN_ALPHA = 2.0 ** 0.25
ATT_SCALE = (HEAD_DIM + ROPE_DIM) ** -0.5
NEG_BIG = -1e30
ADAM_LR = 0.001
ADAM_B1 = 0.9
ADAM_B2 = 0.999
ADAM_EPS = 1e-08
ADAM_WD = 0.01
ADAM_STEP = 10
SMALL_ROWS = 16
MIB = 1024 * 1024


def _dot(a, b):
    return jnp.dot(a, b, preferred_element_type=F32)


def _dot_nt(a, b):
    return lax.dot_general(a, b, (((1,), (1,)), ((), ())), preferred_element_type=F32)


def _dot_tn(a, b):
    return lax.dot_general(a, b, (((0,), (0,)), ((), ())), preferred_element_type=F32)


def _params(vmem_mib, semantics=None):
    return pltpu.CompilerParams(vmem_limit_bytes=vmem_mib * MIB, dimension_semantics=semantics)


def _sigmoid(v):
    return 1.0 / (1.0 + jnp.exp(-v))


def _colsum(v):
    return jnp.sum(v, axis=0, keepdims=True)


def _rowmean(v):
    return jnp.mean(v, axis=-1, keepdims=True)


def _rope_tables(pos, invf):
    ang = pos * invf
    lane = lax.broadcasted_iota(jnp.int32, ang.shape, 1)
    cos_t = jnp.where(lane < ROPE_DIM, jnp.cos(ang), 0.0)
    sin = jnp.sin(ang)
    sin_t = jnp.where(lane < ROPE_DIM // 2, -sin, jnp.where(lane < ROPE_DIM, sin, 0.0))
    return cos_t, sin_t


def _swap_halves(t):
    lane = lax.broadcasted_iota(jnp.int32, t.shape, 1)
    return jnp.where(lane < ROPE_DIM // 2, pltpu.roll(t, 128 - ROPE_DIM // 2, 1), pltpu.roll(t, ROPE_DIM // 2, 1))


def _rope(t, cos_t, sin_t):
    return t * cos_t + _swap_halves(t) * sin_t


def _unrope(g, cos_t, sin_t):
    return g * cos_t - _swap_halves(g) * sin_t


def _mesh_pos():
    return lax.axis_index("x"), lax.axis_index("y"), lax.axis_index("c")


def _other_chips(x, y):
    out = []
    for dx, dy in ((1, 0), (0, 1), (1, 1)):
        px = 1 - x if dx else x
        py = 1 - y if dy else y
        out.append(((px, py), 2 * px + py))
    return out


class _Exchange:
    def __init__(self, inputs, out_shapes, aliases, sems, start, finish):
        self.inputs, self.out_shapes, self.aliases, self.sems = list(inputs), list(out_shapes), dict(aliases), list(sems)
        self.start, self.finish = start, finish


def _from_copies(inputs, out_shapes, aliases, sems, copies):
    def start(ins, outs, sem_refs):
        for send, _ in copies(ins, outs, sem_refs):
            send.start()

    def finish(ins, outs, sem_refs):
        for send, recv in copies(ins, outs, sem_refs):
            recv.wait_recv()
            send.wait_send()

    return _Exchange(inputs, out_shapes, aliases, sems, start, finish)


HBM_MIN_BYTES = 256 * 1024


def _in_hbm(a):
    if a.size * a.dtype.itemsize < HBM_MIN_BYTES:
        return a
    return pltpu.with_memory_space_constraint(a, pltpu.HBM)


def _out_hbm(s):
    if s.size * s.dtype.itemsize < HBM_MIN_BYTES:
        return s
    return pltpu.HBM(s.shape, s.dtype)


def _pcall(body, *, operands, out_shape, **kwargs):
    single = not isinstance(out_shape, (list, tuple))
    shapes = [_out_hbm(s) for s in ([out_shape] if single else out_shape)]
    return pl.pallas_call(body, out_shape=shapes[0] if single else shapes, **kwargs)(*[_in_hbm(a) for a in operands])


def _run_exchange(exchange, name):
    n_in, n_out = len(exchange.inputs), len(exchange.out_shapes)

    def body(*refs):
        ins, outs, sem_refs = refs[:n_in], refs[n_in:n_in + n_out], refs[n_in + n_out:]
        exchange.start(ins, outs, sem_refs)
        exchange.finish(ins, outs, sem_refs)

    any_spec = pl.BlockSpec(memory_space=pl.ANY)
    return pl.pallas_call(
        body, name=name, out_shape=[_out_hbm(s) for s in exchange.out_shapes],
        in_specs=[any_spec] * n_in, out_specs=[any_spec] * n_out,
        scratch_shapes=exchange.sems, input_output_aliases=exchange.aliases,
    )(*[_in_hbm(a) for a in exchange.inputs])


def _pallas(body, *, name, operands, in_specs, out_shape, out_specs, params, scratch_shapes=(), grid=(), prefetch=(),
            exchange=None, first=None, last=None):
    n_pre, n_in, n_out, n_scr = len(prefetch), len(in_specs), len(out_specs), len(scratch_shapes)
    ex_in = exchange.inputs if exchange else []
    ex_out = exchange.out_shapes if exchange else []
    ex_sems = exchange.sems if exchange else []

    def full_body(*refs):
        pre, rest = refs[:n_pre], refs[n_pre:]
        ins, rest = rest[:n_in], rest[n_in:]
        xin, rest = rest[:len(ex_in)], rest[len(ex_in):]
        outs, rest = rest[:n_out], rest[n_out:]
        xout, rest = rest[:len(ex_out)], rest[len(ex_out):]
        scr, sem_refs = rest[:n_scr], rest[n_scr:]
        if exchange:
            @pl.when(first(*pre))
            def _():
                exchange.start(xin, xout, sem_refs)

        body(*pre, *ins, *outs, *scr)
        if exchange:
            @pl.when(last(*pre))
            def _():
                exchange.finish(xin, xout, sem_refs)

    any_spec = pl.BlockSpec(memory_space=pl.ANY)
    aliases = {n_pre + n_in + i: n_out + o for i, o in exchange.aliases.items()} if exchange else {}
    operands = [_in_hbm(a) for a in operands]
    results = pl.pallas_call(
        full_body, name=name, out_shape=[_out_hbm(s) for s in list(out_shape) + ex_out],
        grid_spec=pltpu.PrefetchScalarGridSpec(
            num_scalar_prefetch=n_pre, grid=grid, in_specs=list(in_specs) + [any_spec] * len(ex_in),
            out_specs=list(out_specs) + [any_spec] * len(ex_out), scratch_shapes=list(scratch_shapes) + ex_sems),
        input_output_aliases=aliases, compiler_params=params,
    )(*prefetch, *operands, *[_in_hbm(a) for a in ex_in])
    return results[:n_out], results[n_out:]


def _remote(src, dst, sems, idx, to):
    send_sems, recv_sems = sems
    return pltpu.make_async_remote_copy(src_ref=src, dst_ref=dst, send_sem=send_sems.at[idx], recv_sem=recv_sems.at[idx],
                                        device_id=to, device_id_type=MESH_IDS)


def _sem_pairs(*shape):
    return [pltpu.SemaphoreType.DMA(shape), pltpu.SemaphoreType.DMA(shape)]


def _same_shapes(arrays):
    return [jax.ShapeDtypeStruct(a.shape, a.dtype) for a in arrays]


def _gather_over_ici(slots):
    n = len(slots)

    def copies(ins, outs, sems):
        x, y, c = _mesh_pos()
        k = 2 * x + y
        out = []
        for j, (chip, kj) in enumerate(_other_chips(x, y)):
            for i in range(n):
                to = (*chip, c)
                out.append((_remote(ins[i].at[k, c], outs[i].at[k, c], sems, (j, i), to),
                            _remote(ins[i].at[k, c], outs[i].at[kj, c], sems, (j, i), to)))
        return out

    return _from_copies(slots, _same_shapes(slots), {i: i for i in range(n)}, _sem_pairs(3, n), copies)


def _gather_over_d2d(slots):
    n = len(slots)

    def copies(ins, outs, sems):
        x, y, c = _mesh_pos()
        sibling = (x, y, 1 - c)
        out = []
        for j, (_, kj) in enumerate(_other_chips(x, y)):
            for i in range(n):
                out.append((_remote(ins[i].at[kj, c], outs[i].at[kj, c], sems, (j, i), sibling),
                            _remote(ins[i].at[kj, c], outs[i].at[kj, 1 - c], sems, (j, i), sibling)))
        return out

    return _from_copies(slots, _same_shapes(slots), {i: i for i in range(n)}, _sem_pairs(3, n), copies)


def _gather_all(slots8):
    def copies(ins, outs, sems):
        x, y, c = _mesh_pos()
        me = 4 * x + 2 * y + c
        out = []
        for r in range(1, N_DEV):
            px = 1 - x if r & 4 else x
            py = 1 - y if r & 2 else y
            pc = 1 - c if r & 1 else c
            to = (px, py, pc)
            out.append((_remote(ins[0].at[me], outs[0].at[me], sems, r - 1, to),
                        _remote(ins[0].at[me], outs[0].at[4 * px + 2 * py + pc], sems, r - 1, to)))
        return out

    return _from_copies([slots8], _same_shapes([slots8]), {0: 0}, _sem_pairs(N_DEV - 1), copies)


def _merge(first, second):
    n_in, n_out, n_sem = len(first.inputs), len(first.out_shapes), len(first.sems)

    def start(ins, outs, sems):
        first.start(ins[:n_in], outs[:n_out], sems[:n_sem])
        second.start(ins[n_in:], outs[n_out:], sems[n_sem:])

    def finish(ins, outs, sems):
        first.finish(ins[:n_in], outs[:n_out], sems[:n_sem])
        second.finish(ins[n_in:], outs[n_out:], sems[n_sem:])

    aliases = dict(first.aliases)
    aliases.update({n_in + i: n_out + o for i, o in second.aliases.items()})
    return _Exchange(first.inputs + second.inputs, first.out_shapes + second.out_shapes, aliases,
                     first.sems + second.sems, start, finish)


def _pair_exchange(grads):
    n = len(grads)

    def copies(ins, outs, sems):
        x, y, c = _mesh_pos()
        cps = [_remote(ins[i].at[:, 1 - c], outs[i], sems, i, (x, y, 1 - c)) for i in range(n)]
        return [(cp, cp) for cp in cps]

    shapes = [jax.ShapeDtypeStruct((N_CHIPS,) + g.shape[2:], g.dtype) for g in grads]
    return _from_copies(grads, shapes, {}, _sem_pairs(n), copies)


def _chip_exchange(partials):
    n = len(partials)

    def copies(ins, outs, sems):
        x, y, c = _mesh_pos()
        cps = [_remote(ins[i].at[kj], outs[i].at[j], sems, (j, i), (*chip, c))
               for j, (chip, kj) in enumerate(_other_chips(x, y)) for i in range(n)]
        return [(cp, cp) for cp in cps]

    shapes = [jax.ShapeDtypeStruct((3,) + p.shape[1:], p.dtype) for p in partials]
    return _from_copies(partials, shapes, {}, _sem_pairs(3, n), copies)


def _pair_send(halves):
    n = len(halves)

    def copies(ins, outs, sems):
        x, y, c = _mesh_pos()
        cps = [_remote(ins[i], outs[i], sems, i, (x, y, 1 - c)) for i in range(n)]
        return [(cp, cp) for cp in cps]

    return _from_copies(halves, _same_shapes(halves), {}, _sem_pairs(n), copies)


def _reduce_in_vmem(grads, half_rows, name):
    n = len(grads)

    def body(*refs):
        g, mine, theirs = refs[:n], refs[n:2 * n], refs[2 * n:3 * n]
        landed_pair, partial, landed_chips = refs[3 * n:4 * n], refs[4 * n:5 * n], refs[5 * n:6 * n]
        sems = refs[6 * n:]
        x, y, c = _mesh_pos()
        k = 2 * x + y
        sibling = (x, y, 1 - c)

        def half(i, chip_idx, which):
            return pl.ds(pl.multiple_of((2 * chip_idx + which) * half_rows[i], 8), half_rows[i])

        def run(copies):
            for cp in copies:
                cp.start()
            for cp in copies:
                cp.wait_recv()
                cp.wait_send()

        run([_remote(g[i].at[half(i, kk, 1 - c)], landed_pair[i].at[kk], sems[0:2], (kk, i), sibling)
             for kk in range(N_CHIPS) for i in range(n)])
        for i in range(n):
            for kk in range(N_CHIPS):
                partial[i][kk] = (g[i][half(i, kk, c), :] + landed_pair[i][kk]).astype(BF16)
        run([_remote(partial[i].at[kj], landed_chips[i].at[j], sems[2:4], (j, i), (*chip, c))
             for j, (chip, kj) in enumerate(_other_chips(x, y)) for i in range(n)])
        for i in range(n):
            own = g[i][half(i, k, c), :] + landed_pair[i][k]
            mine[i][...] = ((own + landed_chips[i][0].astype(F32)) + landed_chips[i][1].astype(F32)) \
                + landed_chips[i][2].astype(F32)
        run([_remote(mine[i], theirs[i], sems[4:6], i, sibling) for i in range(n)])

    shapes = [(h, gr.shape[1]) for gr, h in zip(grads, half_rows)]
    halves = [jax.ShapeDtypeStruct(s, F32) for s in shapes]
    vmem = pl.BlockSpec(memory_space=pltpu.VMEM)
    scratch = ([pltpu.VMEM((N_CHIPS,) + s, F32) for s in shapes]
               + [pltpu.VMEM((N_CHIPS,) + s, BF16) for s in shapes]
               + [pltpu.VMEM((3,) + s, BF16) for s in shapes]
               + _sem_pairs(N_CHIPS, n) + _sem_pairs(3, n) + _sem_pairs(n))
    out = pl.pallas_call(
        body, name=name, out_shape=halves + halves, in_specs=[vmem] * n, out_specs=[vmem] * (2 * n),
        scratch_shapes=scratch, compiler_params=_params(48),
    )(*grads)
    return out[:n], out[n:]


def _row_tile(rows):
    for t in (256, 128, 64):
        if rows % t == 0:
            return t
    return rows


def _add_pair(core, chip, grad, landed):
    _, h, cols = landed.shape
    tr = _row_tile(h)

    def body(core_ref, chip_ref, g_ref, l_ref, own_ref, ob_ref):
        s = g_ref[...] + l_ref[...]
        ob_ref[...] = s.astype(BF16)

        @pl.when(pl.program_id(1) == chip_ref[0])
        def _():
            own_ref[...] = s

    return _pcall(
        body, name="grad_add_pair",
        out_shape=[jax.ShapeDtypeStruct((h, cols), F32), jax.ShapeDtypeStruct(landed.shape, BF16)],
        grid_spec=pltpu.PrefetchScalarGridSpec(
            num_scalar_prefetch=2, grid=(h // tr, N_CHIPS),
            in_specs=[pl.BlockSpec((None, None, tr, cols), lambda t, k, core_ref, chip_ref: (k, core_ref[0], t, 0)),
                      pl.BlockSpec((None, tr, cols), lambda t, k, core_ref, chip_ref: (k, t, 0))],
            out_specs=[pl.BlockSpec((tr, cols), lambda t, k, core_ref, chip_ref: (t, 0)),
                       pl.BlockSpec((None, tr, cols), lambda t, k, core_ref, chip_ref: (k, t, 0))]),
        compiler_params=_params(32, ("arbitrary", "arbitrary")),
        operands=(core, chip, grad, landed))


def _add_chips(own, landed):
    h, cols = own.shape
    tr = _row_tile(h)

    def body(p_ref, l_ref, o_ref):
        o_ref[...] = ((p_ref[...] + l_ref[0].astype(F32)) + l_ref[1].astype(F32)) + l_ref[2].astype(F32)

    return _pcall(
        body, name="grad_add_chips", grid=(h // tr,),
        out_shape=jax.ShapeDtypeStruct((h, cols), F32),
        in_specs=[pl.BlockSpec((tr, cols), lambda t: (t, 0)), pl.BlockSpec((3, tr, cols), lambda t: (0, t, 0))],
        out_specs=pl.BlockSpec((tr, cols), lambda t: (t, 0)),
        compiler_params=_params(32, ("arbitrary",)),
        operands=(own, landed))


def _ada_project(c_all, w_ada, b_shard):
    n = w_ada.shape[1]
    tn = 512

    def body(c_ref, w_ref, b_ref, mod_ref, cond_ref):
        cv = c_ref[...]
        cond = cv * _sigmoid(cv)
        mod_ref[...] = _dot(cond.astype(BF16), w_ref[...].astype(BF16)) + b_ref[...]
        cond_ref[0:N_DEV, :] = cond
        cond_ref[N_DEV:2 * N_DEV, :] = jnp.zeros_like(cond)

    return _pcall(
        body, name="ada_project", grid=(n // tn,),
        out_shape=[jax.ShapeDtypeStruct((N_DEV, n), F32), jax.ShapeDtypeStruct((2 * N_DEV, D_MODEL), F32)],
        in_specs=[pl.BlockSpec((N_DEV, D_MODEL), lambda j: (0, 0)), pl.BlockSpec((D_MODEL, tn), lambda j: (0, j)),
                  pl.BlockSpec((1, tn), lambda j: (0, j))],
        out_specs=[pl.BlockSpec((N_DEV, tn), lambda j: (0, j)), pl.BlockSpec((2 * N_DEV, D_MODEL), lambda j: (0, 0))],
        compiler_params=_params(32, ("arbitrary",)),
        operands=(c_all, w_ada, b_shard))


def _in_project(x, pos, sc_a, sh_a, w_in, q_norm_w, kv_norm_w, w_q, w_kv, invf, exchange=None):
    t_len = x.shape[0]
    tm = min(512, t_len)

    def body(x_ref, pos_ref, sc_ref, sh_ref, win_ref, qn_ref, kvn_ref, wq_ref, wkv_ref, invf_ref,
             u_ref, zhg_ref, cq_ref, ckv_ref, q_ref, k_ref, kt_ref, v_ref, vt_ref):
        u = (x_ref[...] * (1.0 + sc_ref[...]) + sh_ref[...]).astype(BF16)
        u_ref[...] = u
        z = _dot_nt(u, win_ref[...])
        zhg_ref[...] = z[:, :HG_COLS]
        cq = z[:, HG_COLS:HG_COLS + Q_RANK]
        ckv = z[:, HG_COLS + Q_RANK:HG_COLS + Q_RANK + KV_RANK]
        cq_ref[...] = cq
        ckv_ref[...] = ckv
        cos_t, sin_t = _rope_tables(pos_ref[...], invf_ref[...])
        k_pe = _rope(z[:, HG_COLS + Q_RANK + KV_RANK:], cos_t, sin_t)
        k_pe_t = jnp.transpose(k_pe).astype(BF16)
        cqn = (cq * lax.rsqrt(_rowmean(cq * cq) + RMS_EPS) * qn_ref[...]).astype(BF16)
        ckvn = (ckv * lax.rsqrt(_rowmean(ckv * ckv) + RMS_EPS) * kvn_ref[...]).astype(BF16)
        for h in range(N_HEADS):
            qh = _dot(cqn, wq_ref[h])
            q_ref[h, :, 0:HEAD_DIM] = qh[:, :HEAD_DIM].astype(BF16)
            q_ref[h, :, HEAD_DIM:QK_DIM] = _rope(qh[:, HEAD_DIM:], cos_t, sin_t).astype(BF16)
            kvh = _dot(ckvn, wkv_ref[h])
            k_ref[h, :, 0:HEAD_DIM] = kvh[:, :HEAD_DIM].astype(BF16)
            k_ref[h, :, HEAD_DIM:QK_DIM] = k_pe.astype(BF16)
            kt_ref[h, 0:HEAD_DIM, :] = jnp.transpose(kvh[:, :HEAD_DIM]).astype(BF16)
            kt_ref[h, HEAD_DIM:QK_DIM, :] = k_pe_t
            v_ref[h] = kvh[:, HEAD_DIM:].astype(BF16)
            vt_ref[h] = jnp.transpose(kvh[:, HEAD_DIM:]).astype(BF16)

    row = lambda i: (i, 0)
    fixed2 = lambda i: (0, 0)
    fixed3 = lambda i: (0, 0, 0)
    heads = lambda i: (0, i, 0)
    n_tiles = t_len // tm
    return _pallas(
        body, name="in_project", grid=(n_tiles,),
        operands=(x, pos, sc_a, sh_a, w_in, q_norm_w, kv_norm_w, w_q, w_kv, invf),
        out_shape=[jax.ShapeDtypeStruct((t_len, D_MODEL), BF16), jax.ShapeDtypeStruct((t_len, HG_COLS), F32),
                   jax.ShapeDtypeStruct((t_len, Q_RANK), F32), jax.ShapeDtypeStruct((t_len, KV_RANK), F32),
                   jax.ShapeDtypeStruct((N_HEADS, t_len, QK_DIM), BF16),
                   jax.ShapeDtypeStruct((N_HEADS, t_len, QK_DIM), BF16),
                   jax.ShapeDtypeStruct((N_HEADS, QK_DIM, t_len), BF16),
                   jax.ShapeDtypeStruct((N_HEADS, t_len, HEAD_DIM), BF16),
                   jax.ShapeDtypeStruct((N_HEADS, HEAD_DIM, t_len), BF16)],
        in_specs=[pl.BlockSpec((tm, D_MODEL), row), pl.BlockSpec((tm, 1), row),
                  pl.BlockSpec((1, D_MODEL), fixed2), pl.BlockSpec((1, D_MODEL), fixed2),
                  pl.BlockSpec((IN_COLS_PAD, D_MODEL), fixed2),
                  pl.BlockSpec((1, Q_RANK), fixed2), pl.BlockSpec((1, KV_RANK), fixed2),
                  pl.BlockSpec((N_HEADS, Q_RANK, QK_DIM), fixed3), pl.BlockSpec((N_HEADS, KV_RANK, 2 * HEAD_DIM), fixed3),
                  pl.BlockSpec((1, 128), fixed2)],
        out_specs=[pl.BlockSpec((tm, D_MODEL), row), pl.BlockSpec((tm, HG_COLS), row),
                   pl.BlockSpec((tm, Q_RANK), row), pl.BlockSpec((tm, KV_RANK), row),
                   pl.BlockSpec((N_HEADS, tm, QK_DIM), heads), pl.BlockSpec((N_HEADS, tm, QK_DIM), heads),
                   pl.BlockSpec((N_HEADS, QK_DIM, tm), lambda i: (0, 0, i)),
                   pl.BlockSpec((N_HEADS, tm, HEAD_DIM), heads),
                   pl.BlockSpec((N_HEADS, HEAD_DIM, tm), lambda i: (0, 0, i))],
        params=_params(48, ("arbitrary",)), exchange=exchange,
        first=lambda: pl.program_id(0) == 0, last=lambda: pl.program_id(0) == n_tiles - 1)


def _lower_bound(lb_raw):
    m = jnp.max(lb_raw, axis=0, keepdims=True)
    e = jnp.exp(lb_raw - m)
    return e[0:1] / jnp.sum(e, axis=0, keepdims=True)


def _tri(inclusive_lower):
    r = lax.broadcasted_iota(jnp.int32, (HG_CHUNK, HG_CHUNK), 0)
    c = lax.broadcasted_iota(jnp.int32, (HG_CHUNK, HG_CHUNK), 1)
    return (c <= r) if inclusive_lower else (c >= r)


def _chunk_rows(n):
    return slice(n * HG_CHUNK, (n + 1) * HG_CHUNK)


def _chunk_prefix_sums(v, inclusive_lower):
    tri = _tri(inclusive_lower).astype(BF16)
    hi = v.astype(BF16)
    rest = v - hi.astype(F32)
    mid = rest.astype(BF16)
    lo = (rest - mid.astype(F32)).astype(BF16)
    pieces = jnp.concatenate([hi, mid, lo], axis=1)
    out = []
    for n in range(v.shape[0] // HG_CHUNK):
        s = _dot(tri, pieces[_chunk_rows(n)])
        out.append((s[:, 0:HEAD_DIM] + s[:, HEAD_DIM:2 * HEAD_DIM]) + s[:, 2 * HEAD_DIM:])
    return jnp.concatenate(out, axis=0)


def _per_chunk(v, row):
    n = v.shape[0] // HG_CHUNK
    v3 = v.reshape(n, HG_CHUNK, HEAD_DIM)
    return jnp.broadcast_to(v3[:, row:row + 1, :], v3.shape).reshape(v.shape)


def _hg_block(q, f_logit, lb):
    sg = _sigmoid(f_logit)
    forget = lb + (1.0 - lb) * sg
    kk = 1.0 - forget
    b = _chunk_prefix_sums(jnp.log(forget), True)
    b_ref = _per_chunk(b, HG_CHUNK // 2 - 1)
    b_last = _per_chunk(b, HG_CHUNK - 1)
    e_i = jnp.exp(b - b_ref)
    e_ri = jnp.exp(b_ref - b)
    e_b = jnp.exp(b)
    e_l = jnp.exp(b_last - b)
    return dict(sg=sg, forget=forget, e_i=e_i, e_ri=e_ri, e_b=e_b, e_l=e_l, dec=jnp.exp(b_last),
                qi=q * e_i, ki=kk * e_ri, qe=q * e_b, kl=kk * e_l)


HG_STEP_HEADS = 4


def _head_cols(hh):
    return slice(hh * HEAD_DIM, (hh + 1) * HEAD_DIM)


def _hgrn_forward(zhg, lb_raw, norm_w, exchange=None):
    t_len = zhg.shape[0]
    tb = min(512, t_len)
    n_chunks = tb // HG_CHUNK
    hs = HG_STEP_HEADS

    def body(q_ref, f_ref, v_ref, g_ref, lb_ref, w_ref, opre_ref, o_ref, st_ref, state):
        @pl.when(pl.program_id(1) == 0)
        def _():
            state[...] = jnp.zeros_like(state)

        causal = _tri(True)
        heads = range(hs)
        blk, v, qi, ki, qe, kl = {}, {}, {}, {}, {}, {}
        for hh in heads:
            cols = _head_cols(hh)
            blk[hh] = _hg_block(q_ref[:, cols], f_ref[:, cols], _lower_bound(lb_ref[:, cols]))
            v[hh] = v_ref[:, cols].astype(BF16)
            qi[hh], ki[hh], qe[hh], kl[hh] = (blk[hh][name].astype(BF16) for name in ("qi", "ki", "qe", "kl"))
        st = {hh: state[hh] for hh in heads}
        parts = {hh: [] for hh in heads}
        for n in range(n_chunks):
            r = _chunk_rows(n)
            for hh in heads:
                a = jnp.where(causal, _dot_nt(qi[hh][r], ki[hh][r]), 0.0).astype(BF16)
                st_ref[hh, n] = st[hh]
                parts[hh].append(_dot(a, v[hh][r]) + _dot_nt(qe[hh][r], st[hh].astype(BF16)))
                st[hh] = st[hh] * blk[hh]["dec"][n * HG_CHUNK:n * HG_CHUNK + 1] + _dot_tn(v[hh][r], kl[hh][r])
        for hh in heads:
            cols = _head_cols(hh)
            state[hh] = st[hh]
            o = jnp.concatenate(parts[hh], axis=0)
            opre_ref[:, cols] = o
            g = g_ref[:, cols]
            o_ref[:, cols] = o * lax.rsqrt(_rowmean(o * o) + RMS_EPS) * w_ref[:, cols] * (g * _sigmoid(g))

    groups = N_HEADS // hs
    wide = hs * HEAD_DIM
    col = lambda off: (lambda h, t: (t, off + h))
    nb = t_len // tb
    return _pallas(
        body, name="hgrn_forward", grid=(groups, nb), operands=(zhg, zhg, zhg, zhg, lb_raw, norm_w),
        out_shape=[jax.ShapeDtypeStruct((t_len, N_HEADS * HEAD_DIM), F32),
                   jax.ShapeDtypeStruct((t_len, N_HEADS * HEAD_DIM), F32),
                   jax.ShapeDtypeStruct((N_HEADS, t_len // HG_CHUNK, HEAD_DIM, HEAD_DIM), F32)],
        in_specs=[pl.BlockSpec((tb, wide), col(0)), pl.BlockSpec((tb, wide), col(groups)),
                  pl.BlockSpec((tb, wide), col(2 * groups)), pl.BlockSpec((tb, wide), col(3 * groups)),
                  pl.BlockSpec((2, wide), lambda h, t: (0, h)), pl.BlockSpec((1, wide), lambda h, t: (0, h))],
        out_specs=[pl.BlockSpec((tb, wide), col(0)), pl.BlockSpec((tb, wide), col(0)),
                   pl.BlockSpec((hs, n_chunks, HEAD_DIM, HEAD_DIM), lambda h, t: (h, t, 0, 0))],
        scratch_shapes=[pltpu.VMEM((hs, HEAD_DIM, HEAD_DIM), F32)],
        params=_params(40, ("arbitrary", "arbitrary")), exchange=exchange,
        first=lambda: (pl.program_id(0) == 0) & (pl.program_id(1) == 0),
        last=lambda: (pl.program_id(0) == groups - 1) & (pl.program_id(1) == nb - 1))


def _hgrn_backward(zhg, lb_raw, norm_w, o_pre, d_cat, states, exchange=None):
    t_len = zhg.shape[0]
    tb = min(512, t_len)
    n_chunks = tb // HG_CHUNK
    nb = t_len // tb
    hs = HG_STEP_HEADS

    def body(q_ref, f_ref, v_ref, g_ref, lb_ref, w_ref, opre_ref, do_ref, st_ref,
             dq_ref, df_ref, dv_ref, dg_ref, sums_ref, gstate):
        @pl.when(pl.program_id(1) == 0)
        def _():
            gstate[...] = jnp.zeros_like(gstate)
            sums_ref[...] = jnp.zeros_like(sums_ref)

        heads = range(hs)
        causal = _tri(True)
        row_id = lax.broadcasted_iota(jnp.int32, (HG_CHUNK, HEAD_DIM), 0)
        lb, d_o, blk, v, qi, ki, qe, kl = ({} for _ in range(8))
        for hh in heads:
            cols = _head_cols(hh)
            lb[hh] = _lower_bound(lb_ref[:, cols])
            w = w_ref[:, cols]
            o = opre_ref[:, cols]
            g = g_ref[:, cols]
            d_out = do_ref[:, cols]
            r = lax.rsqrt(_rowmean(o * o) + RMS_EPS)
            sg_g = _sigmoid(g)
            dg_ref[:, cols] = (d_out * (o * r * w) * (sg_g * (1.0 + g * (1.0 - sg_g)))).astype(BF16)
            d_on = d_out * (g * sg_g)
            sums_ref[1:2, cols] += _colsum(d_on * o * r)
            dy = d_on * w
            d_o[hh] = (r * dy - o * (r * r * r) * _rowmean(dy * o)).astype(BF16)
            blk[hh] = _hg_block(q_ref[:, cols], f_ref[:, cols], lb[hh])
            v[hh] = v_ref[:, cols].astype(BF16)
            qi[hh], ki[hh], qe[hh], kl[hh] = (blk[hh][name].astype(BF16) for name in ("qi", "ki", "qe", "kl"))
        gt = {hh: gstate[hh] for hh in heads}
        d_v, d_qi, d_ki, d_qe, d_kl, d_dec = ({hh: [None] * n_chunks for hh in heads} for _ in range(6))
        for n in reversed(range(n_chunks)):
            rows = _chunk_rows(n)
            for hh in heads:
                st = st_ref[hh, n]
                a = jnp.where(causal, _dot_nt(qi[hh][rows], ki[hh][rows]), 0.0).astype(BF16)
                d_a = jnp.where(causal, _dot_nt(d_o[hh][rows], v[hh][rows]), 0.0).astype(BF16)
                gt_b = gt[hh].astype(BF16)
                d_v[hh][n] = _dot_tn(a, d_o[hh][rows]) + _dot_nt(kl[hh][rows], gt_b)
                d_qi[hh][n] = _dot(d_a, ki[hh][rows])
                d_ki[hh][n] = _dot_tn(d_a, qi[hh][rows])
                d_qe[hh][n] = _dot(d_o[hh][rows], st.astype(BF16))
                d_kl[hh][n] = _dot(v[hh][rows], gt_b)
                d_dec[hh][n] = jnp.where(row_id == HG_CHUNK - 1, _colsum(gt[hh] * st), 0.0)
                gt[hh] = gt[hh] * blk[hh]["dec"][n * HG_CHUNK:n * HG_CHUNK + 1] + _dot_tn(d_o[hh][rows], qe[hh][rows])
        for hh in heads:
            cols = _head_cols(hh)
            b = blk[hh]
            gstate[hh] = gt[hh]
            dqi, dki, dqe, dkl, ddec = (jnp.concatenate(p[hh], axis=0) for p in (d_qi, d_ki, d_qe, d_kl, d_dec))
            dv_ref[:, cols] = jnp.concatenate(d_v[hh], axis=0).astype(BF16)
            dq_ref[:, cols] = (dqi * b["e_i"] + dqe * b["e_b"]).astype(BF16)
            d_k = dki * b["e_ri"] + dkl * b["e_l"]
            t_qi = dqi * b["qi"]
            t_ki = dki * b["ki"]
            t_kl = dkl * b["kl"]
            at_ref, at_last = [], []
            for n in range(n_chunks):
                rows = _chunk_rows(n)
                at_ref.append(jnp.where(row_id == HG_CHUNK // 2 - 1, _colsum(t_ki[rows] - t_qi[rows]), 0.0))
                at_last.append(jnp.where(row_id == HG_CHUNK - 1, _colsum(t_kl[rows]), 0.0))
            d_b = (t_qi - t_ki + dqe * b["qe"] - t_kl + jnp.concatenate(at_ref, axis=0)
                   + jnp.concatenate(at_last, axis=0) + ddec * b["dec"])
            d_forget = _chunk_prefix_sums(d_b, False) / b["forget"] - d_k
            sg = b["sg"]
            df_ref[:, cols] = (d_forget * (1.0 - lb[hh]) * sg * (1.0 - sg)).astype(BF16)
            sums_ref[0:1, cols] += _colsum(d_forget * (1.0 - sg))

    groups = N_HEADS // hs
    wide = hs * HEAD_DIM
    col = lambda off: (lambda h, t: (nb - 1 - t, off + h))
    return _pallas(
        body, name="hgrn_backward", grid=(groups, nb),
        operands=(zhg, zhg, zhg, zhg, lb_raw, norm_w, o_pre, d_cat, states),
        out_shape=[jax.ShapeDtypeStruct((t_len, N_HEADS * HEAD_DIM), BF16)] * 4
        + [jax.ShapeDtypeStruct((8, N_HEADS * HEAD_DIM), F32)],
        in_specs=[pl.BlockSpec((tb, wide), col(0)), pl.BlockSpec((tb, wide), col(groups)),
                  pl.BlockSpec((tb, wide), col(2 * groups)), pl.BlockSpec((tb, wide), col(3 * groups)),
                  pl.BlockSpec((2, wide), lambda h, t: (0, h)), pl.BlockSpec((1, wide), lambda h, t: (0, h)),
                  pl.BlockSpec((tb, wide), col(0)), pl.BlockSpec((tb, wide), col(0)),
                  pl.BlockSpec((hs, n_chunks, HEAD_DIM, HEAD_DIM), lambda h, t: (h, nb - 1 - t, 0, 0))],
        out_specs=[pl.BlockSpec((tb, wide), col(0))] * 4 + [pl.BlockSpec((8, wide), lambda h, t: (0, h))],
        scratch_shapes=[pltpu.VMEM((hs, HEAD_DIM, HEAD_DIM), F32)],
        params=_params(40, ("arbitrary", "arbitrary")), exchange=exchange,
        first=lambda: (pl.program_id(0) == 0) & (pl.program_id(1) == 0),
        last=lambda: (pl.program_id(0) == groups - 1) & (pl.program_id(1) == nb - 1))


ATT_LOG2 = ATT_SCALE * 1.4426950408889634


def _triangle_steps(nq, q_major):
    if q_major:
        pairs = [(i, j) for i in range(nq) for j in range(i + 1)]
    else:
        pairs = [(i, j) for j in range(nq) for i in range(j, nq)]
    return jnp.array([p[0] for p in pairs], jnp.int32), jnp.array([p[1] for p in pairs], jnp.int32)


def _key_le_query(t):
    return lax.broadcasted_iota(jnp.int32, (t, t), 0) <= lax.broadcasted_iota(jnp.int32, (t, t), 1)


def _attention_forward(q, k, v_t, exchange=None):
    t_len = q.shape[1]
    tq = min(512, t_len)
    nq = t_len // tq
    qi_tab, ki_tab = _triangle_steps(nq, True)

    def body(qi_ref, ki_ref, q_ref, k_ref, vt_ref, o_ref, lse_ref, m_s, l_s, acc_s):
        step = pl.program_id(0)
        qi, ki = qi_ref[step], ki_ref[step]

        @pl.when(ki == 0)
        def _():
            m_s[...] = jnp.full_like(m_s, NEG_BIG)
            l_s[...] = jnp.zeros_like(l_s)
            acc_s[...] = jnp.zeros_like(acc_s)

        def accumulate(masked):
            for h in range(N_HEADS):
                s_t = _dot_nt(k_ref[h], q_ref[h]) * ATT_LOG2
                if masked:
                    s_t = jnp.where(_key_le_query(tq), s_t, NEG_BIG)
                m_old = m_s[h]
                m_new = jnp.maximum(m_old, jnp.max(s_t, axis=0, keepdims=True))
                alpha = jnp.exp2(m_old - m_new)
                p_t = jnp.exp2(s_t - m_new)
                l_s[h] = alpha * l_s[h] + jnp.sum(p_t, axis=0, keepdims=True)
                acc_s[h] = alpha * acc_s[h] + _dot(vt_ref[h], p_t.astype(BF16))
                m_s[h] = m_new

        @pl.when(ki < qi)
        def _():
            accumulate(False)

        @pl.when(ki == qi)
        def _():
            accumulate(True)
            for h in range(N_HEADS):
                o_ref[:, h * HEAD_DIM:(h + 1) * HEAD_DIM] = jnp.transpose(acc_s[h] / l_s[h])
                lse_ref[h] = m_s[h] + jnp.log2(l_s[h])

    n_steps = qi_tab.shape[0]
    return _pallas(
        body, name="attention_forward", grid=(n_steps,), prefetch=(qi_tab, ki_tab), operands=(q, k, v_t),
        out_shape=[jax.ShapeDtypeStruct((t_len, N_HEADS * HEAD_DIM), F32),
                   jax.ShapeDtypeStruct((N_HEADS, 1, t_len), F32)],
        in_specs=[pl.BlockSpec((N_HEADS, tq, QK_DIM), lambda s, qt, kt: (0, qt[s], 0)),
                  pl.BlockSpec((N_HEADS, tq, QK_DIM), lambda s, qt, kt: (0, kt[s], 0)),
                  pl.BlockSpec((N_HEADS, HEAD_DIM, tq), lambda s, qt, kt: (0, 0, kt[s]))],
        out_specs=[pl.BlockSpec((tq, N_HEADS * HEAD_DIM), lambda s, qt, kt: (qt[s], 0)),
                   pl.BlockSpec((N_HEADS, 1, tq), lambda s, qt, kt: (0, 0, qt[s]))],
        scratch_shapes=[pltpu.VMEM((N_HEADS, 1, tq), F32), pltpu.VMEM((N_HEADS, 1, tq), F32),
                        pltpu.VMEM((N_HEADS, HEAD_DIM, tq), F32)],
        params=_params(48, ("arbitrary",)), exchange=exchange,
        first=lambda qt, kt: pl.program_id(0) == 0, last=lambda qt, kt: pl.program_id(0) == n_steps - 1)


BWD_HEADS = 4


def _attention_backward(q, k, k_t, v, d_cat, lse, delta, exchange=None):
    t_len = q.shape[1]
    tq = min(512, t_len)
    nq = t_len // tq
    hp = BWD_HEADS
    qi_tab, ki_tab = _triangle_steps(nq, False)

    def body(qi_ref, ki_ref, q_ref, k_ref, kt_ref, v_ref, do_ref, lse_ref, delta_ref, dqt_hbm, dk_ref, dv_ref,
             dqt_s, dk_s, dv_s):
        group, step = pl.program_id(0), pl.program_id(1)
        qi, ki = qi_ref[step], ki_ref[step]

        @pl.when(step == 0)
        def _():
            dqt_s[...] = jnp.zeros_like(dqt_s)

        @pl.when(qi == ki)
        def _():
            dk_s[...] = jnp.zeros_like(dk_s)
            dv_s[...] = jnp.zeros_like(dv_s)

        def accumulate(masked):
            for h in range(hp):
                do_b = do_ref[:, h * HEAD_DIM:(h + 1) * HEAD_DIM].astype(BF16)
                s_t = _dot_nt(k_ref[h], q_ref[h]) * ATT_LOG2
                if masked:
                    s_t = jnp.where(_key_le_query(tq), s_t, NEG_BIG)
                p_t = jnp.exp2(s_t - lse_ref[h])
                dp_t = _dot_nt(v_ref[h], do_b)
                ds_t = (p_t * (dp_t - delta_ref[h]) * ATT_SCALE).astype(BF16)
                dv_s[h] += _dot(p_t.astype(BF16), do_b)
                dk_s[h] += _dot(ds_t, q_ref[h])
                dqt_s[h, qi] += _dot(kt_ref[h], ds_t)

        @pl.when(ki < qi)
        def _():
            accumulate(False)

        @pl.when(ki == qi)
        def _():
            accumulate(True)
            for h in range(hp):
                pltpu.sync_copy(dqt_s.at[h, qi], dqt_hbm.at[group * hp + h, qi])

        @pl.when(qi == nq - 1)
        def _():
            dk_ref[...] = dk_s[...]
            dv_ref[...] = dv_s[...]

    wide = hp * HEAD_DIM
    n_groups, n_steps = N_HEADS // hp, qi_tab.shape[0]
    return _pallas(
        body, name="attention_backward", grid=(n_groups, n_steps), prefetch=(qi_tab, ki_tab),
        operands=(q, k, k_t, v, d_cat, lse, delta),
        out_shape=[jax.ShapeDtypeStruct((N_HEADS, nq, QK_DIM, tq), F32),
                   jax.ShapeDtypeStruct((N_HEADS, t_len, QK_DIM), F32),
                   jax.ShapeDtypeStruct((N_HEADS, t_len, HEAD_DIM), F32)],
        in_specs=[pl.BlockSpec((hp, tq, QK_DIM), lambda g, s, qt, kt: (g, qt[s], 0)),
                  pl.BlockSpec((hp, tq, QK_DIM), lambda g, s, qt, kt: (g, kt[s], 0)),
                  pl.BlockSpec((hp, QK_DIM, tq), lambda g, s, qt, kt: (g, 0, kt[s])),
                  pl.BlockSpec((hp, tq, HEAD_DIM), lambda g, s, qt, kt: (g, kt[s], 0)),
                  pl.BlockSpec((tq, wide), lambda g, s, qt, kt: (qt[s], n_groups + g)),
                  pl.BlockSpec((hp, 1, tq), lambda g, s, qt, kt: (g, 0, qt[s])),
                  pl.BlockSpec((hp, 1, tq), lambda g, s, qt, kt: (g, 0, qt[s]))],
        out_specs=[pl.BlockSpec(memory_space=pl.ANY),
                   pl.BlockSpec((hp, tq, QK_DIM), lambda g, s, qt, kt: (g, kt[s], 0)),
                   pl.BlockSpec((hp, tq, HEAD_DIM), lambda g, s, qt, kt: (g, kt[s], 0))],
        scratch_shapes=[pltpu.VMEM((hp, nq, QK_DIM, tq), F32), pltpu.VMEM((hp, tq, QK_DIM), F32),
                        pltpu.VMEM((hp, tq, HEAD_DIM), F32)],
        params=_params(58, ("arbitrary", "arbitrary")), exchange=exchange,
        first=lambda qt, kt: (pl.program_id(0) == 0) & (pl.program_id(1) == 0),
        last=lambda qt, kt: (pl.program_id(0) == n_groups - 1) & (pl.program_id(1) == n_steps - 1))


def _out_project(o_hg, o_mla, x, g_a, w_out, exchange=None):
    t_len = x.shape[0]
    tm = min(512, t_len)
    half = N_HEADS * HEAD_DIM

    def body(ohg_ref, omla_ref, x_ref, ga_ref, w_ref, cat_ref, mix_ref, xhat_ref, rstd_ref):
        a = ohg_ref[...].astype(BF16)
        b = omla_ref[...].astype(BF16)
        cat_ref[:, 0:half] = a
        cat_ref[:, half:2 * half] = b
        mix = _dot(a, w_ref[0:half, :]) + _dot(b, w_ref[half:2 * half, :])
        mix_ref[...] = mix
        r1 = DN_ALPHA * x_ref[...] + (1.0 + ga_ref[...]) * mix
        xc = r1 - _rowmean(r1)
        rstd = lax.rsqrt(_rowmean(xc * xc) + LN_EPS)
        xhat_ref[...] = xc * rstd
        rstd_ref[...] = rstd

    row = lambda i: (i, 0)
    fixed = lambda i: (0, 0)
    n_tiles = t_len // tm
    return _pallas(
        body, name="out_project", grid=(n_tiles,), operands=(o_hg, o_mla, x, g_a, w_out),
        out_shape=[jax.ShapeDtypeStruct((t_len, D_MODEL), BF16), jax.ShapeDtypeStruct((t_len, D_MODEL), F32),
                   jax.ShapeDtypeStruct((t_len, D_MODEL), F32), jax.ShapeDtypeStruct((t_len, 1), F32)],
        in_specs=[pl.BlockSpec((tm, half), row), pl.BlockSpec((tm, half), row), pl.BlockSpec((tm, D_MODEL), row),
                  pl.BlockSpec((1, D_MODEL), fixed), pl.BlockSpec((D_MODEL, D_MODEL), fixed)],
        out_specs=[pl.BlockSpec((tm, D_MODEL), row), pl.BlockSpec((tm, D_MODEL), row),
                   pl.BlockSpec((tm, D_MODEL), row), pl.BlockSpec((tm, 1), row)],
        params=_params(48, ("arbitrary",)), exchange=exchange,
        first=lambda: pl.program_id(0) == 0, last=lambda: pl.program_id(0) == n_tiles - 1)


V_LN1G, V_LN1B, V_SCM, V_SHM, V_GM, V_GA, V_LN2G, V_LN2B = range(8)
S_DLN2G, S_DLN2B, S_DGM, S_DSCM, S_DSHM, S_DLN1G, S_DLN1B, S_DGA, S_LOSS = range(9)


def _mlp_and_back(xhat1, rstd1, mix, target, o_mla, vecs, w1_top, w1_bottom, w2, w_out):
    t_len = xhat1.shape[0]
    tm = min(256, t_len)
    n_ff = w1_top.shape[0]
    ff = w1_top.shape[2]
    top_rows = w1_top.shape[1]

    def body(xhat_ref, rstd_ref, mix_ref, tgt_ref, omla_ref, vec_ref, w1_top_hbm, w1_bottom_hbm, w2_hbm, wout_hbm,
             act_ref, dhp_ref, um_ref, dh_ref, dmix_ref, dcat_ref, dr1_ref, sums_ref, delta_ref,
             w1_s, w2_s, wout_s, hp_s, load_sems):
        @pl.when(pl.program_id(0) == 0)
        def _():
            loads = [pltpu.make_async_copy(w1_top_hbm, w1_s.at[:, 0:top_rows], load_sems.at[0]),
                     pltpu.make_async_copy(w1_bottom_hbm, w1_s.at[:, top_rows:D_MODEL], load_sems.at[3]),
                     pltpu.make_async_copy(w2_hbm, w2_s, load_sems.at[1]),
                     pltpu.make_async_copy(wout_hbm, wout_s, load_sems.at[2])]
            for cp in loads:
                cp.start()
            sums_ref[...] = jnp.zeros_like(sums_ref)
            for cp in loads:
                cp.wait()

        vec = lambda r: vec_ref[r:r + 1, :]
        xhat = xhat_ref[...]
        x1 = xhat * vec(V_LN1G) + vec(V_LN1B)
        um = (x1 * (1.0 + vec(V_SCM)) + vec(V_SHM)).astype(BF16)
        um_ref[...] = um
        h = jnp.zeros((tm, D_MODEL), F32)
        for j in range(n_ff):
            hp = _dot(um, w1_s[j])
            hp_s[j] = hp
            act = jnp.square(jnp.maximum(hp, 0.0)).astype(BF16)
            act_ref[:, j * ff:(j + 1) * ff] = act
            h = h + _dot(act, w2_s[j])
        r2 = DN_ALPHA * x1 + (1.0 + vec(V_GM)) * h
        xc = r2 - _rowmean(r2)
        rstd2 = lax.rsqrt(_rowmean(xc * xc) + LN_EPS)
        xhat2 = xc * rstd2
        err = xhat2 * vec(V_LN2G) + vec(V_LN2B) - tgt_ref[...]
        loss = 0.5 * jnp.sum(_rowmean(err * err))
        dy = err * (1.0 / D_MODEL)
        dxh = dy * vec(V_LN2G)
        dr2 = rstd2 * (dxh - _rowmean(dxh) - xhat2 * _rowmean(dxh * xhat2))
        dh = ((1.0 + vec(V_GM)) * dr2).astype(BF16)
        dh_ref[...] = dh
        sums_ref[S_DLN2G:S_DLN2G + 1, :] += _colsum(dy * xhat2)
        sums_ref[S_DLN2B:S_DLN2B + 1, :] += _colsum(dy)
        sums_ref[S_DGM:S_DGM + 1, :] += _colsum(dr2 * h)
        sums_ref[S_LOSS:S_LOSS + 1, :] += jnp.full((1, D_MODEL), loss, F32)
        du = jnp.zeros((tm, D_MODEL), F32)
        for j in range(n_ff):
            dhp = (_dot_nt(dh, w2_s[j]) * (2.0 * jnp.maximum(hp_s[j], 0.0))).astype(BF16)
            dhp_ref[:, j * ff:(j + 1) * ff] = dhp
            du = du + _dot_nt(dhp, w1_s[j])
        sums_ref[S_DSCM:S_DSCM + 1, :] += _colsum(du * x1)
        sums_ref[S_DSHM:S_DSHM + 1, :] += _colsum(du)
        dx1 = DN_ALPHA * dr2 + du * (1.0 + vec(V_SCM))
        sums_ref[S_DLN1G:S_DLN1G + 1, :] += _colsum(dx1 * xhat)
        sums_ref[S_DLN1B:S_DLN1B + 1, :] += _colsum(dx1)
        dxh1 = dx1 * vec(V_LN1G)
        dr1 = rstd_ref[...] * (dxh1 - _rowmean(dxh1) - xhat * _rowmean(dxh1 * xhat))
        dr1_ref[...] = dr1
        sums_ref[S_DGA:S_DGA + 1, :] += _colsum(dr1 * mix_ref[...])
        dmix = ((1.0 + vec(V_GA)) * dr1).astype(BF16)
        dmix_ref[...] = dmix
        dcat = _dot_nt(dmix, wout_s[...])
        dcat_ref[...] = dcat
        half = N_HEADS * HEAD_DIM
        for hd in range(N_HEADS):
            prod = dcat[:, half + hd * HEAD_DIM:half + (hd + 1) * HEAD_DIM] * omla_ref[:, hd * HEAD_DIM:(hd + 1) * HEAD_DIM]
            sums = jnp.broadcast_to(jnp.sum(prod, axis=1, keepdims=True), (tm, HEAD_DIM))
            delta_ref[hd] = jnp.transpose(sums)[0:1]

    row = lambda i: (i, 0)
    fixed = lambda i: (0, 0)
    any_spec = pl.BlockSpec(memory_space=pl.ANY)
    return _pcall(
        body, name="mlp_and_back", grid=(t_len // tm,),
        out_shape=[jax.ShapeDtypeStruct((t_len, D_FF), BF16), jax.ShapeDtypeStruct((t_len, D_FF), BF16),
                   jax.ShapeDtypeStruct((t_len, D_MODEL), BF16), jax.ShapeDtypeStruct((t_len, D_MODEL), BF16),
                   jax.ShapeDtypeStruct((t_len, D_MODEL), BF16), jax.ShapeDtypeStruct((t_len, D_MODEL), F32),
                   jax.ShapeDtypeStruct((t_len, D_MODEL), F32), jax.ShapeDtypeStruct((16, D_MODEL), F32),
                   jax.ShapeDtypeStruct((N_HEADS, 1, t_len), F32)],
        in_specs=[pl.BlockSpec((tm, D_MODEL), row), pl.BlockSpec((tm, 1), row), pl.BlockSpec((tm, D_MODEL), row),
                  pl.BlockSpec((tm, D_MODEL), row), pl.BlockSpec((tm, N_HEADS * HEAD_DIM), row),
                  pl.BlockSpec((8, D_MODEL), fixed), any_spec, any_spec, any_spec, any_spec],
        out_specs=[pl.BlockSpec((tm, D_FF), row), pl.BlockSpec((tm, D_FF), row), pl.BlockSpec((tm, D_MODEL), row),
                   pl.BlockSpec((tm, D_MODEL), row), pl.BlockSpec((tm, D_MODEL), row), pl.BlockSpec((tm, D_MODEL), row),
                   pl.BlockSpec((tm, D_MODEL), row), pl.BlockSpec((16, D_MODEL), fixed),
                   pl.BlockSpec((N_HEADS, 1, tm), lambda i: (0, 0, i))],
        scratch_shapes=[pltpu.VMEM((n_ff, D_MODEL, ff), BF16), pltpu.VMEM(w2.shape, BF16), pltpu.VMEM(w_out.shape, BF16),
                        pltpu.VMEM((n_ff, tm, ff), F32), pltpu.SemaphoreType.DMA((4,))],
        compiler_params=_params(56, ("arbitrary",)),
        operands=(xhat1, rstd1, mix, target, o_mla, vecs, w1_top, w1_bottom, w2, w_out))


def _in_project_backward(dq, dk, dv, cq, ckv, pos, invf, q_norm_w, kv_norm_w, w_q, w_kv,
                         d_hq, d_hf, d_hi, d_hg, w_in, dr1, x, sc_a, exchange=None):
    t_len = x.shape[0]
    tm = min(512, t_len)
    per_q = dq.shape[3] // tm
    hgw = N_HEADS * HEAD_DIM

    def body(dq_ref, dk_ref, dv_ref, cq_ref, ckv_ref, pos_ref, invf_ref, qn_ref, kvn_ref, wq_ref, wkv_ref,
             dhq_ref, dhf_ref, dhi_ref, dhg_ref, win_ref, dr1_ref, x_ref, sc_ref,
             dz_ref, gx_ref, sums_ref, dwq_ref, dwkv_ref):
        @pl.when(pl.program_id(0) == 0)
        def _():
            sums_ref[...] = jnp.zeros_like(sums_ref)
            dwq_ref[...] = jnp.zeros_like(dwq_ref)
            dwkv_ref[...] = jnp.zeros_like(dwkv_ref)

        cos_t, sin_t = _rope_tables(pos_ref[...], invf_ref[...])
        cq = cq_ref[...]
        ckv = ckv_ref[...]
        rq = lax.rsqrt(_rowmean(cq * cq) + RMS_EPS)
        rkv = lax.rsqrt(_rowmean(ckv * ckv) + RMS_EPS)
        cqn = (cq * rq * qn_ref[...]).astype(BF16)
        ckvn = (ckv * rkv * kvn_ref[...]).astype(BF16)
        d_cqn = jnp.zeros((tm, Q_RANK), F32)
        d_ckvn = jnp.zeros((tm, KV_RANK), F32)
        d_kpe = jnp.zeros((tm, 128), F32)
        for h in range(N_HEADS):
            dqh = jnp.transpose(dq_ref[h])
            dq_full = jnp.concatenate(
                [dqh[:, :HEAD_DIM].astype(BF16), _unrope(dqh[:, HEAD_DIM:], cos_t, sin_t).astype(BF16)], axis=1)
            d_cqn = d_cqn + _dot_nt(dq_full, wq_ref[h])
            dwq_ref[h] += _dot_tn(cqn, dq_full)
            dkh = dk_ref[h]
            d_kpe = d_kpe + dkh[:, HEAD_DIM:]
            dkv_up = jnp.concatenate([dkh[:, :HEAD_DIM].astype(BF16), dv_ref[h].astype(BF16)], axis=1)
            d_ckvn = d_ckvn + _dot_nt(dkv_up, wkv_ref[h])
            dwkv_ref[h] += _dot_tn(ckvn, dkv_up)
        dyq = d_cqn * qn_ref[...]
        dykv = d_ckvn * kvn_ref[...]
        sums_ref[2:3, 0:Q_RANK] += _colsum(d_cqn * cq * rq)
        sums_ref[3:4, 0:KV_RANK] += _colsum(d_ckvn * ckv * rkv)
        dz_ref[:, 0:hgw] = dhq_ref[...]
        dz_ref[:, hgw:2 * hgw] = dhf_ref[...]
        dz_ref[:, 2 * hgw:3 * hgw] = dhi_ref[...]
        dz_ref[:, 3 * hgw:4 * hgw] = dhg_ref[...]
        dz_ref[:, HG_COLS:HG_COLS + Q_RANK] = (rq * dyq - cq * (rq * rq * rq) * _rowmean(dyq * cq)).astype(BF16)
        dz_ref[:, HG_COLS + Q_RANK:HG_COLS + Q_RANK + KV_RANK] = (
            rkv * dykv - ckv * (rkv * rkv * rkv) * _rowmean(dykv * ckv)).astype(BF16)
        dz_ref[:, HG_COLS + Q_RANK + KV_RANK:] = _unrope(d_kpe, cos_t, sin_t).astype(BF16)
        du = _dot(dz_ref[...], win_ref[...])
        xv = x_ref[...]
        gx_ref[...] = DN_ALPHA * dr1_ref[...] + (1.0 + sc_ref[...]) * du
        sums_ref[0:1, :] += _colsum(du * xv)
        sums_ref[1:2, :] += _colsum(du)

    row = lambda i: (i, 0)
    fixed2 = lambda i: (0, 0)
    fixed3 = lambda i: (0, 0, 0)
    heads = lambda i: (0, i, 0)
    n_tiles = t_len // tm
    return _pallas(
        body, name="in_project_backward", grid=(n_tiles,),
        operands=(dq, dk, dv, cq, ckv, pos, invf, q_norm_w, kv_norm_w, w_q, w_kv, d_hq, d_hf, d_hi, d_hg, w_in, dr1, x,
                  sc_a),
        out_shape=[jax.ShapeDtypeStruct((t_len, IN_COLS_PAD), BF16), jax.ShapeDtypeStruct((t_len, D_MODEL), F32),
                   jax.ShapeDtypeStruct((8, D_MODEL), F32), jax.ShapeDtypeStruct((N_HEADS, Q_RANK, QK_DIM), F32),
                   jax.ShapeDtypeStruct((N_HEADS, KV_RANK, 2 * HEAD_DIM), F32)],
        in_specs=[pl.BlockSpec((N_HEADS, None, QK_DIM, tm), lambda i: (0, i // per_q, 0, i % per_q)),
                  pl.BlockSpec((N_HEADS, tm, QK_DIM), heads),
                  pl.BlockSpec((N_HEADS, tm, HEAD_DIM), heads), pl.BlockSpec((tm, Q_RANK), row),
                  pl.BlockSpec((tm, KV_RANK), row), pl.BlockSpec((tm, 1), row), pl.BlockSpec((1, 128), fixed2),
                  pl.BlockSpec((1, Q_RANK), fixed2), pl.BlockSpec((1, KV_RANK), fixed2),
                  pl.BlockSpec((N_HEADS, Q_RANK, QK_DIM), fixed3), pl.BlockSpec((N_HEADS, KV_RANK, 2 * HEAD_DIM), fixed3),
                  pl.BlockSpec((tm, hgw), row), pl.BlockSpec((tm, hgw), row), pl.BlockSpec((tm, hgw), row),
                  pl.BlockSpec((tm, hgw), row), pl.BlockSpec((IN_COLS_PAD, D_MODEL), fixed2),
                  pl.BlockSpec((tm, D_MODEL), row), pl.BlockSpec((tm, D_MODEL), row), pl.BlockSpec((1, D_MODEL), fixed2)],
        out_specs=[pl.BlockSpec((tm, IN_COLS_PAD), row), pl.BlockSpec((tm, D_MODEL), row),
                   pl.BlockSpec((8, D_MODEL), fixed2), pl.BlockSpec((N_HEADS, Q_RANK, QK_DIM), fixed3),
                   pl.BlockSpec((N_HEADS, KV_RANK, 2 * HEAD_DIM), fixed3)],
        params=_params(48, ("arbitrary",)), exchange=exchange,
        first=lambda: pl.program_id(0) == 0, last=lambda: pl.program_id(0) == n_tiles - 1)


def _weight_grad(a, b, name, n_blocks, bn, a_blocked=False, b_blocked=True, exchange=None, token_tile=512):
    t_len = a.shape[0]
    m = a.shape[1] // n_blocks if a_blocked else a.shape[1]
    bt = min(token_tile, t_len)

    def body(a_ref, b_ref, o_ref):
        @pl.when(pl.program_id(1) == 0)
        def _():
            o_ref[...] = jnp.zeros_like(o_ref)

        o_ref[...] += _dot_tn(a_ref[...].astype(BF16), b_ref[...].astype(BF16))

    a_spec = pl.BlockSpec((bt, m), (lambda n, t: (t, n)) if a_blocked else (lambda n, t: (t, 0)))
    b_spec = pl.BlockSpec((bt, bn), (lambda n, t: (t, n)) if b_blocked else (lambda n, t: (t, 0)))
    nt = t_len // bt
    (out,), landed = _pallas(
        body, name=name, grid=(n_blocks, nt), operands=(a, b),
        out_shape=[jax.ShapeDtypeStruct((n_blocks, m, bn), F32)],
        in_specs=[a_spec, b_spec],
        out_specs=[pl.BlockSpec((None, m, bn), lambda n, t: (n, 0, 0))],
        params=_params(56, ("arbitrary", "arbitrary")), exchange=exchange,
        first=lambda: (pl.program_id(0) == 0) & (pl.program_id(1) == 0),
        last=lambda: (pl.program_id(0) == n_blocks - 1) & (pl.program_id(1) == nt - 1))
    return (out, landed) if exchange else out


SMALL_PLACE = {"ln1_g": (6, 0), "ln1_b": (7, 0), "ln2_g": (8, 0), "ln2_b": (9, 0), "hg_norm_w": (10, 512),
               "mla_q_norm_w": (11, 0), "mla_kv_norm_w": (11, Q_RANK)}
SMALL_LB_ROW, SMALL_LOSS_ROW = 10, 12


def _small_params_step(gathered, params):
    names = list(params)

    def body(g_ref, *refs):
        ins, outs = refs[:3 * len(names)], refs[3 * len(names):]
        loss_ref, outs = outs[0], outs[1:]
        tot = g_ref[0]
        for d in range(1, N_DEV):
            tot = tot + g_ref[d]
        loss_ref[...] = tot[SMALL_LOSS_ROW:SMALL_LOSS_ROW + 1, 0:128]

        def update(i, grad, rows=slice(None), lanes=slice(None)):
            w_ref, m_ref, v_ref = ins[3 * i:3 * i + 3]
            g_out, d_out, nm_out, nv_out = outs[4 * i:4 * i + 4]
            g_out[rows, lanes] = grad
            d_out[rows, lanes], nm_out[rows, lanes], nv_out[rows, lanes] = _adamw_update(
                w_ref[rows, lanes], grad, m_ref[rows, lanes], v_ref[rows, lanes])

        for i, name in enumerate(names):
            if name == "b_ada":
                for r in range(6):
                    update(i, tot[r:r + 1, :], lanes=slice(r * D_MODEL, (r + 1) * D_MODEL))
            elif name == "hg_lower_bounds":
                lb = _lower_bound(ins[3 * i][...])
                d0 = tot[SMALL_LB_ROW:SMALL_LB_ROW + 1, 0:512] * lb * (1.0 - lb)
                update(i, d0, rows=slice(0, 1))
                update(i, -d0, rows=slice(1, 2))
            else:
                row, lane = SMALL_PLACE[name]
                update(i, tot[row:row + 1, lane:lane + params[name][0].shape[1]])

    flat_in = [a for name in names for a in params[name]]
    shapes = [jax.ShapeDtypeStruct((1, 128), F32)] + [jax.ShapeDtypeStruct(params[name][0].shape, F32)
                                                      for name in names for _ in range(4)]
    out = pl.pallas_call(body, name="small_params_step", out_shape=shapes)(gathered, *flat_in)
    return out[0], {name: out[1 + 4 * i:5 + 4 * i] for i, name in enumerate(names)}


def _adamw_update(w, gv, m, v):
    nm = ADAM_B1 * m + (1.0 - ADAM_B1) * gv
    nv = ADAM_B2 * v + (1.0 - ADAM_B2) * jnp.square(gv)
    m_hat = nm / (1.0 - ADAM_B1 ** ADAM_STEP)
    v_hat = nv / (1.0 - ADAM_B2 ** ADAM_STEP)
    return -ADAM_LR * (m_hat / (jnp.sqrt(v_hat) + ADAM_EPS) + ADAM_WD * w), nm, nv


def _adamw_halves(core, w, mine, theirs, m, v, name):
    rows, cols = w.shape
    h = rows // 2
    tr = _row_tile(h)
    per_half = h // tr

    def body(core_ref, w_ref, mine_ref, theirs_ref, m_ref, v_ref, g_ref, d_ref, nm_ref, nv_ref):
        is_mine = pl.program_id(0) // per_half == core_ref[0]
        gv = jnp.where(is_mine, mine_ref[...], theirs_ref[...])
        g_ref[...] = gv
        d_ref[...], nm_ref[...], nv_ref[...] = _adamw_update(w_ref[...], gv, m_ref[...], v_ref[...])

    full = pl.BlockSpec((tr, cols), lambda i, core_ref: (i, 0))
    part = pl.BlockSpec((tr, cols), lambda i, core_ref: (i % per_half, 0))
    return _pcall(
        body, name=name, out_shape=[jax.ShapeDtypeStruct(w.shape, F32)] * 4,
        grid_spec=pltpu.PrefetchScalarGridSpec(
            num_scalar_prefetch=1, grid=(rows // tr,), in_specs=[full, part, part, full, full], out_specs=[full] * 4),
        compiler_params=_params(40, ("arbitrary",)),
        operands=(core, w, mine, theirs, m, v))


def _adamw(w, g, m, v, name):
    rows, cols = w.shape
    tr = _row_tile(rows) if rows >= 8 else rows

    def body(w_ref, g_ref, m_ref, v_ref, d_ref, nm_ref, nv_ref):
        d_ref[...], nm_ref[...], nv_ref[...] = _adamw_update(w_ref[...], g_ref[...], m_ref[...], v_ref[...])

    spec = pl.BlockSpec((tr, cols), lambda i: (i, 0))
    return _pcall(
        body, name=name, grid=(rows // tr,),
        out_shape=[jax.ShapeDtypeStruct(w.shape, F32)] * 3,
        in_specs=[spec] * 4, out_specs=[spec] * 3,
        compiler_params=_params(40, ("arbitrary",)),
        operands=(w, g, m, v))


def kernel(x, c, positions, w_ada, b_ada, w_in, hg_lower_bounds, hg_norm_w, mla_q_norm_w, w_q_up, mla_kv_norm_w, w_kv_up, w_out, ln1_g, ln1_b, w_mlp_in, w_mlp_out, ln2_g, ln2_b, loss_target, m_w_ada, m_b_ada, m_w_in, m_hg_lower_bounds, m_hg_norm_w, m_mla_q_norm_w, m_w_q_up, m_mla_kv_norm_w, m_w_kv_up, m_w_out, m_ln1_g, m_ln1_b, m_w_mlp_in, m_w_mlp_out, m_ln2_g, m_ln2_b, v_w_ada, v_b_ada, v_w_in, v_hg_lower_bounds, v_hg_norm_w, v_mla_q_norm_w, v_w_q_up, v_mla_kv_norm_w, v_w_kv_up, v_w_out, v_ln1_g, v_ln1_b, v_w_mlp_in, v_w_mlp_out, v_ln2_g, v_ln2_b):
    ix, iy, ic = _mesh_pos()
    chip = 2 * ix + iy
    me = 4 * ix + 2 * iy + ic
    core_arr = jnp.reshape(ic, (1,)).astype(jnp.int32)
    chip_arr = jnp.reshape(chip, (1,)).astype(jnp.int32)

    xs = x[0]
    target = loss_target[0]
    t_len = xs.shape[0]
    pos = positions.astype(F32).reshape(t_len, 1)
    inv = 1.0 / (ROPE_THETA ** (jnp.arange(0, ROPE_DIM, 2, dtype=F32) / ROPE_DIM))
    invf = jnp.concatenate([inv, inv, jnp.zeros((128 - ROPE_DIM,), F32)]).reshape(1, 128)

    def slot(w):
        rows, cols = w.shape
        own = w.astype(BF16).reshape(1, 2, rows // 2, cols)
        return lax.dynamic_update_slice(jnp.zeros((N_CHIPS, 2, rows // 2, cols), BF16), own, (chip, 0, 0, 0))

    def slot8(a):
        return lax.dynamic_update_slice(jnp.zeros((N_DEV,) + a.shape, a.dtype), a[None], (me, 0, 0))

    def whole(s):
        return s.reshape(N_CHIPS, 2 * s.shape[2], s.shape[3])

    def halved(g):
        return g.reshape(N_CHIPS, 2, g.shape[1] // 2, g.shape[2])

    ada_cols = w_ada.shape[2]
    c_all, *early = _run_exchange(
        _merge(_gather_all(slot8(jnp.broadcast_to(c, (8, D_MODEL)))),
               _gather_over_ici([slot(jnp.transpose(w_in[0])), slot(w_q_up[0]), slot(w_kv_up[0])])),
        "gather_c_and_mixer_weights_ici")
    b_shard = lax.dynamic_slice(b_ada, (0, chip * ada_cols), (1, ada_cols))
    mod_cols, cond16 = _ada_project(c_all[:, 0, :], w_ada[0], b_shard)
    mod_all, *early = _run_exchange(_merge(_gather_all(slot8(mod_cols)), _gather_over_d2d(early)),
                                    "gather_mod_and_mixer_weights_d2d")
    mod_mine = lax.dynamic_slice(mod_all, (0, me, 0), (N_DEV, 1, ada_cols))[::2, 0, :].reshape(6, D_MODEL)
    sh_a, sc_a, g_a, sh_m, sc_m, g_m = (mod_mine[i:i + 1] for i in range(6))
    g_in, g_q, g_kv = (whole(s) for s in early)
    w_in_full = jnp.pad(g_in.reshape(IN_COLS, D_MODEL), ((0, IN_COLS_PAD - IN_COLS), (0, 0)))
    w_q_full = jnp.pad(g_q, ((0, 0), (0, 0), (0, QK_DIM - g_q.shape[2])))

    w1_rows = D_MODEL // 2
    (u_a, zhg, cq, ckv, q, k, k_t, v, v_t), (s_top, s_out) = _in_project(
        xs, pos, sc_a, sh_a, w_in_full, mla_q_norm_w, mla_kv_norm_w, w_q_full, g_kv, invf,
        _gather_over_ici([slot(w_mlp_in[0, :w1_rows]), slot(w_out[0])]))
    (o_pre, o_hg, states), (s_bottom, s_top, s_out) = _hgrn_forward(
        zhg, hg_lower_bounds, hg_norm_w,
        _merge(_gather_over_ici([slot(w_mlp_in[0, w1_rows:])]), _gather_over_d2d([s_top, s_out])))
    (o_mla, lse), (s_w2, s_bottom) = _attention_forward(
        q, k, v_t, _merge(_gather_over_ici([slot(w_mlp_out[0])]), _gather_over_d2d([s_bottom])))
    w_out_full = whole(s_out).reshape(D_MODEL, D_MODEL)
    (cat, mix, xhat1, rstd1), (s_w2,) = _out_project(o_hg, o_mla, xs, g_a, w_out_full, _gather_over_d2d([s_w2]))
    g_w1_top, g_w1_bottom, g_w2 = whole(s_top), whole(s_bottom), whole(s_w2)
    vecs = jnp.concatenate([ln1_g, ln1_b, sc_m, sh_m, g_m, g_a, ln2_g, ln2_b], axis=0)
    act, dhp, um, dh, dmix, d_cat, dr1, mlp_sums, delta = _mlp_and_back(
        xhat1, rstd1, mix, target, o_mla, vecs, g_w1_top, g_w1_bottom, g_w2, w_out_full)

    gw_1 = halved(_weight_grad(um, dhp, "grad_w_mlp_in", N_CHIPS, D_FF // N_CHIPS, token_tile=4096))
    gw_2, (landed_1,) = _weight_grad(act, dh, "grad_w_mlp_out", N_CHIPS, D_MODEL, a_blocked=True, b_blocked=False,
                                     token_tile=4096, exchange=_pair_exchange([gw_1]))
    gw_out = _weight_grad(cat, dmix, "grad_w_out", 1, D_MODEL, token_tile=2048)
    later = [halved(gw_2), halved(gw_out.reshape(N_CHIPS, D_MODEL // N_CHIPS, D_MODEL))]
    own_1, travels_1 = _add_pair(core_arr, chip_arr, gw_1, landed_1)
    (dq, dk, dv), (landed_1, *landed) = _attention_backward(
        q, k, k_t, v, d_cat, lse, delta, _merge(_chip_exchange([travels_1]), _pair_exchange(later)))
    mine_1 = _add_chips(own_1, landed_1)
    chip_sums = [_add_pair(core_arr, chip_arr, g, l) for g, l in zip(later, landed)]
    (d_hq, d_hf, d_hi, d_hg, hg_sums), (theirs_1, *landed) = _hgrn_backward(
        zhg, hg_lower_bounds, hg_norm_w, o_pre, d_cat, states,
        _merge(_pair_send([mine_1]), _chip_exchange([b for _, b in chip_sums])))
    later_mine = [_add_chips(own, l) for (own, _), l in zip(chip_sums, landed)]
    mlp_mine = [mine_1] + later_mine
    (dz, grad_x, in_sums, gw_q, gw_kv), _ = _in_project_backward(
        dq, dk, dv, cq, ckv, pos, invf, mla_q_norm_w, mla_kv_norm_w, w_q_full, g_kv,
        d_hq, d_hf, d_hi, d_hg, w_in_full, dr1, xs, sc_a)

    zeros = lambda n: jnp.zeros((1, n), F32)
    small = jnp.concatenate([
        in_sums[1:2], in_sums[0:1], mlp_sums[S_DGA:S_DGA + 1],
        mlp_sums[S_DSHM:S_DSHM + 1], mlp_sums[S_DSCM:S_DSCM + 1], mlp_sums[S_DGM:S_DGM + 1],
        mlp_sums[S_DLN1G:S_DLN1G + 1], mlp_sums[S_DLN1B:S_DLN1B + 1],
        mlp_sums[S_DLN2G:S_DLN2G + 1], mlp_sums[S_DLN2B:S_DLN2B + 1],
        jnp.concatenate([hg_sums[0:1], hg_sums[1:2]], axis=1),
        jnp.concatenate([in_sums[2:3, :Q_RANK], in_sums[3:4, :KV_RANK], zeros(D_MODEL - Q_RANK - KV_RANK)], axis=1),
        mlp_sums[S_LOSS:S_LOSS + 1],
        jnp.zeros((SMALL_ROWS - 13, D_MODEL), F32)], axis=0)

    gw_in, (*later_theirs, small_all) = _weight_grad(
        dz, u_a, "grad_w_in", 3, D_MODEL, a_blocked=True, b_blocked=False, token_tile=4096,
        exchange=_merge(_pair_send(later_mine), _gather_all(slot8(small))))
    mlp_theirs = [theirs_1] + list(later_theirs)
    gw_in = gw_in.reshape(IN_COLS_PAD, D_MODEL)
    gw_q = gw_q[:, :, :HEAD_DIM + ROPE_DIM]
    flat = lambda g: g.reshape(g.shape[0] * g.shape[1], g.shape[2])
    mixer_mine, mixer_theirs = _reduce_in_vmem(
        [gw_in, flat(gw_q), flat(gw_kv)], [IN_COLS // N_CHIPS // 2, Q_RANK // 2, KV_RANK // 2], "reduce_mixer_grads")
    reduced = ("w_in", "w_q_up", "w_kv_up", "w_mlp_in", "w_mlp_out", "w_out")
    halves_mine = dict(zip(reduced, list(mixer_mine) + mlp_mine))
    halves_theirs = dict(zip(reduced, list(mixer_theirs) + list(mlp_theirs)))

    small_names = ("b_ada", "hg_lower_bounds", "hg_norm_w", "mla_q_norm_w", "mla_kv_norm_w",
                   "ln1_g", "ln1_b", "ln2_g", "ln2_b")
    loss_row, small_out = _small_params_step(small_all, {
        "b_ada": (b_ada, m_b_ada, v_b_ada),
        "hg_lower_bounds": (hg_lower_bounds, m_hg_lower_bounds, v_hg_lower_bounds),
        "hg_norm_w": (hg_norm_w, m_hg_norm_w, v_hg_norm_w),
        "mla_q_norm_w": (mla_q_norm_w, m_mla_q_norm_w, v_mla_q_norm_w),
        "mla_kv_norm_w": (mla_kv_norm_w, m_mla_kv_norm_w, v_mla_kv_norm_w),
        "ln1_g": (ln1_g, m_ln1_g, v_ln1_g), "ln1_b": (ln1_b, m_ln1_b, v_ln1_b),
        "ln2_g": (ln2_g, m_ln2_g, v_ln2_g), "ln2_b": (ln2_b, m_ln2_b, v_ln2_b)})
    loss = loss_row[0, 0]

    d_mod_all = small_all[:, 0:6, :].reshape(N_DEV, 6 * D_MODEL)
    d_mod_cols = lax.dynamic_slice(d_mod_all, (0, chip * ada_cols), (N_DEV, ada_cols))
    d_mod_cols = jnp.concatenate([d_mod_cols, jnp.zeros_like(d_mod_cols)], axis=0)
    g_w_ada = _weight_grad(cond16, d_mod_cols, "grad_w_ada", 1, ada_cols)[0]

    names = ["w_ada", "b_ada", "w_in", "hg_lower_bounds", "hg_norm_w", "mla_q_norm_w", "w_q_up", "mla_kv_norm_w",
             "w_kv_up", "w_out", "ln1_g", "ln1_b", "w_mlp_in", "w_mlp_out", "ln2_g", "ln2_b"]
    weights = [w_ada, b_ada, w_in, hg_lower_bounds, hg_norm_w, mla_q_norm_w, w_q_up, mla_kv_norm_w,
               w_kv_up, w_out, ln1_g, ln1_b, w_mlp_in, w_mlp_out, ln2_g, ln2_b]
    moms = [m_w_ada, m_b_ada, m_w_in, m_hg_lower_bounds, m_hg_norm_w, m_mla_q_norm_w, m_w_q_up, m_mla_kv_norm_w,
            m_w_kv_up, m_w_out, m_ln1_g, m_ln1_b, m_w_mlp_in, m_w_mlp_out, m_ln2_g, m_ln2_b]
    vels = [v_w_ada, v_b_ada, v_w_in, v_hg_lower_bounds, v_hg_norm_w, v_mla_q_norm_w, v_w_q_up, v_mla_kv_norm_w,
            v_w_kv_up, v_w_out, v_ln1_g, v_ln1_b, v_w_mlp_in, v_w_mlp_out, v_ln2_g, v_ln2_b]
    out_g, out_d, out_m, out_v = [], [], [], []
    for name, w, m, vv in zip(names, weights, moms, vels):
        if name in small_names:
            g, d, nm, nv = small_out[name]
            back = lambda a: a
        elif name == "w_in":
            to2d, back = (lambda a: jnp.transpose(a[0])), (lambda a: jnp.transpose(a)[None])
        else:
            to2d, back = (lambda a, s=w.shape[1:]: a.reshape(s)), (lambda a, s=w.shape: a.reshape(s))
        if name == "w_ada":
            d, nm, nv = _adamw(to2d(w), g_w_ada, to2d(m), to2d(vv), "adamw_" + name)
            g = g_w_ada
        elif name not in small_names:
            g, d, nm, nv = _adamw_halves(core_arr, to2d(w), halves_mine[name], halves_theirs[name], to2d(m), to2d(vv),
                                         "adamw_" + name)
        out_g.append(back(g))
        out_d.append(back(d))
        out_m.append(back(nm))
        out_v.append(back(nv))
    return (loss, grad_x[None], *out_g, *out_d, *out_m, *out_v)
```

```python
import functools

import jax
import jax.numpy as jnp
from jax import lax
from jax.experimental import pallas as pl
from jax.experimental.pallas import tpu as pltpu

F32 = jnp.float32
BF16 = jnp.bfloat16
MESH_IDS = pl.DeviceIdType.MESH

D_MODEL = 1024
N_HEADS = 4
HEAD_DIM = 128
ROPE_DIM = 64
HG_CHUNK = 64
HG_COLS = 2048
Q_RANK = 256
KV_RANK = 256
IN_COLS = 2624
IN_COLS_PAD = 2688
QK_DIM = 256
D_FF = 4096
N_CHIPS = 4
N_DEV = 8
ROPE_THETA = 10000.0
RMS_EPS = 1e-6
LN_EPS = 1e-5
DN_ALPHA = 2.0 ** 0.25
ATT_SCALE = (HEAD_DIM + ROPE_DIM) ** -0.5
NEG_BIG = -1e30
ADAM_LR = 0.001
ADAM_B1 = 0.9
ADAM_B2 = 0.999
ADAM_EPS = 1e-08
ADAM_WD = 0.01
ADAM_STEP = 10
SMALL_ROWS = 16
MIB = 1024 * 1024


def _dot(a, b):
    return jnp.dot(a, b, preferred_element_type=F32)


def _dot_nt(a, b):
    return lax.dot_general(a, b, (((1,), (1,)), ((), ())), preferred_element_type=F32)


def _dot_tn(a, b):
    return lax.dot_general(a, b, (((0,), (0,)), ((), ())), preferred_element_type=F32)


def _params(vmem_mib, semantics=None):
    return pltpu.CompilerParams(vmem_limit_bytes=vmem_mib * MIB, dimension_semantics=semantics)


def _sigmoid(v):
    return 1.0 / (1.0 + jnp.exp(-v))


def _colsum(v):
    return jnp.sum(v, axis=0, keepdims=True)


def _rowmean(v):
    return jnp.mean(v, axis=-1, keepdims=True)


def _rope_tables(pos, invf):
    ang = pos * invf
    lane = lax.broadcasted_iota(jnp.int32, ang.shape, 1)
    cos_t = jnp.where(lane < ROPE_DIM, jnp.cos(ang), 0.0)
    sin = jnp.sin(ang)
    sin_t = jnp.where(lane < ROPE_DIM // 2, -sin, jnp.where(lane < ROPE_DIM, sin, 0.0))
    return cos_t, sin_t


def _swap_halves(t):
    lane = lax.broadcasted_iota(jnp.int32, t.shape, 1)
    return jnp.where(lane < ROPE_DIM // 2, pltpu.roll(t, 128 - ROPE_DIM // 2, 1), pltpu.roll(t, ROPE_DIM // 2, 1))


def _rope(t, cos_t, sin_t):
    return t * cos_t + _swap_halves(t) * sin_t


def _unrope(g, cos_t, sin_t):
    return g * cos_t - _swap_halves(g) * sin_t


def _mesh_pos():
    return lax.axis_index("x"), lax.axis_index("y"), lax.axis_index("c")


def _other_chips(x, y):
    out = []
    for dx, dy in ((1, 0), (0, 1), (1, 1)):
        px = 1 - x if dx else x
        py = 1 - y if dy else y
        out.append(((px, py), 2 * px + py))
    return out


class _Exchange:
    def __init__(self, inputs, out_shapes, aliases, sems, start, finish):
        self.inputs, self.out_shapes, self.aliases, self.sems = list(inputs), list(out_shapes), dict(aliases), list(sems)
        self.start, self.finish = start, finish


def _from_copies(inputs, out_shapes, aliases, sems, copies):
    def start(ins, outs, sem_refs):
        for send, _ in copies(ins, outs, sem_refs):
            send.start()

    def finish(ins, outs, sem_refs):
        for send, recv in copies(ins, outs, sem_refs):
            recv.wait_recv()
            send.wait_send()

    return _Exchange(inputs, out_shapes, aliases, sems, start, finish)


HBM_MIN_BYTES = 256 * 1024


def _in_hbm(a):
    if a.size * a.dtype.itemsize < HBM_MIN_BYTES:
        return a
    return pltpu.with_memory_space_constraint(a, pltpu.HBM)


def _out_hbm(s):
    if s.size * s.dtype.itemsize < HBM_MIN_BYTES:
        return s
    return pltpu.HBM(s.shape, s.dtype)


def _pcall(body, *, operands, out_shape, **kwargs):
    single = not isinstance(out_shape, (list, tuple))
    shapes = [_out_hbm(s) for s in ([out_shape] if single else out_shape)]
    return pl.pallas_call(body, out_shape=shapes[0] if single else shapes, **kwargs)(*[_in_hbm(a) for a in operands])


def _run_exchange(exchange, name):
    n_in, n_out = len(exchange.inputs), len(exchange.out_shapes)

    def body(*refs):
        ins, outs, sem_refs = refs[:n_in], refs[n_in:n_in + n_out], refs[n_in + n_out:]
        exchange.start(ins, outs, sem_refs)
        exchange.finish(ins, outs, sem_refs)

    any_spec = pl.BlockSpec(memory_space=pl.ANY)
    return pl.pallas_call(
        body, name=name, out_shape=[_out_hbm(s) for s in exchange.out_shapes],
        in_specs=[any_spec] * n_in, out_specs=[any_spec] * n_out,
        scratch_shapes=exchange.sems, input_output_aliases=exchange.aliases,
    )(*[_in_hbm(a) for a in exchange.inputs])


def _pallas(body, *, name, operands, in_specs, out_shape, out_specs, params, scratch_shapes=(), grid=(), prefetch=(),
            exchange=None, first=None, last=None):
    n_pre, n_in, n_out, n_scr = len(prefetch), len(in_specs), len(out_specs), len(scratch_shapes)
    ex_in = exchange.inputs if exchange else []
    ex_out = exchange.out_shapes if exchange else []
    ex_sems = exchange.sems if exchange else []

    def full_body(*refs):
        pre, rest = refs[:n_pre], refs[n_pre:]
        ins, rest = rest[:n_in], rest[n_in:]
        xin, rest = rest[:len(ex_in)], rest[len(ex_in):]
        outs, rest = rest[:n_out], rest[n_out:]
        xout, rest = rest[:len(ex_out)], rest[len(ex_out):]
        scr, sem_refs = rest[:n_scr], rest[n_scr:]
        if exchange:
            @pl.when(first(*pre))
            def _():
                exchange.start(xin, xout, sem_refs)

        body(*pre, *ins, *outs, *scr)
        if exchange:
            @pl.when(last(*pre))
            def _():
                exchange.finish(xin, xout, sem_refs)

    any_spec = pl.BlockSpec(memory_space=pl.ANY)
    aliases = {n_pre + n_in + i: n_out + o for i, o in exchange.aliases.items()} if exchange else {}
    operands = [_in_hbm(a) for a in operands]
    results = pl.pallas_call(
        full_body, name=name, out_shape=[_out_hbm(s) for s in list(out_shape) + ex_out],
        grid_spec=pltpu.PrefetchScalarGridSpec(
            num_scalar_prefetch=n_pre, grid=grid, in_specs=list(in_specs) + [any_spec] * len(ex_in),
            out_specs=list(out_specs) + [any_spec] * len(ex_out), scratch_shapes=list(scratch_shapes) + ex_sems),
        input_output_aliases=aliases, compiler_params=params,
    )(*prefetch, *operands, *[_in_hbm(a) for a in ex_in])
    return results[:n_out], results[n_out:]


def _remote(src, dst, sems, idx, to):
    send_sems, recv_sems = sems
    return pltpu.make_async_remote_copy(src_ref=src, dst_ref=dst, send_sem=send_sems.at[idx], recv_sem=recv_sems.at[idx],
                                        device_id=to, device_id_type=MESH_IDS)


def _sem_pairs(*shape):
    return [pltpu.SemaphoreType.DMA(shape), pltpu.SemaphoreType.DMA(shape)]


def _same_shapes(arrays):
    return [jax.ShapeDtypeStruct(a.shape, a.dtype) for a in arrays]


def _gather_over_ici(slots):
    n = len(slots)

    def copies(ins, outs, sems):
        x, y, c = _mesh_pos()
        k = 2 * x + y
        out = []
        for j, (chip, kj) in enumerate(_other_chips(x, y)):
            for i in range(n):
                to = (*chip, c)
                out.append((_remote(ins[i].at[k, c], outs[i].at[k, c], sems, (j, i), to),
                            _remote(ins[i].at[k, c], outs[i].at[kj, c], sems, (j, i), to)))
        return out

    return _from_copies(slots, _same_shapes(slots), {i: i for i in range(n)}, _sem_pairs(3, n), copies)


def _gather_over_d2d(slots):
    n = len(slots)

    def copies(ins, outs, sems):
        x, y, c = _mesh_pos()
        sibling = (x, y, 1 - c)
        out = []
        for j, (_, kj) in enumerate(_other_chips(x, y)):
            for i in range(n):
                out.append((_remote(ins[i].at[kj, c], outs[i].at[kj, c], sems, (j, i), sibling),
                            _remote(ins[i].at[kj, c], outs[i].at[kj, 1 - c], sems, (j, i), sibling)))
        return out

    return _from_copies(slots, _same_shapes(slots), {i: i for i in range(n)}, _sem_pairs(3, n), copies)


def _gather_all(slots8):
    def copies(ins, outs, sems):
        x, y, c = _mesh_pos()
        me = 4 * x + 2 * y + c
        out = []
        for r in range(1, N_DEV):
            px = 1 - x if r & 4 else x
            py = 1 - y if r & 2 else y
            pc = 1 - c if r & 1 else c
            to = (px, py, pc)
            out.append((_remote(ins[0].at[me], outs[0].at[me], sems, r - 1, to),
                        _remote(ins[0].at[me], outs[0].at[4 * px + 2 * py + pc], sems, r - 1, to)))
        return out

    return _from_copies([slots8], _same_shapes([slots8]), {0: 0}, _sem_pairs(N_DEV - 1), copies)


def _merge(first, second):
    n_in, n_out, n_sem = len(first.inputs), len(first.out_shapes), len(first.sems)

    def start(ins, outs, sems):
        first.start(ins[:n_in], outs[:n_out], sems[:n_sem])
        second.start(ins[n_in:], outs[n_out:], sems[n_sem:])

    def finish(ins, outs, sems):
        first.finish(ins[:n_in], outs[:n_out], sems[:n_sem])
        second.finish(ins[n_in:], outs[n_out:], sems[n_sem:])

    aliases = dict(first.aliases)
    aliases.update({n_in + i: n_out + o for i, o in second.aliases.items()})
    return _Exchange(first.inputs + second.inputs, first.out_shapes + second.out_shapes, aliases,
                     first.sems + second.sems, start, finish)


def _pair_exchange(grads):
    n = len(grads)

    def copies(ins, outs, sems):
        x, y, c = _mesh_pos()
        cps = [_remote(ins[i].at[:, 1 - c], outs[i], sems, i, (x, y, 1 - c)) for i in range(n)]
        return [(cp, cp) for cp in cps]

    shapes = [jax.ShapeDtypeStruct((N_CHIPS,) + g.shape[2:], g.dtype) for g in grads]
    return _from_copies(grads, shapes, {}, _sem_pairs(n), copies)


def _chip_exchange(partials):
    n = len(partials)

    def copies(ins, outs, sems):
        x, y, c = _mesh_pos()
        cps = [_remote(ins[i].at[kj], outs[i].at[j], sems, (j, i), (*chip, c))
               for j, (chip, kj) in enumerate(_other_chips(x, y)) for i in range(n)]
        return [(cp, cp) for cp in cps]

    shapes = [jax.ShapeDtypeStruct((3,) + p.shape[1:], p.dtype) for p in partials]
    return _from_copies(partials, shapes, {}, _sem_pairs(3, n), copies)


def _pair_send(halves):
    n = len(halves)

    def copies(ins, outs, sems):
        x, y, c = _mesh_pos()
        cps = [_remote(ins[i], outs[i], sems, i, (x, y, 1 - c)) for i in range(n)]
        return [(cp, cp) for cp in cps]

    return _from_copies(halves, _same_shapes(halves), {}, _sem_pairs(n), copies)


def _reduce_in_vmem(grads, half_rows, name):
    n = len(grads)

    def body(*refs):
        g, mine, theirs = refs[:n], refs[n:2 * n], refs[2 * n:3 * n]
        landed_pair, partial, landed_chips = refs[3 * n:4 * n], refs[4 * n:5 * n], refs[5 * n:6 * n]
        sems = refs[6 * n:]
        x, y, c = _mesh_pos()
        k = 2 * x + y
        sibling = (x, y, 1 - c)

        def half(i, chip_idx, which):
            return pl.ds(pl.multiple_of((2 * chip_idx + which) * half_rows[i], 8), half_rows[i])

        def run(copies):
            for cp in copies:
                cp.start()
            for cp in copies:
                cp.wait_recv()
                cp.wait_send()

        run([_remote(g[i].at[half(i, kk, 1 - c)], landed_pair[i].at[kk], sems[0:2], (kk, i), sibling)
             for kk in range(N_CHIPS) for i in range(n)])
        for i in range(n):
            for kk in range(N_CHIPS):
                partial[i][kk] = (g[i][half(i, kk, c), :] + landed_pair[i][kk]).astype(BF16)
        run([_remote(partial[i].at[kj], landed_chips[i].at[j], sems[2:4], (j, i), (*chip, c))
             for j, (chip, kj) in enumerate(_other_chips(x, y)) for i in range(n)])
        for i in range(n):
            own = g[i][half(i, k, c), :] + landed_pair[i][k]
            mine[i][...] = ((own + landed_chips[i][0].astype(F32)) + landed_chips[i][1].astype(F32)) \
                + landed_chips[i][2].astype(F32)
        run([_remote(mine[i], theirs[i], sems[4:6], i, sibling) for i in range(n)])

    shapes = [(h, gr.shape[1]) for gr, h in zip(grads, half_rows)]
    halves = [jax.ShapeDtypeStruct(s, F32) for s in shapes]
    vmem = pl.BlockSpec(memory_space=pltpu.VMEM)
    scratch = ([pltpu.VMEM((N_CHIPS,) + s, F32) for s in shapes]
               + [pltpu.VMEM((N_CHIPS,) + s, BF16) for s in shapes]
               + [pltpu.VMEM((3,) + s, BF16) for s in shapes]
               + _sem_pairs(N_CHIPS, n) + _sem_pairs(3, n) + _sem_pairs(n))
    out = pl.pallas_call(
        body, name=name, out_shape=halves + halves, in_specs=[vmem] * n, out_specs=[vmem] * (2 * n),
        scratch_shapes=scratch, compiler_params=_params(48),
    )(*grads)
    return out[:n], out[n:]


def _row_tile(rows):
    for t in (256, 128, 64):
        if rows % t == 0:
            return t
    return rows


def _add_pair(core, chip, grad, landed):
    _, h, cols = landed.shape
    tr = _row_tile(h)

    def body(core_ref, chip_ref, g_ref, l_ref, own_ref, ob_ref):
        s = g_ref[...] + l_ref[...]
        ob_ref[...] = s.astype(BF16)

        @pl.when(pl.program_id(1) == chip_ref[0])
        def _():
            own_ref[...] = s

    return _pcall(
        body, name="grad_add_pair",
        out_shape=[jax.ShapeDtypeStruct((h, cols), F32), jax.ShapeDtypeStruct(landed.shape, BF16)],
        grid_spec=pltpu.PrefetchScalarGridSpec(
            num_scalar_prefetch=2, grid=(h // tr, N_CHIPS),
            in_specs=[pl.BlockSpec((None, None, tr, cols), lambda t, k, core_ref, chip_ref: (k, core_ref[0], t, 0)),
                      pl.BlockSpec((None, tr, cols), lambda t, k, core_ref, chip_ref: (k, t, 0))],
            out_specs=[pl.BlockSpec((tr, cols), lambda t, k, core_ref, chip_ref: (t, 0)),
                       pl.BlockSpec((None, tr, cols), lambda t, k, core_ref, chip_ref: (k, t, 0))]),
        compiler_params=_params(32, ("arbitrary", "arbitrary")),
        operands=(core, chip, grad, landed))


def _add_chips(own, landed):
    h, cols = own.shape
    tr = _row_tile(h)

    def body(p_ref, l_ref, o_ref):
        o_ref[...] = ((p_ref[...] + l_ref[0].astype(F32)) + l_ref[1].astype(F32)) + l_ref[2].astype(F32)

    return _pcall(
        body, name="grad_add_chips", grid=(h // tr,),
        out_shape=jax.ShapeDtypeStruct((h, cols), F32),
        in_specs=[pl.BlockSpec((tr, cols), lambda t: (t, 0)), pl.BlockSpec((3, tr, cols), lambda t: (0, t, 0))],
        out_specs=pl.BlockSpec((tr, cols), lambda t: (t, 0)),
        compiler_params=_params(32, ("arbitrary",)),
        operands=(own, landed))


def _ada_project(c_all, w_ada, b_shard):
    n = w_ada.shape[1]
    tn = 512

    def body(c_ref, w_ref, b_ref, mod_ref, cond_ref):
        cv = c_ref[...]
        cond = cv * _sigmoid(cv)
        mod_ref[...] = _dot(cond.astype(BF16), w_ref[...].astype(BF16)) + b_ref[...]
        cond_ref[0:N_DEV, :] = cond
        cond_ref[N_DEV:2 * N_DEV, :] = jnp.zeros_like(cond)

    return _pcall(
        body, name="ada_project", grid=(n // tn,),
        out_shape=[jax.ShapeDtypeStruct((N_DEV, n), F32), jax.ShapeDtypeStruct((2 * N_DEV, D_MODEL), F32)],
        in_specs=[pl.BlockSpec((N_DEV, D_MODEL), lambda j: (0, 0)), pl.BlockSpec((D_MODEL, tn), lambda j: (0, j)),
                  pl.BlockSpec((1, tn), lambda j: (0, j))],
        out_specs=[pl.BlockSpec((N_DEV, tn), lambda j: (0, j)), pl.BlockSpec((2 * N_DEV, D_MODEL), lambda j: (0, 0))],
        compiler_params=_params(32, ("arbitrary",)),
        operands=(c_all, w_ada, b_shard))


def _in_project(x, pos, sc_a, sh_a, w_in, q_norm_w, kv_norm_w, w_q, w_kv, invf, exchange=None):
    t_len = x.shape[0]
    tm = min(512, t_len)

    def body(x_ref, pos_ref, sc_ref, sh_ref, win_ref, qn_ref, kvn_ref, wq_ref, wkv_ref, invf_ref,
             u_ref, zhg_ref, cq_ref, ckv_ref, q_ref, k_ref, kt_ref, v_ref, vt_ref):
        u = (x_ref[...] * (1.0 + sc_ref[...]) + sh_ref[...]).astype(BF16)
        u_ref[...] = u
        z = _dot_nt(u, win_ref[...])
        zhg_ref[...] = z[:, :HG_COLS]
        cq = z[:, HG_COLS:HG_COLS + Q_RANK]
        ckv = z[:, HG_COLS + Q_RANK:HG_COLS + Q_RANK + KV_RANK]
        cq_ref[...] = cq
        ckv_ref[...] = ckv
        cos_t, sin_t = _rope_tables(pos_ref[...], invf_ref[...])
        k_pe = _rope(z[:, HG_COLS + Q_RANK + KV_RANK:], cos_t, sin_t)
        k_pe_t = jnp.transpose(k_pe).astype(BF16)
        cqn = (cq * lax.rsqrt(_rowmean(cq * cq) + RMS_EPS) * qn_ref[...]).astype(BF16)
        ckvn = (ckv * lax.rsqrt(_rowmean(ckv * ckv) + RMS_EPS) * kvn_ref[...]).astype(BF16)
        for h in range(N_HEADS):
            qh = _dot(cqn, wq_ref[h])
            q_ref[h, :, 0:HEAD_DIM] = qh[:, :HEAD_DIM].astype(BF16)
            q_ref[h, :, HEAD_DIM:QK_DIM] = _rope(qh[:, HEAD_DIM:], cos_t, sin_t).astype(BF16)
            kvh = _dot(ckvn, wkv_ref[h])
            k_ref[h, :, 0:HEAD_DIM] = kvh[:, :HEAD_DIM].astype(BF16)
            k_ref[h, :, HEAD_DIM:QK_DIM] = k_pe.astype(BF16)
            kt_ref[h, 0:HEAD_DIM, :] = jnp.transpose(kvh[:, :HEAD_DIM]).astype(BF16)
            kt_ref[h, HEAD_DIM:QK_DIM, :] = k_pe_t
            v_ref[h] = kvh[:, HEAD_DIM:].astype(BF16)
            vt_ref[h] = jnp.transpose(kvh[:, HEAD_DIM:]).astype(BF16)

    row = lambda i: (i, 0)
    fixed2 = lambda i: (0, 0)
    fixed3 = lambda i: (0, 0, 0)
    heads = lambda i: (0, i, 0)
    n_tiles = t_len // tm
    return _pallas(
        body, name="in_project", grid=(n_tiles,),
        operands=(x, pos, sc_a, sh_a, w_in, q_norm_w, kv_norm_w, w_q, w_kv, invf),
        out_shape=[jax.ShapeDtypeStruct((t_len, D_MODEL), BF16), jax.ShapeDtypeStruct((t_len, HG_COLS), F32),
                   jax.ShapeDtypeStruct((t_len, Q_RANK), F32), jax.ShapeDtypeStruct((t_len, KV_RANK), F32),
                   jax.ShapeDtypeStruct((N_HEADS, t_len, QK_DIM), BF16),
                   jax.ShapeDtypeStruct((N_HEADS, t_len, QK_DIM), BF16),
                   jax.ShapeDtypeStruct((N_HEADS, QK_DIM, t_len), BF16),
                   jax.ShapeDtypeStruct((N_HEADS, t_len, HEAD_DIM), BF16),
                   jax.ShapeDtypeStruct((N_HEADS, HEAD_DIM, t_len), BF16)],
        in_specs=[pl.BlockSpec((tm, D_MODEL), row), pl.BlockSpec((tm, 1), row),
                  pl.BlockSpec((1, D_MODEL), fixed2), pl.BlockSpec((1, D_MODEL), fixed2),
                  pl.BlockSpec((IN_COLS_PAD, D_MODEL), fixed2),
                  pl.BlockSpec((1, Q_RANK), fixed2), pl.BlockSpec((1, KV_RANK), fixed2),
                  pl.BlockSpec((N_HEADS, Q_RANK, QK_DIM), fixed3), pl.BlockSpec((N_HEADS, KV_RANK, 2 * HEAD_DIM), fixed3),
                  pl.BlockSpec((1, 128), fixed2)],
        out_specs=[pl.BlockSpec((tm, D_MODEL), row), pl.BlockSpec((tm, HG_COLS), row),
                   pl.BlockSpec((tm, Q_RANK), row), pl.BlockSpec((tm, KV_RANK), row),
                   pl.BlockSpec((N_HEADS, tm, QK_DIM), heads), pl.BlockSpec((N_HEADS, tm, QK_DIM), heads),
                   pl.BlockSpec((N_HEADS, QK_DIM, tm), lambda i: (0, 0, i)),
                   pl.BlockSpec((N_HEADS, tm, HEAD_DIM), heads),
                   pl.BlockSpec((N_HEADS, HEAD_DIM, tm), lambda i: (0, 0, i))],
        params=_params(48, ("arbitrary",)), exchange=exchange,
        first=lambda: pl.program_id(0) == 0, last=lambda: pl.program_id(0) == n_tiles - 1)


def _lower_bound(lb_raw):
    m = jnp.max(lb_raw, axis=0, keepdims=True)
    e = jnp.exp(lb_raw - m)
    return e[0:1] / jnp.sum(e, axis=0, keepdims=True)


def _tri(inclusive_lower):
    r = lax.broadcasted_iota(jnp.int32, (HG_CHUNK, HG_CHUNK), 0)
    c = lax.broadcasted_iota(jnp.int32, (HG_CHUNK, HG_CHUNK), 1)
    return (c <= r) if inclusive_lower else (c >= r)


def _chunk_rows(n):
    return slice(n * HG_CHUNK, (n + 1) * HG_CHUNK)


def _chunk_prefix_sums(v, inclusive_lower):
    tri = _tri(inclusive_lower).astype(BF16)
    hi = v.astype(BF16)
    rest = v - hi.astype(F32)
    mid = rest.astype(BF16)
    lo = (rest - mid.astype(F32)).astype(BF16)
    pieces = jnp.concatenate([hi, mid, lo], axis=1)
    out = []
    for n in range(v.shape[0] // HG_CHUNK):
        s = _dot(tri, pieces[_chunk_rows(n)])
        out.append((s[:, 0:HEAD_DIM] + s[:, HEAD_DIM:2 * HEAD_DIM]) + s[:, 2 * HEAD_DIM:])
    return jnp.concatenate(out, axis=0)


def _per_chunk(v, row):
    n = v.shape[0] // HG_CHUNK
    v3 = v.reshape(n, HG_CHUNK, HEAD_DIM)
    return jnp.broadcast_to(v3[:, row:row + 1, :], v3.shape).reshape(v.shape)


def _hg_block(q, f_logit, lb):
    sg = _sigmoid(f_logit)
    forget = lb + (1.0 - lb) * sg
    kk = 1.0 - forget
    b = _chunk_prefix_sums(jnp.log(forget), True)
    b_ref = _per_chunk(b, HG_CHUNK // 2 - 1)
    b_last = _per_chunk(b, HG_CHUNK - 1)
    e_i = jnp.exp(b - b_ref)
    e_ri = jnp.exp(b_ref - b)
    e_b = jnp.exp(b)
    e_l = jnp.exp(b_last - b)
    return dict(sg=sg, forget=forget, e_i=e_i, e_ri=e_ri, e_b=e_b, e_l=e_l, dec=jnp.exp(b_last),
                qi=q * e_i, ki=kk * e_ri, qe=q * e_b, kl=kk * e_l)


HG_STEP_HEADS = 4


def _head_cols(hh):
    return slice(hh * HEAD_DIM, (hh + 1) * HEAD_DIM)


def _hgrn_forward(zhg, lb_raw, norm_w, exchange=None):
    t_len = zhg.shape[0]
    tb = min(512, t_len)
    n_chunks = tb // HG_CHUNK
    hs = HG_STEP_HEADS

    def body(q_ref, f_ref, v_ref, g_ref, lb_ref, w_ref, opre_ref, o_ref, st_ref, state):
        @pl.when(pl.program_id(1) == 0)
        def _():
            state[...] = jnp.zeros_like(state)

        causal = _tri(True)
        heads = range(hs)
        blk, v, qi, ki, qe, kl = {}, {}, {}, {}, {}, {}
        for hh in heads:
            cols = _head_cols(hh)
            blk[hh] = _hg_block(q_ref[:, cols], f_ref[:, cols], _lower_bound(lb_ref[:, cols]))
            v[hh] = v_ref[:, cols].astype(BF16)
            qi[hh], ki[hh], qe[hh], kl[hh] = (blk[hh][name].astype(BF16) for name in ("qi", "ki", "qe", "kl"))
        st = {hh: state[hh] for hh in heads}
        parts = {hh: [] for hh in heads}
        for n in range(n_chunks):
            r = _chunk_rows(n)
            for hh in heads:
                a = jnp.where(causal, _dot_nt(qi[hh][r], ki[hh][r]), 0.0).astype(BF16)
                st_ref[hh, n] = st[hh]
                parts[hh].append(_dot(a, v[hh][r]) + _dot_nt(qe[hh][r], st[hh].astype(BF16)))
                st[hh] = st[hh] * blk[hh]["dec"][n * HG_CHUNK:n * HG_CHUNK + 1] + _dot_tn(v[hh][r], kl[hh][r])
        for hh in heads:
            cols = _head_cols(hh)
            state[hh] = st[hh]
            o = jnp.concatenate(parts[hh], axis=0)
            opre_ref[:, cols] = o
            g = g_ref[:, cols]
            o_ref[:, cols] = o * lax.rsqrt(_rowmean(o * o) + RMS_EPS) * w_ref[:, cols] * (g * _sigmoid(g))

    groups = N_HEADS // hs
    wide = hs * HEAD_DIM
    col = lambda off: (lambda h, t: (t, off + h))
    nb = t_len // tb
    return _pallas(
        body, name="hgrn_forward", grid=(groups, nb), operands=(zhg, zhg, zhg, zhg, lb_raw, norm_w),
        out_shape=[jax.ShapeDtypeStruct((t_len, N_HEADS * HEAD_DIM), F32),
                   jax.ShapeDtypeStruct((t_len, N_HEADS * HEAD_DIM), F32),
                   jax.ShapeDtypeStruct((N_HEADS, t_len // HG_CHUNK, HEAD_DIM, HEAD_DIM), F32)],
        in_specs=[pl.BlockSpec((tb, wide), col(0)), pl.BlockSpec((tb, wide), col(groups)),
                  pl.BlockSpec((tb, wide), col(2 * groups)), pl.BlockSpec((tb, wide), col(3 * groups)),
                  pl.BlockSpec((2, wide), lambda h, t: (0, h)), pl.BlockSpec((1, wide), lambda h, t: (0, h))],
        out_specs=[pl.BlockSpec((tb, wide), col(0)), pl.BlockSpec((tb, wide), col(0)),
                   pl.BlockSpec((hs, n_chunks, HEAD_DIM, HEAD_DIM), lambda h, t: (h, t, 0, 0))],
        scratch_shapes=[pltpu.VMEM((hs, HEAD_DIM, HEAD_DIM), F32)],
        params=_params(40, ("arbitrary", "arbitrary")), exchange=exchange,
        first=lambda: (pl.program_id(0) == 0) & (pl.program_id(1) == 0),
        last=lambda: (pl.program_id(0) == groups - 1) & (pl.program_id(1) == nb - 1))


def _hgrn_backward(zhg, lb_raw, norm_w, o_pre, d_cat, states, exchange=None):
    t_len = zhg.shape[0]
    tb = min(512, t_len)
    n_chunks = tb // HG_CHUNK
    nb = t_len // tb
    hs = HG_STEP_HEADS

    def body(q_ref, f_ref, v_ref, g_ref, lb_ref, w_ref, opre_ref, do_ref, st_ref,
             dq_ref, df_ref, dv_ref, dg_ref, sums_ref, gstate):
        @pl.when(pl.program_id(1) == 0)
        def _():
            gstate[...] = jnp.zeros_like(gstate)
            sums_ref[...] = jnp.zeros_like(sums_ref)

        heads = range(hs)
        causal = _tri(True)
        row_id = lax.broadcasted_iota(jnp.int32, (HG_CHUNK, HEAD_DIM), 0)
        lb, d_o, blk, v, qi, ki, qe, kl = ({} for _ in range(8))
        for hh in heads:
            cols = _head_cols(hh)
            lb[hh] = _lower_bound(lb_ref[:, cols])
            w = w_ref[:, cols]
            o = opre_ref[:, cols]
            g = g_ref[:, cols]
            d_out = do_ref[:, cols]
            r = lax.rsqrt(_rowmean(o * o) + RMS_EPS)
            sg_g = _sigmoid(g)
            dg_ref[:, cols] = (d_out * (o * r * w) * (sg_g * (1.0 + g * (1.0 - sg_g)))).astype(BF16)
            d_on = d_out * (g * sg_g)
            sums_ref[1:2, cols] += _colsum(d_on * o * r)
            dy = d_on * w
            d_o[hh] = (r * dy - o * (r * r * r) * _rowmean(dy * o)).astype(BF16)
            blk[hh] = _hg_block(q_ref[:, cols], f_ref[:, cols], lb[hh])
            v[hh] = v_ref[:, cols].astype(BF16)
            qi[hh], ki[hh], qe[hh], kl[hh] = (blk[hh][name].astype(BF16) for name in ("qi", "ki", "qe", "kl"))
        gt = {hh: gstate[hh] for hh in heads}
        d_v, d_qi, d_ki, d_qe, d_kl, d_dec = ({hh: [None] * n_chunks for hh in heads} for _ in range(6))
        for n in reversed(range(n_chunks)):
            rows = _chunk_rows(n)
            for hh in heads:
                st = st_ref[hh, n]
                a = jnp.where(causal, _dot_nt(qi[hh][rows], ki[hh][rows]), 0.0).astype(BF16)
                d_a = jnp.where(causal, _dot_nt(d_o[hh][rows], v[hh][rows]), 0.0).astype(BF16)
                gt_b = gt[hh].astype(BF16)
                d_v[hh][n] = _dot_tn(a, d_o[hh][rows]) + _dot_nt(kl[hh][rows], gt_b)
                d_qi[hh][n] = _dot(d_a, ki[hh][rows])
                d_ki[hh][n] = _dot_tn(d_a, qi[hh][rows])
                d_qe[hh][n] = _dot(d_o[hh][rows], st.astype(BF16))
                d_kl[hh][n] = _dot(v[hh][rows], gt_b)
                d_dec[hh][n] = jnp.where(row_id == HG_CHUNK - 1, _colsum(gt[hh] * st), 0.0)
                gt[hh] = gt[hh] * blk[hh]["dec"][n * HG_CHUNK:n * HG_CHUNK + 1] + _dot_tn(d_o[hh][rows], qe[hh][rows])
        for hh in heads:
            cols = _head_cols(hh)
            b = blk[hh]
            gstate[hh] = gt[hh]
            dqi, dki, dqe, dkl, ddec = (jnp.concatenate(p[hh], axis=0) for p in (d_qi, d_ki, d_qe, d_kl, d_dec))
            dv_ref[:, cols] = jnp.concatenate(d_v[hh], axis=0).astype(BF16)
            dq_ref[:, cols] = (dqi * b["e_i"] + dqe * b["e_b"]).astype(BF16)
            d_k = dki * b["e_ri"] + dkl * b["e_l"]
            t_qi = dqi * b["qi"]
            t_ki = dki * b["ki"]
            t_kl = dkl * b["kl"]
            at_ref, at_last = [], []
            for n in range(n_chunks):
                rows = _chunk_rows(n)
                at_ref.append(jnp.where(row_id == HG_CHUNK // 2 - 1, _colsum(t_ki[rows] - t_qi[rows]), 0.0))
                at_last.append(jnp.where(row_id == HG_CHUNK - 1, _colsum(t_kl[rows]), 0.0))
            d_b = (t_qi - t_ki + dqe * b["qe"] - t_kl + jnp.concatenate(at_ref, axis=0)
                   + jnp.concatenate(at_last, axis=0) + ddec * b["dec"])
            d_forget = _chunk_prefix_sums(d_b, False) / b["forget"] - d_k
            sg = b["sg"]
            df_ref[:, cols] = (d_forget * (1.0 - lb[hh]) * sg * (1.0 - sg)).astype(BF16)
            sums_ref[0:1, cols] += _colsum(d_forget * (1.0 - sg))

    groups = N_HEADS // hs
    wide = hs * HEAD_DIM
    col = lambda off: (lambda h, t: (nb - 1 - t, off + h))
    return _pallas(
        body, name="hgrn_backward", grid=(groups, nb),
        operands=(zhg, zhg, zhg, zhg, lb_raw, norm_w, o_pre, d_cat, states),
        out_shape=[jax.ShapeDtypeStruct((t_len, N_HEADS * HEAD_DIM), BF16)] * 4
        + [jax.ShapeDtypeStruct((8, N_HEADS * HEAD_DIM), F32)],
        in_specs=[pl.BlockSpec((tb, wide), col(0)), pl.BlockSpec((tb, wide), col(groups)),
                  pl.BlockSpec((tb, wide), col(2 * groups)), pl.BlockSpec((tb, wide), col(3 * groups)),
                  pl.BlockSpec((2, wide), lambda h, t: (0, h)), pl.BlockSpec((1, wide), lambda h, t: (0, h)),
                  pl.BlockSpec((tb, wide), col(0)), pl.BlockSpec((tb, wide), col(0)),
                  pl.BlockSpec((hs, n_chunks, HEAD_DIM, HEAD_DIM), lambda h, t: (h, nb - 1 - t, 0, 0))],
        out_specs=[pl.BlockSpec((tb, wide), col(0))] * 4 + [pl.BlockSpec((8, wide), lambda h, t: (0, h))],
        scratch_shapes=[pltpu.VMEM((hs, HEAD_DIM, HEAD_DIM), F32)],
        params=_params(40, ("arbitrary", "arbitrary")), exchange=exchange,
        first=lambda: (pl.program_id(0) == 0) & (pl.program_id(1) == 0),
        last=lambda: (pl.program_id(0) == groups - 1) & (pl.program_id(1) == nb - 1))


ATT_LOG2 = ATT_SCALE * 1.4426950408889634


def _triangle_steps(nq, q_major):
    if q_major:
        pairs = [(i, j) for i in range(nq) for j in range(i + 1)]
    else:
        pairs = [(i, j) for j in range(nq) for i in range(j, nq)]
    return jnp.array([p[0] for p in pairs], jnp.int32), jnp.array([p[1] for p in pairs], jnp.int32)


def _key_le_query(t):
    return lax.broadcasted_iota(jnp.int32, (t, t), 0) <= lax.broadcasted_iota(jnp.int32, (t, t), 1)


def _attention_forward(q, k, v_t, exchange=None):
    t_len = q.shape[1]
    tq = min(512, t_len)
    nq = t_len // tq
    qi_tab, ki_tab = _triangle_steps(nq, True)

    def body(qi_ref, ki_ref, q_ref, k_ref, vt_ref, o_ref, lse_ref, m_s, l_s, acc_s):
        step = pl.program_id(0)
        qi, ki = qi_ref[step], ki_ref[step]

        @pl.when(ki == 0)
        def _():
            m_s[...] = jnp.full_like(m_s, NEG_BIG)
            l_s[...] = jnp.zeros_like(l_s)
            acc_s[...] = jnp.zeros_like(acc_s)

        def accumulate(masked):
            s_all = [_dot_nt(k_ref[h], q_ref[h]) * ATT_LOG2 for h in range(N_HEADS)]
            for h in range(N_HEADS):
                s_t = s_all[h]
                if masked:
                    s_t = jnp.where(_key_le_query(tq), s_t, NEG_BIG)
                m_old = m_s[h]
                m_new = jnp.maximum(m_old, jnp.max(s_t, axis=0, keepdims=True))
                alpha = jnp.exp2(m_old - m_new)
                p_t = jnp.exp2(s_t - m_new)
                l_s[h] = alpha * l_s[h] + jnp.sum(p_t, axis=0, keepdims=True)
                acc_s[h] = alpha * acc_s[h] + _dot(vt_ref[h], p_t.astype(BF16))
                m_s[h] = m_new

        @pl.when(ki < qi)
        def _():
            accumulate(False)

        @pl.when(ki == qi)
        def _():
            accumulate(True)
            for h in range(N_HEADS):
                o_ref[:, h * HEAD_DIM:(h + 1) * HEAD_DIM] = jnp.transpose(acc_s[h] / l_s[h])
                lse_ref[h] = m_s[h] + jnp.log2(l_s[h])

    n_steps = qi_tab.shape[0]
    return _pallas(
        body, name="attention_forward", grid=(n_steps,), prefetch=(qi_tab, ki_tab), operands=(q, k, v_t),
        out_shape=[jax.ShapeDtypeStruct((t_len, N_HEADS * HEAD_DIM), F32),
                   jax.ShapeDtypeStruct((N_HEADS, 1, t_len), F32)],
        in_specs=[pl.BlockSpec((N_HEADS, tq, QK_DIM), lambda s, qt, kt: (0, qt[s], 0)),
                  pl.BlockSpec((N_HEADS, tq, QK_DIM), lambda s, qt, kt: (0, kt[s], 0)),
                  pl.BlockSpec((N_HEADS, HEAD_DIM, tq), lambda s, qt, kt: (0, 0, kt[s]))],
        out_specs=[pl.BlockSpec((tq, N_HEADS * HEAD_DIM), lambda s, qt, kt: (qt[s], 0)),
                   pl.BlockSpec((N_HEADS, 1, tq), lambda s, qt, kt: (0, 0, qt[s]))],
        scratch_shapes=[pltpu.VMEM((N_HEADS, 1, tq), F32), pltpu.VMEM((N_HEADS, 1, tq), F32),
                        pltpu.VMEM((N_HEADS, HEAD_DIM, tq), F32)],
        params=_params(48, ("arbitrary",)), exchange=exchange,
        first=lambda qt, kt: pl.program_id(0) == 0, last=lambda qt, kt: pl.program_id(0) == n_steps - 1)


BWD_HEADS = 4


def _attention_backward(q, k, k_t, v, d_cat, lse, delta, exchange=None):
    t_len = q.shape[1]
    tq = min(512, t_len)
    nq = t_len // tq
    hp = BWD_HEADS
    qi_tab, ki_tab = _triangle_steps(nq, False)

    def body(qi_ref, ki_ref, q_ref, k_ref, kt_ref, v_ref, do_ref, lse_ref, delta_ref, dqt_hbm, dk_ref, dv_ref,
             dqt_s, dk_s, dv_s):
        group, step = pl.program_id(0), pl.program_id(1)
        qi, ki = qi_ref[step], ki_ref[step]

        @pl.when(step == 0)
        def _():
            dqt_s[...] = jnp.zeros_like(dqt_s)

        @pl.when(qi == ki)
        def _():
            dk_s[...] = jnp.zeros_like(dk_s)
            dv_s[...] = jnp.zeros_like(dv_s)

        def accumulate(masked):
            for h in range(hp):
                do_b = do_ref[:, h * HEAD_DIM:(h + 1) * HEAD_DIM].astype(BF16)
                s_t = _dot_nt(k_ref[h], q_ref[h]) * ATT_LOG2
                if masked:
                    s_t = jnp.where(_key_le_query(tq), s_t, NEG_BIG)
                p_t = jnp.exp2(s_t - lse_ref[h])
                dp_t = _dot_nt(v_ref[h], do_b)
                ds_t = (p_t * (dp_t - delta_ref[h]) * ATT_SCALE).astype(BF16)
                dv_s[h] += _dot(p_t.astype(BF16), do_b)
                dk_s[h] += _dot(ds_t, q_ref[h])
                dqt_s[h, qi] += _dot(kt_ref[h], ds_t)

        @pl.when(ki < qi)
        def _():
            accumulate(False)

        @pl.when(ki == qi)
        def _():
            accumulate(True)
            for h in range(hp):
                pltpu.sync_copy(dqt_s.at[h, qi], dqt_hbm.at[group * hp + h, qi])

        @pl.when(qi == nq - 1)
        def _():
            dk_ref[...] = dk_s[...]
            dv_ref[...] = dv_s[...]

    wide = hp * HEAD_DIM
    n_groups, n_steps = N_HEADS // hp, qi_tab.shape[0]
    return _pallas(
        body, name="attention_backward", grid=(n_groups, n_steps), prefetch=(qi_tab, ki_tab),
        operands=(q, k, k_t, v, d_cat, lse, delta),
        out_shape=[jax.ShapeDtypeStruct((N_HEADS, nq, QK_DIM, tq), F32),
                   jax.ShapeDtypeStruct((N_HEADS, t_len, QK_DIM), F32),
                   jax.ShapeDtypeStruct((N_HEADS, t_len, HEAD_DIM), F32)],
        in_specs=[pl.BlockSpec((hp, tq, QK_DIM), lambda g, s, qt, kt: (g, qt[s], 0)),
                  pl.BlockSpec((hp, tq, QK_DIM), lambda g, s, qt, kt: (g, kt[s], 0)),
                  pl.BlockSpec((hp, QK_DIM, tq), lambda g, s, qt, kt: (g, 0, kt[s])),
                  pl.BlockSpec((hp, tq, HEAD_DIM), lambda g, s, qt, kt: (g, kt[s], 0)),
                  pl.BlockSpec((tq, wide), lambda g, s, qt, kt: (qt[s], n_groups + g)),
                  pl.BlockSpec((hp, 1, tq), lambda g, s, qt, kt: (g, 0, qt[s])),
                  pl.BlockSpec((hp, 1, tq), lambda g, s, qt, kt: (g, 0, qt[s]))],
        out_specs=[pl.BlockSpec(memory_space=pl.ANY),
                   pl.BlockSpec((hp, tq, QK_DIM), lambda g, s, qt, kt: (g, kt[s], 0)),
                   pl.BlockSpec((hp, tq, HEAD_DIM), lambda g, s, qt, kt: (g, kt[s], 0))],
        scratch_shapes=[pltpu.VMEM((hp, nq, QK_DIM, tq), F32), pltpu.VMEM((hp, tq, QK_DIM), F32),
                        pltpu.VMEM((hp, tq, HEAD_DIM), F32)],
        params=_params(58, ("arbitrary", "arbitrary")), exchange=exchange,
        first=lambda qt, kt: (pl.program_id(0) == 0) & (pl.program_id(1) == 0),
        last=lambda qt, kt: (pl.program_id(0) == n_groups - 1) & (pl.program_id(1) == n_steps - 1))


def _out_project(o_hg, o_mla, x, g_a, w_out, exchange=None):
    t_len = x.shape[0]
    tm = min(512, t_len)
    half = N_HEADS * HEAD_DIM

    def body(ohg_ref, omla_ref, x_ref, ga_ref, w_ref, cat_ref, mix_ref, xhat_ref, rstd_ref):
        a = ohg_ref[...].astype(BF16)
        b = omla_ref[...].astype(BF16)
        cat_ref[:, 0:half] = a
        cat_ref[:, half:2 * half] = b
        mix = _dot(a, w_ref[0:half, :]) + _dot(b, w_ref[half:2 * half, :])
        mix_ref[...] = mix
        r1 = DN_ALPHA * x_ref[...] + (1.0 + ga_ref[...]) * mix
        xc = r1 - _rowmean(r1)
        rstd = lax.rsqrt(_rowmean(xc * xc) + LN_EPS)
        xhat_ref[...] = xc * rstd
        rstd_ref[...] = rstd

    row = lambda i: (i, 0)
    fixed = lambda i: (0, 0)
    n_tiles = t_len // tm
    return _pallas(
        body, name="out_project", grid=(n_tiles,), operands=(o_hg, o_mla, x, g_a, w_out),
        out_shape=[jax.ShapeDtypeStruct((t_len, D_MODEL), BF16), jax.ShapeDtypeStruct((t_len, D_MODEL), F32),
                   jax.ShapeDtypeStruct((t_len, D_MODEL), F32), jax.ShapeDtypeStruct((t_len, 1), F32)],
        in_specs=[pl.BlockSpec((tm, half), row), pl.BlockSpec((tm, half), row), pl.BlockSpec((tm, D_MODEL), row),
                  pl.BlockSpec((1, D_MODEL), fixed), pl.BlockSpec((D_MODEL, D_MODEL), fixed)],
        out_specs=[pl.BlockSpec((tm, D_MODEL), row), pl.BlockSpec((tm, D_MODEL), row),
                   pl.BlockSpec((tm, D_MODEL), row), pl.BlockSpec((tm, 1), row)],
        params=_params(48, ("arbitrary",)), exchange=exchange,
        first=lambda: pl.program_id(0) == 0, last=lambda: pl.program_id(0) == n_tiles - 1)


V_LN1G, V_LN1B, V_SCM, V_SHM, V_GM, V_GA, V_LN2G, V_LN2B = range(8)
S_DLN2G, S_DLN2B, S_DGM, S_DSCM, S_DSHM, S_DLN1G, S_DLN1B, S_DGA, S_LOSS = range(9)


def _mlp_and_back(xhat1, rstd1, mix, target, o_mla, vecs, w1_top, w1_bottom, w2, w_out):
    t_len = xhat1.shape[0]
    tm = min(256, t_len)
    n_ff = w1_top.shape[0]
    ff = w1_top.shape[2]
    top_rows = w1_top.shape[1]

    def body(xhat_ref, rstd_ref, mix_ref, tgt_ref, omla_ref, vec_ref, w1_top_hbm, w1_bottom_hbm, w2_hbm, wout_hbm,
             act_ref, dhp_ref, um_ref, dh_ref, dmix_ref, dcat_ref, dr1_ref, sums_ref, delta_ref,
             w1_s, w2_s, wout_s, hp_s, load_sems):
        @pl.when(pl.program_id(0) == 0)
        def _():
            loads = [pltpu.make_async_copy(w1_top_hbm, w1_s.at[:, 0:top_rows], load_sems.at[0]),
                     pltpu.make_async_copy(w1_bottom_hbm, w1_s.at[:, top_rows:D_MODEL], load_sems.at[3]),
                     pltpu.make_async_copy(w2_hbm, w2_s, load_sems.at[1]),
                     pltpu.make_async_copy(wout_hbm, wout_s, load_sems.at[2])]
            for cp in loads:
                cp.start()
            sums_ref[...] = jnp.zeros_like(sums_ref)
            for cp in loads:
                cp.wait()

        vec = lambda r: vec_ref[r:r + 1, :]
        xhat = xhat_ref[...]
        x1 = xhat * vec(V_LN1G) + vec(V_LN1B)
        um = (x1 * (1.0 + vec(V_SCM)) + vec(V_SHM)).astype(BF16)
        um_ref[...] = um
        h = jnp.zeros((tm, D_MODEL), F32)
        for j in range(n_ff):
            hp = _dot(um, w1_s[j])
            hp_s[j] = hp
            act = jnp.square(jnp.maximum(hp, 0.0)).astype(BF16)
            act_ref[:, j * ff:(j + 1) * ff] = act
            h = h + _dot(act, w2_s[j])
        r2 = DN_ALPHA * x1 + (1.0 + vec(V_GM)) * h
        xc = r2 - _rowmean(r2)
        rstd2 = lax.rsqrt(_rowmean(xc * xc) + LN_EPS)
        xhat2 = xc * rstd2
        err = xhat2 * vec(V_LN2G) + vec(V_LN2B) - tgt_ref[...]
        loss = 0.5 * jnp.sum(_rowmean(err * err))
        dy = err * (1.0 / D_MODEL)
        dxh = dy * vec(V_LN2G)
        dr2 = rstd2 * (dxh - _rowmean(dxh) - xhat2 * _rowmean(dxh * xhat2))
        dh = ((1.0 + vec(V_GM)) * dr2).astype(BF16)
        dh_ref[...] = dh
        sums_ref[S_DLN2G:S_DLN2G + 1, :] += _colsum(dy * xhat2)
        sums_ref[S_DLN2B:S_DLN2B + 1, :] += _colsum(dy)
        sums_ref[S_DGM:S_DGM + 1, :] += _colsum(dr2 * h)
        sums_ref[S_LOSS:S_LOSS + 1, :] += jnp.full((1, D_MODEL), loss, F32)
        du = jnp.zeros((tm, D_MODEL), F32)
        for j in range(n_ff):
            dhp = (_dot_nt(dh, w2_s[j]) * (2.0 * jnp.maximum(hp_s[j], 0.0))).astype(BF16)
            dhp_ref[:, j * ff:(j + 1) * ff] = dhp
            du = du + _dot_nt(dhp, w1_s[j])
        sums_ref[S_DSCM:S_DSCM + 1, :] += _colsum(du * x1)
        sums_ref[S_DSHM:S_DSHM + 1, :] += _colsum(du)
        dx1 = DN_ALPHA * dr2 + du * (1.0 + vec(V_SCM))
        sums_ref[S_DLN1G:S_DLN1G + 1, :] += _colsum(dx1 * xhat)
        sums_ref[S_DLN1B:S_DLN1B + 1, :] += _colsum(dx1)
        dxh1 = dx1 * vec(V_LN1G)
        dr1 = rstd_ref[...] * (dxh1 - _rowmean(dxh1) - xhat * _rowmean(dxh1 * xhat))
        dr1_ref[...] = dr1
        sums_ref[S_DGA:S_DGA + 1, :] += _colsum(dr1 * mix_ref[...])
        dmix = ((1.0 + vec(V_GA)) * dr1).astype(BF16)
        dmix_ref[...] = dmix
        dcat = _dot_nt(dmix, wout_s[...])
        dcat_ref[...] = dcat
        half = N_HEADS * HEAD_DIM
        for hd in range(N_HEADS):
            prod = dcat[:, half + hd * HEAD_DIM:half + (hd + 1) * HEAD_DIM] * omla_ref[:, hd * HEAD_DIM:(hd + 1) * HEAD_DIM]
            sums = jnp.broadcast_to(jnp.sum(prod, axis=1, keepdims=True), (tm, HEAD_DIM))
            delta_ref[hd] = jnp.transpose(sums)[0:1]

    row = lambda i: (i, 0)
    fixed = lambda i: (0, 0)
    any_spec = pl.BlockSpec(memory_space=pl.ANY)
    return _pcall(
        body, name="mlp_and_back", grid=(t_len // tm,),
        out_shape=[jax.ShapeDtypeStruct((t_len, D_FF), BF16), jax.ShapeDtypeStruct((t_len, D_FF), BF16),
                   jax.ShapeDtypeStruct((t_len, D_MODEL), BF16), jax.ShapeDtypeStruct((t_len, D_MODEL), BF16),
                   jax.ShapeDtypeStruct((t_len, D_MODEL), BF16), jax.ShapeDtypeStruct((t_len, D_MODEL), F32),
                   jax.ShapeDtypeStruct((t_len, D_MODEL), F32), jax.ShapeDtypeStruct((16, D_MODEL), F32),
                   jax.ShapeDtypeStruct((N_HEADS, 1, t_len), F32)],
        in_specs=[pl.BlockSpec((tm, D_MODEL), row), pl.BlockSpec((tm, 1), row), pl.BlockSpec((tm, D_MODEL), row),
                  pl.BlockSpec((tm, D_MODEL), row), pl.BlockSpec((tm, N_HEADS * HEAD_DIM), row),
                  pl.BlockSpec((8, D_MODEL), fixed), any_spec, any_spec, any_spec, any_spec],
        out_specs=[pl.BlockSpec((tm, D_FF), row), pl.BlockSpec((tm, D_FF), row), pl.BlockSpec((tm, D_MODEL), row),
                   pl.BlockSpec((tm, D_MODEL), row), pl.BlockSpec((tm, D_MODEL), row), pl.BlockSpec((tm, D_MODEL), row),
                   pl.BlockSpec((tm, D_MODEL), row), pl.BlockSpec((16, D_MODEL), fixed),
                   pl.BlockSpec((N_HEADS, 1, tm), lambda i: (0, 0, i))],
        scratch_shapes=[pltpu.VMEM((n_ff, D_MODEL, ff), BF16), pltpu.VMEM(w2.shape, BF16), pltpu.VMEM(w_out.shape, BF16),
                        pltpu.VMEM((n_ff, tm, ff), F32), pltpu.SemaphoreType.DMA((4,))],
        compiler_params=_params(56, ("arbitrary",)),
        operands=(xhat1, rstd1, mix, target, o_mla, vecs, w1_top, w1_bottom, w2, w_out))


def _in_project_backward(dq, dk, dv, cq, ckv, pos, invf, q_norm_w, kv_norm_w, w_q, w_kv,
                         d_hq, d_hf, d_hi, d_hg, w_in, dr1, x, sc_a, exchange=None):
    t_len = x.shape[0]
    tm = min(512, t_len)
    per_q = dq.shape[3] // tm
    hgw = N_HEADS * HEAD_DIM

    def body(dq_ref, dk_ref, dv_ref, cq_ref, ckv_ref, pos_ref, invf_ref, qn_ref, kvn_ref, wq_ref, wkv_ref,
             dhq_ref, dhf_ref, dhi_ref, dhg_ref, win_ref, dr1_ref, x_ref, sc_ref,
             dz_ref, gx_ref, sums_ref, dwq_ref, dwkv_ref):
        @pl.when(pl.program_id(0) == 0)
        def _():
            sums_ref[...] = jnp.zeros_like(sums_ref)
            dwq_ref[...] = jnp.zeros_like(dwq_ref)
            dwkv_ref[...] = jnp.zeros_like(dwkv_ref)

        cos_t, sin_t = _rope_tables(pos_ref[...], invf_ref[...])
        cq = cq_ref[...]
        ckv = ckv_ref[...]
        rq = lax.rsqrt(_rowmean(cq * cq) + RMS_EPS)
        rkv = lax.rsqrt(_rowmean(ckv * ckv) + RMS_EPS)
        cqn = (cq * rq * qn_ref[...]).astype(BF16)
        ckvn = (ckv * rkv * kvn_ref[...]).astype(BF16)
        d_cqn = jnp.zeros((tm, Q_RANK), F32)
        d_ckvn = jnp.zeros((tm, KV_RANK), F32)
        d_kpe = jnp.zeros((tm, 128), F32)
        for h in range(N_HEADS):
            dqh = jnp.transpose(dq_ref[h])
            dq_full = jnp.concatenate(
                [dqh[:, :HEAD_DIM].astype(BF16), _unrope(dqh[:, HEAD_DIM:], cos_t, sin_t).astype(BF16)], axis=1)
            d_cqn = d_cqn + _dot_nt(dq_full, wq_ref[h])
            dwq_ref[h] += _dot_tn(cqn, dq_full)
            dkh = dk_ref[h]
            d_kpe = d_kpe + dkh[:, HEAD_DIM:]
            dkv_up = jnp.concatenate([dkh[:, :HEAD_DIM].astype(BF16), dv_ref[h].astype(BF16)], axis=1)
            d_ckvn = d_ckvn + _dot_nt(dkv_up, wkv_ref[h])
            dwkv_ref[h] += _dot_tn(ckvn, dkv_up)
        dyq = d_cqn * qn_ref[...]
        dykv = d_ckvn * kvn_ref[...]
        sums_ref[2:3, 0:Q_RANK] += _colsum(d_cqn * cq * rq)
        sums_ref[3:4, 0:KV_RANK] += _colsum(d_ckvn * ckv * rkv)
        dz_ref[:, 0:hgw] = dhq_ref[...]
        dz_ref[:, hgw:2 * hgw] = dhf_ref[...]
        dz_ref[:, 2 * hgw:3 * hgw] = dhi_ref[...]
        dz_ref[:, 3 * hgw:4 * hgw] = dhg_ref[...]
        dz_ref[:, HG_COLS:HG_COLS + Q_RANK] = (rq * dyq - cq * (rq * rq * rq) * _rowmean(dyq * cq)).astype(BF16)
        dz_ref[:, HG_COLS + Q_RANK:HG_COLS + Q_RANK + KV_RANK] = (
            rkv * dykv - ckv * (rkv * rkv * rkv) * _rowmean(dykv * ckv)).astype(BF16)
        dz_ref[:, HG_COLS + Q_RANK + KV_RANK:] = _unrope(d_kpe, cos_t, sin_t).astype(BF16)
        du = _dot(dz_ref[...], win_ref[...])
        xv = x_ref[...]
        gx_ref[...] = DN_ALPHA * dr1_ref[...] + (1.0 + sc_ref[...]) * du
        sums_ref[0:1, :] += _colsum(du * xv)
        sums_ref[1:2, :] += _colsum(du)

    row = lambda i: (i, 0)
    fixed2 = lambda i: (0, 0)
    fixed3 = lambda i: (0, 0, 0)
    heads = lambda i: (0, i, 0)
    n_tiles = t_len // tm
    return _pallas(
        body, name="in_project_backward", grid=(n_tiles,),
        operands=(dq, dk, dv, cq, ckv, pos, invf, q_norm_w, kv_norm_w, w_q, w_kv, d_hq, d_hf, d_hi, d_hg, w_in, dr1, x,
                  sc_a),
        out_shape=[jax.ShapeDtypeStruct((t_len, IN_COLS_PAD), BF16), jax.ShapeDtypeStruct((t_len, D_MODEL), F32),
                   jax.ShapeDtypeStruct((8, D_MODEL), F32), jax.ShapeDtypeStruct((N_HEADS, Q_RANK, QK_DIM), F32),
                   jax.ShapeDtypeStruct((N_HEADS, KV_RANK, 2 * HEAD_DIM), F32)],
        in_specs=[pl.BlockSpec((N_HEADS, None, QK_DIM, tm), lambda i: (0, i // per_q, 0, i % per_q)),
                  pl.BlockSpec((N_HEADS, tm, QK_DIM), heads),
                  pl.BlockSpec((N_HEADS, tm, HEAD_DIM), heads), pl.BlockSpec((tm, Q_RANK), row),
                  pl.BlockSpec((tm, KV_RANK), row), pl.BlockSpec((tm, 1), row), pl.BlockSpec((1, 128), fixed2),
                  pl.BlockSpec((1, Q_RANK), fixed2), pl.BlockSpec((1, KV_RANK), fixed2),
                  pl.BlockSpec((N_HEADS, Q_RANK, QK_DIM), fixed3), pl.BlockSpec((N_HEADS, KV_RANK, 2 * HEAD_DIM), fixed3),
                  pl.BlockSpec((tm, hgw), row), pl.BlockSpec((tm, hgw), row), pl.BlockSpec((tm, hgw), row),
                  pl.BlockSpec((tm, hgw), row), pl.BlockSpec((IN_COLS_PAD, D_MODEL), fixed2),
                  pl.BlockSpec((tm, D_MODEL), row), pl.BlockSpec((tm, D_MODEL), row), pl.BlockSpec((1, D_MODEL), fixed2)],
        out_specs=[pl.BlockSpec((tm, IN_COLS_PAD), row), pl.BlockSpec((tm, D_MODEL), row),
                   pl.BlockSpec((8, D_MODEL), fixed2), pl.BlockSpec((N_HEADS, Q_RANK, QK_DIM), fixed3),
                   pl.BlockSpec((N_HEADS, KV_RANK, 2 * HEAD_DIM), fixed3)],
        params=_params(48, ("arbitrary",)), exchange=exchange,
        first=lambda: pl.program_id(0) == 0, last=lambda: pl.program_id(0) == n_tiles - 1)


def _weight_grad(a, b, name, n_blocks, bn, a_blocked=False, b_blocked=True, exchange=None, token_tile=512):
    t_len = a.shape[0]
    m = a.shape[1] // n_blocks if a_blocked else a.shape[1]
    bt = min(token_tile, t_len)

    def body(a_ref, b_ref, o_ref):
        @pl.when(pl.program_id(1) == 0)
        def _():
            o_ref[...] = jnp.zeros_like(o_ref)

        o_ref[...] += _dot_tn(a_ref[...].astype(BF16), b_ref[...].astype(BF16))

    a_spec = pl.BlockSpec((bt, m), (lambda n, t: (t, n)) if a_blocked else (lambda n, t: (t, 0)))
    b_spec = pl.BlockSpec((bt, bn), (lambda n, t: (t, n)) if b_blocked else (lambda n, t: (t, 0)))
    nt = t_len // bt
    (out,), landed = _pallas(
        body, name=name, grid=(n_blocks, nt), operands=(a, b),
        out_shape=[jax.ShapeDtypeStruct((n_blocks, m, bn), F32)],
        in_specs=[a_spec, b_spec],
        out_specs=[pl.BlockSpec((None, m, bn), lambda n, t: (n, 0, 0))],
        params=_params(56, ("arbitrary", "arbitrary")), exchange=exchange,
        first=lambda: (pl.program_id(0) == 0) & (pl.program_id(1) == 0),
        last=lambda: (pl.program_id(0) == n_blocks - 1) & (pl.program_id(1) == nt - 1))
    return (out, landed) if exchange else out


SMALL_PLACE = {"ln1_g": (6, 0), "ln1_b": (7, 0), "ln2_g": (8, 0), "ln2_b": (9, 0), "hg_norm_w": (10, 512),
               "mla_q_norm_w": (11, 0), "mla_kv_norm_w": (11, Q_RANK)}
SMALL_LB_ROW, SMALL_LOSS_ROW = 10, 12


def _small_params_step(gathered, params):
    names = list(params)

    def body(g_ref, *refs):
        ins, outs = refs[:3 * len(names)], refs[3 * len(names):]
        loss_ref, outs = outs[0], outs[1:]
        tot = g_ref[0]
        for d in range(1, N_DEV):
            tot = tot + g_ref[d]
        loss_ref[...] = tot[SMALL_LOSS_ROW:SMALL_LOSS_ROW + 1, 0:128]

        def update(i, grad, rows=slice(None), lanes=slice(None)):
            w_ref, m_ref, v_ref = ins[3 * i:3 * i + 3]
            g_out, d_out, nm_out, nv_out = outs[4 * i:4 * i + 4]
            g_out[rows, lanes] = grad
            d_out[rows, lanes], nm_out[rows, lanes], nv_out[rows, lanes] = _adamw_update(
                w_ref[rows, lanes], grad, m_ref[rows, lanes], v_ref[rows, lanes])

        for i, name in enumerate(names):
            if name == "b_ada":
                for r in range(6):
                    update(i, tot[r:r + 1, :], lanes=slice(r * D_MODEL, (r + 1) * D_MODEL))
            elif name == "hg_lower_bounds":
                lb = _lower_bound(ins[3 * i][...])
                d0 = tot[SMALL_LB_ROW:SMALL_LB_ROW + 1, 0:512] * lb * (1.0 - lb)
                update(i, d0, rows=slice(0, 1))
                update(i, -d0, rows=slice(1, 2))
            else:
                row, lane = SMALL_PLACE[name]
                update(i, tot[row:row + 1, lane:lane + params[name][0].shape[1]])

    flat_in = [a for name in names for a in params[name]]
    shapes = [jax.ShapeDtypeStruct((1, 128), F32)] + [jax.ShapeDtypeStruct(params[name][0].shape, F32)
                                                      for name in names for _ in range(4)]
    out = pl.pallas_call(body, name="small_params_step", out_shape=shapes)(gathered, *flat_in)
    return out[0], {name: out[1 + 4 * i:5 + 4 * i] for i, name in enumerate(names)}


def _adamw_update(w, gv, m, v):
    nm = ADAM_B1 * m + (1.0 - ADAM_B1) * gv
    nv = ADAM_B2 * v + (1.0 - ADAM_B2) * jnp.square(gv)
    m_hat = nm / (1.0 - ADAM_B1 ** ADAM_STEP)
    v_hat = nv / (1.0 - ADAM_B2 ** ADAM_STEP)
    return -ADAM_LR * (m_hat / (jnp.sqrt(v_hat) + ADAM_EPS) + ADAM_WD * w), nm, nv


def _adamw_halves(core, w, mine, theirs, m, v, name):
    rows, cols = w.shape
    h = rows // 2
    tr = _row_tile(h)
    per_half = h // tr

    def body(core_ref, w_ref, mine_ref, theirs_ref, m_ref, v_ref, g_ref, d_ref, nm_ref, nv_ref):
        is_mine = pl.program_id(0) // per_half == core_ref[0]
        gv = jnp.where(is_mine, mine_ref[...], theirs_ref[...])
        g_ref[...] = gv
        d_ref[...], nm_ref[...], nv_ref[...] = _adamw_update(w_ref[...], gv, m_ref[...], v_ref[...])

    full = pl.BlockSpec((tr, cols), lambda i, core_ref: (i, 0))
    part = pl.BlockSpec((tr, cols), lambda i, core_ref: (i % per_half, 0))
    return _pcall(
        body, name=name, out_shape=[jax.ShapeDtypeStruct(w.shape, F32)] * 4,
        grid_spec=pltpu.PrefetchScalarGridSpec(
            num_scalar_prefetch=1, grid=(rows // tr,), in_specs=[full, part, part, full, full], out_specs=[full] * 4),
        compiler_params=_params(40, ("arbitrary",)),
        operands=(core, w, mine, theirs, m, v))


def _adamw(w, g, m, v, name):
    rows, cols = w.shape
    tr = _row_tile(rows) if rows >= 8 else rows

    def body(w_ref, g_ref, m_ref, v_ref, d_ref, nm_ref, nv_ref):
        d_ref[...], nm_ref[...], nv_ref[...] = _adamw_update(w_ref[...], g_ref[...], m_ref[...], v_ref[...])

    spec = pl.BlockSpec((tr, cols), lambda i: (i, 0))
    return _pcall(
        body, name=name, grid=(rows // tr,),
        out_shape=[jax.ShapeDtypeStruct(w.shape, F32)] * 3,
        in_specs=[spec] * 4, out_specs=[spec] * 3,
        compiler_params=_params(40, ("arbitrary",)),
        operands=(w, g, m, v))


def kernel(x, c, positions, w_ada, b_ada, w_in, hg_lower_bounds, hg_norm_w, mla_q_norm_w, w_q_up, mla_kv_norm_w, w_kv_up, w_out, ln1_g, ln1_b, w_mlp_in, w_mlp_out, ln2_g, ln2_b, loss_target, m_w_ada, m_b_ada, m_w_in, m_hg_lower_bounds, m_hg_norm_w, m_mla_q_norm_w, m_w_q_up, m_mla_kv_norm_w, m_w_kv_up, m_w_out, m_ln1_g, m_ln1_b, m_w_mlp_in, m_w_mlp_out, m_ln2_g, m_ln2_b, v_w_ada, v_b_ada, v_w_in, v_hg_lower_bounds, v_hg_norm_w, v_mla_q_norm_w, v_w_q_up, v_mla_kv_norm_w, v_w_kv_up, v_w_out, v_ln1_g, v_ln1_b, v_w_mlp_in, v_w_mlp_out, v_ln2_g, v_ln2_b):
    ix, iy, ic = _mesh_pos()
    chip = 2 * ix + iy
    me = 4 * ix + 2 * iy + ic
    core_arr = jnp.reshape(ic, (1,)).astype(jnp.int32)
    chip_arr = jnp.reshape(chip, (1,)).astype(jnp.int32)

    xs = x[0]
    target = loss_target[0]
    t_len = xs.shape[0]
    pos = positions.astype(F32).reshape(t_len, 1)
    inv = 1.0 / (ROPE_THETA ** (jnp.arange(0, ROPE_DIM, 2, dtype=F32) / ROPE_DIM))
    invf = jnp.concatenate([inv, inv, jnp.zeros((128 - ROPE_DIM,), F32)]).reshape(1, 128)

    def slot(w):
        rows, cols = w.shape
        own = w.astype(BF16).reshape(1, 2, rows // 2, cols)
        return lax.dynamic_update_slice(jnp.zeros((N_CHIPS, 2, rows // 2, cols), BF16), own, (chip, 0, 0, 0))

    def slot8(a):
        return lax.dynamic_update_slice(jnp.zeros((N_DEV,) + a.shape, a.dtype), a[None], (me, 0, 0))

    def whole(s):
        return s.reshape(N_CHIPS, 2 * s.shape[2], s.shape[3])

    def halved(g):
        return g.reshape(N_CHIPS, 2, g.shape[1] // 2, g.shape[2])

    ada_cols = w_ada.shape[2]
    c_all, *early = _run_exchange(
        _merge(_gather_all(slot8(jnp.broadcast_to(c, (8, D_MODEL)))),
               _gather_over_ici([slot(jnp.transpose(w_in[0])), slot(w_q_up[0]), slot(w_kv_up[0])])),
        "gather_c_and_mixer_weights_ici")
    b_shard = lax.dynamic_slice(b_ada, (0, chip * ada_cols), (1, ada_cols))
    mod_cols, cond16 = _ada_project(c_all[:, 0, :], w_ada[0], b_shard)
    mod_all, *early = _run_exchange(_merge(_gather_all(slot8(mod_cols)), _gather_over_d2d(early)),
                                    "gather_mod_and_mixer_weights_d2d")
    mod_mine = lax.dynamic_slice(mod_all, (0, me, 0), (N_DEV, 1, ada_cols))[::2, 0, :].reshape(6, D_MODEL)
    sh_a, sc_a, g_a, sh_m, sc_m, g_m = (mod_mine[i:i + 1] for i in range(6))
    g_in, g_q, g_kv = (whole(s) for s in early)
    w_in_full = jnp.pad(g_in.reshape(IN_COLS, D_MODEL), ((0, IN_COLS_PAD - IN_COLS), (0, 0)))
    w_q_full = jnp.pad(g_q, ((0, 0), (0, 0), (0, QK_DIM - g_q.shape[2])))

    w1_rows = D_MODEL // 2
    (u_a, zhg, cq, ckv, q, k, k_t, v, v_t), (s_top, s_out) = _in_project(
        xs, pos, sc_a, sh_a, w_in_full, mla_q_norm_w, mla_kv_norm_w, w_q_full, g_kv, invf,
        _gather_over_ici([slot(w_mlp_in[0, :w1_rows]), slot(w_out[0])]))
    (o_pre, o_hg, states), (s_bottom, s_top, s_out) = _hgrn_forward(
        zhg, hg_lower_bounds, hg_norm_w,
        _merge(_gather_over_ici([slot(w_mlp_in[0, w1_rows:])]), _gather_over_d2d([s_top, s_out])))
    (o_mla, lse), (s_w2, s_bottom) = _attention_forward(
        q, k, v_t, _merge(_gather_over_ici([slot(w_mlp_out[0])]), _gather_over_d2d([s_bottom])))
    w_out_full = whole(s_out).reshape(D_MODEL, D_MODEL)
    (cat, mix, xhat1, rstd1), (s_w2,) = _out_project(o_hg, o_mla, xs, g_a, w_out_full, _gather_over_d2d([s_w2]))
    g_w1_top, g_w1_bottom, g_w2 = whole(s_top), whole(s_bottom), whole(s_w2)
    vecs = jnp.concatenate([ln1_g, ln1_b, sc_m, sh_m, g_m, g_a, ln2_g, ln2_b], axis=0)
    act, dhp, um, dh, dmix, d_cat, dr1, mlp_sums, delta = _mlp_and_back(
        xhat1, rstd1, mix, target, o_mla, vecs, g_w1_top, g_w1_bottom, g_w2, w_out_full)

    gw_1 = halved(_weight_grad(um, dhp, "grad_w_mlp_in", N_CHIPS, D_FF // N_CHIPS, token_tile=4096))
    gw_2, (landed_1,) = _weight_grad(act, dh, "grad_w_mlp_out", N_CHIPS, D_MODEL, a_blocked=True, b_blocked=False,
                                     token_tile=4096, exchange=_pair_exchange([gw_1]))
    gw_out = _weight_grad(cat, dmix, "grad_w_out", 1, D_MODEL, token_tile=2048)
    later = [halved(gw_2), halved(gw_out.reshape(N_CHIPS, D_MODEL // N_CHIPS, D_MODEL))]
    own_1, travels_1 = _add_pair(core_arr, chip_arr, gw_1, landed_1)
    (dq, dk, dv), (landed_1, *landed) = _attention_backward(
        q, k, k_t, v, d_cat, lse, delta, _merge(_chip_exchange([travels_1]), _pair_exchange(later)))
    mine_1 = _add_chips(own_1, landed_1)
    chip_sums = [_add_pair(core_arr, chip_arr, g, l) for g, l in zip(later, landed)]
    (d_hq, d_hf, d_hi, d_hg, hg_sums), (theirs_1, *landed) = _hgrn_backward(
        zhg, hg_lower_bounds, hg_norm_w, o_pre, d_cat, states,
        _merge(_pair_send([mine_1]), _chip_exchange([b for _, b in chip_sums])))
    later_mine = [_add_chips(own, l) for (own, _), l in zip(chip_sums, landed)]
    mlp_mine = [mine_1] + later_mine
    (dz, grad_x, in_sums, gw_q, gw_kv), _ = _in_project_backward(
        dq, dk, dv, cq, ckv, pos, invf, mla_q_norm_w, mla_kv_norm_w, w_q_full, g_kv,
        d_hq, d_hf, d_hi, d_hg, w_in_full, dr1, xs, sc_a)

    zeros = lambda n: jnp.zeros((1, n), F32)
    small = jnp.concatenate([
        in_sums[1:2], in_sums[0:1], mlp_sums[S_DGA:S_DGA + 1],
        mlp_sums[S_DSHM:S_DSHM + 1], mlp_sums[S_DSCM:S_DSCM + 1], mlp_sums[S_DGM:S_DGM + 1],
        mlp_sums[S_DLN1G:S_DLN1G + 1], mlp_sums[S_DLN1B:S_DLN1B + 1],
        mlp_sums[S_DLN2G:S_DLN2G + 1], mlp_sums[S_DLN2B:S_DLN2B + 1],
        jnp.concatenate([hg_sums[0:1], hg_sums[1:2]], axis=1),
        jnp.concatenate([in_sums[2:3, :Q_RANK], in_sums[3:4, :KV_RANK], zeros(D_MODEL - Q_RANK - KV_RANK)], axis=1),
        mlp_sums[S_LOSS:S_LOSS + 1],
        jnp.zeros((SMALL_ROWS - 13, D_MODEL), F32)], axis=0)

    gw_in, (*later_theirs, small_all) = _weight_grad(
        dz, u_a, "grad_w_in", 3, D_MODEL, a_blocked=True, b_blocked=False, token_tile=4096,
        exchange=_merge(_pair_send(later_mine), _gather_all(slot8(small))))
    mlp_theirs = [theirs_1] + list(later_theirs)
    gw_in = gw_in.reshape(IN_COLS_PAD, D_MODEL)
    gw_q = gw_q[:, :, :HEAD_DIM + ROPE_DIM]
    flat = lambda g: g.reshape(g.shape[0] * g.shape[1], g.shape[2])
    mixer_mine, mixer_theirs = _reduce_in_vmem(
        [gw_in, flat(gw_q), flat(gw_kv)], [IN_COLS // N_CHIPS // 2, Q_RANK // 2, KV_RANK // 2], "reduce_mixer_grads")
    reduced = ("w_in", "w_q_up", "w_kv_up", "w_mlp_in", "w_mlp_out", "w_out")
    halves_mine = dict(zip(reduced, list(mixer_mine) + mlp_mine))
    halves_theirs = dict(zip(reduced, list(mixer_theirs) + list(mlp_theirs)))

    small_names = ("b_ada", "hg_lower_bounds", "hg_norm_w", "mla_q_norm_w", "mla_kv_norm_w",
                   "ln1_g", "ln1_b", "ln2_g", "ln2_b")
    loss_row, small_out = _small_params_step(small_all, {
        "b_ada": (b_ada, m_b_ada, v_b_ada),
        "hg_lower_bounds": (hg_lower_bounds, m_hg_lower_bounds, v_hg_lower_bounds),
        "hg_norm_w": (hg_norm_w, m_hg_norm_w, v_hg_norm_w),
        "mla_q_norm_w": (mla_q_norm_w, m_mla_q_norm_w, v_mla_q_norm_w),
        "mla_kv_norm_w": (mla_kv_norm_w, m_mla_kv_norm_w, v_mla_kv_norm_w),
        "ln1_g": (ln1_g, m_ln1_g, v_ln1_g), "ln1_b": (ln1_b, m_ln1_b, v_ln1_b),
        "ln2_g": (ln2_g, m_ln2_g, v_ln2_g), "ln2_b": (ln2_b, m_ln2_b, v_ln2_b)})
    loss = loss_row[0, 0]

    d_mod_all = small_all[:, 0:6, :].reshape(N_DEV, 6 * D_MODEL)
    d_mod_cols = lax.dynamic_slice(d_mod_all, (0, chip * ada_cols), (N_DEV, ada_cols))
    d_mod_cols = jnp.concatenate([d_mod_cols, jnp.zeros_like(d_mod_cols)], axis=0)
    g_w_ada = _weight_grad(cond16, d_mod_cols, "grad_w_ada", 1, ada_cols)[0]

    names = ["w_ada", "b_ada", "w_in", "hg_lower_bounds", "hg_norm_w", "mla_q_norm_w", "w_q_up", "mla_kv_norm_w",
             "w_kv_up", "w_out", "ln1_g", "ln1_b", "w_mlp_in", "w_mlp_out", "ln2_g", "ln2_b"]
    weights = [w_ada, b_ada, w_in, hg_lower_bounds, hg_norm_w, mla_q_norm_w, w_q_up, mla_kv_norm_w,
               w_kv_up, w_out, ln1_g, ln1_b, w_mlp_in, w_mlp_out, ln2_g, ln2_b]
    moms = [m_w_ada, m_b_ada, m_w_in, m_hg_lower_bounds, m_hg_norm_w, m_mla_q_norm_w, m_w_q_up, m_mla_kv_norm_w,
            m_w_kv_up, m_w_out, m_ln1_g, m_ln1_b, m_w_mlp_in, m_w_mlp_out, m_ln2_g, m_ln2_b]
    vels = [v_w_ada, v_b_ada, v_w_in, v_hg_lower_bounds, v_hg_norm_w, v_mla_q_norm_w, v_w_q_up, v_mla_kv_norm_w,
            v_w_kv_up, v_w_out, v_ln1_g, v_ln1_b, v_w_mlp_in, v_w_mlp_out, v_ln2_g, v_ln2_b]
    out_g, out_d, out_m, out_v = [], [], [], []
    for name, w, m, vv in zip(names, weights, moms, vels):
        if name in small_names:
            g, d, nm, nv = small_out[name]
            back = lambda a: a
        elif name == "w_in":
            to2d, back = (lambda a: jnp.transpose(a[0])), (lambda a: jnp.transpose(a)[None])
        else:
            to2d, back = (lambda a, s=w.shape[1:]: a.reshape(s)), (lambda a, s=w.shape: a.reshape(s))
        if name == "w_ada":
            d, nm, nv = _adamw(to2d(w), g_w_ada, to2d(m), to2d(vv), "adamw_" + name)
            g = g_w_ada
        elif name not in small_names:
            g, d, nm, nv = _adamw_halves(core_arr, to2d(w), halves_mine[name], halves_theirs[name], to2d(m), to2d(vv),
                                         "adamw_" + name)
        out_g.append(back(g))
        out_d.append(back(d))
        out_m.append(back(nm))
        out_v.append(back(nv))
    return (loss, grad_x[None], *out_g, *out_d, *out_m, *out_v)
```

```python
import functools

import jax
import jax.numpy as jnp
from jax import lax
from jax.experimental import pallas as pl
from jax.experimental.pallas import tpu as pltpu

F32 = jnp.float32
BF16 = jnp.bfloat16
MESH_IDS = pl.DeviceIdType.MESH

D_MODEL = 1024
N_HEADS = 4
HEAD_DIM = 128
ROPE_DIM = 64
HG_CHUNK = 64
HG_COLS = 2048
Q_RANK = 256
KV_RANK = 256
IN_COLS = 2624
IN_COLS_PAD = 2688
QK_DIM = 256
D_FF = 4096
N_CHIPS = 4
N_DEV = 8
ROPE_THETA = 10000.0
RMS_EPS = 1e-6
LN_EPS = 1e-5
DN_ALPHA = 2.0 ** 0.25
ATT_SCALE = (HEAD_DIM + ROPE_DIM) ** -0.5
NEG_BIG = -1e30
ADAM_LR = 0.001
ADAM_B1 = 0.9
ADAM_B2 = 0.999
ADAM_EPS = 1e-08
ADAM_WD = 0.01
ADAM_STEP = 10
SMALL_ROWS = 16
MIB = 1024 * 1024


def _dot(a, b):
    return jnp.dot(a, b, preferred_element_type=F32)


def _dot_nt(a, b):
    return lax.dot_general(a, b, (((1,), (1,)), ((), ())), preferred_element_type=F32)


def _dot_tn(a, b):
    return lax.dot_general(a, b, (((0,), (0,)), ((), ())), preferred_element_type=F32)


def _params(vmem_mib, semantics=None):
    return pltpu.CompilerParams(vmem_limit_bytes=vmem_mib * MIB, dimension_semantics=semantics)


def _sigmoid(v):
    return 1.0 / (1.0 + jnp.exp(-v))


def _colsum(v):
    return jnp.sum(v, axis=0, keepdims=True)


def _rowmean(v):
    return jnp.mean(v, axis=-1, keepdims=True)


def _rope_tables(pos, invf):
    ang = pos * invf
    lane = lax.broadcasted_iota(jnp.int32, ang.shape, 1)
    cos_t = jnp.where(lane < ROPE_DIM, jnp.cos(ang), 0.0)
    sin = jnp.sin(ang)
    sin_t = jnp.where(lane < ROPE_DIM // 2, -sin, jnp.where(lane < ROPE_DIM, sin, 0.0))
    return cos_t, sin_t


def _swap_halves(t):
    lane = lax.broadcasted_iota(jnp.int32, t.shape, 1)
    return jnp.where(lane < ROPE_DIM // 2, pltpu.roll(t, 128 - ROPE_DIM // 2, 1), pltpu.roll(t, ROPE_DIM // 2, 1))


def _rope(t, cos_t, sin_t):
    return t * cos_t + _swap_halves(t) * sin_t


def _unrope(g, cos_t, sin_t):
    return g * cos_t - _swap_halves(g) * sin_t


def _mesh_pos():
    return lax.axis_index("x"), lax.axis_index("y"), lax.axis_index("c")


def _other_chips(x, y):
    out = []
    for dx, dy in ((1, 0), (0, 1), (1, 1)):
        px = 1 - x if dx else x
        py = 1 - y if dy else y
        out.append(((px, py), 2 * px + py))
    return out


class _Exchange:
    def __init__(self, inputs, out_shapes, aliases, sems, start, finish):
        self.inputs, self.out_shapes, self.aliases, self.sems = list(inputs), list(out_shapes), dict(aliases), list(sems)
        self.start, self.finish = start, finish


def _from_copies(inputs, out_shapes, aliases, sems, copies):
    def start(ins, outs, sem_refs):
        for send, _ in copies(ins, outs, sem_refs):
            send.start()

    def finish(ins, outs, sem_refs):
        for send, recv in copies(ins, outs, sem_refs):
            recv.wait_recv()
            send.wait_send()

    return _Exchange(inputs, out_shapes, aliases, sems, start, finish)


HBM_MIN_BYTES = 256 * 1024


def _in_hbm(a):
    if a.size * a.dtype.itemsize < HBM_MIN_BYTES:
        return a
    return pltpu.with_memory_space_constraint(a, pltpu.HBM)


def _out_hbm(s):
    if s.size * s.dtype.itemsize < HBM_MIN_BYTES:
        return s
    return pltpu.HBM(s.shape, s.dtype)


def _pcall(body, *, operands, out_shape, **kwargs):
    single = not isinstance(out_shape, (list, tuple))
    shapes = [_out_hbm(s) for s in ([out_shape] if single else out_shape)]
    return pl.pallas_call(body, out_shape=shapes[0] if single else shapes, **kwargs)(*[_in_hbm(a) for a in operands])


def _run_exchange(exchange, name):
    n_in, n_out = len(exchange.inputs), len(exchange.out_shapes)

    def body(*refs):
        ins, outs, sem_refs = refs[:n_in], refs[n_in:n_in + n_out], refs[n_in + n_out:]
        exchange.start(ins, outs, sem_refs)
        exchange.finish(ins, outs, sem_refs)

    any_spec = pl.BlockSpec(memory_space=pl.ANY)
    return pl.pallas_call(
        body, name=name, out_shape=[_out_hbm(s) for s in exchange.out_shapes],
        in_specs=[any_spec] * n_in, out_specs=[any_spec] * n_out,
        scratch_shapes=exchange.sems, input_output_aliases=exchange.aliases,
    )(*[_in_hbm(a) for a in exchange.inputs])


def _pallas(body, *, name, operands, in_specs, out_shape, out_specs, params, scratch_shapes=(), grid=(), prefetch=(),
            exchange=None, first=None, last=None):
    n_pre, n_in, n_out, n_scr = len(prefetch), len(in_specs), len(out_specs), len(scratch_shapes)
    ex_in = exchange.inputs if exchange else []
    ex_out = exchange.out_shapes if exchange else []
    ex_sems = exchange.sems if exchange else []

    def full_body(*refs):
        pre, rest = refs[:n_pre], refs[n_pre:]
        ins, rest = rest[:n_in], rest[n_in:]
        xin, rest = rest[:len(ex_in)], rest[len(ex_in):]
        outs, rest = rest[:n_out], rest[n_out:]
        xout, rest = rest[:len(ex_out)], rest[len(ex_out):]
        scr, sem_refs = rest[:n_scr], rest[n_scr:]
        if exchange:
            @pl.when(first(*pre))
            def _():
                exchange.start(xin, xout, sem_refs)

        body(*pre, *ins, *outs, *scr)
        if exchange:
            @pl.when(last(*pre))
            def _():
                exchange.finish(xin, xout, sem_refs)

    any_spec = pl.BlockSpec(memory_space=pl.ANY)
    aliases = {n_pre + n_in + i: n_out + o for i, o in exchange.aliases.items()} if exchange else {}
    operands = [_in_hbm(a) for a in operands]
    results = pl.pallas_call(
        full_body, name=name, out_shape=[_out_hbm(s) for s in list(out_shape) + ex_out],
        grid_spec=pltpu.PrefetchScalarGridSpec(
            num_scalar_prefetch=n_pre, grid=grid, in_specs=list(in_specs) + [any_spec] * len(ex_in),
            out_specs=list(out_specs) + [any_spec] * len(ex_out), scratch_shapes=list(scratch_shapes) + ex_sems),
        input_output_aliases=aliases, compiler_params=params,
    )(*prefetch, *operands, *[_in_hbm(a) for a in ex_in])
    return results[:n_out], results[n_out:]


def _remote(src, dst, sems, idx, to):
    send_sems, recv_sems = sems
    return pltpu.make_async_remote_copy(src_ref=src, dst_ref=dst, send_sem=send_sems.at[idx], recv_sem=recv_sems.at[idx],
                                        device_id=to, device_id_type=MESH_IDS)


def _sem_pairs(*shape):
    return [pltpu.SemaphoreType.DMA(shape), pltpu.SemaphoreType.DMA(shape)]


def _same_shapes(arrays):
    return [jax.ShapeDtypeStruct(a.shape, a.dtype) for a in arrays]


def _gather_over_ici(slots):
    n = len(slots)

    def copies(ins, outs, sems):
        x, y, c = _mesh_pos()
        k = 2 * x + y
        out = []
        for j, (chip, kj) in enumerate(_other_chips(x, y)):
            for i in range(n):
                to = (*chip, c)
                out.append((_remote(ins[i].at[k, c], outs[i].at[k, c], sems, (j, i), to),
                            _remote(ins[i].at[k, c], outs[i].at[kj, c], sems, (j, i), to)))
        return out

    return _from_copies(slots, _same_shapes(slots), {i: i for i in range(n)}, _sem_pairs(3, n), copies)


def _gather_over_d2d(slots):
    n = len(slots)

    def copies(ins, outs, sems):
        x, y, c = _mesh_pos()
        sibling = (x, y, 1 - c)
        out = []
        for j, (_, kj) in enumerate(_other_chips(x, y)):
            for i in range(n):
                out.append((_remote(ins[i].at[kj, c], outs[i].at[kj, c], sems, (j, i), sibling),
                            _remote(ins[i].at[kj, c], outs[i].at[kj, 1 - c], sems, (j, i), sibling)))
        return out

    return _from_copies(slots, _same_shapes(slots), {i: i for i in range(n)}, _sem_pairs(3, n), copies)


def _gather_all(slots8):
    def copies(ins, outs, sems):
        x, y, c = _mesh_pos()
        me = 4 * x + 2 * y + c
        out = []
        for r in range(1, N_DEV):
            px = 1 - x if r & 4 else x
            py = 1 - y if r & 2 else y
            pc = 1 - c if r & 1 else c
            to = (px, py, pc)
            out.append((_remote(ins[0].at[me], outs[0].at[me], sems, r - 1, to),
                        _remote(ins[0].at[me], outs[0].at[4 * px + 2 * py + pc], sems, r - 1, to)))
        return out

    return _from_copies([slots8], _same_shapes([slots8]), {0: 0}, _sem_pairs(N_DEV - 1), copies)


def _merge(first, second):
    n_in, n_out, n_sem = len(first.inputs), len(first.out_shapes), len(first.sems)

    def start(ins, outs, sems):
        first.start(ins[:n_in], outs[:n_out], sems[:n_sem])
        second.start(ins[n_in:], outs[n_out:], sems[n_sem:])

    def finish(ins, outs, sems):
        first.finish(ins[:n_in], outs[:n_out], sems[:n_sem])
        second.finish(ins[n_in:], outs[n_out:], sems[n_sem:])

    aliases = dict(first.aliases)
    aliases.update({n_in + i: n_out + o for i, o in second.aliases.items()})
    return _Exchange(first.inputs + second.inputs, first.out_shapes + second.out_shapes, aliases,
                     first.sems + second.sems, start, finish)


def _pair_exchange(grads):
    n = len(grads)

    def copies(ins, outs, sems):
        x, y, c = _mesh_pos()
        cps = [_remote(ins[i].at[:, 1 - c], outs[i], sems, i, (x, y, 1 - c)) for i in range(n)]
        return [(cp, cp) for cp in cps]

    shapes = [jax.ShapeDtypeStruct((N_CHIPS,) + g.shape[2:], g.dtype) for g in grads]
    return _from_copies(grads, shapes, {}, _sem_pairs(n), copies)


def _chip_exchange(partials):
    n = len(partials)

    def copies(ins, outs, sems):
        x, y, c = _mesh_pos()
        cps = [_remote(ins[i].at[kj], outs[i].at[j], sems, (j, i), (*chip, c))
               for j, (chip, kj) in enumerate(_other_chips(x, y)) for i in range(n)]
        return [(cp, cp) for cp in cps]

    shapes = [jax.ShapeDtypeStruct((3,) + p.shape[1:], p.dtype) for p in partials]
    return _from_copies(partials, shapes, {}, _sem_pairs(3, n), copies)


def _pair_send(halves):
    n = len(halves)

    def copies(ins, outs, sems):
        x, y, c = _mesh_pos()
        cps = [_remote(ins[i], outs[i], sems, i, (x, y, 1 - c)) for i in range(n)]
        return [(cp, cp) for cp in cps]

    return _from_copies(halves, _same_shapes(halves), {}, _sem_pairs(n), copies)


def _reduce_in_vmem(grads, half_rows, name):
    n = len(grads)

    def body(*refs):
        g, mine, theirs = refs[:n], refs[n:2 * n], refs[2 * n:3 * n]
        landed_pair, partial, landed_chips = refs[3 * n:4 * n], refs[4 * n:5 * n], refs[5 * n:6 * n]
        sems = refs[6 * n:]
        x, y, c = _mesh_pos()
        k = 2 * x + y
        sibling = (x, y, 1 - c)

        def half(i, chip_idx, which):
            return pl.ds(pl.multiple_of((2 * chip_idx + which) * half_rows[i], 8), half_rows[i])

        def run(copies):
            for cp in copies:
                cp.start()
            for cp in copies:
                cp.wait_recv()
                cp.wait_send()

        run([_remote(g[i].at[half(i, kk, 1 - c)], landed_pair[i].at[kk], sems[0:2], (kk, i), sibling)
             for kk in range(N_CHIPS) for i in range(n)])
        for i in range(n):
            for kk in range(N_CHIPS):
                partial[i][kk] = (g[i][half(i, kk, c), :] + landed_pair[i][kk]).astype(BF16)
        run([_remote(partial[i].at[kj], landed_chips[i].at[j], sems[2:4], (j, i), (*chip, c))
             for j, (chip, kj) in enumerate(_other_chips(x, y)) for i in range(n)])
        for i in range(n):
            own = g[i][half(i, k, c), :] + landed_pair[i][k]
            mine[i][...] = ((own + landed_chips[i][0].astype(F32)) + landed_chips[i][1].astype(F32)) \
                + landed_chips[i][2].astype(F32)
        run([_remote(mine[i], theirs[i], sems[4:6], i, sibling) for i in range(n)])

    shapes = [(h, gr.shape[1]) for gr, h in zip(grads, half_rows)]
    halves = [jax.ShapeDtypeStruct(s, F32) for s in shapes]
    vmem = pl.BlockSpec(memory_space=pltpu.VMEM)
    scratch = ([pltpu.VMEM((N_CHIPS,) + s, F32) for s in shapes]
               + [pltpu.VMEM((N_CHIPS,) + s, BF16) for s in shapes]
               + [pltpu.VMEM((3,) + s, BF16) for s in shapes]
               + _sem_pairs(N_CHIPS, n) + _sem_pairs(3, n) + _sem_pairs(n))
    out = pl.pallas_call(
        body, name=name, out_shape=halves + halves, in_specs=[vmem] * n, out_specs=[vmem] * (2 * n),
        scratch_shapes=scratch, compiler_params=_params(48),
    )(*grads)
    return out[:n], out[n:]


def _row_tile(rows):
    for t in (256, 128, 64):
        if rows % t == 0:
            return t
    return rows


def _add_pair(core, chip, grad, landed):
    _, h, cols = landed.shape
    tr = _row_tile(h)

    def body(core_ref, chip_ref, g_ref, l_ref, own_ref, ob_ref):
        s = g_ref[...] + l_ref[...]
        ob_ref[...] = s.astype(BF16)

        @pl.when(pl.program_id(1) == chip_ref[0])
        def _():
            own_ref[...] = s

    return _pcall(
        body, name="grad_add_pair",
        out_shape=[jax.ShapeDtypeStruct((h, cols), F32), jax.ShapeDtypeStruct(landed.shape, BF16)],
        grid_spec=pltpu.PrefetchScalarGridSpec(
            num_scalar_prefetch=2, grid=(h // tr, N_CHIPS),
            in_specs=[pl.BlockSpec((None, None, tr, cols), lambda t, k, core_ref, chip_ref: (k, core_ref[0], t, 0)),
                      pl.BlockSpec((None, tr, cols), lambda t, k, core_ref, chip_ref: (k, t, 0))],
            out_specs=[pl.BlockSpec((tr, cols), lambda t, k, core_ref, chip_ref: (t, 0)),
                       pl.BlockSpec((None, tr, cols), lambda t, k, core_ref, chip_ref: (k, t, 0))]),
        compiler_params=_params(32, ("arbitrary", "arbitrary")),
        operands=(core, chip, grad, landed))


def _add_chips(own, landed):
    h, cols = own.shape
    tr = _row_tile(h)

    def body(p_ref, l_ref, o_ref):
        o_ref[...] = ((p_ref[...] + l_ref[0].astype(F32)) + l_ref[1].astype(F32)) + l_ref[2].astype(F32)

    return _pcall(
        body, name="grad_add_chips", grid=(h // tr,),
        out_shape=jax.ShapeDtypeStruct((h, cols), F32),
        in_specs=[pl.BlockSpec((tr, cols), lambda t: (t, 0)), pl.BlockSpec((3, tr, cols), lambda t: (0, t, 0))],
        out_specs=pl.BlockSpec((tr, cols), lambda t: (t, 0)),
        compiler_params=_params(32, ("arbitrary",)),
        operands=(own, landed))


def _ada_project(c_all, w_ada, b_shard):
    n = w_ada.shape[1]
    tn = 512

    def body(c_ref, w_ref, b_ref, mod_ref, cond_ref):
        cv = c_ref[...]
        cond = cv * _sigmoid(cv)
        mod_ref[...] = _dot(cond.astype(BF16), w_ref[...].astype(BF16)) + b_ref[...]
        cond_ref[0:N_DEV, :] = cond
        cond_ref[N_DEV:2 * N_DEV, :] = jnp.zeros_like(cond)

    return _pcall(
        body, name="ada_project", grid=(n // tn,),
        out_shape=[jax.ShapeDtypeStruct((N_DEV, n), F32), jax.ShapeDtypeStruct((2 * N_DEV, D_MODEL), F32)],
        in_specs=[pl.BlockSpec((N_DEV, D_MODEL), lambda j: (0, 0)), pl.BlockSpec((D_MODEL, tn), lambda j: (0, j)),
                  pl.BlockSpec((1, tn), lambda j: (0, j))],
        out_specs=[pl.BlockSpec((N_DEV, tn), lambda j: (0, j)), pl.BlockSpec((2 * N_DEV, D_MODEL), lambda j: (0, 0))],
        compiler_params=_params(32, ("arbitrary",)),
        operands=(c_all, w_ada, b_shard))


def _in_project(x, pos, sc_a, sh_a, w_in, q_norm_w, kv_norm_w, w_q, w_kv, invf, exchange=None):
    t_len = x.shape[0]
    tm = min(512, t_len)

    def body(x_ref, pos_ref, sc_ref, sh_ref, win_ref, qn_ref, kvn_ref, wq_ref, wkv_ref, invf_ref,
             u_ref, zhg_ref, cq_ref, ckv_ref, q_ref, k_ref, kt_ref, v_ref, vt_ref):
        u = (x_ref[...] * (1.0 + sc_ref[...]) + sh_ref[...]).astype(BF16)
        u_ref[...] = u
        z = _dot_nt(u, win_ref[...])
        zhg_ref[...] = z[:, :HG_COLS]
        cq = z[:, HG_COLS:HG_COLS + Q_RANK]
        ckv = z[:, HG_COLS + Q_RANK:HG_COLS + Q_RANK + KV_RANK]
        cq_ref[...] = cq
        ckv_ref[...] = ckv
        cos_t, sin_t = _rope_tables(pos_ref[...], invf_ref[...])
        k_pe = _rope(z[:, HG_COLS + Q_RANK + KV_RANK:], cos_t, sin_t)
        k_pe_t = jnp.transpose(k_pe).astype(BF16)
        cqn = (cq * lax.rsqrt(_rowmean(cq * cq) + RMS_EPS) * qn_ref[...]).astype(BF16)
        ckvn = (ckv * lax.rsqrt(_rowmean(ckv * ckv) + RMS_EPS) * kvn_ref[...]).astype(BF16)
        for h in range(N_HEADS):
            qh = _dot(cqn, wq_ref[h])
            q_ref[h, :, 0:HEAD_DIM] = qh[:, :HEAD_DIM].astype(BF16)
            q_ref[h, :, HEAD_DIM:QK_DIM] = _rope(qh[:, HEAD_DIM:], cos_t, sin_t).astype(BF16)
            kvh = _dot(ckvn, wkv_ref[h])
            k_ref[h, :, 0:HEAD_DIM] = kvh[:, :HEAD_DIM].astype(BF16)
            k_ref[h, :, HEAD_DIM:QK_DIM] = k_pe.astype(BF16)
            kt_ref[h, 0:HEAD_DIM, :] = jnp.transpose(kvh[:, :HEAD_DIM]).astype(BF16)
            kt_ref[h, HEAD_DIM:QK_DIM, :] = k_pe_t
            v_ref[h] = kvh[:, HEAD_DIM:].astype(BF16)
            vt_ref[h] = jnp.transpose(kvh[:, HEAD_DIM:]).astype(BF16)

    row = lambda i: (i, 0)
    fixed2 = lambda i: (0, 0)
    fixed3 = lambda i: (0, 0, 0)
    heads = lambda i: (0, i, 0)
    n_tiles = t_len // tm
    return _pallas(
        body, name="in_project", grid=(n_tiles,),
        operands=(x, pos, sc_a, sh_a, w_in, q_norm_w, kv_norm_w, w_q, w_kv, invf),
        out_shape=[jax.ShapeDtypeStruct((t_len, D_MODEL), BF16), jax.ShapeDtypeStruct((t_len, HG_COLS), F32),
                   jax.ShapeDtypeStruct((t_len, Q_RANK), F32), jax.ShapeDtypeStruct((t_len, KV_RANK), F32),
                   jax.ShapeDtypeStruct((N_HEADS, t_len, QK_DIM), BF16),
                   jax.ShapeDtypeStruct((N_HEADS, t_len, QK_DIM), BF16),
                   jax.ShapeDtypeStruct((N_HEADS, QK_DIM, t_len), BF16),
                   jax.ShapeDtypeStruct((N_HEADS, t_len, HEAD_DIM), BF16),
                   jax.ShapeDtypeStruct((N_HEADS, HEAD_DIM, t_len), BF16)],
        in_specs=[pl.BlockSpec((tm, D_MODEL), row), pl.BlockSpec((tm, 1), row),
                  pl.BlockSpec((1, D_MODEL), fixed2), pl.BlockSpec((1, D_MODEL), fixed2),
                  pl.BlockSpec((IN_COLS_PAD, D_MODEL), fixed2),
                  pl.BlockSpec((1, Q_RANK), fixed2), pl.BlockSpec((1, KV_RANK), fixed2),
                  pl.BlockSpec((N_HEADS, Q_RANK, QK_DIM), fixed3), pl.BlockSpec((N_HEADS, KV_RANK, 2 * HEAD_DIM), fixed3),
                  pl.BlockSpec((1, 128), fixed2)],
        out_specs=[pl.BlockSpec((tm, D_MODEL), row), pl.BlockSpec((tm, HG_COLS), row),
                   pl.BlockSpec((tm, Q_RANK), row), pl.BlockSpec((tm, KV_RANK), row),
                   pl.BlockSpec((N_HEADS, tm, QK_DIM), heads), pl.BlockSpec((N_HEADS, tm, QK_DIM), heads),
                   pl.BlockSpec((N_HEADS, QK_DIM, tm), lambda i: (0, 0, i)),
                   pl.BlockSpec((N_HEADS, tm, HEAD_DIM), heads),
                   pl.BlockSpec((N_HEADS, HEAD_DIM, tm), lambda i: (0, 0, i))],
        params=_params(48, ("arbitrary",)), exchange=exchange,
        first=lambda: pl.program_id(0) == 0, last=lambda: pl.program_id(0) == n_tiles - 1)


def _lower_bound(lb_raw):
    m = jnp.max(lb_raw, axis=0, keepdims=True)
    e = jnp.exp(lb_raw - m)
    return e[0:1] / jnp.sum(e, axis=0, keepdims=True)


def _tri(inclusive_lower):
    r = lax.broadcasted_iota(jnp.int32, (HG_CHUNK, HG_CHUNK), 0)
    c = lax.broadcasted_iota(jnp.int32, (HG_CHUNK, HG_CHUNK), 1)
    return (c <= r) if inclusive_lower else (c >= r)


def _chunk_rows(n):
    return slice(n * HG_CHUNK, (n + 1) * HG_CHUNK)


def _chunk_prefix_sums(v, inclusive_lower):
    tri = _tri(inclusive_lower).astype(BF16)
    hi = v.astype(BF16)
    rest = v - hi.astype(F32)
    mid = rest.astype(BF16)
    lo = (rest - mid.astype(F32)).astype(BF16)
    pieces = jnp.concatenate([hi, mid, lo], axis=1)
    out = []
    for n in range(v.shape[0] // HG_CHUNK):
        s = _dot(tri, pieces[_chunk_rows(n)])
        out.append((s[:, 0:HEAD_DIM] + s[:, HEAD_DIM:2 * HEAD_DIM]) + s[:, 2 * HEAD_DIM:])
    return jnp.concatenate(out, axis=0)


def _per_chunk(v, row):
    n = v.shape[0] // HG_CHUNK
    v3 = v.reshape(n, HG_CHUNK, HEAD_DIM)
    return jnp.broadcast_to(v3[:, row:row + 1, :], v3.shape).reshape(v.shape)


def _hg_block(q, f_logit, lb):
    sg = _sigmoid(f_logit)
    forget = lb + (1.0 - lb) * sg
    kk = 1.0 - forget
    b = _chunk_prefix_sums(jnp.log(forget), True)
    b_ref = _per_chunk(b, HG_CHUNK // 2 - 1)
    b_last = _per_chunk(b, HG_CHUNK - 1)
    e_i = jnp.exp(b - b_ref)
    e_ri = jnp.exp(b_ref - b)
    e_b = jnp.exp(b)
    e_l = jnp.exp(b_last - b)
    return dict(sg=sg, forget=forget, e_i=e_i, e_ri=e_ri, e_b=e_b, e_l=e_l, dec=jnp.exp(b_last),
                qi=q * e_i, ki=kk * e_ri, qe=q * e_b, kl=kk * e_l)


HG_STEP_HEADS = 4


def _head_cols(hh):
    return slice(hh * HEAD_DIM, (hh + 1) * HEAD_DIM)


def _hgrn_forward(zhg, lb_raw, norm_w, exchange=None):
    t_len = zhg.shape[0]
    tb = min(1024, t_len)
    n_chunks = tb // HG_CHUNK
    hs = HG_STEP_HEADS

    def body(q_ref, f_ref, v_ref, g_ref, lb_ref, w_ref, opre_ref, o_ref, st_ref, state):
        @pl.when(pl.program_id(1) == 0)
        def _():
            state[...] = jnp.zeros_like(state)

        causal = _tri(True)
        heads = range(hs)
        blk, v, qi, ki, qe, kl = {}, {}, {}, {}, {}, {}
        for hh in heads:
            cols = _head_cols(hh)
            blk[hh] = _hg_block(q_ref[:, cols], f_ref[:, cols], _lower_bound(lb_ref[:, cols]))
            v[hh] = v_ref[:, cols].astype(BF16)
            qi[hh], ki[hh], qe[hh], kl[hh] = (blk[hh][name].astype(BF16) for name in ("qi", "ki", "qe", "kl"))
        st = {hh: state[hh] for hh in heads}
        parts = {hh: [] for hh in heads}
        for n in range(n_chunks):
            r = _chunk_rows(n)
            for hh in heads:
                a = jnp.where(causal, _dot_nt(qi[hh][r], ki[hh][r]), 0.0).astype(BF16)
                st_ref[hh, n] = st[hh]
                parts[hh].append(_dot(a, v[hh][r]) + _dot_nt(qe[hh][r], st[hh].astype(BF16)))
                st[hh] = st[hh] * blk[hh]["dec"][n * HG_CHUNK:n * HG_CHUNK + 1] + _dot_tn(v[hh][r], kl[hh][r])
        for hh in heads:
            cols = _head_cols(hh)
            state[hh] = st[hh]
            o = jnp.concatenate(parts[hh], axis=0)
            opre_ref[:, cols] = o
            g = g_ref[:, cols]
            gated = o * lax.rsqrt(_rowmean(o * o) + RMS_EPS) * w_ref[:, cols] * (g * _sigmoid(g))
            o_ref[:, cols] = gated.astype(BF16)

    groups = N_HEADS // hs
    wide = hs * HEAD_DIM
    col = lambda off: (lambda h, t: (t, off + h))
    nb = t_len // tb
    return _pallas(
        body, name="hgrn_forward", grid=(groups, nb), operands=(zhg, zhg, zhg, zhg, lb_raw, norm_w),
        out_shape=[jax.ShapeDtypeStruct((t_len, N_HEADS * HEAD_DIM), F32),
                   jax.ShapeDtypeStruct((t_len, N_HEADS * HEAD_DIM), BF16),
                   jax.ShapeDtypeStruct((N_HEADS, t_len // HG_CHUNK, HEAD_DIM, HEAD_DIM), F32)],
        in_specs=[pl.BlockSpec((tb, wide), col(0)), pl.BlockSpec((tb, wide), col(groups)),
                  pl.BlockSpec((tb, wide), col(2 * groups)), pl.BlockSpec((tb, wide), col(3 * groups)),
                  pl.BlockSpec((2, wide), lambda h, t: (0, h)), pl.BlockSpec((1, wide), lambda h, t: (0, h))],
        out_specs=[pl.BlockSpec((tb, wide), col(0)), pl.BlockSpec((tb, wide), col(0)),
                   pl.BlockSpec((hs, n_chunks, HEAD_DIM, HEAD_DIM), lambda h, t: (h, t, 0, 0))],
        scratch_shapes=[pltpu.VMEM((hs, HEAD_DIM, HEAD_DIM), F32)],
        params=_params(56, ("arbitrary", "arbitrary")), exchange=exchange,
        first=lambda: (pl.program_id(0) == 0) & (pl.program_id(1) == 0),
        last=lambda: (pl.program_id(0) == groups - 1) & (pl.program_id(1) == nb - 1))


def _hgrn_backward(zhg, lb_raw, norm_w, o_pre, d_cat, states, exchange=None):
    t_len = zhg.shape[0]
    tb = min(1024, t_len)
    n_chunks = tb // HG_CHUNK
    nb = t_len // tb
    hs = HG_STEP_HEADS

    def body(q_ref, f_ref, v_ref, g_ref, lb_ref, w_ref, opre_ref, do_ref, st_ref,
             dq_ref, df_ref, dv_ref, dg_ref, sums_ref, gstate):
        @pl.when(pl.program_id(1) == 0)
        def _():
            gstate[...] = jnp.zeros_like(gstate)
            sums_ref[...] = jnp.zeros_like(sums_ref)

        heads = range(hs)
        causal = _tri(True)
        row_id = lax.broadcasted_iota(jnp.int32, (HG_CHUNK, HEAD_DIM), 0)
        lb, d_o, blk, v, qi, ki, qe, kl = ({} for _ in range(8))
        for hh in heads:
            cols = _head_cols(hh)
            lb[hh] = _lower_bound(lb_ref[:, cols])
            w = w_ref[:, cols]
            o = opre_ref[:, cols]
            g = g_ref[:, cols]
            d_out = do_ref[:, cols]
            r = lax.rsqrt(_rowmean(o * o) + RMS_EPS)
            sg_g = _sigmoid(g)
            dg_ref[:, cols] = (d_out * (o * r * w) * (sg_g * (1.0 + g * (1.0 - sg_g)))).astype(BF16)
            d_on = d_out * (g * sg_g)
            sums_ref[1:2, cols] += _colsum(d_on * o * r)
            dy = d_on * w
            d_o[hh] = (r * dy - o * (r * r * r) * _rowmean(dy * o)).astype(BF16)
            blk[hh] = _hg_block(q_ref[:, cols], f_ref[:, cols], lb[hh])
            v[hh] = v_ref[:, cols].astype(BF16)
            qi[hh], ki[hh], qe[hh], kl[hh] = (blk[hh][name].astype(BF16) for name in ("qi", "ki", "qe", "kl"))
        gt = {hh: gstate[hh] for hh in heads}
        d_v, d_qi, d_ki, d_qe, d_kl, d_dec = ({hh: [None] * n_chunks for hh in heads} for _ in range(6))
        for n in reversed(range(n_chunks)):
            rows = _chunk_rows(n)
            for hh in heads:
                st = st_ref[hh, n]
                a = jnp.where(causal, _dot_nt(qi[hh][rows], ki[hh][rows]), 0.0).astype(BF16)
                d_a = jnp.where(causal, _dot_nt(d_o[hh][rows], v[hh][rows]), 0.0).astype(BF16)
                gt_b = gt[hh].astype(BF16)
                d_v[hh][n] = _dot_tn(a, d_o[hh][rows]) + _dot_nt(kl[hh][rows], gt_b)
                d_qi[hh][n] = _dot(d_a, ki[hh][rows])
                d_ki[hh][n] = _dot_tn(d_a, qi[hh][rows])
                d_qe[hh][n] = _dot(d_o[hh][rows], st.astype(BF16))
                d_kl[hh][n] = _dot(v[hh][rows], gt_b)
                d_dec[hh][n] = jnp.where(row_id == HG_CHUNK - 1, _colsum(gt[hh] * st), 0.0)
                gt[hh] = gt[hh] * blk[hh]["dec"][n * HG_CHUNK:n * HG_CHUNK + 1] + _dot_tn(d_o[hh][rows], qe[hh][rows])
        for hh in heads:
            cols = _head_cols(hh)
            b = blk[hh]
            gstate[hh] = gt[hh]
            dqi, dki, dqe, dkl, ddec = (jnp.concatenate(p[hh], axis=0) for p in (d_qi, d_ki, d_qe, d_kl, d_dec))
            dv_ref[:, cols] = jnp.concatenate(d_v[hh], axis=0).astype(BF16)
            dq_ref[:, cols] = (dqi * b["e_i"] + dqe * b["e_b"]).astype(BF16)
            d_k = dki * b["e_ri"] + dkl * b["e_l"]
            t_qi = dqi * b["qi"]
            t_ki = dki * b["ki"]
            t_kl = dkl * b["kl"]
            at_ref, at_last = [], []
            for n in range(n_chunks):
                rows = _chunk_rows(n)
                at_ref.append(jnp.where(row_id == HG_CHUNK // 2 - 1, _colsum(t_ki[rows] - t_qi[rows]), 0.0))
                at_last.append(jnp.where(row_id == HG_CHUNK - 1, _colsum(t_kl[rows]), 0.0))
            d_b = (t_qi - t_ki + dqe * b["qe"] - t_kl + jnp.concatenate(at_ref, axis=0)
                   + jnp.concatenate(at_last, axis=0) + ddec * b["dec"])
            d_forget = _chunk_prefix_sums(d_b, False) / b["forget"] - d_k
            sg = b["sg"]
            df_ref[:, cols] = (d_forget * (1.0 - lb[hh]) * sg * (1.0 - sg)).astype(BF16)
            sums_ref[0:1, cols] += _colsum(d_forget * (1.0 - sg))

    groups = N_HEADS // hs
    wide = hs * HEAD_DIM
    col = lambda off: (lambda h, t: (nb - 1 - t, off + h))
    return _pallas(
        body, name="hgrn_backward", grid=(groups, nb),
        operands=(zhg, zhg, zhg, zhg, lb_raw, norm_w, o_pre, d_cat, states),
        out_shape=[jax.ShapeDtypeStruct((t_len, N_HEADS * HEAD_DIM), BF16)] * 4
        + [jax.ShapeDtypeStruct((8, N_HEADS * HEAD_DIM), F32)],
        in_specs=[pl.BlockSpec((tb, wide), col(0)), pl.BlockSpec((tb, wide), col(groups)),
                  pl.BlockSpec((tb, wide), col(2 * groups)), pl.BlockSpec((tb, wide), col(3 * groups)),
                  pl.BlockSpec((2, wide), lambda h, t: (0, h)), pl.BlockSpec((1, wide), lambda h, t: (0, h)),
                  pl.BlockSpec((tb, wide), col(0)), pl.BlockSpec((tb, wide), col(0)),
                  pl.BlockSpec((hs, n_chunks, HEAD_DIM, HEAD_DIM), lambda h, t: (h, nb - 1 - t, 0, 0))],
        out_specs=[pl.BlockSpec((tb, wide), col(0))] * 4 + [pl.BlockSpec((8, wide), lambda h, t: (0, h))],
        scratch_shapes=[pltpu.VMEM((hs, HEAD_DIM, HEAD_DIM), F32)],
        params=_params(56, ("arbitrary", "arbitrary")), exchange=exchange,
        first=lambda: (pl.program_id(0) == 0) & (pl.program_id(1) == 0),
        last=lambda: (pl.program_id(0) == groups - 1) & (pl.program_id(1) == nb - 1))


ATT_LOG2 = ATT_SCALE * 1.4426950408889634


def _triangle_steps(nq, q_major):
    if q_major:
        pairs = [(i, j) for i in range(nq) for j in range(i + 1)]
    else:
        pairs = [(i, j) for j in range(nq) for i in range(j, nq)]
    return jnp.array([p[0] for p in pairs], jnp.int32), jnp.array([p[1] for p in pairs], jnp.int32)


def _key_le_query(t):
    return lax.broadcasted_iota(jnp.int32, (t, t), 0) <= lax.broadcasted_iota(jnp.int32, (t, t), 1)


def _attention_forward(q, k, v_t, exchange=None):
    t_len = q.shape[1]
    tq = min(512, t_len)
    nq = t_len // tq
    qi_tab, ki_tab = _triangle_steps(nq, True)

    def body(qi_ref, ki_ref, q_ref, k_ref, vt_ref, o_ref, lse_ref, m_s, l_s, acc_s):
        step = pl.program_id(0)
        qi, ki = qi_ref[step], ki_ref[step]

        @pl.when(ki == 0)
        def _():
            m_s[...] = jnp.full_like(m_s, NEG_BIG)
            l_s[...] = jnp.zeros_like(l_s)
            acc_s[...] = jnp.zeros_like(acc_s)

        def accumulate(masked):
            s_all = [_dot_nt(k_ref[h], q_ref[h]) * ATT_LOG2 for h in range(N_HEADS)]
            for h in range(N_HEADS):
                s_t = s_all[h]
                if masked:
                    s_t = jnp.where(_key_le_query(tq), s_t, NEG_BIG)
                m_old = m_s[h]
                m_new = jnp.maximum(m_old, jnp.max(s_t, axis=0, keepdims=True))
                alpha = jnp.exp2(m_old - m_new)
                p_t = jnp.exp2(s_t - m_new)
                l_s[h] = alpha * l_s[h] + jnp.sum(p_t, axis=0, keepdims=True)
                acc_s[h] = alpha * acc_s[h] + _dot(vt_ref[h], p_t.astype(BF16))
                m_s[h] = m_new

        @pl.when(ki < qi)
        def _():
            accumulate(False)

        @pl.when(ki == qi)
        def _():
            accumulate(True)
            for h in range(N_HEADS):
                o_ref[:, h * HEAD_DIM:(h + 1) * HEAD_DIM] = jnp.transpose(acc_s[h] / l_s[h])
                lse_ref[h] = m_s[h] + jnp.log2(l_s[h])

    n_steps = qi_tab.shape[0]
    return _pallas(
        body, name="attention_forward", grid=(n_steps,), prefetch=(qi_tab, ki_tab), operands=(q, k, v_t),
        out_shape=[jax.ShapeDtypeStruct((t_len, N_HEADS * HEAD_DIM), F32),
                   jax.ShapeDtypeStruct((N_HEADS, 1, t_len), F32)],
        in_specs=[pl.BlockSpec((N_HEADS, tq, QK_DIM), lambda s, qt, kt: (0, qt[s], 0)),
                  pl.BlockSpec((N_HEADS, tq, QK_DIM), lambda s, qt, kt: (0, kt[s], 0)),
                  pl.BlockSpec((N_HEADS, HEAD_DIM, tq), lambda s, qt, kt: (0, 0, kt[s]))],
        out_specs=[pl.BlockSpec((tq, N_HEADS * HEAD_DIM), lambda s, qt, kt: (qt[s], 0)),
                   pl.BlockSpec((N_HEADS, 1, tq), lambda s, qt, kt: (0, 0, qt[s]))],
        scratch_shapes=[pltpu.VMEM((N_HEADS, 1, tq), F32), pltpu.VMEM((N_HEADS, 1, tq), F32),
                        pltpu.VMEM((N_HEADS, HEAD_DIM, tq), F32)],
        params=_params(48, ("arbitrary",)), exchange=exchange,
        first=lambda qt, kt: pl.program_id(0) == 0, last=lambda qt, kt: pl.program_id(0) == n_steps - 1)


BWD_HEADS = 4


def _attention_backward(q, k, k_t, v, d_cat, lse, delta, exchange=None):
    t_len = q.shape[1]
    tq = min(512, t_len)
    nq = t_len // tq
    hp = BWD_HEADS
    qi_tab, ki_tab = _triangle_steps(nq, False)

    def body(qi_ref, ki_ref, q_ref, k_ref, kt_ref, v_ref, do_ref, lse_ref, delta_ref, dqt_hbm, dk_ref, dv_ref,
             dqt_s, dk_s, dv_s):
        group, step = pl.program_id(0), pl.program_id(1)
        qi, ki = qi_ref[step], ki_ref[step]

        @pl.when(step == 0)
        def _():
            dqt_s[...] = jnp.zeros_like(dqt_s)

        @pl.when(qi == ki)
        def _():
            dk_s[...] = jnp.zeros_like(dk_s)
            dv_s[...] = jnp.zeros_like(dv_s)

        def accumulate(masked):
            for h in range(hp):
                do_b = do_ref[:, h * HEAD_DIM:(h + 1) * HEAD_DIM].astype(BF16)
                s_t = _dot_nt(k_ref[h], q_ref[h]) * ATT_LOG2
                if masked:
                    s_t = jnp.where(_key_le_query(tq), s_t, NEG_BIG)
                p_t = jnp.exp2(s_t - lse_ref[h])
                dp_t = _dot_nt(v_ref[h], do_b)
                ds_t = (p_t * (dp_t - delta_ref[h]) * ATT_SCALE).astype(BF16)
                dv_s[h] += _dot(p_t.astype(BF16), do_b)
                dk_s[h] += _dot(ds_t, q_ref[h])
                dqt_s[h, qi] += _dot(kt_ref[h], ds_t)

        @pl.when(ki < qi)
        def _():
            accumulate(False)

        @pl.when(ki == qi)
        def _():
            accumulate(True)
            for h in range(hp):
                pltpu.sync_copy(dqt_s.at[h, qi], dqt_hbm.at[group * hp + h, qi])

        @pl.when(qi == nq - 1)
        def _():
            dk_ref[...] = dk_s[...]
            dv_ref[...] = dv_s[...]

    wide = hp * HEAD_DIM
    n_groups, n_steps = N_HEADS // hp, qi_tab.shape[0]
    return _pallas(
        body, name="attention_backward", grid=(n_groups, n_steps), prefetch=(qi_tab, ki_tab),
        operands=(q, k, k_t, v, d_cat, lse, delta),
        out_shape=[jax.ShapeDtypeStruct((N_HEADS, nq, QK_DIM, tq), F32),
                   jax.ShapeDtypeStruct((N_HEADS, t_len, QK_DIM), F32),
                   jax.ShapeDtypeStruct((N_HEADS, t_len, HEAD_DIM), F32)],
        in_specs=[pl.BlockSpec((hp, tq, QK_DIM), lambda g, s, qt, kt: (g, qt[s], 0)),
                  pl.BlockSpec((hp, tq, QK_DIM), lambda g, s, qt, kt: (g, kt[s], 0)),
                  pl.BlockSpec((hp, QK_DIM, tq), lambda g, s, qt, kt: (g, 0, kt[s])),
                  pl.BlockSpec((hp, tq, HEAD_DIM), lambda g, s, qt, kt: (g, kt[s], 0)),
                  pl.BlockSpec((tq, wide), lambda g, s, qt, kt: (qt[s], n_groups + g)),
                  pl.BlockSpec((hp, 1, tq), lambda g, s, qt, kt: (g, 0, qt[s])),
                  pl.BlockSpec((hp, 1, tq), lambda g, s, qt, kt: (g, 0, qt[s]))],
        out_specs=[pl.BlockSpec(memory_space=pl.ANY),
                   pl.BlockSpec((hp, tq, QK_DIM), lambda g, s, qt, kt: (g, kt[s], 0)),
                   pl.BlockSpec((hp, tq, HEAD_DIM), lambda g, s, qt, kt: (g, kt[s], 0))],
        scratch_shapes=[pltpu.VMEM((hp, nq, QK_DIM, tq), F32), pltpu.VMEM((hp, tq, QK_DIM), F32),
                        pltpu.VMEM((hp, tq, HEAD_DIM), F32)],
        params=_params(58, ("arbitrary", "arbitrary")), exchange=exchange,
        first=lambda qt, kt: (pl.program_id(0) == 0) & (pl.program_id(1) == 0),
        last=lambda qt, kt: (pl.program_id(0) == n_groups - 1) & (pl.program_id(1) == n_steps - 1))


def _out_project(o_hg, o_mla, x, g_a, w_out, exchange=None):
    t_len = x.shape[0]
    tm = min(512, t_len)
    half = N_HEADS * HEAD_DIM

    def body(ohg_ref, omla_ref, x_ref, ga_ref, w_ref, cat_ref, mix_ref, xhat_ref, rstd_ref):
        a = ohg_ref[...].astype(BF16)
        b = omla_ref[...].astype(BF16)
        cat_ref[:, 0:half] = a
        cat_ref[:, half:2 * half] = b
        mix = _dot(a, w_ref[0:half, :]) + _dot(b, w_ref[half:2 * half, :])
        mix_ref[...] = mix
        r1 = DN_ALPHA * x_ref[...] + (1.0 + ga_ref[...]) * mix
        xc = r1 - _rowmean(r1)
        rstd = lax.rsqrt(_rowmean(xc * xc) + LN_EPS)
        xhat_ref[...] = xc * rstd
        rstd_ref[...] = rstd

    row = lambda i: (i, 0)
    fixed = lambda i: (0, 0)
    n_tiles = t_len // tm
    return _pallas(
        body, name="out_project", grid=(n_tiles,), operands=(o_hg, o_mla, x, g_a, w_out),
        out_shape=[jax.ShapeDtypeStruct((t_len, D_MODEL), BF16), jax.ShapeDtypeStruct((t_len, D_MODEL), F32),
                   jax.ShapeDtypeStruct((t_len, D_MODEL), F32), jax.ShapeDtypeStruct((t_len, 1), F32)],
        in_specs=[pl.BlockSpec((tm, half), row), pl.BlockSpec((tm, half), row), pl.BlockSpec((tm, D_MODEL), row),
                  pl.BlockSpec((1, D_MODEL), fixed), pl.BlockSpec((D_MODEL, D_MODEL), fixed)],
        out_specs=[pl.BlockSpec((tm, D_MODEL), row), pl.BlockSpec((tm, D_MODEL), row),
                   pl.BlockSpec((tm, D_MODEL), row), pl.BlockSpec((tm, 1), row)],
        params=_params(48, ("arbitrary",)), exchange=exchange,
        first=lambda: pl.program_id(0) == 0, last=lambda: pl.program_id(0) == n_tiles - 1)


V_LN1G, V_LN1B, V_SCM, V_SHM, V_GM, V_GA, V_LN2G, V_LN2B = range(8)
S_DLN2G, S_DLN2B, S_DGM, S_DSCM, S_DSHM, S_DLN1G, S_DLN1B, S_DGA, S_LOSS = range(9)


def _mlp_and_back(xhat1, rstd1, mix, target, o_mla, vecs, w1_top, w1_bottom, w2, w_out):
    t_len = xhat1.shape[0]
    tm = min(256, t_len)
    n_ff = w1_top.shape[0]
    ff = w1_top.shape[2]
    top_rows = w1_top.shape[1]

    def body(xhat_ref, rstd_ref, mix_ref, tgt_ref, omla_ref, vec_ref, w1_top_hbm, w1_bottom_hbm, w2_hbm, wout_hbm,
             act_ref, dhp_ref, um_ref, dh_ref, dmix_ref, dcat_ref, dr1_ref, sums_ref, delta_ref,
             w1_s, w2_s, wout_s, hp_s, load_sems):
        @pl.when(pl.program_id(0) == 0)
        def _():
            loads = [pltpu.make_async_copy(w1_top_hbm, w1_s.at[:, 0:top_rows], load_sems.at[0]),
                     pltpu.make_async_copy(w1_bottom_hbm, w1_s.at[:, top_rows:D_MODEL], load_sems.at[3]),
                     pltpu.make_async_copy(w2_hbm, w2_s, load_sems.at[1]),
                     pltpu.make_async_copy(wout_hbm, wout_s, load_sems.at[2])]
            for cp in loads:
                cp.start()
            sums_ref[...] = jnp.zeros_like(sums_ref)
            for cp in loads:
                cp.wait()

        vec = lambda r: vec_ref[r:r + 1, :]
        xhat = xhat_ref[...]
        x1 = xhat * vec(V_LN1G) + vec(V_LN1B)
        um = (x1 * (1.0 + vec(V_SCM)) + vec(V_SHM)).astype(BF16)
        um_ref[...] = um
        h = jnp.zeros((tm, D_MODEL), F32)
        for j in range(n_ff):
            hp = _dot(um, w1_s[j])
            hp_s[j] = hp
            act = jnp.square(jnp.maximum(hp, 0.0)).astype(BF16)
            act_ref[:, j * ff:(j + 1) * ff] = act
            h = h + _dot(act, w2_s[j])
        r2 = DN_ALPHA * x1 + (1.0 + vec(V_GM)) * h
        xc = r2 - _rowmean(r2)
        rstd2 = lax.rsqrt(_rowmean(xc * xc) + LN_EPS)
        xhat2 = xc * rstd2
        err = xhat2 * vec(V_LN2G) + vec(V_LN2B) - tgt_ref[...]
        loss = 0.5 * jnp.sum(_rowmean(err * err))
        dy = err * (1.0 / D_MODEL)
        dxh = dy * vec(V_LN2G)
        dr2 = rstd2 * (dxh - _rowmean(dxh) - xhat2 * _rowmean(dxh * xhat2))
        dh = ((1.0 + vec(V_GM)) * dr2).astype(BF16)
        dh_ref[...] = dh
        sums_ref[S_DLN2G:S_DLN2G + 1, :] += _colsum(dy * xhat2)
        sums_ref[S_DLN2B:S_DLN2B + 1, :] += _colsum(dy)
        sums_ref[S_DGM:S_DGM + 1, :] += _colsum(dr2 * h)
        sums_ref[S_LOSS:S_LOSS + 1, :] += jnp.full((1, D_MODEL), loss, F32)
        du = jnp.zeros((tm, D_MODEL), F32)
        for j in range(n_ff):
            dhp = (_dot_nt(dh, w2_s[j]) * (2.0 * jnp.maximum(hp_s[j], 0.0))).astype(BF16)
            dhp_ref[:, j * ff:(j + 1) * ff] = dhp
            du = du + _dot_nt(dhp, w1_s[j])
        sums_ref[S_DSCM:S_DSCM + 1, :] += _colsum(du * x1)
        sums_ref[S_DSHM:S_DSHM + 1, :] += _colsum(du)
        dx1 = DN_ALPHA * dr2 + du * (1.0 + vec(V_SCM))
        sums_ref[S_DLN1G:S_DLN1G + 1, :] += _colsum(dx1 * xhat)
        sums_ref[S_DLN1B:S_DLN1B + 1, :] += _colsum(dx1)
        dxh1 = dx1 * vec(V_LN1G)
        dr1 = rstd_ref[...] * (dxh1 - _rowmean(dxh1) - xhat * _rowmean(dxh1 * xhat))
        dr1_ref[...] = dr1
        sums_ref[S_DGA:S_DGA + 1, :] += _colsum(dr1 * mix_ref[...])
        dmix = ((1.0 + vec(V_GA)) * dr1).astype(BF16)
        dmix_ref[...] = dmix
        dcat = _dot_nt(dmix, wout_s[...])
        dcat_ref[...] = dcat
        half = N_HEADS * HEAD_DIM
        for hd in range(N_HEADS):
            prod = dcat[:, half + hd * HEAD_DIM:half + (hd + 1) * HEAD_DIM] * omla_ref[:, hd * HEAD_DIM:(hd + 1) * HEAD_DIM]
            sums = jnp.broadcast_to(jnp.sum(prod, axis=1, keepdims=True), (tm, HEAD_DIM))
            delta_ref[hd] = jnp.transpose(sums)[0:1]

    row = lambda i: (i, 0)
    fixed = lambda i: (0, 0)
    any_spec = pl.BlockSpec(memory_space=pl.ANY)
    return _pcall(
        body, name="mlp_and_back", grid=(t_len // tm,),
        out_shape=[jax.ShapeDtypeStruct((t_len, D_FF), BF16), jax.ShapeDtypeStruct((t_len, D_FF), BF16),
                   jax.ShapeDtypeStruct((t_len, D_MODEL), BF16), jax.ShapeDtypeStruct((t_len, D_MODEL), BF16),
                   jax.ShapeDtypeStruct((t_len, D_MODEL), BF16), jax.ShapeDtypeStruct((t_len, D_MODEL), F32),
                   jax.ShapeDtypeStruct((t_len, D_MODEL), F32), jax.ShapeDtypeStruct((16, D_MODEL), F32),
                   jax.ShapeDtypeStruct((N_HEADS, 1, t_len), F32)],
        in_specs=[pl.BlockSpec((tm, D_MODEL), row), pl.BlockSpec((tm, 1), row), pl.BlockSpec((tm, D_MODEL), row),
                  pl.BlockSpec((tm, D_MODEL), row), pl.BlockSpec((tm, N_HEADS * HEAD_DIM), row),
                  pl.BlockSpec((8, D_MODEL), fixed), any_spec, any_spec, any_spec, any_spec],
        out_specs=[pl.BlockSpec((tm, D_FF), row), pl.BlockSpec((tm, D_FF), row), pl.BlockSpec((tm, D_MODEL), row),
                   pl.BlockSpec((tm, D_MODEL), row), pl.BlockSpec((tm, D_MODEL), row), pl.BlockSpec((tm, D_MODEL), row),
                   pl.BlockSpec((tm, D_MODEL), row), pl.BlockSpec((16, D_MODEL), fixed),
                   pl.BlockSpec((N_HEADS, 1, tm), lambda i: (0, 0, i))],
        scratch_shapes=[pltpu.VMEM((n_ff, D_MODEL, ff), BF16), pltpu.VMEM(w2.shape, BF16), pltpu.VMEM(w_out.shape, BF16),
                        pltpu.VMEM((n_ff, tm, ff), F32), pltpu.SemaphoreType.DMA((4,))],
        compiler_params=_params(56, ("arbitrary",)),
        operands=(xhat1, rstd1, mix, target, o_mla, vecs, w1_top, w1_bottom, w2, w_out))


def _in_project_backward(dq, dk, dv, cq, ckv, pos, invf, q_norm_w, kv_norm_w, w_q, w_kv,
                         d_hq, d_hf, d_hi, d_hg, w_in, dr1, x, sc_a, exchange=None):
    t_len = x.shape[0]
    tm = min(512, t_len)
    per_q = dq.shape[3] // tm
    hgw = N_HEADS * HEAD_DIM

    def body(dq_ref, dk_ref, dv_ref, cq_ref, ckv_ref, pos_ref, invf_ref, qn_ref, kvn_ref, wq_ref, wkv_ref,
             dhq_ref, dhf_ref, dhi_ref, dhg_ref, win_ref, dr1_ref, x_ref, sc_ref,
             dz_ref, gx_ref, sums_ref, dwq_ref, dwkv_ref):
        @pl.when(pl.program_id(0) == 0)
        def _():
            sums_ref[...] = jnp.zeros_like(sums_ref)
            dwq_ref[...] = jnp.zeros_like(dwq_ref)
            dwkv_ref[...] = jnp.zeros_like(dwkv_ref)

        cos_t, sin_t = _rope_tables(pos_ref[...], invf_ref[...])
        cq = cq_ref[...]
        ckv = ckv_ref[...]
        rq = lax.rsqrt(_rowmean(cq * cq) + RMS_EPS)
        rkv = lax.rsqrt(_rowmean(ckv * ckv) + RMS_EPS)
        cqn = (cq * rq * qn_ref[...]).astype(BF16)
        ckvn = (ckv * rkv * kvn_ref[...]).astype(BF16)
        d_cqn = jnp.zeros((tm, Q_RANK), F32)
        d_ckvn = jnp.zeros((tm, KV_RANK), F32)
        d_kpe = jnp.zeros((tm, 128), F32)
        for h in range(N_HEADS):
            dqh = jnp.transpose(dq_ref[h])
            dq_full = jnp.concatenate(
                [dqh[:, :HEAD_DIM].astype(BF16), _unrope(dqh[:, HEAD_DIM:], cos_t, sin_t).astype(BF16)], axis=1)
            d_cqn = d_cqn + _dot_nt(dq_full, wq_ref[h])
            dwq_ref[h] += _dot_tn(cqn, dq_full)
            dkh = dk_ref[h]
            d_kpe = d_kpe + dkh[:, HEAD_DIM:]
            dkv_up = jnp.concatenate([dkh[:, :HEAD_DIM].astype(BF16), dv_ref[h].astype(BF16)], axis=1)
            d_ckvn = d_ckvn + _dot_nt(dkv_up, wkv_ref[h])
            dwkv_ref[h] += _dot_tn(ckvn, dkv_up)
        dyq = d_cqn * qn_ref[...]
        dykv = d_ckvn * kvn_ref[...]
        sums_ref[2:3, 0:Q_RANK] += _colsum(d_cqn * cq * rq)
        sums_ref[3:4, 0:KV_RANK] += _colsum(d_ckvn * ckv * rkv)
        dz_ref[:, 0:hgw] = dhq_ref[...]
        dz_ref[:, hgw:2 * hgw] = dhf_ref[...]
        dz_ref[:, 2 * hgw:3 * hgw] = dhi_ref[...]
        dz_ref[:, 3 * hgw:4 * hgw] = dhg_ref[...]
        dz_ref[:, HG_COLS:HG_COLS + Q_RANK] = (rq * dyq - cq * (rq * rq * rq) * _rowmean(dyq * cq)).astype(BF16)
        dz_ref[:, HG_COLS + Q_RANK:HG_COLS + Q_RANK + KV_RANK] = (
            rkv * dykv - ckv * (rkv * rkv * rkv) * _rowmean(dykv * ckv)).astype(BF16)
        dz_ref[:, HG_COLS + Q_RANK + KV_RANK:] = _unrope(d_kpe, cos_t, sin_t).astype(BF16)
        du = _dot(dz_ref[...], win_ref[...])
        xv = x_ref[...]
        gx_ref[...] = DN_ALPHA * dr1_ref[...] + (1.0 + sc_ref[...]) * du
        sums_ref[0:1, :] += _colsum(du * xv)
        sums_ref[1:2, :] += _colsum(du)

    row = lambda i: (i, 0)
    fixed2 = lambda i: (0, 0)
    fixed3 = lambda i: (0, 0, 0)
    heads = lambda i: (0, i, 0)
    n_tiles = t_len // tm
    return _pallas(
        body, name="in_project_backward", grid=(n_tiles,),
        operands=(dq, dk, dv, cq, ckv, pos, invf, q_norm_w, kv_norm_w, w_q, w_kv, d_hq, d_hf, d_hi, d_hg, w_in, dr1, x,
                  sc_a),
        out_shape=[jax.ShapeDtypeStruct((t_len, IN_COLS_PAD), BF16), jax.ShapeDtypeStruct((t_len, D_MODEL), F32),
                   jax.ShapeDtypeStruct((8, D_MODEL), F32), jax.ShapeDtypeStruct((N_HEADS, Q_RANK, QK_DIM), F32),
                   jax.ShapeDtypeStruct((N_HEADS, KV_RANK, 2 * HEAD_DIM), F32)],
        in_specs=[pl.BlockSpec((N_HEADS, None, QK_DIM, tm), lambda i: (0, i // per_q, 0, i % per_q)),
                  pl.BlockSpec((N_HEADS, tm, QK_DIM), heads),
                  pl.BlockSpec((N_HEADS, tm, HEAD_DIM), heads), pl.BlockSpec((tm, Q_RANK), row),
                  pl.BlockSpec((tm, KV_RANK), row), pl.BlockSpec((tm, 1), row), pl.BlockSpec((1, 128), fixed2),
                  pl.BlockSpec((1, Q_RANK), fixed2), pl.BlockSpec((1, KV_RANK), fixed2),
                  pl.BlockSpec((N_HEADS, Q_RANK, QK_DIM), fixed3), pl.BlockSpec((N_HEADS, KV_RANK, 2 * HEAD_DIM), fixed3),
                  pl.BlockSpec((tm, hgw), row), pl.BlockSpec((tm, hgw), row), pl.BlockSpec((tm, hgw), row),
                  pl.BlockSpec((tm, hgw), row), pl.BlockSpec((IN_COLS_PAD, D_MODEL), fixed2),
                  pl.BlockSpec((tm, D_MODEL), row), pl.BlockSpec((tm, D_MODEL), row), pl.BlockSpec((1, D_MODEL), fixed2)],
        out_specs=[pl.BlockSpec((tm, IN_COLS_PAD), row), pl.BlockSpec((tm, D_MODEL), row),
                   pl.BlockSpec((8, D_MODEL), fixed2), pl.BlockSpec((N_HEADS, Q_RANK, QK_DIM), fixed3),
                   pl.BlockSpec((N_HEADS, KV_RANK, 2 * HEAD_DIM), fixed3)],
        params=_params(48, ("arbitrary",)), exchange=exchange,
        first=lambda: pl.program_id(0) == 0, last=lambda: pl.program_id(0) == n_tiles - 1)


def _weight_grad(a, b, name, n_blocks, bn, a_blocked=False, b_blocked=True, exchange=None, token_tile=512):
    t_len = a.shape[0]
    m = a.shape[1] // n_blocks if a_blocked else a.shape[1]
    bt = min(token_tile, t_len)

    def body(a_ref, b_ref, o_ref):
        @pl.when(pl.program_id(1) == 0)
        def _():
            o_ref[...] = jnp.zeros_like(o_ref)

        o_ref[...] += _dot_tn(a_ref[...].astype(BF16), b_ref[...].astype(BF16))

    a_spec = pl.BlockSpec((bt, m), (lambda n, t: (t, n)) if a_blocked else (lambda n, t: (t, 0)))
    b_spec = pl.BlockSpec((bt, bn), (lambda n, t: (t, n)) if b_blocked else (lambda n, t: (t, 0)))
    nt = t_len // bt
    (out,), landed = _pallas(
        body, name=name, grid=(n_blocks, nt), operands=(a, b),
        out_shape=[jax.ShapeDtypeStruct((n_blocks, m, bn), F32)],
        in_specs=[a_spec, b_spec],
        out_specs=[pl.BlockSpec((None, m, bn), lambda n, t: (n, 0, 0))],
        params=_params(56, ("arbitrary", "arbitrary")), exchange=exchange,
        first=lambda: (pl.program_id(0) == 0) & (pl.program_id(1) == 0),
        last=lambda: (pl.program_id(0) == n_blocks - 1) & (pl.program_id(1) == nt - 1))
    return (out, landed) if exchange else out


SMALL_PLACE = {"ln1_g": (6, 0), "ln1_b": (7, 0), "ln2_g": (8, 0), "ln2_b": (9, 0), "hg_norm_w": (10, 512),
               "mla_q_norm_w": (11, 0), "mla_kv_norm_w": (11, Q_RANK)}
SMALL_LB_ROW, SMALL_LOSS_ROW = 10, 12


def _small_params_step(gathered, params):
    names = list(params)

    def body(g_ref, *refs):
        ins, outs = refs[:3 * len(names)], refs[3 * len(names):]
        loss_ref, outs = outs[0], outs[1:]
        tot = g_ref[0]
        for d in range(1, N_DEV):
            tot = tot + g_ref[d]
        loss_ref[...] = tot[SMALL_LOSS_ROW:SMALL_LOSS_ROW + 1, 0:128]

        def update(i, grad, rows=slice(None), lanes=slice(None)):
            w_ref, m_ref, v_ref = ins[3 * i:3 * i + 3]
            g_out, d_out, nm_out, nv_out = outs[4 * i:4 * i + 4]
            g_out[rows, lanes] = grad
            d_out[rows, lanes], nm_out[rows, lanes], nv_out[rows, lanes] = _adamw_update(
                w_ref[rows, lanes], grad, m_ref[rows, lanes], v_ref[rows, lanes])

        for i, name in enumerate(names):
            if name == "b_ada":
                for r in range(6):
                    update(i, tot[r:r + 1, :], lanes=slice(r * D_MODEL, (r + 1) * D_MODEL))
            elif name == "hg_lower_bounds":
                lb = _lower_bound(ins[3 * i][...])
                d0 = tot[SMALL_LB_ROW:SMALL_LB_ROW + 1, 0:512] * lb * (1.0 - lb)
                update(i, d0, rows=slice(0, 1))
                update(i, -d0, rows=slice(1, 2))
            else:
                row, lane = SMALL_PLACE[name]
                update(i, tot[row:row + 1, lane:lane + params[name][0].shape[1]])

    flat_in = [a for name in names for a in params[name]]
    shapes = [jax.ShapeDtypeStruct((1, 128), F32)] + [jax.ShapeDtypeStruct(params[name][0].shape, F32)
                                                      for name in names for _ in range(4)]
    out = pl.pallas_call(body, name="small_params_step", out_shape=shapes)(gathered, *flat_in)
    return out[0], {name: out[1 + 4 * i:5 + 4 * i] for i, name in enumerate(names)}


def _adamw_update(w, gv, m, v):
    nm = ADAM_B1 * m + (1.0 - ADAM_B1) * gv
    nv = ADAM_B2 * v + (1.0 - ADAM_B2) * jnp.square(gv)
    m_hat = nm / (1.0 - ADAM_B1 ** ADAM_STEP)
    v_hat = nv / (1.0 - ADAM_B2 ** ADAM_STEP)
    return -ADAM_LR * (m_hat / (jnp.sqrt(v_hat) + ADAM_EPS) + ADAM_WD * w), nm, nv


def _adamw_halves(core, w, mine, theirs, m, v, name):
    rows, cols = w.shape
    h = rows // 2
    tr = _row_tile(h)
    per_half = h // tr

    def body(core_ref, w_ref, mine_ref, theirs_ref, m_ref, v_ref, g_ref, d_ref, nm_ref, nv_ref):
        is_mine = pl.program_id(0) // per_half == core_ref[0]
        gv = jnp.where(is_mine, mine_ref[...], theirs_ref[...])
        g_ref[...] = gv
        d_ref[...], nm_ref[...], nv_ref[...] = _adamw_update(w_ref[...], gv, m_ref[...], v_ref[...])

    full = pl.BlockSpec((tr, cols), lambda i, core_ref: (i, 0))
    part = pl.BlockSpec((tr, cols), lambda i, core_ref: (i % per_half, 0))
    return _pcall(
        body, name=name, out_shape=[jax.ShapeDtypeStruct(w.shape, F32)] * 4,
        grid_spec=pltpu.PrefetchScalarGridSpec(
            num_scalar_prefetch=1, grid=(rows // tr,), in_specs=[full, part, part, full, full], out_specs=[full] * 4),
        compiler_params=_params(40, ("arbitrary",)),
        operands=(core, w, mine, theirs, m, v))


def _adamw(w, g, m, v, name):
    rows, cols = w.shape
    tr = _row_tile(rows) if rows >= 8 else rows

    def body(w_ref, g_ref, m_ref, v_ref, d_ref, nm_ref, nv_ref):
        d_ref[...], nm_ref[...], nv_ref[...] = _adamw_update(w_ref[...], g_ref[...], m_ref[...], v_ref[...])

    spec = pl.BlockSpec((tr, cols), lambda i: (i, 0))
    return _pcall(
        body, name=name, grid=(rows // tr,),
        out_shape=[jax.ShapeDtypeStruct(w.shape, F32)] * 3,
        in_specs=[spec] * 4, out_specs=[spec] * 3,
        compiler_params=_params(40, ("arbitrary",)),
        operands=(w, g, m, v))


def kernel(x, c, positions, w_ada, b_ada, w_in, hg_lower_bounds, hg_norm_w, mla_q_norm_w, w_q_up, mla_kv_norm_w, w_kv_up, w_out, ln1_g, ln1_b, w_mlp_in, w_mlp_out, ln2_g, ln2_b, loss_target, m_w_ada, m_b_ada, m_w_in, m_hg_lower_bounds, m_hg_norm_w, m_mla_q_norm_w, m_w_q_up, m_mla_kv_norm_w, m_w_kv_up, m_w_out, m_ln1_g, m_ln1_b, m_w_mlp_in, m_w_mlp_out, m_ln2_g, m_ln2_b, v_w_ada, v_b_ada, v_w_in, v_hg_lower_bounds, v_hg_norm_w, v_mla_q_norm_w, v_w_q_up, v_mla_kv_norm_w, v_w_kv_up, v_w_out, v_ln1_g, v_ln1_b, v_w_mlp_in, v_w_mlp_out, v_ln2_g, v_ln2_b):
    ix, iy, ic = _mesh_pos()
    chip = 2 * ix + iy
    me = 4 * ix + 2 * iy + ic
    core_arr = jnp.reshape(ic, (1,)).astype(jnp.int32)
    chip_arr = jnp.reshape(chip, (1,)).astype(jnp.int32)

    xs = x[0]
    target = loss_target[0]
    t_len = xs.shape[0]
    pos = positions.astype(F32).reshape(t_len, 1)
    inv = 1.0 / (ROPE_THETA ** (jnp.arange(0, ROPE_DIM, 2, dtype=F32) / ROPE_DIM))
    invf = jnp.concatenate([inv, inv, jnp.zeros((128 - ROPE_DIM,), F32)]).reshape(1, 128)

    def slot(w):
        rows, cols = w.shape
        own = w.astype(BF16).reshape(1, 2, rows // 2, cols)
        return lax.dynamic_update_slice(jnp.zeros((N_CHIPS, 2, rows // 2, cols), BF16), own, (chip, 0, 0, 0))

    def slot8(a):
        return lax.dynamic_update_slice(jnp.zeros((N_DEV,) + a.shape, a.dtype), a[None], (me, 0, 0))

    def whole(s):
        return s.reshape(N_CHIPS, 2 * s.shape[2], s.shape[3])

    def halved(g):
        return g.reshape(N_CHIPS, 2, g.shape[1] // 2, g.shape[2])

    ada_cols = w_ada.shape[2]
    c_all, *early = _run_exchange(
        _merge(_gather_all(slot8(jnp.broadcast_to(c, (8, D_MODEL)))),
               _gather_over_ici([slot(jnp.transpose(w_in[0])), slot(w_q_up[0]), slot(w_kv_up[0])])),
        "gather_c_and_mixer_weights_ici")
    b_shard = lax.dynamic_slice(b_ada, (0, chip * ada_cols), (1, ada_cols))
    mod_cols, cond16 = _ada_project(c_all[:, 0, :], w_ada[0], b_shard)
    mod_all, *early = _run_exchange(_merge(_gather_all(slot8(mod_cols)), _gather_over_d2d(early)),
                                    "gather_mod_and_mixer_weights_d2d")
    mod_mine = lax.dynamic_slice(mod_all, (0, me, 0), (N_DEV, 1, ada_cols))[::2, 0, :].reshape(6, D_MODEL)
    sh_a, sc_a, g_a, sh_m, sc_m, g_m = (mod_mine[i:i + 1] for i in range(6))
    g_in, g_q, g_kv = (whole(s) for s in early)
    w_in_full = jnp.pad(g_in.reshape(IN_COLS, D_MODEL), ((0, IN_COLS_PAD - IN_COLS), (0, 0)))
    w_q_full = jnp.pad(g_q, ((0, 0), (0, 0), (0, QK_DIM - g_q.shape[2])))

    w1_rows = D_MODEL // 2
    (u_a, zhg, cq, ckv, q, k, k_t, v, v_t), (s_top, s_out) = _in_project(
        xs, pos, sc_a, sh_a, w_in_full, mla_q_norm_w, mla_kv_norm_w, w_q_full, g_kv, invf,
        _gather_over_ici([slot(w_mlp_in[0, :w1_rows]), slot(w_out[0])]))
    (o_pre, o_hg, states), (s_bottom, s_top, s_out) = _hgrn_forward(
        zhg, hg_lower_bounds, hg_norm_w,
        _merge(_gather_over_ici([slot(w_mlp_in[0, w1_rows:])]), _gather_over_d2d([s_top, s_out])))
    (o_mla, lse), (s_w2, s_bottom) = _attention_forward(
        q, k, v_t, _merge(_gather_over_ici([slot(w_mlp_out[0])]), _gather_over_d2d([s_bottom])))
    w_out_full = whole(s_out).reshape(D_MODEL, D_MODEL)
    (cat, mix, xhat1, rstd1), (s_w2,) = _out_project(o_hg, o_mla, xs, g_a, w_out_full, _gather_over_d2d([s_w2]))
    g_w1_top, g_w1_bottom, g_w2 = whole(s_top), whole(s_bottom), whole(s_w2)
    vecs = jnp.concatenate([ln1_g, ln1_b, sc_m, sh_m, g_m, g_a, ln2_g, ln2_b], axis=0)
    act, dhp, um, dh, dmix, d_cat, dr1, mlp_sums, delta = _mlp_and_back(
        xhat1, rstd1, mix, target, o_mla, vecs, g_w1_top, g_w1_bottom, g_w2, w_out_full)

    gw_1 = halved(_weight_grad(um, dhp, "grad_w_mlp_in", N_CHIPS, D_FF // N_CHIPS, token_tile=4096))
    gw_2, (landed_1,) = _weight_grad(act, dh, "grad_w_mlp_out", N_CHIPS, D_MODEL, a_blocked=True, b_blocked=False,
                                     token_tile=4096, exchange=_pair_exchange([gw_1]))
    gw_out = _weight_grad(cat, dmix, "grad_w_out", 1, D_MODEL, token_tile=2048)
    later = [halved(gw_2), halved(gw_out.reshape(N_CHIPS, D_MODEL // N_CHIPS, D_MODEL))]
    own_1, travels_1 = _add_pair(core_arr, chip_arr, gw_1, landed_1)
    (dq, dk, dv), (landed_1, *landed) = _attention_backward(
        q, k, k_t, v, d_cat, lse, delta, _merge(_chip_exchange([travels_1]), _pair_exchange(later)))
    mine_1 = _add_chips(own_1, landed_1)
    chip_sums = [_add_pair(core_arr, chip_arr, g, l) for g, l in zip(later, landed)]
    (d_hq, d_hf, d_hi, d_hg, hg_sums), (theirs_1, *landed) = _hgrn_backward(
        zhg, hg_lower_bounds, hg_norm_w, o_pre, d_cat, states,
        _merge(_pair_send([mine_1]), _chip_exchange([b for _, b in chip_sums])))
    later_mine = [_add_chips(own, l) for (own, _), l in zip(chip_sums, landed)]
    mlp_mine = [mine_1] + later_mine
    (dz, grad_x, in_sums, gw_q, gw_kv), _ = _in_project_backward(
        dq, dk, dv, cq, ckv, pos, invf, mla_q_norm_w, mla_kv_norm_w, w_q_full, g_kv,
        d_hq, d_hf, d_hi, d_hg, w_in_full, dr1, xs, sc_a)

    zeros = lambda n: jnp.zeros((1, n), F32)
    small = jnp.concatenate([
        in_sums[1:2], in_sums[0:1], mlp_sums[S_DGA:S_DGA + 1],
        mlp_sums[S_DSHM:S_DSHM + 1], mlp_sums[S_DSCM:S_DSCM + 1], mlp_sums[S_DGM:S_DGM + 1],
        mlp_sums[S_DLN1G:S_DLN1G + 1], mlp_sums[S_DLN1B:S_DLN1B + 1],
        mlp_sums[S_DLN2G:S_DLN2G + 1], mlp_sums[S_DLN2B:S_DLN2B + 1],
        jnp.concatenate([hg_sums[0:1], hg_sums[1:2]], axis=1),
        jnp.concatenate([in_sums[2:3, :Q_RANK], in_sums[3:4, :KV_RANK], zeros(D_MODEL - Q_RANK - KV_RANK)], axis=1),
        mlp_sums[S_LOSS:S_LOSS + 1],
        jnp.zeros((SMALL_ROWS - 13, D_MODEL), F32)], axis=0)

    gw_in, (*later_theirs, small_all) = _weight_grad(
        dz, u_a, "grad_w_in", 3, D_MODEL, a_blocked=True, b_blocked=False, token_tile=4096,
        exchange=_merge(_pair_send(later_mine), _gather_all(slot8(small))))
    mlp_theirs = [theirs_1] + list(later_theirs)
    gw_in = gw_in.reshape(IN_COLS_PAD, D_MODEL)
    gw_q = gw_q[:, :, :HEAD_DIM + ROPE_DIM]
    flat = lambda g: g.reshape(g.shape[0] * g.shape[1], g.shape[2])
    mixer_mine, mixer_theirs = _reduce_in_vmem(
        [gw_in, flat(gw_q), flat(gw_kv)], [IN_COLS // N_CHIPS // 2, Q_RANK // 2, KV_RANK // 2], "reduce_mixer_grads")
    reduced = ("w_in", "w_q_up", "w_kv_up", "w_mlp_in", "w_mlp_out", "w_out")
    halves_mine = dict(zip(reduced, list(mixer_mine) + mlp_mine))
    halves_theirs = dict(zip(reduced, list(mixer_theirs) + list(mlp_theirs)))

    small_names = ("b_ada", "hg_lower_bounds", "hg_norm_w", "mla_q_norm_w", "mla_kv_norm_w",
                   "ln1_g", "ln1_b", "ln2_g", "ln2_b")
    loss_row, small_out = _small_params_step(small_all, {
        "b_ada": (b_ada, m_b_ada, v_b_ada),
        "hg_lower_bounds": (hg_lower_bounds, m_hg_lower_bounds, v_hg_lower_bounds),
        "hg_norm_w": (hg_norm_w, m_hg_norm_w, v_hg_norm_w),
        "mla_q_norm_w": (mla_q_norm_w, m_mla_q_norm_w, v_mla_q_norm_w),
        "mla_kv_norm_w": (mla_kv_norm_w, m_mla_kv_norm_w, v_mla_kv_norm_w),
        "ln1_g": (ln1_g, m_ln1_g, v_ln1_g), "ln1_b": (ln1_b, m_ln1_b, v_ln1_b),
        "ln2_g": (ln2_g, m_ln2_g, v_ln2_g), "ln2_b": (ln2_b, m_ln2_b, v_ln2_b)})
    loss = loss_row[0, 0]

    d_mod_all = small_all[:, 0:6, :].reshape(N_DEV, 6 * D_MODEL)
    d_mod_cols = lax.dynamic_slice(d_mod_all, (0, chip * ada_cols), (N_DEV, ada_cols))
    d_mod_cols = jnp.concatenate([d_mod_cols, jnp.zeros_like(d_mod_cols)], axis=0)
    g_w_ada = _weight_grad(cond16, d_mod_cols, "grad_w_ada", 1, ada_cols)[0]

    names = ["w_ada", "b_ada", "w_in", "hg_lower_bounds", "hg_norm_w", "mla_q_norm_w", "w_q_up", "mla_kv_norm_w",
             "w_kv_up", "w_out", "ln1_g", "ln1_b", "w_mlp_in", "w_mlp_out", "ln2_g", "ln2_b"]
    weights = [w_ada, b_ada, w_in, hg_lower_bounds, hg_norm_w, mla_q_norm_w, w_q_up, mla_kv_norm_w,
               w_kv_up, w_out, ln1_g, ln1_b, w_mlp_in, w_mlp_out, ln2_g, ln2_b]
    moms = [m_w_ada, m_b_ada, m_w_in, m_hg_lower_bounds, m_hg_norm_w, m_mla_q_norm_w, m_w_q_up, m_mla_kv_norm_w,
            m_w_kv_up, m_w_out, m_ln1_g, m_ln1_b, m_w_mlp_in, m_w_mlp_out, m_ln2_g, m_ln2_b]
    vels = [v_w_ada, v_b_ada, v_w_in, v_hg_lower_bounds, v_hg_norm_w, v_mla_q_norm_w, v_w_q_up, v_mla_kv_norm_w,
            v_w_kv_up, v_w_out, v_ln1_g, v_ln1_b, v_w_mlp_in, v_w_mlp_out, v_ln2_g, v_ln2_b]
    out_g, out_d, out_m, out_v = [], [], [], []
    for name, w, m, vv in zip(names, weights, moms, vels):
        if name in small_names:
            g, d, nm, nv = small_out[name]
            back = lambda a: a
        elif name == "w_in":
            to2d, back = (lambda a: jnp.transpose(a[0])), (lambda a: jnp.transpose(a)[None])
        else:
            to2d, back = (lambda a, s=w.shape[1:]: a.reshape(s)), (lambda a, s=w.shape: a.reshape(s))
        if name == "w_ada":
            d, nm, nv = _adamw(to2d(w), g_w_ada, to2d(m), to2d(vv), "adamw_" + name)
            g = g_w_ada
        elif name not in small_names:
            g, d, nm, nv = _adamw_halves(core_arr, to2d(w), halves_mine[name], halves_theirs[name], to2d(m), to2d(vv),
                                         "adamw_" + name)
        out_g.append(back(g))
        out_d.append(back(d))
        out_m.append(back(nm))
        out_v.append(back(nv))
    return (loss, grad_x[None], *out_g, *out_d, *out_m, *out_v)
```

```python
import functools

import jax
import jax.numpy as jnp
from jax import lax
from jax.experimental import pallas as pl
from jax.experimental.pallas import tpu as pltpu

F32 = jnp.float32
BF16 = jnp.bfloat16
MESH_IDS = pl.DeviceIdType.MESH

D_MODEL = 1024
N_HEADS = 4
HEAD_DIM = 128
ROPE_DIM = 64
HG_CHUNK = 64
HG_COLS = 2048
Q_RANK = 256
KV_RANK = 256
IN_COLS = 2624
IN_COLS_PAD = 2688
QK_DIM = 256
QK_REAL = HEAD_DIM + ROPE_DIM
D_FF = 4096
N_CHIPS = 4
N_DEV = 8
ROPE_THETA = 10000.0
RMS_EPS = 1e-6
LN_EPS = 1e-5
DN_ALPHA = 2.0 ** 0.25
ATT_SCALE = (HEAD_DIM + ROPE_DIM) ** -0.5
NEG_BIG = -1e30
ADAM_LR = 0.001
ADAM_B1 = 0.9
ADAM_B2 = 0.999
ADAM_EPS = 1e-08
ADAM_WD = 0.01
ADAM_STEP = 10
SMALL_ROWS = 16
MIB = 1024 * 1024


def _dot(a, b):
    return jnp.dot(a, b, preferred_element_type=F32)


def _dot_nt(a, b):
    return lax.dot_general(a, b, (((1,), (1,)), ((), ())), preferred_element_type=F32)


def _dot_tn(a, b):
    return lax.dot_general(a, b, (((0,), (0,)), ((), ())), preferred_element_type=F32)


def _params(vmem_mib, semantics=None):
    return pltpu.CompilerParams(vmem_limit_bytes=vmem_mib * MIB, dimension_semantics=semantics)


def _sigmoid(v):
    return 1.0 / (1.0 + jnp.exp(-v))


def _colsum(v):
    return jnp.sum(v, axis=0, keepdims=True)


def _rowmean(v):
    return jnp.mean(v, axis=-1, keepdims=True)


def _rope_tables(pos, invf):
    ang = pos * invf
    lane = lax.broadcasted_iota(jnp.int32, ang.shape, 1)
    cos_t = jnp.where(lane < ROPE_DIM, jnp.cos(ang), 0.0)
    sin = jnp.sin(ang)
    sin_t = jnp.where(lane < ROPE_DIM // 2, -sin, jnp.where(lane < ROPE_DIM, sin, 0.0))
    return cos_t, sin_t


def _swap_halves(t):
    lane = lax.broadcasted_iota(jnp.int32, t.shape, 1)
    return jnp.where(lane < ROPE_DIM // 2, pltpu.roll(t, 128 - ROPE_DIM // 2, 1), pltpu.roll(t, ROPE_DIM // 2, 1))


def _rope(t, cos_t, sin_t):
    return t * cos_t + _swap_halves(t) * sin_t


def _unrope(g, cos_t, sin_t):
    return g * cos_t - _swap_halves(g) * sin_t


def _mesh_pos():
    return lax.axis_index("x"), lax.axis_index("y"), lax.axis_index("c")


def _other_chips(x, y):
    out = []
    for dx, dy in ((1, 0), (0, 1), (1, 1)):
        px = 1 - x if dx else x
        py = 1 - y if dy else y
        out.append(((px, py), 2 * px + py))
    return out


class _Exchange:
    def __init__(self, inputs, out_shapes, aliases, sems, start, finish):
        self.inputs, self.out_shapes, self.aliases, self.sems = list(inputs), list(out_shapes), dict(aliases), list(sems)
        self.start, self.finish = start, finish


def _from_copies(inputs, out_shapes, aliases, sems, copies):
    def start(ins, outs, sem_refs):
        for send, _ in copies(ins, outs, sem_refs):
            send.start()

    def finish(ins, outs, sem_refs):
        for send, recv in copies(ins, outs, sem_refs):
            recv.wait_recv()
            send.wait_send()

    return _Exchange(inputs, out_shapes, aliases, sems, start, finish)


HBM_MIN_BYTES = 256 * 1024


def _in_hbm(a):
    if a.size * a.dtype.itemsize < HBM_MIN_BYTES:
        return a
    return pltpu.with_memory_space_constraint(a, pltpu.HBM)


def _out_hbm(s):
    if s.size * s.dtype.itemsize < HBM_MIN_BYTES:
        return s
    return pltpu.HBM(s.shape, s.dtype)


def _pcall(body, *, operands, out_shape, **kwargs):
    single = not isinstance(out_shape, (list, tuple))
    shapes = [_out_hbm(s) for s in ([out_shape] if single else out_shape)]
    return pl.pallas_call(body, out_shape=shapes[0] if single else shapes, **kwargs)(*[_in_hbm(a) for a in operands])


def _run_exchange(exchange, name):
    n_in, n_out = len(exchange.inputs), len(exchange.out_shapes)

    def body(*refs):
        ins, outs, sem_refs = refs[:n_in], refs[n_in:n_in + n_out], refs[n_in + n_out:]
        exchange.start(ins, outs, sem_refs)
        exchange.finish(ins, outs, sem_refs)

    any_spec = pl.BlockSpec(memory_space=pl.ANY)
    return pl.pallas_call(
        body, name=name, out_shape=[_out_hbm(s) for s in exchange.out_shapes],
        in_specs=[any_spec] * n_in, out_specs=[any_spec] * n_out,
        scratch_shapes=exchange.sems, input_output_aliases=exchange.aliases,
    )(*[_in_hbm(a) for a in exchange.inputs])


def _pallas(body, *, name, operands, in_specs, out_shape, out_specs, params, scratch_shapes=(), grid=(), prefetch=(),
            exchange=None, first=None, last=None):
    n_pre, n_in, n_out, n_scr = len(prefetch), len(in_specs), len(out_specs), len(scratch_shapes)
    ex_in = exchange.inputs if exchange else []
    ex_out = exchange.out_shapes if exchange else []
    ex_sems = exchange.sems if exchange else []

    def full_body(*refs):
        pre, rest = refs[:n_pre], refs[n_pre:]
        ins, rest = rest[:n_in], rest[n_in:]
        xin, rest = rest[:len(ex_in)], rest[len(ex_in):]
        outs, rest = rest[:n_out], rest[n_out:]
        xout, rest = rest[:len(ex_out)], rest[len(ex_out):]
        scr, sem_refs = rest[:n_scr], rest[n_scr:]
        if exchange:
            @pl.when(first(*pre))
            def _():
                exchange.start(xin, xout, sem_refs)

        body(*pre, *ins, *outs, *scr)
        if exchange:
            @pl.when(last(*pre))
            def _():
                exchange.finish(xin, xout, sem_refs)

    any_spec = pl.BlockSpec(memory_space=pl.ANY)
    aliases = {n_pre + n_in + i: n_out + o for i, o in exchange.aliases.items()} if exchange else {}
    operands = [_in_hbm(a) for a in operands]
    results = pl.pallas_call(
        full_body, name=name, out_shape=[_out_hbm(s) for s in list(out_shape) + ex_out],
        grid_spec=pltpu.PrefetchScalarGridSpec(
            num_scalar_prefetch=n_pre, grid=grid, in_specs=list(in_specs) + [any_spec] * len(ex_in),
            out_specs=list(out_specs) + [any_spec] * len(ex_out), scratch_shapes=list(scratch_shapes) + ex_sems),
        input_output_aliases=aliases, compiler_params=params,
    )(*prefetch, *operands, *[_in_hbm(a) for a in ex_in])
    return results[:n_out], results[n_out:]


def _remote(src, dst, sems, idx, to):
    send_sems, recv_sems = sems
    return pltpu.make_async_remote_copy(src_ref=src, dst_ref=dst, send_sem=send_sems.at[idx], recv_sem=recv_sems.at[idx],
                                        device_id=to, device_id_type=MESH_IDS)


def _sem_pairs(*shape):
    return [pltpu.SemaphoreType.DMA(shape), pltpu.SemaphoreType.DMA(shape)]


def _same_shapes(arrays):
    return [jax.ShapeDtypeStruct(a.shape, a.dtype) for a in arrays]


def _gather_over_ici(slots):
    n = len(slots)

    def copies(ins, outs, sems):
        x, y, c = _mesh_pos()
        k = 2 * x + y
        out = []
        for j, (chip, kj) in enumerate(_other_chips(x, y)):
            for i in range(n):
                to = (*chip, c)
                out.append((_remote(ins[i].at[k, c], outs[i].at[k, c], sems, (j, i), to),
                            _remote(ins[i].at[k, c], outs[i].at[kj, c], sems, (j, i), to)))
        return out

    return _from_copies(slots, _same_shapes(slots), {i: i for i in range(n)}, _sem_pairs(3, n), copies)


def _gather_over_d2d(slots):
    n = len(slots)

    def copies(ins, outs, sems):
        x, y, c = _mesh_pos()
        sibling = (x, y, 1 - c)
        out = []
        for j, (_, kj) in enumerate(_other_chips(x, y)):
            for i in range(n):
                out.append((_remote(ins[i].at[kj, c], outs[i].at[kj, c], sems, (j, i), sibling),
                            _remote(ins[i].at[kj, c], outs[i].at[kj, 1 - c], sems, (j, i), sibling)))
        return out

    return _from_copies(slots, _same_shapes(slots), {i: i for i in range(n)}, _sem_pairs(3, n), copies)


def _gather_all(slots8):
    def copies(ins, outs, sems):
        x, y, c = _mesh_pos()
        me = 4 * x + 2 * y + c
        out = []
        for r in range(1, N_DEV):
            px = 1 - x if r & 4 else x
            py = 1 - y if r & 2 else y
            pc = 1 - c if r & 1 else c
            to = (px, py, pc)
            out.append((_remote(ins[0].at[me], outs[0].at[me], sems, r - 1, to),
                        _remote(ins[0].at[me], outs[0].at[4 * px + 2 * py + pc], sems, r - 1, to)))
        return out

    return _from_copies([slots8], _same_shapes([slots8]), {0: 0}, _sem_pairs(N_DEV - 1), copies)


def _merge(first, second):
    n_in, n_out, n_sem = len(first.inputs), len(first.out_shapes), len(first.sems)

    def start(ins, outs, sems):
        first.start(ins[:n_in], outs[:n_out], sems[:n_sem])
        second.start(ins[n_in:], outs[n_out:], sems[n_sem:])

    def finish(ins, outs, sems):
        first.finish(ins[:n_in], outs[:n_out], sems[:n_sem])
        second.finish(ins[n_in:], outs[n_out:], sems[n_sem:])

    aliases = dict(first.aliases)
    aliases.update({n_in + i: n_out + o for i, o in second.aliases.items()})
    return _Exchange(first.inputs + second.inputs, first.out_shapes + second.out_shapes, aliases,
                     first.sems + second.sems, start, finish)


def _pair_exchange(grads):
    n = len(grads)

    def copies(ins, outs, sems):
        x, y, c = _mesh_pos()
        cps = [_remote(ins[i].at[:, 1 - c], outs[i], sems, i, (x, y, 1 - c)) for i in range(n)]
        return [(cp, cp) for cp in cps]

    shapes = [jax.ShapeDtypeStruct((N_CHIPS,) + g.shape[2:], g.dtype) for g in grads]
    return _from_copies(grads, shapes, {}, _sem_pairs(n), copies)


def _chip_exchange(partials):
    n = len(partials)

    def copies(ins, outs, sems):
        x, y, c = _mesh_pos()
        cps = [_remote(ins[i].at[kj], outs[i].at[j], sems, (j, i), (*chip, c))
               for j, (chip, kj) in enumerate(_other_chips(x, y)) for i in range(n)]
        return [(cp, cp) for cp in cps]

    shapes = [jax.ShapeDtypeStruct((3,) + p.shape[1:], p.dtype) for p in partials]
    return _from_copies(partials, shapes, {}, _sem_pairs(3, n), copies)


def _pair_send(halves):
    n = len(halves)

    def copies(ins, outs, sems):
        x, y, c = _mesh_pos()
        cps = [_remote(ins[i], outs[i], sems, i, (x, y, 1 - c)) for i in range(n)]
        return [(cp, cp) for cp in cps]

    return _from_copies(halves, _same_shapes(halves), {}, _sem_pairs(n), copies)


def _reduce_in_vmem(grads, half_rows, name):
    n = len(grads)

    def body(*refs):
        g, mine, theirs = refs[:n], refs[n:2 * n], refs[2 * n:3 * n]
        landed_pair, partial, landed_chips = refs[3 * n:4 * n], refs[4 * n:5 * n], refs[5 * n:6 * n]
        sems = refs[6 * n:]
        x, y, c = _mesh_pos()
        k = 2 * x + y
        sibling = (x, y, 1 - c)

        def half(i, chip_idx, which):
            return pl.ds(pl.multiple_of((2 * chip_idx + which) * half_rows[i], 8), half_rows[i])

        def run(copies):
            for cp in copies:
                cp.start()
            for cp in copies:
                cp.wait_recv()
                cp.wait_send()

        run([_remote(g[i].at[half(i, kk, 1 - c)], landed_pair[i].at[kk], sems[0:2], (kk, i), sibling)
             for kk in range(N_CHIPS) for i in range(n)])
        for i in range(n):
            for kk in range(N_CHIPS):
                partial[i][kk] = (g[i][half(i, kk, c), :] + landed_pair[i][kk]).astype(BF16)
        run([_remote(partial[i].at[kj], landed_chips[i].at[j], sems[2:4], (j, i), (*chip, c))
             for j, (chip, kj) in enumerate(_other_chips(x, y)) for i in range(n)])
        for i in range(n):
            own = g[i][half(i, k, c), :] + landed_pair[i][k]
            mine[i][...] = ((own + landed_chips[i][0].astype(F32)) + landed_chips[i][1].astype(F32)) \
                + landed_chips[i][2].astype(F32)
        run([_remote(mine[i], theirs[i], sems[4:6], i, sibling) for i in range(n)])

    shapes = [(h, gr.shape[1]) for gr, h in zip(grads, half_rows)]
    halves = [jax.ShapeDtypeStruct(s, F32) for s in shapes]
    vmem = pl.BlockSpec(memory_space=pltpu.VMEM)
    scratch = ([pltpu.VMEM((N_CHIPS,) + s, F32) for s in shapes]
               + [pltpu.VMEM((N_CHIPS,) + s, BF16) for s in shapes]
               + [pltpu.VMEM((3,) + s, BF16) for s in shapes]
               + _sem_pairs(N_CHIPS, n) + _sem_pairs(3, n) + _sem_pairs(n))
    out = pl.pallas_call(
        body, name=name, out_shape=halves + halves, in_specs=[vmem] * n, out_specs=[vmem] * (2 * n),
        scratch_shapes=scratch, compiler_params=_params(48),
    )(*grads)
    return out[:n], out[n:]


def _row_tile(rows):
    for t in (256, 128, 64):
        if rows % t == 0:
            return t
    return rows


def _add_pair(core, chip, grad, landed):
    _, h, cols = landed.shape
    tr = _row_tile(h)

    def body(core_ref, chip_ref, g_ref, l_ref, own_ref, ob_ref):
        s = g_ref[...] + l_ref[...]
        ob_ref[...] = s.astype(BF16)

        @pl.when(pl.program_id(1) == chip_ref[0])
        def _():
            own_ref[...] = s

    return _pcall(
        body, name="grad_add_pair",
        out_shape=[jax.ShapeDtypeStruct((h, cols), F32), jax.ShapeDtypeStruct(landed.shape, BF16)],
        grid_spec=pltpu.PrefetchScalarGridSpec(
            num_scalar_prefetch=2, grid=(h // tr, N_CHIPS),
            in_specs=[pl.BlockSpec((None, None, tr, cols), lambda t, k, core_ref, chip_ref: (k, core_ref[0], t, 0)),
                      pl.BlockSpec((None, tr, cols), lambda t, k, core_ref, chip_ref: (k, t, 0))],
            out_specs=[pl.BlockSpec((tr, cols), lambda t, k, core_ref, chip_ref: (t, 0)),
                       pl.BlockSpec((None, tr, cols), lambda t, k, core_ref, chip_ref: (k, t, 0))]),
        compiler_params=_params(32, ("arbitrary", "arbitrary")),
        operands=(core, chip, grad, landed))


def _add_chips(own, landed):
    h, cols = own.shape
    tr = _row_tile(h)

    def body(p_ref, l_ref, o_ref):
        o_ref[...] = ((p_ref[...] + l_ref[0].astype(F32)) + l_ref[1].astype(F32)) + l_ref[2].astype(F32)

    return _pcall(
        body, name="grad_add_chips", grid=(h // tr,),
        out_shape=jax.ShapeDtypeStruct((h, cols), F32),
        in_specs=[pl.BlockSpec((tr, cols), lambda t: (t, 0)), pl.BlockSpec((3, tr, cols), lambda t: (0, t, 0))],
        out_specs=pl.BlockSpec((tr, cols), lambda t: (t, 0)),
        compiler_params=_params(32, ("arbitrary",)),
        operands=(own, landed))


def _ada_project(c_all, w_ada, b_shard):
    n = w_ada.shape[1]
    tn = 512

    def body(c_ref, w_ref, b_ref, mod_ref, cond_ref):
        cv = c_ref[...]
        cond = cv * _sigmoid(cv)
        mod_ref[...] = _dot(cond.astype(BF16), w_ref[...].astype(BF16)) + b_ref[...]
        cond_ref[0:N_DEV, :] = cond
        cond_ref[N_DEV:2 * N_DEV, :] = jnp.zeros_like(cond)

    return _pcall(
        body, name="ada_project", grid=(n // tn,),
        out_shape=[jax.ShapeDtypeStruct((N_DEV, n), F32), jax.ShapeDtypeStruct((2 * N_DEV, D_MODEL), F32)],
        in_specs=[pl.BlockSpec((N_DEV, D_MODEL), lambda j: (0, 0)), pl.BlockSpec((D_MODEL, tn), lambda j: (0, j)),
                  pl.BlockSpec((1, tn), lambda j: (0, j))],
        out_specs=[pl.BlockSpec((N_DEV, tn), lambda j: (0, j)), pl.BlockSpec((2 * N_DEV, D_MODEL), lambda j: (0, 0))],
        compiler_params=_params(32, ("arbitrary",)),
        operands=(c_all, w_ada, b_shard))


def _in_project(x, pos, sc_a, sh_a, w_in, q_norm_w, kv_norm_w, w_q, w_kv, invf, exchange=None):
    t_len = x.shape[0]
    tm = min(512, t_len)

    def body(x_ref, pos_ref, sc_ref, sh_ref, win_ref, qn_ref, kvn_ref, wq_ref, wkv_ref, invf_ref,
             u_ref, zhg_ref, cq_ref, ckv_ref, q_ref, k_ref, kt_ref, v_ref, vt_ref):
        u = (x_ref[...] * (1.0 + sc_ref[...]) + sh_ref[...]).astype(BF16)
        u_ref[...] = u
        z = _dot_nt(u, win_ref[...])
        zhg_ref[...] = z[:, :HG_COLS]
        cq = z[:, HG_COLS:HG_COLS + Q_RANK]
        ckv = z[:, HG_COLS + Q_RANK:HG_COLS + Q_RANK + KV_RANK]
        cq_ref[...] = cq
        ckv_ref[...] = ckv
        cos_t, sin_t = _rope_tables(pos_ref[...], invf_ref[...])
        k_pe = _rope(z[:, HG_COLS + Q_RANK + KV_RANK:], cos_t, sin_t)
        k_pe_t = jnp.transpose(k_pe).astype(BF16)
        cqn = (cq * lax.rsqrt(_rowmean(cq * cq) + RMS_EPS) * qn_ref[...]).astype(BF16)
        ckvn = (ckv * lax.rsqrt(_rowmean(ckv * ckv) + RMS_EPS) * kvn_ref[...]).astype(BF16)
        q_up = [_dot(cqn, wq_ref[h]) for h in range(N_HEADS)]
        kv_up = [_dot(ckvn, wkv_ref[h]) for h in range(N_HEADS)]
        for h in range(N_HEADS):
            qh, kvh = q_up[h], kv_up[h]
            q_ref[h, :, 0:HEAD_DIM] = qh[:, :HEAD_DIM].astype(BF16)
            q_ref[h, :, HEAD_DIM:QK_DIM] = _rope(qh[:, HEAD_DIM:], cos_t, sin_t).astype(BF16)
            k_ref[h, :, 0:HEAD_DIM] = kvh[:, :HEAD_DIM].astype(BF16)
            k_ref[h, :, HEAD_DIM:QK_DIM] = k_pe.astype(BF16)
            kt_ref[h, 0:HEAD_DIM, :] = jnp.transpose(kvh[:, :HEAD_DIM]).astype(BF16)
            kt_ref[h, HEAD_DIM:QK_DIM, :] = k_pe_t
            v_ref[h] = kvh[:, HEAD_DIM:].astype(BF16)
            vt_ref[h] = jnp.transpose(kvh[:, HEAD_DIM:]).astype(BF16)

    row = lambda i: (i, 0)
    fixed2 = lambda i: (0, 0)
    fixed3 = lambda i: (0, 0, 0)
    heads = lambda i: (0, i, 0)
    n_tiles = t_len // tm
    return _pallas(
        body, name="in_project", grid=(n_tiles,),
        operands=(x, pos, sc_a, sh_a, w_in, q_norm_w, kv_norm_w, w_q, w_kv, invf),
        out_shape=[jax.ShapeDtypeStruct((t_len, D_MODEL), BF16), jax.ShapeDtypeStruct((t_len, HG_COLS), F32),
                   jax.ShapeDtypeStruct((t_len, Q_RANK), F32), jax.ShapeDtypeStruct((t_len, KV_RANK), F32),
                   jax.ShapeDtypeStruct((N_HEADS, t_len, QK_DIM), BF16),
                   jax.ShapeDtypeStruct((N_HEADS, t_len, QK_DIM), BF16),
                   jax.ShapeDtypeStruct((N_HEADS, QK_DIM, t_len), BF16),
                   jax.ShapeDtypeStruct((N_HEADS, t_len, HEAD_DIM), BF16),
                   jax.ShapeDtypeStruct((N_HEADS, HEAD_DIM, t_len), BF16)],
        in_specs=[pl.BlockSpec((tm, D_MODEL), row), pl.BlockSpec((tm, 1), row),
                  pl.BlockSpec((1, D_MODEL), fixed2), pl.BlockSpec((1, D_MODEL), fixed2),
                  pl.BlockSpec((IN_COLS_PAD, D_MODEL), fixed2),
                  pl.BlockSpec((1, Q_RANK), fixed2), pl.BlockSpec((1, KV_RANK), fixed2),
                  pl.BlockSpec((N_HEADS, Q_RANK, QK_DIM), fixed3), pl.BlockSpec((N_HEADS, KV_RANK, 2 * HEAD_DIM), fixed3),
                  pl.BlockSpec((1, 128), fixed2)],
        out_specs=[pl.BlockSpec((tm, D_MODEL), row), pl.BlockSpec((tm, HG_COLS), row),
                   pl.BlockSpec((tm, Q_RANK), row), pl.BlockSpec((tm, KV_RANK), row),
                   pl.BlockSpec((N_HEADS, tm, QK_DIM), heads), pl.BlockSpec((N_HEADS, tm, QK_DIM), heads),
                   pl.BlockSpec((N_HEADS, QK_DIM, tm), lambda i: (0, 0, i)),
                   pl.BlockSpec((N_HEADS, tm, HEAD_DIM), heads),
                   pl.BlockSpec((N_HEADS, HEAD_DIM, tm), lambda i: (0, 0, i))],
        params=_params(48, ("arbitrary",)), exchange=exchange,
        first=lambda: pl.program_id(0) == 0, last=lambda: pl.program_id(0) == n_tiles - 1)


def _lower_bound(lb_raw):
    m = jnp.max(lb_raw, axis=0, keepdims=True)
    e = jnp.exp(lb_raw - m)
    return e[0:1] / jnp.sum(e, axis=0, keepdims=True)


def _tri(inclusive_lower):
    r = lax.broadcasted_iota(jnp.int32, (HG_CHUNK, HG_CHUNK), 0)
    c = lax.broadcasted_iota(jnp.int32, (HG_CHUNK, HG_CHUNK), 1)
    return (c <= r) if inclusive_lower else (c >= r)


def _chunk_rows(n):
    return slice(n * HG_CHUNK, (n + 1) * HG_CHUNK)


def _chunk_prefix_sums(v, inclusive_lower):
    tri = _tri(inclusive_lower).astype(BF16)
    hi = v.astype(BF16)
    rest = v - hi.astype(F32)
    mid = rest.astype(BF16)
    lo = (rest - mid.astype(F32)).astype(BF16)
    pieces = jnp.concatenate([hi, mid, lo], axis=1)
    out = []
    for n in range(v.shape[0] // HG_CHUNK):
        s = _dot(tri, pieces[_chunk_rows(n)])
        out.append((s[:, 0:HEAD_DIM] + s[:, HEAD_DIM:2 * HEAD_DIM]) + s[:, 2 * HEAD_DIM:])
    return jnp.concatenate(out, axis=0)


def _per_chunk(v, row):
    n = v.shape[0] // HG_CHUNK
    v3 = v.reshape(n, HG_CHUNK, HEAD_DIM)
    return jnp.broadcast_to(v3[:, row:row + 1, :], v3.shape).reshape(v.shape)


def _hg_block(q, f_logit, lb):
    sg = _sigmoid(f_logit)
    forget = lb + (1.0 - lb) * sg
    kk = 1.0 - forget
    b = _chunk_prefix_sums(jnp.log(forget), True)
    b_ref = _per_chunk(b, HG_CHUNK // 2 - 1)
    b_last = _per_chunk(b, HG_CHUNK - 1)
    e_i = jnp.exp(b - b_ref)
    e_ri = jnp.exp(b_ref - b)
    e_b = jnp.exp(b)
    e_l = jnp.exp(b_last - b)
    return dict(sg=sg, forget=forget, e_i=e_i, e_ri=e_ri, e_b=e_b, e_l=e_l, dec=jnp.exp(b_last),
                qi=q * e_i, ki=kk * e_ri, qe=q * e_b, kl=kk * e_l)


HG_STEP_HEADS = 4


def _head_cols(hh):
    return slice(hh * HEAD_DIM, (hh + 1) * HEAD_DIM)


def _hgrn_forward(zhg, lb_raw, norm_w, exchange=None):
    t_len = zhg.shape[0]
    tb = min(512, t_len)
    n_chunks = tb // HG_CHUNK
    hs = HG_STEP_HEADS

    def body(q_ref, f_ref, v_ref, g_ref, lb_ref, w_ref, opre_ref, o_ref, st_ref, state):
        @pl.when(pl.program_id(1) == 0)
        def _():
            state[...] = jnp.zeros_like(state)

        causal = _tri(True)
        heads = range(hs)
        blk, v, qi, ki, qe, kl = {}, {}, {}, {}, {}, {}
        for hh in heads:
            cols = _head_cols(hh)
            blk[hh] = _hg_block(q_ref[:, cols], f_ref[:, cols], _lower_bound(lb_ref[:, cols]))
            v[hh] = v_ref[:, cols].astype(BF16)
            qi[hh], ki[hh], qe[hh], kl[hh] = (blk[hh][name].astype(BF16) for name in ("qi", "ki", "qe", "kl"))
        st = {hh: state[hh] for hh in heads}
        parts = {hh: [] for hh in heads}
        for n in range(n_chunks):
            r = _chunk_rows(n)
            for hh in heads:
                a = jnp.where(causal, _dot_nt(qi[hh][r], ki[hh][r]), 0.0).astype(BF16)
                st_ref[hh, n] = st[hh]
                parts[hh].append(_dot(a, v[hh][r]) + _dot_nt(qe[hh][r], st[hh].astype(BF16)))
                st[hh] = st[hh] * blk[hh]["dec"][n * HG_CHUNK:n * HG_CHUNK + 1] + _dot_tn(v[hh][r], kl[hh][r])
        for hh in heads:
            cols = _head_cols(hh)
            state[hh] = st[hh]
            o = jnp.concatenate(parts[hh], axis=0)
            opre_ref[:, cols] = o
            g = g_ref[:, cols]
            gated = o * lax.rsqrt(_rowmean(o * o) + RMS_EPS) * w_ref[:, cols] * (g * _sigmoid(g))
            o_ref[:, cols] = gated.astype(BF16)

    groups = N_HEADS // hs
    wide = hs * HEAD_DIM
    col = lambda off: (lambda h, t: (t, off + h))
    nb = t_len // tb
    return _pallas(
        body, name="hgrn_forward", grid=(groups, nb), operands=(zhg, zhg, zhg, zhg, lb_raw, norm_w),
        out_shape=[jax.ShapeDtypeStruct((t_len, N_HEADS * HEAD_DIM), F32),
                   jax.ShapeDtypeStruct((t_len, N_HEADS * HEAD_DIM), BF16),
                   jax.ShapeDtypeStruct((N_HEADS, t_len // HG_CHUNK, HEAD_DIM, HEAD_DIM), F32)],
        in_specs=[pl.BlockSpec((tb, wide), col(0)), pl.BlockSpec((tb, wide), col(groups)),
                  pl.BlockSpec((tb, wide), col(2 * groups)), pl.BlockSpec((tb, wide), col(3 * groups)),
                  pl.BlockSpec((2, wide), lambda h, t: (0, h)), pl.BlockSpec((1, wide), lambda h, t: (0, h))],
        out_specs=[pl.BlockSpec((tb, wide), col(0)), pl.BlockSpec((tb, wide), col(0)),
                   pl.BlockSpec((hs, n_chunks, HEAD_DIM, HEAD_DIM), lambda h, t: (h, t, 0, 0))],
        scratch_shapes=[pltpu.VMEM((hs, HEAD_DIM, HEAD_DIM), F32)],
        params=_params(40, ("arbitrary", "arbitrary")), exchange=exchange,
        first=lambda: (pl.program_id(0) == 0) & (pl.program_id(1) == 0),
        last=lambda: (pl.program_id(0) == groups - 1) & (pl.program_id(1) == nb - 1))


def _hgrn_backward(zhg, lb_raw, norm_w, o_pre, d_cat, states, exchange=None):
    t_len = zhg.shape[0]
    tb = min(512, t_len)
    n_chunks = tb // HG_CHUNK
    nb = t_len // tb
    hs = HG_STEP_HEADS

    def body(q_ref, f_ref, v_ref, g_ref, lb_ref, w_ref, opre_ref, do_ref, st_ref,
             dq_ref, df_ref, dv_ref, dg_ref, sums_ref, gstate):
        @pl.when(pl.program_id(1) == 0)
        def _():
            gstate[...] = jnp.zeros_like(gstate)
            sums_ref[...] = jnp.zeros_like(sums_ref)

        heads = range(hs)
        causal = _tri(True)
        row_id = lax.broadcasted_iota(jnp.int32, (HG_CHUNK, HEAD_DIM), 0)
        lb, d_o, blk, v, qi, ki, qe, kl = ({} for _ in range(8))
        for hh in heads:
            cols = _head_cols(hh)
            lb[hh] = _lower_bound(lb_ref[:, cols])
            w = w_ref[:, cols]
            o = opre_ref[:, cols]
            g = g_ref[:, cols]
            d_out = do_ref[:, cols]
            r = lax.rsqrt(_rowmean(o * o) + RMS_EPS)
            sg_g = _sigmoid(g)
            dg_ref[:, cols] = (d_out * (o * r * w) * (sg_g * (1.0 + g * (1.0 - sg_g)))).astype(BF16)
            d_on = d_out * (g * sg_g)
            sums_ref[1:2, cols] += _colsum(d_on * o * r)
            dy = d_on * w
            d_o[hh] = (r * dy - o * (r * r * r) * _rowmean(dy * o)).astype(BF16)
            blk[hh] = _hg_block(q_ref[:, cols], f_ref[:, cols], lb[hh])
            v[hh] = v_ref[:, cols].astype(BF16)
            qi[hh], ki[hh], qe[hh], kl[hh] = (blk[hh][name].astype(BF16) for name in ("qi", "ki", "qe", "kl"))
        gt = {hh: gstate[hh] for hh in heads}
        d_v, d_qi, d_ki, d_qe, d_kl, d_dec = ({hh: [None] * n_chunks for hh in heads} for _ in range(6))
        for n in reversed(range(n_chunks)):
            rows = _chunk_rows(n)
            for hh in heads:
                st = st_ref[hh, n]
                a = jnp.where(causal, _dot_nt(qi[hh][rows], ki[hh][rows]), 0.0).astype(BF16)
                d_a = jnp.where(causal, _dot_nt(d_o[hh][rows], v[hh][rows]), 0.0).astype(BF16)
                gt_b = gt[hh].astype(BF16)
                d_v[hh][n] = _dot_tn(a, d_o[hh][rows]) + _dot_nt(kl[hh][rows], gt_b)
                d_qi[hh][n] = _dot(d_a, ki[hh][rows])
                d_ki[hh][n] = _dot_tn(d_a, qi[hh][rows])
                d_qe[hh][n] = _dot(d_o[hh][rows], st.astype(BF16))
                d_kl[hh][n] = _dot(v[hh][rows], gt_b)
                d_dec[hh][n] = jnp.where(row_id == HG_CHUNK - 1, _colsum(gt[hh] * st), 0.0)
                gt[hh] = gt[hh] * blk[hh]["dec"][n * HG_CHUNK:n * HG_CHUNK + 1] + _dot_tn(d_o[hh][rows], qe[hh][rows])
        for hh in heads:
            cols = _head_cols(hh)
            b = blk[hh]
            gstate[hh] = gt[hh]
            dqi, dki, dqe, dkl, ddec = (jnp.concatenate(p[hh], axis=0) for p in (d_qi, d_ki, d_qe, d_kl, d_dec))
            dv_ref[:, cols] = jnp.concatenate(d_v[hh], axis=0).astype(BF16)
            dq_ref[:, cols] = (dqi * b["e_i"] + dqe * b["e_b"]).astype(BF16)
            d_k = dki * b["e_ri"] + dkl * b["e_l"]
            t_qi = dqi * b["qi"]
            t_ki = dki * b["ki"]
            t_kl = dkl * b["kl"]
            at_ref, at_last = [], []
            for n in range(n_chunks):
                rows = _chunk_rows(n)
                at_ref.append(jnp.where(row_id == HG_CHUNK // 2 - 1, _colsum(t_ki[rows] - t_qi[rows]), 0.0))
                at_last.append(jnp.where(row_id == HG_CHUNK - 1, _colsum(t_kl[rows]), 0.0))
            d_b = (t_qi - t_ki + dqe * b["qe"] - t_kl + jnp.concatenate(at_ref, axis=0)
                   + jnp.concatenate(at_last, axis=0) + ddec * b["dec"])
            d_forget = _chunk_prefix_sums(d_b, False) / b["forget"] - d_k
            sg = b["sg"]
            df_ref[:, cols] = (d_forget * (1.0 - lb[hh]) * sg * (1.0 - sg)).astype(BF16)
            sums_ref[0:1, cols] += _colsum(d_forget * (1.0 - sg))

    groups = N_HEADS // hs
    wide = hs * HEAD_DIM
    col = lambda off: (lambda h, t: (nb - 1 - t, off + h))
    return _pallas(
        body, name="hgrn_backward", grid=(groups, nb),
        operands=(zhg, zhg, zhg, zhg, lb_raw, norm_w, o_pre, d_cat, states),
        out_shape=[jax.ShapeDtypeStruct((t_len, N_HEADS * HEAD_DIM), BF16)] * 4
        + [jax.ShapeDtypeStruct((8, N_HEADS * HEAD_DIM), F32)],
        in_specs=[pl.BlockSpec((tb, wide), col(0)), pl.BlockSpec((tb, wide), col(groups)),
                  pl.BlockSpec((tb, wide), col(2 * groups)), pl.BlockSpec((tb, wide), col(3 * groups)),
                  pl.BlockSpec((2, wide), lambda h, t: (0, h)), pl.BlockSpec((1, wide), lambda h, t: (0, h)),
                  pl.BlockSpec((tb, wide), col(0)), pl.BlockSpec((tb, wide), col(0)),
                  pl.BlockSpec((hs, n_chunks, HEAD_DIM, HEAD_DIM), lambda h, t: (h, nb - 1 - t, 0, 0))],
        out_specs=[pl.BlockSpec((tb, wide), col(0))] * 4 + [pl.BlockSpec((8, wide), lambda h, t: (0, h))],
        scratch_shapes=[pltpu.VMEM((hs, HEAD_DIM, HEAD_DIM), F32)],
        params=_params(40, ("arbitrary", "arbitrary")), exchange=exchange,
        first=lambda: (pl.program_id(0) == 0) & (pl.program_id(1) == 0),
        last=lambda: (pl.program_id(0) == groups - 1) & (pl.program_id(1) == nb - 1))


ATT_LOG2 = ATT_SCALE * 1.4426950408889634


def _triangle_steps(nq, q_major):
    if q_major:
        pairs = [(i, j) for i in range(nq) for j in range(i + 1)]
    else:
        pairs = [(i, j) for j in range(nq) for i in range(j, nq)]
    return jnp.array([p[0] for p in pairs], jnp.int32), jnp.array([p[1] for p in pairs], jnp.int32)


def _key_le_query(t):
    return lax.broadcasted_iota(jnp.int32, (t, t), 0) <= lax.broadcasted_iota(jnp.int32, (t, t), 1)


def _attention_forward(q, k, v_t, exchange=None):
    t_len = q.shape[1]
    tq = min(512, t_len)
    nq = t_len // tq
    qi_tab, ki_tab = _triangle_steps(nq, True)

    def body(qi_ref, ki_ref, q_ref, k_ref, vt_ref, o_ref, lse_ref, m_s, l_s, acc_s):
        step = pl.program_id(0)
        qi, ki = qi_ref[step], ki_ref[step]

        @pl.when(ki == 0)
        def _():
            m_s[...] = jnp.full_like(m_s, NEG_BIG)
            l_s[...] = jnp.zeros_like(l_s)
            acc_s[...] = jnp.zeros_like(acc_s)

        def accumulate(masked):
            s_all = [_dot_nt(k_ref[h], q_ref[h]) * ATT_LOG2 for h in range(N_HEADS)]
            for h in range(N_HEADS):
                s_t = s_all[h]
                if masked:
                    s_t = jnp.where(_key_le_query(tq), s_t, NEG_BIG)
                m_old = m_s[h]
                m_new = jnp.maximum(m_old, jnp.max(s_t, axis=0, keepdims=True))
                alpha = jnp.exp2(m_old - m_new)
                p_t = jnp.exp2(s_t - m_new)
                l_s[h] = alpha * l_s[h] + jnp.sum(p_t, axis=0, keepdims=True)
                acc_s[h] = alpha * acc_s[h] + _dot(vt_ref[h], p_t.astype(BF16))
                m_s[h] = m_new

        @pl.when(ki < qi)
        def _():
            accumulate(False)

        @pl.when(ki == qi)
        def _():
            accumulate(True)
            for h in range(N_HEADS):
                o_ref[:, h * HEAD_DIM:(h + 1) * HEAD_DIM] = jnp.transpose(acc_s[h] / l_s[h])
                lse_ref[h] = m_s[h] + jnp.log2(l_s[h])

    n_steps = qi_tab.shape[0]
    return _pallas(
        body, name="attention_forward", grid=(n_steps,), prefetch=(qi_tab, ki_tab), operands=(q, k, v_t),
        out_shape=[jax.ShapeDtypeStruct((t_len, N_HEADS * HEAD_DIM), F32),
                   jax.ShapeDtypeStruct((N_HEADS, 1, t_len), F32)],
        in_specs=[pl.BlockSpec((N_HEADS, tq, QK_DIM), lambda s, qt, kt: (0, qt[s], 0)),
                  pl.BlockSpec((N_HEADS, tq, QK_DIM), lambda s, qt, kt: (0, kt[s], 0)),
                  pl.BlockSpec((N_HEADS, HEAD_DIM, tq), lambda s, qt, kt: (0, 0, kt[s]))],
        out_specs=[pl.BlockSpec((tq, N_HEADS * HEAD_DIM), lambda s, qt, kt: (qt[s], 0)),
                   pl.BlockSpec((N_HEADS, 1, tq), lambda s, qt, kt: (0, 0, qt[s]))],
        scratch_shapes=[pltpu.VMEM((N_HEADS, 1, tq), F32), pltpu.VMEM((N_HEADS, 1, tq), F32),
                        pltpu.VMEM((N_HEADS, HEAD_DIM, tq), F32)],
        params=_params(48, ("arbitrary",)), exchange=exchange,
        first=lambda qt, kt: pl.program_id(0) == 0, last=lambda qt, kt: pl.program_id(0) == n_steps - 1)


BWD_HEADS = 4


def _attention_backward(q, k, k_t, v, d_cat, lse, delta, exchange=None):
    t_len = q.shape[1]
    tq = min(512, t_len)
    nq = t_len // tq
    hp = BWD_HEADS
    qi_tab, ki_tab = _triangle_steps(nq, False)

    def body(qi_ref, ki_ref, q_ref, k_ref, kt_ref, v_ref, do_ref, lse_ref, delta_ref, dqt_hbm, dk_ref, dv_ref,
             dqt_s, dk_s, dv_s):
        group, step = pl.program_id(0), pl.program_id(1)
        qi, ki = qi_ref[step], ki_ref[step]

        @pl.when(step == 0)
        def _():
            dqt_s[...] = jnp.zeros_like(dqt_s)

        @pl.when(qi == ki)
        def _():
            dk_s[...] = jnp.zeros_like(dk_s)
            dv_s[...] = jnp.zeros_like(dv_s)

        def accumulate(masked):
            for h in range(hp):
                do_b = do_ref[:, h * HEAD_DIM:(h + 1) * HEAD_DIM].astype(BF16)
                s_t = _dot_nt(k_ref[h], q_ref[h]) * ATT_LOG2
                if masked:
                    s_t = jnp.where(_key_le_query(tq), s_t, NEG_BIG)
                p_t = jnp.exp2(s_t - lse_ref[h])
                dp_t = _dot_nt(v_ref[h], do_b)
                ds_t = (p_t * (dp_t - delta_ref[h]) * ATT_SCALE).astype(BF16)
                dv_s[h] += _dot(p_t.astype(BF16), do_b)
                dk_s[h] += _dot(ds_t, q_ref[h])
                dqt_s[h, qi, 0:QK_REAL, :] += _dot(kt_ref[h, 0:QK_REAL, :], ds_t)

        @pl.when(ki < qi)
        def _():
            accumulate(False)

        @pl.when(ki == qi)
        def _():
            accumulate(True)
            for h in range(hp):
                pltpu.sync_copy(dqt_s.at[h, qi], dqt_hbm.at[group * hp + h, qi])

        @pl.when(qi == nq - 1)
        def _():
            dk_ref[...] = dk_s[...]
            dv_ref[...] = dv_s[...]

    wide = hp * HEAD_DIM
    n_groups, n_steps = N_HEADS // hp, qi_tab.shape[0]
    return _pallas(
        body, name="attention_backward", grid=(n_groups, n_steps), prefetch=(qi_tab, ki_tab),
        operands=(q, k, k_t, v, d_cat, lse, delta),
        out_shape=[jax.ShapeDtypeStruct((N_HEADS, nq, QK_DIM, tq), F32),
                   jax.ShapeDtypeStruct((N_HEADS, t_len, QK_DIM), F32),
                   jax.ShapeDtypeStruct((N_HEADS, t_len, HEAD_DIM), F32)],
        in_specs=[pl.BlockSpec((hp, tq, QK_DIM), lambda g, s, qt, kt: (g, qt[s], 0)),
                  pl.BlockSpec((hp, tq, QK_DIM), lambda g, s, qt, kt: (g, kt[s], 0)),
                  pl.BlockSpec((hp, QK_DIM, tq), lambda g, s, qt, kt: (g, 0, kt[s])),
                  pl.BlockSpec((hp, tq, HEAD_DIM), lambda g, s, qt, kt: (g, kt[s], 0)),
                  pl.BlockSpec((tq, wide), lambda g, s, qt, kt: (qt[s], n_groups + g)),
                  pl.BlockSpec((hp, 1, tq), lambda g, s, qt, kt: (g, 0, qt[s])),
                  pl.BlockSpec((hp, 1, tq), lambda g, s, qt, kt: (g, 0, qt[s]))],
        out_specs=[pl.BlockSpec(memory_space=pl.ANY),
                   pl.BlockSpec((hp, tq, QK_DIM), lambda g, s, qt, kt: (g, kt[s], 0)),
                   pl.BlockSpec((hp, tq, HEAD_DIM), lambda g, s, qt, kt: (g, kt[s], 0))],
        scratch_shapes=[pltpu.VMEM((hp, nq, QK_DIM, tq), F32), pltpu.VMEM((hp, tq, QK_DIM), F32),
                        pltpu.VMEM((hp, tq, HEAD_DIM), F32)],
        params=_params(58, ("arbitrary", "arbitrary")), exchange=exchange,
        first=lambda qt, kt: (pl.program_id(0) == 0) & (pl.program_id(1) == 0),
        last=lambda qt, kt: (pl.program_id(0) == n_groups - 1) & (pl.program_id(1) == n_steps - 1))


def _out_project(o_hg, o_mla, x, g_a, w_out, exchange=None):
    t_len = x.shape[0]
    tm = min(512, t_len)
    half = N_HEADS * HEAD_DIM

    def body(ohg_ref, omla_ref, x_ref, ga_ref, w_ref, cat_ref, mix_ref, xhat_ref, rstd_ref):
        a = ohg_ref[...].astype(BF16)
        b = omla_ref[...].astype(BF16)
        cat_ref[:, 0:half] = a
        cat_ref[:, half:2 * half] = b
        mix = _dot(a, w_ref[0:half, :]) + _dot(b, w_ref[half:2 * half, :])
        mix_ref[...] = mix
        r1 = DN_ALPHA * x_ref[...] + (1.0 + ga_ref[...]) * mix
        xc = r1 - _rowmean(r1)
        rstd = lax.rsqrt(_rowmean(xc * xc) + LN_EPS)
        xhat_ref[...] = xc * rstd
        rstd_ref[...] = rstd

    row = lambda i: (i, 0)
    fixed = lambda i: (0, 0)
    n_tiles = t_len // tm
    return _pallas(
        body, name="out_project", grid=(n_tiles,), operands=(o_hg, o_mla, x, g_a, w_out),
        out_shape=[jax.ShapeDtypeStruct((t_len, D_MODEL), BF16), jax.ShapeDtypeStruct((t_len, D_MODEL), F32),
                   jax.ShapeDtypeStruct((t_len, D_MODEL), F32), jax.ShapeDtypeStruct((t_len, 1), F32)],
        in_specs=[pl.BlockSpec((tm, half), row), pl.BlockSpec((tm, half), row), pl.BlockSpec((tm, D_MODEL), row),
                  pl.BlockSpec((1, D_MODEL), fixed), pl.BlockSpec((D_MODEL, D_MODEL), fixed)],
        out_specs=[pl.BlockSpec((tm, D_MODEL), row), pl.BlockSpec((tm, D_MODEL), row),
                   pl.BlockSpec((tm, D_MODEL), row), pl.BlockSpec((tm, 1), row)],
        params=_params(48, ("arbitrary",)), exchange=exchange,
        first=lambda: pl.program_id(0) == 0, last=lambda: pl.program_id(0) == n_tiles - 1)


V_LN1G, V_LN1B, V_SCM, V_SHM, V_GM, V_GA, V_LN2G, V_LN2B = range(8)
S_DLN2G, S_DLN2B, S_DGM, S_DSCM, S_DSHM, S_DLN1G, S_DLN1B, S_DGA, S_LOSS = range(9)


def _mlp_and_back(xhat1, rstd1, mix, target, o_mla, vecs, w1_top, w1_bottom, w2, w_out):
    t_len = xhat1.shape[0]
    tm = min(256, t_len)
    n_ff = w1_top.shape[0]
    ff = w1_top.shape[2]
    top_rows = w1_top.shape[1]

    def body(xhat_ref, rstd_ref, mix_ref, tgt_ref, omla_ref, vec_ref, w1_top_hbm, w1_bottom_hbm, w2_hbm, wout_hbm,
             act_ref, dhp_ref, um_ref, dh_ref, dmix_ref, dcat_ref, dr1_ref, sums_ref, delta_ref,
             w1_s, w2_s, wout_s, hp_s, load_sems):
        @pl.when(pl.program_id(0) == 0)
        def _():
            loads = [pltpu.make_async_copy(w1_top_hbm, w1_s.at[:, 0:top_rows], load_sems.at[0]),
                     pltpu.make_async_copy(w1_bottom_hbm, w1_s.at[:, top_rows:D_MODEL], load_sems.at[3]),
                     pltpu.make_async_copy(w2_hbm, w2_s, load_sems.at[1]),
                     pltpu.make_async_copy(wout_hbm, wout_s, load_sems.at[2])]
            for cp in loads:
                cp.start()
            sums_ref[...] = jnp.zeros_like(sums_ref)
            for cp in loads:
                cp.wait()

        vec = lambda r: vec_ref[r:r + 1, :]
        xhat = xhat_ref[...]
        x1 = xhat * vec(V_LN1G) + vec(V_LN1B)
        um = (x1 * (1.0 + vec(V_SCM)) + vec(V_SHM)).astype(BF16)
        um_ref[...] = um
        h = jnp.zeros((tm, D_MODEL), F32)
        for j in range(n_ff):
            hp = _dot(um, w1_s[j])
            hp_s[j] = hp
            act = jnp.square(jnp.maximum(hp, 0.0)).astype(BF16)
            act_ref[:, j * ff:(j + 1) * ff] = act
            h = h + _dot(act, w2_s[j])
        r2 = DN_ALPHA * x1 + (1.0 + vec(V_GM)) * h
        xc = r2 - _rowmean(r2)
        rstd2 = lax.rsqrt(_rowmean(xc * xc) + LN_EPS)
        xhat2 = xc * rstd2
        err = xhat2 * vec(V_LN2G) + vec(V_LN2B) - tgt_ref[...]
        loss = 0.5 * jnp.sum(_rowmean(err * err))
        dy = err * (1.0 / D_MODEL)
        dxh = dy * vec(V_LN2G)
        dr2 = rstd2 * (dxh - _rowmean(dxh) - xhat2 * _rowmean(dxh * xhat2))
        dh = ((1.0 + vec(V_GM)) * dr2).astype(BF16)
        dh_ref[...] = dh
        sums_ref[S_DLN2G:S_DLN2G + 1, :] += _colsum(dy * xhat2)
        sums_ref[S_DLN2B:S_DLN2B + 1, :] += _colsum(dy)
        sums_ref[S_DGM:S_DGM + 1, :] += _colsum(dr2 * h)
        sums_ref[S_LOSS:S_LOSS + 1, :] += jnp.full((1, D_MODEL), loss, F32)
        du = jnp.zeros((tm, D_MODEL), F32)
        for j in range(n_ff):
            dhp = (_dot_nt(dh, w2_s[j]) * (2.0 * jnp.maximum(hp_s[j], 0.0))).astype(BF16)
            dhp_ref[:, j * ff:(j + 1) * ff] = dhp
            du = du + _dot_nt(dhp, w1_s[j])
        sums_ref[S_DSCM:S_DSCM + 1, :] += _colsum(du * x1)
        sums_ref[S_DSHM:S_DSHM + 1, :] += _colsum(du)
        dx1 = DN_ALPHA * dr2 + du * (1.0 + vec(V_SCM))
        sums_ref[S_DLN1G:S_DLN1G + 1, :] += _colsum(dx1 * xhat)
        sums_ref[S_DLN1B:S_DLN1B + 1, :] += _colsum(dx1)
        dxh1 = dx1 * vec(V_LN1G)
        dr1 = rstd_ref[...] * (dxh1 - _rowmean(dxh1) - xhat * _rowmean(dxh1 * xhat))
        dr1_ref[...] = dr1
        sums_ref[S_DGA:S_DGA + 1, :] += _colsum(dr1 * mix_ref[...])
        dmix = ((1.0 + vec(V_GA)) * dr1).astype(BF16)
        dmix_ref[...] = dmix
        dcat = _dot_nt(dmix, wout_s[...])
        dcat_ref[...] = dcat
        half = N_HEADS * HEAD_DIM
        for hd in range(N_HEADS):
            prod = dcat[:, half + hd * HEAD_DIM:half + (hd + 1) * HEAD_DIM] * omla_ref[:, hd * HEAD_DIM:(hd + 1) * HEAD_DIM]
            sums = jnp.broadcast_to(jnp.sum(prod, axis=1, keepdims=True), (tm, HEAD_DIM))
            delta_ref[hd] = jnp.transpose(sums)[0:1]

    row = lambda i: (i, 0)
    fixed = lambda i: (0, 0)
    any_spec = pl.BlockSpec(memory_space=pl.ANY)
    return _pcall(
        body, name="mlp_and_back", grid=(t_len // tm,),
        out_shape=[jax.ShapeDtypeStruct((t_len, D_FF), BF16), jax.ShapeDtypeStruct((t_len, D_FF), BF16),
                   jax.ShapeDtypeStruct((t_len, D_MODEL), BF16), jax.ShapeDtypeStruct((t_len, D_MODEL), BF16),
                   jax.ShapeDtypeStruct((t_len, D_MODEL), BF16), jax.ShapeDtypeStruct((t_len, D_MODEL), F32),
                   jax.ShapeDtypeStruct((t_len, D_MODEL), F32), jax.ShapeDtypeStruct((16, D_MODEL), F32),
                   jax.ShapeDtypeStruct((N_HEADS, 1, t_len), F32)],
        in_specs=[pl.BlockSpec((tm, D_MODEL), row), pl.BlockSpec((tm, 1), row), pl.BlockSpec((tm, D_MODEL), row),
                  pl.BlockSpec((tm, D_MODEL), row), pl.BlockSpec((tm, N_HEADS * HEAD_DIM), row),
                  pl.BlockSpec((8, D_MODEL), fixed), any_spec, any_spec, any_spec, any_spec],
        out_specs=[pl.BlockSpec((tm, D_FF), row), pl.BlockSpec((tm, D_FF), row), pl.BlockSpec((tm, D_MODEL), row),
                   pl.BlockSpec((tm, D_MODEL), row), pl.BlockSpec((tm, D_MODEL), row), pl.BlockSpec((tm, D_MODEL), row),
                   pl.BlockSpec((tm, D_MODEL), row), pl.BlockSpec((16, D_MODEL), fixed),
                   pl.BlockSpec((N_HEADS, 1, tm), lambda i: (0, 0, i))],
        scratch_shapes=[pltpu.VMEM((n_ff, D_MODEL, ff), BF16), pltpu.VMEM(w2.shape, BF16), pltpu.VMEM(w_out.shape, BF16),
                        pltpu.VMEM((n_ff, tm, ff), F32), pltpu.SemaphoreType.DMA((4,))],
        compiler_params=_params(56, ("arbitrary",)),
        operands=(xhat1, rstd1, mix, target, o_mla, vecs, w1_top, w1_bottom, w2, w_out))


def _in_project_backward(dq, dk, dv, cq, ckv, pos, invf, q_norm_w, kv_norm_w, w_q, w_kv,
                         d_hq, d_hf, d_hi, d_hg, w_in, dr1, x, sc_a, exchange=None):
    t_len = x.shape[0]
    tm = min(512, t_len)
    per_q = dq.shape[3] // tm
    hgw = N_HEADS * HEAD_DIM

    def body(dq_ref, dk_ref, dv_ref, cq_ref, ckv_ref, pos_ref, invf_ref, qn_ref, kvn_ref, wq_ref, wkv_ref,
             dhq_ref, dhf_ref, dhi_ref, dhg_ref, win_ref, dr1_ref, x_ref, sc_ref,
             dz_ref, gx_ref, sums_ref, dwq_ref, dwkv_ref):
        @pl.when(pl.program_id(0) == 0)
        def _():
            sums_ref[...] = jnp.zeros_like(sums_ref)
            dwq_ref[...] = jnp.zeros_like(dwq_ref)
            dwkv_ref[...] = jnp.zeros_like(dwkv_ref)

        cos_t, sin_t = _rope_tables(pos_ref[...], invf_ref[...])
        cq = cq_ref[...]
        ckv = ckv_ref[...]
        rq = lax.rsqrt(_rowmean(cq * cq) + RMS_EPS)
        rkv = lax.rsqrt(_rowmean(ckv * ckv) + RMS_EPS)
        cqn = (cq * rq * qn_ref[...]).astype(BF16)
        ckvn = (ckv * rkv * kvn_ref[...]).astype(BF16)
        d_cqn = jnp.zeros((tm, Q_RANK), F32)
        d_ckvn = jnp.zeros((tm, KV_RANK), F32)
        d_kpe = jnp.zeros((tm, 128), F32)
        for h in range(N_HEADS):
            dqh = jnp.transpose(dq_ref[h])
            dq_full = jnp.concatenate(
                [dqh[:, :HEAD_DIM].astype(BF16), _unrope(dqh[:, HEAD_DIM:], cos_t, sin_t).astype(BF16)], axis=1)
            d_cqn = d_cqn + _dot_nt(dq_full, wq_ref[h])
            dwq_ref[h] += _dot_tn(cqn, dq_full)
            dkh = dk_ref[h]
            d_kpe = d_kpe + dkh[:, HEAD_DIM:]
            dkv_up = jnp.concatenate([dkh[:, :HEAD_DIM].astype(BF16), dv_ref[h].astype(BF16)], axis=1)
            d_ckvn = d_ckvn + _dot_nt(dkv_up, wkv_ref[h])
            dwkv_ref[h] += _dot_tn(ckvn, dkv_up)
        dyq = d_cqn * qn_ref[...]
        dykv = d_ckvn * kvn_ref[...]
        sums_ref[2:3, 0:Q_RANK] += _colsum(d_cqn * cq * rq)
        sums_ref[3:4, 0:KV_RANK] += _colsum(d_ckvn * ckv * rkv)
        dz_ref[:, 0:hgw] = dhq_ref[...]
        dz_ref[:, hgw:2 * hgw] = dhf_ref[...]
        dz_ref[:, 2 * hgw:3 * hgw] = dhi_ref[...]
        dz_ref[:, 3 * hgw:4 * hgw] = dhg_ref[...]
        dz_ref[:, HG_COLS:HG_COLS + Q_RANK] = (rq * dyq - cq * (rq * rq * rq) * _rowmean(dyq * cq)).astype(BF16)
        dz_ref[:, HG_COLS + Q_RANK:HG_COLS + Q_RANK + KV_RANK] = (
            rkv * dykv - ckv * (rkv * rkv * rkv) * _rowmean(dykv * ckv)).astype(BF16)
        dz_ref[:, HG_COLS + Q_RANK + KV_RANK:] = _unrope(d_kpe, cos_t, sin_t).astype(BF16)
        du = _dot(dz_ref[...], win_ref[...])
        xv = x_ref[...]
        gx_ref[...] = DN_ALPHA * dr1_ref[...] + (1.0 + sc_ref[...]) * du
        sums_ref[0:1, :] += _colsum(du * xv)
        sums_ref[1:2, :] += _colsum(du)

    row = lambda i: (i, 0)
    fixed2 = lambda i: (0, 0)
    fixed3 = lambda i: (0, 0, 0)
    heads = lambda i: (0, i, 0)
    n_tiles = t_len // tm
    return _pallas(
        body, name="in_project_backward", grid=(n_tiles,),
        operands=(dq, dk, dv, cq, ckv, pos, invf, q_norm_w, kv_norm_w, w_q, w_kv, d_hq, d_hf, d_hi, d_hg, w_in, dr1, x,
                  sc_a),
        out_shape=[jax.ShapeDtypeStruct((t_len, IN_COLS_PAD), BF16), jax.ShapeDtypeStruct((t_len, D_MODEL), F32),
                   jax.ShapeDtypeStruct((8, D_MODEL), F32), jax.ShapeDtypeStruct((N_HEADS, Q_RANK, QK_DIM), F32),
                   jax.ShapeDtypeStruct((N_HEADS, KV_RANK, 2 * HEAD_DIM), F32)],
        in_specs=[pl.BlockSpec((N_HEADS, None, QK_DIM, tm), lambda i: (0, i // per_q, 0, i % per_q)),
                  pl.BlockSpec((N_HEADS, tm, QK_DIM), heads),
                  pl.BlockSpec((N_HEADS, tm, HEAD_DIM), heads), pl.BlockSpec((tm, Q_RANK), row),
                  pl.BlockSpec((tm, KV_RANK), row), pl.BlockSpec((tm, 1), row), pl.BlockSpec((1, 128), fixed2),
                  pl.BlockSpec((1, Q_RANK), fixed2), pl.BlockSpec((1, KV_RANK), fixed2),
                  pl.BlockSpec((N_HEADS, Q_RANK, QK_DIM), fixed3), pl.BlockSpec((N_HEADS, KV_RANK, 2 * HEAD_DIM), fixed3),
                  pl.BlockSpec((tm, hgw), row), pl.BlockSpec((tm, hgw), row), pl.BlockSpec((tm, hgw), row),
                  pl.BlockSpec((tm, hgw), row), pl.BlockSpec((IN_COLS_PAD, D_MODEL), fixed2),
                  pl.BlockSpec((tm, D_MODEL), row), pl.BlockSpec((tm, D_MODEL), row), pl.BlockSpec((1, D_MODEL), fixed2)],
        out_specs=[pl.BlockSpec((tm, IN_COLS_PAD), row), pl.BlockSpec((tm, D_MODEL), row),
                   pl.BlockSpec((8, D_MODEL), fixed2), pl.BlockSpec((N_HEADS, Q_RANK, QK_DIM), fixed3),
                   pl.BlockSpec((N_HEADS, KV_RANK, 2 * HEAD_DIM), fixed3)],
        params=_params(48, ("arbitrary",)), exchange=exchange,
        first=lambda: pl.program_id(0) == 0, last=lambda: pl.program_id(0) == n_tiles - 1)


def _weight_grad(a, b, name, n_blocks, bn, a_blocked=False, b_blocked=True, exchange=None, token_tile=512):
    t_len = a.shape[0]
    m = a.shape[1] // n_blocks if a_blocked else a.shape[1]
    bt = min(token_tile, t_len)

    def body(a_ref, b_ref, o_ref):
        @pl.when(pl.program_id(1) == 0)
        def _():
            o_ref[...] = jnp.zeros_like(o_ref)

        o_ref[...] += _dot_tn(a_ref[...].astype(BF16), b_ref[...].astype(BF16))

    a_spec = pl.BlockSpec((bt, m), (lambda n, t: (t, n)) if a_blocked else (lambda n, t: (t, 0)))
    b_spec = pl.BlockSpec((bt, bn), (lambda n, t: (t, n)) if b_blocked else (lambda n, t: (t, 0)))
    nt = t_len // bt
    (out,), landed = _pallas(
        body, name=name, grid=(n_blocks, nt), operands=(a, b),
        out_shape=[jax.ShapeDtypeStruct((n_blocks, m, bn), F32)],
        in_specs=[a_spec, b_spec],
        out_specs=[pl.BlockSpec((None, m, bn), lambda n, t: (n, 0, 0))],
        params=_params(56, ("arbitrary", "arbitrary")), exchange=exchange,
        first=lambda: (pl.program_id(0) == 0) & (pl.program_id(1) == 0),
        last=lambda: (pl.program_id(0) == n_blocks - 1) & (pl.program_id(1) == nt - 1))
    return (out, landed) if exchange else out


SMALL_PLACE = {"ln1_g": (6, 0), "ln1_b": (7, 0), "ln2_g": (8, 0), "ln2_b": (9, 0), "hg_norm_w": (10, 512),
               "mla_q_norm_w": (11, 0), "mla_kv_norm_w": (11, Q_RANK)}
SMALL_LB_ROW, SMALL_LOSS_ROW = 10, 12


def _small_params_step(gathered, params):
    names = list(params)

    def body(g_ref, *refs):
        ins, outs = refs[:3 * len(names)], refs[3 * len(names):]
        loss_ref, outs = outs[0], outs[1:]
        tot = g_ref[0]
        for d in range(1, N_DEV):
            tot = tot + g_ref[d]
        loss_ref[...] = tot[SMALL_LOSS_ROW:SMALL_LOSS_ROW + 1, 0:128]

        def update(i, grad, rows=slice(None), lanes=slice(None)):
            w_ref, m_ref, v_ref = ins[3 * i:3 * i + 3]
            g_out, d_out, nm_out, nv_out = outs[4 * i:4 * i + 4]
            g_out[rows, lanes] = grad
            d_out[rows, lanes], nm_out[rows, lanes], nv_out[rows, lanes] = _adamw_update(
                w_ref[rows, lanes], grad, m_ref[rows, lanes], v_ref[rows, lanes])

        for i, name in enumerate(names):
            if name == "b_ada":
                for r in range(6):
                    update(i, tot[r:r + 1, :], lanes=slice(r * D_MODEL, (r + 1) * D_MODEL))
            elif name == "hg_lower_bounds":
                lb = _lower_bound(ins[3 * i][...])
                d0 = tot[SMALL_LB_ROW:SMALL_LB_ROW + 1, 0:512] * lb * (1.0 - lb)
                update(i, d0, rows=slice(0, 1))
                update(i, -d0, rows=slice(1, 2))
            else:
                row, lane = SMALL_PLACE[name]
                update(i, tot[row:row + 1, lane:lane + params[name][0].shape[1]])

    flat_in = [a for name in names for a in params[name]]
    shapes = [jax.ShapeDtypeStruct((1, 128), F32)] + [jax.ShapeDtypeStruct(params[name][0].shape, F32)
                                                      for name in names for _ in range(4)]
    out = pl.pallas_call(body, name="small_params_step", out_shape=shapes)(gathered, *flat_in)
    return out[0], {name: out[1 + 4 * i:5 + 4 * i] for i, name in enumerate(names)}


def _adamw_update(w, gv, m, v):
    nm = ADAM_B1 * m + (1.0 - ADAM_B1) * gv
    nv = ADAM_B2 * v + (1.0 - ADAM_B2) * jnp.square(gv)
    m_hat = nm / (1.0 - ADAM_B1 ** ADAM_STEP)
    v_hat = nv / (1.0 - ADAM_B2 ** ADAM_STEP)
    return -ADAM_LR * (m_hat / (jnp.sqrt(v_hat) + ADAM_EPS) + ADAM_WD * w), nm, nv


def _adamw_halves(core, w, mine, theirs, m, v, name):
    rows, cols = w.shape
    h = rows // 2
    tr = _row_tile(h)
    per_half = h // tr

    def body(core_ref, w_ref, mine_ref, theirs_ref, m_ref, v_ref, g_ref, d_ref, nm_ref, nv_ref):
        is_mine = pl.program_id(0) // per_half == core_ref[0]
        gv = jnp.where(is_mine, mine_ref[...], theirs_ref[...])
        g_ref[...] = gv
        d_ref[...], nm_ref[...], nv_ref[...] = _adamw_update(w_ref[...], gv, m_ref[...], v_ref[...])

    full = pl.BlockSpec((tr, cols), lambda i, core_ref: (i, 0))
    part = pl.BlockSpec((tr, cols), lambda i, core_ref: (i % per_half, 0))
    return _pcall(
        body, name=name, out_shape=[jax.ShapeDtypeStruct(w.shape, F32)] * 4,
        grid_spec=pltpu.PrefetchScalarGridSpec(
            num_scalar_prefetch=1, grid=(rows // tr,), in_specs=[full, part, part, full, full], out_specs=[full] * 4),
        compiler_params=_params(40, ("arbitrary",)),
        operands=(core, w, mine, theirs, m, v))


def _adamw(w, g, m, v, name):
    rows, cols = w.shape
    tr = _row_tile(rows) if rows >= 8 else rows

    def body(w_ref, g_ref, m_ref, v_ref, d_ref, nm_ref, nv_ref):
        d_ref[...], nm_ref[...], nv_ref[...] = _adamw_update(w_ref[...], g_ref[...], m_ref[...], v_ref[...])

    spec = pl.BlockSpec((tr, cols), lambda i: (i, 0))
    return _pcall(
        body, name=name, grid=(rows // tr,),
        out_shape=[jax.ShapeDtypeStruct(w.shape, F32)] * 3,
        in_specs=[spec] * 4, out_specs=[spec] * 3,
        compiler_params=_params(40, ("arbitrary",)),
        operands=(w, g, m, v))


def kernel(x, c, positions, w_ada, b_ada, w_in, hg_lower_bounds, hg_norm_w, mla_q_norm_w, w_q_up, mla_kv_norm_w, w_kv_up, w_out, ln1_g, ln1_b, w_mlp_in, w_mlp_out, ln2_g, ln2_b, loss_target, m_w_ada, m_b_ada, m_w_in, m_hg_lower_bounds, m_hg_norm_w, m_mla_q_norm_w, m_w_q_up, m_mla_kv_norm_w, m_w_kv_up, m_w_out, m_ln1_g, m_ln1_b, m_w_mlp_in, m_w_mlp_out, m_ln2_g, m_ln2_b, v_w_ada, v_b_ada, v_w_in, v_hg_lower_bounds, v_hg_norm_w, v_mla_q_norm_w, v_w_q_up, v_mla_kv_norm_w, v_w_kv_up, v_w_out, v_ln1_g, v_ln1_b, v_w_mlp_in, v_w_mlp_out, v_ln2_g, v_ln2_b):
    ix, iy, ic = _mesh_pos()
    chip = 2 * ix + iy
    me = 4 * ix + 2 * iy + ic
    core_arr = jnp.reshape(ic, (1,)).astype(jnp.int32)
    chip_arr = jnp.reshape(chip, (1,)).astype(jnp.int32)

    xs = x[0]
    target = loss_target[0]
    t_len = xs.shape[0]
    pos = positions.astype(F32).reshape(t_len, 1)
    inv = 1.0 / (ROPE_THETA ** (jnp.arange(0, ROPE_DIM, 2, dtype=F32) / ROPE_DIM))
    invf = jnp.concatenate([inv, inv, jnp.zeros((128 - ROPE_DIM,), F32)]).reshape(1, 128)

    def slot(w):
        rows, cols = w.shape
        own = w.astype(BF16).reshape(1, 2, rows // 2, cols)
        return lax.dynamic_update_slice(jnp.zeros((N_CHIPS, 2, rows // 2, cols), BF16), own, (chip, 0, 0, 0))

    def slot8(a):
        return lax.dynamic_update_slice(jnp.zeros((N_DEV,) + a.shape, a.dtype), a[None], (me, 0, 0))

    def whole(s):
        return s.reshape(N_CHIPS, 2 * s.shape[2], s.shape[3])

    def halved(g):
        return g.reshape(N_CHIPS, 2, g.shape[1] // 2, g.shape[2])

    ada_cols = w_ada.shape[2]
    c_all, *early = _run_exchange(
        _merge(_gather_all(slot8(jnp.broadcast_to(c, (8, D_MODEL)))),
               _gather_over_ici([slot(jnp.transpose(w_in[0])), slot(w_q_up[0]), slot(w_kv_up[0])])),
        "gather_c_and_mixer_weights_ici")
    b_shard = lax.dynamic_slice(b_ada, (0, chip * ada_cols), (1, ada_cols))
    mod_cols, cond16 = _ada_project(c_all[:, 0, :], w_ada[0], b_shard)
    mod_all, *early = _run_exchange(_merge(_gather_all(slot8(mod_cols)), _gather_over_d2d(early)),
                                    "gather_mod_and_mixer_weights_d2d")
    mod_mine = lax.dynamic_slice(mod_all, (0, me, 0), (N_DEV, 1, ada_cols))[::2, 0, :].reshape(6, D_MODEL)
    sh_a, sc_a, g_a, sh_m, sc_m, g_m = (mod_mine[i:i + 1] for i in range(6))
    g_in, g_q, g_kv = (whole(s) for s in early)
    w_in_full = jnp.pad(g_in.reshape(IN_COLS, D_MODEL), ((0, IN_COLS_PAD - IN_COLS), (0, 0)))
    w_q_full = jnp.pad(g_q, ((0, 0), (0, 0), (0, QK_DIM - g_q.shape[2])))

    w1_rows = D_MODEL // 2
    (u_a, zhg, cq, ckv, q, k, k_t, v, v_t), (s_top, s_out) = _in_project(
        xs, pos, sc_a, sh_a, w_in_full, mla_q_norm_w, mla_kv_norm_w, w_q_full, g_kv, invf,
        _gather_over_ici([slot(w_mlp_in[0, :w1_rows]), slot(w_out[0])]))
    (o_pre, o_hg, states), (s_bottom, s_top, s_out) = _hgrn_forward(
        zhg, hg_lower_bounds, hg_norm_w,
        _merge(_gather_over_ici([slot(w_mlp_in[0, w1_rows:])]), _gather_over_d2d([s_top, s_out])))
    (o_mla, lse), (s_w2, s_bottom) = _attention_forward(
        q, k, v_t, _merge(_gather_over_ici([slot(w_mlp_out[0])]), _gather_over_d2d([s_bottom])))
    w_out_full = whole(s_out).reshape(D_MODEL, D_MODEL)
    (cat, mix, xhat1, rstd1), (s_w2,) = _out_project(o_hg, o_mla, xs, g_a, w_out_full, _gather_over_d2d([s_w2]))
    g_w1_top, g_w1_bottom, g_w2 = whole(s_top), whole(s_bottom), whole(s_w2)
    vecs = jnp.concatenate([ln1_g, ln1_b, sc_m, sh_m, g_m, g_a, ln2_g, ln2_b], axis=0)
    act, dhp, um, dh, dmix, d_cat, dr1, mlp_sums, delta = _mlp_and_back(
        xhat1, rstd1, mix, target, o_mla, vecs, g_w1_top, g_w1_bottom, g_w2, w_out_full)

    gw_1 = halved(_weight_grad(um, dhp, "grad_w_mlp_in", N_CHIPS, D_FF // N_CHIPS, token_tile=4096))
    gw_2, (landed_1,) = _weight_grad(act, dh, "grad_w_mlp_out", N_CHIPS, D_MODEL, a_blocked=True, b_blocked=False,
                                     token_tile=4096, exchange=_pair_exchange([gw_1]))
    gw_out = _weight_grad(cat, dmix, "grad_w_out", 1, D_MODEL, token_tile=2048)
    later = [halved(gw_2), halved(gw_out.reshape(N_CHIPS, D_MODEL // N_CHIPS, D_MODEL))]
    own_1, travels_1 = _add_pair(core_arr, chip_arr, gw_1, landed_1)
    (dq, dk, dv), (landed_1, *landed) = _attention_backward(
        q, k, k_t, v, d_cat, lse, delta, _merge(_chip_exchange([travels_1]), _pair_exchange(later)))
    mine_1 = _add_chips(own_1, landed_1)
    chip_sums = [_add_pair(core_arr, chip_arr, g, l) for g, l in zip(later, landed)]
    (d_hq, d_hf, d_hi, d_hg, hg_sums), (theirs_1, *landed) = _hgrn_backward(
        zhg, hg_lower_bounds, hg_norm_w, o_pre, d_cat, states,
        _merge(_pair_send([mine_1]), _chip_exchange([b for _, b in chip_sums])))
    later_mine = [_add_chips(own, l) for (own, _), l in zip(chip_sums, landed)]
    mlp_mine = [mine_1] + later_mine
    (dz, grad_x, in_sums, gw_q, gw_kv), _ = _in_project_backward(
        dq, dk, dv, cq, ckv, pos, invf, mla_q_norm_w, mla_kv_norm_w, w_q_full, g_kv,
        d_hq, d_hf, d_hi, d_hg, w_in_full, dr1, xs, sc_a)

    zeros = lambda n: jnp.zeros((1, n), F32)
    small = jnp.concatenate([
        in_sums[1:2], in_sums[0:1], mlp_sums[S_DGA:S_DGA + 1],
        mlp_sums[S_DSHM:S_DSHM + 1], mlp_sums[S_DSCM:S_DSCM + 1], mlp_sums[S_DGM:S_DGM + 1],
        mlp_sums[S_DLN1G:S_DLN1G + 1], mlp_sums[S_DLN1B:S_DLN1B + 1],
        mlp_sums[S_DLN2G:S_DLN2G + 1], mlp_sums[S_DLN2B:S_DLN2B + 1],
        jnp.concatenate([hg_sums[0:1], hg_sums[1:2]], axis=1),
        jnp.concatenate([in_sums[2:3, :Q_RANK], in_sums[3:4, :KV_RANK], zeros(D_MODEL - Q_RANK - KV_RANK)], axis=1),
        mlp_sums[S_LOSS:S_LOSS + 1],
        jnp.zeros((SMALL_ROWS - 13, D_MODEL), F32)], axis=0)

    gw_in, (*later_theirs, small_all) = _weight_grad(
        dz, u_a, "grad_w_in", 3, D_MODEL, a_blocked=True, b_blocked=False, token_tile=4096,
        exchange=_merge(_pair_send(later_mine), _gather_all(slot8(small))))
    mlp_theirs = [theirs_1] + list(later_theirs)
    gw_in = gw_in.reshape(IN_COLS_PAD, D_MODEL)
    gw_q = gw_q[:, :, :HEAD_DIM + ROPE_DIM]
    flat = lambda g: g.reshape(g.shape[0] * g.shape[1], g.shape[2])
    mixer_mine, mixer_theirs = _reduce_in_vmem(
        [gw_in, flat(gw_q), flat(gw_kv)], [IN_COLS // N_CHIPS // 2, Q_RANK // 2, KV_RANK // 2], "reduce_mixer_grads")
    reduced = ("w_in", "w_q_up", "w_kv_up", "w_mlp_in", "w_mlp_out", "w_out")
    halves_mine = dict(zip(reduced, list(mixer_mine) + mlp_mine))
    halves_theirs = dict(zip(reduced, list(mixer_theirs) + list(mlp_theirs)))

    small_names = ("b_ada", "hg_lower_bounds", "hg_norm_w", "mla_q_norm_w", "mla_kv_norm_w",
                   "ln1_g", "ln1_b", "ln2_g", "ln2_b")
    loss_row, small_out = _small_params_step(small_all, {
        "b_ada": (b_ada, m_b_ada, v_b_ada),
        "hg_lower_bounds": (hg_lower_bounds, m_hg_lower_bounds, v_hg_lower_bounds),
        "hg_norm_w": (hg_norm_w, m_hg_norm_w, v_hg_norm_w),
        "mla_q_norm_w": (mla_q_norm_w, m_mla_q_norm_w, v_mla_q_norm_w),
        "mla_kv_norm_w": (mla_kv_norm_w, m_mla_kv_norm_w, v_mla_kv_norm_w),
        "ln1_g": (ln1_g, m_ln1_g, v_ln1_g), "ln1_b": (ln1_b, m_ln1_b, v_ln1_b),
        "ln2_g": (ln2_g, m_ln2_g, v_ln2_g), "ln2_b": (ln2_b, m_ln2_b, v_ln2_b)})
    loss = loss_row[0, 0]

    d_mod_all = small_all[:, 0:6, :].reshape(N_DEV, 6 * D_MODEL)
    d_mod_cols = lax.dynamic_slice(d_mod_all, (0, chip * ada_cols), (N_DEV, ada_cols))
    d_mod_cols = jnp.concatenate([d_mod_cols, jnp.zeros_like(d_mod_cols)], axis=0)
    g_w_ada = _weight_grad(cond16, d_mod_cols, "grad_w_ada", 1, ada_cols)[0]

    names = ["w_ada", "b_ada", "w_in", "hg_lower_bounds", "hg_norm_w", "mla_q_norm_w", "w_q_up", "mla_kv_norm_w",
             "w_kv_up", "w_out", "ln1_g", "ln1_b", "w_mlp_in", "w_mlp_out", "ln2_g", "ln2_b"]
    weights = [w_ada, b_ada, w_in, hg_lower_bounds, hg_norm_w, mla_q_norm_w, w_q_up, mla_kv_norm_w,
               w_kv_up, w_out, ln1_g, ln1_b, w_mlp_in, w_mlp_out, ln2_g, ln2_b]
    moms = [m_w_ada, m_b_ada, m_w_in, m_hg_lower_bounds, m_hg_norm_w, m_mla_q_norm_w, m_w_q_up, m_mla_kv_norm_w,
            m_w_kv_up, m_w_out, m_ln1_g, m_ln1_b, m_w_mlp_in, m_w_mlp_out, m_ln2_g, m_ln2_b]
    vels = [v_w_ada, v_b_ada, v_w_in, v_hg_lower_bounds, v_hg_norm_w, v_mla_q_norm_w, v_w_q_up, v_mla_kv_norm_w,
            v_w_kv_up, v_w_out, v_ln1_g, v_ln1_b, v_w_mlp_in, v_w_mlp_out, v_ln2_g, v_ln2_b]
    out_g, out_d, out_m, out_v = [], [], [], []
    for name, w, m, vv in zip(names, weights, moms, vels):
        if name in small_names:
            g, d, nm, nv = small_out[name]
            back = lambda a: a
        elif name == "w_in":
            to2d, back = (lambda a: jnp.transpose(a[0])), (lambda a: jnp.transpose(a)[None])
        else:
            to2d, back = (lambda a, s=w.shape[1:]: a.reshape(s)), (lambda a, s=w.shape: a.reshape(s))
        if name == "w_ada":
            d, nm, nv = _adamw(to2d(w), g_w_ada, to2d(m), to2d(vv), "adamw_" + name)
            g = g_w_ada
        elif name not in small_names:
            g, d, nm, nv = _adamw_halves(core_arr, to2d(w), halves_mine[name], halves_theirs[name], to2d(m), to2d(vv),
                                         "adamw_" + name)
        out_g.append(back(g))
        out_d.append(back(d))
        out_m.append(back(nm))
        out_v.append(back(nv))
    return (loss, grad_x[None], *out_g, *out_d, *out_m, *out_v)
```

```python
import functools

import jax
import jax.numpy as jnp
from jax import lax
from jax.experimental import pallas as pl
from jax.experimental.pallas import tpu as pltpu

F32 = jnp.float32
BF16 = jnp.bfloat16
MESH_IDS = pl.DeviceIdType.MESH

D_MODEL = 1024
N_HEADS = 4
HEAD_DIM = 128
ROPE_DIM = 64
HG_CHUNK = 64
HG_COLS = 2048
Q_RANK = 256
KV_RANK = 256
IN_COLS = 2624
IN_COLS_PAD = 2688
QK_DIM = 256
QK_REAL = HEAD_DIM + ROPE_DIM
D_FF = 4096
N_CHIPS = 4
N_DEV = 8
ROPE_THETA = 10000.0
RMS_EPS = 1e-6
LN_EPS = 1e-5
DN_ALPHA = 2.0 ** 0.25
ATT_SCALE = (HEAD_DIM + ROPE_DIM) ** -0.5
NEG_BIG = -1e30
ADAM_LR = 0.001
ADAM_B1 = 0.9
ADAM_B2 = 0.999
ADAM_EPS = 1e-08
ADAM_WD = 0.01
ADAM_STEP = 10
SMALL_ROWS = 16
MIB = 1024 * 1024


def _dot(a, b):
    return jnp.dot(a, b, preferred_element_type=F32)


def _dot_nt(a, b):
    return lax.dot_general(a, b, (((1,), (1,)), ((), ())), preferred_element_type=F32)


def _dot_tn(a, b):
    return lax.dot_general(a, b, (((0,), (0,)), ((), ())), preferred_element_type=F32)


def _params(vmem_mib, semantics=None):
    return pltpu.CompilerParams(vmem_limit_bytes=vmem_mib * MIB, dimension_semantics=semantics)


def _sigmoid(v):
    return 1.0 / (1.0 + jnp.exp(-v))


def _colsum(v):
    return jnp.sum(v, axis=0, keepdims=True)


def _rowmean(v):
    return jnp.mean(v, axis=-1, keepdims=True)


def _rope_tables(pos, invf):
    ang = pos * invf
    lane = lax.broadcasted_iota(jnp.int32, ang.shape, 1)
    cos_t = jnp.where(lane < ROPE_DIM, jnp.cos(ang), 0.0)
    sin = jnp.sin(ang)
    sin_t = jnp.where(lane < ROPE_DIM // 2, -sin, jnp.where(lane < ROPE_DIM, sin, 0.0))
    return cos_t, sin_t


def _swap_halves(t):
    lane = lax.broadcasted_iota(jnp.int32, t.shape, 1)
    return jnp.where(lane < ROPE_DIM // 2, pltpu.roll(t, 128 - ROPE_DIM // 2, 1), pltpu.roll(t, ROPE_DIM // 2, 1))


def _rope(t, cos_t, sin_t):
    return t * cos_t + _swap_halves(t) * sin_t


def _unrope(g, cos_t, sin_t):
    return g * cos_t - _swap_halves(g) * sin_t


def _mesh_pos():
    return lax.axis_index("x"), lax.axis_index("y"), lax.axis_index("c")


def _other_chips(x, y):
    out = []
    for dx, dy in ((1, 0), (0, 1), (1, 1)):
        px = 1 - x if dx else x
        py = 1 - y if dy else y
        out.append(((px, py), 2 * px + py))
    return out


class _Exchange:
    def __init__(self, inputs, out_shapes, aliases, sems, start, finish):
        self.inputs, self.out_shapes, self.aliases, self.sems = list(inputs), list(out_shapes), dict(aliases), list(sems)
        self.start, self.finish = start, finish


def _from_copies(inputs, out_shapes, aliases, sems, copies):
    def start(ins, outs, sem_refs):
        for send, _ in copies(ins, outs, sem_refs):
            send.start()

    def finish(ins, outs, sem_refs):
        for send, recv in copies(ins, outs, sem_refs):
            recv.wait_recv()
            send.wait_send()

    return _Exchange(inputs, out_shapes, aliases, sems, start, finish)


HBM_MIN_BYTES = 256 * 1024


def _in_hbm(a):
    if a.size * a.dtype.itemsize < HBM_MIN_BYTES:
        return a
    return pltpu.with_memory_space_constraint(a, pltpu.HBM)


def _out_hbm(s):
    if s.size * s.dtype.itemsize < HBM_MIN_BYTES:
        return s
    return pltpu.HBM(s.shape, s.dtype)


def _pcall(body, *, operands, out_shape, **kwargs):
    single = not isinstance(out_shape, (list, tuple))
    shapes = [_out_hbm(s) for s in ([out_shape] if single else out_shape)]
    return pl.pallas_call(body, out_shape=shapes[0] if single else shapes, **kwargs)(*[_in_hbm(a) for a in operands])


def _run_exchange(exchange, name):
    n_in, n_out = len(exchange.inputs), len(exchange.out_shapes)

    def body(*refs):
        ins, outs, sem_refs = refs[:n_in], refs[n_in:n_in + n_out], refs[n_in + n_out:]
        exchange.start(ins, outs, sem_refs)
        exchange.finish(ins, outs, sem_refs)

    any_spec = pl.BlockSpec(memory_space=pl.ANY)
    return pl.pallas_call(
        body, name=name, out_shape=[_out_hbm(s) for s in exchange.out_shapes],
        in_specs=[any_spec] * n_in, out_specs=[any_spec] * n_out,
        scratch_shapes=exchange.sems, input_output_aliases=exchange.aliases,
    )(*[_in_hbm(a) for a in exchange.inputs])


def _pallas(body, *, name, operands, in_specs, out_shape, out_specs, params, scratch_shapes=(), grid=(), prefetch=(),
            exchange=None, first=None, last=None):
    n_pre, n_in, n_out, n_scr = len(prefetch), len(in_specs), len(out_specs), len(scratch_shapes)
    ex_in = exchange.inputs if exchange else []
    ex_out = exchange.out_shapes if exchange else []
    ex_sems = exchange.sems if exchange else []

    def full_body(*refs):
        pre, rest = refs[:n_pre], refs[n_pre:]
        ins, rest = rest[:n_in], rest[n_in:]
        xin, rest = rest[:len(ex_in)], rest[len(ex_in):]
        outs, rest = rest[:n_out], rest[n_out:]
        xout, rest = rest[:len(ex_out)], rest[len(ex_out):]
        scr, sem_refs = rest[:n_scr], rest[n_scr:]
        if exchange:
            @pl.when(first(*pre))
            def _():
                exchange.start(xin, xout, sem_refs)

        body(*pre, *ins, *outs, *scr)
        if exchange:
            @pl.when(last(*pre))
            def _():
                exchange.finish(xin, xout, sem_refs)

    any_spec = pl.BlockSpec(memory_space=pl.ANY)
    aliases = {n_pre + n_in + i: n_out + o for i, o in exchange.aliases.items()} if exchange else {}
    operands = [_in_hbm(a) for a in operands]
    results = pl.pallas_call(
        full_body, name=name, out_shape=[_out_hbm(s) for s in list(out_shape) + ex_out],
        grid_spec=pltpu.PrefetchScalarGridSpec(
            num_scalar_prefetch=n_pre, grid=grid, in_specs=list(in_specs) + [any_spec] * len(ex_in),
            out_specs=list(out_specs) + [any_spec] * len(ex_out), scratch_shapes=list(scratch_shapes) + ex_sems),
        input_output_aliases=aliases, compiler_params=params,
    )(*prefetch, *operands, *[_in_hbm(a) for a in ex_in])
    return results[:n_out], results[n_out:]


def _remote(src, dst, sems, idx, to):
    send_sems, recv_sems = sems
    return pltpu.make_async_remote_copy(src_ref=src, dst_ref=dst, send_sem=send_sems.at[idx], recv_sem=recv_sems.at[idx],
                                        device_id=to, device_id_type=MESH_IDS)


def _sem_pairs(*shape):
    return [pltpu.SemaphoreType.DMA(shape), pltpu.SemaphoreType.DMA(shape)]


def _same_shapes(arrays):
    return [jax.ShapeDtypeStruct(a.shape, a.dtype) for a in arrays]


def _gather_over_ici(slots):
    n = len(slots)

    def copies(ins, outs, sems):
        x, y, c = _mesh_pos()
        k = 2 * x + y
        out = []
        for j, (chip, kj) in enumerate(_other_chips(x, y)):
            for i in range(n):
                to = (*chip, c)
                out.append((_remote(ins[i].at[k, c], outs[i].at[k, c], sems, (j, i), to),
                            _remote(ins[i].at[k, c], outs[i].at[kj, c], sems, (j, i), to)))
        return out

    return _from_copies(slots, _same_shapes(slots), {i: i for i in range(n)}, _sem_pairs(3, n), copies)


def _gather_over_ici_in_two_steps(slots):
    n = len(slots)

    def places(x, y):
        return 2 * x + y, 2 * (1 - x) + y, 2 * x + (1 - y), 2 * (1 - x) + (1 - y)

    def rows(i):
        h = slots[i].shape[2]
        cut = h // 2 // 16 * 16
        return pl.ds(0, cut), pl.ds(cut, h - cut)

    def start(ins, outs, sems):
        x, y, c = _mesh_pos()
        k = places(x, y)[0]
        for i in range(n):
            _remote(ins[i].at[k, c], outs[i].at[k, c], sems[0:2], (0, i), (1 - x, y, c)).start()
            _remote(ins[i].at[k, c], outs[i].at[k, c], sems[0:2], (1, i), (x, 1 - y, c)).start()

    def finish(ins, outs, sems):
        x, y, c = _mesh_pos()
        k, kx, ky, kd = places(x, y)
        to_x, to_y = (1 - x, y, c), (x, 1 - y, c)
        passed = []
        for i in range(n):
            upper, lower = rows(i)
            _remote(ins[i].at[k, c], outs[i].at[kx, c], sems[0:2], (0, i), to_x).wait_recv()
            passed.append(_remote(outs[i].at[kx, c, upper], outs[i].at[kx, c, upper], sems[2:4], (0, i), to_y))
            passed[-1].start()
            _remote(ins[i].at[k, c], outs[i].at[ky, c], sems[0:2], (1, i), to_y).wait_recv()
            passed.append(_remote(outs[i].at[ky, c, lower], outs[i].at[ky, c, lower], sems[2:4], (1, i), to_x))
            passed[-1].start()
        for i in range(n):
            upper, lower = rows(i)
            _remote(outs[i].at[kd, c, upper], outs[i].at[kd, c, upper], sems[2:4], (0, i), to_y).wait_recv()
            _remote(outs[i].at[kd, c, lower], outs[i].at[kd, c, lower], sems[2:4], (1, i), to_x).wait_recv()
        for i in range(n):
            _remote(ins[i].at[k, c], outs[i].at[k, c], sems[0:2], (0, i), to_x).wait_send()
            _remote(ins[i].at[k, c], outs[i].at[k, c], sems[0:2], (1, i), to_y).wait_send()
        for cp in passed:
            cp.wait_send()

    return _Exchange(slots, _same_shapes(slots), {i: i for i in range(n)}, _sem_pairs(2, n) + _sem_pairs(2, n),
                     start, finish)


def _gather_over_d2d(slots):
    n = len(slots)

    def copies(ins, outs, sems):
        x, y, c = _mesh_pos()
        sibling = (x, y, 1 - c)
        out = []
        for j, (_, kj) in enumerate(_other_chips(x, y)):
            for i in range(n):
                out.append((_remote(ins[i].at[kj, c], outs[i].at[kj, c], sems, (j, i), sibling),
                            _remote(ins[i].at[kj, c], outs[i].at[kj, 1 - c], sems, (j, i), sibling)))
        return out

    return _from_copies(slots, _same_shapes(slots), {i: i for i in range(n)}, _sem_pairs(3, n), copies)


def _gather_all(slots8):
    def copies(ins, outs, sems):
        x, y, c = _mesh_pos()
        me = 4 * x + 2 * y + c
        out = []
        for r in range(1, N_DEV):
            px = 1 - x if r & 4 else x
            py = 1 - y if r & 2 else y
            pc = 1 - c if r & 1 else c
            to = (px, py, pc)
            out.append((_remote(ins[0].at[me], outs[0].at[me], sems, r - 1, to),
                        _remote(ins[0].at[me], outs[0].at[4 * px + 2 * py + pc], sems, r - 1, to)))
        return out

    return _from_copies([slots8], _same_shapes([slots8]), {0: 0}, _sem_pairs(N_DEV - 1), copies)


def _merge(first, second):
    n_in, n_out, n_sem = len(first.inputs), len(first.out_shapes), len(first.sems)

    def start(ins, outs, sems):
        first.start(ins[:n_in], outs[:n_out], sems[:n_sem])
        second.start(ins[n_in:], outs[n_out:], sems[n_sem:])

    def finish(ins, outs, sems):
        first.finish(ins[:n_in], outs[:n_out], sems[:n_sem])
        second.finish(ins[n_in:], outs[n_out:], sems[n_sem:])

    aliases = dict(first.aliases)
    aliases.update({n_in + i: n_out + o for i, o in second.aliases.items()})
    return _Exchange(first.inputs + second.inputs, first.out_shapes + second.out_shapes, aliases,
                     first.sems + second.sems, start, finish)


def _pair_exchange(grads):
    n = len(grads)

    def copies(ins, outs, sems):
        x, y, c = _mesh_pos()
        cps = [_remote(ins[i].at[:, 1 - c], outs[i], sems, i, (x, y, 1 - c)) for i in range(n)]
        return [(cp, cp) for cp in cps]

    shapes = [jax.ShapeDtypeStruct((N_CHIPS,) + g.shape[2:], g.dtype) for g in grads]
    return _from_copies(grads, shapes, {}, _sem_pairs(n), copies)


def _chip_exchange(partials):
    n = len(partials)

    def copies(ins, outs, sems):
        x, y, c = _mesh_pos()
        cps = [_remote(ins[i].at[kj], outs[i].at[j], sems, (j, i), (*chip, c))
               for j, (chip, kj) in enumerate(_other_chips(x, y)) for i in range(n)]
        return [(cp, cp) for cp in cps]

    shapes = [jax.ShapeDtypeStruct((3,) + p.shape[1:], p.dtype) for p in partials]
    return _from_copies(partials, shapes, {}, _sem_pairs(3, n), copies)


def _pair_send(halves):
    n = len(halves)

    def copies(ins, outs, sems):
        x, y, c = _mesh_pos()
        cps = [_remote(ins[i], outs[i], sems, i, (x, y, 1 - c)) for i in range(n)]
        return [(cp, cp) for cp in cps]

    return _from_copies(halves, _same_shapes(halves), {}, _sem_pairs(n), copies)


def _reduce_in_vmem(grads, half_rows, name):
    n = len(grads)

    def body(*refs):
        g, mine, theirs = refs[:n], refs[n:2 * n], refs[2 * n:3 * n]
        landed_pair, partial, landed_chips = refs[3 * n:4 * n], refs[4 * n:5 * n], refs[5 * n:6 * n]
        sems = refs[6 * n:]
        x, y, c = _mesh_pos()
        k = 2 * x + y
        sibling = (x, y, 1 - c)

        def half(i, chip_idx, which):
            return pl.ds(pl.multiple_of((2 * chip_idx + which) * half_rows[i], 8), half_rows[i])

        def run(copies):
            for cp in copies:
                cp.start()
            for cp in copies:
                cp.wait_recv()
                cp.wait_send()

        run([_remote(g[i].at[half(i, kk, 1 - c)], landed_pair[i].at[kk], sems[0:2], (kk, i), sibling)
             for kk in range(N_CHIPS) for i in range(n)])
        for i in range(n):
            for kk in range(N_CHIPS):
                partial[i][kk] = (g[i][half(i, kk, c), :] + landed_pair[i][kk]).astype(BF16)
        run([_remote(partial[i].at[kj], landed_chips[i].at[j], sems[2:4], (j, i), (*chip, c))
             for j, (chip, kj) in enumerate(_other_chips(x, y)) for i in range(n)])
        for i in range(n):
            own = g[i][half(i, k, c), :] + landed_pair[i][k]
            mine[i][...] = ((own + landed_chips[i][0].astype(F32)) + landed_chips[i][1].astype(F32)) \
                + landed_chips[i][2].astype(F32)
        run([_remote(mine[i], theirs[i], sems[4:6], i, sibling) for i in range(n)])

    shapes = [(h, gr.shape[1]) for gr, h in zip(grads, half_rows)]
    halves = [jax.ShapeDtypeStruct(s, F32) for s in shapes]
    vmem = pl.BlockSpec(memory_space=pltpu.VMEM)
    scratch = ([pltpu.VMEM((N_CHIPS,) + s, F32) for s in shapes]
               + [pltpu.VMEM((N_CHIPS,) + s, BF16) for s in shapes]
               + [pltpu.VMEM((3,) + s, BF16) for s in shapes]
               + _sem_pairs(N_CHIPS, n) + _sem_pairs(3, n) + _sem_pairs(n))
    out = pl.pallas_call(
        body, name=name, out_shape=halves + halves, in_specs=[vmem] * n, out_specs=[vmem] * (2 * n),
        scratch_shapes=scratch, compiler_params=_params(48),
    )(*grads)
    return out[:n], out[n:]


def _row_tile(rows):
    for t in (256, 128, 64):
        if rows % t == 0:
            return t
    return rows


def _add_pair(core, chip, grad, landed):
    _, h, cols = landed.shape
    tr = _row_tile(h)

    def body(core_ref, chip_ref, g_ref, l_ref, own_ref, ob_ref):
        s = g_ref[...] + l_ref[...]
        ob_ref[...] = s.astype(BF16)

        @pl.when(pl.program_id(1) == chip_ref[0])
        def _():
            own_ref[...] = s

    return _pcall(
        body, name="grad_add_pair",
        out_shape=[jax.ShapeDtypeStruct((h, cols), F32), jax.ShapeDtypeStruct(landed.shape, BF16)],
        grid_spec=pltpu.PrefetchScalarGridSpec(
            num_scalar_prefetch=2, grid=(h // tr, N_CHIPS),
            in_specs=[pl.BlockSpec((None, None, tr, cols), lambda t, k, core_ref, chip_ref: (k, core_ref[0], t, 0)),
                      pl.BlockSpec((None, tr, cols), lambda t, k, core_ref, chip_ref: (k, t, 0))],
            out_specs=[pl.BlockSpec((tr, cols), lambda t, k, core_ref, chip_ref: (t, 0)),
                       pl.BlockSpec((None, tr, cols), lambda t, k, core_ref, chip_ref: (k, t, 0))]),
        compiler_params=_params(32, ("arbitrary", "arbitrary")),
        operands=(core, chip, grad, landed))


def _add_chips(own, landed):
    h, cols = own.shape
    tr = _row_tile(h)

    def body(p_ref, l_ref, o_ref):
        o_ref[...] = ((p_ref[...] + l_ref[0].astype(F32)) + l_ref[1].astype(F32)) + l_ref[2].astype(F32)

    return _pcall(
        body, name="grad_add_chips", grid=(h // tr,),
        out_shape=jax.ShapeDtypeStruct((h, cols), F32),
        in_specs=[pl.BlockSpec((tr, cols), lambda t: (t, 0)), pl.BlockSpec((3, tr, cols), lambda t: (0, t, 0))],
        out_specs=pl.BlockSpec((tr, cols), lambda t: (t, 0)),
        compiler_params=_params(32, ("arbitrary",)),
        operands=(own, landed))


def _ada_project(c_all, w_ada, b_shard):
    n = w_ada.shape[1]
    tn = 512

    def body(c_ref, w_ref, b_ref, mod_ref, cond_ref):
        cv = c_ref[...]
        cond = cv * _sigmoid(cv)
        mod_ref[...] = _dot(cond.astype(BF16), w_ref[...].astype(BF16)) + b_ref[...]
        cond_ref[0:N_DEV, :] = cond
        cond_ref[N_DEV:2 * N_DEV, :] = jnp.zeros_like(cond)

    return _pcall(
        body, name="ada_project", grid=(n // tn,),
        out_shape=[jax.ShapeDtypeStruct((N_DEV, n), F32), jax.ShapeDtypeStruct((2 * N_DEV, D_MODEL), F32)],
        in_specs=[pl.BlockSpec((N_DEV, D_MODEL), lambda j: (0, 0)), pl.BlockSpec((D_MODEL, tn), lambda j: (0, j)),
                  pl.BlockSpec((1, tn), lambda j: (0, j))],
        out_specs=[pl.BlockSpec((N_DEV, tn), lambda j: (0, j)), pl.BlockSpec((2 * N_DEV, D_MODEL), lambda j: (0, 0))],
        compiler_params=_params(32, ("arbitrary",)),
        operands=(c_all, w_ada, b_shard))


def _in_project(x, pos, sc_a, sh_a, w_in, q_norm_w, kv_norm_w, w_q, w_kv, invf, exchange=None):
    t_len = x.shape[0]
    tm = min(512, t_len)

    def body(x_ref, pos_ref, sc_ref, sh_ref, win_ref, qn_ref, kvn_ref, wq_ref, wkv_ref, invf_ref,
             u_ref, zhg_ref, cq_ref, ckv_ref, q_ref, k_ref, kt_ref, v_ref, vt_ref):
        u = (x_ref[...] * (1.0 + sc_ref[...]) + sh_ref[...]).astype(BF16)
        u_ref[...] = u
        z = _dot_nt(u, win_ref[...])
        zhg_ref[...] = z[:, :HG_COLS]
        cq = z[:, HG_COLS:HG_COLS + Q_RANK]
        ckv = z[:, HG_COLS + Q_RANK:HG_COLS + Q_RANK + KV_RANK]
        cq_ref[...] = cq
        ckv_ref[...] = ckv
        cos_t, sin_t = _rope_tables(pos_ref[...], invf_ref[...])
        k_pe = _rope(z[:, HG_COLS + Q_RANK + KV_RANK:], cos_t, sin_t)
        k_pe_t = jnp.transpose(k_pe).astype(BF16)
        cqn = (cq * lax.rsqrt(_rowmean(cq * cq) + RMS_EPS) * qn_ref[...]).astype(BF16)
        ckvn = (ckv * lax.rsqrt(_rowmean(ckv * ckv) + RMS_EPS) * kvn_ref[...]).astype(BF16)
        q_up = [_dot(cqn, wq_ref[h]) for h in range(N_HEADS)]
        kv_up = [_dot(ckvn, wkv_ref[h]) for h in range(N_HEADS)]
        for h in range(N_HEADS):
            qh, kvh = q_up[h], kv_up[h]
            q_ref[h, :, 0:HEAD_DIM] = qh[:, :HEAD_DIM].astype(BF16)
            q_ref[h, :, HEAD_DIM:QK_DIM] = _rope(qh[:, HEAD_DIM:], cos_t, sin_t).astype(BF16)
            k_ref[h, :, 0:HEAD_DIM] = kvh[:, :HEAD_DIM].astype(BF16)
            k_ref[h, :, HEAD_DIM:QK_DIM] = k_pe.astype(BF16)
            kt_ref[h, 0:HEAD_DIM, :] = jnp.transpose(kvh[:, :HEAD_DIM]).astype(BF16)
            kt_ref[h, HEAD_DIM:QK_DIM, :] = k_pe_t
            v_ref[h] = kvh[:, HEAD_DIM:].astype(BF16)
            vt_ref[h] = jnp.transpose(kvh[:, HEAD_DIM:]).astype(BF16)

    row = lambda i: (i, 0)
    fixed2 = lambda i: (0, 0)
    fixed3 = lambda i: (0, 0, 0)
    heads = lambda i: (0, i, 0)
    n_tiles = t_len // tm
    return _pallas(
        body, name="in_project", grid=(n_tiles,),
        operands=(x, pos, sc_a, sh_a, w_in, q_norm_w, kv_norm_w, w_q, w_kv, invf),
        out_shape=[jax.ShapeDtypeStruct((t_len, D_MODEL), BF16), jax.ShapeDtypeStruct((t_len, HG_COLS), F32),
                   jax.ShapeDtypeStruct((t_len, Q_RANK), F32), jax.ShapeDtypeStruct((t_len, KV_RANK), F32),
                   jax.ShapeDtypeStruct((N_HEADS, t_len, QK_DIM), BF16),
                   jax.ShapeDtypeStruct((N_HEADS, t_len, QK_DIM), BF16),
                   jax.ShapeDtypeStruct((N_HEADS, QK_DIM, t_len), BF16),
                   jax.ShapeDtypeStruct((N_HEADS, t_len, HEAD_DIM), BF16),
                   jax.ShapeDtypeStruct((N_HEADS, HEAD_DIM, t_len), BF16)],
        in_specs=[pl.BlockSpec((tm, D_MODEL), row), pl.BlockSpec((tm, 1), row),
                  pl.BlockSpec((1, D_MODEL), fixed2), pl.BlockSpec((1, D_MODEL), fixed2),
                  pl.BlockSpec((IN_COLS_PAD, D_MODEL), fixed2),
                  pl.BlockSpec((1, Q_RANK), fixed2), pl.BlockSpec((1, KV_RANK), fixed2),
                  pl.BlockSpec((N_HEADS, Q_RANK, QK_DIM), fixed3), pl.BlockSpec((N_HEADS, KV_RANK, 2 * HEAD_DIM), fixed3),
                  pl.BlockSpec((1, 128), fixed2)],
        out_specs=[pl.BlockSpec((tm, D_MODEL), row), pl.BlockSpec((tm, HG_COLS), row),
                   pl.BlockSpec((tm, Q_RANK), row), pl.BlockSpec((tm, KV_RANK), row),
                   pl.BlockSpec((N_HEADS, tm, QK_DIM), heads), pl.BlockSpec((N_HEADS, tm, QK_DIM), heads),
                   pl.BlockSpec((N_HEADS, QK_DIM, tm), lambda i: (0, 0, i)),
                   pl.BlockSpec((N_HEADS, tm, HEAD_DIM), heads),
                   pl.BlockSpec((N_HEADS, HEAD_DIM, tm), lambda i: (0, 0, i))],
        params=_params(48, ("arbitrary",)), exchange=exchange,
        first=lambda: pl.program_id(0) == 0, last=lambda: pl.program_id(0) == n_tiles - 1)


def _lower_bound(lb_raw):
    m = jnp.max(lb_raw, axis=0, keepdims=True)
    e = jnp.exp(lb_raw - m)
    return e[0:1] / jnp.sum(e, axis=0, keepdims=True)


def _tri(inclusive_lower):
    r = lax.broadcasted_iota(jnp.int32, (HG_CHUNK, HG_CHUNK), 0)
    c = lax.broadcasted_iota(jnp.int32, (HG_CHUNK, HG_CHUNK), 1)
    return (c <= r) if inclusive_lower else (c >= r)


def _chunk_rows(n):
    return slice(n * HG_CHUNK, (n + 1) * HG_CHUNK)


def _chunk_prefix_sums(v, inclusive_lower):
    tri = _tri(inclusive_lower).astype(BF16)
    hi = v.astype(BF16)
    rest = v - hi.astype(F32)
    mid = rest.astype(BF16)
    lo = (rest - mid.astype(F32)).astype(BF16)
    pieces = jnp.concatenate([hi, mid, lo], axis=1)
    out = []
    for n in range(v.shape[0] // HG_CHUNK):
        s = _dot(tri, pieces[_chunk_rows(n)])
        out.append((s[:, 0:HEAD_DIM] + s[:, HEAD_DIM:2 * HEAD_DIM]) + s[:, 2 * HEAD_DIM:])
    return jnp.concatenate(out, axis=0)


def _per_chunk(v, row):
    n = v.shape[0] // HG_CHUNK
    v3 = v.reshape(n, HG_CHUNK, HEAD_DIM)
    return jnp.broadcast_to(v3[:, row:row + 1, :], v3.shape).reshape(v.shape)


def _hg_block(q, f_logit, lb):
    sg = _sigmoid(f_logit)
    forget = lb + (1.0 - lb) * sg
    kk = 1.0 - forget
    b = _chunk_prefix_sums(jnp.log(forget), True)
    b_ref = _per_chunk(b, HG_CHUNK // 2 - 1)
    b_last = _per_chunk(b, HG_CHUNK - 1)
    e_i = jnp.exp(b - b_ref)
    e_ri = jnp.exp(b_ref - b)
    e_b = jnp.exp(b)
    e_l = jnp.exp(b_last - b)
    return dict(sg=sg, forget=forget, e_i=e_i, e_ri=e_ri, e_b=e_b, e_l=e_l, dec=jnp.exp(b_last),
                qi=q * e_i, ki=kk * e_ri, qe=q * e_b, kl=kk * e_l)


HG_STEP_HEADS = 4


def _head_cols(hh):
    return slice(hh * HEAD_DIM, (hh + 1) * HEAD_DIM)


def _hgrn_forward(zhg, lb_raw, norm_w, exchange=None):
    t_len = zhg.shape[0]
    tb = min(512, t_len)
    n_chunks = tb // HG_CHUNK
    hs = HG_STEP_HEADS

    def body(q_ref, f_ref, v_ref, g_ref, lb_ref, w_ref, opre_ref, o_ref, st_ref, state):
        @pl.when(pl.program_id(1) == 0)
        def _():
            state[...] = jnp.zeros_like(state)

        causal = _tri(True)
        heads = range(hs)
        blk, v, qi, ki, qe, kl = {}, {}, {}, {}, {}, {}
        for hh in heads:
            cols = _head_cols(hh)
            blk[hh] = _hg_block(q_ref[:, cols], f_ref[:, cols], _lower_bound(lb_ref[:, cols]))
            v[hh] = v_ref[:, cols].astype(BF16)
            qi[hh], ki[hh], qe[hh], kl[hh] = (blk[hh][name].astype(BF16) for name in ("qi", "ki", "qe", "kl"))
        st = {hh: state[hh] for hh in heads}
        parts = {hh: [] for hh in heads}
        for n in range(n_chunks):
            r = _chunk_rows(n)
            for hh in heads:
                a = jnp.where(causal, _dot_nt(qi[hh][r], ki[hh][r]), 0.0).astype(BF16)
                st_ref[hh, n] = st[hh]
                parts[hh].append(_dot(a, v[hh][r]) + _dot_nt(qe[hh][r], st[hh].astype(BF16)))
                st[hh] = st[hh] * blk[hh]["dec"][n * HG_CHUNK:n * HG_CHUNK + 1] + _dot_tn(v[hh][r], kl[hh][r])
        for hh in heads:
            cols = _head_cols(hh)
            state[hh] = st[hh]
            o = jnp.concatenate(parts[hh], axis=0)
            opre_ref[:, cols] = o
            g = g_ref[:, cols]
            gated = o * lax.rsqrt(_rowmean(o * o) + RMS_EPS) * w_ref[:, cols] * (g * _sigmoid(g))
            o_ref[:, cols] = gated.astype(BF16)

    groups = N_HEADS // hs
    wide = hs * HEAD_DIM
    col = lambda off: (lambda h, t: (t, off + h))
    nb = t_len // tb
    return _pallas(
        body, name="hgrn_forward", grid=(groups, nb), operands=(zhg, zhg, zhg, zhg, lb_raw, norm_w),
        out_shape=[jax.ShapeDtypeStruct((t_len, N_HEADS * HEAD_DIM), F32),
                   jax.ShapeDtypeStruct((t_len, N_HEADS * HEAD_DIM), BF16),
                   jax.ShapeDtypeStruct((N_HEADS, t_len // HG_CHUNK, HEAD_DIM, HEAD_DIM), F32)],
        in_specs=[pl.BlockSpec((tb, wide), col(0)), pl.BlockSpec((tb, wide), col(groups)),
                  pl.BlockSpec((tb, wide), col(2 * groups)), pl.BlockSpec((tb, wide), col(3 * groups)),
                  pl.BlockSpec((2, wide), lambda h, t: (0, h)), pl.BlockSpec((1, wide), lambda h, t: (0, h))],
        out_specs=[pl.BlockSpec((tb, wide), col(0)), pl.BlockSpec((tb, wide), col(0)),
                   pl.BlockSpec((hs, n_chunks, HEAD_DIM, HEAD_DIM), lambda h, t: (h, t, 0, 0))],
        scratch_shapes=[pltpu.VMEM((hs, HEAD_DIM, HEAD_DIM), F32)],
        params=_params(40, ("arbitrary", "arbitrary")), exchange=exchange,
        first=lambda: (pl.program_id(0) == 0) & (pl.program_id(1) == 0),
        last=lambda: (pl.program_id(0) == groups - 1) & (pl.program_id(1) == nb - 1))


def _hgrn_backward(zhg, lb_raw, norm_w, o_pre, d_cat, states, exchange=None):
    t_len = zhg.shape[0]
    tb = min(512, t_len)
    n_chunks = tb // HG_CHUNK
    nb = t_len // tb
    hs = HG_STEP_HEADS

    def body(q_ref, f_ref, v_ref, g_ref, lb_ref, w_ref, opre_ref, do_ref, st_ref,
             dq_ref, df_ref, dv_ref, dg_ref, sums_ref, gstate):
        @pl.when(pl.program_id(1) == 0)
        def _():
            gstate[...] = jnp.zeros_like(gstate)
            sums_ref[...] = jnp.zeros_like(sums_ref)

        heads = range(hs)
        causal = _tri(True)
        row_id = lax.broadcasted_iota(jnp.int32, (HG_CHUNK, HEAD_DIM), 0)
        lb, d_o, blk, v, qi, ki, qe, kl = ({} for _ in range(8))
        for hh in heads:
            cols = _head_cols(hh)
            lb[hh] = _lower_bound(lb_ref[:, cols])
            w = w_ref[:, cols]
            o = opre_ref[:, cols]
            g = g_ref[:, cols]
            d_out = do_ref[:, cols]
            r = lax.rsqrt(_rowmean(o * o) + RMS_EPS)
            sg_g = _sigmoid(g)
            dg_ref[:, cols] = (d_out * (o * r * w) * (sg_g * (1.0 + g * (1.0 - sg_g)))).astype(BF16)
            d_on = d_out * (g * sg_g)
            sums_ref[1:2, cols] += _colsum(d_on * o * r)
            dy = d_on * w
            d_o[hh] = (r * dy - o * (r * r * r) * _rowmean(dy * o)).astype(BF16)
            blk[hh] = _hg_block(q_ref[:, cols], f_ref[:, cols], lb[hh])
            v[hh] = v_ref[:, cols].astype(BF16)
            qi[hh], ki[hh], qe[hh], kl[hh] = (blk[hh][name].astype(BF16) for name in ("qi", "ki", "qe", "kl"))
        gt = {hh: gstate[hh] for hh in heads}
        d_v, d_qi, d_ki, d_qe, d_kl, d_dec = ({hh: [None] * n_chunks for hh in heads} for _ in range(6))
        for n in reversed(range(n_chunks)):
            rows = _chunk_rows(n)
            for hh in heads:
                st = st_ref[hh, n]
                a = jnp.where(causal, _dot_nt(qi[hh][rows], ki[hh][rows]), 0.0).astype(BF16)
                d_a = jnp.where(causal, _dot_nt(d_o[hh][rows], v[hh][rows]), 0.0).astype(BF16)
                gt_b = gt[hh].astype(BF16)
                d_v[hh][n] = _dot_tn(a, d_o[hh][rows]) + _dot_nt(kl[hh][rows], gt_b)
                d_qi[hh][n] = _dot(d_a, ki[hh][rows])
                d_ki[hh][n] = _dot_tn(d_a, qi[hh][rows])
                d_qe[hh][n] = _dot(d_o[hh][rows], st.astype(BF16))
                d_kl[hh][n] = _dot(v[hh][rows], gt_b)
                d_dec[hh][n] = jnp.where(row_id == HG_CHUNK - 1, _colsum(gt[hh] * st), 0.0)
                gt[hh] = gt[hh] * blk[hh]["dec"][n * HG_CHUNK:n * HG_CHUNK + 1] + _dot_tn(d_o[hh][rows], qe[hh][rows])
        for hh in heads:
            cols = _head_cols(hh)
            b = blk[hh]
            gstate[hh] = gt[hh]
            dqi, dki, dqe, dkl, ddec = (jnp.concatenate(p[hh], axis=0) for p in (d_qi, d_ki, d_qe, d_kl, d_dec))
            dv_ref[:, cols] = jnp.concatenate(d_v[hh], axis=0).astype(BF16)
            dq_ref[:, cols] = (dqi * b["e_i"] + dqe * b["e_b"]).astype(BF16)
            d_k = dki * b["e_ri"] + dkl * b["e_l"]
            t_qi = dqi * b["qi"]
            t_ki = dki * b["ki"]
            t_kl = dkl * b["kl"]
            at_ref, at_last = [], []
            for n in range(n_chunks):
                rows = _chunk_rows(n)
                at_ref.append(jnp.where(row_id == HG_CHUNK // 2 - 1, _colsum(t_ki[rows] - t_qi[rows]), 0.0))
                at_last.append(jnp.where(row_id == HG_CHUNK - 1, _colsum(t_kl[rows]), 0.0))
            d_b = (t_qi - t_ki + dqe * b["qe"] - t_kl + jnp.concatenate(at_ref, axis=0)
                   + jnp.concatenate(at_last, axis=0) + ddec * b["dec"])
            d_forget = _chunk_prefix_sums(d_b, False) / b["forget"] - d_k
            sg = b["sg"]
            df_ref[:, cols] = (d_forget * (1.0 - lb[hh]) * sg * (1.0 - sg)).astype(BF16)
            sums_ref[0:1, cols] += _colsum(d_forget * (1.0 - sg))

    groups = N_HEADS // hs
    wide = hs * HEAD_DIM
    col = lambda off: (lambda h, t: (nb - 1 - t, off + h))
    return _pallas(
        body, name="hgrn_backward", grid=(groups, nb),
        operands=(zhg, zhg, zhg, zhg, lb_raw, norm_w, o_pre, d_cat, states),
        out_shape=[jax.ShapeDtypeStruct((t_len, N_HEADS * HEAD_DIM), BF16)] * 4
        + [jax.ShapeDtypeStruct((8, N_HEADS * HEAD_DIM), F32)],
        in_specs=[pl.BlockSpec((tb, wide), col(0)), pl.BlockSpec((tb, wide), col(groups)),
                  pl.BlockSpec((tb, wide), col(2 * groups)), pl.BlockSpec((tb, wide), col(3 * groups)),
                  pl.BlockSpec((2, wide), lambda h, t: (0, h)), pl.BlockSpec((1, wide), lambda h, t: (0, h)),
                  pl.BlockSpec((tb, wide), col(0)), pl.BlockSpec((tb, wide), col(0)),
                  pl.BlockSpec((hs, n_chunks, HEAD_DIM, HEAD_DIM), lambda h, t: (h, nb - 1 - t, 0, 0))],
        out_specs=[pl.BlockSpec((tb, wide), col(0))] * 4 + [pl.BlockSpec((8, wide), lambda h, t: (0, h))],
        scratch_shapes=[pltpu.VMEM((hs, HEAD_DIM, HEAD_DIM), F32)],
        params=_params(40, ("arbitrary", "arbitrary")), exchange=exchange,
        first=lambda: (pl.program_id(0) == 0) & (pl.program_id(1) == 0),
        last=lambda: (pl.program_id(0) == groups - 1) & (pl.program_id(1) == nb - 1))


ATT_LOG2 = ATT_SCALE * 1.4426950408889634


def _triangle_steps(nq, q_major):
    if q_major:
        pairs = [(i, j) for i in range(nq) for j in range(i + 1)]
    else:
        pairs = [(i, j) for j in range(nq) for i in range(j, nq)]
    return jnp.array([p[0] for p in pairs], jnp.int32), jnp.array([p[1] for p in pairs], jnp.int32)


def _key_le_query(t):
    return lax.broadcasted_iota(jnp.int32, (t, t), 0) <= lax.broadcasted_iota(jnp.int32, (t, t), 1)


def _attention_forward(q, k, v_t, exchange=None):
    t_len = q.shape[1]
    tq = min(512, t_len)
    nq = t_len // tq
    qi_tab, ki_tab = _triangle_steps(nq, True)

    def body(qi_ref, ki_ref, q_ref, k_ref, vt_ref, o_ref, lse_ref, m_s, l_s, acc_s):
        step = pl.program_id(0)
        qi, ki = qi_ref[step], ki_ref[step]

        @pl.when(ki == 0)
        def _():
            m_s[...] = jnp.full_like(m_s, NEG_BIG)
            l_s[...] = jnp.zeros_like(l_s)
            acc_s[...] = jnp.zeros_like(acc_s)

        def accumulate(masked):
            s_all = [_dot_nt(k_ref[h], q_ref[h]) * ATT_LOG2 for h in range(N_HEADS)]
            for h in range(N_HEADS):
                s_t = s_all[h]
                if masked:
                    s_t = jnp.where(_key_le_query(tq), s_t, NEG_BIG)
                m_old = m_s[h]
                m_new = jnp.maximum(m_old, jnp.max(s_t, axis=0, keepdims=True))
                alpha = jnp.exp2(m_old - m_new)
                p_t = jnp.exp2(s_t - m_new)
                l_s[h] = alpha * l_s[h] + jnp.sum(p_t, axis=0, keepdims=True)
                acc_s[h] = alpha * acc_s[h] + _dot(vt_ref[h], p_t.astype(BF16))
                m_s[h] = m_new

        @pl.when(ki < qi)
        def _():
            accumulate(False)

        @pl.when(ki == qi)
        def _():
            accumulate(True)
            for h in range(N_HEADS):
                o_ref[:, h * HEAD_DIM:(h + 1) * HEAD_DIM] = jnp.transpose(acc_s[h] / l_s[h])
                lse_ref[h] = m_s[h] + jnp.log2(l_s[h])

    n_steps = qi_tab.shape[0]
    return _pallas(
        body, name="attention_forward", grid=(n_steps,), prefetch=(qi_tab, ki_tab), operands=(q, k, v_t),
        out_shape=[jax.ShapeDtypeStruct((t_len, N_HEADS * HEAD_DIM), F32),
                   jax.ShapeDtypeStruct((N_HEADS, 1, t_len), F32)],
        in_specs=[pl.BlockSpec((N_HEADS, tq, QK_DIM), lambda s, qt, kt: (0, qt[s], 0)),
                  pl.BlockSpec((N_HEADS, tq, QK_DIM), lambda s, qt, kt: (0, kt[s], 0)),
                  pl.BlockSpec((N_HEADS, HEAD_DIM, tq), lambda s, qt, kt: (0, 0, kt[s]))],
        out_specs=[pl.BlockSpec((tq, N_HEADS * HEAD_DIM), lambda s, qt, kt: (qt[s], 0)),
                   pl.BlockSpec((N_HEADS, 1, tq), lambda s, qt, kt: (0, 0, qt[s]))],
        scratch_shapes=[pltpu.VMEM((N_HEADS, 1, tq), F32), pltpu.VMEM((N_HEADS, 1, tq), F32),
                        pltpu.VMEM((N_HEADS, HEAD_DIM, tq), F32)],
        params=_params(48, ("arbitrary",)), exchange=exchange,
        first=lambda qt, kt: pl.program_id(0) == 0, last=lambda qt, kt: pl.program_id(0) == n_steps - 1)


BWD_HEADS = 4


def _attention_backward(q, k, k_t, v, d_cat, lse, delta, exchange=None):
    t_len = q.shape[1]
    tq = min(512, t_len)
    nq = t_len // tq
    hp = BWD_HEADS
    qi_tab, ki_tab = _triangle_steps(nq, False)

    def body(qi_ref, ki_ref, q_ref, k_ref, kt_ref, v_ref, do_ref, lse_ref, delta_ref, dqt_hbm, dk_ref, dv_ref,
             dqt_s, dk_s, dv_s):
        group, step = pl.program_id(0), pl.program_id(1)
        qi, ki = qi_ref[step], ki_ref[step]

        @pl.when(step == 0)
        def _():
            dqt_s[...] = jnp.zeros_like(dqt_s)

        @pl.when(qi == ki)
        def _():
            dk_s[...] = jnp.zeros_like(dk_s)
            dv_s[...] = jnp.zeros_like(dv_s)

        def accumulate(masked):
            for h in range(hp):
                do_b = do_ref[:, h * HEAD_DIM:(h + 1) * HEAD_DIM].astype(BF16)
                s_t = _dot_nt(k_ref[h], q_ref[h]) * ATT_LOG2
                if masked:
                    s_t = jnp.where(_key_le_query(tq), s_t, NEG_BIG)
                p_t = jnp.exp2(s_t - lse_ref[h])
                dp_t = _dot_nt(v_ref[h], do_b)
                ds_t = (p_t * (dp_t - delta_ref[h]) * ATT_SCALE).astype(BF16)
                dv_s[h] += _dot(p_t.astype(BF16), do_b)
                dk_s[h] += _dot(ds_t, q_ref[h])
                dqt_s[h, qi, 0:QK_REAL, :] += _dot(kt_ref[h, 0:QK_REAL, :], ds_t)

        @pl.when(ki < qi)
        def _():
            accumulate(False)

        @pl.when(ki == qi)
        def _():
            accumulate(True)
            for h in range(hp):
                pltpu.sync_copy(dqt_s.at[h, qi], dqt_hbm.at[group * hp + h, qi])

        @pl.when(qi == nq - 1)
        def _():
            dk_ref[...] = dk_s[...]
            dv_ref[...] = dv_s[...]

    wide = hp * HEAD_DIM
    n_groups, n_steps = N_HEADS // hp, qi_tab.shape[0]
    return _pallas(
        body, name="attention_backward", grid=(n_groups, n_steps), prefetch=(qi_tab, ki_tab),
        operands=(q, k, k_t, v, d_cat, lse, delta),
        out_shape=[jax.ShapeDtypeStruct((N_HEADS, nq, QK_DIM, tq), F32),
                   jax.ShapeDtypeStruct((N_HEADS, t_len, QK_DIM), F32),
                   jax.ShapeDtypeStruct((N_HEADS, t_len, HEAD_DIM), F32)],
        in_specs=[pl.BlockSpec((hp, tq, QK_DIM), lambda g, s, qt, kt: (g, qt[s], 0)),
                  pl.BlockSpec((hp, tq, QK_DIM), lambda g, s, qt, kt: (g, kt[s], 0)),
                  pl.BlockSpec((hp, QK_DIM, tq), lambda g, s, qt, kt: (g, 0, kt[s])),
                  pl.BlockSpec((hp, tq, HEAD_DIM), lambda g, s, qt, kt: (g, kt[s], 0)),
                  pl.BlockSpec((tq, wide), lambda g, s, qt, kt: (qt[s], n_groups + g)),
                  pl.BlockSpec((hp, 1, tq), lambda g, s, qt, kt: (g, 0, qt[s])),
                  pl.BlockSpec((hp, 1, tq), lambda g, s, qt, kt: (g, 0, qt[s]))],
        out_specs=[pl.BlockSpec(memory_space=pl.ANY),
                   pl.BlockSpec((hp, tq, QK_DIM), lambda g, s, qt, kt: (g, kt[s], 0)),
                   pl.BlockSpec((hp, tq, HEAD_DIM), lambda g, s, qt, kt: (g, kt[s], 0))],
        scratch_shapes=[pltpu.VMEM((hp, nq, QK_DIM, tq), F32), pltpu.VMEM((hp, tq, QK_DIM), F32),
                        pltpu.VMEM((hp, tq, HEAD_DIM), F32)],
        params=_params(58, ("arbitrary", "arbitrary")), exchange=exchange,
        first=lambda qt, kt: (pl.program_id(0) == 0) & (pl.program_id(1) == 0),
        last=lambda qt, kt: (pl.program_id(0) == n_groups - 1) & (pl.program_id(1) == n_steps - 1))


def _out_project(o_hg, o_mla, x, g_a, w_out, exchange=None):
    t_len = x.shape[0]
    tm = min(512, t_len)
    half = N_HEADS * HEAD_DIM

    def body(ohg_ref, omla_ref, x_ref, ga_ref, w_ref, cat_ref, mix_ref, xhat_ref, rstd_ref):
        a = ohg_ref[...].astype(BF16)
        b = omla_ref[...].astype(BF16)
        cat_ref[:, 0:half] = a
        cat_ref[:, half:2 * half] = b
        mix = _dot(a, w_ref[0:half, :]) + _dot(b, w_ref[half:2 * half, :])
        mix_ref[...] = mix
        r1 = DN_ALPHA * x_ref[...] + (1.0 + ga_ref[...]) * mix
        xc = r1 - _rowmean(r1)
        rstd = lax.rsqrt(_rowmean(xc * xc) + LN_EPS)
        xhat_ref[...] = xc * rstd
        rstd_ref[...] = rstd

    row = lambda i: (i, 0)
    fixed = lambda i: (0, 0)
    n_tiles = t_len // tm
    return _pallas(
        body, name="out_project", grid=(n_tiles,), operands=(o_hg, o_mla, x, g_a, w_out),
        out_shape=[jax.ShapeDtypeStruct((t_len, D_MODEL), BF16), jax.ShapeDtypeStruct((t_len, D_MODEL), F32),
                   jax.ShapeDtypeStruct((t_len, D_MODEL), F32), jax.ShapeDtypeStruct((t_len, 1), F32)],
        in_specs=[pl.BlockSpec((tm, half), row), pl.BlockSpec((tm, half), row), pl.BlockSpec((tm, D_MODEL), row),
                  pl.BlockSpec((1, D_MODEL), fixed), pl.BlockSpec((D_MODEL, D_MODEL), fixed)],
        out_specs=[pl.BlockSpec((tm, D_MODEL), row), pl.BlockSpec((tm, D_MODEL), row),
                   pl.BlockSpec((tm, D_MODEL), row), pl.BlockSpec((tm, 1), row)],
        params=_params(48, ("arbitrary",)), exchange=exchange,
        first=lambda: pl.program_id(0) == 0, last=lambda: pl.program_id(0) == n_tiles - 1)


V_LN1G, V_LN1B, V_SCM, V_SHM, V_GM, V_GA, V_LN2G, V_LN2B = range(8)
S_DLN2G, S_DLN2B, S_DGM, S_DSCM, S_DSHM, S_DLN1G, S_DLN1B, S_DGA, S_LOSS = range(9)


def _mlp_and_back(xhat1, rstd1, mix, target, o_mla, vecs, w1_top, w1_bottom, w2, w_out):
    t_len = xhat1.shape[0]
    tm = min(256, t_len)
    n_ff = w1_top.shape[0]
    ff = w1_top.shape[2]
    top_rows = w1_top.shape[1]

    def body(xhat_ref, rstd_ref, mix_ref, tgt_ref, omla_ref, vec_ref, w1_top_hbm, w1_bottom_hbm, w2_hbm, wout_hbm,
             act_ref, dhp_ref, um_ref, dh_ref, dmix_ref, dcat_ref, dr1_ref, sums_ref, delta_ref,
             w1_s, w2_s, wout_s, hp_s, load_sems):
        @pl.when(pl.program_id(0) == 0)
        def _():
            loads = [pltpu.make_async_copy(w1_top_hbm, w1_s.at[:, 0:top_rows], load_sems.at[0]),
                     pltpu.make_async_copy(w1_bottom_hbm, w1_s.at[:, top_rows:D_MODEL], load_sems.at[3]),
                     pltpu.make_async_copy(w2_hbm, w2_s, load_sems.at[1]),
                     pltpu.make_async_copy(wout_hbm, wout_s, load_sems.at[2])]
            for cp in loads:
                cp.start()
            sums_ref[...] = jnp.zeros_like(sums_ref)
            for cp in loads:
                cp.wait()

        vec = lambda r: vec_ref[r:r + 1, :]
        xhat = xhat_ref[...]
        x1 = xhat * vec(V_LN1G) + vec(V_LN1B)
        um = (x1 * (1.0 + vec(V_SCM)) + vec(V_SHM)).astype(BF16)
        um_ref[...] = um
        h = jnp.zeros((tm, D_MODEL), F32)
        for j in range(n_ff):
            hp = _dot(um, w1_s[j])
            hp_s[j] = hp
            act = jnp.square(jnp.maximum(hp, 0.0)).astype(BF16)
            act_ref[:, j * ff:(j + 1) * ff] = act
            h = h + _dot(act, w2_s[j])
        r2 = DN_ALPHA * x1 + (1.0 + vec(V_GM)) * h
        xc = r2 - _rowmean(r2)
        rstd2 = lax.rsqrt(_rowmean(xc * xc) + LN_EPS)
        xhat2 = xc * rstd2
        err = xhat2 * vec(V_LN2G) + vec(V_LN2B) - tgt_ref[...]
        loss = 0.5 * jnp.sum(_rowmean(err * err))
        dy = err * (1.0 / D_MODEL)
        dxh = dy * vec(V_LN2G)
        dr2 = rstd2 * (dxh - _rowmean(dxh) - xhat2 * _rowmean(dxh * xhat2))
        dh = ((1.0 + vec(V_GM)) * dr2).astype(BF16)
        dh_ref[...] = dh
        sums_ref[S_DLN2G:S_DLN2G + 1, :] += _colsum(dy * xhat2)
        sums_ref[S_DLN2B:S_DLN2B + 1, :] += _colsum(dy)
        sums_ref[S_DGM:S_DGM + 1, :] += _colsum(dr2 * h)
        sums_ref[S_LOSS:S_LOSS + 1, :] += jnp.full((1, D_MODEL), loss, F32)
        du = jnp.zeros((tm, D_MODEL), F32)
        for j in range(n_ff):
            dhp = (_dot_nt(dh, w2_s[j]) * (2.0 * jnp.maximum(hp_s[j], 0.0))).astype(BF16)
            dhp_ref[:, j * ff:(j + 1) * ff] = dhp
            du = du + _dot_nt(dhp, w1_s[j])
        sums_ref[S_DSCM:S_DSCM + 1, :] += _colsum(du * x1)
        sums_ref[S_DSHM:S_DSHM + 1, :] += _colsum(du)
        dx1 = DN_ALPHA * dr2 + du * (1.0 + vec(V_SCM))
        sums_ref[S_DLN1G:S_DLN1G + 1, :] += _colsum(dx1 * xhat)
        sums_ref[S_DLN1B:S_DLN1B + 1, :] += _colsum(dx1)
        dxh1 = dx1 * vec(V_LN1G)
        dr1 = rstd_ref[...] * (dxh1 - _rowmean(dxh1) - xhat * _rowmean(dxh1 * xhat))
        dr1_ref[...] = dr1
        sums_ref[S_DGA:S_DGA + 1, :] += _colsum(dr1 * mix_ref[...])
        dmix = ((1.0 + vec(V_GA)) * dr1).astype(BF16)
        dmix_ref[...] = dmix
        dcat = _dot_nt(dmix, wout_s[...])
        dcat_ref[...] = dcat
        half = N_HEADS * HEAD_DIM
        for hd in range(N_HEADS):
            prod = dcat[:, half + hd * HEAD_DIM:half + (hd + 1) * HEAD_DIM] * omla_ref[:, hd * HEAD_DIM:(hd + 1) * HEAD_DIM]
            sums = jnp.broadcast_to(jnp.sum(prod, axis=1, keepdims=True), (tm, HEAD_DIM))
            delta_ref[hd] = jnp.transpose(sums)[0:1]

    row = lambda i: (i, 0)
    fixed = lambda i: (0, 0)
    any_spec = pl.BlockSpec(memory_space=pl.ANY)
    return _pcall(
        body, name="mlp_and_back", grid=(t_len // tm,),
        out_shape=[jax.ShapeDtypeStruct((t_len, D_FF), BF16), jax.ShapeDtypeStruct((t_len, D_FF), BF16),
                   jax.ShapeDtypeStruct((t_len, D_MODEL), BF16), jax.ShapeDtypeStruct((t_len, D_MODEL), BF16),
                   jax.ShapeDtypeStruct((t_len, D_MODEL), BF16), jax.ShapeDtypeStruct((t_len, D_MODEL), F32),
                   jax.ShapeDtypeStruct((t_len, D_MODEL), F32), jax.ShapeDtypeStruct((16, D_MODEL), F32),
                   jax.ShapeDtypeStruct((N_HEADS, 1, t_len), F32)],
        in_specs=[pl.BlockSpec((tm, D_MODEL), row), pl.BlockSpec((tm, 1), row), pl.BlockSpec((tm, D_MODEL), row),
                  pl.BlockSpec((tm, D_MODEL), row), pl.BlockSpec((tm, N_HEADS * HEAD_DIM), row),
                  pl.BlockSpec((8, D_MODEL), fixed), any_spec, any_spec, any_spec, any_spec],
        out_specs=[pl.BlockSpec((tm, D_FF), row), pl.BlockSpec((tm, D_FF), row), pl.BlockSpec((tm, D_MODEL), row),
                   pl.BlockSpec((tm, D_MODEL), row), pl.BlockSpec((tm, D_MODEL), row), pl.BlockSpec((tm, D_MODEL), row),
                   pl.BlockSpec((tm, D_MODEL), row), pl.BlockSpec((16, D_MODEL), fixed),
                   pl.BlockSpec((N_HEADS, 1, tm), lambda i: (0, 0, i))],
        scratch_shapes=[pltpu.VMEM((n_ff, D_MODEL, ff), BF16), pltpu.VMEM(w2.shape, BF16), pltpu.VMEM(w_out.shape, BF16),
                        pltpu.VMEM((n_ff, tm, ff), F32), pltpu.SemaphoreType.DMA((4,))],
        compiler_params=_params(56, ("arbitrary",)),
        operands=(xhat1, rstd1, mix, target, o_mla, vecs, w1_top, w1_bottom, w2, w_out))


def _in_project_backward(dq, dk, dv, cq, ckv, pos, invf, q_norm_w, kv_norm_w, w_q, w_kv,
                         d_hq, d_hf, d_hi, d_hg, w_in, dr1, x, sc_a, exchange=None):
    t_len = x.shape[0]
    tm = min(512, t_len)
    per_q = dq.shape[3] // tm
    hgw = N_HEADS * HEAD_DIM

    def body(dq_ref, dk_ref, dv_ref, cq_ref, ckv_ref, pos_ref, invf_ref, qn_ref, kvn_ref, wq_ref, wkv_ref,
             dhq_ref, dhf_ref, dhi_ref, dhg_ref, win_ref, dr1_ref, x_ref, sc_ref,
             dz_ref, gx_ref, sums_ref, dwq_ref, dwkv_ref):
        @pl.when(pl.program_id(0) == 0)
        def _():
            sums_ref[...] = jnp.zeros_like(sums_ref)
            dwq_ref[...] = jnp.zeros_like(dwq_ref)
            dwkv_ref[...] = jnp.zeros_like(dwkv_ref)

        cos_t, sin_t = _rope_tables(pos_ref[...], invf_ref[...])
        cq = cq_ref[...]
        ckv = ckv_ref[...]
        rq = lax.rsqrt(_rowmean(cq * cq) + RMS_EPS)
        rkv = lax.rsqrt(_rowmean(ckv * ckv) + RMS_EPS)
        cqn = (cq * rq * qn_ref[...]).astype(BF16)
        ckvn = (ckv * rkv * kvn_ref[...]).astype(BF16)
        d_cqn = jnp.zeros((tm, Q_RANK), F32)
        d_ckvn = jnp.zeros((tm, KV_RANK), F32)
        d_kpe = jnp.zeros((tm, 128), F32)
        for h in range(N_HEADS):
            dqh = jnp.transpose(dq_ref[h])
            dq_full = jnp.concatenate(
                [dqh[:, :HEAD_DIM].astype(BF16), _unrope(dqh[:, HEAD_DIM:], cos_t, sin_t).astype(BF16)], axis=1)
            d_cqn = d_cqn + _dot_nt(dq_full, wq_ref[h])
            dwq_ref[h] += _dot_tn(cqn, dq_full)
            dkh = dk_ref[h]
            d_kpe = d_kpe + dkh[:, HEAD_DIM:]
            dkv_up = jnp.concatenate([dkh[:, :HEAD_DIM].astype(BF16), dv_ref[h].astype(BF16)], axis=1)
            d_ckvn = d_ckvn + _dot_nt(dkv_up, wkv_ref[h])
            dwkv_ref[h] += _dot_tn(ckvn, dkv_up)
        dyq = d_cqn * qn_ref[...]
        dykv = d_ckvn * kvn_ref[...]
        sums_ref[2:3, 0:Q_RANK] += _colsum(d_cqn * cq * rq)
        sums_ref[3:4, 0:KV_RANK] += _colsum(d_ckvn * ckv * rkv)
        dz_ref[:, 0:hgw] = dhq_ref[...]
        dz_ref[:, hgw:2 * hgw] = dhf_ref[...]
        dz_ref[:, 2 * hgw:3 * hgw] = dhi_ref[...]
        dz_ref[:, 3 * hgw:4 * hgw] = dhg_ref[...]
        dz_ref[:, HG_COLS:HG_COLS + Q_RANK] = (rq * dyq - cq * (rq * rq * rq) * _rowmean(dyq * cq)).astype(BF16)
        dz_ref[:, HG_COLS + Q_RANK:HG_COLS + Q_RANK + KV_RANK] = (
            rkv * dykv - ckv * (rkv * rkv * rkv) * _rowmean(dykv * ckv)).astype(BF16)
        dz_ref[:, HG_COLS + Q_RANK + KV_RANK:] = _unrope(d_kpe, cos_t, sin_t).astype(BF16)
        du = _dot(dz_ref[...], win_ref[...])
        xv = x_ref[...]
        gx_ref[...] = DN_ALPHA * dr1_ref[...] + (1.0 + sc_ref[...]) * du
        sums_ref[0:1, :] += _colsum(du * xv)
        sums_ref[1:2, :] += _colsum(du)

    row = lambda i: (i, 0)
    fixed2 = lambda i: (0, 0)
    fixed3 = lambda i: (0, 0, 0)
    heads = lambda i: (0, i, 0)
    n_tiles = t_len // tm
    return _pallas(
        body, name="in_project_backward", grid=(n_tiles,),
        operands=(dq, dk, dv, cq, ckv, pos, invf, q_norm_w, kv_norm_w, w_q, w_kv, d_hq, d_hf, d_hi, d_hg, w_in, dr1, x,
                  sc_a),
        out_shape=[jax.ShapeDtypeStruct((t_len, IN_COLS_PAD), BF16), jax.ShapeDtypeStruct((t_len, D_MODEL), F32),
                   jax.ShapeDtypeStruct((8, D_MODEL), F32), jax.ShapeDtypeStruct((N_HEADS, Q_RANK, QK_DIM), F32),
                   jax.ShapeDtypeStruct((N_HEADS, KV_RANK, 2 * HEAD_DIM), F32)],
        in_specs=[pl.BlockSpec((N_HEADS, None, QK_DIM, tm), lambda i: (0, i // per_q, 0, i % per_q)),
                  pl.BlockSpec((N_HEADS, tm, QK_DIM), heads),
                  pl.BlockSpec((N_HEADS, tm, HEAD_DIM), heads), pl.BlockSpec((tm, Q_RANK), row),
                  pl.BlockSpec((tm, KV_RANK), row), pl.BlockSpec((tm, 1), row), pl.BlockSpec((1, 128), fixed2),
                  pl.BlockSpec((1, Q_RANK), fixed2), pl.BlockSpec((1, KV_RANK), fixed2),
                  pl.BlockSpec((N_HEADS, Q_RANK, QK_DIM), fixed3), pl.BlockSpec((N_HEADS, KV_RANK, 2 * HEAD_DIM), fixed3),
                  pl.BlockSpec((tm, hgw), row), pl.BlockSpec((tm, hgw), row), pl.BlockSpec((tm, hgw), row),
                  pl.BlockSpec((tm, hgw), row), pl.BlockSpec((IN_COLS_PAD, D_MODEL), fixed2),
                  pl.BlockSpec((tm, D_MODEL), row), pl.BlockSpec((tm, D_MODEL), row), pl.BlockSpec((1, D_MODEL), fixed2)],
        out_specs=[pl.BlockSpec((tm, IN_COLS_PAD), row), pl.BlockSpec((tm, D_MODEL), row),
                   pl.BlockSpec((8, D_MODEL), fixed2), pl.BlockSpec((N_HEADS, Q_RANK, QK_DIM), fixed3),
                   pl.BlockSpec((N_HEADS, KV_RANK, 2 * HEAD_DIM), fixed3)],
        params=_params(48, ("arbitrary",)), exchange=exchange,
        first=lambda: pl.program_id(0) == 0, last=lambda: pl.program_id(0) == n_tiles - 1)


def _weight_grad(a, b, name, n_blocks, bn, a_blocked=False, b_blocked=True, exchange=None, token_tile=512):
    t_len = a.shape[0]
    m = a.shape[1] // n_blocks if a_blocked else a.shape[1]
    bt = min(token_tile, t_len)

    def body(a_ref, b_ref, o_ref):
        @pl.when(pl.program_id(1) == 0)
        def _():
            o_ref[...] = jnp.zeros_like(o_ref)

        o_ref[...] += _dot_tn(a_ref[...].astype(BF16), b_ref[...].astype(BF16))

    a_spec = pl.BlockSpec((bt, m), (lambda n, t: (t, n)) if a_blocked else (lambda n, t: (t, 0)))
    b_spec = pl.BlockSpec((bt, bn), (lambda n, t: (t, n)) if b_blocked else (lambda n, t: (t, 0)))
    nt = t_len // bt
    (out,), landed = _pallas(
        body, name=name, grid=(n_blocks, nt), operands=(a, b),
        out_shape=[jax.ShapeDtypeStruct((n_blocks, m, bn), F32)],
        in_specs=[a_spec, b_spec],
        out_specs=[pl.BlockSpec((None, m, bn), lambda n, t: (n, 0, 0))],
        params=_params(56, ("arbitrary", "arbitrary")), exchange=exchange,
        first=lambda: (pl.program_id(0) == 0) & (pl.program_id(1) == 0),
        last=lambda: (pl.program_id(0) == n_blocks - 1) & (pl.program_id(1) == nt - 1))
    return (out, landed) if exchange else out


SMALL_PLACE = {"ln1_g": (6, 0), "ln1_b": (7, 0), "ln2_g": (8, 0), "ln2_b": (9, 0), "hg_norm_w": (10, 512),
               "mla_q_norm_w": (11, 0), "mla_kv_norm_w": (11, Q_RANK)}
SMALL_LB_ROW, SMALL_LOSS_ROW = 10, 12


def _small_params_step(gathered, params):
    names = list(params)

    def body(g_ref, *refs):
        ins, outs = refs[:3 * len(names)], refs[3 * len(names):]
        loss_ref, outs = outs[0], outs[1:]
        tot = g_ref[0]
        for d in range(1, N_DEV):
            tot = tot + g_ref[d]
        loss_ref[...] = tot[SMALL_LOSS_ROW:SMALL_LOSS_ROW + 1, 0:128]

        def update(i, grad, rows=slice(None), lanes=slice(None)):
            w_ref, m_ref, v_ref = ins[3 * i:3 * i + 3]
            g_out, d_out, nm_out, nv_out = outs[4 * i:4 * i + 4]
            g_out[rows, lanes] = grad
            d_out[rows, lanes], nm_out[rows, lanes], nv_out[rows, lanes] = _adamw_update(
                w_ref[rows, lanes], grad, m_ref[rows, lanes], v_ref[rows, lanes])

        for i, name in enumerate(names):
            if name == "b_ada":
                for r in range(6):
                    update(i, tot[r:r + 1, :], lanes=slice(r * D_MODEL, (r + 1) * D_MODEL))
            elif name == "hg_lower_bounds":
                lb = _lower_bound(ins[3 * i][...])
                d0 = tot[SMALL_LB_ROW:SMALL_LB_ROW + 1, 0:512] * lb * (1.0 - lb)
                update(i, d0, rows=slice(0, 1))
                update(i, -d0, rows=slice(1, 2))
            else:
                row, lane = SMALL_PLACE[name]
                update(i, tot[row:row + 1, lane:lane + params[name][0].shape[1]])

    flat_in = [a for name in names for a in params[name]]
    shapes = [jax.ShapeDtypeStruct((1, 128), F32)] + [jax.ShapeDtypeStruct(params[name][0].shape, F32)
                                                      for name in names for _ in range(4)]
    out = pl.pallas_call(body, name="small_params_step", out_shape=shapes)(gathered, *flat_in)
    return out[0], {name: out[1 + 4 * i:5 + 4 * i] for i, name in enumerate(names)}


def _adamw_update(w, gv, m, v):
    nm = ADAM_B1 * m + (1.0 - ADAM_B1) * gv
    nv = ADAM_B2 * v + (1.0 - ADAM_B2) * jnp.square(gv)
    m_hat = nm / (1.0 - ADAM_B1 ** ADAM_STEP)
    v_hat = nv / (1.0 - ADAM_B2 ** ADAM_STEP)
    return -ADAM_LR * (m_hat / (jnp.sqrt(v_hat) + ADAM_EPS) + ADAM_WD * w), nm, nv


def _adamw_halves(core, w, mine, theirs, m, v, name):
    rows, cols = w.shape
    h = rows // 2
    tr = _row_tile(h)
    per_half = h // tr

    def body(core_ref, w_ref, mine_ref, theirs_ref, m_ref, v_ref, g_ref, d_ref, nm_ref, nv_ref):
        is_mine = pl.program_id(0) // per_half == core_ref[0]
        gv = jnp.where(is_mine, mine_ref[...], theirs_ref[...])
        g_ref[...] = gv
        d_ref[...], nm_ref[...], nv_ref[...] = _adamw_update(w_ref[...], gv, m_ref[...], v_ref[...])

    full = pl.BlockSpec((tr, cols), lambda i, core_ref: (i, 0))
    part = pl.BlockSpec((tr, cols), lambda i, core_ref: (i % per_half, 0))
    return _pcall(
        body, name=name, out_shape=[jax.ShapeDtypeStruct(w.shape, F32)] * 4,
        grid_spec=pltpu.PrefetchScalarGridSpec(
            num_scalar_prefetch=1, grid=(rows // tr,), in_specs=[full, part, part, full, full], out_specs=[full] * 4),
        compiler_params=_params(40, ("arbitrary",)),
        operands=(core, w, mine, theirs, m, v))


def _adamw(w, g, m, v, name):
    rows, cols = w.shape
    tr = _row_tile(rows) if rows >= 8 else rows

    def body(w_ref, g_ref, m_ref, v_ref, d_ref, nm_ref, nv_ref):
        d_ref[...], nm_ref[...], nv_ref[...] = _adamw_update(w_ref[...], g_ref[...], m_ref[...], v_ref[...])

    spec = pl.BlockSpec((tr, cols), lambda i: (i, 0))
    return _pcall(
        body, name=name, grid=(rows // tr,),
        out_shape=[jax.ShapeDtypeStruct(w.shape, F32)] * 3,
        in_specs=[spec] * 4, out_specs=[spec] * 3,
        compiler_params=_params(40, ("arbitrary",)),
        operands=(w, g, m, v))


def kernel(x, c, positions, w_ada, b_ada, w_in, hg_lower_bounds, hg_norm_w, mla_q_norm_w, w_q_up, mla_kv_norm_w, w_kv_up, w_out, ln1_g, ln1_b, w_mlp_in, w_mlp_out, ln2_g, ln2_b, loss_target, m_w_ada, m_b_ada, m_w_in, m_hg_lower_bounds, m_hg_norm_w, m_mla_q_norm_w, m_w_q_up, m_mla_kv_norm_w, m_w_kv_up, m_w_out, m_ln1_g, m_ln1_b, m_w_mlp_in, m_w_mlp_out, m_ln2_g, m_ln2_b, v_w_ada, v_b_ada, v_w_in, v_hg_lower_bounds, v_hg_norm_w, v_mla_q_norm_w, v_w_q_up, v_mla_kv_norm_w, v_w_kv_up, v_w_out, v_ln1_g, v_ln1_b, v_w_mlp_in, v_w_mlp_out, v_ln2_g, v_ln2_b):
    ix, iy, ic = _mesh_pos()
    chip = 2 * ix + iy
    me = 4 * ix + 2 * iy + ic
    core_arr = jnp.reshape(ic, (1,)).astype(jnp.int32)
    chip_arr = jnp.reshape(chip, (1,)).astype(jnp.int32)

    xs = x[0]
    target = loss_target[0]
    t_len = xs.shape[0]
    pos = positions.astype(F32).reshape(t_len, 1)
    inv = 1.0 / (ROPE_THETA ** (jnp.arange(0, ROPE_DIM, 2, dtype=F32) / ROPE_DIM))
    invf = jnp.concatenate([inv, inv, jnp.zeros((128 - ROPE_DIM,), F32)]).reshape(1, 128)

    def slot(w):
        rows, cols = w.shape
        own = w.astype(BF16).reshape(1, 2, rows // 2, cols)
        return lax.dynamic_update_slice(jnp.zeros((N_CHIPS, 2, rows // 2, cols), BF16), own, (chip, 0, 0, 0))

    def slot8(a):
        return lax.dynamic_update_slice(jnp.zeros((N_DEV,) + a.shape, a.dtype), a[None], (me, 0, 0))

    def whole(s):
        return s.reshape(N_CHIPS, 2 * s.shape[2], s.shape[3])

    def halved(g):
        return g.reshape(N_CHIPS, 2, g.shape[1] // 2, g.shape[2])

    ada_cols = w_ada.shape[2]
    c_all, *early = _run_exchange(
        _merge(_gather_all(slot8(jnp.broadcast_to(c, (8, D_MODEL)))),
               _gather_over_ici_in_two_steps([slot(jnp.transpose(w_in[0])), slot(w_q_up[0]), slot(w_kv_up[0])])),
        "gather_c_and_mixer_weights_ici")
    b_shard = lax.dynamic_slice(b_ada, (0, chip * ada_cols), (1, ada_cols))
    mod_cols, cond16 = _ada_project(c_all[:, 0, :], w_ada[0], b_shard)
    mod_all, *early = _run_exchange(_merge(_gather_all(slot8(mod_cols)), _gather_over_d2d(early)),
                                    "gather_mod_and_mixer_weights_d2d")
    mod_mine = lax.dynamic_slice(mod_all, (0, me, 0), (N_DEV, 1, ada_cols))[::2, 0, :].reshape(6, D_MODEL)
    sh_a, sc_a, g_a, sh_m, sc_m, g_m = (mod_mine[i:i + 1] for i in range(6))
    g_in, g_q, g_kv = (whole(s) for s in early)
    w_in_full = jnp.pad(g_in.reshape(IN_COLS, D_MODEL), ((0, IN_COLS_PAD - IN_COLS), (0, 0)))
    w_q_full = jnp.pad(g_q, ((0, 0), (0, 0), (0, QK_DIM - g_q.shape[2])))

    w1_rows = D_MODEL // 2
    (u_a, zhg, cq, ckv, q, k, k_t, v, v_t), (s_top, s_out) = _in_project(
        xs, pos, sc_a, sh_a, w_in_full, mla_q_norm_w, mla_kv_norm_w, w_q_full, g_kv, invf,
        _gather_over_ici([slot(w_mlp_in[0, :w1_rows]), slot(w_out[0])]))
    (o_pre, o_hg, states), (s_bottom, s_top, s_out) = _hgrn_forward(
        zhg, hg_lower_bounds, hg_norm_w,
        _merge(_gather_over_ici([slot(w_mlp_in[0, w1_rows:])]), _gather_over_d2d([s_top, s_out])))
    (o_mla, lse), (s_w2, s_bottom) = _attention_forward(
        q, k, v_t, _merge(_gather_over_ici([slot(w_mlp_out[0])]), _gather_over_d2d([s_bottom])))
    w_out_full = whole(s_out).reshape(D_MODEL, D_MODEL)
    (cat, mix, xhat1, rstd1), (s_w2,) = _out_project(o_hg, o_mla, xs, g_a, w_out_full, _gather_over_d2d([s_w2]))
    g_w1_top, g_w1_bottom, g_w2 = whole(s_top), whole(s_bottom), whole(s_w2)
    vecs = jnp.concatenate([ln1_g, ln1_b, sc_m, sh_m, g_m, g_a, ln2_g, ln2_b], axis=0)
    act, dhp, um, dh, dmix, d_cat, dr1, mlp_sums, delta = _mlp_and_back(
        xhat1, rstd1, mix, target, o_mla, vecs, g_w1_top, g_w1_bottom, g_w2, w_out_full)

    gw_1 = halved(_weight_grad(um, dhp, "grad_w_mlp_in", N_CHIPS, D_FF // N_CHIPS, token_tile=4096))
    gw_2, (landed_1,) = _weight_grad(act, dh, "grad_w_mlp_out", N_CHIPS, D_MODEL, a_blocked=True, b_blocked=False,
                                     token_tile=4096, exchange=_pair_exchange([gw_1]))
    gw_out = _weight_grad(cat, dmix, "grad_w_out", 1, D_MODEL, token_tile=2048)
    later = [halved(gw_2), halved(gw_out.reshape(N_CHIPS, D_MODEL // N_CHIPS, D_MODEL))]
    own_1, travels_1 = _add_pair(core_arr, chip_arr, gw_1, landed_1)
    (dq, dk, dv), (landed_1, *landed) = _attention_backward(
        q, k, k_t, v, d_cat, lse, delta, _merge(_chip_exchange([travels_1]), _pair_exchange(later)))
    mine_1 = _add_chips(own_1, landed_1)
    chip_sums = [_add_pair(core_arr, chip_arr, g, l) for g, l in zip(later, landed)]
    (d_hq, d_hf, d_hi, d_hg, hg_sums), (theirs_1, *landed) = _hgrn_backward(
        zhg, hg_lower_bounds, hg_norm_w, o_pre, d_cat, states,
        _merge(_pair_send([mine_1]), _chip_exchange([b for _, b in chip_sums])))
    later_mine = [_add_chips(own, l) for (own, _), l in zip(chip_sums, landed)]
    mlp_mine = [mine_1] + later_mine
    (dz, grad_x, in_sums, gw_q, gw_kv), _ = _in_project_backward(
        dq, dk, dv, cq, ckv, pos, invf, mla_q_norm_w, mla_kv_norm_w, w_q_full, g_kv,
        d_hq, d_hf, d_hi, d_hg, w_in_full, dr1, xs, sc_a)

    zeros = lambda n: jnp.zeros((1, n), F32)
    small = jnp.concatenate([
        in_sums[1:2], in_sums[0:1], mlp_sums[S_DGA:S_DGA + 1],
        mlp_sums[S_DSHM:S_DSHM + 1], mlp_sums[S_DSCM:S_DSCM + 1], mlp_sums[S_DGM:S_DGM + 1],
        mlp_sums[S_DLN1G:S_DLN1G + 1], mlp_sums[S_DLN1B:S_DLN1B + 1],
        mlp_sums[S_DLN2G:S_DLN2G + 1], mlp_sums[S_DLN2B:S_DLN2B + 1],
        jnp.concatenate([hg_sums[0:1], hg_sums[1:2]], axis=1),
        jnp.concatenate([in_sums[2:3, :Q_RANK], in_sums[3:4, :KV_RANK], zeros(D_MODEL - Q_RANK - KV_RANK)], axis=1),
        mlp_sums[S_LOSS:S_LOSS + 1],
        jnp.zeros((SMALL_ROWS - 13, D_MODEL), F32)], axis=0)

    gw_in, (*later_theirs, small_all) = _weight_grad(
        dz, u_a, "grad_w_in", 3, D_MODEL, a_blocked=True, b_blocked=False, token_tile=4096,
        exchange=_merge(_pair_send(later_mine), _gather_all(slot8(small))))
    mlp_theirs = [theirs_1] + list(later_theirs)
    gw_in = gw_in.reshape(IN_COLS_PAD, D_MODEL)
    gw_q = gw_q[:, :, :HEAD_DIM + ROPE_DIM]
    flat = lambda g: g.reshape(g.shape[0] * g.shape[1], g.shape[2])
    mixer_mine, mixer_theirs = _reduce_in_vmem(
        [gw_in, flat(gw_q), flat(gw_kv)], [IN_COLS // N_CHIPS // 2, Q_RANK // 2, KV_RANK // 2], "reduce_mixer_grads")
    reduced = ("w_in", "w_q_up", "w_kv_up", "w_mlp_in", "w_mlp_out", "w_out")
    halves_mine = dict(zip(reduced, list(mixer_mine) + mlp_mine))
    halves_theirs = dict(zip(reduced, list(mixer_theirs) + list(mlp_theirs)))

    small_names = ("b_ada", "hg_lower_bounds", "hg_norm_w", "mla_q_norm_w", "mla_kv_norm_w",
                   "ln1_g", "ln1_b", "ln2_g", "ln2_b")
    loss_row, small_out = _small_params_step(small_all, {
        "b_ada": (b_ada, m_b_ada, v_b_ada),
        "hg_lower_bounds": (hg_lower_bounds, m_hg_lower_bounds, v_hg_lower_bounds),
        "hg_norm_w": (hg_norm_w, m_hg_norm_w, v_hg_norm_w),
        "mla_q_norm_w": (mla_q_norm_w, m_mla_q_norm_w, v_mla_q_norm_w),
        "mla_kv_norm_w": (mla_kv_norm_w, m_mla_kv_norm_w, v_mla_kv_norm_w),
        "ln1_g": (ln1_g, m_ln1_g, v_ln1_g), "ln1_b": (ln1_b, m_ln1_b, v_ln1_b),
        "ln2_g": (ln2_g, m_ln2_g, v_ln2_g), "ln2_b": (ln2_b, m_ln2_b, v_ln2_b)})
    loss = loss_row[0, 0]

    d_mod_all = small_all[:, 0:6, :].reshape(N_DEV, 6 * D_MODEL)
    d_mod_cols = lax.dynamic_slice(d_mod_all, (0, chip * ada_cols), (N_DEV, ada_cols))
    d_mod_cols = jnp.concatenate([d_mod_cols, jnp.zeros_like(d_mod_cols)], axis=0)
    g_w_ada = _weight_grad(cond16, d_mod_cols, "grad_w_ada", 1, ada_cols)[0]

    names = ["w_ada", "b_ada", "w_in", "hg_lower_bounds", "hg_norm_w", "mla_q_norm_w", "w_q_up", "mla_kv_norm_w",
             "w_kv_up", "w_out", "ln1_g", "ln1_b", "w_mlp_in", "w_mlp_out", "ln2_g", "ln2_b"]
    weights = [w_ada, b_ada, w_in, hg_lower_bounds, hg_norm_w, mla_q_norm_w, w_q_up, mla_kv_norm_w,
               w_kv_up, w_out, ln1_g, ln1_b, w_mlp_in, w_mlp_out, ln2_g, ln2_b]
    moms = [m_w_ada, m_b_ada, m_w_in, m_hg_lower_bounds, m_hg_norm_w, m_mla_q_norm_w, m_w_q_up, m_mla_kv_norm_w,
            m_w_kv_up, m_w_out, m_ln1_g, m_ln1_b, m_w_mlp_in, m_w_mlp_out, m_ln2_g, m_ln2_b]
    vels = [v_w_ada, v_b_ada, v_w_in, v_hg_lower_bounds, v_hg_norm_w, v_mla_q_norm_w, v_w_q_up, v_mla_kv_norm_w,
            v_w_kv_up, v_w_out, v_ln1_g, v_ln1_b, v_w_mlp_in, v_w_mlp_out, v_ln2_g, v_ln2_b]
    out_g, out_d, out_m, out_v = [], [], [], []
    for name, w, m, vv in zip(names, weights, moms, vels):
        if name in small_names:
            g, d, nm, nv = small_out[name]
            back = lambda a: a
        elif name == "w_in":
            to2d, back = (lambda a: jnp.transpose(a[0])), (lambda a: jnp.transpose(a)[None])
        else:
            to2d, back = (lambda a, s=w.shape[1:]: a.reshape(s)), (lambda a, s=w.shape: a.reshape(s))
        if name == "w_ada":
            d, nm, nv = _adamw(to2d(w), g_w_ada, to2d(m), to2d(vv), "adamw_" + name)
            g = g_w_ada
        elif name not in small_names:
            g, d, nm, nv = _adamw_halves(core_arr, to2d(w), halves_mine[name], halves_theirs[name], to2d(m), to2d(vv),
                                         "adamw_" + name)
        out_g.append(back(g))
        out_d.append(back(d))
        out_m.append(back(nm))
        out_v.append(back(nv))
    return (loss, grad_x[None], *out_g, *out_d, *out_m, *out_v)
```

```python
import functools

import jax
import jax.numpy as jnp
from jax import lax
from jax.experimental import pallas as pl
from jax.experimental.pallas import tpu as pltpu

F32 = jnp.float32
BF16 = jnp.bfloat16
MESH_IDS = pl.DeviceIdType.MESH

D_MODEL = 1024
N_HEADS = 4
HEAD_DIM = 128
ROPE_DIM = 64
HG_CHUNK = 64
HG_COLS = 2048
Q_RANK = 256
KV_RANK = 256
IN_COLS = 2624
IN_COLS_PAD = 2688
QK_DIM = 256
QK_REAL = HEAD_DIM + ROPE_DIM
D_FF = 4096
N_CHIPS = 4
N_DEV = 8
ROPE_THETA = 10000.0
RMS_EPS = 1e-6
LN_EPS = 1e-5
DN_ALPHA = 2.0 ** 0.25
ATT_SCALE = (HEAD_DIM + ROPE_DIM) ** -0.5
NEG_BIG = -1e30
ADAM_LR = 0.001
ADAM_B1 = 0.9
ADAM_B2 = 0.999
ADAM_EPS = 1e-08
ADAM_WD = 0.01
ADAM_STEP = 10
SMALL_ROWS = 16
MIB = 1024 * 1024


def _dot(a, b):
    return jnp.dot(a, b, preferred_element_type=F32)


def _dot_nt(a, b):
    return lax.dot_general(a, b, (((1,), (1,)), ((), ())), preferred_element_type=F32)


def _dot_tn(a, b):
    return lax.dot_general(a, b, (((0,), (0,)), ((), ())), preferred_element_type=F32)


def _params(vmem_mib, semantics=None):
    return pltpu.CompilerParams(vmem_limit_bytes=vmem_mib * MIB, dimension_semantics=semantics)


def _sigmoid(v):
    return 1.0 / (1.0 + jnp.exp(-v))


def _colsum(v):
    return jnp.sum(v, axis=0, keepdims=True)


def _rowmean(v):
    return jnp.mean(v, axis=-1, keepdims=True)


def _rope_tables(pos, invf):
    ang = pos * invf
    lane = lax.broadcasted_iota(jnp.int32, ang.shape, 1)
    cos_t = jnp.where(lane < ROPE_DIM, jnp.cos(ang), 0.0)
    sin = jnp.sin(ang)
    sin_t = jnp.where(lane < ROPE_DIM // 2, -sin, jnp.where(lane < ROPE_DIM, sin, 0.0))
    return cos_t, sin_t


def _swap_halves(t):
    lane = lax.broadcasted_iota(jnp.int32, t.shape, 1)
    return jnp.where(lane < ROPE_DIM // 2, pltpu.roll(t, 128 - ROPE_DIM // 2, 1), pltpu.roll(t, ROPE_DIM // 2, 1))


def _rope(t, cos_t, sin_t):
    return t * cos_t + _swap_halves(t) * sin_t


def _unrope(g, cos_t, sin_t):
    return g * cos_t - _swap_halves(g) * sin_t


def _mesh_pos():
    return lax.axis_index("x"), lax.axis_index("y"), lax.axis_index("c")


def _other_chips(x, y):
    out = []
    for dx, dy in ((1, 0), (0, 1), (1, 1)):
        px = 1 - x if dx else x
        py = 1 - y if dy else y
        out.append(((px, py), 2 * px + py))
    return out


class _Exchange:
    def __init__(self, inputs, out_shapes, aliases, sems, start, finish):
        self.inputs, self.out_shapes, self.aliases, self.sems = list(inputs), list(out_shapes), dict(aliases), list(sems)
        self.start, self.finish = start, finish


def _from_copies(inputs, out_shapes, aliases, sems, copies):
    def start(ins, outs, sem_refs):
        for send, _ in copies(ins, outs, sem_refs):
            send.start()

    def finish(ins, outs, sem_refs):
        for send, recv in copies(ins, outs, sem_refs):
            recv.wait_recv()
            send.wait_send()

    return _Exchange(inputs, out_shapes, aliases, sems, start, finish)


HBM_MIN_BYTES = 256 * 1024


def _in_hbm(a):
    if a.size * a.dtype.itemsize < HBM_MIN_BYTES:
        return a
    return pltpu.with_memory_space_constraint(a, pltpu.HBM)


def _out_hbm(s):
    if s.size * s.dtype.itemsize < HBM_MIN_BYTES:
        return s
    return pltpu.HBM(s.shape, s.dtype)


def _pcall(body, *, operands, out_shape, **kwargs):
    single = not isinstance(out_shape, (list, tuple))
    shapes = [_out_hbm(s) for s in ([out_shape] if single else out_shape)]
    return pl.pallas_call(body, out_shape=shapes[0] if single else shapes, **kwargs)(*[_in_hbm(a) for a in operands])


def _run_exchange(exchange, name):
    n_in, n_out = len(exchange.inputs), len(exchange.out_shapes)

    def body(*refs):
        ins, outs, sem_refs = refs[:n_in], refs[n_in:n_in + n_out], refs[n_in + n_out:]
        exchange.start(ins, outs, sem_refs)
        exchange.finish(ins, outs, sem_refs)

    any_spec = pl.BlockSpec(memory_space=pl.ANY)
    return pl.pallas_call(
        body, name=name, out_shape=[_out_hbm(s) for s in exchange.out_shapes],
        in_specs=[any_spec] * n_in, out_specs=[any_spec] * n_out,
        scratch_shapes=exchange.sems, input_output_aliases=exchange.aliases,
    )(*[_in_hbm(a) for a in exchange.inputs])


def _pallas(body, *, name, operands, in_specs, out_shape, out_specs, params, scratch_shapes=(), grid=(), prefetch=(),
            exchange=None, first=None, last=None):
    n_pre, n_in, n_out, n_scr = len(prefetch), len(in_specs), len(out_specs), len(scratch_shapes)
    ex_in = exchange.inputs if exchange else []
    ex_out = exchange.out_shapes if exchange else []
    ex_sems = exchange.sems if exchange else []

    def full_body(*refs):
        pre, rest = refs[:n_pre], refs[n_pre:]
        ins, rest = rest[:n_in], rest[n_in:]
        xin, rest = rest[:len(ex_in)], rest[len(ex_in):]
        outs, rest = rest[:n_out], rest[n_out:]
        xout, rest = rest[:len(ex_out)], rest[len(ex_out):]
        scr, sem_refs = rest[:n_scr], rest[n_scr:]
        if exchange:
            @pl.when(first(*pre))
            def _():
                exchange.start(xin, xout, sem_refs)

        body(*pre, *ins, *outs, *scr)
        if exchange:
            @pl.when(last(*pre))
            def _():
                exchange.finish(xin, xout, sem_refs)

    any_spec = pl.BlockSpec(memory_space=pl.ANY)
    aliases = {n_pre + n_in + i: n_out + o for i, o in exchange.aliases.items()} if exchange else {}
    operands = [_in_hbm(a) for a in operands]
    results = pl.pallas_call(
        full_body, name=name, out_shape=[_out_hbm(s) for s in list(out_shape) + ex_out],
        grid_spec=pltpu.PrefetchScalarGridSpec(
            num_scalar_prefetch=n_pre, grid=grid, in_specs=list(in_specs) + [any_spec] * len(ex_in),
            out_specs=list(out_specs) + [any_spec] * len(ex_out), scratch_shapes=list(scratch_shapes) + ex_sems),
        input_output_aliases=aliases, compiler_params=params,
    )(*prefetch, *operands, *[_in_hbm(a) for a in ex_in])
    return results[:n_out], results[n_out:]


def _remote(src, dst, sems, idx, to):
    send_sems, recv_sems = sems
    return pltpu.make_async_remote_copy(src_ref=src, dst_ref=dst, send_sem=send_sems.at[idx], recv_sem=recv_sems.at[idx],
                                        device_id=to, device_id_type=MESH_IDS)


def _sem_pairs(*shape):
    return [pltpu.SemaphoreType.DMA(shape), pltpu.SemaphoreType.DMA(shape)]


def _same_shapes(arrays):
    return [jax.ShapeDtypeStruct(a.shape, a.dtype) for a in arrays]


def _gather_over_ici(slots):
    n = len(slots)

    def copies(ins, outs, sems):
        x, y, c = _mesh_pos()
        k = 2 * x + y
        out = []
        for j, (chip, kj) in enumerate(_other_chips(x, y)):
            for i in range(n):
                to = (*chip, c)
                out.append((_remote(ins[i].at[k, c], outs[i].at[k, c], sems, (j, i), to),
                            _remote(ins[i].at[k, c], outs[i].at[kj, c], sems, (j, i), to)))
        return out

    return _from_copies(slots, _same_shapes(slots), {i: i for i in range(n)}, _sem_pairs(3, n), copies)


def _gather_over_ici_in_two_steps(slots):
    n = len(slots)

    def places(x, y):
        return 2 * x + y, 2 * (1 - x) + y, 2 * x + (1 - y), 2 * (1 - x) + (1 - y)

    def rows(i):
        h = slots[i].shape[2]
        cut = h // 2 // 16 * 16
        return pl.ds(0, cut), pl.ds(cut, h - cut)

    def start(ins, outs, sems):
        x, y, c = _mesh_pos()
        k = places(x, y)[0]
        for i in range(n):
            _remote(ins[i].at[k, c], outs[i].at[k, c], sems[0:2], (0, i), (1 - x, y, c)).start()
            _remote(ins[i].at[k, c], outs[i].at[k, c], sems[0:2], (1, i), (x, 1 - y, c)).start()

    def finish(ins, outs, sems):
        x, y, c = _mesh_pos()
        k, kx, ky, kd = places(x, y)
        to_x, to_y = (1 - x, y, c), (x, 1 - y, c)
        passed = []
        for i in range(n):
            upper, lower = rows(i)
            _remote(ins[i].at[k, c], outs[i].at[kx, c], sems[0:2], (0, i), to_x).wait_recv()
            passed.append(_remote(outs[i].at[kx, c, upper], outs[i].at[kx, c, upper], sems[2:4], (0, i), to_y))
            passed[-1].start()
            _remote(ins[i].at[k, c], outs[i].at[ky, c], sems[0:2], (1, i), to_y).wait_recv()
            passed.append(_remote(outs[i].at[ky, c, lower], outs[i].at[ky, c, lower], sems[2:4], (1, i), to_x))
            passed[-1].start()
        for i in range(n):
            upper, lower = rows(i)
            _remote(outs[i].at[kd, c, upper], outs[i].at[kd, c, upper], sems[2:4], (0, i), to_y).wait_recv()
            _remote(outs[i].at[kd, c, lower], outs[i].at[kd, c, lower], sems[2:4], (1, i), to_x).wait_recv()
        for i in range(n):
            _remote(ins[i].at[k, c], outs[i].at[k, c], sems[0:2], (0, i), to_x).wait_send()
            _remote(ins[i].at[k, c], outs[i].at[k, c], sems[0:2], (1, i), to_y).wait_send()
        for cp in passed:
            cp.wait_send()

    return _Exchange(slots, _same_shapes(slots), {i: i for i in range(n)}, _sem_pairs(2, n) + _sem_pairs(2, n),
                     start, finish)


def _gather_over_d2d(slots):
    n = len(slots)

    def copies(ins, outs, sems):
        x, y, c = _mesh_pos()
        sibling = (x, y, 1 - c)
        out = []
        for j, (_, kj) in enumerate(_other_chips(x, y)):
            for i in range(n):
                out.append((_remote(ins[i].at[kj, c], outs[i].at[kj, c], sems, (j, i), sibling),
                            _remote(ins[i].at[kj, c], outs[i].at[kj, 1 - c], sems, (j, i), sibling)))
        return out

    return _from_copies(slots, _same_shapes(slots), {i: i for i in range(n)}, _sem_pairs(3, n), copies)


def _gather_all(slots8):
    def copies(ins, outs, sems):
        x, y, c = _mesh_pos()
        me = 4 * x + 2 * y + c
        out = []
        for r in range(1, N_DEV):
            px = 1 - x if r & 4 else x
            py = 1 - y if r & 2 else y
            pc = 1 - c if r & 1 else c
            to = (px, py, pc)
            out.append((_remote(ins[0].at[me], outs[0].at[me], sems, r - 1, to),
                        _remote(ins[0].at[me], outs[0].at[4 * px + 2 * py + pc], sems, r - 1, to)))
        return out

    return _from_copies([slots8], _same_shapes([slots8]), {0: 0}, _sem_pairs(N_DEV - 1), copies)


def _merge(first, second):
    n_in, n_out, n_sem = len(first.inputs), len(first.out_shapes), len(first.sems)

    def start(ins, outs, sems):
        first.start(ins[:n_in], outs[:n_out], sems[:n_sem])
        second.start(ins[n_in:], outs[n_out:], sems[n_sem:])

    def finish(ins, outs, sems):
        first.finish(ins[:n_in], outs[:n_out], sems[:n_sem])
        second.finish(ins[n_in:], outs[n_out:], sems[n_sem:])

    aliases = dict(first.aliases)
    aliases.update({n_in + i: n_out + o for i, o in second.aliases.items()})
    return _Exchange(first.inputs + second.inputs, first.out_shapes + second.out_shapes, aliases,
                     first.sems + second.sems, start, finish)


def _pair_exchange(grads):
    n = len(grads)

    def copies(ins, outs, sems):
        x, y, c = _mesh_pos()
        cps = [_remote(ins[i].at[:, 1 - c], outs[i], sems, i, (x, y, 1 - c)) for i in range(n)]
        return [(cp, cp) for cp in cps]

    shapes = [jax.ShapeDtypeStruct((N_CHIPS,) + g.shape[2:], g.dtype) for g in grads]
    return _from_copies(grads, shapes, {}, _sem_pairs(n), copies)


def _chip_exchange(partials):
    n = len(partials)

    def copies(ins, outs, sems):
        x, y, c = _mesh_pos()
        cps = [_remote(ins[i].at[kj], outs[i].at[j], sems, (j, i), (*chip, c))
               for j, (chip, kj) in enumerate(_other_chips(x, y)) for i in range(n)]
        return [(cp, cp) for cp in cps]

    shapes = [jax.ShapeDtypeStruct((3,) + p.shape[1:], p.dtype) for p in partials]
    return _from_copies(partials, shapes, {}, _sem_pairs(3, n), copies)


def _pair_send(halves):
    n = len(halves)

    def copies(ins, outs, sems):
        x, y, c = _mesh_pos()
        cps = [_remote(ins[i], outs[i], sems, i, (x, y, 1 - c)) for i in range(n)]
        return [(cp, cp) for cp in cps]

    return _from_copies(halves, _same_shapes(halves), {}, _sem_pairs(n), copies)


def _reduce_in_vmem(grads, half_rows, name):
    n = len(grads)

    def body(*refs):
        g, mine, theirs = refs[:n], refs[n:2 * n], refs[2 * n:3 * n]
        landed_pair, partial, landed_chips = refs[3 * n:4 * n], refs[4 * n:5 * n], refs[5 * n:6 * n]
        relay = refs[6 * n:7 * n]
        sems = refs[7 * n:]
        x, y, c = _mesh_pos()
        k = 2 * x + y
        sibling = (x, y, 1 - c)

        def half(i, chip_idx, which):
            return pl.ds(pl.multiple_of((2 * chip_idx + which) * half_rows[i], 8), half_rows[i])

        def run(copies):
            for cp in copies:
                cp.start()
            for cp in copies:
                cp.wait_recv()
                cp.wait_send()

        run([_remote(g[i].at[half(i, kk, 1 - c)], landed_pair[i].at[kk], sems[0:2], (kk, i), sibling)
             for kk in range(N_CHIPS) for i in range(n)])
        for i in range(n):
            for kk in range(N_CHIPS):
                partial[i][kk] = (g[i][half(i, kk, c), :] + landed_pair[i][kk]).astype(BF16)
        (chip_x, kx), (chip_y, ky), (_, kd) = _other_chips(x, y)
        to_x, to_y = (*chip_x, c), (*chip_y, c)

        def cut(i):
            rows = half_rows[i] // 2 // 16 * 16
            return pl.ds(0, rows), pl.ds(rows, half_rows[i] - rows)

        first, second = [], []
        for i in range(n):
            upper, lower = cut(i)
            first += [_remote(partial[i].at[kx], landed_chips[i].at[0], sems[2:4], (0, i), to_x),
                      _remote(partial[i].at[ky], landed_chips[i].at[1], sems[2:4], (1, i), to_y),
                      _remote(partial[i].at[kd, upper], relay[i].at[upper], sems[2:4], (2, i), to_x),
                      _remote(partial[i].at[kd, lower], relay[i].at[lower], sems[2:4], (3, i), to_y)]
        for cp in first:
            cp.start()
        for i in range(n):
            upper, lower = cut(i)
            first[4 * i + 2].wait_recv()
            second.append(_remote(relay[i].at[upper], landed_chips[i].at[2, upper], sems[2:4], (4, i), to_y))
            second[-1].start()
            first[4 * i + 3].wait_recv()
            second.append(_remote(relay[i].at[lower], landed_chips[i].at[2, lower], sems[2:4], (5, i), to_x))
            second[-1].start()
        for i in range(n):
            first[4 * i].wait_recv()
            first[4 * i + 1].wait_recv()
        for cp in second:
            cp.wait_recv()
        for cp in first + second:
            cp.wait_send()
        for i in range(n):
            own = g[i][half(i, k, c), :] + landed_pair[i][k]
            mine[i][...] = ((own + landed_chips[i][0].astype(F32)) + landed_chips[i][1].astype(F32)) \
                + landed_chips[i][2].astype(F32)
        run([_remote(mine[i], theirs[i], sems[4:6], i, sibling) for i in range(n)])

    shapes = [(h, gr.shape[1]) for gr, h in zip(grads, half_rows)]
    halves = [jax.ShapeDtypeStruct(s, F32) for s in shapes]
    vmem = pl.BlockSpec(memory_space=pltpu.VMEM)
    scratch = ([pltpu.VMEM((N_CHIPS,) + s, F32) for s in shapes]
               + [pltpu.VMEM((N_CHIPS,) + s, BF16) for s in shapes]
               + [pltpu.VMEM((3,) + s, BF16) for s in shapes]
               + [pltpu.VMEM(s, BF16) for s in shapes]
               + _sem_pairs(N_CHIPS, n) + _sem_pairs(6, n) + _sem_pairs(n))
    out = pl.pallas_call(
        body, name=name, out_shape=halves + halves, in_specs=[vmem] * n, out_specs=[vmem] * (2 * n),
        scratch_shapes=scratch, compiler_params=_params(48),
    )(*grads)
    return out[:n], out[n:]


def _row_tile(rows):
    for t in (256, 128, 64):
        if rows % t == 0:
            return t
    return rows


def _add_pair(core, chip, grad, landed):
    _, h, cols = landed.shape
    tr = _row_tile(h)

    def body(core_ref, chip_ref, g_ref, l_ref, own_ref, ob_ref):
        s = g_ref[...] + l_ref[...]
        ob_ref[...] = s.astype(BF16)

        @pl.when(pl.program_id(1) == chip_ref[0])
        def _():
            own_ref[...] = s

    return _pcall(
        body, name="grad_add_pair",
        out_shape=[jax.ShapeDtypeStruct((h, cols), F32), jax.ShapeDtypeStruct(landed.shape, BF16)],
        grid_spec=pltpu.PrefetchScalarGridSpec(
            num_scalar_prefetch=2, grid=(h // tr, N_CHIPS),
            in_specs=[pl.BlockSpec((None, None, tr, cols), lambda t, k, core_ref, chip_ref: (k, core_ref[0], t, 0)),
                      pl.BlockSpec((None, tr, cols), lambda t, k, core_ref, chip_ref: (k, t, 0))],
            out_specs=[pl.BlockSpec((tr, cols), lambda t, k, core_ref, chip_ref: (t, 0)),
                       pl.BlockSpec((None, tr, cols), lambda t, k, core_ref, chip_ref: (k, t, 0))]),
        compiler_params=_params(32, ("arbitrary", "arbitrary")),
        operands=(core, chip, grad, landed))


def _add_chips(own, landed):
    h, cols = own.shape
    tr = _row_tile(h)

    def body(p_ref, l_ref, o_ref):
        o_ref[...] = ((p_ref[...] + l_ref[0].astype(F32)) + l_ref[1].astype(F32)) + l_ref[2].astype(F32)

    return _pcall(
        body, name="grad_add_chips", grid=(h // tr,),
        out_shape=jax.ShapeDtypeStruct((h, cols), F32),
        in_specs=[pl.BlockSpec((tr, cols), lambda t: (t, 0)), pl.BlockSpec((3, tr, cols), lambda t: (0, t, 0))],
        out_specs=pl.BlockSpec((tr, cols), lambda t: (t, 0)),
        compiler_params=_params(32, ("arbitrary",)),
        operands=(own, landed))


def _ada_project(c_all, w_ada, b_shard):
    n = w_ada.shape[1]
    tn = 512

    def body(c_ref, w_ref, b_ref, mod_ref, cond_ref):
        cv = c_ref[...]
        cond = cv * _sigmoid(cv)
        mod_ref[...] = _dot(cond.astype(BF16), w_ref[...].astype(BF16)) + b_ref[...]
        cond_ref[0:N_DEV, :] = cond
        cond_ref[N_DEV:2 * N_DEV, :] = jnp.zeros_like(cond)

    return _pcall(
        body, name="ada_project", grid=(n // tn,),
        out_shape=[jax.ShapeDtypeStruct((N_DEV, n), F32), jax.ShapeDtypeStruct((2 * N_DEV, D_MODEL), F32)],
        in_specs=[pl.BlockSpec((N_DEV, D_MODEL), lambda j: (0, 0)), pl.BlockSpec((D_MODEL, tn), lambda j: (0, j)),
                  pl.BlockSpec((1, tn), lambda j: (0, j))],
        out_specs=[pl.BlockSpec((N_DEV, tn), lambda j: (0, j)), pl.BlockSpec((2 * N_DEV, D_MODEL), lambda j: (0, 0))],
        compiler_params=_params(32, ("arbitrary",)),
        operands=(c_all, w_ada, b_shard))


def _in_project(x, pos, sc_a, sh_a, w_in, q_norm_w, kv_norm_w, w_q, w_kv, invf, exchange=None):
    t_len = x.shape[0]
    tm = min(512, t_len)

    def body(x_ref, pos_ref, sc_ref, sh_ref, win_ref, qn_ref, kvn_ref, wq_ref, wkv_ref, invf_ref,
             u_ref, zhg_ref, cq_ref, ckv_ref, q_ref, k_ref, kt_ref, v_ref, vt_ref):
        u = (x_ref[...] * (1.0 + sc_ref[...]) + sh_ref[...]).astype(BF16)
        u_ref[...] = u
        z = _dot_nt(u, win_ref[...])
        zhg_ref[...] = z[:, :HG_COLS]
        cq = z[:, HG_COLS:HG_COLS + Q_RANK]
        ckv = z[:, HG_COLS + Q_RANK:HG_COLS + Q_RANK + KV_RANK]
        cq_ref[...] = cq
        ckv_ref[...] = ckv
        cos_t, sin_t = _rope_tables(pos_ref[...], invf_ref[...])
        k_pe = _rope(z[:, HG_COLS + Q_RANK + KV_RANK:], cos_t, sin_t)
        k_pe_t = jnp.transpose(k_pe).astype(BF16)
        cqn = (cq * lax.rsqrt(_rowmean(cq * cq) + RMS_EPS) * qn_ref[...]).astype(BF16)
        ckvn = (ckv * lax.rsqrt(_rowmean(ckv * ckv) + RMS_EPS) * kvn_ref[...]).astype(BF16)
        q_up = [_dot(cqn, wq_ref[h]) for h in range(N_HEADS)]
        kv_up = [_dot(ckvn, wkv_ref[h]) for h in range(N_HEADS)]
        for h in range(N_HEADS):
            qh, kvh = q_up[h], kv_up[h]
            q_ref[h, :, 0:HEAD_DIM] = qh[:, :HEAD_DIM].astype(BF16)
            q_ref[h, :, HEAD_DIM:QK_DIM] = _rope(qh[:, HEAD_DIM:], cos_t, sin_t).astype(BF16)
            k_ref[h, :, 0:HEAD_DIM] = kvh[:, :HEAD_DIM].astype(BF16)
            k_ref[h, :, HEAD_DIM:QK_DIM] = k_pe.astype(BF16)
            kt_ref[h, 0:HEAD_DIM, :] = jnp.transpose(kvh[:, :HEAD_DIM]).astype(BF16)
            kt_ref[h, HEAD_DIM:QK_DIM, :] = k_pe_t
            v_ref[h] = kvh[:, HEAD_DIM:].astype(BF16)
            vt_ref[h] = jnp.transpose(kvh[:, HEAD_DIM:]).astype(BF16)

    row = lambda i: (i, 0)
    fixed2 = lambda i: (0, 0)
    fixed3 = lambda i: (0, 0, 0)
    heads = lambda i: (0, i, 0)
    n_tiles = t_len // tm
    return _pallas(
        body, name="in_project", grid=(n_tiles,),
        operands=(x, pos, sc_a, sh_a, w_in, q_norm_w, kv_norm_w, w_q, w_kv, invf),
        out_shape=[jax.ShapeDtypeStruct((t_len, D_MODEL), BF16), jax.ShapeDtypeStruct((t_len, HG_COLS), F32),
                   jax.ShapeDtypeStruct((t_len, Q_RANK), F32), jax.ShapeDtypeStruct((t_len, KV_RANK), F32),
                   jax.ShapeDtypeStruct((N_HEADS, t_len, QK_DIM), BF16),
                   jax.ShapeDtypeStruct((N_HEADS, t_len, QK_DIM), BF16),
                   jax.ShapeDtypeStruct((N_HEADS, QK_DIM, t_len), BF16),
                   jax.ShapeDtypeStruct((N_HEADS, t_len, HEAD_DIM), BF16),
                   jax.ShapeDtypeStruct((N_HEADS, HEAD_DIM, t_len), BF16)],
        in_specs=[pl.BlockSpec((tm, D_MODEL), row), pl.BlockSpec((tm, 1), row),
                  pl.BlockSpec((1, D_MODEL), fixed2), pl.BlockSpec((1, D_MODEL), fixed2),
                  pl.BlockSpec((IN_COLS_PAD, D_MODEL), fixed2),
                  pl.BlockSpec((1, Q_RANK), fixed2), pl.BlockSpec((1, KV_RANK), fixed2),
                  pl.BlockSpec((N_HEADS, Q_RANK, QK_DIM), fixed3), pl.BlockSpec((N_HEADS, KV_RANK, 2 * HEAD_DIM), fixed3),
                  pl.BlockSpec((1, 128), fixed2)],
        out_specs=[pl.BlockSpec((tm, D_MODEL), row), pl.BlockSpec((tm, HG_COLS), row),
                   pl.BlockSpec((tm, Q_RANK), row), pl.BlockSpec((tm, KV_RANK), row),
                   pl.BlockSpec((N_HEADS, tm, QK_DIM), heads), pl.BlockSpec((N_HEADS, tm, QK_DIM), heads),
                   pl.BlockSpec((N_HEADS, QK_DIM, tm), lambda i: (0, 0, i)),
                   pl.BlockSpec((N_HEADS, tm, HEAD_DIM), heads),
                   pl.BlockSpec((N_HEADS, HEAD_DIM, tm), lambda i: (0, 0, i))],
        params=_params(48, ("arbitrary",)), exchange=exchange,
        first=lambda: pl.program_id(0) == 0, last=lambda: pl.program_id(0) == n_tiles - 1)


def _lower_bound(lb_raw):
    m = jnp.max(lb_raw, axis=0, keepdims=True)
    e = jnp.exp(lb_raw - m)
    return e[0:1] / jnp.sum(e, axis=0, keepdims=True)


def _tri(inclusive_lower):
    r = lax.broadcasted_iota(jnp.int32, (HG_CHUNK, HG_CHUNK), 0)
    c = lax.broadcasted_iota(jnp.int32, (HG_CHUNK, HG_CHUNK), 1)
    return (c <= r) if inclusive_lower else (c >= r)


def _chunk_rows(n):
    return slice(n * HG_CHUNK, (n + 1) * HG_CHUNK)


def _chunk_prefix_sums(v, inclusive_lower):
    tri = _tri(inclusive_lower).astype(BF16)
    hi = v.astype(BF16)
    rest = v - hi.astype(F32)
    mid = rest.astype(BF16)
    lo = (rest - mid.astype(F32)).astype(BF16)
    pieces = jnp.concatenate([hi, mid, lo], axis=1)
    out = []
    for n in range(v.shape[0] // HG_CHUNK):
        s = _dot(tri, pieces[_chunk_rows(n)])
        out.append((s[:, 0:HEAD_DIM] + s[:, HEAD_DIM:2 * HEAD_DIM]) + s[:, 2 * HEAD_DIM:])
    return jnp.concatenate(out, axis=0)


def _per_chunk(v, row):
    n = v.shape[0] // HG_CHUNK
    v3 = v.reshape(n, HG_CHUNK, HEAD_DIM)
    return jnp.broadcast_to(v3[:, row:row + 1, :], v3.shape).reshape(v.shape)


def _hg_block(q, f_logit, lb):
    sg = _sigmoid(f_logit)
    forget = lb + (1.0 - lb) * sg
    kk = 1.0 - forget
    b = _chunk_prefix_sums(jnp.log(forget), True)
    b_ref = _per_chunk(b, HG_CHUNK // 2 - 1)
    b_last = _per_chunk(b, HG_CHUNK - 1)
    e_i = jnp.exp(b - b_ref)
    e_ri = jnp.exp(b_ref - b)
    e_b = jnp.exp(b)
    e_l = jnp.exp(b_last - b)
    return dict(sg=sg, forget=forget, e_i=e_i, e_ri=e_ri, e_b=e_b, e_l=e_l, dec=jnp.exp(b_last),
                qi=q * e_i, ki=kk * e_ri, qe=q * e_b, kl=kk * e_l)


HG_STEP_HEADS = 4


def _head_cols(hh):
    return slice(hh * HEAD_DIM, (hh + 1) * HEAD_DIM)


def _hgrn_forward(zhg, lb_raw, norm_w, exchange=None):
    t_len = zhg.shape[0]
    tb = min(512, t_len)
    n_chunks = tb // HG_CHUNK
    hs = HG_STEP_HEADS

    def body(q_ref, f_ref, v_ref, g_ref, lb_ref, w_ref, opre_ref, o_ref, st_ref, state):
        @pl.when(pl.program_id(1) == 0)
        def _():
            state[...] = jnp.zeros_like(state)

        causal = _tri(True)
        heads = range(hs)
        blk, v, qi, ki, qe, kl = {}, {}, {}, {}, {}, {}
        for hh in heads:
            cols = _head_cols(hh)
            blk[hh] = _hg_block(q_ref[:, cols], f_ref[:, cols], _lower_bound(lb_ref[:, cols]))
            v[hh] = v_ref[:, cols].astype(BF16)
            qi[hh], ki[hh], qe[hh], kl[hh] = (blk[hh][name].astype(BF16) for name in ("qi", "ki", "qe", "kl"))
        st = {hh: state[hh] for hh in heads}
        parts = {hh: [] for hh in heads}
        for n in range(n_chunks):
            r = _chunk_rows(n)
            for hh in heads:
                a = jnp.where(causal, _dot_nt(qi[hh][r], ki[hh][r]), 0.0).astype(BF16)
                st_ref[hh, n] = st[hh]
                parts[hh].append(_dot(a, v[hh][r]) + _dot_nt(qe[hh][r], st[hh].astype(BF16)))
                st[hh] = st[hh] * blk[hh]["dec"][n * HG_CHUNK:n * HG_CHUNK + 1] + _dot_tn(v[hh][r], kl[hh][r])
        for hh in heads:
            cols = _head_cols(hh)
            state[hh] = st[hh]
            o = jnp.concatenate(parts[hh], axis=0)
            opre_ref[:, cols] = o
            g = g_ref[:, cols]
            gated = o * lax.rsqrt(_rowmean(o * o) + RMS_EPS) * w_ref[:, cols] * (g * _sigmoid(g))
            o_ref[:, cols] = gated.astype(BF16)

    groups = N_HEADS // hs
    wide = hs * HEAD_DIM
    col = lambda off: (lambda h, t: (t, off + h))
    nb = t_len // tb
    return _pallas(
        body, name="hgrn_forward", grid=(groups, nb), operands=(zhg, zhg, zhg, zhg, lb_raw, norm_w),
        out_shape=[jax.ShapeDtypeStruct((t_len, N_HEADS * HEAD_DIM), F32),
                   jax.ShapeDtypeStruct((t_len, N_HEADS * HEAD_DIM), BF16),
                   jax.ShapeDtypeStruct((N_HEADS, t_len // HG_CHUNK, HEAD_DIM, HEAD_DIM), F32)],
        in_specs=[pl.BlockSpec((tb, wide), col(0)), pl.BlockSpec((tb, wide), col(groups)),
                  pl.BlockSpec((tb, wide), col(2 * groups)), pl.BlockSpec((tb, wide), col(3 * groups)),
                  pl.BlockSpec((2, wide), lambda h, t: (0, h)), pl.BlockSpec((1, wide), lambda h, t: (0, h))],
        out_specs=[pl.BlockSpec((tb, wide), col(0)), pl.BlockSpec((tb, wide), col(0)),
                   pl.BlockSpec((hs, n_chunks, HEAD_DIM, HEAD_DIM), lambda h, t: (h, t, 0, 0))],
        scratch_shapes=[pltpu.VMEM((hs, HEAD_DIM, HEAD_DIM), F32)],
        params=_params(40, ("arbitrary", "arbitrary")), exchange=exchange,
        first=lambda: (pl.program_id(0) == 0) & (pl.program_id(1) == 0),
        last=lambda: (pl.program_id(0) == groups - 1) & (pl.program_id(1) == nb - 1))


def _hgrn_backward(zhg, lb_raw, norm_w, o_pre, d_cat, states, exchange=None):
    t_len = zhg.shape[0]
    tb = min(512, t_len)
    n_chunks = tb // HG_CHUNK
    nb = t_len // tb
    hs = HG_STEP_HEADS

    def body(q_ref, f_ref, v_ref, g_ref, lb_ref, w_ref, opre_ref, do_ref, st_ref,
             dq_ref, df_ref, dv_ref, dg_ref, sums_ref, gstate):
        @pl.when(pl.program_id(1) == 0)
        def _():
            gstate[...] = jnp.zeros_like(gstate)
            sums_ref[...] = jnp.zeros_like(sums_ref)

        heads = range(hs)
        causal = _tri(True)
        row_id = lax.broadcasted_iota(jnp.int32, (HG_CHUNK, HEAD_DIM), 0)
        lb, d_o, blk, v, qi, ki, qe, kl = ({} for _ in range(8))
        for hh in heads:
            cols = _head_cols(hh)
            lb[hh] = _lower_bound(lb_ref[:, cols])
            w = w_ref[:, cols]
            o = opre_ref[:, cols]
            g = g_ref[:, cols]
            d_out = do_ref[:, cols]
            r = lax.rsqrt(_rowmean(o * o) + RMS_EPS)
            sg_g = _sigmoid(g)
            dg_ref[:, cols] = (d_out * (o * r * w) * (sg_g * (1.0 + g * (1.0 - sg_g)))).astype(BF16)
            d_on = d_out * (g * sg_g)
            sums_ref[1:2, cols] += _colsum(d_on * o * r)
            dy = d_on * w
            d_o[hh] = (r * dy - o * (r * r * r) * _rowmean(dy * o)).astype(BF16)
            blk[hh] = _hg_block(q_ref[:, cols], f_ref[:, cols], lb[hh])
            v[hh] = v_ref[:, cols].astype(BF16)
            qi[hh], ki[hh], qe[hh], kl[hh] = (blk[hh][name].astype(BF16) for name in ("qi", "ki", "qe", "kl"))
        gt = {hh: gstate[hh] for hh in heads}
        d_v, d_qi, d_ki, d_qe, d_kl, d_dec = ({hh: [None] * n_chunks for hh in heads} for _ in range(6))
        for n in reversed(range(n_chunks)):
            rows = _chunk_rows(n)
            for hh in heads:
                st = st_ref[hh, n]
                a = jnp.where(causal, _dot_nt(qi[hh][rows], ki[hh][rows]), 0.0).astype(BF16)
                d_a = jnp.where(causal, _dot_nt(d_o[hh][rows], v[hh][rows]), 0.0).astype(BF16)
                gt_b = gt[hh].astype(BF16)
                d_v[hh][n] = _dot_tn(a, d_o[hh][rows]) + _dot_nt(kl[hh][rows], gt_b)
                d_qi[hh][n] = _dot(d_a, ki[hh][rows])
                d_ki[hh][n] = _dot_tn(d_a, qi[hh][rows])
                d_qe[hh][n] = _dot(d_o[hh][rows], st.astype(BF16))
                d_kl[hh][n] = _dot(v[hh][rows], gt_b)
                d_dec[hh][n] = jnp.where(row_id == HG_CHUNK - 1, _colsum(gt[hh] * st), 0.0)
                gt[hh] = gt[hh] * blk[hh]["dec"][n * HG_CHUNK:n * HG_CHUNK + 1] + _dot_tn(d_o[hh][rows], qe[hh][rows])
        for hh in heads:
            cols = _head_cols(hh)
            b = blk[hh]
            gstate[hh] = gt[hh]
            dqi, dki, dqe, dkl, ddec = (jnp.concatenate(p[hh], axis=0) for p in (d_qi, d_ki, d_qe, d_kl, d_dec))
            dv_ref[:, cols] = jnp.concatenate(d_v[hh], axis=0).astype(BF16)
            dq_ref[:, cols] = (dqi * b["e_i"] + dqe * b["e_b"]).astype(BF16)
            d_k = dki * b["e_ri"] + dkl * b["e_l"]
            t_qi = dqi * b["qi"]
            t_ki = dki * b["ki"]
            t_kl = dkl * b["kl"]
            at_ref, at_last = [], []
            for n in range(n_chunks):
                rows = _chunk_rows(n)
                at_ref.append(jnp.where(row_id == HG_CHUNK // 2 - 1, _colsum(t_ki[rows] - t_qi[rows]), 0.0))
                at_last.append(jnp.where(row_id == HG_CHUNK - 1, _colsum(t_kl[rows]), 0.0))
            d_b = (t_qi - t_ki + dqe * b["qe"] - t_kl + jnp.concatenate(at_ref, axis=0)
                   + jnp.concatenate(at_last, axis=0) + ddec * b["dec"])
            d_forget = _chunk_prefix_sums(d_b, False) / b["forget"] - d_k
            sg = b["sg"]
            df_ref[:, cols] = (d_forget * (1.0 - lb[hh]) * sg * (1.0 - sg)).astype(BF16)
            sums_ref[0:1, cols] += _colsum(d_forget * (1.0 - sg))

    groups = N_HEADS // hs
    wide = hs * HEAD_DIM
    col = lambda off: (lambda h, t: (nb - 1 - t, off + h))
    return _pallas(
        body, name="hgrn_backward", grid=(groups, nb),
        operands=(zhg, zhg, zhg, zhg, lb_raw, norm_w, o_pre, d_cat, states),
        out_shape=[jax.ShapeDtypeStruct((t_len, N_HEADS * HEAD_DIM), BF16)] * 4
        + [jax.ShapeDtypeStruct((8, N_HEADS * HEAD_DIM), F32)],
        in_specs=[pl.BlockSpec((tb, wide), col(0)), pl.BlockSpec((tb, wide), col(groups)),
                  pl.BlockSpec((tb, wide), col(2 * groups)), pl.BlockSpec((tb, wide), col(3 * groups)),
                  pl.BlockSpec((2, wide), lambda h, t: (0, h)), pl.BlockSpec((1, wide), lambda h, t: (0, h)),
                  pl.BlockSpec((tb, wide), col(0)), pl.BlockSpec((tb, wide), col(0)),
                  pl.BlockSpec((hs, n_chunks, HEAD_DIM, HEAD_DIM), lambda h, t: (h, nb - 1 - t, 0, 0))],
        out_specs=[pl.BlockSpec((tb, wide), col(0))] * 4 + [pl.BlockSpec((8, wide), lambda h, t: (0, h))],
        scratch_shapes=[pltpu.VMEM((hs, HEAD_DIM, HEAD_DIM), F32)],
        params=_params(40, ("arbitrary", "arbitrary")), exchange=exchange,
        first=lambda: (pl.program_id(0) == 0) & (pl.program_id(1) == 0),
        last=lambda: (pl.program_id(0) == groups - 1) & (pl.program_id(1) == nb - 1))


ATT_LOG2 = ATT_SCALE * 1.4426950408889634


def _triangle_steps(nq, q_major):
    if q_major:
        pairs = [(i, j) for i in range(nq) for j in range(i + 1)]
    else:
        pairs = [(i, j) for j in range(nq) for i in range(j, nq)]
    return jnp.array([p[0] for p in pairs], jnp.int32), jnp.array([p[1] for p in pairs], jnp.int32)


def _key_le_query(t):
    return lax.broadcasted_iota(jnp.int32, (t, t), 0) <= lax.broadcasted_iota(jnp.int32, (t, t), 1)


def _attention_forward(q, k, v_t, exchange=None):
    t_len = q.shape[1]
    tq = min(512, t_len)
    nq = t_len // tq
    qi_tab, ki_tab = _triangle_steps(nq, True)

    def body(qi_ref, ki_ref, q_ref, k_ref, vt_ref, o_ref, lse_ref, m_s, l_s, acc_s):
        step = pl.program_id(0)
        qi, ki = qi_ref[step], ki_ref[step]

        @pl.when(ki == 0)
        def _():
            m_s[...] = jnp.full_like(m_s, NEG_BIG)
            l_s[...] = jnp.zeros_like(l_s)
            acc_s[...] = jnp.zeros_like(acc_s)

        def accumulate(masked):
            s_all = [_dot_nt(k_ref[h], q_ref[h]) * ATT_LOG2 for h in range(N_HEADS)]
            for h in range(N_HEADS):
                s_t = s_all[h]
                if masked:
                    s_t = jnp.where(_key_le_query(tq), s_t, NEG_BIG)
                m_old = m_s[h]
                m_new = jnp.maximum(m_old, jnp.max(s_t, axis=0, keepdims=True))
                alpha = jnp.exp2(m_old - m_new)
                p_t = jnp.exp2(s_t - m_new)
                l_s[h] = alpha * l_s[h] + jnp.sum(p_t, axis=0, keepdims=True)
                acc_s[h] = alpha * acc_s[h] + _dot(vt_ref[h], p_t.astype(BF16))
                m_s[h] = m_new

        @pl.when(ki < qi)
        def _():
            accumulate(False)

        @pl.when(ki == qi)
        def _():
            accumulate(True)
            for h in range(N_HEADS):
                o_ref[:, h * HEAD_DIM:(h + 1) * HEAD_DIM] = jnp.transpose(acc_s[h] / l_s[h])
                lse_ref[h] = m_s[h] + jnp.log2(l_s[h])

    n_steps = qi_tab.shape[0]
    return _pallas(
        body, name="attention_forward", grid=(n_steps,), prefetch=(qi_tab, ki_tab), operands=(q, k, v_t),
        out_shape=[jax.ShapeDtypeStruct((t_len, N_HEADS * HEAD_DIM), F32),
                   jax.ShapeDtypeStruct((N_HEADS, 1, t_len), F32)],
        in_specs=[pl.BlockSpec((N_HEADS, tq, QK_DIM), lambda s, qt, kt: (0, qt[s], 0)),
                  pl.BlockSpec((N_HEADS, tq, QK_DIM), lambda s, qt, kt: (0, kt[s], 0)),
                  pl.BlockSpec((N_HEADS, HEAD_DIM, tq), lambda s, qt, kt: (0, 0, kt[s]))],
        out_specs=[pl.BlockSpec((tq, N_HEADS * HEAD_DIM), lambda s, qt, kt: (qt[s], 0)),
                   pl.BlockSpec((N_HEADS, 1, tq), lambda s, qt, kt: (0, 0, qt[s]))],
        scratch_shapes=[pltpu.VMEM((N_HEADS, 1, tq), F32), pltpu.VMEM((N_HEADS, 1, tq), F32),
                        pltpu.VMEM((N_HEADS, HEAD_DIM, tq), F32)],
        params=_params(48, ("arbitrary",)), exchange=exchange,
        first=lambda qt, kt: pl.program_id(0) == 0, last=lambda qt, kt: pl.program_id(0) == n_steps - 1)


BWD_HEADS = 4


def _attention_backward(q, k, k_t, v, d_cat, lse, delta, exchange=None):
    t_len = q.shape[1]
    tq = min(512, t_len)
    nq = t_len // tq
    hp = BWD_HEADS
    qi_tab, ki_tab = _triangle_steps(nq, False)

    def body(qi_ref, ki_ref, q_ref, k_ref, kt_ref, v_ref, do_ref, lse_ref, delta_ref, dqt_hbm, dk_ref, dv_ref,
             dqt_s, dk_s, dv_s):
        group, step = pl.program_id(0), pl.program_id(1)
        qi, ki = qi_ref[step], ki_ref[step]

        @pl.when(step == 0)
        def _():
            dqt_s[...] = jnp.zeros_like(dqt_s)

        @pl.when(qi == ki)
        def _():
            dk_s[...] = jnp.zeros_like(dk_s)
            dv_s[...] = jnp.zeros_like(dv_s)

        def accumulate(masked):
            for h in range(hp):
                do_b = do_ref[:, h * HEAD_DIM:(h + 1) * HEAD_DIM].astype(BF16)
                s_t = _dot_nt(k_ref[h], q_ref[h]) * ATT_LOG2
                if masked:
                    s_t = jnp.where(_key_le_query(tq), s_t, NEG_BIG)
                p_t = jnp.exp2(s_t - lse_ref[h])
                dp_t = _dot_nt(v_ref[h], do_b)
                ds_t = (p_t * (dp_t - delta_ref[h]) * ATT_SCALE).astype(BF16)
                dv_s[h] += _dot(p_t.astype(BF16), do_b)
                dk_s[h] += _dot(ds_t, q_ref[h])
                dqt_s[h, qi, 0:QK_REAL, :] += _dot(kt_ref[h, 0:QK_REAL, :], ds_t)

        @pl.when(ki < qi)
        def _():
            accumulate(False)

        @pl.when(ki == qi)
        def _():
            accumulate(True)
            for h in range(hp):
                pltpu.sync_copy(dqt_s.at[h, qi], dqt_hbm.at[group * hp + h, qi])

        @pl.when(qi == nq - 1)
        def _():
            dk_ref[...] = dk_s[...]
            dv_ref[...] = dv_s[...]

    wide = hp * HEAD_DIM
    n_groups, n_steps = N_HEADS // hp, qi_tab.shape[0]
    return _pallas(
        body, name="attention_backward", grid=(n_groups, n_steps), prefetch=(qi_tab, ki_tab),
        operands=(q, k, k_t, v, d_cat, lse, delta),
        out_shape=[jax.ShapeDtypeStruct((N_HEADS, nq, QK_DIM, tq), F32),
                   jax.ShapeDtypeStruct((N_HEADS, t_len, QK_DIM), F32),
                   jax.ShapeDtypeStruct((N_HEADS, t_len, HEAD_DIM), F32)],
        in_specs=[pl.BlockSpec((hp, tq, QK_DIM), lambda g, s, qt, kt: (g, qt[s], 0)),
                  pl.BlockSpec((hp, tq, QK_DIM), lambda g, s, qt, kt: (g, kt[s], 0)),
                  pl.BlockSpec((hp, QK_DIM, tq), lambda g, s, qt, kt: (g, 0, kt[s])),
                  pl.BlockSpec((hp, tq, HEAD_DIM), lambda g, s, qt, kt: (g, kt[s], 0)),
                  pl.BlockSpec((tq, wide), lambda g, s, qt, kt: (qt[s], n_groups + g)),
                  pl.BlockSpec((hp, 1, tq), lambda g, s, qt, kt: (g, 0, qt[s])),
                  pl.BlockSpec((hp, 1, tq), lambda g, s, qt, kt: (g, 0, qt[s]))],
        out_specs=[pl.BlockSpec(memory_space=pl.ANY),
                   pl.BlockSpec((hp, tq, QK_DIM), lambda g, s, qt, kt: (g, kt[s], 0)),
                   pl.BlockSpec((hp, tq, HEAD_DIM), lambda g, s, qt, kt: (g, kt[s], 0))],
        scratch_shapes=[pltpu.VMEM((hp, nq, QK_DIM, tq), F32), pltpu.VMEM((hp, tq, QK_DIM), F32),
                        pltpu.VMEM((hp, tq, HEAD_DIM), F32)],
        params=_params(58, ("arbitrary", "arbitrary")), exchange=exchange,
        first=lambda qt, kt: (pl.program_id(0) == 0) & (pl.program_id(1) == 0),
        last=lambda qt, kt: (pl.program_id(0) == n_groups - 1) & (pl.program_id(1) == n_steps - 1))


def _out_project(o_hg, o_mla, x, g_a, w_out, exchange=None):
    t_len = x.shape[0]
    tm = min(512, t_len)
    half = N_HEADS * HEAD_DIM

    def body(ohg_ref, omla_ref, x_ref, ga_ref, w_ref, cat_ref, mix_ref, xhat_ref, rstd_ref):
        a = ohg_ref[...].astype(BF16)
        b = omla_ref[...].astype(BF16)
        cat_ref[:, 0:half] = a
        cat_ref[:, half:2 * half] = b
        mix = _dot(a, w_ref[0:half, :]) + _dot(b, w_ref[half:2 * half, :])
        mix_ref[...] = mix
        r1 = DN_ALPHA * x_ref[...] + (1.0 + ga_ref[...]) * mix
        xc = r1 - _rowmean(r1)
        rstd = lax.rsqrt(_rowmean(xc * xc) + LN_EPS)
        xhat_ref[...] = xc * rstd
        rstd_ref[...] = rstd

    row = lambda i: (i, 0)
    fixed = lambda i: (0, 0)
    n_tiles = t_len // tm
    return _pallas(
        body, name="out_project", grid=(n_tiles,), operands=(o_hg, o_mla, x, g_a, w_out),
        out_shape=[jax.ShapeDtypeStruct((t_len, D_MODEL), BF16), jax.ShapeDtypeStruct((t_len, D_MODEL), F32),
                   jax.ShapeDtypeStruct((t_len, D_MODEL), F32), jax.ShapeDtypeStruct((t_len, 1), F32)],
        in_specs=[pl.BlockSpec((tm, half), row), pl.BlockSpec((tm, half), row), pl.BlockSpec((tm, D_MODEL), row),
                  pl.BlockSpec((1, D_MODEL), fixed), pl.BlockSpec((D_MODEL, D_MODEL), fixed)],
        out_specs=[pl.BlockSpec((tm, D_MODEL), row), pl.BlockSpec((tm, D_MODEL), row),
                   pl.BlockSpec((tm, D_MODEL), row), pl.BlockSpec((tm, 1), row)],
        params=_params(48, ("arbitrary",)), exchange=exchange,
        first=lambda: pl.program_id(0) == 0, last=lambda: pl.program_id(0) == n_tiles - 1)


V_LN1G, V_LN1B, V_SCM, V_SHM, V_GM, V_GA, V_LN2G, V_LN2B = range(8)
S_DLN2G, S_DLN2B, S_DGM, S_DSCM, S_DSHM, S_DLN1G, S_DLN1B, S_DGA, S_LOSS = range(9)


def _mlp_and_back(xhat1, rstd1, mix, target, o_mla, vecs, w1_top, w1_bottom, w2, w_out):
    t_len = xhat1.shape[0]
    tm = min(256, t_len)
    n_ff = w1_top.shape[0]
    ff = w1_top.shape[2]
    top_rows = w1_top.shape[1]

    def body(xhat_ref, rstd_ref, mix_ref, tgt_ref, omla_ref, vec_ref, w1_top_hbm, w1_bottom_hbm, w2_hbm, wout_hbm,
             act_ref, dhp_ref, um_ref, dh_ref, dmix_ref, dcat_ref, dr1_ref, sums_ref, delta_ref,
             w1_s, w2_s, wout_s, hp_s, load_sems):
        @pl.when(pl.program_id(0) == 0)
        def _():
            loads = [pltpu.make_async_copy(w1_top_hbm, w1_s.at[:, 0:top_rows], load_sems.at[0]),
                     pltpu.make_async_copy(w1_bottom_hbm, w1_s.at[:, top_rows:D_MODEL], load_sems.at[3]),
                     pltpu.make_async_copy(w2_hbm, w2_s, load_sems.at[1]),
                     pltpu.make_async_copy(wout_hbm, wout_s, load_sems.at[2])]
            for cp in loads:
                cp.start()
            sums_ref[...] = jnp.zeros_like(sums_ref)
            for cp in loads:
                cp.wait()

        vec = lambda r: vec_ref[r:r + 1, :]
        xhat = xhat_ref[...]
        x1 = xhat * vec(V_LN1G) + vec(V_LN1B)
        um = (x1 * (1.0 + vec(V_SCM)) + vec(V_SHM)).astype(BF16)
        um_ref[...] = um
        h = jnp.zeros((tm, D_MODEL), F32)
        for j in range(n_ff):
            hp = _dot(um, w1_s[j])
            hp_s[j] = hp
            act = jnp.square(jnp.maximum(hp, 0.0)).astype(BF16)
            act_ref[:, j * ff:(j + 1) * ff] = act
            h = h + _dot(act, w2_s[j])
        r2 = DN_ALPHA * x1 + (1.0 + vec(V_GM)) * h
        xc = r2 - _rowmean(r2)
        rstd2 = lax.rsqrt(_rowmean(xc * xc) + LN_EPS)
        xhat2 = xc * rstd2
        err = xhat2 * vec(V_LN2G) + vec(V_LN2B) - tgt_ref[...]
        loss = 0.5 * jnp.sum(_rowmean(err * err))
        dy = err * (1.0 / D_MODEL)
        dxh = dy * vec(V_LN2G)
        dr2 = rstd2 * (dxh - _rowmean(dxh) - xhat2 * _rowmean(dxh * xhat2))
        dh = ((1.0 + vec(V_GM)) * dr2).astype(BF16)
        dh_ref[...] = dh
        sums_ref[S_DLN2G:S_DLN2G + 1, :] += _colsum(dy * xhat2)
        sums_ref[S_DLN2B:S_DLN2B + 1, :] += _colsum(dy)
        sums_ref[S_DGM:S_DGM + 1, :] += _colsum(dr2 * h)
        sums_ref[S_LOSS:S_LOSS + 1, :] += jnp.full((1, D_MODEL), loss, F32)
        du = jnp.zeros((tm, D_MODEL), F32)
        for j in range(n_ff):
            dhp = (_dot_nt(dh, w2_s[j]) * (2.0 * jnp.maximum(hp_s[j], 0.0))).astype(BF16)
            dhp_ref[:, j * ff:(j + 1) * ff] = dhp
            du = du + _dot_nt(dhp, w1_s[j])
        sums_ref[S_DSCM:S_DSCM + 1, :] += _colsum(du * x1)
        sums_ref[S_DSHM:S_DSHM + 1, :] += _colsum(du)
        dx1 = DN_ALPHA * dr2 + du * (1.0 + vec(V_SCM))
        sums_ref[S_DLN1G:S_DLN1G + 1, :] += _colsum(dx1 * xhat)
        sums_ref[S_DLN1B:S_DLN1B + 1, :] += _colsum(dx1)
        dxh1 = dx1 * vec(V_LN1G)
        dr1 = rstd_ref[...] * (dxh1 - _rowmean(dxh1) - xhat * _rowmean(dxh1 * xhat))
        dr1_ref[...] = dr1
        sums_ref[S_DGA:S_DGA + 1, :] += _colsum(dr1 * mix_ref[...])
        dmix = ((1.0 + vec(V_GA)) * dr1).astype(BF16)
        dmix_ref[...] = dmix
        dcat = _dot_nt(dmix, wout_s[...])
        dcat_ref[...] = dcat
        half = N_HEADS * HEAD_DIM
        for hd in range(N_HEADS):
            prod = dcat[:, half + hd * HEAD_DIM:half + (hd + 1) * HEAD_DIM] * omla_ref[:, hd * HEAD_DIM:(hd + 1) * HEAD_DIM]
            sums = jnp.broadcast_to(jnp.sum(prod, axis=1, keepdims=True), (tm, HEAD_DIM))
            delta_ref[hd] = jnp.transpose(sums)[0:1]

    row = lambda i: (i, 0)
    fixed = lambda i: (0, 0)
    any_spec = pl.BlockSpec(memory_space=pl.ANY)
    return _pcall(
        body, name="mlp_and_back", grid=(t_len // tm,),
        out_shape=[jax.ShapeDtypeStruct((t_len, D_FF), BF16), jax.ShapeDtypeStruct((t_len, D_FF), BF16),
                   jax.ShapeDtypeStruct((t_len, D_MODEL), BF16), jax.ShapeDtypeStruct((t_len, D_MODEL), BF16),
                   jax.ShapeDtypeStruct((t_len, D_MODEL), BF16), jax.ShapeDtypeStruct((t_len, D_MODEL), F32),
                   jax.ShapeDtypeStruct((t_len, D_MODEL), F32), jax.ShapeDtypeStruct((16, D_MODEL), F32),
                   jax.ShapeDtypeStruct((N_HEADS, 1, t_len), F32)],
        in_specs=[pl.BlockSpec((tm, D_MODEL), row), pl.BlockSpec((tm, 1), row), pl.BlockSpec((tm, D_MODEL), row),
                  pl.BlockSpec((tm, D_MODEL), row), pl.BlockSpec((tm, N_HEADS * HEAD_DIM), row),
                  pl.BlockSpec((8, D_MODEL), fixed), any_spec, any_spec, any_spec, any_spec],
        out_specs=[pl.BlockSpec((tm, D_FF), row), pl.BlockSpec((tm, D_FF), row), pl.BlockSpec((tm, D_MODEL), row),
                   pl.BlockSpec((tm, D_MODEL), row), pl.BlockSpec((tm, D_MODEL), row), pl.BlockSpec((tm, D_MODEL), row),
                   pl.BlockSpec((tm, D_MODEL), row), pl.BlockSpec((16, D_MODEL), fixed),
                   pl.BlockSpec((N_HEADS, 1, tm), lambda i: (0, 0, i))],
        scratch_shapes=[pltpu.VMEM((n_ff, D_MODEL, ff), BF16), pltpu.VMEM(w2.shape, BF16), pltpu.VMEM(w_out.shape, BF16),
                        pltpu.VMEM((n_ff, tm, ff), F32), pltpu.SemaphoreType.DMA((4,))],
        compiler_params=_params(56, ("arbitrary",)),
        operands=(xhat1, rstd1, mix, target, o_mla, vecs, w1_top, w1_bottom, w2, w_out))


def _in_project_backward(dq, dk, dv, cq, ckv, pos, invf, q_norm_w, kv_norm_w, w_q, w_kv,
                         d_hq, d_hf, d_hi, d_hg, w_in, dr1, x, sc_a, exchange=None):
    t_len = x.shape[0]
    tm = min(512, t_len)
    per_q = dq.shape[3] // tm
    hgw = N_HEADS * HEAD_DIM

    def body(dq_ref, dk_ref, dv_ref, cq_ref, ckv_ref, pos_ref, invf_ref, qn_ref, kvn_ref, wq_ref, wkv_ref,
             dhq_ref, dhf_ref, dhi_ref, dhg_ref, win_ref, dr1_ref, x_ref, sc_ref,
             dz_ref, gx_ref, sums_ref, dwq_ref, dwkv_ref):
        @pl.when(pl.program_id(0) == 0)
        def _():
            sums_ref[...] = jnp.zeros_like(sums_ref)
            dwq_ref[...] = jnp.zeros_like(dwq_ref)
            dwkv_ref[...] = jnp.zeros_like(dwkv_ref)

        cos_t, sin_t = _rope_tables(pos_ref[...], invf_ref[...])
        cq = cq_ref[...]
        ckv = ckv_ref[...]
        rq = lax.rsqrt(_rowmean(cq * cq) + RMS_EPS)
        rkv = lax.rsqrt(_rowmean(ckv * ckv) + RMS_EPS)
        cqn = (cq * rq * qn_ref[...]).astype(BF16)
        ckvn = (ckv * rkv * kvn_ref[...]).astype(BF16)
        d_cqn = jnp.zeros((tm, Q_RANK), F32)
        d_ckvn = jnp.zeros((tm, KV_RANK), F32)
        d_kpe = jnp.zeros((tm, 128), F32)
        for h in range(N_HEADS):
            dqh = jnp.transpose(dq_ref[h])
            dq_full = jnp.concatenate(
                [dqh[:, :HEAD_DIM].astype(BF16), _unrope(dqh[:, HEAD_DIM:], cos_t, sin_t).astype(BF16)], axis=1)
            d_cqn = d_cqn + _dot_nt(dq_full, wq_ref[h])
            dwq_ref[h] += _dot_tn(cqn, dq_full)
            dkh = dk_ref[h]
            d_kpe = d_kpe + dkh[:, HEAD_DIM:]
            dkv_up = jnp.concatenate([dkh[:, :HEAD_DIM].astype(BF16), dv_ref[h].astype(BF16)], axis=1)
            d_ckvn = d_ckvn + _dot_nt(dkv_up, wkv_ref[h])
            dwkv_ref[h] += _dot_tn(ckvn, dkv_up)
        dyq = d_cqn * qn_ref[...]
        dykv = d_ckvn * kvn_ref[...]
        sums_ref[2:3, 0:Q_RANK] += _colsum(d_cqn * cq * rq)
        sums_ref[3:4, 0:KV_RANK] += _colsum(d_ckvn * ckv * rkv)
        dz_ref[:, 0:hgw] = dhq_ref[...]
        dz_ref[:, hgw:2 * hgw] = dhf_ref[...]
        dz_ref[:, 2 * hgw:3 * hgw] = dhi_ref[...]
        dz_ref[:, 3 * hgw:4 * hgw] = dhg_ref[...]
        dz_ref[:, HG_COLS:HG_COLS + Q_RANK] = (rq * dyq - cq * (rq * rq * rq) * _rowmean(dyq * cq)).astype(BF16)
        dz_ref[:, HG_COLS + Q_RANK:HG_COLS + Q_RANK + KV_RANK] = (
            rkv * dykv - ckv * (rkv * rkv * rkv) * _rowmean(dykv * ckv)).astype(BF16)
        dz_ref[:, HG_COLS + Q_RANK + KV_RANK:] = _unrope(d_kpe, cos_t, sin_t).astype(BF16)
        du = _dot(dz_ref[...], win_ref[...])
        xv = x_ref[...]
        gx_ref[...] = DN_ALPHA * dr1_ref[...] + (1.0 + sc_ref[...]) * du
        sums_ref[0:1, :] += _colsum(du * xv)
        sums_ref[1:2, :] += _colsum(du)

    row = lambda i: (i, 0)
    fixed2 = lambda i: (0, 0)
    fixed3 = lambda i: (0, 0, 0)
    heads = lambda i: (0, i, 0)
    n_tiles = t_len // tm
    return _pallas(
        body, name="in_project_backward", grid=(n_tiles,),
        operands=(dq, dk, dv, cq, ckv, pos, invf, q_norm_w, kv_norm_w, w_q, w_kv, d_hq, d_hf, d_hi, d_hg, w_in, dr1, x,
                  sc_a),
        out_shape=[jax.ShapeDtypeStruct((t_len, IN_COLS_PAD), BF16), jax.ShapeDtypeStruct((t_len, D_MODEL), F32),
                   jax.ShapeDtypeStruct((8, D_MODEL), F32), jax.ShapeDtypeStruct((N_HEADS, Q_RANK, QK_DIM), F32),
                   jax.ShapeDtypeStruct((N_HEADS, KV_RANK, 2 * HEAD_DIM), F32)],
        in_specs=[pl.BlockSpec((N_HEADS, None, QK_DIM, tm), lambda i: (0, i // per_q, 0, i % per_q)),
                  pl.BlockSpec((N_HEADS, tm, QK_DIM), heads),
                  pl.BlockSpec((N_HEADS, tm, HEAD_DIM), heads), pl.BlockSpec((tm, Q_RANK), row),
                  pl.BlockSpec((tm, KV_RANK), row), pl.BlockSpec((tm, 1), row), pl.BlockSpec((1, 128), fixed2),
                  pl.BlockSpec((1, Q_RANK), fixed2), pl.BlockSpec((1, KV_RANK), fixed2),
                  pl.BlockSpec((N_HEADS, Q_RANK, QK_DIM), fixed3), pl.BlockSpec((N_HEADS, KV_RANK, 2 * HEAD_DIM), fixed3),
                  pl.BlockSpec((tm, hgw), row), pl.BlockSpec((tm, hgw), row), pl.BlockSpec((tm, hgw), row),
                  pl.BlockSpec((tm, hgw), row), pl.BlockSpec((IN_COLS_PAD, D_MODEL), fixed2),
                  pl.BlockSpec((tm, D_MODEL), row), pl.BlockSpec((tm, D_MODEL), row), pl.BlockSpec((1, D_MODEL), fixed2)],
        out_specs=[pl.BlockSpec((tm, IN_COLS_PAD), row), pl.BlockSpec((tm, D_MODEL), row),
                   pl.BlockSpec((8, D_MODEL), fixed2), pl.BlockSpec((N_HEADS, Q_RANK, QK_DIM), fixed3),
                   pl.BlockSpec((N_HEADS, KV_RANK, 2 * HEAD_DIM), fixed3)],
        params=_params(48, ("arbitrary",)), exchange=exchange,
        first=lambda: pl.program_id(0) == 0, last=lambda: pl.program_id(0) == n_tiles - 1)


def _weight_grad(a, b, name, n_blocks, bn, a_blocked=False, b_blocked=True, exchange=None, token_tile=512):
    t_len = a.shape[0]
    m = a.shape[1] // n_blocks if a_blocked else a.shape[1]
    bt = min(token_tile, t_len)

    def body(a_ref, b_ref, o_ref):
        @pl.when(pl.program_id(1) == 0)
        def _():
            o_ref[...] = jnp.zeros_like(o_ref)

        o_ref[...] += _dot_tn(a_ref[...].astype(BF16), b_ref[...].astype(BF16))

    a_spec = pl.BlockSpec((bt, m), (lambda n, t: (t, n)) if a_blocked else (lambda n, t: (t, 0)))
    b_spec = pl.BlockSpec((bt, bn), (lambda n, t: (t, n)) if b_blocked else (lambda n, t: (t, 0)))
    nt = t_len // bt
    (out,), landed = _pallas(
        body, name=name, grid=(n_blocks, nt), operands=(a, b),
        out_shape=[jax.ShapeDtypeStruct((n_blocks, m, bn), F32)],
        in_specs=[a_spec, b_spec],
        out_specs=[pl.BlockSpec((None, m, bn), lambda n, t: (n, 0, 0))],
        params=_params(56, ("arbitrary", "arbitrary")), exchange=exchange,
        first=lambda: (pl.program_id(0) == 0) & (pl.program_id(1) == 0),
        last=lambda: (pl.program_id(0) == n_blocks - 1) & (pl.program_id(1) == nt - 1))
    return (out, landed) if exchange else out


SMALL_PLACE = {"ln1_g": (6, 0), "ln1_b": (7, 0), "ln2_g": (8, 0), "ln2_b": (9, 0), "hg_norm_w": (10, 512),
               "mla_q_norm_w": (11, 0), "mla_kv_norm_w": (11, Q_RANK)}
SMALL_LB_ROW, SMALL_LOSS_ROW = 10, 12


def _small_params_step(gathered, params):
    names = list(params)

    def body(g_ref, *refs):
        ins, outs = refs[:3 * len(names)], refs[3 * len(names):]
        loss_ref, outs = outs[0], outs[1:]
        tot = g_ref[0]
        for d in range(1, N_DEV):
            tot = tot + g_ref[d]
        loss_ref[...] = tot[SMALL_LOSS_ROW:SMALL_LOSS_ROW + 1, 0:128]

        def update(i, grad, rows=slice(None), lanes=slice(None)):
            w_ref, m_ref, v_ref = ins[3 * i:3 * i + 3]
            g_out, d_out, nm_out, nv_out = outs[4 * i:4 * i + 4]
            g_out[rows, lanes] = grad
            d_out[rows, lanes], nm_out[rows, lanes], nv_out[rows, lanes] = _adamw_update(
                w_ref[rows, lanes], grad, m_ref[rows, lanes], v_ref[rows, lanes])

        for i, name in enumerate(names):
            if name == "b_ada":
                for r in range(6):
                    update(i, tot[r:r + 1, :], lanes=slice(r * D_MODEL, (r + 1) * D_MODEL))
            elif name == "hg_lower_bounds":
                lb = _lower_bound(ins[3 * i][...])
                d0 = tot[SMALL_LB_ROW:SMALL_LB_ROW + 1, 0:512] * lb * (1.0 - lb)
                update(i, d0, rows=slice(0, 1))
                update(i, -d0, rows=slice(1, 2))
            else:
                row, lane = SMALL_PLACE[name]
                update(i, tot[row:row + 1, lane:lane + params[name][0].shape[1]])

    flat_in = [a for name in names for a in params[name]]
    shapes = [jax.ShapeDtypeStruct((1, 128), F32)] + [jax.ShapeDtypeStruct(params[name][0].shape, F32)
                                                      for name in names for _ in range(4)]
    out = pl.pallas_call(body, name="small_params_step", out_shape=shapes)(gathered, *flat_in)
    return out[0], {name: out[1 + 4 * i:5 + 4 * i] for i, name in enumerate(names)}


def _adamw_update(w, gv, m, v):
    nm = ADAM_B1 * m + (1.0 - ADAM_B1) * gv
    nv = ADAM_B2 * v + (1.0 - ADAM_B2) * jnp.square(gv)
    m_hat = nm / (1.0 - ADAM_B1 ** ADAM_STEP)
    v_hat = nv / (1.0 - ADAM_B2 ** ADAM_STEP)
    return -ADAM_LR * (m_hat / (jnp.sqrt(v_hat) + ADAM_EPS) + ADAM_WD * w), nm, nv


def _adamw_halves(core, w, mine, theirs, m, v, name):
    rows, cols = w.shape
    h = rows // 2
    tr = _row_tile(h)
    per_half = h // tr

    def body(core_ref, w_ref, mine_ref, theirs_ref, m_ref, v_ref, g_ref, d_ref, nm_ref, nv_ref):
        is_mine = pl.program_id(0) // per_half == core_ref[0]
        gv = jnp.where(is_mine, mine_ref[...], theirs_ref[...])
        g_ref[...] = gv
        d_ref[...], nm_ref[...], nv_ref[...] = _adamw_update(w_ref[...], gv, m_ref[...], v_ref[...])

    full = pl.BlockSpec((tr, cols), lambda i, core_ref: (i, 0))
    part = pl.BlockSpec((tr, cols), lambda i, core_ref: (i % per_half, 0))
    return _pcall(
        body, name=name, out_shape=[jax.ShapeDtypeStruct(w.shape, F32)] * 4,
        grid_spec=pltpu.PrefetchScalarGridSpec(
            num_scalar_prefetch=1, grid=(rows // tr,), in_specs=[full, part, part, full, full], out_specs=[full] * 4),
        compiler_params=_params(40, ("arbitrary",)),
        operands=(core, w, mine, theirs, m, v))


def _adamw(w, g, m, v, name):
    rows, cols = w.shape
    tr = _row_tile(rows) if rows >= 8 else rows

    def body(w_ref, g_ref, m_ref, v_ref, d_ref, nm_ref, nv_ref):
        d_ref[...], nm_ref[...], nv_ref[...] = _adamw_update(w_ref[...], g_ref[...], m_ref[...], v_ref[...])

    spec = pl.BlockSpec((tr, cols), lambda i: (i, 0))
    return _pcall(
        body, name=name, grid=(rows // tr,),
        out_shape=[jax.ShapeDtypeStruct(w.shape, F32)] * 3,
        in_specs=[spec] * 4, out_specs=[spec] * 3,
        compiler_params=_params(40, ("arbitrary",)),
        operands=(w, g, m, v))


def kernel(x, c, positions, w_ada, b_ada, w_in, hg_lower_bounds, hg_norm_w, mla_q_norm_w, w_q_up, mla_kv_norm_w, w_kv_up, w_out, ln1_g, ln1_b, w_mlp_in, w_mlp_out, ln2_g, ln2_b, loss_target, m_w_ada, m_b_ada, m_w_in, m_hg_lower_bounds, m_hg_norm_w, m_mla_q_norm_w, m_w_q_up, m_mla_kv_norm_w, m_w_kv_up, m_w_out, m_ln1_g, m_ln1_b, m_w_mlp_in, m_w_mlp_out, m_ln2_g, m_ln2_b, v_w_ada, v_b_ada, v_w_in, v_hg_lower_bounds, v_hg_norm_w, v_mla_q_norm_w, v_w_q_up, v_mla_kv_norm_w, v_w_kv_up, v_w_out, v_ln1_g, v_ln1_b, v_w_mlp_in, v_w_mlp_out, v_ln2_g, v_ln2_b):
    ix, iy, ic = _mesh_pos()
    chip = 2 * ix + iy
    me = 4 * ix + 2 * iy + ic
    core_arr = jnp.reshape(ic, (1,)).astype(jnp.int32)
    chip_arr = jnp.reshape(chip, (1,)).astype(jnp.int32)

    xs = x[0]
    target = loss_target[0]
    t_len = xs.shape[0]
    pos = positions.astype(F32).reshape(t_len, 1)
    inv = 1.0 / (ROPE_THETA ** (jnp.arange(0, ROPE_DIM, 2, dtype=F32) / ROPE_DIM))
    invf = jnp.concatenate([inv, inv, jnp.zeros((128 - ROPE_DIM,), F32)]).reshape(1, 128)

    def slot(w):
        rows, cols = w.shape
        own = w.astype(BF16).reshape(1, 2, rows // 2, cols)
        return lax.dynamic_update_slice(jnp.zeros((N_CHIPS, 2, rows // 2, cols), BF16), own, (chip, 0, 0, 0))

    def slot8(a):
        return lax.dynamic_update_slice(jnp.zeros((N_DEV,) + a.shape, a.dtype), a[None], (me, 0, 0))

    def whole(s):
        return s.reshape(N_CHIPS, 2 * s.shape[2], s.shape[3])

    def halved(g):
        return g.reshape(N_CHIPS, 2, g.shape[1] // 2, g.shape[2])

    ada_cols = w_ada.shape[2]
    c_all, *early = _run_exchange(
        _merge(_gather_all(slot8(jnp.broadcast_to(c, (8, D_MODEL)))),
               _gather_over_ici_in_two_steps([slot(jnp.transpose(w_in[0])), slot(w_q_up[0]), slot(w_kv_up[0])])),
        "gather_c_and_mixer_weights_ici")
    b_shard = lax.dynamic_slice(b_ada, (0, chip * ada_cols), (1, ada_cols))
    mod_cols, cond16 = _ada_project(c_all[:, 0, :], w_ada[0], b_shard)
    mod_all, *early = _run_exchange(_merge(_gather_all(slot8(mod_cols)), _gather_over_d2d(early)),
                                    "gather_mod_and_mixer_weights_d2d")
    mod_mine = lax.dynamic_slice(mod_all, (0, me, 0), (N_DEV, 1, ada_cols))[::2, 0, :].reshape(6, D_MODEL)
    sh_a, sc_a, g_a, sh_m, sc_m, g_m = (mod_mine[i:i + 1] for i in range(6))
    g_in, g_q, g_kv = (whole(s) for s in early)
    w_in_full = jnp.pad(g_in.reshape(IN_COLS, D_MODEL), ((0, IN_COLS_PAD - IN_COLS), (0, 0)))
    w_q_full = jnp.pad(g_q, ((0, 0), (0, 0), (0, QK_DIM - g_q.shape[2])))

    w1_rows = D_MODEL // 2
    (u_a, zhg, cq, ckv, q, k, k_t, v, v_t), (s_top, s_out) = _in_project(
        xs, pos, sc_a, sh_a, w_in_full, mla_q_norm_w, mla_kv_norm_w, w_q_full, g_kv, invf,
        _gather_over_ici([slot(w_mlp_in[0, :w1_rows]), slot(w_out[0])]))
    (o_pre, o_hg, states), (s_bottom, s_top, s_out) = _hgrn_forward(
        zhg, hg_lower_bounds, hg_norm_w,
        _merge(_gather_over_ici([slot(w_mlp_in[0, w1_rows:])]), _gather_over_d2d([s_top, s_out])))
    (o_mla, lse), (s_w2, s_bottom) = _attention_forward(
        q, k, v_t, _merge(_gather_over_ici([slot(w_mlp_out[0])]), _gather_over_d2d([s_bottom])))
    w_out_full = whole(s_out).reshape(D_MODEL, D_MODEL)
    (cat, mix, xhat1, rstd1), (s_w2,) = _out_project(o_hg, o_mla, xs, g_a, w_out_full, _gather_over_d2d([s_w2]))
    g_w1_top, g_w1_bottom, g_w2 = whole(s_top), whole(s_bottom), whole(s_w2)
    vecs = jnp.concatenate([ln1_g, ln1_b, sc_m, sh_m, g_m, g_a, ln2_g, ln2_b], axis=0)
    act, dhp, um, dh, dmix, d_cat, dr1, mlp_sums, delta = _mlp_and_back(
        xhat1, rstd1, mix, target, o_mla, vecs, g_w1_top, g_w1_bottom, g_w2, w_out_full)

    gw_1 = halved(_weight_grad(um, dhp, "grad_w_mlp_in", N_CHIPS, D_FF // N_CHIPS, token_tile=4096))
    gw_2, (landed_1,) = _weight_grad(act, dh, "grad_w_mlp_out", N_CHIPS, D_MODEL, a_blocked=True, b_blocked=False,
                                     token_tile=4096, exchange=_pair_exchange([gw_1]))
    gw_out = _weight_grad(cat, dmix, "grad_w_out", 1, D_MODEL, token_tile=2048)
    later = [halved(gw_2), halved(gw_out.reshape(N_CHIPS, D_MODEL // N_CHIPS, D_MODEL))]
    own_1, travels_1 = _add_pair(core_arr, chip_arr, gw_1, landed_1)
    (dq, dk, dv), (landed_1, *landed) = _attention_backward(
        q, k, k_t, v, d_cat, lse, delta, _merge(_chip_exchange([travels_1]), _pair_exchange(later)))
    mine_1 = _add_chips(own_1, landed_1)
    chip_sums = [_add_pair(core_arr, chip_arr, g, l) for g, l in zip(later, landed)]
    (d_hq, d_hf, d_hi, d_hg, hg_sums), (theirs_1, *landed) = _hgrn_backward(
        zhg, hg_lower_bounds, hg_norm_w, o_pre, d_cat, states,
        _merge(_pair_send([mine_1]), _chip_exchange([b for _, b in chip_sums])))
    later_mine = [_add_chips(own, l) for (own, _), l in zip(chip_sums, landed)]
    mlp_mine = [mine_1] + later_mine
    (dz, grad_x, in_sums, gw_q, gw_kv), _ = _in_project_backward(
        dq, dk, dv, cq, ckv, pos, invf, mla_q_norm_w, mla_kv_norm_w, w_q_full, g_kv,
        d_hq, d_hf, d_hi, d_hg, w_in_full, dr1, xs, sc_a)

    zeros = lambda n: jnp.zeros((1, n), F32)
    small = jnp.concatenate([
        in_sums[1:2], in_sums[0:1], mlp_sums[S_DGA:S_DGA + 1],
        mlp_sums[S_DSHM:S_DSHM + 1], mlp_sums[S_DSCM:S_DSCM + 1], mlp_sums[S_DGM:S_DGM + 1],
        mlp_sums[S_DLN1G:S_DLN1G + 1], mlp_sums[S_DLN1B:S_DLN1B + 1],
        mlp_sums[S_DLN2G:S_DLN2G + 1], mlp_sums[S_DLN2B:S_DLN2B + 1],
        jnp.concatenate([hg_sums[0:1], hg_sums[1:2]], axis=1),
        jnp.concatenate([in_sums[2:3, :Q_RANK], in_sums[3:4, :KV_RANK], zeros(D_MODEL - Q_RANK - KV_RANK)], axis=1),
        mlp_sums[S_LOSS:S_LOSS + 1],
        jnp.zeros((SMALL_ROWS - 13, D_MODEL), F32)], axis=0)

    gw_in, (*later_theirs, small_all) = _weight_grad(
        dz, u_a, "grad_w_in", 3, D_MODEL, a_blocked=True, b_blocked=False, token_tile=4096,
        exchange=_merge(_pair_send(later_mine), _gather_all(slot8(small))))
    mlp_theirs = [theirs_1] + list(later_theirs)
    gw_in = gw_in.reshape(IN_COLS_PAD, D_MODEL)
    gw_q = gw_q[:, :, :HEAD_DIM + ROPE_DIM]
    flat = lambda g: g.reshape(g.shape[0] * g.shape[1], g.shape[2])
    mixer_mine, mixer_theirs = _reduce_in_vmem(
        [gw_in, flat(gw_q), flat(gw_kv)], [IN_COLS // N_CHIPS // 2, Q_RANK // 2, KV_RANK // 2], "reduce_mixer_grads")
    reduced = ("w_in", "w_q_up", "w_kv_up", "w_mlp_in", "w_mlp_out", "w_out")
    halves_mine = dict(zip(reduced, list(mixer_mine) + mlp_mine))
    halves_theirs = dict(zip(reduced, list(mixer_theirs) + list(mlp_theirs)))

    small_names = ("b_ada", "hg_lower_bounds", "hg_norm_w", "mla_q_norm_w", "mla_kv_norm_w",
                   "ln1_g", "ln1_b", "ln2_g", "ln2_b")
    loss_row, small_out = _small_params_step(small_all, {
        "b_ada": (b_ada, m_b_ada, v_b_ada),
        "hg_lower_bounds": (hg_lower_bounds, m_hg_lower_bounds, v_hg_lower_bounds),
        "hg_norm_w": (hg_norm_w, m_hg_norm_w, v_hg_norm_w),
        "mla_q_norm_w": (mla_q_norm_w, m_mla_q_norm_w, v_mla_q_norm_w),
        "mla_kv_norm_w": (mla_kv_norm_w, m_mla_kv_norm_w, v_mla_kv_norm_w),
        "ln1_g": (ln1_g, m_ln1_g, v_ln1_g), "ln1_b": (ln1_b, m_ln1_b, v_ln1_b),
        "ln2_g": (ln2_g, m_ln2_g, v_ln2_g), "ln2_b": (ln2_b, m_ln2_b, v_ln2_b)})
    loss = loss_row[0, 0]

    d_mod_all = small_all[:, 0:6, :].reshape(N_DEV, 6 * D_MODEL)
    d_mod_cols = lax.dynamic_slice(d_mod_all, (0, chip * ada_cols), (N_DEV, ada_cols))
    d_mod_cols = jnp.concatenate([d_mod_cols, jnp.zeros_like(d_mod_cols)], axis=0)
    g_w_ada = _weight_grad(cond16, d_mod_cols, "grad_w_ada", 1, ada_cols)[0]

    names = ["w_ada", "b_ada", "w_in", "hg_lower_bounds", "hg_norm_w", "mla_q_norm_w", "w_q_up", "mla_kv_norm_w",
             "w_kv_up", "w_out", "ln1_g", "ln1_b", "w_mlp_in", "w_mlp_out", "ln2_g", "ln2_b"]
    weights = [w_ada, b_ada, w_in, hg_lower_bounds, hg_norm_w, mla_q_norm_w, w_q_up, mla_kv_norm_w,
               w_kv_up, w_out, ln1_g, ln1_b, w_mlp_in, w_mlp_out, ln2_g, ln2_b]
    moms = [m_w_ada, m_b_ada, m_w_in, m_hg_lower_bounds, m_hg_norm_w, m_mla_q_norm_w, m_w_q_up, m_mla_kv_norm_w,
            m_w_kv_up, m_w_out, m_ln1_g, m_ln1_b, m_w_mlp_in, m_w_mlp_out, m_ln2_g, m_ln2_b]
    vels = [v_w_ada, v_b_ada, v_w_in, v_hg_lower_bounds, v_hg_norm_w, v_mla_q_norm_w, v_w_q_up, v_mla_kv_norm_w,
            v_w_kv_up, v_w_out, v_ln1_g, v_ln1_b, v_w_mlp_in, v_w_mlp_out, v_ln2_g, v_ln2_b]
    out_g, out_d, out_m, out_v = [], [], [], []
    for name, w, m, vv in zip(names, weights, moms, vels):
        if name in small_names:
            g, d, nm, nv = small_out[name]
            back = lambda a: a
        elif name == "w_in":
            to2d, back = (lambda a: jnp.transpose(a[0])), (lambda a: jnp.transpose(a)[None])
        else:
            to2d, back = (lambda a, s=w.shape[1:]: a.reshape(s)), (lambda a, s=w.shape: a.reshape(s))
        if name == "w_ada":
            d, nm, nv = _adamw(to2d(w), g_w_ada, to2d(m), to2d(vv), "adamw_" + name)
            g = g_w_ada
        elif name not in small_names:
            g, d, nm, nv = _adamw_halves(core_arr, to2d(w), halves_mine[name], halves_theirs[name], to2d(m), to2d(vv),
                                         "adamw_" + name)
        out_g.append(back(g))
        out_d.append(back(d))
        out_m.append(back(nm))
        out_v.append(back(nv))
    return (loss, grad_x[None], *out_g, *out_d, *out_m, *out_v)
```
